```python
import jax, jax.numpy as jnp
from jax import lax
import numpy as np

D_MODEL = 1024
BATCH = 8
SEQ = 4096
DEPTH = 1

CHUNK = 64
SGU_BLOCK = 128
SGU_WIDTH = 1024
SGU_GROUPS = 8
SGU_GROUP_DIM = SGU_WIDTH // SGU_GROUPS
HGRN_WIDTH = 1024
HGRN_EXPAND = 128
HGRN_HEADS = HGRN_WIDTH // HGRN_EXPAND
N_BRANCH = 2
D_FF = 2816
CONV_WIDTH = 3
PLE_DIM = 256
LN_EPS = 1e-5
RMS_EPS = 1e-6
ALPHA = (2 * DEPTH) ** 0.25
BETA = (8 * DEPTH) ** -0.25
IN_COLS = 2 * SGU_WIDTH + 4 * HGRN_WIDTH + N_BRANCH * D_MODEL

kernel_name = "chunk_causal_sgu_hgrn2_hybrid"


def layer_norm(x, g, b):
    xf = x.astype(jnp.float32)
    mu = jnp.mean(xf, axis=-1, keepdims=True)
    var = jnp.mean(jnp.square(xf - mu), axis=-1, keepdims=True)
    return ((xf - mu) * lax.rsqrt(var + LN_EPS)).astype(x.dtype) * g + b


def sgu_mixer(u, v, w_s, b_s, g_v, b_v):
    bsz, t_len, _ = u.shape
    n_blk = t_len // SGU_BLOCK
    v = layer_norm(v, g_v, b_v).reshape(bsz, n_blk, SGU_BLOCK, SGU_GROUPS, SGU_GROUP_DIM)
    chunk_id = jnp.arange(SGU_BLOCK) // CHUNK
    mask = chunk_id[:, None] >= chunk_id[None, :]
    w = jnp.where(mask[None], w_s, jnp.zeros((), w_s.dtype))
    mixed = jnp.einsum('gts,bnsgc->bntgc', w, v) + b_s.T[None, None, :, :, None]
    return u * mixed.reshape(bsz, t_len, SGU_WIDTH)


def hgrn2_mixer(q, f_pre, i_in, og, lb, g_norm):
    bsz, t_len, _ = q.shape
    n_chunk = t_len // CHUNK
    f32 = jnp.float32
    qf = jax.nn.silu(q.astype(f32))
    f = lb + (1.0 - lb) * jax.nn.sigmoid(f_pre.astype(f32))
    kf = 1.0 - f
    logf = jnp.log(f)

    def to_chunks(z):
        return z.reshape(bsz, n_chunk, CHUNK, HGRN_HEADS, HGRN_EXPAND).transpose(1, 0, 3, 2, 4)

    qc, kc, ic = to_chunks(qf), to_chunks(kf), to_chunks(i_in.astype(f32))
    cc = jnp.cumsum(to_chunks(logf), axis=3)
    tri = jnp.tril(jnp.ones((CHUNK, CHUNK), dtype=bool))

    def step(state, inp):
        qk, kk, ik, ck = inp
        diff = ck[:, :, :, None, :] - ck[:, :, None, :, :]
        decay = jnp.exp(jnp.where(tri[None, None, :, :, None], diff, -jnp.inf))
        attn = jnp.einsum('bhte,bhse,bhtse->bhts', qk, kk, decay)
        o = (jnp.einsum('bhts,bhsv->bhtv', attn, ik)
             + jnp.einsum('bhte,bhev->bhtv', qk * jnp.exp(ck), state))
        c_last = ck[:, :, -1:, :]
        state = (jnp.exp(c_last[:, :, 0, :])[..., None] * state
                 + jnp.einsum('bhse,bhsv->bhev', kk * jnp.exp(c_last - ck), ik))
        return state, o

    s0 = jnp.zeros((bsz, HGRN_HEADS, HGRN_EXPAND, HGRN_EXPAND), f32)
    _, o = lax.scan(step, s0, (qc, kc, ic, cc))
    o = o.transpose(1, 0, 3, 2, 4).reshape(bsz, t_len, HGRN_HEADS, HGRN_EXPAND)
    o = o * lax.rsqrt(jnp.mean(jnp.square(o), axis=-1, keepdims=True) + RMS_EPS)
    o = o.reshape(bsz, t_len, HGRN_WIDTH).astype(q.dtype) * g_norm
    return o * jax.nn.silu(og)


def causal_dwconv(x, w, b):
    y = lax.conv_general_dilated(
        x, w[:, None, :], window_strides=(1,), padding=[(CONV_WIDTH - 1, 0)],
        dimension_numbers=('NWC', 'WIO', 'NWC'), feature_group_count=x.shape[-1])
    return y + b


def conv_ffn(x, w_up, conv_w, conv_b, w_down):
    h = x @ w_up
    gate, val = h[..., :D_FF], h[..., D_FF:]
    gate = causal_dwconv(gate, conv_w, conv_b)
    return (jax.nn.gelu(gate) * val) @ w_down


def _fwd_setup_inputs(seed: int = 0) -> dict:
    key = jax.random.key(seed)
    ks = jax.random.split(key, 24)
    n = lambda k, s, sc: jax.random.normal(k, s, jnp.float32) * sc
    L, D = DEPTH, D_MODEL
    return {
        "x": n(ks[0], (BATCH, SEQ, D), 1.0),
        "p": n(ks[1], (L, BATCH, SEQ, PLE_DIM), 1.0),
        "w_in": n(ks[2], (L, D, IN_COLS), D ** -0.5),
        "sgu_w_s": n(ks[3], (L, SGU_GROUPS, SGU_BLOCK, SGU_BLOCK), SGU_BLOCK ** -0.5),
        "sgu_b_s": 1.0 + n(ks[4], (L, SGU_GROUPS, SGU_BLOCK), 0.02),
        "sgu_norm_g": 1.0 + n(ks[5], (L, SGU_WIDTH), 0.02),
        "sgu_norm_b": n(ks[6], (L, SGU_WIDTH), 0.02),
        "hgrn_lb_logits": n(ks[7], (L + 1, HGRN_WIDTH), 0.1),
        "hgrn_norm_g": 1.0 + n(ks[8], (L, HGRN_WIDTH), 0.02),
        "w_branch": n(ks[9], (L, N_BRANCH, SGU_WIDTH, D), SGU_WIDTH ** -0.5),
        "w_out": n(ks[10], (L, D, D), D ** -0.5 * BETA),
        "ln1_g": 1.0 + n(ks[11], (L, D), 0.02),
        "ln1_b": n(ks[12], (L, D), 0.02),
        "ffn_w_up": n(ks[13], (L, D, 2 * D_FF), D ** -0.5),
        "ffn_conv_w": n(ks[14], (L, CONV_WIDTH, D_FF), CONV_WIDTH ** -0.5),
        "ffn_conv_b": n(ks[15], (L, D_FF), 0.02),
        "ffn_w_down": n(ks[16], (L, D_FF, D), D_FF ** -0.5 * BETA),
        "ln2_g": 1.0 + n(ks[17], (L, D), 0.02),
        "ln2_b": n(ks[18], (L, D), 0.02),
        "ple_w_proj": n(ks[19], (L, PLE_DIM, D), PLE_DIM ** -0.5 * BETA),
        "ple_w_gate": n(ks[20], (L, D, D), D ** -0.5),
    }


def _fwd_reference(x, p, w_in, sgu_w_s, sgu_b_s, sgu_norm_g, sgu_norm_b, hgrn_lb_logits,
              hgrn_norm_g, w_branch, w_out, ln1_g, ln1_b, ffn_w_up, ffn_conv_w,
              ffn_conv_b, ffn_w_down, ln2_g, ln2_b, ple_w_proj, ple_w_gate):
    splits = [SGU_WIDTH, 2 * SGU_WIDTH,
              2 * SGU_WIDTH + HGRN_WIDTH, 2 * SGU_WIDTH + 2 * HGRN_WIDTH,
              2 * SGU_WIDTH + 3 * HGRN_WIDTH, 2 * SGU_WIDTH + 4 * HGRN_WIDTH,
              2 * SGU_WIDTH + 4 * HGRN_WIDTH + D_MODEL]
    lb_all = jnp.cumsum(jax.nn.softmax(hgrn_lb_logits.astype(jnp.float32), axis=0), axis=0)
    for l in range(DEPTH):
        h = x @ w_in[l]
        u, v, q, f_pre, i_in, og, g_a, g_b = jnp.split(h, splits, axis=-1)
        y_a = sgu_mixer(jax.nn.gelu(u), jax.nn.gelu(v), sgu_w_s[l], sgu_b_s[l],
                        sgu_norm_g[l], sgu_norm_b[l])
        y_b = hgrn2_mixer(q, f_pre, i_in, og, lb_all[l], hgrn_norm_g[l])
        merged = (jax.nn.sigmoid(g_a) * (y_a @ w_branch[l, 0])
                  + jax.nn.sigmoid(g_b) * (y_b @ w_branch[l, 1]))
        x = layer_norm(ALPHA * x + merged @ w_out[l], ln1_g[l], ln1_b[l])
        ffn = conv_ffn(x, ffn_w_up[l], ffn_conv_w[l], ffn_conv_b[l], ffn_w_down[l])
        ple = jax.nn.sigmoid(x @ ple_w_gate[l]) * (p[l] @ ple_w_proj[l])
        x = layer_norm(ALPHA * x + ffn + ple, ln2_g[l], ln2_b[l])
    return x


import jax as _jax
import jax.numpy as _jnp

TWIN_FORMAT = 'train_step'
FWD_PARAMS = ['x', 'p', 'w_in', 'sgu_w_s', 'sgu_b_s', 'sgu_norm_g', 'sgu_norm_b', 'hgrn_lb_logits', 'hgrn_norm_g', 'w_branch', 'w_out', 'ln1_g', 'ln1_b', 'ffn_w_up', 'ffn_conv_w', 'ffn_conv_b', 'ffn_w_down', 'ln2_g', 'ln2_b', 'ple_w_proj', 'ple_w_gate']
TWIN_WEIGHTS = ['w_in', 'sgu_w_s', 'sgu_b_s', 'sgu_norm_g', 'sgu_norm_b', 'hgrn_lb_logits', 'hgrn_norm_g', 'w_branch', 'w_out', 'ln1_g', 'ln1_b', 'ffn_w_up', 'ffn_conv_w', 'ffn_conv_b', 'ffn_w_down', 'ln2_g', 'ln2_b', 'ple_w_proj', 'ple_w_gate']
TWIN_DIFF_INPUT = 'x'
TWIN_INPUTS = ['x', 'p', 'w_in', 'sgu_w_s', 'sgu_b_s', 'sgu_norm_g', 'sgu_norm_b', 'hgrn_lb_logits', 'hgrn_norm_g', 'w_branch', 'w_out', 'ln1_g', 'ln1_b', 'ffn_w_up', 'ffn_conv_w', 'ffn_conv_b', 'ffn_w_down', 'ln2_g', 'ln2_b', 'ple_w_proj', 'ple_w_gate', 'loss_target', 'm_w_in', 'm_sgu_w_s', 'm_sgu_b_s', 'm_sgu_norm_g', 'm_sgu_norm_b', 'm_hgrn_lb_logits', 'm_hgrn_norm_g', 'm_w_branch', 'm_w_out', 'm_ln1_g', 'm_ln1_b', 'm_ffn_w_up', 'm_ffn_conv_w', 'm_ffn_conv_b', 'm_ffn_w_down', 'm_ln2_g', 'm_ln2_b', 'm_ple_w_proj', 'm_ple_w_gate', 'v_w_in', 'v_sgu_w_s', 'v_sgu_b_s', 'v_sgu_norm_g', 'v_sgu_norm_b', 'v_hgrn_lb_logits', 'v_hgrn_norm_g', 'v_w_branch', 'v_w_out', 'v_ln1_g', 'v_ln1_b', 'v_ffn_w_up', 'v_ffn_conv_w', 'v_ffn_conv_b', 'v_ffn_w_down', 'v_ln2_g', 'v_ln2_b', 'v_ple_w_proj', 'v_ple_w_gate']
TWIN_OUTPUTS = ['loss', 'grad_x', 'grad_w_in', 'grad_sgu_w_s', 'grad_sgu_b_s', 'grad_sgu_norm_g', 'grad_sgu_norm_b', 'grad_hgrn_lb_logits', 'grad_hgrn_norm_g', 'grad_w_branch', 'grad_w_out', 'grad_ln1_g', 'grad_ln1_b', 'grad_ffn_w_up', 'grad_ffn_conv_w', 'grad_ffn_conv_b', 'grad_ffn_w_down', 'grad_ln2_g', 'grad_ln2_b', 'grad_ple_w_proj', 'grad_ple_w_gate', 'delta_w_in', 'delta_sgu_w_s', 'delta_sgu_b_s', 'delta_sgu_norm_g', 'delta_sgu_norm_b', 'delta_hgrn_lb_logits', 'delta_hgrn_norm_g', 'delta_w_branch', 'delta_w_out', 'delta_ln1_g', 'delta_ln1_b', 'delta_ffn_w_up', 'delta_ffn_conv_w', 'delta_ffn_conv_b', 'delta_ffn_w_down', 'delta_ln2_g', 'delta_ln2_b', 'delta_ple_w_proj', 'delta_ple_w_gate', 'new_m_w_in', 'new_m_sgu_w_s', 'new_m_sgu_b_s', 'new_m_sgu_norm_g', 'new_m_sgu_norm_b', 'new_m_hgrn_lb_logits', 'new_m_hgrn_norm_g', 'new_m_w_branch', 'new_m_w_out', 'new_m_ln1_g', 'new_m_ln1_b', 'new_m_ffn_w_up', 'new_m_ffn_conv_w', 'new_m_ffn_conv_b', 'new_m_ffn_w_down', 'new_m_ln2_g', 'new_m_ln2_b', 'new_m_ple_w_proj', 'new_m_ple_w_gate', 'new_v_w_in', 'new_v_sgu_w_s', 'new_v_sgu_b_s', 'new_v_sgu_norm_g', 'new_v_sgu_norm_b', 'new_v_hgrn_lb_logits', 'new_v_hgrn_norm_g', 'new_v_w_branch', 'new_v_w_out', 'new_v_ln1_g', 'new_v_ln1_b', 'new_v_ffn_w_up', 'new_v_ffn_conv_w', 'new_v_ffn_conv_b', 'new_v_ffn_w_down', 'new_v_ln2_g', 'new_v_ln2_b', 'new_v_ple_w_proj', 'new_v_ple_w_gate']
TWIN_LEAF_KINDS = {'loss': 'loss', 'grad_x': 'grad_x', 'grad_w_in': 'grad_w', 'grad_sgu_w_s': 'grad_w', 'grad_sgu_b_s': 'grad_w', 'grad_sgu_norm_g': 'grad_w', 'grad_sgu_norm_b': 'grad_w', 'grad_hgrn_lb_logits': 'grad_w', 'grad_hgrn_norm_g': 'grad_w', 'grad_w_branch': 'grad_w', 'grad_w_out': 'grad_w', 'grad_ln1_g': 'grad_w', 'grad_ln1_b': 'grad_w', 'grad_ffn_w_up': 'grad_w', 'grad_ffn_conv_w': 'grad_w', 'grad_ffn_conv_b': 'grad_w', 'grad_ffn_w_down': 'grad_w', 'grad_ln2_g': 'grad_w', 'grad_ln2_b': 'grad_w', 'grad_ple_w_proj': 'grad_w', 'grad_ple_w_gate': 'grad_w', 'delta_w_in': 'delta_w', 'delta_sgu_w_s': 'delta_w', 'delta_sgu_b_s': 'delta_w', 'delta_sgu_norm_g': 'delta_w', 'delta_sgu_norm_b': 'delta_w', 'delta_hgrn_lb_logits': 'delta_w', 'delta_hgrn_norm_g': 'delta_w', 'delta_w_branch': 'delta_w', 'delta_w_out': 'delta_w', 'delta_ln1_g': 'delta_w', 'delta_ln1_b': 'delta_w', 'delta_ffn_w_up': 'delta_w', 'delta_ffn_conv_w': 'delta_w', 'delta_ffn_conv_b': 'delta_w', 'delta_ffn_w_down': 'delta_w', 'delta_ln2_g': 'delta_w', 'delta_ln2_b': 'delta_w', 'delta_ple_w_proj': 'delta_w', 'delta_ple_w_gate': 'delta_w', 'new_m_w_in': 'new_m', 'new_m_sgu_w_s': 'new_m', 'new_m_sgu_b_s': 'new_m', 'new_m_sgu_norm_g': 'new_m', 'new_m_sgu_norm_b': 'new_m', 'new_m_hgrn_lb_logits': 'new_m', 'new_m_hgrn_norm_g': 'new_m', 'new_m_w_branch': 'new_m', 'new_m_w_out': 'new_m', 'new_m_ln1_g': 'new_m', 'new_m_ln1_b': 'new_m', 'new_m_ffn_w_up': 'new_m', 'new_m_ffn_conv_w': 'new_m', 'new_m_ffn_conv_b': 'new_m', 'new_m_ffn_w_down': 'new_m', 'new_m_ln2_g': 'new_m', 'new_m_ln2_b': 'new_m', 'new_m_ple_w_proj': 'new_m', 'new_m_ple_w_gate': 'new_m', 'new_v_w_in': 'new_v', 'new_v_sgu_w_s': 'new_v', 'new_v_sgu_b_s': 'new_v', 'new_v_sgu_norm_g': 'new_v', 'new_v_sgu_norm_b': 'new_v', 'new_v_hgrn_lb_logits': 'new_v', 'new_v_hgrn_norm_g': 'new_v', 'new_v_w_branch': 'new_v', 'new_v_w_out': 'new_v', 'new_v_ln1_g': 'new_v', 'new_v_ln1_b': 'new_v', 'new_v_ffn_w_up': 'new_v', 'new_v_ffn_conv_w': 'new_v', 'new_v_ffn_conv_b': 'new_v', 'new_v_ffn_w_down': 'new_v', 'new_v_ln2_g': 'new_v', 'new_v_ln2_b': 'new_v', 'new_v_ple_w_proj': 'new_v', 'new_v_ple_w_gate': 'new_v'}


def _forward(args):
    return _fwd_reference(*[args[k] for k in FWD_PARAMS])


def _output_shape():
    out = _jax.eval_shape(lambda: _forward(_fwd_setup_inputs(0)))
    return out.shape, out.dtype

N_MICROBATCH = 1
ADAM_LR = 0.001
ADAM_B1 = 0.9
ADAM_B2 = 0.999
ADAM_EPS = 1e-08
ADAM_WD = 0.01
ADAM_STEP = 10
PER_EXAMPLE_BATCH_AXIS = {'x': 0, 'p': 1, 'loss_target': 0}
SHARED_INPUTS = []
_WEIGHT_DTYPES = {'w_in': _jnp.float32, 'sgu_w_s': _jnp.float32, 'sgu_b_s': _jnp.float32, 'sgu_norm_g': _jnp.float32, 'sgu_norm_b': _jnp.float32, 'hgrn_lb_logits': _jnp.float32, 'hgrn_norm_g': _jnp.float32, 'w_branch': _jnp.float32, 'w_out': _jnp.float32, 'ln1_g': _jnp.float32, 'ln1_b': _jnp.float32, 'ffn_w_up': _jnp.float32, 'ffn_conv_w': _jnp.float32, 'ffn_conv_b': _jnp.float32, 'ffn_w_down': _jnp.float32, 'ln2_g': _jnp.float32, 'ln2_b': _jnp.float32, 'ple_w_proj': _jnp.float32, 'ple_w_gate': _jnp.float32}
MOMENT_SCALE = {'w_in': 2.437981e-02, 'sgu_w_s': 2.663228e-02, 'sgu_b_s': 3.155628e-02, 'sgu_norm_g': 2.736476e-02, 'sgu_norm_b': 2.967293e-02, 'hgrn_lb_logits': 2.665637e-03, 'hgrn_norm_g': 2.959857e-02, 'w_branch': 3.842009e-02, 'w_out': 9.078391e-02, 'ln1_g': 8.348360e-01, 'ln1_b': 4.253877e-01, 'ffn_w_up': 3.296666e-02, 'ffn_conv_w': 3.451657e-02, 'ffn_conv_b': 3.494346e-02, 'ffn_w_down': 9.060763e-02, 'ln2_g': 3.200780e+01, 'ln2_b': 1.892418e+00, 'ple_w_proj': 7.453002e-02, 'ple_w_gate': 1.723128e-02}


def _to_microbatches(a, axis):
    t = _jnp.moveaxis(a, axis, 0)
    t = t.reshape((N_MICROBATCH, t.shape[0] // N_MICROBATCH) + t.shape[1:])
    return _jnp.moveaxis(t, 1, axis + 1)


def setup_inputs(seed: int = 0) -> dict:
    inp = _fwd_setup_inputs(seed)
    key = _jax.random.fold_in(_jax.random.key(seed), 7919)
    shape, _ = _output_shape()
    out = dict(inp)
    out["loss_target"] = _jax.random.normal(_jax.random.fold_in(key, 0), shape, _jnp.float32)
    for i, name in enumerate(TWIN_WEIGHTS):
        w = inp[name].astype(_jnp.float32)
        if MOMENT_SCALE is None:
            s = _jnp.sqrt(_jnp.mean(_jnp.square(w)) + 1e-30)
        else:
            s = MOMENT_SCALE[name]
        km, kv = _jax.random.split(_jax.random.fold_in(key, i + 1))
        out[name] = w
        out["m_" + name] = s * _jax.random.normal(km, w.shape, _jnp.float32)
        out["v_" + name] = (s * s) * _jax.random.uniform(kv, w.shape, _jnp.float32, 0.5, 1.5)
    if N_MICROBATCH > 1:
        for name, axis in PER_EXAMPLE_BATCH_AXIS.items():
            out[name] = _to_microbatches(out[name], axis)
    return {'x': out['x'], 'p': out['p'], 'w_in': out['w_in'], 'sgu_w_s': out['sgu_w_s'], 'sgu_b_s': out['sgu_b_s'], 'sgu_norm_g': out['sgu_norm_g'], 'sgu_norm_b': out['sgu_norm_b'], 'hgrn_lb_logits': out['hgrn_lb_logits'], 'hgrn_norm_g': out['hgrn_norm_g'], 'w_branch': out['w_branch'], 'w_out': out['w_out'], 'ln1_g': out['ln1_g'], 'ln1_b': out['ln1_b'], 'ffn_w_up': out['ffn_w_up'], 'ffn_conv_w': out['ffn_conv_w'], 'ffn_conv_b': out['ffn_conv_b'], 'ffn_w_down': out['ffn_w_down'], 'ln2_g': out['ln2_g'], 'ln2_b': out['ln2_b'], 'ple_w_proj': out['ple_w_proj'], 'ple_w_gate': out['ple_w_gate'], 'loss_target': out['loss_target'], 'm_w_in': out['m_w_in'], 'm_sgu_w_s': out['m_sgu_w_s'], 'm_sgu_b_s': out['m_sgu_b_s'], 'm_sgu_norm_g': out['m_sgu_norm_g'], 'm_sgu_norm_b': out['m_sgu_norm_b'], 'm_hgrn_lb_logits': out['m_hgrn_lb_logits'], 'm_hgrn_norm_g': out['m_hgrn_norm_g'], 'm_w_branch': out['m_w_branch'], 'm_w_out': out['m_w_out'], 'm_ln1_g': out['m_ln1_g'], 'm_ln1_b': out['m_ln1_b'], 'm_ffn_w_up': out['m_ffn_w_up'], 'm_ffn_conv_w': out['m_ffn_conv_w'], 'm_ffn_conv_b': out['m_ffn_conv_b'], 'm_ffn_w_down': out['m_ffn_w_down'], 'm_ln2_g': out['m_ln2_g'], 'm_ln2_b': out['m_ln2_b'], 'm_ple_w_proj': out['m_ple_w_proj'], 'm_ple_w_gate': out['m_ple_w_gate'], 'v_w_in': out['v_w_in'], 'v_sgu_w_s': out['v_sgu_w_s'], 'v_sgu_b_s': out['v_sgu_b_s'], 'v_sgu_norm_g': out['v_sgu_norm_g'], 'v_sgu_norm_b': out['v_sgu_norm_b'], 'v_hgrn_lb_logits': out['v_hgrn_lb_logits'], 'v_hgrn_norm_g': out['v_hgrn_norm_g'], 'v_w_branch': out['v_w_branch'], 'v_w_out': out['v_w_out'], 'v_ln1_g': out['v_ln1_g'], 'v_ln1_b': out['v_ln1_b'], 'v_ffn_w_up': out['v_ffn_w_up'], 'v_ffn_conv_w': out['v_ffn_conv_w'], 'v_ffn_conv_b': out['v_ffn_conv_b'], 'v_ffn_w_down': out['v_ffn_w_down'], 'v_ln2_g': out['v_ln2_g'], 'v_ln2_b': out['v_ln2_b'], 'v_ple_w_proj': out['v_ple_w_proj'], 'v_ple_w_gate': out['v_ple_w_gate']}


def _loss(weights, diff, rest, loss_target):
    with _jax.named_scope("forward"):
        args = {**rest, TWIN_DIFF_INPUT: diff, **{k: w.astype(_WEIGHT_DTYPES[k]) for k, w in weights.items()}}
        y = _forward(args)
    with _jax.named_scope("loss_head"):
        err = _jnp.square(y.astype(_jnp.float32) - loss_target)
        return 0.5 * _jnp.sum(_jnp.mean(err, axis=-1)) if err.ndim else 0.5 * err


def _adamw(w, g, m, v):
    m = ADAM_B1 * m + (1.0 - ADAM_B1) * g
    v = ADAM_B2 * v + (1.0 - ADAM_B2) * _jnp.square(g)
    m_hat = m / (1.0 - ADAM_B1 ** ADAM_STEP)
    v_hat = v / (1.0 - ADAM_B2 ** ADAM_STEP)
    delta = -ADAM_LR * (m_hat / (_jnp.sqrt(v_hat) + ADAM_EPS) + ADAM_WD * w)
    return delta, m, v


def reference(x, p, w_in, sgu_w_s, sgu_b_s, sgu_norm_g, sgu_norm_b, hgrn_lb_logits, hgrn_norm_g, w_branch, w_out, ln1_g, ln1_b, ffn_w_up, ffn_conv_w, ffn_conv_b, ffn_w_down, ln2_g, ln2_b, ple_w_proj, ple_w_gate, loss_target, m_w_in, m_sgu_w_s, m_sgu_b_s, m_sgu_norm_g, m_sgu_norm_b, m_hgrn_lb_logits, m_hgrn_norm_g, m_w_branch, m_w_out, m_ln1_g, m_ln1_b, m_ffn_w_up, m_ffn_conv_w, m_ffn_conv_b, m_ffn_w_down, m_ln2_g, m_ln2_b, m_ple_w_proj, m_ple_w_gate, v_w_in, v_sgu_w_s, v_sgu_b_s, v_sgu_norm_g, v_sgu_norm_b, v_hgrn_lb_logits, v_hgrn_norm_g, v_w_branch, v_w_out, v_ln1_g, v_ln1_b, v_ffn_w_up, v_ffn_conv_w, v_ffn_conv_b, v_ffn_w_down, v_ln2_g, v_ln2_b, v_ple_w_proj, v_ple_w_gate):
    given = dict(x=x, p=p, w_in=w_in, sgu_w_s=sgu_w_s, sgu_b_s=sgu_b_s, sgu_norm_g=sgu_norm_g, sgu_norm_b=sgu_norm_b, hgrn_lb_logits=hgrn_lb_logits, hgrn_norm_g=hgrn_norm_g, w_branch=w_branch, w_out=w_out, ln1_g=ln1_g, ln1_b=ln1_b, ffn_w_up=ffn_w_up, ffn_conv_w=ffn_conv_w, ffn_conv_b=ffn_conv_b, ffn_w_down=ffn_w_down, ln2_g=ln2_g, ln2_b=ln2_b, ple_w_proj=ple_w_proj, ple_w_gate=ple_w_gate, loss_target=loss_target, m_w_in=m_w_in, m_sgu_w_s=m_sgu_w_s, m_sgu_b_s=m_sgu_b_s, m_sgu_norm_g=m_sgu_norm_g, m_sgu_norm_b=m_sgu_norm_b, m_hgrn_lb_logits=m_hgrn_lb_logits, m_hgrn_norm_g=m_hgrn_norm_g, m_w_branch=m_w_branch, m_w_out=m_w_out, m_ln1_g=m_ln1_g, m_ln1_b=m_ln1_b, m_ffn_w_up=m_ffn_w_up, m_ffn_conv_w=m_ffn_conv_w, m_ffn_conv_b=m_ffn_conv_b, m_ffn_w_down=m_ffn_w_down, m_ln2_g=m_ln2_g, m_ln2_b=m_ln2_b, m_ple_w_proj=m_ple_w_proj, m_ple_w_gate=m_ple_w_gate, v_w_in=v_w_in, v_sgu_w_s=v_sgu_w_s, v_sgu_b_s=v_sgu_b_s, v_sgu_norm_g=v_sgu_norm_g, v_sgu_norm_b=v_sgu_norm_b, v_hgrn_lb_logits=v_hgrn_lb_logits, v_hgrn_norm_g=v_hgrn_norm_g, v_w_branch=v_w_branch, v_w_out=v_w_out, v_ln1_g=v_ln1_g, v_ln1_b=v_ln1_b, v_ffn_w_up=v_ffn_w_up, v_ffn_conv_w=v_ffn_conv_w, v_ffn_conv_b=v_ffn_conv_b, v_ffn_w_down=v_ffn_w_down, v_ln2_g=v_ln2_g, v_ln2_b=v_ln2_b, v_ple_w_proj=v_ple_w_proj, v_ple_w_gate=v_ple_w_gate)
    weights = {n: given[n] for n in TWIN_WEIGHTS}
    shared = {n: given[n] for n in SHARED_INPUTS}
    per_example = {n: given[n] for n in ['x', 'p']}
    grad_fn = _jax.value_and_grad(_loss, argnums=(0, 1))

    def one_microbatch(ex, loss_target):
        ex = dict(ex)
        diff = ex.pop(TWIN_DIFF_INPUT)
        return grad_fn(weights, diff, {**shared, **ex}, loss_target)

    if N_MICROBATCH == 1:
        loss, (grad_w, grad_x) = one_microbatch(per_example, given["loss_target"])
    else:
        def body(carry, xs):
            loss_sum, grad_sum = carry
            l_k, (gw_k, gx_k) = one_microbatch(xs[0], xs[1])
            with _jax.named_scope("update"):
                return (loss_sum + l_k, _jax.tree.map(_jnp.add, grad_sum, gw_k)), gx_k

        init = (_jnp.zeros((), _jnp.float32), _jax.tree.map(_jnp.zeros_like, weights))
        (loss, grad_w), grad_x = _jax.lax.scan(body, init, (per_example, given["loss_target"]))
    with _jax.named_scope("update"):
        delta_w, new_m, new_v = {}, {}, {}
        for n in TWIN_WEIGHTS:
            delta_w[n], new_m[n], new_v[n] = _adamw(weights[n], grad_w[n], given["m_" + n], given["v_" + n])
    return (loss, grad_x, *[grad_w[n] for n in TWIN_WEIGHTS], *[delta_w[n] for n in TWIN_WEIGHTS],
            *[new_m[n] for n in TWIN_WEIGHTS], *[new_v[n] for n in TWIN_WEIGHTS])
```

```python
import functools
import math

import jax
import jax.numpy as jnp
from jax import lax
from jax.experimental import pallas as pl
from jax.experimental.pallas import tpu as pltpu

F32 = jnp.float32
BF16 = jnp.bfloat16

N_DEV = 8
D_MODEL = 1024
CHUNK = 64
SUB = 16
SGU_BLOCK = 128
GROUPS = 8
HEAD = 128
HEADS = 8
D_FF = 2816
LN_EPS = 1e-5
RMS_EPS = 1e-6
ALPHA = 2.0 ** 0.25
GELU_K = math.sqrt(2.0 / math.pi)
GELU_C = 0.044715
NEG = -1e30
ADAM_LR, ADAM_B1, ADAM_B2, ADAM_EPS, ADAM_WD, ADAM_STEP = 0.001, 0.9, 0.999, 1e-08, 0.01, 10
LANE = 128
VMEM_LIMIT = 48 * 1024 * 1024


def _params(sem):
    return pltpu.CompilerParams(dimension_semantics=sem, vmem_limit_bytes=VMEM_LIMIT)


def _pick(n, prefs):
    for t in prefs:
        if n % t == 0:
            return t
    return n


def _gelu(x):
    return 0.5 * x * (1.0 + jnp.tanh(GELU_K * (x + GELU_C * x * x * x)))


def _gelu_grad(x):
    t = jnp.tanh(GELU_K * (x + GELU_C * x * x * x))
    return 0.5 * (1.0 + t) + 0.5 * x * (1.0 - t * t) * GELU_K * (1.0 + 3.0 * GELU_C * x * x)


def _silu_grad(x, s):
    return s * (1.0 + x * (1.0 - s))


def _dot(a, b):
    return jnp.dot(a.astype(BF16), b.astype(BF16), preferred_element_type=F32)


def _dot_nt(a, b):
    return lax.dot_general(a.astype(BF16), b.astype(BF16), (((1,), (1,)), ((), ())), preferred_element_type=F32)


def _dot_tn(a, b):
    return lax.dot_general(a.astype(BF16), b.astype(BF16), (((0,), (0,)), ((), ())), preferred_element_type=F32)


def _mean(x):
    return jnp.mean(x, axis=-1, keepdims=True)


def _sum0(x):
    return jnp.sum(x, axis=0, keepdims=True)


def _mm(a, b, *, out_dtype, name, reduce_b=False, adds=()):
    squeeze = b.ndim == 2
    a3 = a if a.ndim == 3 else a[None]
    b3 = b if b.ndim == 3 else b[None]
    ba, m, k = a3.shape
    bb, _, n = b3.shape
    tm = _pick(m, (512, 256, 128))
    tn = _pick(n, (512, 256, 128))
    tk = k if k <= 1024 else _pick(k, (1408, 1024, 512))
    nk = k // tk
    if reduce_b:
        bo, steps = 1, bb * nk
        a_map = lambda o, i, j, r: ((r // nk) if ba > 1 else 0, i, r % nk)
        b_map = lambda o, i, j, r: (r // nk, r % nk, j)
    else:
        bo, steps = bb, nk
        a_map = lambda o, i, j, r: (o if ba > 1 else 0, i, r)
        b_map = lambda o, i, j, r: (o, r, j)
    o_map = lambda o, i, j, r: (o, i, j)
    add_arrays = [x if x.ndim == 3 else x[None] for x, _ in adds]
    add_scales = [s for _, s in adds]
    n_add = len(adds)

    def body(a_ref, b_ref, *rest):
        add_refs, o_ref, acc = rest[:n_add], rest[n_add], rest[n_add + 1]
        r = pl.program_id(3)

        @pl.when(r == 0)
        def _():
            acc[...] = jnp.zeros_like(acc)

        acc[...] += _dot(a_ref[...], b_ref[...])

        @pl.when(r == steps - 1)
        def _():
            out = acc[...]
            for ref, s in zip(add_refs, add_scales):
                out = out + s * ref[...].astype(F32)
            o_ref[...] = out.astype(o_ref.dtype)

    out = pl.pallas_call(
        body,
        grid=(bo, m // tm, n // tn, steps),
        in_specs=[pl.BlockSpec((None, tm, tk), a_map), pl.BlockSpec((None, tk, tn), b_map)]
        + [pl.BlockSpec((None, tm, tn), o_map) for _ in adds],
        out_specs=pl.BlockSpec((None, tm, tn), o_map),
        out_shape=jax.ShapeDtypeStruct((bo, m, n), out_dtype),
        scratch_shapes=[pltpu.VMEM((tm, tn), F32)],
        compiler_params=_params(("parallel", "parallel", "parallel", "arbitrary")),
        name=name,
    )(a3, b3, *add_arrays)
    return out[0] if (reduce_b or squeeze) else out


def _mm_tn(a, b, *, out_dtype, name):
    squeeze = b.ndim == 2
    b3 = b if b.ndim == 3 else b[None]
    t, m = a.shape
    bb, _, n = b3.shape
    tm = _pick(m, (512, 256, 128))
    tn = _pick(n, (512, 256, 128))
    tt = _pick(t, (512, 256, 128))
    steps = t // tt

    def body(a_ref, b_ref, o_ref, acc):
        r = pl.program_id(3)

        @pl.when(r == 0)
        def _():
            acc[...] = jnp.zeros_like(acc)

        acc[...] += _dot_tn(a_ref[...], b_ref[...])

        @pl.when(r == steps - 1)
        def _():
            o_ref[...] = acc[...].astype(o_ref.dtype)

    out = pl.pallas_call(
        body,
        grid=(bb, m // tm, n // tn, steps),
        in_specs=[pl.BlockSpec((tt, tm), lambda o, i, j, r: (r, i)),
                  pl.BlockSpec((None, tt, tn), lambda o, i, j, r: (o, r, j))],
        out_specs=pl.BlockSpec((None, tm, tn), lambda o, i, j, r: (o, i, j)),
        out_shape=jax.ShapeDtypeStruct((bb, m, n), out_dtype),
        scratch_shapes=[pltpu.VMEM((tm, tn), F32)],
        compiler_params=_params(("parallel", "parallel", "parallel", "arbitrary")),
        name=name,
    )(a, b3)
    return out[0] if squeeze else out


def _rowwise(fn, rows, consts, row_outs, acc_outs, *, tile, name):
    first = rows[0][0] if isinstance(rows[0], tuple) else rows[0]
    t = first.shape[-2]
    steps = t // tile
    arrays, in_specs = [], []
    for r in rows:
        if isinstance(r, tuple) and isinstance(r[1], pl.BlockSpec):
            arrays.append(r[0])
            in_specs.append(r[1])
        elif isinstance(r, tuple):
            arr, bidx = r
            arrays.append(arr)
            in_specs.append(pl.BlockSpec((None, tile, arr.shape[-1]), functools.partial(lambda i, b: (b, i, 0), b=bidx)))
        else:
            arrays.append(r)
            in_specs.append(pl.BlockSpec((tile, r.shape[-1]), lambda i: (i, 0)))
    for c in consts:
        arrays.append(c)
        in_specs.append(pl.BlockSpec(c.shape, lambda i: (0, 0)))
    n_in, n_row = len(arrays), len(row_outs)
    out_shape = [jax.ShapeDtypeStruct((t, w), dt) for w, dt in row_outs]
    out_specs = [pl.BlockSpec((tile, w), lambda i: (i, 0)) for w, _ in row_outs]
    out_shape += [jax.ShapeDtypeStruct(s, F32) for s in acc_outs]
    out_specs += [pl.BlockSpec(s, lambda i: (0, 0)) for s in acc_outs]

    def body(*refs):
        ins, outs = refs[:n_in], refs[n_in:]
        i = pl.program_id(0)
        res = fn(i, steps, *[r[...] for r in ins])
        res = res if isinstance(res, (tuple, list)) else (res,)
        for ref, val in zip(outs[:n_row], res[:n_row]):
            ref[...] = val.astype(ref.dtype)
        if acc_outs:
            @pl.when(i == 0)
            def _():
                for ref in outs[n_row:]:
                    ref[...] = jnp.zeros_like(ref)

            for ref, val in zip(outs[n_row:], res[n_row:]):
                ref[...] += val

    return pl.pallas_call(
        body,
        grid=(steps,),
        in_specs=in_specs,
        out_specs=out_specs,
        out_shape=out_shape,
        compiler_params=_params(("arbitrary",)),
        name=name,
    )(*arrays)


def _ln_stats(z):
    mu = _mean(z)
    zc = z - mu
    rstd = lax.rsqrt(_mean(zc * zc) + LN_EPS)
    return zc * rstd, rstd


def _ln_bwd(dy, xhat, rstd, g):
    dxh = dy * g
    return rstd * (dxh - _mean(dxh) - xhat * _mean(dxh * xhat))


def _sgu_fwd(h, wm, bs_t, g_v, b_v):
    t = h.shape[1]

    def body(u_ref, v_ref, wm_ref, bs_ref, g_ref, b_ref, y_ref):
        xhat, _ = _ln_stats(_gelu(v_ref[...]))
        vn = (xhat * g_ref[...] + b_ref[...]).astype(BF16)
        gu = _gelu(u_ref[...])
        for g in range(GROUPS):
            sl = slice(g * HEAD, (g + 1) * HEAD)
            mixed = _dot(wm_ref[g], vn[:, sl]) + bs_ref[:, g:g + 1]
            y_ref[:, sl] = (gu[:, sl] * mixed).astype(BF16)

    blk = lambda b: pl.BlockSpec((None, SGU_BLOCK, D_MODEL), functools.partial(lambda i, b: (b, i, 0), b=b))
    whole = lambda s: pl.BlockSpec(s, lambda i: (0,) * len(s))
    return pl.pallas_call(
        body,
        grid=(t // SGU_BLOCK,),
        in_specs=[blk(0), blk(1), whole(wm.shape), whole(bs_t.shape), whole(g_v.shape), whole(b_v.shape)],
        out_specs=pl.BlockSpec((SGU_BLOCK, D_MODEL), lambda i: (i, 0)),
        out_shape=jax.ShapeDtypeStruct((t, D_MODEL), BF16),
        compiler_params=_params(("parallel",)),
        name="sgu_fwd",
    )(h, h, wm, bs_t, g_v, b_v)


def _sgu_bwd(h, dy, wm, wm_t, bs_t, g_v, b_v):
    t = h.shape[1]

    def body(u_ref, v_ref, dy_ref, wm_ref, wmt_ref, bs_ref, g_ref, b_ref,
             du_ref, dv_ref, dw_ref, dbs_ref, dg_ref, db_ref, dvn_ref):
        i = pl.program_id(0)

        @pl.when(i == 0)
        def _():
            dw_ref[...] = jnp.zeros_like(dw_ref)
            dbs_ref[...] = jnp.zeros_like(dbs_ref)
            dg_ref[...] = jnp.zeros_like(dg_ref)
            db_ref[...] = jnp.zeros_like(db_ref)

        u = u_ref[...]
        v = v_ref[...]
        xhat, rstd = _ln_stats(_gelu(v))
        vn = (xhat * g_ref[...] + b_ref[...]).astype(BF16)
        gu = _gelu(u)
        gup = _gelu_grad(u)
        lane = lax.broadcasted_iota(jnp.int32, (SGU_BLOCK, LANE), 1)
        dbs = jnp.zeros((SGU_BLOCK, LANE), F32)
        for g in range(GROUPS):
            sl = slice(g * HEAD, (g + 1) * HEAD)
            vn_g = vn[:, sl]
            mixed = _dot(wm_ref[g], vn_g) + bs_ref[:, g:g + 1]
            dy_g = dy_ref[:, sl]
            du_ref[:, sl] = (dy_g * mixed * gup[:, sl]).astype(BF16)
            dmix = dy_g * gu[:, sl]
            dmb = dmix.astype(BF16)
            dvn_ref[:, sl] = _dot(wmt_ref[g], dmb)
            dw_ref[g] += _dot_nt(dmb, vn_g)
            dbs = dbs + jnp.where(lane == g, jnp.sum(dmix, axis=1, keepdims=True), 0.0)
        dbs_ref[...] += dbs
        dvn = dvn_ref[...]
        dg_ref[...] += _sum0(dvn * xhat)
        db_ref[...] += _sum0(dvn)
        dv_ref[...] = (_ln_bwd(dvn, xhat, rstd, g_ref[...]) * _gelu_grad(v)).astype(BF16)

    blk = lambda b: pl.BlockSpec((None, SGU_BLOCK, D_MODEL), functools.partial(lambda i, b: (b, i, 0), b=b))
    row = pl.BlockSpec((SGU_BLOCK, D_MODEL), lambda i: (i, 0))
    whole = lambda s: pl.BlockSpec(s, lambda i: (0,) * len(s))
    vec = (1, D_MODEL)
    return pl.pallas_call(
        body,
        grid=(t // SGU_BLOCK,),
        in_specs=[blk(0), blk(1), row, whole(wm.shape), whole(wm_t.shape), whole(bs_t.shape), whole(vec), whole(vec)],
        out_specs=[row, row, whole(wm.shape), whole((SGU_BLOCK, LANE)), whole(vec), whole(vec)],
        out_shape=[jax.ShapeDtypeStruct((t, D_MODEL), BF16), jax.ShapeDtypeStruct((t, D_MODEL), BF16),
                   jax.ShapeDtypeStruct(wm.shape, F32), jax.ShapeDtypeStruct((SGU_BLOCK, LANE), F32),
                   jax.ShapeDtypeStruct(vec, F32), jax.ShapeDtypeStruct(vec, F32)],
        scratch_shapes=[pltpu.VMEM((SGU_BLOCK, D_MODEL), F32)],
        compiler_params=_params(("arbitrary",)),
        name="sgu_bwd",
    )(h, h, dy, wm, wm_t, bs_t, g_v, b_v)


def _split3(x):
    hi = x.astype(BF16)
    r1 = x - hi.astype(F32)
    mid = r1.astype(BF16)
    lo = (r1 - mid.astype(F32)).astype(BF16)
    return hi, mid, lo


def _tri_matmul(tri, x):
    hi, mid, lo = _split3(x)
    dot = lambda p: jnp.dot(tri, p, preferred_element_type=F32)
    return dot(hi) + dot(mid) + dot(lo)


def _lower_bound(logits):
    l0, l1 = logits[0:1, :], logits[1:2, :]
    mx = jnp.maximum(l0, l1)
    e0, e1 = jnp.exp(l0 - mx), jnp.exp(l1 - mx)
    return e0 / (e0 + e1)


def _hgrn_gates(q_raw, f_raw, lb):
    q = q_raw * jax.nn.sigmoid(q_raw)
    sig = jax.nn.sigmoid(f_raw)
    f = lb + (1.0 - lb) * sig
    row = lax.broadcasted_iota(jnp.int32, (CHUNK, CHUNK), 0)
    col = lax.broadcasted_iota(jnp.int32, (CHUNK, CHUNK), 1)
    c = _tri_matmul((row >= col).astype(BF16), jnp.log(f))
    return q, sig, f, 1.0 - f, c


def _offdiag_terms(qh, kh, ch, tb):
    rows = slice(tb * SUB, (tb + 1) * SUB)
    r = ch[tb * SUB - 1:tb * SUB, :]
    eqh = jnp.exp(ch[rows] - r)
    ekh = jnp.exp(jnp.minimum(r - ch, 0.0))
    return rows, eqh, qh[rows] * eqh, ekh, kh * ekh


def _hgrn_fwd(h, logits, g_norm):
    t = h.shape[1]
    nc = t // CHUNK

    def body(q_ref, f_ref, i_ref, og_ref, lg_ref, gn_ref, y_ref, o_ref, sall_ref, st_ref, q_s, k_s, c_s):
        @pl.when(pl.program_id(0) == 0)
        def _():
            st_ref[...] = jnp.zeros_like(st_ref)

        lb = _lower_bound(lg_ref[...])
        q, _, _, k, c = _hgrn_gates(q_ref[...], f_ref[...], lb)
        q_s[...] = q
        k_s[...] = k
        c_s[...] = c
        col64 = lax.broadcasted_iota(jnp.int32, (SUB, CHUNK), 1)
        trow = lax.broadcasted_iota(jnp.int32, (SUB, HEAD), 0)

        def head(hd, carry):
            sl = pl.ds(pl.multiple_of(hd * HEAD, HEAD), HEAD)
            qh, kh, ch, ih = q_s[:, sl], k_s[:, sl], c_s[:, sl], i_ref[:, sl]
            st = st_ref[hd]
            sall_ref[hd] = st
            c_last = ch[CHUNK - 1:CHUNK, :]
            o = _dot_nt(qh * jnp.exp(ch), st)
            st_ref[hd] = st * jnp.exp(c_last) + _dot_tn(ih, kh * jnp.exp(c_last - ch))
            a_rows = [jnp.zeros((SUB, CHUNK), F32)]
            for tb in range(1, CHUNK // SUB):
                _, _, q_hat, _, k_hat = _offdiag_terms(qh, kh, ch, tb)
                a_rows.append(jnp.where(col64 < tb * SUB, _dot_nt(q_hat, k_hat), 0.0))
            o = o + _dot(jnp.concatenate(a_rows, axis=0), ih)
            o_rows = []
            for b in range(CHUNK // SUB):
                rows = slice(b * SUB, (b + 1) * SUB)
                qb, cb, kb, ib = qh[rows], ch[rows], kh[rows], ih[rows]
                ob = jnp.zeros((SUB, HEAD), F32)
                for s in range(SUB):
                    dec = jnp.exp(jnp.where(trow >= s, cb - cb[s:s + 1, :], NEG))
                    a = jnp.sum(qb * dec * kb[s:s + 1, :], axis=1, keepdims=True)
                    ob = ob + a * ib[s:s + 1, :]
                o_rows.append(ob)
            o = o + jnp.concatenate(o_rows, axis=0)
            o_ref[:, sl] = o
            og = og_ref[:, sl]
            on = o * lax.rsqrt(_mean(o * o) + RMS_EPS)
            y_ref[:, sl] = (on * gn_ref[:, sl] * (og * jax.nn.sigmoid(og))).astype(BF16)
            return carry

        lax.fori_loop(0, HEADS, head, 0)

    blk = lambda b: pl.BlockSpec((None, CHUNK, D_MODEL), functools.partial(lambda n, b: (b, n, 0), b=b))
    row = pl.BlockSpec((CHUNK, D_MODEL), lambda n: (n, 0))
    whole = lambda s: pl.BlockSpec(s, lambda n: (0,) * len(s))
    return pl.pallas_call(
        body,
        grid=(nc,),
        in_specs=[blk(2), blk(3), blk(4), blk(5), whole(logits.shape), whole(g_norm.shape)],
        out_specs=[row, row, pl.BlockSpec((None, HEADS, HEAD, HEAD), lambda n: (n, 0, 0, 0))],
        out_shape=[jax.ShapeDtypeStruct((t, D_MODEL), BF16), jax.ShapeDtypeStruct((t, D_MODEL), F32),
                   jax.ShapeDtypeStruct((nc, HEADS, HEAD, HEAD), F32)],
        scratch_shapes=[pltpu.VMEM((HEADS, HEAD, HEAD), F32)] + [pltpu.VMEM((CHUNK, D_MODEL), F32)] * 3,
        compiler_params=_params(("arbitrary",)),
        name="hgrn_fwd",
    )(h, h, h, h, logits, g_norm)


def _hgrn_bwd(h, o_all, dy, states, logits, g_norm):
    t = h.shape[1]
    nc = t // CHUNK

    def body(q_ref, f_ref, i_ref, og_ref, o_ref, dy_ref, sall_ref, lg_ref, gn_ref,
             dq_ref, df_ref, di_ref, dog_ref, dlg_ref, dgn_ref,
             dst_ref, dlb_ref, q_s, k_s, c_s, dq_s, dk_s, dc_s):
        n = pl.program_id(0)

        @pl.when(n == 0)
        def _():
            dst_ref[...] = jnp.zeros_like(dst_ref)
            dlb_ref[...] = jnp.zeros_like(dlb_ref)
            dgn_ref[...] = jnp.zeros_like(dgn_ref)

        lb = _lower_bound(lg_ref[...])
        q_raw = q_ref[...]
        q, sig, f, k, c = _hgrn_gates(q_raw, f_ref[...], lb)
        q_s[...] = q
        k_s[...] = k
        c_s[...] = c
        col64 = lax.broadcasted_iota(jnp.int32, (SUB, CHUNK), 1)
        trow = lax.broadcasted_iota(jnp.int32, (SUB, HEAD), 0)
        row64 = lax.broadcasted_iota(jnp.int32, (CHUNK, HEAD), 0)

        def head(hd, carry):
            sl = pl.ds(pl.multiple_of(hd * HEAD, HEAD), HEAD)
            qh, kh, ch, ih = q_s[:, sl], k_s[:, sl], c_s[:, sl], i_ref[:, sl]
            st = sall_ref[hd]
            dst = dst_ref[hd]
            oh, dyh, og, gn = o_ref[:, sl], dy_ref[:, sl], og_ref[:, sl], gn_ref[:, sl]
            sg = jax.nn.sigmoid(og)
            sil = og * sg
            rms = lax.rsqrt(_mean(oh * oh) + RMS_EPS)
            on = oh * rms
            dog_ref[:, sl] = (dyh * on * gn * _silu_grad(og, sg)).astype(BF16)
            dgn_ref[:, sl] += _sum0(dyh * on * sil)
            don = dyh * gn * sil
            do = rms * (don - on * _mean(don * on))
            dob = do.astype(BF16)

            c_last = ch[CHUNK - 1:CHUNK, :]
            eq = jnp.exp(ch)
            q_til = qh * eq
            ekl = jnp.exp(c_last - ch)
            k_til = kh * ekl
            ecl = jnp.exp(c_last)
            dq_til = _dot(dob, st)
            dk_til = _dot(ih, dst)
            di = _dot_nt(k_til, dst)
            dc_last = _sum0(dk_til * k_til) + _sum0(dst * st) * ecl
            dst_ref[hd] = _dot_tn(dob, q_til) + dst * ecl
            dq = dq_til * eq
            dc = dq_til * q_til - dk_til * k_til
            dk = dk_til * ekl

            da_full = _dot_nt(dob, ih)
            a_rows = [jnp.zeros((SUB, CHUNK), F32)]
            dq_rows = [jnp.zeros((SUB, HEAD), F32)]
            dc_rows = [jnp.zeros((SUB, HEAD), F32)]
            for tb in range(1, CHUNK // SUB):
                rows, eqh, q_hat, ekh, k_hat = _offdiag_terms(qh, kh, ch, tb)
                keep = col64 < tb * SUB
                a_rows.append(jnp.where(keep, _dot_nt(q_hat, k_hat), 0.0))
                da = jnp.where(keep, da_full[rows], 0.0)
                dq_hat = _dot(da, k_hat)
                dk_hat = _dot_tn(da, q_hat)
                dq_rows.append(dq_hat * eqh)
                dc_rows.append(dq_hat * q_hat)
                dk = dk + dk_hat * ekh
                dc = dc - dk_hat * k_hat
            di = di + _dot_tn(jnp.concatenate(a_rows, axis=0), dob)

            dk_rows, di_rows = [], []
            for b in range(CHUNK // SUB):
                rows = slice(b * SUB, (b + 1) * SUB)
                qb, cb, kb, ib, dob_ = qh[rows], ch[rows], kh[rows], ih[rows], do[rows]
                dqb, dcb = dq_rows[b], dc_rows[b]
                dkb = jnp.zeros((SUB, HEAD), F32)
                dib = jnp.zeros((SUB, HEAD), F32)
                for s in range(SUB):
                    ks = kb[s:s + 1, :]
                    dec = jnp.exp(jnp.where(trow >= s, cb - cb[s:s + 1, :], NEG))
                    a = jnp.sum(qb * dec * ks, axis=1, keepdims=True)
                    da = jnp.sum(dob_ * ib[s:s + 1, :], axis=1, keepdims=True)
                    gk = da * dec
                    dqb = dqb + gk * ks
                    dks = _sum0(gk * qb)
                    dcb = dcb + gk * ks * qb - jnp.where(trow == s, dks * ks, 0.0)
                    dkb = dkb + jnp.where(trow == s, dks, 0.0)
                    dib = dib + jnp.where(trow == s, _sum0(a * dob_), 0.0)
                dq_rows[b], dc_rows[b] = dqb, dcb
                dk_rows.append(dkb)
                di_rows.append(dib)
            dq = dq + jnp.concatenate(dq_rows, axis=0)
            dk = dk + jnp.concatenate(dk_rows, axis=0)
            dc = dc + jnp.concatenate(dc_rows, axis=0) + jnp.where(row64 == CHUNK - 1, dc_last, 0.0)
            di_ref[:, sl] = (di + jnp.concatenate(di_rows, axis=0)).astype(BF16)
            dq_s[:, sl] = dq
            dk_s[:, sl] = dk
            dc_s[:, sl] = dc
            return carry

        lax.fori_loop(0, HEADS, head, 0)

        row = lax.broadcasted_iota(jnp.int32, (CHUNK, CHUNK), 0)
        col = lax.broadcasted_iota(jnp.int32, (CHUNK, CHUNK), 1)
        dlf = _tri_matmul((row <= col).astype(BF16), dc_s[...])
        df = dlf / f - dk_s[...]
        dlb_ref[...] += _sum0(df * (1.0 - sig))
        df_ref[...] = (df * (1.0 - lb) * sig * (1.0 - sig)).astype(BF16)
        dq_ref[...] = (dq_s[...] * _silu_grad(q_raw, jax.nn.sigmoid(q_raw))).astype(BF16)

        @pl.when(n == nc - 1)
        def _():
            d0 = dlb_ref[...] * lb * (1.0 - lb)
            dlg_ref[0:1, :] = d0
            dlg_ref[1:2, :] = -d0

    rev = lambda n: nc - 1 - n
    blk = lambda b: pl.BlockSpec((None, CHUNK, D_MODEL), functools.partial(lambda n, b: (b, rev(n), 0), b=b))
    row = pl.BlockSpec((CHUNK, D_MODEL), lambda n: (rev(n), 0))
    whole = lambda s: pl.BlockSpec(s, lambda n: (0,) * len(s))
    vec = (1, D_MODEL)
    act = jax.ShapeDtypeStruct((t, D_MODEL), BF16)
    return pl.pallas_call(
        body,
        grid=(nc,),
        in_specs=[blk(2), blk(3), blk(4), blk(5), row, row,
                  pl.BlockSpec((None, HEADS, HEAD, HEAD), lambda n: (rev(n), 0, 0, 0)),
                  whole(logits.shape), whole(vec)],
        out_specs=[row, row, row, row, whole((2, D_MODEL)), whole(vec)],
        out_shape=[act, act, act, act, jax.ShapeDtypeStruct((2, D_MODEL), F32), jax.ShapeDtypeStruct(vec, F32)],
        scratch_shapes=[pltpu.VMEM((HEADS, HEAD, HEAD), F32), pltpu.VMEM(vec, F32)]
        + [pltpu.VMEM((CHUNK, D_MODEL), F32)] * 6,
        compiler_params=_params(("arbitrary",)),
        name="hgrn_bwd",
    )(h, h, h, h, o_all, dy, states, logits, g_norm)


def _merge_fwd(i, n, ga, gb, za, zb):
    return jax.nn.sigmoid(ga) * za + jax.nn.sigmoid(gb) * zb


def _merge_bwd(i, n, ga, gb, za, zb, dm):
    sa, sb = jax.nn.sigmoid(ga), jax.nn.sigmoid(gb)
    return dm * sa, dm * sb, dm * za * sa * (1.0 - sa), dm * zb * sb * (1.0 - sb)


def _ln1_fwd(i, n, x, r1, g, b):
    xhat, _ = _ln_stats(ALPHA * x + r1)
    x1 = xhat * g + b
    return x1, x1


def _ln1_bwd(i, n, x, r1, dx1, g):
    xhat, rstd = _ln_stats(ALPHA * x + r1)
    dz = _ln_bwd(dx1, xhat, rstd, g)
    return dz, dz, _sum0(dx1 * xhat), _sum0(dx1)


def _ln2_loss(i, n, x1, fo, pg, pp, tgt, g, b):
    sg = jax.nn.sigmoid(pg)
    xhat, rstd = _ln_stats(ALPHA * x1 + fo + sg * pp)
    diff = xhat * g + b - tgt
    loss = 0.5 * jnp.sum(_mean(diff * diff), axis=0, keepdims=True)
    dy = diff * (1.0 / D_MODEL)
    dz = _ln_bwd(dy, xhat, rstd, g)
    return (dz, dz, dz * pp * sg * (1.0 - sg), dz * sg,
            jnp.broadcast_to(loss, (8, LANE)), _sum0(dy * xhat), _sum0(dy))


def _shift_down(cur, halo, tile):
    row = lax.broadcasted_iota(jnp.int32, cur.shape, 0)
    m1 = jnp.where(row == 0, halo[7:8, :], pltpu.roll(cur, 1, 0))
    m2 = jnp.where(row == 0, halo[6:7, :], jnp.where(row == 1, halo[7:8, :], pltpu.roll(cur, 2, 0)))
    return m1, m2


def _shift_up(cur, halo, tile):
    row = lax.broadcasted_iota(jnp.int32, cur.shape, 0)
    p1 = jnp.where(row == tile - 1, halo[0:1, :], pltpu.roll(cur, tile - 1, 0))
    p2 = jnp.where(row == tile - 2, halo[0:1, :], jnp.where(row == tile - 1, halo[1:2, :], pltpu.roll(cur, tile - 2, 0)))
    return p1, p2


def _conv_pre(i, gate, halo, w, b, tile):
    halo = jnp.where(i == 0, 0.0, halo)
    m1, m2 = _shift_down(gate, halo, tile)
    return w[0:1, :] * m2 + w[1:2, :] * m1 + w[2:3, :] * gate + b, m1, m2


def _conv_fwd(tile, i, n, gate, halo, val, w, b):
    cg, _, _ = _conv_pre(i, gate, halo, w, b, tile)
    return _gelu(cg) * val


def _conv_bwd_a(tile, i, n, gate, halo, val, dhid, w, b):
    cg, m1, m2 = _conv_pre(i, gate, halo, w, b, tile)
    dcg = dhid * val * _gelu_grad(cg)
    return dcg, dhid * _gelu(cg), _sum0(dcg * m2), _sum0(dcg * m1), _sum0(dcg * gate), _sum0(dcg)


def _conv_bwd_b(tile, i, n, dcg, halo, w):
    halo = jnp.where(i == n - 1, 0.0, halo)
    p1, p2 = _shift_up(dcg, halo, tile)
    return w[2:3, :] * dcg + w[1:2, :] * p1 + w[0:1, :] * p2


def _halo_spec(width, tile, t, nxt):
    per = tile // 8
    last = t // 8 - 1
    if nxt:
        return pl.BlockSpec((8, width), lambda i: (jnp.minimum((i + 1) * per, last), 0))
    return pl.BlockSpec((8, width), lambda i: (jnp.maximum(i * per - 1, 0), 0))


def _adamw(i, n, w, m, v, parts):
    g = parts[0].astype(F32)
    for j in range(1, N_DEV):
        g = g + parts[j].astype(F32)
    m_new = ADAM_B1 * m + (1.0 - ADAM_B1) * g
    v_new = ADAM_B2 * v + (1.0 - ADAM_B2) * (g * g)
    m_hat = m_new / (1.0 - ADAM_B1 ** ADAM_STEP)
    v_hat = v_new / (1.0 - ADAM_B2 ** ADAM_STEP)
    delta = -ADAM_LR * (m_hat / (jnp.sqrt(v_hat) + ADAM_EPS) + ADAM_WD * w)
    return g, delta, m_new, v_new


def _adam_call(w, m, v, parts, name):
    r, c = w.shape
    tile = _pick(r, (256, 128)) if r > 256 else r
    spec = pl.BlockSpec((N_DEV, tile, c), lambda i: (0, i, 0))
    return _rowwise(_adamw, [w, m, v, (parts, spec)], [], [(c, F32)] * 4, [], tile=tile, name=name)


def _peer(k):
    x, y, c = lax.axis_index("x"), lax.axis_index("y"), lax.axis_index("c")
    px = x ^ ((k >> 2) & 1)
    py = y ^ ((k >> 1) & 1)
    pc = c ^ (k & 1)
    return (px, py, pc), 4 * px + 2 * py + pc


def _my_index():
    return 4 * lax.axis_index("x") + 2 * lax.axis_index("y") + lax.axis_index("c")


def _exchange(src, name, scatter):
    shape = src.shape[1:] if scatter else src.shape

    def body(src_ref, out_ref, send_sems, recv_sems, local_sem):
        me = _my_index()
        mine = src_ref.at[me] if scatter else src_ref
        local = pltpu.make_async_copy(mine, out_ref.at[me], local_sem)
        local.start()
        copies = []
        for k in range(1, N_DEV):
            dev, idx = _peer(k)
            copies.append(pltpu.make_async_remote_copy(
                src_ref=src_ref.at[idx] if scatter else src_ref,
                dst_ref=out_ref.at[me],
                send_sem=send_sems.at[k - 1], recv_sem=recv_sems.at[k - 1],
                device_id=dev, device_id_type=pl.DeviceIdType.MESH))
        for cp in copies:
            cp.start()
        for k in range(1, N_DEV):
            dev, idx = _peer(k)
            pltpu.make_async_remote_copy(
                src_ref=mine, dst_ref=out_ref.at[idx],
                send_sem=send_sems.at[k - 1], recv_sem=recv_sems.at[k - 1],
                device_id=dev, device_id_type=pl.DeviceIdType.MESH).wait_recv()
        for cp in copies:
            cp.wait_send()
        local.wait()

    return pl.pallas_call(
        body,
        in_specs=[pl.BlockSpec(memory_space=pl.ANY)],
        out_specs=pl.BlockSpec(memory_space=pl.ANY),
        out_shape=jax.ShapeDtypeStruct((N_DEV,) + tuple(shape), src.dtype),
        scratch_shapes=[pltpu.SemaphoreType.DMA((N_DEV - 1,)), pltpu.SemaphoreType.DMA((N_DEV - 1,)),
                        pltpu.SemaphoreType.DMA],
        name=name,
    )(src)


def _local_step(x, p, tgt, wts, small):
    t = x.shape[0]
    tile = _pick(t, (256, 128))
    d = D_MODEL
    act_b, act_f = (d, BF16), (d, F32)

    chunk_id = jnp.arange(SGU_BLOCK) // CHUNK
    mask = chunk_id[:, None] >= chunk_id[None, :]
    wm = jnp.where(mask[None], small["sgu_w_s"], 0.0)
    wm_b = wm.astype(BF16)
    wm_t = jnp.swapaxes(wm, 1, 2).astype(BF16)
    bs_t = small["sgu_b_s"].T

    h = _mm(x, wts["w_in"], out_dtype=F32, name="mm_h")
    y_a = _sgu_fwd(h, wm_b, bs_t, small["sgu_norm_g"], small["sgu_norm_b"])
    y_b, o_all, states = _hgrn_fwd(h, small["lb_logits"], small["hgrn_norm_g"])
    z_a = _mm(y_a, wts["w_a"], out_dtype=F32, name="mm_za")
    z_b = _mm(y_b, wts["w_b"], out_dtype=F32, name="mm_zb")
    merged, = _rowwise(_merge_fwd, [(h, 6), (h, 7), z_a, z_b], [], [act_b], [], tile=tile, name="merge_fwd")
    r1 = _mm(merged, wts["w_o"], out_dtype=F32, name="mm_r1")
    x1, x1_b = _rowwise(_ln1_fwd, [x, r1], [small["ln1_g"], small["ln1_b"]], [act_f, act_b], [], tile=tile, name="ln1_fwd")
    gate = _mm(x1_b, wts["w_g"], out_dtype=F32, name="mm_gate")
    val = _mm(x1_b, wts["w_v"], out_dtype=F32, name="mm_val")
    pg = _mm(x1_b, wts["w_pg"], out_dtype=F32, name="mm_pg")
    pp = _mm(p, wts["w_pp"], out_dtype=F32, name="mm_pp")
    hid, = _rowwise(functools.partial(_conv_fwd, tile), [gate, (gate, _halo_spec(D_FF, tile, t, False)), val],
                    [small["conv_w"], small["conv_b"]], [(D_FF, BF16)], [], tile=tile, name="conv_fwd")
    fo = _mm(hid, wts["w_down"], out_dtype=F32, name="mm_down")
    dz2, dz2_b, dpg, dpp, loss, dg2, db2 = _rowwise(
        _ln2_loss, [x1, fo, pg, pp, tgt], [small["ln2_g"], small["ln2_b"]],
        [act_f, act_b, act_b, act_b], [(8, LANE), (1, d), (1, d)], tile=tile, name="ln2_loss")

    dhid = _mm(dz2_b, wts["w_down_t"], out_dtype=F32, name="mm_dhid")
    g_down = _mm_tn(hid, dz2_b, out_dtype=BF16, name="mm_g_down")
    dcg, dval, dcw0, dcw1, dcw2, dcb = _rowwise(
        functools.partial(_conv_bwd_a, tile), [gate, (gate, _halo_spec(D_FF, tile, t, False)), val, dhid],
        [small["conv_w"], small["conv_b"]], [(D_FF, F32), (D_FF, BF16)], [(1, D_FF)] * 4, tile=tile, name="conv_bwd_a")
    dgate, = _rowwise(functools.partial(_conv_bwd_b, tile), [dcg, (dcg, _halo_spec(D_FF, tile, t, True))],
                      [small["conv_w"]], [(D_FF, BF16)], [], tile=tile, name="conv_bwd_b")
    g_g = _mm_tn(x1_b, dgate, out_dtype=BF16, name="mm_g_gate")
    g_v = _mm_tn(x1_b, dval, out_dtype=BF16, name="mm_g_val")
    g_pg = _mm_tn(x1_b, dpg, out_dtype=BF16, name="mm_g_pg")
    g_pp = _mm_tn(p, dpp, out_dtype=BF16, name="mm_g_pp")
    dx1 = _mm(dgate, wts["w_g_t"], out_dtype=F32, name="mm_dx1_gate", adds=[(dz2, ALPHA)])
    dx1 = _mm(dval, wts["w_v_t"], out_dtype=F32, name="mm_dx1_val", adds=[(dx1, 1.0)])
    dx1 = _mm(dpg, wts["w_pg_t"], out_dtype=F32, name="mm_dx1_pg", adds=[(dx1, 1.0)])
    dz1, dz1_b, dg1, db1 = _rowwise(_ln1_bwd, [x, r1, dx1], [small["ln1_g"]], [act_f, act_b], [(1, d), (1, d)],
                                    tile=tile, name="ln1_bwd")
    g_o = _mm_tn(merged, dz1_b, out_dtype=BF16, name="mm_g_o")
    dm = _mm(dz1_b, wts["w_o_t"], out_dtype=F32, name="mm_dm")
    dza, dzb, dga, dgb = _rowwise(_merge_bwd, [(h, 6), (h, 7), z_a, z_b, dm], [], [act_b] * 4, [], tile=tile, name="merge_bwd")
    g_a = _mm_tn(y_a, dza, out_dtype=BF16, name="mm_g_a")
    g_b = _mm_tn(y_b, dzb, out_dtype=BF16, name="mm_g_b")
    dy_a = _mm(dza, wts["w_a_t"], out_dtype=F32, name="mm_dya")
    dy_b = _mm(dzb, wts["w_b_t"], out_dtype=F32, name="mm_dyb")
    du, dv, dws, dbs, dgv_n, dbv_n = _sgu_bwd(h, dy_a, wm_b, wm_t, bs_t, small["sgu_norm_g"], small["sgu_norm_b"])
    dq, df, di, dog, dlogits, dgn = _hgrn_bwd(h, o_all, dy_b, states, small["lb_logits"], small["hgrn_norm_g"])
    dh = jnp.stack([du, dv, dq, df, di, dog, dga, dgb])
    g_in = _mm_tn(x, dh, out_dtype=BF16, name="mm_g_in")
    grad_x = _mm(dh, wts["w_in_t"], out_dtype=F32, name="mm_dx", reduce_b=True, adds=[(dz1, ALPHA)])

    big = dict(w_in=g_in, w_a=g_a, w_b=g_b, w_o=g_o, w_g=g_g, w_v=g_v, w_down=g_down, w_pp=g_pp, w_pg=g_pg)
    sm = dict(sgu_w_s=jnp.where(mask[None], dws, 0.0), sgu_b_s=dbs[:, :GROUPS].T, sgu_norm_g=dgv_n, sgu_norm_b=dbv_n,
              lb_logits=dlogits, hgrn_norm_g=dgn, ln1_g=dg1, ln1_b=db1, conv_w=jnp.concatenate([dcw0, dcw1, dcw2], axis=0),
              conv_b=dcb, ln2_g=dg2, ln2_b=db2)
    return loss, grad_x, big, sm


_SMALL = ["sgu_w_s", "sgu_b_s", "sgu_norm_g", "sgu_norm_b", "hgrn_lb_logits", "hgrn_norm_g",
          "ln1_g", "ln1_b", "ffn_conv_b", "ln2_g", "ln2_b"]


def _rows128(a):
    flat = a.reshape(-1)
    rows = -(-flat.shape[0] // (8 * LANE)) * 8
    return jnp.pad(flat, (0, rows * LANE - flat.shape[0])).reshape(rows, LANE)


def _pack(parts):
    blocks = [_rows128(a) for a in parts]
    return jnp.concatenate(blocks, axis=0), [b.shape[0] for b in blocks]


def _unpack(packed, shapes, rows):
    out, r0 = [], 0
    for shp, r in zip(shapes, rows):
        n = math.prod(shp)
        out.append(packed[r0:r0 + r].reshape(-1)[:n].reshape(shp))
        r0 += r
    return out


def kernel(x, p, w_in, sgu_w_s, sgu_b_s, sgu_norm_g, sgu_norm_b, hgrn_lb_logits, hgrn_norm_g, w_branch, w_out, ln1_g, ln1_b, ffn_w_up, ffn_conv_w, ffn_conv_b, ffn_w_down, ln2_g, ln2_b, ple_w_proj, ple_w_gate, loss_target, m_w_in, m_sgu_w_s, m_sgu_b_s, m_sgu_norm_g, m_sgu_norm_b, m_hgrn_lb_logits, m_hgrn_norm_g, m_w_branch, m_w_out, m_ln1_g, m_ln1_b, m_ffn_w_up, m_ffn_conv_w, m_ffn_conv_b, m_ffn_w_down, m_ln2_g, m_ln2_b, m_ple_w_proj, m_ple_w_gate, v_w_in, v_sgu_w_s, v_sgu_b_s, v_sgu_norm_g, v_sgu_norm_b, v_hgrn_lb_logits, v_hgrn_norm_g, v_w_branch, v_w_out, v_ln1_g, v_ln1_b, v_ffn_w_up, v_ffn_conv_w, v_ffn_conv_b, v_ffn_w_down, v_ln2_g, v_ln2_b, v_ple_w_proj, v_ple_w_gate):
    weights = dict(w_in=w_in, sgu_w_s=sgu_w_s, sgu_b_s=sgu_b_s, sgu_norm_g=sgu_norm_g, sgu_norm_b=sgu_norm_b,
                   hgrn_lb_logits=hgrn_lb_logits, hgrn_norm_g=hgrn_norm_g, w_branch=w_branch, w_out=w_out,
                   ln1_g=ln1_g, ln1_b=ln1_b, ffn_w_up=ffn_w_up, ffn_conv_w=ffn_conv_w, ffn_conv_b=ffn_conv_b,
                   ffn_w_down=ffn_w_down, ln2_g=ln2_g, ln2_b=ln2_b, ple_w_proj=ple_w_proj, ple_w_gate=ple_w_gate)
    mom_m = dict(w_in=m_w_in, sgu_w_s=m_sgu_w_s, sgu_b_s=m_sgu_b_s, sgu_norm_g=m_sgu_norm_g, sgu_norm_b=m_sgu_norm_b,
                 hgrn_lb_logits=m_hgrn_lb_logits, hgrn_norm_g=m_hgrn_norm_g, w_branch=m_w_branch, w_out=m_w_out,
                 ln1_g=m_ln1_g, ln1_b=m_ln1_b, ffn_w_up=m_ffn_w_up, ffn_conv_w=m_ffn_conv_w, ffn_conv_b=m_ffn_conv_b,
                 ffn_w_down=m_ffn_w_down, ln2_g=m_ln2_g, ln2_b=m_ln2_b, ple_w_proj=m_ple_w_proj, ple_w_gate=m_ple_w_gate)
    mom_v = dict(w_in=v_w_in, sgu_w_s=v_sgu_w_s, sgu_b_s=v_sgu_b_s, sgu_norm_g=v_sgu_norm_g, sgu_norm_b=v_sgu_norm_b,
                 hgrn_lb_logits=v_hgrn_lb_logits, hgrn_norm_g=v_hgrn_norm_g, w_branch=v_w_branch, w_out=v_w_out,
                 ln1_g=v_ln1_g, ln1_b=v_ln1_b, ffn_w_up=v_ffn_w_up, ffn_conv_w=v_ffn_conv_w, ffn_conv_b=v_ffn_conv_b,
                 ffn_w_down=v_ffn_w_down, ln2_g=v_ln2_g, ln2_b=v_ln2_b, ple_w_proj=v_ple_w_proj, ple_w_gate=v_ple_w_gate)
    d, f = D_MODEL, D_FF
    fs = f // 4
    me = _my_index()

    conv_bits = lax.bitcast_convert_type(ffn_conv_w[0], BF16).reshape(-1)
    conv_rows = jnp.pad(conv_bits, (0, 16 * d - conv_bits.shape[0])).reshape(16, d)
    shard_rows = [w_in[0].astype(BF16),
                  w_branch[0].astype(BF16).reshape(2 * 128, d),
                  w_out[0].astype(BF16),
                  ffn_w_up[0].astype(BF16).reshape(fs, d),
                  ffn_w_down[0].astype(BF16),
                  ple_w_proj[0].astype(BF16).reshape(32, d),
                  ple_w_gate[0].astype(BF16),
                  conv_rows]
    sizes = [a.shape[0] for a in shard_rows]
    offs = [sum(sizes[:i]) for i in range(len(sizes))]
    gathered = _exchange(jnp.concatenate(shard_rows, axis=0), "gather_weights", scatter=False)
    seg = lambda i: gathered[:, offs[i]:offs[i] + sizes[i], :]
    w_in_f = seg(0)
    w_br = seg(1).reshape(N_DEV, 2, 128, d).transpose(1, 0, 2, 3).reshape(2, d, d)
    w_o = seg(2).reshape(d, d)
    w_up = seg(3).reshape(N_DEV, d, fs).transpose(1, 0, 2).reshape(d, 2, f).transpose(1, 0, 2)
    w_down = seg(4).reshape(f, d)
    w_pp = seg(5).reshape(N_DEV, 256, 128).transpose(1, 0, 2).reshape(256, d)
    w_pg = seg(6).reshape(d, d)
    conv_all = lax.bitcast_convert_type(seg(7).reshape(N_DEV, -1)[:, :3 * 352 * 2].reshape(N_DEV, 3, 352, 2), F32)
    conv_w = conv_all.transpose(1, 0, 2).reshape(3, f)
    tr = lambda a: jnp.swapaxes(a, -1, -2)
    wts = dict(w_in=w_in_f, w_in_t=tr(w_in_f), w_a=w_br[0], w_b=w_br[1], w_a_t=tr(w_br[0]), w_b_t=tr(w_br[1]),
               w_o=w_o, w_o_t=tr(w_o), w_g=w_up[0], w_v=w_up[1], w_g_t=tr(w_up[0]), w_v_t=tr(w_up[1]),
               w_down=w_down, w_down_t=tr(w_down),
               w_pp=w_pp, w_pg=w_pg, w_pg_t=tr(w_pg))
    small = dict(sgu_w_s=sgu_w_s[0], sgu_b_s=sgu_b_s[0], sgu_norm_g=sgu_norm_g, sgu_norm_b=sgu_norm_b,
                 lb_logits=hgrn_lb_logits, hgrn_norm_g=hgrn_norm_g, ln1_g=ln1_g, ln1_b=ln1_b, ln2_g=ln2_g, ln2_b=ln2_b,
                 conv_w=conv_w, conv_b=ffn_conv_b)

    loss_blk, grad_x, big, sm = _local_step(x[0], p[0, 0], loss_target[0], wts, small)

    g_up = jnp.concatenate([big["w_g"], big["w_v"]], axis=1).reshape(d, N_DEV, fs).transpose(1, 0, 2)
    parts = [big["w_in"],
             jnp.stack([big["w_a"], big["w_b"]]).reshape(2, N_DEV, 128, d).transpose(1, 0, 2, 3).reshape(N_DEV, 256, d),
             big["w_o"].reshape(N_DEV, 128, d),
             g_up.reshape(N_DEV, fs, d),
             big["w_down"].reshape(N_DEV, 352, d),
             big["w_pp"].reshape(256, N_DEV, 128).transpose(1, 0, 2).reshape(N_DEV, 32, d),
             big["w_pg"].reshape(N_DEV, 128, d)]
    psizes = [a.shape[1] for a in parts]
    poffs = [sum(psizes[:i]) for i in range(len(psizes))]
    recv = _exchange(jnp.concatenate(parts, axis=1), "scatter_grads", scatter=True)
    rseg = lambda i, shp: recv[:, poffs[i]:poffs[i] + psizes[i], :].reshape((N_DEV,) + shp)

    out = {}

    def adam(name, parts8, shape2d):
        w2, m2, v2 = (a.reshape(shape2d) for a in (weights[name], mom_m[name], mom_v[name]))
        res = _adam_call(w2, m2, v2, parts8, "adam_" + name)
        out[name] = tuple(r.reshape(weights[name].shape) for r in res)

    adam("w_in", rseg(0, (d, d)), (d, d))
    adam("w_branch", rseg(1, (256, d)), (256, d))
    adam("w_out", rseg(2, (128, d)), (128, d))
    adam("ffn_w_up", rseg(3, (d, fs)), (d, fs))
    adam("ffn_w_down", rseg(4, (352, d)), (352, d))
    adam("ple_w_proj", rseg(5, (256, 128)), (256, 128))
    adam("ple_w_gate", rseg(6, (128, d)), (128, d))

    sm_parts = [sm["sgu_w_s"], sm["sgu_b_s"], sm["sgu_norm_g"], sm["sgu_norm_b"], sm["lb_logits"], sm["hgrn_norm_g"],
                sm["ln1_g"], sm["ln1_b"], sm["conv_b"], sm["ln2_g"], sm["ln2_b"], sm["conv_w"], loss_blk]
    packed, rows = _pack(sm_parts)
    all_parts = _exchange(packed, "gather_small", scatter=False)
    w_list = [weights[n] for n in _SMALL] + [_place_taps(ffn_conv_w[0], me, f), jnp.zeros((8, LANE), F32)]
    m_list = [mom_m[n] for n in _SMALL] + [_place_taps(m_ffn_conv_w[0], me, f), jnp.zeros((8, LANE), F32)]
    v_list = [mom_v[n] for n in _SMALL] + [_place_taps(v_ffn_conv_w[0], me, f), jnp.ones((8, LANE), F32)]
    w_pk, _ = _pack(w_list)
    m_pk, _ = _pack(m_list)
    v_pk, _ = _pack(v_list)
    res = _adam_call(w_pk, m_pk, v_pk, all_parts, "adam_small")
    shapes = [weights[n].shape for n in _SMALL] + [(3, f), (8, LANE)]
    unpacked = [_unpack(r, shapes, rows) for r in res]
    for j, n in enumerate(_SMALL):
        out[n] = tuple(u[j] for u in unpacked)
    out["ffn_conv_w"] = tuple(lax.dynamic_slice_in_dim(u[len(_SMALL)], me * 352, 352, axis=1)[None] for u in unpacked)
    loss = unpacked[0][len(_SMALL) + 1][0, 0]

    order = ["w_in", "sgu_w_s", "sgu_b_s", "sgu_norm_g", "sgu_norm_b", "hgrn_lb_logits", "hgrn_norm_g", "w_branch", "w_out",
             "ln1_g", "ln1_b", "ffn_w_up", "ffn_conv_w", "ffn_conv_b", "ffn_w_down", "ln2_g", "ln2_b", "ple_w_proj", "ple_w_gate"]
    return (loss, grad_x[None], *[out[n][0] for n in order], *[out[n][1] for n in order],
            *[out[n][2] for n in order], *[out[n][3] for n in order])


def _place_taps(shard, me, f):
    return lax.dynamic_update_slice_in_dim(jnp.zeros((3, f), F32), shard, me * 352, axis=1)
```

```python
import functools
import math

import jax
import jax.numpy as jnp
from jax import lax
from jax.experimental import pallas as pl
from jax.experimental.pallas import tpu as pltpu

F32 = jnp.float32
BF16 = jnp.bfloat16

N_DEV = 8
D_MODEL = 1024
CHUNK = 64
SUB = 16
SGU_BLOCK = 128
GROUPS = 8
HEAD = 128
HEADS = 8
D_FF = 2816
LN_EPS = 1e-5
RMS_EPS = 1e-6
ALPHA = 2.0 ** 0.25
GELU_K = math.sqrt(2.0 / math.pi)
GELU_C = 0.044715
NEG = -1e30
ADAM_LR, ADAM_B1, ADAM_B2, ADAM_EPS, ADAM_WD, ADAM_STEP = 0.001, 0.9, 0.999, 1e-08, 0.01, 10
LANE = 128
SLOT_Q, SLOT_F, SLOT_I, SLOT_OG, SLOT_U, SLOT_V, SLOT_GA, SLOT_GB = range(8)


def _slot_of_group(k):
    return jnp.where(k < 2, k + 4, jnp.where(k < 6, k - 2, k))


MIB = 1024 * 1024
VMEM_V7X = 64 * MIB
VMEM_FLOOR = 32 * MIB
MM_TILES = (1024, 1408, 512, 256, 128)


def _params(sem, need=0):
    limit = min(max(need + need // 4, VMEM_FLOOR), VMEM_V7X - 4 * MIB)
    return pltpu.CompilerParams(dimension_semantics=sem, vmem_limit_bytes=limit)


def _pick(n, prefs):
    for t in prefs:
        if n % t == 0:
            return t
    return n


def _gelu(x):
    return 0.5 * x * (1.0 + jnp.tanh(GELU_K * (x + GELU_C * x * x * x)))


def _gelu_grad(x):
    t = jnp.tanh(GELU_K * (x + GELU_C * x * x * x))
    return 0.5 * (1.0 + t) + 0.5 * x * (1.0 - t * t) * GELU_K * (1.0 + 3.0 * GELU_C * x * x)


def _silu_grad(x, s):
    return s * (1.0 + x * (1.0 - s))


def _dot(a, b):
    return jnp.dot(a.astype(BF16), b.astype(BF16), preferred_element_type=F32)


def _dot_nt(a, b):
    return lax.dot_general(a.astype(BF16), b.astype(BF16), (((1,), (1,)), ((), ())), preferred_element_type=F32)


def _dot_tn(a, b):
    return lax.dot_general(a.astype(BF16), b.astype(BF16), (((0,), (0,)), ((), ())), preferred_element_type=F32)


def _mean(x):
    return jnp.mean(x, axis=-1, keepdims=True)


def _sum0(x):
    return jnp.sum(x, axis=0, keepdims=True)


def _mm(a, b, *, out_dtype, name, trans_b=False, reduce_b=False, adds=()):
    squeeze = b.ndim == 2
    a3 = a if a.ndim == 3 else a[None]
    b3 = b if b.ndim == 3 else b[None]
    ba, m, k = a3.shape
    bb = b3.shape[0]
    n = b3.shape[1] if trans_b else b3.shape[2]
    tm = _pick(m, MM_TILES)
    tn = _pick(n, MM_TILES)
    if reduce_b:
        bo, steps = 1, bb
        a_map = lambda o, i, j, r: (r if ba > 1 else 0, i, 0)
        b_map = (lambda o, i, j, r: (r, j, 0)) if trans_b else (lambda o, i, j, r: (r, 0, j))
    else:
        bo, steps = bb, 1
        a_map = lambda o, i, j, r: (o if ba > 1 else 0, i, 0)
        b_map = (lambda o, i, j, r: (o, j, 0)) if trans_b else (lambda o, i, j, r: (o, 0, j))
    o_map = lambda o, i, j, r: (o, i, j)
    add_arrays = [x if x.ndim == 3 else x[None] for x, _ in adds]
    add_scales = [s for _, s in adds]
    n_add = len(adds)
    dot = _dot_nt if trans_b else _dot

    def finish(acc, add_refs, o_ref):
        for ref, s in zip(add_refs, add_scales):
            acc = acc + s * ref[...].astype(F32)
        o_ref[...] = acc.astype(o_ref.dtype)

    def body_single(a_ref, b_ref, *rest):
        finish(dot(a_ref[...], b_ref[...]), rest[:n_add], rest[n_add])

    def body_reduce(a_ref, b_ref, *rest):
        add_refs, o_ref, acc = rest[:n_add], rest[n_add], rest[n_add + 1]
        r = pl.program_id(3)

        @pl.when(r == 0)
        def _():
            acc[...] = jnp.zeros_like(acc)

        acc[...] += dot(a_ref[...], b_ref[...])

        @pl.when(r == steps - 1)
        def _():
            finish(acc[...], add_refs, o_ref)

    b_block = (None, tn, k) if trans_b else (None, k, tn)
    out_bytes = tm * tn * jnp.dtype(out_dtype).itemsize
    need = 2 * (tm * k * a3.dtype.itemsize + k * tn * b3.dtype.itemsize + out_bytes + n_add * tm * tn * 4)
    need += 2 * tm * tn * 4
    out = pl.pallas_call(
        body_reduce if reduce_b else body_single,
        grid=(bo, m // tm, n // tn, steps),
        in_specs=[pl.BlockSpec((None, tm, k), a_map), pl.BlockSpec(b_block, b_map)]
        + [pl.BlockSpec((None, tm, tn), o_map) for _ in adds],
        out_specs=pl.BlockSpec((None, tm, tn), o_map),
        out_shape=jax.ShapeDtypeStruct((bo, m, n), out_dtype),
        scratch_shapes=[pltpu.VMEM((tm, tn), F32)] if reduce_b else [],
        compiler_params=_params(("parallel", "parallel", "parallel", "arbitrary"), need),
        name=name,
    )(a3, b3, *add_arrays)
    return out[0] if (reduce_b or squeeze) else out


def _mm_tn(a, b, *, out_dtype, name):
    squeeze = b.ndim == 2
    b3 = b if b.ndim == 3 else b[None]
    t, m = a.shape
    bb, _, n = b3.shape
    tm = _pick(m, MM_TILES)
    tn = _pick(n, MM_TILES)
    tt = _pick(t, (1024, 512, 256, 128))
    steps = t // tt
    need = 2 * (tt * tm * a.dtype.itemsize + tt * tn * b3.dtype.itemsize + tm * tn * jnp.dtype(out_dtype).itemsize)
    need += 2 * tm * tn * 4

    def body(a_ref, b_ref, o_ref, acc):
        r = pl.program_id(3)

        @pl.when(r == 0)
        def _():
            acc[...] = jnp.zeros_like(acc)

        acc[...] += _dot_tn(a_ref[...], b_ref[...])

        @pl.when(r == steps - 1)
        def _():
            o_ref[...] = acc[...].astype(o_ref.dtype)

    out = pl.pallas_call(
        body,
        grid=(bb, m // tm, n // tn, steps),
        in_specs=[pl.BlockSpec((tt, tm), lambda o, i, j, r: (r, i)),
                  pl.BlockSpec((None, tt, tn), lambda o, i, j, r: (o, r, j))],
        out_specs=pl.BlockSpec((None, tm, tn), lambda o, i, j, r: (o, i, j)),
        out_shape=jax.ShapeDtypeStruct((bb, m, n), out_dtype),
        scratch_shapes=[pltpu.VMEM((tm, tn), F32)],
        compiler_params=_params(("parallel", "parallel", "parallel", "arbitrary"), need),
        name=name,
    )(a, b3)
    return out[0] if squeeze else out


def _rowwise(fn, rows, consts, row_outs, acc_outs, *, tile, name):
    first = rows[0][0] if isinstance(rows[0], tuple) else rows[0]
    t = first.shape[-2]
    steps = t // tile
    arrays, in_specs = [], []
    for r in rows:
        if isinstance(r, tuple) and isinstance(r[1], pl.BlockSpec):
            arrays.append(r[0])
            in_specs.append(r[1])
        elif isinstance(r, tuple):
            arr, bidx = r
            arrays.append(arr)
            in_specs.append(pl.BlockSpec((None, tile, arr.shape[-1]), functools.partial(lambda i, b: (b, i, 0), b=bidx)))
        else:
            arrays.append(r)
            in_specs.append(pl.BlockSpec((tile, r.shape[-1]), lambda i: (i, 0)))
    for c in consts:
        arrays.append(c)
        in_specs.append(pl.BlockSpec(c.shape, lambda i: (0, 0)))
    n_in, n_row = len(arrays), len(row_outs)
    out_shape, out_specs = [], []
    for ro in row_outs:
        if ro[0] == "stack":
            _, cnt, blk, total, w, dt = ro
            out_shape.append(jax.ShapeDtypeStruct((total, t, w), dt))
            out_specs.append(pl.BlockSpec((cnt, tile, w), functools.partial(lambda i, b: (b, i, 0), b=blk)))
        else:
            w, dt = ro
            out_shape.append(jax.ShapeDtypeStruct((t, w), dt))
            out_specs.append(pl.BlockSpec((tile, w), lambda i: (i, 0)))
    out_shape += [jax.ShapeDtypeStruct(s, F32) for s in acc_outs]
    out_specs += [pl.BlockSpec(s, lambda i: (0, 0)) for s in acc_outs]
    blocks = [math.prod(d for d in sp.block_shape if d) * arr.dtype.itemsize for sp, arr in zip(in_specs, arrays)]
    blocks += [math.prod(d for d in sp.block_shape if d) * sh.dtype.itemsize for sp, sh in zip(out_specs, out_shape)]
    need = 2 * sum(blocks) + 6 * tile * max(a.shape[-1] for a in arrays) * 4

    def body(*refs):
        ins, outs = refs[:n_in], refs[n_in:]
        i = pl.program_id(0)
        res = fn(i, steps, *[r[...] for r in ins])
        res = res if isinstance(res, (tuple, list)) else (res,)
        for ref, val in zip(outs[:n_row], res[:n_row]):
            ref[...] = val.astype(ref.dtype)
        if acc_outs:
            @pl.when(i == 0)
            def _():
                for ref in outs[n_row:]:
                    ref[...] = jnp.zeros_like(ref)

            for ref, val in zip(outs[n_row:], res[n_row:]):
                ref[...] += val

    return pl.pallas_call(
        body,
        grid=(steps,),
        in_specs=in_specs,
        out_specs=out_specs,
        out_shape=out_shape,
        compiler_params=_params(("arbitrary",), need),
        name=name,
    )(*arrays)


def _ln_stats(z):
    mu = _mean(z)
    zc = z - mu
    rstd = lax.rsqrt(_mean(zc * zc) + LN_EPS)
    return zc * rstd, rstd


def _ln_bwd(dy, xhat, rstd, g):
    dxh = dy * g
    return rstd * (dxh - _mean(dxh) - xhat * _mean(dxh * xhat))


def _sgu_fwd(h, wm, bs_t, g_v, b_v):
    t = h.shape[1]

    def body(u_ref, v_ref, wm_ref, bs_ref, g_ref, b_ref, y_ref):
        xhat, _ = _ln_stats(_gelu(v_ref[...]))
        vn = (xhat * g_ref[...] + b_ref[...]).astype(BF16)
        gu = _gelu(u_ref[...])
        for g in range(GROUPS):
            sl = slice(g * HEAD, (g + 1) * HEAD)
            mixed = _dot(wm_ref[g], vn[:, sl]) + bs_ref[:, g:g + 1]
            y_ref[:, sl] = (gu[:, sl] * mixed).astype(BF16)

    blk = lambda b: pl.BlockSpec((None, SGU_BLOCK, D_MODEL), functools.partial(lambda i, b: (b, i, 0), b=b))
    whole = lambda s: pl.BlockSpec(s, lambda i: (0,) * len(s))
    return pl.pallas_call(
        body,
        grid=(t // SGU_BLOCK,),
        in_specs=[blk(SLOT_U), blk(SLOT_V), whole(wm.shape), whole(bs_t.shape), whole(g_v.shape), whole(b_v.shape)],
        out_specs=pl.BlockSpec((SGU_BLOCK, D_MODEL), lambda i: (i, 0)),
        out_shape=jax.ShapeDtypeStruct((t, D_MODEL), BF16),
        compiler_params=_params(("parallel",)),
        name="sgu_fwd",
    )(h, h, wm, bs_t, g_v, b_v)


def _sgu_bwd(h, dy, dh, wm, wm_t, bs_t, g_v, b_v):
    t = h.shape[1]

    def body(u_ref, v_ref, dy_ref, dh_in, wm_ref, wmt_ref, bs_ref, g_ref, b_ref,
             duv_ref, dw_ref, dbs_ref, dg_ref, db_ref, dvn_ref):
        del dh_in
        du_ref, dv_ref = duv_ref.at[0], duv_ref.at[1]
        i = pl.program_id(0)

        @pl.when(i == 0)
        def _():
            dw_ref[...] = jnp.zeros_like(dw_ref)
            dbs_ref[...] = jnp.zeros_like(dbs_ref)
            dg_ref[...] = jnp.zeros_like(dg_ref)
            db_ref[...] = jnp.zeros_like(db_ref)

        u = u_ref[...]
        v = v_ref[...]
        xhat, rstd = _ln_stats(_gelu(v))
        vn = (xhat * g_ref[...] + b_ref[...]).astype(BF16)
        gu = _gelu(u)
        gup = _gelu_grad(u)
        lane = lax.broadcasted_iota(jnp.int32, (SGU_BLOCK, LANE), 1)
        dbs = jnp.zeros((SGU_BLOCK, LANE), F32)
        for g in range(GROUPS):
            sl = slice(g * HEAD, (g + 1) * HEAD)
            vn_g = vn[:, sl]
            mixed = _dot(wm_ref[g], vn_g) + bs_ref[:, g:g + 1]
            dy_g = dy_ref[:, sl]
            du_ref[:, sl] = (dy_g * mixed * gup[:, sl]).astype(BF16)
            dmix = dy_g * gu[:, sl]
            dmb = dmix.astype(BF16)
            dvn_ref[:, sl] = _dot(wmt_ref[g], dmb)
            dw_ref[g] += _dot_nt(dmb, vn_g)
            dbs = dbs + jnp.where(lane == g, jnp.sum(dmix, axis=1, keepdims=True), 0.0)
        dbs_ref[...] += dbs
        dvn = dvn_ref[...]
        dg_ref[...] += _sum0(dvn * xhat)
        db_ref[...] += _sum0(dvn)
        dv_ref[...] = (_ln_bwd(dvn, xhat, rstd, g_ref[...]) * _gelu_grad(v)).astype(BF16)

    blk = lambda b: pl.BlockSpec((None, SGU_BLOCK, D_MODEL), functools.partial(lambda i, b: (b, i, 0), b=b))
    row = pl.BlockSpec((SGU_BLOCK, D_MODEL), lambda i: (i, 0))
    whole = lambda s: pl.BlockSpec(s, lambda i: (0,) * len(s))
    vec = (1, D_MODEL)
    return pl.pallas_call(
        body,
        grid=(t // SGU_BLOCK,),
        in_specs=[blk(SLOT_U), blk(SLOT_V), row, pl.BlockSpec(memory_space=pl.ANY),
                  whole(wm.shape), whole(wm_t.shape), whole(bs_t.shape), whole(vec), whole(vec)],
        out_specs=[pl.BlockSpec((2, SGU_BLOCK, D_MODEL), lambda i: (SLOT_U // 2, i, 0)),
                   whole(wm.shape), whole((SGU_BLOCK, LANE)), whole(vec), whole(vec)],
        out_shape=[jax.ShapeDtypeStruct(dh.shape, BF16),
                   jax.ShapeDtypeStruct(wm.shape, F32), jax.ShapeDtypeStruct((SGU_BLOCK, LANE), F32),
                   jax.ShapeDtypeStruct(vec, F32), jax.ShapeDtypeStruct(vec, F32)],
        scratch_shapes=[pltpu.VMEM((SGU_BLOCK, D_MODEL), F32)],
        input_output_aliases={3: 0},
        compiler_params=_params(("arbitrary",)),
        name="sgu_bwd",
    )(h, h, dy, dh, wm, wm_t, bs_t, g_v, b_v)


def _split3(x):
    hi = x.astype(BF16)
    r1 = x - hi.astype(F32)
    mid = r1.astype(BF16)
    lo = (r1 - mid.astype(F32)).astype(BF16)
    return hi, mid, lo


def _tri_matmul(tri, x):
    hi, mid, lo = _split3(x)
    dot = lambda p: jnp.dot(tri, p, preferred_element_type=F32)
    return dot(hi) + dot(mid) + dot(lo)


def _lower_bound(logits):
    l0, l1 = logits[0:1, :], logits[1:2, :]
    mx = jnp.maximum(l0, l1)
    e0, e1 = jnp.exp(l0 - mx), jnp.exp(l1 - mx)
    return e0 / (e0 + e1)


def _hgrn_gates(q_raw, f_raw, lb):
    q = q_raw * jax.nn.sigmoid(q_raw)
    sig = jax.nn.sigmoid(f_raw)
    f = lb + (1.0 - lb) * sig
    row = lax.broadcasted_iota(jnp.int32, (CHUNK, CHUNK), 0)
    col = lax.broadcasted_iota(jnp.int32, (CHUNK, CHUNK), 1)
    c = _tri_matmul((row >= col).astype(BF16), jnp.log(f))
    return q, sig, f, 1.0 - f, c


def _offdiag_terms(qh, kh, ch, tb):
    rows = slice(tb * SUB, (tb + 1) * SUB)
    r = ch[tb * SUB - 1:tb * SUB, :]
    eqh = jnp.exp(ch[rows] - r)
    ekh = jnp.exp(jnp.minimum(r - ch, 0.0))
    return rows, eqh, qh[rows] * eqh, ekh, kh * ekh


def _hgrn_fwd(h, logits, g_norm):
    t = h.shape[1]
    nc = t // CHUNK

    def body(q_ref, f_ref, i_ref, og_ref, lg_ref, gn_ref, y_ref, o_ref, sall_ref, st_ref, q_s, k_s, c_s):
        @pl.when(pl.program_id(0) == 0)
        def _():
            st_ref[...] = jnp.zeros_like(st_ref)

        lb = _lower_bound(lg_ref[...])
        q, _, _, k, c = _hgrn_gates(q_ref[...], f_ref[...], lb)
        q_s[...] = q
        k_s[...] = k
        c_s[...] = c
        col64 = lax.broadcasted_iota(jnp.int32, (SUB, CHUNK), 1)
        trow = lax.broadcasted_iota(jnp.int32, (SUB, HEAD), 0)

        def head(hd, carry):
            sl = pl.ds(pl.multiple_of(hd * HEAD, HEAD), HEAD)
            qh, kh, ch, ih = q_s[:, sl], k_s[:, sl], c_s[:, sl], i_ref[:, sl]
            st = st_ref[hd]
            sall_ref[hd] = st
            c_last = ch[CHUNK - 1:CHUNK, :]
            o = _dot_nt(qh * jnp.exp(ch), st)
            st_ref[hd] = st * jnp.exp(c_last) + _dot_tn(ih, kh * jnp.exp(c_last - ch))
            a_rows = [jnp.zeros((SUB, CHUNK), F32)]
            for tb in range(1, CHUNK // SUB):
                _, _, q_hat, _, k_hat = _offdiag_terms(qh, kh, ch, tb)
                a_rows.append(jnp.where(col64 < tb * SUB, _dot_nt(q_hat, k_hat), 0.0))
            o = o + _dot(jnp.concatenate(a_rows, axis=0), ih)
            o_rows = []
            for b in range(CHUNK // SUB):
                rows = slice(b * SUB, (b + 1) * SUB)
                qb, cb, kb, ib = qh[rows], ch[rows], kh[rows], ih[rows]
                ob = jnp.zeros((SUB, HEAD), F32)
                for s in range(SUB):
                    dec = jnp.exp(jnp.where(trow >= s, cb - cb[s:s + 1, :], NEG))
                    a = jnp.sum(qb * dec * kb[s:s + 1, :], axis=1, keepdims=True)
                    ob = ob + a * ib[s:s + 1, :]
                o_rows.append(ob)
            o = o + jnp.concatenate(o_rows, axis=0)
            o_ref[:, sl] = o
            og = og_ref[:, sl]
            on = o * lax.rsqrt(_mean(o * o) + RMS_EPS)
            y_ref[:, sl] = (on * gn_ref[:, sl] * (og * jax.nn.sigmoid(og))).astype(BF16)
            return carry

        lax.fori_loop(0, HEADS, head, 0)

    blk = lambda b: pl.BlockSpec((None, CHUNK, D_MODEL), functools.partial(lambda n, b: (b, n, 0), b=b))
    row = pl.BlockSpec((CHUNK, D_MODEL), lambda n: (n, 0))
    whole = lambda s: pl.BlockSpec(s, lambda n: (0,) * len(s))
    return pl.pallas_call(
        body,
        grid=(nc,),
        in_specs=[blk(SLOT_Q), blk(SLOT_F), blk(SLOT_I), blk(SLOT_OG), whole(logits.shape), whole(g_norm.shape)],
        out_specs=[row, row, pl.BlockSpec((None, HEADS, HEAD, HEAD), lambda n: (n, 0, 0, 0))],
        out_shape=[jax.ShapeDtypeStruct((t, D_MODEL), BF16), jax.ShapeDtypeStruct((t, D_MODEL), F32),
                   jax.ShapeDtypeStruct((nc, HEADS, HEAD, HEAD), F32)],
        scratch_shapes=[pltpu.VMEM((HEADS, HEAD, HEAD), F32)] + [pltpu.VMEM((CHUNK, D_MODEL), F32)] * 3,
        compiler_params=_params(("arbitrary",)),
        name="hgrn_fwd",
    )(h, h, h, h, logits, g_norm)


def _hgrn_bwd(h, o_all, dy, states, dh, logits, g_norm):
    t = h.shape[1]
    nc = t // CHUNK

    def body(q_ref, f_ref, i_ref, og_ref, o_ref, dy_ref, sall_ref, dh_in, lg_ref, gn_ref,
             dqfio_ref, dlg_ref, dgn_ref,
             dst_ref, dlb_ref, q_s, k_s, c_s, dq_s, dk_s, dc_s):
        del dh_in
        dq_ref, df_ref, di_ref, dog_ref = (dqfio_ref.at[s] for s in (SLOT_Q, SLOT_F, SLOT_I, SLOT_OG))
        n = pl.program_id(0)

        @pl.when(n == 0)
        def _():
            dst_ref[...] = jnp.zeros_like(dst_ref)
            dlb_ref[...] = jnp.zeros_like(dlb_ref)
            dgn_ref[...] = jnp.zeros_like(dgn_ref)

        lb = _lower_bound(lg_ref[...])
        q_raw = q_ref[...]
        q, sig, f, k, c = _hgrn_gates(q_raw, f_ref[...], lb)
        q_s[...] = q
        k_s[...] = k
        c_s[...] = c
        col64 = lax.broadcasted_iota(jnp.int32, (SUB, CHUNK), 1)
        trow = lax.broadcasted_iota(jnp.int32, (SUB, HEAD), 0)
        row64 = lax.broadcasted_iota(jnp.int32, (CHUNK, HEAD), 0)

        def head(hd, carry):
            sl = pl.ds(pl.multiple_of(hd * HEAD, HEAD), HEAD)
            qh, kh, ch, ih = q_s[:, sl], k_s[:, sl], c_s[:, sl], i_ref[:, sl]
            st = sall_ref[hd]
            dst = dst_ref[hd]
            oh, dyh, og, gn = o_ref[:, sl], dy_ref[:, sl], og_ref[:, sl], gn_ref[:, sl]
            sg = jax.nn.sigmoid(og)
            sil = og * sg
            rms = lax.rsqrt(_mean(oh * oh) + RMS_EPS)
            on = oh * rms
            dog_ref[:, sl] = (dyh * on * gn * _silu_grad(og, sg)).astype(BF16)
            dgn_ref[:, sl] += _sum0(dyh * on * sil)
            don = dyh * gn * sil
            do = rms * (don - on * _mean(don * on))
            dob = do.astype(BF16)

            c_last = ch[CHUNK - 1:CHUNK, :]
            eq = jnp.exp(ch)
            q_til = qh * eq
            ekl = jnp.exp(c_last - ch)
            k_til = kh * ekl
            ecl = jnp.exp(c_last)
            dq_til = _dot(dob, st)
            dk_til = _dot(ih, dst)
            di = _dot_nt(k_til, dst)
            dc_last = _sum0(dk_til * k_til) + _sum0(dst * st) * ecl
            dst_ref[hd] = _dot_tn(dob, q_til) + dst * ecl
            dq = dq_til * eq
            dc = dq_til * q_til - dk_til * k_til
            dk = dk_til * ekl

            da_full = _dot_nt(dob, ih)
            a_rows = [jnp.zeros((SUB, CHUNK), F32)]
            dq_rows = [jnp.zeros((SUB, HEAD), F32)]
            dc_rows = [jnp.zeros((SUB, HEAD), F32)]
            for tb in range(1, CHUNK // SUB):
                rows, eqh, q_hat, ekh, k_hat = _offdiag_terms(qh, kh, ch, tb)
                keep = col64 < tb * SUB
                a_rows.append(jnp.where(keep, _dot_nt(q_hat, k_hat), 0.0))
                da = jnp.where(keep, da_full[rows], 0.0)
                dq_hat = _dot(da, k_hat)
                dk_hat = _dot_tn(da, q_hat)
                dq_rows.append(dq_hat * eqh)
                dc_rows.append(dq_hat * q_hat)
                dk = dk + dk_hat * ekh
                dc = dc - dk_hat * k_hat
            di = di + _dot_tn(jnp.concatenate(a_rows, axis=0), dob)

            dk_rows, di_rows = [], []
            for b in range(CHUNK // SUB):
                rows = slice(b * SUB, (b + 1) * SUB)
                qb, cb, kb, ib, dob_ = qh[rows], ch[rows], kh[rows], ih[rows], do[rows]
                dqb, dcb = dq_rows[b], dc_rows[b]
                dkb = jnp.zeros((SUB, HEAD), F32)
                dib = jnp.zeros((SUB, HEAD), F32)
                for s in range(SUB):
                    ks = kb[s:s + 1, :]
                    dec = jnp.exp(jnp.where(trow >= s, cb - cb[s:s + 1, :], NEG))
                    a = jnp.sum(qb * dec * ks, axis=1, keepdims=True)
                    da = jnp.sum(dob_ * ib[s:s + 1, :], axis=1, keepdims=True)
                    gk = da * dec
                    dqb = dqb + gk * ks
                    dks = _sum0(gk * qb)
                    dcb = dcb + gk * ks * qb - jnp.where(trow == s, dks * ks, 0.0)
                    dkb = dkb + jnp.where(trow == s, dks, 0.0)
                    dib = dib + jnp.where(trow == s, _sum0(a * dob_), 0.0)
                dq_rows[b], dc_rows[b] = dqb, dcb
                dk_rows.append(dkb)
                di_rows.append(dib)
            dq = dq + jnp.concatenate(dq_rows, axis=0)
            dk = dk + jnp.concatenate(dk_rows, axis=0)
            dc = dc + jnp.concatenate(dc_rows, axis=0) + jnp.where(row64 == CHUNK - 1, dc_last, 0.0)
            di_ref[:, sl] = (di + jnp.concatenate(di_rows, axis=0)).astype(BF16)
            dq_s[:, sl] = dq
            dk_s[:, sl] = dk
            dc_s[:, sl] = dc
            return carry

        lax.fori_loop(0, HEADS, head, 0)

        row = lax.broadcasted_iota(jnp.int32, (CHUNK, CHUNK), 0)
        col = lax.broadcasted_iota(jnp.int32, (CHUNK, CHUNK), 1)
        dlf = _tri_matmul((row <= col).astype(BF16), dc_s[...])
        df = dlf / f - dk_s[...]
        dlb_ref[...] += _sum0(df * (1.0 - sig))
        df_ref[...] = (df * (1.0 - lb) * sig * (1.0 - sig)).astype(BF16)
        dq_ref[...] = (dq_s[...] * _silu_grad(q_raw, jax.nn.sigmoid(q_raw))).astype(BF16)

        @pl.when(n == nc - 1)
        def _():
            d0 = dlb_ref[...] * lb * (1.0 - lb)
            dlg_ref[0:1, :] = d0
            dlg_ref[1:2, :] = -d0

    rev = lambda n: nc - 1 - n
    blk = lambda b: pl.BlockSpec((None, CHUNK, D_MODEL), functools.partial(lambda n, b: (b, rev(n), 0), b=b))
    row = pl.BlockSpec((CHUNK, D_MODEL), lambda n: (rev(n), 0))
    whole = lambda s: pl.BlockSpec(s, lambda n: (0,) * len(s))
    vec = (1, D_MODEL)
    act = jax.ShapeDtypeStruct((t, D_MODEL), BF16)
    return pl.pallas_call(
        body,
        grid=(nc,),
        in_specs=[blk(SLOT_Q), blk(SLOT_F), blk(SLOT_I), blk(SLOT_OG), row, row,
                  pl.BlockSpec((None, HEADS, HEAD, HEAD), lambda n: (rev(n), 0, 0, 0)),
                  pl.BlockSpec(memory_space=pl.ANY), whole(logits.shape), whole(vec)],
        out_specs=[pl.BlockSpec((4, CHUNK, D_MODEL), lambda n: (SLOT_Q // 4, rev(n), 0)), whole((2, D_MODEL)), whole(vec)],
        out_shape=[jax.ShapeDtypeStruct(dh.shape, BF16), jax.ShapeDtypeStruct((2, D_MODEL), F32),
                   jax.ShapeDtypeStruct(vec, F32)],
        scratch_shapes=[pltpu.VMEM((HEADS, HEAD, HEAD), F32), pltpu.VMEM(vec, F32)]
        + [pltpu.VMEM((CHUNK, D_MODEL), F32)] * 6,
        input_output_aliases={7: 0},
        compiler_params=_params(("arbitrary",)),
        name="hgrn_bwd",
    )(h, h, h, h, o_all, dy, states, dh, logits, g_norm)


def _merge_fwd(i, n, ga, gb, za, zb):
    return jax.nn.sigmoid(ga) * za + jax.nn.sigmoid(gb) * zb


def _merge_bwd(i, n, ga, gb, za, zb, dm):
    sa, sb = jax.nn.sigmoid(ga), jax.nn.sigmoid(gb)
    dgates = jnp.stack([(dm * za * sa * (1.0 - sa)).astype(BF16), (dm * zb * sb * (1.0 - sb)).astype(BF16)])
    return dgates, dm * sa, dm * sb


def _ln1_fwd(i, n, x, r1, g, b):
    xhat, _ = _ln_stats(ALPHA * x + r1)
    x1 = xhat * g + b
    return x1, x1


def _ln1_bwd(i, n, x, r1, dx1, g):
    xhat, rstd = _ln_stats(ALPHA * x + r1)
    dz = _ln_bwd(dx1, xhat, rstd, g)
    return dz, dz, _sum0(dx1 * xhat), _sum0(dx1)


def _ln2_loss(i, n, x1, fo, pg, pp, tgt, g, b):
    sg = jax.nn.sigmoid(pg)
    xhat, rstd = _ln_stats(ALPHA * x1 + fo + sg * pp)
    diff = xhat * g + b - tgt
    loss = 0.5 * jnp.sum(_mean(diff * diff), axis=0, keepdims=True)
    dy = diff * (1.0 / D_MODEL)
    dz = _ln_bwd(dy, xhat, rstd, g)
    return (dz, dz, dz * pp * sg * (1.0 - sg), dz * sg,
            jnp.broadcast_to(loss, (8, LANE)), _sum0(dy * xhat), _sum0(dy))


def _shift_down(cur, halo, tile):
    row = lax.broadcasted_iota(jnp.int32, cur.shape, 0)
    m1 = jnp.where(row == 0, halo[7:8, :], pltpu.roll(cur, 1, 0))
    m2 = jnp.where(row == 0, halo[6:7, :], jnp.where(row == 1, halo[7:8, :], pltpu.roll(cur, 2, 0)))
    return m1, m2


def _shift_up(cur, halo, tile):
    row = lax.broadcasted_iota(jnp.int32, cur.shape, 0)
    p1 = jnp.where(row == tile - 1, halo[0:1, :], pltpu.roll(cur, tile - 1, 0))
    p2 = jnp.where(row == tile - 2, halo[0:1, :], jnp.where(row == tile - 1, halo[1:2, :], pltpu.roll(cur, tile - 2, 0)))
    return p1, p2


def _conv_pre(i, gate, halo, w, b, tile):
    halo = jnp.where(i == 0, 0.0, halo)
    m1, m2 = _shift_down(gate, halo, tile)
    return w[0:1, :] * m2 + w[1:2, :] * m1 + w[2:3, :] * gate + b, m1, m2


def _conv_fwd(tile, i, n, gate, halo, val, w, b):
    cg, _, _ = _conv_pre(i, gate, halo, w, b, tile)
    return _gelu(cg) * val


def _conv_bwd_a(tile, i, n, gate, halo, val, dhid, w, b):
    cg, m1, m2 = _conv_pre(i, gate, halo, w, b, tile)
    dcg = dhid * val * _gelu_grad(cg)
    return dcg, dhid * _gelu(cg), _sum0(dcg * m2), _sum0(dcg * m1), _sum0(dcg * gate), _sum0(dcg)


def _conv_bwd_b(tile, i, n, dcg, halo, w):
    halo = jnp.where(i == n - 1, 0.0, halo)
    p1, p2 = _shift_up(dcg, halo, tile)
    return w[2:3, :] * dcg + w[1:2, :] * p1 + w[0:1, :] * p2


def _halo_spec(width, tile, t, nxt):
    per = tile // 8
    last = t // 8 - 1
    if nxt:
        return pl.BlockSpec((8, width), lambda i: (jnp.minimum((i + 1) * per, last), 0))
    return pl.BlockSpec((8, width), lambda i: (jnp.maximum(i * per - 1, 0), 0))


def _adamw(i, n, w, m, v, parts):
    g = parts[0].astype(F32)
    for j in range(1, N_DEV):
        g = g + parts[j].astype(F32)
    m_new = ADAM_B1 * m + (1.0 - ADAM_B1) * g
    v_new = ADAM_B2 * v + (1.0 - ADAM_B2) * (g * g)
    m_hat = m_new / (1.0 - ADAM_B1 ** ADAM_STEP)
    v_hat = v_new / (1.0 - ADAM_B2 ** ADAM_STEP)
    delta = -ADAM_LR * (m_hat / (jnp.sqrt(v_hat) + ADAM_EPS) + ADAM_WD * w)
    return g, delta, m_new, v_new


def _adam_call(w, m, v, parts, name):
    r, c = w.shape
    tile = _pick(r, (256, 128)) if r > 256 else r
    spec = pl.BlockSpec((N_DEV, tile, c), lambda i: (0, i, 0))
    return _rowwise(_adamw, [w, m, v, (parts, spec)], [], [(c, F32)] * 4, [], tile=tile, name=name)


def _peer(k):
    x, y, c = lax.axis_index("x"), lax.axis_index("y"), lax.axis_index("c")
    px = x ^ ((k >> 2) & 1)
    py = y ^ ((k >> 1) & 1)
    pc = c ^ (k & 1)
    return (px, py, pc), 4 * px + 2 * py + pc


def _my_index():
    return 4 * lax.axis_index("x") + 2 * lax.axis_index("y") + lax.axis_index("c")


def _exchange(arrays, name, scatter, slotted=()):
    n = len(arrays)
    per = N_DEV - 1

    def body(*refs):
        srcs, outs = refs[:n], refs[n:2 * n]
        send_sems, recv_sems, local_sems = refs[2 * n:]
        me = _my_index()
        place = lambda a, idx: _slot_of_group(idx) if a in slotted else idx
        started = []
        for a in range(n):
            if scatter:
                mine, land = srcs[a].at[place(a, me)], outs[a].at[me]
            else:
                mine, land = srcs[a], outs[a].at[place(a, me)]
            local = pltpu.make_async_copy(mine, land, local_sems.at[a])
            local.start()
            started.append(local)
        copies = []
        for a in range(n):
            for k in range(1, N_DEV):
                dev, idx = _peer(k)
                copies.append(pltpu.make_async_remote_copy(
                    src_ref=srcs[a].at[place(a, idx)] if scatter else srcs[a],
                    dst_ref=outs[a].at[me] if scatter else outs[a].at[place(a, me)],
                    send_sem=send_sems.at[a * per + k - 1], recv_sem=recv_sems.at[a * per + k - 1],
                    device_id=dev, device_id_type=pl.DeviceIdType.MESH))
        for cp in copies:
            cp.start()
        for a in range(n):
            for k in range(1, N_DEV):
                dev, idx = _peer(k)
                pltpu.make_async_remote_copy(
                    src_ref=srcs[a].at[0] if scatter else srcs[a],
                    dst_ref=outs[a].at[idx] if scatter else outs[a].at[place(a, idx)],
                    send_sem=send_sems.at[a * per + k - 1], recv_sem=recv_sems.at[a * per + k - 1],
                    device_id=dev, device_id_type=pl.DeviceIdType.MESH).wait_recv()
        for cp in copies:
            cp.wait_send()
        for local in started:
            local.wait()

    shapes = [tuple(a.shape[1:]) if scatter else tuple(a.shape) for a in arrays]
    return pl.pallas_call(
        body,
        in_specs=[pl.BlockSpec(memory_space=pl.ANY)] * n,
        out_specs=[pl.BlockSpec(memory_space=pl.ANY)] * n,
        out_shape=[jax.ShapeDtypeStruct((N_DEV,) + s, a.dtype) for s, a in zip(shapes, arrays)],
        scratch_shapes=[pltpu.SemaphoreType.DMA((n * per,)), pltpu.SemaphoreType.DMA((n * per,)),
                        pltpu.SemaphoreType.DMA((n,))],
        name=name,
    )(*arrays)


def _local_step(x, p, tgt, wts, small):
    t = x.shape[0]
    tile = _pick(t, (256, 128))
    d = D_MODEL
    act_b, act_f = (d, BF16), (d, F32)
    x_b, p_b = x.astype(BF16), p.astype(BF16)

    chunk_id = jnp.arange(SGU_BLOCK) // CHUNK
    mask = chunk_id[:, None] >= chunk_id[None, :]
    wm = jnp.where(mask[None], small["sgu_w_s"], 0.0)
    wm_b = wm.astype(BF16)
    wm_t = jnp.swapaxes(wm, 1, 2).astype(BF16)
    bs_t = small["sgu_b_s"].T

    h = _mm(x_b, wts["w_in"], out_dtype=F32, name="mm_h")
    y_a = _sgu_fwd(h, wm_b, bs_t, small["sgu_norm_g"], small["sgu_norm_b"])
    y_b, o_all, states = _hgrn_fwd(h, small["lb_logits"], small["hgrn_norm_g"])
    z_a = _mm(y_a, wts["w_a"], out_dtype=F32, name="mm_za")
    z_b = _mm(y_b, wts["w_b"], out_dtype=F32, name="mm_zb")
    gates = [(h, SLOT_GA), (h, SLOT_GB)]
    merged, = _rowwise(_merge_fwd, gates + [z_a, z_b], [], [act_b], [], tile=tile, name="merge_fwd")
    r1 = _mm(merged, wts["w_o"], out_dtype=F32, name="mm_r1")
    x1, x1_b = _rowwise(_ln1_fwd, [x, r1], [small["ln1_g"], small["ln1_b"]], [act_f, act_b], [], tile=tile, name="ln1_fwd")
    gate = _mm(x1_b, wts["w_g"], out_dtype=F32, name="mm_gate")
    val = _mm(x1_b, wts["w_v"], out_dtype=F32, name="mm_val")
    pg = _mm(x1_b, wts["w_pg"], out_dtype=F32, name="mm_pg")
    pp = _mm(p_b, wts["w_pp"], out_dtype=F32, name="mm_pp")
    hid, = _rowwise(functools.partial(_conv_fwd, tile), [gate, (gate, _halo_spec(D_FF, tile, t, False)), val],
                    [small["conv_w"], small["conv_b"]], [(D_FF, BF16)], [], tile=tile, name="conv_fwd")
    fo = _mm(hid, wts["w_down"], out_dtype=F32, name="mm_down")
    dz2, dz2_b, dpg, dpp, loss, dg2, db2 = _rowwise(
        _ln2_loss, [x1, fo, pg, pp, tgt], [small["ln2_g"], small["ln2_b"]],
        [act_f, act_b, act_b, act_b], [(8, LANE), (1, d), (1, d)], tile=tile, name="ln2_loss")

    dhid = _mm(dz2_b, wts["w_down"], out_dtype=F32, name="mm_dhid", trans_b=True)
    g_down = _mm_tn(hid, dz2_b, out_dtype=BF16, name="mm_g_down")
    dcg, dval, dcw0, dcw1, dcw2, dcb = _rowwise(
        functools.partial(_conv_bwd_a, tile), [gate, (gate, _halo_spec(D_FF, tile, t, False)), val, dhid],
        [small["conv_w"], small["conv_b"]], [(D_FF, F32), (D_FF, BF16)], [(1, D_FF)] * 4, tile=tile, name="conv_bwd_a")
    dgate, = _rowwise(functools.partial(_conv_bwd_b, tile), [dcg, (dcg, _halo_spec(D_FF, tile, t, True))],
                      [small["conv_w"]], [(D_FF, BF16)], [], tile=tile, name="conv_bwd_b")
    g_g = _mm_tn(x1_b, dgate, out_dtype=BF16, name="mm_g_gate")
    g_v = _mm_tn(x1_b, dval, out_dtype=BF16, name="mm_g_val")
    g_pg = _mm_tn(x1_b, dpg, out_dtype=BF16, name="mm_g_pg")
    g_pp = _mm_tn(p_b, dpp, out_dtype=BF16, name="mm_g_pp")
    dx1 = _mm(dgate, wts["w_g"], out_dtype=F32, name="mm_dx1_gate", trans_b=True, adds=[(dz2, ALPHA)])
    dx1 = _mm(dval, wts["w_v"], out_dtype=F32, name="mm_dx1_val", trans_b=True, adds=[(dx1, 1.0)])
    dx1 = _mm(dpg, wts["w_pg"], out_dtype=F32, name="mm_dx1_pg", trans_b=True, adds=[(dx1, 1.0)])
    dz1, dz1_b, dg1, db1 = _rowwise(_ln1_bwd, [x, r1, dx1], [small["ln1_g"]], [act_f, act_b], [(1, d), (1, d)],
                                    tile=tile, name="ln1_bwd")
    g_o = _mm_tn(merged, dz1_b, out_dtype=BF16, name="mm_g_o")
    dm = _mm(dz1_b, wts["w_o"], out_dtype=F32, name="mm_dm", trans_b=True)
    dh, dza, dzb = _rowwise(_merge_bwd, gates + [z_a, z_b, dm], [],
                            [("stack", 2, SLOT_GA // 2, 8, d, BF16), act_b, act_b], [], tile=tile, name="merge_bwd")
    g_a = _mm_tn(y_a, dza, out_dtype=BF16, name="mm_g_a")
    g_b = _mm_tn(y_b, dzb, out_dtype=BF16, name="mm_g_b")
    dy_a = _mm(dza, wts["w_a"], out_dtype=F32, name="mm_dya", trans_b=True)
    dy_b = _mm(dzb, wts["w_b"], out_dtype=F32, name="mm_dyb", trans_b=True)
    dh, dws, dbs, dgv_n, dbv_n = _sgu_bwd(h, dy_a, dh, wm_b, wm_t, bs_t, small["sgu_norm_g"], small["sgu_norm_b"])
    dh, dlogits, dgn = _hgrn_bwd(h, o_all, dy_b, states, dh, small["lb_logits"], small["hgrn_norm_g"])
    g_in = _mm_tn(x_b, dh, out_dtype=BF16, name="mm_g_in")
    grad_x = _mm(dh, wts["w_in"], out_dtype=F32, name="mm_dx", trans_b=True, reduce_b=True, adds=[(dz1, ALPHA)])

    big = dict(w_in=g_in, w_a=g_a, w_b=g_b, w_o=g_o, w_g=g_g, w_v=g_v, w_down=g_down, w_pp=g_pp, w_pg=g_pg)
    sm = dict(sgu_w_s=jnp.where(mask[None], dws, 0.0), sgu_b_s=dbs[:, :GROUPS].T, sgu_norm_g=dgv_n, sgu_norm_b=dbv_n,
              lb_logits=dlogits, hgrn_norm_g=dgn, ln1_g=dg1, ln1_b=db1, conv_w=jnp.concatenate([dcw0, dcw1, dcw2], axis=0),
              conv_b=dcb, ln2_g=dg2, ln2_b=db2)
    return loss, grad_x, big, sm


_SMALL = ["sgu_w_s", "sgu_b_s", "sgu_norm_g", "sgu_norm_b", "hgrn_lb_logits", "hgrn_norm_g",
          "ln1_g", "ln1_b", "ffn_conv_b", "ln2_g", "ln2_b"]


def _rows128(a):
    flat = a.reshape(-1)
    rows = -(-flat.shape[0] // (8 * LANE)) * 8
    return jnp.pad(flat, (0, rows * LANE - flat.shape[0])).reshape(rows, LANE)


def _pack(parts):
    blocks = [_rows128(a) for a in parts]
    return jnp.concatenate(blocks, axis=0), [b.shape[0] for b in blocks]


def _unpack(packed, shapes, rows):
    out, r0 = [], 0
    for shp, r in zip(shapes, rows):
        n = math.prod(shp)
        out.append(packed[r0:r0 + r].reshape(-1)[:n].reshape(shp))
        r0 += r
    return out


def kernel(x, p, w_in, sgu_w_s, sgu_b_s, sgu_norm_g, sgu_norm_b, hgrn_lb_logits, hgrn_norm_g, w_branch, w_out, ln1_g, ln1_b, ffn_w_up, ffn_conv_w, ffn_conv_b, ffn_w_down, ln2_g, ln2_b, ple_w_proj, ple_w_gate, loss_target, m_w_in, m_sgu_w_s, m_sgu_b_s, m_sgu_norm_g, m_sgu_norm_b, m_hgrn_lb_logits, m_hgrn_norm_g, m_w_branch, m_w_out, m_ln1_g, m_ln1_b, m_ffn_w_up, m_ffn_conv_w, m_ffn_conv_b, m_ffn_w_down, m_ln2_g, m_ln2_b, m_ple_w_proj, m_ple_w_gate, v_w_in, v_sgu_w_s, v_sgu_b_s, v_sgu_norm_g, v_sgu_norm_b, v_hgrn_lb_logits, v_hgrn_norm_g, v_w_branch, v_w_out, v_ln1_g, v_ln1_b, v_ffn_w_up, v_ffn_conv_w, v_ffn_conv_b, v_ffn_w_down, v_ln2_g, v_ln2_b, v_ple_w_proj, v_ple_w_gate):
    weights = dict(w_in=w_in, sgu_w_s=sgu_w_s, sgu_b_s=sgu_b_s, sgu_norm_g=sgu_norm_g, sgu_norm_b=sgu_norm_b,
                   hgrn_lb_logits=hgrn_lb_logits, hgrn_norm_g=hgrn_norm_g, w_branch=w_branch, w_out=w_out,
                   ln1_g=ln1_g, ln1_b=ln1_b, ffn_w_up=ffn_w_up, ffn_conv_w=ffn_conv_w, ffn_conv_b=ffn_conv_b,
                   ffn_w_down=ffn_w_down, ln2_g=ln2_g, ln2_b=ln2_b, ple_w_proj=ple_w_proj, ple_w_gate=ple_w_gate)
    mom_m = dict(w_in=m_w_in, sgu_w_s=m_sgu_w_s, sgu_b_s=m_sgu_b_s, sgu_norm_g=m_sgu_norm_g, sgu_norm_b=m_sgu_norm_b,
                 hgrn_lb_logits=m_hgrn_lb_logits, hgrn_norm_g=m_hgrn_norm_g, w_branch=m_w_branch, w_out=m_w_out,
                 ln1_g=m_ln1_g, ln1_b=m_ln1_b, ffn_w_up=m_ffn_w_up, ffn_conv_w=m_ffn_conv_w, ffn_conv_b=m_ffn_conv_b,
                 ffn_w_down=m_ffn_w_down, ln2_g=m_ln2_g, ln2_b=m_ln2_b, ple_w_proj=m_ple_w_proj, ple_w_gate=m_ple_w_gate)
    mom_v = dict(w_in=v_w_in, sgu_w_s=v_sgu_w_s, sgu_b_s=v_sgu_b_s, sgu_norm_g=v_sgu_norm_g, sgu_norm_b=v_sgu_norm_b,
                 hgrn_lb_logits=v_hgrn_lb_logits, hgrn_norm_g=v_hgrn_norm_g, w_branch=v_w_branch, w_out=v_w_out,
                 ln1_g=v_ln1_g, ln1_b=v_ln1_b, ffn_w_up=v_ffn_w_up, ffn_conv_w=v_ffn_conv_w, ffn_conv_b=v_ffn_conv_b,
                 ffn_w_down=v_ffn_w_down, ln2_g=v_ln2_g, ln2_b=v_ln2_b, ple_w_proj=v_ple_w_proj, ple_w_gate=v_ple_w_gate)
    d, f = D_MODEL, D_FF
    fs = f // 4
    me = _my_index()

    shards = [w_in[0].astype(BF16), w_branch[0].astype(BF16), w_out[0].astype(BF16), ffn_w_up[0].astype(BF16),
              ffn_w_down[0].astype(BF16), ple_w_proj[0].astype(BF16), ple_w_gate[0].astype(BF16), ffn_conv_w[0]]
    w_in_g, w_br_g, w_o_g, w_up_g, w_down_g, w_pp_g, w_pg_g, conv_g = _exchange(
        shards, "gather_weights", scatter=False, slotted=(0,))
    w_br = w_br_g.transpose(1, 0, 2, 3).reshape(2, d, d)
    w_up = w_up_g.transpose(1, 0, 2).reshape(d, 2, f).transpose(1, 0, 2)
    wts = dict(w_in=w_in_g, w_a=w_br[0], w_b=w_br[1], w_o=w_o_g.reshape(d, d), w_g=w_up[0], w_v=w_up[1],
               w_down=w_down_g.reshape(f, d), w_pp=w_pp_g.transpose(1, 0, 2).reshape(256, d), w_pg=w_pg_g.reshape(d, d))
    conv_w = conv_g.transpose(1, 0, 2).reshape(3, f)
    small = dict(sgu_w_s=sgu_w_s[0], sgu_b_s=sgu_b_s[0], sgu_norm_g=sgu_norm_g, sgu_norm_b=sgu_norm_b,
                 lb_logits=hgrn_lb_logits, hgrn_norm_g=hgrn_norm_g, ln1_g=ln1_g, ln1_b=ln1_b, ln2_g=ln2_g, ln2_b=ln2_b,
                 conv_w=conv_w, conv_b=ffn_conv_b)

    loss_blk, grad_x, big, sm = _local_step(x[0], p[0, 0], loss_target[0], wts, small)

    parts = [big["w_in"],
             jnp.stack([big["w_a"], big["w_b"]]).reshape(2, N_DEV, 128, d).transpose(1, 0, 2, 3),
             big["w_o"].reshape(N_DEV, 128, d),
             jnp.concatenate([big["w_g"], big["w_v"]], axis=1).reshape(d, N_DEV, fs).transpose(1, 0, 2),
             big["w_down"].reshape(N_DEV, 352, d),
             big["w_pp"].reshape(256, N_DEV, 128).transpose(1, 0, 2),
             big["w_pg"].reshape(N_DEV, 128, d)]
    recv = _exchange(parts, "scatter_grads", scatter=True, slotted=(0,))

    out = {}

    def adam(name, parts8, shape2d):
        w2, m2, v2 = (a.reshape(shape2d) for a in (weights[name], mom_m[name], mom_v[name]))
        res = _adam_call(w2, m2, v2, parts8.reshape((N_DEV,) + shape2d), "adam_" + name)
        out[name] = tuple(r.reshape(weights[name].shape) for r in res)

    adam("w_in", recv[0], (d, d))
    adam("w_branch", recv[1], (256, d))
    adam("w_out", recv[2], (128, d))
    adam("ffn_w_up", recv[3], (d, fs))
    adam("ffn_w_down", recv[4], (352, d))
    adam("ple_w_proj", recv[5], (256, 128))
    adam("ple_w_gate", recv[6], (128, d))

    sm_parts = [sm["sgu_w_s"], sm["sgu_b_s"], sm["sgu_norm_g"], sm["sgu_norm_b"], sm["lb_logits"], sm["hgrn_norm_g"],
                sm["ln1_g"], sm["ln1_b"], sm["conv_b"], sm["ln2_g"], sm["ln2_b"], sm["conv_w"], loss_blk]
    packed, rows = _pack(sm_parts)
    all_parts, = _exchange([packed], "gather_small", scatter=False)
    w_list = [weights[n] for n in _SMALL] + [_place_taps(ffn_conv_w[0], me, f), jnp.zeros((8, LANE), F32)]
    m_list = [mom_m[n] for n in _SMALL] + [_place_taps(m_ffn_conv_w[0], me, f), jnp.zeros((8, LANE), F32)]
    v_list = [mom_v[n] for n in _SMALL] + [_place_taps(v_ffn_conv_w[0], me, f), jnp.ones((8, LANE), F32)]
    w_pk, _ = _pack(w_list)
    m_pk, _ = _pack(m_list)
    v_pk, _ = _pack(v_list)
    res = _adam_call(w_pk, m_pk, v_pk, all_parts, "adam_small")
    shapes = [weights[n].shape for n in _SMALL] + [(3, f), (8, LANE)]
    unpacked = [_unpack(r, shapes, rows) for r in res]
    for j, n in enumerate(_SMALL):
        out[n] = tuple(u[j] for u in unpacked)
    out["ffn_conv_w"] = tuple(lax.dynamic_slice_in_dim(u[len(_SMALL)], me * 352, 352, axis=1)[None] for u in unpacked)
    loss = unpacked[0][len(_SMALL) + 1][0, 0]

    order = ["w_in", "sgu_w_s", "sgu_b_s", "sgu_norm_g", "sgu_norm_b", "hgrn_lb_logits", "hgrn_norm_g", "w_branch", "w_out",
             "ln1_g", "ln1_b", "ffn_w_up", "ffn_conv_w", "ffn_conv_b", "ffn_w_down", "ln2_g", "ln2_b", "ple_w_proj", "ple_w_gate"]
    return (loss, grad_x[None], *[out[n][0] for n in order], *[out[n][1] for n in order],
            *[out[n][2] for n in order], *[out[n][3] for n in order])


def _place_taps(shard, me, f):
    return lax.dynamic_update_slice_in_dim(jnp.zeros((3, f), F32), shard, me * 352, axis=1)
```

```python
import functools
import math

import jax
import jax.numpy as jnp
from jax import lax
from jax.experimental import pallas as pl
from jax.experimental.pallas import tpu as pltpu

F32 = jnp.float32
BF16 = jnp.bfloat16

N_DEV = 8
D_MODEL = 1024
CHUNK = 64
SUB = 16
SGU_BLOCK = 128
GROUPS = 8
HEAD = 128
HEADS = 8
D_FF = 2816
LN_EPS = 1e-5
RMS_EPS = 1e-6
ALPHA = 2.0 ** 0.25
GELU_K = math.sqrt(2.0 / math.pi)
GELU_C = 0.044715
NEG = -1e30
ADAM_LR, ADAM_B1, ADAM_B2, ADAM_EPS, ADAM_WD, ADAM_STEP = 0.001, 0.9, 0.999, 1e-08, 0.01, 10
LANE = 128
SLOT_Q, SLOT_F, SLOT_I, SLOT_OG, SLOT_U, SLOT_V, SLOT_GA, SLOT_GB = range(8)


def _slot_of_group(k):
    return jnp.where(k < 2, k + 4, jnp.where(k < 6, k - 2, k))


MIB = 1024 * 1024
VMEM_V7X = 64 * MIB
VMEM_FLOOR = 32 * MIB
MM_TILES = (1024, 1408, 512, 256, 128)


def _params(sem, need=0):
    limit = min(max(need + need // 4, VMEM_FLOOR), VMEM_V7X - 4 * MIB)
    return pltpu.CompilerParams(dimension_semantics=sem, vmem_limit_bytes=limit)


def _pick(n, prefs):
    for t in prefs:
        if n % t == 0:
            return t
    return n


def _gelu(x):
    return 0.5 * x * (1.0 + jnp.tanh(GELU_K * (x + GELU_C * x * x * x)))


def _gelu_grad(x):
    t = jnp.tanh(GELU_K * (x + GELU_C * x * x * x))
    return 0.5 * (1.0 + t) + 0.5 * x * (1.0 - t * t) * GELU_K * (1.0 + 3.0 * GELU_C * x * x)


def _silu_grad(x, s):
    return s * (1.0 + x * (1.0 - s))


def _dot(a, b):
    return jnp.dot(a.astype(BF16), b.astype(BF16), preferred_element_type=F32)


def _dot_nt(a, b):
    return lax.dot_general(a.astype(BF16), b.astype(BF16), (((1,), (1,)), ((), ())), preferred_element_type=F32)


def _dot_tn(a, b):
    return lax.dot_general(a.astype(BF16), b.astype(BF16), (((0,), (0,)), ((), ())), preferred_element_type=F32)


def _mean(x):
    return jnp.mean(x, axis=-1, keepdims=True)


def _sum0(x):
    return jnp.sum(x, axis=0, keepdims=True)


def _mm(a, b, *, out_dtype, name, trans_b=False, reduce_b=False, adds=(), ex=None):
    squeeze = b.ndim == 2
    a3 = a if a.ndim == 3 else a[None]
    b3 = b if b.ndim == 3 else b[None]
    ba, m, k = a3.shape
    bb = b3.shape[0]
    n = b3.shape[1] if trans_b else b3.shape[2]
    tm = _pick(m, MM_TILES)
    tn = _pick(n, MM_TILES)
    if reduce_b:
        bo, steps = 1, bb
        a_map = lambda o, i, j, r: (r if ba > 1 else 0, i, 0)
        b_map = (lambda o, i, j, r: (r, j, 0)) if trans_b else (lambda o, i, j, r: (r, 0, j))
    else:
        bo, steps = bb, 1
        a_map = lambda o, i, j, r: (o if ba > 1 else 0, i, 0)
        b_map = (lambda o, i, j, r: (o, j, 0)) if trans_b else (lambda o, i, j, r: (o, 0, j))
    o_map = lambda o, i, j, r: (o, i, j)
    add_arrays = [x if x.ndim == 3 else x[None] for x, _ in adds]
    add_scales = [s for _, s in adds]
    n_add = len(adds)
    dot = _dot_nt if trans_b else _dot

    def finish(acc, add_refs, o_ref):
        for ref, s in zip(add_refs, add_scales):
            acc = acc + s * ref[...].astype(F32)
        o_ref[...] = acc.astype(o_ref.dtype)

    grid = (bo, m // tm, n // tn, steps)

    def body(*refs):
        ins, (o_ref,), scratch, xrefs = _split_refs(refs, 2 + n_add, 1, 1 if reduce_b else 0, ex)
        a_ref, b_ref, add_refs = ins[0], ins[1], ins[2:]
        step = ((pl.program_id(0) * grid[1] + pl.program_id(1)) * grid[2] + pl.program_id(2)) * grid[3] + pl.program_id(3)
        if ex:
            @pl.when(step == 0)
            def _():
                ex.start(*xrefs)

        if reduce_b:
            acc, = scratch
            r = pl.program_id(3)

            @pl.when(r == 0)
            def _():
                acc[...] = jnp.zeros_like(acc)

            acc[...] += dot(a_ref[...], b_ref[...])

            @pl.when(r == steps - 1)
            def _():
                finish(acc[...], add_refs, o_ref)
        else:
            finish(dot(a_ref[...], b_ref[...]), add_refs, o_ref)

        if ex:
            @pl.when(step == math.prod(grid) - 1)
            def _():
                ex.finish(*xrefs)

    b_block = (None, tn, k) if trans_b else (None, k, tn)
    out_bytes = tm * tn * jnp.dtype(out_dtype).itemsize
    need = 2 * (tm * k * a3.dtype.itemsize + k * tn * b3.dtype.itemsize + out_bytes + n_add * tm * tn * 4)
    need += 2 * tm * tn * 4
    sem = ("arbitrary",) * 4 if ex else ("parallel", "parallel", "parallel", "arbitrary")
    res = pl.pallas_call(
        body,
        grid=grid,
        in_specs=[pl.BlockSpec((None, tm, k), a_map), pl.BlockSpec(b_block, b_map)]
        + [pl.BlockSpec((None, tm, tn), o_map) for _ in adds] + (ex.in_specs if ex else []),
        out_specs=[pl.BlockSpec((None, tm, tn), o_map)] + (ex.out_specs if ex else []),
        out_shape=[jax.ShapeDtypeStruct((bo, m, n), out_dtype)] + (ex.out_shape if ex else []),
        scratch_shapes=([pltpu.VMEM((tm, tn), F32)] if reduce_b else []) + (ex.scratch if ex else []),
        compiler_params=_params(sem, need),
        name=name,
    )(a3, b3, *add_arrays, *(ex.arrays if ex else []))
    out = res[0][0] if (reduce_b or squeeze) else res[0]
    return (out, res[1:]) if ex else out


def _mm_tn(a, b, *, out_dtype, name):
    squeeze = b.ndim == 2
    b3 = b if b.ndim == 3 else b[None]
    t, m = a.shape
    bb, _, n = b3.shape
    tm = _pick(m, MM_TILES)
    tn = _pick(n, MM_TILES)
    tt = _pick(t, (1024, 512, 256, 128))
    steps = t // tt
    need = 2 * (tt * tm * a.dtype.itemsize + tt * tn * b3.dtype.itemsize + tm * tn * jnp.dtype(out_dtype).itemsize)
    need += 2 * tm * tn * 4

    def body(a_ref, b_ref, o_ref, acc):
        r = pl.program_id(3)

        @pl.when(r == 0)
        def _():
            acc[...] = jnp.zeros_like(acc)

        acc[...] += _dot_tn(a_ref[...], b_ref[...])

        @pl.when(r == steps - 1)
        def _():
            o_ref[...] = acc[...].astype(o_ref.dtype)

    out = pl.pallas_call(
        body,
        grid=(bb, m // tm, n // tn, steps),
        in_specs=[pl.BlockSpec((tt, tm), lambda o, i, j, r: (r, i)),
                  pl.BlockSpec((None, tt, tn), lambda o, i, j, r: (o, r, j))],
        out_specs=pl.BlockSpec((None, tm, tn), lambda o, i, j, r: (o, i, j)),
        out_shape=jax.ShapeDtypeStruct((bb, m, n), out_dtype),
        scratch_shapes=[pltpu.VMEM((tm, tn), F32)],
        compiler_params=_params(("parallel", "parallel", "parallel", "arbitrary"), need),
        name=name,
    )(a, b3)
    return out[0] if squeeze else out


def _rowwise(fn, rows, consts, row_outs, acc_outs, *, tile, name):
    first = rows[0][0] if isinstance(rows[0], tuple) else rows[0]
    t = first.shape[-2]
    steps = t // tile
    arrays, in_specs = [], []
    for r in rows:
        if isinstance(r, tuple) and isinstance(r[1], pl.BlockSpec):
            arrays.append(r[0])
            in_specs.append(r[1])
        elif isinstance(r, tuple):
            arr, bidx = r
            arrays.append(arr)
            in_specs.append(pl.BlockSpec((None, tile, arr.shape[-1]), functools.partial(lambda i, b: (b, i, 0), b=bidx)))
        else:
            arrays.append(r)
            in_specs.append(pl.BlockSpec((tile, r.shape[-1]), lambda i: (i, 0)))
    for c in consts:
        arrays.append(c)
        in_specs.append(pl.BlockSpec(c.shape, lambda i: (0, 0)))
    n_in, n_row = len(arrays), len(row_outs)
    out_shape, out_specs = [], []
    for ro in row_outs:
        if ro[0] == "stack":
            _, cnt, blk, total, w, dt = ro
            out_shape.append(jax.ShapeDtypeStruct((total, t, w), dt))
            out_specs.append(pl.BlockSpec((cnt, tile, w), functools.partial(lambda i, b: (b, i, 0), b=blk)))
        else:
            w, dt = ro
            out_shape.append(jax.ShapeDtypeStruct((t, w), dt))
            out_specs.append(pl.BlockSpec((tile, w), lambda i: (i, 0)))
    out_shape += [jax.ShapeDtypeStruct(s, F32) for s in acc_outs]
    out_specs += [pl.BlockSpec(s, lambda i: (0, 0)) for s in acc_outs]
    blocks = [math.prod(d for d in sp.block_shape if d) * arr.dtype.itemsize for sp, arr in zip(in_specs, arrays)]
    blocks += [math.prod(d for d in sp.block_shape if d) * sh.dtype.itemsize for sp, sh in zip(out_specs, out_shape)]
    need = 2 * sum(blocks) + 6 * tile * max(a.shape[-1] for a in arrays) * 4

    def body(*refs):
        ins, outs = refs[:n_in], refs[n_in:]
        i = pl.program_id(0)
        res = fn(i, steps, *[r[...] for r in ins])
        res = res if isinstance(res, (tuple, list)) else (res,)
        for ref, val in zip(outs[:n_row], res[:n_row]):
            ref[...] = val.astype(ref.dtype)
        if acc_outs:
            @pl.when(i == 0)
            def _():
                for ref in outs[n_row:]:
                    ref[...] = jnp.zeros_like(ref)

            for ref, val in zip(outs[n_row:], res[n_row:]):
                ref[...] += val

    return pl.pallas_call(
        body,
        grid=(steps,),
        in_specs=in_specs,
        out_specs=out_specs,
        out_shape=out_shape,
        compiler_params=_params(("arbitrary",), need),
        name=name,
    )(*arrays)


def _ln_stats(z):
    mu = _mean(z)
    zc = z - mu
    rstd = lax.rsqrt(_mean(zc * zc) + LN_EPS)
    return zc * rstd, rstd


def _ln_bwd(dy, xhat, rstd, g):
    dxh = dy * g
    return rstd * (dxh - _mean(dxh) - xhat * _mean(dxh * xhat))


def _sgu_fwd(h, wm, bs_t, g_v, b_v):
    t = h.shape[1]

    def body(u_ref, v_ref, wm_ref, bs_ref, g_ref, b_ref, y_ref):
        xhat, _ = _ln_stats(_gelu(v_ref[...]))
        vn = (xhat * g_ref[...] + b_ref[...]).astype(BF16)
        gu = _gelu(u_ref[...])
        for g in range(GROUPS):
            sl = slice(g * HEAD, (g + 1) * HEAD)
            mixed = _dot(wm_ref[g], vn[:, sl]) + bs_ref[:, g:g + 1]
            y_ref[:, sl] = (gu[:, sl] * mixed).astype(BF16)

    blk = lambda b: pl.BlockSpec((None, SGU_BLOCK, D_MODEL), functools.partial(lambda i, b: (b, i, 0), b=b))
    whole = lambda s: pl.BlockSpec(s, lambda i: (0,) * len(s))
    return pl.pallas_call(
        body,
        grid=(t // SGU_BLOCK,),
        in_specs=[blk(SLOT_U), blk(SLOT_V), whole(wm.shape), whole(bs_t.shape), whole(g_v.shape), whole(b_v.shape)],
        out_specs=pl.BlockSpec((SGU_BLOCK, D_MODEL), lambda i: (i, 0)),
        out_shape=jax.ShapeDtypeStruct((t, D_MODEL), BF16),
        compiler_params=_params(("parallel",)),
        name="sgu_fwd",
    )(h, h, wm, bs_t, g_v, b_v)


def _sgu_bwd(h, dy, dh, wm, wm_t, bs_t, g_v, b_v):
    t = h.shape[1]

    def body(u_ref, v_ref, dy_ref, dh_in, wm_ref, wmt_ref, bs_ref, g_ref, b_ref,
             duv_ref, dw_ref, dbs_ref, dg_ref, db_ref, dvn_ref):
        del dh_in
        du_ref, dv_ref = duv_ref.at[0], duv_ref.at[1]
        i = pl.program_id(0)

        @pl.when(i == 0)
        def _():
            dw_ref[...] = jnp.zeros_like(dw_ref)
            dbs_ref[...] = jnp.zeros_like(dbs_ref)
            dg_ref[...] = jnp.zeros_like(dg_ref)
            db_ref[...] = jnp.zeros_like(db_ref)

        u = u_ref[...]
        v = v_ref[...]
        xhat, rstd = _ln_stats(_gelu(v))
        vn = (xhat * g_ref[...] + b_ref[...]).astype(BF16)
        gu = _gelu(u)
        gup = _gelu_grad(u)
        lane = lax.broadcasted_iota(jnp.int32, (SGU_BLOCK, LANE), 1)
        dbs = jnp.zeros((SGU_BLOCK, LANE), F32)
        for g in range(GROUPS):
            sl = slice(g * HEAD, (g + 1) * HEAD)
            vn_g = vn[:, sl]
            mixed = _dot(wm_ref[g], vn_g) + bs_ref[:, g:g + 1]
            dy_g = dy_ref[:, sl]
            du_ref[:, sl] = (dy_g * mixed * gup[:, sl]).astype(BF16)
            dmix = dy_g * gu[:, sl]
            dmb = dmix.astype(BF16)
            dvn_ref[:, sl] = _dot(wmt_ref[g], dmb)
            dw_ref[g] += _dot_nt(dmb, vn_g)
            dbs = dbs + jnp.where(lane == g, jnp.sum(dmix, axis=1, keepdims=True), 0.0)
        dbs_ref[...] += dbs
        dvn = dvn_ref[...]
        dg_ref[...] += _sum0(dvn * xhat)
        db_ref[...] += _sum0(dvn)
        dv_ref[...] = (_ln_bwd(dvn, xhat, rstd, g_ref[...]) * _gelu_grad(v)).astype(BF16)

    blk = lambda b: pl.BlockSpec((None, SGU_BLOCK, D_MODEL), functools.partial(lambda i, b: (b, i, 0), b=b))
    row = pl.BlockSpec((SGU_BLOCK, D_MODEL), lambda i: (i, 0))
    whole = lambda s: pl.BlockSpec(s, lambda i: (0,) * len(s))
    vec = (1, D_MODEL)
    return pl.pallas_call(
        body,
        grid=(t // SGU_BLOCK,),
        in_specs=[blk(SLOT_U), blk(SLOT_V), row, pl.BlockSpec(memory_space=pl.ANY),
                  whole(wm.shape), whole(wm_t.shape), whole(bs_t.shape), whole(vec), whole(vec)],
        out_specs=[pl.BlockSpec((2, SGU_BLOCK, D_MODEL), lambda i: (SLOT_U // 2, i, 0)),
                   whole(wm.shape), whole((SGU_BLOCK, LANE)), whole(vec), whole(vec)],
        out_shape=[jax.ShapeDtypeStruct(dh.shape, BF16),
                   jax.ShapeDtypeStruct(wm.shape, F32), jax.ShapeDtypeStruct((SGU_BLOCK, LANE), F32),
                   jax.ShapeDtypeStruct(vec, F32), jax.ShapeDtypeStruct(vec, F32)],
        scratch_shapes=[pltpu.VMEM((SGU_BLOCK, D_MODEL), F32)],
        input_output_aliases={3: 0},
        compiler_params=_params(("arbitrary",)),
        name="sgu_bwd",
    )(h, h, dy, dh, wm, wm_t, bs_t, g_v, b_v)


def _split3(x):
    hi = x.astype(BF16)
    r1 = x - hi.astype(F32)
    mid = r1.astype(BF16)
    lo = (r1 - mid.astype(F32)).astype(BF16)
    return hi, mid, lo


def _tri_matmul(tri, x):
    hi, mid, lo = _split3(x)
    dot = lambda p: jnp.dot(tri, p, preferred_element_type=F32)
    return dot(hi) + dot(mid) + dot(lo)


def _lower_bound(logits):
    l0, l1 = logits[0:1, :], logits[1:2, :]
    mx = jnp.maximum(l0, l1)
    e0, e1 = jnp.exp(l0 - mx), jnp.exp(l1 - mx)
    return e0 / (e0 + e1)


def _hgrn_gates(q_raw, f_raw, lb):
    q = q_raw * jax.nn.sigmoid(q_raw)
    sig = jax.nn.sigmoid(f_raw)
    f = lb + (1.0 - lb) * sig
    row = lax.broadcasted_iota(jnp.int32, (CHUNK, CHUNK), 0)
    col = lax.broadcasted_iota(jnp.int32, (CHUNK, CHUNK), 1)
    c = _tri_matmul((row >= col).astype(BF16), jnp.log(f))
    return q, sig, f, 1.0 - f, c


def _offdiag_terms(qh, kh, ch, tb):
    rows = slice(tb * SUB, (tb + 1) * SUB)
    r = ch[tb * SUB - 1:tb * SUB, :]
    eqh = jnp.exp(ch[rows] - r)
    ekh = jnp.exp(jnp.minimum(r - ch, 0.0))
    return rows, eqh, qh[rows] * eqh, ekh, kh * ekh


def _split_refs(refs, n_in, n_out, n_scratch, ex):
    nx = ex.n if ex else 0
    ins, refs = refs[:n_in], refs[n_in:]
    xsrc, refs = refs[:nx], refs[nx:]
    outs, refs = refs[:n_out], refs[n_out:]
    xout, refs = refs[:nx], refs[nx:]
    return ins, outs, refs[:n_scratch], (xsrc, xout, refs[n_scratch:])


def _hgrn_fwd(h, logits, g_norm, ex=None):
    t = h.shape[1]
    nc = t // CHUNK

    def body(*refs):
        ins, outs, scratch, xrefs = _split_refs(refs, 6, 3, 4, ex)
        q_ref, f_ref, i_ref, og_ref, lg_ref, gn_ref = ins
        y_ref, o_ref, sall_ref = outs
        st_ref, q_s, k_s, c_s = scratch

        @pl.when(pl.program_id(0) == 0)
        def _():
            st_ref[...] = jnp.zeros_like(st_ref)
            if ex:
                ex.start(*xrefs)

        lb = _lower_bound(lg_ref[...])
        q, _, _, k, c = _hgrn_gates(q_ref[...], f_ref[...], lb)
        q_s[...] = q
        k_s[...] = k
        c_s[...] = c
        col64 = lax.broadcasted_iota(jnp.int32, (SUB, CHUNK), 1)
        trow = lax.broadcasted_iota(jnp.int32, (SUB, HEAD), 0)

        def head(hd, carry):
            sl = pl.ds(pl.multiple_of(hd * HEAD, HEAD), HEAD)
            qh, kh, ch, ih = q_s[:, sl], k_s[:, sl], c_s[:, sl], i_ref[:, sl]
            st = st_ref[hd]
            sall_ref[hd] = st
            c_last = ch[CHUNK - 1:CHUNK, :]
            o = _dot_nt(qh * jnp.exp(ch), st)
            st_ref[hd] = st * jnp.exp(c_last) + _dot_tn(ih, kh * jnp.exp(c_last - ch))
            a_rows = [jnp.zeros((SUB, CHUNK), F32)]
            for tb in range(1, CHUNK // SUB):
                _, _, q_hat, _, k_hat = _offdiag_terms(qh, kh, ch, tb)
                a_rows.append(jnp.where(col64 < tb * SUB, _dot_nt(q_hat, k_hat), 0.0))
            o = o + _dot(jnp.concatenate(a_rows, axis=0), ih)
            o_rows = []
            for b in range(CHUNK // SUB):
                rows = slice(b * SUB, (b + 1) * SUB)
                qb, cb, kb, ib = qh[rows], ch[rows], kh[rows], ih[rows]
                ob = jnp.zeros((SUB, HEAD), F32)
                for s in range(SUB):
                    dec = jnp.exp(jnp.where(trow >= s, cb - cb[s:s + 1, :], NEG))
                    a = jnp.sum(qb * dec * kb[s:s + 1, :], axis=1, keepdims=True)
                    ob = ob + a * ib[s:s + 1, :]
                o_rows.append(ob)
            o = o + jnp.concatenate(o_rows, axis=0)
            o_ref[:, sl] = o
            og = og_ref[:, sl]
            on = o * lax.rsqrt(_mean(o * o) + RMS_EPS)
            y_ref[:, sl] = (on * gn_ref[:, sl] * (og * jax.nn.sigmoid(og))).astype(BF16)
            return carry

        lax.fori_loop(0, HEADS, head, 0)

        if ex:
            @pl.when(pl.program_id(0) == nc - 1)
            def _():
                ex.finish(*xrefs)

    blk = lambda b: pl.BlockSpec((None, CHUNK, D_MODEL), functools.partial(lambda n, b: (b, n, 0), b=b))
    row = pl.BlockSpec((CHUNK, D_MODEL), lambda n: (n, 0))
    whole = lambda s: pl.BlockSpec(s, lambda n: (0,) * len(s))
    res = pl.pallas_call(
        body,
        grid=(nc,),
        in_specs=[blk(SLOT_Q), blk(SLOT_F), blk(SLOT_I), blk(SLOT_OG), whole(logits.shape), whole(g_norm.shape)]
        + (ex.in_specs if ex else []),
        out_specs=[row, row, pl.BlockSpec((None, HEADS, HEAD, HEAD), lambda n: (n, 0, 0, 0))] + (ex.out_specs if ex else []),
        out_shape=[jax.ShapeDtypeStruct((t, D_MODEL), BF16), jax.ShapeDtypeStruct((t, D_MODEL), F32),
                   jax.ShapeDtypeStruct((nc, HEADS, HEAD, HEAD), F32)] + (ex.out_shape if ex else []),
        scratch_shapes=[pltpu.VMEM((HEADS, HEAD, HEAD), F32)] + [pltpu.VMEM((CHUNK, D_MODEL), F32)] * 3
        + (ex.scratch if ex else []),
        compiler_params=_params(("arbitrary",)),
        name="hgrn_fwd",
    )(h, h, h, h, logits, g_norm, *(ex.arrays if ex else []))
    return res[0], res[1], res[2], res[3:]


def _hgrn_bwd(h, o_all, dy, states, dh, logits, g_norm, ex=None):
    t = h.shape[1]
    nc = t // CHUNK

    def body(*refs):
        ins, outs, scratch, xrefs = _split_refs(refs, 10, 3, 8, ex)
        q_ref, f_ref, i_ref, og_ref, o_ref, dy_ref, sall_ref, _, lg_ref, gn_ref = ins
        dqfio_ref, dlg_ref, dgn_ref = outs
        dst_ref, dlb_ref, q_s, k_s, c_s, dq_s, dk_s, dc_s = scratch
        dq_ref, df_ref, di_ref, dog_ref = (dqfio_ref.at[s] for s in (SLOT_Q, SLOT_F, SLOT_I, SLOT_OG))
        n = pl.program_id(0)

        @pl.when(n == 0)
        def _():
            dst_ref[...] = jnp.zeros_like(dst_ref)
            dlb_ref[...] = jnp.zeros_like(dlb_ref)
            dgn_ref[...] = jnp.zeros_like(dgn_ref)
            if ex:
                ex.start(*xrefs)

        lb = _lower_bound(lg_ref[...])
        q_raw = q_ref[...]
        q, sig, f, k, c = _hgrn_gates(q_raw, f_ref[...], lb)
        q_s[...] = q
        k_s[...] = k
        c_s[...] = c
        col64 = lax.broadcasted_iota(jnp.int32, (SUB, CHUNK), 1)
        trow = lax.broadcasted_iota(jnp.int32, (SUB, HEAD), 0)
        row64 = lax.broadcasted_iota(jnp.int32, (CHUNK, HEAD), 0)

        def head(hd, carry):
            sl = pl.ds(pl.multiple_of(hd * HEAD, HEAD), HEAD)
            qh, kh, ch, ih = q_s[:, sl], k_s[:, sl], c_s[:, sl], i_ref[:, sl]
            st = sall_ref[hd]
            dst = dst_ref[hd]
            oh, dyh, og, gn = o_ref[:, sl], dy_ref[:, sl], og_ref[:, sl], gn_ref[:, sl]
            sg = jax.nn.sigmoid(og)
            sil = og * sg
            rms = lax.rsqrt(_mean(oh * oh) + RMS_EPS)
            on = oh * rms
            dog_ref[:, sl] = (dyh * on * gn * _silu_grad(og, sg)).astype(BF16)
            dgn_ref[:, sl] += _sum0(dyh * on * sil)
            don = dyh * gn * sil
            do = rms * (don - on * _mean(don * on))
            dob = do.astype(BF16)

            c_last = ch[CHUNK - 1:CHUNK, :]
            eq = jnp.exp(ch)
            q_til = qh * eq
            ekl = jnp.exp(c_last - ch)
            k_til = kh * ekl
            ecl = jnp.exp(c_last)
            dq_til = _dot(dob, st)
            dk_til = _dot(ih, dst)
            di = _dot_nt(k_til, dst)
            dc_last = _sum0(dk_til * k_til) + _sum0(dst * st) * ecl
            dst_ref[hd] = _dot_tn(dob, q_til) + dst * ecl
            dq = dq_til * eq
            dc = dq_til * q_til - dk_til * k_til
            dk = dk_til * ekl

            da_full = _dot_nt(dob, ih)
            a_rows = [jnp.zeros((SUB, CHUNK), F32)]
            dq_rows = [jnp.zeros((SUB, HEAD), F32)]
            dc_rows = [jnp.zeros((SUB, HEAD), F32)]
            for tb in range(1, CHUNK // SUB):
                rows, eqh, q_hat, ekh, k_hat = _offdiag_terms(qh, kh, ch, tb)
                keep = col64 < tb * SUB
                a_rows.append(jnp.where(keep, _dot_nt(q_hat, k_hat), 0.0))
                da = jnp.where(keep, da_full[rows], 0.0)
                dq_hat = _dot(da, k_hat)
                dk_hat = _dot_tn(da, q_hat)
                dq_rows.append(dq_hat * eqh)
                dc_rows.append(dq_hat * q_hat)
                dk = dk + dk_hat * ekh
                dc = dc - dk_hat * k_hat
            di = di + _dot_tn(jnp.concatenate(a_rows, axis=0), dob)

            dk_rows, di_rows = [], []
            for b in range(CHUNK // SUB):
                rows = slice(b * SUB, (b + 1) * SUB)
                qb, cb, kb, ib, dob_ = qh[rows], ch[rows], kh[rows], ih[rows], do[rows]
                dqb, dcb = dq_rows[b], dc_rows[b]
                dkb = jnp.zeros((SUB, HEAD), F32)
                dib = jnp.zeros((SUB, HEAD), F32)
                for s in range(SUB):
                    ks = kb[s:s + 1, :]
                    dec = jnp.exp(jnp.where(trow >= s, cb - cb[s:s + 1, :], NEG))
                    a = jnp.sum(qb * dec * ks, axis=1, keepdims=True)
                    da = jnp.sum(dob_ * ib[s:s + 1, :], axis=1, keepdims=True)
                    gk = da * dec
                    dqb = dqb + gk * ks
                    dks = _sum0(gk * qb)
                    dcb = dcb + gk * ks * qb - jnp.where(trow == s, dks * ks, 0.0)
                    dkb = dkb + jnp.where(trow == s, dks, 0.0)
                    dib = dib + jnp.where(trow == s, _sum0(a * dob_), 0.0)
                dq_rows[b], dc_rows[b] = dqb, dcb
                dk_rows.append(dkb)
                di_rows.append(dib)
            dq = dq + jnp.concatenate(dq_rows, axis=0)
            dk = dk + jnp.concatenate(dk_rows, axis=0)
            dc = dc + jnp.concatenate(dc_rows, axis=0) + jnp.where(row64 == CHUNK - 1, dc_last, 0.0)
            di_ref[:, sl] = (di + jnp.concatenate(di_rows, axis=0)).astype(BF16)
            dq_s[:, sl] = dq
            dk_s[:, sl] = dk
            dc_s[:, sl] = dc
            return carry

        lax.fori_loop(0, HEADS, head, 0)

        row = lax.broadcasted_iota(jnp.int32, (CHUNK, CHUNK), 0)
        col = lax.broadcasted_iota(jnp.int32, (CHUNK, CHUNK), 1)
        dlf = _tri_matmul((row <= col).astype(BF16), dc_s[...])
        df = dlf / f - dk_s[...]
        dlb_ref[...] += _sum0(df * (1.0 - sig))
        df_ref[...] = (df * (1.0 - lb) * sig * (1.0 - sig)).astype(BF16)
        dq_ref[...] = (dq_s[...] * _silu_grad(q_raw, jax.nn.sigmoid(q_raw))).astype(BF16)

        @pl.when(n == nc - 1)
        def _():
            d0 = dlb_ref[...] * lb * (1.0 - lb)
            dlg_ref[0:1, :] = d0
            dlg_ref[1:2, :] = -d0
            if ex:
                ex.finish(*xrefs)

    rev = lambda n: nc - 1 - n
    blk = lambda b: pl.BlockSpec((None, CHUNK, D_MODEL), functools.partial(lambda n, b: (b, rev(n), 0), b=b))
    row = pl.BlockSpec((CHUNK, D_MODEL), lambda n: (rev(n), 0))
    whole = lambda s: pl.BlockSpec(s, lambda n: (0,) * len(s))
    vec = (1, D_MODEL)
    res = pl.pallas_call(
        body,
        grid=(nc,),
        in_specs=[blk(SLOT_Q), blk(SLOT_F), blk(SLOT_I), blk(SLOT_OG), row, row,
                  pl.BlockSpec((None, HEADS, HEAD, HEAD), lambda n: (rev(n), 0, 0, 0)),
                  pl.BlockSpec(memory_space=pl.ANY), whole(logits.shape), whole(vec)] + (ex.in_specs if ex else []),
        out_specs=[pl.BlockSpec((4, CHUNK, D_MODEL), lambda n: (SLOT_Q // 4, rev(n), 0)), whole((2, D_MODEL)), whole(vec)]
        + (ex.out_specs if ex else []),
        out_shape=[jax.ShapeDtypeStruct(dh.shape, BF16), jax.ShapeDtypeStruct((2, D_MODEL), F32),
                   jax.ShapeDtypeStruct(vec, F32)] + (ex.out_shape if ex else []),
        scratch_shapes=[pltpu.VMEM((HEADS, HEAD, HEAD), F32), pltpu.VMEM(vec, F32)]
        + [pltpu.VMEM((CHUNK, D_MODEL), F32)] * 6 + (ex.scratch if ex else []),
        input_output_aliases={7: 0},
        compiler_params=_params(("arbitrary",)),
        name="hgrn_bwd",
    )(h, h, h, h, o_all, dy, states, dh, logits, g_norm, *(ex.arrays if ex else []))
    return res[0], res[1], res[2], res[3:]


def _merge_fwd(i, n, ga, gb, za, zb):
    return jax.nn.sigmoid(ga) * za + jax.nn.sigmoid(gb) * zb


def _merge_bwd(i, n, ga, gb, za, zb, dm):
    sa, sb = jax.nn.sigmoid(ga), jax.nn.sigmoid(gb)
    dgates = jnp.stack([(dm * za * sa * (1.0 - sa)).astype(BF16), (dm * zb * sb * (1.0 - sb)).astype(BF16)])
    return dgates, dm * sa, dm * sb


def _ln1_fwd(i, n, x, r1, g, b):
    xhat, _ = _ln_stats(ALPHA * x + r1)
    x1 = xhat * g + b
    return x1, x1


def _ln1_bwd(i, n, x, r1, dx1, g):
    xhat, rstd = _ln_stats(ALPHA * x + r1)
    dz = _ln_bwd(dx1, xhat, rstd, g)
    return dz, dz, _sum0(dx1 * xhat), _sum0(dx1)


def _ln2_loss(i, n, x1, fo, pg, pp, tgt, g, b):
    sg = jax.nn.sigmoid(pg)
    xhat, rstd = _ln_stats(ALPHA * x1 + fo + sg * pp)
    diff = xhat * g + b - tgt
    loss = 0.5 * jnp.sum(_mean(diff * diff), axis=0, keepdims=True)
    dy = diff * (1.0 / D_MODEL)
    dz = _ln_bwd(dy, xhat, rstd, g)
    return (dz, dz, dz * pp * sg * (1.0 - sg), dz * sg,
            jnp.broadcast_to(loss, (8, LANE)), _sum0(dy * xhat), _sum0(dy))


def _shift_down(cur, halo, tile):
    row = lax.broadcasted_iota(jnp.int32, cur.shape, 0)
    m1 = jnp.where(row == 0, halo[7:8, :], pltpu.roll(cur, 1, 0))
    m2 = jnp.where(row == 0, halo[6:7, :], jnp.where(row == 1, halo[7:8, :], pltpu.roll(cur, 2, 0)))
    return m1, m2


def _shift_up(cur, halo, tile):
    row = lax.broadcasted_iota(jnp.int32, cur.shape, 0)
    p1 = jnp.where(row == tile - 1, halo[0:1, :], pltpu.roll(cur, tile - 1, 0))
    p2 = jnp.where(row == tile - 2, halo[0:1, :], jnp.where(row == tile - 1, halo[1:2, :], pltpu.roll(cur, tile - 2, 0)))
    return p1, p2


def _conv_pre(i, gate, halo, w, b, tile):
    halo = jnp.where(i == 0, 0.0, halo)
    m1, m2 = _shift_down(gate, halo, tile)
    return w[0:1, :] * m2 + w[1:2, :] * m1 + w[2:3, :] * gate + b, m1, m2


def _conv_fwd(tile, i, n, gate, halo, val, w, b):
    cg, _, _ = _conv_pre(i, gate, halo, w, b, tile)
    return _gelu(cg) * val


def _conv_bwd_a(tile, i, n, gate, halo, val, dhid, w, b):
    cg, m1, m2 = _conv_pre(i, gate, halo, w, b, tile)
    dcg = dhid * val * _gelu_grad(cg)
    return dcg, dhid * _gelu(cg), _sum0(dcg * m2), _sum0(dcg * m1), _sum0(dcg * gate), _sum0(dcg)


def _conv_bwd_b(tile, i, n, dcg, halo, w):
    halo = jnp.where(i == n - 1, 0.0, halo)
    p1, p2 = _shift_up(dcg, halo, tile)
    return w[2:3, :] * dcg + w[1:2, :] * p1 + w[0:1, :] * p2


def _halo_spec(width, tile, t, nxt):
    per = tile // 8
    last = t // 8 - 1
    if nxt:
        return pl.BlockSpec((8, width), lambda i: (jnp.minimum((i + 1) * per, last), 0))
    return pl.BlockSpec((8, width), lambda i: (jnp.maximum(i * per - 1, 0), 0))


def _adamw(i, n, w, m, v, parts):
    g = parts[0].astype(F32)
    for j in range(1, N_DEV):
        g = g + parts[j].astype(F32)
    m_new = ADAM_B1 * m + (1.0 - ADAM_B1) * g
    v_new = ADAM_B2 * v + (1.0 - ADAM_B2) * (g * g)
    m_hat = m_new / (1.0 - ADAM_B1 ** ADAM_STEP)
    v_hat = v_new / (1.0 - ADAM_B2 ** ADAM_STEP)
    delta = -ADAM_LR * (m_hat / (jnp.sqrt(v_hat) + ADAM_EPS) + ADAM_WD * w)
    return g, delta, m_new, v_new


def _adam_call(w, m, v, parts, name):
    r, c = w.shape
    tile = _pick(r, (256, 128)) if r > 256 else r
    spec = pl.BlockSpec((N_DEV, tile, c), lambda i: (0, i, 0))
    return _rowwise(_adamw, [w, m, v, (parts, spec)], [], [(c, F32)] * 4, [], tile=tile, name=name)


def _peer(k):
    x, y, c = lax.axis_index("x"), lax.axis_index("y"), lax.axis_index("c")
    px = x ^ ((k >> 2) & 1)
    py = y ^ ((k >> 1) & 1)
    pc = c ^ (k & 1)
    return (px, py, pc), 4 * px + 2 * py + pc


def _my_index():
    return 4 * lax.axis_index("x") + 2 * lax.axis_index("y") + lax.axis_index("c")


class _Exchange:
    def __init__(self, entries):
        self.arrays = [a for a, _ in entries]
        self.scatter = [k.startswith("scatter") for _, k in entries]
        self.slots = [k.endswith("slots") for _, k in entries]
        self.n = len(entries)
        self.in_specs = [pl.BlockSpec(memory_space=pl.ANY)] * self.n
        self.out_specs = [pl.BlockSpec(memory_space=pl.ANY)] * self.n
        shapes = [tuple(a.shape[1:]) if sc else tuple(a.shape) for a, sc in zip(self.arrays, self.scatter)]
        self.out_shape = [jax.ShapeDtypeStruct((N_DEV,) + s, a.dtype) for s, a in zip(shapes, self.arrays)]
        per = N_DEV - 1
        self.scratch = [pltpu.SemaphoreType.DMA((self.n * per,)), pltpu.SemaphoreType.DMA((self.n * per,)),
                        pltpu.SemaphoreType.DMA((self.n,))]

    def _copies(self, srcs, outs, sems):
        send_sems, recv_sems, local_sems = sems
        me = _my_index()
        per = N_DEV - 1
        local, sends, arrivals = [], [], []
        for a in range(self.n):
            place = (lambda idx: _slot_of_group(idx)) if self.slots[a] else (lambda idx: idx)
            if self.scatter[a]:
                mine, land = srcs[a].at[place(me)], outs[a].at[me]
            else:
                mine, land = srcs[a], outs[a].at[place(me)]
            local.append(pltpu.make_async_copy(mine, land, local_sems.at[a]))
            for k in range(1, N_DEV):
                dev, idx = _peer(k)
                common = dict(send_sem=send_sems.at[a * per + k - 1], recv_sem=recv_sems.at[a * per + k - 1],
                              device_id=dev, device_id_type=pl.DeviceIdType.MESH)
                if self.scatter[a]:
                    sends.append(pltpu.make_async_remote_copy(src_ref=srcs[a].at[place(idx)], dst_ref=land, **common))
                    arrivals.append(pltpu.make_async_remote_copy(src_ref=mine, dst_ref=outs[a].at[idx], **common))
                else:
                    sends.append(pltpu.make_async_remote_copy(src_ref=mine, dst_ref=land, **common))
                    arrivals.append(pltpu.make_async_remote_copy(src_ref=mine, dst_ref=outs[a].at[place(idx)], **common))
        return local, sends, arrivals

    def start(self, srcs, outs, sems):
        local, sends, _ = self._copies(srcs, outs, sems)
        for cp in local + sends:
            cp.start()

    def finish(self, srcs, outs, sems):
        local, sends, arrivals = self._copies(srcs, outs, sems)
        for cp in arrivals:
            cp.wait_recv()
        for cp in sends:
            cp.wait_send()
        for cp in local:
            cp.wait()


def _exchange(entries, name):
    ex = _Exchange(entries)
    n = ex.n

    def body(*refs):
        srcs, outs, sems = refs[:n], refs[n:2 * n], refs[2 * n:]
        ex.start(srcs, outs, sems)
        ex.finish(srcs, outs, sems)

    return pl.pallas_call(body, in_specs=ex.in_specs, out_specs=ex.out_specs, out_shape=ex.out_shape,
                          scratch_shapes=ex.scratch, name=name)(*ex.arrays)


def _local_step(x, p, tgt, w_in, small, comm):
    t = x.shape[0]
    tile = _pick(t, (256, 128))
    d = D_MODEL
    act_b, act_f = (d, BF16), (d, F32)
    x_b, p_b = x.astype(BF16), p.astype(BF16)

    chunk_id = jnp.arange(SGU_BLOCK) // CHUNK
    mask = chunk_id[:, None] >= chunk_id[None, :]
    wm = jnp.where(mask[None], small["sgu_w_s"], 0.0)
    wm_b = wm.astype(BF16)
    wm_t = jnp.swapaxes(wm, 1, 2).astype(BF16)
    bs_t = small["sgu_b_s"].T

    h = _mm(x_b, w_in, out_dtype=F32, name="mm_h")
    y_a = _sgu_fwd(h, wm_b, bs_t, small["sgu_norm_g"], small["sgu_norm_b"])
    y_b, o_all, states, got = _hgrn_fwd(h, small["lb_logits"], small["hgrn_norm_g"], ex=comm.weights_exchange())
    wts, conv_w = comm.weights(got)
    z_a = _mm(y_a, wts["w_a"], out_dtype=F32, name="mm_za")
    z_b = _mm(y_b, wts["w_b"], out_dtype=F32, name="mm_zb")
    gates = [(h, SLOT_GA), (h, SLOT_GB)]
    merged, = _rowwise(_merge_fwd, gates + [z_a, z_b], [], [act_b], [], tile=tile, name="merge_fwd")
    r1 = _mm(merged, wts["w_o"], out_dtype=F32, name="mm_r1")
    x1, x1_b = _rowwise(_ln1_fwd, [x, r1], [small["ln1_g"], small["ln1_b"]], [act_f, act_b], [], tile=tile, name="ln1_fwd")
    gate = _mm(x1_b, wts["w_g"], out_dtype=F32, name="mm_gate")
    val = _mm(x1_b, wts["w_v"], out_dtype=F32, name="mm_val")
    pg = _mm(x1_b, wts["w_pg"], out_dtype=F32, name="mm_pg")
    pp = _mm(p_b, wts["w_pp"], out_dtype=F32, name="mm_pp")
    hid, = _rowwise(functools.partial(_conv_fwd, tile), [gate, (gate, _halo_spec(D_FF, tile, t, False)), val],
                    [conv_w, small["conv_b"]], [(D_FF, BF16)], [], tile=tile, name="conv_fwd")
    fo = _mm(hid, wts["w_down"], out_dtype=F32, name="mm_down")
    dz2, dz2_b, dpg, dpp, loss, dg2, db2 = _rowwise(
        _ln2_loss, [x1, fo, pg, pp, tgt], [small["ln2_g"], small["ln2_b"]],
        [act_f, act_b, act_b, act_b], [(8, LANE), (1, d), (1, d)], tile=tile, name="ln2_loss")

    dhid = _mm(dz2_b, wts["w_down"], out_dtype=F32, name="mm_dhid", trans_b=True)
    g_down = _mm_tn(hid, dz2_b, out_dtype=BF16, name="mm_g_down")
    dcg, dval, dcw0, dcw1, dcw2, dcb = _rowwise(
        functools.partial(_conv_bwd_a, tile), [gate, (gate, _halo_spec(D_FF, tile, t, False)), val, dhid],
        [conv_w, small["conv_b"]], [(D_FF, F32), (D_FF, BF16)], [(1, D_FF)] * 4, tile=tile, name="conv_bwd_a")
    dgate, = _rowwise(functools.partial(_conv_bwd_b, tile), [dcg, (dcg, _halo_spec(D_FF, tile, t, True))],
                      [conv_w], [(D_FF, BF16)], [], tile=tile, name="conv_bwd_b")
    g_g = _mm_tn(x1_b, dgate, out_dtype=BF16, name="mm_g_gate")
    g_v = _mm_tn(x1_b, dval, out_dtype=BF16, name="mm_g_val")
    g_pg = _mm_tn(x1_b, dpg, out_dtype=BF16, name="mm_g_pg")
    g_pp = _mm_tn(p_b, dpp, out_dtype=BF16, name="mm_g_pp")
    dx1 = _mm(dgate, wts["w_g"], out_dtype=F32, name="mm_dx1_gate", trans_b=True, adds=[(dz2, ALPHA)])
    dx1 = _mm(dval, wts["w_v"], out_dtype=F32, name="mm_dx1_val", trans_b=True, adds=[(dx1, 1.0)])
    dx1 = _mm(dpg, wts["w_pg"], out_dtype=F32, name="mm_dx1_pg", trans_b=True, adds=[(dx1, 1.0)])
    dz1, dz1_b, dg1, db1 = _rowwise(_ln1_bwd, [x, r1, dx1], [small["ln1_g"]], [act_f, act_b], [(1, d), (1, d)],
                                    tile=tile, name="ln1_bwd")
    g_o = _mm_tn(merged, dz1_b, out_dtype=BF16, name="mm_g_o")
    dm = _mm(dz1_b, wts["w_o"], out_dtype=F32, name="mm_dm", trans_b=True)
    dh, dza, dzb = _rowwise(_merge_bwd, gates + [z_a, z_b, dm], [],
                            [("stack", 2, SLOT_GA // 2, 8, d, BF16), act_b, act_b], [], tile=tile, name="merge_bwd")
    g_a = _mm_tn(y_a, dza, out_dtype=BF16, name="mm_g_a")
    g_b = _mm_tn(y_b, dzb, out_dtype=BF16, name="mm_g_b")
    dy_a = _mm(dza, wts["w_a"], out_dtype=F32, name="mm_dya", trans_b=True)
    dy_b = _mm(dzb, wts["w_b"], out_dtype=F32, name="mm_dyb", trans_b=True)
    dh, dws, dbs, dgv_n, dbv_n = _sgu_bwd(h, dy_a, dh, wm_b, wm_t, bs_t, small["sgu_norm_g"], small["sgu_norm_b"])
    big = dict(w_a=g_a, w_b=g_b, w_o=g_o, w_g=g_g, w_v=g_v, w_down=g_down, w_pp=g_pp, w_pg=g_pg)
    sm = dict(sgu_w_s=jnp.where(mask[None], dws, 0.0), sgu_b_s=dbs[:, :GROUPS].T, sgu_norm_g=dgv_n, sgu_norm_b=dbv_n,
              ln1_g=dg1, ln1_b=db1, conv_w=jnp.concatenate([dcw0, dcw1, dcw2], axis=0), conv_b=dcb, ln2_g=dg2, ln2_b=db2,
              loss=loss)
    dh, dlogits, dgn, got = _hgrn_bwd(h, o_all, dy_b, states, dh, small["lb_logits"], small["hgrn_norm_g"],
                                      ex=comm.grads_exchange(big, sm))
    comm.grads_done(got)
    g_in = _mm_tn(x_b, dh, out_dtype=BF16, name="mm_g_in")
    ex = comm.last_exchange(g_in, dict(lb_logits=dlogits, hgrn_norm_g=dgn))
    res = _mm(dh, w_in, out_dtype=F32, name="mm_dx", trans_b=True, reduce_b=True, adds=[(dz1, ALPHA)], ex=ex)
    grad_x, got = res if ex else (res, ())
    comm.last_done(got)
    return grad_x


_SMALL_EARLY = ["sgu_w_s", "sgu_b_s", "sgu_norm_g", "sgu_norm_b", "ln1_g", "ln1_b", "ffn_conv_b", "ln2_g", "ln2_b"]
_SMALL_LATE = ["hgrn_lb_logits", "hgrn_norm_g"]
N_TAPS = D_FF // N_DEV
UP_COLS = 2 * D_FF // N_DEV


class _StepExchanges:
    def __init__(self, shards):
        self.shards = shards

    def weights_exchange(self):
        return _Exchange([(s, "gather") for s in self.shards])

    def weights(self, got):
        d, f = D_MODEL, D_FF
        w_br_g, w_o_g, w_up_g, w_down_g, w_pp_g, w_pg_g, conv_g = got
        w_br = w_br_g.transpose(1, 0, 2, 3).reshape(2, d, d)
        w_up = w_up_g.transpose(1, 0, 2).reshape(d, 2, f).transpose(1, 0, 2)
        wts = dict(w_a=w_br[0], w_b=w_br[1], w_o=w_o_g.reshape(d, d), w_g=w_up[0], w_v=w_up[1],
                   w_down=w_down_g.reshape(f, d), w_pp=w_pp_g.transpose(1, 0, 2).reshape(256, d), w_pg=w_pg_g.reshape(d, d))
        return wts, conv_g.transpose(1, 0, 2).reshape(3, f)

    def grads_exchange(self, big, sm):
        d = D_MODEL
        parts = [jnp.stack([big["w_a"], big["w_b"]]).reshape(2, N_DEV, 128, d).transpose(1, 0, 2, 3),
                 big["w_o"].reshape(N_DEV, 128, d),
                 jnp.concatenate([big["w_g"], big["w_v"]], axis=1).reshape(d, N_DEV, UP_COLS).transpose(1, 0, 2),
                 big["w_down"].reshape(N_DEV, N_TAPS, d),
                 big["w_pp"].reshape(256, N_DEV, 128).transpose(1, 0, 2),
                 big["w_pg"].reshape(N_DEV, 128, d)]
        packed, self.rows_early = _pack([sm[k] for k in ("sgu_w_s", "sgu_b_s", "sgu_norm_g", "sgu_norm_b", "ln1_g", "ln1_b",
                                                         "conv_b", "ln2_g", "ln2_b", "conv_w", "loss")])
        return _Exchange([(a, "scatter") for a in parts] + [(packed, "gather")])

    def grads_done(self, got):
        self.recv, self.small_early = got[:6], got[6]

    def last_exchange(self, g_in, sm):
        packed, self.rows_late = _pack([sm["lb_logits"], sm["hgrn_norm_g"]])
        return _Exchange([(g_in, "scatter_slots"), (packed, "gather")])

    def last_done(self, got):
        self.recv_in, self.small_late = got


def _rows128(a):
    flat = a.reshape(-1)
    rows = -(-flat.shape[0] // (8 * LANE)) * 8
    return jnp.pad(flat, (0, rows * LANE - flat.shape[0])).reshape(rows, LANE)


def _pack(parts):
    blocks = [_rows128(a) for a in parts]
    return jnp.concatenate(blocks, axis=0), [b.shape[0] for b in blocks]


def _unpack(packed, shapes, rows):
    out, r0 = [], 0
    for shp, r in zip(shapes, rows):
        n = math.prod(shp)
        out.append(packed[r0:r0 + r].reshape(-1)[:n].reshape(shp))
        r0 += r
    return out


def kernel(x, p, w_in, sgu_w_s, sgu_b_s, sgu_norm_g, sgu_norm_b, hgrn_lb_logits, hgrn_norm_g, w_branch, w_out, ln1_g, ln1_b, ffn_w_up, ffn_conv_w, ffn_conv_b, ffn_w_down, ln2_g, ln2_b, ple_w_proj, ple_w_gate, loss_target, m_w_in, m_sgu_w_s, m_sgu_b_s, m_sgu_norm_g, m_sgu_norm_b, m_hgrn_lb_logits, m_hgrn_norm_g, m_w_branch, m_w_out, m_ln1_g, m_ln1_b, m_ffn_w_up, m_ffn_conv_w, m_ffn_conv_b, m_ffn_w_down, m_ln2_g, m_ln2_b, m_ple_w_proj, m_ple_w_gate, v_w_in, v_sgu_w_s, v_sgu_b_s, v_sgu_norm_g, v_sgu_norm_b, v_hgrn_lb_logits, v_hgrn_norm_g, v_w_branch, v_w_out, v_ln1_g, v_ln1_b, v_ffn_w_up, v_ffn_conv_w, v_ffn_conv_b, v_ffn_w_down, v_ln2_g, v_ln2_b, v_ple_w_proj, v_ple_w_gate):
    weights = dict(w_in=w_in, sgu_w_s=sgu_w_s, sgu_b_s=sgu_b_s, sgu_norm_g=sgu_norm_g, sgu_norm_b=sgu_norm_b,
                   hgrn_lb_logits=hgrn_lb_logits, hgrn_norm_g=hgrn_norm_g, w_branch=w_branch, w_out=w_out,
                   ln1_g=ln1_g, ln1_b=ln1_b, ffn_w_up=ffn_w_up, ffn_conv_w=ffn_conv_w, ffn_conv_b=ffn_conv_b,
                   ffn_w_down=ffn_w_down, ln2_g=ln2_g, ln2_b=ln2_b, ple_w_proj=ple_w_proj, ple_w_gate=ple_w_gate)
    mom_m = dict(w_in=m_w_in, sgu_w_s=m_sgu_w_s, sgu_b_s=m_sgu_b_s, sgu_norm_g=m_sgu_norm_g, sgu_norm_b=m_sgu_norm_b,
                 hgrn_lb_logits=m_hgrn_lb_logits, hgrn_norm_g=m_hgrn_norm_g, w_branch=m_w_branch, w_out=m_w_out,
                 ln1_g=m_ln1_g, ln1_b=m_ln1_b, ffn_w_up=m_ffn_w_up, ffn_conv_w=m_ffn_conv_w, ffn_conv_b=m_ffn_conv_b,
                 ffn_w_down=m_ffn_w_down, ln2_g=m_ln2_g, ln2_b=m_ln2_b, ple_w_proj=m_ple_w_proj, ple_w_gate=m_ple_w_gate)
    mom_v = dict(w_in=v_w_in, sgu_w_s=v_sgu_w_s, sgu_b_s=v_sgu_b_s, sgu_norm_g=v_sgu_norm_g, sgu_norm_b=v_sgu_norm_b,
                 hgrn_lb_logits=v_hgrn_lb_logits, hgrn_norm_g=v_hgrn_norm_g, w_branch=v_w_branch, w_out=v_w_out,
                 ln1_g=v_ln1_g, ln1_b=v_ln1_b, ffn_w_up=v_ffn_w_up, ffn_conv_w=v_ffn_conv_w, ffn_conv_b=v_ffn_conv_b,
                 ffn_w_down=v_ffn_w_down, ln2_g=v_ln2_g, ln2_b=v_ln2_b, ple_w_proj=v_ple_w_proj, ple_w_gate=v_ple_w_gate)
    d, f = D_MODEL, D_FF
    me = _my_index()

    w_in_g, = _exchange([(w_in[0].astype(BF16), "gather_slots")], "gather_w_in")
    comm = _StepExchanges([w_branch[0].astype(BF16), w_out[0].astype(BF16), ffn_w_up[0].astype(BF16),
                           ffn_w_down[0].astype(BF16), ple_w_proj[0].astype(BF16), ple_w_gate[0].astype(BF16), ffn_conv_w[0]])
    small = dict(sgu_w_s=sgu_w_s[0], sgu_b_s=sgu_b_s[0], sgu_norm_g=sgu_norm_g, sgu_norm_b=sgu_norm_b,
                 lb_logits=hgrn_lb_logits, hgrn_norm_g=hgrn_norm_g, ln1_g=ln1_g, ln1_b=ln1_b, ln2_g=ln2_g, ln2_b=ln2_b,
                 conv_b=ffn_conv_b)
    grad_x = _local_step(x[0], p[0, 0], loss_target[0], w_in_g, small, comm)

    out = {}

    def adam(name, parts8, shape2d):
        w2, m2, v2 = (a.reshape(shape2d) for a in (weights[name], mom_m[name], mom_v[name]))
        res = _adam_call(w2, m2, v2, parts8.reshape((N_DEV,) + shape2d), "adam_" + name)
        out[name] = tuple(r.reshape(weights[name].shape) for r in res)

    adam("w_in", comm.recv_in, (d, d))
    adam("w_branch", comm.recv[0], (256, d))
    adam("w_out", comm.recv[1], (128, d))
    adam("ffn_w_up", comm.recv[2], (d, UP_COLS))
    adam("ffn_w_down", comm.recv[3], (N_TAPS, d))
    adam("ple_w_proj", comm.recv[4], (256, 128))
    adam("ple_w_gate", comm.recv[5], (128, d))

    def adam_small(names, extra_w, extra_m, extra_v, extra_shapes, parts8, rows, label):
        pk = lambda src, extra: _pack([src[n] for n in names] + extra)[0]
        res = _adam_call(pk(weights, extra_w), pk(mom_m, extra_m), pk(mom_v, extra_v), parts8, label)
        shapes = [weights[n].shape for n in names] + extra_shapes
        unpacked = [_unpack(r, shapes, rows) for r in res]
        for j, n in enumerate(names):
            out[n] = tuple(u[j] for u in unpacked)
        return [[u[len(names) + j] for u in unpacked] for j in range(len(extra_shapes))]

    blank = jnp.zeros((8, LANE), F32)
    taps, loss_rows = adam_small(
        _SMALL_EARLY, [_place_taps(ffn_conv_w[0], me, f), blank], [_place_taps(m_ffn_conv_w[0], me, f), blank],
        [_place_taps(v_ffn_conv_w[0], me, f), blank + 1.0], [(3, f), (8, LANE)], comm.small_early, comm.rows_early,
        "adam_small_early")
    adam_small(_SMALL_LATE, [], [], [], [], comm.small_late, comm.rows_late, "adam_small_late")
    out["ffn_conv_w"] = tuple(lax.dynamic_slice_in_dim(u, me * N_TAPS, N_TAPS, axis=1)[None] for u in taps)
    loss = loss_rows[0][0, 0]

    order = ["w_in", "sgu_w_s", "sgu_b_s", "sgu_norm_g", "sgu_norm_b", "hgrn_lb_logits", "hgrn_norm_g", "w_branch", "w_out",
             "ln1_g", "ln1_b", "ffn_w_up", "ffn_conv_w", "ffn_conv_b", "ffn_w_down", "ln2_g", "ln2_b", "ple_w_proj", "ple_w_gate"]
    return (loss, grad_x[None], *[out[n][0] for n in order], *[out[n][1] for n in order],
            *[out[n][2] for n in order], *[out[n][3] for n in order])


def _place_taps(shard, me, f):
    return lax.dynamic_update_slice_in_dim(jnp.zeros((3, f), F32), shard, me * N_TAPS, axis=1)
```

```python
import functools
import math

import jax
import jax.numpy as jnp
from jax import lax
from jax.experimental import pallas as pl
from jax.experimental.pallas import tpu as pltpu

F32 = jnp.float32
BF16 = jnp.bfloat16

N_DEV = 8
D_MODEL = 1024
CHUNK = 64
SUB = 16
SGU_BLOCK = 128
GROUPS = 8
HEAD = 128
HEADS = 8
HEAD_UNROLL = 4
D_FF = 2816
LN_EPS = 1e-5
RMS_EPS = 1e-6
ALPHA = 2.0 ** 0.25
GELU_K = math.sqrt(2.0 / math.pi)
GELU_C = 0.044715
NEG = -1e30
ADAM_LR, ADAM_B1, ADAM_B2, ADAM_EPS, ADAM_WD, ADAM_STEP = 0.001, 0.9, 0.999, 1e-08, 0.01, 10
LANE = 128
SLOT_Q, SLOT_F, SLOT_I, SLOT_OG, SLOT_U, SLOT_V, SLOT_GA, SLOT_GB = range(8)


def _slot_of_group(k):
    return jnp.where(k < 2, k + 4, jnp.where(k < 6, k - 2, k))


MIB = 1024 * 1024
VMEM_V7X = 64 * MIB
VMEM_FLOOR = 32 * MIB
MM_TILES = (1024, 1408, 512, 256, 128)


def _params(sem, need=0):
    limit = min(max(need + need // 4, VMEM_FLOOR), VMEM_V7X - 4 * MIB)
    return pltpu.CompilerParams(dimension_semantics=sem, vmem_limit_bytes=limit)


def _pick(n, prefs):
    for t in prefs:
        if n % t == 0:
            return t
    return n


def _gelu(x):
    return 0.5 * x * (1.0 + jnp.tanh(GELU_K * (x + GELU_C * x * x * x)))


def _gelu_grad(x):
    t = jnp.tanh(GELU_K * (x + GELU_C * x * x * x))
    return 0.5 * (1.0 + t) + 0.5 * x * (1.0 - t * t) * GELU_K * (1.0 + 3.0 * GELU_C * x * x)


def _silu_grad(x, s):
    return s * (1.0 + x * (1.0 - s))


def _dot(a, b):
    return jnp.dot(a.astype(BF16), b.astype(BF16), preferred_element_type=F32)


def _dot_nt(a, b):
    return lax.dot_general(a.astype(BF16), b.astype(BF16), (((1,), (1,)), ((), ())), preferred_element_type=F32)


def _dot_tn(a, b):
    return lax.dot_general(a.astype(BF16), b.astype(BF16), (((0,), (0,)), ((), ())), preferred_element_type=F32)


def _mean(x):
    return jnp.mean(x, axis=-1, keepdims=True)


def _sum0(x):
    return jnp.sum(x, axis=0, keepdims=True)


def _mm(a, b, *, out_dtype, name, trans_b=False, reduce_b=False, adds=(), ex=None):
    squeeze = b.ndim == 2
    a3 = a if a.ndim == 3 else a[None]
    b3 = b if b.ndim == 3 else b[None]
    ba, m, k = a3.shape
    bb = b3.shape[0]
    n = b3.shape[1] if trans_b else b3.shape[2]
    tm = _pick(m, MM_TILES)
    tn = _pick(n, MM_TILES)
    if reduce_b:
        bo, steps = 1, bb
        a_map = lambda o, i, j, r: (r if ba > 1 else 0, i, 0)
        b_map = (lambda o, i, j, r: (r, j, 0)) if trans_b else (lambda o, i, j, r: (r, 0, j))
    else:
        bo, steps = bb, 1
        a_map = lambda o, i, j, r: (o if ba > 1 else 0, i, 0)
        b_map = (lambda o, i, j, r: (o, j, 0)) if trans_b else (lambda o, i, j, r: (o, 0, j))
    o_map = lambda o, i, j, r: (o, i, j)
    add_arrays = [x if x.ndim == 3 else x[None] for x, _ in adds]
    add_scales = [s for _, s in adds]
    n_add = len(adds)
    dot = _dot_nt if trans_b else _dot

    def finish(acc, add_refs, o_ref):
        for ref, s in zip(add_refs, add_scales):
            acc = acc + s * ref[...].astype(F32)
        o_ref[...] = acc.astype(o_ref.dtype)

    grid = (bo, m // tm, n // tn, steps)

    def body(*refs):
        ins, (o_ref,), scratch, xrefs = _split_refs(refs, 2 + n_add, 1, 1 if reduce_b else 0, ex)
        a_ref, b_ref, add_refs = ins[0], ins[1], ins[2:]
        step = ((pl.program_id(0) * grid[1] + pl.program_id(1)) * grid[2] + pl.program_id(2)) * grid[3] + pl.program_id(3)
        if ex:
            @pl.when(step == 0)
            def _():
                ex.start(*xrefs)

        if reduce_b:
            acc, = scratch
            r = pl.program_id(3)

            @pl.when(r == 0)
            def _():
                acc[...] = jnp.zeros_like(acc)

            acc[...] += dot(a_ref[...], b_ref[...])

            @pl.when(r == steps - 1)
            def _():
                finish(acc[...], add_refs, o_ref)
        else:
            finish(dot(a_ref[...], b_ref[...]), add_refs, o_ref)

        if ex:
            @pl.when(step == math.prod(grid) - 1)
            def _():
                ex.finish(*xrefs)

    b_block = (None, tn, k) if trans_b else (None, k, tn)
    out_bytes = tm * tn * jnp.dtype(out_dtype).itemsize
    need = 2 * (tm * k * a3.dtype.itemsize + k * tn * b3.dtype.itemsize + out_bytes + n_add * tm * tn * 4)
    need += 2 * tm * tn * 4
    sem = ("arbitrary",) * 4 if ex else ("parallel", "parallel", "parallel", "arbitrary")
    res = pl.pallas_call(
        body,
        grid=grid,
        in_specs=[pl.BlockSpec((None, tm, k), a_map), pl.BlockSpec(b_block, b_map)]
        + [pl.BlockSpec((None, tm, tn), o_map) for _ in adds] + (ex.in_specs if ex else []),
        out_specs=[pl.BlockSpec((None, tm, tn), o_map)] + (ex.out_specs if ex else []),
        out_shape=[jax.ShapeDtypeStruct((bo, m, n), out_dtype)] + (ex.out_shape if ex else []),
        scratch_shapes=([pltpu.VMEM((tm, tn), F32)] if reduce_b else []) + (ex.scratch if ex else []),
        compiler_params=_params(sem, need),
        name=name,
    )(a3, b3, *add_arrays, *(ex.arrays if ex else []))
    out = res[0][0] if (reduce_b or squeeze) else res[0]
    return (out, res[1:]) if ex else out


def _mm_tn(a, b, *, out_dtype, name):
    squeeze = b.ndim == 2
    b3 = b if b.ndim == 3 else b[None]
    t, m = a.shape
    bb, _, n = b3.shape
    tm = _pick(m, MM_TILES)
    tn = _pick(n, MM_TILES)
    tt = _pick(t, (1024, 512, 256, 128))
    steps = t // tt
    need = 2 * (tt * tm * a.dtype.itemsize + tt * tn * b3.dtype.itemsize + tm * tn * jnp.dtype(out_dtype).itemsize)
    need += 2 * tm * tn * 4

    def body(a_ref, b_ref, o_ref, acc):
        r = pl.program_id(3)

        @pl.when(r == 0)
        def _():
            acc[...] = jnp.zeros_like(acc)

        acc[...] += _dot_tn(a_ref[...], b_ref[...])

        @pl.when(r == steps - 1)
        def _():
            o_ref[...] = acc[...].astype(o_ref.dtype)

    out = pl.pallas_call(
        body,
        grid=(bb, m // tm, n // tn, steps),
        in_specs=[pl.BlockSpec((tt, tm), lambda o, i, j, r: (r, i)),
                  pl.BlockSpec((None, tt, tn), lambda o, i, j, r: (o, r, j))],
        out_specs=pl.BlockSpec((None, tm, tn), lambda o, i, j, r: (o, i, j)),
        out_shape=jax.ShapeDtypeStruct((bb, m, n), out_dtype),
        scratch_shapes=[pltpu.VMEM((tm, tn), F32)],
        compiler_params=_params(("parallel", "parallel", "parallel", "arbitrary"), need),
        name=name,
    )(a, b3)
    return out[0] if squeeze else out


def _rowwise(fn, rows, consts, row_outs, acc_outs, *, tile, name):
    first = rows[0][0] if isinstance(rows[0], tuple) else rows[0]
    t = first.shape[-2]
    steps = t // tile
    arrays, in_specs = [], []
    for r in rows:
        if isinstance(r, tuple) and isinstance(r[1], pl.BlockSpec):
            arrays.append(r[0])
            in_specs.append(r[1])
        elif isinstance(r, tuple):
            arr, bidx = r
            arrays.append(arr)
            in_specs.append(pl.BlockSpec((None, tile, arr.shape[-1]), functools.partial(lambda i, b: (b, i, 0), b=bidx)))
        else:
            arrays.append(r)
            in_specs.append(pl.BlockSpec((tile, r.shape[-1]), lambda i: (i, 0)))
    for c in consts:
        arrays.append(c)
        in_specs.append(pl.BlockSpec(c.shape, lambda i: (0, 0)))
    n_in, n_row = len(arrays), len(row_outs)
    out_shape, out_specs = [], []
    for ro in row_outs:
        if ro[0] == "stack":
            _, cnt, blk, total, w, dt = ro
            out_shape.append(jax.ShapeDtypeStruct((total, t, w), dt))
            out_specs.append(pl.BlockSpec((cnt, tile, w), functools.partial(lambda i, b: (b, i, 0), b=blk)))
        else:
            w, dt = ro
            out_shape.append(jax.ShapeDtypeStruct((t, w), dt))
            out_specs.append(pl.BlockSpec((tile, w), lambda i: (i, 0)))
    out_shape += [jax.ShapeDtypeStruct(s, F32) for s in acc_outs]
    out_specs += [pl.BlockSpec(s, lambda i: (0, 0)) for s in acc_outs]
    blocks = [math.prod(d for d in sp.block_shape if d) * arr.dtype.itemsize for sp, arr in zip(in_specs, arrays)]
    blocks += [math.prod(d for d in sp.block_shape if d) * sh.dtype.itemsize for sp, sh in zip(out_specs, out_shape)]
    need = 2 * sum(blocks) + 6 * tile * max(a.shape[-1] for a in arrays) * 4

    def body(*refs):
        ins, outs = refs[:n_in], refs[n_in:]
        i = pl.program_id(0)
        res = fn(i, steps, *[r[...] for r in ins])
        res = res if isinstance(res, (tuple, list)) else (res,)
        for ref, val in zip(outs[:n_row], res[:n_row]):
            ref[...] = val.astype(ref.dtype)
        if acc_outs:
            @pl.when(i == 0)
            def _():
                for ref in outs[n_row:]:
                    ref[...] = jnp.zeros_like(ref)

            for ref, val in zip(outs[n_row:], res[n_row:]):
                ref[...] += val

    return pl.pallas_call(
        body,
        grid=(steps,),
        in_specs=in_specs,
        out_specs=out_specs,
        out_shape=out_shape,
        compiler_params=_params(("arbitrary",), need),
        name=name,
    )(*arrays)


def _ln_stats(z):
    mu = _mean(z)
    zc = z - mu
    rstd = lax.rsqrt(_mean(zc * zc) + LN_EPS)
    return zc * rstd, rstd


def _ln_bwd(dy, xhat, rstd, g):
    dxh = dy * g
    return rstd * (dxh - _mean(dxh) - xhat * _mean(dxh * xhat))


def _sgu_fwd(h, wm, bs_t, g_v, b_v):
    t = h.shape[1]

    def body(u_ref, v_ref, wm_ref, bs_ref, g_ref, b_ref, y_ref):
        xhat, _ = _ln_stats(_gelu(v_ref[...]))
        vn = (xhat * g_ref[...] + b_ref[...]).astype(BF16)
        gu = _gelu(u_ref[...])
        for g in range(GROUPS):
            sl = slice(g * HEAD, (g + 1) * HEAD)
            mixed = _dot(wm_ref[g], vn[:, sl]) + bs_ref[:, g:g + 1]
            y_ref[:, sl] = (gu[:, sl] * mixed).astype(BF16)

    blk = lambda b: pl.BlockSpec((None, SGU_BLOCK, D_MODEL), functools.partial(lambda i, b: (b, i, 0), b=b))
    whole = lambda s: pl.BlockSpec(s, lambda i: (0,) * len(s))
    return pl.pallas_call(
        body,
        grid=(t // SGU_BLOCK,),
        in_specs=[blk(SLOT_U), blk(SLOT_V), whole(wm.shape), whole(bs_t.shape), whole(g_v.shape), whole(b_v.shape)],
        out_specs=pl.BlockSpec((SGU_BLOCK, D_MODEL), lambda i: (i, 0)),
        out_shape=jax.ShapeDtypeStruct((t, D_MODEL), BF16),
        compiler_params=_params(("parallel",)),
        name="sgu_fwd",
    )(h, h, wm, bs_t, g_v, b_v)


def _sgu_bwd(h, dy, dh, wm, wm_t, bs_t, g_v, b_v):
    t = h.shape[1]

    def body(u_ref, v_ref, dy_ref, dh_in, wm_ref, wmt_ref, bs_ref, g_ref, b_ref,
             duv_ref, dw_ref, dbs_ref, dg_ref, db_ref, dvn_ref):
        del dh_in
        du_ref, dv_ref = duv_ref.at[0], duv_ref.at[1]
        i = pl.program_id(0)

        @pl.when(i == 0)
        def _():
            dw_ref[...] = jnp.zeros_like(dw_ref)
            dbs_ref[...] = jnp.zeros_like(dbs_ref)
            dg_ref[...] = jnp.zeros_like(dg_ref)
            db_ref[...] = jnp.zeros_like(db_ref)

        u = u_ref[...]
        v = v_ref[...]
        xhat, rstd = _ln_stats(_gelu(v))
        vn = (xhat * g_ref[...] + b_ref[...]).astype(BF16)
        gu = _gelu(u)
        gup = _gelu_grad(u)
        lane = lax.broadcasted_iota(jnp.int32, (SGU_BLOCK, LANE), 1)
        dbs = jnp.zeros((SGU_BLOCK, LANE), F32)
        for g in range(GROUPS):
            sl = slice(g * HEAD, (g + 1) * HEAD)
            vn_g = vn[:, sl]
            mixed = _dot(wm_ref[g], vn_g) + bs_ref[:, g:g + 1]
            dy_g = dy_ref[:, sl]
            du_ref[:, sl] = (dy_g * mixed * gup[:, sl]).astype(BF16)
            dmix = dy_g * gu[:, sl]
            dmb = dmix.astype(BF16)
            dvn_ref[:, sl] = _dot(wmt_ref[g], dmb)
            dw_ref[g] += _dot_nt(dmb, vn_g)
            dbs = dbs + jnp.where(lane == g, jnp.sum(dmix, axis=1, keepdims=True), 0.0)
        dbs_ref[...] += dbs
        dvn = dvn_ref[...]
        dg_ref[...] += _sum0(dvn * xhat)
        db_ref[...] += _sum0(dvn)
        dv_ref[...] = (_ln_bwd(dvn, xhat, rstd, g_ref[...]) * _gelu_grad(v)).astype(BF16)

    blk = lambda b: pl.BlockSpec((None, SGU_BLOCK, D_MODEL), functools.partial(lambda i, b: (b, i, 0), b=b))
    row = pl.BlockSpec((SGU_BLOCK, D_MODEL), lambda i: (i, 0))
    whole = lambda s: pl.BlockSpec(s, lambda i: (0,) * len(s))
    vec = (1, D_MODEL)
    return pl.pallas_call(
        body,
        grid=(t // SGU_BLOCK,),
        in_specs=[blk(SLOT_U), blk(SLOT_V), row, pl.BlockSpec(memory_space=pl.ANY),
                  whole(wm.shape), whole(wm_t.shape), whole(bs_t.shape), whole(vec), whole(vec)],
        out_specs=[pl.BlockSpec((2, SGU_BLOCK, D_MODEL), lambda i: (SLOT_U // 2, i, 0)),
                   whole(wm.shape), whole((SGU_BLOCK, LANE)), whole(vec), whole(vec)],
        out_shape=[jax.ShapeDtypeStruct(dh.shape, BF16),
                   jax.ShapeDtypeStruct(wm.shape, F32), jax.ShapeDtypeStruct((SGU_BLOCK, LANE), F32),
                   jax.ShapeDtypeStruct(vec, F32), jax.ShapeDtypeStruct(vec, F32)],
        scratch_shapes=[pltpu.VMEM((SGU_BLOCK, D_MODEL), F32)],
        input_output_aliases={3: 0},
        compiler_params=_params(("arbitrary",)),
        name="sgu_bwd",
    )(h, h, dy, dh, wm, wm_t, bs_t, g_v, b_v)


def _split3(x):
    hi = x.astype(BF16)
    r1 = x - hi.astype(F32)
    mid = r1.astype(BF16)
    lo = (r1 - mid.astype(F32)).astype(BF16)
    return hi, mid, lo


def _tri_matmul(tri, x):
    hi, mid, lo = _split3(x)
    dot = lambda p: jnp.dot(tri, p, preferred_element_type=F32)
    return dot(hi) + dot(mid) + dot(lo)


def _lower_bound(logits):
    l0, l1 = logits[0:1, :], logits[1:2, :]
    mx = jnp.maximum(l0, l1)
    e0, e1 = jnp.exp(l0 - mx), jnp.exp(l1 - mx)
    return e0 / (e0 + e1)


def _hgrn_gates(q_raw, f_raw, lb):
    q = q_raw * jax.nn.sigmoid(q_raw)
    sig = jax.nn.sigmoid(f_raw)
    f = lb + (1.0 - lb) * sig
    row = lax.broadcasted_iota(jnp.int32, (CHUNK, CHUNK), 0)
    col = lax.broadcasted_iota(jnp.int32, (CHUNK, CHUNK), 1)
    c = _tri_matmul((row >= col).astype(BF16), jnp.log(f))
    return q, sig, f, 1.0 - f, c


def _offdiag_terms(qh, kh, ch, tb):
    rows = slice(tb * SUB, (tb + 1) * SUB)
    r = ch[tb * SUB - 1:tb * SUB, :]
    eqh = jnp.exp(ch[rows] - r)
    ekh = jnp.exp(jnp.minimum(r - ch, 0.0))
    return rows, eqh, qh[rows] * eqh, ekh, kh * ekh


def _diag_decay(cb, s, trow):
    return jnp.exp(jnp.where(trow >= s, cb - cb[s:s + 1, :], NEG))


def _split_refs(refs, n_in, n_out, n_scratch, ex):
    nx = ex.n if ex else 0
    ins, refs = refs[:n_in], refs[n_in:]
    xsrc, refs = refs[:nx], refs[nx:]
    outs, refs = refs[:n_out], refs[n_out:]
    xout, refs = refs[:nx], refs[nx:]
    return ins, outs, refs[:n_scratch], (xsrc, xout, refs[n_scratch:])


def _hgrn_fwd(h, logits, g_norm, ex=None):
    t = h.shape[1]
    nc = t // CHUNK

    def body(*refs):
        ins, outs, scratch, xrefs = _split_refs(refs, 6, 3, 4, ex)
        q_ref, f_ref, i_ref, og_ref, lg_ref, gn_ref = ins
        y_ref, o_ref, sall_ref = outs
        st_ref, q_s, k_s, c_s = scratch

        @pl.when(pl.program_id(0) == 0)
        def _():
            st_ref[...] = jnp.zeros_like(st_ref)
            if ex:
                ex.start(*xrefs)

        lb = _lower_bound(lg_ref[...])
        q, _, _, k, c = _hgrn_gates(q_ref[...], f_ref[...], lb)
        q_s[...] = q
        k_s[...] = k
        c_s[...] = c
        col64 = lax.broadcasted_iota(jnp.int32, (SUB, CHUNK), 1)
        trow = lax.broadcasted_iota(jnp.int32, (SUB, HEAD), 0)

        def head(hd, carry):
            sl = pl.ds(pl.multiple_of(hd * HEAD, HEAD), HEAD)
            qh, kh, ch, ih = q_s[:, sl], k_s[:, sl], c_s[:, sl], i_ref[:, sl]
            st = st_ref[hd]
            sall_ref[hd] = st
            c_last = ch[CHUNK - 1:CHUNK, :]
            o = _dot_nt(qh * jnp.exp(ch), st)
            st_ref[hd] = st * jnp.exp(c_last) + _dot_tn(ih, kh * jnp.exp(c_last - ch))
            a_rows = [jnp.zeros((SUB, CHUNK), F32)]
            for tb in range(1, CHUNK // SUB):
                _, _, q_hat, _, k_hat = _offdiag_terms(qh, kh, ch, tb)
                a_rows.append(jnp.where(col64 < tb * SUB, _dot_nt(q_hat, k_hat), 0.0))
            o = o + _dot(jnp.concatenate(a_rows, axis=0), ih)
            o_rows = []
            for b in range(CHUNK // SUB):
                rows = slice(b * SUB, (b + 1) * SUB)
                qb, cb, kb, ib = qh[rows], ch[rows], kh[rows], ih[rows]
                ob = jnp.zeros((SUB, HEAD), F32)
                for s in range(SUB):
                    a = jnp.sum(qb * _diag_decay(cb, s, trow) * kb[s:s + 1, :], axis=1, keepdims=True)
                    ob = ob + a * ib[s:s + 1, :]
                o_rows.append(ob)
            o = o + jnp.concatenate(o_rows, axis=0)
            o_ref[:, sl] = o
            og = og_ref[:, sl]
            on = o * lax.rsqrt(_mean(o * o) + RMS_EPS)
            y_ref[:, sl] = (on * gn_ref[:, sl] * (og * jax.nn.sigmoid(og))).astype(BF16)
            return carry

        lax.fori_loop(0, HEADS, head, 0, unroll=HEAD_UNROLL)

        if ex:
            @pl.when(pl.program_id(0) == nc - 1)
            def _():
                ex.finish(*xrefs)

    blk = lambda b: pl.BlockSpec((None, CHUNK, D_MODEL), functools.partial(lambda n, b: (b, n, 0), b=b))
    row = pl.BlockSpec((CHUNK, D_MODEL), lambda n: (n, 0))
    whole = lambda s: pl.BlockSpec(s, lambda n: (0,) * len(s))
    res = pl.pallas_call(
        body,
        grid=(nc,),
        in_specs=[blk(SLOT_Q), blk(SLOT_F), blk(SLOT_I), blk(SLOT_OG), whole(logits.shape), whole(g_norm.shape)]
        + (ex.in_specs if ex else []),
        out_specs=[row, row, pl.BlockSpec((None, HEADS, HEAD, HEAD), lambda n: (n, 0, 0, 0))] + (ex.out_specs if ex else []),
        out_shape=[jax.ShapeDtypeStruct((t, D_MODEL), BF16), jax.ShapeDtypeStruct((t, D_MODEL), F32),
                   jax.ShapeDtypeStruct((nc, HEADS, HEAD, HEAD), F32)] + (ex.out_shape if ex else []),
        scratch_shapes=[pltpu.VMEM((HEADS, HEAD, HEAD), F32)] + [pltpu.VMEM((CHUNK, D_MODEL), F32)] * 3
        + (ex.scratch if ex else []),
        compiler_params=_params(("arbitrary",)),
        name="hgrn_fwd",
    )(h, h, h, h, logits, g_norm, *(ex.arrays if ex else []))
    return res[0], res[1], res[2], res[3:]


def _hgrn_bwd(h, o_all, dy, states, dh, logits, g_norm, ex=None):
    t = h.shape[1]
    nc = t // CHUNK

    def body(*refs):
        ins, outs, scratch, xrefs = _split_refs(refs, 10, 3, 8, ex)
        q_ref, f_ref, i_ref, og_ref, o_ref, dy_ref, sall_ref, _, lg_ref, gn_ref = ins
        dqfio_ref, dlg_ref, dgn_ref = outs
        dst_ref, dlb_ref, q_s, k_s, c_s, dq_s, dk_s, dc_s = scratch
        dq_ref, df_ref, di_ref, dog_ref = (dqfio_ref.at[s] for s in (SLOT_Q, SLOT_F, SLOT_I, SLOT_OG))
        n = pl.program_id(0)

        @pl.when(n == 0)
        def _():
            dst_ref[...] = jnp.zeros_like(dst_ref)
            dlb_ref[...] = jnp.zeros_like(dlb_ref)
            dgn_ref[...] = jnp.zeros_like(dgn_ref)
            if ex:
                ex.start(*xrefs)

        lb = _lower_bound(lg_ref[...])
        q_raw = q_ref[...]
        q, sig, f, k, c = _hgrn_gates(q_raw, f_ref[...], lb)
        q_s[...] = q
        k_s[...] = k
        c_s[...] = c
        col64 = lax.broadcasted_iota(jnp.int32, (SUB, CHUNK), 1)
        trow = lax.broadcasted_iota(jnp.int32, (SUB, HEAD), 0)
        row64 = lax.broadcasted_iota(jnp.int32, (CHUNK, HEAD), 0)

        def head(hd, carry):
            sl = pl.ds(pl.multiple_of(hd * HEAD, HEAD), HEAD)
            qh, kh, ch, ih = q_s[:, sl], k_s[:, sl], c_s[:, sl], i_ref[:, sl]
            st = sall_ref[hd]
            dst = dst_ref[hd]
            oh, dyh, og, gn = o_ref[:, sl], dy_ref[:, sl], og_ref[:, sl], gn_ref[:, sl]
            sg = jax.nn.sigmoid(og)
            sil = og * sg
            rms = lax.rsqrt(_mean(oh * oh) + RMS_EPS)
            on = oh * rms
            dog_ref[:, sl] = (dyh * on * gn * _silu_grad(og, sg)).astype(BF16)
            dgn_ref[:, sl] += _sum0(dyh * on * sil)
            don = dyh * gn * sil
            do = rms * (don - on * _mean(don * on))
            dob = do.astype(BF16)

            c_last = ch[CHUNK - 1:CHUNK, :]
            eq = jnp.exp(ch)
            q_til = qh * eq
            ekl = jnp.exp(c_last - ch)
            k_til = kh * ekl
            ecl = jnp.exp(c_last)
            dq_til = _dot(dob, st)
            dk_til = _dot(ih, dst)
            di = _dot_nt(k_til, dst)
            dc_last = _sum0(dk_til * k_til) + _sum0(dst * st) * ecl
            dst_ref[hd] = _dot_tn(dob, q_til) + dst * ecl
            dq = dq_til * eq
            dc = dq_til * q_til - dk_til * k_til
            dk = dk_til * ekl

            da_full = _dot_nt(dob, ih)
            a_rows = [jnp.zeros((SUB, CHUNK), F32)]
            dq_rows = [jnp.zeros((SUB, HEAD), F32)]
            dc_rows = [jnp.zeros((SUB, HEAD), F32)]
            for tb in range(1, CHUNK // SUB):
                rows, eqh, q_hat, ekh, k_hat = _offdiag_terms(qh, kh, ch, tb)
                keep = col64 < tb * SUB
                a_rows.append(jnp.where(keep, _dot_nt(q_hat, k_hat), 0.0))
                da = jnp.where(keep, da_full[rows], 0.0)
                dq_hat = _dot(da, k_hat)
                dk_hat = _dot_tn(da, q_hat)
                dq_rows.append(dq_hat * eqh)
                dc_rows.append(dq_hat * q_hat)
                dk = dk + dk_hat * ekh
                dc = dc - dk_hat * k_hat
            di = di + _dot_tn(jnp.concatenate(a_rows, axis=0), dob)

            dk_rows, di_rows = [], []
            for b in range(CHUNK // SUB):
                rows = slice(b * SUB, (b + 1) * SUB)
                qb, cb, kb, ib, dob_ = qh[rows], ch[rows], kh[rows], ih[rows], do[rows]
                dq_diag = jnp.zeros((SUB, HEAD), F32)
                dk_diag = jnp.zeros((SUB, HEAD), F32)
                di_diag = jnp.zeros((SUB, HEAD), F32)
                for s in range(SUB):
                    ks = kb[s:s + 1, :]
                    dec = _diag_decay(cb, s, trow)
                    a = jnp.sum(qb * dec * ks, axis=1, keepdims=True)
                    gk = jnp.sum(dob_ * ib[s:s + 1, :], axis=1, keepdims=True) * dec
                    dq_diag = dq_diag + gk * ks
                    dk_diag = dk_diag + jnp.where(trow == s, _sum0(gk * qb), 0.0)
                    di_diag = di_diag + jnp.where(trow == s, _sum0(a * dob_), 0.0)
                dq_rows[b] = dq_rows[b] + dq_diag
                dc_rows[b] = dc_rows[b] + qb * dq_diag - kb * dk_diag
                dk_rows.append(dk_diag)
                di_rows.append(di_diag)
            dq = dq + jnp.concatenate(dq_rows, axis=0)
            dk = dk + jnp.concatenate(dk_rows, axis=0)
            dc = dc + jnp.concatenate(dc_rows, axis=0) + jnp.where(row64 == CHUNK - 1, dc_last, 0.0)
            di_ref[:, sl] = (di + jnp.concatenate(di_rows, axis=0)).astype(BF16)
            dq_s[:, sl] = dq
            dk_s[:, sl] = dk
            dc_s[:, sl] = dc
            return carry

        lax.fori_loop(0, HEADS, head, 0, unroll=HEAD_UNROLL)

        row = lax.broadcasted_iota(jnp.int32, (CHUNK, CHUNK), 0)
        col = lax.broadcasted_iota(jnp.int32, (CHUNK, CHUNK), 1)
        dlf = _tri_matmul((row <= col).astype(BF16), dc_s[...])
        df = dlf / f - dk_s[...]
        dlb_ref[...] += _sum0(df * (1.0 - sig))
        df_ref[...] = (df * (1.0 - lb) * sig * (1.0 - sig)).astype(BF16)
        dq_ref[...] = (dq_s[...] * _silu_grad(q_raw, jax.nn.sigmoid(q_raw))).astype(BF16)

        @pl.when(n == nc - 1)
        def _():
            d0 = dlb_ref[...] * lb * (1.0 - lb)
            dlg_ref[0:1, :] = d0
            dlg_ref[1:2, :] = -d0
            if ex:
                ex.finish(*xrefs)

    rev = lambda n: nc - 1 - n
    blk = lambda b: pl.BlockSpec((None, CHUNK, D_MODEL), functools.partial(lambda n, b: (b, rev(n), 0), b=b))
    row = pl.BlockSpec((CHUNK, D_MODEL), lambda n: (rev(n), 0))
    whole = lambda s: pl.BlockSpec(s, lambda n: (0,) * len(s))
    vec = (1, D_MODEL)
    res = pl.pallas_call(
        body,
        grid=(nc,),
        in_specs=[blk(SLOT_Q), blk(SLOT_F), blk(SLOT_I), blk(SLOT_OG), row, row,
                  pl.BlockSpec((None, HEADS, HEAD, HEAD), lambda n: (rev(n), 0, 0, 0)),
                  pl.BlockSpec(memory_space=pl.ANY), whole(logits.shape), whole(vec)] + (ex.in_specs if ex else []),
        out_specs=[pl.BlockSpec((4, CHUNK, D_MODEL), lambda n: (SLOT_Q // 4, rev(n), 0)), whole((2, D_MODEL)), whole(vec)]
        + (ex.out_specs if ex else []),
        out_shape=[jax.ShapeDtypeStruct(dh.shape, BF16), jax.ShapeDtypeStruct((2, D_MODEL), F32),
                   jax.ShapeDtypeStruct(vec, F32)] + (ex.out_shape if ex else []),
        scratch_shapes=[pltpu.VMEM((HEADS, HEAD, HEAD), F32), pltpu.VMEM(vec, F32)]
        + [pltpu.VMEM((CHUNK, D_MODEL), F32)] * 6 + (ex.scratch if ex else []),
        input_output_aliases={7: 0},
        compiler_params=_params(("arbitrary",)),
        name="hgrn_bwd",
    )(h, h, h, h, o_all, dy, states, dh, logits, g_norm, *(ex.arrays if ex else []))
    return res[0], res[1], res[2], res[3:]


def _merge_fwd(i, n, ga, gb, za, zb):
    return jax.nn.sigmoid(ga) * za + jax.nn.sigmoid(gb) * zb


def _merge_bwd(i, n, ga, gb, za, zb, dm):
    sa, sb = jax.nn.sigmoid(ga), jax.nn.sigmoid(gb)
    dgates = jnp.stack([(dm * za * sa * (1.0 - sa)).astype(BF16), (dm * zb * sb * (1.0 - sb)).astype(BF16)])
    return dgates, dm * sa, dm * sb


def _ln1_fwd(i, n, x, r1, g, b):
    xhat, _ = _ln_stats(ALPHA * x + r1)
    x1 = xhat * g + b
    return x1, x1


def _ln1_bwd(i, n, x, r1, dx1, g):
    xhat, rstd = _ln_stats(ALPHA * x + r1)
    dz = _ln_bwd(dx1, xhat, rstd, g)
    return dz, dz, _sum0(dx1 * xhat), _sum0(dx1)


def _ln2_loss(i, n, x1, fo, pg, pp, tgt, g, b):
    sg = jax.nn.sigmoid(pg)
    xhat, rstd = _ln_stats(ALPHA * x1 + fo + sg * pp)
    diff = xhat * g + b - tgt
    loss = 0.5 * jnp.sum(_mean(diff * diff), axis=0, keepdims=True)
    dy = diff * (1.0 / D_MODEL)
    dz = _ln_bwd(dy, xhat, rstd, g)
    return (dz, dz, dz * pp * sg * (1.0 - sg), dz * sg,
            jnp.broadcast_to(loss, (8, LANE)), _sum0(dy * xhat), _sum0(dy))


def _shift_down(cur, halo, tile):
    row = lax.broadcasted_iota(jnp.int32, cur.shape, 0)
    m1 = jnp.where(row == 0, halo[7:8, :], pltpu.roll(cur, 1, 0))
    m2 = jnp.where(row == 0, halo[6:7, :], jnp.where(row == 1, halo[7:8, :], pltpu.roll(cur, 2, 0)))
    return m1, m2


def _shift_up(cur, halo, tile):
    row = lax.broadcasted_iota(jnp.int32, cur.shape, 0)
    p1 = jnp.where(row == tile - 1, halo[0:1, :], pltpu.roll(cur, tile - 1, 0))
    p2 = jnp.where(row == tile - 2, halo[0:1, :], jnp.where(row == tile - 1, halo[1:2, :], pltpu.roll(cur, tile - 2, 0)))
    return p1, p2


def _conv_pre(i, gate, halo, w, b, tile):
    halo = jnp.where(i == 0, 0.0, halo)
    m1, m2 = _shift_down(gate, halo, tile)
    return w[0:1, :] * m2 + w[1:2, :] * m1 + w[2:3, :] * gate + b, m1, m2


def _conv_fwd(tile, i, n, gate, halo, val, w, b):
    cg, _, _ = _conv_pre(i, gate, halo, w, b, tile)
    return _gelu(cg) * val


def _conv_bwd_a(tile, i, n, gate, halo, val, dhid, w, b):
    cg, m1, m2 = _conv_pre(i, gate, halo, w, b, tile)
    dcg = dhid * val * _gelu_grad(cg)
    return dcg, dhid * _gelu(cg), _sum0(dcg * m2), _sum0(dcg * m1), _sum0(dcg * gate), _sum0(dcg)


def _conv_bwd_b(tile, i, n, dcg, halo, w):
    halo = jnp.where(i == n - 1, 0.0, halo)
    p1, p2 = _shift_up(dcg, halo, tile)
    return w[2:3, :] * dcg + w[1:2, :] * p1 + w[0:1, :] * p2


def _halo_spec(width, tile, t, nxt):
    per = tile // 8
    last = t // 8 - 1
    if nxt:
        return pl.BlockSpec((8, width), lambda i: (jnp.minimum((i + 1) * per, last), 0))
    return pl.BlockSpec((8, width), lambda i: (jnp.maximum(i * per - 1, 0), 0))


def _adamw(i, n, w, m, v, parts):
    g = parts[0].astype(F32)
    for j in range(1, N_DEV):
        g = g + parts[j].astype(F32)
    m_new = ADAM_B1 * m + (1.0 - ADAM_B1) * g
    v_new = ADAM_B2 * v + (1.0 - ADAM_B2) * (g * g)
    m_hat = m_new / (1.0 - ADAM_B1 ** ADAM_STEP)
    v_hat = v_new / (1.0 - ADAM_B2 ** ADAM_STEP)
    delta = -ADAM_LR * (m_hat / (jnp.sqrt(v_hat) + ADAM_EPS) + ADAM_WD * w)
    return g, delta, m_new, v_new


def _adam_call(w, m, v, parts, name):
    r, c = w.shape
    tile = _pick(r, (256, 128)) if r > 256 else r
    spec = pl.BlockSpec((N_DEV, tile, c), lambda i: (0, i, 0))
    return _rowwise(_adamw, [w, m, v, (parts, spec)], [], [(c, F32)] * 4, [], tile=tile, name=name)


def _peer(k):
    x, y, c = lax.axis_index("x"), lax.axis_index("y"), lax.axis_index("c")
    px = x ^ ((k >> 2) & 1)
    py = y ^ ((k >> 1) & 1)
    pc = c ^ (k & 1)
    return (px, py, pc), 4 * px + 2 * py + pc


def _my_index():
    return 4 * lax.axis_index("x") + 2 * lax.axis_index("y") + lax.axis_index("c")


class _Exchange:
    def __init__(self, entries):
        self.arrays = [a for a, _ in entries]
        self.scatter = [k.startswith("scatter") for _, k in entries]
        self.slots = [k.endswith("slots") for _, k in entries]
        self.n = len(entries)
        self.in_specs = [pl.BlockSpec(memory_space=pl.ANY)] * self.n
        self.out_specs = [pl.BlockSpec(memory_space=pl.ANY)] * self.n
        shapes = [tuple(a.shape[1:]) if sc else tuple(a.shape) for a, sc in zip(self.arrays, self.scatter)]
        self.out_shape = [jax.ShapeDtypeStruct((N_DEV,) + s, a.dtype) for s, a in zip(shapes, self.arrays)]
        per = N_DEV - 1
        self.scratch = [pltpu.SemaphoreType.DMA((self.n * per,)), pltpu.SemaphoreType.DMA((self.n * per,)),
                        pltpu.SemaphoreType.DMA((self.n,))]

    def _copies(self, srcs, outs, sems):
        send_sems, recv_sems, local_sems = sems
        me = _my_index()
        per = N_DEV - 1
        local, sends, arrivals = [], [], []
        for a in range(self.n):
            place = (lambda idx: _slot_of_group(idx)) if self.slots[a] else (lambda idx: idx)
            if self.scatter[a]:
                mine, land = srcs[a].at[place(me)], outs[a].at[me]
            else:
                mine, land = srcs[a], outs[a].at[place(me)]
            local.append(pltpu.make_async_copy(mine, land, local_sems.at[a]))
            for k in range(1, N_DEV):
                dev, idx = _peer(k)
                common = dict(send_sem=send_sems.at[a * per + k - 1], recv_sem=recv_sems.at[a * per + k - 1],
                              device_id=dev, device_id_type=pl.DeviceIdType.MESH)
                if self.scatter[a]:
                    sends.append(pltpu.make_async_remote_copy(src_ref=srcs[a].at[place(idx)], dst_ref=land, **common))
                    arrivals.append(pltpu.make_async_remote_copy(src_ref=mine, dst_ref=outs[a].at[idx], **common))
                else:
                    sends.append(pltpu.make_async_remote_copy(src_ref=mine, dst_ref=land, **common))
                    arrivals.append(pltpu.make_async_remote_copy(src_ref=mine, dst_ref=outs[a].at[place(idx)], **common))
        return local, sends, arrivals

    def start(self, srcs, outs, sems):
        local, sends, _ = self._copies(srcs, outs, sems)
        for cp in local + sends:
            cp.start()

    def finish(self, srcs, outs, sems):
        local, sends, arrivals = self._copies(srcs, outs, sems)
        for cp in arrivals:
            cp.wait_recv()
        for cp in sends:
            cp.wait_send()
        for cp in local:
            cp.wait()


def _gather_slots_two_level(shard, name):
    def body(src_ref, out_ref, send_sems, recv_sems, local_sem):
        x, y, c = lax.axis_index("x"), lax.axis_index("y"), lax.axis_index("c")
        me, sibling = (x, y, c), (x, y, 1 - c)
        chips = [(1 - x, y), (x, 1 - y), (1 - x, 1 - y)]

        def land(px, py, pc):
            return out_ref.at[_slot_of_group(4 * px + 2 * py + pc)]

        def copy(k, block, to, src=None):
            return pltpu.make_async_remote_copy(
                src_ref=land(*block) if src is None else src, dst_ref=land(*block),
                send_sem=send_sems.at[k], recv_sem=recv_sems.at[k], device_id=to, device_id_type=pl.DeviceIdType.MESH)

        mine = pltpu.make_async_copy(src_ref, land(*me), local_sem)
        mine.start()
        first = [copy(0, me, sibling, src=src_ref)]
        first += [copy(1 + j, me, (*chip, c), src=src_ref) for j, chip in enumerate(chips)]
        for cp in first:
            cp.start()
        passed = [copy(4 + j, (*chip, c), sibling) for j, chip in enumerate(chips)]
        for j, chip in enumerate(chips):
            copy(1 + j, (*chip, c), me).wait_recv()
            passed[j].start()
        copy(0, sibling, me).wait_recv()
        for j, chip in enumerate(chips):
            copy(4 + j, (*chip, 1 - c), me).wait_recv()
        for cp in first + passed:
            cp.wait_send()
        mine.wait()

    return pl.pallas_call(
        body,
        in_specs=[pl.BlockSpec(memory_space=pl.ANY)],
        out_specs=pl.BlockSpec(memory_space=pl.ANY),
        out_shape=jax.ShapeDtypeStruct((N_DEV,) + tuple(shard.shape), shard.dtype),
        scratch_shapes=[pltpu.SemaphoreType.DMA((N_DEV - 1,)), pltpu.SemaphoreType.DMA((N_DEV - 1,)),
                        pltpu.SemaphoreType.DMA],
        name=name,
    )(shard)


def _exchange(entries, name):
    ex = _Exchange(entries)
    n = ex.n

    def body(*refs):
        srcs, outs, sems = refs[:n], refs[n:2 * n], refs[2 * n:]
        ex.start(srcs, outs, sems)
        ex.finish(srcs, outs, sems)

    return pl.pallas_call(body, in_specs=ex.in_specs, out_specs=ex.out_specs, out_shape=ex.out_shape,
                          scratch_shapes=ex.scratch, name=name)(*ex.arrays)


def _local_step(x, p, tgt, w_in, small, comm):
    t = x.shape[0]
    tile = _pick(t, (256, 128))
    d = D_MODEL
    act_b, act_f = (d, BF16), (d, F32)
    x_b, p_b = x.astype(BF16), p.astype(BF16)

    chunk_id = jnp.arange(SGU_BLOCK) // CHUNK
    mask = chunk_id[:, None] >= chunk_id[None, :]
    wm = jnp.where(mask[None], small["sgu_w_s"], 0.0)
    wm_b = wm.astype(BF16)
    wm_t = jnp.swapaxes(wm, 1, 2).astype(BF16)
    bs_t = small["sgu_b_s"].T

    h = _mm(x_b, w_in, out_dtype=F32, name="mm_h")
    y_a = _sgu_fwd(h, wm_b, bs_t, small["sgu_norm_g"], small["sgu_norm_b"])
    y_b, o_all, states, got = _hgrn_fwd(h, small["lb_logits"], small["hgrn_norm_g"], ex=comm.weights_exchange())
    wts, conv_w = comm.weights(got)
    z_a = _mm(y_a, wts["w_a"], out_dtype=F32, name="mm_za")
    z_b = _mm(y_b, wts["w_b"], out_dtype=F32, name="mm_zb")
    gates = [(h, SLOT_GA), (h, SLOT_GB)]
    merged, = _rowwise(_merge_fwd, gates + [z_a, z_b], [], [act_b], [], tile=tile, name="merge_fwd")
    r1 = _mm(merged, wts["w_o"], out_dtype=F32, name="mm_r1")
    x1, x1_b = _rowwise(_ln1_fwd, [x, r1], [small["ln1_g"], small["ln1_b"]], [act_f, act_b], [], tile=tile, name="ln1_fwd")
    gate = _mm(x1_b, wts["w_g"], out_dtype=F32, name="mm_gate")
    val = _mm(x1_b, wts["w_v"], out_dtype=F32, name="mm_val")
    pg = _mm(x1_b, wts["w_pg"], out_dtype=F32, name="mm_pg")
    pp = _mm(p_b, wts["w_pp"], out_dtype=F32, name="mm_pp")
    hid, = _rowwise(functools.partial(_conv_fwd, tile), [gate, (gate, _halo_spec(D_FF, tile, t, False)), val],
                    [conv_w, small["conv_b"]], [(D_FF, BF16)], [], tile=tile, name="conv_fwd")
    fo = _mm(hid, wts["w_down"], out_dtype=F32, name="mm_down")
    dz2, dz2_b, dpg, dpp, loss, dg2, db2 = _rowwise(
        _ln2_loss, [x1, fo, pg, pp, tgt], [small["ln2_g"], small["ln2_b"]],
        [act_f, act_b, act_b, act_b], [(8, LANE), (1, d), (1, d)], tile=tile, name="ln2_loss")

    dhid = _mm(dz2_b, wts["w_down"], out_dtype=F32, name="mm_dhid", trans_b=True)
    g_down = _mm_tn(hid, dz2_b, out_dtype=BF16, name="mm_g_down")
    dcg, dval, dcw0, dcw1, dcw2, dcb = _rowwise(
        functools.partial(_conv_bwd_a, tile), [gate, (gate, _halo_spec(D_FF, tile, t, False)), val, dhid],
        [conv_w, small["conv_b"]], [(D_FF, F32), (D_FF, BF16)], [(1, D_FF)] * 4, tile=tile, name="conv_bwd_a")
    dgate, = _rowwise(functools.partial(_conv_bwd_b, tile), [dcg, (dcg, _halo_spec(D_FF, tile, t, True))],
                      [conv_w], [(D_FF, BF16)], [], tile=tile, name="conv_bwd_b")
    g_g = _mm_tn(x1_b, dgate, out_dtype=BF16, name="mm_g_gate")
    g_v = _mm_tn(x1_b, dval, out_dtype=BF16, name="mm_g_val")
    g_pg = _mm_tn(x1_b, dpg, out_dtype=BF16, name="mm_g_pg")
    g_pp = _mm_tn(p_b, dpp, out_dtype=BF16, name="mm_g_pp")
    dx1 = _mm(dgate, wts["w_g"], out_dtype=F32, name="mm_dx1_gate", trans_b=True, adds=[(dz2, ALPHA)])
    dx1 = _mm(dval, wts["w_v"], out_dtype=F32, name="mm_dx1_val", trans_b=True, adds=[(dx1, 1.0)])
    dx1 = _mm(dpg, wts["w_pg"], out_dtype=F32, name="mm_dx1_pg", trans_b=True, adds=[(dx1, 1.0)])
    dz1, dz1_b, dg1, db1 = _rowwise(_ln1_bwd, [x, r1, dx1], [small["ln1_g"]], [act_f, act_b], [(1, d), (1, d)],
                                    tile=tile, name="ln1_bwd")
    g_o = _mm_tn(merged, dz1_b, out_dtype=BF16, name="mm_g_o")
    dm = _mm(dz1_b, wts["w_o"], out_dtype=F32, name="mm_dm", trans_b=True)
    dh, dza, dzb = _rowwise(_merge_bwd, gates + [z_a, z_b, dm], [],
                            [("stack", 2, SLOT_GA // 2, 8, d, BF16), act_b, act_b], [], tile=tile, name="merge_bwd")
    g_a = _mm_tn(y_a, dza, out_dtype=BF16, name="mm_g_a")
    g_b = _mm_tn(y_b, dzb, out_dtype=BF16, name="mm_g_b")
    dy_a = _mm(dza, wts["w_a"], out_dtype=F32, name="mm_dya", trans_b=True)
    dy_b = _mm(dzb, wts["w_b"], out_dtype=F32, name="mm_dyb", trans_b=True)
    dh, dws, dbs, dgv_n, dbv_n = _sgu_bwd(h, dy_a, dh, wm_b, wm_t, bs_t, small["sgu_norm_g"], small["sgu_norm_b"])
    big = dict(w_a=g_a, w_b=g_b, w_o=g_o, w_g=g_g, w_v=g_v, w_down=g_down, w_pp=g_pp, w_pg=g_pg)
    sm = dict(sgu_w_s=jnp.where(mask[None], dws, 0.0), sgu_b_s=dbs[:, :GROUPS].T, sgu_norm_g=dgv_n, sgu_norm_b=dbv_n,
              ln1_g=dg1, ln1_b=db1, conv_w=jnp.concatenate([dcw0, dcw1, dcw2], axis=0), conv_b=dcb, ln2_g=dg2, ln2_b=db2,
              loss=loss)
    dh, dlogits, dgn, got = _hgrn_bwd(h, o_all, dy_b, states, dh, small["lb_logits"], small["hgrn_norm_g"],
                                      ex=comm.grads_exchange(big, sm))
    comm.grads_done(got)
    g_in = _mm_tn(x_b, dh, out_dtype=BF16, name="mm_g_in")
    ex = comm.last_exchange(g_in, dict(lb_logits=dlogits, hgrn_norm_g=dgn))
    res = _mm(dh, w_in, out_dtype=F32, name="mm_dx", trans_b=True, reduce_b=True, adds=[(dz1, ALPHA)], ex=ex)
    grad_x, got = res if ex else (res, ())
    comm.last_done(got)
    return grad_x


_SMALL_EARLY = ["sgu_w_s", "sgu_b_s", "sgu_norm_g", "sgu_norm_b", "ln1_g", "ln1_b", "ffn_conv_b", "ln2_g", "ln2_b"]
_SMALL_LATE = ["hgrn_lb_logits", "hgrn_norm_g"]
N_TAPS = D_FF // N_DEV
UP_COLS = 2 * D_FF // N_DEV


class _StepExchanges:
    def __init__(self, shards):
        self.shards = shards

    def weights_exchange(self):
        return _Exchange([(s, "gather") for s in self.shards])

    def weights(self, got):
        d, f = D_MODEL, D_FF
        w_br_g, w_o_g, w_up_g, w_down_g, w_pp_g, w_pg_g, conv_g = got
        w_br = w_br_g.transpose(1, 0, 2, 3).reshape(2, d, d)
        w_up = w_up_g.transpose(1, 0, 2).reshape(d, 2, f).transpose(1, 0, 2)
        wts = dict(w_a=w_br[0], w_b=w_br[1], w_o=w_o_g.reshape(d, d), w_g=w_up[0], w_v=w_up[1],
                   w_down=w_down_g.reshape(f, d), w_pp=w_pp_g.transpose(1, 0, 2).reshape(256, d), w_pg=w_pg_g.reshape(d, d))
        return wts, conv_g.transpose(1, 0, 2).reshape(3, f)

    def grads_exchange(self, big, sm):
        d = D_MODEL
        parts = [jnp.stack([big["w_a"], big["w_b"]]).reshape(2, N_DEV, 128, d).transpose(1, 0, 2, 3),
                 big["w_o"].reshape(N_DEV, 128, d),
                 jnp.concatenate([big["w_g"], big["w_v"]], axis=1).reshape(d, N_DEV, UP_COLS).transpose(1, 0, 2),
                 big["w_down"].reshape(N_DEV, N_TAPS, d),
                 big["w_pp"].reshape(256, N_DEV, 128).transpose(1, 0, 2),
                 big["w_pg"].reshape(N_DEV, 128, d)]
        packed, self.rows_early = _pack([sm[k] for k in ("sgu_w_s", "sgu_b_s", "sgu_norm_g", "sgu_norm_b", "ln1_g", "ln1_b",
                                                         "conv_b", "ln2_g", "ln2_b", "conv_w", "loss")])
        return _Exchange([(a, "scatter") for a in parts] + [(packed, "gather")])

    def grads_done(self, got):
        self.recv, self.small_early = got[:6], got[6]

    def last_exchange(self, g_in, sm):
        packed, self.rows_late = _pack([sm["lb_logits"], sm["hgrn_norm_g"]])
        return _Exchange([(g_in, "scatter_slots"), (packed, "gather")])

    def last_done(self, got):
        self.recv_in, self.small_late = got


def _rows128(a):
    flat = a.reshape(-1)
    rows = -(-flat.shape[0] // (8 * LANE)) * 8
    return jnp.pad(flat, (0, rows * LANE - flat.shape[0])).reshape(rows, LANE)


def _pack(parts):
    blocks = [_rows128(a) for a in parts]
    return jnp.concatenate(blocks, axis=0), [b.shape[0] for b in blocks]


def _unpack(packed, shapes, rows):
    out, r0 = [], 0
    for shp, r in zip(shapes, rows):
        n = math.prod(shp)
        out.append(packed[r0:r0 + r].reshape(-1)[:n].reshape(shp))
        r0 += r
    return out


def kernel(x, p, w_in, sgu_w_s, sgu_b_s, sgu_norm_g, sgu_norm_b, hgrn_lb_logits, hgrn_norm_g, w_branch, w_out, ln1_g, ln1_b, ffn_w_up, ffn_conv_w, ffn_conv_b, ffn_w_down, ln2_g, ln2_b, ple_w_proj, ple_w_gate, loss_target, m_w_in, m_sgu_w_s, m_sgu_b_s, m_sgu_norm_g, m_sgu_norm_b, m_hgrn_lb_logits, m_hgrn_norm_g, m_w_branch, m_w_out, m_ln1_g, m_ln1_b, m_ffn_w_up, m_ffn_conv_w, m_ffn_conv_b, m_ffn_w_down, m_ln2_g, m_ln2_b, m_ple_w_proj, m_ple_w_gate, v_w_in, v_sgu_w_s, v_sgu_b_s, v_sgu_norm_g, v_sgu_norm_b, v_hgrn_lb_logits, v_hgrn_norm_g, v_w_branch, v_w_out, v_ln1_g, v_ln1_b, v_ffn_w_up, v_ffn_conv_w, v_ffn_conv_b, v_ffn_w_down, v_ln2_g, v_ln2_b, v_ple_w_proj, v_ple_w_gate):
    weights = dict(w_in=w_in, sgu_w_s=sgu_w_s, sgu_b_s=sgu_b_s, sgu_norm_g=sgu_norm_g, sgu_norm_b=sgu_norm_b,
                   hgrn_lb_logits=hgrn_lb_logits, hgrn_norm_g=hgrn_norm_g, w_branch=w_branch, w_out=w_out,
                   ln1_g=ln1_g, ln1_b=ln1_b, ffn_w_up=ffn_w_up, ffn_conv_w=ffn_conv_w, ffn_conv_b=ffn_conv_b,
                   ffn_w_down=ffn_w_down, ln2_g=ln2_g, ln2_b=ln2_b, ple_w_proj=ple_w_proj, ple_w_gate=ple_w_gate)
    mom_m = dict(w_in=m_w_in, sgu_w_s=m_sgu_w_s, sgu_b_s=m_sgu_b_s, sgu_norm_g=m_sgu_norm_g, sgu_norm_b=m_sgu_norm_b,
                 hgrn_lb_logits=m_hgrn_lb_logits, hgrn_norm_g=m_hgrn_norm_g, w_branch=m_w_branch, w_out=m_w_out,
                 ln1_g=m_ln1_g, ln1_b=m_ln1_b, ffn_w_up=m_ffn_w_up, ffn_conv_w=m_ffn_conv_w, ffn_conv_b=m_ffn_conv_b,
                 ffn_w_down=m_ffn_w_down, ln2_g=m_ln2_g, ln2_b=m_ln2_b, ple_w_proj=m_ple_w_proj, ple_w_gate=m_ple_w_gate)
    mom_v = dict(w_in=v_w_in, sgu_w_s=v_sgu_w_s, sgu_b_s=v_sgu_b_s, sgu_norm_g=v_sgu_norm_g, sgu_norm_b=v_sgu_norm_b,
                 hgrn_lb_logits=v_hgrn_lb_logits, hgrn_norm_g=v_hgrn_norm_g, w_branch=v_w_branch, w_out=v_w_out,
                 ln1_g=v_ln1_g, ln1_b=v_ln1_b, ffn_w_up=v_ffn_w_up, ffn_conv_w=v_ffn_conv_w, ffn_conv_b=v_ffn_conv_b,
                 ffn_w_down=v_ffn_w_down, ln2_g=v_ln2_g, ln2_b=v_ln2_b, ple_w_proj=v_ple_w_proj, ple_w_gate=v_ple_w_gate)
    d, f = D_MODEL, D_FF
    me = _my_index()

    w_in_g = _gather_slots_two_level(w_in[0].astype(BF16), "gather_w_in")
    comm = _StepExchanges([w_branch[0].astype(BF16), w_out[0].astype(BF16), ffn_w_up[0].astype(BF16),
                           ffn_w_down[0].astype(BF16), ple_w_proj[0].astype(BF16), ple_w_gate[0].astype(BF16), ffn_conv_w[0]])
    small = dict(sgu_w_s=sgu_w_s[0], sgu_b_s=sgu_b_s[0], sgu_norm_g=sgu_norm_g, sgu_norm_b=sgu_norm_b,
                 lb_logits=hgrn_lb_logits, hgrn_norm_g=hgrn_norm_g, ln1_g=ln1_g, ln1_b=ln1_b, ln2_g=ln2_g, ln2_b=ln2_b,
                 conv_b=ffn_conv_b)
    grad_x = _local_step(x[0], p[0, 0], loss_target[0], w_in_g, small, comm)

    out = {}

    def adam(name, parts8, shape2d):
        w2, m2, v2 = (a.reshape(shape2d) for a in (weights[name], mom_m[name], mom_v[name]))
        res = _adam_call(w2, m2, v2, parts8.reshape((N_DEV,) + shape2d), "adam_" + name)
        out[name] = tuple(r.reshape(weights[name].shape) for r in res)

    adam("w_in", comm.recv_in, (d, d))
    adam("w_branch", comm.recv[0], (256, d))
    adam("w_out", comm.recv[1], (128, d))
    adam("ffn_w_up", comm.recv[2], (d, UP_COLS))
    adam("ffn_w_down", comm.recv[3], (N_TAPS, d))
    adam("ple_w_proj", comm.recv[4], (256, 128))
    adam("ple_w_gate", comm.recv[5], (128, d))

    def adam_small(names, extra_w, extra_m, extra_v, extra_shapes, parts8, rows, label):
        pk = lambda src, extra: _pack([src[n] for n in names] + extra)[0]
        res = _adam_call(pk(weights, extra_w), pk(mom_m, extra_m), pk(mom_v, extra_v), parts8, label)
        shapes = [weights[n].shape for n in names] + extra_shapes
        unpacked = [_unpack(r, shapes, rows) for r in res]
        for j, n in enumerate(names):
            out[n] = tuple(u[j] for u in unpacked)
        return [[u[len(names) + j] for u in unpacked] for j in range(len(extra_shapes))]

    blank = jnp.zeros((8, LANE), F32)
    taps, loss_rows = adam_small(
        _SMALL_EARLY, [_place_taps(ffn_conv_w[0], me, f), blank], [_place_taps(m_ffn_conv_w[0], me, f), blank],
        [_place_taps(v_ffn_conv_w[0], me, f), blank + 1.0], [(3, f), (8, LANE)], comm.small_early, comm.rows_early,
        "adam_small_early")
    adam_small(_SMALL_LATE, [], [], [], [], comm.small_late, comm.rows_late, "adam_small_late")
    out["ffn_conv_w"] = tuple(lax.dynamic_slice_in_dim(u, me * N_TAPS, N_TAPS, axis=1)[None] for u in taps)
    loss = loss_rows[0][0, 0]

    order = ["w_in", "sgu_w_s", "sgu_b_s", "sgu_norm_g", "sgu_norm_b", "hgrn_lb_logits", "hgrn_norm_g", "w_branch", "w_out",
             "ln1_g", "ln1_b", "ffn_w_up", "ffn_conv_w", "ffn_conv_b", "ffn_w_down", "ln2_g", "ln2_b", "ple_w_proj", "ple_w_gate"]
    return (loss, grad_x[None], *[out[n][0] for n in order], *[out[n][1] for n in order],
            *[out[n][2] for n in order], *[out[n][3] for n in order])


def _place_taps(shard, me, f):
    return lax.dynamic_update_slice_in_dim(jnp.zeros((3, f), F32), shard, me * N_TAPS, axis=1)
```

```python
import functools
import math

import jax
import jax.numpy as jnp
from jax import lax
from jax.experimental import pallas as pl
from jax.experimental.pallas import tpu as pltpu

F32 = jnp.float32
BF16 = jnp.bfloat16

N_DEV = 8
D_MODEL = 1024
CHUNK = 64
SUB = 16
SGU_BLOCK = 128
GROUPS = 8
HEAD = 128
HEADS = 8
HEAD_UNROLL = 4
D_FF = 2816
LN_EPS = 1e-5
RMS_EPS = 1e-6
ALPHA = 2.0 ** 0.25
GELU_K = math.sqrt(2.0 / math.pi)
GELU_C = 0.044715
NEG = -1e30
ADAM_LR, ADAM_B1, ADAM_B2, ADAM_EPS, ADAM_WD, ADAM_STEP = 0.001, 0.9, 0.999, 1e-08, 0.01, 10
LANE = 128
SLOT_Q, SLOT_F, SLOT_I, SLOT_OG, SLOT_U, SLOT_V, SLOT_GA, SLOT_GB = range(8)


def _slot_of_group(k):
    return jnp.where(k < 2, k + 4, jnp.where(k < 6, k - 2, k))


MIB = 1024 * 1024
VMEM_V7X = 64 * MIB
VMEM_FLOOR = 32 * MIB
MM_TILES = (1024, 1408, 512, 256, 128)


def _params(sem, need=0):
    limit = min(max(need + need // 4, VMEM_FLOOR), VMEM_V7X - 4 * MIB)
    return pltpu.CompilerParams(dimension_semantics=sem, vmem_limit_bytes=limit)


def _pick(n, prefs):
    for t in prefs:
        if n % t == 0:
            return t
    return n


def _gelu(x):
    return 0.5 * x * (1.0 + jnp.tanh(GELU_K * (x + GELU_C * x * x * x)))


def _gelu_grad(x):
    t = jnp.tanh(GELU_K * (x + GELU_C * x * x * x))
    return 0.5 * (1.0 + t) + 0.5 * x * (1.0 - t * t) * GELU_K * (1.0 + 3.0 * GELU_C * x * x)


def _silu_grad(x, s):
    return s * (1.0 + x * (1.0 - s))


def _dot(a, b):
    return jnp.dot(a.astype(BF16), b.astype(BF16), preferred_element_type=F32)


def _dot_nt(a, b):
    return lax.dot_general(a.astype(BF16), b.astype(BF16), (((1,), (1,)), ((), ())), preferred_element_type=F32)


def _dot_tn(a, b):
    return lax.dot_general(a.astype(BF16), b.astype(BF16), (((0,), (0,)), ((), ())), preferred_element_type=F32)


def _mean(x):
    return jnp.mean(x, axis=-1, keepdims=True)


def _sum0(x):
    return jnp.sum(x, axis=0, keepdims=True)


def _mm(a, b, *, out_dtype, name, trans_b=False, reduce_b=False, adds=(), ex=None):
    squeeze = b.ndim == 2
    a3 = a if a.ndim == 3 else a[None]
    b3 = b if b.ndim == 3 else b[None]
    ba, m, k = a3.shape
    bb = b3.shape[0]
    n = b3.shape[1] if trans_b else b3.shape[2]
    tm = _pick(m, MM_TILES)
    tn = _pick(n, MM_TILES)
    if reduce_b:
        bo, steps = 1, bb
        a_map = lambda o, i, j, r: (r if ba > 1 else 0, i, 0)
        b_map = (lambda o, i, j, r: (r, j, 0)) if trans_b else (lambda o, i, j, r: (r, 0, j))
    else:
        bo, steps = bb, 1
        a_map = lambda o, i, j, r: (o if ba > 1 else 0, i, 0)
        b_map = (lambda o, i, j, r: (o, j, 0)) if trans_b else (lambda o, i, j, r: (o, 0, j))
    o_map = lambda o, i, j, r: (o, i, j)
    add_arrays = [x if x.ndim == 3 else x[None] for x, _ in adds]
    add_scales = [s for _, s in adds]
    n_add = len(adds)
    dot = _dot_nt if trans_b else _dot

    def finish(acc, add_refs, o_ref):
        for ref, s in zip(add_refs, add_scales):
            acc = acc + s * ref[...].astype(F32)
        o_ref[...] = acc.astype(o_ref.dtype)

    grid = (bo, m // tm, n // tn, steps)

    def body(*refs):
        ins, (o_ref,), scratch, xrefs = _split_refs(refs, 2 + n_add, 1, 1 if reduce_b else 0, ex)
        a_ref, b_ref, add_refs = ins[0], ins[1], ins[2:]
        step = ((pl.program_id(0) * grid[1] + pl.program_id(1)) * grid[2] + pl.program_id(2)) * grid[3] + pl.program_id(3)
        if ex:
            @pl.when(step == 0)
            def _():
                ex.start(*xrefs)

        if reduce_b:
            acc, = scratch
            r = pl.program_id(3)

            @pl.when(r == 0)
            def _():
                acc[...] = jnp.zeros_like(acc)

            acc[...] += dot(a_ref[...], b_ref[...])

            @pl.when(r == steps - 1)
            def _():
                finish(acc[...], add_refs, o_ref)
        else:
            finish(dot(a_ref[...], b_ref[...]), add_refs, o_ref)

        if ex:
            @pl.when(step == math.prod(grid) - 1)
            def _():
                ex.finish(*xrefs)

    b_block = (None, tn, k) if trans_b else (None, k, tn)
    out_bytes = tm * tn * jnp.dtype(out_dtype).itemsize
    need = 2 * (tm * k * a3.dtype.itemsize + k * tn * b3.dtype.itemsize + out_bytes + n_add * tm * tn * 4)
    need += 2 * tm * tn * 4
    sem = ("arbitrary",) * 4 if ex else ("parallel", "parallel", "parallel", "arbitrary")
    res = pl.pallas_call(
        body,
        grid=grid,
        in_specs=[pl.BlockSpec((None, tm, k), a_map), pl.BlockSpec(b_block, b_map)]
        + [pl.BlockSpec((None, tm, tn), o_map) for _ in adds] + (ex.in_specs if ex else []),
        out_specs=[pl.BlockSpec((None, tm, tn), o_map)] + (ex.out_specs if ex else []),
        out_shape=[jax.ShapeDtypeStruct((bo, m, n), out_dtype)] + (ex.out_shape if ex else []),
        scratch_shapes=([pltpu.VMEM((tm, tn), F32)] if reduce_b else []) + (ex.scratch if ex else []),
        compiler_params=_params(sem, need),
        name=name,
    )(a3, b3, *add_arrays, *(ex.arrays if ex else []))
    out = res[0][0] if (reduce_b or squeeze) else res[0]
    return (out, res[1:]) if ex else out


def _mm_tn(a, b, *, out_dtype, name):
    squeeze = b.ndim == 2
    b3 = b if b.ndim == 3 else b[None]
    t, m = a.shape
    bb, _, n = b3.shape
    tm = _pick(m, MM_TILES)
    tn = _pick(n, MM_TILES)
    tt = _pick(t, (1024, 512, 256, 128))
    steps = t // tt
    need = 2 * (tt * tm * a.dtype.itemsize + tt * tn * b3.dtype.itemsize + tm * tn * jnp.dtype(out_dtype).itemsize)
    need += 2 * tm * tn * 4

    def body(a_ref, b_ref, o_ref, acc):
        r = pl.program_id(3)

        @pl.when(r == 0)
        def _():
            acc[...] = jnp.zeros_like(acc)

        acc[...] += _dot_tn(a_ref[...], b_ref[...])

        @pl.when(r == steps - 1)
        def _():
            o_ref[...] = acc[...].astype(o_ref.dtype)

    out = pl.pallas_call(
        body,
        grid=(bb, m // tm, n // tn, steps),
        in_specs=[pl.BlockSpec((tt, tm), lambda o, i, j, r: (r, i)),
                  pl.BlockSpec((None, tt, tn), lambda o, i, j, r: (o, r, j))],
        out_specs=pl.BlockSpec((None, tm, tn), lambda o, i, j, r: (o, i, j)),
        out_shape=jax.ShapeDtypeStruct((bb, m, n), out_dtype),
        scratch_shapes=[pltpu.VMEM((tm, tn), F32)],
        compiler_params=_params(("parallel", "parallel", "parallel", "arbitrary"), need),
        name=name,
    )(a, b3)
    return out[0] if squeeze else out


def _rowwise(fn, rows, consts, row_outs, acc_outs, *, tile, name):
    first = rows[0][0] if isinstance(rows[0], tuple) else rows[0]
    t = first.shape[-2]
    steps = t // tile
    arrays, in_specs = [], []
    for r in rows:
        if isinstance(r, tuple) and isinstance(r[1], pl.BlockSpec):
            arrays.append(r[0])
            in_specs.append(r[1])
        elif isinstance(r, tuple):
            arr, bidx = r
            arrays.append(arr)
            in_specs.append(pl.BlockSpec((None, tile, arr.shape[-1]), functools.partial(lambda i, b: (b, i, 0), b=bidx)))
        else:
            arrays.append(r)
            in_specs.append(pl.BlockSpec((tile, r.shape[-1]), lambda i: (i, 0)))
    for c in consts:
        arrays.append(c)
        in_specs.append(pl.BlockSpec(c.shape, lambda i: (0, 0)))
    n_in, n_row = len(arrays), len(row_outs)
    out_shape, out_specs = [], []
    for ro in row_outs:
        if ro[0] == "stack":
            _, cnt, blk, total, w, dt = ro
            out_shape.append(jax.ShapeDtypeStruct((total, t, w), dt))
            out_specs.append(pl.BlockSpec((cnt, tile, w), functools.partial(lambda i, b: (b, i, 0), b=blk)))
        else:
            w, dt = ro
            out_shape.append(jax.ShapeDtypeStruct((t, w), dt))
            out_specs.append(pl.BlockSpec((tile, w), lambda i: (i, 0)))
    out_shape += [jax.ShapeDtypeStruct(s, F32) for s in acc_outs]
    out_specs += [pl.BlockSpec(s, lambda i: (0, 0)) for s in acc_outs]
    blocks = [math.prod(d for d in sp.block_shape if d) * arr.dtype.itemsize for sp, arr in zip(in_specs, arrays)]
    blocks += [math.prod(d for d in sp.block_shape if d) * sh.dtype.itemsize for sp, sh in zip(out_specs, out_shape)]
    need = 2 * sum(blocks) + 6 * tile * max(a.shape[-1] for a in arrays) * 4

    def body(*refs):
        ins, outs = refs[:n_in], refs[n_in:]
        i = pl.program_id(0)
        res = fn(i, steps, *[r[...] for r in ins])
        res = res if isinstance(res, (tuple, list)) else (res,)
        for ref, val in zip(outs[:n_row], res[:n_row]):
            ref[...] = val.astype(ref.dtype)
        if acc_outs:
            @pl.when(i == 0)
            def _():
                for ref in outs[n_row:]:
                    ref[...] = jnp.zeros_like(ref)

            for ref, val in zip(outs[n_row:], res[n_row:]):
                ref[...] += val

    return pl.pallas_call(
        body,
        grid=(steps,),
        in_specs=in_specs,
        out_specs=out_specs,
        out_shape=out_shape,
        compiler_params=_params(("arbitrary",), need),
        name=name,
    )(*arrays)


def _ln_stats(z):
    mu = _mean(z)
    zc = z - mu
    rstd = lax.rsqrt(_mean(zc * zc) + LN_EPS)
    return zc * rstd, rstd


def _ln_bwd(dy, xhat, rstd, g):
    dxh = dy * g
    return rstd * (dxh - _mean(dxh) - xhat * _mean(dxh * xhat))


def _sgu_fwd(h, wm, bs_t, g_v, b_v):
    t = h.shape[1]

    def body(u_ref, v_ref, wm_ref, bs_ref, g_ref, b_ref, y_ref):
        xhat, _ = _ln_stats(_gelu(v_ref[...]))
        vn = (xhat * g_ref[...] + b_ref[...]).astype(BF16)
        gu = _gelu(u_ref[...])
        for g in range(GROUPS):
            sl = slice(g * HEAD, (g + 1) * HEAD)
            mixed = _dot(wm_ref[g], vn[:, sl]) + bs_ref[:, g:g + 1]
            y_ref[:, sl] = (gu[:, sl] * mixed).astype(BF16)

    blk = lambda b: pl.BlockSpec((None, SGU_BLOCK, D_MODEL), functools.partial(lambda i, b: (b, i, 0), b=b))
    whole = lambda s: pl.BlockSpec(s, lambda i: (0,) * len(s))
    return pl.pallas_call(
        body,
        grid=(t // SGU_BLOCK,),
        in_specs=[blk(SLOT_U), blk(SLOT_V), whole(wm.shape), whole(bs_t.shape), whole(g_v.shape), whole(b_v.shape)],
        out_specs=pl.BlockSpec((SGU_BLOCK, D_MODEL), lambda i: (i, 0)),
        out_shape=jax.ShapeDtypeStruct((t, D_MODEL), BF16),
        compiler_params=_params(("parallel",)),
        name="sgu_fwd",
    )(h, h, wm, bs_t, g_v, b_v)


def _sgu_bwd(h, dy, dh, wm, wm_t, bs_t, g_v, b_v):
    t = h.shape[1]

    def body(u_ref, v_ref, dy_ref, dh_in, wm_ref, wmt_ref, bs_ref, g_ref, b_ref,
             duv_ref, dw_ref, dbs_ref, dg_ref, db_ref, dvn_ref):
        del dh_in
        du_ref, dv_ref = duv_ref.at[0], duv_ref.at[1]
        i = pl.program_id(0)

        @pl.when(i == 0)
        def _():
            dw_ref[...] = jnp.zeros_like(dw_ref)
            dbs_ref[...] = jnp.zeros_like(dbs_ref)
            dg_ref[...] = jnp.zeros_like(dg_ref)
            db_ref[...] = jnp.zeros_like(db_ref)

        u = u_ref[...]
        v = v_ref[...]
        xhat, rstd = _ln_stats(_gelu(v))
        vn = (xhat * g_ref[...] + b_ref[...]).astype(BF16)
        gu = _gelu(u)
        gup = _gelu_grad(u)
        lane = lax.broadcasted_iota(jnp.int32, (SGU_BLOCK, LANE), 1)
        dbs = jnp.zeros((SGU_BLOCK, LANE), F32)
        for g in range(GROUPS):
            sl = slice(g * HEAD, (g + 1) * HEAD)
            vn_g = vn[:, sl]
            mixed = _dot(wm_ref[g], vn_g) + bs_ref[:, g:g + 1]
            dy_g = dy_ref[:, sl]
            du_ref[:, sl] = (dy_g * mixed * gup[:, sl]).astype(BF16)
            dmix = dy_g * gu[:, sl]
            dmb = dmix.astype(BF16)
            dvn_ref[:, sl] = _dot(wmt_ref[g], dmb)
            dw_ref[g] += _dot_nt(dmb, vn_g)
            dbs = dbs + jnp.where(lane == g, jnp.sum(dmix, axis=1, keepdims=True), 0.0)
        dbs_ref[...] += dbs
        dvn = dvn_ref[...]
        dg_ref[...] += _sum0(dvn * xhat)
        db_ref[...] += _sum0(dvn)
        dv_ref[...] = (_ln_bwd(dvn, xhat, rstd, g_ref[...]) * _gelu_grad(v)).astype(BF16)

    blk = lambda b: pl.BlockSpec((None, SGU_BLOCK, D_MODEL), functools.partial(lambda i, b: (b, i, 0), b=b))
    row = pl.BlockSpec((SGU_BLOCK, D_MODEL), lambda i: (i, 0))
    whole = lambda s: pl.BlockSpec(s, lambda i: (0,) * len(s))
    vec = (1, D_MODEL)
    return pl.pallas_call(
        body,
        grid=(t // SGU_BLOCK,),
        in_specs=[blk(SLOT_U), blk(SLOT_V), row, pl.BlockSpec(memory_space=pl.ANY),
                  whole(wm.shape), whole(wm_t.shape), whole(bs_t.shape), whole(vec), whole(vec)],
        out_specs=[pl.BlockSpec((2, SGU_BLOCK, D_MODEL), lambda i: (SLOT_U // 2, i, 0)),
                   whole(wm.shape), whole((SGU_BLOCK, LANE)), whole(vec), whole(vec)],
        out_shape=[jax.ShapeDtypeStruct(dh.shape, BF16),
                   jax.ShapeDtypeStruct(wm.shape, F32), jax.ShapeDtypeStruct((SGU_BLOCK, LANE), F32),
                   jax.ShapeDtypeStruct(vec, F32), jax.ShapeDtypeStruct(vec, F32)],
        scratch_shapes=[pltpu.VMEM((SGU_BLOCK, D_MODEL), F32)],
        input_output_aliases={3: 0},
        compiler_params=_params(("arbitrary",)),
        name="sgu_bwd",
    )(h, h, dy, dh, wm, wm_t, bs_t, g_v, b_v)


def _split3(x):
    hi = x.astype(BF16)
    r1 = x - hi.astype(F32)
    mid = r1.astype(BF16)
    lo = (r1 - mid.astype(F32)).astype(BF16)
    return hi, mid, lo


def _tri_matmul(tri, x):
    hi, mid, lo = _split3(x)
    dot = lambda p: jnp.dot(tri, p, preferred_element_type=F32)
    return dot(hi) + dot(mid) + dot(lo)


def _lower_bound(logits):
    l0, l1 = logits[0:1, :], logits[1:2, :]
    mx = jnp.maximum(l0, l1)
    e0, e1 = jnp.exp(l0 - mx), jnp.exp(l1 - mx)
    return e0 / (e0 + e1)


def _hgrn_gates(q_raw, f_raw, lb):
    q = q_raw * jax.nn.sigmoid(q_raw)
    sig = jax.nn.sigmoid(f_raw)
    f = lb + (1.0 - lb) * sig
    row = lax.broadcasted_iota(jnp.int32, (CHUNK, CHUNK), 0)
    col = lax.broadcasted_iota(jnp.int32, (CHUNK, CHUNK), 1)
    c = _tri_matmul((row >= col).astype(BF16), jnp.log(f))
    return q, sig, f, 1.0 - f, c


def _offdiag_terms(qh, kh, ch, tb):
    rows = slice(tb * SUB, (tb + 1) * SUB)
    r = ch[tb * SUB - 1:tb * SUB, :]
    eqh = jnp.exp(ch[rows] - r)
    ekh = jnp.exp(jnp.minimum(r - ch, 0.0))
    return rows, eqh, qh[rows] * eqh, ekh, kh * ekh


def _diag_decay(cb, s, trow):
    return jnp.exp(jnp.where(trow >= s, cb - cb[s:s + 1, :], NEG))


def _split_refs(refs, n_in, n_out, n_scratch, ex):
    nx = ex.n if ex else 0
    ins, refs = refs[:n_in], refs[n_in:]
    xsrc, refs = refs[:nx], refs[nx:]
    outs, refs = refs[:n_out], refs[n_out:]
    xout, refs = refs[:nx], refs[nx:]
    return ins, outs, refs[:n_scratch], (xsrc, xout, refs[n_scratch:])


def _hgrn_fwd(h, logits, g_norm, ex=None):
    t = h.shape[1]
    nc = t // CHUNK

    def body(*refs):
        ins, outs, scratch, xrefs = _split_refs(refs, 6, 3, 4, ex)
        q_ref, f_ref, i_ref, og_ref, lg_ref, gn_ref = ins
        y_ref, o_ref, sall_ref = outs
        st_ref, q_s, k_s, c_s = scratch

        @pl.when(pl.program_id(0) == 0)
        def _():
            st_ref[...] = jnp.zeros_like(st_ref)
            if ex:
                ex.start(*xrefs)

        if ex:
            @pl.when(pl.program_id(0) == (3 * nc) // 4)
            def _():
                ex.relay(*xrefs)

        lb = _lower_bound(lg_ref[...])
        q, _, _, k, c = _hgrn_gates(q_ref[...], f_ref[...], lb)
        q_s[...] = q
        k_s[...] = k
        c_s[...] = c
        col64 = lax.broadcasted_iota(jnp.int32, (SUB, CHUNK), 1)
        trow = lax.broadcasted_iota(jnp.int32, (SUB, HEAD), 0)

        def head(hd, carry):
            sl = pl.ds(pl.multiple_of(hd * HEAD, HEAD), HEAD)
            qh, kh, ch, ih = q_s[:, sl], k_s[:, sl], c_s[:, sl], i_ref[:, sl]
            st = st_ref[hd]
            sall_ref[hd] = st
            c_last = ch[CHUNK - 1:CHUNK, :]
            o = _dot_nt(qh * jnp.exp(ch), st)
            st_ref[hd] = st * jnp.exp(c_last) + _dot_tn(ih, kh * jnp.exp(c_last - ch))
            a_rows = [jnp.zeros((SUB, CHUNK), F32)]
            for tb in range(1, CHUNK // SUB):
                _, _, q_hat, _, k_hat = _offdiag_terms(qh, kh, ch, tb)
                a_rows.append(jnp.where(col64 < tb * SUB, _dot_nt(q_hat, k_hat), 0.0))
            o = o + _dot(jnp.concatenate(a_rows, axis=0), ih)
            o_rows = []
            for b in range(CHUNK // SUB):
                rows = slice(b * SUB, (b + 1) * SUB)
                qb, cb, kb, ib = qh[rows], ch[rows], kh[rows], ih[rows]
                ob = jnp.zeros((SUB, HEAD), F32)
                for s in range(SUB):
                    a = jnp.sum(qb * _diag_decay(cb, s, trow) * kb[s:s + 1, :], axis=1, keepdims=True)
                    ob = ob + a * ib[s:s + 1, :]
                o_rows.append(ob)
            o = o + jnp.concatenate(o_rows, axis=0)
            o_ref[:, sl] = o
            og = og_ref[:, sl]
            on = o * lax.rsqrt(_mean(o * o) + RMS_EPS)
            y_ref[:, sl] = (on * gn_ref[:, sl] * (og * jax.nn.sigmoid(og))).astype(BF16)
            return carry

        lax.fori_loop(0, HEADS, head, 0, unroll=HEAD_UNROLL)

        if ex:
            @pl.when(pl.program_id(0) == nc - 1)
            def _():
                ex.finish(*xrefs)

    blk = lambda b: pl.BlockSpec((None, CHUNK, D_MODEL), functools.partial(lambda n, b: (b, n, 0), b=b))
    row = pl.BlockSpec((CHUNK, D_MODEL), lambda n: (n, 0))
    whole = lambda s: pl.BlockSpec(s, lambda n: (0,) * len(s))
    res = pl.pallas_call(
        body,
        grid=(nc,),
        in_specs=[blk(SLOT_Q), blk(SLOT_F), blk(SLOT_I), blk(SLOT_OG), whole(logits.shape), whole(g_norm.shape)]
        + (ex.in_specs if ex else []),
        out_specs=[row, row, pl.BlockSpec((None, HEADS, HEAD, HEAD), lambda n: (n, 0, 0, 0))] + (ex.out_specs if ex else []),
        out_shape=[jax.ShapeDtypeStruct((t, D_MODEL), BF16), jax.ShapeDtypeStruct((t, D_MODEL), F32),
                   jax.ShapeDtypeStruct((nc, HEADS, HEAD, HEAD), F32)] + (ex.out_shape if ex else []),
        scratch_shapes=[pltpu.VMEM((HEADS, HEAD, HEAD), F32)] + [pltpu.VMEM((CHUNK, D_MODEL), F32)] * 3
        + (ex.scratch if ex else []),
        compiler_params=_params(("arbitrary",)),
        name="hgrn_fwd",
    )(h, h, h, h, logits, g_norm, *(ex.arrays if ex else []))
    return res[0], res[1], res[2], res[3:]


def _hgrn_bwd(h, o_all, dy, states, dh, logits, g_norm, ex=None):
    t = h.shape[1]
    nc = t // CHUNK

    def body(*refs):
        ins, outs, scratch, xrefs = _split_refs(refs, 10, 3, 8, ex)
        q_ref, f_ref, i_ref, og_ref, o_ref, dy_ref, sall_ref, _, lg_ref, gn_ref = ins
        dqfio_ref, dlg_ref, dgn_ref = outs
        dst_ref, dlb_ref, q_s, k_s, c_s, dq_s, dk_s, dc_s = scratch
        dq_ref, df_ref, di_ref, dog_ref = (dqfio_ref.at[s] for s in (SLOT_Q, SLOT_F, SLOT_I, SLOT_OG))
        n = pl.program_id(0)

        @pl.when(n == 0)
        def _():
            dst_ref[...] = jnp.zeros_like(dst_ref)
            dlb_ref[...] = jnp.zeros_like(dlb_ref)
            dgn_ref[...] = jnp.zeros_like(dgn_ref)
            if ex:
                ex.start(*xrefs)

        lb = _lower_bound(lg_ref[...])
        q_raw = q_ref[...]
        q, sig, f, k, c = _hgrn_gates(q_raw, f_ref[...], lb)
        q_s[...] = q
        k_s[...] = k
        c_s[...] = c
        col64 = lax.broadcasted_iota(jnp.int32, (SUB, CHUNK), 1)
        trow = lax.broadcasted_iota(jnp.int32, (SUB, HEAD), 0)
        row64 = lax.broadcasted_iota(jnp.int32, (CHUNK, HEAD), 0)

        def head(hd, carry):
            sl = pl.ds(pl.multiple_of(hd * HEAD, HEAD), HEAD)
            qh, kh, ch, ih = q_s[:, sl], k_s[:, sl], c_s[:, sl], i_ref[:, sl]
            st = sall_ref[hd]
            dst = dst_ref[hd]
            oh, dyh, og, gn = o_ref[:, sl], dy_ref[:, sl], og_ref[:, sl], gn_ref[:, sl]
            sg = jax.nn.sigmoid(og)
            sil = og * sg
            rms = lax.rsqrt(_mean(oh * oh) + RMS_EPS)
            on = oh * rms
            dog_ref[:, sl] = (dyh * on * gn * _silu_grad(og, sg)).astype(BF16)
            dgn_ref[:, sl] += _sum0(dyh * on * sil)
            don = dyh * gn * sil
            do = rms * (don - on * _mean(don * on))
            dob = do.astype(BF16)

            c_last = ch[CHUNK - 1:CHUNK, :]
            eq = jnp.exp(ch)
            q_til = qh * eq
            ekl = jnp.exp(c_last - ch)
            k_til = kh * ekl
            ecl = jnp.exp(c_last)
            dq_til = _dot(dob, st)
            dk_til = _dot(ih, dst)
            di = _dot_nt(k_til, dst)
            dc_last = _sum0(dk_til * k_til) + _sum0(dst * st) * ecl
            dst_ref[hd] = _dot_tn(dob, q_til) + dst * ecl
            dq = dq_til * eq
            dc = dq_til * q_til - dk_til * k_til
            dk = dk_til * ekl

            da_full = _dot_nt(dob, ih)
            a_rows = [jnp.zeros((SUB, CHUNK), F32)]
            dq_rows = [jnp.zeros((SUB, HEAD), F32)]
            dc_rows = [jnp.zeros((SUB, HEAD), F32)]
            for tb in range(1, CHUNK // SUB):
                rows, eqh, q_hat, ekh, k_hat = _offdiag_terms(qh, kh, ch, tb)
                keep = col64 < tb * SUB
                a_rows.append(jnp.where(keep, _dot_nt(q_hat, k_hat), 0.0))
                da = jnp.where(keep, da_full[rows], 0.0)
                dq_hat = _dot(da, k_hat)
                dk_hat = _dot_tn(da, q_hat)
                dq_rows.append(dq_hat * eqh)
                dc_rows.append(dq_hat * q_hat)
                dk = dk + dk_hat * ekh
                dc = dc - dk_hat * k_hat
            di = di + _dot_tn(jnp.concatenate(a_rows, axis=0), dob)

            dk_rows, di_rows = [], []
            for b in range(CHUNK // SUB):
                rows = slice(b * SUB, (b + 1) * SUB)
                qb, cb, kb, ib, dob_ = qh[rows], ch[rows], kh[rows], ih[rows], do[rows]
                dq_diag = jnp.zeros((SUB, HEAD), F32)
                dk_diag = jnp.zeros((SUB, HEAD), F32)
                di_diag = jnp.zeros((SUB, HEAD), F32)
                for s in range(SUB):
                    ks = kb[s:s + 1, :]
                    dec = _diag_decay(cb, s, trow)
                    a = jnp.sum(qb * dec * ks, axis=1, keepdims=True)
                    gk = jnp.sum(dob_ * ib[s:s + 1, :], axis=1, keepdims=True) * dec
                    dq_diag = dq_diag + gk * ks
                    dk_diag = dk_diag + jnp.where(trow == s, _sum0(gk * qb), 0.0)
                    di_diag = di_diag + jnp.where(trow == s, _sum0(a * dob_), 0.0)
                dq_rows[b] = dq_rows[b] + dq_diag
                dc_rows[b] = dc_rows[b] + qb * dq_diag - kb * dk_diag
                dk_rows.append(dk_diag)
                di_rows.append(di_diag)
            dq = dq + jnp.concatenate(dq_rows, axis=0)
            dk = dk + jnp.concatenate(dk_rows, axis=0)
            dc = dc + jnp.concatenate(dc_rows, axis=0) + jnp.where(row64 == CHUNK - 1, dc_last, 0.0)
            di_ref[:, sl] = (di + jnp.concatenate(di_rows, axis=0)).astype(BF16)
            dq_s[:, sl] = dq
            dk_s[:, sl] = dk
            dc_s[:, sl] = dc
            return carry

        lax.fori_loop(0, HEADS, head, 0, unroll=HEAD_UNROLL)

        row = lax.broadcasted_iota(jnp.int32, (CHUNK, CHUNK), 0)
        col = lax.broadcasted_iota(jnp.int32, (CHUNK, CHUNK), 1)
        dlf = _tri_matmul((row <= col).astype(BF16), dc_s[...])
        df = dlf / f - dk_s[...]
        dlb_ref[...] += _sum0(df * (1.0 - sig))
        df_ref[...] = (df * (1.0 - lb) * sig * (1.0 - sig)).astype(BF16)
        dq_ref[...] = (dq_s[...] * _silu_grad(q_raw, jax.nn.sigmoid(q_raw))).astype(BF16)

        @pl.when(n == nc - 1)
        def _():
            d0 = dlb_ref[...] * lb * (1.0 - lb)
            dlg_ref[0:1, :] = d0
            dlg_ref[1:2, :] = -d0
            if ex:
                ex.finish(*xrefs)

    rev = lambda n: nc - 1 - n
    blk = lambda b: pl.BlockSpec((None, CHUNK, D_MODEL), functools.partial(lambda n, b: (b, rev(n), 0), b=b))
    row = pl.BlockSpec((CHUNK, D_MODEL), lambda n: (rev(n), 0))
    whole = lambda s: pl.BlockSpec(s, lambda n: (0,) * len(s))
    vec = (1, D_MODEL)
    res = pl.pallas_call(
        body,
        grid=(nc,),
        in_specs=[blk(SLOT_Q), blk(SLOT_F), blk(SLOT_I), blk(SLOT_OG), row, row,
                  pl.BlockSpec((None, HEADS, HEAD, HEAD), lambda n: (rev(n), 0, 0, 0)),
                  pl.BlockSpec(memory_space=pl.ANY), whole(logits.shape), whole(vec)] + (ex.in_specs if ex else []),
        out_specs=[pl.BlockSpec((4, CHUNK, D_MODEL), lambda n: (SLOT_Q // 4, rev(n), 0)), whole((2, D_MODEL)), whole(vec)]
        + (ex.out_specs if ex else []),
        out_shape=[jax.ShapeDtypeStruct(dh.shape, BF16), jax.ShapeDtypeStruct((2, D_MODEL), F32),
                   jax.ShapeDtypeStruct(vec, F32)] + (ex.out_shape if ex else []),
        scratch_shapes=[pltpu.VMEM((HEADS, HEAD, HEAD), F32), pltpu.VMEM(vec, F32)]
        + [pltpu.VMEM((CHUNK, D_MODEL), F32)] * 6 + (ex.scratch if ex else []),
        input_output_aliases={7: 0},
        compiler_params=_params(("arbitrary",)),
        name="hgrn_bwd",
    )(h, h, h, h, o_all, dy, states, dh, logits, g_norm, *(ex.arrays if ex else []))
    return res[0], res[1], res[2], res[3:]


def _merge_fwd(i, n, ga, gb, za, zb):
    return jax.nn.sigmoid(ga) * za + jax.nn.sigmoid(gb) * zb


def _merge_bwd(i, n, ga, gb, za, zb, dm):
    sa, sb = jax.nn.sigmoid(ga), jax.nn.sigmoid(gb)
    dgates = jnp.stack([(dm * za * sa * (1.0 - sa)).astype(BF16), (dm * zb * sb * (1.0 - sb)).astype(BF16)])
    return dgates, dm * sa, dm * sb


def _ln1_fwd(i, n, x, r1, g, b):
    xhat, _ = _ln_stats(ALPHA * x + r1)
    x1 = xhat * g + b
    return x1, x1


def _ln1_bwd(i, n, x, r1, dx1, g):
    xhat, rstd = _ln_stats(ALPHA * x + r1)
    dz = _ln_bwd(dx1, xhat, rstd, g)
    return dz, dz, _sum0(dx1 * xhat), _sum0(dx1)


def _ln2_loss(i, n, x1, fo, pg, pp, tgt, g, b):
    sg = jax.nn.sigmoid(pg)
    xhat, rstd = _ln_stats(ALPHA * x1 + fo + sg * pp)
    diff = xhat * g + b - tgt
    loss = 0.5 * jnp.sum(_mean(diff * diff), axis=0, keepdims=True)
    dy = diff * (1.0 / D_MODEL)
    dz = _ln_bwd(dy, xhat, rstd, g)
    return (dz, dz, dz * pp * sg * (1.0 - sg), dz * sg,
            jnp.broadcast_to(loss, (8, LANE)), _sum0(dy * xhat), _sum0(dy))


def _shift_down(cur, halo, tile):
    row = lax.broadcasted_iota(jnp.int32, cur.shape, 0)
    m1 = jnp.where(row == 0, halo[7:8, :], pltpu.roll(cur, 1, 0))
    m2 = jnp.where(row == 0, halo[6:7, :], jnp.where(row == 1, halo[7:8, :], pltpu.roll(cur, 2, 0)))
    return m1, m2


def _shift_up(cur, halo, tile):
    row = lax.broadcasted_iota(jnp.int32, cur.shape, 0)
    p1 = jnp.where(row == tile - 1, halo[0:1, :], pltpu.roll(cur, tile - 1, 0))
    p2 = jnp.where(row == tile - 2, halo[0:1, :], jnp.where(row == tile - 1, halo[1:2, :], pltpu.roll(cur, tile - 2, 0)))
    return p1, p2


def _conv_pre(i, gate, halo, w, b, tile):
    halo = jnp.where(i == 0, 0.0, halo)
    m1, m2 = _shift_down(gate, halo, tile)
    return w[0:1, :] * m2 + w[1:2, :] * m1 + w[2:3, :] * gate + b, m1, m2


def _conv_fwd(tile, i, n, gate, halo, val, w, b):
    cg, _, _ = _conv_pre(i, gate, halo, w, b, tile)
    return _gelu(cg) * val


def _conv_bwd_a(tile, i, n, gate, halo, val, dhid, w, b):
    cg, m1, m2 = _conv_pre(i, gate, halo, w, b, tile)
    dcg = dhid * val * _gelu_grad(cg)
    return dcg, dhid * _gelu(cg), _sum0(dcg * m2), _sum0(dcg * m1), _sum0(dcg * gate), _sum0(dcg)


def _conv_bwd_b(tile, i, n, dcg, halo, w):
    halo = jnp.where(i == n - 1, 0.0, halo)
    p1, p2 = _shift_up(dcg, halo, tile)
    return w[2:3, :] * dcg + w[1:2, :] * p1 + w[0:1, :] * p2


def _halo_spec(width, tile, t, nxt):
    per = tile // 8
    last = t // 8 - 1
    if nxt:
        return pl.BlockSpec((8, width), lambda i: (jnp.minimum((i + 1) * per, last), 0))
    return pl.BlockSpec((8, width), lambda i: (jnp.maximum(i * per - 1, 0), 0))


def _adamw(i, n, w, m, v, parts):
    g = parts[0].astype(F32)
    for j in range(1, N_DEV):
        g = g + parts[j].astype(F32)
    m_new = ADAM_B1 * m + (1.0 - ADAM_B1) * g
    v_new = ADAM_B2 * v + (1.0 - ADAM_B2) * (g * g)
    m_hat = m_new / (1.0 - ADAM_B1 ** ADAM_STEP)
    v_hat = v_new / (1.0 - ADAM_B2 ** ADAM_STEP)
    delta = -ADAM_LR * (m_hat / (jnp.sqrt(v_hat) + ADAM_EPS) + ADAM_WD * w)
    return g, delta, m_new, v_new


def _adam_call(w, m, v, parts, name):
    r, c = w.shape
    tile = _pick(r, (256, 128)) if r > 256 else r
    spec = pl.BlockSpec((N_DEV, tile, c), lambda i: (0, i, 0))
    return _rowwise(_adamw, [w, m, v, (parts, spec)], [], [(c, F32)] * 4, [], tile=tile, name=name)


def _peer(k):
    x, y, c = lax.axis_index("x"), lax.axis_index("y"), lax.axis_index("c")
    px = x ^ ((k >> 2) & 1)
    py = y ^ ((k >> 1) & 1)
    pc = c ^ (k & 1)
    return (px, py, pc), 4 * px + 2 * py + pc


def _my_index():
    return 4 * lax.axis_index("x") + 2 * lax.axis_index("y") + lax.axis_index("c")


class _Exchange:
    KINDS = ("gather", "gather+relay", "gather+relay+slots", "scatter", "scatter+slots")

    def __init__(self, entries):
        assert all(k in self.KINDS for _, k in entries), [k for _, k in entries]
        self.arrays = [a for a, _ in entries]
        self.scatter = [k.startswith("scatter") for _, k in entries]
        self.relayed = ["+relay" in k for _, k in entries]
        self.slots = ["+slots" in k for _, k in entries]
        self.n = len(entries)
        self.in_specs = [pl.BlockSpec(memory_space=pl.ANY)] * self.n
        self.out_specs = [pl.BlockSpec(memory_space=pl.ANY)] * self.n
        shapes = [tuple(a.shape[1:]) if sc else tuple(a.shape) for a, sc in zip(self.arrays, self.scatter)]
        self.out_shape = [jax.ShapeDtypeStruct((N_DEV,) + s, a.dtype) for s, a in zip(shapes, self.arrays)]
        per = N_DEV - 1
        self.scratch = [pltpu.SemaphoreType.DMA((self.n * per,)), pltpu.SemaphoreType.DMA((self.n * per,)),
                        pltpu.SemaphoreType.DMA((self.n,))]

    def _copies(self, srcs, outs, sems):
        send_sems, recv_sems, local_sems = sems
        x, y, c = lax.axis_index("x"), lax.axis_index("y"), lax.axis_index("c")
        me = _my_index()
        per = N_DEV - 1
        local, first, passed, relay_arrivals, arrivals = [], [], [], [], []
        for a in range(self.n):
            place = (lambda idx: _slot_of_group(idx)) if self.slots[a] else (lambda idx: idx)

            def copy(k, src, dst, dev, a=a):
                return pltpu.make_async_remote_copy(
                    src_ref=src, dst_ref=dst, send_sem=send_sems.at[a * per + k], recv_sem=recv_sems.at[a * per + k],
                    device_id=dev, device_id_type=pl.DeviceIdType.MESH)

            if self.scatter[a]:
                mine, land = srcs[a].at[place(me)], outs[a].at[me]
            else:
                mine, land = srcs[a], outs[a].at[place(me)]
            local.append(pltpu.make_async_copy(mine, land, local_sems.at[a]))
            if self.relayed[a]:
                block = lambda px, py, pc, a=a, place=place: outs[a].at[place(4 * px + 2 * py + pc)]
                chips = [(1 - x, y), (x, 1 - y), (1 - x, 1 - y)]
                first.append(copy(0, mine, land, (x, y, 1 - c)))
                arrivals.append(copy(0, mine, block(x, y, 1 - c), (x, y, 1 - c)))
                for j, (px, py) in enumerate(chips):
                    first.append(copy(1 + j, mine, land, (px, py, c)))
                    relay_arrivals.append(copy(1 + j, mine, block(px, py, c), (px, py, c)))
                    passed.append(copy(4 + j, block(px, py, c), block(px, py, c), (x, y, 1 - c)))
                    arrivals.append(copy(4 + j, mine, block(px, py, 1 - c), (x, y, 1 - c)))
                continue
            for k in range(1, N_DEV):
                dev, idx = _peer(k)
                if self.scatter[a]:
                    first.append(copy(k - 1, srcs[a].at[place(idx)], land, dev))
                    arrivals.append(copy(k - 1, mine, outs[a].at[idx], dev))
                else:
                    first.append(copy(k - 1, mine, land, dev))
                    arrivals.append(copy(k - 1, mine, outs[a].at[place(idx)], dev))
        return local, first, passed, relay_arrivals, arrivals

    def start(self, srcs, outs, sems):
        local, first, _, _, _ = self._copies(srcs, outs, sems)
        for cp in local + first:
            cp.start()

    def relay(self, srcs, outs, sems):
        _, _, passed, relay_arrivals, _ = self._copies(srcs, outs, sems)
        for landed, onward in zip(relay_arrivals, passed):
            landed.wait_recv()
            onward.start()

    def finish(self, srcs, outs, sems):
        local, first, passed, _, arrivals = self._copies(srcs, outs, sems)
        for cp in arrivals:
            cp.wait_recv()
        for cp in first + passed:
            cp.wait_send()
        for cp in local:
            cp.wait()


def _exchange(entries, name):
    ex = _Exchange(entries)
    n = ex.n

    def body(*refs):
        xrefs = refs[:n], refs[n:2 * n], refs[2 * n:]
        ex.start(*xrefs)
        ex.relay(*xrefs)
        ex.finish(*xrefs)

    return pl.pallas_call(body, in_specs=ex.in_specs, out_specs=ex.out_specs, out_shape=ex.out_shape,
                          scratch_shapes=ex.scratch, name=name)(*ex.arrays)


def _local_step(x, p, tgt, w_in, small, comm):
    t = x.shape[0]
    tile = _pick(t, (256, 128))
    d = D_MODEL
    act_b, act_f = (d, BF16), (d, F32)
    x_b, p_b = x.astype(BF16), p.astype(BF16)

    chunk_id = jnp.arange(SGU_BLOCK) // CHUNK
    mask = chunk_id[:, None] >= chunk_id[None, :]
    wm = jnp.where(mask[None], small["sgu_w_s"], 0.0)
    wm_b = wm.astype(BF16)
    wm_t = jnp.swapaxes(wm, 1, 2).astype(BF16)
    bs_t = small["sgu_b_s"].T

    h = _mm(x_b, w_in, out_dtype=F32, name="mm_h")
    y_a = _sgu_fwd(h, wm_b, bs_t, small["sgu_norm_g"], small["sgu_norm_b"])
    y_b, o_all, states, got = _hgrn_fwd(h, small["lb_logits"], small["hgrn_norm_g"], ex=comm.weights_exchange())
    wts, conv_w = comm.weights(got)
    z_a = _mm(y_a, wts["w_a"], out_dtype=F32, name="mm_za")
    z_b = _mm(y_b, wts["w_b"], out_dtype=F32, name="mm_zb")
    gates = [(h, SLOT_GA), (h, SLOT_GB)]
    merged, = _rowwise(_merge_fwd, gates + [z_a, z_b], [], [act_b], [], tile=tile, name="merge_fwd")
    r1 = _mm(merged, wts["w_o"], out_dtype=F32, name="mm_r1")
    x1, x1_b = _rowwise(_ln1_fwd, [x, r1], [small["ln1_g"], small["ln1_b"]], [act_f, act_b], [], tile=tile, name="ln1_fwd")
    gate = _mm(x1_b, wts["w_g"], out_dtype=F32, name="mm_gate")
    val = _mm(x1_b, wts["w_v"], out_dtype=F32, name="mm_val")
    pg = _mm(x1_b, wts["w_pg"], out_dtype=F32, name="mm_pg")
    pp = _mm(p_b, wts["w_pp"], out_dtype=F32, name="mm_pp")
    hid, = _rowwise(functools.partial(_conv_fwd, tile), [gate, (gate, _halo_spec(D_FF, tile, t, False)), val],
                    [conv_w, small["conv_b"]], [(D_FF, BF16)], [], tile=tile, name="conv_fwd")
    fo = _mm(hid, wts["w_down"], out_dtype=F32, name="mm_down")
    dz2, dz2_b, dpg, dpp, loss, dg2, db2 = _rowwise(
        _ln2_loss, [x1, fo, pg, pp, tgt], [small["ln2_g"], small["ln2_b"]],
        [act_f, act_b, act_b, act_b], [(8, LANE), (1, d), (1, d)], tile=tile, name="ln2_loss")

    dhid = _mm(dz2_b, wts["w_down"], out_dtype=F32, name="mm_dhid", trans_b=True)
    g_down = _mm_tn(hid, dz2_b, out_dtype=BF16, name="mm_g_down")
    dcg, dval, dcw0, dcw1, dcw2, dcb = _rowwise(
        functools.partial(_conv_bwd_a, tile), [gate, (gate, _halo_spec(D_FF, tile, t, False)), val, dhid],
        [conv_w, small["conv_b"]], [(D_FF, F32), (D_FF, BF16)], [(1, D_FF)] * 4, tile=tile, name="conv_bwd_a")
    dgate, = _rowwise(functools.partial(_conv_bwd_b, tile), [dcg, (dcg, _halo_spec(D_FF, tile, t, True))],
                      [conv_w], [(D_FF, BF16)], [], tile=tile, name="conv_bwd_b")
    g_g = _mm_tn(x1_b, dgate, out_dtype=BF16, name="mm_g_gate")
    g_v = _mm_tn(x1_b, dval, out_dtype=BF16, name="mm_g_val")
    g_pg = _mm_tn(x1_b, dpg, out_dtype=BF16, name="mm_g_pg")
    g_pp = _mm_tn(p_b, dpp, out_dtype=BF16, name="mm_g_pp")
    dx1 = _mm(dgate, wts["w_g"], out_dtype=F32, name="mm_dx1_gate", trans_b=True, adds=[(dz2, ALPHA)])
    dx1 = _mm(dval, wts["w_v"], out_dtype=F32, name="mm_dx1_val", trans_b=True, adds=[(dx1, 1.0)])
    dx1 = _mm(dpg, wts["w_pg"], out_dtype=F32, name="mm_dx1_pg", trans_b=True, adds=[(dx1, 1.0)])
    dz1, dz1_b, dg1, db1 = _rowwise(_ln1_bwd, [x, r1, dx1], [small["ln1_g"]], [act_f, act_b], [(1, d), (1, d)],
                                    tile=tile, name="ln1_bwd")
    g_o = _mm_tn(merged, dz1_b, out_dtype=BF16, name="mm_g_o")
    dm = _mm(dz1_b, wts["w_o"], out_dtype=F32, name="mm_dm", trans_b=True)
    dh, dza, dzb = _rowwise(_merge_bwd, gates + [z_a, z_b, dm], [],
                            [("stack", 2, SLOT_GA // 2, 8, d, BF16), act_b, act_b], [], tile=tile, name="merge_bwd")
    g_a = _mm_tn(y_a, dza, out_dtype=BF16, name="mm_g_a")
    g_b = _mm_tn(y_b, dzb, out_dtype=BF16, name="mm_g_b")
    dy_a = _mm(dza, wts["w_a"], out_dtype=F32, name="mm_dya", trans_b=True)
    dy_b = _mm(dzb, wts["w_b"], out_dtype=F32, name="mm_dyb", trans_b=True)
    dh, dws, dbs, dgv_n, dbv_n = _sgu_bwd(h, dy_a, dh, wm_b, wm_t, bs_t, small["sgu_norm_g"], small["sgu_norm_b"])
    big = dict(w_a=g_a, w_b=g_b, w_o=g_o, w_g=g_g, w_v=g_v, w_down=g_down, w_pp=g_pp, w_pg=g_pg)
    sm = dict(sgu_w_s=jnp.where(mask[None], dws, 0.0), sgu_b_s=dbs[:, :GROUPS].T, sgu_norm_g=dgv_n, sgu_norm_b=dbv_n,
              ln1_g=dg1, ln1_b=db1, conv_w=jnp.concatenate([dcw0, dcw1, dcw2], axis=0), conv_b=dcb, ln2_g=dg2, ln2_b=db2,
              loss=loss)
    dh, dlogits, dgn, got = _hgrn_bwd(h, o_all, dy_b, states, dh, small["lb_logits"], small["hgrn_norm_g"],
                                      ex=comm.grads_exchange(big, sm))
    comm.grads_done(got)
    g_in = _mm_tn(x_b, dh, out_dtype=BF16, name="mm_g_in")
    ex = comm.last_exchange(g_in, dict(lb_logits=dlogits, hgrn_norm_g=dgn))
    res = _mm(dh, w_in, out_dtype=F32, name="mm_dx", trans_b=True, reduce_b=True, adds=[(dz1, ALPHA)], ex=ex)
    grad_x, got = res if ex else (res, ())
    comm.last_done(got)
    return grad_x


_SMALL_EARLY = ["sgu_w_s", "sgu_b_s", "sgu_norm_g", "sgu_norm_b", "ln1_g", "ln1_b", "ffn_conv_b", "ln2_g", "ln2_b"]
_SMALL_LATE = ["hgrn_lb_logits", "hgrn_norm_g"]
N_TAPS = D_FF // N_DEV
UP_COLS = 2 * D_FF // N_DEV


class _StepExchanges:
    def __init__(self, shards):
        self.shards = shards

    def weights_exchange(self):
        return _Exchange([(s, "gather+relay") for s in self.shards])

    def weights(self, got):
        d, f = D_MODEL, D_FF
        w_br_g, w_o_g, w_up_g, w_down_g, w_pp_g, w_pg_g, conv_g = got
        w_br = w_br_g.transpose(1, 0, 2, 3).reshape(2, d, d)
        w_up = w_up_g.transpose(1, 0, 2).reshape(d, 2, f).transpose(1, 0, 2)
        wts = dict(w_a=w_br[0], w_b=w_br[1], w_o=w_o_g.reshape(d, d), w_g=w_up[0], w_v=w_up[1],
                   w_down=w_down_g.reshape(f, d), w_pp=w_pp_g.transpose(1, 0, 2).reshape(256, d), w_pg=w_pg_g.reshape(d, d))
        return wts, conv_g.transpose(1, 0, 2).reshape(3, f)

    def grads_exchange(self, big, sm):
        d = D_MODEL
        parts = [jnp.stack([big["w_a"], big["w_b"]]).reshape(2, N_DEV, 128, d).transpose(1, 0, 2, 3),
                 big["w_o"].reshape(N_DEV, 128, d),
                 jnp.concatenate([big["w_g"], big["w_v"]], axis=1).reshape(d, N_DEV, UP_COLS).transpose(1, 0, 2),
                 big["w_down"].reshape(N_DEV, N_TAPS, d),
                 big["w_pp"].reshape(256, N_DEV, 128).transpose(1, 0, 2),
                 big["w_pg"].reshape(N_DEV, 128, d)]
        packed, self.rows_early = _pack([sm[k] for k in ("sgu_w_s", "sgu_b_s", "sgu_norm_g", "sgu_norm_b", "ln1_g", "ln1_b",
                                                         "conv_b", "ln2_g", "ln2_b", "conv_w", "loss")])
        return _Exchange([(a, "scatter") for a in parts] + [(packed, "gather")])

    def grads_done(self, got):
        self.recv, self.small_early = got[:6], got[6]

    def last_exchange(self, g_in, sm):
        packed, self.rows_late = _pack([sm["lb_logits"], sm["hgrn_norm_g"]])
        return _Exchange([(g_in, "scatter+slots"), (packed, "gather")])

    def last_done(self, got):
        self.recv_in, self.small_late = got


def _rows128(a):
    flat = a.reshape(-1)
    rows = -(-flat.shape[0] // (8 * LANE)) * 8
    return jnp.pad(flat, (0, rows * LANE - flat.shape[0])).reshape(rows, LANE)


def _pack(parts):
    blocks = [_rows128(a) for a in parts]
    return jnp.concatenate(blocks, axis=0), [b.shape[0] for b in blocks]


def _unpack(packed, shapes, rows):
    out, r0 = [], 0
    for shp, r in zip(shapes, rows):
        n = math.prod(shp)
        out.append(packed[r0:r0 + r].reshape(-1)[:n].reshape(shp))
        r0 += r
    return out


def kernel(x, p, w_in, sgu_w_s, sgu_b_s, sgu_norm_g, sgu_norm_b, hgrn_lb_logits, hgrn_norm_g, w_branch, w_out, ln1_g, ln1_b, ffn_w_up, ffn_conv_w, ffn_conv_b, ffn_w_down, ln2_g, ln2_b, ple_w_proj, ple_w_gate, loss_target, m_w_in, m_sgu_w_s, m_sgu_b_s, m_sgu_norm_g, m_sgu_norm_b, m_hgrn_lb_logits, m_hgrn_norm_g, m_w_branch, m_w_out, m_ln1_g, m_ln1_b, m_ffn_w_up, m_ffn_conv_w, m_ffn_conv_b, m_ffn_w_down, m_ln2_g, m_ln2_b, m_ple_w_proj, m_ple_w_gate, v_w_in, v_sgu_w_s, v_sgu_b_s, v_sgu_norm_g, v_sgu_norm_b, v_hgrn_lb_logits, v_hgrn_norm_g, v_w_branch, v_w_out, v_ln1_g, v_ln1_b, v_ffn_w_up, v_ffn_conv_w, v_ffn_conv_b, v_ffn_w_down, v_ln2_g, v_ln2_b, v_ple_w_proj, v_ple_w_gate):
    weights = dict(w_in=w_in, sgu_w_s=sgu_w_s, sgu_b_s=sgu_b_s, sgu_norm_g=sgu_norm_g, sgu_norm_b=sgu_norm_b,
                   hgrn_lb_logits=hgrn_lb_logits, hgrn_norm_g=hgrn_norm_g, w_branch=w_branch, w_out=w_out,
                   ln1_g=ln1_g, ln1_b=ln1_b, ffn_w_up=ffn_w_up, ffn_conv_w=ffn_conv_w, ffn_conv_b=ffn_conv_b,
                   ffn_w_down=ffn_w_down, ln2_g=ln2_g, ln2_b=ln2_b, ple_w_proj=ple_w_proj, ple_w_gate=ple_w_gate)
    mom_m = dict(w_in=m_w_in, sgu_w_s=m_sgu_w_s, sgu_b_s=m_sgu_b_s, sgu_norm_g=m_sgu_norm_g, sgu_norm_b=m_sgu_norm_b,
                 hgrn_lb_logits=m_hgrn_lb_logits, hgrn_norm_g=m_hgrn_norm_g, w_branch=m_w_branch, w_out=m_w_out,
                 ln1_g=m_ln1_g, ln1_b=m_ln1_b, ffn_w_up=m_ffn_w_up, ffn_conv_w=m_ffn_conv_w, ffn_conv_b=m_ffn_conv_b,
                 ffn_w_down=m_ffn_w_down, ln2_g=m_ln2_g, ln2_b=m_ln2_b, ple_w_proj=m_ple_w_proj, ple_w_gate=m_ple_w_gate)
    mom_v = dict(w_in=v_w_in, sgu_w_s=v_sgu_w_s, sgu_b_s=v_sgu_b_s, sgu_norm_g=v_sgu_norm_g, sgu_norm_b=v_sgu_norm_b,
                 hgrn_lb_logits=v_hgrn_lb_logits, hgrn_norm_g=v_hgrn_norm_g, w_branch=v_w_branch, w_out=v_w_out,
                 ln1_g=v_ln1_g, ln1_b=v_ln1_b, ffn_w_up=v_ffn_w_up, ffn_conv_w=v_ffn_conv_w, ffn_conv_b=v_ffn_conv_b,
                 ffn_w_down=v_ffn_w_down, ln2_g=v_ln2_g, ln2_b=v_ln2_b, ple_w_proj=v_ple_w_proj, ple_w_gate=v_ple_w_gate)
    d, f = D_MODEL, D_FF
    me = _my_index()

    w_in_g, = _exchange([(w_in[0].astype(BF16), "gather+relay+slots")], "gather_w_in")
    comm = _StepExchanges([w_branch[0].astype(BF16), w_out[0].astype(BF16), ffn_w_up[0].astype(BF16),
                           ffn_w_down[0].astype(BF16), ple_w_proj[0].astype(BF16), ple_w_gate[0].astype(BF16), ffn_conv_w[0]])
    small = dict(sgu_w_s=sgu_w_s[0], sgu_b_s=sgu_b_s[0], sgu_norm_g=sgu_norm_g, sgu_norm_b=sgu_norm_b,
                 lb_logits=hgrn_lb_logits, hgrn_norm_g=hgrn_norm_g, ln1_g=ln1_g, ln1_b=ln1_b, ln2_g=ln2_g, ln2_b=ln2_b,
                 conv_b=ffn_conv_b)
    grad_x = _local_step(x[0], p[0, 0], loss_target[0], w_in_g, small, comm)

    out = {}

    def adam(name, parts8, shape2d):
        w2, m2, v2 = (a.reshape(shape2d) for a in (weights[name], mom_m[name], mom_v[name]))
        res = _adam_call(w2, m2, v2, parts8.reshape((N_DEV,) + shape2d), "adam_" + name)
        out[name] = tuple(r.reshape(weights[name].shape) for r in res)

    adam("w_in", comm.recv_in, (d, d))
    adam("w_branch", comm.recv[0], (256, d))
    adam("w_out", comm.recv[1], (128, d))
    adam("ffn_w_up", comm.recv[2], (d, UP_COLS))
    adam("ffn_w_down", comm.recv[3], (N_TAPS, d))
    adam("ple_w_proj", comm.recv[4], (256, 128))
    adam("ple_w_gate", comm.recv[5], (128, d))

    def adam_small(names, extra_w, extra_m, extra_v, extra_shapes, parts8, rows, label):
        pk = lambda src, extra: _pack([src[n] for n in names] + extra)[0]
        res = _adam_call(pk(weights, extra_w), pk(mom_m, extra_m), pk(mom_v, extra_v), parts8, label)
        shapes = [weights[n].shape for n in names] + extra_shapes
        unpacked = [_unpack(r, shapes, rows) for r in res]
        for j, n in enumerate(names):
            out[n] = tuple(u[j] for u in unpacked)
        return [[u[len(names) + j] for u in unpacked] for j in range(len(extra_shapes))]

    blank = jnp.zeros((8, LANE), F32)
    taps, loss_rows = adam_small(
        _SMALL_EARLY, [_place_taps(ffn_conv_w[0], me, f), blank], [_place_taps(m_ffn_conv_w[0], me, f), blank],
        [_place_taps(v_ffn_conv_w[0], me, f), blank + 1.0], [(3, f), (8, LANE)], comm.small_early, comm.rows_early,
        "adam_small_early")
    adam_small(_SMALL_LATE, [], [], [], [], comm.small_late, comm.rows_late, "adam_small_late")
    out["ffn_conv_w"] = tuple(lax.dynamic_slice_in_dim(u, me * N_TAPS, N_TAPS, axis=1)[None] for u in taps)
    loss = loss_rows[0][0, 0]

    order = ["w_in", "sgu_w_s", "sgu_b_s", "sgu_norm_g", "sgu_norm_b", "hgrn_lb_logits", "hgrn_norm_g", "w_branch", "w_out",
             "ln1_g", "ln1_b", "ffn_w_up", "ffn_conv_w", "ffn_conv_b", "ffn_w_down", "ln2_g", "ln2_b", "ple_w_proj", "ple_w_gate"]
    return (loss, grad_x[None], *[out[n][0] for n in order], *[out[n][1] for n in order],
            *[out[n][2] for n in order], *[out[n][3] for n in order])


def _place_taps(shard, me, f):
    return lax.dynamic_update_slice_in_dim(jnp.zeros((3, f), F32), shard, me * N_TAPS, axis=1)
```

```python
import functools
import math

import jax
import jax.numpy as jnp
from jax import lax
from jax.experimental import pallas as pl
from jax.experimental.pallas import tpu as pltpu

F32 = jnp.float32
BF16 = jnp.bfloat16

N_DEV = 8
D_MODEL = 1024
CHUNK = 64
SUB = 16
SGU_BLOCK = 128
GROUPS = 8
HEAD = 128
HEADS = 8
HEAD_UNROLL = 4
D_FF = 2816
LN_EPS = 1e-5
RMS_EPS = 1e-6
ALPHA = 2.0 ** 0.25
GELU_K = math.sqrt(2.0 / math.pi)
GELU_C = 0.044715
NEG = -1e30
ADAM_LR, ADAM_B1, ADAM_B2, ADAM_EPS, ADAM_WD, ADAM_STEP = 0.001, 0.9, 0.999, 1e-08, 0.01, 10
LANE = 128
SLOT_Q, SLOT_F, SLOT_I, SLOT_OG, SLOT_U, SLOT_V, SLOT_GA, SLOT_GB = range(8)


def _slot_of_group(k):
    return jnp.where(k < 2, k + 4, jnp.where(k < 6, k - 2, k))


MIB = 1024 * 1024
VMEM_V7X = 64 * MIB
VMEM_FLOOR = 32 * MIB
MM_TILES = (1024, 1408, 512, 256, 128)


def _params(sem, need=0):
    limit = min(max(need + need // 4, VMEM_FLOOR), VMEM_V7X - 4 * MIB)
    return pltpu.CompilerParams(dimension_semantics=sem, vmem_limit_bytes=limit)


def _in_hbm(arrays):
    return [pltpu.with_memory_space_constraint(a, pltpu.HBM) for a in arrays]


def _pick(n, prefs):
    for t in prefs:
        if n % t == 0:
            return t
    return n


def _gelu(x):
    return 0.5 * x * (1.0 + jnp.tanh(GELU_K * (x + GELU_C * x * x * x)))


def _gelu_grad(x):
    t = jnp.tanh(GELU_K * (x + GELU_C * x * x * x))
    return 0.5 * (1.0 + t) + 0.5 * x * (1.0 - t * t) * GELU_K * (1.0 + 3.0 * GELU_C * x * x)


def _silu_grad(x, s):
    return s * (1.0 + x * (1.0 - s))


def _dot(a, b):
    return jnp.dot(a.astype(BF16), b.astype(BF16), preferred_element_type=F32)


def _dot_nt(a, b):
    return lax.dot_general(a.astype(BF16), b.astype(BF16), (((1,), (1,)), ((), ())), preferred_element_type=F32)


def _dot_tn(a, b):
    return lax.dot_general(a.astype(BF16), b.astype(BF16), (((0,), (0,)), ((), ())), preferred_element_type=F32)


def _mean(x):
    return jnp.mean(x, axis=-1, keepdims=True)


def _sum0(x):
    return jnp.sum(x, axis=0, keepdims=True)


def _mm(a, b, *, out_dtype, name, trans_b=False, reduce_b=False, adds=(), ex=None):
    squeeze = b.ndim == 2
    a3 = a if a.ndim == 3 else a[None]
    b3 = b if b.ndim == 3 else b[None]
    ba, m, k = a3.shape
    bb = b3.shape[0]
    n = b3.shape[1] if trans_b else b3.shape[2]
    tm = _pick(m, MM_TILES)
    tn = _pick(n, MM_TILES)
    if reduce_b:
        bo, steps = 1, bb
        a_map = lambda o, i, j, r: (r if ba > 1 else 0, i, 0)
        b_map = (lambda o, i, j, r: (r, j, 0)) if trans_b else (lambda o, i, j, r: (r, 0, j))
    else:
        bo, steps = bb, 1
        a_map = lambda o, i, j, r: (o if ba > 1 else 0, i, 0)
        b_map = (lambda o, i, j, r: (o, j, 0)) if trans_b else (lambda o, i, j, r: (o, 0, j))
    o_map = lambda o, i, j, r: (o, i, j)
    add_arrays = [x if x.ndim == 3 else x[None] for x, _ in adds]
    add_scales = [s for _, s in adds]
    n_add = len(adds)
    dot = _dot_nt if trans_b else _dot

    def finish(acc, add_refs, o_ref):
        for ref, s in zip(add_refs, add_scales):
            acc = acc + s * ref[...].astype(F32)
        o_ref[...] = acc.astype(o_ref.dtype)

    grid = (bo, m // tm, n // tn, steps)

    def body(*refs):
        ins, (o_ref,), scratch, xrefs = _split_refs(refs, 2 + n_add, 1, 1 if reduce_b else 0, ex)
        a_ref, b_ref, add_refs = ins[0], ins[1], ins[2:]
        step = ((pl.program_id(0) * grid[1] + pl.program_id(1)) * grid[2] + pl.program_id(2)) * grid[3] + pl.program_id(3)
        if ex:
            @pl.when(step == 0)
            def _():
                ex.start(*xrefs)

        if reduce_b:
            acc, = scratch
            r = pl.program_id(3)

            @pl.when(r == 0)
            def _():
                acc[...] = jnp.zeros_like(acc)

            acc[...] += dot(a_ref[...], b_ref[...])

            @pl.when(r == steps - 1)
            def _():
                finish(acc[...], add_refs, o_ref)
        else:
            finish(dot(a_ref[...], b_ref[...]), add_refs, o_ref)

        if ex:
            @pl.when(step == math.prod(grid) - 1)
            def _():
                ex.finish(*xrefs)

    b_block = (None, tn, k) if trans_b else (None, k, tn)
    out_bytes = tm * tn * jnp.dtype(out_dtype).itemsize
    need = 2 * (tm * k * a3.dtype.itemsize + k * tn * b3.dtype.itemsize + out_bytes + n_add * tm * tn * 4)
    need += 2 * tm * tn * 4
    sem = ("arbitrary",) * 4 if ex else ("parallel", "parallel", "parallel", "arbitrary")
    res = pl.pallas_call(
        body,
        grid=grid,
        in_specs=[pl.BlockSpec((None, tm, k), a_map), pl.BlockSpec(b_block, b_map)]
        + [pl.BlockSpec((None, tm, tn), o_map) for _ in adds] + (ex.in_specs if ex else []),
        out_specs=[pl.BlockSpec((None, tm, tn), o_map)] + (ex.out_specs if ex else []),
        out_shape=[pltpu.HBM((bo, m, n), out_dtype)] + (ex.out_shape if ex else []),
        scratch_shapes=([pltpu.VMEM((tm, tn), F32)] if reduce_b else []) + (ex.scratch if ex else []),
        compiler_params=_params(sem, need),
        name=name,
    )(*_in_hbm([a3, b3, *add_arrays]), *(ex.arrays if ex else []))
    out = res[0][0] if (reduce_b or squeeze) else res[0]
    return (out, res[1:]) if ex else out


def _mm_tn(a, b, *, out_dtype, name):
    squeeze = b.ndim == 2
    b3 = b if b.ndim == 3 else b[None]
    t, m = a.shape
    bb, _, n = b3.shape
    tm = _pick(m, MM_TILES)
    tn = _pick(n, MM_TILES)
    tt = _pick(t, (1024, 512, 256, 128))
    steps = t // tt
    need = 2 * (tt * tm * a.dtype.itemsize + tt * tn * b3.dtype.itemsize + tm * tn * jnp.dtype(out_dtype).itemsize)
    need += 2 * tm * tn * 4

    def body(a_ref, b_ref, o_ref, acc):
        r = pl.program_id(3)

        @pl.when(r == 0)
        def _():
            acc[...] = jnp.zeros_like(acc)

        acc[...] += _dot_tn(a_ref[...], b_ref[...])

        @pl.when(r == steps - 1)
        def _():
            o_ref[...] = acc[...].astype(o_ref.dtype)

    out = pl.pallas_call(
        body,
        grid=(bb, m // tm, n // tn, steps),
        in_specs=[pl.BlockSpec((tt, tm), lambda o, i, j, r: (r, i)),
                  pl.BlockSpec((None, tt, tn), lambda o, i, j, r: (o, r, j))],
        out_specs=pl.BlockSpec((None, tm, tn), lambda o, i, j, r: (o, i, j)),
        out_shape=pltpu.HBM((bb, m, n), out_dtype),
        scratch_shapes=[pltpu.VMEM((tm, tn), F32)],
        compiler_params=_params(("parallel", "parallel", "parallel", "arbitrary"), need),
        name=name,
    )(*_in_hbm([a, b3]))
    return out[0] if squeeze else out


def _rowwise(fn, rows, consts, row_outs, acc_outs, *, tile, name):
    first = rows[0][0] if isinstance(rows[0], tuple) else rows[0]
    t = first.shape[-2]
    steps = t // tile
    arrays, in_specs = [], []
    for r in rows:
        if isinstance(r, tuple) and isinstance(r[1], pl.BlockSpec):
            arrays.append(r[0])
            in_specs.append(r[1])
        elif isinstance(r, tuple):
            arr, bidx = r
            arrays.append(arr)
            in_specs.append(pl.BlockSpec((None, tile, arr.shape[-1]), functools.partial(lambda i, b: (b, i, 0), b=bidx)))
        else:
            arrays.append(r)
            in_specs.append(pl.BlockSpec((tile, r.shape[-1]), lambda i: (i, 0)))
    for c in consts:
        arrays.append(c)
        in_specs.append(pl.BlockSpec(c.shape, lambda i: (0, 0)))
    n_in, n_row = len(arrays), len(row_outs)
    out_shape, out_specs = [], []
    for ro in row_outs:
        if ro[0] == "stack":
            _, cnt, blk, total, w, dt = ro
            out_shape.append(pltpu.HBM((total, t, w), dt))
            out_specs.append(pl.BlockSpec((cnt, tile, w), functools.partial(lambda i, b: (b, i, 0), b=blk)))
        else:
            w, dt = ro
            out_shape.append(pltpu.HBM((t, w), dt))
            out_specs.append(pl.BlockSpec((tile, w), lambda i: (i, 0)))
    out_shape += [jax.ShapeDtypeStruct(s, F32) for s in acc_outs]
    out_specs += [pl.BlockSpec(s, lambda i: (0, 0)) for s in acc_outs]
    blocks = [math.prod(d for d in sp.block_shape if d) * arr.dtype.itemsize for sp, arr in zip(in_specs, arrays)]
    blocks += [math.prod(d for d in sp.block_shape if d) * jnp.dtype(sh.dtype).itemsize
               for sp, sh in zip(out_specs, out_shape)]
    need = 2 * sum(blocks) + 6 * tile * max(a.shape[-1] for a in arrays) * 4

    def body(*refs):
        ins, outs = refs[:n_in], refs[n_in:]
        i = pl.program_id(0)
        res = fn(i, steps, *[r[...] for r in ins])
        res = res if isinstance(res, (tuple, list)) else (res,)
        for ref, val in zip(outs[:n_row], res[:n_row]):
            ref[...] = val.astype(ref.dtype)
        if acc_outs:
            @pl.when(i == 0)
            def _():
                for ref in outs[n_row:]:
                    ref[...] = jnp.zeros_like(ref)

            for ref, val in zip(outs[n_row:], res[n_row:]):
                ref[...] += val

    return pl.pallas_call(
        body,
        grid=(steps,),
        in_specs=in_specs,
        out_specs=out_specs,
        out_shape=out_shape,
        compiler_params=_params(("arbitrary",), need),
        name=name,
    )(*_in_hbm(arrays))


def _ln_stats(z):
    mu = _mean(z)
    zc = z - mu
    rstd = lax.rsqrt(_mean(zc * zc) + LN_EPS)
    return zc * rstd, rstd


def _ln_bwd(dy, xhat, rstd, g):
    dxh = dy * g
    return rstd * (dxh - _mean(dxh) - xhat * _mean(dxh * xhat))


def _sgu_fwd(h, wm, bs_t, g_v, b_v):
    t = h.shape[1]

    def body(u_ref, v_ref, wm_ref, bs_ref, g_ref, b_ref, y_ref):
        xhat, _ = _ln_stats(_gelu(v_ref[...]))
        vn = (xhat * g_ref[...] + b_ref[...]).astype(BF16)
        gu = _gelu(u_ref[...])
        for g in range(GROUPS):
            sl = slice(g * HEAD, (g + 1) * HEAD)
            mixed = _dot(wm_ref[g], vn[:, sl]) + bs_ref[:, g:g + 1]
            y_ref[:, sl] = (gu[:, sl] * mixed).astype(BF16)

    blk = lambda b: pl.BlockSpec((None, SGU_BLOCK, D_MODEL), functools.partial(lambda i, b: (b, i, 0), b=b))
    whole = lambda s: pl.BlockSpec(s, lambda i: (0,) * len(s))
    return pl.pallas_call(
        body,
        grid=(t // SGU_BLOCK,),
        in_specs=[blk(SLOT_U), blk(SLOT_V), whole(wm.shape), whole(bs_t.shape), whole(g_v.shape), whole(b_v.shape)],
        out_specs=pl.BlockSpec((SGU_BLOCK, D_MODEL), lambda i: (i, 0)),
        out_shape=pltpu.HBM((t, D_MODEL), BF16),
        compiler_params=_params(("parallel",)),
        name="sgu_fwd",
    )(*_in_hbm([h, h]), wm, bs_t, g_v, b_v)


def _sgu_bwd(h, dy, dh, wm, wm_t, bs_t, g_v, b_v):
    t = h.shape[1]

    def body(u_ref, v_ref, dy_ref, dh_in, wm_ref, wmt_ref, bs_ref, g_ref, b_ref,
             duv_ref, dw_ref, dbs_ref, dg_ref, db_ref, dvn_ref):
        del dh_in
        du_ref, dv_ref = duv_ref.at[0], duv_ref.at[1]
        i = pl.program_id(0)

        @pl.when(i == 0)
        def _():
            dw_ref[...] = jnp.zeros_like(dw_ref)
            dbs_ref[...] = jnp.zeros_like(dbs_ref)
            dg_ref[...] = jnp.zeros_like(dg_ref)
            db_ref[...] = jnp.zeros_like(db_ref)

        u = u_ref[...]
        v = v_ref[...]
        xhat, rstd = _ln_stats(_gelu(v))
        vn = (xhat * g_ref[...] + b_ref[...]).astype(BF16)
        gu = _gelu(u)
        gup = _gelu_grad(u)
        lane = lax.broadcasted_iota(jnp.int32, (SGU_BLOCK, LANE), 1)
        dbs = jnp.zeros((SGU_BLOCK, LANE), F32)
        for g in range(GROUPS):
            sl = slice(g * HEAD, (g + 1) * HEAD)
            vn_g = vn[:, sl]
            mixed = _dot(wm_ref[g], vn_g) + bs_ref[:, g:g + 1]
            dy_g = dy_ref[:, sl]
            du_ref[:, sl] = (dy_g * mixed * gup[:, sl]).astype(BF16)
            dmix = dy_g * gu[:, sl]
            dmb = dmix.astype(BF16)
            dvn_ref[:, sl] = _dot(wmt_ref[g], dmb)
            dw_ref[g] += _dot_nt(dmb, vn_g)
            dbs = dbs + jnp.where(lane == g, jnp.sum(dmix, axis=1, keepdims=True), 0.0)
        dbs_ref[...] += dbs
        dvn = dvn_ref[...]
        dg_ref[...] += _sum0(dvn * xhat)
        db_ref[...] += _sum0(dvn)
        dv_ref[...] = (_ln_bwd(dvn, xhat, rstd, g_ref[...]) * _gelu_grad(v)).astype(BF16)

    blk = lambda b: pl.BlockSpec((None, SGU_BLOCK, D_MODEL), functools.partial(lambda i, b: (b, i, 0), b=b))
    row = pl.BlockSpec((SGU_BLOCK, D_MODEL), lambda i: (i, 0))
    whole = lambda s: pl.BlockSpec(s, lambda i: (0,) * len(s))
    vec = (1, D_MODEL)
    return pl.pallas_call(
        body,
        grid=(t // SGU_BLOCK,),
        in_specs=[blk(SLOT_U), blk(SLOT_V), row, pl.BlockSpec(memory_space=pl.ANY),
                  whole(wm.shape), whole(wm_t.shape), whole(bs_t.shape), whole(vec), whole(vec)],
        out_specs=[pl.BlockSpec((2, SGU_BLOCK, D_MODEL), lambda i: (SLOT_U // 2, i, 0)),
                   whole(wm.shape), whole((SGU_BLOCK, LANE)), whole(vec), whole(vec)],
        out_shape=[pltpu.HBM(dh.shape, BF16),
                   jax.ShapeDtypeStruct(wm.shape, F32), jax.ShapeDtypeStruct((SGU_BLOCK, LANE), F32),
                   jax.ShapeDtypeStruct(vec, F32), jax.ShapeDtypeStruct(vec, F32)],
        scratch_shapes=[pltpu.VMEM((SGU_BLOCK, D_MODEL), F32)],
        input_output_aliases={3: 0},
        compiler_params=_params(("arbitrary",)),
        name="sgu_bwd",
    )(*_in_hbm([h, h, dy, dh]), wm, wm_t, bs_t, g_v, b_v)


def _split3(x):
    hi = x.astype(BF16)
    r1 = x - hi.astype(F32)
    mid = r1.astype(BF16)
    lo = (r1 - mid.astype(F32)).astype(BF16)
    return hi, mid, lo


def _tri_matmul(tri, x):
    hi, mid, lo = _split3(x)
    dot = lambda p: jnp.dot(tri, p, preferred_element_type=F32)
    return dot(hi) + dot(mid) + dot(lo)


def _lower_bound(logits):
    l0, l1 = logits[0:1, :], logits[1:2, :]
    mx = jnp.maximum(l0, l1)
    e0, e1 = jnp.exp(l0 - mx), jnp.exp(l1 - mx)
    return e0 / (e0 + e1)


def _hgrn_gates(q_raw, f_raw, lb):
    q = q_raw * jax.nn.sigmoid(q_raw)
    sig = jax.nn.sigmoid(f_raw)
    f = lb + (1.0 - lb) * sig
    row = lax.broadcasted_iota(jnp.int32, (CHUNK, CHUNK), 0)
    col = lax.broadcasted_iota(jnp.int32, (CHUNK, CHUNK), 1)
    c = _tri_matmul((row >= col).astype(BF16), jnp.log(f))
    return q, sig, f, 1.0 - f, c


def _offdiag_terms(qh, kh, ch, tb):
    rows = slice(tb * SUB, (tb + 1) * SUB)
    r = ch[tb * SUB - 1:tb * SUB, :]
    eqh = jnp.exp(ch[rows] - r)
    ekh = jnp.exp(jnp.minimum(r - ch, 0.0))
    return rows, eqh, qh[rows] * eqh, ekh, kh * ekh


def _diag_decay(cb, s, trow):
    return jnp.exp(jnp.where(trow >= s, cb - cb[s:s + 1, :], NEG))


def _split_refs(refs, n_in, n_out, n_scratch, ex):
    nx = ex.n if ex else 0
    ins, refs = refs[:n_in], refs[n_in:]
    xsrc, refs = refs[:nx], refs[nx:]
    outs, refs = refs[:n_out], refs[n_out:]
    xout, refs = refs[:nx], refs[nx:]
    return ins, outs, refs[:n_scratch], (xsrc, xout, refs[n_scratch:])


def _hgrn_fwd(h, logits, g_norm, ex=None):
    t = h.shape[1]
    nc = t // CHUNK

    def body(*refs):
        ins, outs, scratch, xrefs = _split_refs(refs, 6, 3, 4, ex)
        q_ref, f_ref, i_ref, og_ref, lg_ref, gn_ref = ins
        y_ref, o_ref, sall_ref = outs
        st_ref, q_s, k_s, c_s = scratch

        @pl.when(pl.program_id(0) == 0)
        def _():
            st_ref[...] = jnp.zeros_like(st_ref)
            if ex:
                ex.start(*xrefs)

        if ex:
            @pl.when(pl.program_id(0) == (3 * nc) // 4)
            def _():
                ex.relay(*xrefs)

        lb = _lower_bound(lg_ref[...])
        q, _, _, k, c = _hgrn_gates(q_ref[...], f_ref[...], lb)
        q_s[...] = q
        k_s[...] = k
        c_s[...] = c
        col64 = lax.broadcasted_iota(jnp.int32, (SUB, CHUNK), 1)
        trow = lax.broadcasted_iota(jnp.int32, (SUB, HEAD), 0)

        def head(hd, carry):
            sl = pl.ds(pl.multiple_of(hd * HEAD, HEAD), HEAD)
            qh, kh, ch, ih = q_s[:, sl], k_s[:, sl], c_s[:, sl], i_ref[:, sl]
            st = st_ref[hd]
            sall_ref[hd] = st
            c_last = ch[CHUNK - 1:CHUNK, :]
            o = _dot_nt(qh * jnp.exp(ch), st)
            st_ref[hd] = st * jnp.exp(c_last) + _dot_tn(ih, kh * jnp.exp(c_last - ch))
            a_rows = [jnp.zeros((SUB, CHUNK), F32)]
            for tb in range(1, CHUNK // SUB):
                _, _, q_hat, _, k_hat = _offdiag_terms(qh, kh, ch, tb)
                a_rows.append(jnp.where(col64 < tb * SUB, _dot_nt(q_hat, k_hat), 0.0))
            o = o + _dot(jnp.concatenate(a_rows, axis=0), ih)
            o_rows = []
            for b in range(CHUNK // SUB):
                rows = slice(b * SUB, (b + 1) * SUB)
                qb, cb, kb, ib = qh[rows], ch[rows], kh[rows], ih[rows]
                ob = jnp.zeros((SUB, HEAD), F32)
                for s in range(SUB):
                    a = jnp.sum(qb * _diag_decay(cb, s, trow) * kb[s:s + 1, :], axis=1, keepdims=True)
                    ob = ob + a * ib[s:s + 1, :]
                o_rows.append(ob)
            o = o + jnp.concatenate(o_rows, axis=0)
            o_ref[:, sl] = o
            og = og_ref[:, sl]
            on = o * lax.rsqrt(_mean(o * o) + RMS_EPS)
            y_ref[:, sl] = (on * gn_ref[:, sl] * (og * jax.nn.sigmoid(og))).astype(BF16)
            return carry

        lax.fori_loop(0, HEADS, head, 0, unroll=HEAD_UNROLL)

        if ex:
            @pl.when(pl.program_id(0) == nc - 1)
            def _():
                ex.finish(*xrefs)

    blk = lambda b: pl.BlockSpec((None, CHUNK, D_MODEL), functools.partial(lambda n, b: (b, n, 0), b=b))
    row = pl.BlockSpec((CHUNK, D_MODEL), lambda n: (n, 0))
    whole = lambda s: pl.BlockSpec(s, lambda n: (0,) * len(s))
    res = pl.pallas_call(
        body,
        grid=(nc,),
        in_specs=[blk(SLOT_Q), blk(SLOT_F), blk(SLOT_I), blk(SLOT_OG), whole(logits.shape), whole(g_norm.shape)]
        + (ex.in_specs if ex else []),
        out_specs=[row, row, pl.BlockSpec((None, HEADS, HEAD, HEAD), lambda n: (n, 0, 0, 0))] + (ex.out_specs if ex else []),
        out_shape=[pltpu.HBM((t, D_MODEL), BF16), pltpu.HBM((t, D_MODEL), F32),
                   pltpu.HBM((nc, HEADS, HEAD, HEAD), F32)] + (ex.out_shape if ex else []),
        scratch_shapes=[pltpu.VMEM((HEADS, HEAD, HEAD), F32)] + [pltpu.VMEM((CHUNK, D_MODEL), F32)] * 3
        + (ex.scratch if ex else []),
        compiler_params=_params(("arbitrary",)),
        name="hgrn_fwd",
    )(*_in_hbm([h, h, h, h]), logits, g_norm, *(ex.arrays if ex else []))
    return res[0], res[1], res[2], res[3:]


def _hgrn_bwd(h, o_all, dy, states, dh, logits, g_norm, ex=None):
    t = h.shape[1]
    nc = t // CHUNK

    def body(*refs):
        ins, outs, scratch, xrefs = _split_refs(refs, 10, 3, 8, ex)
        q_ref, f_ref, i_ref, og_ref, o_ref, dy_ref, sall_ref, _, lg_ref, gn_ref = ins
        dqfio_ref, dlg_ref, dgn_ref = outs
        dst_ref, dlb_ref, q_s, k_s, c_s, dq_s, dk_s, dc_s = scratch
        dq_ref, df_ref, di_ref, dog_ref = (dqfio_ref.at[s] for s in (SLOT_Q, SLOT_F, SLOT_I, SLOT_OG))
        n = pl.program_id(0)

        @pl.when(n == 0)
        def _():
            dst_ref[...] = jnp.zeros_like(dst_ref)
            dlb_ref[...] = jnp.zeros_like(dlb_ref)
            dgn_ref[...] = jnp.zeros_like(dgn_ref)
            if ex:
                ex.start(*xrefs)

        lb = _lower_bound(lg_ref[...])
        q_raw = q_ref[...]
        q, sig, f, k, c = _hgrn_gates(q_raw, f_ref[...], lb)
        q_s[...] = q
        k_s[...] = k
        c_s[...] = c
        col64 = lax.broadcasted_iota(jnp.int32, (SUB, CHUNK), 1)
        trow = lax.broadcasted_iota(jnp.int32, (SUB, HEAD), 0)
        row64 = lax.broadcasted_iota(jnp.int32, (CHUNK, HEAD), 0)

        def head(hd, carry):
            sl = pl.ds(pl.multiple_of(hd * HEAD, HEAD), HEAD)
            qh, kh, ch, ih = q_s[:, sl], k_s[:, sl], c_s[:, sl], i_ref[:, sl]
            st = sall_ref[hd]
            dst = dst_ref[hd]
            oh, dyh, og, gn = o_ref[:, sl], dy_ref[:, sl], og_ref[:, sl], gn_ref[:, sl]
            sg = jax.nn.sigmoid(og)
            sil = og * sg
            rms = lax.rsqrt(_mean(oh * oh) + RMS_EPS)
            on = oh * rms
            dog_ref[:, sl] = (dyh * on * gn * _silu_grad(og, sg)).astype(BF16)
            dgn_ref[:, sl] += _sum0(dyh * on * sil)
            don = dyh * gn * sil
            do = rms * (don - on * _mean(don * on))
            dob = do.astype(BF16)

            c_last = ch[CHUNK - 1:CHUNK, :]
            eq = jnp.exp(ch)
            q_til = qh * eq
            ekl = jnp.exp(c_last - ch)
            k_til = kh * ekl
            ecl = jnp.exp(c_last)
            dq_til = _dot(dob, st)
            dk_til = _dot(ih, dst)
            di = _dot_nt(k_til, dst)
            dc_last = _sum0(dk_til * k_til) + _sum0(dst * st) * ecl
            dst_ref[hd] = _dot_tn(dob, q_til) + dst * ecl
            dq = dq_til * eq
            dc = dq_til * q_til - dk_til * k_til
            dk = dk_til * ekl

            da_full = _dot_nt(dob, ih)
            a_rows = [jnp.zeros((SUB, CHUNK), F32)]
            dq_rows = [jnp.zeros((SUB, HEAD), F32)]
            dc_rows = [jnp.zeros((SUB, HEAD), F32)]
            for tb in range(1, CHUNK // SUB):
                rows, eqh, q_hat, ekh, k_hat = _offdiag_terms(qh, kh, ch, tb)
                keep = col64 < tb * SUB
                a_rows.append(jnp.where(keep, _dot_nt(q_hat, k_hat), 0.0))
                da = jnp.where(keep, da_full[rows], 0.0)
                dq_hat = _dot(da, k_hat)
                dk_hat = _dot_tn(da, q_hat)
                dq_rows.append(dq_hat * eqh)
                dc_rows.append(dq_hat * q_hat)
                dk = dk + dk_hat * ekh
                dc = dc - dk_hat * k_hat
            di = di + _dot_tn(jnp.concatenate(a_rows, axis=0), dob)

            dk_rows, di_rows = [], []
            for b in range(CHUNK // SUB):
                rows = slice(b * SUB, (b + 1) * SUB)
                qb, cb, kb, ib, dob_ = qh[rows], ch[rows], kh[rows], ih[rows], do[rows]
                dq_diag = jnp.zeros((SUB, HEAD), F32)
                dk_diag = jnp.zeros((SUB, HEAD), F32)
                di_diag = jnp.zeros((SUB, HEAD), F32)
                for s in range(SUB):
                    ks = kb[s:s + 1, :]
                    dec = _diag_decay(cb, s, trow)
                    a = jnp.sum(qb * dec * ks, axis=1, keepdims=True)
                    gk = jnp.sum(dob_ * ib[s:s + 1, :], axis=1, keepdims=True) * dec
                    dq_diag = dq_diag + gk * ks
                    dk_diag = dk_diag + jnp.where(trow == s, _sum0(gk * qb), 0.0)
                    di_diag = di_diag + jnp.where(trow == s, _sum0(a * dob_), 0.0)
                dq_rows[b] = dq_rows[b] + dq_diag
                dc_rows[b] = dc_rows[b] + qb * dq_diag - kb * dk_diag
                dk_rows.append(dk_diag)
                di_rows.append(di_diag)
            dq = dq + jnp.concatenate(dq_rows, axis=0)
            dk = dk + jnp.concatenate(dk_rows, axis=0)
            dc = dc + jnp.concatenate(dc_rows, axis=0) + jnp.where(row64 == CHUNK - 1, dc_last, 0.0)
            di_ref[:, sl] = (di + jnp.concatenate(di_rows, axis=0)).astype(BF16)
            dq_s[:, sl] = dq
            dk_s[:, sl] = dk
            dc_s[:, sl] = dc
            return carry

        lax.fori_loop(0, HEADS, head, 0, unroll=HEAD_UNROLL)

        row = lax.broadcasted_iota(jnp.int32, (CHUNK, CHUNK), 0)
        col = lax.broadcasted_iota(jnp.int32, (CHUNK, CHUNK), 1)
        dlf = _tri_matmul((row <= col).astype(BF16), dc_s[...])
        df = dlf / f - dk_s[...]
        dlb_ref[...] += _sum0(df * (1.0 - sig))
        df_ref[...] = (df * (1.0 - lb) * sig * (1.0 - sig)).astype(BF16)
        dq_ref[...] = (dq_s[...] * _silu_grad(q_raw, jax.nn.sigmoid(q_raw))).astype(BF16)

        @pl.when(n == nc - 1)
        def _():
            d0 = dlb_ref[...] * lb * (1.0 - lb)
            dlg_ref[0:1, :] = d0
            dlg_ref[1:2, :] = -d0
            if ex:
                ex.finish(*xrefs)

    rev = lambda n: nc - 1 - n
    blk = lambda b: pl.BlockSpec((None, CHUNK, D_MODEL), functools.partial(lambda n, b: (b, rev(n), 0), b=b))
    row = pl.BlockSpec((CHUNK, D_MODEL), lambda n: (rev(n), 0))
    whole = lambda s: pl.BlockSpec(s, lambda n: (0,) * len(s))
    vec = (1, D_MODEL)
    res = pl.pallas_call(
        body,
        grid=(nc,),
        in_specs=[blk(SLOT_Q), blk(SLOT_F), blk(SLOT_I), blk(SLOT_OG), row, row,
                  pl.BlockSpec((None, HEADS, HEAD, HEAD), lambda n: (rev(n), 0, 0, 0)),
                  pl.BlockSpec(memory_space=pl.ANY), whole(logits.shape), whole(vec)] + (ex.in_specs if ex else []),
        out_specs=[pl.BlockSpec((4, CHUNK, D_MODEL), lambda n: (SLOT_Q // 4, rev(n), 0)), whole((2, D_MODEL)), whole(vec)]
        + (ex.out_specs if ex else []),
        out_shape=[pltpu.HBM(dh.shape, BF16), jax.ShapeDtypeStruct((2, D_MODEL), F32),
                   jax.ShapeDtypeStruct(vec, F32)] + (ex.out_shape if ex else []),
        scratch_shapes=[pltpu.VMEM((HEADS, HEAD, HEAD), F32), pltpu.VMEM(vec, F32)]
        + [pltpu.VMEM((CHUNK, D_MODEL), F32)] * 6 + (ex.scratch if ex else []),
        input_output_aliases={7: 0},
        compiler_params=_params(("arbitrary",)),
        name="hgrn_bwd",
    )(*_in_hbm([h, h, h, h, o_all, dy, states, dh]), logits, g_norm, *(ex.arrays if ex else []))
    return res[0], res[1], res[2], res[3:]


def _merge_fwd(i, n, ga, gb, za, zb):
    return jax.nn.sigmoid(ga) * za + jax.nn.sigmoid(gb) * zb


def _merge_bwd(i, n, ga, gb, za, zb, dm):
    sa, sb = jax.nn.sigmoid(ga), jax.nn.sigmoid(gb)
    dgates = jnp.stack([(dm * za * sa * (1.0 - sa)).astype(BF16), (dm * zb * sb * (1.0 - sb)).astype(BF16)])
    return dgates, dm * sa, dm * sb


def _ln1_fwd(i, n, x, r1, g, b):
    xhat, _ = _ln_stats(ALPHA * x + r1)
    x1 = xhat * g + b
    return x1, x1


def _ln1_bwd(i, n, x, r1, dx1, g):
    xhat, rstd = _ln_stats(ALPHA * x + r1)
    dz = _ln_bwd(dx1, xhat, rstd, g)
    return dz, dz, _sum0(dx1 * xhat), _sum0(dx1)


def _ln2_loss(i, n, x1, fo, pg, pp, tgt, g, b):
    sg = jax.nn.sigmoid(pg)
    xhat, rstd = _ln_stats(ALPHA * x1 + fo + sg * pp)
    diff = xhat * g + b - tgt
    loss = 0.5 * jnp.sum(_mean(diff * diff), axis=0, keepdims=True)
    dy = diff * (1.0 / D_MODEL)
    dz = _ln_bwd(dy, xhat, rstd, g)
    return (dz, dz, dz * pp * sg * (1.0 - sg), dz * sg,
            jnp.broadcast_to(loss, (8, LANE)), _sum0(dy * xhat), _sum0(dy))


def _shift_down(cur, halo, tile):
    row = lax.broadcasted_iota(jnp.int32, cur.shape, 0)
    m1 = jnp.where(row == 0, halo[7:8, :], pltpu.roll(cur, 1, 0))
    m2 = jnp.where(row == 0, halo[6:7, :], jnp.where(row == 1, halo[7:8, :], pltpu.roll(cur, 2, 0)))
    return m1, m2


def _shift_up(cur, halo, tile):
    row = lax.broadcasted_iota(jnp.int32, cur.shape, 0)
    p1 = jnp.where(row == tile - 1, halo[0:1, :], pltpu.roll(cur, tile - 1, 0))
    p2 = jnp.where(row == tile - 2, halo[0:1, :], jnp.where(row == tile - 1, halo[1:2, :], pltpu.roll(cur, tile - 2, 0)))
    return p1, p2


def _conv_pre(i, gate, halo, w, b, tile):
    halo = jnp.where(i == 0, 0.0, halo)
    m1, m2 = _shift_down(gate, halo, tile)
    return w[0:1, :] * m2 + w[1:2, :] * m1 + w[2:3, :] * gate + b, m1, m2


def _conv_fwd(tile, i, n, gate, halo, val, w, b):
    cg, _, _ = _conv_pre(i, gate, halo, w, b, tile)
    return _gelu(cg) * val


def _conv_bwd_a(tile, i, n, gate, halo, val, dhid, w, b):
    cg, m1, m2 = _conv_pre(i, gate, halo, w, b, tile)
    dcg = dhid * val * _gelu_grad(cg)
    return dcg, dhid * _gelu(cg), _sum0(dcg * m2), _sum0(dcg * m1), _sum0(dcg * gate), _sum0(dcg)


def _conv_bwd_b(tile, i, n, dcg, halo, w):
    halo = jnp.where(i == n - 1, 0.0, halo)
    p1, p2 = _shift_up(dcg, halo, tile)
    return w[2:3, :] * dcg + w[1:2, :] * p1 + w[0:1, :] * p2


def _halo_spec(width, tile, t, nxt):
    per = tile // 8
    last = t // 8 - 1
    if nxt:
        return pl.BlockSpec((8, width), lambda i: (jnp.minimum((i + 1) * per, last), 0))
    return pl.BlockSpec((8, width), lambda i: (jnp.maximum(i * per - 1, 0), 0))


def _adamw(i, n, w, m, v, parts):
    g = parts[0].astype(F32)
    for j in range(1, N_DEV):
        g = g + parts[j].astype(F32)
    m_new = ADAM_B1 * m + (1.0 - ADAM_B1) * g
    v_new = ADAM_B2 * v + (1.0 - ADAM_B2) * (g * g)
    m_hat = m_new / (1.0 - ADAM_B1 ** ADAM_STEP)
    v_hat = v_new / (1.0 - ADAM_B2 ** ADAM_STEP)
    delta = -ADAM_LR * (m_hat / (jnp.sqrt(v_hat) + ADAM_EPS) + ADAM_WD * w)
    return g, delta, m_new, v_new


def _adam_call(w, m, v, parts, name):
    r, c = w.shape
    tile = _pick(r, (256, 128)) if r > 256 else r
    spec = pl.BlockSpec((N_DEV, tile, c), lambda i: (0, i, 0))
    return _rowwise(_adamw, [w, m, v, (parts, spec)], [], [(c, F32)] * 4, [], tile=tile, name=name)


def _peer(k):
    x, y, c = lax.axis_index("x"), lax.axis_index("y"), lax.axis_index("c")
    px = x ^ ((k >> 2) & 1)
    py = y ^ ((k >> 1) & 1)
    pc = c ^ (k & 1)
    return (px, py, pc), 4 * px + 2 * py + pc


def _my_index():
    return 4 * lax.axis_index("x") + 2 * lax.axis_index("y") + lax.axis_index("c")


class _Exchange:
    KINDS = ("gather", "gather+relay", "gather+relay+slots", "scatter", "scatter+slots")

    def __init__(self, entries):
        assert all(k in self.KINDS for _, k in entries), [k for _, k in entries]
        self.arrays = [a for a, _ in entries]
        self.scatter = [k.startswith("scatter") for _, k in entries]
        self.relayed = ["+relay" in k for _, k in entries]
        self.slots = ["+slots" in k for _, k in entries]
        self.n = len(entries)
        self.in_specs = [pl.BlockSpec(memory_space=pl.ANY)] * self.n
        self.out_specs = [pl.BlockSpec(memory_space=pl.ANY)] * self.n
        shapes = [tuple(a.shape[1:]) if sc else tuple(a.shape) for a, sc in zip(self.arrays, self.scatter)]
        self.out_shape = [pltpu.HBM((N_DEV,) + s, a.dtype) for s, a in zip(shapes, self.arrays)]
        per = N_DEV - 1
        self.scratch = [pltpu.SemaphoreType.DMA((self.n * per,)), pltpu.SemaphoreType.DMA((self.n * per,)),
                        pltpu.SemaphoreType.DMA((self.n,))]

    def _copies(self, srcs, outs, sems):
        send_sems, recv_sems, local_sems = sems
        x, y, c = lax.axis_index("x"), lax.axis_index("y"), lax.axis_index("c")
        me = _my_index()
        per = N_DEV - 1
        local, first, passed, relay_arrivals, arrivals = [], [], [], [], []
        for a in range(self.n):
            place = (lambda idx: _slot_of_group(idx)) if self.slots[a] else (lambda idx: idx)

            def copy(k, src, dst, dev, a=a):
                return pltpu.make_async_remote_copy(
                    src_ref=src, dst_ref=dst, send_sem=send_sems.at[a * per + k], recv_sem=recv_sems.at[a * per + k],
                    device_id=dev, device_id_type=pl.DeviceIdType.MESH)

            if self.scatter[a]:
                mine, land = srcs[a].at[place(me)], outs[a].at[me]
            else:
                mine, land = srcs[a], outs[a].at[place(me)]
            local.append(pltpu.make_async_copy(mine, land, local_sems.at[a]))
            if self.relayed[a]:
                block = lambda px, py, pc, a=a, place=place: outs[a].at[place(4 * px + 2 * py + pc)]
                chips = [(1 - x, y), (x, 1 - y), (1 - x, 1 - y)]
                first.append(copy(0, mine, land, (x, y, 1 - c)))
                arrivals.append(copy(0, mine, block(x, y, 1 - c), (x, y, 1 - c)))
                for j, (px, py) in enumerate(chips):
                    first.append(copy(1 + j, mine, land, (px, py, c)))
                    relay_arrivals.append(copy(1 + j, mine, block(px, py, c), (px, py, c)))
                    passed.append(copy(4 + j, block(px, py, c), block(px, py, c), (x, y, 1 - c)))
                    arrivals.append(copy(4 + j, mine, block(px, py, 1 - c), (x, y, 1 - c)))
                continue
            for k in range(1, N_DEV):
                dev, idx = _peer(k)
                if self.scatter[a]:
                    first.append(copy(k - 1, srcs[a].at[place(idx)], land, dev))
                    arrivals.append(copy(k - 1, mine, outs[a].at[idx], dev))
                else:
                    first.append(copy(k - 1, mine, land, dev))
                    arrivals.append(copy(k - 1, mine, outs[a].at[place(idx)], dev))
        return local, first, passed, relay_arrivals, arrivals

    def start(self, srcs, outs, sems):
        local, first, _, _, _ = self._copies(srcs, outs, sems)
        for cp in local + first:
            cp.start()

    def relay(self, srcs, outs, sems):
        _, _, passed, relay_arrivals, _ = self._copies(srcs, outs, sems)
        for landed, onward in zip(relay_arrivals, passed):
            landed.wait_recv()
            onward.start()

    def finish(self, srcs, outs, sems):
        local, first, passed, _, arrivals = self._copies(srcs, outs, sems)
        for cp in arrivals:
            cp.wait_recv()
        for cp in first + passed:
            cp.wait_send()
        for cp in local:
            cp.wait()


def _exchange(entries, name):
    ex = _Exchange(entries)
    n = ex.n

    def body(*refs):
        xrefs = refs[:n], refs[n:2 * n], refs[2 * n:]
        ex.start(*xrefs)
        ex.relay(*xrefs)
        ex.finish(*xrefs)

    return pl.pallas_call(body, in_specs=ex.in_specs, out_specs=ex.out_specs, out_shape=ex.out_shape,
                          scratch_shapes=ex.scratch, name=name)(*ex.arrays)


def _local_step(x, p, tgt, w_in, small, comm):
    t = x.shape[0]
    tile = _pick(t, (256, 128))
    d = D_MODEL
    act_b, act_f = (d, BF16), (d, F32)
    x_b, p_b = x.astype(BF16), p.astype(BF16)

    chunk_id = jnp.arange(SGU_BLOCK) // CHUNK
    mask = chunk_id[:, None] >= chunk_id[None, :]
    wm = jnp.where(mask[None], small["sgu_w_s"], 0.0)
    wm_b = wm.astype(BF16)
    wm_t = jnp.swapaxes(wm, 1, 2).astype(BF16)
    bs_t = small["sgu_b_s"].T

    h = _mm(x_b, w_in, out_dtype=F32, name="mm_h")
    y_a = _sgu_fwd(h, wm_b, bs_t, small["sgu_norm_g"], small["sgu_norm_b"])
    y_b, o_all, states, got = _hgrn_fwd(h, small["lb_logits"], small["hgrn_norm_g"], ex=comm.weights_exchange())
    wts, conv_w = comm.weights(got)
    z_a = _mm(y_a, wts["w_a"], out_dtype=F32, name="mm_za")
    z_b = _mm(y_b, wts["w_b"], out_dtype=F32, name="mm_zb")
    gates = [(h, SLOT_GA), (h, SLOT_GB)]
    merged, = _rowwise(_merge_fwd, gates + [z_a, z_b], [], [act_b], [], tile=tile, name="merge_fwd")
    r1 = _mm(merged, wts["w_o"], out_dtype=F32, name="mm_r1")
    x1, x1_b = _rowwise(_ln1_fwd, [x, r1], [small["ln1_g"], small["ln1_b"]], [act_f, act_b], [], tile=tile, name="ln1_fwd")
    gate = _mm(x1_b, wts["w_g"], out_dtype=F32, name="mm_gate")
    val = _mm(x1_b, wts["w_v"], out_dtype=F32, name="mm_val")
    pg = _mm(x1_b, wts["w_pg"], out_dtype=F32, name="mm_pg")
    pp = _mm(p_b, wts["w_pp"], out_dtype=F32, name="mm_pp")
    hid, = _rowwise(functools.partial(_conv_fwd, tile), [gate, (gate, _halo_spec(D_FF, tile, t, False)), val],
                    [conv_w, small["conv_b"]], [(D_FF, BF16)], [], tile=tile, name="conv_fwd")
    fo = _mm(hid, wts["w_down"], out_dtype=F32, name="mm_down")
    dz2, dz2_b, dpg, dpp, loss, dg2, db2 = _rowwise(
        _ln2_loss, [x1, fo, pg, pp, tgt], [small["ln2_g"], small["ln2_b"]],
        [act_f, act_b, act_b, act_b], [(8, LANE), (1, d), (1, d)], tile=tile, name="ln2_loss")

    dhid = _mm(dz2_b, wts["w_down"], out_dtype=F32, name="mm_dhid", trans_b=True)
    g_down = _mm_tn(hid, dz2_b, out_dtype=BF16, name="mm_g_down")
    dcg, dval, dcw0, dcw1, dcw2, dcb = _rowwise(
        functools.partial(_conv_bwd_a, tile), [gate, (gate, _halo_spec(D_FF, tile, t, False)), val, dhid],
        [conv_w, small["conv_b"]], [(D_FF, F32), (D_FF, BF16)], [(1, D_FF)] * 4, tile=tile, name="conv_bwd_a")
    dgate, = _rowwise(functools.partial(_conv_bwd_b, tile), [dcg, (dcg, _halo_spec(D_FF, tile, t, True))],
                      [conv_w], [(D_FF, BF16)], [], tile=tile, name="conv_bwd_b")
    g_g = _mm_tn(x1_b, dgate, out_dtype=BF16, name="mm_g_gate")
    g_v = _mm_tn(x1_b, dval, out_dtype=BF16, name="mm_g_val")
    g_pg = _mm_tn(x1_b, dpg, out_dtype=BF16, name="mm_g_pg")
    g_pp = _mm_tn(p_b, dpp, out_dtype=BF16, name="mm_g_pp")
    dx1 = _mm(dgate, wts["w_g"], out_dtype=F32, name="mm_dx1_gate", trans_b=True, adds=[(dz2, ALPHA)])
    dx1 = _mm(dval, wts["w_v"], out_dtype=F32, name="mm_dx1_val", trans_b=True, adds=[(dx1, 1.0)])
    dx1 = _mm(dpg, wts["w_pg"], out_dtype=F32, name="mm_dx1_pg", trans_b=True, adds=[(dx1, 1.0)])
    dz1, dz1_b, dg1, db1 = _rowwise(_ln1_bwd, [x, r1, dx1], [small["ln1_g"]], [act_f, act_b], [(1, d), (1, d)],
                                    tile=tile, name="ln1_bwd")
    g_o = _mm_tn(merged, dz1_b, out_dtype=BF16, name="mm_g_o")
    dm = _mm(dz1_b, wts["w_o"], out_dtype=F32, name="mm_dm", trans_b=True)
    dh, dza, dzb = _rowwise(_merge_bwd, gates + [z_a, z_b, dm], [],
                            [("stack", 2, SLOT_GA // 2, 8, d, BF16), act_b, act_b], [], tile=tile, name="merge_bwd")
    g_a = _mm_tn(y_a, dza, out_dtype=BF16, name="mm_g_a")
    g_b = _mm_tn(y_b, dzb, out_dtype=BF16, name="mm_g_b")
    dy_a = _mm(dza, wts["w_a"], out_dtype=F32, name="mm_dya", trans_b=True)
    dy_b = _mm(dzb, wts["w_b"], out_dtype=F32, name="mm_dyb", trans_b=True)
    dh, dws, dbs, dgv_n, dbv_n = _sgu_bwd(h, dy_a, dh, wm_b, wm_t, bs_t, small["sgu_norm_g"], small["sgu_norm_b"])
    big = dict(w_a=g_a, w_b=g_b, w_o=g_o, w_g=g_g, w_v=g_v, w_down=g_down, w_pp=g_pp, w_pg=g_pg)
    sm = dict(sgu_w_s=jnp.where(mask[None], dws, 0.0), sgu_b_s=dbs[:, :GROUPS].T, sgu_norm_g=dgv_n, sgu_norm_b=dbv_n,
              ln1_g=dg1, ln1_b=db1, conv_w=jnp.concatenate([dcw0, dcw1, dcw2], axis=0), conv_b=dcb, ln2_g=dg2, ln2_b=db2,
              loss=loss)
    dh, dlogits, dgn, got = _hgrn_bwd(h, o_all, dy_b, states, dh, small["lb_logits"], small["hgrn_norm_g"],
                                      ex=comm.grads_exchange(big, sm))
    comm.grads_done(got)
    g_in = _mm_tn(x_b, dh, out_dtype=BF16, name="mm_g_in")
    ex = comm.last_exchange(g_in, dict(lb_logits=dlogits, hgrn_norm_g=dgn))
    res = _mm(dh, w_in, out_dtype=F32, name="mm_dx", trans_b=True, reduce_b=True, adds=[(dz1, ALPHA)], ex=ex)
    grad_x, got = res if ex else (res, ())
    comm.last_done(got)
    return grad_x


_SMALL_EARLY = ["sgu_w_s", "sgu_b_s", "sgu_norm_g", "sgu_norm_b", "ln1_g", "ln1_b", "ffn_conv_b", "ln2_g", "ln2_b"]
_SMALL_LATE = ["hgrn_lb_logits", "hgrn_norm_g"]
N_TAPS = D_FF // N_DEV
UP_COLS = 2 * D_FF // N_DEV


class _StepExchanges:
    def __init__(self, shards):
        self.shards = shards

    def weights_exchange(self):
        return _Exchange([(s, "gather+relay") for s in self.shards])

    def weights(self, got):
        d, f = D_MODEL, D_FF
        w_br_g, w_o_g, w_up_g, w_down_g, w_pp_g, w_pg_g, conv_g = got
        w_br = w_br_g.transpose(1, 0, 2, 3).reshape(2, d, d)
        w_up = w_up_g.transpose(1, 0, 2).reshape(d, 2, f).transpose(1, 0, 2)
        wts = dict(w_a=w_br[0], w_b=w_br[1], w_o=w_o_g.reshape(d, d), w_g=w_up[0], w_v=w_up[1],
                   w_down=w_down_g.reshape(f, d), w_pp=w_pp_g.transpose(1, 0, 2).reshape(256, d), w_pg=w_pg_g.reshape(d, d))
        return wts, conv_g.transpose(1, 0, 2).reshape(3, f)

    def grads_exchange(self, big, sm):
        d = D_MODEL
        parts = [jnp.stack([big["w_a"], big["w_b"]]).reshape(2, N_DEV, 128, d).transpose(1, 0, 2, 3),
                 big["w_o"].reshape(N_DEV, 128, d),
                 jnp.concatenate([big["w_g"], big["w_v"]], axis=1).reshape(d, N_DEV, UP_COLS).transpose(1, 0, 2),
                 big["w_down"].reshape(N_DEV, N_TAPS, d),
                 big["w_pp"].reshape(256, N_DEV, 128).transpose(1, 0, 2),
                 big["w_pg"].reshape(N_DEV, 128, d)]
        packed, self.rows_early = _pack([sm[k] for k in ("sgu_w_s", "sgu_b_s", "sgu_norm_g", "sgu_norm_b", "ln1_g", "ln1_b",
                                                         "conv_b", "ln2_g", "ln2_b", "conv_w", "loss")])
        return _Exchange([(a, "scatter") for a in parts] + [(packed, "gather")])

    def grads_done(self, got):
        self.recv, self.small_early = got[:6], got[6]

    def last_exchange(self, g_in, sm):
        packed, self.rows_late = _pack([sm["lb_logits"], sm["hgrn_norm_g"]])
        return _Exchange([(g_in, "scatter+slots"), (packed, "gather")])

    def last_done(self, got):
        self.recv_in, self.small_late = got


def _rows128(a):
    flat = a.reshape(-1)
    rows = -(-flat.shape[0] // (8 * LANE)) * 8
    return jnp.pad(flat, (0, rows * LANE - flat.shape[0])).reshape(rows, LANE)


def _pack(parts):
    blocks = [_rows128(a) for a in parts]
    return jnp.concatenate(blocks, axis=0), [b.shape[0] for b in blocks]


def _unpack(packed, shapes, rows):
    out, r0 = [], 0
    for shp, r in zip(shapes, rows):
        n = math.prod(shp)
        out.append(packed[r0:r0 + r].reshape(-1)[:n].reshape(shp))
        r0 += r
    return out


def kernel(x, p, w_in, sgu_w_s, sgu_b_s, sgu_norm_g, sgu_norm_b, hgrn_lb_logits, hgrn_norm_g, w_branch, w_out, ln1_g, ln1_b, ffn_w_up, ffn_conv_w, ffn_conv_b, ffn_w_down, ln2_g, ln2_b, ple_w_proj, ple_w_gate, loss_target, m_w_in, m_sgu_w_s, m_sgu_b_s, m_sgu_norm_g, m_sgu_norm_b, m_hgrn_lb_logits, m_hgrn_norm_g, m_w_branch, m_w_out, m_ln1_g, m_ln1_b, m_ffn_w_up, m_ffn_conv_w, m_ffn_conv_b, m_ffn_w_down, m_ln2_g, m_ln2_b, m_ple_w_proj, m_ple_w_gate, v_w_in, v_sgu_w_s, v_sgu_b_s, v_sgu_norm_g, v_sgu_norm_b, v_hgrn_lb_logits, v_hgrn_norm_g, v_w_branch, v_w_out, v_ln1_g, v_ln1_b, v_ffn_w_up, v_ffn_conv_w, v_ffn_conv_b, v_ffn_w_down, v_ln2_g, v_ln2_b, v_ple_w_proj, v_ple_w_gate):
    weights = dict(w_in=w_in, sgu_w_s=sgu_w_s, sgu_b_s=sgu_b_s, sgu_norm_g=sgu_norm_g, sgu_norm_b=sgu_norm_b,
                   hgrn_lb_logits=hgrn_lb_logits, hgrn_norm_g=hgrn_norm_g, w_branch=w_branch, w_out=w_out,
                   ln1_g=ln1_g, ln1_b=ln1_b, ffn_w_up=ffn_w_up, ffn_conv_w=ffn_conv_w, ffn_conv_b=ffn_conv_b,
                   ffn_w_down=ffn_w_down, ln2_g=ln2_g, ln2_b=ln2_b, ple_w_proj=ple_w_proj, ple_w_gate=ple_w_gate)
    mom_m = dict(w_in=m_w_in, sgu_w_s=m_sgu_w_s, sgu_b_s=m_sgu_b_s, sgu_norm_g=m_sgu_norm_g, sgu_norm_b=m_sgu_norm_b,
                 hgrn_lb_logits=m_hgrn_lb_logits, hgrn_norm_g=m_hgrn_norm_g, w_branch=m_w_branch, w_out=m_w_out,
                 ln1_g=m_ln1_g, ln1_b=m_ln1_b, ffn_w_up=m_ffn_w_up, ffn_conv_w=m_ffn_conv_w, ffn_conv_b=m_ffn_conv_b,
                 ffn_w_down=m_ffn_w_down, ln2_g=m_ln2_g, ln2_b=m_ln2_b, ple_w_proj=m_ple_w_proj, ple_w_gate=m_ple_w_gate)
    mom_v = dict(w_in=v_w_in, sgu_w_s=v_sgu_w_s, sgu_b_s=v_sgu_b_s, sgu_norm_g=v_sgu_norm_g, sgu_norm_b=v_sgu_norm_b,
                 hgrn_lb_logits=v_hgrn_lb_logits, hgrn_norm_g=v_hgrn_norm_g, w_branch=v_w_branch, w_out=v_w_out,
                 ln1_g=v_ln1_g, ln1_b=v_ln1_b, ffn_w_up=v_ffn_w_up, ffn_conv_w=v_ffn_conv_w, ffn_conv_b=v_ffn_conv_b,
                 ffn_w_down=v_ffn_w_down, ln2_g=v_ln2_g, ln2_b=v_ln2_b, ple_w_proj=v_ple_w_proj, ple_w_gate=v_ple_w_gate)
    d, f = D_MODEL, D_FF
    me = _my_index()

    w_in_g, = _exchange([(w_in[0].astype(BF16), "gather+relay+slots")], "gather_w_in")
    comm = _StepExchanges([w_branch[0].astype(BF16), w_out[0].astype(BF16), ffn_w_up[0].astype(BF16),
                           ffn_w_down[0].astype(BF16), ple_w_proj[0].astype(BF16), ple_w_gate[0].astype(BF16), ffn_conv_w[0]])
    small = dict(sgu_w_s=sgu_w_s[0], sgu_b_s=sgu_b_s[0], sgu_norm_g=sgu_norm_g, sgu_norm_b=sgu_norm_b,
                 lb_logits=hgrn_lb_logits, hgrn_norm_g=hgrn_norm_g, ln1_g=ln1_g, ln1_b=ln1_b, ln2_g=ln2_g, ln2_b=ln2_b,
                 conv_b=ffn_conv_b)
    grad_x = _local_step(x[0], p[0, 0], loss_target[0], w_in_g, small, comm)

    out = {}

    def adam(name, parts8, shape2d):
        w2, m2, v2 = (a.reshape(shape2d) for a in (weights[name], mom_m[name], mom_v[name]))
        res = _adam_call(w2, m2, v2, parts8.reshape((N_DEV,) + shape2d), "adam_" + name)
        out[name] = tuple(r.reshape(weights[name].shape) for r in res)

    adam("w_in", comm.recv_in, (d, d))
    adam("w_branch", comm.recv[0], (256, d))
    adam("w_out", comm.recv[1], (128, d))
    adam("ffn_w_up", comm.recv[2], (d, UP_COLS))
    adam("ffn_w_down", comm.recv[3], (N_TAPS, d))
    adam("ple_w_proj", comm.recv[4], (256, 128))
    adam("ple_w_gate", comm.recv[5], (128, d))

    def adam_small(names, extra_w, extra_m, extra_v, extra_shapes, parts8, rows, label):
        pk = lambda src, extra: _pack([src[n] for n in names] + extra)[0]
        res = _adam_call(pk(weights, extra_w), pk(mom_m, extra_m), pk(mom_v, extra_v), parts8, label)
        shapes = [weights[n].shape for n in names] + extra_shapes
        unpacked = [_unpack(r, shapes, rows) for r in res]
        for j, n in enumerate(names):
            out[n] = tuple(u[j] for u in unpacked)
        return [[u[len(names) + j] for u in unpacked] for j in range(len(extra_shapes))]

    blank = jnp.zeros((8, LANE), F32)
    taps, loss_rows = adam_small(
        _SMALL_EARLY, [_place_taps(ffn_conv_w[0], me, f), blank], [_place_taps(m_ffn_conv_w[0], me, f), blank],
        [_place_taps(v_ffn_conv_w[0], me, f), blank + 1.0], [(3, f), (8, LANE)], comm.small_early, comm.rows_early,
        "adam_small_early")
    adam_small(_SMALL_LATE, [], [], [], [], comm.small_late, comm.rows_late, "adam_small_late")
    out["ffn_conv_w"] = tuple(lax.dynamic_slice_in_dim(u, me * N_TAPS, N_TAPS, axis=1)[None] for u in taps)
    loss = loss_rows[0][0, 0]

    order = ["w_in", "sgu_w_s", "sgu_b_s", "sgu_norm_g", "sgu_norm_b", "hgrn_lb_logits", "hgrn_norm_g", "w_branch", "w_out",
             "ln1_g", "ln1_b", "ffn_w_up", "ffn_conv_w", "ffn_conv_b", "ffn_w_down", "ln2_g", "ln2_b", "ple_w_proj", "ple_w_gate"]
    return (loss, grad_x[None], *[out[n][0] for n in order], *[out[n][1] for n in order],
            *[out[n][2] for n in order], *[out[n][3] for n in order])


def _place_taps(shard, me, f):
    return lax.dynamic_update_slice_in_dim(jnp.zeros((3, f), F32), shard, me * N_TAPS, axis=1)
```

```python
import functools
import math

import jax
import jax.numpy as jnp
from jax import lax
from jax.experimental import pallas as pl
from jax.experimental.pallas import tpu as pltpu

F32 = jnp.float32
BF16 = jnp.bfloat16

N_DEV = 8
N_CHIP = 4
D_MODEL = 1024
CHUNK = 64
SUB = 16
SGU_BLOCK = 128
GROUPS = 8
HEAD = 128
HEADS = 8
HEAD_UNROLL = 4
D_FF = 2816
LN_EPS = 1e-5
RMS_EPS = 1e-6
ALPHA = 2.0 ** 0.25
GELU_K = math.sqrt(2.0 / math.pi)
GELU_C = 0.044715
NEG = -1e30
ADAM_LR, ADAM_B1, ADAM_B2, ADAM_EPS, ADAM_WD, ADAM_STEP = 0.001, 0.9, 0.999, 1e-08, 0.01, 10
LANE = 128
SLOT_Q, SLOT_F, SLOT_I, SLOT_OG, SLOT_U, SLOT_V, SLOT_GA, SLOT_GB = range(8)


def _slot_of_group(k):
    return jnp.where(k < 2, k + 4, jnp.where(k < 6, k - 2, k))


MIB = 1024 * 1024
VMEM_V7X = 64 * MIB
VMEM_FLOOR = 32 * MIB
MM_TILES = (1024, 1408, 512, 256, 128)


def _params(sem, need=0):
    limit = min(max(need + need // 4, VMEM_FLOOR), VMEM_V7X - 4 * MIB)
    return pltpu.CompilerParams(dimension_semantics=sem, vmem_limit_bytes=limit)


def _pick(n, prefs):
    for t in prefs:
        if n % t == 0:
            return t
    return n


def _gelu(x):
    return 0.5 * x * (1.0 + jnp.tanh(GELU_K * (x + GELU_C * x * x * x)))


def _gelu_grad(x):
    t = jnp.tanh(GELU_K * (x + GELU_C * x * x * x))
    return 0.5 * (1.0 + t) + 0.5 * x * (1.0 - t * t) * GELU_K * (1.0 + 3.0 * GELU_C * x * x)


def _silu_grad(x, s):
    return s * (1.0 + x * (1.0 - s))


def _dot(a, b):
    return jnp.dot(a.astype(BF16), b.astype(BF16), preferred_element_type=F32)


def _dot_nt(a, b):
    return lax.dot_general(a.astype(BF16), b.astype(BF16), (((1,), (1,)), ((), ())), preferred_element_type=F32)


def _dot_tn(a, b):
    return lax.dot_general(a.astype(BF16), b.astype(BF16), (((0,), (0,)), ((), ())), preferred_element_type=F32)


def _mean(x):
    return jnp.mean(x, axis=-1, keepdims=True)


def _sum0(x):
    return jnp.sum(x, axis=0, keepdims=True)


def _mm(a, b, *, out_dtype, name, trans_b=False, reduce_b=False, adds=(), ex=None):
    squeeze = b.ndim == 2
    a3 = a if a.ndim == 3 else a[None]
    b3 = b if b.ndim == 3 else b[None]
    ba, m, k = a3.shape
    bb = b3.shape[0]
    n = b3.shape[1] if trans_b else b3.shape[2]
    tm = _pick(m, MM_TILES)
    tn = _pick(n, MM_TILES)
    if reduce_b:
        bo, steps = 1, bb
        a_map = lambda o, i, j, r: (r if ba > 1 else 0, i, 0)
        b_map = (lambda o, i, j, r: (r, j, 0)) if trans_b else (lambda o, i, j, r: (r, 0, j))
    else:
        bo, steps = bb, 1
        a_map = lambda o, i, j, r: (o if ba > 1 else 0, i, 0)
        b_map = (lambda o, i, j, r: (o, j, 0)) if trans_b else (lambda o, i, j, r: (o, 0, j))
    o_map = lambda o, i, j, r: (o, i, j)
    add_arrays = [x if x.ndim == 3 else x[None] for x, _ in adds]
    add_scales = [s for _, s in adds]
    n_add = len(adds)
    dot = _dot_nt if trans_b else _dot

    def finish(acc, add_refs, o_ref):
        for ref, s in zip(add_refs, add_scales):
            acc = acc + s * ref[...].astype(F32)
        o_ref[...] = acc.astype(o_ref.dtype)

    grid = (bo, m // tm, n // tn, steps)

    def body(*refs):
        ins, (o_ref,), scratch, xrefs = _split_refs(refs, 2 + n_add, 1, 1 if reduce_b else 0, ex)
        a_ref, b_ref, add_refs = ins[0], ins[1], ins[2:]
        step = ((pl.program_id(0) * grid[1] + pl.program_id(1)) * grid[2] + pl.program_id(2)) * grid[3] + pl.program_id(3)
        if ex:
            @pl.when(step == 0)
            def _():
                ex.start(*xrefs)

        if reduce_b:
            acc, = scratch
            r = pl.program_id(3)

            @pl.when(r == 0)
            def _():
                acc[...] = jnp.zeros_like(acc)

            acc[...] += dot(a_ref[...], b_ref[...])

            @pl.when(r == steps - 1)
            def _():
                finish(acc[...], add_refs, o_ref)
        else:
            finish(dot(a_ref[...], b_ref[...]), add_refs, o_ref)

        if ex:
            @pl.when(step == math.prod(grid) - 1)
            def _():
                ex.finish(*xrefs)

    b_block = (None, tn, k) if trans_b else (None, k, tn)
    out_bytes = tm * tn * jnp.dtype(out_dtype).itemsize
    need = 2 * (tm * k * a3.dtype.itemsize + k * tn * b3.dtype.itemsize + out_bytes + n_add * tm * tn * 4)
    need += 2 * tm * tn * 4
    sem = ("arbitrary",) * 4 if ex else ("parallel", "parallel", "parallel", "arbitrary")
    res = pl.pallas_call(
        body,
        grid=grid,
        in_specs=[pl.BlockSpec((None, tm, k), a_map), pl.BlockSpec(b_block, b_map)]
        + [pl.BlockSpec((None, tm, tn), o_map) for _ in adds] + (ex.in_specs if ex else []),
        out_specs=[pl.BlockSpec((None, tm, tn), o_map)] + (ex.out_specs if ex else []),
        out_shape=[jax.ShapeDtypeStruct((bo, m, n), out_dtype)] + (ex.out_shape if ex else []),
        scratch_shapes=([pltpu.VMEM((tm, tn), F32)] if reduce_b else []) + (ex.scratch if ex else []),
        compiler_params=_params(sem, need),
        name=name,
    )(a3, b3, *add_arrays, *(ex.arrays if ex else []))
    out = res[0][0] if (reduce_b or squeeze) else res[0]
    return (out, res[1:]) if ex else out


def _mm_tn(a, b, *, out_dtype, name):
    squeeze = b.ndim == 2
    b3 = b if b.ndim == 3 else b[None]
    t, m = a.shape
    bb, _, n = b3.shape
    tm = _pick(m, MM_TILES)
    tn = _pick(n, MM_TILES)
    tt = _pick(t, (1024, 512, 256, 128))
    steps = t // tt
    need = 2 * (tt * tm * a.dtype.itemsize + tt * tn * b3.dtype.itemsize + tm * tn * jnp.dtype(out_dtype).itemsize)
    need += 2 * tm * tn * 4

    def body(a_ref, b_ref, o_ref, acc):
        r = pl.program_id(3)

        @pl.when(r == 0)
        def _():
            acc[...] = jnp.zeros_like(acc)

        acc[...] += _dot_tn(a_ref[...], b_ref[...])

        @pl.when(r == steps - 1)
        def _():
            o_ref[...] = acc[...].astype(o_ref.dtype)

    out = pl.pallas_call(
        body,
        grid=(bb, m // tm, n // tn, steps),
        in_specs=[pl.BlockSpec((tt, tm), lambda o, i, j, r: (r, i)),
                  pl.BlockSpec((None, tt, tn), lambda o, i, j, r: (o, r, j))],
        out_specs=pl.BlockSpec((None, tm, tn), lambda o, i, j, r: (o, i, j)),
        out_shape=jax.ShapeDtypeStruct((bb, m, n), out_dtype),
        scratch_shapes=[pltpu.VMEM((tm, tn), F32)],
        compiler_params=_params(("parallel", "parallel", "parallel", "arbitrary"), need),
        name=name,
    )(a, b3)
    return out[0] if squeeze else out


def _rowwise(fn, rows, consts, row_outs, acc_outs, *, tile, name):
    first = rows[0][0] if isinstance(rows[0], tuple) else rows[0]
    t = first.shape[-2]
    steps = t // tile
    arrays, in_specs = [], []
    for r in rows:
        if isinstance(r, tuple) and isinstance(r[1], pl.BlockSpec):
            arrays.append(r[0])
            in_specs.append(r[1])
        elif isinstance(r, tuple):
            arr, bidx = r
            arrays.append(arr)
            in_specs.append(pl.BlockSpec((None, tile, arr.shape[-1]), functools.partial(lambda i, b: (b, i, 0), b=bidx)))
        else:
            arrays.append(r)
            in_specs.append(pl.BlockSpec((tile, r.shape[-1]), lambda i: (i, 0)))
    for c in consts:
        arrays.append(c)
        in_specs.append(pl.BlockSpec(c.shape, lambda i: (0, 0)))
    n_in, n_row = len(arrays), len(row_outs)
    out_shape, out_specs = [], []
    for ro in row_outs:
        if ro[0] == "stack":
            _, cnt, blk, total, w, dt = ro
            out_shape.append(jax.ShapeDtypeStruct((total, t, w), dt))
            out_specs.append(pl.BlockSpec((cnt, tile, w), functools.partial(lambda i, b: (b, i, 0), b=blk)))
        else:
            w, dt = ro
            out_shape.append(jax.ShapeDtypeStruct((t, w), dt))
            out_specs.append(pl.BlockSpec((tile, w), lambda i: (i, 0)))
    out_shape += [jax.ShapeDtypeStruct(s, F32) for s in acc_outs]
    out_specs += [pl.BlockSpec(s, lambda i: (0, 0)) for s in acc_outs]
    blocks = [math.prod(d for d in sp.block_shape if d) * arr.dtype.itemsize for sp, arr in zip(in_specs, arrays)]
    blocks += [math.prod(d for d in sp.block_shape if d) * jnp.dtype(sh.dtype).itemsize
               for sp, sh in zip(out_specs, out_shape)]
    need = 2 * sum(blocks) + 6 * tile * max(a.shape[-1] for a in arrays) * 4

    def body(*refs):
        ins, outs = refs[:n_in], refs[n_in:]
        i = pl.program_id(0)
        res = fn(i, steps, *[r[...] for r in ins])
        res = res if isinstance(res, (tuple, list)) else (res,)
        for ref, val in zip(outs[:n_row], res[:n_row]):
            ref[...] = val.astype(ref.dtype)
        if acc_outs:
            @pl.when(i == 0)
            def _():
                for ref in outs[n_row:]:
                    ref[...] = jnp.zeros_like(ref)

            for ref, val in zip(outs[n_row:], res[n_row:]):
                ref[...] += val

    return pl.pallas_call(
        body,
        grid=(steps,),
        in_specs=in_specs,
        out_specs=out_specs,
        out_shape=out_shape,
        compiler_params=_params(("arbitrary",), need),
        name=name,
    )(*arrays)


def _ln_stats(z):
    mu = _mean(z)
    zc = z - mu
    rstd = lax.rsqrt(_mean(zc * zc) + LN_EPS)
    return zc * rstd, rstd


def _ln_bwd(dy, xhat, rstd, g):
    dxh = dy * g
    return rstd * (dxh - _mean(dxh) - xhat * _mean(dxh * xhat))


def _sgu_fwd(h, wm, bs_t, g_v, b_v):
    t = h.shape[1]

    def body(u_ref, v_ref, wm_ref, bs_ref, g_ref, b_ref, y_ref):
        xhat, _ = _ln_stats(_gelu(v_ref[...]))
        vn = (xhat * g_ref[...] + b_ref[...]).astype(BF16)
        gu = _gelu(u_ref[...])
        for g in range(GROUPS):
            sl = slice(g * HEAD, (g + 1) * HEAD)
            mixed = _dot(wm_ref[g], vn[:, sl]) + bs_ref[:, g:g + 1]
            y_ref[:, sl] = (gu[:, sl] * mixed).astype(BF16)

    blk = lambda b: pl.BlockSpec((None, SGU_BLOCK, D_MODEL), functools.partial(lambda i, b: (b, i, 0), b=b))
    whole = lambda s: pl.BlockSpec(s, lambda i: (0,) * len(s))
    return pl.pallas_call(
        body,
        grid=(t // SGU_BLOCK,),
        in_specs=[blk(SLOT_U), blk(SLOT_V), whole(wm.shape), whole(bs_t.shape), whole(g_v.shape), whole(b_v.shape)],
        out_specs=pl.BlockSpec((SGU_BLOCK, D_MODEL), lambda i: (i, 0)),
        out_shape=jax.ShapeDtypeStruct((t, D_MODEL), BF16),
        compiler_params=_params(("parallel",)),
        name="sgu_fwd",
    )(h, h, wm, bs_t, g_v, b_v)


def _sgu_bwd(h, dy, dh, wm, wm_t, bs_t, g_v, b_v):
    t = h.shape[1]

    def body(u_ref, v_ref, dy_ref, dh_in, wm_ref, wmt_ref, bs_ref, g_ref, b_ref,
             duv_ref, dw_ref, dbs_ref, dg_ref, db_ref, dvn_ref):
        del dh_in
        du_ref, dv_ref = duv_ref.at[0], duv_ref.at[1]
        i = pl.program_id(0)

        @pl.when(i == 0)
        def _():
            dw_ref[...] = jnp.zeros_like(dw_ref)
            dbs_ref[...] = jnp.zeros_like(dbs_ref)
            dg_ref[...] = jnp.zeros_like(dg_ref)
            db_ref[...] = jnp.zeros_like(db_ref)

        u = u_ref[...]
        v = v_ref[...]
        xhat, rstd = _ln_stats(_gelu(v))
        vn = (xhat * g_ref[...] + b_ref[...]).astype(BF16)
        gu = _gelu(u)
        gup = _gelu_grad(u)
        lane = lax.broadcasted_iota(jnp.int32, (SGU_BLOCK, LANE), 1)
        dbs = jnp.zeros((SGU_BLOCK, LANE), F32)
        for g in range(GROUPS):
            sl = slice(g * HEAD, (g + 1) * HEAD)
            vn_g = vn[:, sl]
            mixed = _dot(wm_ref[g], vn_g) + bs_ref[:, g:g + 1]
            dy_g = dy_ref[:, sl]
            du_ref[:, sl] = (dy_g * mixed * gup[:, sl]).astype(BF16)
            dmix = dy_g * gu[:, sl]
            dmb = dmix.astype(BF16)
            dvn_ref[:, sl] = _dot(wmt_ref[g], dmb)
            dw_ref[g] += _dot_nt(dmb, vn_g)
            dbs = dbs + jnp.where(lane == g, jnp.sum(dmix, axis=1, keepdims=True), 0.0)
        dbs_ref[...] += dbs
        dvn = dvn_ref[...]
        dg_ref[...] += _sum0(dvn * xhat)
        db_ref[...] += _sum0(dvn)
        dv_ref[...] = (_ln_bwd(dvn, xhat, rstd, g_ref[...]) * _gelu_grad(v)).astype(BF16)

    blk = lambda b: pl.BlockSpec((None, SGU_BLOCK, D_MODEL), functools.partial(lambda i, b: (b, i, 0), b=b))
    row = pl.BlockSpec((SGU_BLOCK, D_MODEL), lambda i: (i, 0))
    whole = lambda s: pl.BlockSpec(s, lambda i: (0,) * len(s))
    vec = (1, D_MODEL)
    return pl.pallas_call(
        body,
        grid=(t // SGU_BLOCK,),
        in_specs=[blk(SLOT_U), blk(SLOT_V), row, pl.BlockSpec(memory_space=pl.ANY),
                  whole(wm.shape), whole(wm_t.shape), whole(bs_t.shape), whole(vec), whole(vec)],
        out_specs=[pl.BlockSpec((2, SGU_BLOCK, D_MODEL), lambda i: (SLOT_U // 2, i, 0)),
                   whole(wm.shape), whole((SGU_BLOCK, LANE)), whole(vec), whole(vec)],
        out_shape=[jax.ShapeDtypeStruct(dh.shape, BF16),
                   jax.ShapeDtypeStruct(wm.shape, F32), jax.ShapeDtypeStruct((SGU_BLOCK, LANE), F32),
                   jax.ShapeDtypeStruct(vec, F32), jax.ShapeDtypeStruct(vec, F32)],
        scratch_shapes=[pltpu.VMEM((SGU_BLOCK, D_MODEL), F32)],
        input_output_aliases={3: 0},
        compiler_params=_params(("arbitrary",)),
        name="sgu_bwd",
    )(h, h, dy, dh, wm, wm_t, bs_t, g_v, b_v)


def _split3(x):
    hi = x.astype(BF16)
    r1 = x - hi.astype(F32)
    mid = r1.astype(BF16)
    lo = (r1 - mid.astype(F32)).astype(BF16)
    return hi, mid, lo


def _tri_matmul(tri, x):
    hi, mid, lo = _split3(x)
    dot = lambda p: jnp.dot(tri, p, preferred_element_type=F32)
    return dot(hi) + dot(mid) + dot(lo)


def _lower_bound(logits):
    l0, l1 = logits[0:1, :], logits[1:2, :]
    mx = jnp.maximum(l0, l1)
    e0, e1 = jnp.exp(l0 - mx), jnp.exp(l1 - mx)
    return e0 / (e0 + e1)


def _hgrn_gates(q_raw, f_raw, lb):
    q = q_raw * jax.nn.sigmoid(q_raw)
    sig = jax.nn.sigmoid(f_raw)
    f = lb + (1.0 - lb) * sig
    row = lax.broadcasted_iota(jnp.int32, (CHUNK, CHUNK), 0)
    col = lax.broadcasted_iota(jnp.int32, (CHUNK, CHUNK), 1)
    c = _tri_matmul((row >= col).astype(BF16), jnp.log(f))
    return q, sig, f, 1.0 - f, c


def _offdiag_terms(qh, kh, ch, tb):
    rows = slice(tb * SUB, (tb + 1) * SUB)
    r = ch[tb * SUB - 1:tb * SUB, :]
    eqh = jnp.exp(ch[rows] - r)
    ekh = jnp.exp(jnp.minimum(r - ch, 0.0))
    return rows, eqh, qh[rows] * eqh, ekh, kh * ekh


def _diag_decay(cb, s, trow):
    return jnp.exp(jnp.where(trow >= s, cb - cb[s:s + 1, :], NEG))


def _split_refs(refs, n_in, n_out, n_scratch, ex):
    nx = ex.n if ex else 0
    ins, refs = refs[:n_in], refs[n_in:]
    xsrc, refs = refs[:nx], refs[nx:]
    outs, refs = refs[:n_out], refs[n_out:]
    xout, refs = refs[:nx], refs[nx:]
    return ins, outs, refs[:n_scratch], (xsrc, xout, refs[n_scratch:])


def _hgrn_fwd(h, logits, g_norm, ex=None):
    t = h.shape[1]
    nc = t // CHUNK

    def body(*refs):
        ins, outs, scratch, xrefs = _split_refs(refs, 6, 3, 4, ex)
        q_ref, f_ref, i_ref, og_ref, lg_ref, gn_ref = ins
        y_ref, o_ref, sall_ref = outs
        st_ref, q_s, k_s, c_s = scratch

        @pl.when(pl.program_id(0) == 0)
        def _():
            st_ref[...] = jnp.zeros_like(st_ref)
            if ex:
                ex.start(*xrefs)

        if ex:
            @pl.when(pl.program_id(0) == (3 * nc) // 4)
            def _():
                ex.relay(*xrefs)

        lb = _lower_bound(lg_ref[...])
        q, _, _, k, c = _hgrn_gates(q_ref[...], f_ref[...], lb)
        q_s[...] = q
        k_s[...] = k
        c_s[...] = c
        col64 = lax.broadcasted_iota(jnp.int32, (SUB, CHUNK), 1)
        trow = lax.broadcasted_iota(jnp.int32, (SUB, HEAD), 0)

        def head(hd, carry):
            sl = pl.ds(pl.multiple_of(hd * HEAD, HEAD), HEAD)
            qh, kh, ch, ih = q_s[:, sl], k_s[:, sl], c_s[:, sl], i_ref[:, sl]
            st = st_ref[hd]
            sall_ref[hd] = st
            c_last = ch[CHUNK - 1:CHUNK, :]
            o = _dot_nt(qh * jnp.exp(ch), st)
            st_ref[hd] = st * jnp.exp(c_last) + _dot_tn(ih, kh * jnp.exp(c_last - ch))
            a_rows = [jnp.zeros((SUB, CHUNK), F32)]
            for tb in range(1, CHUNK // SUB):
                _, _, q_hat, _, k_hat = _offdiag_terms(qh, kh, ch, tb)
                a_rows.append(jnp.where(col64 < tb * SUB, _dot_nt(q_hat, k_hat), 0.0))
            o = o + _dot(jnp.concatenate(a_rows, axis=0), ih)
            o_rows = []
            for b in range(CHUNK // SUB):
                rows = slice(b * SUB, (b + 1) * SUB)
                qb, cb, kb, ib = qh[rows], ch[rows], kh[rows], ih[rows]
                ob = jnp.zeros((SUB, HEAD), F32)
                for s in range(SUB):
                    a = jnp.sum(qb * _diag_decay(cb, s, trow) * kb[s:s + 1, :], axis=1, keepdims=True)
                    ob = ob + a * ib[s:s + 1, :]
                o_rows.append(ob)
            o = o + jnp.concatenate(o_rows, axis=0)
            o_ref[:, sl] = o
            og = og_ref[:, sl]
            on = o * lax.rsqrt(_mean(o * o) + RMS_EPS)
            y_ref[:, sl] = (on * gn_ref[:, sl] * (og * jax.nn.sigmoid(og))).astype(BF16)
            return carry

        lax.fori_loop(0, HEADS, head, 0, unroll=HEAD_UNROLL)

        if ex:
            @pl.when(pl.program_id(0) == nc - 1)
            def _():
                ex.finish(*xrefs)

    blk = lambda b: pl.BlockSpec((None, CHUNK, D_MODEL), functools.partial(lambda n, b: (b, n, 0), b=b))
    row = pl.BlockSpec((CHUNK, D_MODEL), lambda n: (n, 0))
    whole = lambda s: pl.BlockSpec(s, lambda n: (0,) * len(s))
    res = pl.pallas_call(
        body,
        grid=(nc,),
        in_specs=[blk(SLOT_Q), blk(SLOT_F), blk(SLOT_I), blk(SLOT_OG), whole(logits.shape), whole(g_norm.shape)]
        + (ex.in_specs if ex else []),
        out_specs=[row, row, pl.BlockSpec((None, HEADS, HEAD, HEAD), lambda n: (n, 0, 0, 0))] + (ex.out_specs if ex else []),
        out_shape=[jax.ShapeDtypeStruct((t, D_MODEL), BF16), jax.ShapeDtypeStruct((t, D_MODEL), F32),
                   jax.ShapeDtypeStruct((nc, HEADS, HEAD, HEAD), F32)] + (ex.out_shape if ex else []),
        scratch_shapes=[pltpu.VMEM((HEADS, HEAD, HEAD), F32)] + [pltpu.VMEM((CHUNK, D_MODEL), F32)] * 3
        + (ex.scratch if ex else []),
        compiler_params=_params(("arbitrary",)),
        name="hgrn_fwd",
    )(h, h, h, h, logits, g_norm, *(ex.arrays if ex else []))
    return res[0], res[1], res[2], res[3:]


def _hgrn_bwd(h, o_all, dy, states, dh, logits, g_norm, ex=None):
    t = h.shape[1]
    nc = t // CHUNK

    def body(*refs):
        ins, outs, scratch, xrefs = _split_refs(refs, 10, 3, 8, ex)
        q_ref, f_ref, i_ref, og_ref, o_ref, dy_ref, sall_ref, _, lg_ref, gn_ref = ins
        dqfio_ref, dlg_ref, dgn_ref = outs
        dst_ref, dlb_ref, q_s, k_s, c_s, dq_s, dk_s, dc_s = scratch
        dq_ref, df_ref, di_ref, dog_ref = (dqfio_ref.at[s] for s in (SLOT_Q, SLOT_F, SLOT_I, SLOT_OG))
        n = pl.program_id(0)

        @pl.when(n == 0)
        def _():
            dst_ref[...] = jnp.zeros_like(dst_ref)
            dlb_ref[...] = jnp.zeros_like(dlb_ref)
            dgn_ref[...] = jnp.zeros_like(dgn_ref)
            if ex:
                ex.start(*xrefs)

        lb = _lower_bound(lg_ref[...])
        q_raw = q_ref[...]
        q, sig, f, k, c = _hgrn_gates(q_raw, f_ref[...], lb)
        q_s[...] = q
        k_s[...] = k
        c_s[...] = c
        col64 = lax.broadcasted_iota(jnp.int32, (SUB, CHUNK), 1)
        trow = lax.broadcasted_iota(jnp.int32, (SUB, HEAD), 0)
        row64 = lax.broadcasted_iota(jnp.int32, (CHUNK, HEAD), 0)

        def head(hd, carry):
            sl = pl.ds(pl.multiple_of(hd * HEAD, HEAD), HEAD)
            qh, kh, ch, ih = q_s[:, sl], k_s[:, sl], c_s[:, sl], i_ref[:, sl]
            st = sall_ref[hd]
            dst = dst_ref[hd]
            oh, dyh, og, gn = o_ref[:, sl], dy_ref[:, sl], og_ref[:, sl], gn_ref[:, sl]
            sg = jax.nn.sigmoid(og)
            sil = og * sg
            rms = lax.rsqrt(_mean(oh * oh) + RMS_EPS)
            on = oh * rms
            dog_ref[:, sl] = (dyh * on * gn * _silu_grad(og, sg)).astype(BF16)
            dgn_ref[:, sl] += _sum0(dyh * on * sil)
            don = dyh * gn * sil
            do = rms * (don - on * _mean(don * on))
            dob = do.astype(BF16)

            c_last = ch[CHUNK - 1:CHUNK, :]
            eq = jnp.exp(ch)
            q_til = qh * eq
            ekl = jnp.exp(c_last - ch)
            k_til = kh * ekl
            ecl = jnp.exp(c_last)
            dq_til = _dot(dob, st)
            dk_til = _dot(ih, dst)
            di = _dot_nt(k_til, dst)
            dc_last = _sum0(dk_til * k_til) + _sum0(dst * st) * ecl
            dst_ref[hd] = _dot_tn(dob, q_til) + dst * ecl
            dq = dq_til * eq
            dc = dq_til * q_til - dk_til * k_til
            dk = dk_til * ekl

            da_full = _dot_nt(dob, ih)
            a_rows = [jnp.zeros((SUB, CHUNK), F32)]
            dq_rows = [jnp.zeros((SUB, HEAD), F32)]
            dc_rows = [jnp.zeros((SUB, HEAD), F32)]
            for tb in range(1, CHUNK // SUB):
                rows, eqh, q_hat, ekh, k_hat = _offdiag_terms(qh, kh, ch, tb)
                keep = col64 < tb * SUB
                a_rows.append(jnp.where(keep, _dot_nt(q_hat, k_hat), 0.0))
                da = jnp.where(keep, da_full[rows], 0.0)
                dq_hat = _dot(da, k_hat)
                dk_hat = _dot_tn(da, q_hat)
                dq_rows.append(dq_hat * eqh)
                dc_rows.append(dq_hat * q_hat)
                dk = dk + dk_hat * ekh
                dc = dc - dk_hat * k_hat
            di = di + _dot_tn(jnp.concatenate(a_rows, axis=0), dob)

            dk_rows, di_rows = [], []
            for b in range(CHUNK // SUB):
                rows = slice(b * SUB, (b + 1) * SUB)
                qb, cb, kb, ib, dob_ = qh[rows], ch[rows], kh[rows], ih[rows], do[rows]
                dq_diag = jnp.zeros((SUB, HEAD), F32)
                dk_diag = jnp.zeros((SUB, HEAD), F32)
                di_diag = jnp.zeros((SUB, HEAD), F32)
                for s in range(SUB):
                    ks = kb[s:s + 1, :]
                    dec = _diag_decay(cb, s, trow)
                    a = jnp.sum(qb * dec * ks, axis=1, keepdims=True)
                    gk = jnp.sum(dob_ * ib[s:s + 1, :], axis=1, keepdims=True) * dec
                    dq_diag = dq_diag + gk * ks
                    dk_diag = dk_diag + jnp.where(trow == s, _sum0(gk * qb), 0.0)
                    di_diag = di_diag + jnp.where(trow == s, _sum0(a * dob_), 0.0)
                dq_rows[b] = dq_rows[b] + dq_diag
                dc_rows[b] = dc_rows[b] + qb * dq_diag - kb * dk_diag
                dk_rows.append(dk_diag)
                di_rows.append(di_diag)
            dq = dq + jnp.concatenate(dq_rows, axis=0)
            dk = dk + jnp.concatenate(dk_rows, axis=0)
            dc = dc + jnp.concatenate(dc_rows, axis=0) + jnp.where(row64 == CHUNK - 1, dc_last, 0.0)
            di_ref[:, sl] = (di + jnp.concatenate(di_rows, axis=0)).astype(BF16)
            dq_s[:, sl] = dq
            dk_s[:, sl] = dk
            dc_s[:, sl] = dc
            return carry

        lax.fori_loop(0, HEADS, head, 0, unroll=HEAD_UNROLL)

        row = lax.broadcasted_iota(jnp.int32, (CHUNK, CHUNK), 0)
        col = lax.broadcasted_iota(jnp.int32, (CHUNK, CHUNK), 1)
        dlf = _tri_matmul((row <= col).astype(BF16), dc_s[...])
        df = dlf / f - dk_s[...]
        dlb_ref[...] += _sum0(df * (1.0 - sig))
        df_ref[...] = (df * (1.0 - lb) * sig * (1.0 - sig)).astype(BF16)
        dq_ref[...] = (dq_s[...] * _silu_grad(q_raw, jax.nn.sigmoid(q_raw))).astype(BF16)

        @pl.when(n == nc - 1)
        def _():
            d0 = dlb_ref[...] * lb * (1.0 - lb)
            dlg_ref[0:1, :] = d0
            dlg_ref[1:2, :] = -d0
            if ex:
                ex.finish(*xrefs)

    rev = lambda n: nc - 1 - n
    blk = lambda b: pl.BlockSpec((None, CHUNK, D_MODEL), functools.partial(lambda n, b: (b, rev(n), 0), b=b))
    row = pl.BlockSpec((CHUNK, D_MODEL), lambda n: (rev(n), 0))
    whole = lambda s: pl.BlockSpec(s, lambda n: (0,) * len(s))
    vec = (1, D_MODEL)
    res = pl.pallas_call(
        body,
        grid=(nc,),
        in_specs=[blk(SLOT_Q), blk(SLOT_F), blk(SLOT_I), blk(SLOT_OG), row, row,
                  pl.BlockSpec((None, HEADS, HEAD, HEAD), lambda n: (rev(n), 0, 0, 0)),
                  pl.BlockSpec(memory_space=pl.ANY), whole(logits.shape), whole(vec)] + (ex.in_specs if ex else []),
        out_specs=[pl.BlockSpec((4, CHUNK, D_MODEL), lambda n: (SLOT_Q // 4, rev(n), 0)), whole((2, D_MODEL)), whole(vec)]
        + (ex.out_specs if ex else []),
        out_shape=[jax.ShapeDtypeStruct(dh.shape, BF16), jax.ShapeDtypeStruct((2, D_MODEL), F32),
                   jax.ShapeDtypeStruct(vec, F32)] + (ex.out_shape if ex else []),
        scratch_shapes=[pltpu.VMEM((HEADS, HEAD, HEAD), F32), pltpu.VMEM(vec, F32)]
        + [pltpu.VMEM((CHUNK, D_MODEL), F32)] * 6 + (ex.scratch if ex else []),
        input_output_aliases={7: 0},
        compiler_params=_params(("arbitrary",)),
        name="hgrn_bwd",
    )(h, h, h, h, o_all, dy, states, dh, logits, g_norm, *(ex.arrays if ex else []))
    return res[0], res[1], res[2], res[3:]


def _merge_fwd(i, n, ga, gb, za, zb):
    return jax.nn.sigmoid(ga) * za + jax.nn.sigmoid(gb) * zb


def _merge_bwd(i, n, ga, gb, za, zb, dm):
    sa, sb = jax.nn.sigmoid(ga), jax.nn.sigmoid(gb)
    dgates = jnp.stack([(dm * za * sa * (1.0 - sa)).astype(BF16), (dm * zb * sb * (1.0 - sb)).astype(BF16)])
    return dgates, dm * sa, dm * sb


def _ln1_fwd(i, n, x, r1, g, b):
    xhat, _ = _ln_stats(ALPHA * x + r1)
    x1 = xhat * g + b
    return x1, x1


def _ln1_bwd(i, n, x, r1, dx1, g):
    xhat, rstd = _ln_stats(ALPHA * x + r1)
    dz = _ln_bwd(dx1, xhat, rstd, g)
    return dz, dz, _sum0(dx1 * xhat), _sum0(dx1)


def _ln2_loss(i, n, x1, fo, pg, pp, tgt, g, b):
    sg = jax.nn.sigmoid(pg)
    xhat, rstd = _ln_stats(ALPHA * x1 + fo + sg * pp)
    diff = xhat * g + b - tgt
    loss = 0.5 * jnp.sum(_mean(diff * diff), axis=0, keepdims=True)
    dy = diff * (1.0 / D_MODEL)
    dz = _ln_bwd(dy, xhat, rstd, g)
    return (dz, dz, dz * pp * sg * (1.0 - sg), dz * sg,
            jnp.broadcast_to(loss, (8, LANE)), _sum0(dy * xhat), _sum0(dy))


def _shift_down(cur, halo, tile):
    row = lax.broadcasted_iota(jnp.int32, cur.shape, 0)
    m1 = jnp.where(row == 0, halo[7:8, :], pltpu.roll(cur, 1, 0))
    m2 = jnp.where(row == 0, halo[6:7, :], jnp.where(row == 1, halo[7:8, :], pltpu.roll(cur, 2, 0)))
    return m1, m2


def _shift_up(cur, halo, tile):
    row = lax.broadcasted_iota(jnp.int32, cur.shape, 0)
    p1 = jnp.where(row == tile - 1, halo[0:1, :], pltpu.roll(cur, tile - 1, 0))
    p2 = jnp.where(row == tile - 2, halo[0:1, :], jnp.where(row == tile - 1, halo[1:2, :], pltpu.roll(cur, tile - 2, 0)))
    return p1, p2


def _conv_pre(i, gate, halo, w, b, tile):
    halo = jnp.where(i == 0, 0.0, halo)
    m1, m2 = _shift_down(gate, halo, tile)
    return w[0:1, :] * m2 + w[1:2, :] * m1 + w[2:3, :] * gate + b, m1, m2


def _conv_fwd(tile, i, n, gate, halo, val, w, b):
    cg, _, _ = _conv_pre(i, gate, halo, w, b, tile)
    return _gelu(cg) * val


def _conv_bwd_a(tile, i, n, gate, halo, val, dhid, w, b):
    cg, m1, m2 = _conv_pre(i, gate, halo, w, b, tile)
    dcg = dhid * val * _gelu_grad(cg)
    return dcg, dhid * _gelu(cg), _sum0(dcg * m2), _sum0(dcg * m1), _sum0(dcg * gate), _sum0(dcg)


def _conv_bwd_b(tile, i, n, dcg, halo, w):
    halo = jnp.where(i == n - 1, 0.0, halo)
    p1, p2 = _shift_up(dcg, halo, tile)
    return w[2:3, :] * dcg + w[1:2, :] * p1 + w[0:1, :] * p2


def _halo_spec(width, tile, t, nxt):
    per = tile // 8
    last = t // 8 - 1
    if nxt:
        return pl.BlockSpec((8, width), lambda i: (jnp.minimum((i + 1) * per, last), 0))
    return pl.BlockSpec((8, width), lambda i: (jnp.maximum(i * per - 1, 0), 0))


def _adamw(i, n, w, m, v, parts):
    g = parts[0].astype(F32)
    for j in range(1, parts.shape[0]):
        g = g + parts[j].astype(F32)
    m_new = ADAM_B1 * m + (1.0 - ADAM_B1) * g
    v_new = ADAM_B2 * v + (1.0 - ADAM_B2) * (g * g)
    m_hat = m_new / (1.0 - ADAM_B1 ** ADAM_STEP)
    v_hat = v_new / (1.0 - ADAM_B2 ** ADAM_STEP)
    delta = -ADAM_LR * (m_hat / (jnp.sqrt(v_hat) + ADAM_EPS) + ADAM_WD * w)
    return g, delta, m_new, v_new


def _adam_call(w, m, v, parts, name):
    r, c = w.shape
    tile = _pick(r, (256, 128)) if r > 256 else r
    spec = pl.BlockSpec((parts.shape[0], tile, c), lambda i: (0, i, 0))
    return _rowwise(_adamw, [w, m, v, (parts, spec)], [], [(c, F32)] * 4, [], tile=tile, name=name)


def _peer(k):
    x, y, c = lax.axis_index("x"), lax.axis_index("y"), lax.axis_index("c")
    px = x ^ ((k >> 2) & 1)
    py = y ^ ((k >> 1) & 1)
    pc = c ^ (k & 1)
    return (px, py, pc), 4 * px + 2 * py + pc


def _my_index():
    return 4 * lax.axis_index("x") + 2 * lax.axis_index("y") + lax.axis_index("c")


class _Exchange:
    KINDS = ("gather", "gather+relay", "gather+relay+slots", "scatter", "scatter+pairs")

    def __init__(self, entries):
        assert all(k in self.KINDS for _, k in entries), [k for _, k in entries]
        self.arrays = [a for a, _ in entries]
        self.scatter = [k.startswith("scatter") for _, k in entries]
        self.relayed = ["+relay" in k for _, k in entries]
        self.slots = ["+slots" in k for _, k in entries]
        self.pairs = ["+pairs" in k for _, k in entries]
        self.n = len(entries)
        self.in_specs = [pl.BlockSpec(memory_space=pl.ANY)] * self.n
        self.out_specs = [pl.BlockSpec(memory_space=pl.ANY)] * self.n
        shapes = [tuple(a.shape[1:]) if sc else tuple(a.shape) for a, sc in zip(self.arrays, self.scatter)]
        counts = [N_CHIP if p else N_DEV for p in self.pairs]
        self.out_shape = [jax.ShapeDtypeStruct((n,) + s, a.dtype) for n, s, a in zip(counts, shapes, self.arrays)]
        per = N_DEV - 1
        self.scratch = [pltpu.SemaphoreType.DMA((self.n * per,)), pltpu.SemaphoreType.DMA((self.n * per,)),
                        pltpu.SemaphoreType.DMA((self.n,))]

    def _copies(self, srcs, outs, sems):
        send_sems, recv_sems, local_sems = sems
        x, y, c = lax.axis_index("x"), lax.axis_index("y"), lax.axis_index("c")
        me = _my_index()
        per = N_DEV - 1
        local, first, passed, relay_arrivals, arrivals = [], [], [], [], []
        for a in range(self.n):
            place = (lambda idx: _slot_of_group(idx)) if self.slots[a] else (lambda idx: idx)

            def copy(k, src, dst, dev, a=a):
                return pltpu.make_async_remote_copy(
                    src_ref=src, dst_ref=dst, send_sem=send_sems.at[a * per + k], recv_sem=recv_sems.at[a * per + k],
                    device_id=dev, device_id_type=pl.DeviceIdType.MESH)

            if self.pairs[a]:
                chip = 2 * x + y
                for k in range(N_CHIP):
                    to = chip ^ k
                    piece = srcs[a].at[_slot_of_group(2 * to + c) // 2]
                    if k == 0:
                        local.append(pltpu.make_async_copy(piece, outs[a].at[chip], local_sems.at[a]))
                    else:
                        dev = (to // 2, to % 2, c)
                        first.append(copy(k - 1, piece, outs[a].at[chip], dev))
                        arrivals.append(copy(k - 1, piece, outs[a].at[to], dev))
                continue
            if self.scatter[a]:
                mine, land = srcs[a].at[place(me)], outs[a].at[me]
            else:
                mine, land = srcs[a], outs[a].at[place(me)]
            local.append(pltpu.make_async_copy(mine, land, local_sems.at[a]))
            if self.relayed[a]:
                block = lambda px, py, pc, a=a, place=place: outs[a].at[place(4 * px + 2 * py + pc)]
                chips = [(1 - x, y), (x, 1 - y), (1 - x, 1 - y)]
                first.append(copy(0, mine, land, (x, y, 1 - c)))
                arrivals.append(copy(0, mine, block(x, y, 1 - c), (x, y, 1 - c)))
                for j, (px, py) in enumerate(chips):
                    first.append(copy(1 + j, mine, land, (px, py, c)))
                    relay_arrivals.append(copy(1 + j, mine, block(px, py, c), (px, py, c)))
                    passed.append(copy(4 + j, block(px, py, c), block(px, py, c), (x, y, 1 - c)))
                    arrivals.append(copy(4 + j, mine, block(px, py, 1 - c), (x, y, 1 - c)))
                continue
            for k in range(1, N_DEV):
                dev, idx = _peer(k)
                if self.scatter[a]:
                    first.append(copy(k - 1, srcs[a].at[place(idx)], land, dev))
                    arrivals.append(copy(k - 1, mine, outs[a].at[idx], dev))
                else:
                    first.append(copy(k - 1, mine, land, dev))
                    arrivals.append(copy(k - 1, mine, outs[a].at[place(idx)], dev))
        return local, first, passed, relay_arrivals, arrivals

    def start(self, srcs, outs, sems):
        local, first, _, _, _ = self._copies(srcs, outs, sems)
        for cp in local + first:
            cp.start()

    def relay(self, srcs, outs, sems):
        _, _, passed, relay_arrivals, _ = self._copies(srcs, outs, sems)
        for landed, onward in zip(relay_arrivals, passed):
            landed.wait_recv()
            onward.start()

    def finish(self, srcs, outs, sems):
        local, first, passed, _, arrivals = self._copies(srcs, outs, sems)
        for cp in arrivals:
            cp.wait_recv()
        for cp in first + passed:
            cp.wait_send()
        for cp in local:
            cp.wait()


def _pair_sums(g):
    n, r, cols = g.shape
    half = n // 2

    def body(g_ref, own_ref, got_ref, send_sems, recv_sems, local_sems):
        x, y, c = lax.axis_index("x"), lax.axis_index("y"), lax.axis_index("c")
        copies = []
        for j in range(half):
            copies.append(pltpu.make_async_copy(g_ref.at[2 * j + c], own_ref.at[j], local_sems.at[j]))
            copies.append(pltpu.make_async_remote_copy(
                src_ref=g_ref.at[2 * j + 1 - c], dst_ref=got_ref.at[j], send_sem=send_sems.at[j], recv_sem=recv_sems.at[j],
                device_id=(x, y, 1 - c), device_id_type=pl.DeviceIdType.MESH))
        for cp in copies:
            cp.start()
        for cp in copies:
            cp.wait()

    kept = jax.ShapeDtypeStruct((half, r, cols), g.dtype)
    own, got = pl.pallas_call(
        body,
        in_specs=[pl.BlockSpec(memory_space=pl.ANY)],
        out_specs=[pl.BlockSpec(memory_space=pl.ANY)] * 2,
        out_shape=[kept, kept],
        scratch_shapes=[pltpu.SemaphoreType.DMA((half,))] * 3,
        name="pair_swap",
    )(g)
    add = lambda i, n, a, b: a.astype(F32) + b.astype(F32)
    pair, = _rowwise(add, [own.reshape(half * r, cols), got.reshape(half * r, cols)], [], [(cols, g.dtype)], [],
                     tile=_pick(half * r, (512, 256, 128)), name="pair_add")
    return pair.reshape(half, r, cols)


def _exchange(entries, name):
    ex = _Exchange(entries)
    n = ex.n

    def body(*refs):
        xrefs = refs[:n], refs[n:2 * n], refs[2 * n:]
        ex.start(*xrefs)
        ex.relay(*xrefs)
        ex.finish(*xrefs)

    return pl.pallas_call(body, in_specs=ex.in_specs, out_specs=ex.out_specs, out_shape=ex.out_shape,
                          scratch_shapes=ex.scratch, name=name)(*ex.arrays)


def _local_step(x, p, tgt, w_in, small, comm):
    t = x.shape[0]
    tile = _pick(t, (256, 128))
    d = D_MODEL
    act_b, act_f = (d, BF16), (d, F32)
    x_b, p_b = x.astype(BF16), p.astype(BF16)

    chunk_id = jnp.arange(SGU_BLOCK) // CHUNK
    mask = chunk_id[:, None] >= chunk_id[None, :]
    wm = jnp.where(mask[None], small["sgu_w_s"], 0.0)
    wm_b = wm.astype(BF16)
    wm_t = jnp.swapaxes(wm, 1, 2).astype(BF16)
    bs_t = small["sgu_b_s"].T

    h = _mm(x_b, w_in, out_dtype=F32, name="mm_h")
    y_a = _sgu_fwd(h, wm_b, bs_t, small["sgu_norm_g"], small["sgu_norm_b"])
    y_b, o_all, states, got = _hgrn_fwd(h, small["lb_logits"], small["hgrn_norm_g"], ex=comm.weights_exchange())
    wts, conv_w = comm.weights(got)
    z_a = _mm(y_a, wts["w_a"], out_dtype=F32, name="mm_za")
    z_b = _mm(y_b, wts["w_b"], out_dtype=F32, name="mm_zb")
    gates = [(h, SLOT_GA), (h, SLOT_GB)]
    merged, = _rowwise(_merge_fwd, gates + [z_a, z_b], [], [act_b], [], tile=tile, name="merge_fwd")
    r1 = _mm(merged, wts["w_o"], out_dtype=F32, name="mm_r1")
    x1, x1_b = _rowwise(_ln1_fwd, [x, r1], [small["ln1_g"], small["ln1_b"]], [act_f, act_b], [], tile=tile, name="ln1_fwd")
    gate = _mm(x1_b, wts["w_g"], out_dtype=F32, name="mm_gate")
    val = _mm(x1_b, wts["w_v"], out_dtype=F32, name="mm_val")
    pg = _mm(x1_b, wts["w_pg"], out_dtype=F32, name="mm_pg")
    pp = _mm(p_b, wts["w_pp"], out_dtype=F32, name="mm_pp")
    hid, = _rowwise(functools.partial(_conv_fwd, tile), [gate, (gate, _halo_spec(D_FF, tile, t, False)), val],
                    [conv_w, small["conv_b"]], [(D_FF, BF16)], [], tile=tile, name="conv_fwd")
    fo = _mm(hid, wts["w_down"], out_dtype=F32, name="mm_down")
    dz2, dz2_b, dpg, dpp, loss, dg2, db2 = _rowwise(
        _ln2_loss, [x1, fo, pg, pp, tgt], [small["ln2_g"], small["ln2_b"]],
        [act_f, act_b, act_b, act_b], [(8, LANE), (1, d), (1, d)], tile=tile, name="ln2_loss")

    dhid = _mm(dz2_b, wts["w_down"], out_dtype=F32, name="mm_dhid", trans_b=True)
    g_down = _mm_tn(hid, dz2_b, out_dtype=BF16, name="mm_g_down")
    dcg, dval, dcw0, dcw1, dcw2, dcb = _rowwise(
        functools.partial(_conv_bwd_a, tile), [gate, (gate, _halo_spec(D_FF, tile, t, False)), val, dhid],
        [conv_w, small["conv_b"]], [(D_FF, F32), (D_FF, BF16)], [(1, D_FF)] * 4, tile=tile, name="conv_bwd_a")
    dgate, = _rowwise(functools.partial(_conv_bwd_b, tile), [dcg, (dcg, _halo_spec(D_FF, tile, t, True))],
                      [conv_w], [(D_FF, BF16)], [], tile=tile, name="conv_bwd_b")
    g_g = _mm_tn(x1_b, dgate, out_dtype=BF16, name="mm_g_gate")
    g_v = _mm_tn(x1_b, dval, out_dtype=BF16, name="mm_g_val")
    g_pg = _mm_tn(x1_b, dpg, out_dtype=BF16, name="mm_g_pg")
    g_pp = _mm_tn(p_b, dpp, out_dtype=BF16, name="mm_g_pp")
    dx1 = _mm(dgate, wts["w_g"], out_dtype=F32, name="mm_dx1_gate", trans_b=True, adds=[(dz2, ALPHA)])
    dx1 = _mm(dval, wts["w_v"], out_dtype=F32, name="mm_dx1_val", trans_b=True, adds=[(dx1, 1.0)])
    dx1 = _mm(dpg, wts["w_pg"], out_dtype=F32, name="mm_dx1_pg", trans_b=True, adds=[(dx1, 1.0)])
    dz1, dz1_b, dg1, db1 = _rowwise(_ln1_bwd, [x, r1, dx1], [small["ln1_g"]], [act_f, act_b], [(1, d), (1, d)],
                                    tile=tile, name="ln1_bwd")
    g_o = _mm_tn(merged, dz1_b, out_dtype=BF16, name="mm_g_o")
    dm = _mm(dz1_b, wts["w_o"], out_dtype=F32, name="mm_dm", trans_b=True)
    dh, dza, dzb = _rowwise(_merge_bwd, gates + [z_a, z_b, dm], [],
                            [("stack", 2, SLOT_GA // 2, 8, d, BF16), act_b, act_b], [], tile=tile, name="merge_bwd")
    g_a = _mm_tn(y_a, dza, out_dtype=BF16, name="mm_g_a")
    g_b = _mm_tn(y_b, dzb, out_dtype=BF16, name="mm_g_b")
    dy_a = _mm(dza, wts["w_a"], out_dtype=F32, name="mm_dya", trans_b=True)
    dy_b = _mm(dzb, wts["w_b"], out_dtype=F32, name="mm_dyb", trans_b=True)
    dh, dws, dbs, dgv_n, dbv_n = _sgu_bwd(h, dy_a, dh, wm_b, wm_t, bs_t, small["sgu_norm_g"], small["sgu_norm_b"])
    big = dict(w_a=g_a, w_b=g_b, w_o=g_o, w_g=g_g, w_v=g_v, w_down=g_down, w_pp=g_pp, w_pg=g_pg)
    sm = dict(sgu_w_s=jnp.where(mask[None], dws, 0.0), sgu_b_s=dbs[:, :GROUPS].T, sgu_norm_g=dgv_n, sgu_norm_b=dbv_n,
              ln1_g=dg1, ln1_b=db1, conv_w=jnp.concatenate([dcw0, dcw1, dcw2], axis=0), conv_b=dcb, ln2_g=dg2, ln2_b=db2,
              loss=loss)
    dh, dlogits, dgn, got = _hgrn_bwd(h, o_all, dy_b, states, dh, small["lb_logits"], small["hgrn_norm_g"],
                                      ex=comm.grads_exchange(big, sm))
    comm.grads_done(got)
    g_in = _mm_tn(x_b, dh, out_dtype=BF16, name="mm_g_in")
    ex = comm.last_exchange(g_in, dict(lb_logits=dlogits, hgrn_norm_g=dgn))
    res = _mm(dh, w_in, out_dtype=F32, name="mm_dx", trans_b=True, reduce_b=True, adds=[(dz1, ALPHA)], ex=ex)
    grad_x, got = res if ex else (res, ())
    comm.last_done(got)
    return grad_x


_SMALL_EARLY = ["sgu_w_s", "sgu_b_s", "sgu_norm_g", "sgu_norm_b", "ln1_g", "ln1_b", "ffn_conv_b", "ln2_g", "ln2_b"]
_SMALL_LATE = ["hgrn_lb_logits", "hgrn_norm_g"]
N_TAPS = D_FF // N_DEV
UP_COLS = 2 * D_FF // N_DEV


class _StepExchanges:
    def __init__(self, shards):
        self.shards = shards

    def weights_exchange(self):
        return _Exchange([(s, "gather+relay") for s in self.shards])

    def weights(self, got):
        d, f = D_MODEL, D_FF
        w_br_g, w_o_g, w_up_g, w_down_g, w_pp_g, w_pg_g, conv_g = got
        w_br = w_br_g.transpose(1, 0, 2, 3).reshape(2, d, d)
        w_up = w_up_g.transpose(1, 0, 2).reshape(d, 2, f).transpose(1, 0, 2)
        wts = dict(w_a=w_br[0], w_b=w_br[1], w_o=w_o_g.reshape(d, d), w_g=w_up[0], w_v=w_up[1],
                   w_down=w_down_g.reshape(f, d), w_pp=w_pp_g.transpose(1, 0, 2).reshape(256, d), w_pg=w_pg_g.reshape(d, d))
        return wts, conv_g.transpose(1, 0, 2).reshape(3, f)

    def grads_exchange(self, big, sm):
        d = D_MODEL
        parts = [jnp.stack([big["w_a"], big["w_b"]]).reshape(2, N_DEV, 128, d).transpose(1, 0, 2, 3),
                 big["w_o"].reshape(N_DEV, 128, d),
                 jnp.concatenate([big["w_g"], big["w_v"]], axis=1).reshape(d, N_DEV, UP_COLS).transpose(1, 0, 2),
                 big["w_down"].reshape(N_DEV, N_TAPS, d),
                 big["w_pp"].reshape(256, N_DEV, 128).transpose(1, 0, 2),
                 big["w_pg"].reshape(N_DEV, 128, d)]
        packed, self.rows_early = _pack([sm[k] for k in ("sgu_w_s", "sgu_b_s", "sgu_norm_g", "sgu_norm_b", "ln1_g", "ln1_b",
                                                         "conv_b", "ln2_g", "ln2_b", "conv_w", "loss")])
        return _Exchange([(a, "scatter") for a in parts] + [(packed, "gather")])

    def grads_done(self, got):
        self.recv, self.small_early = got[:6], got[6]

    def last_exchange(self, g_in, sm):
        packed, self.rows_late = _pack([sm["lb_logits"], sm["hgrn_norm_g"]])
        return _Exchange([(_pair_sums(g_in), "scatter+pairs"), (packed, "gather")])

    def last_done(self, got):
        self.recv_in, self.small_late = got


def _rows128(a):
    flat = a.reshape(-1)
    rows = -(-flat.shape[0] // (8 * LANE)) * 8
    return jnp.pad(flat, (0, rows * LANE - flat.shape[0])).reshape(rows, LANE)


def _pack(parts):
    blocks = [_rows128(a) for a in parts]
    return jnp.concatenate(blocks, axis=0), [b.shape[0] for b in blocks]


def _unpack(packed, shapes, rows):
    out, r0 = [], 0
    for shp, r in zip(shapes, rows):
        n = math.prod(shp)
        out.append(packed[r0:r0 + r].reshape(-1)[:n].reshape(shp))
        r0 += r
    return out


def kernel(x, p, w_in, sgu_w_s, sgu_b_s, sgu_norm_g, sgu_norm_b, hgrn_lb_logits, hgrn_norm_g, w_branch, w_out, ln1_g, ln1_b, ffn_w_up, ffn_conv_w, ffn_conv_b, ffn_w_down, ln2_g, ln2_b, ple_w_proj, ple_w_gate, loss_target, m_w_in, m_sgu_w_s, m_sgu_b_s, m_sgu_norm_g, m_sgu_norm_b, m_hgrn_lb_logits, m_hgrn_norm_g, m_w_branch, m_w_out, m_ln1_g, m_ln1_b, m_ffn_w_up, m_ffn_conv_w, m_ffn_conv_b, m_ffn_w_down, m_ln2_g, m_ln2_b, m_ple_w_proj, m_ple_w_gate, v_w_in, v_sgu_w_s, v_sgu_b_s, v_sgu_norm_g, v_sgu_norm_b, v_hgrn_lb_logits, v_hgrn_norm_g, v_w_branch, v_w_out, v_ln1_g, v_ln1_b, v_ffn_w_up, v_ffn_conv_w, v_ffn_conv_b, v_ffn_w_down, v_ln2_g, v_ln2_b, v_ple_w_proj, v_ple_w_gate):
    weights = dict(w_in=w_in, sgu_w_s=sgu_w_s, sgu_b_s=sgu_b_s, sgu_norm_g=sgu_norm_g, sgu_norm_b=sgu_norm_b,
                   hgrn_lb_logits=hgrn_lb_logits, hgrn_norm_g=hgrn_norm_g, w_branch=w_branch, w_out=w_out,
                   ln1_g=ln1_g, ln1_b=ln1_b, ffn_w_up=ffn_w_up, ffn_conv_w=ffn_conv_w, ffn_conv_b=ffn_conv_b,
                   ffn_w_down=ffn_w_down, ln2_g=ln2_g, ln2_b=ln2_b, ple_w_proj=ple_w_proj, ple_w_gate=ple_w_gate)
    mom_m = dict(w_in=m_w_in, sgu_w_s=m_sgu_w_s, sgu_b_s=m_sgu_b_s, sgu_norm_g=m_sgu_norm_g, sgu_norm_b=m_sgu_norm_b,
                 hgrn_lb_logits=m_hgrn_lb_logits, hgrn_norm_g=m_hgrn_norm_g, w_branch=m_w_branch, w_out=m_w_out,
                 ln1_g=m_ln1_g, ln1_b=m_ln1_b, ffn_w_up=m_ffn_w_up, ffn_conv_w=m_ffn_conv_w, ffn_conv_b=m_ffn_conv_b,
                 ffn_w_down=m_ffn_w_down, ln2_g=m_ln2_g, ln2_b=m_ln2_b, ple_w_proj=m_ple_w_proj, ple_w_gate=m_ple_w_gate)
    mom_v = dict(w_in=v_w_in, sgu_w_s=v_sgu_w_s, sgu_b_s=v_sgu_b_s, sgu_norm_g=v_sgu_norm_g, sgu_norm_b=v_sgu_norm_b,
                 hgrn_lb_logits=v_hgrn_lb_logits, hgrn_norm_g=v_hgrn_norm_g, w_branch=v_w_branch, w_out=v_w_out,
                 ln1_g=v_ln1_g, ln1_b=v_ln1_b, ffn_w_up=v_ffn_w_up, ffn_conv_w=v_ffn_conv_w, ffn_conv_b=v_ffn_conv_b,
                 ffn_w_down=v_ffn_w_down, ln2_g=v_ln2_g, ln2_b=v_ln2_b, ple_w_proj=v_ple_w_proj, ple_w_gate=v_ple_w_gate)
    d, f = D_MODEL, D_FF
    me = _my_index()

    w_in_g, = _exchange([(w_in[0].astype(BF16), "gather+relay+slots")], "gather_w_in")
    comm = _StepExchanges([w_branch[0].astype(BF16), w_out[0].astype(BF16), ffn_w_up[0].astype(BF16),
                           ffn_w_down[0].astype(BF16), ple_w_proj[0].astype(BF16), ple_w_gate[0].astype(BF16), ffn_conv_w[0]])
    small = dict(sgu_w_s=sgu_w_s[0], sgu_b_s=sgu_b_s[0], sgu_norm_g=sgu_norm_g, sgu_norm_b=sgu_norm_b,
                 lb_logits=hgrn_lb_logits, hgrn_norm_g=hgrn_norm_g, ln1_g=ln1_g, ln1_b=ln1_b, ln2_g=ln2_g, ln2_b=ln2_b,
                 conv_b=ffn_conv_b)
    grad_x = _local_step(x[0], p[0, 0], loss_target[0], w_in_g, small, comm)

    out = {}

    def adam(name, parts8, shape2d):
        w2, m2, v2 = (a.reshape(shape2d) for a in (weights[name], mom_m[name], mom_v[name]))
        res = _adam_call(w2, m2, v2, parts8.reshape(parts8.shape[:1] + shape2d), "adam_" + name)
        out[name] = tuple(r.reshape(weights[name].shape) for r in res)

    adam("w_in", comm.recv_in, (d, d))
    adam("w_branch", comm.recv[0], (256, d))
    adam("w_out", comm.recv[1], (128, d))
    adam("ffn_w_up", comm.recv[2], (d, UP_COLS))
    adam("ffn_w_down", comm.recv[3], (N_TAPS, d))
    adam("ple_w_proj", comm.recv[4], (256, 128))
    adam("ple_w_gate", comm.recv[5], (128, d))

    def adam_small(names, extra_w, extra_m, extra_v, extra_shapes, parts8, rows, label):
        pk = lambda src, extra: _pack([src[n] for n in names] + extra)[0]
        res = _adam_call(pk(weights, extra_w), pk(mom_m, extra_m), pk(mom_v, extra_v), parts8, label)
        shapes = [weights[n].shape for n in names] + extra_shapes
        unpacked = [_unpack(r, shapes, rows) for r in res]
        for j, n in enumerate(names):
            out[n] = tuple(u[j] for u in unpacked)
        return [[u[len(names) + j] for u in unpacked] for j in range(len(extra_shapes))]

    blank = jnp.zeros((8, LANE), F32)
    taps, loss_rows = adam_small(
        _SMALL_EARLY, [_place_taps(ffn_conv_w[0], me, f), blank], [_place_taps(m_ffn_conv_w[0], me, f), blank],
        [_place_taps(v_ffn_conv_w[0], me, f), blank + 1.0], [(3, f), (8, LANE)], comm.small_early, comm.rows_early,
        "adam_small_early")
    adam_small(_SMALL_LATE, [], [], [], [], comm.small_late, comm.rows_late, "adam_small_late")
    out["ffn_conv_w"] = tuple(lax.dynamic_slice_in_dim(u, me * N_TAPS, N_TAPS, axis=1)[None] for u in taps)
    loss = loss_rows[0][0, 0]

    order = ["w_in", "sgu_w_s", "sgu_b_s", "sgu_norm_g", "sgu_norm_b", "hgrn_lb_logits", "hgrn_norm_g", "w_branch", "w_out",
             "ln1_g", "ln1_b", "ffn_w_up", "ffn_conv_w", "ffn_conv_b", "ffn_w_down", "ln2_g", "ln2_b", "ple_w_proj", "ple_w_gate"]
    return (loss, grad_x[None], *[out[n][0] for n in order], *[out[n][1] for n in order],
            *[out[n][2] for n in order], *[out[n][3] for n in order])


def _place_taps(shard, me, f):
    return lax.dynamic_update_slice_in_dim(jnp.zeros((3, f), F32), shard, me * N_TAPS, axis=1)
```

```python
import functools
import math

import jax
import jax.numpy as jnp
from jax import lax
from jax.experimental import pallas as pl
from jax.experimental.pallas import tpu as pltpu

F32 = jnp.float32
BF16 = jnp.bfloat16

N_DEV = 8
N_CHIP = 4
D_MODEL = 1024
CHUNK = 64
SUB = 16
SGU_BLOCK = 128
GROUPS = 8
HEAD = 128
HEADS = 8
HEAD_UNROLL = 4
D_FF = 2816
LN_EPS = 1e-5
RMS_EPS = 1e-6
ALPHA = 2.0 ** 0.25
GELU_K = math.sqrt(2.0 / math.pi)
GELU_C = 0.044715
NEG = -1e30
ADAM_LR, ADAM_B1, ADAM_B2, ADAM_EPS, ADAM_WD, ADAM_STEP = 0.001, 0.9, 0.999, 1e-08, 0.01, 10
LANE = 128
SLOT_Q, SLOT_F, SLOT_I, SLOT_OG, SLOT_U, SLOT_V, SLOT_GA, SLOT_GB = range(8)


def _slot_of_group(k):
    return jnp.where(k < 2, k + 4, jnp.where(k < 6, k - 2, k))


MIB = 1024 * 1024
VMEM_V7X = 64 * MIB
VMEM_FLOOR = 32 * MIB
MM_TILES = (1024, 1408, 512, 256, 128)


def _params(sem, need=0):
    limit = min(max(need + need // 4, VMEM_FLOOR), VMEM_V7X - 4 * MIB)
    return pltpu.CompilerParams(dimension_semantics=sem, vmem_limit_bytes=limit)


def _pick(n, prefs):
    for t in prefs:
        if n % t == 0:
            return t
    return n


def _gelu(x):
    return 0.5 * x * (1.0 + jnp.tanh(GELU_K * (x + GELU_C * x * x * x)))


def _gelu_grad(x):
    t = jnp.tanh(GELU_K * (x + GELU_C * x * x * x))
    return 0.5 * (1.0 + t) + 0.5 * x * (1.0 - t * t) * GELU_K * (1.0 + 3.0 * GELU_C * x * x)


def _silu_grad(x, s):
    return s * (1.0 + x * (1.0 - s))


def _dot(a, b):
    return jnp.dot(a.astype(BF16), b.astype(BF16), preferred_element_type=F32)


def _dot_nt(a, b):
    return lax.dot_general(a.astype(BF16), b.astype(BF16), (((1,), (1,)), ((), ())), preferred_element_type=F32)


def _dot_tn(a, b):
    return lax.dot_general(a.astype(BF16), b.astype(BF16), (((0,), (0,)), ((), ())), preferred_element_type=F32)


def _mean(x):
    return jnp.mean(x, axis=-1, keepdims=True)


def _sum0(x):
    return jnp.sum(x, axis=0, keepdims=True)


def _mm(a, b, *, out_dtype, name, trans_b=False, reduce_b=False, adds=(), ex=None):
    squeeze = b.ndim == 2
    a3 = a if a.ndim == 3 else a[None]
    b3 = b if b.ndim == 3 else b[None]
    ba, m, k = a3.shape
    bb = b3.shape[0]
    n = b3.shape[1] if trans_b else b3.shape[2]
    tm = _pick(m, MM_TILES)
    tn = _pick(n, MM_TILES)
    if reduce_b:
        bo, steps = 1, bb
        a_map = lambda o, i, j, r: (r if ba > 1 else 0, i, 0)
        b_map = (lambda o, i, j, r: (r, j, 0)) if trans_b else (lambda o, i, j, r: (r, 0, j))
    else:
        bo, steps = bb, 1
        a_map = lambda o, i, j, r: (o if ba > 1 else 0, i, 0)
        b_map = (lambda o, i, j, r: (o, j, 0)) if trans_b else (lambda o, i, j, r: (o, 0, j))
    o_map = lambda o, i, j, r: (o, i, j)
    add_arrays = [x if x.ndim == 3 else x[None] for x, _ in adds]
    add_scales = [s for _, s in adds]
    n_add = len(adds)
    dot = _dot_nt if trans_b else _dot

    def finish(acc, add_refs, o_ref):
        for ref, s in zip(add_refs, add_scales):
            acc = acc + s * ref[...].astype(F32)
        o_ref[...] = acc.astype(o_ref.dtype)

    grid = (bo, m // tm, n // tn, steps)

    def body(*refs):
        ins, (o_ref,), scratch, xrefs = _split_refs(refs, 2 + n_add, 1, 1 if reduce_b else 0, ex)
        a_ref, b_ref, add_refs = ins[0], ins[1], ins[2:]
        step = ((pl.program_id(0) * grid[1] + pl.program_id(1)) * grid[2] + pl.program_id(2)) * grid[3] + pl.program_id(3)
        if ex:
            @pl.when(step == 0)
            def _():
                ex.start(*xrefs)

        if reduce_b:
            acc, = scratch
            r = pl.program_id(3)

            @pl.when(r == 0)
            def _():
                acc[...] = jnp.zeros_like(acc)

            acc[...] += dot(a_ref[...], b_ref[...])

            @pl.when(r == steps - 1)
            def _():
                finish(acc[...], add_refs, o_ref)
        else:
            finish(dot(a_ref[...], b_ref[...]), add_refs, o_ref)

        if ex:
            @pl.when(step == math.prod(grid) - 1)
            def _():
                ex.finish(*xrefs)

    b_block = (None, tn, k) if trans_b else (None, k, tn)
    out_bytes = tm * tn * jnp.dtype(out_dtype).itemsize
    need = 2 * (tm * k * a3.dtype.itemsize + k * tn * b3.dtype.itemsize + out_bytes + n_add * tm * tn * 4)
    need += 2 * tm * tn * 4
    sem = ("arbitrary",) * 4 if ex else ("parallel", "parallel", "parallel", "arbitrary")
    res = pl.pallas_call(
        body,
        grid=grid,
        in_specs=[pl.BlockSpec((None, tm, k), a_map), pl.BlockSpec(b_block, b_map)]
        + [pl.BlockSpec((None, tm, tn), o_map) for _ in adds] + (ex.in_specs if ex else []),
        out_specs=[pl.BlockSpec((None, tm, tn), o_map)] + (ex.out_specs if ex else []),
        out_shape=[jax.ShapeDtypeStruct((bo, m, n), out_dtype)] + (ex.out_shape if ex else []),
        scratch_shapes=([pltpu.VMEM((tm, tn), F32)] if reduce_b else []) + (ex.scratch if ex else []),
        compiler_params=_params(sem, need),
        name=name,
    )(a3, b3, *add_arrays, *(ex.arrays if ex else []))
    out = res[0][0] if (reduce_b or squeeze) else res[0]
    return (out, res[1:]) if ex else out


def _mm_tn(a, b, *, out_dtype, name):
    squeeze = b.ndim == 2
    b3 = b if b.ndim == 3 else b[None]
    t, m = a.shape
    bb, _, n = b3.shape
    tm = _pick(m, MM_TILES)
    tn = _pick(n, MM_TILES)
    tt = _pick(t, (1024, 512, 256, 128))
    steps = t // tt
    need = 2 * (tt * tm * a.dtype.itemsize + tt * tn * b3.dtype.itemsize + tm * tn * jnp.dtype(out_dtype).itemsize)
    need += 2 * tm * tn * 4

    def body(a_ref, b_ref, o_ref, acc):
        r = pl.program_id(3)

        @pl.when(r == 0)
        def _():
            acc[...] = jnp.zeros_like(acc)

        acc[...] += _dot_tn(a_ref[...], b_ref[...])

        @pl.when(r == steps - 1)
        def _():
            o_ref[...] = acc[...].astype(o_ref.dtype)

    out = pl.pallas_call(
        body,
        grid=(bb, m // tm, n // tn, steps),
        in_specs=[pl.BlockSpec((tt, tm), lambda o, i, j, r: (r, i)),
                  pl.BlockSpec((None, tt, tn), lambda o, i, j, r: (o, r, j))],
        out_specs=pl.BlockSpec((None, tm, tn), lambda o, i, j, r: (o, i, j)),
        out_shape=jax.ShapeDtypeStruct((bb, m, n), out_dtype),
        scratch_shapes=[pltpu.VMEM((tm, tn), F32)],
        compiler_params=_params(("parallel", "parallel", "parallel", "arbitrary"), need),
        name=name,
    )(a, b3)
    return out[0] if squeeze else out


def _rowwise(fn, rows, consts, row_outs, acc_outs, *, tile, name):
    first = rows[0][0] if isinstance(rows[0], tuple) else rows[0]
    t = first.shape[-2]
    steps = t // tile
    arrays, in_specs = [], []
    for r in rows:
        if isinstance(r, tuple) and isinstance(r[1], pl.BlockSpec):
            arrays.append(r[0])
            in_specs.append(r[1])
        elif isinstance(r, tuple):
            arr, bidx = r
            arrays.append(arr)
            in_specs.append(pl.BlockSpec((None, tile, arr.shape[-1]), functools.partial(lambda i, b: (b, i, 0), b=bidx)))
        else:
            arrays.append(r)
            in_specs.append(pl.BlockSpec((tile, r.shape[-1]), lambda i: (i, 0)))
    for c in consts:
        arrays.append(c)
        in_specs.append(pl.BlockSpec(c.shape, lambda i: (0, 0)))
    n_in, n_row = len(arrays), len(row_outs)
    out_shape, out_specs = [], []
    for ro in row_outs:
        if ro[0] == "stack":
            _, cnt, blk, total, w, dt = ro
            out_shape.append(jax.ShapeDtypeStruct((total, t, w), dt))
            out_specs.append(pl.BlockSpec((cnt, tile, w), functools.partial(lambda i, b: (b, i, 0), b=blk)))
        else:
            w, dt = ro
            out_shape.append(jax.ShapeDtypeStruct((t, w), dt))
            out_specs.append(pl.BlockSpec((tile, w), lambda i: (i, 0)))
    out_shape += [jax.ShapeDtypeStruct(s, F32) for s in acc_outs]
    out_specs += [pl.BlockSpec(s, lambda i: (0, 0)) for s in acc_outs]
    blocks = [math.prod(d for d in sp.block_shape if d) * arr.dtype.itemsize for sp, arr in zip(in_specs, arrays)]
    blocks += [math.prod(d for d in sp.block_shape if d) * jnp.dtype(sh.dtype).itemsize
               for sp, sh in zip(out_specs, out_shape)]
    need = 2 * sum(blocks) + 6 * tile * max(a.shape[-1] for a in arrays) * 4

    def body(*refs):
        ins, outs = refs[:n_in], refs[n_in:]
        i = pl.program_id(0)
        res = fn(i, steps, *[r[...] for r in ins])
        res = res if isinstance(res, (tuple, list)) else (res,)
        for ref, val in zip(outs[:n_row], res[:n_row]):
            ref[...] = val.astype(ref.dtype)
        if acc_outs:
            @pl.when(i == 0)
            def _():
                for ref in outs[n_row:]:
                    ref[...] = jnp.zeros_like(ref)

            for ref, val in zip(outs[n_row:], res[n_row:]):
                ref[...] += val

    return pl.pallas_call(
        body,
        grid=(steps,),
        in_specs=in_specs,
        out_specs=out_specs,
        out_shape=out_shape,
        compiler_params=_params(("arbitrary",), need),
        name=name,
    )(*arrays)


def _ln_stats(z):
    mu = _mean(z)
    zc = z - mu
    rstd = lax.rsqrt(_mean(zc * zc) + LN_EPS)
    return zc * rstd, rstd


def _ln_bwd(dy, xhat, rstd, g):
    dxh = dy * g
    return rstd * (dxh - _mean(dxh) - xhat * _mean(dxh * xhat))


def _sgu_fwd(h, wm, bs_t, g_v, b_v):
    t = h.shape[1]

    def body(u_ref, v_ref, wm_ref, bs_ref, g_ref, b_ref, y_ref):
        xhat, _ = _ln_stats(_gelu(v_ref[...]))
        vn = (xhat * g_ref[...] + b_ref[...]).astype(BF16)
        gu = _gelu(u_ref[...])
        for g in range(GROUPS):
            sl = slice(g * HEAD, (g + 1) * HEAD)
            mixed = _dot(wm_ref[g], vn[:, sl]) + bs_ref[:, g:g + 1]
            y_ref[:, sl] = (gu[:, sl] * mixed).astype(BF16)

    blk = lambda b: pl.BlockSpec((None, SGU_BLOCK, D_MODEL), functools.partial(lambda i, b: (b, i, 0), b=b))
    whole = lambda s: pl.BlockSpec(s, lambda i: (0,) * len(s))
    return pl.pallas_call(
        body,
        grid=(t // SGU_BLOCK,),
        in_specs=[blk(SLOT_U), blk(SLOT_V), whole(wm.shape), whole(bs_t.shape), whole(g_v.shape), whole(b_v.shape)],
        out_specs=pl.BlockSpec((SGU_BLOCK, D_MODEL), lambda i: (i, 0)),
        out_shape=jax.ShapeDtypeStruct((t, D_MODEL), BF16),
        compiler_params=_params(("parallel",)),
        name="sgu_fwd",
    )(h, h, wm, bs_t, g_v, b_v)


def _sgu_bwd(h, dy, dh, wm, wm_t, bs_t, g_v, b_v):
    t = h.shape[1]

    def body(u_ref, v_ref, dy_ref, dh_in, wm_ref, wmt_ref, bs_ref, g_ref, b_ref,
             duv_ref, dw_ref, dbs_ref, dg_ref, db_ref, dvn_ref):
        del dh_in
        du_ref, dv_ref = duv_ref.at[0], duv_ref.at[1]
        i = pl.program_id(0)

        @pl.when(i == 0)
        def _():
            dw_ref[...] = jnp.zeros_like(dw_ref)
            dbs_ref[...] = jnp.zeros_like(dbs_ref)
            dg_ref[...] = jnp.zeros_like(dg_ref)
            db_ref[...] = jnp.zeros_like(db_ref)

        u = u_ref[...]
        v = v_ref[...]
        xhat, rstd = _ln_stats(_gelu(v))
        vn = (xhat * g_ref[...] + b_ref[...]).astype(BF16)
        gu = _gelu(u)
        gup = _gelu_grad(u)
        lane = lax.broadcasted_iota(jnp.int32, (SGU_BLOCK, LANE), 1)
        dbs = jnp.zeros((SGU_BLOCK, LANE), F32)
        for g in range(GROUPS):
            sl = slice(g * HEAD, (g + 1) * HEAD)
            vn_g = vn[:, sl]
            mixed = _dot(wm_ref[g], vn_g) + bs_ref[:, g:g + 1]
            dy_g = dy_ref[:, sl]
            du_ref[:, sl] = (dy_g * mixed * gup[:, sl]).astype(BF16)
            dmix = dy_g * gu[:, sl]
            dmb = dmix.astype(BF16)
            dvn_ref[:, sl] = _dot(wmt_ref[g], dmb)
            dw_ref[g] += _dot_nt(dmb, vn_g)
            dbs = dbs + jnp.where(lane == g, jnp.sum(dmix, axis=1, keepdims=True), 0.0)
        dbs_ref[...] += dbs
        dvn = dvn_ref[...]
        dg_ref[...] += _sum0(dvn * xhat)
        db_ref[...] += _sum0(dvn)
        dv_ref[...] = (_ln_bwd(dvn, xhat, rstd, g_ref[...]) * _gelu_grad(v)).astype(BF16)

    blk = lambda b: pl.BlockSpec((None, SGU_BLOCK, D_MODEL), functools.partial(lambda i, b: (b, i, 0), b=b))
    row = pl.BlockSpec((SGU_BLOCK, D_MODEL), lambda i: (i, 0))
    whole = lambda s: pl.BlockSpec(s, lambda i: (0,) * len(s))
    vec = (1, D_MODEL)
    return pl.pallas_call(
        body,
        grid=(t // SGU_BLOCK,),
        in_specs=[blk(SLOT_U), blk(SLOT_V), row, pl.BlockSpec(memory_space=pl.ANY),
                  whole(wm.shape), whole(wm_t.shape), whole(bs_t.shape), whole(vec), whole(vec)],
        out_specs=[pl.BlockSpec((2, SGU_BLOCK, D_MODEL), lambda i: (SLOT_U // 2, i, 0)),
                   whole(wm.shape), whole((SGU_BLOCK, LANE)), whole(vec), whole(vec)],
        out_shape=[jax.ShapeDtypeStruct(dh.shape, BF16),
                   jax.ShapeDtypeStruct(wm.shape, F32), jax.ShapeDtypeStruct((SGU_BLOCK, LANE), F32),
                   jax.ShapeDtypeStruct(vec, F32), jax.ShapeDtypeStruct(vec, F32)],
        scratch_shapes=[pltpu.VMEM((SGU_BLOCK, D_MODEL), F32)],
        input_output_aliases={3: 0},
        compiler_params=_params(("arbitrary",)),
        name="sgu_bwd",
    )(h, h, dy, dh, wm, wm_t, bs_t, g_v, b_v)


def _split3(x):
    hi = x.astype(BF16)
    r1 = x - hi.astype(F32)
    mid = r1.astype(BF16)
    lo = (r1 - mid.astype(F32)).astype(BF16)
    return hi, mid, lo


def _tri_matmul(tri, x):
    hi, mid, lo = _split3(x)
    dot = lambda p: jnp.dot(tri, p, preferred_element_type=F32)
    return dot(hi) + dot(mid) + dot(lo)


def _lower_bound(logits):
    l0, l1 = logits[0:1, :], logits[1:2, :]
    mx = jnp.maximum(l0, l1)
    e0, e1 = jnp.exp(l0 - mx), jnp.exp(l1 - mx)
    return e0 / (e0 + e1)


def _hgrn_gates(q_raw, f_raw, lb):
    q = q_raw * jax.nn.sigmoid(q_raw)
    sig = jax.nn.sigmoid(f_raw)
    f = lb + (1.0 - lb) * sig
    row = lax.broadcasted_iota(jnp.int32, (CHUNK, CHUNK), 0)
    col = lax.broadcasted_iota(jnp.int32, (CHUNK, CHUNK), 1)
    c = _tri_matmul((row >= col).astype(BF16), jnp.log(f))
    return q, sig, f, 1.0 - f, c


def _offdiag_terms(qh, kh, ch, tb):
    rows = slice(tb * SUB, (tb + 1) * SUB)
    r = ch[tb * SUB - 1:tb * SUB, :]
    eqh = jnp.exp(ch[rows] - r)
    ekh = jnp.exp(jnp.minimum(r - ch, 0.0))
    return rows, eqh, qh[rows] * eqh, ekh, kh * ekh


def _diag_decay(cb, s, trow):
    return jnp.exp(jnp.where(trow >= s, cb - cb[s:s + 1, :], NEG))


def _split_refs(refs, n_in, n_out, n_scratch, ex):
    nx = ex.n if ex else 0
    ins, refs = refs[:n_in], refs[n_in:]
    xsrc, refs = refs[:nx], refs[nx:]
    outs, refs = refs[:n_out], refs[n_out:]
    xout, refs = refs[:nx], refs[nx:]
    return ins, outs, refs[:n_scratch], (xsrc, xout, refs[n_scratch:])


def _hgrn_fwd(h, logits, g_norm, ex=None):
    t = h.shape[1]
    nc = t // CHUNK

    def body(*refs):
        ins, outs, scratch, xrefs = _split_refs(refs, 6, 3, 4, ex)
        q_ref, f_ref, i_ref, og_ref, lg_ref, gn_ref = ins
        y_ref, o_ref, sall_ref = outs
        st_ref, q_s, k_s, c_s = scratch

        @pl.when(pl.program_id(0) == 0)
        def _():
            st_ref[...] = jnp.zeros_like(st_ref)
            if ex:
                ex.start(*xrefs)

        if ex:
            @pl.when(pl.program_id(0) == (3 * nc) // 4)
            def _():
                ex.relay(*xrefs)

        lb = _lower_bound(lg_ref[...])
        q, _, _, k, c = _hgrn_gates(q_ref[...], f_ref[...], lb)
        q_s[...] = q
        k_s[...] = k
        c_s[...] = c
        col64 = lax.broadcasted_iota(jnp.int32, (SUB, CHUNK), 1)
        trow = lax.broadcasted_iota(jnp.int32, (SUB, HEAD), 0)

        def head(hd, carry):
            sl = pl.ds(pl.multiple_of(hd * HEAD, HEAD), HEAD)
            qh, kh, ch, ih = q_s[:, sl], k_s[:, sl], c_s[:, sl], i_ref[:, sl]
            st = st_ref[hd]
            sall_ref[hd] = st
            c_last = ch[CHUNK - 1:CHUNK, :]
            o = _dot_nt(qh * jnp.exp(ch), st)
            st_ref[hd] = st * jnp.exp(c_last) + _dot_tn(ih, kh * jnp.exp(c_last - ch))
            a_rows = [jnp.zeros((SUB, CHUNK), F32)]
            for tb in range(1, CHUNK // SUB):
                _, _, q_hat, _, k_hat = _offdiag_terms(qh, kh, ch, tb)
                a_rows.append(jnp.where(col64 < tb * SUB, _dot_nt(q_hat, k_hat), 0.0))
            o = o + _dot(jnp.concatenate(a_rows, axis=0), ih)
            o_rows = []
            for b in range(CHUNK // SUB):
                rows = slice(b * SUB, (b + 1) * SUB)
                qb, cb, kb, ib = qh[rows], ch[rows], kh[rows], ih[rows]
                ob = jnp.zeros((SUB, HEAD), F32)
                for s in range(SUB):
                    a = jnp.sum(qb * _diag_decay(cb, s, trow) * kb[s:s + 1, :], axis=1, keepdims=True)
                    ob = ob + a * ib[s:s + 1, :]
                o_rows.append(ob)
            o = o + jnp.concatenate(o_rows, axis=0)
            o_ref[:, sl] = o
            og = og_ref[:, sl]
            on = o * lax.rsqrt(_mean(o * o) + RMS_EPS)
            y_ref[:, sl] = (on * gn_ref[:, sl] * (og * jax.nn.sigmoid(og))).astype(BF16)
            return carry

        lax.fori_loop(0, HEADS, head, 0, unroll=HEAD_UNROLL)

        if ex:
            @pl.when(pl.program_id(0) == nc - 1)
            def _():
                ex.finish(*xrefs)

    blk = lambda b: pl.BlockSpec((None, CHUNK, D_MODEL), functools.partial(lambda n, b: (b, n, 0), b=b))
    row = pl.BlockSpec((CHUNK, D_MODEL), lambda n: (n, 0))
    whole = lambda s: pl.BlockSpec(s, lambda n: (0,) * len(s))
    res = pl.pallas_call(
        body,
        grid=(nc,),
        in_specs=[blk(SLOT_Q), blk(SLOT_F), blk(SLOT_I), blk(SLOT_OG), whole(logits.shape), whole(g_norm.shape)]
        + (ex.in_specs if ex else []),
        out_specs=[row, row, pl.BlockSpec((None, HEADS, HEAD, HEAD), lambda n: (n, 0, 0, 0))] + (ex.out_specs if ex else []),
        out_shape=[jax.ShapeDtypeStruct((t, D_MODEL), BF16), jax.ShapeDtypeStruct((t, D_MODEL), F32),
                   jax.ShapeDtypeStruct((nc, HEADS, HEAD, HEAD), F32)] + (ex.out_shape if ex else []),
        scratch_shapes=[pltpu.VMEM((HEADS, HEAD, HEAD), F32)] + [pltpu.VMEM((CHUNK, D_MODEL), F32)] * 3
        + (ex.scratch if ex else []),
        compiler_params=_params(("arbitrary",)),
        name="hgrn_fwd",
    )(h, h, h, h, logits, g_norm, *(ex.arrays if ex else []))
    return res[0], res[1], res[2], res[3:]


def _hgrn_bwd(h, o_all, dy, states, dh, logits, g_norm, ex=None):
    t = h.shape[1]
    nc = t // CHUNK

    def body(*refs):
        ins, outs, scratch, xrefs = _split_refs(refs, 10, 3, 8, ex)
        q_ref, f_ref, i_ref, og_ref, o_ref, dy_ref, sall_ref, _, lg_ref, gn_ref = ins
        dqfio_ref, dlg_ref, dgn_ref = outs
        dst_ref, dlb_ref, q_s, k_s, c_s, dq_s, dk_s, dc_s = scratch
        dq_ref, df_ref, di_ref, dog_ref = (dqfio_ref.at[s] for s in (SLOT_Q, SLOT_F, SLOT_I, SLOT_OG))
        n = pl.program_id(0)

        @pl.when(n == 0)
        def _():
            dst_ref[...] = jnp.zeros_like(dst_ref)
            dlb_ref[...] = jnp.zeros_like(dlb_ref)
            dgn_ref[...] = jnp.zeros_like(dgn_ref)
            if ex:
                ex.start(*xrefs)

        lb = _lower_bound(lg_ref[...])
        q_raw = q_ref[...]
        q, sig, f, k, c = _hgrn_gates(q_raw, f_ref[...], lb)
        q_s[...] = q
        k_s[...] = k
        c_s[...] = c
        col64 = lax.broadcasted_iota(jnp.int32, (SUB, CHUNK), 1)
        trow = lax.broadcasted_iota(jnp.int32, (SUB, HEAD), 0)
        row64 = lax.broadcasted_iota(jnp.int32, (CHUNK, HEAD), 0)

        def head(hd, carry):
            sl = pl.ds(pl.multiple_of(hd * HEAD, HEAD), HEAD)
            qh, kh, ch, ih = q_s[:, sl], k_s[:, sl], c_s[:, sl], i_ref[:, sl]
            st = sall_ref[hd]
            dst = dst_ref[hd]
            oh, dyh, og, gn = o_ref[:, sl], dy_ref[:, sl], og_ref[:, sl], gn_ref[:, sl]
            sg = jax.nn.sigmoid(og)
            sil = og * sg
            rms = lax.rsqrt(_mean(oh * oh) + RMS_EPS)
            on = oh * rms
            dog_ref[:, sl] = (dyh * on * gn * _silu_grad(og, sg)).astype(BF16)
            dgn_ref[:, sl] += _sum0(dyh * on * sil)
            don = dyh * gn * sil
            do = rms * (don - on * _mean(don * on))
            dob = do.astype(BF16)

            c_last = ch[CHUNK - 1:CHUNK, :]
            eq = jnp.exp(ch)
            q_til = qh * eq
            ekl = jnp.exp(c_last - ch)
            k_til = kh * ekl
            ecl = jnp.exp(c_last)
            dq_til = _dot(dob, st)
            dk_til = _dot(ih, dst)
            di = _dot_nt(k_til, dst)
            dc_last = _sum0(dk_til * k_til) + _sum0(dst * st) * ecl
            dst_ref[hd] = _dot_tn(dob, q_til) + dst * ecl
            dq = dq_til * eq
            dc = dq_til * q_til - dk_til * k_til
            dk = dk_til * ekl

            da_full = _dot_nt(dob, ih)
            a_rows = [jnp.zeros((SUB, CHUNK), F32)]
            dq_rows = [jnp.zeros((SUB, HEAD), F32)]
            dc_rows = [jnp.zeros((SUB, HEAD), F32)]
            for tb in range(1, CHUNK // SUB):
                rows, eqh, q_hat, ekh, k_hat = _offdiag_terms(qh, kh, ch, tb)
                keep = col64 < tb * SUB
                a_rows.append(jnp.where(keep, _dot_nt(q_hat, k_hat), 0.0))
                da = jnp.where(keep, da_full[rows], 0.0)
                dq_hat = _dot(da, k_hat)
                dk_hat = _dot_tn(da, q_hat)
                dq_rows.append(dq_hat * eqh)
                dc_rows.append(dq_hat * q_hat)
                dk = dk + dk_hat * ekh
                dc = dc - dk_hat * k_hat
            di = di + _dot_tn(jnp.concatenate(a_rows, axis=0), dob)

            dk_rows, di_rows = [], []
            for b in range(CHUNK // SUB):
                rows = slice(b * SUB, (b + 1) * SUB)
                qb, cb, kb, ib, dob_ = qh[rows], ch[rows], kh[rows], ih[rows], do[rows]
                dq_diag = jnp.zeros((SUB, HEAD), F32)
                dk_diag = jnp.zeros((SUB, HEAD), F32)
                di_diag = jnp.zeros((SUB, HEAD), F32)
                for s in range(SUB):
                    ks = kb[s:s + 1, :]
                    dec = _diag_decay(cb, s, trow)
                    a = jnp.sum(qb * dec * ks, axis=1, keepdims=True)
                    gk = jnp.sum(dob_ * ib[s:s + 1, :], axis=1, keepdims=True) * dec
                    dq_diag = dq_diag + gk * ks
                    dk_diag = dk_diag + jnp.where(trow == s, _sum0(gk * qb), 0.0)
                    di_diag = di_diag + jnp.where(trow == s, _sum0(a * dob_), 0.0)
                dq_rows[b] = dq_rows[b] + dq_diag
                dc_rows[b] = dc_rows[b] + qb * dq_diag - kb * dk_diag
                dk_rows.append(dk_diag)
                di_rows.append(di_diag)
            dq = dq + jnp.concatenate(dq_rows, axis=0)
            dk = dk + jnp.concatenate(dk_rows, axis=0)
            dc = dc + jnp.concatenate(dc_rows, axis=0) + jnp.where(row64 == CHUNK - 1, dc_last, 0.0)
            di_ref[:, sl] = (di + jnp.concatenate(di_rows, axis=0)).astype(BF16)
            dq_s[:, sl] = dq
            dk_s[:, sl] = dk
            dc_s[:, sl] = dc
            return carry

        lax.fori_loop(0, HEADS, head, 0, unroll=HEAD_UNROLL)

        row = lax.broadcasted_iota(jnp.int32, (CHUNK, CHUNK), 0)
        col = lax.broadcasted_iota(jnp.int32, (CHUNK, CHUNK), 1)
        dlf = _tri_matmul((row <= col).astype(BF16), dc_s[...])
        df = dlf / f - dk_s[...]
        dlb_ref[...] += _sum0(df * (1.0 - sig))
        df_ref[...] = (df * (1.0 - lb) * sig * (1.0 - sig)).astype(BF16)
        dq_ref[...] = (dq_s[...] * _silu_grad(q_raw, jax.nn.sigmoid(q_raw))).astype(BF16)

        @pl.when(n == nc - 1)
        def _():
            d0 = dlb_ref[...] * lb * (1.0 - lb)
            dlg_ref[0:1, :] = d0
            dlg_ref[1:2, :] = -d0
            if ex:
                ex.finish(*xrefs)

    rev = lambda n: nc - 1 - n
    blk = lambda b: pl.BlockSpec((None, CHUNK, D_MODEL), functools.partial(lambda n, b: (b, rev(n), 0), b=b))
    row = pl.BlockSpec((CHUNK, D_MODEL), lambda n: (rev(n), 0))
    whole = lambda s: pl.BlockSpec(s, lambda n: (0,) * len(s))
    vec = (1, D_MODEL)
    res = pl.pallas_call(
        body,
        grid=(nc,),
        in_specs=[blk(SLOT_Q), blk(SLOT_F), blk(SLOT_I), blk(SLOT_OG), row, row,
                  pl.BlockSpec((None, HEADS, HEAD, HEAD), lambda n: (rev(n), 0, 0, 0)),
                  pl.BlockSpec(memory_space=pl.ANY), whole(logits.shape), whole(vec)] + (ex.in_specs if ex else []),
        out_specs=[pl.BlockSpec((4, CHUNK, D_MODEL), lambda n: (SLOT_Q // 4, rev(n), 0)), whole((2, D_MODEL)), whole(vec)]
        + (ex.out_specs if ex else []),
        out_shape=[jax.ShapeDtypeStruct(dh.shape, BF16), jax.ShapeDtypeStruct((2, D_MODEL), F32),
                   jax.ShapeDtypeStruct(vec, F32)] + (ex.out_shape if ex else []),
        scratch_shapes=[pltpu.VMEM((HEADS, HEAD, HEAD), F32), pltpu.VMEM(vec, F32)]
        + [pltpu.VMEM((CHUNK, D_MODEL), F32)] * 6 + (ex.scratch if ex else []),
        input_output_aliases={7: 0},
        compiler_params=_params(("arbitrary",)),
        name="hgrn_bwd",
    )(h, h, h, h, o_all, dy, states, dh, logits, g_norm, *(ex.arrays if ex else []))
    return res[0], res[1], res[2], res[3:]


def _merge_fwd(i, n, ga, gb, za, zb):
    return jax.nn.sigmoid(ga) * za + jax.nn.sigmoid(gb) * zb


def _merge_bwd(i, n, ga, gb, za, zb, dm):
    sa, sb = jax.nn.sigmoid(ga), jax.nn.sigmoid(gb)
    dgates = jnp.stack([(dm * za * sa * (1.0 - sa)).astype(BF16), (dm * zb * sb * (1.0 - sb)).astype(BF16)])
    return dgates, dm * sa, dm * sb


def _ln1_fwd(i, n, x, r1, g, b):
    xhat, _ = _ln_stats(ALPHA * x + r1)
    x1 = xhat * g + b
    return x1, x1


def _ln1_bwd(i, n, x, r1, dx1, g):
    xhat, rstd = _ln_stats(ALPHA * x + r1)
    dz = _ln_bwd(dx1, xhat, rstd, g)
    return dz, dz, _sum0(dx1 * xhat), _sum0(dx1)


def _ln2_loss(i, n, x1, fo, pg, pp, tgt, g, b):
    sg = jax.nn.sigmoid(pg)
    xhat, rstd = _ln_stats(ALPHA * x1 + fo + sg * pp)
    diff = xhat * g + b - tgt
    loss = 0.5 * jnp.sum(_mean(diff * diff), axis=0, keepdims=True)
    dy = diff * (1.0 / D_MODEL)
    dz = _ln_bwd(dy, xhat, rstd, g)
    return (dz, dz, dz * pp * sg * (1.0 - sg), dz * sg,
            jnp.broadcast_to(loss, (8, LANE)), _sum0(dy * xhat), _sum0(dy))


def _shift_down(cur, halo, tile):
    row = lax.broadcasted_iota(jnp.int32, cur.shape, 0)
    m1 = jnp.where(row == 0, halo[7:8, :], pltpu.roll(cur, 1, 0))
    m2 = jnp.where(row == 0, halo[6:7, :], jnp.where(row == 1, halo[7:8, :], pltpu.roll(cur, 2, 0)))
    return m1, m2


def _shift_up(cur, halo, tile):
    row = lax.broadcasted_iota(jnp.int32, cur.shape, 0)
    p1 = jnp.where(row == tile - 1, halo[0:1, :], pltpu.roll(cur, tile - 1, 0))
    p2 = jnp.where(row == tile - 2, halo[0:1, :], jnp.where(row == tile - 1, halo[1:2, :], pltpu.roll(cur, tile - 2, 0)))
    return p1, p2


def _conv_pre(i, gate, halo, w, b, tile):
    halo = jnp.where(i == 0, 0.0, halo)
    m1, m2 = _shift_down(gate, halo, tile)
    return w[0:1, :] * m2 + w[1:2, :] * m1 + w[2:3, :] * gate + b, m1, m2


def _conv_fwd(tile, i, n, gate, halo, val, w, b):
    cg, _, _ = _conv_pre(i, gate, halo, w, b, tile)
    return _gelu(cg) * val


def _conv_bwd_a(tile, i, n, gate, halo, val, dhid, w, b):
    cg, m1, m2 = _conv_pre(i, gate, halo, w, b, tile)
    dcg = dhid * val * _gelu_grad(cg)
    return dcg, dhid * _gelu(cg), _sum0(dcg * m2), _sum0(dcg * m1), _sum0(dcg * gate), _sum0(dcg)


def _conv_bwd_b(tile, i, n, dcg, halo, w):
    halo = jnp.where(i == n - 1, 0.0, halo)
    p1, p2 = _shift_up(dcg, halo, tile)
    return w[2:3, :] * dcg + w[1:2, :] * p1 + w[0:1, :] * p2


def _halo_spec(width, tile, t, nxt):
    per = tile // 8
    last = t // 8 - 1
    if nxt:
        return pl.BlockSpec((8, width), lambda i: (jnp.minimum((i + 1) * per, last), 0))
    return pl.BlockSpec((8, width), lambda i: (jnp.maximum(i * per - 1, 0), 0))


def _adamw(i, n, w, m, v, parts):
    g = parts[0].astype(F32)
    for j in range(1, parts.shape[0]):
        g = g + parts[j].astype(F32)
    m_new = ADAM_B1 * m + (1.0 - ADAM_B1) * g
    v_new = ADAM_B2 * v + (1.0 - ADAM_B2) * (g * g)
    m_hat = m_new / (1.0 - ADAM_B1 ** ADAM_STEP)
    v_hat = v_new / (1.0 - ADAM_B2 ** ADAM_STEP)
    delta = -ADAM_LR * (m_hat / (jnp.sqrt(v_hat) + ADAM_EPS) + ADAM_WD * w)
    return g, delta, m_new, v_new


def _adam_call(w, m, v, parts, name):
    r, c = w.shape
    tile = _pick(r, (256, 128)) if r > 256 else r
    spec = pl.BlockSpec((parts.shape[0], tile, c), lambda i: (0, i, 0))
    return _rowwise(_adamw, [w, m, v, (parts, spec)], [], [(c, F32)] * 4, [], tile=tile, name=name)


def _peer(k):
    x, y, c = lax.axis_index("x"), lax.axis_index("y"), lax.axis_index("c")
    px = x ^ ((k >> 2) & 1)
    py = y ^ ((k >> 1) & 1)
    pc = c ^ (k & 1)
    return (px, py, pc), 4 * px + 2 * py + pc


def _my_index():
    return 4 * lax.axis_index("x") + 2 * lax.axis_index("y") + lax.axis_index("c")


class _Exchange:
    KINDS = ("gather", "gather+relay", "gather+relay+slots", "scatter", "scatter+pairs")

    def __init__(self, entries):
        assert all(k in self.KINDS for _, k in entries), [k for _, k in entries]
        self.arrays = [a for a, _ in entries]
        self.scatter = [k.startswith("scatter") for _, k in entries]
        self.relayed = ["+relay" in k for _, k in entries]
        self.slots = ["+slots" in k for _, k in entries]
        self.pairs = ["+pairs" in k for _, k in entries]
        self.n = len(entries)
        self.in_specs = [pl.BlockSpec(memory_space=pl.ANY)] * self.n
        self.out_specs = [pl.BlockSpec(memory_space=pl.ANY)] * self.n
        shapes = [tuple(a.shape[1:]) if sc else tuple(a.shape) for a, sc in zip(self.arrays, self.scatter)]
        counts = [N_CHIP if p else N_DEV for p in self.pairs]
        self.out_shape = [jax.ShapeDtypeStruct((n,) + s, a.dtype) for n, s, a in zip(counts, shapes, self.arrays)]
        per = N_DEV - 1
        self.scratch = [pltpu.SemaphoreType.DMA((self.n * per,)), pltpu.SemaphoreType.DMA((self.n * per,)),
                        pltpu.SemaphoreType.DMA((self.n,))]

    def _copies(self, srcs, outs, sems):
        send_sems, recv_sems, local_sems = sems
        x, y, c = lax.axis_index("x"), lax.axis_index("y"), lax.axis_index("c")
        me = _my_index()
        per = N_DEV - 1
        local, first, passed, relay_arrivals, arrivals = [], [], [], [], []
        for a in range(self.n):
            place = (lambda idx: _slot_of_group(idx)) if self.slots[a] else (lambda idx: idx)

            def copy(k, src, dst, dev, a=a):
                return pltpu.make_async_remote_copy(
                    src_ref=src, dst_ref=dst, send_sem=send_sems.at[a * per + k], recv_sem=recv_sems.at[a * per + k],
                    device_id=dev, device_id_type=pl.DeviceIdType.MESH)

            if self.pairs[a]:
                chip = 2 * x + y
                for k in range(N_CHIP):
                    to = chip ^ k
                    piece = srcs[a].at[_slot_of_group(2 * to + c) // 2]
                    if k == 0:
                        local.append(pltpu.make_async_copy(piece, outs[a].at[chip], local_sems.at[a]))
                    else:
                        dev = (to // 2, to % 2, c)
                        first.append(copy(k - 1, piece, outs[a].at[chip], dev))
                        arrivals.append(copy(k - 1, piece, outs[a].at[to], dev))
                continue
            if self.scatter[a]:
                mine, land = srcs[a].at[place(me)], outs[a].at[me]
            else:
                mine, land = srcs[a], outs[a].at[place(me)]
            local.append(pltpu.make_async_copy(mine, land, local_sems.at[a]))
            if self.relayed[a]:
                block = lambda px, py, pc, a=a, place=place: outs[a].at[place(4 * px + 2 * py + pc)]
                chips = [(1 - x, y), (x, 1 - y), (1 - x, 1 - y)]
                first.append(copy(0, mine, land, (x, y, 1 - c)))
                arrivals.append(copy(0, mine, block(x, y, 1 - c), (x, y, 1 - c)))
                for j, (px, py) in enumerate(chips):
                    first.append(copy(1 + j, mine, land, (px, py, c)))
                    relay_arrivals.append(copy(1 + j, mine, block(px, py, c), (px, py, c)))
                    passed.append(copy(4 + j, block(px, py, c), block(px, py, c), (x, y, 1 - c)))
                    arrivals.append(copy(4 + j, mine, block(px, py, 1 - c), (x, y, 1 - c)))
                continue
            for k in range(1, N_DEV):
                dev, idx = _peer(k)
                if self.scatter[a]:
                    first.append(copy(k - 1, srcs[a].at[place(idx)], land, dev))
                    arrivals.append(copy(k - 1, mine, outs[a].at[idx], dev))
                else:
                    first.append(copy(k - 1, mine, land, dev))
                    arrivals.append(copy(k - 1, mine, outs[a].at[place(idx)], dev))
        return local, first, passed, relay_arrivals, arrivals

    def start(self, srcs, outs, sems):
        local, first, _, _, _ = self._copies(srcs, outs, sems)
        for cp in local + first:
            cp.start()

    def relay(self, srcs, outs, sems):
        _, _, passed, relay_arrivals, _ = self._copies(srcs, outs, sems)
        for landed, onward in zip(relay_arrivals, passed):
            landed.wait_recv()
            onward.start()

    def finish(self, srcs, outs, sems):
        local, first, passed, _, arrivals = self._copies(srcs, outs, sems)
        for cp in arrivals:
            cp.wait_recv()
        for cp in first + passed:
            cp.wait_send()
        for cp in local:
            cp.wait()


def _pair_sums(g):
    n, r, cols = g.shape
    half = n // 2

    def swap(g_ref, got_ref, send_sems, recv_sems):
        x, y, c = lax.axis_index("x"), lax.axis_index("y"), lax.axis_index("c")
        copies = [pltpu.make_async_remote_copy(
            src_ref=g_ref.at[2 * j + 1 - c], dst_ref=got_ref.at[j], send_sem=send_sems.at[j], recv_sem=recv_sems.at[j],
            device_id=(x, y, 1 - c), device_id_type=pl.DeviceIdType.MESH) for j in range(half)]
        for cp in copies:
            cp.start()
        for cp in copies:
            cp.wait()

    got = pl.pallas_call(
        swap,
        in_specs=[pl.BlockSpec(memory_space=pl.ANY)],
        out_specs=pl.BlockSpec(memory_space=pl.ANY),
        out_shape=jax.ShapeDtypeStruct((half, r, cols), g.dtype),
        scratch_shapes=[pltpu.SemaphoreType.DMA((half,))] * 2,
        name="pair_swap",
    )(g)

    def add(mine_ref, got_ref, out_ref):
        mine = jnp.where(lax.axis_index("c") == 0, mine_ref[0].astype(F32), mine_ref[1].astype(F32))
        out_ref[...] = (mine + got_ref[...].astype(F32)).astype(out_ref.dtype)

    tile = _pick(r, (512, 256, 128))
    return pl.pallas_call(
        add,
        grid=(half, r // tile),
        in_specs=[pl.BlockSpec((None, 2, tile, cols), lambda j, i: (j, 0, i, 0)),
                  pl.BlockSpec((None, tile, cols), lambda j, i: (j, i, 0))],
        out_specs=pl.BlockSpec((None, tile, cols), lambda j, i: (j, i, 0)),
        out_shape=jax.ShapeDtypeStruct((half, r, cols), g.dtype),
        compiler_params=_params(("parallel", "parallel")),
        name="pair_add",
    )(g.reshape(half, 2, r, cols), got)


def _exchange(entries, name):
    ex = _Exchange(entries)
    n = ex.n

    def body(*refs):
        xrefs = refs[:n], refs[n:2 * n], refs[2 * n:]
        ex.start(*xrefs)
        ex.relay(*xrefs)
        ex.finish(*xrefs)

    return pl.pallas_call(body, in_specs=ex.in_specs, out_specs=ex.out_specs, out_shape=ex.out_shape,
                          scratch_shapes=ex.scratch, name=name)(*ex.arrays)


def _local_step(x, p, tgt, w_in, small, comm):
    t = x.shape[0]
    tile = _pick(t, (256, 128))
    d = D_MODEL
    act_b, act_f = (d, BF16), (d, F32)
    x_b, p_b = x.astype(BF16), p.astype(BF16)

    chunk_id = jnp.arange(SGU_BLOCK) // CHUNK
    mask = chunk_id[:, None] >= chunk_id[None, :]
    wm = jnp.where(mask[None], small["sgu_w_s"], 0.0)
    wm_b = wm.astype(BF16)
    wm_t = jnp.swapaxes(wm, 1, 2).astype(BF16)
    bs_t = small["sgu_b_s"].T

    h = _mm(x_b, w_in, out_dtype=F32, name="mm_h")
    y_a = _sgu_fwd(h, wm_b, bs_t, small["sgu_norm_g"], small["sgu_norm_b"])
    y_b, o_all, states, got = _hgrn_fwd(h, small["lb_logits"], small["hgrn_norm_g"], ex=comm.weights_exchange())
    wts, conv_w = comm.weights(got)
    z_a = _mm(y_a, wts["w_a"], out_dtype=F32, name="mm_za")
    z_b = _mm(y_b, wts["w_b"], out_dtype=F32, name="mm_zb")
    gates = [(h, SLOT_GA), (h, SLOT_GB)]
    merged, = _rowwise(_merge_fwd, gates + [z_a, z_b], [], [act_b], [], tile=tile, name="merge_fwd")
    r1 = _mm(merged, wts["w_o"], out_dtype=F32, name="mm_r1")
    x1, x1_b = _rowwise(_ln1_fwd, [x, r1], [small["ln1_g"], small["ln1_b"]], [act_f, act_b], [], tile=tile, name="ln1_fwd")
    gate = _mm(x1_b, wts["w_g"], out_dtype=F32, name="mm_gate")
    val = _mm(x1_b, wts["w_v"], out_dtype=F32, name="mm_val")
    pg = _mm(x1_b, wts["w_pg"], out_dtype=F32, name="mm_pg")
    pp = _mm(p_b, wts["w_pp"], out_dtype=F32, name="mm_pp")
    hid, = _rowwise(functools.partial(_conv_fwd, tile), [gate, (gate, _halo_spec(D_FF, tile, t, False)), val],
                    [conv_w, small["conv_b"]], [(D_FF, BF16)], [], tile=tile, name="conv_fwd")
    fo = _mm(hid, wts["w_down"], out_dtype=F32, name="mm_down")
    dz2, dz2_b, dpg, dpp, loss, dg2, db2 = _rowwise(
        _ln2_loss, [x1, fo, pg, pp, tgt], [small["ln2_g"], small["ln2_b"]],
        [act_f, act_b, act_b, act_b], [(8, LANE), (1, d), (1, d)], tile=tile, name="ln2_loss")

    dhid = _mm(dz2_b, wts["w_down"], out_dtype=F32, name="mm_dhid", trans_b=True)
    g_down = _mm_tn(hid, dz2_b, out_dtype=BF16, name="mm_g_down")
    dcg, dval, dcw0, dcw1, dcw2, dcb = _rowwise(
        functools.partial(_conv_bwd_a, tile), [gate, (gate, _halo_spec(D_FF, tile, t, False)), val, dhid],
        [conv_w, small["conv_b"]], [(D_FF, F32), (D_FF, BF16)], [(1, D_FF)] * 4, tile=tile, name="conv_bwd_a")
    dgate, = _rowwise(functools.partial(_conv_bwd_b, tile), [dcg, (dcg, _halo_spec(D_FF, tile, t, True))],
                      [conv_w], [(D_FF, BF16)], [], tile=tile, name="conv_bwd_b")
    g_g = _mm_tn(x1_b, dgate, out_dtype=BF16, name="mm_g_gate")
    g_v = _mm_tn(x1_b, dval, out_dtype=BF16, name="mm_g_val")
    g_pg = _mm_tn(x1_b, dpg, out_dtype=BF16, name="mm_g_pg")
    g_pp = _mm_tn(p_b, dpp, out_dtype=BF16, name="mm_g_pp")
    dx1 = _mm(dgate, wts["w_g"], out_dtype=F32, name="mm_dx1_gate", trans_b=True, adds=[(dz2, ALPHA)])
    dx1 = _mm(dval, wts["w_v"], out_dtype=F32, name="mm_dx1_val", trans_b=True, adds=[(dx1, 1.0)])
    dx1 = _mm(dpg, wts["w_pg"], out_dtype=F32, name="mm_dx1_pg", trans_b=True, adds=[(dx1, 1.0)])
    dz1, dz1_b, dg1, db1 = _rowwise(_ln1_bwd, [x, r1, dx1], [small["ln1_g"]], [act_f, act_b], [(1, d), (1, d)],
                                    tile=tile, name="ln1_bwd")
    g_o = _mm_tn(merged, dz1_b, out_dtype=BF16, name="mm_g_o")
    dm = _mm(dz1_b, wts["w_o"], out_dtype=F32, name="mm_dm", trans_b=True)
    dh, dza, dzb = _rowwise(_merge_bwd, gates + [z_a, z_b, dm], [],
                            [("stack", 2, SLOT_GA // 2, 8, d, BF16), act_b, act_b], [], tile=tile, name="merge_bwd")
    g_a = _mm_tn(y_a, dza, out_dtype=BF16, name="mm_g_a")
    g_b = _mm_tn(y_b, dzb, out_dtype=BF16, name="mm_g_b")
    dy_a = _mm(dza, wts["w_a"], out_dtype=F32, name="mm_dya", trans_b=True)
    dy_b = _mm(dzb, wts["w_b"], out_dtype=F32, name="mm_dyb", trans_b=True)
    dh, dws, dbs, dgv_n, dbv_n = _sgu_bwd(h, dy_a, dh, wm_b, wm_t, bs_t, small["sgu_norm_g"], small["sgu_norm_b"])
    big = dict(w_a=g_a, w_b=g_b, w_o=g_o, w_g=g_g, w_v=g_v, w_down=g_down, w_pp=g_pp, w_pg=g_pg)
    sm = dict(sgu_w_s=jnp.where(mask[None], dws, 0.0), sgu_b_s=dbs[:, :GROUPS].T, sgu_norm_g=dgv_n, sgu_norm_b=dbv_n,
              ln1_g=dg1, ln1_b=db1, conv_w=jnp.concatenate([dcw0, dcw1, dcw2], axis=0), conv_b=dcb, ln2_g=dg2, ln2_b=db2,
              loss=loss)
    dh, dlogits, dgn, got = _hgrn_bwd(h, o_all, dy_b, states, dh, small["lb_logits"], small["hgrn_norm_g"],
                                      ex=comm.grads_exchange(big, sm))
    comm.grads_done(got)
    g_in = _mm_tn(x_b, dh, out_dtype=BF16, name="mm_g_in")
    ex = comm.last_exchange(g_in, dict(lb_logits=dlogits, hgrn_norm_g=dgn))
    res = _mm(dh, w_in, out_dtype=F32, name="mm_dx", trans_b=True, reduce_b=True, adds=[(dz1, ALPHA)], ex=ex)
    grad_x, got = res if ex else (res, ())
    comm.last_done(got)
    return grad_x


_SMALL_EARLY = ["sgu_w_s", "sgu_b_s", "sgu_norm_g", "sgu_norm_b", "ln1_g", "ln1_b", "ffn_conv_b", "ln2_g", "ln2_b"]
_SMALL_LATE = ["hgrn_lb_logits", "hgrn_norm_g"]
N_TAPS = D_FF // N_DEV
UP_COLS = 2 * D_FF // N_DEV


class _StepExchanges:
    def __init__(self, shards):
        self.shards = shards

    def weights_exchange(self):
        return _Exchange([(s, "gather+relay") for s in self.shards])

    def weights(self, got):
        d, f = D_MODEL, D_FF
        w_br_g, w_o_g, w_up_g, w_down_g, w_pp_g, w_pg_g, conv_g = got
        w_br = w_br_g.transpose(1, 0, 2, 3).reshape(2, d, d)
        w_up = w_up_g.transpose(1, 0, 2).reshape(d, 2, f).transpose(1, 0, 2)
        wts = dict(w_a=w_br[0], w_b=w_br[1], w_o=w_o_g.reshape(d, d), w_g=w_up[0], w_v=w_up[1],
                   w_down=w_down_g.reshape(f, d), w_pp=w_pp_g.transpose(1, 0, 2).reshape(256, d), w_pg=w_pg_g.reshape(d, d))
        return wts, conv_g.transpose(1, 0, 2).reshape(3, f)

    def grads_exchange(self, big, sm):
        d = D_MODEL
        parts = [jnp.stack([big["w_a"], big["w_b"]]).reshape(2, N_DEV, 128, d).transpose(1, 0, 2, 3),
                 big["w_o"].reshape(N_DEV, 128, d),
                 jnp.concatenate([big["w_g"], big["w_v"]], axis=1).reshape(d, N_DEV, UP_COLS).transpose(1, 0, 2),
                 big["w_down"].reshape(N_DEV, N_TAPS, d),
                 big["w_pp"].reshape(256, N_DEV, 128).transpose(1, 0, 2),
                 big["w_pg"].reshape(N_DEV, 128, d)]
        packed, self.rows_early = _pack([sm[k] for k in ("sgu_w_s", "sgu_b_s", "sgu_norm_g", "sgu_norm_b", "ln1_g", "ln1_b",
                                                         "conv_b", "ln2_g", "ln2_b", "conv_w", "loss")])
        return _Exchange([(a, "scatter") for a in parts] + [(packed, "gather")])

    def grads_done(self, got):
        self.recv, self.small_early = got[:6], got[6]

    def last_exchange(self, g_in, sm):
        packed, self.rows_late = _pack([sm["lb_logits"], sm["hgrn_norm_g"]])
        return _Exchange([(_pair_sums(g_in), "scatter+pairs"), (packed, "gather")])

    def last_done(self, got):
        self.recv_in, self.small_late = got


def _rows128(a):
    flat = a.reshape(-1)
    rows = -(-flat.shape[0] // (8 * LANE)) * 8
    return jnp.pad(flat, (0, rows * LANE - flat.shape[0])).reshape(rows, LANE)


def _pack(parts):
    blocks = [_rows128(a) for a in parts]
    return jnp.concatenate(blocks, axis=0), [b.shape[0] for b in blocks]


def _unpack(packed, shapes, rows):
    out, r0 = [], 0
    for shp, r in zip(shapes, rows):
        n = math.prod(shp)
        out.append(packed[r0:r0 + r].reshape(-1)[:n].reshape(shp))
        r0 += r
    return out


def kernel(x, p, w_in, sgu_w_s, sgu_b_s, sgu_norm_g, sgu_norm_b, hgrn_lb_logits, hgrn_norm_g, w_branch, w_out, ln1_g, ln1_b, ffn_w_up, ffn_conv_w, ffn_conv_b, ffn_w_down, ln2_g, ln2_b, ple_w_proj, ple_w_gate, loss_target, m_w_in, m_sgu_w_s, m_sgu_b_s, m_sgu_norm_g, m_sgu_norm_b, m_hgrn_lb_logits, m_hgrn_norm_g, m_w_branch, m_w_out, m_ln1_g, m_ln1_b, m_ffn_w_up, m_ffn_conv_w, m_ffn_conv_b, m_ffn_w_down, m_ln2_g, m_ln2_b, m_ple_w_proj, m_ple_w_gate, v_w_in, v_sgu_w_s, v_sgu_b_s, v_sgu_norm_g, v_sgu_norm_b, v_hgrn_lb_logits, v_hgrn_norm_g, v_w_branch, v_w_out, v_ln1_g, v_ln1_b, v_ffn_w_up, v_ffn_conv_w, v_ffn_conv_b, v_ffn_w_down, v_ln2_g, v_ln2_b, v_ple_w_proj, v_ple_w_gate):
    weights = dict(w_in=w_in, sgu_w_s=sgu_w_s, sgu_b_s=sgu_b_s, sgu_norm_g=sgu_norm_g, sgu_norm_b=sgu_norm_b,
                   hgrn_lb_logits=hgrn_lb_logits, hgrn_norm_g=hgrn_norm_g, w_branch=w_branch, w_out=w_out,
                   ln1_g=ln1_g, ln1_b=ln1_b, ffn_w_up=ffn_w_up, ffn_conv_w=ffn_conv_w, ffn_conv_b=ffn_conv_b,
                   ffn_w_down=ffn_w_down, ln2_g=ln2_g, ln2_b=ln2_b, ple_w_proj=ple_w_proj, ple_w_gate=ple_w_gate)
    mom_m = dict(w_in=m_w_in, sgu_w_s=m_sgu_w_s, sgu_b_s=m_sgu_b_s, sgu_norm_g=m_sgu_norm_g, sgu_norm_b=m_sgu_norm_b,
                 hgrn_lb_logits=m_hgrn_lb_logits, hgrn_norm_g=m_hgrn_norm_g, w_branch=m_w_branch, w_out=m_w_out,
                 ln1_g=m_ln1_g, ln1_b=m_ln1_b, ffn_w_up=m_ffn_w_up, ffn_conv_w=m_ffn_conv_w, ffn_conv_b=m_ffn_conv_b,
                 ffn_w_down=m_ffn_w_down, ln2_g=m_ln2_g, ln2_b=m_ln2_b, ple_w_proj=m_ple_w_proj, ple_w_gate=m_ple_w_gate)
    mom_v = dict(w_in=v_w_in, sgu_w_s=v_sgu_w_s, sgu_b_s=v_sgu_b_s, sgu_norm_g=v_sgu_norm_g, sgu_norm_b=v_sgu_norm_b,
                 hgrn_lb_logits=v_hgrn_lb_logits, hgrn_norm_g=v_hgrn_norm_g, w_branch=v_w_branch, w_out=v_w_out,
                 ln1_g=v_ln1_g, ln1_b=v_ln1_b, ffn_w_up=v_ffn_w_up, ffn_conv_w=v_ffn_conv_w, ffn_conv_b=v_ffn_conv_b,
                 ffn_w_down=v_ffn_w_down, ln2_g=v_ln2_g, ln2_b=v_ln2_b, ple_w_proj=v_ple_w_proj, ple_w_gate=v_ple_w_gate)
    d, f = D_MODEL, D_FF
    me = _my_index()

    w_in_g, = _exchange([(w_in[0].astype(BF16), "gather+relay+slots")], "gather_w_in")
    comm = _StepExchanges([w_branch[0].astype(BF16), w_out[0].astype(BF16), ffn_w_up[0].astype(BF16),
                           ffn_w_down[0].astype(BF16), ple_w_proj[0].astype(BF16), ple_w_gate[0].astype(BF16), ffn_conv_w[0]])
    small = dict(sgu_w_s=sgu_w_s[0], sgu_b_s=sgu_b_s[0], sgu_norm_g=sgu_norm_g, sgu_norm_b=sgu_norm_b,
                 lb_logits=hgrn_lb_logits, hgrn_norm_g=hgrn_norm_g, ln1_g=ln1_g, ln1_b=ln1_b, ln2_g=ln2_g, ln2_b=ln2_b,
                 conv_b=ffn_conv_b)
    grad_x = _local_step(x[0], p[0, 0], loss_target[0], w_in_g, small, comm)

    out = {}

    def adam(name, parts8, shape2d):
        w2, m2, v2 = (a.reshape(shape2d) for a in (weights[name], mom_m[name], mom_v[name]))
        res = _adam_call(w2, m2, v2, parts8.reshape(parts8.shape[:1] + shape2d), "adam_" + name)
        out[name] = tuple(r.reshape(weights[name].shape) for r in res)

    adam("w_in", comm.recv_in, (d, d))
    adam("w_branch", comm.recv[0], (256, d))
    adam("w_out", comm.recv[1], (128, d))
    adam("ffn_w_up", comm.recv[2], (d, UP_COLS))
    adam("ffn_w_down", comm.recv[3], (N_TAPS, d))
    adam("ple_w_proj", comm.recv[4], (256, 128))
    adam("ple_w_gate", comm.recv[5], (128, d))

    def adam_small(names, extra_w, extra_m, extra_v, extra_shapes, parts8, rows, label):
        pk = lambda src, extra: _pack([src[n] for n in names] + extra)[0]
        res = _adam_call(pk(weights, extra_w), pk(mom_m, extra_m), pk(mom_v, extra_v), parts8, label)
        shapes = [weights[n].shape for n in names] + extra_shapes
        unpacked = [_unpack(r, shapes, rows) for r in res]
        for j, n in enumerate(names):
            out[n] = tuple(u[j] for u in unpacked)
        return [[u[len(names) + j] for u in unpacked] for j in range(len(extra_shapes))]

    blank = jnp.zeros((8, LANE), F32)
    taps, loss_rows = adam_small(
        _SMALL_EARLY, [_place_taps(ffn_conv_w[0], me, f), blank], [_place_taps(m_ffn_conv_w[0], me, f), blank],
        [_place_taps(v_ffn_conv_w[0], me, f), blank + 1.0], [(3, f), (8, LANE)], comm.small_early, comm.rows_early,
        "adam_small_early")
    adam_small(_SMALL_LATE, [], [], [], [], comm.small_late, comm.rows_late, "adam_small_late")
    out["ffn_conv_w"] = tuple(lax.dynamic_slice_in_dim(u, me * N_TAPS, N_TAPS, axis=1)[None] for u in taps)
    loss = loss_rows[0][0, 0]

    order = ["w_in", "sgu_w_s", "sgu_b_s", "sgu_norm_g", "sgu_norm_b", "hgrn_lb_logits", "hgrn_norm_g", "w_branch", "w_out",
             "ln1_g", "ln1_b", "ffn_w_up", "ffn_conv_w", "ffn_conv_b", "ffn_w_down", "ln2_g", "ln2_b", "ple_w_proj", "ple_w_gate"]
    return (loss, grad_x[None], *[out[n][0] for n in order], *[out[n][1] for n in order],
            *[out[n][2] for n in order], *[out[n][3] for n in order])


def _place_taps(shard, me, f):
    return lax.dynamic_update_slice_in_dim(jnp.zeros((3, f), F32), shard, me * N_TAPS, axis=1)
```

```python
import functools
import math

import jax
import jax.numpy as jnp
from jax import lax
from jax.experimental import pallas as pl
from jax.experimental.pallas import tpu as pltpu

F32 = jnp.float32
BF16 = jnp.bfloat16

N_DEV = 8
N_CHIP = 4
D_MODEL = 1024
CHUNK = 64
SUB = 16
SGU_BLOCK = 128
GROUPS = 8
HEAD = 128
HEADS = 8
HEAD_UNROLL = 4
D_FF = 2816
LN_EPS = 1e-5
RMS_EPS = 1e-6
ALPHA = 2.0 ** 0.25
GELU_K = math.sqrt(2.0 / math.pi)
GELU_C = 0.044715
NEG = -1e30
ADAM_LR, ADAM_B1, ADAM_B2, ADAM_EPS, ADAM_WD, ADAM_STEP = 0.001, 0.9, 0.999, 1e-08, 0.01, 10
LANE = 128
SLOT_Q, SLOT_F, SLOT_I, SLOT_OG, SLOT_U, SLOT_V, SLOT_GA, SLOT_GB = range(8)


def _slot_of_group(k):
    return jnp.where(k < 2, k + 4, jnp.where(k < 6, k - 2, k))


MIB = 1024 * 1024
VMEM_V7X = 64 * MIB
VMEM_FLOOR = 32 * MIB
MM_TILES = (1024, 1408, 512, 256, 128)


def _params(sem, need=0):
    limit = min(max(need + need // 4, VMEM_FLOOR), VMEM_V7X - 4 * MIB)
    return pltpu.CompilerParams(dimension_semantics=sem, vmem_limit_bytes=limit)


def _pick(n, prefs):
    for t in prefs:
        if n % t == 0:
            return t
    return n


def _gelu(x):
    return 0.5 * x * (1.0 + jnp.tanh(GELU_K * (x + GELU_C * x * x * x)))


def _gelu_grad(x):
    t = jnp.tanh(GELU_K * (x + GELU_C * x * x * x))
    return 0.5 * (1.0 + t) + 0.5 * x * (1.0 - t * t) * GELU_K * (1.0 + 3.0 * GELU_C * x * x)


def _silu_grad(x, s):
    return s * (1.0 + x * (1.0 - s))


def _dot(a, b):
    return jnp.dot(a.astype(BF16), b.astype(BF16), preferred_element_type=F32)


def _dot_nt(a, b):
    return lax.dot_general(a.astype(BF16), b.astype(BF16), (((1,), (1,)), ((), ())), preferred_element_type=F32)


def _dot_tn(a, b):
    return lax.dot_general(a.astype(BF16), b.astype(BF16), (((0,), (0,)), ((), ())), preferred_element_type=F32)


def _mean(x):
    return jnp.mean(x, axis=-1, keepdims=True)


def _sum0(x):
    return jnp.sum(x, axis=0, keepdims=True)


def _mm(a, b, *, out_dtype, name, trans_b=False, reduce_b=False, adds=(), ex=None):
    squeeze = b.ndim == 2
    a3 = a if a.ndim == 3 else a[None]
    b3 = b if b.ndim == 3 else b[None]
    ba, m, k = a3.shape
    bb = b3.shape[0]
    n = b3.shape[1] if trans_b else b3.shape[2]
    tm = _pick(m, MM_TILES)
    tn = _pick(n, MM_TILES)
    if reduce_b:
        bo, steps = 1, bb
        a_map = lambda o, i, j, r: (r if ba > 1 else 0, i, 0)
        b_map = (lambda o, i, j, r: (r, j, 0)) if trans_b else (lambda o, i, j, r: (r, 0, j))
    else:
        bo, steps = bb, 1
        a_map = lambda o, i, j, r: (o if ba > 1 else 0, i, 0)
        b_map = (lambda o, i, j, r: (o, j, 0)) if trans_b else (lambda o, i, j, r: (o, 0, j))
    o_map = lambda o, i, j, r: (o, i, j)
    add_arrays = [x if x.ndim == 3 else x[None] for x, _ in adds]
    add_scales = [s for _, s in adds]
    n_add = len(adds)
    dot = _dot_nt if trans_b else _dot

    def finish(acc, add_refs, o_ref):
        for ref, s in zip(add_refs, add_scales):
            acc = acc + s * ref[...].astype(F32)
        o_ref[...] = acc.astype(o_ref.dtype)

    grid = (bo, m // tm, n // tn, steps)

    def body(*refs):
        ins, (o_ref,), scratch, xrefs = _split_refs(refs, 2 + n_add, 1, 1 if reduce_b else 0, ex)
        a_ref, b_ref, add_refs = ins[0], ins[1], ins[2:]
        step = ((pl.program_id(0) * grid[1] + pl.program_id(1)) * grid[2] + pl.program_id(2)) * grid[3] + pl.program_id(3)
        if ex:
            @pl.when(step == 0)
            def _():
                ex.start(*xrefs)

        if reduce_b:
            acc, = scratch
            r = pl.program_id(3)

            @pl.when(r == 0)
            def _():
                acc[...] = jnp.zeros_like(acc)

            acc[...] += dot(a_ref[...], b_ref[...])

            @pl.when(r == steps - 1)
            def _():
                finish(acc[...], add_refs, o_ref)
        else:
            finish(dot(a_ref[...], b_ref[...]), add_refs, o_ref)

        if ex:
            @pl.when(step == math.prod(grid) - 1)
            def _():
                ex.finish(*xrefs)

    b_block = (None, tn, k) if trans_b else (None, k, tn)
    out_bytes = tm * tn * jnp.dtype(out_dtype).itemsize
    need = 2 * (tm * k * a3.dtype.itemsize + k * tn * b3.dtype.itemsize + out_bytes + n_add * tm * tn * 4)
    need += 2 * tm * tn * 4
    sem = ("arbitrary",) * 4 if ex else ("parallel", "parallel", "parallel", "arbitrary")
    res = pl.pallas_call(
        body,
        grid=grid,
        in_specs=[pl.BlockSpec((None, tm, k), a_map), pl.BlockSpec(b_block, b_map)]
        + [pl.BlockSpec((None, tm, tn), o_map) for _ in adds] + (ex.in_specs if ex else []),
        out_specs=[pl.BlockSpec((None, tm, tn), o_map)] + (ex.out_specs if ex else []),
        out_shape=[jax.ShapeDtypeStruct((bo, m, n), out_dtype)] + (ex.out_shape if ex else []),
        scratch_shapes=([pltpu.VMEM((tm, tn), F32)] if reduce_b else []) + (ex.scratch if ex else []),
        compiler_params=_params(sem, need),
        name=name,
    )(a3, b3, *add_arrays, *(ex.arrays if ex else []))
    out = res[0][0] if (reduce_b or squeeze) else res[0]
    return (out, res[1:]) if ex else out


def _mm_tn(a, b, *, out_dtype, name):
    squeeze = b.ndim == 2
    b3 = b if b.ndim == 3 else b[None]
    t, m = a.shape
    bb, _, n = b3.shape
    tm = _pick(m, MM_TILES)
    tn = _pick(n, MM_TILES)
    tt = _pick(t, (1024, 512, 256, 128))
    steps = t // tt
    need = 2 * (tt * tm * a.dtype.itemsize + tt * tn * b3.dtype.itemsize + tm * tn * jnp.dtype(out_dtype).itemsize)
    need += 2 * tm * tn * 4

    def body(a_ref, b_ref, o_ref, acc):
        r = pl.program_id(3)

        @pl.when(r == 0)
        def _():
            acc[...] = jnp.zeros_like(acc)

        acc[...] += _dot_tn(a_ref[...], b_ref[...])

        @pl.when(r == steps - 1)
        def _():
            o_ref[...] = acc[...].astype(o_ref.dtype)

    out = pl.pallas_call(
        body,
        grid=(bb, m // tm, n // tn, steps),
        in_specs=[pl.BlockSpec((tt, tm), lambda o, i, j, r: (r, i)),
                  pl.BlockSpec((None, tt, tn), lambda o, i, j, r: (o, r, j))],
        out_specs=pl.BlockSpec((None, tm, tn), lambda o, i, j, r: (o, i, j)),
        out_shape=jax.ShapeDtypeStruct((bb, m, n), out_dtype),
        scratch_shapes=[pltpu.VMEM((tm, tn), F32)],
        compiler_params=_params(("parallel", "parallel", "parallel", "arbitrary"), need),
        name=name,
    )(a, b3)
    return out[0] if squeeze else out


def _rowwise(fn, rows, consts, row_outs, acc_outs, *, tile, name):
    first = rows[0][0] if isinstance(rows[0], tuple) else rows[0]
    t = first.shape[-2]
    steps = t // tile
    arrays, in_specs = [], []
    for r in rows:
        if isinstance(r, tuple) and isinstance(r[1], pl.BlockSpec):
            arrays.append(r[0])
            in_specs.append(r[1])
        elif isinstance(r, tuple):
            arr, bidx = r
            arrays.append(arr)
            in_specs.append(pl.BlockSpec((None, tile, arr.shape[-1]), functools.partial(lambda i, b: (b, i, 0), b=bidx)))
        else:
            arrays.append(r)
            in_specs.append(pl.BlockSpec((tile, r.shape[-1]), lambda i: (i, 0)))
    for c in consts:
        arrays.append(c)
        in_specs.append(pl.BlockSpec(c.shape, lambda i: (0, 0)))
    n_in, n_row = len(arrays), len(row_outs)
    out_shape, out_specs = [], []
    for ro in row_outs:
        if ro[0] == "stack":
            _, cnt, blk, total, w, dt = ro
            out_shape.append(jax.ShapeDtypeStruct((total, t, w), dt))
            out_specs.append(pl.BlockSpec((cnt, tile, w), functools.partial(lambda i, b: (b, i, 0), b=blk)))
        else:
            w, dt = ro
            out_shape.append(jax.ShapeDtypeStruct((t, w), dt))
            out_specs.append(pl.BlockSpec((tile, w), lambda i: (i, 0)))
    out_shape += [jax.ShapeDtypeStruct(s, F32) for s in acc_outs]
    out_specs += [pl.BlockSpec(s, lambda i: (0, 0)) for s in acc_outs]
    blocks = [math.prod(d for d in sp.block_shape if d) * arr.dtype.itemsize for sp, arr in zip(in_specs, arrays)]
    blocks += [math.prod(d for d in sp.block_shape if d) * jnp.dtype(sh.dtype).itemsize
               for sp, sh in zip(out_specs, out_shape)]
    need = 2 * sum(blocks) + 6 * tile * max(a.shape[-1] for a in arrays) * 4

    def body(*refs):
        ins, outs = refs[:n_in], refs[n_in:]
        i = pl.program_id(0)
        res = fn(i, steps, *[r[...] for r in ins])
        res = res if isinstance(res, (tuple, list)) else (res,)
        for ref, val in zip(outs[:n_row], res[:n_row]):
            ref[...] = val.astype(ref.dtype)
        if acc_outs:
            @pl.when(i == 0)
            def _():
                for ref in outs[n_row:]:
                    ref[...] = jnp.zeros_like(ref)

            for ref, val in zip(outs[n_row:], res[n_row:]):
                ref[...] += val

    return pl.pallas_call(
        body,
        grid=(steps,),
        in_specs=in_specs,
        out_specs=out_specs,
        out_shape=out_shape,
        compiler_params=_params(("arbitrary",), need),
        name=name,
    )(*arrays)


def _ln_stats(z):
    mu = _mean(z)
    zc = z - mu
    rstd = lax.rsqrt(_mean(zc * zc) + LN_EPS)
    return zc * rstd, rstd


def _ln_bwd(dy, xhat, rstd, g):
    dxh = dy * g
    return rstd * (dxh - _mean(dxh) - xhat * _mean(dxh * xhat))


def _sgu_fwd(h, wm, bs_t, g_v, b_v):
    t = h.shape[1]

    def body(u_ref, v_ref, wm_ref, bs_ref, g_ref, b_ref, y_ref):
        xhat, _ = _ln_stats(_gelu(v_ref[...]))
        vn = (xhat * g_ref[...] + b_ref[...]).astype(BF16)
        gu = _gelu(u_ref[...])
        for g in range(GROUPS):
            sl = slice(g * HEAD, (g + 1) * HEAD)
            mixed = _dot(wm_ref[g], vn[:, sl]) + bs_ref[:, g:g + 1]
            y_ref[:, sl] = (gu[:, sl] * mixed).astype(BF16)

    blk = lambda b: pl.BlockSpec((None, SGU_BLOCK, D_MODEL), functools.partial(lambda i, b: (b, i, 0), b=b))
    whole = lambda s: pl.BlockSpec(s, lambda i: (0,) * len(s))
    return pl.pallas_call(
        body,
        grid=(t // SGU_BLOCK,),
        in_specs=[blk(SLOT_U), blk(SLOT_V), whole(wm.shape), whole(bs_t.shape), whole(g_v.shape), whole(b_v.shape)],
        out_specs=pl.BlockSpec((SGU_BLOCK, D_MODEL), lambda i: (i, 0)),
        out_shape=jax.ShapeDtypeStruct((t, D_MODEL), BF16),
        compiler_params=_params(("parallel",)),
        name="sgu_fwd",
    )(h, h, wm, bs_t, g_v, b_v)


def _sgu_bwd(h, dy, dh, wm, wm_t, bs_t, g_v, b_v):
    t = h.shape[1]

    def body(u_ref, v_ref, dy_ref, dh_in, wm_ref, wmt_ref, bs_ref, g_ref, b_ref,
             duv_ref, dw_ref, dbs_ref, dg_ref, db_ref, dvn_ref):
        del dh_in
        du_ref, dv_ref = duv_ref.at[0], duv_ref.at[1]
        i = pl.program_id(0)

        @pl.when(i == 0)
        def _():
            dw_ref[...] = jnp.zeros_like(dw_ref)
            dbs_ref[...] = jnp.zeros_like(dbs_ref)
            dg_ref[...] = jnp.zeros_like(dg_ref)
            db_ref[...] = jnp.zeros_like(db_ref)

        u = u_ref[...]
        v = v_ref[...]
        xhat, rstd = _ln_stats(_gelu(v))
        vn = (xhat * g_ref[...] + b_ref[...]).astype(BF16)
        gu = _gelu(u)
        gup = _gelu_grad(u)
        lane = lax.broadcasted_iota(jnp.int32, (SGU_BLOCK, LANE), 1)
        dbs = jnp.zeros((SGU_BLOCK, LANE), F32)
        for g in range(GROUPS):
            sl = slice(g * HEAD, (g + 1) * HEAD)
            vn_g = vn[:, sl]
            mixed = _dot(wm_ref[g], vn_g) + bs_ref[:, g:g + 1]
            dy_g = dy_ref[:, sl]
            du_ref[:, sl] = (dy_g * mixed * gup[:, sl]).astype(BF16)
            dmix = dy_g * gu[:, sl]
            dmb = dmix.astype(BF16)
            dvn_ref[:, sl] = _dot(wmt_ref[g], dmb)
            dw_ref[g] += _dot_nt(dmb, vn_g)
            dbs = dbs + jnp.where(lane == g, jnp.sum(dmix, axis=1, keepdims=True), 0.0)
        dbs_ref[...] += dbs
        dvn = dvn_ref[...]
        dg_ref[...] += _sum0(dvn * xhat)
        db_ref[...] += _sum0(dvn)
        dv_ref[...] = (_ln_bwd(dvn, xhat, rstd, g_ref[...]) * _gelu_grad(v)).astype(BF16)

    blk = lambda b: pl.BlockSpec((None, SGU_BLOCK, D_MODEL), functools.partial(lambda i, b: (b, i, 0), b=b))
    row = pl.BlockSpec((SGU_BLOCK, D_MODEL), lambda i: (i, 0))
    whole = lambda s: pl.BlockSpec(s, lambda i: (0,) * len(s))
    vec = (1, D_MODEL)
    return pl.pallas_call(
        body,
        grid=(t // SGU_BLOCK,),
        in_specs=[blk(SLOT_U), blk(SLOT_V), row, pl.BlockSpec(memory_space=pl.ANY),
                  whole(wm.shape), whole(wm_t.shape), whole(bs_t.shape), whole(vec), whole(vec)],
        out_specs=[pl.BlockSpec((2, SGU_BLOCK, D_MODEL), lambda i: (SLOT_U // 2, i, 0)),
                   whole(wm.shape), whole((SGU_BLOCK, LANE)), whole(vec), whole(vec)],
        out_shape=[jax.ShapeDtypeStruct(dh.shape, BF16),
                   jax.ShapeDtypeStruct(wm.shape, F32), jax.ShapeDtypeStruct((SGU_BLOCK, LANE), F32),
                   jax.ShapeDtypeStruct(vec, F32), jax.ShapeDtypeStruct(vec, F32)],
        scratch_shapes=[pltpu.VMEM((SGU_BLOCK, D_MODEL), F32)],
        input_output_aliases={3: 0},
        compiler_params=_params(("arbitrary",)),
        name="sgu_bwd",
    )(h, h, dy, dh, wm, wm_t, bs_t, g_v, b_v)


def _split3(x):
    hi = x.astype(BF16)
    r1 = x - hi.astype(F32)
    mid = r1.astype(BF16)
    lo = (r1 - mid.astype(F32)).astype(BF16)
    return hi, mid, lo


def _tri_matmul(tri, x):
    hi, mid, lo = _split3(x)
    dot = lambda p: jnp.dot(tri, p, preferred_element_type=F32)
    return dot(hi) + dot(mid) + dot(lo)


def _lower_bound(logits):
    l0, l1 = logits[0:1, :], logits[1:2, :]
    mx = jnp.maximum(l0, l1)
    e0, e1 = jnp.exp(l0 - mx), jnp.exp(l1 - mx)
    return e0 / (e0 + e1)


def _hgrn_gates(q_raw, f_raw, lb):
    q = q_raw * jax.nn.sigmoid(q_raw)
    sig = jax.nn.sigmoid(f_raw)
    f = lb + (1.0 - lb) * sig
    row = lax.broadcasted_iota(jnp.int32, (CHUNK, CHUNK), 0)
    col = lax.broadcasted_iota(jnp.int32, (CHUNK, CHUNK), 1)
    c = _tri_matmul((row >= col).astype(BF16), jnp.log(f))
    return q, sig, f, 1.0 - f, c


def _offdiag_terms(qh, kh, ch, tb):
    rows = slice(tb * SUB, (tb + 1) * SUB)
    r = ch[tb * SUB - 1:tb * SUB, :]
    eqh = jnp.exp(ch[rows] - r)
    ekh = jnp.exp(jnp.minimum(r - ch, 0.0))
    return rows, eqh, qh[rows] * eqh, ekh, kh * ekh


def _diag_decay(cb, s, trow):
    return jnp.exp(jnp.where(trow >= s, cb - cb[s:s + 1, :], NEG))


def _split_refs(refs, n_in, n_out, n_scratch, ex):
    nx = ex.n if ex else 0
    ins, refs = refs[:n_in], refs[n_in:]
    xsrc, refs = refs[:nx], refs[nx:]
    outs, refs = refs[:n_out], refs[n_out:]
    xout, refs = refs[:nx], refs[nx:]
    return ins, outs, refs[:n_scratch], (xsrc, xout, refs[n_scratch:])


def _hgrn_fwd(h, logits, g_norm, ex=None):
    t = h.shape[1]
    nc = t // CHUNK

    def body(*refs):
        ins, outs, scratch, xrefs = _split_refs(refs, 6, 3, 4, ex)
        q_ref, f_ref, i_ref, og_ref, lg_ref, gn_ref = ins
        y_ref, o_ref, sall_ref = outs
        st_ref, q_s, k_s, c_s = scratch

        @pl.when(pl.program_id(0) == 0)
        def _():
            st_ref[...] = jnp.zeros_like(st_ref)
            if ex:
                ex.start(*xrefs)

        if ex:
            @pl.when(pl.program_id(0) == (3 * nc) // 4)
            def _():
                ex.relay(*xrefs)

        lb = _lower_bound(lg_ref[...])
        q, _, _, k, c = _hgrn_gates(q_ref[...], f_ref[...], lb)
        q_s[...] = q
        k_s[...] = k
        c_s[...] = c
        col64 = lax.broadcasted_iota(jnp.int32, (SUB, CHUNK), 1)
        trow = lax.broadcasted_iota(jnp.int32, (SUB, HEAD), 0)

        def head(hd, carry):
            sl = pl.ds(pl.multiple_of(hd * HEAD, HEAD), HEAD)
            qh, kh, ch, ih = q_s[:, sl], k_s[:, sl], c_s[:, sl], i_ref[:, sl]
            st = st_ref[hd]
            sall_ref[hd] = st
            c_last = ch[CHUNK - 1:CHUNK, :]
            o = _dot_nt(qh * jnp.exp(ch), st)
            st_ref[hd] = st * jnp.exp(c_last) + _dot_tn(ih, kh * jnp.exp(c_last - ch))
            a_rows = [jnp.zeros((SUB, CHUNK), F32)]
            for tb in range(1, CHUNK // SUB):
                _, _, q_hat, _, k_hat = _offdiag_terms(qh, kh, ch, tb)
                a_rows.append(jnp.where(col64 < tb * SUB, _dot_nt(q_hat, k_hat), 0.0))
            o = o + _dot(jnp.concatenate(a_rows, axis=0), ih)
            o_rows = []
            for b in range(CHUNK // SUB):
                rows = slice(b * SUB, (b + 1) * SUB)
                qb, cb, kb, ib = qh[rows], ch[rows], kh[rows], ih[rows]
                ob = jnp.zeros((SUB, HEAD), F32)
                for s in range(SUB):
                    a = jnp.sum(qb * _diag_decay(cb, s, trow) * kb[s:s + 1, :], axis=1, keepdims=True)
                    ob = ob + a * ib[s:s + 1, :]
                o_rows.append(ob)
            o = o + jnp.concatenate(o_rows, axis=0)
            o_ref[:, sl] = o
            og = og_ref[:, sl]
            on = o * lax.rsqrt(_mean(o * o) + RMS_EPS)
            y_ref[:, sl] = (on * gn_ref[:, sl] * (og * jax.nn.sigmoid(og))).astype(BF16)
            return carry

        lax.fori_loop(0, HEADS, head, 0, unroll=HEAD_UNROLL)

        if ex:
            @pl.when(pl.program_id(0) == nc - 1)
            def _():
                ex.finish(*xrefs)

    blk = lambda b: pl.BlockSpec((None, CHUNK, D_MODEL), functools.partial(lambda n, b: (b, n, 0), b=b))
    row = pl.BlockSpec((CHUNK, D_MODEL), lambda n: (n, 0))
    whole = lambda s: pl.BlockSpec(s, lambda n: (0,) * len(s))
    res = pl.pallas_call(
        body,
        grid=(nc,),
        in_specs=[blk(SLOT_Q), blk(SLOT_F), blk(SLOT_I), blk(SLOT_OG), whole(logits.shape), whole(g_norm.shape)]
        + (ex.in_specs if ex else []),
        out_specs=[row, row, pl.BlockSpec((None, HEADS, HEAD, HEAD), lambda n: (n, 0, 0, 0))] + (ex.out_specs if ex else []),
        out_shape=[jax.ShapeDtypeStruct((t, D_MODEL), BF16), jax.ShapeDtypeStruct((t, D_MODEL), F32),
                   jax.ShapeDtypeStruct((nc, HEADS, HEAD, HEAD), F32)] + (ex.out_shape if ex else []),
        scratch_shapes=[pltpu.VMEM((HEADS, HEAD, HEAD), F32)] + [pltpu.VMEM((CHUNK, D_MODEL), F32)] * 3
        + (ex.scratch if ex else []),
        compiler_params=_params(("arbitrary",)),
        name="hgrn_fwd",
    )(h, h, h, h, logits, g_norm, *(ex.arrays if ex else []))
    return res[0], res[1], res[2], res[3:]


def _hgrn_bwd(h, o_all, dy, states, dh, logits, g_norm, ex=None):
    t = h.shape[1]
    nc = t // CHUNK

    def body(*refs):
        ins, outs, scratch, xrefs = _split_refs(refs, 10, 3, 8, ex)
        q_ref, f_ref, i_ref, og_ref, o_ref, dy_ref, sall_ref, _, lg_ref, gn_ref = ins
        dqfio_ref, dlg_ref, dgn_ref = outs
        dst_ref, dlb_ref, q_s, k_s, c_s, dq_s, dk_s, dc_s = scratch
        dq_ref, df_ref, di_ref, dog_ref = (dqfio_ref.at[s] for s in (SLOT_Q, SLOT_F, SLOT_I, SLOT_OG))
        n = pl.program_id(0)

        @pl.when(n == 0)
        def _():
            dst_ref[...] = jnp.zeros_like(dst_ref)
            dlb_ref[...] = jnp.zeros_like(dlb_ref)
            dgn_ref[...] = jnp.zeros_like(dgn_ref)
            if ex:
                ex.start(*xrefs)

        lb = _lower_bound(lg_ref[...])
        q_raw = q_ref[...]
        q, sig, f, k, c = _hgrn_gates(q_raw, f_ref[...], lb)
        q_s[...] = q
        k_s[...] = k
        c_s[...] = c
        col64 = lax.broadcasted_iota(jnp.int32, (SUB, CHUNK), 1)
        trow = lax.broadcasted_iota(jnp.int32, (SUB, HEAD), 0)
        row64 = lax.broadcasted_iota(jnp.int32, (CHUNK, HEAD), 0)

        def head(hd, carry):
            sl = pl.ds(pl.multiple_of(hd * HEAD, HEAD), HEAD)
            qh, kh, ch, ih = q_s[:, sl], k_s[:, sl], c_s[:, sl], i_ref[:, sl]
            st = sall_ref[hd]
            dst = dst_ref[hd]
            oh, dyh, og, gn = o_ref[:, sl], dy_ref[:, sl], og_ref[:, sl], gn_ref[:, sl]
            sg = jax.nn.sigmoid(og)
            sil = og * sg
            rms = lax.rsqrt(_mean(oh * oh) + RMS_EPS)
            on = oh * rms
            dog_ref[:, sl] = (dyh * on * gn * _silu_grad(og, sg)).astype(BF16)
            dgn_ref[:, sl] += _sum0(dyh * on * sil)
            don = dyh * gn * sil
            do = rms * (don - on * _mean(don * on))
            dob = do.astype(BF16)

            c_last = ch[CHUNK - 1:CHUNK, :]
            eq = jnp.exp(ch)
            q_til = qh * eq
            ekl = jnp.exp(c_last - ch)
            k_til = kh * ekl
            ecl = jnp.exp(c_last)
            dq_til = _dot(dob, st)
            dk_til = _dot(ih, dst)
            di = _dot_nt(k_til, dst)
            dc_last = _sum0(dk_til * k_til) + _sum0(dst * st) * ecl
            dst_ref[hd] = _dot_tn(dob, q_til) + dst * ecl
            dq = dq_til * eq
            dc = dq_til * q_til - dk_til * k_til
            dk = dk_til * ekl

            da_full = _dot_nt(dob, ih)
            a_rows = [jnp.zeros((SUB, CHUNK), F32)]
            dq_rows = [jnp.zeros((SUB, HEAD), F32)]
            dc_rows = [jnp.zeros((SUB, HEAD), F32)]
            for tb in range(1, CHUNK // SUB):
                rows, eqh, q_hat, ekh, k_hat = _offdiag_terms(qh, kh, ch, tb)
                keep = col64 < tb * SUB
                a_rows.append(jnp.where(keep, _dot_nt(q_hat, k_hat), 0.0))
                da = jnp.where(keep, da_full[rows], 0.0)
                dq_hat = _dot(da, k_hat)
                dk_hat = _dot_tn(da, q_hat)
                dq_rows.append(dq_hat * eqh)
                dc_rows.append(dq_hat * q_hat)
                dk = dk + dk_hat * ekh
                dc = dc - dk_hat * k_hat
            di = di + _dot_tn(jnp.concatenate(a_rows, axis=0), dob)

            dk_rows, di_rows = [], []
            for b in range(CHUNK // SUB):
                rows = slice(b * SUB, (b + 1) * SUB)
                qb, cb, kb, ib, dob_ = qh[rows], ch[rows], kh[rows], ih[rows], do[rows]
                dq_diag = jnp.zeros((SUB, HEAD), F32)
                dk_diag = jnp.zeros((SUB, HEAD), F32)
                di_diag = jnp.zeros((SUB, HEAD), F32)
                for s in range(SUB):
                    ks = kb[s:s + 1, :]
                    dec = _diag_decay(cb, s, trow)
                    a = jnp.sum(qb * dec * ks, axis=1, keepdims=True)
                    gk = jnp.sum(dob_ * ib[s:s + 1, :], axis=1, keepdims=True) * dec
                    dq_diag = dq_diag + gk * ks
                    dk_diag = dk_diag + jnp.where(trow == s, _sum0(gk * qb), 0.0)
                    di_diag = di_diag + jnp.where(trow == s, _sum0(a * dob_), 0.0)
                dq_rows[b] = dq_rows[b] + dq_diag
                dc_rows[b] = dc_rows[b] + qb * dq_diag - kb * dk_diag
                dk_rows.append(dk_diag)
                di_rows.append(di_diag)
            dq = dq + jnp.concatenate(dq_rows, axis=0)
            dk = dk + jnp.concatenate(dk_rows, axis=0)
            dc = dc + jnp.concatenate(dc_rows, axis=0) + jnp.where(row64 == CHUNK - 1, dc_last, 0.0)
            di_ref[:, sl] = (di + jnp.concatenate(di_rows, axis=0)).astype(BF16)
            dq_s[:, sl] = dq
            dk_s[:, sl] = dk
            dc_s[:, sl] = dc
            return carry

        lax.fori_loop(0, HEADS, head, 0, unroll=HEAD_UNROLL)

        row = lax.broadcasted_iota(jnp.int32, (CHUNK, CHUNK), 0)
        col = lax.broadcasted_iota(jnp.int32, (CHUNK, CHUNK), 1)
        dlf = _tri_matmul((row <= col).astype(BF16), dc_s[...])
        df = dlf / f - dk_s[...]
        dlb_ref[...] += _sum0(df * (1.0 - sig))
        df_ref[...] = (df * (1.0 - lb) * sig * (1.0 - sig)).astype(BF16)
        dq_ref[...] = (dq_s[...] * _silu_grad(q_raw, jax.nn.sigmoid(q_raw))).astype(BF16)

        @pl.when(n == nc - 1)
        def _():
            d0 = dlb_ref[...] * lb * (1.0 - lb)
            dlg_ref[0:1, :] = d0
            dlg_ref[1:2, :] = -d0
            if ex:
                ex.finish(*xrefs)

    rev = lambda n: nc - 1 - n
    blk = lambda b: pl.BlockSpec((None, CHUNK, D_MODEL), functools.partial(lambda n, b: (b, rev(n), 0), b=b))
    row = pl.BlockSpec((CHUNK, D_MODEL), lambda n: (rev(n), 0))
    whole = lambda s: pl.BlockSpec(s, lambda n: (0,) * len(s))
    vec = (1, D_MODEL)
    res = pl.pallas_call(
        body,
        grid=(nc,),
        in_specs=[blk(SLOT_Q), blk(SLOT_F), blk(SLOT_I), blk(SLOT_OG), row, row,
                  pl.BlockSpec((None, HEADS, HEAD, HEAD), lambda n: (rev(n), 0, 0, 0)),
                  pl.BlockSpec(memory_space=pl.ANY), whole(logits.shape), whole(vec)] + (ex.in_specs if ex else []),
        out_specs=[pl.BlockSpec((4, CHUNK, D_MODEL), lambda n: (SLOT_Q // 4, rev(n), 0)), whole((2, D_MODEL)), whole(vec)]
        + (ex.out_specs if ex else []),
        out_shape=[jax.ShapeDtypeStruct(dh.shape, BF16), jax.ShapeDtypeStruct((2, D_MODEL), F32),
                   jax.ShapeDtypeStruct(vec, F32)] + (ex.out_shape if ex else []),
        scratch_shapes=[pltpu.VMEM((HEADS, HEAD, HEAD), F32), pltpu.VMEM(vec, F32)]
        + [pltpu.VMEM((CHUNK, D_MODEL), F32)] * 6 + (ex.scratch if ex else []),
        input_output_aliases={7: 0},
        compiler_params=_params(("arbitrary",)),
        name="hgrn_bwd",
    )(h, h, h, h, o_all, dy, states, dh, logits, g_norm, *(ex.arrays if ex else []))
    return res[0], res[1], res[2], res[3:]


def _merge_fwd(i, n, ga, gb, za, zb):
    return jax.nn.sigmoid(ga) * za + jax.nn.sigmoid(gb) * zb


def _merge_bwd(i, n, ga, gb, za, zb, dm):
    sa, sb = jax.nn.sigmoid(ga), jax.nn.sigmoid(gb)
    dgates = jnp.stack([(dm * za * sa * (1.0 - sa)).astype(BF16), (dm * zb * sb * (1.0 - sb)).astype(BF16)])
    return dgates, dm * sa, dm * sb


def _ln1_fwd(i, n, x, r1, g, b):
    xhat, _ = _ln_stats(ALPHA * x + r1)
    x1 = xhat * g + b
    return x1, x1


def _ln1_bwd(i, n, x, r1, dx1, g):
    xhat, rstd = _ln_stats(ALPHA * x + r1)
    dz = _ln_bwd(dx1, xhat, rstd, g)
    return dz, dz, _sum0(dx1 * xhat), _sum0(dx1)


def _ln2_loss(i, n, x1, fo, pg, pp, tgt, g, b):
    sg = jax.nn.sigmoid(pg)
    xhat, rstd = _ln_stats(ALPHA * x1 + fo + sg * pp)
    diff = xhat * g + b - tgt
    loss = 0.5 * jnp.sum(_mean(diff * diff), axis=0, keepdims=True)
    dy = diff * (1.0 / D_MODEL)
    dz = _ln_bwd(dy, xhat, rstd, g)
    return (dz, dz, dz * pp * sg * (1.0 - sg), dz * sg,
            jnp.broadcast_to(loss, (8, LANE)), _sum0(dy * xhat), _sum0(dy))


def _shift_down(cur, halo, tile):
    row = lax.broadcasted_iota(jnp.int32, cur.shape, 0)
    m1 = jnp.where(row == 0, halo[7:8, :], pltpu.roll(cur, 1, 0))
    m2 = jnp.where(row == 0, halo[6:7, :], jnp.where(row == 1, halo[7:8, :], pltpu.roll(cur, 2, 0)))
    return m1, m2


def _shift_up(cur, halo, tile):
    row = lax.broadcasted_iota(jnp.int32, cur.shape, 0)
    p1 = jnp.where(row == tile - 1, halo[0:1, :], pltpu.roll(cur, tile - 1, 0))
    p2 = jnp.where(row == tile - 2, halo[0:1, :], jnp.where(row == tile - 1, halo[1:2, :], pltpu.roll(cur, tile - 2, 0)))
    return p1, p2


def _conv_pre(i, gate, halo, w, b, tile):
    halo = jnp.where(i == 0, 0.0, halo)
    m1, m2 = _shift_down(gate, halo, tile)
    return w[0:1, :] * m2 + w[1:2, :] * m1 + w[2:3, :] * gate + b, m1, m2


def _conv_fwd(tile, i, n, gate, halo, val, w, b):
    cg, _, _ = _conv_pre(i, gate, halo, w, b, tile)
    return _gelu(cg) * val


def _conv_bwd_a(tile, i, n, gate, halo, val, dhid, w, b):
    cg, m1, m2 = _conv_pre(i, gate, halo, w, b, tile)
    dcg = dhid * val * _gelu_grad(cg)
    return dcg, dhid * _gelu(cg), _sum0(dcg * m2), _sum0(dcg * m1), _sum0(dcg * gate), _sum0(dcg)


def _conv_bwd_b(tile, i, n, dcg, halo, w):
    halo = jnp.where(i == n - 1, 0.0, halo)
    p1, p2 = _shift_up(dcg, halo, tile)
    return w[2:3, :] * dcg + w[1:2, :] * p1 + w[0:1, :] * p2


def _halo_spec(width, tile, t, nxt):
    per = tile // 8
    last = t // 8 - 1
    if nxt:
        return pl.BlockSpec((8, width), lambda i: (jnp.minimum((i + 1) * per, last), 0))
    return pl.BlockSpec((8, width), lambda i: (jnp.maximum(i * per - 1, 0), 0))


def _adamw(i, n, w, m, v, parts):
    g = parts[0].astype(F32)
    for j in range(1, parts.shape[0]):
        g = g + parts[j].astype(F32)
    m_new = ADAM_B1 * m + (1.0 - ADAM_B1) * g
    v_new = ADAM_B2 * v + (1.0 - ADAM_B2) * (g * g)
    m_hat = m_new / (1.0 - ADAM_B1 ** ADAM_STEP)
    v_hat = v_new / (1.0 - ADAM_B2 ** ADAM_STEP)
    delta = -ADAM_LR * (m_hat / (jnp.sqrt(v_hat) + ADAM_EPS) + ADAM_WD * w)
    return g, delta, m_new, v_new


def _adam_call(w, m, v, parts, name):
    r, c = w.shape
    tile = _pick(r, (256, 128)) if r > 256 else r
    spec = pl.BlockSpec((parts.shape[0], tile, c), lambda i: (0, i, 0))
    return _rowwise(_adamw, [w, m, v, (parts, spec)], [], [(c, F32)] * 4, [], tile=tile, name=name)


def _peer(k):
    x, y, c = lax.axis_index("x"), lax.axis_index("y"), lax.axis_index("c")
    px = x ^ ((k >> 2) & 1)
    py = y ^ ((k >> 1) & 1)
    pc = c ^ (k & 1)
    return (px, py, pc), 4 * px + 2 * py + pc


def _my_index():
    return 4 * lax.axis_index("x") + 2 * lax.axis_index("y") + lax.axis_index("c")


class _Exchange:
    KINDS = ("gather", "gather+relay", "gather+relay+slots", "scatter", "scatter+pairs")

    def __init__(self, entries):
        assert all(k in self.KINDS for _, k in entries), [k for _, k in entries]
        self.arrays = [a for a, _ in entries]
        self.scatter = [k.startswith("scatter") for _, k in entries]
        self.relayed = ["+relay" in k for _, k in entries]
        self.slots = ["+slots" in k for _, k in entries]
        self.pairs = ["+pairs" in k for _, k in entries]
        self.n = len(entries)
        self.in_specs = [pl.BlockSpec(memory_space=pl.ANY)] * self.n
        self.out_specs = [pl.BlockSpec(memory_space=pl.ANY)] * self.n
        shapes = [tuple(a.shape[1:]) if sc else tuple(a.shape) for a, sc in zip(self.arrays, self.scatter)]
        counts = [N_CHIP if p else N_DEV for p in self.pairs]
        self.out_shape = [jax.ShapeDtypeStruct((n,) + s, a.dtype) for n, s, a in zip(counts, shapes, self.arrays)]
        per = N_DEV - 1
        self.scratch = [pltpu.SemaphoreType.DMA((self.n * per,)), pltpu.SemaphoreType.DMA((self.n * per,)),
                        pltpu.SemaphoreType.DMA((self.n,))]

    def _copies(self, srcs, outs, sems):
        send_sems, recv_sems, local_sems = sems
        x, y, c = lax.axis_index("x"), lax.axis_index("y"), lax.axis_index("c")
        me = _my_index()
        per = N_DEV - 1
        local, first, passed, relay_arrivals, arrivals = [], [], [], [], []
        for a in range(self.n):
            place = (lambda idx: _slot_of_group(idx)) if self.slots[a] else (lambda idx: idx)

            def copy(k, src, dst, dev, a=a):
                return pltpu.make_async_remote_copy(
                    src_ref=src, dst_ref=dst, send_sem=send_sems.at[a * per + k], recv_sem=recv_sems.at[a * per + k],
                    device_id=dev, device_id_type=pl.DeviceIdType.MESH)

            if self.pairs[a]:
                chip = 2 * x + y
                for k in range(N_CHIP):
                    to = chip ^ k
                    piece = srcs[a].at[_slot_of_group(2 * to + c) // 2]
                    if k == 0:
                        local.append(pltpu.make_async_copy(piece, outs[a].at[chip], local_sems.at[a]))
                    else:
                        dev = (to // 2, to % 2, c)
                        first.append(copy(k - 1, piece, outs[a].at[chip], dev))
                        arrivals.append(copy(k - 1, piece, outs[a].at[to], dev))
                continue
            if self.scatter[a]:
                mine, land = srcs[a].at[place(me)], outs[a].at[me]
            else:
                mine, land = srcs[a], outs[a].at[place(me)]
            local.append(pltpu.make_async_copy(mine, land, local_sems.at[a]))
            if self.relayed[a]:
                block = lambda px, py, pc, a=a, place=place: outs[a].at[place(4 * px + 2 * py + pc)]
                chips = [(1 - x, y), (x, 1 - y), (1 - x, 1 - y)]
                first.append(copy(0, mine, land, (x, y, 1 - c)))
                arrivals.append(copy(0, mine, block(x, y, 1 - c), (x, y, 1 - c)))
                for j, (px, py) in enumerate(chips):
                    first.append(copy(1 + j, mine, land, (px, py, c)))
                    relay_arrivals.append(copy(1 + j, mine, block(px, py, c), (px, py, c)))
                    passed.append(copy(4 + j, block(px, py, c), block(px, py, c), (x, y, 1 - c)))
                    arrivals.append(copy(4 + j, mine, block(px, py, 1 - c), (x, y, 1 - c)))
                continue
            for k in range(1, N_DEV):
                dev, idx = _peer(k)
                if self.scatter[a]:
                    first.append(copy(k - 1, srcs[a].at[place(idx)], land, dev))
                    arrivals.append(copy(k - 1, mine, outs[a].at[idx], dev))
                else:
                    first.append(copy(k - 1, mine, land, dev))
                    arrivals.append(copy(k - 1, mine, outs[a].at[place(idx)], dev))
        return local, first, passed, relay_arrivals, arrivals

    def start(self, srcs, outs, sems):
        local, first, _, _, _ = self._copies(srcs, outs, sems)
        for cp in local + first:
            cp.start()

    def relay(self, srcs, outs, sems):
        _, _, passed, relay_arrivals, _ = self._copies(srcs, outs, sems)
        for landed, onward in zip(relay_arrivals, passed):
            landed.wait_recv()
            onward.start()

    def finish(self, srcs, outs, sems):
        local, first, passed, _, arrivals = self._copies(srcs, outs, sems)
        for cp in arrivals:
            cp.wait_recv()
        for cp in first + passed:
            cp.wait_send()
        for cp in local:
            cp.wait()


def _gather_project(x_b, shard):
    t, d = x_b.shape
    tm = _pick(t, MM_TILES)
    nrow = t // tm
    per = N_DEV - 1

    def parties():
        x, y, c = lax.axis_index("x"), lax.axis_index("y"), lax.axis_index("c")
        chips = [(1 - x, y), (x, 1 - y), (1 - x, 1 - y)]
        return (x, y, c), (x, y, 1 - c), [(px, py, c) for px, py in chips], [(px, py, 1 - c) for px, py in chips]

    slot = lambda dev: _slot_of_group(4 * dev[0] + 2 * dev[1] + dev[2])
    me, sibling, over_ici, passed_on = parties()
    order = jnp.stack([slot(dev) for dev in [me, sibling] + over_ici + passed_on]).astype(jnp.int32)

    def body(order_ref, x_ref, shard_ref, h_ref, wall_ref, wbuf, fetch_sem, send_sems, recv_sems, local_sem):
        del order_ref
        me, sibling, over_ici, passed_on = parties()
        j, i = pl.program_id(0), pl.program_id(1)
        land = lambda dev: wall_ref.at[slot(dev)]

        def copy(k, src, block, to):
            return pltpu.make_async_remote_copy(src_ref=src, dst_ref=land(block), send_sem=send_sems.at[k],
                                                recv_sem=recv_sems.at[k], device_id=to, device_id_type=pl.DeviceIdType.MESH)

        def fetch(src):
            cp = pltpu.make_async_copy(src, wbuf, fetch_sem)
            cp.start()
            cp.wait()

        keep = pltpu.make_async_copy(shard_ref, land(me), local_sem)
        first = [copy(0, shard_ref, me, sibling)] + [copy(1 + n, shard_ref, me, dev) for n, dev in enumerate(over_ici)]
        onward = [copy(4 + n, land(dev), dev, sibling) for n, dev in enumerate(over_ici)]

        @pl.when(jnp.logical_and(i == 0, j == 0))
        def _():
            keep.start()
            for cp in first:
                cp.start()
            fetch(shard_ref)

        @pl.when(jnp.logical_and(i == 0, j == 1))
        def _():
            copy(0, shard_ref, sibling, me).wait_recv()
            fetch(land(sibling))

        for n, dev in enumerate(over_ici):
            @pl.when(jnp.logical_and(i == 0, j == 2 + n))
            def _(n=n, dev=dev):
                copy(1 + n, shard_ref, dev, me).wait_recv()
                onward[n].start()
                fetch(land(dev))

        for n, dev in enumerate(passed_on):
            @pl.when(jnp.logical_and(i == 0, j == 5 + n))
            def _(n=n, dev=dev):
                copy(4 + n, shard_ref, dev, me).wait_recv()
                fetch(land(dev))

        h_ref[...] = _dot(x_ref[...], wbuf[...])

        @pl.when(jnp.logical_and(i == nrow - 1, j == N_DEV - 1))
        def _():
            for cp in first + onward:
                cp.wait_send()
            keep.wait()

    need = 2 * (tm * d * 2 + tm * d * 4) + d * d * 2 + tm * d * 4
    h, w_all = pl.pallas_call(
        body,
        grid_spec=pltpu.PrefetchScalarGridSpec(
            num_scalar_prefetch=1,
            grid=(N_DEV, nrow),
            in_specs=[pl.BlockSpec((tm, d), lambda j, i, order: (i, 0)), pl.BlockSpec(memory_space=pl.ANY)],
            out_specs=[pl.BlockSpec((None, tm, d), lambda j, i, order: (order[j], i, 0)), pl.BlockSpec(memory_space=pl.ANY)],
            scratch_shapes=[pltpu.VMEM((d, d), BF16), pltpu.SemaphoreType.DMA, pltpu.SemaphoreType.DMA((per,)),
                            pltpu.SemaphoreType.DMA((per,)), pltpu.SemaphoreType.DMA],
        ),
        out_shape=[jax.ShapeDtypeStruct((N_DEV, t, d), F32), jax.ShapeDtypeStruct((N_DEV, d, d), BF16)],
        compiler_params=_params(("arbitrary", "arbitrary"), need),
        name="gather_project",
    )(order, x_b, shard)
    return h, w_all


def _pair_sums(g):
    n, r, cols = g.shape
    half = n // 2

    def swap(g_ref, got_ref, send_sems, recv_sems):
        x, y, c = lax.axis_index("x"), lax.axis_index("y"), lax.axis_index("c")
        copies = [pltpu.make_async_remote_copy(
            src_ref=g_ref.at[2 * j + 1 - c], dst_ref=got_ref.at[j], send_sem=send_sems.at[j], recv_sem=recv_sems.at[j],
            device_id=(x, y, 1 - c), device_id_type=pl.DeviceIdType.MESH) for j in range(half)]
        for cp in copies:
            cp.start()
        for cp in copies:
            cp.wait()

    got = pl.pallas_call(
        swap,
        in_specs=[pl.BlockSpec(memory_space=pl.ANY)],
        out_specs=pl.BlockSpec(memory_space=pl.ANY),
        out_shape=jax.ShapeDtypeStruct((half, r, cols), g.dtype),
        scratch_shapes=[pltpu.SemaphoreType.DMA((half,))] * 2,
        name="pair_swap",
    )(g)

    def add(mine_ref, got_ref, out_ref):
        mine = jnp.where(lax.axis_index("c") == 0, mine_ref[0].astype(F32), mine_ref[1].astype(F32))
        out_ref[...] = (mine + got_ref[...].astype(F32)).astype(out_ref.dtype)

    tile = _pick(r, (512, 256, 128))
    return pl.pallas_call(
        add,
        grid=(half, r // tile),
        in_specs=[pl.BlockSpec((None, 2, tile, cols), lambda j, i: (j, 0, i, 0)),
                  pl.BlockSpec((None, tile, cols), lambda j, i: (j, i, 0))],
        out_specs=pl.BlockSpec((None, tile, cols), lambda j, i: (j, i, 0)),
        out_shape=jax.ShapeDtypeStruct((half, r, cols), g.dtype),
        compiler_params=_params(("parallel", "parallel")),
        name="pair_add",
    )(g.reshape(half, 2, r, cols), got)


def _exchange(entries, name):
    ex = _Exchange(entries)
    n = ex.n

    def body(*refs):
        xrefs = refs[:n], refs[n:2 * n], refs[2 * n:]
        ex.start(*xrefs)
        ex.relay(*xrefs)
        ex.finish(*xrefs)

    return pl.pallas_call(body, in_specs=ex.in_specs, out_specs=ex.out_specs, out_shape=ex.out_shape,
                          scratch_shapes=ex.scratch, name=name)(*ex.arrays)


def _local_step(x, p, tgt, small, comm):
    t = x.shape[0]
    tile = _pick(t, (256, 128))
    d = D_MODEL
    act_b, act_f = (d, BF16), (d, F32)
    x_b, p_b = x.astype(BF16), p.astype(BF16)

    chunk_id = jnp.arange(SGU_BLOCK) // CHUNK
    mask = chunk_id[:, None] >= chunk_id[None, :]
    wm = jnp.where(mask[None], small["sgu_w_s"], 0.0)
    wm_b = wm.astype(BF16)
    wm_t = jnp.swapaxes(wm, 1, 2).astype(BF16)
    bs_t = small["sgu_b_s"].T

    h, w_in = comm.project_in(x_b)
    y_a = _sgu_fwd(h, wm_b, bs_t, small["sgu_norm_g"], small["sgu_norm_b"])
    y_b, o_all, states, got = _hgrn_fwd(h, small["lb_logits"], small["hgrn_norm_g"], ex=comm.weights_exchange())
    wts, conv_w = comm.weights(got)
    z_a = _mm(y_a, wts["w_a"], out_dtype=F32, name="mm_za")
    z_b = _mm(y_b, wts["w_b"], out_dtype=F32, name="mm_zb")
    gates = [(h, SLOT_GA), (h, SLOT_GB)]
    merged, = _rowwise(_merge_fwd, gates + [z_a, z_b], [], [act_b], [], tile=tile, name="merge_fwd")
    r1 = _mm(merged, wts["w_o"], out_dtype=F32, name="mm_r1")
    x1, x1_b = _rowwise(_ln1_fwd, [x, r1], [small["ln1_g"], small["ln1_b"]], [act_f, act_b], [], tile=tile, name="ln1_fwd")
    gate = _mm(x1_b, wts["w_g"], out_dtype=F32, name="mm_gate")
    val = _mm(x1_b, wts["w_v"], out_dtype=F32, name="mm_val")
    pg = _mm(x1_b, wts["w_pg"], out_dtype=F32, name="mm_pg")
    pp = _mm(p_b, wts["w_pp"], out_dtype=F32, name="mm_pp")
    hid, = _rowwise(functools.partial(_conv_fwd, tile), [gate, (gate, _halo_spec(D_FF, tile, t, False)), val],
                    [conv_w, small["conv_b"]], [(D_FF, BF16)], [], tile=tile, name="conv_fwd")
    fo = _mm(hid, wts["w_down"], out_dtype=F32, name="mm_down")
    dz2, dz2_b, dpg, dpp, loss, dg2, db2 = _rowwise(
        _ln2_loss, [x1, fo, pg, pp, tgt], [small["ln2_g"], small["ln2_b"]],
        [act_f, act_b, act_b, act_b], [(8, LANE), (1, d), (1, d)], tile=tile, name="ln2_loss")

    dhid = _mm(dz2_b, wts["w_down"], out_dtype=F32, name="mm_dhid", trans_b=True)
    g_down = _mm_tn(hid, dz2_b, out_dtype=BF16, name="mm_g_down")
    dcg, dval, dcw0, dcw1, dcw2, dcb = _rowwise(
        functools.partial(_conv_bwd_a, tile), [gate, (gate, _halo_spec(D_FF, tile, t, False)), val, dhid],
        [conv_w, small["conv_b"]], [(D_FF, F32), (D_FF, BF16)], [(1, D_FF)] * 4, tile=tile, name="conv_bwd_a")
    dgate, = _rowwise(functools.partial(_conv_bwd_b, tile), [dcg, (dcg, _halo_spec(D_FF, tile, t, True))],
                      [conv_w], [(D_FF, BF16)], [], tile=tile, name="conv_bwd_b")
    g_g = _mm_tn(x1_b, dgate, out_dtype=BF16, name="mm_g_gate")
    g_v = _mm_tn(x1_b, dval, out_dtype=BF16, name="mm_g_val")
    g_pg = _mm_tn(x1_b, dpg, out_dtype=BF16, name="mm_g_pg")
    g_pp = _mm_tn(p_b, dpp, out_dtype=BF16, name="mm_g_pp")
    dx1 = _mm(dgate, wts["w_g"], out_dtype=F32, name="mm_dx1_gate", trans_b=True, adds=[(dz2, ALPHA)])
    dx1 = _mm(dval, wts["w_v"], out_dtype=F32, name="mm_dx1_val", trans_b=True, adds=[(dx1, 1.0)])
    dx1 = _mm(dpg, wts["w_pg"], out_dtype=F32, name="mm_dx1_pg", trans_b=True, adds=[(dx1, 1.0)])
    dz1, dz1_b, dg1, db1 = _rowwise(_ln1_bwd, [x, r1, dx1], [small["ln1_g"]], [act_f, act_b], [(1, d), (1, d)],
                                    tile=tile, name="ln1_bwd")
    g_o = _mm_tn(merged, dz1_b, out_dtype=BF16, name="mm_g_o")
    dm = _mm(dz1_b, wts["w_o"], out_dtype=F32, name="mm_dm", trans_b=True)
    dh, dza, dzb = _rowwise(_merge_bwd, gates + [z_a, z_b, dm], [],
                            [("stack", 2, SLOT_GA // 2, 8, d, BF16), act_b, act_b], [], tile=tile, name="merge_bwd")
    g_a = _mm_tn(y_a, dza, out_dtype=BF16, name="mm_g_a")
    g_b = _mm_tn(y_b, dzb, out_dtype=BF16, name="mm_g_b")
    dy_a = _mm(dza, wts["w_a"], out_dtype=F32, name="mm_dya", trans_b=True)
    dy_b = _mm(dzb, wts["w_b"], out_dtype=F32, name="mm_dyb", trans_b=True)
    dh, dws, dbs, dgv_n, dbv_n = _sgu_bwd(h, dy_a, dh, wm_b, wm_t, bs_t, small["sgu_norm_g"], small["sgu_norm_b"])
    big = dict(w_a=g_a, w_b=g_b, w_o=g_o, w_g=g_g, w_v=g_v, w_down=g_down, w_pp=g_pp, w_pg=g_pg)
    sm = dict(sgu_w_s=jnp.where(mask[None], dws, 0.0), sgu_b_s=dbs[:, :GROUPS].T, sgu_norm_g=dgv_n, sgu_norm_b=dbv_n,
              ln1_g=dg1, ln1_b=db1, conv_w=jnp.concatenate([dcw0, dcw1, dcw2], axis=0), conv_b=dcb, ln2_g=dg2, ln2_b=db2,
              loss=loss)
    dh, dlogits, dgn, got = _hgrn_bwd(h, o_all, dy_b, states, dh, small["lb_logits"], small["hgrn_norm_g"],
                                      ex=comm.grads_exchange(big, sm))
    comm.grads_done(got)
    g_in = _mm_tn(x_b, dh, out_dtype=BF16, name="mm_g_in")
    ex = comm.last_exchange(g_in, dict(lb_logits=dlogits, hgrn_norm_g=dgn))
    res = _mm(dh, w_in, out_dtype=F32, name="mm_dx", trans_b=True, reduce_b=True, adds=[(dz1, ALPHA)], ex=ex)
    grad_x, got = res if ex else (res, ())
    comm.last_done(got)
    return grad_x


_SMALL_EARLY = ["sgu_w_s", "sgu_b_s", "sgu_norm_g", "sgu_norm_b", "ln1_g", "ln1_b", "ffn_conv_b", "ln2_g", "ln2_b"]
_SMALL_LATE = ["hgrn_lb_logits", "hgrn_norm_g"]
N_TAPS = D_FF // N_DEV
UP_COLS = 2 * D_FF // N_DEV


class _StepExchanges:
    def __init__(self, w_in_shard, shards):
        self.w_in_shard = w_in_shard
        self.shards = shards

    def project_in(self, x_b):
        return _gather_project(x_b, self.w_in_shard)

    def weights_exchange(self):
        return _Exchange([(s, "gather+relay") for s in self.shards])

    def weights(self, got):
        d, f = D_MODEL, D_FF
        w_br_g, w_o_g, w_up_g, w_down_g, w_pp_g, w_pg_g, conv_g = got
        w_br = w_br_g.transpose(1, 0, 2, 3).reshape(2, d, d)
        w_up = w_up_g.transpose(1, 0, 2).reshape(d, 2, f).transpose(1, 0, 2)
        wts = dict(w_a=w_br[0], w_b=w_br[1], w_o=w_o_g.reshape(d, d), w_g=w_up[0], w_v=w_up[1],
                   w_down=w_down_g.reshape(f, d), w_pp=w_pp_g.transpose(1, 0, 2).reshape(256, d), w_pg=w_pg_g.reshape(d, d))
        return wts, conv_g.transpose(1, 0, 2).reshape(3, f)

    def grads_exchange(self, big, sm):
        d = D_MODEL
        parts = [jnp.stack([big["w_a"], big["w_b"]]).reshape(2, N_DEV, 128, d).transpose(1, 0, 2, 3),
                 big["w_o"].reshape(N_DEV, 128, d),
                 jnp.concatenate([big["w_g"], big["w_v"]], axis=1).reshape(d, N_DEV, UP_COLS).transpose(1, 0, 2),
                 big["w_down"].reshape(N_DEV, N_TAPS, d),
                 big["w_pp"].reshape(256, N_DEV, 128).transpose(1, 0, 2),
                 big["w_pg"].reshape(N_DEV, 128, d)]
        packed, self.rows_early = _pack([sm[k] for k in ("sgu_w_s", "sgu_b_s", "sgu_norm_g", "sgu_norm_b", "ln1_g", "ln1_b",
                                                         "conv_b", "ln2_g", "ln2_b", "conv_w", "loss")])
        return _Exchange([(a, "scatter") for a in parts] + [(packed, "gather")])

    def grads_done(self, got):
        self.recv, self.small_early = got[:6], got[6]

    def last_exchange(self, g_in, sm):
        packed, self.rows_late = _pack([sm["lb_logits"], sm["hgrn_norm_g"]])
        return _Exchange([(_pair_sums(g_in), "scatter+pairs"), (packed, "gather")])

    def last_done(self, got):
        self.recv_in, self.small_late = got


def _rows128(a):
    flat = a.reshape(-1)
    rows = -(-flat.shape[0] // (8 * LANE)) * 8
    return jnp.pad(flat, (0, rows * LANE - flat.shape[0])).reshape(rows, LANE)


def _pack(parts):
    blocks = [_rows128(a) for a in parts]
    return jnp.concatenate(blocks, axis=0), [b.shape[0] for b in blocks]


def _unpack(packed, shapes, rows):
    out, r0 = [], 0
    for shp, r in zip(shapes, rows):
        n = math.prod(shp)
        out.append(packed[r0:r0 + r].reshape(-1)[:n].reshape(shp))
        r0 += r
    return out


def kernel(x, p, w_in, sgu_w_s, sgu_b_s, sgu_norm_g, sgu_norm_b, hgrn_lb_logits, hgrn_norm_g, w_branch, w_out, ln1_g, ln1_b, ffn_w_up, ffn_conv_w, ffn_conv_b, ffn_w_down, ln2_g, ln2_b, ple_w_proj, ple_w_gate, loss_target, m_w_in, m_sgu_w_s, m_sgu_b_s, m_sgu_norm_g, m_sgu_norm_b, m_hgrn_lb_logits, m_hgrn_norm_g, m_w_branch, m_w_out, m_ln1_g, m_ln1_b, m_ffn_w_up, m_ffn_conv_w, m_ffn_conv_b, m_ffn_w_down, m_ln2_g, m_ln2_b, m_ple_w_proj, m_ple_w_gate, v_w_in, v_sgu_w_s, v_sgu_b_s, v_sgu_norm_g, v_sgu_norm_b, v_hgrn_lb_logits, v_hgrn_norm_g, v_w_branch, v_w_out, v_ln1_g, v_ln1_b, v_ffn_w_up, v_ffn_conv_w, v_ffn_conv_b, v_ffn_w_down, v_ln2_g, v_ln2_b, v_ple_w_proj, v_ple_w_gate):
    weights = dict(w_in=w_in, sgu_w_s=sgu_w_s, sgu_b_s=sgu_b_s, sgu_norm_g=sgu_norm_g, sgu_norm_b=sgu_norm_b,
                   hgrn_lb_logits=hgrn_lb_logits, hgrn_norm_g=hgrn_norm_g, w_branch=w_branch, w_out=w_out,
                   ln1_g=ln1_g, ln1_b=ln1_b, ffn_w_up=ffn_w_up, ffn_conv_w=ffn_conv_w, ffn_conv_b=ffn_conv_b,
                   ffn_w_down=ffn_w_down, ln2_g=ln2_g, ln2_b=ln2_b, ple_w_proj=ple_w_proj, ple_w_gate=ple_w_gate)
    mom_m = dict(w_in=m_w_in, sgu_w_s=m_sgu_w_s, sgu_b_s=m_sgu_b_s, sgu_norm_g=m_sgu_norm_g, sgu_norm_b=m_sgu_norm_b,
                 hgrn_lb_logits=m_hgrn_lb_logits, hgrn_norm_g=m_hgrn_norm_g, w_branch=m_w_branch, w_out=m_w_out,
                 ln1_g=m_ln1_g, ln1_b=m_ln1_b, ffn_w_up=m_ffn_w_up, ffn_conv_w=m_ffn_conv_w, ffn_conv_b=m_ffn_conv_b,
                 ffn_w_down=m_ffn_w_down, ln2_g=m_ln2_g, ln2_b=m_ln2_b, ple_w_proj=m_ple_w_proj, ple_w_gate=m_ple_w_gate)
    mom_v = dict(w_in=v_w_in, sgu_w_s=v_sgu_w_s, sgu_b_s=v_sgu_b_s, sgu_norm_g=v_sgu_norm_g, sgu_norm_b=v_sgu_norm_b,
                 hgrn_lb_logits=v_hgrn_lb_logits, hgrn_norm_g=v_hgrn_norm_g, w_branch=v_w_branch, w_out=v_w_out,
                 ln1_g=v_ln1_g, ln1_b=v_ln1_b, ffn_w_up=v_ffn_w_up, ffn_conv_w=v_ffn_conv_w, ffn_conv_b=v_ffn_conv_b,
                 ffn_w_down=v_ffn_w_down, ln2_g=v_ln2_g, ln2_b=v_ln2_b, ple_w_proj=v_ple_w_proj, ple_w_gate=v_ple_w_gate)
    d, f = D_MODEL, D_FF
    me = _my_index()

    comm = _StepExchanges(w_in[0].astype(BF16),
                          [w_branch[0].astype(BF16), w_out[0].astype(BF16), ffn_w_up[0].astype(BF16),
                           ffn_w_down[0].astype(BF16), ple_w_proj[0].astype(BF16), ple_w_gate[0].astype(BF16), ffn_conv_w[0]])
    small = dict(sgu_w_s=sgu_w_s[0], sgu_b_s=sgu_b_s[0], sgu_norm_g=sgu_norm_g, sgu_norm_b=sgu_norm_b,
                 lb_logits=hgrn_lb_logits, hgrn_norm_g=hgrn_norm_g, ln1_g=ln1_g, ln1_b=ln1_b, ln2_g=ln2_g, ln2_b=ln2_b,
                 conv_b=ffn_conv_b)
    grad_x = _local_step(x[0], p[0, 0], loss_target[0], small, comm)

    out = {}

    def adam(name, parts8, shape2d):
        w2, m2, v2 = (a.reshape(shape2d) for a in (weights[name], mom_m[name], mom_v[name]))
        res = _adam_call(w2, m2, v2, parts8.reshape(parts8.shape[:1] + shape2d), "adam_" + name)
        out[name] = tuple(r.reshape(weights[name].shape) for r in res)

    adam("w_in", comm.recv_in, (d, d))
    adam("w_branch", comm.recv[0], (256, d))
    adam("w_out", comm.recv[1], (128, d))
    adam("ffn_w_up", comm.recv[2], (d, UP_COLS))
    adam("ffn_w_down", comm.recv[3], (N_TAPS, d))
    adam("ple_w_proj", comm.recv[4], (256, 128))
    adam("ple_w_gate", comm.recv[5], (128, d))

    def adam_small(names, extra_w, extra_m, extra_v, extra_shapes, parts8, rows, label):
        pk = lambda src, extra: _pack([src[n] for n in names] + extra)[0]
        res = _adam_call(pk(weights, extra_w), pk(mom_m, extra_m), pk(mom_v, extra_v), parts8, label)
        shapes = [weights[n].shape for n in names] + extra_shapes
        unpacked = [_unpack(r, shapes, rows) for r in res]
        for j, n in enumerate(names):
            out[n] = tuple(u[j] for u in unpacked)
        return [[u[len(names) + j] for u in unpacked] for j in range(len(extra_shapes))]

    blank = jnp.zeros((8, LANE), F32)
    taps, loss_rows = adam_small(
        _SMALL_EARLY, [_place_taps(ffn_conv_w[0], me, f), blank], [_place_taps(m_ffn_conv_w[0], me, f), blank],
        [_place_taps(v_ffn_conv_w[0], me, f), blank + 1.0], [(3, f), (8, LANE)], comm.small_early, comm.rows_early,
        "adam_small_early")
    adam_small(_SMALL_LATE, [], [], [], [], comm.small_late, comm.rows_late, "adam_small_late")
    out["ffn_conv_w"] = tuple(lax.dynamic_slice_in_dim(u, me * N_TAPS, N_TAPS, axis=1)[None] for u in taps)
    loss = loss_rows[0][0, 0]

    order = ["w_in", "sgu_w_s", "sgu_b_s", "sgu_norm_g", "sgu_norm_b", "hgrn_lb_logits", "hgrn_norm_g", "w_branch", "w_out",
             "ln1_g", "ln1_b", "ffn_w_up", "ffn_conv_w", "ffn_conv_b", "ffn_w_down", "ln2_g", "ln2_b", "ple_w_proj", "ple_w_gate"]
    return (loss, grad_x[None], *[out[n][0] for n in order], *[out[n][1] for n in order],
            *[out[n][2] for n in order], *[out[n][3] for n in order])


def _place_taps(shard, me, f):
    return lax.dynamic_update_slice_in_dim(jnp.zeros((3, f), F32), shard, me * N_TAPS, axis=1)
```

```python
import functools
import math

import jax
import jax.numpy as jnp
from jax import lax
from jax.experimental import pallas as pl
from jax.experimental.pallas import tpu as pltpu

F32 = jnp.float32
BF16 = jnp.bfloat16

N_DEV = 8
N_CHIP = 4
D_MODEL = 1024
CHUNK = 64
SUB = 16
SGU_BLOCK = 128
GROUPS = 8
HEAD = 128
HEADS = 8
HEAD_UNROLL = 4
D_FF = 2816
LN_EPS = 1e-5
RMS_EPS = 1e-6
ALPHA = 2.0 ** 0.25
GELU_K = math.sqrt(2.0 / math.pi)
GELU_C = 0.044715
NEG = -1e30
ADAM_LR, ADAM_B1, ADAM_B2, ADAM_EPS, ADAM_WD, ADAM_STEP = 0.001, 0.9, 0.999, 1e-08, 0.01, 10
LANE = 128
SLOT_Q, SLOT_F, SLOT_I, SLOT_OG, SLOT_U, SLOT_V, SLOT_GA, SLOT_GB = range(8)


def _slot_of_group(k):
    return jnp.where(k < 2, k + 4, jnp.where(k < 6, k - 2, k))


MIB = 1024 * 1024
VMEM_V7X = 64 * MIB
VMEM_FLOOR = 32 * MIB
MM_TILES = (1024, 1408, 512, 256, 128)


def _params(sem, need=0):
    limit = min(max(need + need // 4, VMEM_FLOOR), VMEM_V7X - 4 * MIB)
    return pltpu.CompilerParams(dimension_semantics=sem, vmem_limit_bytes=limit)


def _pick(n, prefs):
    for t in prefs:
        if n % t == 0:
            return t
    return n


def _gelu(x):
    return 0.5 * x * (1.0 + jnp.tanh(GELU_K * (x + GELU_C * x * x * x)))


def _gelu_grad(x):
    t = jnp.tanh(GELU_K * (x + GELU_C * x * x * x))
    return 0.5 * (1.0 + t) + 0.5 * x * (1.0 - t * t) * GELU_K * (1.0 + 3.0 * GELU_C * x * x)


def _silu_grad(x, s):
    return s * (1.0 + x * (1.0 - s))


def _dot(a, b):
    return jnp.dot(a.astype(BF16), b.astype(BF16), preferred_element_type=F32)


def _dot_nt(a, b):
    return lax.dot_general(a.astype(BF16), b.astype(BF16), (((1,), (1,)), ((), ())), preferred_element_type=F32)


def _dot_tn(a, b):
    return lax.dot_general(a.astype(BF16), b.astype(BF16), (((0,), (0,)), ((), ())), preferred_element_type=F32)


def _mean(x):
    return jnp.mean(x, axis=-1, keepdims=True)


def _sum0(x):
    return jnp.sum(x, axis=0, keepdims=True)


def _mm(a, b, *, out_dtype, name, trans_b=False, reduce_b=False, adds=(), ex=None):
    squeeze = b.ndim == 2
    a3 = a if a.ndim == 3 else a[None]
    b3 = b if b.ndim == 3 else b[None]
    ba, m, k = a3.shape
    bb = b3.shape[0]
    n = b3.shape[1] if trans_b else b3.shape[2]
    tm = _pick(m, MM_TILES)
    tn = _pick(n, MM_TILES)
    if reduce_b:
        bo, steps = 1, bb
        a_map = lambda o, i, j, r: (r if ba > 1 else 0, i, 0)
        b_map = (lambda o, i, j, r: (r, j, 0)) if trans_b else (lambda o, i, j, r: (r, 0, j))
    else:
        bo, steps = bb, 1
        a_map = lambda o, i, j, r: (o if ba > 1 else 0, i, 0)
        b_map = (lambda o, i, j, r: (o, j, 0)) if trans_b else (lambda o, i, j, r: (o, 0, j))
    o_map = lambda o, i, j, r: (o, i, j)
    add_arrays = [x if x.ndim == 3 else x[None] for x, _ in adds]
    add_scales = [s for _, s in adds]
    n_add = len(adds)
    dot = _dot_nt if trans_b else _dot

    def finish(acc, add_refs, o_ref):
        for ref, s in zip(add_refs, add_scales):
            acc = acc + s * ref[...].astype(F32)
        o_ref[...] = acc.astype(o_ref.dtype)

    grid = (bo, m // tm, n // tn, steps)

    def body(*refs):
        ins, (o_ref,), scratch, xrefs = _split_refs(refs, 2 + n_add, 1, 1 if reduce_b else 0, ex)
        a_ref, b_ref, add_refs = ins[0], ins[1], ins[2:]
        step = ((pl.program_id(0) * grid[1] + pl.program_id(1)) * grid[2] + pl.program_id(2)) * grid[3] + pl.program_id(3)
        if ex:
            @pl.when(step == 0)
            def _():
                ex.start(*xrefs)

        if reduce_b:
            acc, = scratch
            r = pl.program_id(3)

            @pl.when(r == 0)
            def _():
                acc[...] = jnp.zeros_like(acc)

            acc[...] += dot(a_ref[...], b_ref[...])

            @pl.when(r == steps - 1)
            def _():
                finish(acc[...], add_refs, o_ref)
        else:
            finish(dot(a_ref[...], b_ref[...]), add_refs, o_ref)

        if ex:
            @pl.when(step == math.prod(grid) - 1)
            def _():
                ex.finish(*xrefs)

    b_block = (None, tn, k) if trans_b else (None, k, tn)
    out_bytes = tm * tn * jnp.dtype(out_dtype).itemsize
    need = 2 * (tm * k * a3.dtype.itemsize + k * tn * b3.dtype.itemsize + out_bytes + n_add * tm * tn * 4)
    need += 2 * tm * tn * 4
    sem = ("arbitrary",) * 4 if ex else ("parallel", "parallel", "parallel", "arbitrary")
    res = pl.pallas_call(
        body,
        grid=grid,
        in_specs=[pl.BlockSpec((None, tm, k), a_map), pl.BlockSpec(b_block, b_map)]
        + [pl.BlockSpec((None, tm, tn), o_map) for _ in adds] + (ex.in_specs if ex else []),
        out_specs=[pl.BlockSpec((None, tm, tn), o_map)] + (ex.out_specs if ex else []),
        out_shape=[jax.ShapeDtypeStruct((bo, m, n), out_dtype)] + (ex.out_shape if ex else []),
        scratch_shapes=([pltpu.VMEM((tm, tn), F32)] if reduce_b else []) + (ex.scratch if ex else []),
        compiler_params=_params(sem, need),
        name=name,
    )(a3, b3, *add_arrays, *(ex.arrays if ex else []))
    out = res[0][0] if (reduce_b or squeeze) else res[0]
    return (out, res[1:]) if ex else out


def _mm_tn(a, b, *, out_dtype, name):
    squeeze = b.ndim == 2
    b3 = b if b.ndim == 3 else b[None]
    t, m = a.shape
    bb, _, n = b3.shape
    tm = _pick(m, MM_TILES)
    tn = _pick(n, MM_TILES)
    tt = _pick(t, (1024, 512, 256, 128))
    steps = t // tt
    need = 2 * (tt * tm * a.dtype.itemsize + tt * tn * b3.dtype.itemsize + tm * tn * jnp.dtype(out_dtype).itemsize)
    need += 2 * tm * tn * 4

    def body(a_ref, b_ref, o_ref, acc):
        r = pl.program_id(3)

        @pl.when(r == 0)
        def _():
            acc[...] = jnp.zeros_like(acc)

        acc[...] += _dot_tn(a_ref[...], b_ref[...])

        @pl.when(r == steps - 1)
        def _():
            o_ref[...] = acc[...].astype(o_ref.dtype)

    out = pl.pallas_call(
        body,
        grid=(bb, m // tm, n // tn, steps),
        in_specs=[pl.BlockSpec((tt, tm), lambda o, i, j, r: (r, i)),
                  pl.BlockSpec((None, tt, tn), lambda o, i, j, r: (o, r, j))],
        out_specs=pl.BlockSpec((None, tm, tn), lambda o, i, j, r: (o, i, j)),
        out_shape=jax.ShapeDtypeStruct((bb, m, n), out_dtype),
        scratch_shapes=[pltpu.VMEM((tm, tn), F32)],
        compiler_params=_params(("parallel", "parallel", "parallel", "arbitrary"), need),
        name=name,
    )(a, b3)
    return out[0] if squeeze else out


def _rowwise(fn, rows, consts, row_outs, acc_outs, *, tile, name):
    first = rows[0][0] if isinstance(rows[0], tuple) else rows[0]
    t = first.shape[-2]
    steps = t // tile
    arrays, in_specs = [], []
    for r in rows:
        if isinstance(r, tuple) and isinstance(r[1], pl.BlockSpec):
            arrays.append(r[0])
            in_specs.append(r[1])
        elif isinstance(r, tuple):
            arr, bidx = r
            arrays.append(arr)
            in_specs.append(pl.BlockSpec((None, tile, arr.shape[-1]), functools.partial(lambda i, b: (b, i, 0), b=bidx)))
        else:
            arrays.append(r)
            in_specs.append(pl.BlockSpec((tile, r.shape[-1]), lambda i: (i, 0)))
    for c in consts:
        arrays.append(c)
        in_specs.append(pl.BlockSpec(c.shape, lambda i: (0, 0)))
    n_in, n_row = len(arrays), len(row_outs)
    out_shape, out_specs = [], []
    for ro in row_outs:
        if ro[0] == "stack":
            _, cnt, blk, total, w, dt = ro
            out_shape.append(jax.ShapeDtypeStruct((total, t, w), dt))
            out_specs.append(pl.BlockSpec((cnt, tile, w), functools.partial(lambda i, b: (b, i, 0), b=blk)))
        else:
            w, dt = ro
            out_shape.append(jax.ShapeDtypeStruct((t, w), dt))
            out_specs.append(pl.BlockSpec((tile, w), lambda i: (i, 0)))
    out_shape += [jax.ShapeDtypeStruct(s, F32) for s in acc_outs]
    out_specs += [pl.BlockSpec(s, lambda i: (0, 0)) for s in acc_outs]
    blocks = [math.prod(d for d in sp.block_shape if d) * arr.dtype.itemsize for sp, arr in zip(in_specs, arrays)]
    blocks += [math.prod(d for d in sp.block_shape if d) * jnp.dtype(sh.dtype).itemsize
               for sp, sh in zip(out_specs, out_shape)]
    need = 2 * sum(blocks) + 6 * tile * max(a.shape[-1] for a in arrays) * 4

    def body(*refs):
        ins, outs = refs[:n_in], refs[n_in:]
        i = pl.program_id(0)
        res = fn(i, steps, *[r[...] for r in ins])
        res = res if isinstance(res, (tuple, list)) else (res,)
        for ref, val in zip(outs[:n_row], res[:n_row]):
            ref[...] = val.astype(ref.dtype)
        if acc_outs:
            @pl.when(i == 0)
            def _():
                for ref in outs[n_row:]:
                    ref[...] = jnp.zeros_like(ref)

            for ref, val in zip(outs[n_row:], res[n_row:]):
                ref[...] += val

    return pl.pallas_call(
        body,
        grid=(steps,),
        in_specs=in_specs,
        out_specs=out_specs,
        out_shape=out_shape,
        compiler_params=_params(("arbitrary",), need),
        name=name,
    )(*arrays)


def _ln_stats(z):
    mu = _mean(z)
    zc = z - mu
    rstd = lax.rsqrt(_mean(zc * zc) + LN_EPS)
    return zc * rstd, rstd


def _ln_bwd(dy, xhat, rstd, g):
    dxh = dy * g
    return rstd * (dxh - _mean(dxh) - xhat * _mean(dxh * xhat))


def _ride(ex, xrefs, step, steps):
    if not ex:
        return
    for at, act in ((0, ex.start), ((3 * steps) // 4, ex.relay), (steps - 1, ex.finish)):
        @pl.when(step == at)
        def _(act=act):
            act(*xrefs)


def _sgu_fwd(h, wm, bs_t, g_v, b_v, ex=None):
    t = h.shape[1]
    steps = t // SGU_BLOCK

    def body(*refs):
        (u_ref, v_ref, wm_ref, bs_ref, g_ref, b_ref), (y_ref,), _, xrefs = _split_refs(refs, 6, 1, 0, ex)
        xhat, _ = _ln_stats(_gelu(v_ref[...]))
        vn = (xhat * g_ref[...] + b_ref[...]).astype(BF16)
        gu = _gelu(u_ref[...])
        for g in range(GROUPS):
            sl = slice(g * HEAD, (g + 1) * HEAD)
            mixed = _dot(wm_ref[g], vn[:, sl]) + bs_ref[:, g:g + 1]
            y_ref[:, sl] = (gu[:, sl] * mixed).astype(BF16)
        _ride(ex, xrefs, pl.program_id(0), steps)

    blk = lambda b: pl.BlockSpec((None, SGU_BLOCK, D_MODEL), functools.partial(lambda i, b: (b, i, 0), b=b))
    whole = lambda s: pl.BlockSpec(s, lambda i: (0,) * len(s))
    res = pl.pallas_call(
        body,
        grid=(steps,),
        in_specs=[blk(SLOT_U), blk(SLOT_V), whole(wm.shape), whole(bs_t.shape), whole(g_v.shape), whole(b_v.shape)]
        + (ex.in_specs if ex else []),
        out_specs=[pl.BlockSpec((SGU_BLOCK, D_MODEL), lambda i: (i, 0))] + (ex.out_specs if ex else []),
        out_shape=[jax.ShapeDtypeStruct((t, D_MODEL), BF16)] + (ex.out_shape if ex else []),
        scratch_shapes=ex.scratch if ex else [],
        compiler_params=_params(("arbitrary",)),
        name="sgu_fwd",
    )(h, h, wm, bs_t, g_v, b_v, *(ex.arrays if ex else []))
    return res[0], res[1:]


def _sgu_bwd(h, dy, dh, wm, wm_t, bs_t, g_v, b_v):
    t = h.shape[1]

    def body(u_ref, v_ref, dy_ref, dh_in, wm_ref, wmt_ref, bs_ref, g_ref, b_ref,
             duv_ref, dw_ref, dbs_ref, dg_ref, db_ref, dvn_ref):
        del dh_in
        du_ref, dv_ref = duv_ref.at[0], duv_ref.at[1]
        i = pl.program_id(0)

        @pl.when(i == 0)
        def _():
            dw_ref[...] = jnp.zeros_like(dw_ref)
            dbs_ref[...] = jnp.zeros_like(dbs_ref)
            dg_ref[...] = jnp.zeros_like(dg_ref)
            db_ref[...] = jnp.zeros_like(db_ref)

        u = u_ref[...]
        v = v_ref[...]
        xhat, rstd = _ln_stats(_gelu(v))
        vn = (xhat * g_ref[...] + b_ref[...]).astype(BF16)
        gu = _gelu(u)
        gup = _gelu_grad(u)
        lane = lax.broadcasted_iota(jnp.int32, (SGU_BLOCK, LANE), 1)
        dbs = jnp.zeros((SGU_BLOCK, LANE), F32)
        for g in range(GROUPS):
            sl = slice(g * HEAD, (g + 1) * HEAD)
            vn_g = vn[:, sl]
            mixed = _dot(wm_ref[g], vn_g) + bs_ref[:, g:g + 1]
            dy_g = dy_ref[:, sl]
            du_ref[:, sl] = (dy_g * mixed * gup[:, sl]).astype(BF16)
            dmix = dy_g * gu[:, sl]
            dmb = dmix.astype(BF16)
            dvn_ref[:, sl] = _dot(wmt_ref[g], dmb)
            dw_ref[g] += _dot_nt(dmb, vn_g)
            dbs = dbs + jnp.where(lane == g, jnp.sum(dmix, axis=1, keepdims=True), 0.0)
        dbs_ref[...] += dbs
        dvn = dvn_ref[...]
        dg_ref[...] += _sum0(dvn * xhat)
        db_ref[...] += _sum0(dvn)
        dv_ref[...] = (_ln_bwd(dvn, xhat, rstd, g_ref[...]) * _gelu_grad(v)).astype(BF16)

    blk = lambda b: pl.BlockSpec((None, SGU_BLOCK, D_MODEL), functools.partial(lambda i, b: (b, i, 0), b=b))
    row = pl.BlockSpec((SGU_BLOCK, D_MODEL), lambda i: (i, 0))
    whole = lambda s: pl.BlockSpec(s, lambda i: (0,) * len(s))
    vec = (1, D_MODEL)
    return pl.pallas_call(
        body,
        grid=(t // SGU_BLOCK,),
        in_specs=[blk(SLOT_U), blk(SLOT_V), row, pl.BlockSpec(memory_space=pl.ANY),
                  whole(wm.shape), whole(wm_t.shape), whole(bs_t.shape), whole(vec), whole(vec)],
        out_specs=[pl.BlockSpec((2, SGU_BLOCK, D_MODEL), lambda i: (SLOT_U // 2, i, 0)),
                   whole(wm.shape), whole((SGU_BLOCK, LANE)), whole(vec), whole(vec)],
        out_shape=[jax.ShapeDtypeStruct(dh.shape, BF16),
                   jax.ShapeDtypeStruct(wm.shape, F32), jax.ShapeDtypeStruct((SGU_BLOCK, LANE), F32),
                   jax.ShapeDtypeStruct(vec, F32), jax.ShapeDtypeStruct(vec, F32)],
        scratch_shapes=[pltpu.VMEM((SGU_BLOCK, D_MODEL), F32)],
        input_output_aliases={3: 0},
        compiler_params=_params(("arbitrary",)),
        name="sgu_bwd",
    )(h, h, dy, dh, wm, wm_t, bs_t, g_v, b_v)


def _split3(x):
    hi = x.astype(BF16)
    r1 = x - hi.astype(F32)
    mid = r1.astype(BF16)
    lo = (r1 - mid.astype(F32)).astype(BF16)
    return hi, mid, lo


def _tri_matmul(tri, x):
    hi, mid, lo = _split3(x)
    dot = lambda p: jnp.dot(tri, p, preferred_element_type=F32)
    return dot(hi) + dot(mid) + dot(lo)


def _lower_bound(logits):
    l0, l1 = logits[0:1, :], logits[1:2, :]
    mx = jnp.maximum(l0, l1)
    e0, e1 = jnp.exp(l0 - mx), jnp.exp(l1 - mx)
    return e0 / (e0 + e1)


def _hgrn_gates(q_raw, f_raw, lb):
    q = q_raw * jax.nn.sigmoid(q_raw)
    sig = jax.nn.sigmoid(f_raw)
    f = lb + (1.0 - lb) * sig
    row = lax.broadcasted_iota(jnp.int32, (CHUNK, CHUNK), 0)
    col = lax.broadcasted_iota(jnp.int32, (CHUNK, CHUNK), 1)
    c = _tri_matmul((row >= col).astype(BF16), jnp.log(f))
    return q, sig, f, 1.0 - f, c


def _offdiag_terms(qh, kh, ch, tb):
    rows = slice(tb * SUB, (tb + 1) * SUB)
    r = ch[tb * SUB - 1:tb * SUB, :]
    eqh = jnp.exp(ch[rows] - r)
    ekh = jnp.exp(jnp.minimum(r - ch, 0.0))
    return rows, eqh, qh[rows] * eqh, ekh, kh * ekh


def _diag_decay(cb, s, trow):
    return jnp.exp(jnp.where(trow >= s, cb - cb[s:s + 1, :], NEG))


def _split_refs(refs, n_in, n_out, n_scratch, ex):
    nx = ex.n if ex else 0
    ins, refs = refs[:n_in], refs[n_in:]
    xsrc, refs = refs[:nx], refs[nx:]
    outs, refs = refs[:n_out], refs[n_out:]
    xout, refs = refs[:nx], refs[nx:]
    return ins, outs, refs[:n_scratch], (xsrc, xout, refs[n_scratch:])


def _hgrn_fwd(h, logits, g_norm, ex=None):
    t = h.shape[1]
    nc = t // CHUNK

    def body(*refs):
        ins, outs, scratch, xrefs = _split_refs(refs, 6, 3, 4, ex)
        q_ref, f_ref, i_ref, og_ref, lg_ref, gn_ref = ins
        y_ref, o_ref, sall_ref = outs
        st_ref, q_s, k_s, c_s = scratch

        @pl.when(pl.program_id(0) == 0)
        def _():
            st_ref[...] = jnp.zeros_like(st_ref)
            if ex:
                ex.start(*xrefs)

        if ex:
            @pl.when(pl.program_id(0) == (3 * nc) // 4)
            def _():
                ex.relay(*xrefs)

        lb = _lower_bound(lg_ref[...])
        q, _, _, k, c = _hgrn_gates(q_ref[...], f_ref[...], lb)
        q_s[...] = q
        k_s[...] = k
        c_s[...] = c
        col64 = lax.broadcasted_iota(jnp.int32, (SUB, CHUNK), 1)
        trow = lax.broadcasted_iota(jnp.int32, (SUB, HEAD), 0)

        def head(hd, carry):
            sl = pl.ds(pl.multiple_of(hd * HEAD, HEAD), HEAD)
            qh, kh, ch, ih = q_s[:, sl], k_s[:, sl], c_s[:, sl], i_ref[:, sl]
            st = st_ref[hd]
            sall_ref[hd] = st
            c_last = ch[CHUNK - 1:CHUNK, :]
            o = _dot_nt(qh * jnp.exp(ch), st)
            st_ref[hd] = st * jnp.exp(c_last) + _dot_tn(ih, kh * jnp.exp(c_last - ch))
            a_rows = [jnp.zeros((SUB, CHUNK), F32)]
            for tb in range(1, CHUNK // SUB):
                _, _, q_hat, _, k_hat = _offdiag_terms(qh, kh, ch, tb)
                a_rows.append(jnp.where(col64 < tb * SUB, _dot_nt(q_hat, k_hat), 0.0))
            o = o + _dot(jnp.concatenate(a_rows, axis=0), ih)
            o_rows = []
            for b in range(CHUNK // SUB):
                rows = slice(b * SUB, (b + 1) * SUB)
                qb, cb, kb, ib = qh[rows], ch[rows], kh[rows], ih[rows]
                ob = jnp.zeros((SUB, HEAD), F32)
                for s in range(SUB):
                    a = jnp.sum(qb * _diag_decay(cb, s, trow) * kb[s:s + 1, :], axis=1, keepdims=True)
                    ob = ob + a * ib[s:s + 1, :]
                o_rows.append(ob)
            o = o + jnp.concatenate(o_rows, axis=0)
            o_ref[:, sl] = o
            og = og_ref[:, sl]
            on = o * lax.rsqrt(_mean(o * o) + RMS_EPS)
            y_ref[:, sl] = (on * gn_ref[:, sl] * (og * jax.nn.sigmoid(og))).astype(BF16)
            return carry

        lax.fori_loop(0, HEADS, head, 0, unroll=HEAD_UNROLL)

        if ex:
            @pl.when(pl.program_id(0) == nc - 1)
            def _():
                ex.finish(*xrefs)

    blk = lambda b: pl.BlockSpec((None, CHUNK, D_MODEL), functools.partial(lambda n, b: (b, n, 0), b=b))
    row = pl.BlockSpec((CHUNK, D_MODEL), lambda n: (n, 0))
    whole = lambda s: pl.BlockSpec(s, lambda n: (0,) * len(s))
    res = pl.pallas_call(
        body,
        grid=(nc,),
        in_specs=[blk(SLOT_Q), blk(SLOT_F), blk(SLOT_I), blk(SLOT_OG), whole(logits.shape), whole(g_norm.shape)]
        + (ex.in_specs if ex else []),
        out_specs=[row, row, pl.BlockSpec((None, HEADS, HEAD, HEAD), lambda n: (n, 0, 0, 0))] + (ex.out_specs if ex else []),
        out_shape=[jax.ShapeDtypeStruct((t, D_MODEL), BF16), jax.ShapeDtypeStruct((t, D_MODEL), F32),
                   jax.ShapeDtypeStruct((nc, HEADS, HEAD, HEAD), F32)] + (ex.out_shape if ex else []),
        scratch_shapes=[pltpu.VMEM((HEADS, HEAD, HEAD), F32)] + [pltpu.VMEM((CHUNK, D_MODEL), F32)] * 3
        + (ex.scratch if ex else []),
        compiler_params=_params(("arbitrary",)),
        name="hgrn_fwd",
    )(h, h, h, h, logits, g_norm, *(ex.arrays if ex else []))
    return res[0], res[1], res[2], res[3:]


def _hgrn_bwd(h, o_all, dy, states, dh, logits, g_norm, ex=None):
    t = h.shape[1]
    nc = t // CHUNK

    def body(*refs):
        ins, outs, scratch, xrefs = _split_refs(refs, 10, 3, 8, ex)
        q_ref, f_ref, i_ref, og_ref, o_ref, dy_ref, sall_ref, _, lg_ref, gn_ref = ins
        dqfio_ref, dlg_ref, dgn_ref = outs
        dst_ref, dlb_ref, q_s, k_s, c_s, dq_s, dk_s, dc_s = scratch
        dq_ref, df_ref, di_ref, dog_ref = (dqfio_ref.at[s] for s in (SLOT_Q, SLOT_F, SLOT_I, SLOT_OG))
        n = pl.program_id(0)

        @pl.when(n == 0)
        def _():
            dst_ref[...] = jnp.zeros_like(dst_ref)
            dlb_ref[...] = jnp.zeros_like(dlb_ref)
            dgn_ref[...] = jnp.zeros_like(dgn_ref)
            if ex:
                ex.start(*xrefs)

        lb = _lower_bound(lg_ref[...])
        q_raw = q_ref[...]
        q, sig, f, k, c = _hgrn_gates(q_raw, f_ref[...], lb)
        q_s[...] = q
        k_s[...] = k
        c_s[...] = c
        col64 = lax.broadcasted_iota(jnp.int32, (SUB, CHUNK), 1)
        trow = lax.broadcasted_iota(jnp.int32, (SUB, HEAD), 0)
        row64 = lax.broadcasted_iota(jnp.int32, (CHUNK, HEAD), 0)

        def head(hd, carry):
            sl = pl.ds(pl.multiple_of(hd * HEAD, HEAD), HEAD)
            qh, kh, ch, ih = q_s[:, sl], k_s[:, sl], c_s[:, sl], i_ref[:, sl]
            st = sall_ref[hd]
            dst = dst_ref[hd]
            oh, dyh, og, gn = o_ref[:, sl], dy_ref[:, sl], og_ref[:, sl], gn_ref[:, sl]
            sg = jax.nn.sigmoid(og)
            sil = og * sg
            rms = lax.rsqrt(_mean(oh * oh) + RMS_EPS)
            on = oh * rms
            dog_ref[:, sl] = (dyh * on * gn * _silu_grad(og, sg)).astype(BF16)
            dgn_ref[:, sl] += _sum0(dyh * on * sil)
            don = dyh * gn * sil
            do = rms * (don - on * _mean(don * on))
            dob = do.astype(BF16)

            c_last = ch[CHUNK - 1:CHUNK, :]
            eq = jnp.exp(ch)
            q_til = qh * eq
            ekl = jnp.exp(c_last - ch)
            k_til = kh * ekl
            ecl = jnp.exp(c_last)
            dq_til = _dot(dob, st)
            dk_til = _dot(ih, dst)
            di = _dot_nt(k_til, dst)
            dc_last = _sum0(dk_til * k_til) + _sum0(dst * st) * ecl
            dst_ref[hd] = _dot_tn(dob, q_til) + dst * ecl
            dq = dq_til * eq
            dc = dq_til * q_til - dk_til * k_til
            dk = dk_til * ekl

            da_full = _dot_nt(dob, ih)
            a_rows = [jnp.zeros((SUB, CHUNK), F32)]
            dq_rows = [jnp.zeros((SUB, HEAD), F32)]
            dc_rows = [jnp.zeros((SUB, HEAD), F32)]
            for tb in range(1, CHUNK // SUB):
                rows, eqh, q_hat, ekh, k_hat = _offdiag_terms(qh, kh, ch, tb)
                keep = col64 < tb * SUB
                a_rows.append(jnp.where(keep, _dot_nt(q_hat, k_hat), 0.0))
                da = jnp.where(keep, da_full[rows], 0.0)
                dq_hat = _dot(da, k_hat)
                dk_hat = _dot_tn(da, q_hat)
                dq_rows.append(dq_hat * eqh)
                dc_rows.append(dq_hat * q_hat)
                dk = dk + dk_hat * ekh
                dc = dc - dk_hat * k_hat
            di = di + _dot_tn(jnp.concatenate(a_rows, axis=0), dob)

            dk_rows, di_rows = [], []
            for b in range(CHUNK // SUB):
                rows = slice(b * SUB, (b + 1) * SUB)
                qb, cb, kb, ib, dob_ = qh[rows], ch[rows], kh[rows], ih[rows], do[rows]
                dq_diag = jnp.zeros((SUB, HEAD), F32)
                dk_diag = jnp.zeros((SUB, HEAD), F32)
                di_diag = jnp.zeros((SUB, HEAD), F32)
                for s in range(SUB):
                    ks = kb[s:s + 1, :]
                    dec = _diag_decay(cb, s, trow)
                    a = jnp.sum(qb * dec * ks, axis=1, keepdims=True)
                    gk = jnp.sum(dob_ * ib[s:s + 1, :], axis=1, keepdims=True) * dec
                    dq_diag = dq_diag + gk * ks
                    dk_diag = dk_diag + jnp.where(trow == s, _sum0(gk * qb), 0.0)
                    di_diag = di_diag + jnp.where(trow == s, _sum0(a * dob_), 0.0)
                dq_rows[b] = dq_rows[b] + dq_diag
                dc_rows[b] = dc_rows[b] + qb * dq_diag - kb * dk_diag
                dk_rows.append(dk_diag)
                di_rows.append(di_diag)
            dq = dq + jnp.concatenate(dq_rows, axis=0)
            dk = dk + jnp.concatenate(dk_rows, axis=0)
            dc = dc + jnp.concatenate(dc_rows, axis=0) + jnp.where(row64 == CHUNK - 1, dc_last, 0.0)
            di_ref[:, sl] = (di + jnp.concatenate(di_rows, axis=0)).astype(BF16)
            dq_s[:, sl] = dq
            dk_s[:, sl] = dk
            dc_s[:, sl] = dc
            return carry

        lax.fori_loop(0, HEADS, head, 0, unroll=HEAD_UNROLL)

        row = lax.broadcasted_iota(jnp.int32, (CHUNK, CHUNK), 0)
        col = lax.broadcasted_iota(jnp.int32, (CHUNK, CHUNK), 1)
        dlf = _tri_matmul((row <= col).astype(BF16), dc_s[...])
        df = dlf / f - dk_s[...]
        dlb_ref[...] += _sum0(df * (1.0 - sig))
        df_ref[...] = (df * (1.0 - lb) * sig * (1.0 - sig)).astype(BF16)
        dq_ref[...] = (dq_s[...] * _silu_grad(q_raw, jax.nn.sigmoid(q_raw))).astype(BF16)

        @pl.when(n == nc - 1)
        def _():
            d0 = dlb_ref[...] * lb * (1.0 - lb)
            dlg_ref[0:1, :] = d0
            dlg_ref[1:2, :] = -d0
            if ex:
                ex.finish(*xrefs)

    rev = lambda n: nc - 1 - n
    blk = lambda b: pl.BlockSpec((None, CHUNK, D_MODEL), functools.partial(lambda n, b: (b, rev(n), 0), b=b))
    row = pl.BlockSpec((CHUNK, D_MODEL), lambda n: (rev(n), 0))
    whole = lambda s: pl.BlockSpec(s, lambda n: (0,) * len(s))
    vec = (1, D_MODEL)
    res = pl.pallas_call(
        body,
        grid=(nc,),
        in_specs=[blk(SLOT_Q), blk(SLOT_F), blk(SLOT_I), blk(SLOT_OG), row, row,
                  pl.BlockSpec((None, HEADS, HEAD, HEAD), lambda n: (rev(n), 0, 0, 0)),
                  pl.BlockSpec(memory_space=pl.ANY), whole(logits.shape), whole(vec)] + (ex.in_specs if ex else []),
        out_specs=[pl.BlockSpec((4, CHUNK, D_MODEL), lambda n: (SLOT_Q // 4, rev(n), 0)), whole((2, D_MODEL)), whole(vec)]
        + (ex.out_specs if ex else []),
        out_shape=[jax.ShapeDtypeStruct(dh.shape, BF16), jax.ShapeDtypeStruct((2, D_MODEL), F32),
                   jax.ShapeDtypeStruct(vec, F32)] + (ex.out_shape if ex else []),
        scratch_shapes=[pltpu.VMEM((HEADS, HEAD, HEAD), F32), pltpu.VMEM(vec, F32)]
        + [pltpu.VMEM((CHUNK, D_MODEL), F32)] * 6 + (ex.scratch if ex else []),
        input_output_aliases={7: 0},
        compiler_params=_params(("arbitrary",)),
        name="hgrn_bwd",
    )(h, h, h, h, o_all, dy, states, dh, logits, g_norm, *(ex.arrays if ex else []))
    return res[0], res[1], res[2], res[3:]


def _merge_fwd(i, n, ga, gb, za, zb):
    return jax.nn.sigmoid(ga) * za + jax.nn.sigmoid(gb) * zb


def _merge_bwd(i, n, ga, gb, za, zb, dm):
    sa, sb = jax.nn.sigmoid(ga), jax.nn.sigmoid(gb)
    dgates = jnp.stack([(dm * za * sa * (1.0 - sa)).astype(BF16), (dm * zb * sb * (1.0 - sb)).astype(BF16)])
    return dgates, dm * sa, dm * sb


def _ln1_fwd(i, n, x, r1, g, b):
    xhat, _ = _ln_stats(ALPHA * x + r1)
    x1 = xhat * g + b
    return x1, x1


def _ln1_bwd(i, n, x, r1, dx1, g):
    xhat, rstd = _ln_stats(ALPHA * x + r1)
    dz = _ln_bwd(dx1, xhat, rstd, g)
    return dz, dz, _sum0(dx1 * xhat), _sum0(dx1)


def _ln2_loss(i, n, x1, fo, pg, pp, tgt, g, b):
    sg = jax.nn.sigmoid(pg)
    xhat, rstd = _ln_stats(ALPHA * x1 + fo + sg * pp)
    diff = xhat * g + b - tgt
    loss = 0.5 * jnp.sum(_mean(diff * diff), axis=0, keepdims=True)
    dy = diff * (1.0 / D_MODEL)
    dz = _ln_bwd(dy, xhat, rstd, g)
    return (dz, dz, dz * pp * sg * (1.0 - sg), dz * sg,
            jnp.broadcast_to(loss, (8, LANE)), _sum0(dy * xhat), _sum0(dy))


def _shift_down(cur, halo, tile):
    row = lax.broadcasted_iota(jnp.int32, cur.shape, 0)
    m1 = jnp.where(row == 0, halo[7:8, :], pltpu.roll(cur, 1, 0))
    m2 = jnp.where(row == 0, halo[6:7, :], jnp.where(row == 1, halo[7:8, :], pltpu.roll(cur, 2, 0)))
    return m1, m2


def _shift_up(cur, halo, tile):
    row = lax.broadcasted_iota(jnp.int32, cur.shape, 0)
    p1 = jnp.where(row == tile - 1, halo[0:1, :], pltpu.roll(cur, tile - 1, 0))
    p2 = jnp.where(row == tile - 2, halo[0:1, :], jnp.where(row == tile - 1, halo[1:2, :], pltpu.roll(cur, tile - 2, 0)))
    return p1, p2


def _conv_pre(i, gate, halo, w, b, tile):
    halo = jnp.where(i == 0, 0.0, halo)
    m1, m2 = _shift_down(gate, halo, tile)
    return w[0:1, :] * m2 + w[1:2, :] * m1 + w[2:3, :] * gate + b, m1, m2


def _conv_fwd(tile, i, n, gate, halo, val, w, b):
    cg, _, _ = _conv_pre(i, gate, halo, w, b, tile)
    return _gelu(cg) * val


def _conv_bwd_a(tile, i, n, gate, halo, val, dhid, w, b):
    cg, m1, m2 = _conv_pre(i, gate, halo, w, b, tile)
    dcg = dhid * val * _gelu_grad(cg)
    return dcg, dhid * _gelu(cg), _sum0(dcg * m2), _sum0(dcg * m1), _sum0(dcg * gate), _sum0(dcg)


def _conv_bwd_b(tile, i, n, dcg, halo, w):
    halo = jnp.where(i == n - 1, 0.0, halo)
    p1, p2 = _shift_up(dcg, halo, tile)
    return w[2:3, :] * dcg + w[1:2, :] * p1 + w[0:1, :] * p2


def _halo_spec(width, tile, t, nxt):
    per = tile // 8
    last = t // 8 - 1
    if nxt:
        return pl.BlockSpec((8, width), lambda i: (jnp.minimum((i + 1) * per, last), 0))
    return pl.BlockSpec((8, width), lambda i: (jnp.maximum(i * per - 1, 0), 0))


def _adamw(i, n, w, m, v, parts):
    g = parts[0].astype(F32)
    for j in range(1, parts.shape[0]):
        g = g + parts[j].astype(F32)
    m_new = ADAM_B1 * m + (1.0 - ADAM_B1) * g
    v_new = ADAM_B2 * v + (1.0 - ADAM_B2) * (g * g)
    m_hat = m_new / (1.0 - ADAM_B1 ** ADAM_STEP)
    v_hat = v_new / (1.0 - ADAM_B2 ** ADAM_STEP)
    delta = -ADAM_LR * (m_hat / (jnp.sqrt(v_hat) + ADAM_EPS) + ADAM_WD * w)
    return g, delta, m_new, v_new


def _adam_call(w, m, v, parts, name):
    r, c = w.shape
    tile = _pick(r, (256, 128)) if r > 256 else r
    spec = pl.BlockSpec((parts.shape[0], tile, c), lambda i: (0, i, 0))
    return _rowwise(_adamw, [w, m, v, (parts, spec)], [], [(c, F32)] * 4, [], tile=tile, name=name)


def _peer(k):
    x, y, c = lax.axis_index("x"), lax.axis_index("y"), lax.axis_index("c")
    px = x ^ ((k >> 2) & 1)
    py = y ^ ((k >> 1) & 1)
    pc = c ^ (k & 1)
    return (px, py, pc), 4 * px + 2 * py + pc


def _my_index():
    return 4 * lax.axis_index("x") + 2 * lax.axis_index("y") + lax.axis_index("c")


class _Exchange:
    KINDS = ("gather", "gather+relay", "scatter", "scatter+pairs")

    def __init__(self, entries):
        assert all(k in self.KINDS for _, k in entries), [k for _, k in entries]
        self.arrays = [a for a, _ in entries]
        self.scatter = [k.startswith("scatter") for _, k in entries]
        self.relayed = ["+relay" in k for _, k in entries]
        self.pairs = ["+pairs" in k for _, k in entries]
        self.n = len(entries)
        self.in_specs = [pl.BlockSpec(memory_space=pl.ANY)] * self.n
        self.out_specs = [pl.BlockSpec(memory_space=pl.ANY)] * self.n
        shapes = [tuple(a.shape[1:]) if sc else tuple(a.shape) for a, sc in zip(self.arrays, self.scatter)]
        counts = [N_CHIP if p else N_DEV for p in self.pairs]
        self.out_shape = [jax.ShapeDtypeStruct((n,) + s, a.dtype) for n, s, a in zip(counts, shapes, self.arrays)]
        per = N_DEV - 1
        self.scratch = [pltpu.SemaphoreType.DMA((self.n * per,)), pltpu.SemaphoreType.DMA((self.n * per,)),
                        pltpu.SemaphoreType.DMA((self.n,))]

    def _copies(self, srcs, outs, sems):
        send_sems, recv_sems, local_sems = sems
        x, y, c = lax.axis_index("x"), lax.axis_index("y"), lax.axis_index("c")
        me = _my_index()
        per = N_DEV - 1
        local, first, passed, relay_arrivals, arrivals = [], [], [], [], []
        for a in range(self.n):

            def copy(k, src, dst, dev, a=a):
                return pltpu.make_async_remote_copy(
                    src_ref=src, dst_ref=dst, send_sem=send_sems.at[a * per + k], recv_sem=recv_sems.at[a * per + k],
                    device_id=dev, device_id_type=pl.DeviceIdType.MESH)

            if self.pairs[a]:
                chip = 2 * x + y
                for k in range(N_CHIP):
                    to = chip ^ k
                    piece = srcs[a].at[_slot_of_group(2 * to + c) // 2]
                    if k == 0:
                        local.append(pltpu.make_async_copy(piece, outs[a].at[chip], local_sems.at[a]))
                    else:
                        dev = (to // 2, to % 2, c)
                        first.append(copy(k - 1, piece, outs[a].at[chip], dev))
                        arrivals.append(copy(k - 1, piece, outs[a].at[to], dev))
                continue
            mine = srcs[a].at[me] if self.scatter[a] else srcs[a]
            land = outs[a].at[me]
            local.append(pltpu.make_async_copy(mine, land, local_sems.at[a]))
            if self.relayed[a]:
                block = lambda px, py, pc, a=a: outs[a].at[4 * px + 2 * py + pc]
                chips = [(1 - x, y), (x, 1 - y), (1 - x, 1 - y)]
                first.append(copy(0, mine, land, (x, y, 1 - c)))
                arrivals.append(copy(0, mine, block(x, y, 1 - c), (x, y, 1 - c)))
                for j, (px, py) in enumerate(chips):
                    first.append(copy(1 + j, mine, land, (px, py, c)))
                    relay_arrivals.append(copy(1 + j, mine, block(px, py, c), (px, py, c)))
                    passed.append(copy(4 + j, block(px, py, c), block(px, py, c), (x, y, 1 - c)))
                    arrivals.append(copy(4 + j, mine, block(px, py, 1 - c), (x, y, 1 - c)))
                continue
            for k in range(1, N_DEV):
                dev, idx = _peer(k)
                if self.scatter[a]:
                    first.append(copy(k - 1, srcs[a].at[idx], land, dev))
                else:
                    first.append(copy(k - 1, mine, land, dev))
                arrivals.append(copy(k - 1, mine, outs[a].at[idx], dev))
        return local, first, passed, relay_arrivals, arrivals

    def start(self, srcs, outs, sems):
        local, first, _, _, _ = self._copies(srcs, outs, sems)
        for cp in local + first:
            cp.start()

    def relay(self, srcs, outs, sems):
        _, _, passed, relay_arrivals, _ = self._copies(srcs, outs, sems)
        for landed, onward in zip(relay_arrivals, passed):
            landed.wait_recv()
            onward.start()

    def finish(self, srcs, outs, sems):
        local, first, passed, _, arrivals = self._copies(srcs, outs, sems)
        for cp in arrivals:
            cp.wait_recv()
        for cp in first + passed:
            cp.wait_send()
        for cp in local:
            cp.wait()


def _gather_project(x_b, shard):
    t, d = x_b.shape
    tm = _pick(t, MM_TILES)
    nrow = t // tm
    per = N_DEV - 1

    def parties():
        x, y, c = lax.axis_index("x"), lax.axis_index("y"), lax.axis_index("c")
        chips = [(1 - x, y), (x, 1 - y), (1 - x, 1 - y)]
        return (x, y, c), (x, y, 1 - c), [(px, py, c) for px, py in chips], [(px, py, 1 - c) for px, py in chips]

    slot = lambda dev: _slot_of_group(4 * dev[0] + 2 * dev[1] + dev[2])
    ici_step, passed_step = (2, 3, 6), (4, 5, 7)
    me, sibling, over_ici, passed_on = parties()
    by_step = {0: me, 1: sibling, **dict(zip(ici_step, over_ici)), **dict(zip(passed_step, passed_on))}
    order = jnp.stack([slot(by_step[j]) for j in range(N_DEV)]).astype(jnp.int32)

    def body(order_ref, x_ref, shard_ref, h_ref, wall_ref, wbuf, fetch_sem, send_sems, recv_sems, local_sem):
        del order_ref
        me, sibling, over_ici, passed_on = parties()
        j, i = pl.program_id(0), pl.program_id(1)
        land = lambda dev: wall_ref.at[slot(dev)]

        def copy(k, src, block, to):
            return pltpu.make_async_remote_copy(src_ref=src, dst_ref=land(block), send_sem=send_sems.at[k],
                                                recv_sem=recv_sems.at[k], device_id=to, device_id_type=pl.DeviceIdType.MESH)

        def fetch(src):
            cp = pltpu.make_async_copy(src, wbuf, fetch_sem)
            cp.start()
            cp.wait()

        keep = pltpu.make_async_copy(shard_ref, land(me), local_sem)
        first = [copy(0, shard_ref, me, sibling)] + [copy(1 + n, shard_ref, me, dev) for n, dev in enumerate(over_ici)]
        onward = [copy(4 + n, land(dev), dev, sibling) for n, dev in enumerate(over_ici)]

        @pl.when(jnp.logical_and(i == 0, j == 0))
        def _():
            keep.start()
            for cp in first:
                cp.start()
            fetch(shard_ref)

        @pl.when(jnp.logical_and(i == 0, j == 1))
        def _():
            copy(0, shard_ref, sibling, me).wait_recv()
            fetch(land(sibling))

        for n, dev in enumerate(over_ici):
            @pl.when(jnp.logical_and(i == 0, j == ici_step[n]))
            def _(n=n, dev=dev):
                copy(1 + n, shard_ref, dev, me).wait_recv()
                onward[n].start()
                fetch(land(dev))

        for n, dev in enumerate(passed_on):
            @pl.when(jnp.logical_and(i == 0, j == passed_step[n]))
            def _(n=n, dev=dev):
                copy(4 + n, shard_ref, dev, me).wait_recv()
                fetch(land(dev))

        h_ref[...] = _dot(x_ref[...], wbuf[...])

        @pl.when(jnp.logical_and(i == nrow - 1, j == N_DEV - 1))
        def _():
            for cp in first + onward:
                cp.wait_send()
            keep.wait()

    need = 2 * (tm * d * 2 + tm * d * 4) + d * d * 2 + tm * d * 4
    h, w_all = pl.pallas_call(
        body,
        grid_spec=pltpu.PrefetchScalarGridSpec(
            num_scalar_prefetch=1,
            grid=(N_DEV, nrow),
            in_specs=[pl.BlockSpec((tm, d), lambda j, i, order: (i, 0)), pl.BlockSpec(memory_space=pl.ANY)],
            out_specs=[pl.BlockSpec((None, tm, d), lambda j, i, order: (order[j], i, 0)), pl.BlockSpec(memory_space=pl.ANY)],
            scratch_shapes=[pltpu.VMEM((d, d), BF16), pltpu.SemaphoreType.DMA, pltpu.SemaphoreType.DMA((per,)),
                            pltpu.SemaphoreType.DMA((per,)), pltpu.SemaphoreType.DMA],
        ),
        out_shape=[jax.ShapeDtypeStruct((N_DEV, t, d), F32), jax.ShapeDtypeStruct((N_DEV, d, d), BF16)],
        compiler_params=_params(("arbitrary", "arbitrary"), need),
        name="gather_project",
    )(order, x_b, shard)
    return h, w_all


def _pair_sums(g):
    n, r, cols = g.shape
    half = n // 2

    def swap(g_ref, got_ref, send_sems, recv_sems):
        x, y, c = lax.axis_index("x"), lax.axis_index("y"), lax.axis_index("c")
        copies = [pltpu.make_async_remote_copy(
            src_ref=g_ref.at[2 * j + 1 - c], dst_ref=got_ref.at[j], send_sem=send_sems.at[j], recv_sem=recv_sems.at[j],
            device_id=(x, y, 1 - c), device_id_type=pl.DeviceIdType.MESH) for j in range(half)]
        for cp in copies:
            cp.start()
        for cp in copies:
            cp.wait()

    got = pl.pallas_call(
        swap,
        in_specs=[pl.BlockSpec(memory_space=pl.ANY)],
        out_specs=pl.BlockSpec(memory_space=pl.ANY),
        out_shape=jax.ShapeDtypeStruct((half, r, cols), g.dtype),
        scratch_shapes=[pltpu.SemaphoreType.DMA((half,))] * 2,
        name="pair_swap",
    )(g)

    def add(mine_ref, got_ref, out_ref):
        mine = jnp.where(lax.axis_index("c") == 0, mine_ref[0].astype(F32), mine_ref[1].astype(F32))
        out_ref[...] = (mine + got_ref[...].astype(F32)).astype(out_ref.dtype)

    tile = _pick(r, (512, 256, 128))
    return pl.pallas_call(
        add,
        grid=(half, r // tile),
        in_specs=[pl.BlockSpec((None, 2, tile, cols), lambda j, i: (j, 0, i, 0)),
                  pl.BlockSpec((None, tile, cols), lambda j, i: (j, i, 0))],
        out_specs=pl.BlockSpec((None, tile, cols), lambda j, i: (j, i, 0)),
        out_shape=jax.ShapeDtypeStruct((half, r, cols), g.dtype),
        compiler_params=_params(("parallel", "parallel")),
        name="pair_add",
    )(g.reshape(half, 2, r, cols), got)


def _local_step(x, p, tgt, small, comm):
    t = x.shape[0]
    tile = _pick(t, (256, 128))
    d = D_MODEL
    act_b, act_f = (d, BF16), (d, F32)
    x_b, p_b = x.astype(BF16), p.astype(BF16)

    chunk_id = jnp.arange(SGU_BLOCK) // CHUNK
    mask = chunk_id[:, None] >= chunk_id[None, :]
    wm = jnp.where(mask[None], small["sgu_w_s"], 0.0)
    wm_b = wm.astype(BF16)
    wm_t = jnp.swapaxes(wm, 1, 2).astype(BF16)
    bs_t = small["sgu_b_s"].T

    h, w_in = comm.project_in(x_b)
    y_a, got_a = _sgu_fwd(h, wm_b, bs_t, small["sgu_norm_g"], small["sgu_norm_b"], ex=comm.weights_exchange(0))
    y_b, o_all, states, got_b = _hgrn_fwd(h, small["lb_logits"], small["hgrn_norm_g"], ex=comm.weights_exchange(1))
    wts, conv_w = comm.weights(list(got_a) + list(got_b))
    z_a = _mm(y_a, wts["w_a"], out_dtype=F32, name="mm_za")
    z_b = _mm(y_b, wts["w_b"], out_dtype=F32, name="mm_zb")
    gates = [(h, SLOT_GA), (h, SLOT_GB)]
    merged, = _rowwise(_merge_fwd, gates + [z_a, z_b], [], [act_b], [], tile=tile, name="merge_fwd")
    r1 = _mm(merged, wts["w_o"], out_dtype=F32, name="mm_r1")
    x1, x1_b = _rowwise(_ln1_fwd, [x, r1], [small["ln1_g"], small["ln1_b"]], [act_f, act_b], [], tile=tile, name="ln1_fwd")
    gate = _mm(x1_b, wts["w_g"], out_dtype=F32, name="mm_gate")
    val = _mm(x1_b, wts["w_v"], out_dtype=F32, name="mm_val")
    pg = _mm(x1_b, wts["w_pg"], out_dtype=F32, name="mm_pg")
    pp = _mm(p_b, wts["w_pp"], out_dtype=F32, name="mm_pp")
    hid, = _rowwise(functools.partial(_conv_fwd, tile), [gate, (gate, _halo_spec(D_FF, tile, t, False)), val],
                    [conv_w, small["conv_b"]], [(D_FF, BF16)], [], tile=tile, name="conv_fwd")
    fo = _mm(hid, wts["w_down"], out_dtype=F32, name="mm_down")
    dz2, dz2_b, dpg, dpp, loss, dg2, db2 = _rowwise(
        _ln2_loss, [x1, fo, pg, pp, tgt], [small["ln2_g"], small["ln2_b"]],
        [act_f, act_b, act_b, act_b], [(8, LANE), (1, d), (1, d)], tile=tile, name="ln2_loss")

    dhid = _mm(dz2_b, wts["w_down"], out_dtype=F32, name="mm_dhid", trans_b=True)
    g_down = _mm_tn(hid, dz2_b, out_dtype=BF16, name="mm_g_down")
    dcg, dval, dcw0, dcw1, dcw2, dcb = _rowwise(
        functools.partial(_conv_bwd_a, tile), [gate, (gate, _halo_spec(D_FF, tile, t, False)), val, dhid],
        [conv_w, small["conv_b"]], [(D_FF, F32), (D_FF, BF16)], [(1, D_FF)] * 4, tile=tile, name="conv_bwd_a")
    dgate, = _rowwise(functools.partial(_conv_bwd_b, tile), [dcg, (dcg, _halo_spec(D_FF, tile, t, True))],
                      [conv_w], [(D_FF, BF16)], [], tile=tile, name="conv_bwd_b")
    g_g = _mm_tn(x1_b, dgate, out_dtype=BF16, name="mm_g_gate")
    g_v = _mm_tn(x1_b, dval, out_dtype=BF16, name="mm_g_val")
    g_pg = _mm_tn(x1_b, dpg, out_dtype=BF16, name="mm_g_pg")
    g_pp = _mm_tn(p_b, dpp, out_dtype=BF16, name="mm_g_pp")
    dx1 = _mm(dgate, wts["w_g"], out_dtype=F32, name="mm_dx1_gate", trans_b=True, adds=[(dz2, ALPHA)])
    dx1 = _mm(dval, wts["w_v"], out_dtype=F32, name="mm_dx1_val", trans_b=True, adds=[(dx1, 1.0)])
    dx1 = _mm(dpg, wts["w_pg"], out_dtype=F32, name="mm_dx1_pg", trans_b=True, adds=[(dx1, 1.0)])
    dz1, dz1_b, dg1, db1 = _rowwise(_ln1_bwd, [x, r1, dx1], [small["ln1_g"]], [act_f, act_b], [(1, d), (1, d)],
                                    tile=tile, name="ln1_bwd")
    g_o = _mm_tn(merged, dz1_b, out_dtype=BF16, name="mm_g_o")
    dm = _mm(dz1_b, wts["w_o"], out_dtype=F32, name="mm_dm", trans_b=True)
    dh, dza, dzb = _rowwise(_merge_bwd, gates + [z_a, z_b, dm], [],
                            [("stack", 2, SLOT_GA // 2, 8, d, BF16), act_b, act_b], [], tile=tile, name="merge_bwd")
    g_a = _mm_tn(y_a, dza, out_dtype=BF16, name="mm_g_a")
    g_b = _mm_tn(y_b, dzb, out_dtype=BF16, name="mm_g_b")
    dy_a = _mm(dza, wts["w_a"], out_dtype=F32, name="mm_dya", trans_b=True)
    dy_b = _mm(dzb, wts["w_b"], out_dtype=F32, name="mm_dyb", trans_b=True)
    dh, dws, dbs, dgv_n, dbv_n = _sgu_bwd(h, dy_a, dh, wm_b, wm_t, bs_t, small["sgu_norm_g"], small["sgu_norm_b"])
    big = dict(w_a=g_a, w_b=g_b, w_o=g_o, w_g=g_g, w_v=g_v, w_down=g_down, w_pp=g_pp, w_pg=g_pg)
    sm = dict(sgu_w_s=jnp.where(mask[None], dws, 0.0), sgu_b_s=dbs[:, :GROUPS].T, sgu_norm_g=dgv_n, sgu_norm_b=dbv_n,
              ln1_g=dg1, ln1_b=db1, conv_w=jnp.concatenate([dcw0, dcw1, dcw2], axis=0), conv_b=dcb, ln2_g=dg2, ln2_b=db2,
              loss=loss)
    dh, dlogits, dgn, got = _hgrn_bwd(h, o_all, dy_b, states, dh, small["lb_logits"], small["hgrn_norm_g"],
                                      ex=comm.grads_exchange(big, sm))
    comm.grads_done(got)
    g_in = _mm_tn(x_b, dh, out_dtype=BF16, name="mm_g_in")
    ex = comm.last_exchange(g_in, dict(lb_logits=dlogits, hgrn_norm_g=dgn))
    res = _mm(dh, w_in, out_dtype=F32, name="mm_dx", trans_b=True, reduce_b=True, adds=[(dz1, ALPHA)], ex=ex)
    grad_x, got = res if ex else (res, ())
    comm.last_done(got)
    return grad_x


_SMALL_EARLY = ["sgu_w_s", "sgu_b_s", "sgu_norm_g", "sgu_norm_b", "ln1_g", "ln1_b", "ffn_conv_b", "ln2_g", "ln2_b"]
_SMALL_LATE = ["hgrn_lb_logits", "hgrn_norm_g"]
N_TAPS = D_FF // N_DEV
UP_COLS = 2 * D_FF // N_DEV


class _StepExchanges:
    def __init__(self, w_in_shard, shards):
        self.w_in_shard = w_in_shard
        self.shards = shards

    def project_in(self, x_b):
        return _gather_project(x_b, self.w_in_shard)

    def weights_exchange(self, part):
        return _Exchange([(s, "gather+relay") for s in (self.shards[:2] if part == 0 else self.shards[2:])])

    def weights(self, got):
        d, f = D_MODEL, D_FF
        w_br_g, w_o_g, w_up_g, w_down_g, w_pp_g, w_pg_g, conv_g = got
        w_br = w_br_g.transpose(1, 0, 2, 3).reshape(2, d, d)
        w_up = w_up_g.transpose(1, 0, 2).reshape(d, 2, f).transpose(1, 0, 2)
        wts = dict(w_a=w_br[0], w_b=w_br[1], w_o=w_o_g.reshape(d, d), w_g=w_up[0], w_v=w_up[1],
                   w_down=w_down_g.reshape(f, d), w_pp=w_pp_g.transpose(1, 0, 2).reshape(256, d), w_pg=w_pg_g.reshape(d, d))
        return wts, conv_g.transpose(1, 0, 2).reshape(3, f)

    def grads_exchange(self, big, sm):
        d = D_MODEL
        parts = [jnp.stack([big["w_a"], big["w_b"]]).reshape(2, N_DEV, 128, d).transpose(1, 0, 2, 3),
                 big["w_o"].reshape(N_DEV, 128, d),
                 jnp.concatenate([big["w_g"], big["w_v"]], axis=1).reshape(d, N_DEV, UP_COLS).transpose(1, 0, 2),
                 big["w_down"].reshape(N_DEV, N_TAPS, d),
                 big["w_pp"].reshape(256, N_DEV, 128).transpose(1, 0, 2),
                 big["w_pg"].reshape(N_DEV, 128, d)]
        packed, self.rows_early = _pack([sm[k] for k in ("sgu_w_s", "sgu_b_s", "sgu_norm_g", "sgu_norm_b", "ln1_g", "ln1_b",
                                                         "conv_b", "ln2_g", "ln2_b", "conv_w", "loss")])
        return _Exchange([(a, "scatter") for a in parts] + [(packed, "gather")])

    def grads_done(self, got):
        self.recv, self.small_early = got[:6], got[6]

    def last_exchange(self, g_in, sm):
        packed, self.rows_late = _pack([sm["lb_logits"], sm["hgrn_norm_g"]])
        return _Exchange([(_pair_sums(g_in), "scatter+pairs"), (packed, "gather")])

    def last_done(self, got):
        self.recv_in, self.small_late = got


def _rows128(a):
    flat = a.reshape(-1)
    rows = -(-flat.shape[0] // (8 * LANE)) * 8
    return jnp.pad(flat, (0, rows * LANE - flat.shape[0])).reshape(rows, LANE)


def _pack(parts):
    blocks = [_rows128(a) for a in parts]
    return jnp.concatenate(blocks, axis=0), [b.shape[0] for b in blocks]


def _unpack(packed, shapes, rows):
    out, r0 = [], 0
    for shp, r in zip(shapes, rows):
        n = math.prod(shp)
        out.append(packed[r0:r0 + r].reshape(-1)[:n].reshape(shp))
        r0 += r
    return out


def kernel(x, p, w_in, sgu_w_s, sgu_b_s, sgu_norm_g, sgu_norm_b, hgrn_lb_logits, hgrn_norm_g, w_branch, w_out, ln1_g, ln1_b, ffn_w_up, ffn_conv_w, ffn_conv_b, ffn_w_down, ln2_g, ln2_b, ple_w_proj, ple_w_gate, loss_target, m_w_in, m_sgu_w_s, m_sgu_b_s, m_sgu_norm_g, m_sgu_norm_b, m_hgrn_lb_logits, m_hgrn_norm_g, m_w_branch, m_w_out, m_ln1_g, m_ln1_b, m_ffn_w_up, m_ffn_conv_w, m_ffn_conv_b, m_ffn_w_down, m_ln2_g, m_ln2_b, m_ple_w_proj, m_ple_w_gate, v_w_in, v_sgu_w_s, v_sgu_b_s, v_sgu_norm_g, v_sgu_norm_b, v_hgrn_lb_logits, v_hgrn_norm_g, v_w_branch, v_w_out, v_ln1_g, v_ln1_b, v_ffn_w_up, v_ffn_conv_w, v_ffn_conv_b, v_ffn_w_down, v_ln2_g, v_ln2_b, v_ple_w_proj, v_ple_w_gate):
    weights = dict(w_in=w_in, sgu_w_s=sgu_w_s, sgu_b_s=sgu_b_s, sgu_norm_g=sgu_norm_g, sgu_norm_b=sgu_norm_b,
                   hgrn_lb_logits=hgrn_lb_logits, hgrn_norm_g=hgrn_norm_g, w_branch=w_branch, w_out=w_out,
                   ln1_g=ln1_g, ln1_b=ln1_b, ffn_w_up=ffn_w_up, ffn_conv_w=ffn_conv_w, ffn_conv_b=ffn_conv_b,
                   ffn_w_down=ffn_w_down, ln2_g=ln2_g, ln2_b=ln2_b, ple_w_proj=ple_w_proj, ple_w_gate=ple_w_gate)
    mom_m = dict(w_in=m_w_in, sgu_w_s=m_sgu_w_s, sgu_b_s=m_sgu_b_s, sgu_norm_g=m_sgu_norm_g, sgu_norm_b=m_sgu_norm_b,
                 hgrn_lb_logits=m_hgrn_lb_logits, hgrn_norm_g=m_hgrn_norm_g, w_branch=m_w_branch, w_out=m_w_out,
                 ln1_g=m_ln1_g, ln1_b=m_ln1_b, ffn_w_up=m_ffn_w_up, ffn_conv_w=m_ffn_conv_w, ffn_conv_b=m_ffn_conv_b,
                 ffn_w_down=m_ffn_w_down, ln2_g=m_ln2_g, ln2_b=m_ln2_b, ple_w_proj=m_ple_w_proj, ple_w_gate=m_ple_w_gate)
    mom_v = dict(w_in=v_w_in, sgu_w_s=v_sgu_w_s, sgu_b_s=v_sgu_b_s, sgu_norm_g=v_sgu_norm_g, sgu_norm_b=v_sgu_norm_b,
                 hgrn_lb_logits=v_hgrn_lb_logits, hgrn_norm_g=v_hgrn_norm_g, w_branch=v_w_branch, w_out=v_w_out,
                 ln1_g=v_ln1_g, ln1_b=v_ln1_b, ffn_w_up=v_ffn_w_up, ffn_conv_w=v_ffn_conv_w, ffn_conv_b=v_ffn_conv_b,
                 ffn_w_down=v_ffn_w_down, ln2_g=v_ln2_g, ln2_b=v_ln2_b, ple_w_proj=v_ple_w_proj, ple_w_gate=v_ple_w_gate)
    d, f = D_MODEL, D_FF
    me = _my_index()

    comm = _StepExchanges(w_in[0].astype(BF16),
                          [w_branch[0].astype(BF16), w_out[0].astype(BF16), ffn_w_up[0].astype(BF16),
                           ffn_w_down[0].astype(BF16), ple_w_proj[0].astype(BF16), ple_w_gate[0].astype(BF16), ffn_conv_w[0]])
    small = dict(sgu_w_s=sgu_w_s[0], sgu_b_s=sgu_b_s[0], sgu_norm_g=sgu_norm_g, sgu_norm_b=sgu_norm_b,
                 lb_logits=hgrn_lb_logits, hgrn_norm_g=hgrn_norm_g, ln1_g=ln1_g, ln1_b=ln1_b, ln2_g=ln2_g, ln2_b=ln2_b,
                 conv_b=ffn_conv_b)
    grad_x = _local_step(x[0], p[0, 0], loss_target[0], small, comm)

    out = {}

    def adam(name, parts8, shape2d):
        w2, m2, v2 = (a.reshape(shape2d) for a in (weights[name], mom_m[name], mom_v[name]))
        res = _adam_call(w2, m2, v2, parts8.reshape(parts8.shape[:1] + shape2d), "adam_" + name)
        out[name] = tuple(r.reshape(weights[name].shape) for r in res)

    adam("w_in", comm.recv_in, (d, d))
    adam("w_branch", comm.recv[0], (256, d))
    adam("w_out", comm.recv[1], (128, d))
    adam("ffn_w_up", comm.recv[2], (d, UP_COLS))
    adam("ffn_w_down", comm.recv[3], (N_TAPS, d))
    adam("ple_w_proj", comm.recv[4], (256, 128))
    adam("ple_w_gate", comm.recv[5], (128, d))

    def adam_small(names, extra_w, extra_m, extra_v, extra_shapes, parts8, rows, label):
        pk = lambda src, extra: _pack([src[n] for n in names] + extra)[0]
        res = _adam_call(pk(weights, extra_w), pk(mom_m, extra_m), pk(mom_v, extra_v), parts8, label)
        shapes = [weights[n].shape for n in names] + extra_shapes
        unpacked = [_unpack(r, shapes, rows) for r in res]
        for j, n in enumerate(names):
            out[n] = tuple(u[j] for u in unpacked)
        return [[u[len(names) + j] for u in unpacked] for j in range(len(extra_shapes))]

    blank = jnp.zeros((8, LANE), F32)
    taps, loss_rows = adam_small(
        _SMALL_EARLY, [_place_taps(ffn_conv_w[0], me, f), blank], [_place_taps(m_ffn_conv_w[0], me, f), blank],
        [_place_taps(v_ffn_conv_w[0], me, f), blank + 1.0], [(3, f), (8, LANE)], comm.small_early, comm.rows_early,
        "adam_small_early")
    adam_small(_SMALL_LATE, [], [], [], [], comm.small_late, comm.rows_late, "adam_small_late")
    out["ffn_conv_w"] = tuple(lax.dynamic_slice_in_dim(u, me * N_TAPS, N_TAPS, axis=1)[None] for u in taps)
    loss = loss_rows[0][0, 0]

    order = ["w_in", "sgu_w_s", "sgu_b_s", "sgu_norm_g", "sgu_norm_b", "hgrn_lb_logits", "hgrn_norm_g", "w_branch", "w_out",
             "ln1_g", "ln1_b", "ffn_w_up", "ffn_conv_w", "ffn_conv_b", "ffn_w_down", "ln2_g", "ln2_b", "ple_w_proj", "ple_w_gate"]
    return (loss, grad_x[None], *[out[n][0] for n in order], *[out[n][1] for n in order],
            *[out[n][2] for n in order], *[out[n][3] for n in order])


def _place_taps(shard, me, f):
    return lax.dynamic_update_slice_in_dim(jnp.zeros((3, f), F32), shard, me * N_TAPS, axis=1)
```

```python
import functools
import math

import jax
import jax.numpy as jnp
from jax import lax
from jax.experimental import pallas as pl
from jax.experimental.pallas import tpu as pltpu

F32 = jnp.float32
BF16 = jnp.bfloat16

N_DEV = 8
N_CHIP = 4
D_MODEL = 1024
CHUNK = 64
SUB = 16
SGU_BLOCK = 128
GROUPS = 8
HEAD = 128
HEADS = 8
HEAD_UNROLL = 4
D_FF = 2816
LN_EPS = 1e-5
RMS_EPS = 1e-6
ALPHA = 2.0 ** 0.25
GELU_K = math.sqrt(2.0 / math.pi)
GELU_C = 0.044715
NEG = -1e30
ADAM_LR, ADAM_B1, ADAM_B2, ADAM_EPS, ADAM_WD, ADAM_STEP = 0.001, 0.9, 0.999, 1e-08, 0.01, 10
LANE = 128
SLOT_Q, SLOT_F, SLOT_I, SLOT_OG, SLOT_U, SLOT_V, SLOT_GA, SLOT_GB = range(8)


def _slot_of_group(k):
    return jnp.where(k < 2, k + 4, jnp.where(k < 6, k - 2, k))


MIB = 1024 * 1024
VMEM_V7X = 64 * MIB
VMEM_FLOOR = 32 * MIB
MM_TILES = (1024, 1408, 512, 256, 128)


def _params(sem, need=0):
    limit = min(max(need + need // 4, VMEM_FLOOR), VMEM_V7X - 4 * MIB)
    return pltpu.CompilerParams(dimension_semantics=sem, vmem_limit_bytes=limit)


def _pick(n, prefs):
    for t in prefs:
        if n % t == 0:
            return t
    return n


def _gelu(x):
    return 0.5 * x * (1.0 + jnp.tanh(GELU_K * (x + GELU_C * x * x * x)))


def _gelu_grad(x):
    t = jnp.tanh(GELU_K * (x + GELU_C * x * x * x))
    return 0.5 * (1.0 + t) + 0.5 * x * (1.0 - t * t) * GELU_K * (1.0 + 3.0 * GELU_C * x * x)


def _silu_grad(x, s):
    return s * (1.0 + x * (1.0 - s))


def _dot(a, b):
    return jnp.dot(a.astype(BF16), b.astype(BF16), preferred_element_type=F32)


def _dot_nt(a, b):
    return lax.dot_general(a.astype(BF16), b.astype(BF16), (((1,), (1,)), ((), ())), preferred_element_type=F32)


def _dot_tn(a, b):
    return lax.dot_general(a.astype(BF16), b.astype(BF16), (((0,), (0,)), ((), ())), preferred_element_type=F32)


def _mean(x):
    return jnp.mean(x, axis=-1, keepdims=True)


def _sum0(x):
    return jnp.sum(x, axis=0, keepdims=True)


def _mm(a, b, *, out_dtype, name, trans_b=False, reduce_b=False, adds=(), ex=None):
    squeeze = b.ndim == 2
    a3 = a if a.ndim == 3 else a[None]
    b3 = b if b.ndim == 3 else b[None]
    ba, m, k = a3.shape
    bb = b3.shape[0]
    n = b3.shape[1] if trans_b else b3.shape[2]
    tm = _pick(m, MM_TILES)
    tn = _pick(n, MM_TILES)
    if reduce_b:
        bo, steps = 1, bb
        a_map = lambda o, i, j, r: (r if ba > 1 else 0, i, 0)
        b_map = (lambda o, i, j, r: (r, j, 0)) if trans_b else (lambda o, i, j, r: (r, 0, j))
    else:
        bo, steps = bb, 1
        a_map = lambda o, i, j, r: (o if ba > 1 else 0, i, 0)
        b_map = (lambda o, i, j, r: (o, j, 0)) if trans_b else (lambda o, i, j, r: (o, 0, j))
    o_map = lambda o, i, j, r: (o, i, j)
    add_arrays = [x if x.ndim == 3 else x[None] for x, _ in adds]
    add_scales = [s for _, s in adds]
    n_add = len(adds)
    dot = _dot_nt if trans_b else _dot

    def finish(acc, add_refs, o_ref):
        for ref, s in zip(add_refs, add_scales):
            acc = acc + s * ref[...].astype(F32)
        o_ref[...] = acc.astype(o_ref.dtype)

    grid = (bo, m // tm, n // tn, steps)

    def body(*refs):
        ins, (o_ref,), scratch, xrefs = _split_refs(refs, 2 + n_add, 1, 1 if reduce_b else 0, ex)
        a_ref, b_ref, add_refs = ins[0], ins[1], ins[2:]
        step = ((pl.program_id(0) * grid[1] + pl.program_id(1)) * grid[2] + pl.program_id(2)) * grid[3] + pl.program_id(3)
        if ex:
            @pl.when(step == 0)
            def _():
                ex.start(*xrefs)

        if reduce_b:
            acc, = scratch
            r = pl.program_id(3)

            @pl.when(r == 0)
            def _():
                acc[...] = jnp.zeros_like(acc)

            acc[...] += dot(a_ref[...], b_ref[...])

            @pl.when(r == steps - 1)
            def _():
                finish(acc[...], add_refs, o_ref)
        else:
            finish(dot(a_ref[...], b_ref[...]), add_refs, o_ref)

        if ex:
            @pl.when(step == math.prod(grid) - 1)
            def _():
                ex.finish(*xrefs)

    b_block = (None, tn, k) if trans_b else (None, k, tn)
    out_bytes = tm * tn * jnp.dtype(out_dtype).itemsize
    need = 2 * (tm * k * a3.dtype.itemsize + k * tn * b3.dtype.itemsize + out_bytes + n_add * tm * tn * 4)
    need += 2 * tm * tn * 4
    sem = ("arbitrary",) * 4 if ex else ("parallel", "parallel", "parallel", "arbitrary")
    res = pl.pallas_call(
        body,
        grid=grid,
        in_specs=[pl.BlockSpec((None, tm, k), a_map), pl.BlockSpec(b_block, b_map)]
        + [pl.BlockSpec((None, tm, tn), o_map) for _ in adds] + (ex.in_specs if ex else []),
        out_specs=[pl.BlockSpec((None, tm, tn), o_map)] + (ex.out_specs if ex else []),
        out_shape=[jax.ShapeDtypeStruct((bo, m, n), out_dtype)] + (ex.out_shape if ex else []),
        scratch_shapes=([pltpu.VMEM((tm, tn), F32)] if reduce_b else []) + (ex.scratch if ex else []),
        compiler_params=_params(sem, need),
        name=name,
    )(a3, b3, *add_arrays, *(ex.arrays if ex else []))
    out = res[0][0] if (reduce_b or squeeze) else res[0]
    return (out, res[1:]) if ex else out


def _mm_tn(a, b, *, out_dtype, name):
    squeeze = b.ndim == 2
    b3 = b if b.ndim == 3 else b[None]
    t, m = a.shape
    bb, _, n = b3.shape
    tm = _pick(m, MM_TILES)
    tn = _pick(n, MM_TILES)
    tt = _pick(t, (1024, 512, 256, 128))
    steps = t // tt
    need = 2 * (tt * tm * a.dtype.itemsize + tt * tn * b3.dtype.itemsize + tm * tn * jnp.dtype(out_dtype).itemsize)
    need += 2 * tm * tn * 4

    def body(a_ref, b_ref, o_ref, acc):
        r = pl.program_id(3)

        @pl.when(r == 0)
        def _():
            acc[...] = jnp.zeros_like(acc)

        acc[...] += _dot_tn(a_ref[...], b_ref[...])

        @pl.when(r == steps - 1)
        def _():
            o_ref[...] = acc[...].astype(o_ref.dtype)

    out = pl.pallas_call(
        body,
        grid=(bb, m // tm, n // tn, steps),
        in_specs=[pl.BlockSpec((tt, tm), lambda o, i, j, r: (r, i)),
                  pl.BlockSpec((None, tt, tn), lambda o, i, j, r: (o, r, j))],
        out_specs=pl.BlockSpec((None, tm, tn), lambda o, i, j, r: (o, i, j)),
        out_shape=jax.ShapeDtypeStruct((bb, m, n), out_dtype),
        scratch_shapes=[pltpu.VMEM((tm, tn), F32)],
        compiler_params=_params(("parallel", "parallel", "parallel", "arbitrary"), need),
        name=name,
    )(a, b3)
    return out[0] if squeeze else out


def _rowwise(fn, rows, consts, row_outs, acc_outs, *, tile, name):
    first = rows[0][0] if isinstance(rows[0], tuple) else rows[0]
    t = first.shape[-2]
    steps = t // tile
    arrays, in_specs = [], []
    for r in rows:
        if isinstance(r, tuple) and isinstance(r[1], pl.BlockSpec):
            arrays.append(r[0])
            in_specs.append(r[1])
        elif isinstance(r, tuple):
            arr, bidx = r
            arrays.append(arr)
            in_specs.append(pl.BlockSpec((None, tile, arr.shape[-1]), functools.partial(lambda i, b: (b, i, 0), b=bidx)))
        else:
            arrays.append(r)
            in_specs.append(pl.BlockSpec((tile, r.shape[-1]), lambda i: (i, 0)))
    for c in consts:
        arrays.append(c)
        in_specs.append(pl.BlockSpec(c.shape, lambda i: (0, 0)))
    n_in, n_row = len(arrays), len(row_outs)
    out_shape, out_specs = [], []
    for ro in row_outs:
        if ro[0] == "stack":
            _, cnt, blk, total, w, dt = ro
            out_shape.append(jax.ShapeDtypeStruct((total, t, w), dt))
            out_specs.append(pl.BlockSpec((cnt, tile, w), functools.partial(lambda i, b: (b, i, 0), b=blk)))
        else:
            w, dt = ro
            out_shape.append(jax.ShapeDtypeStruct((t, w), dt))
            out_specs.append(pl.BlockSpec((tile, w), lambda i: (i, 0)))
    out_shape += [jax.ShapeDtypeStruct(s, F32) for s in acc_outs]
    out_specs += [pl.BlockSpec(s, lambda i: (0, 0)) for s in acc_outs]
    blocks = [math.prod(d for d in sp.block_shape if d) * arr.dtype.itemsize for sp, arr in zip(in_specs, arrays)]
    blocks += [math.prod(d for d in sp.block_shape if d) * jnp.dtype(sh.dtype).itemsize
               for sp, sh in zip(out_specs, out_shape)]
    need = 2 * sum(blocks) + 6 * tile * max(a.shape[-1] for a in arrays) * 4

    def body(*refs):
        ins, outs = refs[:n_in], refs[n_in:]
        i = pl.program_id(0)
        res = fn(i, steps, *[r[...] for r in ins])
        res = res if isinstance(res, (tuple, list)) else (res,)
        for ref, val in zip(outs[:n_row], res[:n_row]):
            ref[...] = val.astype(ref.dtype)
        if acc_outs:
            @pl.when(i == 0)
            def _():
                for ref in outs[n_row:]:
                    ref[...] = jnp.zeros_like(ref)

            for ref, val in zip(outs[n_row:], res[n_row:]):
                ref[...] += val

    return pl.pallas_call(
        body,
        grid=(steps,),
        in_specs=in_specs,
        out_specs=out_specs,
        out_shape=out_shape,
        compiler_params=_params(("arbitrary",), need),
        name=name,
    )(*arrays)


def _ln_stats(z):
    mu = _mean(z)
    zc = z - mu
    rstd = lax.rsqrt(_mean(zc * zc) + LN_EPS)
    return zc * rstd, rstd


def _ln_bwd(dy, xhat, rstd, g):
    dxh = dy * g
    return rstd * (dxh - _mean(dxh) - xhat * _mean(dxh * xhat))


def _ride(ex, xrefs, step, steps):
    if not ex:
        return
    for at, act in ((0, ex.start), ((3 * steps) // 4, ex.relay), (steps - 1, ex.finish)):
        @pl.when(step == at)
        def _(act=act):
            act(*xrefs)


def _sgu_fwd(h, wm, bs_t, g_v, b_v, ex=None):
    t = h.shape[1]
    steps = t // SGU_BLOCK

    def body(*refs):
        (u_ref, v_ref, wm_ref, bs_ref, g_ref, b_ref), (y_ref,), _, xrefs = _split_refs(refs, 6, 1, 0, ex)
        xhat, _ = _ln_stats(_gelu(v_ref[...]))
        vn = (xhat * g_ref[...] + b_ref[...]).astype(BF16)
        gu = _gelu(u_ref[...])
        for g in range(GROUPS):
            sl = slice(g * HEAD, (g + 1) * HEAD)
            mixed = _dot(wm_ref[g], vn[:, sl]) + bs_ref[:, g:g + 1]
            y_ref[:, sl] = (gu[:, sl] * mixed).astype(BF16)
        _ride(ex, xrefs, pl.program_id(0), steps)

    blk = lambda b: pl.BlockSpec((None, SGU_BLOCK, D_MODEL), functools.partial(lambda i, b: (b, i, 0), b=b))
    whole = lambda s: pl.BlockSpec(s, lambda i: (0,) * len(s))
    res = pl.pallas_call(
        body,
        grid=(steps,),
        in_specs=[blk(SLOT_U), blk(SLOT_V), whole(wm.shape), whole(bs_t.shape), whole(g_v.shape), whole(b_v.shape)]
        + (ex.in_specs if ex else []),
        out_specs=[pl.BlockSpec((SGU_BLOCK, D_MODEL), lambda i: (i, 0))] + (ex.out_specs if ex else []),
        out_shape=[jax.ShapeDtypeStruct((t, D_MODEL), BF16)] + (ex.out_shape if ex else []),
        scratch_shapes=ex.scratch if ex else [],
        compiler_params=_params(("arbitrary",)),
        name="sgu_fwd",
    )(h, h, wm, bs_t, g_v, b_v, *(ex.arrays if ex else []))
    return res[0], res[1:]


def _sgu_bwd(h, dy, dh, wm, wm_t, bs_t, g_v, b_v):
    t = h.shape[1]

    def body(u_ref, v_ref, dy_ref, dh_in, wm_ref, wmt_ref, bs_ref, g_ref, b_ref,
             duv_ref, dw_ref, dbs_ref, dg_ref, db_ref, dvn_ref):
        del dh_in
        du_ref, dv_ref = duv_ref.at[0], duv_ref.at[1]
        i = pl.program_id(0)

        @pl.when(i == 0)
        def _():
            dw_ref[...] = jnp.zeros_like(dw_ref)
            dbs_ref[...] = jnp.zeros_like(dbs_ref)
            dg_ref[...] = jnp.zeros_like(dg_ref)
            db_ref[...] = jnp.zeros_like(db_ref)

        u = u_ref[...]
        v = v_ref[...]
        xhat, rstd = _ln_stats(_gelu(v))
        vn = (xhat * g_ref[...] + b_ref[...]).astype(BF16)
        gu = _gelu(u)
        gup = _gelu_grad(u)
        lane = lax.broadcasted_iota(jnp.int32, (SGU_BLOCK, LANE), 1)
        dbs = jnp.zeros((SGU_BLOCK, LANE), F32)
        for g in range(GROUPS):
            sl = slice(g * HEAD, (g + 1) * HEAD)
            vn_g = vn[:, sl]
            mixed = _dot(wm_ref[g], vn_g) + bs_ref[:, g:g + 1]
            dy_g = dy_ref[:, sl]
            du_ref[:, sl] = (dy_g * mixed * gup[:, sl]).astype(BF16)
            dmix = dy_g * gu[:, sl]
            dmb = dmix.astype(BF16)
            dvn_ref[:, sl] = _dot(wmt_ref[g], dmb)
            dw_ref[g] += _dot_nt(dmb, vn_g)
            dbs = dbs + jnp.where(lane == g, jnp.sum(dmix, axis=1, keepdims=True), 0.0)
        dbs_ref[...] += dbs
        dvn = dvn_ref[...]
        dg_ref[...] += _sum0(dvn * xhat)
        db_ref[...] += _sum0(dvn)
        dv_ref[...] = (_ln_bwd(dvn, xhat, rstd, g_ref[...]) * _gelu_grad(v)).astype(BF16)

    blk = lambda b: pl.BlockSpec((None, SGU_BLOCK, D_MODEL), functools.partial(lambda i, b: (b, i, 0), b=b))
    row = pl.BlockSpec((SGU_BLOCK, D_MODEL), lambda i: (i, 0))
    whole = lambda s: pl.BlockSpec(s, lambda i: (0,) * len(s))
    vec = (1, D_MODEL)
    return pl.pallas_call(
        body,
        grid=(t // SGU_BLOCK,),
        in_specs=[blk(SLOT_U), blk(SLOT_V), row, pl.BlockSpec(memory_space=pl.ANY),
                  whole(wm.shape), whole(wm_t.shape), whole(bs_t.shape), whole(vec), whole(vec)],
        out_specs=[pl.BlockSpec((2, SGU_BLOCK, D_MODEL), lambda i: (SLOT_U // 2, i, 0)),
                   whole(wm.shape), whole((SGU_BLOCK, LANE)), whole(vec), whole(vec)],
        out_shape=[jax.ShapeDtypeStruct(dh.shape, BF16),
                   jax.ShapeDtypeStruct(wm.shape, F32), jax.ShapeDtypeStruct((SGU_BLOCK, LANE), F32),
                   jax.ShapeDtypeStruct(vec, F32), jax.ShapeDtypeStruct(vec, F32)],
        scratch_shapes=[pltpu.VMEM((SGU_BLOCK, D_MODEL), F32)],
        input_output_aliases={3: 0},
        compiler_params=_params(("arbitrary",)),
        name="sgu_bwd",
    )(h, h, dy, dh, wm, wm_t, bs_t, g_v, b_v)


def _split3(x):
    hi = x.astype(BF16)
    r1 = x - hi.astype(F32)
    mid = r1.astype(BF16)
    lo = (r1 - mid.astype(F32)).astype(BF16)
    return hi, mid, lo


def _tri_matmul(tri, x):
    hi, mid, lo = _split3(x)
    dot = lambda p: jnp.dot(tri, p, preferred_element_type=F32)
    return dot(hi) + dot(mid) + dot(lo)


def _lower_bound(logits):
    l0, l1 = logits[0:1, :], logits[1:2, :]
    mx = jnp.maximum(l0, l1)
    e0, e1 = jnp.exp(l0 - mx), jnp.exp(l1 - mx)
    return e0 / (e0 + e1)


def _hgrn_gates(q_raw, f_raw, lb):
    q = q_raw * jax.nn.sigmoid(q_raw)
    sig = jax.nn.sigmoid(f_raw)
    f = lb + (1.0 - lb) * sig
    row = lax.broadcasted_iota(jnp.int32, (CHUNK, CHUNK), 0)
    col = lax.broadcasted_iota(jnp.int32, (CHUNK, CHUNK), 1)
    c = _tri_matmul((row >= col).astype(BF16), jnp.log(f))
    return q, sig, f, 1.0 - f, c


def _offdiag_terms(qh, kh, ch, tb):
    rows = slice(tb * SUB, (tb + 1) * SUB)
    r = ch[tb * SUB - 1:tb * SUB, :]
    eqh = jnp.exp(ch[rows] - r)
    ekh = jnp.exp(jnp.minimum(r - ch, 0.0))
    return rows, eqh, qh[rows] * eqh, ekh, kh * ekh


def _diag_decay(cb, s, trow):
    return jnp.exp(jnp.where(trow >= s, cb - cb[s:s + 1, :], NEG))


def _split_refs(refs, n_in, n_out, n_scratch, ex):
    nx = ex.n if ex else 0
    ins, refs = refs[:n_in], refs[n_in:]
    xsrc, refs = refs[:nx], refs[nx:]
    outs, refs = refs[:n_out], refs[n_out:]
    xout, refs = refs[:nx], refs[nx:]
    return ins, outs, refs[:n_scratch], (xsrc, xout, refs[n_scratch:])


def _hgrn_fwd(h, logits, g_norm, ex=None):
    t = h.shape[1]
    nc = t // CHUNK

    def body(*refs):
        ins, outs, scratch, xrefs = _split_refs(refs, 6, 3, 4, ex)
        q_ref, f_ref, i_ref, og_ref, lg_ref, gn_ref = ins
        y_ref, o_ref, sall_ref = outs
        st_ref, q_s, k_s, c_s = scratch

        @pl.when(pl.program_id(0) == 0)
        def _():
            st_ref[...] = jnp.zeros_like(st_ref)
            if ex:
                ex.start(*xrefs)

        if ex:
            @pl.when(pl.program_id(0) == (3 * nc) // 4)
            def _():
                ex.relay(*xrefs)

        lb = _lower_bound(lg_ref[...])
        q, _, _, k, c = _hgrn_gates(q_ref[...], f_ref[...], lb)
        q_s[...] = q
        k_s[...] = k
        c_s[...] = c
        col64 = lax.broadcasted_iota(jnp.int32, (SUB, CHUNK), 1)
        trow = lax.broadcasted_iota(jnp.int32, (SUB, HEAD), 0)

        def head(hd, carry):
            sl = pl.ds(pl.multiple_of(hd * HEAD, HEAD), HEAD)
            qh, kh, ch, ih = q_s[:, sl], k_s[:, sl], c_s[:, sl], i_ref[:, sl]
            st = st_ref[hd]
            sall_ref[hd] = st
            c_last = ch[CHUNK - 1:CHUNK, :]
            o = _dot_nt(qh * jnp.exp(ch), st)
            st_ref[hd] = st * jnp.exp(c_last) + _dot_tn(ih, kh * jnp.exp(c_last - ch))
            a_rows = [jnp.zeros((SUB, CHUNK), F32)]
            for tb in range(1, CHUNK // SUB):
                _, _, q_hat, _, k_hat = _offdiag_terms(qh, kh, ch, tb)
                a_rows.append(jnp.where(col64 < tb * SUB, _dot_nt(q_hat, k_hat), 0.0))
            o = o + _dot(jnp.concatenate(a_rows, axis=0), ih)
            o_rows = []
            for b in range(CHUNK // SUB):
                rows = slice(b * SUB, (b + 1) * SUB)
                qb, cb, kb, ib = qh[rows], ch[rows], kh[rows], ih[rows]
                ob = jnp.zeros((SUB, HEAD), F32)
                for s in range(SUB):
                    a = jnp.sum(qb * _diag_decay(cb, s, trow) * kb[s:s + 1, :], axis=1, keepdims=True)
                    ob = ob + a * ib[s:s + 1, :]
                o_rows.append(ob)
            o = o + jnp.concatenate(o_rows, axis=0)
            o_ref[:, sl] = o
            og = og_ref[:, sl]
            on = o * lax.rsqrt(_mean(o * o) + RMS_EPS)
            y_ref[:, sl] = (on * gn_ref[:, sl] * (og * jax.nn.sigmoid(og))).astype(BF16)
            return carry

        lax.fori_loop(0, HEADS, head, 0, unroll=HEAD_UNROLL)

        if ex:
            @pl.when(pl.program_id(0) == nc - 1)
            def _():
                ex.finish(*xrefs)

    blk = lambda b: pl.BlockSpec((None, CHUNK, D_MODEL), functools.partial(lambda n, b: (b, n, 0), b=b))
    row = pl.BlockSpec((CHUNK, D_MODEL), lambda n: (n, 0))
    whole = lambda s: pl.BlockSpec(s, lambda n: (0,) * len(s))
    res = pl.pallas_call(
        body,
        grid=(nc,),
        in_specs=[blk(SLOT_Q), blk(SLOT_F), blk(SLOT_I), blk(SLOT_OG), whole(logits.shape), whole(g_norm.shape)]
        + (ex.in_specs if ex else []),
        out_specs=[row, row, pl.BlockSpec((None, HEADS, HEAD, HEAD), lambda n: (n, 0, 0, 0))] + (ex.out_specs if ex else []),
        out_shape=[jax.ShapeDtypeStruct((t, D_MODEL), BF16), jax.ShapeDtypeStruct((t, D_MODEL), F32),
                   jax.ShapeDtypeStruct((nc, HEADS, HEAD, HEAD), F32)] + (ex.out_shape if ex else []),
        scratch_shapes=[pltpu.VMEM((HEADS, HEAD, HEAD), F32)] + [pltpu.VMEM((CHUNK, D_MODEL), F32)] * 3
        + (ex.scratch if ex else []),
        compiler_params=_params(("arbitrary",)),
        name="hgrn_fwd",
    )(h, h, h, h, logits, g_norm, *(ex.arrays if ex else []))
    return res[0], res[1], res[2], res[3:]


def _hgrn_bwd(h, o_all, dy, states, dh, logits, g_norm, ex=None):
    t = h.shape[1]
    nc = t // CHUNK

    def body(*refs):
        ins, outs, scratch, xrefs = _split_refs(refs, 10, 3, 8, ex)
        q_ref, f_ref, i_ref, og_ref, o_ref, dy_ref, sall_ref, _, lg_ref, gn_ref = ins
        dqfio_ref, dlg_ref, dgn_ref = outs
        dst_ref, dlb_ref, q_s, k_s, c_s, dq_s, dk_s, dc_s = scratch
        dq_ref, df_ref, di_ref, dog_ref = (dqfio_ref.at[s] for s in (SLOT_Q, SLOT_F, SLOT_I, SLOT_OG))
        n = pl.program_id(0)

        @pl.when(n == 0)
        def _():
            dst_ref[...] = jnp.zeros_like(dst_ref)
            dlb_ref[...] = jnp.zeros_like(dlb_ref)
            dgn_ref[...] = jnp.zeros_like(dgn_ref)
            if ex:
                ex.start(*xrefs)

        lb = _lower_bound(lg_ref[...])
        q_raw = q_ref[...]
        q, sig, f, k, c = _hgrn_gates(q_raw, f_ref[...], lb)
        q_s[...] = q
        k_s[...] = k
        c_s[...] = c
        col64 = lax.broadcasted_iota(jnp.int32, (SUB, CHUNK), 1)
        trow = lax.broadcasted_iota(jnp.int32, (SUB, HEAD), 0)
        row64 = lax.broadcasted_iota(jnp.int32, (CHUNK, HEAD), 0)

        def head(hd, carry):
            sl = pl.ds(pl.multiple_of(hd * HEAD, HEAD), HEAD)
            qh, kh, ch, ih = q_s[:, sl], k_s[:, sl], c_s[:, sl], i_ref[:, sl]
            st = sall_ref[hd]
            dst = dst_ref[hd]
            oh, dyh, og, gn = o_ref[:, sl], dy_ref[:, sl], og_ref[:, sl], gn_ref[:, sl]
            sg = jax.nn.sigmoid(og)
            sil = og * sg
            rms = lax.rsqrt(_mean(oh * oh) + RMS_EPS)
            on = oh * rms
            dog_ref[:, sl] = (dyh * on * gn * _silu_grad(og, sg)).astype(BF16)
            dgn_ref[:, sl] += _sum0(dyh * on * sil)
            don = dyh * gn * sil
            do = rms * (don - on * _mean(don * on))
            dob = do.astype(BF16)

            c_last = ch[CHUNK - 1:CHUNK, :]
            eq = jnp.exp(ch)
            q_til = qh * eq
            ekl = jnp.exp(c_last - ch)
            k_til = kh * ekl
            ecl = jnp.exp(c_last)
            dq_til = _dot(dob, st)
            dk_til = _dot(ih, dst)
            di = _dot_nt(k_til, dst)
            dc_last = _sum0(dk_til * k_til) + _sum0(dst * st) * ecl
            dst_ref[hd] = _dot_tn(dob, q_til) + dst * ecl
            dq = dq_til * eq
            dc = dq_til * q_til - dk_til * k_til
            dk = dk_til * ekl

            da_full = _dot_nt(dob, ih)
            a_rows = [jnp.zeros((SUB, CHUNK), F32)]
            dq_rows = [jnp.zeros((SUB, HEAD), F32)]
            dc_rows = [jnp.zeros((SUB, HEAD), F32)]
            for tb in range(1, CHUNK // SUB):
                rows, eqh, q_hat, ekh, k_hat = _offdiag_terms(qh, kh, ch, tb)
                keep = col64 < tb * SUB
                a_rows.append(jnp.where(keep, _dot_nt(q_hat, k_hat), 0.0))
                da = jnp.where(keep, da_full[rows], 0.0)
                dq_hat = _dot(da, k_hat)
                dk_hat = _dot_tn(da, q_hat)
                dq_rows.append(dq_hat * eqh)
                dc_rows.append(dq_hat * q_hat)
                dk = dk + dk_hat * ekh
                dc = dc - dk_hat * k_hat
            di = di + _dot_tn(jnp.concatenate(a_rows, axis=0), dob)

            dk_rows, di_rows = [], []
            for b in range(CHUNK // SUB):
                rows = slice(b * SUB, (b + 1) * SUB)
                qb, cb, kb, ib, dob_ = qh[rows], ch[rows], kh[rows], ih[rows], do[rows]
                dq_diag = jnp.zeros((SUB, HEAD), F32)
                dk_diag = jnp.zeros((SUB, HEAD), F32)
                di_diag = jnp.zeros((SUB, HEAD), F32)
                for s in range(SUB):
                    ks = kb[s:s + 1, :]
                    dec = _diag_decay(cb, s, trow)
                    a = jnp.sum(qb * dec * ks, axis=1, keepdims=True)
                    gk = jnp.sum(dob_ * ib[s:s + 1, :], axis=1, keepdims=True) * dec
                    dq_diag = dq_diag + gk * ks
                    dk_diag = dk_diag + jnp.where(trow == s, _sum0(gk * qb), 0.0)
                    di_diag = di_diag + jnp.where(trow == s, _sum0(a * dob_), 0.0)
                dq_rows[b] = dq_rows[b] + dq_diag
                dc_rows[b] = dc_rows[b] + qb * dq_diag - kb * dk_diag
                dk_rows.append(dk_diag)
                di_rows.append(di_diag)
            dq = dq + jnp.concatenate(dq_rows, axis=0)
            dk = dk + jnp.concatenate(dk_rows, axis=0)
            dc = dc + jnp.concatenate(dc_rows, axis=0) + jnp.where(row64 == CHUNK - 1, dc_last, 0.0)
            di_ref[:, sl] = (di + jnp.concatenate(di_rows, axis=0)).astype(BF16)
            dq_s[:, sl] = dq
            dk_s[:, sl] = dk
            dc_s[:, sl] = dc
            return carry

        lax.fori_loop(0, HEADS, head, 0, unroll=HEAD_UNROLL)

        row = lax.broadcasted_iota(jnp.int32, (CHUNK, CHUNK), 0)
        col = lax.broadcasted_iota(jnp.int32, (CHUNK, CHUNK), 1)
        dlf = _tri_matmul((row <= col).astype(BF16), dc_s[...])
        df = dlf / f - dk_s[...]
        dlb_ref[...] += _sum0(df * (1.0 - sig))
        df_ref[...] = (df * (1.0 - lb) * sig * (1.0 - sig)).astype(BF16)
        dq_ref[...] = (dq_s[...] * _silu_grad(q_raw, jax.nn.sigmoid(q_raw))).astype(BF16)

        @pl.when(n == nc - 1)
        def _():
            d0 = dlb_ref[...] * lb * (1.0 - lb)
            dlg_ref[0:1, :] = d0
            dlg_ref[1:2, :] = -d0
            if ex:
                ex.finish(*xrefs)

    rev = lambda n: nc - 1 - n
    blk = lambda b: pl.BlockSpec((None, CHUNK, D_MODEL), functools.partial(lambda n, b: (b, rev(n), 0), b=b))
    row = pl.BlockSpec((CHUNK, D_MODEL), lambda n: (rev(n), 0))
    whole = lambda s: pl.BlockSpec(s, lambda n: (0,) * len(s))
    vec = (1, D_MODEL)
    res = pl.pallas_call(
        body,
        grid=(nc,),
        in_specs=[blk(SLOT_Q), blk(SLOT_F), blk(SLOT_I), blk(SLOT_OG), row, row,
                  pl.BlockSpec((None, HEADS, HEAD, HEAD), lambda n: (rev(n), 0, 0, 0)),
                  pl.BlockSpec(memory_space=pl.ANY), whole(logits.shape), whole(vec)] + (ex.in_specs if ex else []),
        out_specs=[pl.BlockSpec((4, CHUNK, D_MODEL), lambda n: (SLOT_Q // 4, rev(n), 0)), whole((2, D_MODEL)), whole(vec)]
        + (ex.out_specs if ex else []),
        out_shape=[jax.ShapeDtypeStruct(dh.shape, BF16), jax.ShapeDtypeStruct((2, D_MODEL), F32),
                   jax.ShapeDtypeStruct(vec, F32)] + (ex.out_shape if ex else []),
        scratch_shapes=[pltpu.VMEM((HEADS, HEAD, HEAD), F32), pltpu.VMEM(vec, F32)]
        + [pltpu.VMEM((CHUNK, D_MODEL), F32)] * 6 + (ex.scratch if ex else []),
        input_output_aliases={7: 0},
        compiler_params=_params(("arbitrary",)),
        name="hgrn_bwd",
    )(h, h, h, h, o_all, dy, states, dh, logits, g_norm, *(ex.arrays if ex else []))
    return res[0], res[1], res[2], res[3:]


def _merge_fwd(i, n, ga, gb, za, zb):
    return jax.nn.sigmoid(ga) * za + jax.nn.sigmoid(gb) * zb


def _merge_bwd(i, n, ga, gb, za, zb, dm):
    sa, sb = jax.nn.sigmoid(ga), jax.nn.sigmoid(gb)
    dgates = jnp.stack([(dm * za * sa * (1.0 - sa)).astype(BF16), (dm * zb * sb * (1.0 - sb)).astype(BF16)])
    return dgates, dm * sa, dm * sb


def _ln1_fwd(i, n, x, r1, g, b):
    xhat, _ = _ln_stats(ALPHA * x + r1)
    x1 = xhat * g + b
    return x1, x1


def _ln1_bwd(i, n, x, r1, dx1, g):
    xhat, rstd = _ln_stats(ALPHA * x + r1)
    dz = _ln_bwd(dx1, xhat, rstd, g)
    return dz, dz, _sum0(dx1 * xhat), _sum0(dx1)


def _ln2_loss(i, n, x1, fo, pg, pp, tgt, g, b):
    sg = jax.nn.sigmoid(pg)
    xhat, rstd = _ln_stats(ALPHA * x1 + fo + sg * pp)
    diff = xhat * g + b - tgt
    loss = 0.5 * jnp.sum(_mean(diff * diff), axis=0, keepdims=True)
    dy = diff * (1.0 / D_MODEL)
    dz = _ln_bwd(dy, xhat, rstd, g)
    return (dz, dz, dz * pp * sg * (1.0 - sg), dz * sg,
            jnp.broadcast_to(loss, (8, LANE)), _sum0(dy * xhat), _sum0(dy))


def _shift_down(cur, halo, tile):
    row = lax.broadcasted_iota(jnp.int32, cur.shape, 0)
    m1 = jnp.where(row == 0, halo[7:8, :], pltpu.roll(cur, 1, 0))
    m2 = jnp.where(row == 0, halo[6:7, :], jnp.where(row == 1, halo[7:8, :], pltpu.roll(cur, 2, 0)))
    return m1, m2


def _shift_up(cur, halo, tile):
    row = lax.broadcasted_iota(jnp.int32, cur.shape, 0)
    p1 = jnp.where(row == tile - 1, halo[0:1, :], pltpu.roll(cur, tile - 1, 0))
    p2 = jnp.where(row == tile - 2, halo[0:1, :], jnp.where(row == tile - 1, halo[1:2, :], pltpu.roll(cur, tile - 2, 0)))
    return p1, p2


def _conv_pre(i, gate, halo, w, b, tile):
    halo = jnp.where(i == 0, 0.0, halo)
    m1, m2 = _shift_down(gate, halo, tile)
    return w[0:1, :] * m2 + w[1:2, :] * m1 + w[2:3, :] * gate + b, m1, m2


def _conv_fwd(tile, i, n, gate, halo, val, w, b):
    cg, _, _ = _conv_pre(i, gate, halo, w, b, tile)
    return _gelu(cg) * val


def _conv_bwd_a(tile, i, n, gate, halo, val, dhid, w, b):
    cg, m1, m2 = _conv_pre(i, gate, halo, w, b, tile)
    dcg = dhid * val * _gelu_grad(cg)
    return dcg, dhid * _gelu(cg), _sum0(dcg * m2), _sum0(dcg * m1), _sum0(dcg * gate), _sum0(dcg)


def _conv_bwd_b(tile, i, n, dcg, halo, w):
    dcg = dcg.astype(F32)
    halo = jnp.where(i == n - 1, 0.0, halo.astype(F32))
    p1, p2 = _shift_up(dcg, halo, tile)
    return w[2:3, :] * dcg + w[1:2, :] * p1 + w[0:1, :] * p2


def _halo_spec(width, tile, t, nxt, rows=8):
    per = tile // rows
    last = t // rows - 1
    if nxt:
        return pl.BlockSpec((rows, width), lambda i: (jnp.minimum((i + 1) * per, last), 0))
    return pl.BlockSpec((rows, width), lambda i: (jnp.maximum(i * per - 1, 0), 0))


def _adamw(i, n, w, m, v, parts):
    g = parts[0].astype(F32)
    for j in range(1, parts.shape[0]):
        g = g + parts[j].astype(F32)
    m_new = ADAM_B1 * m + (1.0 - ADAM_B1) * g
    v_new = ADAM_B2 * v + (1.0 - ADAM_B2) * (g * g)
    m_hat = m_new / (1.0 - ADAM_B1 ** ADAM_STEP)
    v_hat = v_new / (1.0 - ADAM_B2 ** ADAM_STEP)
    delta = -ADAM_LR * (m_hat / (jnp.sqrt(v_hat) + ADAM_EPS) + ADAM_WD * w)
    return g, delta, m_new, v_new


def _adam_call(w, m, v, parts, name):
    r, c = w.shape
    tile = _pick(r, (256, 128)) if r > 256 else r
    spec = pl.BlockSpec((parts.shape[0], tile, c), lambda i: (0, i, 0))
    return _rowwise(_adamw, [w, m, v, (parts, spec)], [], [(c, F32)] * 4, [], tile=tile, name=name)


def _peer(k):
    x, y, c = lax.axis_index("x"), lax.axis_index("y"), lax.axis_index("c")
    px = x ^ ((k >> 2) & 1)
    py = y ^ ((k >> 1) & 1)
    pc = c ^ (k & 1)
    return (px, py, pc), 4 * px + 2 * py + pc


def _my_index():
    return 4 * lax.axis_index("x") + 2 * lax.axis_index("y") + lax.axis_index("c")


class _Exchange:
    KINDS = ("gather", "gather+relay", "scatter", "scatter+pairs")

    def __init__(self, entries):
        assert all(k in self.KINDS for _, k in entries), [k for _, k in entries]
        self.arrays = [a for a, _ in entries]
        self.scatter = [k.startswith("scatter") for _, k in entries]
        self.relayed = ["+relay" in k for _, k in entries]
        self.pairs = ["+pairs" in k for _, k in entries]
        self.n = len(entries)
        self.in_specs = [pl.BlockSpec(memory_space=pl.ANY)] * self.n
        self.out_specs = [pl.BlockSpec(memory_space=pl.ANY)] * self.n
        shapes = [tuple(a.shape[1:]) if sc else tuple(a.shape) for a, sc in zip(self.arrays, self.scatter)]
        counts = [N_CHIP if p else N_DEV for p in self.pairs]
        self.out_shape = [jax.ShapeDtypeStruct((n,) + s, a.dtype) for n, s, a in zip(counts, shapes, self.arrays)]
        per = N_DEV - 1
        self.scratch = [pltpu.SemaphoreType.DMA((self.n * per,)), pltpu.SemaphoreType.DMA((self.n * per,)),
                        pltpu.SemaphoreType.DMA((self.n,))]

    def _copies(self, srcs, outs, sems):
        send_sems, recv_sems, local_sems = sems
        x, y, c = lax.axis_index("x"), lax.axis_index("y"), lax.axis_index("c")
        me = _my_index()
        per = N_DEV - 1
        local, first, passed, relay_arrivals, arrivals = [], [], [], [], []
        for a in range(self.n):

            def copy(k, src, dst, dev, a=a):
                return pltpu.make_async_remote_copy(
                    src_ref=src, dst_ref=dst, send_sem=send_sems.at[a * per + k], recv_sem=recv_sems.at[a * per + k],
                    device_id=dev, device_id_type=pl.DeviceIdType.MESH)

            if self.pairs[a]:
                chip = 2 * x + y
                for k in range(N_CHIP):
                    to = chip ^ k
                    piece = srcs[a].at[_slot_of_group(2 * to + c) // 2]
                    if k == 0:
                        local.append(pltpu.make_async_copy(piece, outs[a].at[chip], local_sems.at[a]))
                    else:
                        dev = (to // 2, to % 2, c)
                        first.append(copy(k - 1, piece, outs[a].at[chip], dev))
                        arrivals.append(copy(k - 1, piece, outs[a].at[to], dev))
                continue
            mine = srcs[a].at[me] if self.scatter[a] else srcs[a]
            land = outs[a].at[me]
            local.append(pltpu.make_async_copy(mine, land, local_sems.at[a]))
            if self.relayed[a]:
                block = lambda px, py, pc, a=a: outs[a].at[4 * px + 2 * py + pc]
                chips = [(1 - x, y), (x, 1 - y), (1 - x, 1 - y)]
                first.append(copy(0, mine, land, (x, y, 1 - c)))
                arrivals.append(copy(0, mine, block(x, y, 1 - c), (x, y, 1 - c)))
                for j, (px, py) in enumerate(chips):
                    first.append(copy(1 + j, mine, land, (px, py, c)))
                    relay_arrivals.append(copy(1 + j, mine, block(px, py, c), (px, py, c)))
                    passed.append(copy(4 + j, block(px, py, c), block(px, py, c), (x, y, 1 - c)))
                    arrivals.append(copy(4 + j, mine, block(px, py, 1 - c), (x, y, 1 - c)))
                continue
            for k in range(1, N_DEV):
                dev, idx = _peer(k)
                if self.scatter[a]:
                    first.append(copy(k - 1, srcs[a].at[idx], land, dev))
                else:
                    first.append(copy(k - 1, mine, land, dev))
                arrivals.append(copy(k - 1, mine, outs[a].at[idx], dev))
        return local, first, passed, relay_arrivals, arrivals

    def start(self, srcs, outs, sems):
        local, first, _, _, _ = self._copies(srcs, outs, sems)
        for cp in local + first:
            cp.start()

    def relay(self, srcs, outs, sems):
        _, _, passed, relay_arrivals, _ = self._copies(srcs, outs, sems)
        for landed, onward in zip(relay_arrivals, passed):
            landed.wait_recv()
            onward.start()

    def finish(self, srcs, outs, sems):
        local, first, passed, _, arrivals = self._copies(srcs, outs, sems)
        for cp in arrivals:
            cp.wait_recv()
        for cp in first + passed:
            cp.wait_send()
        for cp in local:
            cp.wait()


def _gather_project(x_b, shard):
    t, d = x_b.shape
    tm = _pick(t, MM_TILES)
    nrow = t // tm
    per = N_DEV - 1

    def parties():
        x, y, c = lax.axis_index("x"), lax.axis_index("y"), lax.axis_index("c")
        chips = [(1 - x, y), (x, 1 - y), (1 - x, 1 - y)]
        return (x, y, c), (x, y, 1 - c), [(px, py, c) for px, py in chips], [(px, py, 1 - c) for px, py in chips]

    slot = lambda dev: _slot_of_group(4 * dev[0] + 2 * dev[1] + dev[2])
    ici_step, passed_step = (2, 3, 6), (4, 5, 7)
    me, sibling, over_ici, passed_on = parties()
    by_step = {0: me, 1: sibling, **dict(zip(ici_step, over_ici)), **dict(zip(passed_step, passed_on))}
    order = jnp.stack([slot(by_step[j]) for j in range(N_DEV)]).astype(jnp.int32)

    def body(order_ref, x_ref, shard_ref, h_ref, wall_ref, wbuf, fetch_sem, send_sems, recv_sems, local_sem):
        del order_ref
        me, sibling, over_ici, passed_on = parties()
        j, i = pl.program_id(0), pl.program_id(1)
        land = lambda dev: wall_ref.at[slot(dev)]

        def copy(k, src, block, to):
            return pltpu.make_async_remote_copy(src_ref=src, dst_ref=land(block), send_sem=send_sems.at[k],
                                                recv_sem=recv_sems.at[k], device_id=to, device_id_type=pl.DeviceIdType.MESH)

        def fetch(src):
            cp = pltpu.make_async_copy(src, wbuf, fetch_sem)
            cp.start()
            cp.wait()

        keep = pltpu.make_async_copy(shard_ref, land(me), local_sem)
        first = [copy(0, shard_ref, me, sibling)] + [copy(1 + n, shard_ref, me, dev) for n, dev in enumerate(over_ici)]
        onward = [copy(4 + n, land(dev), dev, sibling) for n, dev in enumerate(over_ici)]

        @pl.when(jnp.logical_and(i == 0, j == 0))
        def _():
            keep.start()
            for cp in first:
                cp.start()
            fetch(shard_ref)

        @pl.when(jnp.logical_and(i == 0, j == 1))
        def _():
            copy(0, shard_ref, sibling, me).wait_recv()
            fetch(land(sibling))

        for n, dev in enumerate(over_ici):
            @pl.when(jnp.logical_and(i == 0, j == ici_step[n]))
            def _(n=n, dev=dev):
                copy(1 + n, shard_ref, dev, me).wait_recv()
                onward[n].start()
                fetch(land(dev))

        for n, dev in enumerate(passed_on):
            @pl.when(jnp.logical_and(i == 0, j == passed_step[n]))
            def _(n=n, dev=dev):
                copy(4 + n, shard_ref, dev, me).wait_recv()
                fetch(land(dev))

        h_ref[...] = _dot(x_ref[...], wbuf[...])

        @pl.when(jnp.logical_and(i == nrow - 1, j == N_DEV - 1))
        def _():
            for cp in first + onward:
                cp.wait_send()
            keep.wait()

    need = 2 * (tm * d * 2 + tm * d * 4) + d * d * 2 + tm * d * 4
    h, w_all = pl.pallas_call(
        body,
        grid_spec=pltpu.PrefetchScalarGridSpec(
            num_scalar_prefetch=1,
            grid=(N_DEV, nrow),
            in_specs=[pl.BlockSpec((tm, d), lambda j, i, order: (i, 0)), pl.BlockSpec(memory_space=pl.ANY)],
            out_specs=[pl.BlockSpec((None, tm, d), lambda j, i, order: (order[j], i, 0)), pl.BlockSpec(memory_space=pl.ANY)],
            scratch_shapes=[pltpu.VMEM((d, d), BF16), pltpu.SemaphoreType.DMA, pltpu.SemaphoreType.DMA((per,)),
                            pltpu.SemaphoreType.DMA((per,)), pltpu.SemaphoreType.DMA],
        ),
        out_shape=[jax.ShapeDtypeStruct((N_DEV, t, d), F32), jax.ShapeDtypeStruct((N_DEV, d, d), BF16)],
        compiler_params=_params(("arbitrary", "arbitrary"), need),
        name="gather_project",
    )(order, x_b, shard)
    return h, w_all


def _pair_sums(g):
    n, r, cols = g.shape
    half = n // 2

    def swap(g_ref, got_ref, send_sems, recv_sems):
        x, y, c = lax.axis_index("x"), lax.axis_index("y"), lax.axis_index("c")
        copies = [pltpu.make_async_remote_copy(
            src_ref=g_ref.at[2 * j + 1 - c], dst_ref=got_ref.at[j], send_sem=send_sems.at[j], recv_sem=recv_sems.at[j],
            device_id=(x, y, 1 - c), device_id_type=pl.DeviceIdType.MESH) for j in range(half)]
        for cp in copies:
            cp.start()
        for cp in copies:
            cp.wait()

    got = pl.pallas_call(
        swap,
        in_specs=[pl.BlockSpec(memory_space=pl.ANY)],
        out_specs=pl.BlockSpec(memory_space=pl.ANY),
        out_shape=jax.ShapeDtypeStruct((half, r, cols), g.dtype),
        scratch_shapes=[pltpu.SemaphoreType.DMA((half,))] * 2,
        name="pair_swap",
    )(g)

    def add(mine_ref, got_ref, out_ref):
        mine = jnp.where(lax.axis_index("c") == 0, mine_ref[0].astype(F32), mine_ref[1].astype(F32))
        out_ref[...] = (mine + got_ref[...].astype(F32)).astype(out_ref.dtype)

    tile = _pick(r, (512, 256, 128))
    return pl.pallas_call(
        add,
        grid=(half, r // tile),
        in_specs=[pl.BlockSpec((None, 2, tile, cols), lambda j, i: (j, 0, i, 0)),
                  pl.BlockSpec((None, tile, cols), lambda j, i: (j, i, 0))],
        out_specs=pl.BlockSpec((None, tile, cols), lambda j, i: (j, i, 0)),
        out_shape=jax.ShapeDtypeStruct((half, r, cols), g.dtype),
        compiler_params=_params(("parallel", "parallel")),
        name="pair_add",
    )(g.reshape(half, 2, r, cols), got)


def _local_step(x, p, tgt, small, comm):
    t = x.shape[0]
    tile = _pick(t, (256, 128))
    d = D_MODEL
    act_b, act_f = (d, BF16), (d, F32)
    x_b, p_b = x.astype(BF16), p.astype(BF16)

    chunk_id = jnp.arange(SGU_BLOCK) // CHUNK
    mask = chunk_id[:, None] >= chunk_id[None, :]
    wm = jnp.where(mask[None], small["sgu_w_s"], 0.0)
    wm_b = wm.astype(BF16)
    wm_t = jnp.swapaxes(wm, 1, 2).astype(BF16)
    bs_t = small["sgu_b_s"].T

    h, w_in = comm.project_in(x_b)
    y_a, got_a = _sgu_fwd(h, wm_b, bs_t, small["sgu_norm_g"], small["sgu_norm_b"], ex=comm.weights_exchange(0))
    y_b, o_all, states, got_b = _hgrn_fwd(h, small["lb_logits"], small["hgrn_norm_g"], ex=comm.weights_exchange(1))
    wts, conv_w = comm.weights(list(got_a) + list(got_b))
    z_a = _mm(y_a, wts["w_a"], out_dtype=F32, name="mm_za")
    z_b = _mm(y_b, wts["w_b"], out_dtype=F32, name="mm_zb")
    gates = [(h, SLOT_GA), (h, SLOT_GB)]
    merged, = _rowwise(_merge_fwd, gates + [z_a, z_b], [], [act_b], [], tile=tile, name="merge_fwd")
    r1 = _mm(merged, wts["w_o"], out_dtype=F32, name="mm_r1")
    x1, x1_b = _rowwise(_ln1_fwd, [x, r1], [small["ln1_g"], small["ln1_b"]], [act_f, act_b], [], tile=tile, name="ln1_fwd")
    gate = _mm(x1_b, wts["w_g"], out_dtype=F32, name="mm_gate")
    val = _mm(x1_b, wts["w_v"], out_dtype=F32, name="mm_val")
    pg = _mm(x1_b, wts["w_pg"], out_dtype=F32, name="mm_pg")
    pp = _mm(p_b, wts["w_pp"], out_dtype=F32, name="mm_pp")
    hid, = _rowwise(functools.partial(_conv_fwd, tile), [gate, (gate, _halo_spec(D_FF, tile, t, False)), val],
                    [conv_w, small["conv_b"]], [(D_FF, BF16)], [], tile=tile, name="conv_fwd")
    fo = _mm(hid, wts["w_down"], out_dtype=F32, name="mm_down")
    dz2, dz2_b, dpg, dpp, loss, dg2, db2 = _rowwise(
        _ln2_loss, [x1, fo, pg, pp, tgt], [small["ln2_g"], small["ln2_b"]],
        [act_f, act_b, act_b, act_b], [(8, LANE), (1, d), (1, d)], tile=tile, name="ln2_loss")

    dhid = _mm(dz2_b, wts["w_down"], out_dtype=BF16, name="mm_dhid", trans_b=True)
    g_down = _mm_tn(hid, dz2_b, out_dtype=BF16, name="mm_g_down")
    dcg, dval, dcw0, dcw1, dcw2, dcb = _rowwise(
        functools.partial(_conv_bwd_a, tile), [gate, (gate, _halo_spec(D_FF, tile, t, False)), val, dhid],
        [conv_w, small["conv_b"]], [(D_FF, BF16), (D_FF, BF16)], [(1, D_FF)] * 4, tile=tile, name="conv_bwd_a")
    dgate, = _rowwise(functools.partial(_conv_bwd_b, tile), [dcg, (dcg, _halo_spec(D_FF, tile, t, True, rows=16))],
                      [conv_w], [(D_FF, BF16)], [], tile=tile, name="conv_bwd_b")
    g_g = _mm_tn(x1_b, dgate, out_dtype=BF16, name="mm_g_gate")
    g_v = _mm_tn(x1_b, dval, out_dtype=BF16, name="mm_g_val")
    g_pg = _mm_tn(x1_b, dpg, out_dtype=BF16, name="mm_g_pg")
    g_pp = _mm_tn(p_b, dpp, out_dtype=BF16, name="mm_g_pp")
    dx1 = _mm(dgate, wts["w_g"], out_dtype=F32, name="mm_dx1_gate", trans_b=True, adds=[(dz2, ALPHA)])
    dx1 = _mm(dval, wts["w_v"], out_dtype=F32, name="mm_dx1_val", trans_b=True, adds=[(dx1, 1.0)])
    dx1 = _mm(dpg, wts["w_pg"], out_dtype=F32, name="mm_dx1_pg", trans_b=True, adds=[(dx1, 1.0)])
    dz1, dz1_b, dg1, db1 = _rowwise(_ln1_bwd, [x, r1, dx1], [small["ln1_g"]], [act_f, act_b], [(1, d), (1, d)],
                                    tile=tile, name="ln1_bwd")
    g_o = _mm_tn(merged, dz1_b, out_dtype=BF16, name="mm_g_o")
    dm = _mm(dz1_b, wts["w_o"], out_dtype=BF16, name="mm_dm", trans_b=True)
    dh, dza, dzb = _rowwise(_merge_bwd, gates + [z_a, z_b, dm], [],
                            [("stack", 2, SLOT_GA // 2, 8, d, BF16), act_b, act_b], [], tile=tile, name="merge_bwd")
    g_a = _mm_tn(y_a, dza, out_dtype=BF16, name="mm_g_a")
    g_b = _mm_tn(y_b, dzb, out_dtype=BF16, name="mm_g_b")
    dy_a = _mm(dza, wts["w_a"], out_dtype=BF16, name="mm_dya", trans_b=True)
    dy_b = _mm(dzb, wts["w_b"], out_dtype=F32, name="mm_dyb", trans_b=True)
    dh, dws, dbs, dgv_n, dbv_n = _sgu_bwd(h, dy_a, dh, wm_b, wm_t, bs_t, small["sgu_norm_g"], small["sgu_norm_b"])
    big = dict(w_a=g_a, w_b=g_b, w_o=g_o, w_g=g_g, w_v=g_v, w_down=g_down, w_pp=g_pp, w_pg=g_pg)
    sm = dict(sgu_w_s=jnp.where(mask[None], dws, 0.0), sgu_b_s=dbs[:, :GROUPS].T, sgu_norm_g=dgv_n, sgu_norm_b=dbv_n,
              ln1_g=dg1, ln1_b=db1, conv_w=jnp.concatenate([dcw0, dcw1, dcw2], axis=0), conv_b=dcb, ln2_g=dg2, ln2_b=db2,
              loss=loss)
    dh, dlogits, dgn, got = _hgrn_bwd(h, o_all, dy_b, states, dh, small["lb_logits"], small["hgrn_norm_g"],
                                      ex=comm.grads_exchange(big, sm))
    comm.grads_done(got)
    g_in = _mm_tn(x_b, dh, out_dtype=BF16, name="mm_g_in")
    ex = comm.last_exchange(g_in, dict(lb_logits=dlogits, hgrn_norm_g=dgn))
    res = _mm(dh, w_in, out_dtype=F32, name="mm_dx", trans_b=True, reduce_b=True, adds=[(dz1, ALPHA)], ex=ex)
    grad_x, got = res if ex else (res, ())
    comm.last_done(got)
    return grad_x


_SMALL_EARLY = ["sgu_w_s", "sgu_b_s", "sgu_norm_g", "sgu_norm_b", "ln1_g", "ln1_b", "ffn_conv_b", "ln2_g", "ln2_b"]
_SMALL_LATE = ["hgrn_lb_logits", "hgrn_norm_g"]
N_TAPS = D_FF // N_DEV
UP_COLS = 2 * D_FF // N_DEV


class _StepExchanges:
    def __init__(self, w_in_shard, shards):
        self.w_in_shard = w_in_shard
        self.shards = shards

    def project_in(self, x_b):
        return _gather_project(x_b, self.w_in_shard)

    def weights_exchange(self, part):
        return _Exchange([(s, "gather+relay") for s in (self.shards[:2] if part == 0 else self.shards[2:])])

    def weights(self, got):
        d, f = D_MODEL, D_FF
        w_br_g, w_o_g, w_up_g, w_down_g, w_pp_g, w_pg_g, conv_g = got
        w_br = w_br_g.transpose(1, 0, 2, 3).reshape(2, d, d)
        w_up = w_up_g.transpose(1, 0, 2).reshape(d, 2, f).transpose(1, 0, 2)
        wts = dict(w_a=w_br[0], w_b=w_br[1], w_o=w_o_g.reshape(d, d), w_g=w_up[0], w_v=w_up[1],
                   w_down=w_down_g.reshape(f, d), w_pp=w_pp_g.transpose(1, 0, 2).reshape(256, d), w_pg=w_pg_g.reshape(d, d))
        return wts, conv_g.transpose(1, 0, 2).reshape(3, f)

    def grads_exchange(self, big, sm):
        d = D_MODEL
        parts = [jnp.stack([big["w_a"], big["w_b"]]).reshape(2, N_DEV, 128, d).transpose(1, 0, 2, 3),
                 big["w_o"].reshape(N_DEV, 128, d),
                 jnp.concatenate([big["w_g"], big["w_v"]], axis=1).reshape(d, N_DEV, UP_COLS).transpose(1, 0, 2),
                 big["w_down"].reshape(N_DEV, N_TAPS, d),
                 big["w_pp"].reshape(256, N_DEV, 128).transpose(1, 0, 2),
                 big["w_pg"].reshape(N_DEV, 128, d)]
        packed, self.rows_early = _pack([sm[k] for k in ("sgu_w_s", "sgu_b_s", "sgu_norm_g", "sgu_norm_b", "ln1_g", "ln1_b",
                                                         "conv_b", "ln2_g", "ln2_b", "conv_w", "loss")])
        return _Exchange([(a, "scatter") for a in parts] + [(packed, "gather")])

    def grads_done(self, got):
        self.recv, self.small_early = got[:6], got[6]

    def last_exchange(self, g_in, sm):
        packed, self.rows_late = _pack([sm["lb_logits"], sm["hgrn_norm_g"]])
        return _Exchange([(_pair_sums(g_in), "scatter+pairs"), (packed, "gather")])

    def last_done(self, got):
        self.recv_in, self.small_late = got


def _rows128(a):
    flat = a.reshape(-1)
    rows = -(-flat.shape[0] // (8 * LANE)) * 8
    return jnp.pad(flat, (0, rows * LANE - flat.shape[0])).reshape(rows, LANE)


def _pack(parts):
    blocks = [_rows128(a) for a in parts]
    return jnp.concatenate(blocks, axis=0), [b.shape[0] for b in blocks]


def _unpack(packed, shapes, rows):
    out, r0 = [], 0
    for shp, r in zip(shapes, rows):
        n = math.prod(shp)
        out.append(packed[r0:r0 + r].reshape(-1)[:n].reshape(shp))
        r0 += r
    return out


def kernel(x, p, w_in, sgu_w_s, sgu_b_s, sgu_norm_g, sgu_norm_b, hgrn_lb_logits, hgrn_norm_g, w_branch, w_out, ln1_g, ln1_b, ffn_w_up, ffn_conv_w, ffn_conv_b, ffn_w_down, ln2_g, ln2_b, ple_w_proj, ple_w_gate, loss_target, m_w_in, m_sgu_w_s, m_sgu_b_s, m_sgu_norm_g, m_sgu_norm_b, m_hgrn_lb_logits, m_hgrn_norm_g, m_w_branch, m_w_out, m_ln1_g, m_ln1_b, m_ffn_w_up, m_ffn_conv_w, m_ffn_conv_b, m_ffn_w_down, m_ln2_g, m_ln2_b, m_ple_w_proj, m_ple_w_gate, v_w_in, v_sgu_w_s, v_sgu_b_s, v_sgu_norm_g, v_sgu_norm_b, v_hgrn_lb_logits, v_hgrn_norm_g, v_w_branch, v_w_out, v_ln1_g, v_ln1_b, v_ffn_w_up, v_ffn_conv_w, v_ffn_conv_b, v_ffn_w_down, v_ln2_g, v_ln2_b, v_ple_w_proj, v_ple_w_gate):
    weights = dict(w_in=w_in, sgu_w_s=sgu_w_s, sgu_b_s=sgu_b_s, sgu_norm_g=sgu_norm_g, sgu_norm_b=sgu_norm_b,
                   hgrn_lb_logits=hgrn_lb_logits, hgrn_norm_g=hgrn_norm_g, w_branch=w_branch, w_out=w_out,
                   ln1_g=ln1_g, ln1_b=ln1_b, ffn_w_up=ffn_w_up, ffn_conv_w=ffn_conv_w, ffn_conv_b=ffn_conv_b,
                   ffn_w_down=ffn_w_down, ln2_g=ln2_g, ln2_b=ln2_b, ple_w_proj=ple_w_proj, ple_w_gate=ple_w_gate)
    mom_m = dict(w_in=m_w_in, sgu_w_s=m_sgu_w_s, sgu_b_s=m_sgu_b_s, sgu_norm_g=m_sgu_norm_g, sgu_norm_b=m_sgu_norm_b,
                 hgrn_lb_logits=m_hgrn_lb_logits, hgrn_norm_g=m_hgrn_norm_g, w_branch=m_w_branch, w_out=m_w_out,
                 ln1_g=m_ln1_g, ln1_b=m_ln1_b, ffn_w_up=m_ffn_w_up, ffn_conv_w=m_ffn_conv_w, ffn_conv_b=m_ffn_conv_b,
                 ffn_w_down=m_ffn_w_down, ln2_g=m_ln2_g, ln2_b=m_ln2_b, ple_w_proj=m_ple_w_proj, ple_w_gate=m_ple_w_gate)
    mom_v = dict(w_in=v_w_in, sgu_w_s=v_sgu_w_s, sgu_b_s=v_sgu_b_s, sgu_norm_g=v_sgu_norm_g, sgu_norm_b=v_sgu_norm_b,
                 hgrn_lb_logits=v_hgrn_lb_logits, hgrn_norm_g=v_hgrn_norm_g, w_branch=v_w_branch, w_out=v_w_out,
                 ln1_g=v_ln1_g, ln1_b=v_ln1_b, ffn_w_up=v_ffn_w_up, ffn_conv_w=v_ffn_conv_w, ffn_conv_b=v_ffn_conv_b,
                 ffn_w_down=v_ffn_w_down, ln2_g=v_ln2_g, ln2_b=v_ln2_b, ple_w_proj=v_ple_w_proj, ple_w_gate=v_ple_w_gate)
    d, f = D_MODEL, D_FF
    me = _my_index()

    comm = _StepExchanges(w_in[0].astype(BF16),
                          [w_branch[0].astype(BF16), w_out[0].astype(BF16), ffn_w_up[0].astype(BF16),
                           ffn_w_down[0].astype(BF16), ple_w_proj[0].astype(BF16), ple_w_gate[0].astype(BF16), ffn_conv_w[0]])
    small = dict(sgu_w_s=sgu_w_s[0], sgu_b_s=sgu_b_s[0], sgu_norm_g=sgu_norm_g, sgu_norm_b=sgu_norm_b,
                 lb_logits=hgrn_lb_logits, hgrn_norm_g=hgrn_norm_g, ln1_g=ln1_g, ln1_b=ln1_b, ln2_g=ln2_g, ln2_b=ln2_b,
                 conv_b=ffn_conv_b)
    grad_x = _local_step(x[0], p[0, 0], loss_target[0], small, comm)

    out = {}

    def adam(name, parts8, shape2d):
        w2, m2, v2 = (a.reshape(shape2d) for a in (weights[name], mom_m[name], mom_v[name]))
        res = _adam_call(w2, m2, v2, parts8.reshape(parts8.shape[:1] + shape2d), "adam_" + name)
        out[name] = tuple(r.reshape(weights[name].shape) for r in res)

    adam("w_in", comm.recv_in, (d, d))
    adam("w_branch", comm.recv[0], (256, d))
    adam("w_out", comm.recv[1], (128, d))
    adam("ffn_w_up", comm.recv[2], (d, UP_COLS))
    adam("ffn_w_down", comm.recv[3], (N_TAPS, d))
    adam("ple_w_proj", comm.recv[4], (256, 128))
    adam("ple_w_gate", comm.recv[5], (128, d))

    def adam_small(names, extra_w, extra_m, extra_v, extra_shapes, parts8, rows, label):
        pk = lambda src, extra: _pack([src[n] for n in names] + extra)[0]
        res = _adam_call(pk(weights, extra_w), pk(mom_m, extra_m), pk(mom_v, extra_v), parts8, label)
        shapes = [weights[n].shape for n in names] + extra_shapes
        unpacked = [_unpack(r, shapes, rows) for r in res]
        for j, n in enumerate(names):
            out[n] = tuple(u[j] for u in unpacked)
        return [[u[len(names) + j] for u in unpacked] for j in range(len(extra_shapes))]

    blank = jnp.zeros((8, LANE), F32)
    taps, loss_rows = adam_small(
        _SMALL_EARLY, [_place_taps(ffn_conv_w[0], me, f), blank], [_place_taps(m_ffn_conv_w[0], me, f), blank],
        [_place_taps(v_ffn_conv_w[0], me, f), blank + 1.0], [(3, f), (8, LANE)], comm.small_early, comm.rows_early,
        "adam_small_early")
    adam_small(_SMALL_LATE, [], [], [], [], comm.small_late, comm.rows_late, "adam_small_late")
    out["ffn_conv_w"] = tuple(lax.dynamic_slice_in_dim(u, me * N_TAPS, N_TAPS, axis=1)[None] for u in taps)
    loss = loss_rows[0][0, 0]

    order = ["w_in", "sgu_w_s", "sgu_b_s", "sgu_norm_g", "sgu_norm_b", "hgrn_lb_logits", "hgrn_norm_g", "w_branch", "w_out",
             "ln1_g", "ln1_b", "ffn_w_up", "ffn_conv_w", "ffn_conv_b", "ffn_w_down", "ln2_g", "ln2_b", "ple_w_proj", "ple_w_gate"]
    return (loss, grad_x[None], *[out[n][0] for n in order], *[out[n][1] for n in order],
            *[out[n][2] for n in order], *[out[n][3] for n in order])


def _place_taps(shard, me, f):
    return lax.dynamic_update_slice_in_dim(jnp.zeros((3, f), F32), shard, me * N_TAPS, axis=1)
```

```python
import functools
import math

import jax
import jax.numpy as jnp
from jax import lax
from jax.experimental import pallas as pl
from jax.experimental.pallas import tpu as pltpu

F32 = jnp.float32
BF16 = jnp.bfloat16

N_DEV = 8
N_CHIP = 4
D_MODEL = 1024
CHUNK = 64
SUB = 16
SGU_BLOCK = 128
GROUPS = 8
HEAD = 128
HEADS = 8
HEAD_UNROLL = 4
D_FF = 2816
LN_EPS = 1e-5
RMS_EPS = 1e-6
ALPHA = 2.0 ** 0.25
GELU_K = math.sqrt(2.0 / math.pi)
GELU_C = 0.044715
NEG = -1e30
ADAM_LR, ADAM_B1, ADAM_B2, ADAM_EPS, ADAM_WD, ADAM_STEP = 0.001, 0.9, 0.999, 1e-08, 0.01, 10
LANE = 128
SLOT_Q, SLOT_F, SLOT_I, SLOT_OG, SLOT_U, SLOT_V, SLOT_GA, SLOT_GB = range(8)


def _slot_of_group(k):
    return jnp.where(k < 2, k + 4, jnp.where(k < 6, k - 2, k))


MIB = 1024 * 1024
VMEM_V7X = 64 * MIB
VMEM_FLOOR = 32 * MIB
MM_TILES = (1024, 1408, 512, 256, 128)


def _params(sem, need=0):
    limit = min(max(need + need // 4, VMEM_FLOOR), VMEM_V7X - 4 * MIB)
    return pltpu.CompilerParams(dimension_semantics=sem, vmem_limit_bytes=limit)


def _pick(n, prefs):
    for t in prefs:
        if n % t == 0:
            return t
    return n


def _gelu(x):
    return 0.5 * x * (1.0 + jnp.tanh(GELU_K * (x + GELU_C * x * x * x)))


def _gelu_and_grad(x):
    x2 = x * x
    t = jnp.tanh(GELU_K * x * (1.0 + GELU_C * x2))
    half = 0.5 * (1.0 + t)
    return x * half, half + 0.5 * x * (1.0 - t * t) * GELU_K * (1.0 + 3.0 * GELU_C * x2)


def _silu_grad(x, s):
    return s * (1.0 + x * (1.0 - s))


def _dot(a, b):
    return jnp.dot(a.astype(BF16), b.astype(BF16), preferred_element_type=F32)


def _dot_nt(a, b):
    return lax.dot_general(a.astype(BF16), b.astype(BF16), (((1,), (1,)), ((), ())), preferred_element_type=F32)


def _dot_tn(a, b):
    return lax.dot_general(a.astype(BF16), b.astype(BF16), (((0,), (0,)), ((), ())), preferred_element_type=F32)


def _mean(x):
    return jnp.mean(x, axis=-1, keepdims=True)


def _sum0(x):
    return jnp.sum(x, axis=0, keepdims=True)


def _mm(a, b, *, out_dtype, name, trans_b=False, reduce_b=False, adds=(), ex=None):
    squeeze = b.ndim == 2
    a3 = a if a.ndim == 3 else a[None]
    b3 = b if b.ndim == 3 else b[None]
    ba, m, k = a3.shape
    bb = b3.shape[0]
    n = b3.shape[1] if trans_b else b3.shape[2]
    tm = _pick(m, MM_TILES)
    tn = _pick(n, MM_TILES)
    if reduce_b:
        bo, steps = 1, bb
        a_map = lambda o, i, j, r: (r if ba > 1 else 0, i, 0)
        b_map = (lambda o, i, j, r: (r, j, 0)) if trans_b else (lambda o, i, j, r: (r, 0, j))
    else:
        bo, steps = bb, 1
        a_map = lambda o, i, j, r: (o if ba > 1 else 0, i, 0)
        b_map = (lambda o, i, j, r: (o, j, 0)) if trans_b else (lambda o, i, j, r: (o, 0, j))
    o_map = lambda o, i, j, r: (o, i, j)
    add_arrays = [x if x.ndim == 3 else x[None] for x, _ in adds]
    add_scales = [s for _, s in adds]
    n_add = len(adds)
    dot = _dot_nt if trans_b else _dot

    def finish(acc, add_refs, o_ref):
        for ref, s in zip(add_refs, add_scales):
            acc = acc + s * ref[...].astype(F32)
        o_ref[...] = acc.astype(o_ref.dtype)

    grid = (bo, m // tm, n // tn, steps)

    def body(*refs):
        ins, (o_ref,), scratch, xrefs = _split_refs(refs, 2 + n_add, 1, 1 if reduce_b else 0, ex)
        a_ref, b_ref, add_refs = ins[0], ins[1], ins[2:]
        step = ((pl.program_id(0) * grid[1] + pl.program_id(1)) * grid[2] + pl.program_id(2)) * grid[3] + pl.program_id(3)
        if ex:
            @pl.when(step == 0)
            def _():
                ex.start(*xrefs)

        if reduce_b:
            acc, = scratch
            r = pl.program_id(3)

            @pl.when(r == 0)
            def _():
                acc[...] = jnp.zeros_like(acc)

            acc[...] += dot(a_ref[...], b_ref[...])

            @pl.when(r == steps - 1)
            def _():
                finish(acc[...], add_refs, o_ref)
        else:
            finish(dot(a_ref[...], b_ref[...]), add_refs, o_ref)

        if ex:
            @pl.when(step == math.prod(grid) - 1)
            def _():
                ex.finish(*xrefs)

    b_block = (None, tn, k) if trans_b else (None, k, tn)
    out_bytes = tm * tn * jnp.dtype(out_dtype).itemsize
    need = 2 * (tm * k * a3.dtype.itemsize + k * tn * b3.dtype.itemsize + out_bytes + n_add * tm * tn * 4)
    need += 2 * tm * tn * 4
    sem = ("arbitrary",) * 4 if ex else ("parallel", "parallel", "parallel", "arbitrary")
    res = pl.pallas_call(
        body,
        grid=grid,
        in_specs=[pl.BlockSpec((None, tm, k), a_map), pl.BlockSpec(b_block, b_map)]
        + [pl.BlockSpec((None, tm, tn), o_map) for _ in adds] + (ex.in_specs if ex else []),
        out_specs=[pl.BlockSpec((None, tm, tn), o_map)] + (ex.out_specs if ex else []),
        out_shape=[jax.ShapeDtypeStruct((bo, m, n), out_dtype)] + (ex.out_shape if ex else []),
        scratch_shapes=([pltpu.VMEM((tm, tn), F32)] if reduce_b else []) + (ex.scratch if ex else []),
        compiler_params=_params(sem, need),
        name=name,
    )(a3, b3, *add_arrays, *(ex.arrays if ex else []))
    out = res[0][0] if (reduce_b or squeeze) else res[0]
    return (out, res[1:]) if ex else out


def _mm_tn(a, b, *, out_dtype, name):
    squeeze = b.ndim == 2
    b3 = b if b.ndim == 3 else b[None]
    t, m = a.shape
    bb, _, n = b3.shape
    tm = _pick(m, MM_TILES)
    tn = _pick(n, MM_TILES)
    tt = _pick(t, (1024, 512, 256, 128))
    steps = t // tt
    need = 2 * (tt * tm * a.dtype.itemsize + tt * tn * b3.dtype.itemsize + tm * tn * jnp.dtype(out_dtype).itemsize)
    need += 2 * tm * tn * 4

    def body(a_ref, b_ref, o_ref, acc):
        r = pl.program_id(3)

        @pl.when(r == 0)
        def _():
            acc[...] = jnp.zeros_like(acc)

        acc[...] += _dot_tn(a_ref[...], b_ref[...])

        @pl.when(r == steps - 1)
        def _():
            o_ref[...] = acc[...].astype(o_ref.dtype)

    out = pl.pallas_call(
        body,
        grid=(bb, m // tm, n // tn, steps),
        in_specs=[pl.BlockSpec((tt, tm), lambda o, i, j, r: (r, i)),
                  pl.BlockSpec((None, tt, tn), lambda o, i, j, r: (o, r, j))],
        out_specs=pl.BlockSpec((None, tm, tn), lambda o, i, j, r: (o, i, j)),
        out_shape=jax.ShapeDtypeStruct((bb, m, n), out_dtype),
        scratch_shapes=[pltpu.VMEM((tm, tn), F32)],
        compiler_params=_params(("parallel", "parallel", "parallel", "arbitrary"), need),
        name=name,
    )(a, b3)
    return out[0] if squeeze else out


def _rowwise(fn, rows, consts, row_outs, acc_outs, *, tile, name):
    first = rows[0][0] if isinstance(rows[0], tuple) else rows[0]
    t = first.shape[-2]
    steps = t // tile
    arrays, in_specs = [], []
    for r in rows:
        if isinstance(r, tuple) and isinstance(r[1], pl.BlockSpec):
            arrays.append(r[0])
            in_specs.append(r[1])
        elif isinstance(r, tuple):
            arr, bidx = r
            arrays.append(arr)
            in_specs.append(pl.BlockSpec((None, tile, arr.shape[-1]), functools.partial(lambda i, b: (b, i, 0), b=bidx)))
        else:
            arrays.append(r)
            in_specs.append(pl.BlockSpec((tile, r.shape[-1]), lambda i: (i, 0)))
    for c in consts:
        arrays.append(c)
        in_specs.append(pl.BlockSpec(c.shape, lambda i: (0, 0)))
    n_in, n_row = len(arrays), len(row_outs)
    out_shape, out_specs = [], []
    for ro in row_outs:
        if ro[0] == "stack":
            _, cnt, blk, total, w, dt = ro
            out_shape.append(jax.ShapeDtypeStruct((total, t, w), dt))
            out_specs.append(pl.BlockSpec((cnt, tile, w), functools.partial(lambda i, b: (b, i, 0), b=blk)))
        else:
            w, dt = ro
            out_shape.append(jax.ShapeDtypeStruct((t, w), dt))
            out_specs.append(pl.BlockSpec((tile, w), lambda i: (i, 0)))
    out_shape += [jax.ShapeDtypeStruct(s, F32) for s in acc_outs]
    out_specs += [pl.BlockSpec(s, lambda i: (0, 0)) for s in acc_outs]
    blocks = [math.prod(d for d in sp.block_shape if d) * arr.dtype.itemsize for sp, arr in zip(in_specs, arrays)]
    blocks += [math.prod(d for d in sp.block_shape if d) * jnp.dtype(sh.dtype).itemsize
               for sp, sh in zip(out_specs, out_shape)]
    need = 2 * sum(blocks) + 6 * tile * max(a.shape[-1] for a in arrays) * 4

    def body(*refs):
        ins, outs = refs[:n_in], refs[n_in:]
        i = pl.program_id(0)
        res = fn(i, steps, *[r[...] for r in ins])
        res = res if isinstance(res, (tuple, list)) else (res,)
        for ref, val in zip(outs[:n_row], res[:n_row]):
            ref[...] = val.astype(ref.dtype)
        if acc_outs:
            @pl.when(i == 0)
            def _():
                for ref in outs[n_row:]:
                    ref[...] = jnp.zeros_like(ref)

            for ref, val in zip(outs[n_row:], res[n_row:]):
                ref[...] += val

    return pl.pallas_call(
        body,
        grid=(steps,),
        in_specs=in_specs,
        out_specs=out_specs,
        out_shape=out_shape,
        compiler_params=_params(("arbitrary",), need),
        name=name,
    )(*arrays)


def _ln_stats(z):
    mu = _mean(z)
    zc = z - mu
    rstd = lax.rsqrt(_mean(zc * zc) + LN_EPS)
    return zc * rstd, rstd


def _ln_bwd(dy, xhat, rstd, g):
    dxh = dy * g
    return rstd * (dxh - _mean(dxh) - xhat * _mean(dxh * xhat))


def _ride(ex, xrefs, step, steps):
    if not ex:
        return
    for at, act in ((0, ex.start), (steps - 2, ex.relay), (steps - 1, ex.finish)):
        @pl.when(step == at)
        def _(act=act):
            act(*xrefs)


def _sgu_fwd(h, wm, bs_t, g_v, b_v, ex=None):
    t = h.shape[1]
    steps = t // SGU_BLOCK

    def body(*refs):
        (u_ref, v_ref, wm_ref, bs_ref, g_ref, b_ref), (y_ref,), _, xrefs = _split_refs(refs, 6, 1, 0, ex)
        xhat, _ = _ln_stats(_gelu(v_ref[...]))
        vn = (xhat * g_ref[...] + b_ref[...]).astype(BF16)
        gu = _gelu(u_ref[...])
        for g in range(GROUPS):
            sl = slice(g * HEAD, (g + 1) * HEAD)
            mixed = _dot(wm_ref[g], vn[:, sl]) + bs_ref[:, g:g + 1]
            y_ref[:, sl] = (gu[:, sl] * mixed).astype(BF16)
        _ride(ex, xrefs, pl.program_id(0), steps)

    blk = lambda b: pl.BlockSpec((None, SGU_BLOCK, D_MODEL), functools.partial(lambda i, b: (b, i, 0), b=b))
    whole = lambda s: pl.BlockSpec(s, lambda i: (0,) * len(s))
    res = pl.pallas_call(
        body,
        grid=(steps,),
        in_specs=[blk(SLOT_U), blk(SLOT_V), whole(wm.shape), whole(bs_t.shape), whole(g_v.shape), whole(b_v.shape)]
        + (ex.in_specs if ex else []),
        out_specs=[pl.BlockSpec((SGU_BLOCK, D_MODEL), lambda i: (i, 0))] + (ex.out_specs if ex else []),
        out_shape=[jax.ShapeDtypeStruct((t, D_MODEL), BF16)] + (ex.out_shape if ex else []),
        scratch_shapes=ex.scratch if ex else [],
        compiler_params=_params(("arbitrary",)),
        name="sgu_fwd",
    )(h, h, wm, bs_t, g_v, b_v, *(ex.arrays if ex else []))
    return res[0], res[1:]


def _sgu_bwd(h, dy, dh, wm, wm_t, bs_t, g_v, b_v):
    t = h.shape[1]

    def body(u_ref, v_ref, dy_ref, dh_in, wm_ref, wmt_ref, bs_ref, g_ref, b_ref,
             duv_ref, dw_ref, dbs_ref, dg_ref, db_ref, dvn_ref):
        del dh_in
        du_ref, dv_ref = duv_ref.at[0], duv_ref.at[1]
        i = pl.program_id(0)

        @pl.when(i == 0)
        def _():
            dw_ref[...] = jnp.zeros_like(dw_ref)
            dbs_ref[...] = jnp.zeros_like(dbs_ref)
            dg_ref[...] = jnp.zeros_like(dg_ref)
            db_ref[...] = jnp.zeros_like(db_ref)

        u = u_ref[...]
        v = v_ref[...]
        gv, gvp = _gelu_and_grad(v)
        xhat, rstd = _ln_stats(gv)
        vn = (xhat * g_ref[...] + b_ref[...]).astype(BF16)
        gu, gup = _gelu_and_grad(u)
        lane = lax.broadcasted_iota(jnp.int32, (SGU_BLOCK, LANE), 1)
        dbs = jnp.zeros((SGU_BLOCK, LANE), F32)
        for g in range(GROUPS):
            sl = slice(g * HEAD, (g + 1) * HEAD)
            vn_g = vn[:, sl]
            mixed = _dot(wm_ref[g], vn_g) + bs_ref[:, g:g + 1]
            dy_g = dy_ref[:, sl]
            du_ref[:, sl] = (dy_g * mixed * gup[:, sl]).astype(BF16)
            dmix = dy_g * gu[:, sl]
            dmb = dmix.astype(BF16)
            dvn_ref[:, sl] = _dot(wmt_ref[g], dmb)
            dw_ref[g] += _dot_nt(dmb, vn_g)
            dbs = dbs + jnp.where(lane == g, jnp.sum(dmix, axis=1, keepdims=True), 0.0)
        dbs_ref[...] += dbs
        dvn = dvn_ref[...]
        dg_ref[...] += _sum0(dvn * xhat)
        db_ref[...] += _sum0(dvn)
        dv_ref[...] = (_ln_bwd(dvn, xhat, rstd, g_ref[...]) * gvp).astype(BF16)

    blk = lambda b: pl.BlockSpec((None, SGU_BLOCK, D_MODEL), functools.partial(lambda i, b: (b, i, 0), b=b))
    row = pl.BlockSpec((SGU_BLOCK, D_MODEL), lambda i: (i, 0))
    whole = lambda s: pl.BlockSpec(s, lambda i: (0,) * len(s))
    vec = (1, D_MODEL)
    return pl.pallas_call(
        body,
        grid=(t // SGU_BLOCK,),
        in_specs=[blk(SLOT_U), blk(SLOT_V), row, pl.BlockSpec(memory_space=pl.ANY),
                  whole(wm.shape), whole(wm_t.shape), whole(bs_t.shape), whole(vec), whole(vec)],
        out_specs=[pl.BlockSpec((2, SGU_BLOCK, D_MODEL), lambda i: (SLOT_U // 2, i, 0)),
                   whole(wm.shape), whole((SGU_BLOCK, LANE)), whole(vec), whole(vec)],
        out_shape=[jax.ShapeDtypeStruct(dh.shape, BF16),
                   jax.ShapeDtypeStruct(wm.shape, F32), jax.ShapeDtypeStruct((SGU_BLOCK, LANE), F32),
                   jax.ShapeDtypeStruct(vec, F32), jax.ShapeDtypeStruct(vec, F32)],
        scratch_shapes=[pltpu.VMEM((SGU_BLOCK, D_MODEL), F32)],
        input_output_aliases={3: 0},
        compiler_params=_params(("arbitrary",)),
        name="sgu_bwd",
    )(h, h, dy, dh, wm, wm_t, bs_t, g_v, b_v)


def _split3(x):
    hi = x.astype(BF16)
    r1 = x - hi.astype(F32)
    mid = r1.astype(BF16)
    lo = (r1 - mid.astype(F32)).astype(BF16)
    return hi, mid, lo


def _tri_matmul(tri, x):
    hi, mid, lo = _split3(x)
    dot = lambda p: jnp.dot(tri, p, preferred_element_type=F32)
    return dot(hi) + dot(mid) + dot(lo)


def _lower_bound(logits):
    l0, l1 = logits[0:1, :], logits[1:2, :]
    mx = jnp.maximum(l0, l1)
    e0, e1 = jnp.exp(l0 - mx), jnp.exp(l1 - mx)
    return e0 / (e0 + e1)


def _hgrn_gates(q_raw, f_raw, lb):
    q = q_raw * jax.nn.sigmoid(q_raw)
    sig = jax.nn.sigmoid(f_raw)
    f = lb + (1.0 - lb) * sig
    row = lax.broadcasted_iota(jnp.int32, (CHUNK, CHUNK), 0)
    col = lax.broadcasted_iota(jnp.int32, (CHUNK, CHUNK), 1)
    c = _tri_matmul((row >= col).astype(BF16), jnp.log(f))
    return q, sig, f, 1.0 - f, c


def _offdiag_terms(qh, kh, ch, tb):
    rows = slice(tb * SUB, (tb + 1) * SUB)
    r = ch[tb * SUB - 1:tb * SUB, :]
    eqh = jnp.exp(ch[rows] - r)
    ekh = jnp.exp(jnp.minimum(r - ch, 0.0))
    return rows, eqh, qh[rows] * eqh, ekh, kh * ekh


def _diag_decay(cb, s, trow):
    return jnp.exp(jnp.where(trow >= s, cb - cb[s:s + 1, :], NEG))


def _split_refs(refs, n_in, n_out, n_scratch, ex):
    nx = ex.n if ex else 0
    ins, refs = refs[:n_in], refs[n_in:]
    xsrc, refs = refs[:nx], refs[nx:]
    outs, refs = refs[:n_out], refs[n_out:]
    xout, refs = refs[:nx], refs[nx:]
    return ins, outs, refs[:n_scratch], (xsrc, xout, refs[n_scratch:])


def _hgrn_fwd(h, logits, g_norm, ex=None):
    t = h.shape[1]
    nc = t // CHUNK

    def body(*refs):
        ins, outs, scratch, xrefs = _split_refs(refs, 6, 3, 4, ex)
        q_ref, f_ref, i_ref, og_ref, lg_ref, gn_ref = ins
        y_ref, o_ref, sall_ref = outs
        st_ref, q_s, k_s, c_s = scratch

        @pl.when(pl.program_id(0) == 0)
        def _():
            st_ref[...] = jnp.zeros_like(st_ref)
            if ex:
                ex.start(*xrefs)

        if ex:
            @pl.when(pl.program_id(0) == nc - 2)
            def _():
                ex.relay(*xrefs)

        lb = _lower_bound(lg_ref[...])
        q, _, _, k, c = _hgrn_gates(q_ref[...], f_ref[...], lb)
        q_s[...] = q
        k_s[...] = k
        c_s[...] = c
        col64 = lax.broadcasted_iota(jnp.int32, (SUB, CHUNK), 1)
        trow = lax.broadcasted_iota(jnp.int32, (SUB, HEAD), 0)

        def head(hd, carry):
            sl = pl.ds(pl.multiple_of(hd * HEAD, HEAD), HEAD)
            qh, kh, ch, ih = q_s[:, sl], k_s[:, sl], c_s[:, sl], i_ref[:, sl]
            st = st_ref[hd]
            sall_ref[hd] = st
            c_last = ch[CHUNK - 1:CHUNK, :]
            o = _dot_nt(qh * jnp.exp(ch), st)
            st_ref[hd] = st * jnp.exp(c_last) + _dot_tn(ih, kh * jnp.exp(c_last - ch))
            a_rows = [jnp.zeros((SUB, CHUNK), F32)]
            for tb in range(1, CHUNK // SUB):
                _, _, q_hat, _, k_hat = _offdiag_terms(qh, kh, ch, tb)
                a_rows.append(jnp.where(col64 < tb * SUB, _dot_nt(q_hat, k_hat), 0.0))
            o = o + _dot(jnp.concatenate(a_rows, axis=0), ih)
            o_rows = []
            for b in range(CHUNK // SUB):
                rows = slice(b * SUB, (b + 1) * SUB)
                qb, cb, kb, ib = qh[rows], ch[rows], kh[rows], ih[rows]
                ob = jnp.zeros((SUB, HEAD), F32)
                for s in range(SUB):
                    a = jnp.sum(qb * _diag_decay(cb, s, trow) * kb[s:s + 1, :], axis=1, keepdims=True)
                    ob = ob + a * ib[s:s + 1, :]
                o_rows.append(ob)
            o = o + jnp.concatenate(o_rows, axis=0)
            o_ref[:, sl] = o
            og = og_ref[:, sl]
            on = o * lax.rsqrt(_mean(o * o) + RMS_EPS)
            y_ref[:, sl] = (on * gn_ref[:, sl] * (og * jax.nn.sigmoid(og))).astype(BF16)
            return carry

        lax.fori_loop(0, HEADS, head, 0, unroll=HEAD_UNROLL)

        if ex:
            @pl.when(pl.program_id(0) == nc - 1)
            def _():
                ex.finish(*xrefs)

    blk = lambda b: pl.BlockSpec((None, CHUNK, D_MODEL), functools.partial(lambda n, b: (b, n, 0), b=b))
    row = pl.BlockSpec((CHUNK, D_MODEL), lambda n: (n, 0))
    whole = lambda s: pl.BlockSpec(s, lambda n: (0,) * len(s))
    res = pl.pallas_call(
        body,
        grid=(nc,),
        in_specs=[blk(SLOT_Q), blk(SLOT_F), blk(SLOT_I), blk(SLOT_OG), whole(logits.shape), whole(g_norm.shape)]
        + (ex.in_specs if ex else []),
        out_specs=[row, row, pl.BlockSpec((None, HEADS, HEAD, HEAD), lambda n: (n, 0, 0, 0))] + (ex.out_specs if ex else []),
        out_shape=[jax.ShapeDtypeStruct((t, D_MODEL), BF16), jax.ShapeDtypeStruct((t, D_MODEL), F32),
                   jax.ShapeDtypeStruct((nc, HEADS, HEAD, HEAD), F32)] + (ex.out_shape if ex else []),
        scratch_shapes=[pltpu.VMEM((HEADS, HEAD, HEAD), F32)] + [pltpu.VMEM((CHUNK, D_MODEL), F32)] * 3
        + (ex.scratch if ex else []),
        compiler_params=_params(("arbitrary",)),
        name="hgrn_fwd",
    )(h, h, h, h, logits, g_norm, *(ex.arrays if ex else []))
    return res[0], res[1], res[2], res[3:]


def _hgrn_bwd(h, o_all, dy, states, dh, logits, g_norm, ex=None):
    t = h.shape[1]
    nc = t // CHUNK

    def body(*refs):
        ins, outs, scratch, xrefs = _split_refs(refs, 10, 3, 8, ex)
        q_ref, f_ref, i_ref, og_ref, o_ref, dy_ref, sall_ref, _, lg_ref, gn_ref = ins
        dqfio_ref, dlg_ref, dgn_ref = outs
        dst_ref, dlb_ref, q_s, k_s, c_s, dq_s, dk_s, dc_s = scratch
        dq_ref, df_ref, di_ref, dog_ref = (dqfio_ref.at[s] for s in (SLOT_Q, SLOT_F, SLOT_I, SLOT_OG))
        n = pl.program_id(0)

        @pl.when(n == 0)
        def _():
            dst_ref[...] = jnp.zeros_like(dst_ref)
            dlb_ref[...] = jnp.zeros_like(dlb_ref)
            dgn_ref[...] = jnp.zeros_like(dgn_ref)
            if ex:
                ex.start(*xrefs)

        lb = _lower_bound(lg_ref[...])
        q_raw = q_ref[...]
        q, sig, f, k, c = _hgrn_gates(q_raw, f_ref[...], lb)
        q_s[...] = q
        k_s[...] = k
        c_s[...] = c
        col64 = lax.broadcasted_iota(jnp.int32, (SUB, CHUNK), 1)
        trow = lax.broadcasted_iota(jnp.int32, (SUB, HEAD), 0)
        row64 = lax.broadcasted_iota(jnp.int32, (CHUNK, HEAD), 0)

        def head(hd, carry):
            sl = pl.ds(pl.multiple_of(hd * HEAD, HEAD), HEAD)
            qh, kh, ch, ih = q_s[:, sl], k_s[:, sl], c_s[:, sl], i_ref[:, sl]
            st = sall_ref[hd]
            dst = dst_ref[hd]
            oh, dyh, og, gn = o_ref[:, sl], dy_ref[:, sl], og_ref[:, sl], gn_ref[:, sl]
            sg = jax.nn.sigmoid(og)
            sil = og * sg
            rms = lax.rsqrt(_mean(oh * oh) + RMS_EPS)
            on = oh * rms
            dog_ref[:, sl] = (dyh * on * gn * _silu_grad(og, sg)).astype(BF16)
            dgn_ref[:, sl] += _sum0(dyh * on * sil)
            don = dyh * gn * sil
            do = rms * (don - on * _mean(don * on))
            dob = do.astype(BF16)

            c_last = ch[CHUNK - 1:CHUNK, :]
            eq = jnp.exp(ch)
            q_til = qh * eq
            ekl = jnp.exp(c_last - ch)
            k_til = kh * ekl
            ecl = jnp.exp(c_last)
            dq_til = _dot(dob, st)
            dk_til = _dot(ih, dst)
            di = _dot_nt(k_til, dst)
            dc_last = _sum0(dk_til * k_til) + _sum0(dst * st) * ecl
            dst_ref[hd] = _dot_tn(dob, q_til) + dst * ecl
            dq = dq_til * eq
            dc = dq_til * q_til - dk_til * k_til
            dk = dk_til * ekl

            da_full = _dot_nt(dob, ih)
            a_rows = [jnp.zeros((SUB, CHUNK), F32)]
            dq_rows = [jnp.zeros((SUB, HEAD), F32)]
            dc_rows = [jnp.zeros((SUB, HEAD), F32)]
            for tb in range(1, CHUNK // SUB):
                rows, eqh, q_hat, ekh, k_hat = _offdiag_terms(qh, kh, ch, tb)
                keep = col64 < tb * SUB
                a_rows.append(jnp.where(keep, _dot_nt(q_hat, k_hat), 0.0))
                da = jnp.where(keep, da_full[rows], 0.0)
                dq_hat = _dot(da, k_hat)
                dk_hat = _dot_tn(da, q_hat)
                dq_rows.append(dq_hat * eqh)
                dc_rows.append(dq_hat * q_hat)
                dk = dk + dk_hat * ekh
                dc = dc - dk_hat * k_hat
            di = di + _dot_tn(jnp.concatenate(a_rows, axis=0), dob)

            dk_rows, di_rows = [], []
            for b in range(CHUNK // SUB):
                rows = slice(b * SUB, (b + 1) * SUB)
                qb, cb, kb, ib, dob_ = qh[rows], ch[rows], kh[rows], ih[rows], do[rows]
                dq_diag = jnp.zeros((SUB, HEAD), F32)
                dk_diag = jnp.zeros((SUB, HEAD), F32)
                di_diag = jnp.zeros((SUB, HEAD), F32)
                for s in range(SUB):
                    ks = kb[s:s + 1, :]
                    dec = _diag_decay(cb, s, trow)
                    a = jnp.sum(qb * dec * ks, axis=1, keepdims=True)
                    gk = jnp.sum(dob_ * ib[s:s + 1, :], axis=1, keepdims=True) * dec
                    dq_diag = dq_diag + gk * ks
                    dk_diag = dk_diag + jnp.where(trow == s, _sum0(gk * qb), 0.0)
                    di_diag = di_diag + jnp.where(trow == s, _sum0(a * dob_), 0.0)
                dq_rows[b] = dq_rows[b] + dq_diag
                dc_rows[b] = dc_rows[b] + qb * dq_diag - kb * dk_diag
                dk_rows.append(dk_diag)
                di_rows.append(di_diag)
            dq = dq + jnp.concatenate(dq_rows, axis=0)
            dk = dk + jnp.concatenate(dk_rows, axis=0)
            dc = dc + jnp.concatenate(dc_rows, axis=0) + jnp.where(row64 == CHUNK - 1, dc_last, 0.0)
            di_ref[:, sl] = (di + jnp.concatenate(di_rows, axis=0)).astype(BF16)
            dq_s[:, sl] = dq
            dk_s[:, sl] = dk
            dc_s[:, sl] = dc
            return carry

        lax.fori_loop(0, HEADS, head, 0, unroll=HEAD_UNROLL)

        row = lax.broadcasted_iota(jnp.int32, (CHUNK, CHUNK), 0)
        col = lax.broadcasted_iota(jnp.int32, (CHUNK, CHUNK), 1)
        dlf = _tri_matmul((row <= col).astype(BF16), dc_s[...])
        df = dlf / f - dk_s[...]
        dlb_ref[...] += _sum0(df * (1.0 - sig))
        df_ref[...] = (df * (1.0 - lb) * sig * (1.0 - sig)).astype(BF16)
        dq_ref[...] = (dq_s[...] * _silu_grad(q_raw, jax.nn.sigmoid(q_raw))).astype(BF16)

        @pl.when(n == nc - 1)
        def _():
            d0 = dlb_ref[...] * lb * (1.0 - lb)
            dlg_ref[0:1, :] = d0
            dlg_ref[1:2, :] = -d0
            if ex:
                ex.finish(*xrefs)

    rev = lambda n: nc - 1 - n
    blk = lambda b: pl.BlockSpec((None, CHUNK, D_MODEL), functools.partial(lambda n, b: (b, rev(n), 0), b=b))
    row = pl.BlockSpec((CHUNK, D_MODEL), lambda n: (rev(n), 0))
    whole = lambda s: pl.BlockSpec(s, lambda n: (0,) * len(s))
    vec = (1, D_MODEL)
    res = pl.pallas_call(
        body,
        grid=(nc,),
        in_specs=[blk(SLOT_Q), blk(SLOT_F), blk(SLOT_I), blk(SLOT_OG), row, row,
                  pl.BlockSpec((None, HEADS, HEAD, HEAD), lambda n: (rev(n), 0, 0, 0)),
                  pl.BlockSpec(memory_space=pl.ANY), whole(logits.shape), whole(vec)] + (ex.in_specs if ex else []),
        out_specs=[pl.BlockSpec((4, CHUNK, D_MODEL), lambda n: (SLOT_Q // 4, rev(n), 0)), whole((2, D_MODEL)), whole(vec)]
        + (ex.out_specs if ex else []),
        out_shape=[jax.ShapeDtypeStruct(dh.shape, BF16), jax.ShapeDtypeStruct((2, D_MODEL), F32),
                   jax.ShapeDtypeStruct(vec, F32)] + (ex.out_shape if ex else []),
        scratch_shapes=[pltpu.VMEM((HEADS, HEAD, HEAD), F32), pltpu.VMEM(vec, F32)]
        + [pltpu.VMEM((CHUNK, D_MODEL), F32)] * 6 + (ex.scratch if ex else []),
        input_output_aliases={7: 0},
        compiler_params=_params(("arbitrary",)),
        name="hgrn_bwd",
    )(h, h, h, h, o_all, dy, states, dh, logits, g_norm, *(ex.arrays if ex else []))
    return res[0], res[1], res[2], res[3:]


def _merge_fwd(i, n, ga, gb, za, zb):
    return jax.nn.sigmoid(ga) * za + jax.nn.sigmoid(gb) * zb


def _merge_bwd(i, n, ga, gb, za, zb, dm):
    sa, sb = jax.nn.sigmoid(ga), jax.nn.sigmoid(gb)
    dgates = jnp.stack([(dm * za * sa * (1.0 - sa)).astype(BF16), (dm * zb * sb * (1.0 - sb)).astype(BF16)])
    return dgates, dm * sa, dm * sb


def _ln1_fwd(i, n, x, r1, g, b):
    xhat, _ = _ln_stats(ALPHA * x + r1)
    x1 = xhat * g + b
    return x1, x1


def _ln1_bwd(i, n, x, r1, dx1, g):
    xhat, rstd = _ln_stats(ALPHA * x + r1)
    dz = _ln_bwd(dx1, xhat, rstd, g)
    return dz, dz, _sum0(dx1 * xhat), _sum0(dx1)


def _ln2_loss(i, n, x1, fo, pg, pp, tgt, g, b):
    sg = jax.nn.sigmoid(pg)
    xhat, rstd = _ln_stats(ALPHA * x1 + fo + sg * pp)
    diff = xhat * g + b - tgt
    loss = 0.5 * jnp.sum(_mean(diff * diff), axis=0, keepdims=True)
    dy = diff * (1.0 / D_MODEL)
    dz = _ln_bwd(dy, xhat, rstd, g)
    return (dz, dz, dz * pp * sg * (1.0 - sg), dz * sg,
            jnp.broadcast_to(loss, (8, LANE)), _sum0(dy * xhat), _sum0(dy))


def _shift_down(cur, halo, tile):
    row = lax.broadcasted_iota(jnp.int32, cur.shape, 0)
    m1 = jnp.where(row == 0, halo[7:8, :], pltpu.roll(cur, 1, 0))
    m2 = jnp.where(row == 0, halo[6:7, :], jnp.where(row == 1, halo[7:8, :], pltpu.roll(cur, 2, 0)))
    return m1, m2


def _shift_up(cur, halo, tile):
    row = lax.broadcasted_iota(jnp.int32, cur.shape, 0)
    p1 = jnp.where(row == tile - 1, halo[0:1, :], pltpu.roll(cur, tile - 1, 0))
    p2 = jnp.where(row == tile - 2, halo[0:1, :], jnp.where(row == tile - 1, halo[1:2, :], pltpu.roll(cur, tile - 2, 0)))
    return p1, p2


def _conv_pre(i, gate, halo, w, b, tile):
    halo = jnp.where(i == 0, 0.0, halo)
    m1, m2 = _shift_down(gate, halo, tile)
    return w[0:1, :] * m2 + w[1:2, :] * m1 + w[2:3, :] * gate + b, m1, m2


def _conv_fwd(tile, i, n, gate, halo, val, w, b):
    cg, _, _ = _conv_pre(i, gate, halo, w, b, tile)
    return _gelu(cg) * val


def _conv_bwd_a(tile, i, n, gate, halo, val, dhid, w, b):
    cg, m1, m2 = _conv_pre(i, gate, halo, w, b, tile)
    act, slope = _gelu_and_grad(cg)
    dcg = dhid * val * slope
    return dcg, dhid * act, _sum0(dcg * m2), _sum0(dcg * m1), _sum0(dcg * gate), _sum0(dcg)


def _conv_bwd_b(tile, i, n, dcg, halo, w):
    dcg = dcg.astype(F32)
    halo = jnp.where(i == n - 1, 0.0, halo.astype(F32))
    p1, p2 = _shift_up(dcg, halo, tile)
    return w[2:3, :] * dcg + w[1:2, :] * p1 + w[0:1, :] * p2


def _halo_spec(width, tile, t, nxt, rows=8):
    per = tile // rows
    last = t // rows - 1
    if nxt:
        return pl.BlockSpec((rows, width), lambda i: (jnp.minimum((i + 1) * per, last), 0))
    return pl.BlockSpec((rows, width), lambda i: (jnp.maximum(i * per - 1, 0), 0))


def _adamw(i, n, w, m, v, parts):
    g = parts[0].astype(F32)
    for j in range(1, parts.shape[0]):
        g = g + parts[j].astype(F32)
    m_new = ADAM_B1 * m + (1.0 - ADAM_B1) * g
    v_new = ADAM_B2 * v + (1.0 - ADAM_B2) * (g * g)
    m_hat = m_new / (1.0 - ADAM_B1 ** ADAM_STEP)
    v_hat = v_new / (1.0 - ADAM_B2 ** ADAM_STEP)
    delta = -ADAM_LR * (m_hat / (jnp.sqrt(v_hat) + ADAM_EPS) + ADAM_WD * w)
    return g, delta, m_new, v_new


def _adam_call(w, m, v, parts, name):
    r, c = w.shape
    tile = _pick(r, (256, 128)) if r > 256 else r
    spec = pl.BlockSpec((parts.shape[0], tile, c), lambda i: (0, i, 0))
    return _rowwise(_adamw, [w, m, v, (parts, spec)], [], [(c, F32)] * 4, [], tile=tile, name=name)


def _peer(k):
    x, y, c = lax.axis_index("x"), lax.axis_index("y"), lax.axis_index("c")
    px = x ^ ((k >> 2) & 1)
    py = y ^ ((k >> 1) & 1)
    pc = c ^ (k & 1)
    return (px, py, pc), 4 * px + 2 * py + pc


def _my_index():
    return 4 * lax.axis_index("x") + 2 * lax.axis_index("y") + lax.axis_index("c")


class _Exchange:
    KINDS = ("gather", "gather+relay", "scatter", "scatter+pairs")

    def __init__(self, entries):
        assert all(k in self.KINDS for _, k in entries), [k for _, k in entries]
        self.arrays = [a for a, _ in entries]
        self.scatter = [k.startswith("scatter") for _, k in entries]
        self.relayed = ["+relay" in k for _, k in entries]
        self.pairs = ["+pairs" in k for _, k in entries]
        self.n = len(entries)
        self.in_specs = [pl.BlockSpec(memory_space=pl.ANY)] * self.n
        self.out_specs = [pl.BlockSpec(memory_space=pl.ANY)] * self.n
        shapes = [tuple(a.shape[1:]) if sc else tuple(a.shape) for a, sc in zip(self.arrays, self.scatter)]
        counts = [N_CHIP if p else N_DEV for p in self.pairs]
        self.out_shape = [jax.ShapeDtypeStruct((n,) + s, a.dtype) for n, s, a in zip(counts, shapes, self.arrays)]
        per = N_DEV - 1
        self.scratch = [pltpu.SemaphoreType.DMA((self.n * per,)), pltpu.SemaphoreType.DMA((self.n * per,)),
                        pltpu.SemaphoreType.DMA((self.n,))]

    def _copies(self, srcs, outs, sems):
        send_sems, recv_sems, local_sems = sems
        x, y, c = lax.axis_index("x"), lax.axis_index("y"), lax.axis_index("c")
        me = _my_index()
        per = N_DEV - 1
        local, first, passed, relay_arrivals, arrivals = [], [], [], [], []
        for a in range(self.n):

            def copy(k, src, dst, dev, a=a):
                return pltpu.make_async_remote_copy(
                    src_ref=src, dst_ref=dst, send_sem=send_sems.at[a * per + k], recv_sem=recv_sems.at[a * per + k],
                    device_id=dev, device_id_type=pl.DeviceIdType.MESH)

            if self.pairs[a]:
                chip = 2 * x + y
                for k in range(N_CHIP):
                    to = chip ^ k
                    piece = srcs[a].at[_slot_of_group(2 * to + c) // 2]
                    if k == 0:
                        local.append(pltpu.make_async_copy(piece, outs[a].at[chip], local_sems.at[a]))
                    else:
                        dev = (to // 2, to % 2, c)
                        first.append(copy(k - 1, piece, outs[a].at[chip], dev))
                        arrivals.append(copy(k - 1, piece, outs[a].at[to], dev))
                continue
            mine = srcs[a].at[me] if self.scatter[a] else srcs[a]
            land = outs[a].at[me]
            local.append(pltpu.make_async_copy(mine, land, local_sems.at[a]))
            if self.relayed[a]:
                block = lambda px, py, pc, a=a: outs[a].at[4 * px + 2 * py + pc]
                chips = [(1 - x, y), (x, 1 - y), (1 - x, 1 - y)]
                first.append(copy(0, mine, land, (x, y, 1 - c)))
                arrivals.append(copy(0, mine, block(x, y, 1 - c), (x, y, 1 - c)))
                for j, (px, py) in enumerate(chips):
                    first.append(copy(1 + j, mine, land, (px, py, c)))
                    relay_arrivals.append(copy(1 + j, mine, block(px, py, c), (px, py, c)))
                    passed.append(copy(4 + j, block(px, py, c), block(px, py, c), (x, y, 1 - c)))
                    arrivals.append(copy(4 + j, mine, block(px, py, 1 - c), (x, y, 1 - c)))
                continue
            for k in range(1, N_DEV):
                dev, idx = _peer(k)
                if self.scatter[a]:
                    first.append(copy(k - 1, srcs[a].at[idx], land, dev))
                else:
                    first.append(copy(k - 1, mine, land, dev))
                arrivals.append(copy(k - 1, mine, outs[a].at[idx], dev))
        return local, first, passed, relay_arrivals, arrivals

    def start(self, srcs, outs, sems):
        local, first, _, _, _ = self._copies(srcs, outs, sems)
        for cp in local + first:
            cp.start()

    def relay(self, srcs, outs, sems):
        _, _, passed, relay_arrivals, _ = self._copies(srcs, outs, sems)
        for landed, onward in zip(relay_arrivals, passed):
            landed.wait_recv()
            onward.start()

    def finish(self, srcs, outs, sems):
        local, first, passed, _, arrivals = self._copies(srcs, outs, sems)
        for cp in arrivals:
            cp.wait_recv()
        for cp in first + passed:
            cp.wait_send()
        for cp in local:
            cp.wait()


def _gather_project(x_b, shard):
    t, d = x_b.shape
    tm = _pick(t, MM_TILES)
    nrow = t // tm
    per = N_DEV - 1

    def parties():
        x, y, c = lax.axis_index("x"), lax.axis_index("y"), lax.axis_index("c")
        chips = [(1 - x, y), (x, 1 - y), (1 - x, 1 - y)]
        return (x, y, c), (x, y, 1 - c), [(px, py, c) for px, py in chips], [(px, py, 1 - c) for px, py in chips]

    slot = lambda dev: _slot_of_group(4 * dev[0] + 2 * dev[1] + dev[2])
    ici_step, passed_step = (2, 3, 6), (4, 5, 7)
    me, sibling, over_ici, passed_on = parties()
    by_step = {0: me, 1: sibling, **dict(zip(ici_step, over_ici)), **dict(zip(passed_step, passed_on))}
    order = jnp.stack([slot(by_step[j]) for j in range(N_DEV)]).astype(jnp.int32)

    def body(order_ref, x_ref, shard_ref, h_ref, wall_ref, wbuf, fetch_sem, send_sems, recv_sems, local_sem):
        del order_ref
        me, sibling, over_ici, passed_on = parties()
        j, i = pl.program_id(0), pl.program_id(1)
        land = lambda dev: wall_ref.at[slot(dev)]

        def copy(k, src, block, to):
            return pltpu.make_async_remote_copy(src_ref=src, dst_ref=land(block), send_sem=send_sems.at[k],
                                                recv_sem=recv_sems.at[k], device_id=to, device_id_type=pl.DeviceIdType.MESH)

        def fetch(src):
            cp = pltpu.make_async_copy(src, wbuf, fetch_sem)
            cp.start()
            cp.wait()

        keep = pltpu.make_async_copy(shard_ref, land(me), local_sem)
        first = [copy(0, shard_ref, me, sibling)] + [copy(1 + n, shard_ref, me, dev) for n, dev in enumerate(over_ici)]
        onward = [copy(4 + n, land(dev), dev, sibling) for n, dev in enumerate(over_ici)]

        @pl.when(jnp.logical_and(i == 0, j == 0))
        def _():
            keep.start()
            for cp in first:
                cp.start()
            fetch(shard_ref)

        @pl.when(jnp.logical_and(i == 0, j == 1))
        def _():
            copy(0, shard_ref, sibling, me).wait_recv()
            fetch(land(sibling))

        for n, dev in enumerate(over_ici):
            @pl.when(jnp.logical_and(i == 0, j == ici_step[n]))
            def _(n=n, dev=dev):
                copy(1 + n, shard_ref, dev, me).wait_recv()
                onward[n].start()
                fetch(land(dev))

        for n, dev in enumerate(passed_on):
            @pl.when(jnp.logical_and(i == 0, j == passed_step[n]))
            def _(n=n, dev=dev):
                copy(4 + n, shard_ref, dev, me).wait_recv()
                fetch(land(dev))

        h_ref[...] = _dot(x_ref[...], wbuf[...])

        @pl.when(jnp.logical_and(i == nrow - 1, j == N_DEV - 1))
        def _():
            for cp in first + onward:
                cp.wait_send()
            keep.wait()

    need = 2 * (tm * d * 2 + tm * d * 4) + d * d * 2 + tm * d * 4
    h, w_all = pl.pallas_call(
        body,
        grid_spec=pltpu.PrefetchScalarGridSpec(
            num_scalar_prefetch=1,
            grid=(N_DEV, nrow),
            in_specs=[pl.BlockSpec((tm, d), lambda j, i, order: (i, 0)), pl.BlockSpec(memory_space=pl.ANY)],
            out_specs=[pl.BlockSpec((None, tm, d), lambda j, i, order: (order[j], i, 0)), pl.BlockSpec(memory_space=pl.ANY)],
            scratch_shapes=[pltpu.VMEM((d, d), BF16), pltpu.SemaphoreType.DMA, pltpu.SemaphoreType.DMA((per,)),
                            pltpu.SemaphoreType.DMA((per,)), pltpu.SemaphoreType.DMA],
        ),
        out_shape=[jax.ShapeDtypeStruct((N_DEV, t, d), F32), jax.ShapeDtypeStruct((N_DEV, d, d), BF16)],
        compiler_params=_params(("arbitrary", "arbitrary"), need),
        name="gather_project",
    )(order, x_b, shard)
    return h, w_all


def _pair_sums(g):
    n, r, cols = g.shape
    half = n // 2

    def swap(g_ref, got_ref, send_sems, recv_sems):
        x, y, c = lax.axis_index("x"), lax.axis_index("y"), lax.axis_index("c")
        copies = [pltpu.make_async_remote_copy(
            src_ref=g_ref.at[2 * j + 1 - c], dst_ref=got_ref.at[j], send_sem=send_sems.at[j], recv_sem=recv_sems.at[j],
            device_id=(x, y, 1 - c), device_id_type=pl.DeviceIdType.MESH) for j in range(half)]
        for cp in copies:
            cp.start()
        for cp in copies:
            cp.wait()

    got = pl.pallas_call(
        swap,
        in_specs=[pl.BlockSpec(memory_space=pl.ANY)],
        out_specs=pl.BlockSpec(memory_space=pl.ANY),
        out_shape=jax.ShapeDtypeStruct((half, r, cols), g.dtype),
        scratch_shapes=[pltpu.SemaphoreType.DMA((half,))] * 2,
        name="pair_swap",
    )(g)

    def add(mine_ref, got_ref, out_ref):
        mine = jnp.where(lax.axis_index("c") == 0, mine_ref[0].astype(F32), mine_ref[1].astype(F32))
        out_ref[...] = (mine + got_ref[...].astype(F32)).astype(out_ref.dtype)

    tile = _pick(r, (512, 256, 128))
    return pl.pallas_call(
        add,
        grid=(half, r // tile),
        in_specs=[pl.BlockSpec((None, 2, tile, cols), lambda j, i: (j, 0, i, 0)),
                  pl.BlockSpec((None, tile, cols), lambda j, i: (j, i, 0))],
        out_specs=pl.BlockSpec((None, tile, cols), lambda j, i: (j, i, 0)),
        out_shape=jax.ShapeDtypeStruct((half, r, cols), g.dtype),
        compiler_params=_params(("parallel", "parallel")),
        name="pair_add",
    )(g.reshape(half, 2, r, cols), got)


def _local_step(x, p, tgt, small, comm):
    t = x.shape[0]
    tile = _pick(t, (256, 128))
    d = D_MODEL
    act_b, act_f = (d, BF16), (d, F32)
    x_b, p_b = x.astype(BF16), p.astype(BF16)

    chunk_id = jnp.arange(SGU_BLOCK) // CHUNK
    mask = chunk_id[:, None] >= chunk_id[None, :]
    wm = jnp.where(mask[None], small["sgu_w_s"], 0.0)
    wm_b = wm.astype(BF16)
    wm_t = jnp.swapaxes(wm, 1, 2).astype(BF16)
    bs_t = small["sgu_b_s"].T

    h, w_in = comm.project_in(x_b)
    y_a, got_a = _sgu_fwd(h, wm_b, bs_t, small["sgu_norm_g"], small["sgu_norm_b"], ex=comm.weights_exchange(0))
    y_b, o_all, states, got_b = _hgrn_fwd(h, small["lb_logits"], small["hgrn_norm_g"], ex=comm.weights_exchange(1))
    wts, conv_w = comm.weights(list(got_a) + list(got_b))
    z_a = _mm(y_a, wts["w_a"], out_dtype=F32, name="mm_za")
    z_b = _mm(y_b, wts["w_b"], out_dtype=F32, name="mm_zb")
    gates = [(h, SLOT_GA), (h, SLOT_GB)]
    merged, = _rowwise(_merge_fwd, gates + [z_a, z_b], [], [act_b], [], tile=tile, name="merge_fwd")
    r1 = _mm(merged, wts["w_o"], out_dtype=F32, name="mm_r1")
    x1, x1_b = _rowwise(_ln1_fwd, [x, r1], [small["ln1_g"], small["ln1_b"]], [act_f, act_b], [], tile=tile, name="ln1_fwd")
    gate = _mm(x1_b, wts["w_g"], out_dtype=F32, name="mm_gate")
    val = _mm(x1_b, wts["w_v"], out_dtype=F32, name="mm_val")
    pg = _mm(x1_b, wts["w_pg"], out_dtype=F32, name="mm_pg")
    pp = _mm(p_b, wts["w_pp"], out_dtype=F32, name="mm_pp")
    hid, = _rowwise(functools.partial(_conv_fwd, tile), [gate, (gate, _halo_spec(D_FF, tile, t, False)), val],
                    [conv_w, small["conv_b"]], [(D_FF, BF16)], [], tile=tile, name="conv_fwd")
    fo = _mm(hid, wts["w_down"], out_dtype=F32, name="mm_down")
    dz2, dz2_b, dpg, dpp, loss, dg2, db2 = _rowwise(
        _ln2_loss, [x1, fo, pg, pp, tgt], [small["ln2_g"], small["ln2_b"]],
        [act_f, act_b, act_b, act_b], [(8, LANE), (1, d), (1, d)], tile=tile, name="ln2_loss")

    dhid = _mm(dz2_b, wts["w_down"], out_dtype=BF16, name="mm_dhid", trans_b=True)
    g_down = _mm_tn(hid, dz2_b, out_dtype=BF16, name="mm_g_down")
    dcg, dval, dcw0, dcw1, dcw2, dcb = _rowwise(
        functools.partial(_conv_bwd_a, tile), [gate, (gate, _halo_spec(D_FF, tile, t, False)), val, dhid],
        [conv_w, small["conv_b"]], [(D_FF, BF16), (D_FF, BF16)], [(1, D_FF)] * 4, tile=tile, name="conv_bwd_a")
    dgate, = _rowwise(functools.partial(_conv_bwd_b, tile), [dcg, (dcg, _halo_spec(D_FF, tile, t, True, rows=16))],
                      [conv_w], [(D_FF, BF16)], [], tile=tile, name="conv_bwd_b")
    g_g = _mm_tn(x1_b, dgate, out_dtype=BF16, name="mm_g_gate")
    g_v = _mm_tn(x1_b, dval, out_dtype=BF16, name="mm_g_val")
    g_pg = _mm_tn(x1_b, dpg, out_dtype=BF16, name="mm_g_pg")
    g_pp = _mm_tn(p_b, dpp, out_dtype=BF16, name="mm_g_pp")
    dx1 = _mm(dgate, wts["w_g"], out_dtype=F32, name="mm_dx1_gate", trans_b=True, adds=[(dz2, ALPHA)])
    dx1 = _mm(dval, wts["w_v"], out_dtype=F32, name="mm_dx1_val", trans_b=True, adds=[(dx1, 1.0)])
    dx1 = _mm(dpg, wts["w_pg"], out_dtype=F32, name="mm_dx1_pg", trans_b=True, adds=[(dx1, 1.0)])
    dz1, dz1_b, dg1, db1 = _rowwise(_ln1_bwd, [x, r1, dx1], [small["ln1_g"]], [act_f, act_b], [(1, d), (1, d)],
                                    tile=tile, name="ln1_bwd")
    g_o = _mm_tn(merged, dz1_b, out_dtype=BF16, name="mm_g_o")
    dm = _mm(dz1_b, wts["w_o"], out_dtype=BF16, name="mm_dm", trans_b=True)
    dh, dza, dzb = _rowwise(_merge_bwd, gates + [z_a, z_b, dm], [],
                            [("stack", 2, SLOT_GA // 2, 8, d, BF16), act_b, act_b], [], tile=tile, name="merge_bwd")
    g_a = _mm_tn(y_a, dza, out_dtype=BF16, name="mm_g_a")
    g_b = _mm_tn(y_b, dzb, out_dtype=BF16, name="mm_g_b")
    dy_a = _mm(dza, wts["w_a"], out_dtype=BF16, name="mm_dya", trans_b=True)
    dy_b = _mm(dzb, wts["w_b"], out_dtype=F32, name="mm_dyb", trans_b=True)
    dh, dws, dbs, dgv_n, dbv_n = _sgu_bwd(h, dy_a, dh, wm_b, wm_t, bs_t, small["sgu_norm_g"], small["sgu_norm_b"])
    big = dict(w_a=g_a, w_b=g_b, w_o=g_o, w_g=g_g, w_v=g_v, w_down=g_down, w_pp=g_pp, w_pg=g_pg)
    sm = dict(sgu_w_s=jnp.where(mask[None], dws, 0.0), sgu_b_s=dbs[:, :GROUPS].T, sgu_norm_g=dgv_n, sgu_norm_b=dbv_n,
              ln1_g=dg1, ln1_b=db1, conv_w=jnp.concatenate([dcw0, dcw1, dcw2], axis=0), conv_b=dcb, ln2_g=dg2, ln2_b=db2,
              loss=loss)
    dh, dlogits, dgn, got = _hgrn_bwd(h, o_all, dy_b, states, dh, small["lb_logits"], small["hgrn_norm_g"],
                                      ex=comm.grads_exchange(big, sm))
    comm.grads_done(got)
    g_in = _mm_tn(x_b, dh, out_dtype=BF16, name="mm_g_in")
    ex = comm.last_exchange(g_in, dict(lb_logits=dlogits, hgrn_norm_g=dgn))
    res = _mm(dh, w_in, out_dtype=F32, name="mm_dx", trans_b=True, reduce_b=True, adds=[(dz1, ALPHA)], ex=ex)
    grad_x, got = res if ex else (res, ())
    comm.last_done(got)
    return grad_x


_SMALL_EARLY = ["sgu_w_s", "sgu_b_s", "sgu_norm_g", "sgu_norm_b", "ln1_g", "ln1_b", "ffn_conv_b", "ln2_g", "ln2_b"]
_SMALL_LATE = ["hgrn_lb_logits", "hgrn_norm_g"]
N_TAPS = D_FF // N_DEV
UP_COLS = 2 * D_FF // N_DEV


class _StepExchanges:
    def __init__(self, w_in_shard, shards):
        self.w_in_shard = w_in_shard
        self.shards = shards

    def project_in(self, x_b):
        return _gather_project(x_b, self.w_in_shard)

    def weights_exchange(self, part):
        return _Exchange([(s, "gather+relay") for s in (self.shards[:2] if part == 0 else self.shards[2:])])

    def weights(self, got):
        d, f = D_MODEL, D_FF
        w_br_g, w_o_g, w_up_g, w_down_g, w_pp_g, w_pg_g, conv_g = got
        w_br = w_br_g.transpose(1, 0, 2, 3).reshape(2, d, d)
        w_up = w_up_g.transpose(1, 0, 2).reshape(d, 2, f).transpose(1, 0, 2)
        wts = dict(w_a=w_br[0], w_b=w_br[1], w_o=w_o_g.reshape(d, d), w_g=w_up[0], w_v=w_up[1],
                   w_down=w_down_g.reshape(f, d), w_pp=w_pp_g.transpose(1, 0, 2).reshape(256, d), w_pg=w_pg_g.reshape(d, d))
        return wts, conv_g.transpose(1, 0, 2).reshape(3, f)

    def grads_exchange(self, big, sm):
        d = D_MODEL
        parts = [jnp.stack([big["w_a"], big["w_b"]]).reshape(2, N_DEV, 128, d).transpose(1, 0, 2, 3),
                 big["w_o"].reshape(N_DEV, 128, d),
                 jnp.concatenate([big["w_g"], big["w_v"]], axis=1).reshape(d, N_DEV, UP_COLS).transpose(1, 0, 2),
                 big["w_down"].reshape(N_DEV, N_TAPS, d),
                 big["w_pp"].reshape(256, N_DEV, 128).transpose(1, 0, 2),
                 big["w_pg"].reshape(N_DEV, 128, d)]
        packed, self.rows_early = _pack([sm[k] for k in ("sgu_w_s", "sgu_b_s", "sgu_norm_g", "sgu_norm_b", "ln1_g", "ln1_b",
                                                         "conv_b", "ln2_g", "ln2_b", "conv_w", "loss")])
        return _Exchange([(a, "scatter") for a in parts] + [(packed, "gather")])

    def grads_done(self, got):
        self.recv, self.small_early = got[:6], got[6]

    def last_exchange(self, g_in, sm):
        packed, self.rows_late = _pack([sm["lb_logits"], sm["hgrn_norm_g"]])
        return _Exchange([(_pair_sums(g_in), "scatter+pairs"), (packed, "gather")])

    def last_done(self, got):
        self.recv_in, self.small_late = got


def _rows128(a):
    flat = a.reshape(-1)
    rows = -(-flat.shape[0] // (8 * LANE)) * 8
    return jnp.pad(flat, (0, rows * LANE - flat.shape[0])).reshape(rows, LANE)


def _pack(parts):
    blocks = [_rows128(a) for a in parts]
    return jnp.concatenate(blocks, axis=0), [b.shape[0] for b in blocks]


def _unpack(packed, shapes, rows):
    out, r0 = [], 0
    for shp, r in zip(shapes, rows):
        n = math.prod(shp)
        out.append(packed[r0:r0 + r].reshape(-1)[:n].reshape(shp))
        r0 += r
    return out


def kernel(x, p, w_in, sgu_w_s, sgu_b_s, sgu_norm_g, sgu_norm_b, hgrn_lb_logits, hgrn_norm_g, w_branch, w_out, ln1_g, ln1_b, ffn_w_up, ffn_conv_w, ffn_conv_b, ffn_w_down, ln2_g, ln2_b, ple_w_proj, ple_w_gate, loss_target, m_w_in, m_sgu_w_s, m_sgu_b_s, m_sgu_norm_g, m_sgu_norm_b, m_hgrn_lb_logits, m_hgrn_norm_g, m_w_branch, m_w_out, m_ln1_g, m_ln1_b, m_ffn_w_up, m_ffn_conv_w, m_ffn_conv_b, m_ffn_w_down, m_ln2_g, m_ln2_b, m_ple_w_proj, m_ple_w_gate, v_w_in, v_sgu_w_s, v_sgu_b_s, v_sgu_norm_g, v_sgu_norm_b, v_hgrn_lb_logits, v_hgrn_norm_g, v_w_branch, v_w_out, v_ln1_g, v_ln1_b, v_ffn_w_up, v_ffn_conv_w, v_ffn_conv_b, v_ffn_w_down, v_ln2_g, v_ln2_b, v_ple_w_proj, v_ple_w_gate):
    weights = dict(w_in=w_in, sgu_w_s=sgu_w_s, sgu_b_s=sgu_b_s, sgu_norm_g=sgu_norm_g, sgu_norm_b=sgu_norm_b,
                   hgrn_lb_logits=hgrn_lb_logits, hgrn_norm_g=hgrn_norm_g, w_branch=w_branch, w_out=w_out,
                   ln1_g=ln1_g, ln1_b=ln1_b, ffn_w_up=ffn_w_up, ffn_conv_w=ffn_conv_w, ffn_conv_b=ffn_conv_b,
                   ffn_w_down=ffn_w_down, ln2_g=ln2_g, ln2_b=ln2_b, ple_w_proj=ple_w_proj, ple_w_gate=ple_w_gate)
    mom_m = dict(w_in=m_w_in, sgu_w_s=m_sgu_w_s, sgu_b_s=m_sgu_b_s, sgu_norm_g=m_sgu_norm_g, sgu_norm_b=m_sgu_norm_b,
                 hgrn_lb_logits=m_hgrn_lb_logits, hgrn_norm_g=m_hgrn_norm_g, w_branch=m_w_branch, w_out=m_w_out,
                 ln1_g=m_ln1_g, ln1_b=m_ln1_b, ffn_w_up=m_ffn_w_up, ffn_conv_w=m_ffn_conv_w, ffn_conv_b=m_ffn_conv_b,
                 ffn_w_down=m_ffn_w_down, ln2_g=m_ln2_g, ln2_b=m_ln2_b, ple_w_proj=m_ple_w_proj, ple_w_gate=m_ple_w_gate)
    mom_v = dict(w_in=v_w_in, sgu_w_s=v_sgu_w_s, sgu_b_s=v_sgu_b_s, sgu_norm_g=v_sgu_norm_g, sgu_norm_b=v_sgu_norm_b,
                 hgrn_lb_logits=v_hgrn_lb_logits, hgrn_norm_g=v_hgrn_norm_g, w_branch=v_w_branch, w_out=v_w_out,
                 ln1_g=v_ln1_g, ln1_b=v_ln1_b, ffn_w_up=v_ffn_w_up, ffn_conv_w=v_ffn_conv_w, ffn_conv_b=v_ffn_conv_b,
                 ffn_w_down=v_ffn_w_down, ln2_g=v_ln2_g, ln2_b=v_ln2_b, ple_w_proj=v_ple_w_proj, ple_w_gate=v_ple_w_gate)
    d, f = D_MODEL, D_FF
    me = _my_index()

    comm = _StepExchanges(w_in[0].astype(BF16),
                          [w_branch[0].astype(BF16), w_out[0].astype(BF16), ffn_w_up[0].astype(BF16),
                           ffn_w_down[0].astype(BF16), ple_w_proj[0].astype(BF16), ple_w_gate[0].astype(BF16), ffn_conv_w[0]])
    small = dict(sgu_w_s=sgu_w_s[0], sgu_b_s=sgu_b_s[0], sgu_norm_g=sgu_norm_g, sgu_norm_b=sgu_norm_b,
                 lb_logits=hgrn_lb_logits, hgrn_norm_g=hgrn_norm_g, ln1_g=ln1_g, ln1_b=ln1_b, ln2_g=ln2_g, ln2_b=ln2_b,
                 conv_b=ffn_conv_b)
    grad_x = _local_step(x[0], p[0, 0], loss_target[0], small, comm)

    out = {}

    def adam(name, parts8, shape2d):
        w2, m2, v2 = (a.reshape(shape2d) for a in (weights[name], mom_m[name], mom_v[name]))
        res = _adam_call(w2, m2, v2, parts8.reshape(parts8.shape[:1] + shape2d), "adam_" + name)
        out[name] = tuple(r.reshape(weights[name].shape) for r in res)

    adam("w_in", comm.recv_in, (d, d))
    adam("w_branch", comm.recv[0], (256, d))
    adam("w_out", comm.recv[1], (128, d))
    adam("ffn_w_up", comm.recv[2], (d, UP_COLS))
    adam("ffn_w_down", comm.recv[3], (N_TAPS, d))
    adam("ple_w_proj", comm.recv[4], (256, 128))
    adam("ple_w_gate", comm.recv[5], (128, d))

    def adam_small(names, extra_w, extra_m, extra_v, extra_shapes, parts8, rows, label):
        pk = lambda src, extra: _pack([src[n] for n in names] + extra)[0]
        res = _adam_call(pk(weights, extra_w), pk(mom_m, extra_m), pk(mom_v, extra_v), parts8, label)
        shapes = [weights[n].shape for n in names] + extra_shapes
        unpacked = [_unpack(r, shapes, rows) for r in res]
        for j, n in enumerate(names):
            out[n] = tuple(u[j] for u in unpacked)
        return [[u[len(names) + j] for u in unpacked] for j in range(len(extra_shapes))]

    blank = jnp.zeros((8, LANE), F32)
    taps, loss_rows = adam_small(
        _SMALL_EARLY, [_place_taps(ffn_conv_w[0], me, f), blank], [_place_taps(m_ffn_conv_w[0], me, f), blank],
        [_place_taps(v_ffn_conv_w[0], me, f), blank + 1.0], [(3, f), (8, LANE)], comm.small_early, comm.rows_early,
        "adam_small_early")
    adam_small(_SMALL_LATE, [], [], [], [], comm.small_late, comm.rows_late, "adam_small_late")
    out["ffn_conv_w"] = tuple(lax.dynamic_slice_in_dim(u, me * N_TAPS, N_TAPS, axis=1)[None] for u in taps)
    loss = loss_rows[0][0, 0]

    order = ["w_in", "sgu_w_s", "sgu_b_s", "sgu_norm_g", "sgu_norm_b", "hgrn_lb_logits", "hgrn_norm_g", "w_branch", "w_out",
             "ln1_g", "ln1_b", "ffn_w_up", "ffn_conv_w", "ffn_conv_b", "ffn_w_down", "ln2_g", "ln2_b", "ple_w_proj", "ple_w_gate"]
    return (loss, grad_x[None], *[out[n][0] for n in order], *[out[n][1] for n in order],
            *[out[n][2] for n in order], *[out[n][3] for n in order])


def _place_taps(shard, me, f):
    return lax.dynamic_update_slice_in_dim(jnp.zeros((3, f), F32), shard, me * N_TAPS, axis=1)
```

```python
import functools
import math

import jax
import jax.numpy as jnp
from jax import lax
from jax.experimental import pallas as pl
from jax.experimental.pallas import tpu as pltpu

F32 = jnp.float32
BF16 = jnp.bfloat16

N_DEV = 8
N_CHIP = 4
D_MODEL = 1024
CHUNK = 64
SUB = 16
SGU_BLOCK = 128
GROUPS = 8
HEAD = 128
HEADS = 8
HEAD_UNROLL = 4
D_FF = 2816
LN_EPS = 1e-5
RMS_EPS = 1e-6
ALPHA = 2.0 ** 0.25
GELU_K = math.sqrt(2.0 / math.pi)
GELU_C = 0.044715
NEG = -1e30
ADAM_LR, ADAM_B1, ADAM_B2, ADAM_EPS, ADAM_WD, ADAM_STEP = 0.001, 0.9, 0.999, 1e-08, 0.01, 10
LANE = 128
SLOT_Q, SLOT_F, SLOT_I, SLOT_OG, SLOT_U, SLOT_V, SLOT_GA, SLOT_GB = range(8)


def _slot_of_group(k):
    return jnp.where(k < 2, k + 4, jnp.where(k < 6, k - 2, k))


MIB = 1024 * 1024
VMEM_V7X = 64 * MIB
VMEM_FLOOR = 32 * MIB
MM_TILES = (1024, 1408, 512, 256, 128)


def _params(sem, need=0):
    limit = min(max(need + need // 4, VMEM_FLOOR), VMEM_V7X - 4 * MIB)
    return pltpu.CompilerParams(dimension_semantics=sem, vmem_limit_bytes=limit)


def _pick(n, prefs):
    for t in prefs:
        if n % t == 0:
            return t
    return n


def _gelu(x):
    return 0.5 * x * (1.0 + jnp.tanh(GELU_K * (x + GELU_C * x * x * x)))


def _gelu_and_grad(x):
    x2 = x * x
    t = jnp.tanh(GELU_K * x * (1.0 + GELU_C * x2))
    half = 0.5 * (1.0 + t)
    return x * half, half + 0.5 * x * (1.0 - t * t) * GELU_K * (1.0 + 3.0 * GELU_C * x2)


def _silu_grad(x, s):
    return s * (1.0 + x * (1.0 - s))


def _dot(a, b):
    return jnp.dot(a.astype(BF16), b.astype(BF16), preferred_element_type=F32)


def _dot_nt(a, b):
    return lax.dot_general(a.astype(BF16), b.astype(BF16), (((1,), (1,)), ((), ())), preferred_element_type=F32)


def _dot_tn(a, b):
    return lax.dot_general(a.astype(BF16), b.astype(BF16), (((0,), (0,)), ((), ())), preferred_element_type=F32)


def _mean(x):
    return jnp.mean(x, axis=-1, keepdims=True)


def _sum0(x):
    return jnp.sum(x, axis=0, keepdims=True)


def _mm(a, b, *, out_dtype, name, trans_b=False, reduce_b=False, adds=(), ex=None):
    squeeze = b.ndim == 2
    a3 = a if a.ndim == 3 else a[None]
    b3 = b if b.ndim == 3 else b[None]
    ba, m, k = a3.shape
    bb = b3.shape[0]
    n = b3.shape[1] if trans_b else b3.shape[2]
    tm = _pick(m, MM_TILES)
    tn = _pick(n, MM_TILES)
    if reduce_b:
        bo, steps = 1, bb
        a_map = lambda o, i, j, r: (r if ba > 1 else 0, i, 0)
        b_map = (lambda o, i, j, r: (r, j, 0)) if trans_b else (lambda o, i, j, r: (r, 0, j))
    else:
        bo, steps = bb, 1
        a_map = lambda o, i, j, r: (o if ba > 1 else 0, i, 0)
        b_map = (lambda o, i, j, r: (o, j, 0)) if trans_b else (lambda o, i, j, r: (o, 0, j))
    o_map = lambda o, i, j, r: (o, i, j)
    add_arrays = [x if x.ndim == 3 else x[None] for x, _ in adds]
    add_scales = [s for _, s in adds]
    n_add = len(adds)
    dot = _dot_nt if trans_b else _dot

    def finish(acc, add_refs, o_ref):
        for ref, s in zip(add_refs, add_scales):
            acc = acc + s * ref[...].astype(F32)
        o_ref[...] = acc.astype(o_ref.dtype)

    grid = (bo, m // tm, n // tn, steps)

    def body(*refs):
        ins, (o_ref,), scratch, xrefs = _split_refs(refs, 2 + n_add, 1, 1 if reduce_b else 0, ex)
        a_ref, b_ref, add_refs = ins[0], ins[1], ins[2:]
        step = ((pl.program_id(0) * grid[1] + pl.program_id(1)) * grid[2] + pl.program_id(2)) * grid[3] + pl.program_id(3)
        if ex:
            @pl.when(step == 0)
            def _():
                ex.start(*xrefs)

        if reduce_b:
            acc, = scratch
            r = pl.program_id(3)

            @pl.when(r == 0)
            def _():
                acc[...] = jnp.zeros_like(acc)

            acc[...] += dot(a_ref[...], b_ref[...])

            @pl.when(r == steps - 1)
            def _():
                finish(acc[...], add_refs, o_ref)
        else:
            finish(dot(a_ref[...], b_ref[...]), add_refs, o_ref)

        if ex:
            @pl.when(step == math.prod(grid) - 1)
            def _():
                ex.finish(*xrefs)

    b_block = (None, tn, k) if trans_b else (None, k, tn)
    out_bytes = tm * tn * jnp.dtype(out_dtype).itemsize
    need = 2 * (tm * k * a3.dtype.itemsize + k * tn * b3.dtype.itemsize + out_bytes + n_add * tm * tn * 4)
    need += 2 * tm * tn * 4
    sem = ("arbitrary",) * 4 if ex else ("parallel", "parallel", "parallel", "arbitrary")
    res = pl.pallas_call(
        body,
        grid=grid,
        in_specs=[pl.BlockSpec((None, tm, k), a_map), pl.BlockSpec(b_block, b_map)]
        + [pl.BlockSpec((None, tm, tn), o_map) for _ in adds] + (ex.in_specs if ex else []),
        out_specs=[pl.BlockSpec((None, tm, tn), o_map)] + (ex.out_specs if ex else []),
        out_shape=[jax.ShapeDtypeStruct((bo, m, n), out_dtype)] + (ex.out_shape if ex else []),
        scratch_shapes=([pltpu.VMEM((tm, tn), F32)] if reduce_b else []) + (ex.scratch if ex else []),
        compiler_params=_params(sem, need),
        name=name,
    )(a3, b3, *add_arrays, *(ex.arrays if ex else []))
    out = res[0][0] if (reduce_b or squeeze) else res[0]
    return (out, res[1:]) if ex else out


def _mm_tn(a, b, *, out_dtype, name):
    squeeze = b.ndim == 2
    b3 = b if b.ndim == 3 else b[None]
    t, m = a.shape
    bb, _, n = b3.shape
    tm = _pick(m, MM_TILES)
    tn = _pick(n, MM_TILES)
    tt = _pick(t, (1024, 512, 256, 128))
    steps = t // tt
    need = 2 * (tt * tm * a.dtype.itemsize + tt * tn * b3.dtype.itemsize + tm * tn * jnp.dtype(out_dtype).itemsize)
    need += 2 * tm * tn * 4

    def body(a_ref, b_ref, o_ref, acc):
        r = pl.program_id(3)

        @pl.when(r == 0)
        def _():
            acc[...] = jnp.zeros_like(acc)

        acc[...] += _dot_tn(a_ref[...], b_ref[...])

        @pl.when(r == steps - 1)
        def _():
            o_ref[...] = acc[...].astype(o_ref.dtype)

    out = pl.pallas_call(
        body,
        grid=(bb, m // tm, n // tn, steps),
        in_specs=[pl.BlockSpec((tt, tm), lambda o, i, j, r: (r, i)),
                  pl.BlockSpec((None, tt, tn), lambda o, i, j, r: (o, r, j))],
        out_specs=pl.BlockSpec((None, tm, tn), lambda o, i, j, r: (o, i, j)),
        out_shape=jax.ShapeDtypeStruct((bb, m, n), out_dtype),
        scratch_shapes=[pltpu.VMEM((tm, tn), F32)],
        compiler_params=_params(("parallel", "parallel", "parallel", "arbitrary"), need),
        name=name,
    )(a, b3)
    return out[0] if squeeze else out


def _rowwise(fn, rows, consts, row_outs, acc_outs, *, tile, name, ex=None):
    first = rows[0][0] if isinstance(rows[0], tuple) else rows[0]
    t = first.shape[-2]
    steps = t // tile
    arrays, in_specs = [], []
    for r in rows:
        if isinstance(r, tuple) and isinstance(r[1], pl.BlockSpec):
            arrays.append(r[0])
            in_specs.append(r[1])
        elif isinstance(r, tuple):
            arr, bidx = r
            arrays.append(arr)
            in_specs.append(pl.BlockSpec((None, tile, arr.shape[-1]), functools.partial(lambda i, b: (b, i, 0), b=bidx)))
        else:
            arrays.append(r)
            in_specs.append(pl.BlockSpec((tile, r.shape[-1]), lambda i: (i, 0)))
    for c in consts:
        arrays.append(c)
        in_specs.append(pl.BlockSpec(c.shape, lambda i: (0, 0)))
    n_in, n_row = len(arrays), len(row_outs)
    out_shape, out_specs = [], []
    for ro in row_outs:
        if ro[0] == "stack":
            _, cnt, blk, total, w, dt = ro
            out_shape.append(jax.ShapeDtypeStruct((total, t, w), dt))
            out_specs.append(pl.BlockSpec((cnt, tile, w), functools.partial(lambda i, b: (b, i, 0), b=blk)))
        else:
            w, dt = ro
            out_shape.append(jax.ShapeDtypeStruct((t, w), dt))
            out_specs.append(pl.BlockSpec((tile, w), lambda i: (i, 0)))
    out_shape += [jax.ShapeDtypeStruct(s, F32) for s in acc_outs]
    out_specs += [pl.BlockSpec(s, lambda i: (0, 0)) for s in acc_outs]
    blocks = [math.prod(d for d in sp.block_shape if d) * arr.dtype.itemsize for sp, arr in zip(in_specs, arrays)]
    blocks += [math.prod(d for d in sp.block_shape if d) * jnp.dtype(sh.dtype).itemsize
               for sp, sh in zip(out_specs, out_shape)]
    need = 2 * sum(blocks) + 6 * tile * max(a.shape[-1] for a in arrays) * 4

    def body(*refs):
        ins, outs, _, xrefs = _split_refs(refs, n_in, len(out_shape), 0, ex)
        i = pl.program_id(0)
        _ride(ex, xrefs, i, steps)
        res = fn(i, steps, *[r[...] for r in ins])
        res = res if isinstance(res, (tuple, list)) else (res,)
        for ref, val in zip(outs[:n_row], res[:n_row]):
            ref[...] = val.astype(ref.dtype)
        if acc_outs:
            @pl.when(i == 0)
            def _():
                for ref in outs[n_row:]:
                    ref[...] = jnp.zeros_like(ref)

            for ref, val in zip(outs[n_row:], res[n_row:]):
                ref[...] += val

    res = pl.pallas_call(
        body,
        grid=(steps,),
        in_specs=in_specs + (ex.in_specs if ex else []),
        out_specs=out_specs + (ex.out_specs if ex else []),
        out_shape=out_shape + (ex.out_shape if ex else []),
        scratch_shapes=ex.scratch if ex else [],
        compiler_params=_params(("arbitrary",), need),
        name=name,
    )(*arrays, *(ex.arrays if ex else []))
    return (res[:len(out_shape)], res[len(out_shape):]) if ex else res


def _ln_stats(z):
    mu = _mean(z)
    zc = z - mu
    rstd = lax.rsqrt(_mean(zc * zc) + LN_EPS)
    return zc * rstd, rstd


def _ln_bwd(dy, xhat, rstd, g):
    dxh = dy * g
    return rstd * (dxh - _mean(dxh) - xhat * _mean(dxh * xhat))


def _ride(ex, xrefs, step, steps):
    if not ex:
        return
    acts = ((0, ex.start), (steps // 2, functools.partial(ex.relay, two_hop=False)),
            ((3 * steps) // 4, functools.partial(ex.relay, two_hop=True)), (steps - 1, ex.finish))
    for at, act in acts:
        @pl.when(step == at)
        def _(act=act):
            act(*xrefs)


def _sgu_fwd(h, wm, bs_t, g_v, b_v, ex=None):
    t = h.shape[1]
    steps = t // SGU_BLOCK

    def body(*refs):
        (u_ref, v_ref, wm_ref, bs_ref, g_ref, b_ref), (y_ref,), _, xrefs = _split_refs(refs, 6, 1, 0, ex)
        xhat, _ = _ln_stats(_gelu(v_ref[...]))
        vn = (xhat * g_ref[...] + b_ref[...]).astype(BF16)
        gu = _gelu(u_ref[...])
        for g in range(GROUPS):
            sl = slice(g * HEAD, (g + 1) * HEAD)
            mixed = _dot(wm_ref[g], vn[:, sl]) + bs_ref[:, g:g + 1]
            y_ref[:, sl] = (gu[:, sl] * mixed).astype(BF16)
        _ride(ex, xrefs, pl.program_id(0), steps)

    blk = lambda b: pl.BlockSpec((None, SGU_BLOCK, D_MODEL), functools.partial(lambda i, b: (b, i, 0), b=b))
    whole = lambda s: pl.BlockSpec(s, lambda i: (0,) * len(s))
    res = pl.pallas_call(
        body,
        grid=(steps,),
        in_specs=[blk(SLOT_U), blk(SLOT_V), whole(wm.shape), whole(bs_t.shape), whole(g_v.shape), whole(b_v.shape)]
        + (ex.in_specs if ex else []),
        out_specs=[pl.BlockSpec((SGU_BLOCK, D_MODEL), lambda i: (i, 0))] + (ex.out_specs if ex else []),
        out_shape=[jax.ShapeDtypeStruct((t, D_MODEL), BF16)] + (ex.out_shape if ex else []),
        scratch_shapes=ex.scratch if ex else [],
        compiler_params=_params(("arbitrary",)),
        name="sgu_fwd",
    )(h, h, wm, bs_t, g_v, b_v, *(ex.arrays if ex else []))
    return res[0], res[1:]


def _sgu_bwd(h, dy, dh, wm, wm_t, bs_t, g_v, b_v):
    t = h.shape[1]

    def body(u_ref, v_ref, dy_ref, dh_in, wm_ref, wmt_ref, bs_ref, g_ref, b_ref,
             duv_ref, dw_ref, dbs_ref, dg_ref, db_ref, dvn_ref):
        del dh_in
        du_ref, dv_ref = duv_ref.at[0], duv_ref.at[1]
        i = pl.program_id(0)

        @pl.when(i == 0)
        def _():
            dw_ref[...] = jnp.zeros_like(dw_ref)
            dbs_ref[...] = jnp.zeros_like(dbs_ref)
            dg_ref[...] = jnp.zeros_like(dg_ref)
            db_ref[...] = jnp.zeros_like(db_ref)

        u = u_ref[...]
        v = v_ref[...]
        gv, gvp = _gelu_and_grad(v)
        xhat, rstd = _ln_stats(gv)
        vn = (xhat * g_ref[...] + b_ref[...]).astype(BF16)
        gu, gup = _gelu_and_grad(u)
        lane = lax.broadcasted_iota(jnp.int32, (SGU_BLOCK, LANE), 1)
        dbs = jnp.zeros((SGU_BLOCK, LANE), F32)
        for g in range(GROUPS):
            sl = slice(g * HEAD, (g + 1) * HEAD)
            vn_g = vn[:, sl]
            mixed = _dot(wm_ref[g], vn_g) + bs_ref[:, g:g + 1]
            dy_g = dy_ref[:, sl]
            du_ref[:, sl] = (dy_g * mixed * gup[:, sl]).astype(BF16)
            dmix = dy_g * gu[:, sl]
            dmb = dmix.astype(BF16)
            dvn_ref[:, sl] = _dot(wmt_ref[g], dmb)
            dw_ref[g] += _dot_nt(dmb, vn_g)
            dbs = dbs + jnp.where(lane == g, jnp.sum(dmix, axis=1, keepdims=True), 0.0)
        dbs_ref[...] += dbs
        dvn = dvn_ref[...]
        dg_ref[...] += _sum0(dvn * xhat)
        db_ref[...] += _sum0(dvn)
        dv_ref[...] = (_ln_bwd(dvn, xhat, rstd, g_ref[...]) * gvp).astype(BF16)

    blk = lambda b: pl.BlockSpec((None, SGU_BLOCK, D_MODEL), functools.partial(lambda i, b: (b, i, 0), b=b))
    row = pl.BlockSpec((SGU_BLOCK, D_MODEL), lambda i: (i, 0))
    whole = lambda s: pl.BlockSpec(s, lambda i: (0,) * len(s))
    vec = (1, D_MODEL)
    return pl.pallas_call(
        body,
        grid=(t // SGU_BLOCK,),
        in_specs=[blk(SLOT_U), blk(SLOT_V), row, pl.BlockSpec(memory_space=pl.ANY),
                  whole(wm.shape), whole(wm_t.shape), whole(bs_t.shape), whole(vec), whole(vec)],
        out_specs=[pl.BlockSpec((2, SGU_BLOCK, D_MODEL), lambda i: (SLOT_U // 2, i, 0)),
                   whole(wm.shape), whole((SGU_BLOCK, LANE)), whole(vec), whole(vec)],
        out_shape=[jax.ShapeDtypeStruct(dh.shape, BF16),
                   jax.ShapeDtypeStruct(wm.shape, F32), jax.ShapeDtypeStruct((SGU_BLOCK, LANE), F32),
                   jax.ShapeDtypeStruct(vec, F32), jax.ShapeDtypeStruct(vec, F32)],
        scratch_shapes=[pltpu.VMEM((SGU_BLOCK, D_MODEL), F32)],
        input_output_aliases={3: 0},
        compiler_params=_params(("arbitrary",)),
        name="sgu_bwd",
    )(h, h, dy, dh, wm, wm_t, bs_t, g_v, b_v)


def _split3(x):
    hi = x.astype(BF16)
    r1 = x - hi.astype(F32)
    mid = r1.astype(BF16)
    lo = (r1 - mid.astype(F32)).astype(BF16)
    return hi, mid, lo


def _tri_matmul(tri, x):
    hi, mid, lo = _split3(x)
    dot = lambda p: jnp.dot(tri, p, preferred_element_type=F32)
    return dot(hi) + dot(mid) + dot(lo)


def _lower_bound(logits):
    l0, l1 = logits[0:1, :], logits[1:2, :]
    mx = jnp.maximum(l0, l1)
    e0, e1 = jnp.exp(l0 - mx), jnp.exp(l1 - mx)
    return e0 / (e0 + e1)


def _hgrn_gates(q_raw, f_raw, lb):
    q = q_raw * jax.nn.sigmoid(q_raw)
    sig = jax.nn.sigmoid(f_raw)
    f = lb + (1.0 - lb) * sig
    row = lax.broadcasted_iota(jnp.int32, (CHUNK, CHUNK), 0)
    col = lax.broadcasted_iota(jnp.int32, (CHUNK, CHUNK), 1)
    c = _tri_matmul((row >= col).astype(BF16), jnp.log(f))
    return q, sig, f, 1.0 - f, c


def _offdiag_terms(qh, kh, ch, tb):
    rows = slice(tb * SUB, (tb + 1) * SUB)
    r = ch[tb * SUB - 1:tb * SUB, :]
    eqh = jnp.exp(ch[rows] - r)
    ekh = jnp.exp(jnp.minimum(r - ch, 0.0))
    return rows, eqh, qh[rows] * eqh, ekh, kh * ekh


def _diag_decay(cb, s, trow):
    return jnp.exp(jnp.where(trow >= s, cb - cb[s:s + 1, :], NEG))


def _split_refs(refs, n_in, n_out, n_scratch, ex):
    nx = ex.n if ex else 0
    ins, refs = refs[:n_in], refs[n_in:]
    xsrc, refs = refs[:nx], refs[nx:]
    outs, refs = refs[:n_out], refs[n_out:]
    xout, refs = refs[:nx], refs[nx:]
    return ins, outs, refs[:n_scratch], (xsrc, xout, refs[n_scratch:])


def _hgrn_fwd(h, logits, g_norm, ex=None):
    t = h.shape[1]
    nc = t // CHUNK

    def body(*refs):
        ins, outs, scratch, xrefs = _split_refs(refs, 6, 3, 4, ex)
        q_ref, f_ref, i_ref, og_ref, lg_ref, gn_ref = ins
        y_ref, o_ref, sall_ref = outs
        st_ref, q_s, k_s, c_s = scratch

        @pl.when(pl.program_id(0) == 0)
        def _():
            st_ref[...] = jnp.zeros_like(st_ref)

        lb = _lower_bound(lg_ref[...])
        q, _, _, k, c = _hgrn_gates(q_ref[...], f_ref[...], lb)
        q_s[...] = q
        k_s[...] = k
        c_s[...] = c
        col64 = lax.broadcasted_iota(jnp.int32, (SUB, CHUNK), 1)
        trow = lax.broadcasted_iota(jnp.int32, (SUB, HEAD), 0)

        def head(hd, carry):
            sl = pl.ds(pl.multiple_of(hd * HEAD, HEAD), HEAD)
            qh, kh, ch, ih = q_s[:, sl], k_s[:, sl], c_s[:, sl], i_ref[:, sl]
            st = st_ref[hd]
            sall_ref[hd] = st
            c_last = ch[CHUNK - 1:CHUNK, :]
            o = _dot_nt(qh * jnp.exp(ch), st)
            st_ref[hd] = st * jnp.exp(c_last) + _dot_tn(ih, kh * jnp.exp(c_last - ch))
            a_rows = [jnp.zeros((SUB, CHUNK), F32)]
            for tb in range(1, CHUNK // SUB):
                _, _, q_hat, _, k_hat = _offdiag_terms(qh, kh, ch, tb)
                a_rows.append(jnp.where(col64 < tb * SUB, _dot_nt(q_hat, k_hat), 0.0))
            o = o + _dot(jnp.concatenate(a_rows, axis=0), ih)
            o_rows = []
            for b in range(CHUNK // SUB):
                rows = slice(b * SUB, (b + 1) * SUB)
                qb, cb, kb, ib = qh[rows], ch[rows], kh[rows], ih[rows]
                ob = jnp.zeros((SUB, HEAD), F32)
                for s in range(SUB):
                    a = jnp.sum(qb * _diag_decay(cb, s, trow) * kb[s:s + 1, :], axis=1, keepdims=True)
                    ob = ob + a * ib[s:s + 1, :]
                o_rows.append(ob)
            o = o + jnp.concatenate(o_rows, axis=0)
            o_ref[:, sl] = o
            og = og_ref[:, sl]
            on = o * lax.rsqrt(_mean(o * o) + RMS_EPS)
            y_ref[:, sl] = (on * gn_ref[:, sl] * (og * jax.nn.sigmoid(og))).astype(BF16)
            return carry

        lax.fori_loop(0, HEADS, head, 0, unroll=HEAD_UNROLL)
        _ride(ex, xrefs, pl.program_id(0), nc)

    blk = lambda b: pl.BlockSpec((None, CHUNK, D_MODEL), functools.partial(lambda n, b: (b, n, 0), b=b))
    row = pl.BlockSpec((CHUNK, D_MODEL), lambda n: (n, 0))
    whole = lambda s: pl.BlockSpec(s, lambda n: (0,) * len(s))
    res = pl.pallas_call(
        body,
        grid=(nc,),
        in_specs=[blk(SLOT_Q), blk(SLOT_F), blk(SLOT_I), blk(SLOT_OG), whole(logits.shape), whole(g_norm.shape)]
        + (ex.in_specs if ex else []),
        out_specs=[row, row, pl.BlockSpec((None, HEADS, HEAD, HEAD), lambda n: (n, 0, 0, 0))] + (ex.out_specs if ex else []),
        out_shape=[jax.ShapeDtypeStruct((t, D_MODEL), BF16), jax.ShapeDtypeStruct((t, D_MODEL), F32),
                   jax.ShapeDtypeStruct((nc, HEADS, HEAD, HEAD), F32)] + (ex.out_shape if ex else []),
        scratch_shapes=[pltpu.VMEM((HEADS, HEAD, HEAD), F32)] + [pltpu.VMEM((CHUNK, D_MODEL), F32)] * 3
        + (ex.scratch if ex else []),
        compiler_params=_params(("arbitrary",)),
        name="hgrn_fwd",
    )(h, h, h, h, logits, g_norm, *(ex.arrays if ex else []))
    return res[0], res[1], res[2], res[3:]


def _hgrn_bwd(h, o_all, dy, states, dh, logits, g_norm, ex=None):
    t = h.shape[1]
    nc = t // CHUNK

    def body(*refs):
        ins, outs, scratch, xrefs = _split_refs(refs, 10, 3, 8, ex)
        q_ref, f_ref, i_ref, og_ref, o_ref, dy_ref, sall_ref, _, lg_ref, gn_ref = ins
        dqfio_ref, dlg_ref, dgn_ref = outs
        dst_ref, dlb_ref, q_s, k_s, c_s, dq_s, dk_s, dc_s = scratch
        dq_ref, df_ref, di_ref, dog_ref = (dqfio_ref.at[s] for s in (SLOT_Q, SLOT_F, SLOT_I, SLOT_OG))
        n = pl.program_id(0)

        @pl.when(n == 0)
        def _():
            dst_ref[...] = jnp.zeros_like(dst_ref)
            dlb_ref[...] = jnp.zeros_like(dlb_ref)
            dgn_ref[...] = jnp.zeros_like(dgn_ref)
            if ex:
                ex.start(*xrefs)

        lb = _lower_bound(lg_ref[...])
        q_raw = q_ref[...]
        q, sig, f, k, c = _hgrn_gates(q_raw, f_ref[...], lb)
        q_s[...] = q
        k_s[...] = k
        c_s[...] = c
        col64 = lax.broadcasted_iota(jnp.int32, (SUB, CHUNK), 1)
        trow = lax.broadcasted_iota(jnp.int32, (SUB, HEAD), 0)
        row64 = lax.broadcasted_iota(jnp.int32, (CHUNK, HEAD), 0)

        def head(hd, carry):
            sl = pl.ds(pl.multiple_of(hd * HEAD, HEAD), HEAD)
            qh, kh, ch, ih = q_s[:, sl], k_s[:, sl], c_s[:, sl], i_ref[:, sl]
            st = sall_ref[hd]
            dst = dst_ref[hd]
            oh, dyh, og, gn = o_ref[:, sl], dy_ref[:, sl], og_ref[:, sl], gn_ref[:, sl]
            sg = jax.nn.sigmoid(og)
            sil = og * sg
            rms = lax.rsqrt(_mean(oh * oh) + RMS_EPS)
            on = oh * rms
            dog_ref[:, sl] = (dyh * on * gn * _silu_grad(og, sg)).astype(BF16)
            dgn_ref[:, sl] += _sum0(dyh * on * sil)
            don = dyh * gn * sil
            do = rms * (don - on * _mean(don * on))
            dob = do.astype(BF16)

            c_last = ch[CHUNK - 1:CHUNK, :]
            eq = jnp.exp(ch)
            q_til = qh * eq
            ekl = jnp.exp(c_last - ch)
            k_til = kh * ekl
            ecl = jnp.exp(c_last)
            dq_til = _dot(dob, st)
            dk_til = _dot(ih, dst)
            di = _dot_nt(k_til, dst)
            dc_last = _sum0(dk_til * k_til) + _sum0(dst * st) * ecl
            dst_ref[hd] = _dot_tn(dob, q_til) + dst * ecl
            dq = dq_til * eq
            dc = dq_til * q_til - dk_til * k_til
            dk = dk_til * ekl

            da_full = _dot_nt(dob, ih)
            a_rows = [jnp.zeros((SUB, CHUNK), F32)]
            dq_rows = [jnp.zeros((SUB, HEAD), F32)]
            dc_rows = [jnp.zeros((SUB, HEAD), F32)]
            for tb in range(1, CHUNK // SUB):
                rows, eqh, q_hat, ekh, k_hat = _offdiag_terms(qh, kh, ch, tb)
                keep = col64 < tb * SUB
                a_rows.append(jnp.where(keep, _dot_nt(q_hat, k_hat), 0.0))
                da = jnp.where(keep, da_full[rows], 0.0)
                dq_hat = _dot(da, k_hat)
                dk_hat = _dot_tn(da, q_hat)
                dq_rows.append(dq_hat * eqh)
                dc_rows.append(dq_hat * q_hat)
                dk = dk + dk_hat * ekh
                dc = dc - dk_hat * k_hat
            di = di + _dot_tn(jnp.concatenate(a_rows, axis=0), dob)

            dk_rows, di_rows = [], []
            for b in range(CHUNK // SUB):
                rows = slice(b * SUB, (b + 1) * SUB)
                qb, cb, kb, ib, dob_ = qh[rows], ch[rows], kh[rows], ih[rows], do[rows]
                dq_diag = jnp.zeros((SUB, HEAD), F32)
                dk_diag = jnp.zeros((SUB, HEAD), F32)
                di_diag = jnp.zeros((SUB, HEAD), F32)
                for s in range(SUB):
                    ks = kb[s:s + 1, :]
                    dec = _diag_decay(cb, s, trow)
                    a = jnp.sum(qb * dec * ks, axis=1, keepdims=True)
                    gk = jnp.sum(dob_ * ib[s:s + 1, :], axis=1, keepdims=True) * dec
                    dq_diag = dq_diag + gk * ks
                    dk_diag = dk_diag + jnp.where(trow == s, _sum0(gk * qb), 0.0)
                    di_diag = di_diag + jnp.where(trow == s, _sum0(a * dob_), 0.0)
                dq_rows[b] = dq_rows[b] + dq_diag
                dc_rows[b] = dc_rows[b] + qb * dq_diag - kb * dk_diag
                dk_rows.append(dk_diag)
                di_rows.append(di_diag)
            dq = dq + jnp.concatenate(dq_rows, axis=0)
            dk = dk + jnp.concatenate(dk_rows, axis=0)
            dc = dc + jnp.concatenate(dc_rows, axis=0) + jnp.where(row64 == CHUNK - 1, dc_last, 0.0)
            di_ref[:, sl] = (di + jnp.concatenate(di_rows, axis=0)).astype(BF16)
            dq_s[:, sl] = dq
            dk_s[:, sl] = dk
            dc_s[:, sl] = dc
            return carry

        lax.fori_loop(0, HEADS, head, 0, unroll=HEAD_UNROLL)

        row = lax.broadcasted_iota(jnp.int32, (CHUNK, CHUNK), 0)
        col = lax.broadcasted_iota(jnp.int32, (CHUNK, CHUNK), 1)
        dlf = _tri_matmul((row <= col).astype(BF16), dc_s[...])
        df = dlf / f - dk_s[...]
        dlb_ref[...] += _sum0(df * (1.0 - sig))
        df_ref[...] = (df * (1.0 - lb) * sig * (1.0 - sig)).astype(BF16)
        dq_ref[...] = (dq_s[...] * _silu_grad(q_raw, jax.nn.sigmoid(q_raw))).astype(BF16)

        @pl.when(n == nc - 1)
        def _():
            d0 = dlb_ref[...] * lb * (1.0 - lb)
            dlg_ref[0:1, :] = d0
            dlg_ref[1:2, :] = -d0
            if ex:
                ex.finish(*xrefs)

    rev = lambda n: nc - 1 - n
    blk = lambda b: pl.BlockSpec((None, CHUNK, D_MODEL), functools.partial(lambda n, b: (b, rev(n), 0), b=b))
    row = pl.BlockSpec((CHUNK, D_MODEL), lambda n: (rev(n), 0))
    whole = lambda s: pl.BlockSpec(s, lambda n: (0,) * len(s))
    vec = (1, D_MODEL)
    res = pl.pallas_call(
        body,
        grid=(nc,),
        in_specs=[blk(SLOT_Q), blk(SLOT_F), blk(SLOT_I), blk(SLOT_OG), row, row,
                  pl.BlockSpec((None, HEADS, HEAD, HEAD), lambda n: (rev(n), 0, 0, 0)),
                  pl.BlockSpec(memory_space=pl.ANY), whole(logits.shape), whole(vec)] + (ex.in_specs if ex else []),
        out_specs=[pl.BlockSpec((4, CHUNK, D_MODEL), lambda n: (SLOT_Q // 4, rev(n), 0)), whole((2, D_MODEL)), whole(vec)]
        + (ex.out_specs if ex else []),
        out_shape=[jax.ShapeDtypeStruct(dh.shape, BF16), jax.ShapeDtypeStruct((2, D_MODEL), F32),
                   jax.ShapeDtypeStruct(vec, F32)] + (ex.out_shape if ex else []),
        scratch_shapes=[pltpu.VMEM((HEADS, HEAD, HEAD), F32), pltpu.VMEM(vec, F32)]
        + [pltpu.VMEM((CHUNK, D_MODEL), F32)] * 6 + (ex.scratch if ex else []),
        input_output_aliases={7: 0},
        compiler_params=_params(("arbitrary",)),
        name="hgrn_bwd",
    )(h, h, h, h, o_all, dy, states, dh, logits, g_norm, *(ex.arrays if ex else []))
    return res[0], res[1], res[2], res[3:]


def _merge_fwd(i, n, ga, gb, za, zb):
    return jax.nn.sigmoid(ga) * za + jax.nn.sigmoid(gb) * zb


def _merge_bwd(i, n, ga, gb, za, zb, dm):
    sa, sb = jax.nn.sigmoid(ga), jax.nn.sigmoid(gb)
    dgates = jnp.stack([(dm * za * sa * (1.0 - sa)).astype(BF16), (dm * zb * sb * (1.0 - sb)).astype(BF16)])
    return dgates, dm * sa, dm * sb


def _ln1_fwd(i, n, x, r1, g, b):
    xhat, _ = _ln_stats(ALPHA * x + r1)
    x1 = xhat * g + b
    return x1, x1


def _ln1_bwd(i, n, x, r1, dx1, g):
    xhat, rstd = _ln_stats(ALPHA * x + r1)
    dz = _ln_bwd(dx1, xhat, rstd, g)
    return dz, dz, _sum0(dx1 * xhat), _sum0(dx1)


def _ln2_loss(i, n, x1, fo, pg, pp, tgt, g, b):
    sg = jax.nn.sigmoid(pg)
    xhat, rstd = _ln_stats(ALPHA * x1 + fo + sg * pp)
    diff = xhat * g + b - tgt
    loss = 0.5 * jnp.sum(_mean(diff * diff), axis=0, keepdims=True)
    dy = diff * (1.0 / D_MODEL)
    dz = _ln_bwd(dy, xhat, rstd, g)
    return (dz, dz, dz * pp * sg * (1.0 - sg), dz * sg,
            jnp.broadcast_to(loss, (8, LANE)), _sum0(dy * xhat), _sum0(dy))


def _shift_down(cur, halo, tile):
    row = lax.broadcasted_iota(jnp.int32, cur.shape, 0)
    m1 = jnp.where(row == 0, halo[7:8, :], pltpu.roll(cur, 1, 0))
    m2 = jnp.where(row == 0, halo[6:7, :], jnp.where(row == 1, halo[7:8, :], pltpu.roll(cur, 2, 0)))
    return m1, m2


def _shift_up(cur, halo, tile):
    row = lax.broadcasted_iota(jnp.int32, cur.shape, 0)
    p1 = jnp.where(row == tile - 1, halo[0:1, :], pltpu.roll(cur, tile - 1, 0))
    p2 = jnp.where(row == tile - 2, halo[0:1, :], jnp.where(row == tile - 1, halo[1:2, :], pltpu.roll(cur, tile - 2, 0)))
    return p1, p2


def _conv_pre(i, gate, halo, w, b, tile):
    halo = jnp.where(i == 0, 0.0, halo)
    m1, m2 = _shift_down(gate, halo, tile)
    return w[0:1, :] * m2 + w[1:2, :] * m1 + w[2:3, :] * gate + b, m1, m2


def _conv_fwd(tile, i, n, gate, halo, val, w, b):
    cg, _, _ = _conv_pre(i, gate, halo, w, b, tile)
    return _gelu(cg) * val


def _conv_bwd_a(tile, i, n, gate, halo, val, dhid, w, b):
    cg, m1, m2 = _conv_pre(i, gate, halo, w, b, tile)
    act, slope = _gelu_and_grad(cg)
    dcg = dhid * val * slope
    return dcg, dhid * act, _sum0(dcg * m2), _sum0(dcg * m1), _sum0(dcg * gate), _sum0(dcg)


def _conv_bwd_b(tile, i, n, dcg, halo, w):
    dcg = dcg.astype(F32)
    halo = jnp.where(i == n - 1, 0.0, halo.astype(F32))
    p1, p2 = _shift_up(dcg, halo, tile)
    return w[2:3, :] * dcg + w[1:2, :] * p1 + w[0:1, :] * p2


def _halo_spec(width, tile, t, nxt, rows=8):
    per = tile // rows
    last = t // rows - 1
    if nxt:
        return pl.BlockSpec((rows, width), lambda i: (jnp.minimum((i + 1) * per, last), 0))
    return pl.BlockSpec((rows, width), lambda i: (jnp.maximum(i * per - 1, 0), 0))


def _adamw(i, n, w, m, v, parts):
    g = parts[0].astype(F32)
    for j in range(1, parts.shape[0]):
        g = g + parts[j].astype(F32)
    m_new = ADAM_B1 * m + (1.0 - ADAM_B1) * g
    v_new = ADAM_B2 * v + (1.0 - ADAM_B2) * (g * g)
    m_hat = m_new / (1.0 - ADAM_B1 ** ADAM_STEP)
    v_hat = v_new / (1.0 - ADAM_B2 ** ADAM_STEP)
    delta = -ADAM_LR * (m_hat / (jnp.sqrt(v_hat) + ADAM_EPS) + ADAM_WD * w)
    return g, delta, m_new, v_new


def _adam_call(w, m, v, parts, name):
    r, c = w.shape
    tile = _pick(r, (256, 128)) if r > 256 else r
    spec = pl.BlockSpec((parts.shape[0], tile, c), lambda i: (0, i, 0))
    return _rowwise(_adamw, [w, m, v, (parts, spec)], [], [(c, F32)] * 4, [], tile=tile, name=name)


def _peer(k):
    x, y, c = lax.axis_index("x"), lax.axis_index("y"), lax.axis_index("c")
    px = x ^ ((k >> 2) & 1)
    py = y ^ ((k >> 1) & 1)
    pc = c ^ (k & 1)
    return (px, py, pc), 4 * px + 2 * py + pc


def _my_index():
    return 4 * lax.axis_index("x") + 2 * lax.axis_index("y") + lax.axis_index("c")


class _Exchange:
    KINDS = ("gather", "gather+relay", "scatter", "scatter+pairs")

    def __init__(self, entries):
        assert all(k in self.KINDS for _, k in entries), [k for _, k in entries]
        self.arrays = [a for a, _ in entries]
        self.scatter = [k.startswith("scatter") for _, k in entries]
        self.relayed = ["+relay" in k for _, k in entries]
        self.pairs = ["+pairs" in k for _, k in entries]
        self.n = len(entries)
        self.in_specs = [pl.BlockSpec(memory_space=pl.ANY)] * self.n
        self.out_specs = [pl.BlockSpec(memory_space=pl.ANY)] * self.n
        shapes = [tuple(a.shape[1:]) if sc else tuple(a.shape) for a, sc in zip(self.arrays, self.scatter)]
        counts = [N_CHIP if p else N_DEV for p in self.pairs]
        self.out_shape = [jax.ShapeDtypeStruct((n,) + s, a.dtype) for n, s, a in zip(counts, shapes, self.arrays)]
        per = N_DEV - 1
        self.scratch = [pltpu.SemaphoreType.DMA((self.n * per,)), pltpu.SemaphoreType.DMA((self.n * per,)),
                        pltpu.SemaphoreType.DMA((self.n,))]

    def _copies(self, srcs, outs, sems):
        send_sems, recv_sems, local_sems = sems
        x, y, c = lax.axis_index("x"), lax.axis_index("y"), lax.axis_index("c")
        me = _my_index()
        per = N_DEV - 1
        local, first, passed, relay_arrivals, arrivals = [], [], [], [], []
        for a in range(self.n):

            def copy(k, src, dst, dev, a=a):
                return pltpu.make_async_remote_copy(
                    src_ref=src, dst_ref=dst, send_sem=send_sems.at[a * per + k], recv_sem=recv_sems.at[a * per + k],
                    device_id=dev, device_id_type=pl.DeviceIdType.MESH)

            if self.pairs[a]:
                chip = 2 * x + y
                for k in range(N_CHIP):
                    to = chip ^ k
                    piece = srcs[a].at[_slot_of_group(2 * to + c) // 2]
                    if k == 0:
                        local.append(pltpu.make_async_copy(piece, outs[a].at[chip], local_sems.at[a]))
                    else:
                        dev = (to // 2, to % 2, c)
                        first.append(copy(k - 1, piece, outs[a].at[chip], dev))
                        arrivals.append(copy(k - 1, piece, outs[a].at[to], dev))
                continue
            mine = srcs[a].at[me] if self.scatter[a] else srcs[a]
            land = outs[a].at[me]
            local.append(pltpu.make_async_copy(mine, land, local_sems.at[a]))
            if self.relayed[a]:
                block = lambda px, py, pc, a=a: outs[a].at[4 * px + 2 * py + pc]
                chips = [(1 - x, y), (x, 1 - y), (1 - x, 1 - y)]
                first.append(copy(0, mine, land, (x, y, 1 - c)))
                arrivals.append(copy(0, mine, block(x, y, 1 - c), (x, y, 1 - c)))
                for j, (px, py) in enumerate(chips):
                    first.append(copy(1 + j, mine, land, (px, py, c)))
                    relay_arrivals.append(copy(1 + j, mine, block(px, py, c), (px, py, c)))
                    passed.append(copy(4 + j, block(px, py, c), block(px, py, c), (x, y, 1 - c)))
                    arrivals.append(copy(4 + j, mine, block(px, py, 1 - c), (x, y, 1 - c)))
                continue
            for k in range(1, N_DEV):
                dev, idx = _peer(k)
                if self.scatter[a]:
                    first.append(copy(k - 1, srcs[a].at[idx], land, dev))
                else:
                    first.append(copy(k - 1, mine, land, dev))
                arrivals.append(copy(k - 1, mine, outs[a].at[idx], dev))
        return local, first, passed, relay_arrivals, arrivals

    def start(self, srcs, outs, sems):
        local, first, _, _, _ = self._copies(srcs, outs, sems)
        for cp in local + first:
            cp.start()

    def relay(self, srcs, outs, sems, two_hop):
        _, _, passed, relay_arrivals, _ = self._copies(srcs, outs, sems)
        for n, (landed, onward) in enumerate(zip(relay_arrivals, passed)):
            if (n % 3 == 2) == two_hop:
                landed.wait_recv()
                onward.start()

    def finish(self, srcs, outs, sems):
        local, first, passed, _, arrivals = self._copies(srcs, outs, sems)
        for cp in arrivals:
            cp.wait_recv()
        for cp in first + passed:
            cp.wait_send()
        for cp in local:
            cp.wait()


def _gather_project(x_b, shard):
    t, d = x_b.shape
    tm = _pick(t, MM_TILES)
    nrow = t // tm
    per = N_DEV - 1

    def parties():
        x, y, c = lax.axis_index("x"), lax.axis_index("y"), lax.axis_index("c")
        chips = [(1 - x, y), (x, 1 - y), (1 - x, 1 - y)]
        return (x, y, c), (x, y, 1 - c), [(px, py, c) for px, py in chips], [(px, py, 1 - c) for px, py in chips]

    slot = lambda dev: _slot_of_group(4 * dev[0] + 2 * dev[1] + dev[2])
    ici_step, passed_step = (2, 3, 6), (4, 5, 7)
    me, sibling, over_ici, passed_on = parties()
    by_step = {0: me, 1: sibling, **dict(zip(ici_step, over_ici)), **dict(zip(passed_step, passed_on))}
    order = jnp.stack([slot(by_step[j]) for j in range(N_DEV)]).astype(jnp.int32)

    def body(order_ref, x_ref, shard_ref, h_ref, wall_ref, wbuf, fetch_sem, send_sems, recv_sems, local_sem):
        del order_ref
        me, sibling, over_ici, passed_on = parties()
        j, i = pl.program_id(0), pl.program_id(1)
        land = lambda dev: wall_ref.at[slot(dev)]

        def copy(k, src, block, to):
            return pltpu.make_async_remote_copy(src_ref=src, dst_ref=land(block), send_sem=send_sems.at[k],
                                                recv_sem=recv_sems.at[k], device_id=to, device_id_type=pl.DeviceIdType.MESH)

        def fetch(src):
            cp = pltpu.make_async_copy(src, wbuf, fetch_sem)
            cp.start()
            cp.wait()

        keep = pltpu.make_async_copy(shard_ref, land(me), local_sem)
        first = [copy(0, shard_ref, me, sibling)] + [copy(1 + n, shard_ref, me, dev) for n, dev in enumerate(over_ici)]
        onward = [copy(4 + n, land(dev), dev, sibling) for n, dev in enumerate(over_ici)]

        @pl.when(jnp.logical_and(i == 0, j == 0))
        def _():
            keep.start()
            for cp in first:
                cp.start()
            fetch(shard_ref)

        @pl.when(jnp.logical_and(i == 0, j == 1))
        def _():
            copy(0, shard_ref, sibling, me).wait_recv()
            fetch(land(sibling))

        for n, dev in enumerate(over_ici):
            @pl.when(jnp.logical_and(i == 0, j == ici_step[n]))
            def _(n=n, dev=dev):
                copy(1 + n, shard_ref, dev, me).wait_recv()
                onward[n].start()
                fetch(land(dev))

        for n, dev in enumerate(passed_on):
            @pl.when(jnp.logical_and(i == 0, j == passed_step[n]))
            def _(n=n, dev=dev):
                copy(4 + n, shard_ref, dev, me).wait_recv()
                fetch(land(dev))

        h_ref[...] = _dot(x_ref[...], wbuf[...])

        @pl.when(jnp.logical_and(i == nrow - 1, j == N_DEV - 1))
        def _():
            for cp in first + onward:
                cp.wait_send()
            keep.wait()

    need = 2 * (tm * d * 2 + tm * d * 4) + d * d * 2 + tm * d * 4
    h, w_all = pl.pallas_call(
        body,
        grid_spec=pltpu.PrefetchScalarGridSpec(
            num_scalar_prefetch=1,
            grid=(N_DEV, nrow),
            in_specs=[pl.BlockSpec((tm, d), lambda j, i, order: (i, 0)), pl.BlockSpec(memory_space=pl.ANY)],
            out_specs=[pl.BlockSpec((None, tm, d), lambda j, i, order: (order[j], i, 0)), pl.BlockSpec(memory_space=pl.ANY)],
            scratch_shapes=[pltpu.VMEM((d, d), BF16), pltpu.SemaphoreType.DMA, pltpu.SemaphoreType.DMA((per,)),
                            pltpu.SemaphoreType.DMA((per,)), pltpu.SemaphoreType.DMA],
        ),
        out_shape=[jax.ShapeDtypeStruct((N_DEV, t, d), F32), jax.ShapeDtypeStruct((N_DEV, d, d), BF16)],
        compiler_params=_params(("arbitrary", "arbitrary"), need),
        name="gather_project",
    )(order, x_b, shard)
    return h, w_all


def _pair_sums(g):
    n, r, cols = g.shape
    half = n // 2

    def swap(g_ref, got_ref, send_sems, recv_sems):
        x, y, c = lax.axis_index("x"), lax.axis_index("y"), lax.axis_index("c")
        copies = [pltpu.make_async_remote_copy(
            src_ref=g_ref.at[2 * j + 1 - c], dst_ref=got_ref.at[j], send_sem=send_sems.at[j], recv_sem=recv_sems.at[j],
            device_id=(x, y, 1 - c), device_id_type=pl.DeviceIdType.MESH) for j in range(half)]
        for cp in copies:
            cp.start()
        for cp in copies:
            cp.wait()

    got = pl.pallas_call(
        swap,
        in_specs=[pl.BlockSpec(memory_space=pl.ANY)],
        out_specs=pl.BlockSpec(memory_space=pl.ANY),
        out_shape=jax.ShapeDtypeStruct((half, r, cols), g.dtype),
        scratch_shapes=[pltpu.SemaphoreType.DMA((half,))] * 2,
        name="pair_swap",
    )(g)

    def add(mine_ref, got_ref, out_ref):
        mine = jnp.where(lax.axis_index("c") == 0, mine_ref[0].astype(F32), mine_ref[1].astype(F32))
        out_ref[...] = (mine + got_ref[...].astype(F32)).astype(out_ref.dtype)

    tile = _pick(r, (512, 256, 128))
    return pl.pallas_call(
        add,
        grid=(half, r // tile),
        in_specs=[pl.BlockSpec((None, 2, tile, cols), lambda j, i: (j, 0, i, 0)),
                  pl.BlockSpec((None, tile, cols), lambda j, i: (j, i, 0))],
        out_specs=pl.BlockSpec((None, tile, cols), lambda j, i: (j, i, 0)),
        out_shape=jax.ShapeDtypeStruct((half, r, cols), g.dtype),
        compiler_params=_params(("parallel", "parallel")),
        name="pair_add",
    )(g.reshape(half, 2, r, cols), got)


def _local_step(x, p, tgt, small, comm):
    t = x.shape[0]
    tile = _pick(t, (256, 128))
    d = D_MODEL
    act_b, act_f = (d, BF16), (d, F32)
    x_b, p_b = x.astype(BF16), p.astype(BF16)

    chunk_id = jnp.arange(SGU_BLOCK) // CHUNK
    mask = chunk_id[:, None] >= chunk_id[None, :]
    wm = jnp.where(mask[None], small["sgu_w_s"], 0.0)
    wm_b = wm.astype(BF16)
    wm_t = jnp.swapaxes(wm, 1, 2).astype(BF16)
    bs_t = small["sgu_b_s"].T

    h, w_in = comm.project_in(x_b)
    y_a, got_a = _sgu_fwd(h, wm_b, bs_t, small["sgu_norm_g"], small["sgu_norm_b"], ex=comm.weights_exchange(0))
    y_b, o_all, states, got_b = _hgrn_fwd(h, small["lb_logits"], small["hgrn_norm_g"], ex=comm.weights_exchange(1))
    wts, conv_w = comm.weights(got_a, got_b)
    z_a = _mm(y_a, wts["w_a"], out_dtype=F32, name="mm_za")
    z_b = _mm(y_b, wts["w_b"], out_dtype=F32, name="mm_zb")
    gates = [(h, SLOT_GA), (h, SLOT_GB)]
    merged, = _rowwise(_merge_fwd, gates + [z_a, z_b], [], [act_b], [], tile=tile, name="merge_fwd")
    r1 = _mm(merged, wts["w_o"], out_dtype=F32, name="mm_r1")
    x1, x1_b = _rowwise(_ln1_fwd, [x, r1], [small["ln1_g"], small["ln1_b"]], [act_f, act_b], [], tile=tile, name="ln1_fwd")
    gate = _mm(x1_b, wts["w_g"], out_dtype=F32, name="mm_gate")
    val = _mm(x1_b, wts["w_v"], out_dtype=F32, name="mm_val")
    pg = _mm(x1_b, wts["w_pg"], out_dtype=F32, name="mm_pg")
    pp = _mm(p_b, wts["w_pp"], out_dtype=F32, name="mm_pp")
    ex = comm.weights_exchange(2)
    res = _rowwise(functools.partial(_conv_fwd, tile), [gate, (gate, _halo_spec(D_FF, tile, t, False)), val],
                   [conv_w, small["conv_b"]], [(D_FF, BF16)], [], tile=tile, name="conv_fwd", ex=ex)
    (hid,), got_c = res if ex else (res, ())
    wts["w_down"] = comm.w_down(got_c)
    fo = _mm(hid, wts["w_down"], out_dtype=F32, name="mm_down")
    dz2, dz2_b, dpg, dpp, loss, dg2, db2 = _rowwise(
        _ln2_loss, [x1, fo, pg, pp, tgt], [small["ln2_g"], small["ln2_b"]],
        [act_f, act_b, act_b, act_b], [(8, LANE), (1, d), (1, d)], tile=tile, name="ln2_loss")

    dhid = _mm(dz2_b, wts["w_down"], out_dtype=BF16, name="mm_dhid", trans_b=True)
    g_down = _mm_tn(hid, dz2_b, out_dtype=BF16, name="mm_g_down")
    dcg, dval, dcw0, dcw1, dcw2, dcb = _rowwise(
        functools.partial(_conv_bwd_a, tile), [gate, (gate, _halo_spec(D_FF, tile, t, False)), val, dhid],
        [conv_w, small["conv_b"]], [(D_FF, BF16), (D_FF, BF16)], [(1, D_FF)] * 4, tile=tile, name="conv_bwd_a")
    dgate, = _rowwise(functools.partial(_conv_bwd_b, tile), [dcg, (dcg, _halo_spec(D_FF, tile, t, True, rows=16))],
                      [conv_w], [(D_FF, BF16)], [], tile=tile, name="conv_bwd_b")
    g_g = _mm_tn(x1_b, dgate, out_dtype=BF16, name="mm_g_gate")
    g_v = _mm_tn(x1_b, dval, out_dtype=BF16, name="mm_g_val")
    g_pg = _mm_tn(x1_b, dpg, out_dtype=BF16, name="mm_g_pg")
    g_pp = _mm_tn(p_b, dpp, out_dtype=BF16, name="mm_g_pp")
    dx1 = _mm(dgate, wts["w_g"], out_dtype=F32, name="mm_dx1_gate", trans_b=True, adds=[(dz2, ALPHA)])
    dx1 = _mm(dval, wts["w_v"], out_dtype=F32, name="mm_dx1_val", trans_b=True, adds=[(dx1, 1.0)])
    dx1 = _mm(dpg, wts["w_pg"], out_dtype=F32, name="mm_dx1_pg", trans_b=True, adds=[(dx1, 1.0)])
    dz1, dz1_b, dg1, db1 = _rowwise(_ln1_bwd, [x, r1, dx1], [small["ln1_g"]], [act_f, act_b], [(1, d), (1, d)],
                                    tile=tile, name="ln1_bwd")
    g_o = _mm_tn(merged, dz1_b, out_dtype=BF16, name="mm_g_o")
    dm = _mm(dz1_b, wts["w_o"], out_dtype=BF16, name="mm_dm", trans_b=True)
    dh, dza, dzb = _rowwise(_merge_bwd, gates + [z_a, z_b, dm], [],
                            [("stack", 2, SLOT_GA // 2, 8, d, BF16), act_b, act_b], [], tile=tile, name="merge_bwd")
    g_a = _mm_tn(y_a, dza, out_dtype=BF16, name="mm_g_a")
    g_b = _mm_tn(y_b, dzb, out_dtype=BF16, name="mm_g_b")
    dy_a = _mm(dza, wts["w_a"], out_dtype=BF16, name="mm_dya", trans_b=True)
    dy_b = _mm(dzb, wts["w_b"], out_dtype=F32, name="mm_dyb", trans_b=True)
    dh, dws, dbs, dgv_n, dbv_n = _sgu_bwd(h, dy_a, dh, wm_b, wm_t, bs_t, small["sgu_norm_g"], small["sgu_norm_b"])
    big = dict(w_a=g_a, w_b=g_b, w_o=g_o, w_g=g_g, w_v=g_v, w_down=g_down, w_pp=g_pp, w_pg=g_pg)
    sm = dict(sgu_w_s=jnp.where(mask[None], dws, 0.0), sgu_b_s=dbs[:, :GROUPS].T, sgu_norm_g=dgv_n, sgu_norm_b=dbv_n,
              ln1_g=dg1, ln1_b=db1, conv_w=jnp.concatenate([dcw0, dcw1, dcw2], axis=0), conv_b=dcb, ln2_g=dg2, ln2_b=db2,
              loss=loss)
    dh, dlogits, dgn, got = _hgrn_bwd(h, o_all, dy_b, states, dh, small["lb_logits"], small["hgrn_norm_g"],
                                      ex=comm.grads_exchange(big, sm))
    comm.grads_done(got)
    g_in = _mm_tn(x_b, dh, out_dtype=BF16, name="mm_g_in")
    ex = comm.last_exchange(g_in, dict(lb_logits=dlogits, hgrn_norm_g=dgn))
    res = _mm(dh, w_in, out_dtype=F32, name="mm_dx", trans_b=True, reduce_b=True, adds=[(dz1, ALPHA)], ex=ex)
    grad_x, got = res if ex else (res, ())
    comm.last_done(got)
    return grad_x


_SMALL_EARLY = ["sgu_w_s", "sgu_b_s", "sgu_norm_g", "sgu_norm_b", "ln1_g", "ln1_b", "ffn_conv_b", "ln2_g", "ln2_b"]
_SMALL_LATE = ["hgrn_lb_logits", "hgrn_norm_g"]
N_TAPS = D_FF // N_DEV
UP_COLS = 2 * D_FF // N_DEV


class _StepExchanges:
    def __init__(self, w_in_shard, shards):
        self.w_in_shard = w_in_shard
        self.shards = shards

    def project_in(self, x_b):
        return _gather_project(x_b, self.w_in_shard)

    PARTS = ((0, 1), (2, 4, 5, 6), (3,))

    def weights_exchange(self, part):
        return _Exchange([(self.shards[n], "gather+relay") for n in self.PARTS[part]])

    def weights(self, got_first, got_second):
        d, f = D_MODEL, D_FF
        w_br_g, w_o_g = got_first
        w_up_g, w_pp_g, w_pg_g, conv_g = got_second
        w_br = w_br_g.transpose(1, 0, 2, 3).reshape(2, d, d)
        w_up = w_up_g.transpose(1, 0, 2).reshape(d, 2, f).transpose(1, 0, 2)
        wts = dict(w_a=w_br[0], w_b=w_br[1], w_o=w_o_g.reshape(d, d), w_g=w_up[0], w_v=w_up[1],
                   w_pp=w_pp_g.transpose(1, 0, 2).reshape(256, d), w_pg=w_pg_g.reshape(d, d))
        return wts, conv_g.transpose(1, 0, 2).reshape(3, f)

    def w_down(self, got_third):
        return got_third[0].reshape(D_FF, D_MODEL)

    def grads_exchange(self, big, sm):
        d = D_MODEL
        parts = [jnp.stack([big["w_a"], big["w_b"]]).reshape(2, N_DEV, 128, d).transpose(1, 0, 2, 3),
                 big["w_o"].reshape(N_DEV, 128, d),
                 jnp.concatenate([big["w_g"], big["w_v"]], axis=1).reshape(d, N_DEV, UP_COLS).transpose(1, 0, 2),
                 big["w_down"].reshape(N_DEV, N_TAPS, d),
                 big["w_pp"].reshape(256, N_DEV, 128).transpose(1, 0, 2),
                 big["w_pg"].reshape(N_DEV, 128, d)]
        packed, self.rows_early = _pack([sm[k] for k in ("sgu_w_s", "sgu_b_s", "sgu_norm_g", "sgu_norm_b", "ln1_g", "ln1_b",
                                                         "conv_b", "ln2_g", "ln2_b", "conv_w", "loss")])
        return _Exchange([(a, "scatter") for a in parts] + [(packed, "gather")])

    def grads_done(self, got):
        self.recv, self.small_early = got[:6], got[6]

    def last_exchange(self, g_in, sm):
        packed, self.rows_late = _pack([sm["lb_logits"], sm["hgrn_norm_g"]])
        return _Exchange([(_pair_sums(g_in), "scatter+pairs"), (packed, "gather")])

    def last_done(self, got):
        self.recv_in, self.small_late = got


def _rows128(a):
    flat = a.reshape(-1)
    rows = -(-flat.shape[0] // (8 * LANE)) * 8
    return jnp.pad(flat, (0, rows * LANE - flat.shape[0])).reshape(rows, LANE)


def _pack(parts):
    blocks = [_rows128(a) for a in parts]
    return jnp.concatenate(blocks, axis=0), [b.shape[0] for b in blocks]


def _unpack(packed, shapes, rows):
    out, r0 = [], 0
    for shp, r in zip(shapes, rows):
        n = math.prod(shp)
        out.append(packed[r0:r0 + r].reshape(-1)[:n].reshape(shp))
        r0 += r
    return out


def kernel(x, p, w_in, sgu_w_s, sgu_b_s, sgu_norm_g, sgu_norm_b, hgrn_lb_logits, hgrn_norm_g, w_branch, w_out, ln1_g, ln1_b, ffn_w_up, ffn_conv_w, ffn_conv_b, ffn_w_down, ln2_g, ln2_b, ple_w_proj, ple_w_gate, loss_target, m_w_in, m_sgu_w_s, m_sgu_b_s, m_sgu_norm_g, m_sgu_norm_b, m_hgrn_lb_logits, m_hgrn_norm_g, m_w_branch, m_w_out, m_ln1_g, m_ln1_b, m_ffn_w_up, m_ffn_conv_w, m_ffn_conv_b, m_ffn_w_down, m_ln2_g, m_ln2_b, m_ple_w_proj, m_ple_w_gate, v_w_in, v_sgu_w_s, v_sgu_b_s, v_sgu_norm_g, v_sgu_norm_b, v_hgrn_lb_logits, v_hgrn_norm_g, v_w_branch, v_w_out, v_ln1_g, v_ln1_b, v_ffn_w_up, v_ffn_conv_w, v_ffn_conv_b, v_ffn_w_down, v_ln2_g, v_ln2_b, v_ple_w_proj, v_ple_w_gate):
    weights = dict(w_in=w_in, sgu_w_s=sgu_w_s, sgu_b_s=sgu_b_s, sgu_norm_g=sgu_norm_g, sgu_norm_b=sgu_norm_b,
                   hgrn_lb_logits=hgrn_lb_logits, hgrn_norm_g=hgrn_norm_g, w_branch=w_branch, w_out=w_out,
                   ln1_g=ln1_g, ln1_b=ln1_b, ffn_w_up=ffn_w_up, ffn_conv_w=ffn_conv_w, ffn_conv_b=ffn_conv_b,
                   ffn_w_down=ffn_w_down, ln2_g=ln2_g, ln2_b=ln2_b, ple_w_proj=ple_w_proj, ple_w_gate=ple_w_gate)
    mom_m = dict(w_in=m_w_in, sgu_w_s=m_sgu_w_s, sgu_b_s=m_sgu_b_s, sgu_norm_g=m_sgu_norm_g, sgu_norm_b=m_sgu_norm_b,
                 hgrn_lb_logits=m_hgrn_lb_logits, hgrn_norm_g=m_hgrn_norm_g, w_branch=m_w_branch, w_out=m_w_out,
                 ln1_g=m_ln1_g, ln1_b=m_ln1_b, ffn_w_up=m_ffn_w_up, ffn_conv_w=m_ffn_conv_w, ffn_conv_b=m_ffn_conv_b,
                 ffn_w_down=m_ffn_w_down, ln2_g=m_ln2_g, ln2_b=m_ln2_b, ple_w_proj=m_ple_w_proj, ple_w_gate=m_ple_w_gate)
    mom_v = dict(w_in=v_w_in, sgu_w_s=v_sgu_w_s, sgu_b_s=v_sgu_b_s, sgu_norm_g=v_sgu_norm_g, sgu_norm_b=v_sgu_norm_b,
                 hgrn_lb_logits=v_hgrn_lb_logits, hgrn_norm_g=v_hgrn_norm_g, w_branch=v_w_branch, w_out=v_w_out,
                 ln1_g=v_ln1_g, ln1_b=v_ln1_b, ffn_w_up=v_ffn_w_up, ffn_conv_w=v_ffn_conv_w, ffn_conv_b=v_ffn_conv_b,
                 ffn_w_down=v_ffn_w_down, ln2_g=v_ln2_g, ln2_b=v_ln2_b, ple_w_proj=v_ple_w_proj, ple_w_gate=v_ple_w_gate)
    d, f = D_MODEL, D_FF
    me = _my_index()

    comm = _StepExchanges(w_in[0].astype(BF16),
                          [w_branch[0].astype(BF16), w_out[0].astype(BF16), ffn_w_up[0].astype(BF16),
                           ffn_w_down[0].astype(BF16), ple_w_proj[0].astype(BF16), ple_w_gate[0].astype(BF16), ffn_conv_w[0]])
    small = dict(sgu_w_s=sgu_w_s[0], sgu_b_s=sgu_b_s[0], sgu_norm_g=sgu_norm_g, sgu_norm_b=sgu_norm_b,
                 lb_logits=hgrn_lb_logits, hgrn_norm_g=hgrn_norm_g, ln1_g=ln1_g, ln1_b=ln1_b, ln2_g=ln2_g, ln2_b=ln2_b,
                 conv_b=ffn_conv_b)
    grad_x = _local_step(x[0], p[0, 0], loss_target[0], small, comm)

    out = {}

    def adam(name, parts8, shape2d):
        w2, m2, v2 = (a.reshape(shape2d) for a in (weights[name], mom_m[name], mom_v[name]))
        res = _adam_call(w2, m2, v2, parts8.reshape(parts8.shape[:1] + shape2d), "adam_" + name)
        out[name] = tuple(r.reshape(weights[name].shape) for r in res)

    adam("w_in", comm.recv_in, (d, d))
    adam("w_branch", comm.recv[0], (256, d))
    adam("w_out", comm.recv[1], (128, d))
    adam("ffn_w_up", comm.recv[2], (d, UP_COLS))
    adam("ffn_w_down", comm.recv[3], (N_TAPS, d))
    adam("ple_w_proj", comm.recv[4], (256, 128))
    adam("ple_w_gate", comm.recv[5], (128, d))

    def adam_small(names, extra_w, extra_m, extra_v, extra_shapes, parts8, rows, label):
        pk = lambda src, extra: _pack([src[n] for n in names] + extra)[0]
        res = _adam_call(pk(weights, extra_w), pk(mom_m, extra_m), pk(mom_v, extra_v), parts8, label)
        shapes = [weights[n].shape for n in names] + extra_shapes
        unpacked = [_unpack(r, shapes, rows) for r in res]
        for j, n in enumerate(names):
            out[n] = tuple(u[j] for u in unpacked)
        return [[u[len(names) + j] for u in unpacked] for j in range(len(extra_shapes))]

    blank = jnp.zeros((8, LANE), F32)
    taps, loss_rows = adam_small(
        _SMALL_EARLY, [_place_taps(ffn_conv_w[0], me, f), blank], [_place_taps(m_ffn_conv_w[0], me, f), blank],
        [_place_taps(v_ffn_conv_w[0], me, f), blank + 1.0], [(3, f), (8, LANE)], comm.small_early, comm.rows_early,
        "adam_small_early")
    adam_small(_SMALL_LATE, [], [], [], [], comm.small_late, comm.rows_late, "adam_small_late")
    out["ffn_conv_w"] = tuple(lax.dynamic_slice_in_dim(u, me * N_TAPS, N_TAPS, axis=1)[None] for u in taps)
    loss = loss_rows[0][0, 0]

    order = ["w_in", "sgu_w_s", "sgu_b_s", "sgu_norm_g", "sgu_norm_b", "hgrn_lb_logits", "hgrn_norm_g", "w_branch", "w_out",
             "ln1_g", "ln1_b", "ffn_w_up", "ffn_conv_w", "ffn_conv_b", "ffn_w_down", "ln2_g", "ln2_b", "ple_w_proj", "ple_w_gate"]
    return (loss, grad_x[None], *[out[n][0] for n in order], *[out[n][1] for n in order],
            *[out[n][2] for n in order], *[out[n][3] for n in order])


def _place_taps(shard, me, f):
    return lax.dynamic_update_slice_in_dim(jnp.zeros((3, f), F32), shard, me * N_TAPS, axis=1)
```

```python
import functools
import math

import jax
import jax.numpy as jnp
from jax import lax
from jax.experimental import pallas as pl
from jax.experimental.pallas import tpu as pltpu

F32 = jnp.float32
BF16 = jnp.bfloat16

N_DEV = 8
N_CHIP = 4
D_MODEL = 1024
CHUNK = 64
SUB = 16
SGU_BLOCK = 128
GROUPS = 8
HEAD = 128
HEADS = 8
HEAD_UNROLL = 4
D_FF = 2816
LN_EPS = 1e-5
RMS_EPS = 1e-6
ALPHA = 2.0 ** 0.25
GELU_K = math.sqrt(2.0 / math.pi)
GELU_C = 0.044715
NEG = -1e30
ADAM_LR, ADAM_B1, ADAM_B2, ADAM_EPS, ADAM_WD, ADAM_STEP = 0.001, 0.9, 0.999, 1e-08, 0.01, 10
LANE = 128
SLOT_Q, SLOT_F, SLOT_I, SLOT_OG, SLOT_U, SLOT_V, SLOT_GA, SLOT_GB = range(8)


def _slot_of_group(k):
    return jnp.where(k < 2, k + 4, jnp.where(k < 6, k - 2, k))


MIB = 1024 * 1024
VMEM_V7X = 64 * MIB
VMEM_FLOOR = 32 * MIB
MM_TILES = (1024, 1408, 512, 256, 128)


def _params(sem, need=0):
    limit = min(max(need + need // 4, VMEM_FLOOR), VMEM_V7X - 4 * MIB)
    return pltpu.CompilerParams(dimension_semantics=sem, vmem_limit_bytes=limit)


def _pick(n, prefs):
    for t in prefs:
        if n % t == 0:
            return t
    return n


def _gelu(x):
    return 0.5 * x * (1.0 + jnp.tanh(GELU_K * (x + GELU_C * x * x * x)))


def _gelu_and_grad(x):
    x2 = x * x
    t = jnp.tanh(GELU_K * x * (1.0 + GELU_C * x2))
    half = 0.5 * (1.0 + t)
    return x * half, half + 0.5 * x * (1.0 - t * t) * GELU_K * (1.0 + 3.0 * GELU_C * x2)


def _silu_grad(x, s):
    return s * (1.0 + x * (1.0 - s))


def _dot(a, b):
    return jnp.dot(a.astype(BF16), b.astype(BF16), preferred_element_type=F32)


def _dot_nt(a, b):
    return lax.dot_general(a.astype(BF16), b.astype(BF16), (((1,), (1,)), ((), ())), preferred_element_type=F32)


def _dot_tn(a, b):
    return lax.dot_general(a.astype(BF16), b.astype(BF16), (((0,), (0,)), ((), ())), preferred_element_type=F32)


def _mean(x):
    return jnp.mean(x, axis=-1, keepdims=True)


def _sum0(x):
    return jnp.sum(x, axis=0, keepdims=True)


def _mm(a, b, *, out_dtype, name, trans_b=False, reduce_b=False, adds=(), ex=None):
    squeeze = b.ndim == 2
    a3 = a if a.ndim == 3 else a[None]
    b3 = b if b.ndim == 3 else b[None]
    ba, m, k = a3.shape
    bb = b3.shape[0]
    n = b3.shape[1] if trans_b else b3.shape[2]
    tm = _pick(m, MM_TILES)
    tn = _pick(n, MM_TILES)
    if reduce_b:
        bo, steps = 1, bb
        a_map = lambda o, i, j, r: (r if ba > 1 else 0, i, 0)
        b_map = (lambda o, i, j, r: (r, j, 0)) if trans_b else (lambda o, i, j, r: (r, 0, j))
    else:
        bo, steps = bb, 1
        a_map = lambda o, i, j, r: (o if ba > 1 else 0, i, 0)
        b_map = (lambda o, i, j, r: (o, j, 0)) if trans_b else (lambda o, i, j, r: (o, 0, j))
    o_map = lambda o, i, j, r: (o, i, j)
    add_arrays = [x if x.ndim == 3 else x[None] for x, _ in adds]
    add_scales = [s for _, s in adds]
    n_add = len(adds)
    dot = _dot_nt if trans_b else _dot

    def finish(acc, add_refs, o_ref):
        for ref, s in zip(add_refs, add_scales):
            acc = acc + s * ref[...].astype(F32)
        o_ref[...] = acc.astype(o_ref.dtype)

    grid = (bo, m // tm, n // tn, steps)

    def body(*refs):
        ins, (o_ref,), scratch, xrefs = _split_refs(refs, 2 + n_add, 1, 1 if reduce_b else 0, ex)
        a_ref, b_ref, add_refs = ins[0], ins[1], ins[2:]
        step = ((pl.program_id(0) * grid[1] + pl.program_id(1)) * grid[2] + pl.program_id(2)) * grid[3] + pl.program_id(3)
        if ex:
            @pl.when(step == 0)
            def _():
                ex.start(*xrefs)

        if reduce_b:
            acc, = scratch
            r = pl.program_id(3)

            @pl.when(r == 0)
            def _():
                acc[...] = jnp.zeros_like(acc)

            acc[...] += dot(a_ref[...], b_ref[...])

            @pl.when(r == steps - 1)
            def _():
                finish(acc[...], add_refs, o_ref)
        else:
            finish(dot(a_ref[...], b_ref[...]), add_refs, o_ref)

        if ex:
            @pl.when(step == math.prod(grid) - 1)
            def _():
                ex.finish(*xrefs)

    b_block = (None, tn, k) if trans_b else (None, k, tn)
    out_bytes = tm * tn * jnp.dtype(out_dtype).itemsize
    need = 2 * (tm * k * a3.dtype.itemsize + k * tn * b3.dtype.itemsize + out_bytes + n_add * tm * tn * 4)
    need += 2 * tm * tn * 4
    sem = ("arbitrary",) * 4 if ex else ("parallel", "parallel", "parallel", "arbitrary")
    res = pl.pallas_call(
        body,
        grid=grid,
        in_specs=[pl.BlockSpec((None, tm, k), a_map), pl.BlockSpec(b_block, b_map)]
        + [pl.BlockSpec((None, tm, tn), o_map) for _ in adds] + (ex.in_specs if ex else []),
        out_specs=[pl.BlockSpec((None, tm, tn), o_map)] + (ex.out_specs if ex else []),
        out_shape=[jax.ShapeDtypeStruct((bo, m, n), out_dtype)] + (ex.out_shape if ex else []),
        scratch_shapes=([pltpu.VMEM((tm, tn), F32)] if reduce_b else []) + (ex.scratch if ex else []),
        compiler_params=_params(sem, need),
        name=name,
    )(a3, b3, *add_arrays, *(ex.arrays if ex else []))
    out = res[0][0] if (reduce_b or squeeze) else res[0]
    return (out, res[1:]) if ex else out


def _mm_tn(a, b, *, out_dtype, name):
    squeeze = b.ndim == 2
    b3 = b if b.ndim == 3 else b[None]
    t, m = a.shape
    bb, _, n = b3.shape
    tm = _pick(m, MM_TILES)
    tn = _pick(n, MM_TILES)
    tt = _pick(t, (1024, 512, 256, 128))
    steps = t // tt
    need = 2 * (tt * tm * a.dtype.itemsize + tt * tn * b3.dtype.itemsize + tm * tn * jnp.dtype(out_dtype).itemsize)
    need += 2 * tm * tn * 4

    def body(a_ref, b_ref, o_ref, acc):
        r = pl.program_id(3)

        @pl.when(r == 0)
        def _():
            acc[...] = jnp.zeros_like(acc)

        acc[...] += _dot_tn(a_ref[...], b_ref[...])

        @pl.when(r == steps - 1)
        def _():
            o_ref[...] = acc[...].astype(o_ref.dtype)

    out = pl.pallas_call(
        body,
        grid=(bb, m // tm, n // tn, steps),
        in_specs=[pl.BlockSpec((tt, tm), lambda o, i, j, r: (r, i)),
                  pl.BlockSpec((None, tt, tn), lambda o, i, j, r: (o, r, j))],
        out_specs=pl.BlockSpec((None, tm, tn), lambda o, i, j, r: (o, i, j)),
        out_shape=jax.ShapeDtypeStruct((bb, m, n), out_dtype),
        scratch_shapes=[pltpu.VMEM((tm, tn), F32)],
        compiler_params=_params(("parallel", "parallel", "parallel", "arbitrary"), need),
        name=name,
    )(a, b3)
    return out[0] if squeeze else out


def _rowwise(fn, rows, consts, row_outs, acc_outs, *, tile, name):
    first = rows[0][0] if isinstance(rows[0], tuple) else rows[0]
    t = first.shape[-2]
    steps = t // tile
    arrays, in_specs = [], []
    for r in rows:
        if isinstance(r, tuple) and isinstance(r[1], pl.BlockSpec):
            arrays.append(r[0])
            in_specs.append(r[1])
        elif isinstance(r, tuple):
            arr, bidx = r
            arrays.append(arr)
            in_specs.append(pl.BlockSpec((None, tile, arr.shape[-1]), functools.partial(lambda i, b: (b, i, 0), b=bidx)))
        else:
            arrays.append(r)
            in_specs.append(pl.BlockSpec((tile, r.shape[-1]), lambda i: (i, 0)))
    for c in consts:
        arrays.append(c)
        in_specs.append(pl.BlockSpec(c.shape, lambda i: (0, 0)))
    n_in, n_row = len(arrays), len(row_outs)
    out_shape, out_specs = [], []
    for ro in row_outs:
        if ro[0] == "stack":
            _, cnt, blk, total, w, dt = ro
            out_shape.append(jax.ShapeDtypeStruct((total, t, w), dt))
            out_specs.append(pl.BlockSpec((cnt, tile, w), functools.partial(lambda i, b: (b, i, 0), b=blk)))
        else:
            w, dt = ro
            out_shape.append(jax.ShapeDtypeStruct((t, w), dt))
            out_specs.append(pl.BlockSpec((tile, w), lambda i: (i, 0)))
    out_shape += [jax.ShapeDtypeStruct(s, F32) for s in acc_outs]
    out_specs += [pl.BlockSpec(s, lambda i: (0, 0)) for s in acc_outs]
    blocks = [math.prod(d for d in sp.block_shape if d) * arr.dtype.itemsize for sp, arr in zip(in_specs, arrays)]
    blocks += [math.prod(d for d in sp.block_shape if d) * jnp.dtype(sh.dtype).itemsize
               for sp, sh in zip(out_specs, out_shape)]
    need = 2 * sum(blocks) + 6 * tile * max(a.shape[-1] for a in arrays) * 4

    def body(*refs):
        ins, outs = refs[:n_in], refs[n_in:]
        i = pl.program_id(0)
        res = fn(i, steps, *[r[...] for r in ins])
        res = res if isinstance(res, (tuple, list)) else (res,)
        for ref, val in zip(outs[:n_row], res[:n_row]):
            ref[...] = val.astype(ref.dtype)
        if acc_outs:
            @pl.when(i == 0)
            def _():
                for ref in outs[n_row:]:
                    ref[...] = jnp.zeros_like(ref)

            for ref, val in zip(outs[n_row:], res[n_row:]):
                ref[...] += val

    return pl.pallas_call(
        body,
        grid=(steps,),
        in_specs=in_specs,
        out_specs=out_specs,
        out_shape=out_shape,
        compiler_params=_params(("arbitrary",), need),
        name=name,
    )(*arrays)


def _ln_stats(z):
    mu = _mean(z)
    zc = z - mu
    rstd = lax.rsqrt(_mean(zc * zc) + LN_EPS)
    return zc * rstd, rstd


def _ln_bwd(dy, xhat, rstd, g):
    dxh = dy * g
    return rstd * (dxh - _mean(dxh) - xhat * _mean(dxh * xhat))


def _ride(ex, xrefs, step, steps):
    if not ex:
        return
    for at, act in ((0, ex.start), (steps - 2, ex.relay), (steps - 1, ex.finish)):
        @pl.when(step == at)
        def _(act=act):
            act(*xrefs)


def _sgu_fwd(h, wm, bs_t, g_v, b_v, ex=None):
    t = h.shape[1]
    steps = t // SGU_BLOCK

    def body(*refs):
        (u_ref, v_ref, wm_ref, bs_ref, g_ref, b_ref), (y_ref,), _, xrefs = _split_refs(refs, 6, 1, 0, ex)
        xhat, _ = _ln_stats(_gelu(v_ref[...]))
        vn = (xhat * g_ref[...] + b_ref[...]).astype(BF16)
        gu = _gelu(u_ref[...])
        for g in range(GROUPS):
            sl = slice(g * HEAD, (g + 1) * HEAD)
            mixed = _dot(wm_ref[g], vn[:, sl]) + bs_ref[:, g:g + 1]
            y_ref[:, sl] = (gu[:, sl] * mixed).astype(BF16)
        _ride(ex, xrefs, pl.program_id(0), steps)

    blk = lambda b: pl.BlockSpec((None, SGU_BLOCK, D_MODEL), functools.partial(lambda i, b: (b, i, 0), b=b))
    whole = lambda s: pl.BlockSpec(s, lambda i: (0,) * len(s))
    res = pl.pallas_call(
        body,
        grid=(steps,),
        in_specs=[blk(SLOT_U), blk(SLOT_V), whole(wm.shape), whole(bs_t.shape), whole(g_v.shape), whole(b_v.shape)]
        + (ex.in_specs if ex else []),
        out_specs=[pl.BlockSpec((SGU_BLOCK, D_MODEL), lambda i: (i, 0))] + (ex.out_specs if ex else []),
        out_shape=[jax.ShapeDtypeStruct((t, D_MODEL), BF16)] + (ex.out_shape if ex else []),
        scratch_shapes=ex.scratch if ex else [],
        compiler_params=_params(("arbitrary",)),
        name="sgu_fwd",
    )(h, h, wm, bs_t, g_v, b_v, *(ex.arrays if ex else []))
    return res[0], res[1:]


def _sgu_bwd(h, dy, dh, wm, wm_t, bs_t, g_v, b_v):
    t = h.shape[1]

    def body(u_ref, v_ref, dy_ref, dh_in, wm_ref, wmt_ref, bs_ref, g_ref, b_ref,
             duv_ref, dw_ref, dbs_ref, dg_ref, db_ref, dvn_ref):
        del dh_in
        du_ref, dv_ref = duv_ref.at[0], duv_ref.at[1]
        i = pl.program_id(0)

        @pl.when(i == 0)
        def _():
            dw_ref[...] = jnp.zeros_like(dw_ref)
            dbs_ref[...] = jnp.zeros_like(dbs_ref)
            dg_ref[...] = jnp.zeros_like(dg_ref)
            db_ref[...] = jnp.zeros_like(db_ref)

        u = u_ref[...]
        v = v_ref[...]
        gv, gvp = _gelu_and_grad(v)
        xhat, rstd = _ln_stats(gv)
        vn = (xhat * g_ref[...] + b_ref[...]).astype(BF16)
        gu, gup = _gelu_and_grad(u)
        lane = lax.broadcasted_iota(jnp.int32, (SGU_BLOCK, LANE), 1)
        dbs = jnp.zeros((SGU_BLOCK, LANE), F32)
        for g in range(GROUPS):
            sl = slice(g * HEAD, (g + 1) * HEAD)
            vn_g = vn[:, sl]
            mixed = _dot(wm_ref[g], vn_g) + bs_ref[:, g:g + 1]
            dy_g = dy_ref[:, sl]
            du_ref[:, sl] = (dy_g * mixed * gup[:, sl]).astype(BF16)
            dmix = dy_g * gu[:, sl]
            dmb = dmix.astype(BF16)
            dvn_ref[:, sl] = _dot(wmt_ref[g], dmb)
            dw_ref[g] += _dot_nt(dmb, vn_g)
            dbs = dbs + jnp.where(lane == g, jnp.sum(dmix, axis=1, keepdims=True), 0.0)
        dbs_ref[...] += dbs
        dvn = dvn_ref[...]
        dg_ref[...] += _sum0(dvn * xhat)
        db_ref[...] += _sum0(dvn)
        dv_ref[...] = (_ln_bwd(dvn, xhat, rstd, g_ref[...]) * gvp).astype(BF16)

    blk = lambda b: pl.BlockSpec((None, SGU_BLOCK, D_MODEL), functools.partial(lambda i, b: (b, i, 0), b=b))
    row = pl.BlockSpec((SGU_BLOCK, D_MODEL), lambda i: (i, 0))
    whole = lambda s: pl.BlockSpec(s, lambda i: (0,) * len(s))
    vec = (1, D_MODEL)
    return pl.pallas_call(
        body,
        grid=(t // SGU_BLOCK,),
        in_specs=[blk(SLOT_U), blk(SLOT_V), row, pl.BlockSpec(memory_space=pl.ANY),
                  whole(wm.shape), whole(wm_t.shape), whole(bs_t.shape), whole(vec), whole(vec)],
        out_specs=[pl.BlockSpec((2, SGU_BLOCK, D_MODEL), lambda i: (SLOT_U // 2, i, 0)),
                   whole(wm.shape), whole((SGU_BLOCK, LANE)), whole(vec), whole(vec)],
        out_shape=[jax.ShapeDtypeStruct(dh.shape, BF16),
                   jax.ShapeDtypeStruct(wm.shape, F32), jax.ShapeDtypeStruct((SGU_BLOCK, LANE), F32),
                   jax.ShapeDtypeStruct(vec, F32), jax.ShapeDtypeStruct(vec, F32)],
        scratch_shapes=[pltpu.VMEM((SGU_BLOCK, D_MODEL), F32)],
        input_output_aliases={3: 0},
        compiler_params=_params(("arbitrary",)),
        name="sgu_bwd",
    )(h, h, dy, dh, wm, wm_t, bs_t, g_v, b_v)


def _split3(x):
    hi = x.astype(BF16)
    r1 = x - hi.astype(F32)
    mid = r1.astype(BF16)
    lo = (r1 - mid.astype(F32)).astype(BF16)
    return hi, mid, lo


def _tri_matmul(tri, x):
    hi, mid, lo = _split3(x)
    dot = lambda p: jnp.dot(tri, p, preferred_element_type=F32)
    return dot(hi) + dot(mid) + dot(lo)


def _lower_bound(logits):
    l0, l1 = logits[0:1, :], logits[1:2, :]
    mx = jnp.maximum(l0, l1)
    e0, e1 = jnp.exp(l0 - mx), jnp.exp(l1 - mx)
    return e0 / (e0 + e1)


def _hgrn_gates(q_raw, f_raw, lb):
    q = q_raw * jax.nn.sigmoid(q_raw)
    sig = jax.nn.sigmoid(f_raw)
    f = lb + (1.0 - lb) * sig
    row = lax.broadcasted_iota(jnp.int32, (CHUNK, CHUNK), 0)
    col = lax.broadcasted_iota(jnp.int32, (CHUNK, CHUNK), 1)
    c = _tri_matmul((row >= col).astype(BF16), jnp.log(f))
    return q, sig, f, 1.0 - f, c


def _offdiag_terms(qh, kh, ch, tb):
    rows = slice(tb * SUB, (tb + 1) * SUB)
    r = ch[tb * SUB - 1:tb * SUB, :]
    eqh = jnp.exp(ch[rows] - r)
    ekh = jnp.exp(jnp.minimum(r - ch, 0.0))
    return rows, eqh, qh[rows] * eqh, ekh, kh * ekh


def _diag_decay(cb, s, trow):
    return jnp.exp(jnp.where(trow >= s, cb - cb[s:s + 1, :], NEG))


def _split_refs(refs, n_in, n_out, n_scratch, ex):
    nx = ex.n if ex else 0
    ins, refs = refs[:n_in], refs[n_in:]
    xsrc, refs = refs[:nx], refs[nx:]
    outs, refs = refs[:n_out], refs[n_out:]
    xout, refs = refs[:nx], refs[nx:]
    return ins, outs, refs[:n_scratch], (xsrc, xout, refs[n_scratch:])


def _hgrn_fwd(h, logits, g_norm, ex=None):
    t = h.shape[1]
    nc = t // CHUNK

    def body(*refs):
        ins, outs, scratch, xrefs = _split_refs(refs, 6, 3, 4, ex)
        q_ref, f_ref, i_ref, og_ref, lg_ref, gn_ref = ins
        y_ref, o_ref, sall_ref = outs
        st_ref, q_s, k_s, c_s = scratch

        @pl.when(pl.program_id(0) == 0)
        def _():
            st_ref[...] = jnp.zeros_like(st_ref)

        lb = _lower_bound(lg_ref[...])
        q, _, _, k, c = _hgrn_gates(q_ref[...], f_ref[...], lb)
        q_s[...] = q
        k_s[...] = k
        c_s[...] = c
        col64 = lax.broadcasted_iota(jnp.int32, (SUB, CHUNK), 1)
        trow = lax.broadcasted_iota(jnp.int32, (SUB, HEAD), 0)

        def head(hd, carry):
            sl = pl.ds(pl.multiple_of(hd * HEAD, HEAD), HEAD)
            qh, kh, ch, ih = q_s[:, sl], k_s[:, sl], c_s[:, sl], i_ref[:, sl]
            st = st_ref[hd]
            sall_ref[hd] = st
            c_last = ch[CHUNK - 1:CHUNK, :]
            o = _dot_nt(qh * jnp.exp(ch), st)
            st_ref[hd] = st * jnp.exp(c_last) + _dot_tn(ih, kh * jnp.exp(c_last - ch))
            a_rows = [jnp.zeros((SUB, CHUNK), F32)]
            for tb in range(1, CHUNK // SUB):
                _, _, q_hat, _, k_hat = _offdiag_terms(qh, kh, ch, tb)
                a_rows.append(jnp.where(col64 < tb * SUB, _dot_nt(q_hat, k_hat), 0.0))
            o = o + _dot(jnp.concatenate(a_rows, axis=0), ih)
            o_rows = []
            for b in range(CHUNK // SUB):
                rows = slice(b * SUB, (b + 1) * SUB)
                qb, cb, kb, ib = qh[rows], ch[rows], kh[rows], ih[rows]
                ob = jnp.zeros((SUB, HEAD), F32)
                for s in range(SUB):
                    a = jnp.sum(qb * _diag_decay(cb, s, trow) * kb[s:s + 1, :], axis=1, keepdims=True)
                    ob = ob + a * ib[s:s + 1, :]
                o_rows.append(ob)
            o = o + jnp.concatenate(o_rows, axis=0)
            o_ref[:, sl] = o
            og = og_ref[:, sl]
            on = o * lax.rsqrt(_mean(o * o) + RMS_EPS)
            y_ref[:, sl] = (on * gn_ref[:, sl] * (og * jax.nn.sigmoid(og))).astype(BF16)
            return carry

        lax.fori_loop(0, HEADS, head, 0, unroll=HEAD_UNROLL)
        _ride(ex, xrefs, pl.program_id(0), nc)

    blk = lambda b: pl.BlockSpec((None, CHUNK, D_MODEL), functools.partial(lambda n, b: (b, n, 0), b=b))
    row = pl.BlockSpec((CHUNK, D_MODEL), lambda n: (n, 0))
    whole = lambda s: pl.BlockSpec(s, lambda n: (0,) * len(s))
    res = pl.pallas_call(
        body,
        grid=(nc,),
        in_specs=[blk(SLOT_Q), blk(SLOT_F), blk(SLOT_I), blk(SLOT_OG), whole(logits.shape), whole(g_norm.shape)]
        + (ex.in_specs if ex else []),
        out_specs=[row, row, pl.BlockSpec((None, HEADS, HEAD, HEAD), lambda n: (n, 0, 0, 0))] + (ex.out_specs if ex else []),
        out_shape=[jax.ShapeDtypeStruct((t, D_MODEL), BF16), jax.ShapeDtypeStruct((t, D_MODEL), F32),
                   jax.ShapeDtypeStruct((nc, HEADS, HEAD, HEAD), F32)] + (ex.out_shape if ex else []),
        scratch_shapes=[pltpu.VMEM((HEADS, HEAD, HEAD), F32)] + [pltpu.VMEM((CHUNK, D_MODEL), F32)] * 3
        + (ex.scratch if ex else []),
        compiler_params=_params(("arbitrary",)),
        name="hgrn_fwd",
    )(h, h, h, h, logits, g_norm, *(ex.arrays if ex else []))
    return res[0], res[1], res[2], res[3:]


def _hgrn_bwd(h, o_all, dy, states, dh, logits, g_norm, ex=None):
    t = h.shape[1]
    nc = t // CHUNK

    def body(*refs):
        ins, outs, scratch, xrefs = _split_refs(refs, 10, 3, 8, ex)
        q_ref, f_ref, i_ref, og_ref, o_ref, dy_ref, sall_ref, _, lg_ref, gn_ref = ins
        dqfio_ref, dlg_ref, dgn_ref = outs
        dst_ref, dlb_ref, q_s, k_s, c_s, dq_s, dk_s, dc_s = scratch
        dq_ref, df_ref, di_ref, dog_ref = (dqfio_ref.at[s] for s in (SLOT_Q, SLOT_F, SLOT_I, SLOT_OG))
        n = pl.program_id(0)

        @pl.when(n == 0)
        def _():
            dst_ref[...] = jnp.zeros_like(dst_ref)
            dlb_ref[...] = jnp.zeros_like(dlb_ref)
            dgn_ref[...] = jnp.zeros_like(dgn_ref)
            if ex:
                ex.start(*xrefs)

        lb = _lower_bound(lg_ref[...])
        q_raw = q_ref[...]
        q, sig, f, k, c = _hgrn_gates(q_raw, f_ref[...], lb)
        q_s[...] = q
        k_s[...] = k
        c_s[...] = c
        col64 = lax.broadcasted_iota(jnp.int32, (SUB, CHUNK), 1)
        trow = lax.broadcasted_iota(jnp.int32, (SUB, HEAD), 0)
        row64 = lax.broadcasted_iota(jnp.int32, (CHUNK, HEAD), 0)

        def head(hd, carry):
            sl = pl.ds(pl.multiple_of(hd * HEAD, HEAD), HEAD)
            qh, kh, ch, ih = q_s[:, sl], k_s[:, sl], c_s[:, sl], i_ref[:, sl]
            st = sall_ref[hd]
            dst = dst_ref[hd]
            oh, dyh, og, gn = o_ref[:, sl], dy_ref[:, sl], og_ref[:, sl], gn_ref[:, sl]
            sg = jax.nn.sigmoid(og)
            sil = og * sg
            rms = lax.rsqrt(_mean(oh * oh) + RMS_EPS)
            on = oh * rms
            dog_ref[:, sl] = (dyh * on * gn * _silu_grad(og, sg)).astype(BF16)
            dgn_ref[:, sl] += _sum0(dyh * on * sil)
            don = dyh * gn * sil
            do = rms * (don - on * _mean(don * on))
            dob = do.astype(BF16)

            c_last = ch[CHUNK - 1:CHUNK, :]
            eq = jnp.exp(ch)
            q_til = qh * eq
            ekl = jnp.exp(c_last - ch)
            k_til = kh * ekl
            ecl = jnp.exp(c_last)
            dq_til = _dot(dob, st)
            dk_til = _dot(ih, dst)
            di = _dot_nt(k_til, dst)
            dc_last = _sum0(dk_til * k_til) + _sum0(dst * st) * ecl
            dst_ref[hd] = _dot_tn(dob, q_til) + dst * ecl
            dq = dq_til * eq
            dc = dq_til * q_til - dk_til * k_til
            dk = dk_til * ekl

            da_full = _dot_nt(dob, ih)
            a_rows = [jnp.zeros((SUB, CHUNK), F32)]
            dq_rows = [jnp.zeros((SUB, HEAD), F32)]
            dc_rows = [jnp.zeros((SUB, HEAD), F32)]
            for tb in range(1, CHUNK // SUB):
                rows, eqh, q_hat, ekh, k_hat = _offdiag_terms(qh, kh, ch, tb)
                keep = col64 < tb * SUB
                a_rows.append(jnp.where(keep, _dot_nt(q_hat, k_hat), 0.0))
                da = jnp.where(keep, da_full[rows], 0.0)
                dq_hat = _dot(da, k_hat)
                dk_hat = _dot_tn(da, q_hat)
                dq_rows.append(dq_hat * eqh)
                dc_rows.append(dq_hat * q_hat)
                dk = dk + dk_hat * ekh
                dc = dc - dk_hat * k_hat
            di = di + _dot_tn(jnp.concatenate(a_rows, axis=0), dob)

            dk_rows, di_rows = [], []
            for b in range(CHUNK // SUB):
                rows = slice(b * SUB, (b + 1) * SUB)
                qb, cb, kb, ib, dob_ = qh[rows], ch[rows], kh[rows], ih[rows], do[rows]
                dq_diag = jnp.zeros((SUB, HEAD), F32)
                dk_diag = jnp.zeros((SUB, HEAD), F32)
                di_diag = jnp.zeros((SUB, HEAD), F32)
                for s in range(SUB):
                    ks = kb[s:s + 1, :]
                    dec = _diag_decay(cb, s, trow)
                    a = jnp.sum(qb * dec * ks, axis=1, keepdims=True)
                    gk = jnp.sum(dob_ * ib[s:s + 1, :], axis=1, keepdims=True) * dec
                    dq_diag = dq_diag + gk * ks
                    dk_diag = dk_diag + jnp.where(trow == s, _sum0(gk * qb), 0.0)
                    di_diag = di_diag + jnp.where(trow == s, _sum0(a * dob_), 0.0)
                dq_rows[b] = dq_rows[b] + dq_diag
                dc_rows[b] = dc_rows[b] + qb * dq_diag - kb * dk_diag
                dk_rows.append(dk_diag)
                di_rows.append(di_diag)
            dq = dq + jnp.concatenate(dq_rows, axis=0)
            dk = dk + jnp.concatenate(dk_rows, axis=0)
            dc = dc + jnp.concatenate(dc_rows, axis=0) + jnp.where(row64 == CHUNK - 1, dc_last, 0.0)
            di_ref[:, sl] = (di + jnp.concatenate(di_rows, axis=0)).astype(BF16)
            dq_s[:, sl] = dq
            dk_s[:, sl] = dk
            dc_s[:, sl] = dc
            return carry

        lax.fori_loop(0, HEADS, head, 0, unroll=HEAD_UNROLL)

        row = lax.broadcasted_iota(jnp.int32, (CHUNK, CHUNK), 0)
        col = lax.broadcasted_iota(jnp.int32, (CHUNK, CHUNK), 1)
        dlf = _tri_matmul((row <= col).astype(BF16), dc_s[...])
        df = dlf / f - dk_s[...]
        dlb_ref[...] += _sum0(df * (1.0 - sig))
        df_ref[...] = (df * (1.0 - lb) * sig * (1.0 - sig)).astype(BF16)
        dq_ref[...] = (dq_s[...] * _silu_grad(q_raw, jax.nn.sigmoid(q_raw))).astype(BF16)

        @pl.when(n == nc - 1)
        def _():
            d0 = dlb_ref[...] * lb * (1.0 - lb)
            dlg_ref[0:1, :] = d0
            dlg_ref[1:2, :] = -d0
            if ex:
                ex.finish(*xrefs)

    rev = lambda n: nc - 1 - n
    blk = lambda b: pl.BlockSpec((None, CHUNK, D_MODEL), functools.partial(lambda n, b: (b, rev(n), 0), b=b))
    row = pl.BlockSpec((CHUNK, D_MODEL), lambda n: (rev(n), 0))
    whole = lambda s: pl.BlockSpec(s, lambda n: (0,) * len(s))
    vec = (1, D_MODEL)
    res = pl.pallas_call(
        body,
        grid=(nc,),
        in_specs=[blk(SLOT_Q), blk(SLOT_F), blk(SLOT_I), blk(SLOT_OG), row, row,
                  pl.BlockSpec((None, HEADS, HEAD, HEAD), lambda n: (rev(n), 0, 0, 0)),
                  pl.BlockSpec(memory_space=pl.ANY), whole(logits.shape), whole(vec)] + (ex.in_specs if ex else []),
        out_specs=[pl.BlockSpec((4, CHUNK, D_MODEL), lambda n: (SLOT_Q // 4, rev(n), 0)), whole((2, D_MODEL)), whole(vec)]
        + (ex.out_specs if ex else []),
        out_shape=[jax.ShapeDtypeStruct(dh.shape, BF16), jax.ShapeDtypeStruct((2, D_MODEL), F32),
                   jax.ShapeDtypeStruct(vec, F32)] + (ex.out_shape if ex else []),
        scratch_shapes=[pltpu.VMEM((HEADS, HEAD, HEAD), F32), pltpu.VMEM(vec, F32)]
        + [pltpu.VMEM((CHUNK, D_MODEL), F32)] * 6 + (ex.scratch if ex else []),
        input_output_aliases={7: 0},
        compiler_params=_params(("arbitrary",)),
        name="hgrn_bwd",
    )(h, h, h, h, o_all, dy, states, dh, logits, g_norm, *(ex.arrays if ex else []))
    return res[0], res[1], res[2], res[3:]


def _merge_fwd(i, n, ga, gb, za, zb):
    return jax.nn.sigmoid(ga) * za + jax.nn.sigmoid(gb) * zb


def _merge_bwd(i, n, ga, gb, za, zb, dm):
    sa, sb = jax.nn.sigmoid(ga), jax.nn.sigmoid(gb)
    dgates = jnp.stack([(dm * za * sa * (1.0 - sa)).astype(BF16), (dm * zb * sb * (1.0 - sb)).astype(BF16)])
    return dgates, dm * sa, dm * sb


def _ln1_fwd(i, n, x, r1, g, b):
    xhat, _ = _ln_stats(ALPHA * x + r1)
    x1 = xhat * g + b
    return x1, x1


def _ln1_bwd(i, n, x, r1, dx1, g):
    xhat, rstd = _ln_stats(ALPHA * x + r1)
    dz = _ln_bwd(dx1, xhat, rstd, g)
    return dz, dz, _sum0(dx1 * xhat), _sum0(dx1)


def _ln2_loss(i, n, x1, fo, pg, pp, tgt, g, b):
    sg = jax.nn.sigmoid(pg)
    xhat, rstd = _ln_stats(ALPHA * x1 + fo + sg * pp)
    diff = xhat * g + b - tgt
    loss = 0.5 * jnp.sum(_mean(diff * diff), axis=0, keepdims=True)
    dy = diff * (1.0 / D_MODEL)
    dz = _ln_bwd(dy, xhat, rstd, g)
    return (dz, dz, dz * pp * sg * (1.0 - sg), dz * sg,
            jnp.broadcast_to(loss, (8, LANE)), _sum0(dy * xhat), _sum0(dy))


def _shift_down(cur, halo, tile):
    row = lax.broadcasted_iota(jnp.int32, cur.shape, 0)
    m1 = jnp.where(row == 0, halo[7:8, :], pltpu.roll(cur, 1, 0))
    m2 = jnp.where(row == 0, halo[6:7, :], jnp.where(row == 1, halo[7:8, :], pltpu.roll(cur, 2, 0)))
    return m1, m2


def _shift_up(cur, halo, tile):
    row = lax.broadcasted_iota(jnp.int32, cur.shape, 0)
    p1 = jnp.where(row == tile - 1, halo[0:1, :], pltpu.roll(cur, tile - 1, 0))
    p2 = jnp.where(row == tile - 2, halo[0:1, :], jnp.where(row == tile - 1, halo[1:2, :], pltpu.roll(cur, tile - 2, 0)))
    return p1, p2


def _conv_pre(i, gate, halo, w, b, tile):
    halo = jnp.where(i == 0, 0.0, halo)
    m1, m2 = _shift_down(gate, halo, tile)
    return w[0:1, :] * m2 + w[1:2, :] * m1 + w[2:3, :] * gate + b, m1, m2


def _conv_fwd(tile, i, n, gate, halo, val, w, b):
    cg, _, _ = _conv_pre(i, gate, halo, w, b, tile)
    return _gelu(cg) * val


def _conv_bwd_a(tile, i, n, gate, halo, val, dhid, w, b):
    cg, m1, m2 = _conv_pre(i, gate, halo, w, b, tile)
    act, slope = _gelu_and_grad(cg)
    dcg = dhid * val * slope
    return dcg, dhid * act, _sum0(dcg * m2), _sum0(dcg * m1), _sum0(dcg * gate), _sum0(dcg)


def _conv_bwd_b(tile, i, n, dcg, halo, w):
    dcg = dcg.astype(F32)
    halo = jnp.where(i == n - 1, 0.0, halo.astype(F32))
    p1, p2 = _shift_up(dcg, halo, tile)
    return w[2:3, :] * dcg + w[1:2, :] * p1 + w[0:1, :] * p2


def _halo_spec(width, tile, t, nxt, rows=8):
    per = tile // rows
    last = t // rows - 1
    if nxt:
        return pl.BlockSpec((rows, width), lambda i: (jnp.minimum((i + 1) * per, last), 0))
    return pl.BlockSpec((rows, width), lambda i: (jnp.maximum(i * per - 1, 0), 0))


def _adamw(i, n, w, m, v, parts):
    g = parts[0].astype(F32)
    for j in range(1, parts.shape[0]):
        g = g + parts[j].astype(F32)
    m_new = ADAM_B1 * m + (1.0 - ADAM_B1) * g
    v_new = ADAM_B2 * v + (1.0 - ADAM_B2) * (g * g)
    m_hat = m_new / (1.0 - ADAM_B1 ** ADAM_STEP)
    v_hat = v_new / (1.0 - ADAM_B2 ** ADAM_STEP)
    delta = -ADAM_LR * (m_hat / (jnp.sqrt(v_hat) + ADAM_EPS) + ADAM_WD * w)
    return g, delta, m_new, v_new


def _adam_call(w, m, v, parts, name):
    r, c = w.shape
    tile = _pick(r, (256, 128)) if r > 256 else r
    spec = pl.BlockSpec((parts.shape[0], tile, c), lambda i: (0, i, 0))
    return _rowwise(_adamw, [w, m, v, (parts, spec)], [], [(c, F32)] * 4, [], tile=tile, name=name)


def _peer(k):
    x, y, c = lax.axis_index("x"), lax.axis_index("y"), lax.axis_index("c")
    px = x ^ ((k >> 2) & 1)
    py = y ^ ((k >> 1) & 1)
    pc = c ^ (k & 1)
    return (px, py, pc), 4 * px + 2 * py + pc


def _my_index():
    return 4 * lax.axis_index("x") + 2 * lax.axis_index("y") + lax.axis_index("c")


class _Exchange:
    KINDS = ("gather", "gather+relay", "scatter", "scatter+pairs")

    def __init__(self, entries):
        assert all(k in self.KINDS for _, k in entries), [k for _, k in entries]
        self.arrays = [a for a, _ in entries]
        self.scatter = [k.startswith("scatter") for _, k in entries]
        self.relayed = ["+relay" in k for _, k in entries]
        self.pairs = ["+pairs" in k for _, k in entries]
        self.n = len(entries)
        self.in_specs = [pl.BlockSpec(memory_space=pl.ANY)] * self.n
        self.out_specs = [pl.BlockSpec(memory_space=pl.ANY)] * self.n
        shapes = [tuple(a.shape[1:]) if sc else tuple(a.shape) for a, sc in zip(self.arrays, self.scatter)]
        counts = [N_CHIP if p else N_DEV for p in self.pairs]
        self.out_shape = [jax.ShapeDtypeStruct((n,) + s, a.dtype) for n, s, a in zip(counts, shapes, self.arrays)]
        per = N_DEV - 1
        self.scratch = [pltpu.SemaphoreType.DMA((self.n * per,)), pltpu.SemaphoreType.DMA((self.n * per,)),
                        pltpu.SemaphoreType.DMA((self.n,))]

    def _copies(self, srcs, outs, sems):
        send_sems, recv_sems, local_sems = sems
        x, y, c = lax.axis_index("x"), lax.axis_index("y"), lax.axis_index("c")
        me = _my_index()
        per = N_DEV - 1
        local, first, passed, relay_arrivals, arrivals = [], [], [], [], []
        for a in range(self.n):

            def copy(k, src, dst, dev, a=a):
                return pltpu.make_async_remote_copy(
                    src_ref=src, dst_ref=dst, send_sem=send_sems.at[a * per + k], recv_sem=recv_sems.at[a * per + k],
                    device_id=dev, device_id_type=pl.DeviceIdType.MESH)

            if self.pairs[a]:
                chip = 2 * x + y
                for k in range(N_CHIP):
                    to = chip ^ k
                    piece = srcs[a].at[_slot_of_group(2 * to + c) // 2]
                    if k == 0:
                        local.append(pltpu.make_async_copy(piece, outs[a].at[chip], local_sems.at[a]))
                    else:
                        dev = (to // 2, to % 2, c)
                        first.append(copy(k - 1, piece, outs[a].at[chip], dev))
                        arrivals.append(copy(k - 1, piece, outs[a].at[to], dev))
                continue
            mine = srcs[a].at[me] if self.scatter[a] else srcs[a]
            land = outs[a].at[me]
            if not self.relayed[a]:
                local.append(pltpu.make_async_copy(mine, land, local_sems.at[a]))
            if self.relayed[a]:
                block = lambda px, py, pc, a=a: outs[a].at[4 * px + 2 * py + pc]
                chips = [(1 - x, y), (x, 1 - y), (1 - x, 1 - y)]
                first.append(copy(0, mine, land, (x, y, 1 - c)))
                arrivals.append(copy(0, mine, block(x, y, 1 - c), (x, y, 1 - c)))
                for j, (px, py) in enumerate(chips):
                    first.append(copy(1 + j, mine, land, (px, py, c)))
                    relay_arrivals.append(copy(1 + j, mine, block(px, py, c), (px, py, c)))
                    passed.append(copy(4 + j, block(px, py, c), block(px, py, c), (x, y, 1 - c)))
                    arrivals.append(copy(4 + j, mine, block(px, py, 1 - c), (x, y, 1 - c)))
                continue
            for k in range(1, N_DEV):
                dev, idx = _peer(k)
                if self.scatter[a]:
                    first.append(copy(k - 1, srcs[a].at[idx], land, dev))
                else:
                    first.append(copy(k - 1, mine, land, dev))
                arrivals.append(copy(k - 1, mine, outs[a].at[idx], dev))
        return local, first, passed, relay_arrivals, arrivals

    def start(self, srcs, outs, sems):
        local, first, _, _, _ = self._copies(srcs, outs, sems)
        for cp in local + first:
            cp.start()

    def relay(self, srcs, outs, sems):
        _, _, passed, relay_arrivals, _ = self._copies(srcs, outs, sems)
        for landed, onward in zip(relay_arrivals, passed):
            landed.wait_recv()
            onward.start()

    def finish(self, srcs, outs, sems):
        local, first, passed, _, arrivals = self._copies(srcs, outs, sems)
        for cp in arrivals:
            cp.wait_recv()
        for cp in first + passed:
            cp.wait_send()
        for cp in local:
            cp.wait()


def _gather_project(x_b, shard):
    t, d = x_b.shape
    tm = _pick(t, MM_TILES)
    nrow = t // tm
    per = N_DEV - 1

    def parties():
        x, y, c = lax.axis_index("x"), lax.axis_index("y"), lax.axis_index("c")
        chips = [(1 - x, y), (x, 1 - y), (1 - x, 1 - y)]
        return (x, y, c), (x, y, 1 - c), [(px, py, c) for px, py in chips], [(px, py, 1 - c) for px, py in chips]

    slot = lambda dev: _slot_of_group(4 * dev[0] + 2 * dev[1] + dev[2])
    ici_step, passed_step = (2, 3, 6), (4, 5, 7)
    me, sibling, over_ici, passed_on = parties()
    by_step = {0: me, 1: sibling, **dict(zip(ici_step, over_ici)), **dict(zip(passed_step, passed_on))}
    order = jnp.stack([slot(by_step[j]) for j in range(N_DEV)]).astype(jnp.int32)

    def body(order_ref, x_ref, shard_ref, h_ref, wall_ref, wbuf, fetch_sem, send_sems, recv_sems):
        del order_ref
        me, sibling, over_ici, passed_on = parties()
        j, i = pl.program_id(0), pl.program_id(1)
        land = lambda dev: wall_ref.at[slot(dev)]

        def copy(k, src, block, to):
            return pltpu.make_async_remote_copy(src_ref=src, dst_ref=land(block), send_sem=send_sems.at[k],
                                                recv_sem=recv_sems.at[k], device_id=to, device_id_type=pl.DeviceIdType.MESH)

        def fetch(src):
            cp = pltpu.make_async_copy(src, wbuf, fetch_sem)
            cp.start()
            cp.wait()

        first = [copy(0, shard_ref, me, sibling)] + [copy(1 + n, shard_ref, me, dev) for n, dev in enumerate(over_ici)]
        onward = [copy(4 + n, land(dev), dev, sibling) for n, dev in enumerate(over_ici)]

        @pl.when(jnp.logical_and(i == 0, j == 0))
        def _():
            for cp in first:
                cp.start()
            fetch(shard_ref)

        @pl.when(jnp.logical_and(i == 0, j == 1))
        def _():
            copy(0, shard_ref, sibling, me).wait_recv()
            fetch(land(sibling))

        for n, dev in enumerate(over_ici):
            @pl.when(jnp.logical_and(i == 0, j == ici_step[n]))
            def _(n=n, dev=dev):
                copy(1 + n, shard_ref, dev, me).wait_recv()
                onward[n].start()
                fetch(land(dev))

        for n, dev in enumerate(passed_on):
            @pl.when(jnp.logical_and(i == 0, j == passed_step[n]))
            def _(n=n, dev=dev):
                copy(4 + n, shard_ref, dev, me).wait_recv()
                fetch(land(dev))

        h_ref[...] = _dot(x_ref[...], wbuf[...])

        @pl.when(jnp.logical_and(i == nrow - 1, j == N_DEV - 1))
        def _():
            for cp in first + onward:
                cp.wait_send()

    need = 2 * (tm * d * 2 + tm * d * 4) + d * d * 2 + tm * d * 4
    h, w_all = pl.pallas_call(
        body,
        grid_spec=pltpu.PrefetchScalarGridSpec(
            num_scalar_prefetch=1,
            grid=(N_DEV, nrow),
            in_specs=[pl.BlockSpec((tm, d), lambda j, i, order: (i, 0)), pl.BlockSpec(memory_space=pl.ANY)],
            out_specs=[pl.BlockSpec((None, tm, d), lambda j, i, order: (order[j], i, 0)), pl.BlockSpec(memory_space=pl.ANY)],
            scratch_shapes=[pltpu.VMEM((d, d), BF16), pltpu.SemaphoreType.DMA, pltpu.SemaphoreType.DMA((per,)),
                            pltpu.SemaphoreType.DMA((per,))],
        ),
        out_shape=[jax.ShapeDtypeStruct((N_DEV, t, d), F32), jax.ShapeDtypeStruct((N_DEV, d, d), BF16)],
        compiler_params=_params(("arbitrary", "arbitrary"), need),
        name="gather_project",
    )(order, x_b, shard)
    return h, lax.dynamic_update_slice_in_dim(w_all, shard[None], slot(me), axis=0)


def _pair_sums(g):
    n, r, cols = g.shape
    half = n // 2

    def swap(g_ref, got_ref, send_sems, recv_sems):
        x, y, c = lax.axis_index("x"), lax.axis_index("y"), lax.axis_index("c")
        copies = [pltpu.make_async_remote_copy(
            src_ref=g_ref.at[2 * j + 1 - c], dst_ref=got_ref.at[j], send_sem=send_sems.at[j], recv_sem=recv_sems.at[j],
            device_id=(x, y, 1 - c), device_id_type=pl.DeviceIdType.MESH) for j in range(half)]
        for cp in copies:
            cp.start()
        for cp in copies:
            cp.wait()

    got = pl.pallas_call(
        swap,
        in_specs=[pl.BlockSpec(memory_space=pl.ANY)],
        out_specs=pl.BlockSpec(memory_space=pl.ANY),
        out_shape=jax.ShapeDtypeStruct((half, r, cols), g.dtype),
        scratch_shapes=[pltpu.SemaphoreType.DMA((half,))] * 2,
        name="pair_swap",
    )(g)

    def add(mine_ref, got_ref, out_ref):
        mine = jnp.where(lax.axis_index("c") == 0, mine_ref[0].astype(F32), mine_ref[1].astype(F32))
        out_ref[...] = (mine + got_ref[...].astype(F32)).astype(out_ref.dtype)

    tile = _pick(r, (512, 256, 128))
    return pl.pallas_call(
        add,
        grid=(half, r // tile),
        in_specs=[pl.BlockSpec((None, 2, tile, cols), lambda j, i: (j, 0, i, 0)),
                  pl.BlockSpec((None, tile, cols), lambda j, i: (j, i, 0))],
        out_specs=pl.BlockSpec((None, tile, cols), lambda j, i: (j, i, 0)),
        out_shape=jax.ShapeDtypeStruct((half, r, cols), g.dtype),
        compiler_params=_params(("parallel", "parallel")),
        name="pair_add",
    )(g.reshape(half, 2, r, cols), got)


def _local_step(x, p, tgt, small, comm):
    t = x.shape[0]
    tile = _pick(t, (256, 128))
    d = D_MODEL
    act_b, act_f = (d, BF16), (d, F32)
    x_b, p_b = x.astype(BF16), p.astype(BF16)

    chunk_id = jnp.arange(SGU_BLOCK) // CHUNK
    mask = chunk_id[:, None] >= chunk_id[None, :]
    wm = jnp.where(mask[None], small["sgu_w_s"], 0.0)
    wm_b = wm.astype(BF16)
    wm_t = jnp.swapaxes(wm, 1, 2).astype(BF16)
    bs_t = small["sgu_b_s"].T

    h, w_in = comm.project_in(x_b)
    y_a, got_a = _sgu_fwd(h, wm_b, bs_t, small["sgu_norm_g"], small["sgu_norm_b"], ex=comm.weights_exchange(0))
    y_b, o_all, states, got_b = _hgrn_fwd(h, small["lb_logits"], small["hgrn_norm_g"], ex=comm.weights_exchange(1))
    wts, conv_w = comm.weights(list(got_a) + list(got_b))
    z_a = _mm(y_a, wts["w_a"], out_dtype=F32, name="mm_za")
    z_b = _mm(y_b, wts["w_b"], out_dtype=F32, name="mm_zb")
    gates = [(h, SLOT_GA), (h, SLOT_GB)]
    merged, = _rowwise(_merge_fwd, gates + [z_a, z_b], [], [act_b], [], tile=tile, name="merge_fwd")
    r1 = _mm(merged, wts["w_o"], out_dtype=F32, name="mm_r1")
    x1, x1_b = _rowwise(_ln1_fwd, [x, r1], [small["ln1_g"], small["ln1_b"]], [act_f, act_b], [], tile=tile, name="ln1_fwd")
    gate = _mm(x1_b, wts["w_g"], out_dtype=F32, name="mm_gate")
    val = _mm(x1_b, wts["w_v"], out_dtype=F32, name="mm_val")
    pg = _mm(x1_b, wts["w_pg"], out_dtype=F32, name="mm_pg")
    pp = _mm(p_b, wts["w_pp"], out_dtype=F32, name="mm_pp")
    hid, = _rowwise(functools.partial(_conv_fwd, tile), [gate, (gate, _halo_spec(D_FF, tile, t, False)), val],
                    [conv_w, small["conv_b"]], [(D_FF, BF16)], [], tile=tile, name="conv_fwd")
    fo = _mm(hid, wts["w_down"], out_dtype=F32, name="mm_down")
    dz2, dz2_b, dpg, dpp, loss, dg2, db2 = _rowwise(
        _ln2_loss, [x1, fo, pg, pp, tgt], [small["ln2_g"], small["ln2_b"]],
        [act_f, act_b, act_b, act_b], [(8, LANE), (1, d), (1, d)], tile=tile, name="ln2_loss")

    dhid = _mm(dz2_b, wts["w_down"], out_dtype=BF16, name="mm_dhid", trans_b=True)
    g_down = _mm_tn(hid, dz2_b, out_dtype=BF16, name="mm_g_down")
    dcg, dval, dcw0, dcw1, dcw2, dcb = _rowwise(
        functools.partial(_conv_bwd_a, tile), [gate, (gate, _halo_spec(D_FF, tile, t, False)), val, dhid],
        [conv_w, small["conv_b"]], [(D_FF, BF16), (D_FF, BF16)], [(1, D_FF)] * 4, tile=tile, name="conv_bwd_a")
    dgate, = _rowwise(functools.partial(_conv_bwd_b, tile), [dcg, (dcg, _halo_spec(D_FF, tile, t, True, rows=16))],
                      [conv_w], [(D_FF, BF16)], [], tile=tile, name="conv_bwd_b")
    g_g = _mm_tn(x1_b, dgate, out_dtype=BF16, name="mm_g_gate")
    g_v = _mm_tn(x1_b, dval, out_dtype=BF16, name="mm_g_val")
    g_pg = _mm_tn(x1_b, dpg, out_dtype=BF16, name="mm_g_pg")
    g_pp = _mm_tn(p_b, dpp, out_dtype=BF16, name="mm_g_pp")
    dx1 = _mm(dgate, wts["w_g"], out_dtype=F32, name="mm_dx1_gate", trans_b=True, adds=[(dz2, ALPHA)])
    dx1 = _mm(dval, wts["w_v"], out_dtype=F32, name="mm_dx1_val", trans_b=True, adds=[(dx1, 1.0)])
    dx1 = _mm(dpg, wts["w_pg"], out_dtype=F32, name="mm_dx1_pg", trans_b=True, adds=[(dx1, 1.0)])
    dz1, dz1_b, dg1, db1 = _rowwise(_ln1_bwd, [x, r1, dx1], [small["ln1_g"]], [act_f, act_b], [(1, d), (1, d)],
                                    tile=tile, name="ln1_bwd")
    g_o = _mm_tn(merged, dz1_b, out_dtype=BF16, name="mm_g_o")
    dm = _mm(dz1_b, wts["w_o"], out_dtype=BF16, name="mm_dm", trans_b=True)
    dh, dza, dzb = _rowwise(_merge_bwd, gates + [z_a, z_b, dm], [],
                            [("stack", 2, SLOT_GA // 2, 8, d, BF16), act_b, act_b], [], tile=tile, name="merge_bwd")
    g_a = _mm_tn(y_a, dza, out_dtype=BF16, name="mm_g_a")
    g_b = _mm_tn(y_b, dzb, out_dtype=BF16, name="mm_g_b")
    dy_a = _mm(dza, wts["w_a"], out_dtype=BF16, name="mm_dya", trans_b=True)
    dy_b = _mm(dzb, wts["w_b"], out_dtype=F32, name="mm_dyb", trans_b=True)
    dh, dws, dbs, dgv_n, dbv_n = _sgu_bwd(h, dy_a, dh, wm_b, wm_t, bs_t, small["sgu_norm_g"], small["sgu_norm_b"])
    big = dict(w_a=g_a, w_b=g_b, w_o=g_o, w_g=g_g, w_v=g_v, w_down=g_down, w_pp=g_pp, w_pg=g_pg)
    sm = dict(sgu_w_s=jnp.where(mask[None], dws, 0.0), sgu_b_s=dbs[:, :GROUPS].T, sgu_norm_g=dgv_n, sgu_norm_b=dbv_n,
              ln1_g=dg1, ln1_b=db1, conv_w=jnp.concatenate([dcw0, dcw1, dcw2], axis=0), conv_b=dcb, ln2_g=dg2, ln2_b=db2,
              loss=loss)
    dh, dlogits, dgn, got = _hgrn_bwd(h, o_all, dy_b, states, dh, small["lb_logits"], small["hgrn_norm_g"],
                                      ex=comm.grads_exchange(big, sm))
    comm.grads_done(got)
    g_in = _mm_tn(x_b, dh, out_dtype=BF16, name="mm_g_in")
    ex = comm.last_exchange(g_in, dict(lb_logits=dlogits, hgrn_norm_g=dgn))
    res = _mm(dh, w_in, out_dtype=F32, name="mm_dx", trans_b=True, reduce_b=True, adds=[(dz1, ALPHA)], ex=ex)
    grad_x, got = res if ex else (res, ())
    comm.last_done(got)
    return grad_x


_SMALL_EARLY = ["sgu_w_s", "sgu_b_s", "sgu_norm_g", "sgu_norm_b", "ln1_g", "ln1_b", "ffn_conv_b", "ln2_g", "ln2_b"]
_SMALL_LATE = ["hgrn_lb_logits", "hgrn_norm_g"]
N_TAPS = D_FF // N_DEV
UP_COLS = 2 * D_FF // N_DEV


class _StepExchanges:
    def __init__(self, w_in_shard, shards):
        self.w_in_shard = w_in_shard
        self.shards = shards

    def project_in(self, x_b):
        return _gather_project(x_b, self.w_in_shard)

    def weights_exchange(self, part):
        return _Exchange([(s, "gather+relay") for s in (self.shards[:2] if part == 0 else self.shards[2:])])

    def weights(self, got):
        d, f = D_MODEL, D_FF
        me = _my_index()
        got = [lax.dynamic_update_slice_in_dim(g, s[None], me, axis=0) for g, s in zip(got, self.shards)]
        w_br_g, w_o_g, w_up_g, w_down_g, w_pp_g, w_pg_g, conv_g = got
        w_br = w_br_g.transpose(1, 0, 2, 3).reshape(2, d, d)
        w_up = w_up_g.transpose(1, 0, 2).reshape(d, 2, f).transpose(1, 0, 2)
        wts = dict(w_a=w_br[0], w_b=w_br[1], w_o=w_o_g.reshape(d, d), w_g=w_up[0], w_v=w_up[1],
                   w_down=w_down_g.reshape(f, d), w_pp=w_pp_g.transpose(1, 0, 2).reshape(256, d), w_pg=w_pg_g.reshape(d, d))
        return wts, conv_g.transpose(1, 0, 2).reshape(3, f)

    def grads_exchange(self, big, sm):
        d = D_MODEL
        parts = [jnp.stack([big["w_a"], big["w_b"]]).reshape(2, N_DEV, 128, d).transpose(1, 0, 2, 3),
                 big["w_o"].reshape(N_DEV, 128, d),
                 jnp.concatenate([big["w_g"], big["w_v"]], axis=1).reshape(d, N_DEV, UP_COLS).transpose(1, 0, 2),
                 big["w_down"].reshape(N_DEV, N_TAPS, d),
                 big["w_pp"].reshape(256, N_DEV, 128).transpose(1, 0, 2),
                 big["w_pg"].reshape(N_DEV, 128, d)]
        packed, self.rows_early = _pack([sm[k] for k in ("sgu_w_s", "sgu_b_s", "sgu_norm_g", "sgu_norm_b", "ln1_g", "ln1_b",
                                                         "conv_b", "ln2_g", "ln2_b", "conv_w", "loss")])
        return _Exchange([(a, "scatter") for a in parts] + [(packed, "gather")])

    def grads_done(self, got):
        self.recv, self.small_early = got[:6], got[6]

    def last_exchange(self, g_in, sm):
        packed, self.rows_late = _pack([sm["lb_logits"], sm["hgrn_norm_g"]])
        return _Exchange([(_pair_sums(g_in), "scatter+pairs"), (packed, "gather")])

    def last_done(self, got):
        self.recv_in, self.small_late = got


def _rows128(a):
    flat = a.reshape(-1)
    rows = -(-flat.shape[0] // (8 * LANE)) * 8
    return jnp.pad(flat, (0, rows * LANE - flat.shape[0])).reshape(rows, LANE)


def _pack(parts):
    blocks = [_rows128(a) for a in parts]
    return jnp.concatenate(blocks, axis=0), [b.shape[0] for b in blocks]


def _unpack(packed, shapes, rows):
    out, r0 = [], 0
    for shp, r in zip(shapes, rows):
        n = math.prod(shp)
        out.append(packed[r0:r0 + r].reshape(-1)[:n].reshape(shp))
        r0 += r
    return out


def kernel(x, p, w_in, sgu_w_s, sgu_b_s, sgu_norm_g, sgu_norm_b, hgrn_lb_logits, hgrn_norm_g, w_branch, w_out, ln1_g, ln1_b, ffn_w_up, ffn_conv_w, ffn_conv_b, ffn_w_down, ln2_g, ln2_b, ple_w_proj, ple_w_gate, loss_target, m_w_in, m_sgu_w_s, m_sgu_b_s, m_sgu_norm_g, m_sgu_norm_b, m_hgrn_lb_logits, m_hgrn_norm_g, m_w_branch, m_w_out, m_ln1_g, m_ln1_b, m_ffn_w_up, m_ffn_conv_w, m_ffn_conv_b, m_ffn_w_down, m_ln2_g, m_ln2_b, m_ple_w_proj, m_ple_w_gate, v_w_in, v_sgu_w_s, v_sgu_b_s, v_sgu_norm_g, v_sgu_norm_b, v_hgrn_lb_logits, v_hgrn_norm_g, v_w_branch, v_w_out, v_ln1_g, v_ln1_b, v_ffn_w_up, v_ffn_conv_w, v_ffn_conv_b, v_ffn_w_down, v_ln2_g, v_ln2_b, v_ple_w_proj, v_ple_w_gate):
    weights = dict(w_in=w_in, sgu_w_s=sgu_w_s, sgu_b_s=sgu_b_s, sgu_norm_g=sgu_norm_g, sgu_norm_b=sgu_norm_b,
                   hgrn_lb_logits=hgrn_lb_logits, hgrn_norm_g=hgrn_norm_g, w_branch=w_branch, w_out=w_out,
                   ln1_g=ln1_g, ln1_b=ln1_b, ffn_w_up=ffn_w_up, ffn_conv_w=ffn_conv_w, ffn_conv_b=ffn_conv_b,
                   ffn_w_down=ffn_w_down, ln2_g=ln2_g, ln2_b=ln2_b, ple_w_proj=ple_w_proj, ple_w_gate=ple_w_gate)
    mom_m = dict(w_in=m_w_in, sgu_w_s=m_sgu_w_s, sgu_b_s=m_sgu_b_s, sgu_norm_g=m_sgu_norm_g, sgu_norm_b=m_sgu_norm_b,
                 hgrn_lb_logits=m_hgrn_lb_logits, hgrn_norm_g=m_hgrn_norm_g, w_branch=m_w_branch, w_out=m_w_out,
                 ln1_g=m_ln1_g, ln1_b=m_ln1_b, ffn_w_up=m_ffn_w_up, ffn_conv_w=m_ffn_conv_w, ffn_conv_b=m_ffn_conv_b,
                 ffn_w_down=m_ffn_w_down, ln2_g=m_ln2_g, ln2_b=m_ln2_b, ple_w_proj=m_ple_w_proj, ple_w_gate=m_ple_w_gate)
    mom_v = dict(w_in=v_w_in, sgu_w_s=v_sgu_w_s, sgu_b_s=v_sgu_b_s, sgu_norm_g=v_sgu_norm_g, sgu_norm_b=v_sgu_norm_b,
                 hgrn_lb_logits=v_hgrn_lb_logits, hgrn_norm_g=v_hgrn_norm_g, w_branch=v_w_branch, w_out=v_w_out,
                 ln1_g=v_ln1_g, ln1_b=v_ln1_b, ffn_w_up=v_ffn_w_up, ffn_conv_w=v_ffn_conv_w, ffn_conv_b=v_ffn_conv_b,
                 ffn_w_down=v_ffn_w_down, ln2_g=v_ln2_g, ln2_b=v_ln2_b, ple_w_proj=v_ple_w_proj, ple_w_gate=v_ple_w_gate)
    d, f = D_MODEL, D_FF
    me = _my_index()

    comm = _StepExchanges(w_in[0].astype(BF16),
                          [w_branch[0].astype(BF16), w_out[0].astype(BF16), ffn_w_up[0].astype(BF16),
                           ffn_w_down[0].astype(BF16), ple_w_proj[0].astype(BF16), ple_w_gate[0].astype(BF16), ffn_conv_w[0]])
    small = dict(sgu_w_s=sgu_w_s[0], sgu_b_s=sgu_b_s[0], sgu_norm_g=sgu_norm_g, sgu_norm_b=sgu_norm_b,
                 lb_logits=hgrn_lb_logits, hgrn_norm_g=hgrn_norm_g, ln1_g=ln1_g, ln1_b=ln1_b, ln2_g=ln2_g, ln2_b=ln2_b,
                 conv_b=ffn_conv_b)
    grad_x = _local_step(x[0], p[0, 0], loss_target[0], small, comm)

    out = {}

    def adam(name, parts8, shape2d):
        w2, m2, v2 = (a.reshape(shape2d) for a in (weights[name], mom_m[name], mom_v[name]))
        res = _adam_call(w2, m2, v2, parts8.reshape(parts8.shape[:1] + shape2d), "adam_" + name)
        out[name] = tuple(r.reshape(weights[name].shape) for r in res)

    adam("w_in", comm.recv_in, (d, d))
    adam("w_branch", comm.recv[0], (256, d))
    adam("w_out", comm.recv[1], (128, d))
    adam("ffn_w_up", comm.recv[2], (d, UP_COLS))
    adam("ffn_w_down", comm.recv[3], (N_TAPS, d))
    adam("ple_w_proj", comm.recv[4], (256, 128))
    adam("ple_w_gate", comm.recv[5], (128, d))

    def adam_small(names, extra_w, extra_m, extra_v, extra_shapes, parts8, rows, label):
        pk = lambda src, extra: _pack([src[n] for n in names] + extra)[0]
        res = _adam_call(pk(weights, extra_w), pk(mom_m, extra_m), pk(mom_v, extra_v), parts8, label)
        shapes = [weights[n].shape for n in names] + extra_shapes
        unpacked = [_unpack(r, shapes, rows) for r in res]
        for j, n in enumerate(names):
            out[n] = tuple(u[j] for u in unpacked)
        return [[u[len(names) + j] for u in unpacked] for j in range(len(extra_shapes))]

    blank = jnp.zeros((8, LANE), F32)
    taps, loss_rows = adam_small(
        _SMALL_EARLY, [_place_taps(ffn_conv_w[0], me, f), blank], [_place_taps(m_ffn_conv_w[0], me, f), blank],
        [_place_taps(v_ffn_conv_w[0], me, f), blank + 1.0], [(3, f), (8, LANE)], comm.small_early, comm.rows_early,
        "adam_small_early")
    adam_small(_SMALL_LATE, [], [], [], [], comm.small_late, comm.rows_late, "adam_small_late")
    out["ffn_conv_w"] = tuple(lax.dynamic_slice_in_dim(u, me * N_TAPS, N_TAPS, axis=1)[None] for u in taps)
    loss = loss_rows[0][0, 0]

    order = ["w_in", "sgu_w_s", "sgu_b_s", "sgu_norm_g", "sgu_norm_b", "hgrn_lb_logits", "hgrn_norm_g", "w_branch", "w_out",
             "ln1_g", "ln1_b", "ffn_w_up", "ffn_conv_w", "ffn_conv_b", "ffn_w_down", "ln2_g", "ln2_b", "ple_w_proj", "ple_w_gate"]
    return (loss, grad_x[None], *[out[n][0] for n in order], *[out[n][1] for n in order],
            *[out[n][2] for n in order], *[out[n][3] for n in order])


def _place_taps(shard, me, f):
    return lax.dynamic_update_slice_in_dim(jnp.zeros((3, f), F32), shard, me * N_TAPS, axis=1)
```

```python
import functools
import math

import jax
import jax.numpy as jnp
from jax import lax
from jax.experimental import pallas as pl
from jax.experimental.pallas import tpu as pltpu

F32 = jnp.float32
BF16 = jnp.bfloat16

N_DEV = 8
N_CHIP = 4
D_MODEL = 1024
CHUNK = 64
SUB = 16
SGU_BLOCK = 128
GROUPS = 8
HEAD = 128
HEADS = 8
HEAD_UNROLL = 4
D_FF = 2816
LN_EPS = 1e-5
RMS_EPS = 1e-6
ALPHA = 2.0 ** 0.25
GELU_K = math.sqrt(2.0 / math.pi)
GELU_C = 0.044715
NEG = -1e30
ADAM_LR, ADAM_B1, ADAM_B2, ADAM_EPS, ADAM_WD, ADAM_STEP = 0.001, 0.9, 0.999, 1e-08, 0.01, 10
LANE = 128
SLOT_Q, SLOT_F, SLOT_I, SLOT_OG, SLOT_U, SLOT_V, SLOT_GA, SLOT_GB = range(8)


def _slot_of_group(k):
    return jnp.where(k < 2, k + 4, jnp.where(k < 6, k - 2, k))


MIB = 1024 * 1024
VMEM_V7X = 64 * MIB
VMEM_FLOOR = 32 * MIB
MM_TILES = (1024, 1408, 512, 256, 128)


def _params(sem, need=0):
    limit = min(max(need + need // 4, VMEM_FLOOR), VMEM_V7X - 4 * MIB)
    return pltpu.CompilerParams(dimension_semantics=sem, vmem_limit_bytes=limit)


def _pick(n, prefs):
    for t in prefs:
        if n % t == 0:
            return t
    return n


def _gelu(x):
    return 0.5 * x * (1.0 + jnp.tanh(GELU_K * (x + GELU_C * x * x * x)))


def _gelu_and_grad(x):
    x2 = x * x
    t = jnp.tanh(GELU_K * x * (1.0 + GELU_C * x2))
    half = 0.5 * (1.0 + t)
    return x * half, half + 0.5 * x * (1.0 - t * t) * GELU_K * (1.0 + 3.0 * GELU_C * x2)


def _silu_grad(x, s):
    return s * (1.0 + x * (1.0 - s))


def _dot(a, b):
    return jnp.dot(a.astype(BF16), b.astype(BF16), preferred_element_type=F32)


def _dot_nt(a, b):
    return lax.dot_general(a.astype(BF16), b.astype(BF16), (((1,), (1,)), ((), ())), preferred_element_type=F32)


def _dot_tn(a, b):
    return lax.dot_general(a.astype(BF16), b.astype(BF16), (((0,), (0,)), ((), ())), preferred_element_type=F32)


def _mean(x):
    return jnp.mean(x, axis=-1, keepdims=True)


def _sum0(x):
    return jnp.sum(x, axis=0, keepdims=True)


def _mm(a, b, *, out_dtype, name, trans_b=False, reduce_b=False, adds=(), ex=None):
    squeeze = b.ndim == 2
    a3 = a if a.ndim == 3 else a[None]
    b3 = b if b.ndim == 3 else b[None]
    ba, m, k = a3.shape
    bb = b3.shape[0]
    n = b3.shape[1] if trans_b else b3.shape[2]
    tm = _pick(m, MM_TILES)
    tn = _pick(n, MM_TILES)
    if reduce_b:
        bo, steps = 1, bb
        a_map = lambda o, i, j, r: (r if ba > 1 else 0, i, 0)
        b_map = (lambda o, i, j, r: (r, j, 0)) if trans_b else (lambda o, i, j, r: (r, 0, j))
    else:
        bo, steps = bb, 1
        a_map = lambda o, i, j, r: (o if ba > 1 else 0, i, 0)
        b_map = (lambda o, i, j, r: (o, j, 0)) if trans_b else (lambda o, i, j, r: (o, 0, j))
    o_map = lambda o, i, j, r: (o, i, j)
    add_arrays = [x if x.ndim == 3 else x[None] for x, _ in adds]
    add_scales = [s for _, s in adds]
    n_add = len(adds)
    dot = _dot_nt if trans_b else _dot

    def finish(acc, add_refs, o_ref):
        for ref, s in zip(add_refs, add_scales):
            acc = acc + s * ref[...].astype(F32)
        o_ref[...] = acc.astype(o_ref.dtype)

    grid = (bo, m // tm, n // tn, steps)

    def body(*refs):
        ins, (o_ref,), scratch, xrefs = _split_refs(refs, 2 + n_add, 1, 1 if reduce_b else 0, ex)
        a_ref, b_ref, add_refs = ins[0], ins[1], ins[2:]
        step = ((pl.program_id(0) * grid[1] + pl.program_id(1)) * grid[2] + pl.program_id(2)) * grid[3] + pl.program_id(3)
        if ex:
            @pl.when(step == 0)
            def _():
                ex.start(*xrefs)

        if reduce_b:
            acc, = scratch
            r = pl.program_id(3)

            @pl.when(r == 0)
            def _():
                acc[...] = jnp.zeros_like(acc)

            acc[...] += dot(a_ref[...], b_ref[...])

            @pl.when(r == steps - 1)
            def _():
                finish(acc[...], add_refs, o_ref)
        else:
            finish(dot(a_ref[...], b_ref[...]), add_refs, o_ref)

        if ex:
            @pl.when(step == math.prod(grid) - 1)
            def _():
                ex.finish(*xrefs)

    b_block = (None, tn, k) if trans_b else (None, k, tn)
    out_bytes = tm * tn * jnp.dtype(out_dtype).itemsize
    need = 2 * (tm * k * a3.dtype.itemsize + k * tn * b3.dtype.itemsize + out_bytes + n_add * tm * tn * 4)
    need += 2 * tm * tn * 4
    sem = ("arbitrary",) * 4 if ex else ("parallel", "parallel", "parallel", "arbitrary")
    res = pl.pallas_call(
        body,
        grid=grid,
        in_specs=[pl.BlockSpec((None, tm, k), a_map), pl.BlockSpec(b_block, b_map)]
        + [pl.BlockSpec((None, tm, tn), o_map) for _ in adds] + (ex.in_specs if ex else []),
        out_specs=[pl.BlockSpec((None, tm, tn), o_map)] + (ex.out_specs if ex else []),
        out_shape=[jax.ShapeDtypeStruct((bo, m, n), out_dtype)] + (ex.out_shape if ex else []),
        scratch_shapes=([pltpu.VMEM((tm, tn), F32)] if reduce_b else []) + (ex.scratch if ex else []),
        compiler_params=_params(sem, need),
        name=name,
    )(a3, b3, *add_arrays, *(ex.arrays if ex else []))
    out = res[0][0] if (reduce_b or squeeze) else res[0]
    return (out, res[1:]) if ex else out


def _mm_tn(a, b, *, out_dtype, name):
    squeeze = b.ndim == 2
    b3 = b if b.ndim == 3 else b[None]
    t, m = a.shape
    bb, _, n = b3.shape
    tm = _pick(m, MM_TILES)
    tn = _pick(n, MM_TILES)
    tt = _pick(t, (1024, 512, 256, 128))
    steps = t // tt
    need = 2 * (tt * tm * a.dtype.itemsize + tt * tn * b3.dtype.itemsize + tm * tn * jnp.dtype(out_dtype).itemsize)
    need += 2 * tm * tn * 4

    def body(a_ref, b_ref, o_ref, acc):
        r = pl.program_id(3)

        @pl.when(r == 0)
        def _():
            acc[...] = jnp.zeros_like(acc)

        acc[...] += _dot_tn(a_ref[...], b_ref[...])

        @pl.when(r == steps - 1)
        def _():
            o_ref[...] = acc[...].astype(o_ref.dtype)

    out = pl.pallas_call(
        body,
        grid=(bb, m // tm, n // tn, steps),
        in_specs=[pl.BlockSpec((tt, tm), lambda o, i, j, r: (r, i)),
                  pl.BlockSpec((None, tt, tn), lambda o, i, j, r: (o, r, j))],
        out_specs=pl.BlockSpec((None, tm, tn), lambda o, i, j, r: (o, i, j)),
        out_shape=jax.ShapeDtypeStruct((bb, m, n), out_dtype),
        scratch_shapes=[pltpu.VMEM((tm, tn), F32)],
        compiler_params=_params(("parallel", "parallel", "parallel", "arbitrary"), need),
        name=name,
    )(a, b3)
    return out[0] if squeeze else out


def _rowwise(fn, rows, consts, row_outs, acc_outs, *, tile, name):
    first = rows[0][0] if isinstance(rows[0], tuple) else rows[0]
    t = first.shape[-2]
    steps = t // tile
    arrays, in_specs = [], []
    for r in rows:
        if isinstance(r, tuple) and isinstance(r[1], pl.BlockSpec):
            arrays.append(r[0])
            in_specs.append(r[1])
        elif isinstance(r, tuple):
            arr, bidx = r
            arrays.append(arr)
            in_specs.append(pl.BlockSpec((None, tile, arr.shape[-1]), functools.partial(lambda i, b: (b, i, 0), b=bidx)))
        else:
            arrays.append(r)
            in_specs.append(pl.BlockSpec((tile, r.shape[-1]), lambda i: (i, 0)))
    for c in consts:
        arrays.append(c)
        in_specs.append(pl.BlockSpec(c.shape, lambda i: (0, 0)))
    n_in, n_row = len(arrays), len(row_outs)
    out_shape, out_specs = [], []
    for ro in row_outs:
        if ro[0] == "stack":
            _, cnt, blk, total, w, dt = ro
            out_shape.append(jax.ShapeDtypeStruct((total, t, w), dt))
            out_specs.append(pl.BlockSpec((cnt, tile, w), functools.partial(lambda i, b: (b, i, 0), b=blk)))
        else:
            w, dt = ro
            out_shape.append(jax.ShapeDtypeStruct((t, w), dt))
            out_specs.append(pl.BlockSpec((tile, w), lambda i: (i, 0)))
    out_shape += [jax.ShapeDtypeStruct(s, F32) for s in acc_outs]
    out_specs += [pl.BlockSpec(s, lambda i: (0, 0)) for s in acc_outs]
    blocks = [math.prod(d for d in sp.block_shape if d) * arr.dtype.itemsize for sp, arr in zip(in_specs, arrays)]
    blocks += [math.prod(d for d in sp.block_shape if d) * jnp.dtype(sh.dtype).itemsize
               for sp, sh in zip(out_specs, out_shape)]
    need = 2 * sum(blocks) + 6 * tile * max(a.shape[-1] for a in arrays) * 4

    def body(*refs):
        ins, outs = refs[:n_in], refs[n_in:]
        i = pl.program_id(0)
        res = fn(i, steps, *[r[...] for r in ins])
        res = res if isinstance(res, (tuple, list)) else (res,)
        for ref, val in zip(outs[:n_row], res[:n_row]):
            ref[...] = val.astype(ref.dtype)
        if acc_outs:
            @pl.when(i == 0)
            def _():
                for ref in outs[n_row:]:
                    ref[...] = jnp.zeros_like(ref)

            for ref, val in zip(outs[n_row:], res[n_row:]):
                ref[...] += val

    return pl.pallas_call(
        body,
        grid=(steps,),
        in_specs=in_specs,
        out_specs=out_specs,
        out_shape=out_shape,
        compiler_params=_params(("arbitrary",), need),
        name=name,
    )(*arrays)


def _ln_stats(z):
    mu = _mean(z)
    zc = z - mu
    rstd = lax.rsqrt(_mean(zc * zc) + LN_EPS)
    return zc * rstd, rstd


def _ln_bwd(dy, xhat, rstd, g):
    dxh = dy * g
    return rstd * (dxh - _mean(dxh) - xhat * _mean(dxh * xhat))


def _ride(ex, xrefs, step, steps):
    if not ex:
        return
    for at, act in ((0, ex.start), (steps - 2, ex.relay), (steps - 1, ex.finish)):
        @pl.when(step == at)
        def _(act=act):
            act(*xrefs)


def _sgu_fwd(h, wm, bs_t, g_v, b_v, ex=None):
    t = h.shape[1]
    steps = t // SGU_BLOCK

    def body(*refs):
        (u_ref, v_ref, wm_ref, bs_ref, g_ref, b_ref), (y_ref,), _, xrefs = _split_refs(refs, 6, 1, 0, ex)
        xhat, _ = _ln_stats(_gelu(v_ref[...]))
        vn = (xhat * g_ref[...] + b_ref[...]).astype(BF16)
        gu = _gelu(u_ref[...])
        for g in range(GROUPS):
            sl = slice(g * HEAD, (g + 1) * HEAD)
            mixed = _dot(wm_ref[g], vn[:, sl]) + bs_ref[:, g:g + 1]
            y_ref[:, sl] = (gu[:, sl] * mixed).astype(BF16)
        _ride(ex, xrefs, pl.program_id(0), steps)

    blk = lambda b: pl.BlockSpec((None, SGU_BLOCK, D_MODEL), functools.partial(lambda i, b: (b, i, 0), b=b))
    whole = lambda s: pl.BlockSpec(s, lambda i: (0,) * len(s))
    res = pl.pallas_call(
        body,
        grid=(steps,),
        in_specs=[blk(SLOT_U), blk(SLOT_V), whole(wm.shape), whole(bs_t.shape), whole(g_v.shape), whole(b_v.shape)]
        + (ex.in_specs if ex else []),
        out_specs=[pl.BlockSpec((SGU_BLOCK, D_MODEL), lambda i: (i, 0))] + (ex.out_specs if ex else []),
        out_shape=[jax.ShapeDtypeStruct((t, D_MODEL), BF16)] + (ex.out_shape if ex else []),
        scratch_shapes=ex.scratch if ex else [],
        compiler_params=_params(("arbitrary",)),
        name="sgu_fwd",
    )(h, h, wm, bs_t, g_v, b_v, *(ex.arrays if ex else []))
    return res[0], res[1:]


def _sgu_bwd(h, dy, dh, wm, wm_t, bs_t, g_v, b_v):
    t = h.shape[1]

    def body(u_ref, v_ref, dy_ref, dh_in, wm_ref, wmt_ref, bs_ref, g_ref, b_ref,
             duv_ref, dw_ref, dbs_ref, dg_ref, db_ref, dvn_ref):
        del dh_in
        du_ref, dv_ref = duv_ref.at[0], duv_ref.at[1]
        i = pl.program_id(0)

        @pl.when(i == 0)
        def _():
            dw_ref[...] = jnp.zeros_like(dw_ref)
            dbs_ref[...] = jnp.zeros_like(dbs_ref)
            dg_ref[...] = jnp.zeros_like(dg_ref)
            db_ref[...] = jnp.zeros_like(db_ref)

        u = u_ref[...]
        v = v_ref[...]
        gv, gvp = _gelu_and_grad(v)
        xhat, rstd = _ln_stats(gv)
        vn = (xhat * g_ref[...] + b_ref[...]).astype(BF16)
        gu, gup = _gelu_and_grad(u)
        lane = lax.broadcasted_iota(jnp.int32, (SGU_BLOCK, LANE), 1)
        dbs = jnp.zeros((SGU_BLOCK, LANE), F32)
        for g in range(GROUPS):
            sl = slice(g * HEAD, (g + 1) * HEAD)
            vn_g = vn[:, sl]
            mixed = _dot(wm_ref[g], vn_g) + bs_ref[:, g:g + 1]
            dy_g = dy_ref[:, sl]
            du_ref[:, sl] = (dy_g * mixed * gup[:, sl]).astype(BF16)
            dmix = dy_g * gu[:, sl]
            dmb = dmix.astype(BF16)
            dvn_ref[:, sl] = _dot(wmt_ref[g], dmb)
            dw_ref[g] += _dot_nt(dmb, vn_g)
            dbs = dbs + jnp.where(lane == g, jnp.sum(dmix, axis=1, keepdims=True), 0.0)
        dbs_ref[...] += dbs
        dvn = dvn_ref[...]
        dg_ref[...] += _sum0(dvn * xhat)
        db_ref[...] += _sum0(dvn)
        dv_ref[...] = (_ln_bwd(dvn, xhat, rstd, g_ref[...]) * gvp).astype(BF16)

    blk = lambda b: pl.BlockSpec((None, SGU_BLOCK, D_MODEL), functools.partial(lambda i, b: (b, i, 0), b=b))
    row = pl.BlockSpec((SGU_BLOCK, D_MODEL), lambda i: (i, 0))
    whole = lambda s: pl.BlockSpec(s, lambda i: (0,) * len(s))
    vec = (1, D_MODEL)
    return pl.pallas_call(
        body,
        grid=(t // SGU_BLOCK,),
        in_specs=[blk(SLOT_U), blk(SLOT_V), row, pl.BlockSpec(memory_space=pl.ANY),
                  whole(wm.shape), whole(wm_t.shape), whole(bs_t.shape), whole(vec), whole(vec)],
        out_specs=[pl.BlockSpec((2, SGU_BLOCK, D_MODEL), lambda i: (SLOT_U // 2, i, 0)),
                   whole(wm.shape), whole((SGU_BLOCK, LANE)), whole(vec), whole(vec)],
        out_shape=[jax.ShapeDtypeStruct(dh.shape, BF16),
                   jax.ShapeDtypeStruct(wm.shape, F32), jax.ShapeDtypeStruct((SGU_BLOCK, LANE), F32),
                   jax.ShapeDtypeStruct(vec, F32), jax.ShapeDtypeStruct(vec, F32)],
        scratch_shapes=[pltpu.VMEM((SGU_BLOCK, D_MODEL), F32)],
        input_output_aliases={3: 0},
        compiler_params=_params(("arbitrary",)),
        name="sgu_bwd",
    )(h, h, dy, dh, wm, wm_t, bs_t, g_v, b_v)


def _split3(x):
    hi = x.astype(BF16)
    r1 = x - hi.astype(F32)
    mid = r1.astype(BF16)
    lo = (r1 - mid.astype(F32)).astype(BF16)
    return hi, mid, lo


def _tri_matmul(tri, x):
    hi, mid, lo = _split3(x)
    dot = lambda p: jnp.dot(tri, p, preferred_element_type=F32)
    return dot(hi) + dot(mid) + dot(lo)


def _lower_bound(logits):
    l0, l1 = logits[0:1, :], logits[1:2, :]
    mx = jnp.maximum(l0, l1)
    e0, e1 = jnp.exp(l0 - mx), jnp.exp(l1 - mx)
    return e0 / (e0 + e1)


def _hgrn_gates(q_raw, f_raw, lb):
    q = q_raw * jax.nn.sigmoid(q_raw)
    sig = jax.nn.sigmoid(f_raw)
    f = lb + (1.0 - lb) * sig
    row = lax.broadcasted_iota(jnp.int32, (CHUNK, CHUNK), 0)
    col = lax.broadcasted_iota(jnp.int32, (CHUNK, CHUNK), 1)
    c = _tri_matmul((row >= col).astype(BF16), jnp.log(f))
    return q, sig, f, 1.0 - f, c


def _offdiag_terms(qh, kh, ch, tb):
    rows = slice(tb * SUB, (tb + 1) * SUB)
    r = ch[tb * SUB - 1:tb * SUB, :]
    eqh = jnp.exp(ch[rows] - r)
    ekh = jnp.exp(jnp.minimum(r - ch, 0.0))
    return rows, eqh, qh[rows] * eqh, ekh, kh * ekh


def _diag_decay(cb, s, trow):
    return jnp.exp(jnp.where(trow >= s, cb - cb[s:s + 1, :], NEG))


def _split_refs(refs, n_in, n_out, n_scratch, ex):
    nx = ex.n if ex else 0
    ins, refs = refs[:n_in], refs[n_in:]
    xsrc, refs = refs[:nx], refs[nx:]
    outs, refs = refs[:n_out], refs[n_out:]
    xout, refs = refs[:nx], refs[nx:]
    return ins, outs, refs[:n_scratch], (xsrc, xout, refs[n_scratch:])


def _hgrn_fwd(h, logits, g_norm, ex=None):
    t = h.shape[1]
    nc = t // CHUNK

    def body(*refs):
        ins, outs, scratch, xrefs = _split_refs(refs, 3, 3, 4, ex)
        hgrn_ref, lg_ref, gn_ref = ins
        q_ref, f_ref, i_ref, og_ref = (hgrn_ref.at[s] for s in (SLOT_Q, SLOT_F, SLOT_I, SLOT_OG))
        y_ref, o_ref, sall_ref = outs
        st_ref, q_s, k_s, c_s = scratch

        @pl.when(pl.program_id(0) == 0)
        def _():
            st_ref[...] = jnp.zeros_like(st_ref)

        lb = _lower_bound(lg_ref[...])
        q, _, _, k, c = _hgrn_gates(q_ref[...], f_ref[...], lb)
        q_s[...] = q
        k_s[...] = k
        c_s[...] = c
        col64 = lax.broadcasted_iota(jnp.int32, (SUB, CHUNK), 1)
        trow = lax.broadcasted_iota(jnp.int32, (SUB, HEAD), 0)

        def head(hd, carry):
            sl = pl.ds(pl.multiple_of(hd * HEAD, HEAD), HEAD)
            qh, kh, ch, ih = q_s[:, sl], k_s[:, sl], c_s[:, sl], i_ref[:, sl]
            st = st_ref[hd]
            sall_ref[hd] = st
            c_last = ch[CHUNK - 1:CHUNK, :]
            o = _dot_nt(qh * jnp.exp(ch), st)
            st_ref[hd] = st * jnp.exp(c_last) + _dot_tn(ih, kh * jnp.exp(c_last - ch))
            a_rows = [jnp.zeros((SUB, CHUNK), F32)]
            for tb in range(1, CHUNK // SUB):
                _, _, q_hat, _, k_hat = _offdiag_terms(qh, kh, ch, tb)
                a_rows.append(jnp.where(col64 < tb * SUB, _dot_nt(q_hat, k_hat), 0.0))
            o = o + _dot(jnp.concatenate(a_rows, axis=0), ih)
            o_rows = []
            for b in range(CHUNK // SUB):
                rows = slice(b * SUB, (b + 1) * SUB)
                qb, cb, kb, ib = qh[rows], ch[rows], kh[rows], ih[rows]
                ob = jnp.zeros((SUB, HEAD), F32)
                for s in range(SUB):
                    a = jnp.sum(qb * _diag_decay(cb, s, trow) * kb[s:s + 1, :], axis=1, keepdims=True)
                    ob = ob + a * ib[s:s + 1, :]
                o_rows.append(ob)
            o = o + jnp.concatenate(o_rows, axis=0)
            o_ref[:, sl] = o
            og = og_ref[:, sl]
            on = o * lax.rsqrt(_mean(o * o) + RMS_EPS)
            y_ref[:, sl] = (on * gn_ref[:, sl] * (og * jax.nn.sigmoid(og))).astype(BF16)
            return carry

        lax.fori_loop(0, HEADS, head, 0, unroll=HEAD_UNROLL)
        _ride(ex, xrefs, pl.program_id(0), nc)

    row = pl.BlockSpec((CHUNK, D_MODEL), lambda n: (n, 0))
    whole = lambda s: pl.BlockSpec(s, lambda n: (0,) * len(s))
    res = pl.pallas_call(
        body,
        grid=(nc,),
        in_specs=[pl.BlockSpec((4, CHUNK, D_MODEL), lambda n: (SLOT_Q // 4, n, 0)), whole(logits.shape), whole(g_norm.shape)]
        + (ex.in_specs if ex else []),
        out_specs=[row, row, pl.BlockSpec((None, HEADS, HEAD, HEAD), lambda n: (n, 0, 0, 0))] + (ex.out_specs if ex else []),
        out_shape=[jax.ShapeDtypeStruct((t, D_MODEL), BF16), jax.ShapeDtypeStruct((t, D_MODEL), F32),
                   jax.ShapeDtypeStruct((nc, HEADS, HEAD, HEAD), F32)] + (ex.out_shape if ex else []),
        scratch_shapes=[pltpu.VMEM((HEADS, HEAD, HEAD), F32)] + [pltpu.VMEM((CHUNK, D_MODEL), F32)] * 3
        + (ex.scratch if ex else []),
        compiler_params=_params(("arbitrary",)),
        name="hgrn_fwd",
    )(h, logits, g_norm, *(ex.arrays if ex else []))
    return res[0], res[1], res[2], res[3:]


def _hgrn_bwd(h, o_all, dy, states, dh, logits, g_norm, ex=None):
    t = h.shape[1]
    nc = t // CHUNK

    def body(*refs):
        ins, outs, scratch, xrefs = _split_refs(refs, 7, 3, 8, ex)
        hgrn_ref, o_ref, dy_ref, sall_ref, _, lg_ref, gn_ref = ins
        q_ref, f_ref, i_ref, og_ref = (hgrn_ref.at[s] for s in (SLOT_Q, SLOT_F, SLOT_I, SLOT_OG))
        dqfio_ref, dlg_ref, dgn_ref = outs
        dst_ref, dlb_ref, q_s, k_s, c_s, dq_s, dk_s, dc_s = scratch
        dq_ref, df_ref, di_ref, dog_ref = (dqfio_ref.at[s] for s in (SLOT_Q, SLOT_F, SLOT_I, SLOT_OG))
        n = pl.program_id(0)

        @pl.when(n == 0)
        def _():
            dst_ref[...] = jnp.zeros_like(dst_ref)
            dlb_ref[...] = jnp.zeros_like(dlb_ref)
            dgn_ref[...] = jnp.zeros_like(dgn_ref)
            if ex:
                ex.start(*xrefs)

        lb = _lower_bound(lg_ref[...])
        q_raw = q_ref[...]
        q, sig, f, k, c = _hgrn_gates(q_raw, f_ref[...], lb)
        q_s[...] = q
        k_s[...] = k
        c_s[...] = c
        col64 = lax.broadcasted_iota(jnp.int32, (SUB, CHUNK), 1)
        trow = lax.broadcasted_iota(jnp.int32, (SUB, HEAD), 0)
        row64 = lax.broadcasted_iota(jnp.int32, (CHUNK, HEAD), 0)

        def head(hd, carry):
            sl = pl.ds(pl.multiple_of(hd * HEAD, HEAD), HEAD)
            qh, kh, ch, ih = q_s[:, sl], k_s[:, sl], c_s[:, sl], i_ref[:, sl]
            st = sall_ref[hd]
            dst = dst_ref[hd]
            oh, dyh, og, gn = o_ref[:, sl], dy_ref[:, sl], og_ref[:, sl], gn_ref[:, sl]
            sg = jax.nn.sigmoid(og)
            sil = og * sg
            rms = lax.rsqrt(_mean(oh * oh) + RMS_EPS)
            on = oh * rms
            dog_ref[:, sl] = (dyh * on * gn * _silu_grad(og, sg)).astype(BF16)
            dgn_ref[:, sl] += _sum0(dyh * on * sil)
            don = dyh * gn * sil
            do = rms * (don - on * _mean(don * on))
            dob = do.astype(BF16)

            c_last = ch[CHUNK - 1:CHUNK, :]
            eq = jnp.exp(ch)
            q_til = qh * eq
            ekl = jnp.exp(c_last - ch)
            k_til = kh * ekl
            ecl = jnp.exp(c_last)
            dq_til = _dot(dob, st)
            dk_til = _dot(ih, dst)
            di = _dot_nt(k_til, dst)
            dc_last = _sum0(dk_til * k_til) + _sum0(dst * st) * ecl
            dst_ref[hd] = _dot_tn(dob, q_til) + dst * ecl
            dq = dq_til * eq
            dc = dq_til * q_til - dk_til * k_til
            dk = dk_til * ekl

            da_full = _dot_nt(dob, ih)
            a_rows = [jnp.zeros((SUB, CHUNK), F32)]
            dq_rows = [jnp.zeros((SUB, HEAD), F32)]
            dc_rows = [jnp.zeros((SUB, HEAD), F32)]
            for tb in range(1, CHUNK // SUB):
                rows, eqh, q_hat, ekh, k_hat = _offdiag_terms(qh, kh, ch, tb)
                keep = col64 < tb * SUB
                a_rows.append(jnp.where(keep, _dot_nt(q_hat, k_hat), 0.0))
                da = jnp.where(keep, da_full[rows], 0.0)
                dq_hat = _dot(da, k_hat)
                dk_hat = _dot_tn(da, q_hat)
                dq_rows.append(dq_hat * eqh)
                dc_rows.append(dq_hat * q_hat)
                dk = dk + dk_hat * ekh
                dc = dc - dk_hat * k_hat
            di = di + _dot_tn(jnp.concatenate(a_rows, axis=0), dob)

            dk_rows, di_rows = [], []
            for b in range(CHUNK // SUB):
                rows = slice(b * SUB, (b + 1) * SUB)
                qb, cb, kb, ib, dob_ = qh[rows], ch[rows], kh[rows], ih[rows], do[rows]
                dq_diag = jnp.zeros((SUB, HEAD), F32)
                dk_diag = jnp.zeros((SUB, HEAD), F32)
                di_diag = jnp.zeros((SUB, HEAD), F32)
                for s in range(SUB):
                    ks = kb[s:s + 1, :]
                    dec = _diag_decay(cb, s, trow)
                    a = jnp.sum(qb * dec * ks, axis=1, keepdims=True)
                    gk = jnp.sum(dob_ * ib[s:s + 1, :], axis=1, keepdims=True) * dec
                    dq_diag = dq_diag + gk * ks
                    dk_diag = dk_diag + jnp.where(trow == s, _sum0(gk * qb), 0.0)
                    di_diag = di_diag + jnp.where(trow == s, _sum0(a * dob_), 0.0)
                dq_rows[b] = dq_rows[b] + dq_diag
                dc_rows[b] = dc_rows[b] + qb * dq_diag - kb * dk_diag
                dk_rows.append(dk_diag)
                di_rows.append(di_diag)
            dq = dq + jnp.concatenate(dq_rows, axis=0)
            dk = dk + jnp.concatenate(dk_rows, axis=0)
            dc = dc + jnp.concatenate(dc_rows, axis=0) + jnp.where(row64 == CHUNK - 1, dc_last, 0.0)
            di_ref[:, sl] = (di + jnp.concatenate(di_rows, axis=0)).astype(BF16)
            dq_s[:, sl] = dq
            dk_s[:, sl] = dk
            dc_s[:, sl] = dc
            return carry

        lax.fori_loop(0, HEADS, head, 0, unroll=HEAD_UNROLL)

        row = lax.broadcasted_iota(jnp.int32, (CHUNK, CHUNK), 0)
        col = lax.broadcasted_iota(jnp.int32, (CHUNK, CHUNK), 1)
        dlf = _tri_matmul((row <= col).astype(BF16), dc_s[...])
        df = dlf / f - dk_s[...]
        dlb_ref[...] += _sum0(df * (1.0 - sig))
        df_ref[...] = (df * (1.0 - lb) * sig * (1.0 - sig)).astype(BF16)
        dq_ref[...] = (dq_s[...] * _silu_grad(q_raw, jax.nn.sigmoid(q_raw))).astype(BF16)

        @pl.when(n == nc - 1)
        def _():
            d0 = dlb_ref[...] * lb * (1.0 - lb)
            dlg_ref[0:1, :] = d0
            dlg_ref[1:2, :] = -d0
            if ex:
                ex.finish(*xrefs)

    rev = lambda n: nc - 1 - n
    slots = pl.BlockSpec((4, CHUNK, D_MODEL), lambda n: (SLOT_Q // 4, rev(n), 0))
    row = pl.BlockSpec((CHUNK, D_MODEL), lambda n: (rev(n), 0))
    whole = lambda s: pl.BlockSpec(s, lambda n: (0,) * len(s))
    vec = (1, D_MODEL)
    res = pl.pallas_call(
        body,
        grid=(nc,),
        in_specs=[slots, row, row, pl.BlockSpec((None, HEADS, HEAD, HEAD), lambda n: (rev(n), 0, 0, 0)),
                  pl.BlockSpec(memory_space=pl.ANY), whole(logits.shape), whole(vec)] + (ex.in_specs if ex else []),
        out_specs=[slots, whole((2, D_MODEL)), whole(vec)] + (ex.out_specs if ex else []),
        out_shape=[jax.ShapeDtypeStruct(dh.shape, BF16), jax.ShapeDtypeStruct((2, D_MODEL), F32),
                   jax.ShapeDtypeStruct(vec, F32)] + (ex.out_shape if ex else []),
        scratch_shapes=[pltpu.VMEM((HEADS, HEAD, HEAD), F32), pltpu.VMEM(vec, F32)]
        + [pltpu.VMEM((CHUNK, D_MODEL), F32)] * 6 + (ex.scratch if ex else []),
        input_output_aliases={4: 0},
        compiler_params=_params(("arbitrary",)),
        name="hgrn_bwd",
    )(h, o_all, dy, states, dh, logits, g_norm, *(ex.arrays if ex else []))
    return res[0], res[1], res[2], res[3:]


def _merge_fwd(i, n, ga, gb, za, zb):
    return jax.nn.sigmoid(ga) * za + jax.nn.sigmoid(gb) * zb


def _merge_bwd(i, n, ga, gb, za, zb, dm):
    sa, sb = jax.nn.sigmoid(ga), jax.nn.sigmoid(gb)
    dgates = jnp.stack([(dm * za * sa * (1.0 - sa)).astype(BF16), (dm * zb * sb * (1.0 - sb)).astype(BF16)])
    return dgates, dm * sa, dm * sb


def _ln1_fwd(i, n, x, r1, g, b):
    xhat, _ = _ln_stats(ALPHA * x + r1)
    x1 = xhat * g + b
    return x1, x1


def _ln1_bwd(i, n, x, r1, dx1, g):
    xhat, rstd = _ln_stats(ALPHA * x + r1)
    dz = _ln_bwd(dx1, xhat, rstd, g)
    return dz, dz, _sum0(dx1 * xhat), _sum0(dx1)


def _ln2_loss(i, n, x1, fo, pg, pp, tgt, g, b):
    sg = jax.nn.sigmoid(pg)
    xhat, rstd = _ln_stats(ALPHA * x1 + fo + sg * pp)
    diff = xhat * g + b - tgt
    loss = 0.5 * jnp.sum(_mean(diff * diff), axis=0, keepdims=True)
    dy = diff * (1.0 / D_MODEL)
    dz = _ln_bwd(dy, xhat, rstd, g)
    return (dz, dz, dz * pp * sg * (1.0 - sg), dz * sg,
            jnp.broadcast_to(loss, (8, LANE)), _sum0(dy * xhat), _sum0(dy))


def _shift_down(cur, halo, tile):
    row = lax.broadcasted_iota(jnp.int32, cur.shape, 0)
    m1 = jnp.where(row == 0, halo[7:8, :], pltpu.roll(cur, 1, 0))
    m2 = jnp.where(row == 0, halo[6:7, :], jnp.where(row == 1, halo[7:8, :], pltpu.roll(cur, 2, 0)))
    return m1, m2


def _shift_up(cur, halo, tile):
    row = lax.broadcasted_iota(jnp.int32, cur.shape, 0)
    p1 = jnp.where(row == tile - 1, halo[0:1, :], pltpu.roll(cur, tile - 1, 0))
    p2 = jnp.where(row == tile - 2, halo[0:1, :], jnp.where(row == tile - 1, halo[1:2, :], pltpu.roll(cur, tile - 2, 0)))
    return p1, p2


def _conv_pre(i, gate, halo, w, b, tile):
    halo = jnp.where(i == 0, 0.0, halo)
    m1, m2 = _shift_down(gate, halo, tile)
    return w[0:1, :] * m2 + w[1:2, :] * m1 + w[2:3, :] * gate + b, m1, m2


def _conv_fwd(tile, i, n, gate, halo, val, w, b):
    cg, _, _ = _conv_pre(i, gate, halo, w, b, tile)
    return _gelu(cg) * val


def _conv_bwd_a(tile, i, n, gate, halo, val, dhid, w, b):
    cg, m1, m2 = _conv_pre(i, gate, halo, w, b, tile)
    act, slope = _gelu_and_grad(cg)
    dcg = dhid * val * slope
    return dcg, dhid * act, _sum0(dcg * m2), _sum0(dcg * m1), _sum0(dcg * gate), _sum0(dcg)


def _conv_bwd_b(tile, i, n, dcg, halo, w):
    dcg = dcg.astype(F32)
    halo = jnp.where(i == n - 1, 0.0, halo.astype(F32))
    p1, p2 = _shift_up(dcg, halo, tile)
    return w[2:3, :] * dcg + w[1:2, :] * p1 + w[0:1, :] * p2


def _halo_spec(width, tile, t, nxt, rows=8):
    per = tile // rows
    last = t // rows - 1
    if nxt:
        return pl.BlockSpec((rows, width), lambda i: (jnp.minimum((i + 1) * per, last), 0))
    return pl.BlockSpec((rows, width), lambda i: (jnp.maximum(i * per - 1, 0), 0))


def _adamw(i, n, w, m, v, parts):
    g = parts[0].astype(F32)
    for j in range(1, parts.shape[0]):
        g = g + parts[j].astype(F32)
    m_new = ADAM_B1 * m + (1.0 - ADAM_B1) * g
    v_new = ADAM_B2 * v + (1.0 - ADAM_B2) * (g * g)
    m_hat = m_new / (1.0 - ADAM_B1 ** ADAM_STEP)
    v_hat = v_new / (1.0 - ADAM_B2 ** ADAM_STEP)
    delta = -ADAM_LR * (m_hat / (jnp.sqrt(v_hat) + ADAM_EPS) + ADAM_WD * w)
    return g, delta, m_new, v_new


def _adam_call(w, m, v, parts, name):
    r, c = w.shape
    tile = _pick(r, (256, 128)) if r > 256 else r
    spec = pl.BlockSpec((parts.shape[0], tile, c), lambda i: (0, i, 0))
    return _rowwise(_adamw, [w, m, v, (parts, spec)], [], [(c, F32)] * 4, [], tile=tile, name=name)


def _peer(k):
    x, y, c = lax.axis_index("x"), lax.axis_index("y"), lax.axis_index("c")
    px = x ^ ((k >> 2) & 1)
    py = y ^ ((k >> 1) & 1)
    pc = c ^ (k & 1)
    return (px, py, pc), 4 * px + 2 * py + pc


def _my_index():
    return 4 * lax.axis_index("x") + 2 * lax.axis_index("y") + lax.axis_index("c")


class _Exchange:
    KINDS = ("gather", "gather+relay", "scatter", "scatter+pairs")

    def __init__(self, entries):
        assert all(k in self.KINDS for _, k in entries), [k for _, k in entries]
        self.arrays = [a for a, _ in entries]
        self.scatter = [k.startswith("scatter") for _, k in entries]
        self.relayed = ["+relay" in k for _, k in entries]
        self.pairs = ["+pairs" in k for _, k in entries]
        self.n = len(entries)
        self.in_specs = [pl.BlockSpec(memory_space=pl.ANY)] * self.n
        self.out_specs = [pl.BlockSpec(memory_space=pl.ANY)] * self.n
        shapes = [tuple(a.shape[1:]) if sc else tuple(a.shape) for a, sc in zip(self.arrays, self.scatter)]
        counts = [N_CHIP if p else N_DEV for p in self.pairs]
        self.out_shape = [jax.ShapeDtypeStruct((n,) + s, a.dtype) for n, s, a in zip(counts, shapes, self.arrays)]
        per = N_DEV - 1
        self.scratch = [pltpu.SemaphoreType.DMA((self.n * per,)), pltpu.SemaphoreType.DMA((self.n * per,)),
                        pltpu.SemaphoreType.DMA((self.n,))]

    def _copies(self, srcs, outs, sems):
        send_sems, recv_sems, local_sems = sems
        x, y, c = lax.axis_index("x"), lax.axis_index("y"), lax.axis_index("c")
        me = _my_index()
        per = N_DEV - 1
        local, first, passed, relay_arrivals, arrivals = [], [], [], [], []
        for a in range(self.n):

            def copy(k, src, dst, dev, a=a):
                return pltpu.make_async_remote_copy(
                    src_ref=src, dst_ref=dst, send_sem=send_sems.at[a * per + k], recv_sem=recv_sems.at[a * per + k],
                    device_id=dev, device_id_type=pl.DeviceIdType.MESH)

            if self.pairs[a]:
                chip = 2 * x + y
                for k in range(N_CHIP):
                    to = chip ^ k
                    piece = srcs[a].at[_slot_of_group(2 * to + c) // 2]
                    if k == 0:
                        local.append(pltpu.make_async_copy(piece, outs[a].at[chip], local_sems.at[a]))
                    else:
                        dev = (to // 2, to % 2, c)
                        first.append(copy(k - 1, piece, outs[a].at[chip], dev))
                        arrivals.append(copy(k - 1, piece, outs[a].at[to], dev))
                continue
            mine = srcs[a].at[me] if self.scatter[a] else srcs[a]
            land = outs[a].at[me]
            local.append(pltpu.make_async_copy(mine, land, local_sems.at[a]))
            if self.relayed[a]:
                block = lambda px, py, pc, a=a: outs[a].at[4 * px + 2 * py + pc]
                chips = [(1 - x, y), (x, 1 - y), (1 - x, 1 - y)]
                first.append(copy(0, mine, land, (x, y, 1 - c)))
                arrivals.append(copy(0, mine, block(x, y, 1 - c), (x, y, 1 - c)))
                for j, (px, py) in enumerate(chips):
                    first.append(copy(1 + j, mine, land, (px, py, c)))
                    relay_arrivals.append(copy(1 + j, mine, block(px, py, c), (px, py, c)))
                    passed.append(copy(4 + j, block(px, py, c), block(px, py, c), (x, y, 1 - c)))
                    arrivals.append(copy(4 + j, mine, block(px, py, 1 - c), (x, y, 1 - c)))
                continue
            for k in range(1, N_DEV):
                dev, idx = _peer(k)
                if self.scatter[a]:
                    first.append(copy(k - 1, srcs[a].at[idx], land, dev))
                else:
                    first.append(copy(k - 1, mine, land, dev))
                arrivals.append(copy(k - 1, mine, outs[a].at[idx], dev))
        return local, first, passed, relay_arrivals, arrivals

    def start(self, srcs, outs, sems):
        local, first, _, _, _ = self._copies(srcs, outs, sems)
        for cp in local + first:
            cp.start()

    def relay(self, srcs, outs, sems):
        _, _, passed, relay_arrivals, _ = self._copies(srcs, outs, sems)
        for landed, onward in zip(relay_arrivals, passed):
            landed.wait_recv()
            onward.start()

    def finish(self, srcs, outs, sems):
        local, first, passed, _, arrivals = self._copies(srcs, outs, sems)
        for cp in arrivals:
            cp.wait_recv()
        for cp in first + passed:
            cp.wait_send()
        for cp in local:
            cp.wait()


def _gather_project(x_b, shard):
    t, d = x_b.shape
    tm = _pick(t, MM_TILES)
    nrow = t // tm
    per = N_DEV - 1

    def parties():
        x, y, c = lax.axis_index("x"), lax.axis_index("y"), lax.axis_index("c")
        chips = [(1 - x, y), (x, 1 - y), (1 - x, 1 - y)]
        return (x, y, c), (x, y, 1 - c), [(px, py, c) for px, py in chips], [(px, py, 1 - c) for px, py in chips]

    slot = lambda dev: _slot_of_group(4 * dev[0] + 2 * dev[1] + dev[2])
    ici_step, passed_step = (2, 3, 6), (4, 5, 7)
    me, sibling, over_ici, passed_on = parties()
    by_step = {0: me, 1: sibling, **dict(zip(ici_step, over_ici)), **dict(zip(passed_step, passed_on))}
    order = jnp.stack([slot(by_step[j]) for j in range(N_DEV)]).astype(jnp.int32)

    def body(order_ref, x_ref, shard_ref, h_ref, wall_ref, wbuf, fetch_sem, send_sems, recv_sems, local_sem):
        del order_ref
        me, sibling, over_ici, passed_on = parties()
        j, i = pl.program_id(0), pl.program_id(1)
        land = lambda dev: wall_ref.at[slot(dev)]

        def copy(k, src, block, to):
            return pltpu.make_async_remote_copy(src_ref=src, dst_ref=land(block), send_sem=send_sems.at[k],
                                                recv_sem=recv_sems.at[k], device_id=to, device_id_type=pl.DeviceIdType.MESH)

        def fetch(src):
            cp = pltpu.make_async_copy(src, wbuf, fetch_sem)
            cp.start()
            cp.wait()

        keep = pltpu.make_async_copy(shard_ref, land(me), local_sem)
        first = [copy(0, shard_ref, me, sibling)] + [copy(1 + n, shard_ref, me, dev) for n, dev in enumerate(over_ici)]
        onward = [copy(4 + n, land(dev), dev, sibling) for n, dev in enumerate(over_ici)]

        @pl.when(jnp.logical_and(i == 0, j == 0))
        def _():
            keep.start()
            for cp in first:
                cp.start()
            fetch(shard_ref)

        @pl.when(jnp.logical_and(i == 0, j == 1))
        def _():
            copy(0, shard_ref, sibling, me).wait_recv()
            fetch(land(sibling))

        for n, dev in enumerate(over_ici):
            @pl.when(jnp.logical_and(i == 0, j == ici_step[n]))
            def _(n=n, dev=dev):
                copy(1 + n, shard_ref, dev, me).wait_recv()
                onward[n].start()
                fetch(land(dev))

        for n, dev in enumerate(passed_on):
            @pl.when(jnp.logical_and(i == 0, j == passed_step[n]))
            def _(n=n, dev=dev):
                copy(4 + n, shard_ref, dev, me).wait_recv()
                fetch(land(dev))

        h_ref[...] = _dot(x_ref[...], wbuf[...])

        @pl.when(jnp.logical_and(i == nrow - 1, j == N_DEV - 1))
        def _():
            for cp in first + onward:
                cp.wait_send()
            keep.wait()

    need = 2 * (tm * d * 2 + tm * d * 4) + d * d * 2 + tm * d * 4
    h, w_all = pl.pallas_call(
        body,
        grid_spec=pltpu.PrefetchScalarGridSpec(
            num_scalar_prefetch=1,
            grid=(N_DEV, nrow),
            in_specs=[pl.BlockSpec((tm, d), lambda j, i, order: (i, 0)), pl.BlockSpec(memory_space=pl.ANY)],
            out_specs=[pl.BlockSpec((None, tm, d), lambda j, i, order: (order[j], i, 0)), pl.BlockSpec(memory_space=pl.ANY)],
            scratch_shapes=[pltpu.VMEM((d, d), BF16), pltpu.SemaphoreType.DMA, pltpu.SemaphoreType.DMA((per,)),
                            pltpu.SemaphoreType.DMA((per,)), pltpu.SemaphoreType.DMA],
        ),
        out_shape=[jax.ShapeDtypeStruct((N_DEV, t, d), F32), jax.ShapeDtypeStruct((N_DEV, d, d), BF16)],
        compiler_params=_params(("arbitrary", "arbitrary"), need),
        name="gather_project",
    )(order, x_b, shard)
    return h, w_all


def _pair_sums(g):
    n, r, cols = g.shape
    half = n // 2

    def swap(g_ref, got_ref, send_sems, recv_sems):
        x, y, c = lax.axis_index("x"), lax.axis_index("y"), lax.axis_index("c")
        copies = [pltpu.make_async_remote_copy(
            src_ref=g_ref.at[2 * j + 1 - c], dst_ref=got_ref.at[j], send_sem=send_sems.at[j], recv_sem=recv_sems.at[j],
            device_id=(x, y, 1 - c), device_id_type=pl.DeviceIdType.MESH) for j in range(half)]
        for cp in copies:
            cp.start()
        for cp in copies:
            cp.wait()

    got = pl.pallas_call(
        swap,
        in_specs=[pl.BlockSpec(memory_space=pl.ANY)],
        out_specs=pl.BlockSpec(memory_space=pl.ANY),
        out_shape=jax.ShapeDtypeStruct((half, r, cols), g.dtype),
        scratch_shapes=[pltpu.SemaphoreType.DMA((half,))] * 2,
        name="pair_swap",
    )(g)

    def add(mine_ref, got_ref, out_ref):
        mine = jnp.where(lax.axis_index("c") == 0, mine_ref[0].astype(F32), mine_ref[1].astype(F32))
        out_ref[...] = (mine + got_ref[...].astype(F32)).astype(out_ref.dtype)

    tile = _pick(r, (512, 256, 128))
    return pl.pallas_call(
        add,
        grid=(half, r // tile),
        in_specs=[pl.BlockSpec((None, 2, tile, cols), lambda j, i: (j, 0, i, 0)),
                  pl.BlockSpec((None, tile, cols), lambda j, i: (j, i, 0))],
        out_specs=pl.BlockSpec((None, tile, cols), lambda j, i: (j, i, 0)),
        out_shape=jax.ShapeDtypeStruct((half, r, cols), g.dtype),
        compiler_params=_params(("parallel", "parallel")),
        name="pair_add",
    )(g.reshape(half, 2, r, cols), got)


def _local_step(x, p, tgt, small, comm):
    t = x.shape[0]
    tile = _pick(t, (256, 128))
    d = D_MODEL
    act_b, act_f = (d, BF16), (d, F32)
    x_b, p_b = x.astype(BF16), p.astype(BF16)

    chunk_id = jnp.arange(SGU_BLOCK) // CHUNK
    mask = chunk_id[:, None] >= chunk_id[None, :]
    wm = jnp.where(mask[None], small["sgu_w_s"], 0.0)
    wm_b = wm.astype(BF16)
    wm_t = jnp.swapaxes(wm, 1, 2).astype(BF16)
    bs_t = small["sgu_b_s"].T

    h, w_in = comm.project_in(x_b)
    y_a, got_a = _sgu_fwd(h, wm_b, bs_t, small["sgu_norm_g"], small["sgu_norm_b"], ex=comm.weights_exchange(0))
    y_b, o_all, states, got_b = _hgrn_fwd(h, small["lb_logits"], small["hgrn_norm_g"], ex=comm.weights_exchange(1))
    wts, conv_w = comm.weights(list(got_a) + list(got_b))
    z_a = _mm(y_a, wts["w_a"], out_dtype=F32, name="mm_za")
    z_b = _mm(y_b, wts["w_b"], out_dtype=F32, name="mm_zb")
    gates = [(h, SLOT_GA), (h, SLOT_GB)]
    merged, = _rowwise(_merge_fwd, gates + [z_a, z_b], [], [act_b], [], tile=tile, name="merge_fwd")
    r1 = _mm(merged, wts["w_o"], out_dtype=F32, name="mm_r1")
    x1, x1_b = _rowwise(_ln1_fwd, [x, r1], [small["ln1_g"], small["ln1_b"]], [act_f, act_b], [], tile=tile, name="ln1_fwd")
    gate = _mm(x1_b, wts["w_g"], out_dtype=F32, name="mm_gate")
    val = _mm(x1_b, wts["w_v"], out_dtype=F32, name="mm_val")
    pg = _mm(x1_b, wts["w_pg"], out_dtype=F32, name="mm_pg")
    pp = _mm(p_b, wts["w_pp"], out_dtype=F32, name="mm_pp")
    hid, = _rowwise(functools.partial(_conv_fwd, tile), [gate, (gate, _halo_spec(D_FF, tile, t, False)), val],
                    [conv_w, small["conv_b"]], [(D_FF, BF16)], [], tile=tile, name="conv_fwd")
    fo = _mm(hid, wts["w_down"], out_dtype=F32, name="mm_down")
    dz2, dz2_b, dpg, dpp, loss, dg2, db2 = _rowwise(
        _ln2_loss, [x1, fo, pg, pp, tgt], [small["ln2_g"], small["ln2_b"]],
        [act_f, act_b, act_b, act_b], [(8, LANE), (1, d), (1, d)], tile=tile, name="ln2_loss")

    dhid = _mm(dz2_b, wts["w_down"], out_dtype=BF16, name="mm_dhid", trans_b=True)
    g_down = _mm_tn(hid, dz2_b, out_dtype=BF16, name="mm_g_down")
    dcg, dval, dcw0, dcw1, dcw2, dcb = _rowwise(
        functools.partial(_conv_bwd_a, tile), [gate, (gate, _halo_spec(D_FF, tile, t, False)), val, dhid],
        [conv_w, small["conv_b"]], [(D_FF, BF16), (D_FF, BF16)], [(1, D_FF)] * 4, tile=tile, name="conv_bwd_a")
    dgate, = _rowwise(functools.partial(_conv_bwd_b, tile), [dcg, (dcg, _halo_spec(D_FF, tile, t, True, rows=16))],
                      [conv_w], [(D_FF, BF16)], [], tile=tile, name="conv_bwd_b")
    g_g = _mm_tn(x1_b, dgate, out_dtype=BF16, name="mm_g_gate")
    g_v = _mm_tn(x1_b, dval, out_dtype=BF16, name="mm_g_val")
    g_pg = _mm_tn(x1_b, dpg, out_dtype=BF16, name="mm_g_pg")
    g_pp = _mm_tn(p_b, dpp, out_dtype=BF16, name="mm_g_pp")
    dx1 = _mm(dgate, wts["w_g"], out_dtype=F32, name="mm_dx1_gate", trans_b=True, adds=[(dz2, ALPHA)])
    dx1 = _mm(dval, wts["w_v"], out_dtype=F32, name="mm_dx1_val", trans_b=True, adds=[(dx1, 1.0)])
    dx1 = _mm(dpg, wts["w_pg"], out_dtype=F32, name="mm_dx1_pg", trans_b=True, adds=[(dx1, 1.0)])
    dz1, dz1_b, dg1, db1 = _rowwise(_ln1_bwd, [x, r1, dx1], [small["ln1_g"]], [act_f, act_b], [(1, d), (1, d)],
                                    tile=tile, name="ln1_bwd")
    g_o = _mm_tn(merged, dz1_b, out_dtype=BF16, name="mm_g_o")
    dm = _mm(dz1_b, wts["w_o"], out_dtype=BF16, name="mm_dm", trans_b=True)
    dh, dza, dzb = _rowwise(_merge_bwd, gates + [z_a, z_b, dm], [],
                            [("stack", 2, SLOT_GA // 2, 8, d, BF16), act_b, act_b], [], tile=tile, name="merge_bwd")
    g_a = _mm_tn(y_a, dza, out_dtype=BF16, name="mm_g_a")
    g_b = _mm_tn(y_b, dzb, out_dtype=BF16, name="mm_g_b")
    dy_a = _mm(dza, wts["w_a"], out_dtype=BF16, name="mm_dya", trans_b=True)
    dy_b = _mm(dzb, wts["w_b"], out_dtype=F32, name="mm_dyb", trans_b=True)
    dh, dws, dbs, dgv_n, dbv_n = _sgu_bwd(h, dy_a, dh, wm_b, wm_t, bs_t, small["sgu_norm_g"], small["sgu_norm_b"])
    big = dict(w_a=g_a, w_b=g_b, w_o=g_o, w_g=g_g, w_v=g_v, w_down=g_down, w_pp=g_pp, w_pg=g_pg)
    sm = dict(sgu_w_s=jnp.where(mask[None], dws, 0.0), sgu_b_s=dbs[:, :GROUPS].T, sgu_norm_g=dgv_n, sgu_norm_b=dbv_n,
              ln1_g=dg1, ln1_b=db1, conv_w=jnp.concatenate([dcw0, dcw1, dcw2], axis=0), conv_b=dcb, ln2_g=dg2, ln2_b=db2,
              loss=loss)
    dh, dlogits, dgn, got = _hgrn_bwd(h, o_all, dy_b, states, dh, small["lb_logits"], small["hgrn_norm_g"],
                                      ex=comm.grads_exchange(big, sm))
    comm.grads_done(got)
    g_in = _mm_tn(x_b, dh, out_dtype=BF16, name="mm_g_in")
    ex = comm.last_exchange(g_in, dict(lb_logits=dlogits, hgrn_norm_g=dgn))
    res = _mm(dh, w_in, out_dtype=F32, name="mm_dx", trans_b=True, reduce_b=True, adds=[(dz1, ALPHA)], ex=ex)
    grad_x, got = res if ex else (res, ())
    comm.last_done(got)
    return grad_x


_SMALL_EARLY = ["sgu_w_s", "sgu_b_s", "sgu_norm_g", "sgu_norm_b", "ln1_g", "ln1_b", "ffn_conv_b", "ln2_g", "ln2_b"]
_SMALL_LATE = ["hgrn_lb_logits", "hgrn_norm_g"]
N_TAPS = D_FF // N_DEV
UP_COLS = 2 * D_FF // N_DEV


class _StepExchanges:
    def __init__(self, w_in_shard, shards):
        self.w_in_shard = w_in_shard
        self.shards = shards

    def project_in(self, x_b):
        return _gather_project(x_b, self.w_in_shard)

    def weights_exchange(self, part):
        return _Exchange([(s, "gather+relay") for s in (self.shards[:2] if part == 0 else self.shards[2:])])

    def weights(self, got):
        d, f = D_MODEL, D_FF
        w_br_g, w_o_g, w_up_g, w_down_g, w_pp_g, w_pg_g, conv_g = got
        w_br = w_br_g.transpose(1, 0, 2, 3).reshape(2, d, d)
        w_up = w_up_g.transpose(1, 0, 2).reshape(d, 2, f).transpose(1, 0, 2)
        wts = dict(w_a=w_br[0], w_b=w_br[1], w_o=w_o_g.reshape(d, d), w_g=w_up[0], w_v=w_up[1],
                   w_down=w_down_g.reshape(f, d), w_pp=w_pp_g.transpose(1, 0, 2).reshape(256, d), w_pg=w_pg_g.reshape(d, d))
        return wts, conv_g.transpose(1, 0, 2).reshape(3, f)

    def grads_exchange(self, big, sm):
        d = D_MODEL
        parts = [jnp.stack([big["w_a"], big["w_b"]]).reshape(2, N_DEV, 128, d).transpose(1, 0, 2, 3),
                 big["w_o"].reshape(N_DEV, 128, d),
                 jnp.concatenate([big["w_g"], big["w_v"]], axis=1).reshape(d, N_DEV, UP_COLS).transpose(1, 0, 2),
                 big["w_down"].reshape(N_DEV, N_TAPS, d),
                 big["w_pp"].reshape(256, N_DEV, 128).transpose(1, 0, 2),
                 big["w_pg"].reshape(N_DEV, 128, d)]
        packed, self.rows_early = _pack([sm[k] for k in ("sgu_w_s", "sgu_b_s", "sgu_norm_g", "sgu_norm_b", "ln1_g", "ln1_b",
                                                         "conv_b", "ln2_g", "ln2_b", "conv_w", "loss")])
        return _Exchange([(a, "scatter") for a in parts] + [(packed, "gather")])

    def grads_done(self, got):
        self.recv, self.small_early = got[:6], got[6]

    def last_exchange(self, g_in, sm):
        packed, self.rows_late = _pack([sm["lb_logits"], sm["hgrn_norm_g"]])
        return _Exchange([(_pair_sums(g_in), "scatter+pairs"), (packed, "gather")])

    def last_done(self, got):
        self.recv_in, self.small_late = got


def _rows128(a):
    flat = a.reshape(-1)
    rows = -(-flat.shape[0] // (8 * LANE)) * 8
    return jnp.pad(flat, (0, rows * LANE - flat.shape[0])).reshape(rows, LANE)


def _pack(parts):
    blocks = [_rows128(a) for a in parts]
    return jnp.concatenate(blocks, axis=0), [b.shape[0] for b in blocks]


def _unpack(packed, shapes, rows):
    out, r0 = [], 0
    for shp, r in zip(shapes, rows):
        n = math.prod(shp)
        out.append(packed[r0:r0 + r].reshape(-1)[:n].reshape(shp))
        r0 += r
    return out


def kernel(x, p, w_in, sgu_w_s, sgu_b_s, sgu_norm_g, sgu_norm_b, hgrn_lb_logits, hgrn_norm_g, w_branch, w_out, ln1_g, ln1_b, ffn_w_up, ffn_conv_w, ffn_conv_b, ffn_w_down, ln2_g, ln2_b, ple_w_proj, ple_w_gate, loss_target, m_w_in, m_sgu_w_s, m_sgu_b_s, m_sgu_norm_g, m_sgu_norm_b, m_hgrn_lb_logits, m_hgrn_norm_g, m_w_branch, m_w_out, m_ln1_g, m_ln1_b, m_ffn_w_up, m_ffn_conv_w, m_ffn_conv_b, m_ffn_w_down, m_ln2_g, m_ln2_b, m_ple_w_proj, m_ple_w_gate, v_w_in, v_sgu_w_s, v_sgu_b_s, v_sgu_norm_g, v_sgu_norm_b, v_hgrn_lb_logits, v_hgrn_norm_g, v_w_branch, v_w_out, v_ln1_g, v_ln1_b, v_ffn_w_up, v_ffn_conv_w, v_ffn_conv_b, v_ffn_w_down, v_ln2_g, v_ln2_b, v_ple_w_proj, v_ple_w_gate):
    weights = dict(w_in=w_in, sgu_w_s=sgu_w_s, sgu_b_s=sgu_b_s, sgu_norm_g=sgu_norm_g, sgu_norm_b=sgu_norm_b,
                   hgrn_lb_logits=hgrn_lb_logits, hgrn_norm_g=hgrn_norm_g, w_branch=w_branch, w_out=w_out,
                   ln1_g=ln1_g, ln1_b=ln1_b, ffn_w_up=ffn_w_up, ffn_conv_w=ffn_conv_w, ffn_conv_b=ffn_conv_b,
                   ffn_w_down=ffn_w_down, ln2_g=ln2_g, ln2_b=ln2_b, ple_w_proj=ple_w_proj, ple_w_gate=ple_w_gate)
    mom_m = dict(w_in=m_w_in, sgu_w_s=m_sgu_w_s, sgu_b_s=m_sgu_b_s, sgu_norm_g=m_sgu_norm_g, sgu_norm_b=m_sgu_norm_b,
                 hgrn_lb_logits=m_hgrn_lb_logits, hgrn_norm_g=m_hgrn_norm_g, w_branch=m_w_branch, w_out=m_w_out,
                 ln1_g=m_ln1_g, ln1_b=m_ln1_b, ffn_w_up=m_ffn_w_up, ffn_conv_w=m_ffn_conv_w, ffn_conv_b=m_ffn_conv_b,
                 ffn_w_down=m_ffn_w_down, ln2_g=m_ln2_g, ln2_b=m_ln2_b, ple_w_proj=m_ple_w_proj, ple_w_gate=m_ple_w_gate)
    mom_v = dict(w_in=v_w_in, sgu_w_s=v_sgu_w_s, sgu_b_s=v_sgu_b_s, sgu_norm_g=v_sgu_norm_g, sgu_norm_b=v_sgu_norm_b,
                 hgrn_lb_logits=v_hgrn_lb_logits, hgrn_norm_g=v_hgrn_norm_g, w_branch=v_w_branch, w_out=v_w_out,
                 ln1_g=v_ln1_g, ln1_b=v_ln1_b, ffn_w_up=v_ffn_w_up, ffn_conv_w=v_ffn_conv_w, ffn_conv_b=v_ffn_conv_b,
                 ffn_w_down=v_ffn_w_down, ln2_g=v_ln2_g, ln2_b=v_ln2_b, ple_w_proj=v_ple_w_proj, ple_w_gate=v_ple_w_gate)
    d, f = D_MODEL, D_FF
    me = _my_index()

    comm = _StepExchanges(w_in[0].astype(BF16),
                          [w_branch[0].astype(BF16), w_out[0].astype(BF16), ffn_w_up[0].astype(BF16),
                           ffn_w_down[0].astype(BF16), ple_w_proj[0].astype(BF16), ple_w_gate[0].astype(BF16), ffn_conv_w[0]])
    small = dict(sgu_w_s=sgu_w_s[0], sgu_b_s=sgu_b_s[0], sgu_norm_g=sgu_norm_g, sgu_norm_b=sgu_norm_b,
                 lb_logits=hgrn_lb_logits, hgrn_norm_g=hgrn_norm_g, ln1_g=ln1_g, ln1_b=ln1_b, ln2_g=ln2_g, ln2_b=ln2_b,
                 conv_b=ffn_conv_b)
    grad_x = _local_step(x[0], p[0, 0], loss_target[0], small, comm)

    out = {}

    def adam(name, parts8, shape2d):
        w2, m2, v2 = (a.reshape(shape2d) for a in (weights[name], mom_m[name], mom_v[name]))
        res = _adam_call(w2, m2, v2, parts8.reshape(parts8.shape[:1] + shape2d), "adam_" + name)
        out[name] = tuple(r.reshape(weights[name].shape) for r in res)

    adam("w_in", comm.recv_in, (d, d))
    adam("w_branch", comm.recv[0], (256, d))
    adam("w_out", comm.recv[1], (128, d))
    adam("ffn_w_up", comm.recv[2], (d, UP_COLS))
    adam("ffn_w_down", comm.recv[3], (N_TAPS, d))
    adam("ple_w_proj", comm.recv[4], (256, 128))
    adam("ple_w_gate", comm.recv[5], (128, d))

    def adam_small(names, extra_w, extra_m, extra_v, extra_shapes, parts8, rows, label):
        pk = lambda src, extra: _pack([src[n] for n in names] + extra)[0]
        res = _adam_call(pk(weights, extra_w), pk(mom_m, extra_m), pk(mom_v, extra_v), parts8, label)
        shapes = [weights[n].shape for n in names] + extra_shapes
        unpacked = [_unpack(r, shapes, rows) for r in res]
        for j, n in enumerate(names):
            out[n] = tuple(u[j] for u in unpacked)
        return [[u[len(names) + j] for u in unpacked] for j in range(len(extra_shapes))]

    blank = jnp.zeros((8, LANE), F32)
    taps, loss_rows = adam_small(
        _SMALL_EARLY, [_place_taps(ffn_conv_w[0], me, f), blank], [_place_taps(m_ffn_conv_w[0], me, f), blank],
        [_place_taps(v_ffn_conv_w[0], me, f), blank + 1.0], [(3, f), (8, LANE)], comm.small_early, comm.rows_early,
        "adam_small_early")
    adam_small(_SMALL_LATE, [], [], [], [], comm.small_late, comm.rows_late, "adam_small_late")
    out["ffn_conv_w"] = tuple(lax.dynamic_slice_in_dim(u, me * N_TAPS, N_TAPS, axis=1)[None] for u in taps)
    loss = loss_rows[0][0, 0]

    order = ["w_in", "sgu_w_s", "sgu_b_s", "sgu_norm_g", "sgu_norm_b", "hgrn_lb_logits", "hgrn_norm_g", "w_branch", "w_out",
             "ln1_g", "ln1_b", "ffn_w_up", "ffn_conv_w", "ffn_conv_b", "ffn_w_down", "ln2_g", "ln2_b", "ple_w_proj", "ple_w_gate"]
    return (loss, grad_x[None], *[out[n][0] for n in order], *[out[n][1] for n in order],
            *[out[n][2] for n in order], *[out[n][3] for n in order])


def _place_taps(shard, me, f):
    return lax.dynamic_update_slice_in_dim(jnp.zeros((3, f), F32), shard, me * N_TAPS, axis=1)
```

```python
import functools
import math

import jax
import jax.numpy as jnp
from jax import lax
from jax.experimental import pallas as pl
from jax.experimental.pallas import tpu as pltpu

F32 = jnp.float32
BF16 = jnp.bfloat16

N_DEV = 8
N_CHIP = 4
D_MODEL = 1024
CHUNK = 64
SUB = 16
SGU_BLOCK = 128
GROUPS = 8
HEAD = 128
HEADS = 8
FWD_CHUNKS_PER_STEP = 2
HEAD_UNROLL = 4
D_FF = 2816
LN_EPS = 1e-5
RMS_EPS = 1e-6
ALPHA = 2.0 ** 0.25
GELU_K = math.sqrt(2.0 / math.pi)
GELU_C = 0.044715
NEG = -1e30
ADAM_LR, ADAM_B1, ADAM_B2, ADAM_EPS, ADAM_WD, ADAM_STEP = 0.001, 0.9, 0.999, 1e-08, 0.01, 10
LANE = 128
SLOT_Q, SLOT_F, SLOT_I, SLOT_OG, SLOT_U, SLOT_V, SLOT_GA, SLOT_GB = range(8)


def _slot_of_group(k):
    return jnp.where(k < 2, k + 4, jnp.where(k < 6, k - 2, k))


MIB = 1024 * 1024
VMEM_V7X = 64 * MIB
VMEM_FLOOR = 32 * MIB
MM_TILES = (1024, 1408, 512, 256, 128)


def _params(sem, need=0):
    limit = min(max(need + need // 4, VMEM_FLOOR), VMEM_V7X - 4 * MIB)
    return pltpu.CompilerParams(dimension_semantics=sem, vmem_limit_bytes=limit)


def _pick(n, prefs):
    for t in prefs:
        if n % t == 0:
            return t
    return n


def _gelu(x):
    return 0.5 * x * (1.0 + jnp.tanh(GELU_K * (x + GELU_C * x * x * x)))


def _gelu_and_grad(x):
    x2 = x * x
    t = jnp.tanh(GELU_K * x * (1.0 + GELU_C * x2))
    half = 0.5 * (1.0 + t)
    return x * half, half + 0.5 * x * (1.0 - t * t) * GELU_K * (1.0 + 3.0 * GELU_C * x2)


def _silu_grad(x, s):
    return s * (1.0 + x * (1.0 - s))


def _dot(a, b):
    return jnp.dot(a.astype(BF16), b.astype(BF16), preferred_element_type=F32)


def _dot_nt(a, b):
    return lax.dot_general(a.astype(BF16), b.astype(BF16), (((1,), (1,)), ((), ())), preferred_element_type=F32)


def _dot_tn(a, b):
    return lax.dot_general(a.astype(BF16), b.astype(BF16), (((0,), (0,)), ((), ())), preferred_element_type=F32)


def _mean(x):
    return jnp.mean(x, axis=-1, keepdims=True)


def _sum0(x):
    return jnp.sum(x, axis=0, keepdims=True)


def _mm(a, b, *, out_dtype, name, trans_b=False, reduce_b=False, adds=(), ex=None):
    squeeze = b.ndim == 2
    a3 = a if a.ndim == 3 else a[None]
    b3 = b if b.ndim == 3 else b[None]
    ba, m, k = a3.shape
    bb = b3.shape[0]
    n = b3.shape[1] if trans_b else b3.shape[2]
    tm = _pick(m, MM_TILES)
    tn = _pick(n, MM_TILES)
    if reduce_b:
        bo, steps = 1, bb
        a_map = lambda o, i, j, r: (r if ba > 1 else 0, i, 0)
        b_map = (lambda o, i, j, r: (r, j, 0)) if trans_b else (lambda o, i, j, r: (r, 0, j))
    else:
        bo, steps = bb, 1
        a_map = lambda o, i, j, r: (o if ba > 1 else 0, i, 0)
        b_map = (lambda o, i, j, r: (o, j, 0)) if trans_b else (lambda o, i, j, r: (o, 0, j))
    o_map = lambda o, i, j, r: (o, i, j)
    add_arrays = [x if x.ndim == 3 else x[None] for x, _ in adds]
    add_scales = [s for _, s in adds]
    n_add = len(adds)
    dot = _dot_nt if trans_b else _dot

    def finish(acc, add_refs, o_ref):
        for ref, s in zip(add_refs, add_scales):
            acc = acc + s * ref[...].astype(F32)
        o_ref[...] = acc.astype(o_ref.dtype)

    grid = (bo, m // tm, n // tn, steps)

    def body(*refs):
        ins, (o_ref,), scratch, xrefs = _split_refs(refs, 2 + n_add, 1, 1 if reduce_b else 0, ex)
        a_ref, b_ref, add_refs = ins[0], ins[1], ins[2:]
        step = ((pl.program_id(0) * grid[1] + pl.program_id(1)) * grid[2] + pl.program_id(2)) * grid[3] + pl.program_id(3)
        if ex:
            @pl.when(step == 0)
            def _():
                ex.start(*xrefs)

        if reduce_b:
            acc, = scratch
            r = pl.program_id(3)

            @pl.when(r == 0)
            def _():
                acc[...] = jnp.zeros_like(acc)

            acc[...] += dot(a_ref[...], b_ref[...])

            @pl.when(r == steps - 1)
            def _():
                finish(acc[...], add_refs, o_ref)
        else:
            finish(dot(a_ref[...], b_ref[...]), add_refs, o_ref)

        if ex:
            @pl.when(step == math.prod(grid) - 1)
            def _():
                ex.finish(*xrefs)

    b_block = (None, tn, k) if trans_b else (None, k, tn)
    out_bytes = tm * tn * jnp.dtype(out_dtype).itemsize
    need = 2 * (tm * k * a3.dtype.itemsize + k * tn * b3.dtype.itemsize + out_bytes + n_add * tm * tn * 4)
    need += 2 * tm * tn * 4
    sem = ("arbitrary",) * 4 if ex else ("parallel", "parallel", "parallel", "arbitrary")
    res = pl.pallas_call(
        body,
        grid=grid,
        in_specs=[pl.BlockSpec((None, tm, k), a_map), pl.BlockSpec(b_block, b_map)]
        + [pl.BlockSpec((None, tm, tn), o_map) for _ in adds] + (ex.in_specs if ex else []),
        out_specs=[pl.BlockSpec((None, tm, tn), o_map)] + (ex.out_specs if ex else []),
        out_shape=[jax.ShapeDtypeStruct((bo, m, n), out_dtype)] + (ex.out_shape if ex else []),
        scratch_shapes=([pltpu.VMEM((tm, tn), F32)] if reduce_b else []) + (ex.scratch if ex else []),
        compiler_params=_params(sem, need),
        name=name,
    )(a3, b3, *add_arrays, *(ex.arrays if ex else []))
    out = res[0][0] if (reduce_b or squeeze) else res[0]
    return (out, res[1:]) if ex else out


def _mm_tn(a, b, *, out_dtype, name):
    squeeze = b.ndim == 2
    b3 = b if b.ndim == 3 else b[None]
    t, m = a.shape
    bb, _, n = b3.shape
    tm = _pick(m, MM_TILES)
    tn = _pick(n, MM_TILES)
    tt = _pick(t, (1024, 512, 256, 128))
    steps = t // tt
    need = 2 * (tt * tm * a.dtype.itemsize + tt * tn * b3.dtype.itemsize + tm * tn * jnp.dtype(out_dtype).itemsize)
    need += 2 * tm * tn * 4

    def body(a_ref, b_ref, o_ref, acc):
        r = pl.program_id(3)

        @pl.when(r == 0)
        def _():
            acc[...] = jnp.zeros_like(acc)

        acc[...] += _dot_tn(a_ref[...], b_ref[...])

        @pl.when(r == steps - 1)
        def _():
            o_ref[...] = acc[...].astype(o_ref.dtype)

    out = pl.pallas_call(
        body,
        grid=(bb, m // tm, n // tn, steps),
        in_specs=[pl.BlockSpec((tt, tm), lambda o, i, j, r: (r, i)),
                  pl.BlockSpec((None, tt, tn), lambda o, i, j, r: (o, r, j))],
        out_specs=pl.BlockSpec((None, tm, tn), lambda o, i, j, r: (o, i, j)),
        out_shape=jax.ShapeDtypeStruct((bb, m, n), out_dtype),
        scratch_shapes=[pltpu.VMEM((tm, tn), F32)],
        compiler_params=_params(("parallel", "parallel", "parallel", "arbitrary"), need),
        name=name,
    )(a, b3)
    return out[0] if squeeze else out


def _rowwise(fn, rows, consts, row_outs, acc_outs, *, tile, name):
    first = rows[0][0] if isinstance(rows[0], tuple) else rows[0]
    t = first.shape[-2]
    steps = t // tile
    arrays, in_specs = [], []
    for r in rows:
        if isinstance(r, tuple) and isinstance(r[1], pl.BlockSpec):
            arrays.append(r[0])
            in_specs.append(r[1])
        elif isinstance(r, tuple):
            arr, bidx = r
            arrays.append(arr)
            in_specs.append(pl.BlockSpec((None, tile, arr.shape[-1]), functools.partial(lambda i, b: (b, i, 0), b=bidx)))
        else:
            arrays.append(r)
            in_specs.append(pl.BlockSpec((tile, r.shape[-1]), lambda i: (i, 0)))
    for c in consts:
        arrays.append(c)
        in_specs.append(pl.BlockSpec(c.shape, lambda i: (0, 0)))
    n_in, n_row = len(arrays), len(row_outs)
    out_shape, out_specs = [], []
    for ro in row_outs:
        if ro[0] == "stack":
            _, cnt, blk, total, w, dt = ro
            out_shape.append(jax.ShapeDtypeStruct((total, t, w), dt))
            out_specs.append(pl.BlockSpec((cnt, tile, w), functools.partial(lambda i, b: (b, i, 0), b=blk)))
        else:
            w, dt = ro
            out_shape.append(jax.ShapeDtypeStruct((t, w), dt))
            out_specs.append(pl.BlockSpec((tile, w), lambda i: (i, 0)))
    out_shape += [jax.ShapeDtypeStruct(s, F32) for s in acc_outs]
    out_specs += [pl.BlockSpec(s, lambda i: (0, 0)) for s in acc_outs]
    blocks = [math.prod(d for d in sp.block_shape if d) * arr.dtype.itemsize for sp, arr in zip(in_specs, arrays)]
    blocks += [math.prod(d for d in sp.block_shape if d) * jnp.dtype(sh.dtype).itemsize
               for sp, sh in zip(out_specs, out_shape)]
    need = 2 * sum(blocks) + 6 * tile * max(a.shape[-1] for a in arrays) * 4

    def body(*refs):
        ins, outs = refs[:n_in], refs[n_in:]
        i = pl.program_id(0)
        res = fn(i, steps, *[r[...] for r in ins])
        res = res if isinstance(res, (tuple, list)) else (res,)
        for ref, val in zip(outs[:n_row], res[:n_row]):
            ref[...] = val.astype(ref.dtype)
        if acc_outs:
            @pl.when(i == 0)
            def _():
                for ref in outs[n_row:]:
                    ref[...] = jnp.zeros_like(ref)

            for ref, val in zip(outs[n_row:], res[n_row:]):
                ref[...] += val

    return pl.pallas_call(
        body,
        grid=(steps,),
        in_specs=in_specs,
        out_specs=out_specs,
        out_shape=out_shape,
        compiler_params=_params(("arbitrary",), need),
        name=name,
    )(*arrays)


def _ln_stats(z):
    mu = _mean(z)
    zc = z - mu
    rstd = lax.rsqrt(_mean(zc * zc) + LN_EPS)
    return zc * rstd, rstd


def _ln_bwd(dy, xhat, rstd, g):
    dxh = dy * g
    return rstd * (dxh - _mean(dxh) - xhat * _mean(dxh * xhat))


def _ride(ex, xrefs, step, steps):
    if not ex:
        return
    for at, act in ((0, ex.start), (steps - 2, ex.relay), (steps - 1, ex.finish)):
        @pl.when(step == at)
        def _(act=act):
            act(*xrefs)


def _sgu_fwd(h, wm, bs_t, g_v, b_v, ex=None):
    t = h.shape[1]
    steps = t // SGU_BLOCK

    def body(*refs):
        (u_ref, v_ref, wm_ref, bs_ref, g_ref, b_ref), (y_ref,), _, xrefs = _split_refs(refs, 6, 1, 0, ex)
        xhat, _ = _ln_stats(_gelu(v_ref[...]))
        vn = (xhat * g_ref[...] + b_ref[...]).astype(BF16)
        gu = _gelu(u_ref[...])
        for g in range(GROUPS):
            sl = slice(g * HEAD, (g + 1) * HEAD)
            mixed = _dot(wm_ref[g], vn[:, sl]) + bs_ref[:, g:g + 1]
            y_ref[:, sl] = (gu[:, sl] * mixed).astype(BF16)
        _ride(ex, xrefs, pl.program_id(0), steps)

    blk = lambda b: pl.BlockSpec((None, SGU_BLOCK, D_MODEL), functools.partial(lambda i, b: (b, i, 0), b=b))
    whole = lambda s: pl.BlockSpec(s, lambda i: (0,) * len(s))
    res = pl.pallas_call(
        body,
        grid=(steps,),
        in_specs=[blk(SLOT_U), blk(SLOT_V), whole(wm.shape), whole(bs_t.shape), whole(g_v.shape), whole(b_v.shape)]
        + (ex.in_specs if ex else []),
        out_specs=[pl.BlockSpec((SGU_BLOCK, D_MODEL), lambda i: (i, 0))] + (ex.out_specs if ex else []),
        out_shape=[jax.ShapeDtypeStruct((t, D_MODEL), BF16)] + (ex.out_shape if ex else []),
        scratch_shapes=ex.scratch if ex else [],
        compiler_params=_params(("arbitrary",)),
        name="sgu_fwd",
    )(h, h, wm, bs_t, g_v, b_v, *(ex.arrays if ex else []))
    return res[0], res[1:]


def _sgu_bwd(h, dy, dh, wm, wm_t, bs_t, g_v, b_v):
    t = h.shape[1]

    def body(u_ref, v_ref, dy_ref, dh_in, wm_ref, wmt_ref, bs_ref, g_ref, b_ref,
             duv_ref, dw_ref, dbs_ref, dg_ref, db_ref, dvn_ref):
        del dh_in
        du_ref, dv_ref = duv_ref.at[0], duv_ref.at[1]
        i = pl.program_id(0)

        @pl.when(i == 0)
        def _():
            dw_ref[...] = jnp.zeros_like(dw_ref)
            dbs_ref[...] = jnp.zeros_like(dbs_ref)
            dg_ref[...] = jnp.zeros_like(dg_ref)
            db_ref[...] = jnp.zeros_like(db_ref)

        u = u_ref[...]
        v = v_ref[...]
        gv, gvp = _gelu_and_grad(v)
        xhat, rstd = _ln_stats(gv)
        vn = (xhat * g_ref[...] + b_ref[...]).astype(BF16)
        gu, gup = _gelu_and_grad(u)
        lane = lax.broadcasted_iota(jnp.int32, (SGU_BLOCK, LANE), 1)
        dbs = jnp.zeros((SGU_BLOCK, LANE), F32)
        for g in range(GROUPS):
            sl = slice(g * HEAD, (g + 1) * HEAD)
            vn_g = vn[:, sl]
            mixed = _dot(wm_ref[g], vn_g) + bs_ref[:, g:g + 1]
            dy_g = dy_ref[:, sl]
            du_ref[:, sl] = (dy_g * mixed * gup[:, sl]).astype(BF16)
            dmix = dy_g * gu[:, sl]
            dmb = dmix.astype(BF16)
            dvn_ref[:, sl] = _dot(wmt_ref[g], dmb)
            dw_ref[g] += _dot_nt(dmb, vn_g)
            dbs = dbs + jnp.where(lane == g, jnp.sum(dmix, axis=1, keepdims=True), 0.0)
        dbs_ref[...] += dbs
        dvn = dvn_ref[...]
        dg_ref[...] += _sum0(dvn * xhat)
        db_ref[...] += _sum0(dvn)
        dv_ref[...] = (_ln_bwd(dvn, xhat, rstd, g_ref[...]) * gvp).astype(BF16)

    blk = lambda b: pl.BlockSpec((None, SGU_BLOCK, D_MODEL), functools.partial(lambda i, b: (b, i, 0), b=b))
    row = pl.BlockSpec((SGU_BLOCK, D_MODEL), lambda i: (i, 0))
    whole = lambda s: pl.BlockSpec(s, lambda i: (0,) * len(s))
    vec = (1, D_MODEL)
    return pl.pallas_call(
        body,
        grid=(t // SGU_BLOCK,),
        in_specs=[blk(SLOT_U), blk(SLOT_V), row, pl.BlockSpec(memory_space=pl.ANY),
                  whole(wm.shape), whole(wm_t.shape), whole(bs_t.shape), whole(vec), whole(vec)],
        out_specs=[pl.BlockSpec((2, SGU_BLOCK, D_MODEL), lambda i: (SLOT_U // 2, i, 0)),
                   whole(wm.shape), whole((SGU_BLOCK, LANE)), whole(vec), whole(vec)],
        out_shape=[jax.ShapeDtypeStruct(dh.shape, BF16),
                   jax.ShapeDtypeStruct(wm.shape, F32), jax.ShapeDtypeStruct((SGU_BLOCK, LANE), F32),
                   jax.ShapeDtypeStruct(vec, F32), jax.ShapeDtypeStruct(vec, F32)],
        scratch_shapes=[pltpu.VMEM((SGU_BLOCK, D_MODEL), F32)],
        input_output_aliases={3: 0},
        compiler_params=_params(("arbitrary",)),
        name="sgu_bwd",
    )(h, h, dy, dh, wm, wm_t, bs_t, g_v, b_v)


def _split3(x):
    hi = x.astype(BF16)
    r1 = x - hi.astype(F32)
    mid = r1.astype(BF16)
    lo = (r1 - mid.astype(F32)).astype(BF16)
    return hi, mid, lo


def _tri_matmul(tri, x):
    hi, mid, lo = _split3(x)
    dot = lambda p: jnp.dot(tri, p, preferred_element_type=F32)
    return dot(hi) + dot(mid) + dot(lo)


def _lower_bound(logits):
    l0, l1 = logits[0:1, :], logits[1:2, :]
    mx = jnp.maximum(l0, l1)
    e0, e1 = jnp.exp(l0 - mx), jnp.exp(l1 - mx)
    return e0 / (e0 + e1)


def _hgrn_gates(q_raw, f_raw, lb):
    q = q_raw * jax.nn.sigmoid(q_raw)
    sig = jax.nn.sigmoid(f_raw)
    f = lb + (1.0 - lb) * sig
    row = lax.broadcasted_iota(jnp.int32, (CHUNK, CHUNK), 0)
    col = lax.broadcasted_iota(jnp.int32, (CHUNK, CHUNK), 1)
    c = _tri_matmul((row >= col).astype(BF16), jnp.log(f))
    return q, sig, f, 1.0 - f, c


def _offdiag_terms(qh, kh, ch, tb):
    rows = slice(tb * SUB, (tb + 1) * SUB)
    r = ch[tb * SUB - 1:tb * SUB, :]
    eqh = jnp.exp(ch[rows] - r)
    ekh = jnp.exp(jnp.minimum(r - ch, 0.0))
    return rows, eqh, qh[rows] * eqh, ekh, kh * ekh


def _diag_decay(cb, s, trow):
    return jnp.exp(jnp.where(trow >= s, cb - cb[s:s + 1, :], NEG))


def _split_refs(refs, n_in, n_out, n_scratch, ex):
    nx = ex.n if ex else 0
    ins, refs = refs[:n_in], refs[n_in:]
    xsrc, refs = refs[:nx], refs[nx:]
    outs, refs = refs[:n_out], refs[n_out:]
    xout, refs = refs[:nx], refs[nx:]
    return ins, outs, refs[:n_scratch], (xsrc, xout, refs[n_scratch:])


def _hgrn_fwd(h, logits, g_norm, ex=None):
    t = h.shape[1]
    nc = t // CHUNK
    per_step = FWD_CHUNKS_PER_STEP if nc % FWD_CHUNKS_PER_STEP == 0 else 1
    steps = nc // per_step

    def body(*refs):
        ins, outs, scratch, xrefs = _split_refs(refs, 3, 3, 4, ex)
        hgrn_ref, lg_ref, gn_ref = ins
        y_ref, o_ref, sall_ref = outs
        st_ref, q_s, k_s, c_s = scratch

        @pl.when(pl.program_id(0) == 0)
        def _():
            st_ref[...] = jnp.zeros_like(st_ref)

        lb = _lower_bound(lg_ref[...])
        for sub in range(per_step):
            rows = pl.ds(sub * CHUNK, CHUNK)
            chunk(lb, sall_ref.at[sub], *(r.at[rows] for r in (hgrn_ref.at[SLOT_Q], hgrn_ref.at[SLOT_F], hgrn_ref.at[SLOT_I],
                                                               hgrn_ref.at[SLOT_OG], y_ref, o_ref)),
                  gn_ref, st_ref, q_s, k_s, c_s)
        _ride(ex, xrefs, pl.program_id(0), steps)

    def chunk(lb, sall_ref, q_ref, f_ref, i_ref, og_ref, y_ref, o_ref, gn_ref, st_ref, q_s, k_s, c_s):
        q, _, _, k, c = _hgrn_gates(q_ref[...], f_ref[...], lb)
        q_s[...] = q
        k_s[...] = k
        c_s[...] = c
        col64 = lax.broadcasted_iota(jnp.int32, (SUB, CHUNK), 1)
        trow = lax.broadcasted_iota(jnp.int32, (SUB, HEAD), 0)

        def head(hd, carry):
            sl = pl.ds(pl.multiple_of(hd * HEAD, HEAD), HEAD)
            qh, kh, ch, ih = q_s[:, sl], k_s[:, sl], c_s[:, sl], i_ref[:, sl]
            st = st_ref[hd]
            sall_ref[hd] = st
            c_last = ch[CHUNK - 1:CHUNK, :]
            o = _dot_nt(qh * jnp.exp(ch), st)
            st_ref[hd] = st * jnp.exp(c_last) + _dot_tn(ih, kh * jnp.exp(c_last - ch))
            a_rows = [jnp.zeros((SUB, CHUNK), F32)]
            for tb in range(1, CHUNK // SUB):
                _, _, q_hat, _, k_hat = _offdiag_terms(qh, kh, ch, tb)
                a_rows.append(jnp.where(col64 < tb * SUB, _dot_nt(q_hat, k_hat), 0.0))
            o = o + _dot(jnp.concatenate(a_rows, axis=0), ih)
            o_rows = []
            for b in range(CHUNK // SUB):
                rows = slice(b * SUB, (b + 1) * SUB)
                qb, cb, kb, ib = qh[rows], ch[rows], kh[rows], ih[rows]
                ob = jnp.zeros((SUB, HEAD), F32)
                for s in range(SUB):
                    a = jnp.sum(qb * _diag_decay(cb, s, trow) * kb[s:s + 1, :], axis=1, keepdims=True)
                    ob = ob + a * ib[s:s + 1, :]
                o_rows.append(ob)
            o = o + jnp.concatenate(o_rows, axis=0)
            o_ref[:, sl] = o
            og = og_ref[:, sl]
            on = o * lax.rsqrt(_mean(o * o) + RMS_EPS)
            y_ref[:, sl] = (on * gn_ref[:, sl] * (og * jax.nn.sigmoid(og))).astype(BF16)
            return carry

        lax.fori_loop(0, HEADS, head, 0, unroll=HEAD_UNROLL)

    rows = per_step * CHUNK
    row = pl.BlockSpec((rows, D_MODEL), lambda n: (n, 0))
    whole = lambda s: pl.BlockSpec(s, lambda n: (0,) * len(s))
    res = pl.pallas_call(
        body,
        grid=(steps,),
        in_specs=[pl.BlockSpec((4, rows, D_MODEL), lambda n: (SLOT_Q // 4, n, 0)), whole(logits.shape), whole(g_norm.shape)]
        + (ex.in_specs if ex else []),
        out_specs=[row, row, pl.BlockSpec((per_step, HEADS, HEAD, HEAD), lambda n: (n, 0, 0, 0))]
        + (ex.out_specs if ex else []),
        out_shape=[jax.ShapeDtypeStruct((t, D_MODEL), BF16), jax.ShapeDtypeStruct((t, D_MODEL), F32),
                   jax.ShapeDtypeStruct((nc, HEADS, HEAD, HEAD), F32)] + (ex.out_shape if ex else []),
        scratch_shapes=[pltpu.VMEM((HEADS, HEAD, HEAD), F32)] + [pltpu.VMEM((CHUNK, D_MODEL), F32)] * 3
        + (ex.scratch if ex else []),
        compiler_params=_params(("arbitrary",)),
        name="hgrn_fwd",
    )(h, logits, g_norm, *(ex.arrays if ex else []))
    return res[0], res[1], res[2], res[3:]


def _hgrn_bwd(h, o_all, dy, states, dh, logits, g_norm, ex=None):
    t = h.shape[1]
    nc = t // CHUNK

    def body(*refs):
        ins, outs, scratch, xrefs = _split_refs(refs, 7, 3, 8, ex)
        hgrn_ref, o_ref, dy_ref, sall_ref, _, lg_ref, gn_ref = ins
        q_ref, f_ref, i_ref, og_ref = (hgrn_ref.at[s] for s in (SLOT_Q, SLOT_F, SLOT_I, SLOT_OG))
        dqfio_ref, dlg_ref, dgn_ref = outs
        dst_ref, dlb_ref, q_s, k_s, c_s, dq_s, dk_s, dc_s = scratch
        dq_ref, df_ref, di_ref, dog_ref = (dqfio_ref.at[s] for s in (SLOT_Q, SLOT_F, SLOT_I, SLOT_OG))
        n = pl.program_id(0)

        @pl.when(n == 0)
        def _():
            dst_ref[...] = jnp.zeros_like(dst_ref)
            dlb_ref[...] = jnp.zeros_like(dlb_ref)
            dgn_ref[...] = jnp.zeros_like(dgn_ref)
            if ex:
                ex.start(*xrefs)

        lb = _lower_bound(lg_ref[...])
        q_raw = q_ref[...]
        q, sig, f, k, c = _hgrn_gates(q_raw, f_ref[...], lb)
        q_s[...] = q
        k_s[...] = k
        c_s[...] = c
        col64 = lax.broadcasted_iota(jnp.int32, (SUB, CHUNK), 1)
        trow = lax.broadcasted_iota(jnp.int32, (SUB, HEAD), 0)
        row64 = lax.broadcasted_iota(jnp.int32, (CHUNK, HEAD), 0)

        def head(hd, carry):
            sl = pl.ds(pl.multiple_of(hd * HEAD, HEAD), HEAD)
            qh, kh, ch, ih = q_s[:, sl], k_s[:, sl], c_s[:, sl], i_ref[:, sl]
            st = sall_ref[hd]
            dst = dst_ref[hd]
            oh, dyh, og, gn = o_ref[:, sl], dy_ref[:, sl], og_ref[:, sl], gn_ref[:, sl]
            sg = jax.nn.sigmoid(og)
            sil = og * sg
            rms = lax.rsqrt(_mean(oh * oh) + RMS_EPS)
            on = oh * rms
            dog_ref[:, sl] = (dyh * on * gn * _silu_grad(og, sg)).astype(BF16)
            dgn_ref[:, sl] += _sum0(dyh * on * sil)
            don = dyh * gn * sil
            do = rms * (don - on * _mean(don * on))
            dob = do.astype(BF16)

            c_last = ch[CHUNK - 1:CHUNK, :]
            eq = jnp.exp(ch)
            q_til = qh * eq
            ekl = jnp.exp(c_last - ch)
            k_til = kh * ekl
            ecl = jnp.exp(c_last)
            dq_til = _dot(dob, st)
            dk_til = _dot(ih, dst)
            di = _dot_nt(k_til, dst)
            dc_last = _sum0(dk_til * k_til) + _sum0(dst * st) * ecl
            dst_ref[hd] = _dot_tn(dob, q_til) + dst * ecl
            dq = dq_til * eq
            dc = dq_til * q_til - dk_til * k_til
            dk = dk_til * ekl

            da_full = _dot_nt(dob, ih)
            a_rows = [jnp.zeros((SUB, CHUNK), F32)]
            dq_rows = [jnp.zeros((SUB, HEAD), F32)]
            dc_rows = [jnp.zeros((SUB, HEAD), F32)]
            for tb in range(1, CHUNK // SUB):
                rows, eqh, q_hat, ekh, k_hat = _offdiag_terms(qh, kh, ch, tb)
                keep = col64 < tb * SUB
                a_rows.append(jnp.where(keep, _dot_nt(q_hat, k_hat), 0.0))
                da = jnp.where(keep, da_full[rows], 0.0)
                dq_hat = _dot(da, k_hat)
                dk_hat = _dot_tn(da, q_hat)
                dq_rows.append(dq_hat * eqh)
                dc_rows.append(dq_hat * q_hat)
                dk = dk + dk_hat * ekh
                dc = dc - dk_hat * k_hat
            di = di + _dot_tn(jnp.concatenate(a_rows, axis=0), dob)

            dk_rows, di_rows = [], []
            for b in range(CHUNK // SUB):
                rows = slice(b * SUB, (b + 1) * SUB)
                qb, cb, kb, ib, dob_ = qh[rows], ch[rows], kh[rows], ih[rows], do[rows]
                dq_diag = jnp.zeros((SUB, HEAD), F32)
                dk_diag = jnp.zeros((SUB, HEAD), F32)
                di_diag = jnp.zeros((SUB, HEAD), F32)
                for s in range(SUB):
                    ks = kb[s:s + 1, :]
                    dec = _diag_decay(cb, s, trow)
                    a = jnp.sum(qb * dec * ks, axis=1, keepdims=True)
                    gk = jnp.sum(dob_ * ib[s:s + 1, :], axis=1, keepdims=True) * dec
                    dq_diag = dq_diag + gk * ks
                    dk_diag = dk_diag + jnp.where(trow == s, _sum0(gk * qb), 0.0)
                    di_diag = di_diag + jnp.where(trow == s, _sum0(a * dob_), 0.0)
                dq_rows[b] = dq_rows[b] + dq_diag
                dc_rows[b] = dc_rows[b] + qb * dq_diag - kb * dk_diag
                dk_rows.append(dk_diag)
                di_rows.append(di_diag)
            dq = dq + jnp.concatenate(dq_rows, axis=0)
            dk = dk + jnp.concatenate(dk_rows, axis=0)
            dc = dc + jnp.concatenate(dc_rows, axis=0) + jnp.where(row64 == CHUNK - 1, dc_last, 0.0)
            di_ref[:, sl] = (di + jnp.concatenate(di_rows, axis=0)).astype(BF16)
            dq_s[:, sl] = dq
            dk_s[:, sl] = dk
            dc_s[:, sl] = dc
            return carry

        lax.fori_loop(0, HEADS, head, 0, unroll=HEAD_UNROLL)

        row = lax.broadcasted_iota(jnp.int32, (CHUNK, CHUNK), 0)
        col = lax.broadcasted_iota(jnp.int32, (CHUNK, CHUNK), 1)
        dlf = _tri_matmul((row <= col).astype(BF16), dc_s[...])
        df = dlf / f - dk_s[...]
        dlb_ref[...] += _sum0(df * (1.0 - sig))
        df_ref[...] = (df * (1.0 - lb) * sig * (1.0 - sig)).astype(BF16)
        dq_ref[...] = (dq_s[...] * _silu_grad(q_raw, jax.nn.sigmoid(q_raw))).astype(BF16)

        @pl.when(n == nc - 1)
        def _():
            d0 = dlb_ref[...] * lb * (1.0 - lb)
            dlg_ref[0:1, :] = d0
            dlg_ref[1:2, :] = -d0
            if ex:
                ex.finish(*xrefs)

    rev = lambda n: nc - 1 - n
    slots = pl.BlockSpec((4, CHUNK, D_MODEL), lambda n: (SLOT_Q // 4, rev(n), 0))
    row = pl.BlockSpec((CHUNK, D_MODEL), lambda n: (rev(n), 0))
    whole = lambda s: pl.BlockSpec(s, lambda n: (0,) * len(s))
    vec = (1, D_MODEL)
    res = pl.pallas_call(
        body,
        grid=(nc,),
        in_specs=[slots, row, row, pl.BlockSpec((None, HEADS, HEAD, HEAD), lambda n: (rev(n), 0, 0, 0)),
                  pl.BlockSpec(memory_space=pl.ANY), whole(logits.shape), whole(vec)] + (ex.in_specs if ex else []),
        out_specs=[slots, whole((2, D_MODEL)), whole(vec)] + (ex.out_specs if ex else []),
        out_shape=[jax.ShapeDtypeStruct(dh.shape, BF16), jax.ShapeDtypeStruct((2, D_MODEL), F32),
                   jax.ShapeDtypeStruct(vec, F32)] + (ex.out_shape if ex else []),
        scratch_shapes=[pltpu.VMEM((HEADS, HEAD, HEAD), F32), pltpu.VMEM(vec, F32)]
        + [pltpu.VMEM((CHUNK, D_MODEL), F32)] * 6 + (ex.scratch if ex else []),
        input_output_aliases={4: 0},
        compiler_params=_params(("arbitrary",)),
        name="hgrn_bwd",
    )(h, o_all, dy, states, dh, logits, g_norm, *(ex.arrays if ex else []))
    return res[0], res[1], res[2], res[3:]


def _merge_fwd(i, n, ga, gb, za, zb):
    return jax.nn.sigmoid(ga) * za + jax.nn.sigmoid(gb) * zb


def _merge_bwd(i, n, ga, gb, za, zb, dm):
    sa, sb = jax.nn.sigmoid(ga), jax.nn.sigmoid(gb)
    dgates = jnp.stack([(dm * za * sa * (1.0 - sa)).astype(BF16), (dm * zb * sb * (1.0 - sb)).astype(BF16)])
    return dgates, dm * sa, dm * sb


def _ln1_fwd(i, n, x, r1, g, b):
    xhat, _ = _ln_stats(ALPHA * x + r1)
    x1 = xhat * g + b
    return x1, x1


def _ln1_bwd(i, n, x, r1, dx1, g):
    xhat, rstd = _ln_stats(ALPHA * x + r1)
    dz = _ln_bwd(dx1, xhat, rstd, g)
    return dz, dz, _sum0(dx1 * xhat), _sum0(dx1)


def _ln2_loss(i, n, x1, fo, pg, pp, tgt, g, b):
    sg = jax.nn.sigmoid(pg)
    xhat, rstd = _ln_stats(ALPHA * x1 + fo + sg * pp)
    diff = xhat * g + b - tgt
    loss = 0.5 * jnp.sum(_mean(diff * diff), axis=0, keepdims=True)
    dy = diff * (1.0 / D_MODEL)
    dz = _ln_bwd(dy, xhat, rstd, g)
    return (dz, dz, dz * pp * sg * (1.0 - sg), dz * sg,
            jnp.broadcast_to(loss, (8, LANE)), _sum0(dy * xhat), _sum0(dy))


def _shift_down(cur, halo, tile):
    row = lax.broadcasted_iota(jnp.int32, cur.shape, 0)
    m1 = jnp.where(row == 0, halo[7:8, :], pltpu.roll(cur, 1, 0))
    m2 = jnp.where(row == 0, halo[6:7, :], jnp.where(row == 1, halo[7:8, :], pltpu.roll(cur, 2, 0)))
    return m1, m2


def _shift_up(cur, halo, tile):
    row = lax.broadcasted_iota(jnp.int32, cur.shape, 0)
    p1 = jnp.where(row == tile - 1, halo[0:1, :], pltpu.roll(cur, tile - 1, 0))
    p2 = jnp.where(row == tile - 2, halo[0:1, :], jnp.where(row == tile - 1, halo[1:2, :], pltpu.roll(cur, tile - 2, 0)))
    return p1, p2


def _conv_pre(i, gate, halo, w, b, tile):
    halo = jnp.where(i == 0, 0.0, halo)
    m1, m2 = _shift_down(gate, halo, tile)
    return w[0:1, :] * m2 + w[1:2, :] * m1 + w[2:3, :] * gate + b, m1, m2


def _conv_fwd(tile, i, n, gate, halo, val, w, b):
    cg, _, _ = _conv_pre(i, gate, halo, w, b, tile)
    return _gelu(cg) * val


def _conv_bwd_a(tile, i, n, gate, halo, val, dhid, w, b):
    cg, m1, m2 = _conv_pre(i, gate, halo, w, b, tile)
    act, slope = _gelu_and_grad(cg)
    dcg = dhid * val * slope
    return dcg, dhid * act, _sum0(dcg * m2), _sum0(dcg * m1), _sum0(dcg * gate), _sum0(dcg)


def _conv_bwd_b(tile, i, n, dcg, halo, w):
    dcg = dcg.astype(F32)
    halo = jnp.where(i == n - 1, 0.0, halo.astype(F32))
    p1, p2 = _shift_up(dcg, halo, tile)
    return w[2:3, :] * dcg + w[1:2, :] * p1 + w[0:1, :] * p2


def _halo_spec(width, tile, t, nxt, rows=8):
    per = tile // rows
    last = t // rows - 1
    if nxt:
        return pl.BlockSpec((rows, width), lambda i: (jnp.minimum((i + 1) * per, last), 0))
    return pl.BlockSpec((rows, width), lambda i: (jnp.maximum(i * per - 1, 0), 0))


def _adamw(i, n, w, m, v, parts):
    g = parts[0].astype(F32)
    for j in range(1, parts.shape[0]):
        g = g + parts[j].astype(F32)
    m_new = ADAM_B1 * m + (1.0 - ADAM_B1) * g
    v_new = ADAM_B2 * v + (1.0 - ADAM_B2) * (g * g)
    m_hat = m_new / (1.0 - ADAM_B1 ** ADAM_STEP)
    v_hat = v_new / (1.0 - ADAM_B2 ** ADAM_STEP)
    delta = -ADAM_LR * (m_hat / (jnp.sqrt(v_hat) + ADAM_EPS) + ADAM_WD * w)
    return g, delta, m_new, v_new


def _adam_call(w, m, v, parts, name):
    r, c = w.shape
    tile = _pick(r, (256, 128)) if r > 256 else r
    spec = pl.BlockSpec((parts.shape[0], tile, c), lambda i: (0, i, 0))
    return _rowwise(_adamw, [w, m, v, (parts, spec)], [], [(c, F32)] * 4, [], tile=tile, name=name)


def _peer(k):
    x, y, c = lax.axis_index("x"), lax.axis_index("y"), lax.axis_index("c")
    px = x ^ ((k >> 2) & 1)
    py = y ^ ((k >> 1) & 1)
    pc = c ^ (k & 1)
    return (px, py, pc), 4 * px + 2 * py + pc


def _my_index():
    return 4 * lax.axis_index("x") + 2 * lax.axis_index("y") + lax.axis_index("c")


class _Exchange:
    KINDS = ("gather", "gather+relay", "scatter", "scatter+pairs")

    def __init__(self, entries):
        assert all(k in self.KINDS for _, k in entries), [k for _, k in entries]
        self.arrays = [a for a, _ in entries]
        self.scatter = [k.startswith("scatter") for _, k in entries]
        self.relayed = ["+relay" in k for _, k in entries]
        self.pairs = ["+pairs" in k for _, k in entries]
        self.n = len(entries)
        self.in_specs = [pl.BlockSpec(memory_space=pl.ANY)] * self.n
        self.out_specs = [pl.BlockSpec(memory_space=pl.ANY)] * self.n
        shapes = [tuple(a.shape[1:]) if sc else tuple(a.shape) for a, sc in zip(self.arrays, self.scatter)]
        counts = [N_CHIP if p else N_DEV for p in self.pairs]
        self.out_shape = [jax.ShapeDtypeStruct((n,) + s, a.dtype) for n, s, a in zip(counts, shapes, self.arrays)]
        per = N_DEV - 1
        self.scratch = [pltpu.SemaphoreType.DMA((self.n * per,)), pltpu.SemaphoreType.DMA((self.n * per,)),
                        pltpu.SemaphoreType.DMA((self.n,))]

    def _copies(self, srcs, outs, sems):
        send_sems, recv_sems, local_sems = sems
        x, y, c = lax.axis_index("x"), lax.axis_index("y"), lax.axis_index("c")
        me = _my_index()
        per = N_DEV - 1
        local, first, passed, relay_arrivals, arrivals = [], [], [], [], []
        for a in range(self.n):

            def copy(k, src, dst, dev, a=a):
                return pltpu.make_async_remote_copy(
                    src_ref=src, dst_ref=dst, send_sem=send_sems.at[a * per + k], recv_sem=recv_sems.at[a * per + k],
                    device_id=dev, device_id_type=pl.DeviceIdType.MESH)

            if self.pairs[a]:
                chip = 2 * x + y
                for k in range(N_CHIP):
                    to = chip ^ k
                    piece = srcs[a].at[_slot_of_group(2 * to + c) // 2]
                    if k == 0:
                        local.append(pltpu.make_async_copy(piece, outs[a].at[chip], local_sems.at[a]))
                    else:
                        dev = (to // 2, to % 2, c)
                        first.append(copy(k - 1, piece, outs[a].at[chip], dev))
                        arrivals.append(copy(k - 1, piece, outs[a].at[to], dev))
                continue
            mine = srcs[a].at[me] if self.scatter[a] else srcs[a]
            land = outs[a].at[me]
            local.append(pltpu.make_async_copy(mine, land, local_sems.at[a]))
            if self.relayed[a]:
                block = lambda px, py, pc, a=a: outs[a].at[4 * px + 2 * py + pc]
                chips = [(1 - x, y), (x, 1 - y), (1 - x, 1 - y)]
                first.append(copy(0, mine, land, (x, y, 1 - c)))
                arrivals.append(copy(0, mine, block(x, y, 1 - c), (x, y, 1 - c)))
                for j, (px, py) in enumerate(chips):
                    first.append(copy(1 + j, mine, land, (px, py, c)))
                    relay_arrivals.append(copy(1 + j, mine, block(px, py, c), (px, py, c)))
                    passed.append(copy(4 + j, block(px, py, c), block(px, py, c), (x, y, 1 - c)))
                    arrivals.append(copy(4 + j, mine, block(px, py, 1 - c), (x, y, 1 - c)))
                continue
            for k in range(1, N_DEV):
                dev, idx = _peer(k)
                if self.scatter[a]:
                    first.append(copy(k - 1, srcs[a].at[idx], land, dev))
                else:
                    first.append(copy(k - 1, mine, land, dev))
                arrivals.append(copy(k - 1, mine, outs[a].at[idx], dev))
        return local, first, passed, relay_arrivals, arrivals

    def start(self, srcs, outs, sems):
        local, first, _, _, _ = self._copies(srcs, outs, sems)
        for cp in local + first:
            cp.start()

    def relay(self, srcs, outs, sems):
        _, _, passed, relay_arrivals, _ = self._copies(srcs, outs, sems)
        for landed, onward in zip(relay_arrivals, passed):
            landed.wait_recv()
            onward.start()

    def finish(self, srcs, outs, sems):
        local, first, passed, _, arrivals = self._copies(srcs, outs, sems)
        for cp in arrivals:
            cp.wait_recv()
        for cp in first + passed:
            cp.wait_send()
        for cp in local:
            cp.wait()


def _gather_project(x_b, shard):
    t, d = x_b.shape
    tm = _pick(t, MM_TILES)
    nrow = t // tm
    per = N_DEV - 1

    def parties():
        x, y, c = lax.axis_index("x"), lax.axis_index("y"), lax.axis_index("c")
        chips = [(1 - x, y), (x, 1 - y), (1 - x, 1 - y)]
        return (x, y, c), (x, y, 1 - c), [(px, py, c) for px, py in chips], [(px, py, 1 - c) for px, py in chips]

    slot = lambda dev: _slot_of_group(4 * dev[0] + 2 * dev[1] + dev[2])
    ici_step, passed_step = (2, 3, 6), (4, 5, 7)
    me, sibling, over_ici, passed_on = parties()
    by_step = {0: me, 1: sibling, **dict(zip(ici_step, over_ici)), **dict(zip(passed_step, passed_on))}
    order = jnp.stack([slot(by_step[j]) for j in range(N_DEV)]).astype(jnp.int32)

    def body(order_ref, x_ref, shard_ref, h_ref, wall_ref, wbuf, fetch_sem, send_sems, recv_sems, local_sem):
        del order_ref
        me, sibling, over_ici, passed_on = parties()
        j, i = pl.program_id(0), pl.program_id(1)
        land = lambda dev: wall_ref.at[slot(dev)]

        def copy(k, src, block, to):
            return pltpu.make_async_remote_copy(src_ref=src, dst_ref=land(block), send_sem=send_sems.at[k],
                                                recv_sem=recv_sems.at[k], device_id=to, device_id_type=pl.DeviceIdType.MESH)

        def fetch(src):
            cp = pltpu.make_async_copy(src, wbuf, fetch_sem)
            cp.start()
            cp.wait()

        keep = pltpu.make_async_copy(shard_ref, land(me), local_sem)
        first = [copy(0, shard_ref, me, sibling)] + [copy(1 + n, shard_ref, me, dev) for n, dev in enumerate(over_ici)]
        onward = [copy(4 + n, land(dev), dev, sibling) for n, dev in enumerate(over_ici)]

        @pl.when(jnp.logical_and(i == 0, j == 0))
        def _():
            keep.start()
            for cp in first:
                cp.start()
            fetch(shard_ref)

        @pl.when(jnp.logical_and(i == 0, j == 1))
        def _():
            copy(0, shard_ref, sibling, me).wait_recv()
            fetch(land(sibling))

        for n, dev in enumerate(over_ici):
            @pl.when(jnp.logical_and(i == 0, j == ici_step[n]))
            def _(n=n, dev=dev):
                copy(1 + n, shard_ref, dev, me).wait_recv()
                onward[n].start()
                fetch(land(dev))

        for n, dev in enumerate(passed_on):
            @pl.when(jnp.logical_and(i == 0, j == passed_step[n]))
            def _(n=n, dev=dev):
                copy(4 + n, shard_ref, dev, me).wait_recv()
                fetch(land(dev))

        h_ref[...] = _dot(x_ref[...], wbuf[...])

        @pl.when(jnp.logical_and(i == nrow - 1, j == N_DEV - 1))
        def _():
            for cp in first + onward:
                cp.wait_send()
            keep.wait()

    need = 2 * (tm * d * 2 + tm * d * 4) + d * d * 2 + tm * d * 4
    h, w_all = pl.pallas_call(
        body,
        grid_spec=pltpu.PrefetchScalarGridSpec(
            num_scalar_prefetch=1,
            grid=(N_DEV, nrow),
            in_specs=[pl.BlockSpec((tm, d), lambda j, i, order: (i, 0)), pl.BlockSpec(memory_space=pl.ANY)],
            out_specs=[pl.BlockSpec((None, tm, d), lambda j, i, order: (order[j], i, 0)), pl.BlockSpec(memory_space=pl.ANY)],
            scratch_shapes=[pltpu.VMEM((d, d), BF16), pltpu.SemaphoreType.DMA, pltpu.SemaphoreType.DMA((per,)),
                            pltpu.SemaphoreType.DMA((per,)), pltpu.SemaphoreType.DMA],
        ),
        out_shape=[jax.ShapeDtypeStruct((N_DEV, t, d), F32), jax.ShapeDtypeStruct((N_DEV, d, d), BF16)],
        compiler_params=_params(("arbitrary", "arbitrary"), need),
        name="gather_project",
    )(order, x_b, shard)
    return h, w_all


def _pair_sums(g):
    n, r, cols = g.shape
    half = n // 2

    def swap(g_ref, got_ref, send_sems, recv_sems):
        x, y, c = lax.axis_index("x"), lax.axis_index("y"), lax.axis_index("c")
        copies = [pltpu.make_async_remote_copy(
            src_ref=g_ref.at[2 * j + 1 - c], dst_ref=got_ref.at[j], send_sem=send_sems.at[j], recv_sem=recv_sems.at[j],
            device_id=(x, y, 1 - c), device_id_type=pl.DeviceIdType.MESH) for j in range(half)]
        for cp in copies:
            cp.start()
        for cp in copies:
            cp.wait()

    got = pl.pallas_call(
        swap,
        in_specs=[pl.BlockSpec(memory_space=pl.ANY)],
        out_specs=pl.BlockSpec(memory_space=pl.ANY),
        out_shape=jax.ShapeDtypeStruct((half, r, cols), g.dtype),
        scratch_shapes=[pltpu.SemaphoreType.DMA((half,))] * 2,
        name="pair_swap",
    )(g)

    def add(mine_ref, got_ref, out_ref):
        mine = jnp.where(lax.axis_index("c") == 0, mine_ref[0].astype(F32), mine_ref[1].astype(F32))
        out_ref[...] = (mine + got_ref[...].astype(F32)).astype(out_ref.dtype)

    tile = _pick(r, (512, 256, 128))
    return pl.pallas_call(
        add,
        grid=(half, r // tile),
        in_specs=[pl.BlockSpec((None, 2, tile, cols), lambda j, i: (j, 0, i, 0)),
                  pl.BlockSpec((None, tile, cols), lambda j, i: (j, i, 0))],
        out_specs=pl.BlockSpec((None, tile, cols), lambda j, i: (j, i, 0)),
        out_shape=jax.ShapeDtypeStruct((half, r, cols), g.dtype),
        compiler_params=_params(("parallel", "parallel")),
        name="pair_add",
    )(g.reshape(half, 2, r, cols), got)


def _local_step(x, p, tgt, small, comm):
    t = x.shape[0]
    tile = _pick(t, (256, 128))
    d = D_MODEL
    act_b, act_f = (d, BF16), (d, F32)
    x_b, p_b = x.astype(BF16), p.astype(BF16)

    chunk_id = jnp.arange(SGU_BLOCK) // CHUNK
    mask = chunk_id[:, None] >= chunk_id[None, :]
    wm = jnp.where(mask[None], small["sgu_w_s"], 0.0)
    wm_b = wm.astype(BF16)
    wm_t = jnp.swapaxes(wm, 1, 2).astype(BF16)
    bs_t = small["sgu_b_s"].T

    h, w_in = comm.project_in(x_b)
    y_a, got_a = _sgu_fwd(h, wm_b, bs_t, small["sgu_norm_g"], small["sgu_norm_b"], ex=comm.weights_exchange(0))
    y_b, o_all, states, got_b = _hgrn_fwd(h, small["lb_logits"], small["hgrn_norm_g"], ex=comm.weights_exchange(1))
    wts, conv_w = comm.weights(list(got_a) + list(got_b))
    z_a = _mm(y_a, wts["w_a"], out_dtype=F32, name="mm_za")
    z_b = _mm(y_b, wts["w_b"], out_dtype=F32, name="mm_zb")
    gates = [(h, SLOT_GA), (h, SLOT_GB)]
    merged, = _rowwise(_merge_fwd, gates + [z_a, z_b], [], [act_b], [], tile=tile, name="merge_fwd")
    r1 = _mm(merged, wts["w_o"], out_dtype=F32, name="mm_r1")
    x1, x1_b = _rowwise(_ln1_fwd, [x, r1], [small["ln1_g"], small["ln1_b"]], [act_f, act_b], [], tile=tile, name="ln1_fwd")
    gate = _mm(x1_b, wts["w_g"], out_dtype=F32, name="mm_gate")
    val = _mm(x1_b, wts["w_v"], out_dtype=F32, name="mm_val")
    pg = _mm(x1_b, wts["w_pg"], out_dtype=F32, name="mm_pg")
    pp = _mm(p_b, wts["w_pp"], out_dtype=F32, name="mm_pp")
    hid, = _rowwise(functools.partial(_conv_fwd, tile), [gate, (gate, _halo_spec(D_FF, tile, t, False)), val],
                    [conv_w, small["conv_b"]], [(D_FF, BF16)], [], tile=tile, name="conv_fwd")
    fo = _mm(hid, wts["w_down"], out_dtype=F32, name="mm_down")
    dz2, dz2_b, dpg, dpp, loss, dg2, db2 = _rowwise(
        _ln2_loss, [x1, fo, pg, pp, tgt], [small["ln2_g"], small["ln2_b"]],
        [act_f, act_b, act_b, act_b], [(8, LANE), (1, d), (1, d)], tile=tile, name="ln2_loss")

    dhid = _mm(dz2_b, wts["w_down"], out_dtype=BF16, name="mm_dhid", trans_b=True)
    g_down = _mm_tn(hid, dz2_b, out_dtype=BF16, name="mm_g_down")
    dcg, dval, dcw0, dcw1, dcw2, dcb = _rowwise(
        functools.partial(_conv_bwd_a, tile), [gate, (gate, _halo_spec(D_FF, tile, t, False)), val, dhid],
        [conv_w, small["conv_b"]], [(D_FF, BF16), (D_FF, BF16)], [(1, D_FF)] * 4, tile=tile, name="conv_bwd_a")
    dgate, = _rowwise(functools.partial(_conv_bwd_b, tile), [dcg, (dcg, _halo_spec(D_FF, tile, t, True, rows=16))],
                      [conv_w], [(D_FF, BF16)], [], tile=tile, name="conv_bwd_b")
    g_g = _mm_tn(x1_b, dgate, out_dtype=BF16, name="mm_g_gate")
    g_v = _mm_tn(x1_b, dval, out_dtype=BF16, name="mm_g_val")
    g_pg = _mm_tn(x1_b, dpg, out_dtype=BF16, name="mm_g_pg")
    g_pp = _mm_tn(p_b, dpp, out_dtype=BF16, name="mm_g_pp")
    dx1 = _mm(dgate, wts["w_g"], out_dtype=F32, name="mm_dx1_gate", trans_b=True, adds=[(dz2, ALPHA)])
    dx1 = _mm(dval, wts["w_v"], out_dtype=F32, name="mm_dx1_val", trans_b=True, adds=[(dx1, 1.0)])
    dx1 = _mm(dpg, wts["w_pg"], out_dtype=F32, name="mm_dx1_pg", trans_b=True, adds=[(dx1, 1.0)])
    dz1, dz1_b, dg1, db1 = _rowwise(_ln1_bwd, [x, r1, dx1], [small["ln1_g"]], [act_f, act_b], [(1, d), (1, d)],
                                    tile=tile, name="ln1_bwd")
    g_o = _mm_tn(merged, dz1_b, out_dtype=BF16, name="mm_g_o")
    dm = _mm(dz1_b, wts["w_o"], out_dtype=BF16, name="mm_dm", trans_b=True)
    dh, dza, dzb = _rowwise(_merge_bwd, gates + [z_a, z_b, dm], [],
                            [("stack", 2, SLOT_GA // 2, 8, d, BF16), act_b, act_b], [], tile=tile, name="merge_bwd")
    g_a = _mm_tn(y_a, dza, out_dtype=BF16, name="mm_g_a")
    g_b = _mm_tn(y_b, dzb, out_dtype=BF16, name="mm_g_b")
    dy_a = _mm(dza, wts["w_a"], out_dtype=BF16, name="mm_dya", trans_b=True)
    dy_b = _mm(dzb, wts["w_b"], out_dtype=F32, name="mm_dyb", trans_b=True)
    dh, dws, dbs, dgv_n, dbv_n = _sgu_bwd(h, dy_a, dh, wm_b, wm_t, bs_t, small["sgu_norm_g"], small["sgu_norm_b"])
    big = dict(w_a=g_a, w_b=g_b, w_o=g_o, w_g=g_g, w_v=g_v, w_down=g_down, w_pp=g_pp, w_pg=g_pg)
    sm = dict(sgu_w_s=jnp.where(mask[None], dws, 0.0), sgu_b_s=dbs[:, :GROUPS].T, sgu_norm_g=dgv_n, sgu_norm_b=dbv_n,
              ln1_g=dg1, ln1_b=db1, conv_w=jnp.concatenate([dcw0, dcw1, dcw2], axis=0), conv_b=dcb, ln2_g=dg2, ln2_b=db2,
              loss=loss)
    dh, dlogits, dgn, got = _hgrn_bwd(h, o_all, dy_b, states, dh, small["lb_logits"], small["hgrn_norm_g"],
                                      ex=comm.grads_exchange(big, sm))
    comm.grads_done(got)
    g_in = _mm_tn(x_b, dh, out_dtype=BF16, name="mm_g_in")
    ex = comm.last_exchange(g_in, dict(lb_logits=dlogits, hgrn_norm_g=dgn))
    res = _mm(dh, w_in, out_dtype=F32, name="mm_dx", trans_b=True, reduce_b=True, adds=[(dz1, ALPHA)], ex=ex)
    grad_x, got = res if ex else (res, ())
    comm.last_done(got)
    return grad_x


_SMALL_EARLY = ["sgu_w_s", "sgu_b_s", "sgu_norm_g", "sgu_norm_b", "ln1_g", "ln1_b", "ffn_conv_b", "ln2_g", "ln2_b"]
_SMALL_LATE = ["hgrn_lb_logits", "hgrn_norm_g"]
N_TAPS = D_FF // N_DEV
UP_COLS = 2 * D_FF // N_DEV


class _StepExchanges:
    def __init__(self, w_in_shard, shards):
        self.w_in_shard = w_in_shard
        self.shards = shards

    def project_in(self, x_b):
        return _gather_project(x_b, self.w_in_shard)

    def weights_exchange(self, part):
        return _Exchange([(s, "gather+relay") for s in (self.shards[:2] if part == 0 else self.shards[2:])])

    def weights(self, got):
        d, f = D_MODEL, D_FF
        w_br_g, w_o_g, w_up_g, w_down_g, w_pp_g, w_pg_g, conv_g = got
        w_br = w_br_g.transpose(1, 0, 2, 3).reshape(2, d, d)
        w_up = w_up_g.transpose(1, 0, 2).reshape(d, 2, f).transpose(1, 0, 2)
        wts = dict(w_a=w_br[0], w_b=w_br[1], w_o=w_o_g.reshape(d, d), w_g=w_up[0], w_v=w_up[1],
                   w_down=w_down_g.reshape(f, d), w_pp=w_pp_g.transpose(1, 0, 2).reshape(256, d), w_pg=w_pg_g.reshape(d, d))
        return wts, conv_g.transpose(1, 0, 2).reshape(3, f)

    def grads_exchange(self, big, sm):
        d = D_MODEL
        parts = [jnp.stack([big["w_a"], big["w_b"]]).reshape(2, N_DEV, 128, d).transpose(1, 0, 2, 3),
                 big["w_o"].reshape(N_DEV, 128, d),
                 jnp.concatenate([big["w_g"], big["w_v"]], axis=1).reshape(d, N_DEV, UP_COLS).transpose(1, 0, 2),
                 big["w_down"].reshape(N_DEV, N_TAPS, d),
                 big["w_pp"].reshape(256, N_DEV, 128).transpose(1, 0, 2),
                 big["w_pg"].reshape(N_DEV, 128, d)]
        packed, self.rows_early = _pack([sm[k] for k in ("sgu_w_s", "sgu_b_s", "sgu_norm_g", "sgu_norm_b", "ln1_g", "ln1_b",
                                                         "conv_b", "ln2_g", "ln2_b", "conv_w", "loss")])
        return _Exchange([(a, "scatter") for a in parts] + [(packed, "gather")])

    def grads_done(self, got):
        self.recv, self.small_early = got[:6], got[6]

    def last_exchange(self, g_in, sm):
        packed, self.rows_late = _pack([sm["lb_logits"], sm["hgrn_norm_g"]])
        return _Exchange([(_pair_sums(g_in), "scatter+pairs"), (packed, "gather")])

    def last_done(self, got):
        self.recv_in, self.small_late = got


def _rows128(a):
    flat = a.reshape(-1)
    rows = -(-flat.shape[0] // (8 * LANE)) * 8
    return jnp.pad(flat, (0, rows * LANE - flat.shape[0])).reshape(rows, LANE)


def _pack(parts):
    blocks = [_rows128(a) for a in parts]
    return jnp.concatenate(blocks, axis=0), [b.shape[0] for b in blocks]


def _unpack(packed, shapes, rows):
    out, r0 = [], 0
    for shp, r in zip(shapes, rows):
        n = math.prod(shp)
        out.append(packed[r0:r0 + r].reshape(-1)[:n].reshape(shp))
        r0 += r
    return out


def kernel(x, p, w_in, sgu_w_s, sgu_b_s, sgu_norm_g, sgu_norm_b, hgrn_lb_logits, hgrn_norm_g, w_branch, w_out, ln1_g, ln1_b, ffn_w_up, ffn_conv_w, ffn_conv_b, ffn_w_down, ln2_g, ln2_b, ple_w_proj, ple_w_gate, loss_target, m_w_in, m_sgu_w_s, m_sgu_b_s, m_sgu_norm_g, m_sgu_norm_b, m_hgrn_lb_logits, m_hgrn_norm_g, m_w_branch, m_w_out, m_ln1_g, m_ln1_b, m_ffn_w_up, m_ffn_conv_w, m_ffn_conv_b, m_ffn_w_down, m_ln2_g, m_ln2_b, m_ple_w_proj, m_ple_w_gate, v_w_in, v_sgu_w_s, v_sgu_b_s, v_sgu_norm_g, v_sgu_norm_b, v_hgrn_lb_logits, v_hgrn_norm_g, v_w_branch, v_w_out, v_ln1_g, v_ln1_b, v_ffn_w_up, v_ffn_conv_w, v_ffn_conv_b, v_ffn_w_down, v_ln2_g, v_ln2_b, v_ple_w_proj, v_ple_w_gate):
    weights = dict(w_in=w_in, sgu_w_s=sgu_w_s, sgu_b_s=sgu_b_s, sgu_norm_g=sgu_norm_g, sgu_norm_b=sgu_norm_b,
                   hgrn_lb_logits=hgrn_lb_logits, hgrn_norm_g=hgrn_norm_g, w_branch=w_branch, w_out=w_out,
                   ln1_g=ln1_g, ln1_b=ln1_b, ffn_w_up=ffn_w_up, ffn_conv_w=ffn_conv_w, ffn_conv_b=ffn_conv_b,
                   ffn_w_down=ffn_w_down, ln2_g=ln2_g, ln2_b=ln2_b, ple_w_proj=ple_w_proj, ple_w_gate=ple_w_gate)
    mom_m = dict(w_in=m_w_in, sgu_w_s=m_sgu_w_s, sgu_b_s=m_sgu_b_s, sgu_norm_g=m_sgu_norm_g, sgu_norm_b=m_sgu_norm_b,
                 hgrn_lb_logits=m_hgrn_lb_logits, hgrn_norm_g=m_hgrn_norm_g, w_branch=m_w_branch, w_out=m_w_out,
                 ln1_g=m_ln1_g, ln1_b=m_ln1_b, ffn_w_up=m_ffn_w_up, ffn_conv_w=m_ffn_conv_w, ffn_conv_b=m_ffn_conv_b,
                 ffn_w_down=m_ffn_w_down, ln2_g=m_ln2_g, ln2_b=m_ln2_b, ple_w_proj=m_ple_w_proj, ple_w_gate=m_ple_w_gate)
    mom_v = dict(w_in=v_w_in, sgu_w_s=v_sgu_w_s, sgu_b_s=v_sgu_b_s, sgu_norm_g=v_sgu_norm_g, sgu_norm_b=v_sgu_norm_b,
                 hgrn_lb_logits=v_hgrn_lb_logits, hgrn_norm_g=v_hgrn_norm_g, w_branch=v_w_branch, w_out=v_w_out,
                 ln1_g=v_ln1_g, ln1_b=v_ln1_b, ffn_w_up=v_ffn_w_up, ffn_conv_w=v_ffn_conv_w, ffn_conv_b=v_ffn_conv_b,
                 ffn_w_down=v_ffn_w_down, ln2_g=v_ln2_g, ln2_b=v_ln2_b, ple_w_proj=v_ple_w_proj, ple_w_gate=v_ple_w_gate)
    d, f = D_MODEL, D_FF
    me = _my_index()

    comm = _StepExchanges(w_in[0].astype(BF16),
                          [w_branch[0].astype(BF16), w_out[0].astype(BF16), ffn_w_up[0].astype(BF16),
                           ffn_w_down[0].astype(BF16), ple_w_proj[0].astype(BF16), ple_w_gate[0].astype(BF16), ffn_conv_w[0]])
    small = dict(sgu_w_s=sgu_w_s[0], sgu_b_s=sgu_b_s[0], sgu_norm_g=sgu_norm_g, sgu_norm_b=sgu_norm_b,
                 lb_logits=hgrn_lb_logits, hgrn_norm_g=hgrn_norm_g, ln1_g=ln1_g, ln1_b=ln1_b, ln2_g=ln2_g, ln2_b=ln2_b,
                 conv_b=ffn_conv_b)
    grad_x = _local_step(x[0], p[0, 0], loss_target[0], small, comm)

    out = {}

    def adam(name, parts8, shape2d):
        w2, m2, v2 = (a.reshape(shape2d) for a in (weights[name], mom_m[name], mom_v[name]))
        res = _adam_call(w2, m2, v2, parts8.reshape(parts8.shape[:1] + shape2d), "adam_" + name)
        out[name] = tuple(r.reshape(weights[name].shape) for r in res)

    adam("w_in", comm.recv_in, (d, d))
    adam("w_branch", comm.recv[0], (256, d))
    adam("w_out", comm.recv[1], (128, d))
    adam("ffn_w_up", comm.recv[2], (d, UP_COLS))
    adam("ffn_w_down", comm.recv[3], (N_TAPS, d))
    adam("ple_w_proj", comm.recv[4], (256, 128))
    adam("ple_w_gate", comm.recv[5], (128, d))

    def adam_small(names, extra_w, extra_m, extra_v, extra_shapes, parts8, rows, label):
        pk = lambda src, extra: _pack([src[n] for n in names] + extra)[0]
        res = _adam_call(pk(weights, extra_w), pk(mom_m, extra_m), pk(mom_v, extra_v), parts8, label)
        shapes = [weights[n].shape for n in names] + extra_shapes
        unpacked = [_unpack(r, shapes, rows) for r in res]
        for j, n in enumerate(names):
            out[n] = tuple(u[j] for u in unpacked)
        return [[u[len(names) + j] for u in unpacked] for j in range(len(extra_shapes))]

    blank = jnp.zeros((8, LANE), F32)
    taps, loss_rows = adam_small(
        _SMALL_EARLY, [_place_taps(ffn_conv_w[0], me, f), blank], [_place_taps(m_ffn_conv_w[0], me, f), blank],
        [_place_taps(v_ffn_conv_w[0], me, f), blank + 1.0], [(3, f), (8, LANE)], comm.small_early, comm.rows_early,
        "adam_small_early")
    adam_small(_SMALL_LATE, [], [], [], [], comm.small_late, comm.rows_late, "adam_small_late")
    out["ffn_conv_w"] = tuple(lax.dynamic_slice_in_dim(u, me * N_TAPS, N_TAPS, axis=1)[None] for u in taps)
    loss = loss_rows[0][0, 0]

    order = ["w_in", "sgu_w_s", "sgu_b_s", "sgu_norm_g", "sgu_norm_b", "hgrn_lb_logits", "hgrn_norm_g", "w_branch", "w_out",
             "ln1_g", "ln1_b", "ffn_w_up", "ffn_conv_w", "ffn_conv_b", "ffn_w_down", "ln2_g", "ln2_b", "ple_w_proj", "ple_w_gate"]
    return (loss, grad_x[None], *[out[n][0] for n in order], *[out[n][1] for n in order],
            *[out[n][2] for n in order], *[out[n][3] for n in order])


def _place_taps(shard, me, f):
    return lax.dynamic_update_slice_in_dim(jnp.zeros((3, f), F32), shard, me * N_TAPS, axis=1)
```

```python
import functools
import math

import jax
import jax.numpy as jnp
from jax import lax
from jax.experimental import pallas as pl
from jax.experimental.pallas import tpu as pltpu

F32 = jnp.float32
BF16 = jnp.bfloat16

N_DEV = 8
N_CHIP = 4
D_MODEL = 1024
CHUNK = 64
SUB = 16
SGU_BLOCK = 128
GROUPS = 8
HEAD = 128
HEADS = 8
FWD_CHUNKS_PER_STEP = 2
HEAD_UNROLL = 8
D_FF = 2816
LN_EPS = 1e-5
RMS_EPS = 1e-6
ALPHA = 2.0 ** 0.25
GELU_K = math.sqrt(2.0 / math.pi)
GELU_C = 0.044715
NEG = -1e30
ADAM_LR, ADAM_B1, ADAM_B2, ADAM_EPS, ADAM_WD, ADAM_STEP = 0.001, 0.9, 0.999, 1e-08, 0.01, 10
LANE = 128
SLOT_Q, SLOT_F, SLOT_I, SLOT_OG, SLOT_U, SLOT_V, SLOT_GA, SLOT_GB = range(8)


def _slot_of_group(k):
    return jnp.where(k < 2, k + 4, jnp.where(k < 6, k - 2, k))


MIB = 1024 * 1024
VMEM_V7X = 64 * MIB
VMEM_FLOOR = 32 * MIB
MM_TILES = (1024, 1408, 512, 256, 128)


def _params(sem, need=0):
    limit = min(max(need + need // 4, VMEM_FLOOR), VMEM_V7X - 4 * MIB)
    return pltpu.CompilerParams(dimension_semantics=sem, vmem_limit_bytes=limit)


def _pick(n, prefs):
    for t in prefs:
        if n % t == 0:
            return t
    return n


def _gelu(x):
    return 0.5 * x * (1.0 + jnp.tanh(GELU_K * (x + GELU_C * x * x * x)))


def _gelu_and_grad(x):
    x2 = x * x
    t = jnp.tanh(GELU_K * x * (1.0 + GELU_C * x2))
    half = 0.5 * (1.0 + t)
    return x * half, half + 0.5 * x * (1.0 - t * t) * GELU_K * (1.0 + 3.0 * GELU_C * x2)


def _silu_grad(x, s):
    return s * (1.0 + x * (1.0 - s))


def _dot(a, b):
    return jnp.dot(a.astype(BF16), b.astype(BF16), preferred_element_type=F32)


def _dot_nt(a, b):
    return lax.dot_general(a.astype(BF16), b.astype(BF16), (((1,), (1,)), ((), ())), preferred_element_type=F32)


def _dot_tn(a, b):
    return lax.dot_general(a.astype(BF16), b.astype(BF16), (((0,), (0,)), ((), ())), preferred_element_type=F32)


def _mean(x):
    return jnp.mean(x, axis=-1, keepdims=True)


def _sum0(x):
    return jnp.sum(x, axis=0, keepdims=True)


def _mm(a, b, *, out_dtype, name, trans_b=False, reduce_b=False, adds=(), ex=None):
    squeeze = b.ndim == 2
    a3 = a if a.ndim == 3 else a[None]
    b3 = b if b.ndim == 3 else b[None]
    ba, m, k = a3.shape
    bb = b3.shape[0]
    n = b3.shape[1] if trans_b else b3.shape[2]
    tm = _pick(m, MM_TILES)
    tn = _pick(n, MM_TILES)
    if reduce_b:
        bo, steps = 1, bb
        a_map = lambda o, i, j, r: (r if ba > 1 else 0, i, 0)
        b_map = (lambda o, i, j, r: (r, j, 0)) if trans_b else (lambda o, i, j, r: (r, 0, j))
    else:
        bo, steps = bb, 1
        a_map = lambda o, i, j, r: (o if ba > 1 else 0, i, 0)
        b_map = (lambda o, i, j, r: (o, j, 0)) if trans_b else (lambda o, i, j, r: (o, 0, j))
    o_map = lambda o, i, j, r: (o, i, j)
    add_arrays = [x if x.ndim == 3 else x[None] for x, _ in adds]
    add_scales = [s for _, s in adds]
    n_add = len(adds)
    dot = _dot_nt if trans_b else _dot

    def finish(acc, add_refs, o_ref):
        for ref, s in zip(add_refs, add_scales):
            acc = acc + s * ref[...].astype(F32)
        o_ref[...] = acc.astype(o_ref.dtype)

    grid = (bo, m // tm, n // tn, steps)

    def body(*refs):
        ins, (o_ref,), scratch, xrefs = _split_refs(refs, 2 + n_add, 1, 1 if reduce_b else 0, ex)
        a_ref, b_ref, add_refs = ins[0], ins[1], ins[2:]
        step = ((pl.program_id(0) * grid[1] + pl.program_id(1)) * grid[2] + pl.program_id(2)) * grid[3] + pl.program_id(3)
        if ex:
            @pl.when(step == 0)
            def _():
                ex.start(*xrefs)

        if reduce_b:
            acc, = scratch
            r = pl.program_id(3)

            @pl.when(r == 0)
            def _():
                acc[...] = jnp.zeros_like(acc)

            acc[...] += dot(a_ref[...], b_ref[...])

            @pl.when(r == steps - 1)
            def _():
                finish(acc[...], add_refs, o_ref)
        else:
            finish(dot(a_ref[...], b_ref[...]), add_refs, o_ref)

        if ex:
            @pl.when(step == math.prod(grid) - 1)
            def _():
                ex.finish(*xrefs)

    b_block = (None, tn, k) if trans_b else (None, k, tn)
    out_bytes = tm * tn * jnp.dtype(out_dtype).itemsize
    need = 2 * (tm * k * a3.dtype.itemsize + k * tn * b3.dtype.itemsize + out_bytes + n_add * tm * tn * 4)
    need += 2 * tm * tn * 4
    sem = ("arbitrary",) * 4 if ex else ("parallel", "parallel", "parallel", "arbitrary")
    res = pl.pallas_call(
        body,
        grid=grid,
        in_specs=[pl.BlockSpec((None, tm, k), a_map), pl.BlockSpec(b_block, b_map)]
        + [pl.BlockSpec((None, tm, tn), o_map) for _ in adds] + (ex.in_specs if ex else []),
        out_specs=[pl.BlockSpec((None, tm, tn), o_map)] + (ex.out_specs if ex else []),
        out_shape=[jax.ShapeDtypeStruct((bo, m, n), out_dtype)] + (ex.out_shape if ex else []),
        scratch_shapes=([pltpu.VMEM((tm, tn), F32)] if reduce_b else []) + (ex.scratch if ex else []),
        compiler_params=_params(sem, need),
        name=name,
    )(a3, b3, *add_arrays, *(ex.arrays if ex else []))
    out = res[0][0] if (reduce_b or squeeze) else res[0]
    return (out, res[1:]) if ex else out


def _mm_tn(a, b, *, out_dtype, name):
    squeeze = b.ndim == 2
    b3 = b if b.ndim == 3 else b[None]
    t, m = a.shape
    bb, _, n = b3.shape
    tm = _pick(m, MM_TILES)
    tn = _pick(n, MM_TILES)
    tt = _pick(t, (1024, 512, 256, 128))
    steps = t // tt
    need = 2 * (tt * tm * a.dtype.itemsize + tt * tn * b3.dtype.itemsize + tm * tn * jnp.dtype(out_dtype).itemsize)
    need += 2 * tm * tn * 4

    def body(a_ref, b_ref, o_ref, acc):
        r = pl.program_id(3)

        @pl.when(r == 0)
        def _():
            acc[...] = jnp.zeros_like(acc)

        acc[...] += _dot_tn(a_ref[...], b_ref[...])

        @pl.when(r == steps - 1)
        def _():
            o_ref[...] = acc[...].astype(o_ref.dtype)

    out = pl.pallas_call(
        body,
        grid=(bb, m // tm, n // tn, steps),
        in_specs=[pl.BlockSpec((tt, tm), lambda o, i, j, r: (r, i)),
                  pl.BlockSpec((None, tt, tn), lambda o, i, j, r: (o, r, j))],
        out_specs=pl.BlockSpec((None, tm, tn), lambda o, i, j, r: (o, i, j)),
        out_shape=jax.ShapeDtypeStruct((bb, m, n), out_dtype),
        scratch_shapes=[pltpu.VMEM((tm, tn), F32)],
        compiler_params=_params(("parallel", "parallel", "parallel", "arbitrary"), need),
        name=name,
    )(a, b3)
    return out[0] if squeeze else out


def _rowwise(fn, rows, consts, row_outs, acc_outs, *, tile, name):
    first = rows[0][0] if isinstance(rows[0], tuple) else rows[0]
    t = first.shape[-2]
    steps = t // tile
    arrays, in_specs = [], []
    for r in rows:
        if isinstance(r, tuple) and isinstance(r[1], pl.BlockSpec):
            arrays.append(r[0])
            in_specs.append(r[1])
        elif isinstance(r, tuple):
            arr, bidx = r
            arrays.append(arr)
            in_specs.append(pl.BlockSpec((None, tile, arr.shape[-1]), functools.partial(lambda i, b: (b, i, 0), b=bidx)))
        else:
            arrays.append(r)
            in_specs.append(pl.BlockSpec((tile, r.shape[-1]), lambda i: (i, 0)))
    for c in consts:
        arrays.append(c)
        in_specs.append(pl.BlockSpec(c.shape, lambda i: (0, 0)))
    n_in, n_row = len(arrays), len(row_outs)
    out_shape, out_specs = [], []
    for ro in row_outs:
        if ro[0] == "stack":
            _, cnt, blk, total, w, dt = ro
            out_shape.append(jax.ShapeDtypeStruct((total, t, w), dt))
            out_specs.append(pl.BlockSpec((cnt, tile, w), functools.partial(lambda i, b: (b, i, 0), b=blk)))
        else:
            w, dt = ro
            out_shape.append(jax.ShapeDtypeStruct((t, w), dt))
            out_specs.append(pl.BlockSpec((tile, w), lambda i: (i, 0)))
    out_shape += [jax.ShapeDtypeStruct(s, F32) for s in acc_outs]
    out_specs += [pl.BlockSpec(s, lambda i: (0, 0)) for s in acc_outs]
    blocks = [math.prod(d for d in sp.block_shape if d) * arr.dtype.itemsize for sp, arr in zip(in_specs, arrays)]
    blocks += [math.prod(d for d in sp.block_shape if d) * jnp.dtype(sh.dtype).itemsize
               for sp, sh in zip(out_specs, out_shape)]
    need = 2 * sum(blocks) + 6 * tile * max(a.shape[-1] for a in arrays) * 4

    def body(*refs):
        ins, outs = refs[:n_in], refs[n_in:]
        i = pl.program_id(0)
        res = fn(i, steps, *[r[...] for r in ins])
        res = res if isinstance(res, (tuple, list)) else (res,)
        for ref, val in zip(outs[:n_row], res[:n_row]):
            ref[...] = val.astype(ref.dtype)
        if acc_outs:
            @pl.when(i == 0)
            def _():
                for ref in outs[n_row:]:
                    ref[...] = jnp.zeros_like(ref)

            for ref, val in zip(outs[n_row:], res[n_row:]):
                ref[...] += val

    return pl.pallas_call(
        body,
        grid=(steps,),
        in_specs=in_specs,
        out_specs=out_specs,
        out_shape=out_shape,
        compiler_params=_params(("arbitrary",), need),
        name=name,
    )(*arrays)


def _ln_stats(z):
    mu = _mean(z)
    zc = z - mu
    rstd = lax.rsqrt(_mean(zc * zc) + LN_EPS)
    return zc * rstd, rstd


def _ln_bwd(dy, xhat, rstd, g):
    dxh = dy * g
    return rstd * (dxh - _mean(dxh) - xhat * _mean(dxh * xhat))


def _ride(ex, xrefs, step, steps):
    if not ex:
        return
    for at, act in ((0, ex.start), (steps - 2, ex.relay), (steps - 1, ex.finish)):
        @pl.when(step == at)
        def _(act=act):
            act(*xrefs)


def _sgu_fwd(h, wm, bs_t, g_v, b_v, ex=None):
    t = h.shape[1]
    steps = t // SGU_BLOCK

    def body(*refs):
        (u_ref, v_ref, wm_ref, bs_ref, g_ref, b_ref), (y_ref,), _, xrefs = _split_refs(refs, 6, 1, 0, ex)
        xhat, _ = _ln_stats(_gelu(v_ref[...]))
        vn = (xhat * g_ref[...] + b_ref[...]).astype(BF16)
        gu = _gelu(u_ref[...])
        for g in range(GROUPS):
            sl = slice(g * HEAD, (g + 1) * HEAD)
            mixed = _dot(wm_ref[g], vn[:, sl]) + bs_ref[:, g:g + 1]
            y_ref[:, sl] = (gu[:, sl] * mixed).astype(BF16)
        _ride(ex, xrefs, pl.program_id(0), steps)

    blk = lambda b: pl.BlockSpec((None, SGU_BLOCK, D_MODEL), functools.partial(lambda i, b: (b, i, 0), b=b))
    whole = lambda s: pl.BlockSpec(s, lambda i: (0,) * len(s))
    res = pl.pallas_call(
        body,
        grid=(steps,),
        in_specs=[blk(SLOT_U), blk(SLOT_V), whole(wm.shape), whole(bs_t.shape), whole(g_v.shape), whole(b_v.shape)]
        + (ex.in_specs if ex else []),
        out_specs=[pl.BlockSpec((SGU_BLOCK, D_MODEL), lambda i: (i, 0))] + (ex.out_specs if ex else []),
        out_shape=[jax.ShapeDtypeStruct((t, D_MODEL), BF16)] + (ex.out_shape if ex else []),
        scratch_shapes=ex.scratch if ex else [],
        compiler_params=_params(("arbitrary",)),
        name="sgu_fwd",
    )(h, h, wm, bs_t, g_v, b_v, *(ex.arrays if ex else []))
    return res[0], res[1:]


def _sgu_bwd(h, dy, dh, wm, wm_t, bs_t, g_v, b_v):
    t = h.shape[1]

    def body(u_ref, v_ref, dy_ref, dh_in, wm_ref, wmt_ref, bs_ref, g_ref, b_ref,
             duv_ref, dw_ref, dbs_ref, dg_ref, db_ref, dvn_ref):
        del dh_in
        du_ref, dv_ref = duv_ref.at[0], duv_ref.at[1]
        i = pl.program_id(0)

        @pl.when(i == 0)
        def _():
            dw_ref[...] = jnp.zeros_like(dw_ref)
            dbs_ref[...] = jnp.zeros_like(dbs_ref)
            dg_ref[...] = jnp.zeros_like(dg_ref)
            db_ref[...] = jnp.zeros_like(db_ref)

        u = u_ref[...]
        v = v_ref[...]
        gv, gvp = _gelu_and_grad(v)
        xhat, rstd = _ln_stats(gv)
        vn = (xhat * g_ref[...] + b_ref[...]).astype(BF16)
        gu, gup = _gelu_and_grad(u)
        lane = lax.broadcasted_iota(jnp.int32, (SGU_BLOCK, LANE), 1)
        dbs = jnp.zeros((SGU_BLOCK, LANE), F32)
        for g in range(GROUPS):
            sl = slice(g * HEAD, (g + 1) * HEAD)
            vn_g = vn[:, sl]
            mixed = _dot(wm_ref[g], vn_g) + bs_ref[:, g:g + 1]
            dy_g = dy_ref[:, sl]
            du_ref[:, sl] = (dy_g * mixed * gup[:, sl]).astype(BF16)
            dmix = dy_g * gu[:, sl]
            dmb = dmix.astype(BF16)
            dvn_ref[:, sl] = _dot(wmt_ref[g], dmb)
            dw_ref[g] += _dot_nt(dmb, vn_g)
            dbs = dbs + jnp.where(lane == g, jnp.sum(dmix, axis=1, keepdims=True), 0.0)
        dbs_ref[...] += dbs
        dvn = dvn_ref[...]
        dg_ref[...] += _sum0(dvn * xhat)
        db_ref[...] += _sum0(dvn)
        dv_ref[...] = (_ln_bwd(dvn, xhat, rstd, g_ref[...]) * gvp).astype(BF16)

    blk = lambda b: pl.BlockSpec((None, SGU_BLOCK, D_MODEL), functools.partial(lambda i, b: (b, i, 0), b=b))
    row = pl.BlockSpec((SGU_BLOCK, D_MODEL), lambda i: (i, 0))
    whole = lambda s: pl.BlockSpec(s, lambda i: (0,) * len(s))
    vec = (1, D_MODEL)
    return pl.pallas_call(
        body,
        grid=(t // SGU_BLOCK,),
        in_specs=[blk(SLOT_U), blk(SLOT_V), row, pl.BlockSpec(memory_space=pl.ANY),
                  whole(wm.shape), whole(wm_t.shape), whole(bs_t.shape), whole(vec), whole(vec)],
        out_specs=[pl.BlockSpec((2, SGU_BLOCK, D_MODEL), lambda i: (SLOT_U // 2, i, 0)),
                   whole(wm.shape), whole((SGU_BLOCK, LANE)), whole(vec), whole(vec)],
        out_shape=[jax.ShapeDtypeStruct(dh.shape, BF16),
                   jax.ShapeDtypeStruct(wm.shape, F32), jax.ShapeDtypeStruct((SGU_BLOCK, LANE), F32),
                   jax.ShapeDtypeStruct(vec, F32), jax.ShapeDtypeStruct(vec, F32)],
        scratch_shapes=[pltpu.VMEM((SGU_BLOCK, D_MODEL), F32)],
        input_output_aliases={3: 0},
        compiler_params=_params(("arbitrary",)),
        name="sgu_bwd",
    )(h, h, dy, dh, wm, wm_t, bs_t, g_v, b_v)


def _split3(x):
    hi = x.astype(BF16)
    r1 = x - hi.astype(F32)
    mid = r1.astype(BF16)
    lo = (r1 - mid.astype(F32)).astype(BF16)
    return hi, mid, lo


def _tri_matmul(tri, x):
    hi, mid, lo = _split3(x)
    dot = lambda p: jnp.dot(tri, p, preferred_element_type=F32)
    return dot(hi) + dot(mid) + dot(lo)


def _lower_bound(logits):
    l0, l1 = logits[0:1, :], logits[1:2, :]
    mx = jnp.maximum(l0, l1)
    e0, e1 = jnp.exp(l0 - mx), jnp.exp(l1 - mx)
    return e0 / (e0 + e1)


def _hgrn_gates(q_raw, f_raw, lb):
    q = q_raw * jax.nn.sigmoid(q_raw)
    sig = jax.nn.sigmoid(f_raw)
    f = lb + (1.0 - lb) * sig
    row = lax.broadcasted_iota(jnp.int32, (CHUNK, CHUNK), 0)
    col = lax.broadcasted_iota(jnp.int32, (CHUNK, CHUNK), 1)
    c = _tri_matmul((row >= col).astype(BF16), jnp.log(f))
    return q, sig, f, 1.0 - f, c


def _offdiag_terms(qh, kh, ch, tb):
    rows = slice(tb * SUB, (tb + 1) * SUB)
    r = ch[tb * SUB - 1:tb * SUB, :]
    eqh = jnp.exp(ch[rows] - r)
    ekh = jnp.exp(jnp.minimum(r - ch, 0.0))
    return rows, eqh, qh[rows] * eqh, ekh, kh * ekh


def _diag_decay(cb, s, trow):
    return jnp.exp(jnp.where(trow >= s, cb - cb[s:s + 1, :], NEG))


def _split_refs(refs, n_in, n_out, n_scratch, ex):
    nx = ex.n if ex else 0
    ins, refs = refs[:n_in], refs[n_in:]
    xsrc, refs = refs[:nx], refs[nx:]
    outs, refs = refs[:n_out], refs[n_out:]
    xout, refs = refs[:nx], refs[nx:]
    return ins, outs, refs[:n_scratch], (xsrc, xout, refs[n_scratch:])


def _hgrn_fwd(h, logits, g_norm, ex=None):
    t = h.shape[1]
    nc = t // CHUNK
    per_step = FWD_CHUNKS_PER_STEP if nc % FWD_CHUNKS_PER_STEP == 0 else 1
    steps = nc // per_step

    def body(*refs):
        ins, outs, scratch, xrefs = _split_refs(refs, 3, 3, 4, ex)
        hgrn_ref, lg_ref, gn_ref = ins
        y_ref, o_ref, sall_ref = outs
        st_ref, q_s, k_s, c_s = scratch

        @pl.when(pl.program_id(0) == 0)
        def _():
            st_ref[...] = jnp.zeros_like(st_ref)

        lb = _lower_bound(lg_ref[...])
        for sub in range(per_step):
            rows = pl.ds(sub * CHUNK, CHUNK)
            chunk(lb, sall_ref.at[sub], *(r.at[rows] for r in (hgrn_ref.at[SLOT_Q], hgrn_ref.at[SLOT_F], hgrn_ref.at[SLOT_I],
                                                               hgrn_ref.at[SLOT_OG], y_ref, o_ref)),
                  gn_ref, st_ref, q_s, k_s, c_s)
        _ride(ex, xrefs, pl.program_id(0), steps)

    def chunk(lb, sall_ref, q_ref, f_ref, i_ref, og_ref, y_ref, o_ref, gn_ref, st_ref, q_s, k_s, c_s):
        q, _, _, k, c = _hgrn_gates(q_ref[...], f_ref[...], lb)
        q_s[...] = q
        k_s[...] = k
        c_s[...] = c
        col64 = lax.broadcasted_iota(jnp.int32, (SUB, CHUNK), 1)
        trow = lax.broadcasted_iota(jnp.int32, (SUB, HEAD), 0)

        def head(hd, carry):
            sl = pl.ds(pl.multiple_of(hd * HEAD, HEAD), HEAD)
            qh, kh, ch, ih = q_s[:, sl], k_s[:, sl], c_s[:, sl], i_ref[:, sl]
            st = st_ref[hd]
            sall_ref[hd] = st
            c_last = ch[CHUNK - 1:CHUNK, :]
            o = _dot_nt(qh * jnp.exp(ch), st)
            st_ref[hd] = st * jnp.exp(c_last) + _dot_tn(ih, kh * jnp.exp(c_last - ch))
            a_rows = [jnp.zeros((SUB, CHUNK), F32)]
            for tb in range(1, CHUNK // SUB):
                _, _, q_hat, _, k_hat = _offdiag_terms(qh, kh, ch, tb)
                a_rows.append(jnp.where(col64 < tb * SUB, _dot_nt(q_hat, k_hat), 0.0))
            o = o + _dot(jnp.concatenate(a_rows, axis=0), ih)
            o_rows = []
            for b in range(CHUNK // SUB):
                rows = slice(b * SUB, (b + 1) * SUB)
                qb, cb, kb, ib = qh[rows], ch[rows], kh[rows], ih[rows]
                ob = jnp.zeros((SUB, HEAD), F32)
                for s in range(SUB):
                    a = jnp.sum(qb * _diag_decay(cb, s, trow) * kb[s:s + 1, :], axis=1, keepdims=True)
                    ob = ob + a * ib[s:s + 1, :]
                o_rows.append(ob)
            o = o + jnp.concatenate(o_rows, axis=0)
            o_ref[:, sl] = o
            og = og_ref[:, sl]
            on = o * lax.rsqrt(_mean(o * o) + RMS_EPS)
            y_ref[:, sl] = (on * gn_ref[:, sl] * (og * jax.nn.sigmoid(og))).astype(BF16)
            return carry

        lax.fori_loop(0, HEADS, head, 0, unroll=HEAD_UNROLL)

    rows = per_step * CHUNK
    row = pl.BlockSpec((rows, D_MODEL), lambda n: (n, 0))
    whole = lambda s: pl.BlockSpec(s, lambda n: (0,) * len(s))
    res = pl.pallas_call(
        body,
        grid=(steps,),
        in_specs=[pl.BlockSpec((4, rows, D_MODEL), lambda n: (SLOT_Q // 4, n, 0)), whole(logits.shape), whole(g_norm.shape)]
        + (ex.in_specs if ex else []),
        out_specs=[row, row, pl.BlockSpec((per_step, HEADS, HEAD, HEAD), lambda n: (n, 0, 0, 0))]
        + (ex.out_specs if ex else []),
        out_shape=[jax.ShapeDtypeStruct((t, D_MODEL), BF16), jax.ShapeDtypeStruct((t, D_MODEL), F32),
                   jax.ShapeDtypeStruct((nc, HEADS, HEAD, HEAD), F32)] + (ex.out_shape if ex else []),
        scratch_shapes=[pltpu.VMEM((HEADS, HEAD, HEAD), F32)] + [pltpu.VMEM((CHUNK, D_MODEL), F32)] * 3
        + (ex.scratch if ex else []),
        compiler_params=_params(("arbitrary",)),
        name="hgrn_fwd",
    )(h, logits, g_norm, *(ex.arrays if ex else []))
    return res[0], res[1], res[2], res[3:]


def _hgrn_bwd(h, o_all, dy, states, dh, logits, g_norm, ex=None):
    t = h.shape[1]
    nc = t // CHUNK

    def body(*refs):
        ins, outs, scratch, xrefs = _split_refs(refs, 7, 3, 8, ex)
        hgrn_ref, o_ref, dy_ref, sall_ref, _, lg_ref, gn_ref = ins
        q_ref, f_ref, i_ref, og_ref = (hgrn_ref.at[s] for s in (SLOT_Q, SLOT_F, SLOT_I, SLOT_OG))
        dqfio_ref, dlg_ref, dgn_ref = outs
        dst_ref, dlb_ref, q_s, k_s, c_s, dq_s, dk_s, dc_s = scratch
        dq_ref, df_ref, di_ref, dog_ref = (dqfio_ref.at[s] for s in (SLOT_Q, SLOT_F, SLOT_I, SLOT_OG))
        n = pl.program_id(0)

        @pl.when(n == 0)
        def _():
            dst_ref[...] = jnp.zeros_like(dst_ref)
            dlb_ref[...] = jnp.zeros_like(dlb_ref)
            dgn_ref[...] = jnp.zeros_like(dgn_ref)
            if ex:
                ex.start(*xrefs)

        lb = _lower_bound(lg_ref[...])
        q_raw = q_ref[...]
        q, sig, f, k, c = _hgrn_gates(q_raw, f_ref[...], lb)
        q_s[...] = q
        k_s[...] = k
        c_s[...] = c
        col64 = lax.broadcasted_iota(jnp.int32, (SUB, CHUNK), 1)
        trow = lax.broadcasted_iota(jnp.int32, (SUB, HEAD), 0)
        row64 = lax.broadcasted_iota(jnp.int32, (CHUNK, HEAD), 0)

        def head(hd, carry):
            sl = pl.ds(pl.multiple_of(hd * HEAD, HEAD), HEAD)
            qh, kh, ch, ih = q_s[:, sl], k_s[:, sl], c_s[:, sl], i_ref[:, sl]
            st = sall_ref[hd]
            dst = dst_ref[hd]
            oh, dyh, og, gn = o_ref[:, sl], dy_ref[:, sl], og_ref[:, sl], gn_ref[:, sl]
            sg = jax.nn.sigmoid(og)
            sil = og * sg
            rms = lax.rsqrt(_mean(oh * oh) + RMS_EPS)
            on = oh * rms
            dog_ref[:, sl] = (dyh * on * gn * _silu_grad(og, sg)).astype(BF16)
            dgn_ref[:, sl] += _sum0(dyh * on * sil)
            don = dyh * gn * sil
            do = rms * (don - on * _mean(don * on))
            dob = do.astype(BF16)

            c_last = ch[CHUNK - 1:CHUNK, :]
            eq = jnp.exp(ch)
            q_til = qh * eq
            ekl = jnp.exp(c_last - ch)
            k_til = kh * ekl
            ecl = jnp.exp(c_last)
            dq_til = _dot(dob, st)
            dk_til = _dot(ih, dst)
            di = _dot_nt(k_til, dst)
            dc_last = _sum0(dk_til * k_til) + _sum0(dst * st) * ecl
            dst_ref[hd] = _dot_tn(dob, q_til) + dst * ecl
            dq = dq_til * eq
            dc = dq_til * q_til - dk_til * k_til
            dk = dk_til * ekl

            da_full = _dot_nt(dob, ih)
            a_rows = [jnp.zeros((SUB, CHUNK), F32)]
            dq_rows = [jnp.zeros((SUB, HEAD), F32)]
            dc_rows = [jnp.zeros((SUB, HEAD), F32)]
            for tb in range(1, CHUNK // SUB):
                rows, eqh, q_hat, ekh, k_hat = _offdiag_terms(qh, kh, ch, tb)
                keep = col64 < tb * SUB
                a_rows.append(jnp.where(keep, _dot_nt(q_hat, k_hat), 0.0))
                da = jnp.where(keep, da_full[rows], 0.0)
                dq_hat = _dot(da, k_hat)
                dk_hat = _dot_tn(da, q_hat)
                dq_rows.append(dq_hat * eqh)
                dc_rows.append(dq_hat * q_hat)
                dk = dk + dk_hat * ekh
                dc = dc - dk_hat * k_hat
            di = di + _dot_tn(jnp.concatenate(a_rows, axis=0), dob)

            dk_rows, di_rows = [], []
            for b in range(CHUNK // SUB):
                rows = slice(b * SUB, (b + 1) * SUB)
                qb, cb, kb, ib, dob_ = qh[rows], ch[rows], kh[rows], ih[rows], do[rows]
                dq_diag = jnp.zeros((SUB, HEAD), F32)
                dk_diag = jnp.zeros((SUB, HEAD), F32)
                di_diag = jnp.zeros((SUB, HEAD), F32)
                for s in range(SUB):
                    ks = kb[s:s + 1, :]
                    dec = _diag_decay(cb, s, trow)
                    a = jnp.sum(qb * dec * ks, axis=1, keepdims=True)
                    gk = jnp.sum(dob_ * ib[s:s + 1, :], axis=1, keepdims=True) * dec
                    dq_diag = dq_diag + gk * ks
                    dk_diag = dk_diag + jnp.where(trow == s, _sum0(gk * qb), 0.0)
                    di_diag = di_diag + jnp.where(trow == s, _sum0(a * dob_), 0.0)
                dq_rows[b] = dq_rows[b] + dq_diag
                dc_rows[b] = dc_rows[b] + qb * dq_diag - kb * dk_diag
                dk_rows.append(dk_diag)
                di_rows.append(di_diag)
            dq = dq + jnp.concatenate(dq_rows, axis=0)
            dk = dk + jnp.concatenate(dk_rows, axis=0)
            dc = dc + jnp.concatenate(dc_rows, axis=0) + jnp.where(row64 == CHUNK - 1, dc_last, 0.0)
            di_ref[:, sl] = (di + jnp.concatenate(di_rows, axis=0)).astype(BF16)
            dq_s[:, sl] = dq
            dk_s[:, sl] = dk
            dc_s[:, sl] = dc
            return carry

        lax.fori_loop(0, HEADS, head, 0, unroll=HEAD_UNROLL)

        row = lax.broadcasted_iota(jnp.int32, (CHUNK, CHUNK), 0)
        col = lax.broadcasted_iota(jnp.int32, (CHUNK, CHUNK), 1)
        dlf = _tri_matmul((row <= col).astype(BF16), dc_s[...])
        df = dlf / f - dk_s[...]
        dlb_ref[...] += _sum0(df * (1.0 - sig))
        df_ref[...] = (df * (1.0 - lb) * sig * (1.0 - sig)).astype(BF16)
        dq_ref[...] = (dq_s[...] * _silu_grad(q_raw, jax.nn.sigmoid(q_raw))).astype(BF16)

        @pl.when(n == nc - 1)
        def _():
            d0 = dlb_ref[...] * lb * (1.0 - lb)
            dlg_ref[0:1, :] = d0
            dlg_ref[1:2, :] = -d0
            if ex:
                ex.finish(*xrefs)

    rev = lambda n: nc - 1 - n
    slots = pl.BlockSpec((4, CHUNK, D_MODEL), lambda n: (SLOT_Q // 4, rev(n), 0))
    row = pl.BlockSpec((CHUNK, D_MODEL), lambda n: (rev(n), 0))
    whole = lambda s: pl.BlockSpec(s, lambda n: (0,) * len(s))
    vec = (1, D_MODEL)
    res = pl.pallas_call(
        body,
        grid=(nc,),
        in_specs=[slots, row, row, pl.BlockSpec((None, HEADS, HEAD, HEAD), lambda n: (rev(n), 0, 0, 0)),
                  pl.BlockSpec(memory_space=pl.ANY), whole(logits.shape), whole(vec)] + (ex.in_specs if ex else []),
        out_specs=[slots, whole((2, D_MODEL)), whole(vec)] + (ex.out_specs if ex else []),
        out_shape=[jax.ShapeDtypeStruct(dh.shape, BF16), jax.ShapeDtypeStruct((2, D_MODEL), F32),
                   jax.ShapeDtypeStruct(vec, F32)] + (ex.out_shape if ex else []),
        scratch_shapes=[pltpu.VMEM((HEADS, HEAD, HEAD), F32), pltpu.VMEM(vec, F32)]
        + [pltpu.VMEM((CHUNK, D_MODEL), F32)] * 6 + (ex.scratch if ex else []),
        input_output_aliases={4: 0},
        compiler_params=_params(("arbitrary",)),
        name="hgrn_bwd",
    )(h, o_all, dy, states, dh, logits, g_norm, *(ex.arrays if ex else []))
    return res[0], res[1], res[2], res[3:]


def _merge_fwd(i, n, ga, gb, za, zb):
    return jax.nn.sigmoid(ga) * za + jax.nn.sigmoid(gb) * zb


def _merge_bwd(i, n, ga, gb, za, zb, dm):
    sa, sb = jax.nn.sigmoid(ga), jax.nn.sigmoid(gb)
    dgates = jnp.stack([(dm * za * sa * (1.0 - sa)).astype(BF16), (dm * zb * sb * (1.0 - sb)).astype(BF16)])
    return dgates, dm * sa, dm * sb


def _ln1_fwd(i, n, x, r1, g, b):
    xhat, _ = _ln_stats(ALPHA * x + r1)
    x1 = xhat * g + b
    return x1, x1


def _ln1_bwd(i, n, x, r1, dx1, g):
    xhat, rstd = _ln_stats(ALPHA * x + r1)
    dz = _ln_bwd(dx1, xhat, rstd, g)
    return dz, dz, _sum0(dx1 * xhat), _sum0(dx1)


def _ln2_loss(i, n, x1, fo, pg, pp, tgt, g, b):
    sg = jax.nn.sigmoid(pg)
    xhat, rstd = _ln_stats(ALPHA * x1 + fo + sg * pp)
    diff = xhat * g + b - tgt
    loss = 0.5 * jnp.sum(_mean(diff * diff), axis=0, keepdims=True)
    dy = diff * (1.0 / D_MODEL)
    dz = _ln_bwd(dy, xhat, rstd, g)
    return (dz, dz, dz * pp * sg * (1.0 - sg), dz * sg,
            jnp.broadcast_to(loss, (8, LANE)), _sum0(dy * xhat), _sum0(dy))


def _shift_down(cur, halo, tile):
    row = lax.broadcasted_iota(jnp.int32, cur.shape, 0)
    m1 = jnp.where(row == 0, halo[7:8, :], pltpu.roll(cur, 1, 0))
    m2 = jnp.where(row == 0, halo[6:7, :], jnp.where(row == 1, halo[7:8, :], pltpu.roll(cur, 2, 0)))
    return m1, m2


def _shift_up(cur, halo, tile):
    row = lax.broadcasted_iota(jnp.int32, cur.shape, 0)
    p1 = jnp.where(row == tile - 1, halo[0:1, :], pltpu.roll(cur, tile - 1, 0))
    p2 = jnp.where(row == tile - 2, halo[0:1, :], jnp.where(row == tile - 1, halo[1:2, :], pltpu.roll(cur, tile - 2, 0)))
    return p1, p2


def _conv_pre(i, gate, halo, w, b, tile):
    halo = jnp.where(i == 0, 0.0, halo)
    m1, m2 = _shift_down(gate, halo, tile)
    return w[0:1, :] * m2 + w[1:2, :] * m1 + w[2:3, :] * gate + b, m1, m2


def _conv_fwd(tile, i, n, gate, halo, val, w, b):
    cg, _, _ = _conv_pre(i, gate, halo, w, b, tile)
    return _gelu(cg) * val


def _conv_bwd_a(tile, i, n, gate, halo, val, dhid, w, b):
    cg, m1, m2 = _conv_pre(i, gate, halo, w, b, tile)
    act, slope = _gelu_and_grad(cg)
    dcg = dhid * val * slope
    return dcg, dhid * act, _sum0(dcg * m2), _sum0(dcg * m1), _sum0(dcg * gate), _sum0(dcg)


def _conv_bwd_b(tile, i, n, dcg, halo, w):
    dcg = dcg.astype(F32)
    halo = jnp.where(i == n - 1, 0.0, halo.astype(F32))
    p1, p2 = _shift_up(dcg, halo, tile)
    return w[2:3, :] * dcg + w[1:2, :] * p1 + w[0:1, :] * p2


def _halo_spec(width, tile, t, nxt, rows=8):
    per = tile // rows
    last = t // rows - 1
    if nxt:
        return pl.BlockSpec((rows, width), lambda i: (jnp.minimum((i + 1) * per, last), 0))
    return pl.BlockSpec((rows, width), lambda i: (jnp.maximum(i * per - 1, 0), 0))


def _adamw(i, n, w, m, v, parts):
    g = parts[0].astype(F32)
    for j in range(1, parts.shape[0]):
        g = g + parts[j].astype(F32)
    m_new = ADAM_B1 * m + (1.0 - ADAM_B1) * g
    v_new = ADAM_B2 * v + (1.0 - ADAM_B2) * (g * g)
    m_hat = m_new / (1.0 - ADAM_B1 ** ADAM_STEP)
    v_hat = v_new / (1.0 - ADAM_B2 ** ADAM_STEP)
    delta = -ADAM_LR * (m_hat / (jnp.sqrt(v_hat) + ADAM_EPS) + ADAM_WD * w)
    return g, delta, m_new, v_new


def _adam_call(w, m, v, parts, name):
    r, c = w.shape
    tile = _pick(r, (256, 128)) if r > 256 else r
    spec = pl.BlockSpec((parts.shape[0], tile, c), lambda i: (0, i, 0))
    return _rowwise(_adamw, [w, m, v, (parts, spec)], [], [(c, F32)] * 4, [], tile=tile, name=name)


def _peer(k):
    x, y, c = lax.axis_index("x"), lax.axis_index("y"), lax.axis_index("c")
    px = x ^ ((k >> 2) & 1)
    py = y ^ ((k >> 1) & 1)
    pc = c ^ (k & 1)
    return (px, py, pc), 4 * px + 2 * py + pc


def _my_index():
    return 4 * lax.axis_index("x") + 2 * lax.axis_index("y") + lax.axis_index("c")


class _Exchange:
    KINDS = ("gather", "gather+relay", "scatter", "scatter+pairs")

    def __init__(self, entries):
        assert all(k in self.KINDS for _, k in entries), [k for _, k in entries]
        self.arrays = [a for a, _ in entries]
        self.scatter = [k.startswith("scatter") for _, k in entries]
        self.relayed = ["+relay" in k for _, k in entries]
        self.pairs = ["+pairs" in k for _, k in entries]
        self.n = len(entries)
        self.in_specs = [pl.BlockSpec(memory_space=pl.ANY)] * self.n
        self.out_specs = [pl.BlockSpec(memory_space=pl.ANY)] * self.n
        shapes = [tuple(a.shape[1:]) if sc else tuple(a.shape) for a, sc in zip(self.arrays, self.scatter)]
        counts = [N_CHIP if p else N_DEV for p in self.pairs]
        self.out_shape = [jax.ShapeDtypeStruct((n,) + s, a.dtype) for n, s, a in zip(counts, shapes, self.arrays)]
        per = N_DEV - 1
        self.scratch = [pltpu.SemaphoreType.DMA((self.n * per,)), pltpu.SemaphoreType.DMA((self.n * per,)),
                        pltpu.SemaphoreType.DMA((self.n,))]

    def _copies(self, srcs, outs, sems):
        send_sems, recv_sems, local_sems = sems
        x, y, c = lax.axis_index("x"), lax.axis_index("y"), lax.axis_index("c")
        me = _my_index()
        per = N_DEV - 1
        local, first, passed, relay_arrivals, arrivals = [], [], [], [], []
        for a in range(self.n):

            def copy(k, src, dst, dev, a=a):
                return pltpu.make_async_remote_copy(
                    src_ref=src, dst_ref=dst, send_sem=send_sems.at[a * per + k], recv_sem=recv_sems.at[a * per + k],
                    device_id=dev, device_id_type=pl.DeviceIdType.MESH)

            if self.pairs[a]:
                chip = 2 * x + y
                for k in range(N_CHIP):
                    to = chip ^ k
                    piece = srcs[a].at[_slot_of_group(2 * to + c) // 2]
                    if k == 0:
                        local.append(pltpu.make_async_copy(piece, outs[a].at[chip], local_sems.at[a]))
                    else:
                        dev = (to // 2, to % 2, c)
                        first.append(copy(k - 1, piece, outs[a].at[chip], dev))
                        arrivals.append(copy(k - 1, piece, outs[a].at[to], dev))
                continue
            mine = srcs[a].at[me] if self.scatter[a] else srcs[a]
            land = outs[a].at[me]
            local.append(pltpu.make_async_copy(mine, land, local_sems.at[a]))
            if self.relayed[a]:
                block = lambda px, py, pc, a=a: outs[a].at[4 * px + 2 * py + pc]
                chips = [(1 - x, y), (x, 1 - y), (1 - x, 1 - y)]
                first.append(copy(0, mine, land, (x, y, 1 - c)))
                arrivals.append(copy(0, mine, block(x, y, 1 - c), (x, y, 1 - c)))
                for j, (px, py) in enumerate(chips):
                    first.append(copy(1 + j, mine, land, (px, py, c)))
                    relay_arrivals.append(copy(1 + j, mine, block(px, py, c), (px, py, c)))
                    passed.append(copy(4 + j, block(px, py, c), block(px, py, c), (x, y, 1 - c)))
                    arrivals.append(copy(4 + j, mine, block(px, py, 1 - c), (x, y, 1 - c)))
                continue
            for k in range(1, N_DEV):
                dev, idx = _peer(k)
                if self.scatter[a]:
                    first.append(copy(k - 1, srcs[a].at[idx], land, dev))
                else:
                    first.append(copy(k - 1, mine, land, dev))
                arrivals.append(copy(k - 1, mine, outs[a].at[idx], dev))
        return local, first, passed, relay_arrivals, arrivals

    def start(self, srcs, outs, sems):
        local, first, _, _, _ = self._copies(srcs, outs, sems)
        for cp in local + first:
            cp.start()

    def relay(self, srcs, outs, sems):
        _, _, passed, relay_arrivals, _ = self._copies(srcs, outs, sems)
        for landed, onward in zip(relay_arrivals, passed):
            landed.wait_recv()
            onward.start()

    def finish(self, srcs, outs, sems):
        local, first, passed, _, arrivals = self._copies(srcs, outs, sems)
        for cp in arrivals:
            cp.wait_recv()
        for cp in first + passed:
            cp.wait_send()
        for cp in local:
            cp.wait()


def _gather_project(x_b, shard):
    t, d = x_b.shape
    tm = _pick(t, MM_TILES)
    nrow = t // tm
    per = N_DEV - 1

    def parties():
        x, y, c = lax.axis_index("x"), lax.axis_index("y"), lax.axis_index("c")
        chips = [(1 - x, y), (x, 1 - y), (1 - x, 1 - y)]
        return (x, y, c), (x, y, 1 - c), [(px, py, c) for px, py in chips], [(px, py, 1 - c) for px, py in chips]

    slot = lambda dev: _slot_of_group(4 * dev[0] + 2 * dev[1] + dev[2])
    ici_step, passed_step = (2, 3, 6), (4, 5, 7)
    me, sibling, over_ici, passed_on = parties()
    by_step = {0: me, 1: sibling, **dict(zip(ici_step, over_ici)), **dict(zip(passed_step, passed_on))}
    order = jnp.stack([slot(by_step[j]) for j in range(N_DEV)]).astype(jnp.int32)

    def body(order_ref, x_ref, shard_ref, h_ref, wall_ref, wbuf, fetch_sem, send_sems, recv_sems, local_sem):
        del order_ref
        me, sibling, over_ici, passed_on = parties()
        j, i = pl.program_id(0), pl.program_id(1)
        land = lambda dev: wall_ref.at[slot(dev)]

        def copy(k, src, block, to):
            return pltpu.make_async_remote_copy(src_ref=src, dst_ref=land(block), send_sem=send_sems.at[k],
                                                recv_sem=recv_sems.at[k], device_id=to, device_id_type=pl.DeviceIdType.MESH)

        def fetch(src):
            cp = pltpu.make_async_copy(src, wbuf, fetch_sem)
            cp.start()
            cp.wait()

        keep = pltpu.make_async_copy(shard_ref, land(me), local_sem)
        first = [copy(0, shard_ref, me, sibling)] + [copy(1 + n, shard_ref, me, dev) for n, dev in enumerate(over_ici)]
        onward = [copy(4 + n, land(dev), dev, sibling) for n, dev in enumerate(over_ici)]

        @pl.when(jnp.logical_and(i == 0, j == 0))
        def _():
            keep.start()
            for cp in first:
                cp.start()
            fetch(shard_ref)

        @pl.when(jnp.logical_and(i == 0, j == 1))
        def _():
            copy(0, shard_ref, sibling, me).wait_recv()
            fetch(land(sibling))

        for n, dev in enumerate(over_ici):
            @pl.when(jnp.logical_and(i == 0, j == ici_step[n]))
            def _(n=n, dev=dev):
                copy(1 + n, shard_ref, dev, me).wait_recv()
                onward[n].start()
                fetch(land(dev))

        for n, dev in enumerate(passed_on):
            @pl.when(jnp.logical_and(i == 0, j == passed_step[n]))
            def _(n=n, dev=dev):
                copy(4 + n, shard_ref, dev, me).wait_recv()
                fetch(land(dev))

        h_ref[...] = _dot(x_ref[...], wbuf[...])

        @pl.when(jnp.logical_and(i == nrow - 1, j == N_DEV - 1))
        def _():
            for cp in first + onward:
                cp.wait_send()
            keep.wait()

    need = 2 * (tm * d * 2 + tm * d * 4) + d * d * 2 + tm * d * 4
    h, w_all = pl.pallas_call(
        body,
        grid_spec=pltpu.PrefetchScalarGridSpec(
            num_scalar_prefetch=1,
            grid=(N_DEV, nrow),
            in_specs=[pl.BlockSpec((tm, d), lambda j, i, order: (i, 0)), pl.BlockSpec(memory_space=pl.ANY)],
            out_specs=[pl.BlockSpec((None, tm, d), lambda j, i, order: (order[j], i, 0)), pl.BlockSpec(memory_space=pl.ANY)],
            scratch_shapes=[pltpu.VMEM((d, d), BF16), pltpu.SemaphoreType.DMA, pltpu.SemaphoreType.DMA((per,)),
                            pltpu.SemaphoreType.DMA((per,)), pltpu.SemaphoreType.DMA],
        ),
        out_shape=[jax.ShapeDtypeStruct((N_DEV, t, d), F32), jax.ShapeDtypeStruct((N_DEV, d, d), BF16)],
        compiler_params=_params(("arbitrary", "arbitrary"), need),
        name="gather_project",
    )(order, x_b, shard)
    return h, w_all


def _pair_sums(g):
    n, r, cols = g.shape
    half = n // 2

    def swap(g_ref, got_ref, send_sems, recv_sems):
        x, y, c = lax.axis_index("x"), lax.axis_index("y"), lax.axis_index("c")
        copies = [pltpu.make_async_remote_copy(
            src_ref=g_ref.at[2 * j + 1 - c], dst_ref=got_ref.at[j], send_sem=send_sems.at[j], recv_sem=recv_sems.at[j],
            device_id=(x, y, 1 - c), device_id_type=pl.DeviceIdType.MESH) for j in range(half)]
        for cp in copies:
            cp.start()
        for cp in copies:
            cp.wait()

    got = pl.pallas_call(
        swap,
        in_specs=[pl.BlockSpec(memory_space=pl.ANY)],
        out_specs=pl.BlockSpec(memory_space=pl.ANY),
        out_shape=jax.ShapeDtypeStruct((half, r, cols), g.dtype),
        scratch_shapes=[pltpu.SemaphoreType.DMA((half,))] * 2,
        name="pair_swap",
    )(g)

    def add(mine_ref, got_ref, out_ref):
        mine = jnp.where(lax.axis_index("c") == 0, mine_ref[0].astype(F32), mine_ref[1].astype(F32))
        out_ref[...] = (mine + got_ref[...].astype(F32)).astype(out_ref.dtype)

    tile = _pick(r, (512, 256, 128))
    return pl.pallas_call(
        add,
        grid=(half, r // tile),
        in_specs=[pl.BlockSpec((None, 2, tile, cols), lambda j, i: (j, 0, i, 0)),
                  pl.BlockSpec((None, tile, cols), lambda j, i: (j, i, 0))],
        out_specs=pl.BlockSpec((None, tile, cols), lambda j, i: (j, i, 0)),
        out_shape=jax.ShapeDtypeStruct((half, r, cols), g.dtype),
        compiler_params=_params(("parallel", "parallel")),
        name="pair_add",
    )(g.reshape(half, 2, r, cols), got)


def _local_step(x, p, tgt, small, comm):
    t = x.shape[0]
    tile = _pick(t, (256, 128))
    d = D_MODEL
    act_b, act_f = (d, BF16), (d, F32)
    x_b, p_b = x.astype(BF16), p.astype(BF16)

    chunk_id = jnp.arange(SGU_BLOCK) // CHUNK
    mask = chunk_id[:, None] >= chunk_id[None, :]
    wm = jnp.where(mask[None], small["sgu_w_s"], 0.0)
    wm_b = wm.astype(BF16)
    wm_t = jnp.swapaxes(wm, 1, 2).astype(BF16)
    bs_t = small["sgu_b_s"].T

    h, w_in = comm.project_in(x_b)
    y_a, got_a = _sgu_fwd(h, wm_b, bs_t, small["sgu_norm_g"], small["sgu_norm_b"], ex=comm.weights_exchange(0))
    y_b, o_all, states, got_b = _hgrn_fwd(h, small["lb_logits"], small["hgrn_norm_g"], ex=comm.weights_exchange(1))
    wts, conv_w = comm.weights(list(got_a) + list(got_b))
    z_a = _mm(y_a, wts["w_a"], out_dtype=F32, name="mm_za")
    z_b = _mm(y_b, wts["w_b"], out_dtype=F32, name="mm_zb")
    gates = [(h, SLOT_GA), (h, SLOT_GB)]
    merged, = _rowwise(_merge_fwd, gates + [z_a, z_b], [], [act_b], [], tile=tile, name="merge_fwd")
    r1 = _mm(merged, wts["w_o"], out_dtype=F32, name="mm_r1")
    x1, x1_b = _rowwise(_ln1_fwd, [x, r1], [small["ln1_g"], small["ln1_b"]], [act_f, act_b], [], tile=tile, name="ln1_fwd")
    gate = _mm(x1_b, wts["w_g"], out_dtype=F32, name="mm_gate")
    val = _mm(x1_b, wts["w_v"], out_dtype=F32, name="mm_val")
    pg = _mm(x1_b, wts["w_pg"], out_dtype=F32, name="mm_pg")
    pp = _mm(p_b, wts["w_pp"], out_dtype=F32, name="mm_pp")
    hid, = _rowwise(functools.partial(_conv_fwd, tile), [gate, (gate, _halo_spec(D_FF, tile, t, False)), val],
                    [conv_w, small["conv_b"]], [(D_FF, BF16)], [], tile=tile, name="conv_fwd")
    fo = _mm(hid, wts["w_down"], out_dtype=F32, name="mm_down")
    dz2, dz2_b, dpg, dpp, loss, dg2, db2 = _rowwise(
        _ln2_loss, [x1, fo, pg, pp, tgt], [small["ln2_g"], small["ln2_b"]],
        [act_f, act_b, act_b, act_b], [(8, LANE), (1, d), (1, d)], tile=tile, name="ln2_loss")

    dhid = _mm(dz2_b, wts["w_down"], out_dtype=BF16, name="mm_dhid", trans_b=True)
    g_down = _mm_tn(hid, dz2_b, out_dtype=BF16, name="mm_g_down")
    dcg, dval, dcw0, dcw1, dcw2, dcb = _rowwise(
        functools.partial(_conv_bwd_a, tile), [gate, (gate, _halo_spec(D_FF, tile, t, False)), val, dhid],
        [conv_w, small["conv_b"]], [(D_FF, BF16), (D_FF, BF16)], [(1, D_FF)] * 4, tile=tile, name="conv_bwd_a")
    dgate, = _rowwise(functools.partial(_conv_bwd_b, tile), [dcg, (dcg, _halo_spec(D_FF, tile, t, True, rows=16))],
                      [conv_w], [(D_FF, BF16)], [], tile=tile, name="conv_bwd_b")
    g_g = _mm_tn(x1_b, dgate, out_dtype=BF16, name="mm_g_gate")
    g_v = _mm_tn(x1_b, dval, out_dtype=BF16, name="mm_g_val")
    g_pg = _mm_tn(x1_b, dpg, out_dtype=BF16, name="mm_g_pg")
    g_pp = _mm_tn(p_b, dpp, out_dtype=BF16, name="mm_g_pp")
    dx1 = _mm(dgate, wts["w_g"], out_dtype=F32, name="mm_dx1_gate", trans_b=True, adds=[(dz2, ALPHA)])
    dx1 = _mm(dval, wts["w_v"], out_dtype=F32, name="mm_dx1_val", trans_b=True, adds=[(dx1, 1.0)])
    dx1 = _mm(dpg, wts["w_pg"], out_dtype=F32, name="mm_dx1_pg", trans_b=True, adds=[(dx1, 1.0)])
    dz1, dz1_b, dg1, db1 = _rowwise(_ln1_bwd, [x, r1, dx1], [small["ln1_g"]], [act_f, act_b], [(1, d), (1, d)],
                                    tile=tile, name="ln1_bwd")
    g_o = _mm_tn(merged, dz1_b, out_dtype=BF16, name="mm_g_o")
    dm = _mm(dz1_b, wts["w_o"], out_dtype=BF16, name="mm_dm", trans_b=True)
    dh, dza, dzb = _rowwise(_merge_bwd, gates + [z_a, z_b, dm], [],
                            [("stack", 2, SLOT_GA // 2, 8, d, BF16), act_b, act_b], [], tile=tile, name="merge_bwd")
    g_a = _mm_tn(y_a, dza, out_dtype=BF16, name="mm_g_a")
    g_b = _mm_tn(y_b, dzb, out_dtype=BF16, name="mm_g_b")
    dy_a = _mm(dza, wts["w_a"], out_dtype=BF16, name="mm_dya", trans_b=True)
    dy_b = _mm(dzb, wts["w_b"], out_dtype=F32, name="mm_dyb", trans_b=True)
    dh, dws, dbs, dgv_n, dbv_n = _sgu_bwd(h, dy_a, dh, wm_b, wm_t, bs_t, small["sgu_norm_g"], small["sgu_norm_b"])
    big = dict(w_a=g_a, w_b=g_b, w_o=g_o, w_g=g_g, w_v=g_v, w_down=g_down, w_pp=g_pp, w_pg=g_pg)
    sm = dict(sgu_w_s=jnp.where(mask[None], dws, 0.0), sgu_b_s=dbs[:, :GROUPS].T, sgu_norm_g=dgv_n, sgu_norm_b=dbv_n,
              ln1_g=dg1, ln1_b=db1, conv_w=jnp.concatenate([dcw0, dcw1, dcw2], axis=0), conv_b=dcb, ln2_g=dg2, ln2_b=db2,
              loss=loss)
    dh, dlogits, dgn, got = _hgrn_bwd(h, o_all, dy_b, states, dh, small["lb_logits"], small["hgrn_norm_g"],
                                      ex=comm.grads_exchange(big, sm))
    comm.grads_done(got)
    g_in = _mm_tn(x_b, dh, out_dtype=BF16, name="mm_g_in")
    ex = comm.last_exchange(g_in, dict(lb_logits=dlogits, hgrn_norm_g=dgn))
    res = _mm(dh, w_in, out_dtype=F32, name="mm_dx", trans_b=True, reduce_b=True, adds=[(dz1, ALPHA)], ex=ex)
    grad_x, got = res if ex else (res, ())
    comm.last_done(got)
    return grad_x


_SMALL_EARLY = ["sgu_w_s", "sgu_b_s", "sgu_norm_g", "sgu_norm_b", "ln1_g", "ln1_b", "ffn_conv_b", "ln2_g", "ln2_b"]
_SMALL_LATE = ["hgrn_lb_logits", "hgrn_norm_g"]
N_TAPS = D_FF // N_DEV
UP_COLS = 2 * D_FF // N_DEV


class _StepExchanges:
    def __init__(self, w_in_shard, shards):
        self.w_in_shard = w_in_shard
        self.shards = shards

    def project_in(self, x_b):
        return _gather_project(x_b, self.w_in_shard)

    def weights_exchange(self, part):
        return _Exchange([(s, "gather+relay") for s in (self.shards[:2] if part == 0 else self.shards[2:])])

    def weights(self, got):
        d, f = D_MODEL, D_FF
        w_br_g, w_o_g, w_up_g, w_down_g, w_pp_g, w_pg_g, conv_g = got
        w_br = w_br_g.transpose(1, 0, 2, 3).reshape(2, d, d)
        w_up = w_up_g.transpose(1, 0, 2).reshape(d, 2, f).transpose(1, 0, 2)
        wts = dict(w_a=w_br[0], w_b=w_br[1], w_o=w_o_g.reshape(d, d), w_g=w_up[0], w_v=w_up[1],
                   w_down=w_down_g.reshape(f, d), w_pp=w_pp_g.transpose(1, 0, 2).reshape(256, d), w_pg=w_pg_g.reshape(d, d))
        return wts, conv_g.transpose(1, 0, 2).reshape(3, f)

    def grads_exchange(self, big, sm):
        d = D_MODEL
        parts = [jnp.stack([big["w_a"], big["w_b"]]).reshape(2, N_DEV, 128, d).transpose(1, 0, 2, 3),
                 big["w_o"].reshape(N_DEV, 128, d),
                 jnp.concatenate([big["w_g"], big["w_v"]], axis=1).reshape(d, N_DEV, UP_COLS).transpose(1, 0, 2),
                 big["w_down"].reshape(N_DEV, N_TAPS, d),
                 big["w_pp"].reshape(256, N_DEV, 128).transpose(1, 0, 2),
                 big["w_pg"].reshape(N_DEV, 128, d)]
        packed, self.rows_early = _pack([sm[k] for k in ("sgu_w_s", "sgu_b_s", "sgu_norm_g", "sgu_norm_b", "ln1_g", "ln1_b",
                                                         "conv_b", "ln2_g", "ln2_b", "conv_w", "loss")])
        return _Exchange([(a, "scatter") for a in parts] + [(packed, "gather")])

    def grads_done(self, got):
        self.recv, self.small_early = got[:6], got[6]

    def last_exchange(self, g_in, sm):
        packed, self.rows_late = _pack([sm["lb_logits"], sm["hgrn_norm_g"]])
        return _Exchange([(_pair_sums(g_in), "scatter+pairs"), (packed, "gather")])

    def last_done(self, got):
        self.recv_in, self.small_late = got


def _rows128(a):
    flat = a.reshape(-1)
    rows = -(-flat.shape[0] // (8 * LANE)) * 8
    return jnp.pad(flat, (0, rows * LANE - flat.shape[0])).reshape(rows, LANE)


def _pack(parts):
    blocks = [_rows128(a) for a in parts]
    return jnp.concatenate(blocks, axis=0), [b.shape[0] for b in blocks]


def _unpack(packed, shapes, rows):
    out, r0 = [], 0
    for shp, r in zip(shapes, rows):
        n = math.prod(shp)
        out.append(packed[r0:r0 + r].reshape(-1)[:n].reshape(shp))
        r0 += r
    return out


def kernel(x, p, w_in, sgu_w_s, sgu_b_s, sgu_norm_g, sgu_norm_b, hgrn_lb_logits, hgrn_norm_g, w_branch, w_out, ln1_g, ln1_b, ffn_w_up, ffn_conv_w, ffn_conv_b, ffn_w_down, ln2_g, ln2_b, ple_w_proj, ple_w_gate, loss_target, m_w_in, m_sgu_w_s, m_sgu_b_s, m_sgu_norm_g, m_sgu_norm_b, m_hgrn_lb_logits, m_hgrn_norm_g, m_w_branch, m_w_out, m_ln1_g, m_ln1_b, m_ffn_w_up, m_ffn_conv_w, m_ffn_conv_b, m_ffn_w_down, m_ln2_g, m_ln2_b, m_ple_w_proj, m_ple_w_gate, v_w_in, v_sgu_w_s, v_sgu_b_s, v_sgu_norm_g, v_sgu_norm_b, v_hgrn_lb_logits, v_hgrn_norm_g, v_w_branch, v_w_out, v_ln1_g, v_ln1_b, v_ffn_w_up, v_ffn_conv_w, v_ffn_conv_b, v_ffn_w_down, v_ln2_g, v_ln2_b, v_ple_w_proj, v_ple_w_gate):
    weights = dict(w_in=w_in, sgu_w_s=sgu_w_s, sgu_b_s=sgu_b_s, sgu_norm_g=sgu_norm_g, sgu_norm_b=sgu_norm_b,
                   hgrn_lb_logits=hgrn_lb_logits, hgrn_norm_g=hgrn_norm_g, w_branch=w_branch, w_out=w_out,
                   ln1_g=ln1_g, ln1_b=ln1_b, ffn_w_up=ffn_w_up, ffn_conv_w=ffn_conv_w, ffn_conv_b=ffn_conv_b,
                   ffn_w_down=ffn_w_down, ln2_g=ln2_g, ln2_b=ln2_b, ple_w_proj=ple_w_proj, ple_w_gate=ple_w_gate)
    mom_m = dict(w_in=m_w_in, sgu_w_s=m_sgu_w_s, sgu_b_s=m_sgu_b_s, sgu_norm_g=m_sgu_norm_g, sgu_norm_b=m_sgu_norm_b,
                 hgrn_lb_logits=m_hgrn_lb_logits, hgrn_norm_g=m_hgrn_norm_g, w_branch=m_w_branch, w_out=m_w_out,
                 ln1_g=m_ln1_g, ln1_b=m_ln1_b, ffn_w_up=m_ffn_w_up, ffn_conv_w=m_ffn_conv_w, ffn_conv_b=m_ffn_conv_b,
                 ffn_w_down=m_ffn_w_down, ln2_g=m_ln2_g, ln2_b=m_ln2_b, ple_w_proj=m_ple_w_proj, ple_w_gate=m_ple_w_gate)
    mom_v = dict(w_in=v_w_in, sgu_w_s=v_sgu_w_s, sgu_b_s=v_sgu_b_s, sgu_norm_g=v_sgu_norm_g, sgu_norm_b=v_sgu_norm_b,
                 hgrn_lb_logits=v_hgrn_lb_logits, hgrn_norm_g=v_hgrn_norm_g, w_branch=v_w_branch, w_out=v_w_out,
                 ln1_g=v_ln1_g, ln1_b=v_ln1_b, ffn_w_up=v_ffn_w_up, ffn_conv_w=v_ffn_conv_w, ffn_conv_b=v_ffn_conv_b,
                 ffn_w_down=v_ffn_w_down, ln2_g=v_ln2_g, ln2_b=v_ln2_b, ple_w_proj=v_ple_w_proj, ple_w_gate=v_ple_w_gate)
    d, f = D_MODEL, D_FF
    me = _my_index()

    comm = _StepExchanges(w_in[0].astype(BF16),
                          [w_branch[0].astype(BF16), w_out[0].astype(BF16), ffn_w_up[0].astype(BF16),
                           ffn_w_down[0].astype(BF16), ple_w_proj[0].astype(BF16), ple_w_gate[0].astype(BF16), ffn_conv_w[0]])
    small = dict(sgu_w_s=sgu_w_s[0], sgu_b_s=sgu_b_s[0], sgu_norm_g=sgu_norm_g, sgu_norm_b=sgu_norm_b,
                 lb_logits=hgrn_lb_logits, hgrn_norm_g=hgrn_norm_g, ln1_g=ln1_g, ln1_b=ln1_b, ln2_g=ln2_g, ln2_b=ln2_b,
                 conv_b=ffn_conv_b)
    grad_x = _local_step(x[0], p[0, 0], loss_target[0], small, comm)

    out = {}

    def adam(name, parts8, shape2d):
        w2, m2, v2 = (a.reshape(shape2d) for a in (weights[name], mom_m[name], mom_v[name]))
        res = _adam_call(w2, m2, v2, parts8.reshape(parts8.shape[:1] + shape2d), "adam_" + name)
        out[name] = tuple(r.reshape(weights[name].shape) for r in res)

    adam("w_in", comm.recv_in, (d, d))
    adam("w_branch", comm.recv[0], (256, d))
    adam("w_out", comm.recv[1], (128, d))
    adam("ffn_w_up", comm.recv[2], (d, UP_COLS))
    adam("ffn_w_down", comm.recv[3], (N_TAPS, d))
    adam("ple_w_proj", comm.recv[4], (256, 128))
    adam("ple_w_gate", comm.recv[5], (128, d))

    def adam_small(names, extra_w, extra_m, extra_v, extra_shapes, parts8, rows, label):
        pk = lambda src, extra: _pack([src[n] for n in names] + extra)[0]
        res = _adam_call(pk(weights, extra_w), pk(mom_m, extra_m), pk(mom_v, extra_v), parts8, label)
        shapes = [weights[n].shape for n in names] + extra_shapes
        unpacked = [_unpack(r, shapes, rows) for r in res]
        for j, n in enumerate(names):
            out[n] = tuple(u[j] for u in unpacked)
        return [[u[len(names) + j] for u in unpacked] for j in range(len(extra_shapes))]

    blank = jnp.zeros((8, LANE), F32)
    taps, loss_rows = adam_small(
        _SMALL_EARLY, [_place_taps(ffn_conv_w[0], me, f), blank], [_place_taps(m_ffn_conv_w[0], me, f), blank],
        [_place_taps(v_ffn_conv_w[0], me, f), blank + 1.0], [(3, f), (8, LANE)], comm.small_early, comm.rows_early,
        "adam_small_early")
    adam_small(_SMALL_LATE, [], [], [], [], comm.small_late, comm.rows_late, "adam_small_late")
    out["ffn_conv_w"] = tuple(lax.dynamic_slice_in_dim(u, me * N_TAPS, N_TAPS, axis=1)[None] for u in taps)
    loss = loss_rows[0][0, 0]

    order = ["w_in", "sgu_w_s", "sgu_b_s", "sgu_norm_g", "sgu_norm_b", "hgrn_lb_logits", "hgrn_norm_g", "w_branch", "w_out",
             "ln1_g", "ln1_b", "ffn_w_up", "ffn_conv_w", "ffn_conv_b", "ffn_w_down", "ln2_g", "ln2_b", "ple_w_proj", "ple_w_gate"]
    return (loss, grad_x[None], *[out[n][0] for n in order], *[out[n][1] for n in order],
            *[out[n][2] for n in order], *[out[n][3] for n in order])


def _place_taps(shard, me, f):
    return lax.dynamic_update_slice_in_dim(jnp.zeros((3, f), F32), shard, me * N_TAPS, axis=1)
```

```python
import functools
import math

import jax
import jax.numpy as jnp
from jax import lax
from jax.experimental import pallas as pl
from jax.experimental.pallas import tpu as pltpu

F32 = jnp.float32
BF16 = jnp.bfloat16

N_DEV = 8
N_CHIP = 4
D_MODEL = 1024
CHUNK = 64
SUB = 16
SGU_BLOCK = 128
GROUPS = 8
HEAD = 128
HEADS = 8
CHUNKS_PER_STEP = 2
HEAD_UNROLL = 8
D_FF = 2816
LN_EPS = 1e-5
RMS_EPS = 1e-6
ALPHA = 2.0 ** 0.25
GELU_K = math.sqrt(2.0 / math.pi)
GELU_C = 0.044715
NEG = -1e30
ADAM_LR, ADAM_B1, ADAM_B2, ADAM_EPS, ADAM_WD, ADAM_STEP = 0.001, 0.9, 0.999, 1e-08, 0.01, 10
LANE = 128
SLOT_Q, SLOT_F, SLOT_I, SLOT_OG, SLOT_U, SLOT_V, SLOT_GA, SLOT_GB = range(8)


def _slot_of_group(k):
    return jnp.where(k < 2, k + 4, jnp.where(k < 6, k - 2, k))


MIB = 1024 * 1024
VMEM_V7X = 64 * MIB
VMEM_FLOOR = 32 * MIB
MM_TILES = (1024, 1408, 512, 256, 128)


def _params(sem, need=0):
    limit = min(max(need + need // 4, VMEM_FLOOR), VMEM_V7X - 4 * MIB)
    return pltpu.CompilerParams(dimension_semantics=sem, vmem_limit_bytes=limit)


def _pick(n, prefs):
    for t in prefs:
        if n % t == 0:
            return t
    return n


def _gelu(x):
    return 0.5 * x * (1.0 + jnp.tanh(GELU_K * (x + GELU_C * x * x * x)))


def _gelu_and_grad(x):
    x2 = x * x
    t = jnp.tanh(GELU_K * x * (1.0 + GELU_C * x2))
    half = 0.5 * (1.0 + t)
    return x * half, half + 0.5 * x * (1.0 - t * t) * GELU_K * (1.0 + 3.0 * GELU_C * x2)


def _silu_grad(x, s):
    return s * (1.0 + x * (1.0 - s))


def _dot(a, b):
    return jnp.dot(a.astype(BF16), b.astype(BF16), preferred_element_type=F32)


def _dot_nt(a, b):
    return lax.dot_general(a.astype(BF16), b.astype(BF16), (((1,), (1,)), ((), ())), preferred_element_type=F32)


def _dot_tn(a, b):
    return lax.dot_general(a.astype(BF16), b.astype(BF16), (((0,), (0,)), ((), ())), preferred_element_type=F32)


def _mean(x):
    return jnp.mean(x, axis=-1, keepdims=True)


def _sum0(x):
    return jnp.sum(x, axis=0, keepdims=True)


def _mm(a, b, *, out_dtype, name, trans_b=False, reduce_b=False, adds=(), ex=None):
    squeeze = b.ndim == 2
    a3 = a if a.ndim == 3 else a[None]
    b3 = b if b.ndim == 3 else b[None]
    ba, m, k = a3.shape
    bb = b3.shape[0]
    n = b3.shape[1] if trans_b else b3.shape[2]
    tm = _pick(m, MM_TILES)
    tn = _pick(n, MM_TILES)
    if reduce_b:
        bo, steps = 1, bb
        a_map = lambda o, i, j, r: (r if ba > 1 else 0, i, 0)
        b_map = (lambda o, i, j, r: (r, j, 0)) if trans_b else (lambda o, i, j, r: (r, 0, j))
    else:
        bo, steps = bb, 1
        a_map = lambda o, i, j, r: (o if ba > 1 else 0, i, 0)
        b_map = (lambda o, i, j, r: (o, j, 0)) if trans_b else (lambda o, i, j, r: (o, 0, j))
    o_map = lambda o, i, j, r: (o, i, j)
    add_arrays = [x if x.ndim == 3 else x[None] for x, _ in adds]
    add_scales = [s for _, s in adds]
    n_add = len(adds)
    dot = _dot_nt if trans_b else _dot

    def finish(acc, add_refs, o_ref):
        for ref, s in zip(add_refs, add_scales):
            acc = acc + s * ref[...].astype(F32)
        o_ref[...] = acc.astype(o_ref.dtype)

    grid = (bo, m // tm, n // tn, steps)

    def body(*refs):
        ins, (o_ref,), scratch, xrefs = _split_refs(refs, 2 + n_add, 1, 1 if reduce_b else 0, ex)
        a_ref, b_ref, add_refs = ins[0], ins[1], ins[2:]
        step = ((pl.program_id(0) * grid[1] + pl.program_id(1)) * grid[2] + pl.program_id(2)) * grid[3] + pl.program_id(3)
        if ex:
            @pl.when(step == 0)
            def _():
                ex.start(*xrefs)

        if reduce_b:
            acc, = scratch
            r = pl.program_id(3)

            @pl.when(r == 0)
            def _():
                acc[...] = jnp.zeros_like(acc)

            acc[...] += dot(a_ref[...], b_ref[...])

            @pl.when(r == steps - 1)
            def _():
                finish(acc[...], add_refs, o_ref)
        else:
            finish(dot(a_ref[...], b_ref[...]), add_refs, o_ref)

        if ex:
            @pl.when(step == math.prod(grid) - 1)
            def _():
                ex.finish(*xrefs)

    b_block = (None, tn, k) if trans_b else (None, k, tn)
    out_bytes = tm * tn * jnp.dtype(out_dtype).itemsize
    need = 2 * (tm * k * a3.dtype.itemsize + k * tn * b3.dtype.itemsize + out_bytes + n_add * tm * tn * 4)
    need += 2 * tm * tn * 4
    sem = ("arbitrary",) * 4 if ex else ("parallel", "parallel", "parallel", "arbitrary")
    res = pl.pallas_call(
        body,
        grid=grid,
        in_specs=[pl.BlockSpec((None, tm, k), a_map), pl.BlockSpec(b_block, b_map)]
        + [pl.BlockSpec((None, tm, tn), o_map) for _ in adds] + (ex.in_specs if ex else []),
        out_specs=[pl.BlockSpec((None, tm, tn), o_map)] + (ex.out_specs if ex else []),
        out_shape=[jax.ShapeDtypeStruct((bo, m, n), out_dtype)] + (ex.out_shape if ex else []),
        scratch_shapes=([pltpu.VMEM((tm, tn), F32)] if reduce_b else []) + (ex.scratch if ex else []),
        compiler_params=_params(sem, need),
        name=name,
    )(a3, b3, *add_arrays, *(ex.arrays if ex else []))
    out = res[0][0] if (reduce_b or squeeze) else res[0]
    return (out, res[1:]) if ex else out


def _mm_tn(a, b, *, out_dtype, name):
    squeeze = b.ndim == 2
    b3 = b if b.ndim == 3 else b[None]
    t, m = a.shape
    bb, _, n = b3.shape
    tm = _pick(m, MM_TILES)
    tn = _pick(n, MM_TILES)
    tt = _pick(t, (1024, 512, 256, 128))
    steps = t // tt
    need = 2 * (tt * tm * a.dtype.itemsize + tt * tn * b3.dtype.itemsize + tm * tn * jnp.dtype(out_dtype).itemsize)
    need += 2 * tm * tn * 4

    def body(a_ref, b_ref, o_ref, acc):
        r = pl.program_id(3)

        @pl.when(r == 0)
        def _():
            acc[...] = jnp.zeros_like(acc)

        acc[...] += _dot_tn(a_ref[...], b_ref[...])

        @pl.when(r == steps - 1)
        def _():
            o_ref[...] = acc[...].astype(o_ref.dtype)

    out = pl.pallas_call(
        body,
        grid=(bb, m // tm, n // tn, steps),
        in_specs=[pl.BlockSpec((tt, tm), lambda o, i, j, r: (r, i)),
                  pl.BlockSpec((None, tt, tn), lambda o, i, j, r: (o, r, j))],
        out_specs=pl.BlockSpec((None, tm, tn), lambda o, i, j, r: (o, i, j)),
        out_shape=jax.ShapeDtypeStruct((bb, m, n), out_dtype),
        scratch_shapes=[pltpu.VMEM((tm, tn), F32)],
        compiler_params=_params(("parallel", "parallel", "parallel", "arbitrary"), need),
        name=name,
    )(a, b3)
    return out[0] if squeeze else out


def _rowwise(fn, rows, consts, row_outs, acc_outs, *, tile, name):
    first = rows[0][0] if isinstance(rows[0], tuple) else rows[0]
    t = first.shape[-2]
    steps = t // tile
    arrays, in_specs = [], []
    for r in rows:
        if isinstance(r, tuple) and isinstance(r[1], pl.BlockSpec):
            arrays.append(r[0])
            in_specs.append(r[1])
        elif isinstance(r, tuple):
            arr, bidx = r
            arrays.append(arr)
            in_specs.append(pl.BlockSpec((None, tile, arr.shape[-1]), functools.partial(lambda i, b: (b, i, 0), b=bidx)))
        else:
            arrays.append(r)
            in_specs.append(pl.BlockSpec((tile, r.shape[-1]), lambda i: (i, 0)))
    for c in consts:
        arrays.append(c)
        in_specs.append(pl.BlockSpec(c.shape, lambda i: (0, 0)))
    n_in, n_row = len(arrays), len(row_outs)
    out_shape, out_specs = [], []
    for ro in row_outs:
        if ro[0] == "stack":
            _, cnt, blk, total, w, dt = ro
            out_shape.append(jax.ShapeDtypeStruct((total, t, w), dt))
            out_specs.append(pl.BlockSpec((cnt, tile, w), functools.partial(lambda i, b: (b, i, 0), b=blk)))
        else:
            w, dt = ro
            out_shape.append(jax.ShapeDtypeStruct((t, w), dt))
            out_specs.append(pl.BlockSpec((tile, w), lambda i: (i, 0)))
    out_shape += [jax.ShapeDtypeStruct(s, F32) for s in acc_outs]
    out_specs += [pl.BlockSpec(s, lambda i: (0, 0)) for s in acc_outs]
    blocks = [math.prod(d for d in sp.block_shape if d) * arr.dtype.itemsize for sp, arr in zip(in_specs, arrays)]
    blocks += [math.prod(d for d in sp.block_shape if d) * jnp.dtype(sh.dtype).itemsize
               for sp, sh in zip(out_specs, out_shape)]
    need = 2 * sum(blocks) + 6 * tile * max(a.shape[-1] for a in arrays) * 4

    def body(*refs):
        ins, outs = refs[:n_in], refs[n_in:]
        i = pl.program_id(0)
        res = fn(i, steps, *[r[...] for r in ins])
        res = res if isinstance(res, (tuple, list)) else (res,)
        for ref, val in zip(outs[:n_row], res[:n_row]):
            ref[...] = val.astype(ref.dtype)
        if acc_outs:
            @pl.when(i == 0)
            def _():
                for ref in outs[n_row:]:
                    ref[...] = jnp.zeros_like(ref)

            for ref, val in zip(outs[n_row:], res[n_row:]):
                ref[...] += val

    return pl.pallas_call(
        body,
        grid=(steps,),
        in_specs=in_specs,
        out_specs=out_specs,
        out_shape=out_shape,
        compiler_params=_params(("arbitrary",), need),
        name=name,
    )(*arrays)


def _ln_stats(z):
    mu = _mean(z)
    zc = z - mu
    rstd = lax.rsqrt(_mean(zc * zc) + LN_EPS)
    return zc * rstd, rstd


def _ln_bwd(dy, xhat, rstd, g):
    dxh = dy * g
    return rstd * (dxh - _mean(dxh) - xhat * _mean(dxh * xhat))


def _ride(ex, xrefs, step, steps):
    if not ex:
        return
    for at, act in ((0, ex.start), (steps - 2, ex.relay), (steps - 1, ex.finish)):
        @pl.when(step == at)
        def _(act=act):
            act(*xrefs)


def _sgu_fwd(h, wm, bs_t, g_v, b_v, ex=None):
    t = h.shape[1]
    steps = t // SGU_BLOCK

    def body(*refs):
        (u_ref, v_ref, wm_ref, bs_ref, g_ref, b_ref), (y_ref,), _, xrefs = _split_refs(refs, 6, 1, 0, ex)
        xhat, _ = _ln_stats(_gelu(v_ref[...]))
        vn = (xhat * g_ref[...] + b_ref[...]).astype(BF16)
        gu = _gelu(u_ref[...])
        for g in range(GROUPS):
            sl = slice(g * HEAD, (g + 1) * HEAD)
            mixed = _dot(wm_ref[g], vn[:, sl]) + bs_ref[:, g:g + 1]
            y_ref[:, sl] = (gu[:, sl] * mixed).astype(BF16)
        _ride(ex, xrefs, pl.program_id(0), steps)

    blk = lambda b: pl.BlockSpec((None, SGU_BLOCK, D_MODEL), functools.partial(lambda i, b: (b, i, 0), b=b))
    whole = lambda s: pl.BlockSpec(s, lambda i: (0,) * len(s))
    res = pl.pallas_call(
        body,
        grid=(steps,),
        in_specs=[blk(SLOT_U), blk(SLOT_V), whole(wm.shape), whole(bs_t.shape), whole(g_v.shape), whole(b_v.shape)]
        + (ex.in_specs if ex else []),
        out_specs=[pl.BlockSpec((SGU_BLOCK, D_MODEL), lambda i: (i, 0))] + (ex.out_specs if ex else []),
        out_shape=[jax.ShapeDtypeStruct((t, D_MODEL), BF16)] + (ex.out_shape if ex else []),
        scratch_shapes=ex.scratch if ex else [],
        compiler_params=_params(("arbitrary",)),
        name="sgu_fwd",
    )(h, h, wm, bs_t, g_v, b_v, *(ex.arrays if ex else []))
    return res[0], res[1:]


def _sgu_bwd(h, dy, dh, wm, wm_t, bs_t, g_v, b_v):
    t = h.shape[1]

    def body(u_ref, v_ref, dy_ref, dh_in, wm_ref, wmt_ref, bs_ref, g_ref, b_ref,
             duv_ref, dw_ref, dbs_ref, dg_ref, db_ref, dvn_ref):
        del dh_in
        du_ref, dv_ref = duv_ref.at[0], duv_ref.at[1]
        i = pl.program_id(0)

        @pl.when(i == 0)
        def _():
            dw_ref[...] = jnp.zeros_like(dw_ref)
            dbs_ref[...] = jnp.zeros_like(dbs_ref)
            dg_ref[...] = jnp.zeros_like(dg_ref)
            db_ref[...] = jnp.zeros_like(db_ref)

        u = u_ref[...]
        v = v_ref[...]
        gv, gvp = _gelu_and_grad(v)
        xhat, rstd = _ln_stats(gv)
        vn = (xhat * g_ref[...] + b_ref[...]).astype(BF16)
        gu, gup = _gelu_and_grad(u)
        lane = lax.broadcasted_iota(jnp.int32, (SGU_BLOCK, LANE), 1)
        dbs = jnp.zeros((SGU_BLOCK, LANE), F32)
        for g in range(GROUPS):
            sl = slice(g * HEAD, (g + 1) * HEAD)
            vn_g = vn[:, sl]
            mixed = _dot(wm_ref[g], vn_g) + bs_ref[:, g:g + 1]
            dy_g = dy_ref[:, sl]
            du_ref[:, sl] = (dy_g * mixed * gup[:, sl]).astype(BF16)
            dmix = dy_g * gu[:, sl]
            dmb = dmix.astype(BF16)
            dvn_ref[:, sl] = _dot(wmt_ref[g], dmb)
            dw_ref[g] += _dot_nt(dmb, vn_g)
            dbs = dbs + jnp.where(lane == g, jnp.sum(dmix, axis=1, keepdims=True), 0.0)
        dbs_ref[...] += dbs
        dvn = dvn_ref[...]
        dg_ref[...] += _sum0(dvn * xhat)
        db_ref[...] += _sum0(dvn)
        dv_ref[...] = (_ln_bwd(dvn, xhat, rstd, g_ref[...]) * gvp).astype(BF16)

    blk = lambda b: pl.BlockSpec((None, SGU_BLOCK, D_MODEL), functools.partial(lambda i, b: (b, i, 0), b=b))
    row = pl.BlockSpec((SGU_BLOCK, D_MODEL), lambda i: (i, 0))
    whole = lambda s: pl.BlockSpec(s, lambda i: (0,) * len(s))
    vec = (1, D_MODEL)
    return pl.pallas_call(
        body,
        grid=(t // SGU_BLOCK,),
        in_specs=[blk(SLOT_U), blk(SLOT_V), row, pl.BlockSpec(memory_space=pl.ANY),
                  whole(wm.shape), whole(wm_t.shape), whole(bs_t.shape), whole(vec), whole(vec)],
        out_specs=[pl.BlockSpec((2, SGU_BLOCK, D_MODEL), lambda i: (SLOT_U // 2, i, 0)),
                   whole(wm.shape), whole((SGU_BLOCK, LANE)), whole(vec), whole(vec)],
        out_shape=[jax.ShapeDtypeStruct(dh.shape, BF16),
                   jax.ShapeDtypeStruct(wm.shape, F32), jax.ShapeDtypeStruct((SGU_BLOCK, LANE), F32),
                   jax.ShapeDtypeStruct(vec, F32), jax.ShapeDtypeStruct(vec, F32)],
        scratch_shapes=[pltpu.VMEM((SGU_BLOCK, D_MODEL), F32)],
        input_output_aliases={3: 0},
        compiler_params=_params(("arbitrary",)),
        name="sgu_bwd",
    )(h, h, dy, dh, wm, wm_t, bs_t, g_v, b_v)


def _split3(x):
    hi = x.astype(BF16)
    r1 = x - hi.astype(F32)
    mid = r1.astype(BF16)
    lo = (r1 - mid.astype(F32)).astype(BF16)
    return hi, mid, lo


def _tri_matmul(tri, x):
    hi, mid, lo = _split3(x)
    dot = lambda p: jnp.dot(tri, p, preferred_element_type=F32)
    return dot(hi) + dot(mid) + dot(lo)


def _lower_bound(logits):
    l0, l1 = logits[0:1, :], logits[1:2, :]
    mx = jnp.maximum(l0, l1)
    e0, e1 = jnp.exp(l0 - mx), jnp.exp(l1 - mx)
    return e0 / (e0 + e1)


def _hgrn_gates(q_raw, f_raw, lb):
    q = q_raw * jax.nn.sigmoid(q_raw)
    sig = jax.nn.sigmoid(f_raw)
    f = lb + (1.0 - lb) * sig
    row = lax.broadcasted_iota(jnp.int32, (CHUNK, CHUNK), 0)
    col = lax.broadcasted_iota(jnp.int32, (CHUNK, CHUNK), 1)
    c = _tri_matmul((row >= col).astype(BF16), jnp.log(f))
    return q, sig, f, 1.0 - f, c


def _offdiag_terms(qh, kh, ch, tb):
    rows = slice(tb * SUB, (tb + 1) * SUB)
    r = ch[tb * SUB - 1:tb * SUB, :]
    eqh = jnp.exp(ch[rows] - r)
    ekh = jnp.exp(jnp.minimum(r - ch, 0.0))
    return rows, eqh, qh[rows] * eqh, ekh, kh * ekh


def _diag_decay(cb, s, trow):
    return jnp.exp(jnp.where(trow >= s, cb - cb[s:s + 1, :], NEG))


def _split_refs(refs, n_in, n_out, n_scratch, ex):
    nx = ex.n if ex else 0
    ins, refs = refs[:n_in], refs[n_in:]
    xsrc, refs = refs[:nx], refs[nx:]
    outs, refs = refs[:n_out], refs[n_out:]
    xout, refs = refs[:nx], refs[nx:]
    return ins, outs, refs[:n_scratch], (xsrc, xout, refs[n_scratch:])


def _hgrn_fwd(h, logits, g_norm, ex=None):
    t = h.shape[1]
    nc = t // CHUNK
    per_step = CHUNKS_PER_STEP if nc % CHUNKS_PER_STEP == 0 else 1
    steps = nc // per_step

    def body(*refs):
        ins, outs, scratch, xrefs = _split_refs(refs, 3, 3, 4, ex)
        hgrn_ref, lg_ref, gn_ref = ins
        y_ref, o_ref, sall_ref = outs
        st_ref, q_s, k_s, c_s = scratch

        @pl.when(pl.program_id(0) == 0)
        def _():
            st_ref[...] = jnp.zeros_like(st_ref)

        lb = _lower_bound(lg_ref[...])
        for sub in range(per_step):
            rows = pl.ds(sub * CHUNK, CHUNK)
            chunk(lb, sall_ref.at[sub], *(r.at[rows] for r in (hgrn_ref.at[SLOT_Q], hgrn_ref.at[SLOT_F], hgrn_ref.at[SLOT_I],
                                                               hgrn_ref.at[SLOT_OG], y_ref, o_ref)),
                  gn_ref, st_ref, q_s.at[sub], k_s.at[sub], c_s.at[sub])
        _ride(ex, xrefs, pl.program_id(0), steps)

    def chunk(lb, sall_ref, q_ref, f_ref, i_ref, og_ref, y_ref, o_ref, gn_ref, st_ref, q_s, k_s, c_s):
        q, _, _, k, c = _hgrn_gates(q_ref[...], f_ref[...], lb)
        q_s[...] = q
        k_s[...] = k
        c_s[...] = c
        col64 = lax.broadcasted_iota(jnp.int32, (SUB, CHUNK), 1)
        trow = lax.broadcasted_iota(jnp.int32, (SUB, HEAD), 0)

        def head(hd, carry):
            sl = pl.ds(pl.multiple_of(hd * HEAD, HEAD), HEAD)
            qh, kh, ch, ih = q_s[:, sl], k_s[:, sl], c_s[:, sl], i_ref[:, sl]
            st = st_ref[hd]
            sall_ref[hd] = st
            c_last = ch[CHUNK - 1:CHUNK, :]
            o = _dot_nt(qh * jnp.exp(ch), st)
            st_ref[hd] = st * jnp.exp(c_last) + _dot_tn(ih, kh * jnp.exp(c_last - ch))
            a_rows = [jnp.zeros((SUB, CHUNK), F32)]
            for tb in range(1, CHUNK // SUB):
                _, _, q_hat, _, k_hat = _offdiag_terms(qh, kh, ch, tb)
                a_rows.append(jnp.where(col64 < tb * SUB, _dot_nt(q_hat, k_hat), 0.0))
            o = o + _dot(jnp.concatenate(a_rows, axis=0), ih)
            o_rows = []
            for b in range(CHUNK // SUB):
                rows = slice(b * SUB, (b + 1) * SUB)
                qb, cb, kb, ib = qh[rows], ch[rows], kh[rows], ih[rows]
                ob = jnp.zeros((SUB, HEAD), F32)
                for s in range(SUB):
                    a = jnp.sum(qb * _diag_decay(cb, s, trow) * kb[s:s + 1, :], axis=1, keepdims=True)
                    ob = ob + a * ib[s:s + 1, :]
                o_rows.append(ob)
            o = o + jnp.concatenate(o_rows, axis=0)
            o_ref[:, sl] = o
            og = og_ref[:, sl]
            on = o * lax.rsqrt(_mean(o * o) + RMS_EPS)
            y_ref[:, sl] = (on * gn_ref[:, sl] * (og * jax.nn.sigmoid(og))).astype(BF16)
            return carry

        lax.fori_loop(0, HEADS, head, 0, unroll=HEAD_UNROLL)

    rows = per_step * CHUNK
    row = pl.BlockSpec((rows, D_MODEL), lambda n: (n, 0))
    whole = lambda s: pl.BlockSpec(s, lambda n: (0,) * len(s))
    res = pl.pallas_call(
        body,
        grid=(steps,),
        in_specs=[pl.BlockSpec((4, rows, D_MODEL), lambda n: (SLOT_Q // 4, n, 0)), whole(logits.shape), whole(g_norm.shape)]
        + (ex.in_specs if ex else []),
        out_specs=[row, row, pl.BlockSpec((per_step, HEADS, HEAD, HEAD), lambda n: (n, 0, 0, 0))]
        + (ex.out_specs if ex else []),
        out_shape=[jax.ShapeDtypeStruct((t, D_MODEL), BF16), jax.ShapeDtypeStruct((t, D_MODEL), F32),
                   jax.ShapeDtypeStruct((nc, HEADS, HEAD, HEAD), F32)] + (ex.out_shape if ex else []),
        scratch_shapes=[pltpu.VMEM((HEADS, HEAD, HEAD), F32)] + [pltpu.VMEM((per_step, CHUNK, D_MODEL), F32)] * 3
        + (ex.scratch if ex else []),
        compiler_params=_params(("arbitrary",)),
        name="hgrn_fwd",
    )(h, logits, g_norm, *(ex.arrays if ex else []))
    return res[0], res[1], res[2], res[3:]


def _hgrn_bwd(h, o_all, dy, states, dh, logits, g_norm, ex=None):
    t = h.shape[1]
    nc = t // CHUNK
    per_step = CHUNKS_PER_STEP if nc % CHUNKS_PER_STEP == 0 else 1
    steps = nc // per_step
    hgrn_slots = (SLOT_Q, SLOT_F, SLOT_I, SLOT_OG)

    def body(*refs):
        ins, outs, scratch, xrefs = _split_refs(refs, 7, 3, 8, ex)
        hgrn_ref, o_ref, dy_ref, sall_ref, _, lg_ref, gn_ref = ins
        dqfio_ref, dlg_ref, dgn_ref = outs
        dst_ref, dlb_ref = scratch[:2]
        n = pl.program_id(0)

        @pl.when(n == 0)
        def _():
            dst_ref[...] = jnp.zeros_like(dst_ref)
            dlb_ref[...] = jnp.zeros_like(dlb_ref)
            dgn_ref[...] = jnp.zeros_like(dgn_ref)
            if ex:
                ex.start(*xrefs)

        lb = _lower_bound(lg_ref[...])
        for sub in reversed(range(per_step)):
            rows = pl.ds(sub * CHUNK, CHUNK)
            chunk(lb, sall_ref.at[sub], [hgrn_ref.at[s].at[rows] for s in hgrn_slots], o_ref.at[rows], dy_ref.at[rows],
                  [dqfio_ref.at[s].at[rows] for s in hgrn_slots], gn_ref, dgn_ref, dst_ref, dlb_ref,
                  [z.at[sub] for z in scratch[2:]])

        @pl.when(n == steps - 1)
        def _():
            d0 = dlb_ref[...] * lb * (1.0 - lb)
            dlg_ref[0:1, :] = d0
            dlg_ref[1:2, :] = -d0
            if ex:
                ex.finish(*xrefs)

    def chunk(lb, sall_ref, h_refs, o_ref, dy_ref, dh_refs, gn_ref, dgn_ref, dst_ref, dlb_ref, work):
        q_ref, f_ref, i_ref, og_ref = h_refs
        dq_ref, df_ref, di_ref, dog_ref = dh_refs
        q_s, k_s, c_s, dq_s, dk_s, dc_s = work
        q_raw = q_ref[...]
        q, sig, f, k, c = _hgrn_gates(q_raw, f_ref[...], lb)
        q_s[...] = q
        k_s[...] = k
        c_s[...] = c
        col64 = lax.broadcasted_iota(jnp.int32, (SUB, CHUNK), 1)
        trow = lax.broadcasted_iota(jnp.int32, (SUB, HEAD), 0)
        row64 = lax.broadcasted_iota(jnp.int32, (CHUNK, HEAD), 0)

        def head(hd, carry):
            sl = pl.ds(pl.multiple_of(hd * HEAD, HEAD), HEAD)
            qh, kh, ch, ih = q_s[:, sl], k_s[:, sl], c_s[:, sl], i_ref[:, sl]
            st = sall_ref[hd]
            dst = dst_ref[hd]
            oh, dyh, og, gn = o_ref[:, sl], dy_ref[:, sl], og_ref[:, sl], gn_ref[:, sl]
            sg = jax.nn.sigmoid(og)
            sil = og * sg
            rms = lax.rsqrt(_mean(oh * oh) + RMS_EPS)
            on = oh * rms
            dog_ref[:, sl] = (dyh * on * gn * _silu_grad(og, sg)).astype(BF16)
            dgn_ref[:, sl] += _sum0(dyh * on * sil)
            don = dyh * gn * sil
            do = rms * (don - on * _mean(don * on))
            dob = do.astype(BF16)

            c_last = ch[CHUNK - 1:CHUNK, :]
            eq = jnp.exp(ch)
            q_til = qh * eq
            ekl = jnp.exp(c_last - ch)
            k_til = kh * ekl
            ecl = jnp.exp(c_last)
            dq_til = _dot(dob, st)
            dk_til = _dot(ih, dst)
            di = _dot_nt(k_til, dst)
            dc_last = _sum0(dk_til * k_til) + _sum0(dst * st) * ecl
            dst_ref[hd] = _dot_tn(dob, q_til) + dst * ecl
            dq = dq_til * eq
            dc = dq_til * q_til - dk_til * k_til
            dk = dk_til * ekl

            da_full = _dot_nt(dob, ih)
            a_rows = [jnp.zeros((SUB, CHUNK), F32)]
            dq_rows = [jnp.zeros((SUB, HEAD), F32)]
            dc_rows = [jnp.zeros((SUB, HEAD), F32)]
            for tb in range(1, CHUNK // SUB):
                rows, eqh, q_hat, ekh, k_hat = _offdiag_terms(qh, kh, ch, tb)
                keep = col64 < tb * SUB
                a_rows.append(jnp.where(keep, _dot_nt(q_hat, k_hat), 0.0))
                da = jnp.where(keep, da_full[rows], 0.0)
                dq_hat = _dot(da, k_hat)
                dk_hat = _dot_tn(da, q_hat)
                dq_rows.append(dq_hat * eqh)
                dc_rows.append(dq_hat * q_hat)
                dk = dk + dk_hat * ekh
                dc = dc - dk_hat * k_hat
            di = di + _dot_tn(jnp.concatenate(a_rows, axis=0), dob)

            dk_rows, di_rows = [], []
            for b in range(CHUNK // SUB):
                rows = slice(b * SUB, (b + 1) * SUB)
                qb, cb, kb, ib, dob_ = qh[rows], ch[rows], kh[rows], ih[rows], do[rows]
                dq_diag = jnp.zeros((SUB, HEAD), F32)
                dk_diag = jnp.zeros((SUB, HEAD), F32)
                di_diag = jnp.zeros((SUB, HEAD), F32)
                for s in range(SUB):
                    ks = kb[s:s + 1, :]
                    dec = _diag_decay(cb, s, trow)
                    a = jnp.sum(qb * dec * ks, axis=1, keepdims=True)
                    gk = jnp.sum(dob_ * ib[s:s + 1, :], axis=1, keepdims=True) * dec
                    dq_diag = dq_diag + gk * ks
                    dk_diag = dk_diag + jnp.where(trow == s, _sum0(gk * qb), 0.0)
                    di_diag = di_diag + jnp.where(trow == s, _sum0(a * dob_), 0.0)
                dq_rows[b] = dq_rows[b] + dq_diag
                dc_rows[b] = dc_rows[b] + qb * dq_diag - kb * dk_diag
                dk_rows.append(dk_diag)
                di_rows.append(di_diag)
            dq = dq + jnp.concatenate(dq_rows, axis=0)
            dk = dk + jnp.concatenate(dk_rows, axis=0)
            dc = dc + jnp.concatenate(dc_rows, axis=0) + jnp.where(row64 == CHUNK - 1, dc_last, 0.0)
            di_ref[:, sl] = (di + jnp.concatenate(di_rows, axis=0)).astype(BF16)
            dq_s[:, sl] = dq
            dk_s[:, sl] = dk
            dc_s[:, sl] = dc
            return carry

        lax.fori_loop(0, HEADS, head, 0, unroll=HEAD_UNROLL)

        row = lax.broadcasted_iota(jnp.int32, (CHUNK, CHUNK), 0)
        col = lax.broadcasted_iota(jnp.int32, (CHUNK, CHUNK), 1)
        dlf = _tri_matmul((row <= col).astype(BF16), dc_s[...])
        df = dlf / f - dk_s[...]
        dlb_ref[...] += _sum0(df * (1.0 - sig))
        df_ref[...] = (df * (1.0 - lb) * sig * (1.0 - sig)).astype(BF16)
        dq_ref[...] = (dq_s[...] * _silu_grad(q_raw, jax.nn.sigmoid(q_raw))).astype(BF16)

    rev = lambda n: steps - 1 - n
    rows = per_step * CHUNK
    slots = pl.BlockSpec((4, rows, D_MODEL), lambda n: (SLOT_Q // 4, rev(n), 0))
    row = pl.BlockSpec((rows, D_MODEL), lambda n: (rev(n), 0))
    whole = lambda s: pl.BlockSpec(s, lambda n: (0,) * len(s))
    vec = (1, D_MODEL)
    res = pl.pallas_call(
        body,
        grid=(steps,),
        in_specs=[slots, row, row, pl.BlockSpec((per_step, HEADS, HEAD, HEAD), lambda n: (rev(n), 0, 0, 0)),
                  pl.BlockSpec(memory_space=pl.ANY), whole(logits.shape), whole(vec)] + (ex.in_specs if ex else []),
        out_specs=[slots, whole((2, D_MODEL)), whole(vec)] + (ex.out_specs if ex else []),
        out_shape=[jax.ShapeDtypeStruct(dh.shape, BF16), jax.ShapeDtypeStruct((2, D_MODEL), F32),
                   jax.ShapeDtypeStruct(vec, F32)] + (ex.out_shape if ex else []),
        scratch_shapes=[pltpu.VMEM((HEADS, HEAD, HEAD), F32), pltpu.VMEM(vec, F32)]
        + [pltpu.VMEM((per_step, CHUNK, D_MODEL), F32)] * 6 + (ex.scratch if ex else []),
        input_output_aliases={4: 0},
        compiler_params=_params(("arbitrary",)),
        name="hgrn_bwd",
    )(h, o_all, dy, states, dh, logits, g_norm, *(ex.arrays if ex else []))
    return res[0], res[1], res[2], res[3:]


def _merge_fwd(i, n, ga, gb, za, zb):
    return jax.nn.sigmoid(ga) * za + jax.nn.sigmoid(gb) * zb


def _merge_bwd(i, n, ga, gb, za, zb, dm):
    sa, sb = jax.nn.sigmoid(ga), jax.nn.sigmoid(gb)
    dgates = jnp.stack([(dm * za * sa * (1.0 - sa)).astype(BF16), (dm * zb * sb * (1.0 - sb)).astype(BF16)])
    return dgates, dm * sa, dm * sb


def _ln1_fwd(i, n, x, r1, g, b):
    xhat, _ = _ln_stats(ALPHA * x + r1)
    x1 = xhat * g + b
    return x1, x1


def _ln1_bwd(i, n, x, r1, dx1, g):
    xhat, rstd = _ln_stats(ALPHA * x + r1)
    dz = _ln_bwd(dx1, xhat, rstd, g)
    return dz, dz, _sum0(dx1 * xhat), _sum0(dx1)


def _ln2_loss(i, n, x1, fo, pg, pp, tgt, g, b):
    sg = jax.nn.sigmoid(pg)
    xhat, rstd = _ln_stats(ALPHA * x1 + fo + sg * pp)
    diff = xhat * g + b - tgt
    loss = 0.5 * jnp.sum(_mean(diff * diff), axis=0, keepdims=True)
    dy = diff * (1.0 / D_MODEL)
    dz = _ln_bwd(dy, xhat, rstd, g)
    return (dz, dz, dz * pp * sg * (1.0 - sg), dz * sg,
            jnp.broadcast_to(loss, (8, LANE)), _sum0(dy * xhat), _sum0(dy))


def _shift_down(cur, halo, tile):
    row = lax.broadcasted_iota(jnp.int32, cur.shape, 0)
    m1 = jnp.where(row == 0, halo[7:8, :], pltpu.roll(cur, 1, 0))
    m2 = jnp.where(row == 0, halo[6:7, :], jnp.where(row == 1, halo[7:8, :], pltpu.roll(cur, 2, 0)))
    return m1, m2


def _shift_up(cur, halo, tile):
    row = lax.broadcasted_iota(jnp.int32, cur.shape, 0)
    p1 = jnp.where(row == tile - 1, halo[0:1, :], pltpu.roll(cur, tile - 1, 0))
    p2 = jnp.where(row == tile - 2, halo[0:1, :], jnp.where(row == tile - 1, halo[1:2, :], pltpu.roll(cur, tile - 2, 0)))
    return p1, p2


def _conv_pre(i, gate, halo, w, b, tile):
    halo = jnp.where(i == 0, 0.0, halo)
    m1, m2 = _shift_down(gate, halo, tile)
    return w[0:1, :] * m2 + w[1:2, :] * m1 + w[2:3, :] * gate + b, m1, m2


def _conv_fwd(tile, i, n, gate, halo, val, w, b):
    cg, _, _ = _conv_pre(i, gate, halo, w, b, tile)
    return _gelu(cg) * val


def _conv_bwd_a(tile, i, n, gate, halo, val, dhid, w, b):
    cg, m1, m2 = _conv_pre(i, gate, halo, w, b, tile)
    act, slope = _gelu_and_grad(cg)
    dcg = dhid * val * slope
    return dcg, dhid * act, _sum0(dcg * m2), _sum0(dcg * m1), _sum0(dcg * gate), _sum0(dcg)


def _conv_bwd_b(tile, i, n, dcg, halo, w):
    dcg = dcg.astype(F32)
    halo = jnp.where(i == n - 1, 0.0, halo.astype(F32))
    p1, p2 = _shift_up(dcg, halo, tile)
    return w[2:3, :] * dcg + w[1:2, :] * p1 + w[0:1, :] * p2


def _halo_spec(width, tile, t, nxt, rows=8):
    per = tile // rows
    last = t // rows - 1
    if nxt:
        return pl.BlockSpec((rows, width), lambda i: (jnp.minimum((i + 1) * per, last), 0))
    return pl.BlockSpec((rows, width), lambda i: (jnp.maximum(i * per - 1, 0), 0))


def _adamw(i, n, w, m, v, parts):
    g = parts[0].astype(F32)
    for j in range(1, parts.shape[0]):
        g = g + parts[j].astype(F32)
    m_new = ADAM_B1 * m + (1.0 - ADAM_B1) * g
    v_new = ADAM_B2 * v + (1.0 - ADAM_B2) * (g * g)
    m_hat = m_new / (1.0 - ADAM_B1 ** ADAM_STEP)
    v_hat = v_new / (1.0 - ADAM_B2 ** ADAM_STEP)
    delta = -ADAM_LR * (m_hat / (jnp.sqrt(v_hat) + ADAM_EPS) + ADAM_WD * w)
    return g, delta, m_new, v_new


def _adam_call(w, m, v, parts, name):
    r, c = w.shape
    tile = _pick(r, (256, 128)) if r > 256 else r
    spec = pl.BlockSpec((parts.shape[0], tile, c), lambda i: (0, i, 0))
    return _rowwise(_adamw, [w, m, v, (parts, spec)], [], [(c, F32)] * 4, [], tile=tile, name=name)


def _peer(k):
    x, y, c = lax.axis_index("x"), lax.axis_index("y"), lax.axis_index("c")
    px = x ^ ((k >> 2) & 1)
    py = y ^ ((k >> 1) & 1)
    pc = c ^ (k & 1)
    return (px, py, pc), 4 * px + 2 * py + pc


def _my_index():
    return 4 * lax.axis_index("x") + 2 * lax.axis_index("y") + lax.axis_index("c")


class _Exchange:
    KINDS = ("gather", "gather+relay", "scatter", "scatter+pairs")

    def __init__(self, entries):
        assert all(k in self.KINDS for _, k in entries), [k for _, k in entries]
        self.arrays = [a for a, _ in entries]
        self.scatter = [k.startswith("scatter") for _, k in entries]
        self.relayed = ["+relay" in k for _, k in entries]
        self.pairs = ["+pairs" in k for _, k in entries]
        self.n = len(entries)
        self.in_specs = [pl.BlockSpec(memory_space=pl.ANY)] * self.n
        self.out_specs = [pl.BlockSpec(memory_space=pl.ANY)] * self.n
        shapes = [tuple(a.shape[1:]) if sc else tuple(a.shape) for a, sc in zip(self.arrays, self.scatter)]
        counts = [N_CHIP if p else N_DEV for p in self.pairs]
        self.out_shape = [jax.ShapeDtypeStruct((n,) + s, a.dtype) for n, s, a in zip(counts, shapes, self.arrays)]
        per = N_DEV - 1
        self.scratch = [pltpu.SemaphoreType.DMA((self.n * per,)), pltpu.SemaphoreType.DMA((self.n * per,)),
                        pltpu.SemaphoreType.DMA((self.n,))]

    def _copies(self, srcs, outs, sems):
        send_sems, recv_sems, local_sems = sems
        x, y, c = lax.axis_index("x"), lax.axis_index("y"), lax.axis_index("c")
        me = _my_index()
        per = N_DEV - 1
        local, first, passed, relay_arrivals, arrivals = [], [], [], [], []
        for a in range(self.n):

            def copy(k, src, dst, dev, a=a):
                return pltpu.make_async_remote_copy(
                    src_ref=src, dst_ref=dst, send_sem=send_sems.at[a * per + k], recv_sem=recv_sems.at[a * per + k],
                    device_id=dev, device_id_type=pl.DeviceIdType.MESH)

            if self.pairs[a]:
                chip = 2 * x + y
                for k in range(N_CHIP):
                    to = chip ^ k
                    piece = srcs[a].at[_slot_of_group(2 * to + c) // 2]
                    if k == 0:
                        local.append(pltpu.make_async_copy(piece, outs[a].at[chip], local_sems.at[a]))
                    else:
                        dev = (to // 2, to % 2, c)
                        first.append(copy(k - 1, piece, outs[a].at[chip], dev))
                        arrivals.append(copy(k - 1, piece, outs[a].at[to], dev))
                continue
            mine = srcs[a].at[me] if self.scatter[a] else srcs[a]
            land = outs[a].at[me]
            local.append(pltpu.make_async_copy(mine, land, local_sems.at[a]))
            if self.relayed[a]:
                block = lambda px, py, pc, a=a: outs[a].at[4 * px + 2 * py + pc]
                chips = [(1 - x, y), (x, 1 - y), (1 - x, 1 - y)]
                first.append(copy(0, mine, land, (x, y, 1 - c)))
                arrivals.append(copy(0, mine, block(x, y, 1 - c), (x, y, 1 - c)))
                for j, (px, py) in enumerate(chips):
                    first.append(copy(1 + j, mine, land, (px, py, c)))
                    relay_arrivals.append(copy(1 + j, mine, block(px, py, c), (px, py, c)))
                    passed.append(copy(4 + j, block(px, py, c), block(px, py, c), (x, y, 1 - c)))
                    arrivals.append(copy(4 + j, mine, block(px, py, 1 - c), (x, y, 1 - c)))
                continue
            for k in range(1, N_DEV):
                dev, idx = _peer(k)
                if self.scatter[a]:
                    first.append(copy(k - 1, srcs[a].at[idx], land, dev))
                else:
                    first.append(copy(k - 1, mine, land, dev))
                arrivals.append(copy(k - 1, mine, outs[a].at[idx], dev))
        return local, first, passed, relay_arrivals, arrivals

    def start(self, srcs, outs, sems):
        local, first, _, _, _ = self._copies(srcs, outs, sems)
        for cp in local + first:
            cp.start()

    def relay(self, srcs, outs, sems):
        _, _, passed, relay_arrivals, _ = self._copies(srcs, outs, sems)
        for landed, onward in zip(relay_arrivals, passed):
            landed.wait_recv()
            onward.start()

    def finish(self, srcs, outs, sems):
        local, first, passed, _, arrivals = self._copies(srcs, outs, sems)
        for cp in arrivals:
            cp.wait_recv()
        for cp in first + passed:
            cp.wait_send()
        for cp in local:
            cp.wait()


def _gather_project(x_b, shard):
    t, d = x_b.shape
    tm = _pick(t, MM_TILES)
    nrow = t // tm
    per = N_DEV - 1

    def parties():
        x, y, c = lax.axis_index("x"), lax.axis_index("y"), lax.axis_index("c")
        chips = [(1 - x, y), (x, 1 - y), (1 - x, 1 - y)]
        return (x, y, c), (x, y, 1 - c), [(px, py, c) for px, py in chips], [(px, py, 1 - c) for px, py in chips]

    slot = lambda dev: _slot_of_group(4 * dev[0] + 2 * dev[1] + dev[2])
    ici_step, passed_step = (2, 3, 6), (4, 5, 7)
    me, sibling, over_ici, passed_on = parties()
    by_step = {0: me, 1: sibling, **dict(zip(ici_step, over_ici)), **dict(zip(passed_step, passed_on))}
    order = jnp.stack([slot(by_step[j]) for j in range(N_DEV)]).astype(jnp.int32)

    def body(order_ref, x_ref, shard_ref, h_ref, wall_ref, wbuf, fetch_sem, send_sems, recv_sems, local_sem):
        del order_ref
        me, sibling, over_ici, passed_on = parties()
        j, i = pl.program_id(0), pl.program_id(1)
        land = lambda dev: wall_ref.at[slot(dev)]

        def copy(k, src, block, to):
            return pltpu.make_async_remote_copy(src_ref=src, dst_ref=land(block), send_sem=send_sems.at[k],
                                                recv_sem=recv_sems.at[k], device_id=to, device_id_type=pl.DeviceIdType.MESH)

        def fetch(src):
            cp = pltpu.make_async_copy(src, wbuf, fetch_sem)
            cp.start()
            cp.wait()

        keep = pltpu.make_async_copy(shard_ref, land(me), local_sem)
        first = [copy(0, shard_ref, me, sibling)] + [copy(1 + n, shard_ref, me, dev) for n, dev in enumerate(over_ici)]
        onward = [copy(4 + n, land(dev), dev, sibling) for n, dev in enumerate(over_ici)]

        @pl.when(jnp.logical_and(i == 0, j == 0))
        def _():
            keep.start()
            for cp in first:
                cp.start()
            fetch(shard_ref)

        @pl.when(jnp.logical_and(i == 0, j == 1))
        def _():
            copy(0, shard_ref, sibling, me).wait_recv()
            fetch(land(sibling))

        for n, dev in enumerate(over_ici):
            @pl.when(jnp.logical_and(i == 0, j == ici_step[n]))
            def _(n=n, dev=dev):
                copy(1 + n, shard_ref, dev, me).wait_recv()
                onward[n].start()
                fetch(land(dev))

        for n, dev in enumerate(passed_on):
            @pl.when(jnp.logical_and(i == 0, j == passed_step[n]))
            def _(n=n, dev=dev):
                copy(4 + n, shard_ref, dev, me).wait_recv()
                fetch(land(dev))

        h_ref[...] = _dot(x_ref[...], wbuf[...])

        @pl.when(jnp.logical_and(i == nrow - 1, j == N_DEV - 1))
        def _():
            for cp in first + onward:
                cp.wait_send()
            keep.wait()

    need = 2 * (tm * d * 2 + tm * d * 4) + d * d * 2 + tm * d * 4
    h, w_all = pl.pallas_call(
        body,
        grid_spec=pltpu.PrefetchScalarGridSpec(
            num_scalar_prefetch=1,
            grid=(N_DEV, nrow),
            in_specs=[pl.BlockSpec((tm, d), lambda j, i, order: (i, 0)), pl.BlockSpec(memory_space=pl.ANY)],
            out_specs=[pl.BlockSpec((None, tm, d), lambda j, i, order: (order[j], i, 0)), pl.BlockSpec(memory_space=pl.ANY)],
            scratch_shapes=[pltpu.VMEM((d, d), BF16), pltpu.SemaphoreType.DMA, pltpu.SemaphoreType.DMA((per,)),
                            pltpu.SemaphoreType.DMA((per,)), pltpu.SemaphoreType.DMA],
        ),
        out_shape=[jax.ShapeDtypeStruct((N_DEV, t, d), F32), jax.ShapeDtypeStruct((N_DEV, d, d), BF16)],
        compiler_params=_params(("arbitrary", "arbitrary"), need),
        name="gather_project",
    )(order, x_b, shard)
    return h, w_all


def _pair_sums(g):
    n, r, cols = g.shape
    half = n // 2

    def swap(g_ref, got_ref, send_sems, recv_sems):
        x, y, c = lax.axis_index("x"), lax.axis_index("y"), lax.axis_index("c")
        copies = [pltpu.make_async_remote_copy(
            src_ref=g_ref.at[2 * j + 1 - c], dst_ref=got_ref.at[j], send_sem=send_sems.at[j], recv_sem=recv_sems.at[j],
            device_id=(x, y, 1 - c), device_id_type=pl.DeviceIdType.MESH) for j in range(half)]
        for cp in copies:
            cp.start()
        for cp in copies:
            cp.wait()

    got = pl.pallas_call(
        swap,
        in_specs=[pl.BlockSpec(memory_space=pl.ANY)],
        out_specs=pl.BlockSpec(memory_space=pl.ANY),
        out_shape=jax.ShapeDtypeStruct((half, r, cols), g.dtype),
        scratch_shapes=[pltpu.SemaphoreType.DMA((half,))] * 2,
        name="pair_swap",
    )(g)

    def add(mine_ref, got_ref, out_ref):
        mine = jnp.where(lax.axis_index("c") == 0, mine_ref[0].astype(F32), mine_ref[1].astype(F32))
        out_ref[...] = (mine + got_ref[...].astype(F32)).astype(out_ref.dtype)

    tile = _pick(r, (512, 256, 128))
    return pl.pallas_call(
        add,
        grid=(half, r // tile),
        in_specs=[pl.BlockSpec((None, 2, tile, cols), lambda j, i: (j, 0, i, 0)),
                  pl.BlockSpec((None, tile, cols), lambda j, i: (j, i, 0))],
        out_specs=pl.BlockSpec((None, tile, cols), lambda j, i: (j, i, 0)),
        out_shape=jax.ShapeDtypeStruct((half, r, cols), g.dtype),
        compiler_params=_params(("parallel", "parallel")),
        name="pair_add",
    )(g.reshape(half, 2, r, cols), got)


def _local_step(x, p, tgt, small, comm):
    t = x.shape[0]
    tile = _pick(t, (256, 128))
    d = D_MODEL
    act_b, act_f = (d, BF16), (d, F32)
    x_b, p_b = x.astype(BF16), p.astype(BF16)

    chunk_id = jnp.arange(SGU_BLOCK) // CHUNK
    mask = chunk_id[:, None] >= chunk_id[None, :]
    wm = jnp.where(mask[None], small["sgu_w_s"], 0.0)
    wm_b = wm.astype(BF16)
    wm_t = jnp.swapaxes(wm, 1, 2).astype(BF16)
    bs_t = small["sgu_b_s"].T

    h, w_in = comm.project_in(x_b)
    y_a, got_a = _sgu_fwd(h, wm_b, bs_t, small["sgu_norm_g"], small["sgu_norm_b"], ex=comm.weights_exchange(0))
    y_b, o_all, states, got_b = _hgrn_fwd(h, small["lb_logits"], small["hgrn_norm_g"], ex=comm.weights_exchange(1))
    wts, conv_w = comm.weights(list(got_a) + list(got_b))
    z_a = _mm(y_a, wts["w_a"], out_dtype=F32, name="mm_za")
    z_b = _mm(y_b, wts["w_b"], out_dtype=F32, name="mm_zb")
    gates = [(h, SLOT_GA), (h, SLOT_GB)]
    merged, = _rowwise(_merge_fwd, gates + [z_a, z_b], [], [act_b], [], tile=tile, name="merge_fwd")
    r1 = _mm(merged, wts["w_o"], out_dtype=F32, name="mm_r1")
    x1, x1_b = _rowwise(_ln1_fwd, [x, r1], [small["ln1_g"], small["ln1_b"]], [act_f, act_b], [], tile=tile, name="ln1_fwd")
    gate = _mm(x1_b, wts["w_g"], out_dtype=F32, name="mm_gate")
    val = _mm(x1_b, wts["w_v"], out_dtype=F32, name="mm_val")
    pg = _mm(x1_b, wts["w_pg"], out_dtype=F32, name="mm_pg")
    pp = _mm(p_b, wts["w_pp"], out_dtype=F32, name="mm_pp")
    hid, = _rowwise(functools.partial(_conv_fwd, tile), [gate, (gate, _halo_spec(D_FF, tile, t, False)), val],
                    [conv_w, small["conv_b"]], [(D_FF, BF16)], [], tile=tile, name="conv_fwd")
    fo = _mm(hid, wts["w_down"], out_dtype=F32, name="mm_down")
    dz2, dz2_b, dpg, dpp, loss, dg2, db2 = _rowwise(
        _ln2_loss, [x1, fo, pg, pp, tgt], [small["ln2_g"], small["ln2_b"]],
        [act_f, act_b, act_b, act_b], [(8, LANE), (1, d), (1, d)], tile=tile, name="ln2_loss")

    dhid = _mm(dz2_b, wts["w_down"], out_dtype=BF16, name="mm_dhid", trans_b=True)
    g_down = _mm_tn(hid, dz2_b, out_dtype=BF16, name="mm_g_down")
    dcg, dval, dcw0, dcw1, dcw2, dcb = _rowwise(
        functools.partial(_conv_bwd_a, tile), [gate, (gate, _halo_spec(D_FF, tile, t, False)), val, dhid],
        [conv_w, small["conv_b"]], [(D_FF, BF16), (D_FF, BF16)], [(1, D_FF)] * 4, tile=tile, name="conv_bwd_a")
    dgate, = _rowwise(functools.partial(_conv_bwd_b, tile), [dcg, (dcg, _halo_spec(D_FF, tile, t, True, rows=16))],
                      [conv_w], [(D_FF, BF16)], [], tile=tile, name="conv_bwd_b")
    g_g = _mm_tn(x1_b, dgate, out_dtype=BF16, name="mm_g_gate")
    g_v = _mm_tn(x1_b, dval, out_dtype=BF16, name="mm_g_val")
    g_pg = _mm_tn(x1_b, dpg, out_dtype=BF16, name="mm_g_pg")
    g_pp = _mm_tn(p_b, dpp, out_dtype=BF16, name="mm_g_pp")
    dx1 = _mm(dgate, wts["w_g"], out_dtype=F32, name="mm_dx1_gate", trans_b=True, adds=[(dz2, ALPHA)])
    dx1 = _mm(dval, wts["w_v"], out_dtype=F32, name="mm_dx1_val", trans_b=True, adds=[(dx1, 1.0)])
    dx1 = _mm(dpg, wts["w_pg"], out_dtype=F32, name="mm_dx1_pg", trans_b=True, adds=[(dx1, 1.0)])
    dz1, dz1_b, dg1, db1 = _rowwise(_ln1_bwd, [x, r1, dx1], [small["ln1_g"]], [act_f, act_b], [(1, d), (1, d)],
                                    tile=tile, name="ln1_bwd")
    g_o = _mm_tn(merged, dz1_b, out_dtype=BF16, name="mm_g_o")
    dm = _mm(dz1_b, wts["w_o"], out_dtype=BF16, name="mm_dm", trans_b=True)
    dh, dza, dzb = _rowwise(_merge_bwd, gates + [z_a, z_b, dm], [],
                            [("stack", 2, SLOT_GA // 2, 8, d, BF16), act_b, act_b], [], tile=tile, name="merge_bwd")
    g_a = _mm_tn(y_a, dza, out_dtype=BF16, name="mm_g_a")
    g_b = _mm_tn(y_b, dzb, out_dtype=BF16, name="mm_g_b")
    dy_a = _mm(dza, wts["w_a"], out_dtype=BF16, name="mm_dya", trans_b=True)
    dy_b = _mm(dzb, wts["w_b"], out_dtype=F32, name="mm_dyb", trans_b=True)
    dh, dws, dbs, dgv_n, dbv_n = _sgu_bwd(h, dy_a, dh, wm_b, wm_t, bs_t, small["sgu_norm_g"], small["sgu_norm_b"])
    big = dict(w_a=g_a, w_b=g_b, w_o=g_o, w_g=g_g, w_v=g_v, w_down=g_down, w_pp=g_pp, w_pg=g_pg)
    sm = dict(sgu_w_s=jnp.where(mask[None], dws, 0.0), sgu_b_s=dbs[:, :GROUPS].T, sgu_norm_g=dgv_n, sgu_norm_b=dbv_n,
              ln1_g=dg1, ln1_b=db1, conv_w=jnp.concatenate([dcw0, dcw1, dcw2], axis=0), conv_b=dcb, ln2_g=dg2, ln2_b=db2,
              loss=loss)
    dh, dlogits, dgn, got = _hgrn_bwd(h, o_all, dy_b, states, dh, small["lb_logits"], small["hgrn_norm_g"],
                                      ex=comm.grads_exchange(big, sm))
    comm.grads_done(got)
    g_in = _mm_tn(x_b, dh, out_dtype=BF16, name="mm_g_in")
    ex = comm.last_exchange(g_in, dict(lb_logits=dlogits, hgrn_norm_g=dgn))
    res = _mm(dh, w_in, out_dtype=F32, name="mm_dx", trans_b=True, reduce_b=True, adds=[(dz1, ALPHA)], ex=ex)
    grad_x, got = res if ex else (res, ())
    comm.last_done(got)
    return grad_x


_SMALL_EARLY = ["sgu_w_s", "sgu_b_s", "sgu_norm_g", "sgu_norm_b", "ln1_g", "ln1_b", "ffn_conv_b", "ln2_g", "ln2_b"]
_SMALL_LATE = ["hgrn_lb_logits", "hgrn_norm_g"]
N_TAPS = D_FF // N_DEV
UP_COLS = 2 * D_FF // N_DEV


class _StepExchanges:
    def __init__(self, w_in_shard, shards):
        self.w_in_shard = w_in_shard
        self.shards = shards

    def project_in(self, x_b):
        return _gather_project(x_b, self.w_in_shard)

    def weights_exchange(self, part):
        return _Exchange([(s, "gather+relay") for s in (self.shards[:2] if part == 0 else self.shards[2:])])

    def weights(self, got):
        d, f = D_MODEL, D_FF
        w_br_g, w_o_g, w_up_g, w_down_g, w_pp_g, w_pg_g, conv_g = got
        w_br = w_br_g.transpose(1, 0, 2, 3).reshape(2, d, d)
        w_up = w_up_g.transpose(1, 0, 2).reshape(d, 2, f).transpose(1, 0, 2)
        wts = dict(w_a=w_br[0], w_b=w_br[1], w_o=w_o_g.reshape(d, d), w_g=w_up[0], w_v=w_up[1],
                   w_down=w_down_g.reshape(f, d), w_pp=w_pp_g.transpose(1, 0, 2).reshape(256, d), w_pg=w_pg_g.reshape(d, d))
        return wts, conv_g.transpose(1, 0, 2).reshape(3, f)

    def grads_exchange(self, big, sm):
        d = D_MODEL
        parts = [jnp.stack([big["w_a"], big["w_b"]]).reshape(2, N_DEV, 128, d).transpose(1, 0, 2, 3),
                 big["w_o"].reshape(N_DEV, 128, d),
                 jnp.concatenate([big["w_g"], big["w_v"]], axis=1).reshape(d, N_DEV, UP_COLS).transpose(1, 0, 2),
                 big["w_down"].reshape(N_DEV, N_TAPS, d),
                 big["w_pp"].reshape(256, N_DEV, 128).transpose(1, 0, 2),
                 big["w_pg"].reshape(N_DEV, 128, d)]
        packed, self.rows_early = _pack([sm[k] for k in ("sgu_w_s", "sgu_b_s", "sgu_norm_g", "sgu_norm_b", "ln1_g", "ln1_b",
                                                         "conv_b", "ln2_g", "ln2_b", "conv_w", "loss")])
        return _Exchange([(a, "scatter") for a in parts] + [(packed, "gather")])

    def grads_done(self, got):
        self.recv, self.small_early = got[:6], got[6]

    def last_exchange(self, g_in, sm):
        packed, self.rows_late = _pack([sm["lb_logits"], sm["hgrn_norm_g"]])
        return _Exchange([(_pair_sums(g_in), "scatter+pairs"), (packed, "gather")])

    def last_done(self, got):
        self.recv_in, self.small_late = got


def _rows128(a):
    flat = a.reshape(-1)
    rows = -(-flat.shape[0] // (8 * LANE)) * 8
    return jnp.pad(flat, (0, rows * LANE - flat.shape[0])).reshape(rows, LANE)


def _pack(parts):
    blocks = [_rows128(a) for a in parts]
    return jnp.concatenate(blocks, axis=0), [b.shape[0] for b in blocks]


def _unpack(packed, shapes, rows):
    out, r0 = [], 0
    for shp, r in zip(shapes, rows):
        n = math.prod(shp)
        out.append(packed[r0:r0 + r].reshape(-1)[:n].reshape(shp))
        r0 += r
    return out


def kernel(x, p, w_in, sgu_w_s, sgu_b_s, sgu_norm_g, sgu_norm_b, hgrn_lb_logits, hgrn_norm_g, w_branch, w_out, ln1_g, ln1_b, ffn_w_up, ffn_conv_w, ffn_conv_b, ffn_w_down, ln2_g, ln2_b, ple_w_proj, ple_w_gate, loss_target, m_w_in, m_sgu_w_s, m_sgu_b_s, m_sgu_norm_g, m_sgu_norm_b, m_hgrn_lb_logits, m_hgrn_norm_g, m_w_branch, m_w_out, m_ln1_g, m_ln1_b, m_ffn_w_up, m_ffn_conv_w, m_ffn_conv_b, m_ffn_w_down, m_ln2_g, m_ln2_b, m_ple_w_proj, m_ple_w_gate, v_w_in, v_sgu_w_s, v_sgu_b_s, v_sgu_norm_g, v_sgu_norm_b, v_hgrn_lb_logits, v_hgrn_norm_g, v_w_branch, v_w_out, v_ln1_g, v_ln1_b, v_ffn_w_up, v_ffn_conv_w, v_ffn_conv_b, v_ffn_w_down, v_ln2_g, v_ln2_b, v_ple_w_proj, v_ple_w_gate):
    weights = dict(w_in=w_in, sgu_w_s=sgu_w_s, sgu_b_s=sgu_b_s, sgu_norm_g=sgu_norm_g, sgu_norm_b=sgu_norm_b,
                   hgrn_lb_logits=hgrn_lb_logits, hgrn_norm_g=hgrn_norm_g, w_branch=w_branch, w_out=w_out,
                   ln1_g=ln1_g, ln1_b=ln1_b, ffn_w_up=ffn_w_up, ffn_conv_w=ffn_conv_w, ffn_conv_b=ffn_conv_b,
                   ffn_w_down=ffn_w_down, ln2_g=ln2_g, ln2_b=ln2_b, ple_w_proj=ple_w_proj, ple_w_gate=ple_w_gate)
    mom_m = dict(w_in=m_w_in, sgu_w_s=m_sgu_w_s, sgu_b_s=m_sgu_b_s, sgu_norm_g=m_sgu_norm_g, sgu_norm_b=m_sgu_norm_b,
                 hgrn_lb_logits=m_hgrn_lb_logits, hgrn_norm_g=m_hgrn_norm_g, w_branch=m_w_branch, w_out=m_w_out,
                 ln1_g=m_ln1_g, ln1_b=m_ln1_b, ffn_w_up=m_ffn_w_up, ffn_conv_w=m_ffn_conv_w, ffn_conv_b=m_ffn_conv_b,
                 ffn_w_down=m_ffn_w_down, ln2_g=m_ln2_g, ln2_b=m_ln2_b, ple_w_proj=m_ple_w_proj, ple_w_gate=m_ple_w_gate)
    mom_v = dict(w_in=v_w_in, sgu_w_s=v_sgu_w_s, sgu_b_s=v_sgu_b_s, sgu_norm_g=v_sgu_norm_g, sgu_norm_b=v_sgu_norm_b,
                 hgrn_lb_logits=v_hgrn_lb_logits, hgrn_norm_g=v_hgrn_norm_g, w_branch=v_w_branch, w_out=v_w_out,
                 ln1_g=v_ln1_g, ln1_b=v_ln1_b, ffn_w_up=v_ffn_w_up, ffn_conv_w=v_ffn_conv_w, ffn_conv_b=v_ffn_conv_b,
                 ffn_w_down=v_ffn_w_down, ln2_g=v_ln2_g, ln2_b=v_ln2_b, ple_w_proj=v_ple_w_proj, ple_w_gate=v_ple_w_gate)
    d, f = D_MODEL, D_FF
    me = _my_index()

    comm = _StepExchanges(w_in[0].astype(BF16),
                          [w_branch[0].astype(BF16), w_out[0].astype(BF16), ffn_w_up[0].astype(BF16),
                           ffn_w_down[0].astype(BF16), ple_w_proj[0].astype(BF16), ple_w_gate[0].astype(BF16), ffn_conv_w[0]])
    small = dict(sgu_w_s=sgu_w_s[0], sgu_b_s=sgu_b_s[0], sgu_norm_g=sgu_norm_g, sgu_norm_b=sgu_norm_b,
                 lb_logits=hgrn_lb_logits, hgrn_norm_g=hgrn_norm_g, ln1_g=ln1_g, ln1_b=ln1_b, ln2_g=ln2_g, ln2_b=ln2_b,
                 conv_b=ffn_conv_b)
    grad_x = _local_step(x[0], p[0, 0], loss_target[0], small, comm)

    out = {}

    def adam(name, parts8, shape2d):
        w2, m2, v2 = (a.reshape(shape2d) for a in (weights[name], mom_m[name], mom_v[name]))
        res = _adam_call(w2, m2, v2, parts8.reshape(parts8.shape[:1] + shape2d), "adam_" + name)
        out[name] = tuple(r.reshape(weights[name].shape) for r in res)

    adam("w_in", comm.recv_in, (d, d))
    adam("w_branch", comm.recv[0], (256, d))
    adam("w_out", comm.recv[1], (128, d))
    adam("ffn_w_up", comm.recv[2], (d, UP_COLS))
    adam("ffn_w_down", comm.recv[3], (N_TAPS, d))
    adam("ple_w_proj", comm.recv[4], (256, 128))
    adam("ple_w_gate", comm.recv[5], (128, d))

    def adam_small(names, extra_w, extra_m, extra_v, extra_shapes, parts8, rows, label):
        pk = lambda src, extra: _pack([src[n] for n in names] + extra)[0]
        res = _adam_call(pk(weights, extra_w), pk(mom_m, extra_m), pk(mom_v, extra_v), parts8, label)
        shapes = [weights[n].shape for n in names] + extra_shapes
        unpacked = [_unpack(r, shapes, rows) for r in res]
        for j, n in enumerate(names):
            out[n] = tuple(u[j] for u in unpacked)
        return [[u[len(names) + j] for u in unpacked] for j in range(len(extra_shapes))]

    blank = jnp.zeros((8, LANE), F32)
    taps, loss_rows = adam_small(
        _SMALL_EARLY, [_place_taps(ffn_conv_w[0], me, f), blank], [_place_taps(m_ffn_conv_w[0], me, f), blank],
        [_place_taps(v_ffn_conv_w[0], me, f), blank + 1.0], [(3, f), (8, LANE)], comm.small_early, comm.rows_early,
        "adam_small_early")
    adam_small(_SMALL_LATE, [], [], [], [], comm.small_late, comm.rows_late, "adam_small_late")
    out["ffn_conv_w"] = tuple(lax.dynamic_slice_in_dim(u, me * N_TAPS, N_TAPS, axis=1)[None] for u in taps)
    loss = loss_rows[0][0, 0]

    order = ["w_in", "sgu_w_s", "sgu_b_s", "sgu_norm_g", "sgu_norm_b", "hgrn_lb_logits", "hgrn_norm_g", "w_branch", "w_out",
             "ln1_g", "ln1_b", "ffn_w_up", "ffn_conv_w", "ffn_conv_b", "ffn_w_down", "ln2_g", "ln2_b", "ple_w_proj", "ple_w_gate"]
    return (loss, grad_x[None], *[out[n][0] for n in order], *[out[n][1] for n in order],
            *[out[n][2] for n in order], *[out[n][3] for n in order])


def _place_taps(shard, me, f):
    return lax.dynamic_update_slice_in_dim(jnp.zeros((3, f), F32), shard, me * N_TAPS, axis=1)
```

```python
import functools
import math

import jax
import jax.numpy as jnp
from jax import lax
from jax.experimental import pallas as pl
from jax.experimental.pallas import tpu as pltpu

F32 = jnp.float32
BF16 = jnp.bfloat16

N_DEV = 8
N_CHIP = 4
D_MODEL = 1024
CHUNK = 64
SUB = 16
SGU_BLOCK = 128
GROUPS = 8
HEAD = 128
HEADS = 8
CHUNKS_PER_STEP = 2
HEAD_UNROLL = 8
D_FF = 2816
LN_EPS = 1e-5
RMS_EPS = 1e-6
ALPHA = 2.0 ** 0.25
GELU_K = math.sqrt(2.0 / math.pi)
GELU_C = 0.044715
NEG = -1e30
ADAM_LR, ADAM_B1, ADAM_B2, ADAM_EPS, ADAM_WD, ADAM_STEP = 0.001, 0.9, 0.999, 1e-08, 0.01, 10
LANE = 128
SLOT_Q, SLOT_F, SLOT_I, SLOT_OG, SLOT_U, SLOT_V, SLOT_GA, SLOT_GB = range(8)


def _slot_of_group(k):
    return jnp.where(k < 2, k + 4, jnp.where(k < 6, k - 2, k))


MIB = 1024 * 1024
VMEM_V7X = 64 * MIB
VMEM_FLOOR = 32 * MIB
MM_TILES = (1024, 1408, 512, 256, 128)


def _params(sem, need=0):
    limit = min(max(need + need // 4, VMEM_FLOOR), VMEM_V7X - 4 * MIB)
    return pltpu.CompilerParams(dimension_semantics=sem, vmem_limit_bytes=limit)


def _pick(n, prefs):
    for t in prefs:
        if n % t == 0:
            return t
    return n


def _gelu(x):
    return 0.5 * x * (1.0 + jnp.tanh(GELU_K * (x + GELU_C * x * x * x)))


def _gelu_and_grad(x):
    x2 = x * x
    t = jnp.tanh(GELU_K * x * (1.0 + GELU_C * x2))
    half = 0.5 * (1.0 + t)
    return x * half, half + 0.5 * x * (1.0 - t * t) * GELU_K * (1.0 + 3.0 * GELU_C * x2)


def _silu_grad(x, s):
    return s * (1.0 + x * (1.0 - s))


def _dot(a, b):
    return jnp.dot(a.astype(BF16), b.astype(BF16), preferred_element_type=F32)


def _dot_nt(a, b):
    return lax.dot_general(a.astype(BF16), b.astype(BF16), (((1,), (1,)), ((), ())), preferred_element_type=F32)


def _dot_tn(a, b):
    return lax.dot_general(a.astype(BF16), b.astype(BF16), (((0,), (0,)), ((), ())), preferred_element_type=F32)


def _mean(x):
    return jnp.mean(x, axis=-1, keepdims=True)


def _sum0(x):
    return jnp.sum(x, axis=0, keepdims=True)


def _mm(a, b, *, out_dtype, name, trans_b=False, reduce_b=False, adds=(), ex=None):
    squeeze = b.ndim == 2
    a3 = a if a.ndim == 3 else a[None]
    b3 = b if b.ndim == 3 else b[None]
    ba, m, k = a3.shape
    bb = b3.shape[0]
    n = b3.shape[1] if trans_b else b3.shape[2]
    tm = _pick(m, MM_TILES)
    tn = _pick(n, MM_TILES)
    if reduce_b:
        bo, steps = 1, bb
        a_map = lambda o, i, j, r: (r if ba > 1 else 0, i, 0)
        b_map = (lambda o, i, j, r: (r, j, 0)) if trans_b else (lambda o, i, j, r: (r, 0, j))
    else:
        bo, steps = bb, 1
        a_map = lambda o, i, j, r: (o if ba > 1 else 0, i, 0)
        b_map = (lambda o, i, j, r: (o, j, 0)) if trans_b else (lambda o, i, j, r: (o, 0, j))
    o_map = lambda o, i, j, r: (o, i, j)
    add_arrays = [x if x.ndim == 3 else x[None] for x, _ in adds]
    add_scales = [s for _, s in adds]
    n_add = len(adds)
    dot = _dot_nt if trans_b else _dot

    def finish(acc, add_refs, o_ref):
        for ref, s in zip(add_refs, add_scales):
            acc = acc + s * ref[...].astype(F32)
        o_ref[...] = acc.astype(o_ref.dtype)

    grid = (bo, m // tm, n // tn, steps)

    def body(*refs):
        ins, (o_ref,), scratch, xrefs = _split_refs(refs, 2 + n_add, 1, 1 if reduce_b else 0, ex)
        a_ref, b_ref, add_refs = ins[0], ins[1], ins[2:]
        step = ((pl.program_id(0) * grid[1] + pl.program_id(1)) * grid[2] + pl.program_id(2)) * grid[3] + pl.program_id(3)
        if ex:
            @pl.when(step == 0)
            def _():
                ex.start(*xrefs)

        if reduce_b:
            acc, = scratch
            r = pl.program_id(3)

            @pl.when(r == 0)
            def _():
                acc[...] = jnp.zeros_like(acc)

            acc[...] += dot(a_ref[...], b_ref[...])

            @pl.when(r == steps - 1)
            def _():
                finish(acc[...], add_refs, o_ref)
        else:
            finish(dot(a_ref[...], b_ref[...]), add_refs, o_ref)

        if ex:
            @pl.when(step == math.prod(grid) - 1)
            def _():
                ex.finish(*xrefs)

    b_block = (None, tn, k) if trans_b else (None, k, tn)
    out_bytes = tm * tn * jnp.dtype(out_dtype).itemsize
    need = 2 * (tm * k * a3.dtype.itemsize + k * tn * b3.dtype.itemsize + out_bytes + n_add * tm * tn * 4)
    need += 2 * tm * tn * 4
    sem = ("arbitrary",) * 4 if ex else ("parallel", "parallel", "parallel", "arbitrary")
    res = pl.pallas_call(
        body,
        grid=grid,
        in_specs=[pl.BlockSpec((None, tm, k), a_map), pl.BlockSpec(b_block, b_map)]
        + [pl.BlockSpec((None, tm, tn), o_map) for _ in adds] + (ex.in_specs if ex else []),
        out_specs=[pl.BlockSpec((None, tm, tn), o_map)] + (ex.out_specs if ex else []),
        out_shape=[jax.ShapeDtypeStruct((bo, m, n), out_dtype)] + (ex.out_shape if ex else []),
        scratch_shapes=([pltpu.VMEM((tm, tn), F32)] if reduce_b else []) + (ex.scratch if ex else []),
        compiler_params=_params(sem, need),
        name=name,
    )(a3, b3, *add_arrays, *(ex.arrays if ex else []))
    out = res[0][0] if (reduce_b or squeeze) else res[0]
    return (out, res[1:]) if ex else out


def _mm_tn(a, b, *, out_dtype, name):
    squeeze = b.ndim == 2
    b3 = b if b.ndim == 3 else b[None]
    t, m = a.shape
    bb, _, n = b3.shape
    tm = _pick(m, MM_TILES)
    tn = _pick(n, MM_TILES)
    tt = _pick(t, (1024, 512, 256, 128))
    steps = t // tt
    need = 2 * (tt * tm * a.dtype.itemsize + tt * tn * b3.dtype.itemsize + tm * tn * jnp.dtype(out_dtype).itemsize)
    need += 2 * tm * tn * 4

    def body(a_ref, b_ref, o_ref, acc):
        r = pl.program_id(3)

        @pl.when(r == 0)
        def _():
            acc[...] = jnp.zeros_like(acc)

        acc[...] += _dot_tn(a_ref[...], b_ref[...])

        @pl.when(r == steps - 1)
        def _():
            o_ref[...] = acc[...].astype(o_ref.dtype)

    out = pl.pallas_call(
        body,
        grid=(bb, m // tm, n // tn, steps),
        in_specs=[pl.BlockSpec((tt, tm), lambda o, i, j, r: (r, i)),
                  pl.BlockSpec((None, tt, tn), lambda o, i, j, r: (o, r, j))],
        out_specs=pl.BlockSpec((None, tm, tn), lambda o, i, j, r: (o, i, j)),
        out_shape=jax.ShapeDtypeStruct((bb, m, n), out_dtype),
        scratch_shapes=[pltpu.VMEM((tm, tn), F32)],
        compiler_params=_params(("parallel", "parallel", "parallel", "arbitrary"), need),
        name=name,
    )(a, b3)
    return out[0] if squeeze else out


def _rowwise(fn, rows, consts, row_outs, acc_outs, *, tile, name):
    first = rows[0][0] if isinstance(rows[0], tuple) else rows[0]
    t = first.shape[-2]
    steps = t // tile
    arrays, in_specs = [], []
    for r in rows:
        if isinstance(r, tuple) and isinstance(r[1], pl.BlockSpec):
            arrays.append(r[0])
            in_specs.append(r[1])
        elif isinstance(r, tuple):
            arr, bidx = r
            arrays.append(arr)
            in_specs.append(pl.BlockSpec((None, tile, arr.shape[-1]), functools.partial(lambda i, b: (b, i, 0), b=bidx)))
        else:
            arrays.append(r)
            in_specs.append(pl.BlockSpec((tile, r.shape[-1]), lambda i: (i, 0)))
    for c in consts:
        arrays.append(c)
        in_specs.append(pl.BlockSpec(c.shape, lambda i: (0, 0)))
    n_in, n_row = len(arrays), len(row_outs)
    out_shape, out_specs = [], []
    for ro in row_outs:
        if ro[0] == "stack":
            _, cnt, blk, total, w, dt = ro
            out_shape.append(jax.ShapeDtypeStruct((total, t, w), dt))
            out_specs.append(pl.BlockSpec((cnt, tile, w), functools.partial(lambda i, b: (b, i, 0), b=blk)))
        else:
            w, dt = ro
            out_shape.append(jax.ShapeDtypeStruct((t, w), dt))
            out_specs.append(pl.BlockSpec((tile, w), lambda i: (i, 0)))
    out_shape += [jax.ShapeDtypeStruct(s, F32) for s in acc_outs]
    out_specs += [pl.BlockSpec(s, lambda i: (0, 0)) for s in acc_outs]
    blocks = [math.prod(d for d in sp.block_shape if d) * arr.dtype.itemsize for sp, arr in zip(in_specs, arrays)]
    blocks += [math.prod(d for d in sp.block_shape if d) * jnp.dtype(sh.dtype).itemsize
               for sp, sh in zip(out_specs, out_shape)]
    need = 2 * sum(blocks) + 6 * tile * max(a.shape[-1] for a in arrays) * 4

    def body(*refs):
        ins, outs = refs[:n_in], refs[n_in:]
        i = pl.program_id(0)
        res = fn(i, steps, *[r[...] for r in ins])
        res = res if isinstance(res, (tuple, list)) else (res,)
        for ref, val in zip(outs[:n_row], res[:n_row]):
            ref[...] = val.astype(ref.dtype)
        if acc_outs:
            @pl.when(i == 0)
            def _():
                for ref in outs[n_row:]:
                    ref[...] = jnp.zeros_like(ref)

            for ref, val in zip(outs[n_row:], res[n_row:]):
                ref[...] += val

    return pl.pallas_call(
        body,
        grid=(steps,),
        in_specs=in_specs,
        out_specs=out_specs,
        out_shape=out_shape,
        compiler_params=_params(("arbitrary",), need),
        name=name,
    )(*arrays)


def _ln_stats(z):
    mu = _mean(z)
    zc = z - mu
    rstd = lax.rsqrt(_mean(zc * zc) + LN_EPS)
    return zc * rstd, rstd


def _ln_bwd(dy, xhat, rstd, g):
    dxh = dy * g
    return rstd * (dxh - _mean(dxh) - xhat * _mean(dxh * xhat))


def _ride(ex, xrefs, step, steps):
    if not ex:
        return
    for at, act in ((0, ex.start), (steps - 2, ex.relay), (steps - 1, ex.finish)):
        @pl.when(step == at)
        def _(act=act):
            act(*xrefs)


def _sgu_fwd(h, wm, bs_t, g_v, b_v, ex=None):
    t = h.shape[1]
    steps = t // SGU_BLOCK

    def body(*refs):
        (u_ref, v_ref, wm_ref, bs_ref, g_ref, b_ref), (y_ref,), _, xrefs = _split_refs(refs, 6, 1, 0, ex)
        xhat, _ = _ln_stats(_gelu(v_ref[...]))
        vn = (xhat * g_ref[...] + b_ref[...]).astype(BF16)
        gu = _gelu(u_ref[...])
        for g in range(GROUPS):
            sl = slice(g * HEAD, (g + 1) * HEAD)
            mixed = _dot(wm_ref[g], vn[:, sl]) + bs_ref[:, g:g + 1]
            y_ref[:, sl] = (gu[:, sl] * mixed).astype(BF16)
        _ride(ex, xrefs, pl.program_id(0), steps)

    blk = lambda b: pl.BlockSpec((None, SGU_BLOCK, D_MODEL), functools.partial(lambda i, b: (b, i, 0), b=b))
    whole = lambda s: pl.BlockSpec(s, lambda i: (0,) * len(s))
    res = pl.pallas_call(
        body,
        grid=(steps,),
        in_specs=[blk(SLOT_U), blk(SLOT_V), whole(wm.shape), whole(bs_t.shape), whole(g_v.shape), whole(b_v.shape)]
        + (ex.in_specs if ex else []),
        out_specs=[pl.BlockSpec((SGU_BLOCK, D_MODEL), lambda i: (i, 0))] + (ex.out_specs if ex else []),
        out_shape=[jax.ShapeDtypeStruct((t, D_MODEL), BF16)] + (ex.out_shape if ex else []),
        scratch_shapes=ex.scratch if ex else [],
        compiler_params=_params(("arbitrary",)),
        name="sgu_fwd",
    )(h, h, wm, bs_t, g_v, b_v, *(ex.arrays if ex else []))
    return res[0], res[1:]


def _sgu_bwd(h, dy, dh, wm, wm_t, bs_t, g_v, b_v):
    t = h.shape[1]

    def body(u_ref, v_ref, dy_ref, dh_in, wm_ref, wmt_ref, bs_ref, g_ref, b_ref,
             duv_ref, dw_ref, dbs_ref, dg_ref, db_ref, dvn_ref):
        del dh_in
        du_ref, dv_ref = duv_ref.at[0], duv_ref.at[1]
        i = pl.program_id(0)

        @pl.when(i == 0)
        def _():
            dw_ref[...] = jnp.zeros_like(dw_ref)
            dbs_ref[...] = jnp.zeros_like(dbs_ref)
            dg_ref[...] = jnp.zeros_like(dg_ref)
            db_ref[...] = jnp.zeros_like(db_ref)

        u = u_ref[...]
        v = v_ref[...]
        gv, gvp = _gelu_and_grad(v)
        xhat, rstd = _ln_stats(gv)
        vn = (xhat * g_ref[...] + b_ref[...]).astype(BF16)
        gu, gup = _gelu_and_grad(u)
        lane = lax.broadcasted_iota(jnp.int32, (SGU_BLOCK, LANE), 1)
        dbs = jnp.zeros((SGU_BLOCK, LANE), F32)
        for g in range(GROUPS):
            sl = slice(g * HEAD, (g + 1) * HEAD)
            vn_g = vn[:, sl]
            mixed = _dot(wm_ref[g], vn_g) + bs_ref[:, g:g + 1]
            dy_g = dy_ref[:, sl]
            du_ref[:, sl] = (dy_g * mixed * gup[:, sl]).astype(BF16)
            dmix = dy_g * gu[:, sl]
            dmb = dmix.astype(BF16)
            dvn_ref[:, sl] = _dot(wmt_ref[g], dmb)
            dw_ref[g] += _dot_nt(dmb, vn_g)
            dbs = dbs + jnp.where(lane == g, jnp.sum(dmix, axis=1, keepdims=True), 0.0)
        dbs_ref[...] += dbs
        dvn = dvn_ref[...]
        dg_ref[...] += _sum0(dvn * xhat)
        db_ref[...] += _sum0(dvn)
        dv_ref[...] = (_ln_bwd(dvn, xhat, rstd, g_ref[...]) * gvp).astype(BF16)

    blk = lambda b: pl.BlockSpec((None, SGU_BLOCK, D_MODEL), functools.partial(lambda i, b: (b, i, 0), b=b))
    row = pl.BlockSpec((SGU_BLOCK, D_MODEL), lambda i: (i, 0))
    whole = lambda s: pl.BlockSpec(s, lambda i: (0,) * len(s))
    vec = (1, D_MODEL)
    return pl.pallas_call(
        body,
        grid=(t // SGU_BLOCK,),
        in_specs=[blk(SLOT_U), blk(SLOT_V), row, pl.BlockSpec(memory_space=pl.ANY),
                  whole(wm.shape), whole(wm_t.shape), whole(bs_t.shape), whole(vec), whole(vec)],
        out_specs=[pl.BlockSpec((2, SGU_BLOCK, D_MODEL), lambda i: (SLOT_U // 2, i, 0)),
                   whole(wm.shape), whole((SGU_BLOCK, LANE)), whole(vec), whole(vec)],
        out_shape=[jax.ShapeDtypeStruct(dh.shape, BF16),
                   jax.ShapeDtypeStruct(wm.shape, F32), jax.ShapeDtypeStruct((SGU_BLOCK, LANE), F32),
                   jax.ShapeDtypeStruct(vec, F32), jax.ShapeDtypeStruct(vec, F32)],
        scratch_shapes=[pltpu.VMEM((SGU_BLOCK, D_MODEL), F32)],
        input_output_aliases={3: 0},
        compiler_params=_params(("arbitrary",)),
        name="sgu_bwd",
    )(h, h, dy, dh, wm, wm_t, bs_t, g_v, b_v)


def _split3(x):
    hi = x.astype(BF16)
    r1 = x - hi.astype(F32)
    mid = r1.astype(BF16)
    lo = (r1 - mid.astype(F32)).astype(BF16)
    return hi, mid, lo


def _tri_matmul(tri, x):
    hi, mid, lo = _split3(x)
    dot = lambda p: jnp.dot(tri, p, preferred_element_type=F32)
    return dot(hi) + dot(mid) + dot(lo)


def _lower_bound(logits):
    l0, l1 = logits[0:1, :], logits[1:2, :]
    mx = jnp.maximum(l0, l1)
    e0, e1 = jnp.exp(l0 - mx), jnp.exp(l1 - mx)
    return e0 / (e0 + e1)


def _hgrn_gates(q_raw, f_raw, lb):
    q = q_raw * jax.nn.sigmoid(q_raw)
    sig = jax.nn.sigmoid(f_raw)
    f = lb + (1.0 - lb) * sig
    row = lax.broadcasted_iota(jnp.int32, (CHUNK, CHUNK), 0)
    col = lax.broadcasted_iota(jnp.int32, (CHUNK, CHUNK), 1)
    c = _tri_matmul((row >= col).astype(BF16), jnp.log(f))
    return q, sig, f, 1.0 - f, c


def _offdiag_terms(qh, kh, ch, tb):
    rows = slice(tb * SUB, (tb + 1) * SUB)
    r = ch[tb * SUB - 1:tb * SUB, :]
    eqh = jnp.exp(ch[rows] - r)
    ekh = jnp.exp(jnp.minimum(r - ch, 0.0))
    return rows, eqh, qh[rows] * eqh, ekh, kh * ekh


def _diag_decay(cb, s, trow):
    return jnp.exp(jnp.where(trow >= s, cb - cb[s:s + 1, :], NEG))


def _split_refs(refs, n_in, n_out, n_scratch, ex):
    nx = ex.n if ex else 0
    ins, refs = refs[:n_in], refs[n_in:]
    xsrc, refs = refs[:nx], refs[nx:]
    outs, refs = refs[:n_out], refs[n_out:]
    xout, refs = refs[:nx], refs[nx:]
    return ins, outs, refs[:n_scratch], (xsrc, xout, refs[n_scratch:])


def _hgrn_fwd(h, logits, g_norm, ex=None):
    t = h.shape[1]
    nc = t // CHUNK
    per_step = CHUNKS_PER_STEP if nc % CHUNKS_PER_STEP == 0 else 1
    steps = nc // per_step

    def body(*refs):
        ins, outs, scratch, xrefs = _split_refs(refs, 3, 3, 4, ex)
        hgrn_ref, lg_ref, gn_ref = ins
        y_ref, o_ref, sall_ref = outs
        st_ref, q_s, k_s, c_s = scratch

        @pl.when(pl.program_id(0) == 0)
        def _():
            st_ref[...] = jnp.zeros_like(st_ref)

        lb = _lower_bound(lg_ref[...])
        for sub in range(per_step):
            rows = pl.ds(sub * CHUNK, CHUNK)
            chunk(lb, sall_ref.at[sub], *(r.at[rows] for r in (hgrn_ref.at[SLOT_Q], hgrn_ref.at[SLOT_F], hgrn_ref.at[SLOT_I],
                                                               hgrn_ref.at[SLOT_OG], y_ref, o_ref)),
                  gn_ref, st_ref, q_s.at[sub], k_s.at[sub], c_s.at[sub])
        _ride(ex, xrefs, pl.program_id(0), steps)

    def chunk(lb, sall_ref, q_ref, f_ref, i_ref, og_ref, y_ref, o_ref, gn_ref, st_ref, q_s, k_s, c_s):
        q, _, _, k, c = _hgrn_gates(q_ref[...], f_ref[...], lb)
        q_s[...] = q
        k_s[...] = k
        c_s[...] = c
        col64 = lax.broadcasted_iota(jnp.int32, (SUB, CHUNK), 1)
        trow = lax.broadcasted_iota(jnp.int32, (SUB, HEAD), 0)

        def head(hd, carry):
            sl = pl.ds(pl.multiple_of(hd * HEAD, HEAD), HEAD)
            qh, kh, ch, ih = q_s[:, sl], k_s[:, sl], c_s[:, sl], i_ref[:, sl]
            st = st_ref[hd]
            sall_ref[hd] = st
            c_last = ch[CHUNK - 1:CHUNK, :]
            o = _dot_nt(qh * jnp.exp(ch), st)
            st_ref[hd] = st * jnp.exp(c_last) + _dot_tn(ih, kh * jnp.exp(c_last - ch))
            a_rows = [jnp.zeros((SUB, CHUNK), F32)]
            for tb in range(1, CHUNK // SUB):
                _, _, q_hat, _, k_hat = _offdiag_terms(qh, kh, ch, tb)
                a_rows.append(jnp.where(col64 < tb * SUB, _dot_nt(q_hat, k_hat), 0.0))
            o = o + _dot(jnp.concatenate(a_rows, axis=0), ih)
            o_rows = []
            for b in range(CHUNK // SUB):
                rows = slice(b * SUB, (b + 1) * SUB)
                qb, cb, kb, ib = qh[rows], ch[rows], kh[rows], ih[rows]
                ob = jnp.zeros((SUB, HEAD), F32)
                for s in range(SUB):
                    a = jnp.sum(qb * _diag_decay(cb, s, trow) * kb[s:s + 1, :], axis=1, keepdims=True)
                    ob = ob + a * ib[s:s + 1, :]
                o_rows.append(ob)
            o = o + jnp.concatenate(o_rows, axis=0)
            o_ref[:, sl] = o
            og = og_ref[:, sl]
            on = o * lax.rsqrt(_mean(o * o) + RMS_EPS)
            y_ref[:, sl] = (on * gn_ref[:, sl] * (og * jax.nn.sigmoid(og))).astype(BF16)
            return carry

        lax.fori_loop(0, HEADS, head, 0, unroll=HEAD_UNROLL)

    rows = per_step * CHUNK
    row = pl.BlockSpec((rows, D_MODEL), lambda n: (n, 0))
    whole = lambda s: pl.BlockSpec(s, lambda n: (0,) * len(s))
    res = pl.pallas_call(
        body,
        grid=(steps,),
        in_specs=[pl.BlockSpec((4, rows, D_MODEL), lambda n: (SLOT_Q // 4, n, 0)), whole(logits.shape), whole(g_norm.shape)]
        + (ex.in_specs if ex else []),
        out_specs=[row, row, pl.BlockSpec((per_step, HEADS, HEAD, HEAD), lambda n: (n, 0, 0, 0))]
        + (ex.out_specs if ex else []),
        out_shape=[jax.ShapeDtypeStruct((t, D_MODEL), BF16), jax.ShapeDtypeStruct((t, D_MODEL), F32),
                   jax.ShapeDtypeStruct((nc, HEADS, HEAD, HEAD), F32)] + (ex.out_shape if ex else []),
        scratch_shapes=[pltpu.VMEM((HEADS, HEAD, HEAD), F32)] + [pltpu.VMEM((per_step, CHUNK, D_MODEL), F32)] * 3
        + (ex.scratch if ex else []),
        compiler_params=_params(("arbitrary",)),
        name="hgrn_fwd",
    )(h, logits, g_norm, *(ex.arrays if ex else []))
    return res[0], res[1], res[2], res[3:]


def _hgrn_bwd(h, o_all, dy, states, dh, logits, g_norm, ex=None):
    t = h.shape[1]
    nc = t // CHUNK

    def body(*refs):
        ins, outs, scratch, xrefs = _split_refs(refs, 7, 3, 8, ex)
        hgrn_ref, o_ref, dy_ref, sall_ref, _, lg_ref, gn_ref = ins
        q_ref, f_ref, i_ref, og_ref = (hgrn_ref.at[s] for s in (SLOT_Q, SLOT_F, SLOT_I, SLOT_OG))
        dqfio_ref, dlg_ref, dgn_ref = outs
        dst_ref, dlb_ref, q_s, k_s, c_s, dq_s, dk_s, dc_s = scratch
        dq_ref, df_ref, di_ref, dog_ref = (dqfio_ref.at[s] for s in (SLOT_Q, SLOT_F, SLOT_I, SLOT_OG))
        n = pl.program_id(0)

        @pl.when(n == 0)
        def _():
            dst_ref[...] = jnp.zeros_like(dst_ref)
            dlb_ref[...] = jnp.zeros_like(dlb_ref)
            dgn_ref[...] = jnp.zeros_like(dgn_ref)
            if ex:
                ex.start(*xrefs)

        lb = _lower_bound(lg_ref[...])
        q_raw = q_ref[...]
        q, sig, f, k, c = _hgrn_gates(q_raw, f_ref[...], lb)
        q_s[...] = q
        k_s[...] = k
        c_s[...] = c
        col64 = lax.broadcasted_iota(jnp.int32, (SUB, CHUNK), 1)
        trow = lax.broadcasted_iota(jnp.int32, (SUB, HEAD), 0)
        row64 = lax.broadcasted_iota(jnp.int32, (CHUNK, HEAD), 0)

        def head(hd, carry):
            sl = pl.ds(pl.multiple_of(hd * HEAD, HEAD), HEAD)
            qh, kh, ch, ih = q_s[:, sl], k_s[:, sl], c_s[:, sl], i_ref[:, sl]
            st = sall_ref[hd]
            dst = dst_ref[hd]
            oh, dyh, og, gn = o_ref[:, sl], dy_ref[:, sl], og_ref[:, sl], gn_ref[:, sl]
            sg = jax.nn.sigmoid(og)
            sil = og * sg
            rms = lax.rsqrt(_mean(oh * oh) + RMS_EPS)
            on = oh * rms
            dog_ref[:, sl] = (dyh * on * gn * _silu_grad(og, sg)).astype(BF16)
            dgn_ref[:, sl] += _sum0(dyh * on * sil)
            don = dyh * gn * sil
            do = rms * (don - on * _mean(don * on))
            dob = do.astype(BF16)

            c_last = ch[CHUNK - 1:CHUNK, :]
            eq = jnp.exp(ch)
            q_til = qh * eq
            ekl = jnp.exp(c_last - ch)
            k_til = kh * ekl
            ecl = jnp.exp(c_last)
            dq_til = _dot(dob, st)
            dk_til = _dot(ih, dst)
            di = _dot_nt(k_til, dst)
            dc_last = _sum0(dk_til * k_til) + _sum0(dst * st) * ecl
            dst_ref[hd] = _dot_tn(dob, q_til) + dst * ecl
            dq = dq_til * eq
            dc = dq_til * q_til - dk_til * k_til
            dk = dk_til * ekl

            da_full = _dot_nt(dob, ih)
            a_rows = [jnp.zeros((SUB, CHUNK), F32)]
            dq_rows = [jnp.zeros((SUB, HEAD), F32)]
            dc_rows = [jnp.zeros((SUB, HEAD), F32)]
            for tb in range(1, CHUNK // SUB):
                rows, eqh, q_hat, ekh, k_hat = _offdiag_terms(qh, kh, ch, tb)
                keep = col64 < tb * SUB
                a_rows.append(jnp.where(keep, _dot_nt(q_hat, k_hat), 0.0))
                da = jnp.where(keep, da_full[rows], 0.0)
                dq_hat = _dot(da, k_hat)
                dk_hat = _dot_tn(da, q_hat)
                dq_rows.append(dq_hat * eqh)
                dc_rows.append(dq_hat * q_hat)
                dk = dk + dk_hat * ekh
                dc = dc - dk_hat * k_hat
            di = di + _dot_tn(jnp.concatenate(a_rows, axis=0), dob)

            dk_rows, di_rows = [], []
            for b in range(CHUNK // SUB):
                rows = slice(b * SUB, (b + 1) * SUB)
                qb, cb, kb, ib, dob_ = qh[rows], ch[rows], kh[rows], ih[rows], do[rows]
                dq_diag = jnp.zeros((SUB, HEAD), F32)
                dk_diag = jnp.zeros((SUB, HEAD), F32)
                di_diag = jnp.zeros((SUB, HEAD), F32)
                for s in range(SUB):
                    ks = kb[s:s + 1, :]
                    dec = _diag_decay(cb, s, trow)
                    a = jnp.sum(qb * dec * ks, axis=1, keepdims=True)
                    gk = jnp.sum(dob_ * ib[s:s + 1, :], axis=1, keepdims=True) * dec
                    dq_diag = dq_diag + gk * ks
                    dk_diag = dk_diag + jnp.where(trow == s, _sum0(gk * qb), 0.0)
                    di_diag = di_diag + jnp.where(trow == s, _sum0(a * dob_), 0.0)
                dq_rows[b] = dq_rows[b] + dq_diag
                dc_rows[b] = dc_rows[b] + qb * dq_diag - kb * dk_diag
                dk_rows.append(dk_diag)
                di_rows.append(di_diag)
            dq = dq + jnp.concatenate(dq_rows, axis=0)
            dk = dk + jnp.concatenate(dk_rows, axis=0)
            dc = dc + jnp.concatenate(dc_rows, axis=0) + jnp.where(row64 == CHUNK - 1, dc_last, 0.0)
            di_ref[:, sl] = (di + jnp.concatenate(di_rows, axis=0)).astype(BF16)
            dq_s[:, sl] = dq
            dk_s[:, sl] = dk
            dc_s[:, sl] = dc
            return carry

        lax.fori_loop(0, HEADS, head, 0, unroll=HEAD_UNROLL)

        row = lax.broadcasted_iota(jnp.int32, (CHUNK, CHUNK), 0)
        col = lax.broadcasted_iota(jnp.int32, (CHUNK, CHUNK), 1)
        dlf = _tri_matmul((row <= col).astype(BF16), dc_s[...])
        df = dlf / f - dk_s[...]
        dlb_ref[...] += _sum0(df * (1.0 - sig))
        df_ref[...] = (df * (1.0 - lb) * sig * (1.0 - sig)).astype(BF16)
        dq_ref[...] = (dq_s[...] * _silu_grad(q_raw, jax.nn.sigmoid(q_raw))).astype(BF16)

        @pl.when(n == nc - 1)
        def _():
            d0 = dlb_ref[...] * lb * (1.0 - lb)
            dlg_ref[0:1, :] = d0
            dlg_ref[1:2, :] = -d0
            if ex:
                ex.finish(*xrefs)

    rev = lambda n: nc - 1 - n
    slots = pl.BlockSpec((4, CHUNK, D_MODEL), lambda n: (SLOT_Q // 4, rev(n), 0))
    row = pl.BlockSpec((CHUNK, D_MODEL), lambda n: (rev(n), 0))
    whole = lambda s: pl.BlockSpec(s, lambda n: (0,) * len(s))
    vec = (1, D_MODEL)
    res = pl.pallas_call(
        body,
        grid=(nc,),
        in_specs=[slots, row, row, pl.BlockSpec((None, HEADS, HEAD, HEAD), lambda n: (rev(n), 0, 0, 0)),
                  pl.BlockSpec(memory_space=pl.ANY), whole(logits.shape), whole(vec)] + (ex.in_specs if ex else []),
        out_specs=[slots, whole((2, D_MODEL)), whole(vec)] + (ex.out_specs if ex else []),
        out_shape=[jax.ShapeDtypeStruct(dh.shape, BF16), jax.ShapeDtypeStruct((2, D_MODEL), F32),
                   jax.ShapeDtypeStruct(vec, F32)] + (ex.out_shape if ex else []),
        scratch_shapes=[pltpu.VMEM((HEADS, HEAD, HEAD), F32), pltpu.VMEM(vec, F32)]
        + [pltpu.VMEM((CHUNK, D_MODEL), F32)] * 6 + (ex.scratch if ex else []),
        input_output_aliases={4: 0},
        compiler_params=_params(("arbitrary",)),
        name="hgrn_bwd",
    )(h, o_all, dy, states, dh, logits, g_norm, *(ex.arrays if ex else []))
    return res[0], res[1], res[2], res[3:]


def _merge_fwd(i, n, ga, gb, za, zb):
    return jax.nn.sigmoid(ga) * za + jax.nn.sigmoid(gb) * zb


def _merge_bwd(i, n, ga, gb, za, zb, dm):
    sa, sb = jax.nn.sigmoid(ga), jax.nn.sigmoid(gb)
    dgates = jnp.stack([(dm * za * sa * (1.0 - sa)).astype(BF16), (dm * zb * sb * (1.0 - sb)).astype(BF16)])
    return dgates, dm * sa, dm * sb


def _ln1_fwd(i, n, x, r1, g, b):
    xhat, _ = _ln_stats(ALPHA * x + r1)
    x1 = xhat * g + b
    return x1, x1


def _ln1_bwd(i, n, x, r1, dx1, g):
    xhat, rstd = _ln_stats(ALPHA * x + r1)
    dz = _ln_bwd(dx1, xhat, rstd, g)
    return dz, dz, _sum0(dx1 * xhat), _sum0(dx1)


def _ln2_loss(i, n, x1, fo, pg, pp, tgt, g, b):
    sg = jax.nn.sigmoid(pg)
    xhat, rstd = _ln_stats(ALPHA * x1 + fo + sg * pp)
    diff = xhat * g + b - tgt
    loss = 0.5 * jnp.sum(_mean(diff * diff), axis=0, keepdims=True)
    dy = diff * (1.0 / D_MODEL)
    dz = _ln_bwd(dy, xhat, rstd, g)
    return (dz, dz, dz * pp * sg * (1.0 - sg), dz * sg,
            jnp.broadcast_to(loss, (8, LANE)), _sum0(dy * xhat), _sum0(dy))


def _shift_down(cur, halo, tile):
    row = lax.broadcasted_iota(jnp.int32, cur.shape, 0)
    m1 = jnp.where(row == 0, halo[7:8, :], pltpu.roll(cur, 1, 0))
    m2 = jnp.where(row == 0, halo[6:7, :], jnp.where(row == 1, halo[7:8, :], pltpu.roll(cur, 2, 0)))
    return m1, m2


def _shift_up(cur, halo, tile):
    row = lax.broadcasted_iota(jnp.int32, cur.shape, 0)
    p1 = jnp.where(row == tile - 1, halo[0:1, :], pltpu.roll(cur, tile - 1, 0))
    p2 = jnp.where(row == tile - 2, halo[0:1, :], jnp.where(row == tile - 1, halo[1:2, :], pltpu.roll(cur, tile - 2, 0)))
    return p1, p2


def _conv_pre(i, gate, halo, w, b, tile):
    halo = jnp.where(i == 0, 0.0, halo)
    m1, m2 = _shift_down(gate, halo, tile)
    return w[0:1, :] * m2 + w[1:2, :] * m1 + w[2:3, :] * gate + b, m1, m2


def _conv_fwd(tile, i, n, gate, halo, val, w, b):
    cg, _, _ = _conv_pre(i, gate, halo, w, b, tile)
    return _gelu(cg) * val


def _conv_bwd_a(tile, i, n, gate, halo, val, dhid, w, b):
    cg, m1, m2 = _conv_pre(i, gate, halo, w, b, tile)
    act, slope = _gelu_and_grad(cg)
    dcg = dhid * val * slope
    return dcg, dhid * act, _sum0(dcg * m2), _sum0(dcg * m1), _sum0(dcg * gate), _sum0(dcg)


def _conv_bwd_b(tile, i, n, dcg, halo, w):
    dcg = dcg.astype(F32)
    halo = jnp.where(i == n - 1, 0.0, halo.astype(F32))
    p1, p2 = _shift_up(dcg, halo, tile)
    return w[2:3, :] * dcg + w[1:2, :] * p1 + w[0:1, :] * p2


def _halo_spec(width, tile, t, nxt, rows=8):
    per = tile // rows
    last = t // rows - 1
    if nxt:
        return pl.BlockSpec((rows, width), lambda i: (jnp.minimum((i + 1) * per, last), 0))
    return pl.BlockSpec((rows, width), lambda i: (jnp.maximum(i * per - 1, 0), 0))


def _adamw(i, n, w, m, v, parts):
    g = parts[0].astype(F32)
    for j in range(1, parts.shape[0]):
        g = g + parts[j].astype(F32)
    m_new = ADAM_B1 * m + (1.0 - ADAM_B1) * g
    v_new = ADAM_B2 * v + (1.0 - ADAM_B2) * (g * g)
    m_hat = m_new / (1.0 - ADAM_B1 ** ADAM_STEP)
    v_hat = v_new / (1.0 - ADAM_B2 ** ADAM_STEP)
    delta = -ADAM_LR * (m_hat / (jnp.sqrt(v_hat) + ADAM_EPS) + ADAM_WD * w)
    return g, delta, m_new, v_new


def _adam_call(w, m, v, parts, name):
    r, c = w.shape
    tile = _pick(r, (256, 128)) if r > 256 else r
    spec = pl.BlockSpec((parts.shape[0], tile, c), lambda i: (0, i, 0))
    return _rowwise(_adamw, [w, m, v, (parts, spec)], [], [(c, F32)] * 4, [], tile=tile, name=name)


def _peer(k):
    x, y, c = lax.axis_index("x"), lax.axis_index("y"), lax.axis_index("c")
    px = x ^ ((k >> 2) & 1)
    py = y ^ ((k >> 1) & 1)
    pc = c ^ (k & 1)
    return (px, py, pc), 4 * px + 2 * py + pc


def _my_index():
    return 4 * lax.axis_index("x") + 2 * lax.axis_index("y") + lax.axis_index("c")


class _Exchange:
    KINDS = ("gather", "gather+relay", "scatter", "scatter+pairs")

    def __init__(self, entries):
        assert all(k in self.KINDS for _, k in entries), [k for _, k in entries]
        self.arrays = [a for a, _ in entries]
        self.scatter = [k.startswith("scatter") for _, k in entries]
        self.relayed = ["+relay" in k for _, k in entries]
        self.pairs = ["+pairs" in k for _, k in entries]
        self.n = len(entries)
        self.in_specs = [pl.BlockSpec(memory_space=pl.ANY)] * self.n
        self.out_specs = [pl.BlockSpec(memory_space=pl.ANY)] * self.n
        shapes = [tuple(a.shape[1:]) if sc else tuple(a.shape) for a, sc in zip(self.arrays, self.scatter)]
        counts = [N_CHIP if p else N_DEV for p in self.pairs]
        self.out_shape = [jax.ShapeDtypeStruct((n,) + s, a.dtype) for n, s, a in zip(counts, shapes, self.arrays)]
        per = N_DEV - 1
        self.scratch = [pltpu.SemaphoreType.DMA((self.n * per,)), pltpu.SemaphoreType.DMA((self.n * per,)),
                        pltpu.SemaphoreType.DMA((self.n,))]

    def _copies(self, srcs, outs, sems):
        send_sems, recv_sems, local_sems = sems
        x, y, c = lax.axis_index("x"), lax.axis_index("y"), lax.axis_index("c")
        me = _my_index()
        per = N_DEV - 1
        local, first, passed, relay_arrivals, arrivals = [], [], [], [], []
        for a in range(self.n):

            def copy(k, src, dst, dev, a=a):
                return pltpu.make_async_remote_copy(
                    src_ref=src, dst_ref=dst, send_sem=send_sems.at[a * per + k], recv_sem=recv_sems.at[a * per + k],
                    device_id=dev, device_id_type=pl.DeviceIdType.MESH)

            if self.pairs[a]:
                chip = 2 * x + y
                for k in range(N_CHIP):
                    to = chip ^ k
                    piece = srcs[a].at[_slot_of_group(2 * to + c) // 2]
                    if k == 0:
                        local.append(pltpu.make_async_copy(piece, outs[a].at[chip], local_sems.at[a]))
                    else:
                        dev = (to // 2, to % 2, c)
                        first.append(copy(k - 1, piece, outs[a].at[chip], dev))
                        arrivals.append(copy(k - 1, piece, outs[a].at[to], dev))
                continue
            mine = srcs[a].at[me] if self.scatter[a] else srcs[a]
            land = outs[a].at[me]
            local.append(pltpu.make_async_copy(mine, land, local_sems.at[a]))
            if self.relayed[a]:
                block = lambda px, py, pc, a=a: outs[a].at[4 * px + 2 * py + pc]
                chips = [(1 - x, y), (x, 1 - y), (1 - x, 1 - y)]
                first.append(copy(0, mine, land, (x, y, 1 - c)))
                arrivals.append(copy(0, mine, block(x, y, 1 - c), (x, y, 1 - c)))
                for j, (px, py) in enumerate(chips):
                    first.append(copy(1 + j, mine, land, (px, py, c)))
                    relay_arrivals.append(copy(1 + j, mine, block(px, py, c), (px, py, c)))
                    passed.append(copy(4 + j, block(px, py, c), block(px, py, c), (x, y, 1 - c)))
                    arrivals.append(copy(4 + j, mine, block(px, py, 1 - c), (x, y, 1 - c)))
                continue
            for k in range(1, N_DEV):
                dev, idx = _peer(k)
                if self.scatter[a]:
                    first.append(copy(k - 1, srcs[a].at[idx], land, dev))
                else:
                    first.append(copy(k - 1, mine, land, dev))
                arrivals.append(copy(k - 1, mine, outs[a].at[idx], dev))
        return local, first, passed, relay_arrivals, arrivals

    def start(self, srcs, outs, sems):
        local, first, _, _, _ = self._copies(srcs, outs, sems)
        for cp in local + first:
            cp.start()

    def relay(self, srcs, outs, sems):
        _, _, passed, relay_arrivals, _ = self._copies(srcs, outs, sems)
        for landed, onward in zip(relay_arrivals, passed):
            landed.wait_recv()
            onward.start()

    def finish(self, srcs, outs, sems):
        local, first, passed, _, arrivals = self._copies(srcs, outs, sems)
        for cp in arrivals:
            cp.wait_recv()
        for cp in first + passed:
            cp.wait_send()
        for cp in local:
            cp.wait()


def _gather_project(x_b, shard):
    t, d = x_b.shape
    tm = _pick(t, MM_TILES)
    nrow = t // tm
    per = N_DEV - 1

    def parties():
        x, y, c = lax.axis_index("x"), lax.axis_index("y"), lax.axis_index("c")
        chips = [(1 - x, y), (x, 1 - y), (1 - x, 1 - y)]
        return (x, y, c), (x, y, 1 - c), [(px, py, c) for px, py in chips], [(px, py, 1 - c) for px, py in chips]

    slot = lambda dev: _slot_of_group(4 * dev[0] + 2 * dev[1] + dev[2])
    ici_step, passed_step = (2, 3, 6), (4, 5, 7)
    me, sibling, over_ici, passed_on = parties()
    by_step = {0: me, 1: sibling, **dict(zip(ici_step, over_ici)), **dict(zip(passed_step, passed_on))}
    order = jnp.stack([slot(by_step[j]) for j in range(N_DEV)]).astype(jnp.int32)

    def body(order_ref, x_ref, shard_ref, h_ref, wall_ref, wbuf, fetch_sem, send_sems, recv_sems, local_sem):
        del order_ref
        me, sibling, over_ici, passed_on = parties()
        j, i = pl.program_id(0), pl.program_id(1)
        land = lambda dev: wall_ref.at[slot(dev)]

        def copy(k, src, block, to):
            return pltpu.make_async_remote_copy(src_ref=src, dst_ref=land(block), send_sem=send_sems.at[k],
                                                recv_sem=recv_sems.at[k], device_id=to, device_id_type=pl.DeviceIdType.MESH)

        def fetch(src):
            cp = pltpu.make_async_copy(src, wbuf, fetch_sem)
            cp.start()
            cp.wait()

        keep = pltpu.make_async_copy(shard_ref, land(me), local_sem)
        first = [copy(0, shard_ref, me, sibling)] + [copy(1 + n, shard_ref, me, dev) for n, dev in enumerate(over_ici)]
        onward = [copy(4 + n, land(dev), dev, sibling) for n, dev in enumerate(over_ici)]

        @pl.when(jnp.logical_and(i == 0, j == 0))
        def _():
            keep.start()
            for cp in first:
                cp.start()
            fetch(shard_ref)

        @pl.when(jnp.logical_and(i == 0, j == 1))
        def _():
            copy(0, shard_ref, sibling, me).wait_recv()
            fetch(land(sibling))

        for n, dev in enumerate(over_ici):
            @pl.when(jnp.logical_and(i == 0, j == ici_step[n]))
            def _(n=n, dev=dev):
                copy(1 + n, shard_ref, dev, me).wait_recv()
                onward[n].start()
                fetch(land(dev))

        for n, dev in enumerate(passed_on):
            @pl.when(jnp.logical_and(i == 0, j == passed_step[n]))
            def _(n=n, dev=dev):
                copy(4 + n, shard_ref, dev, me).wait_recv()
                fetch(land(dev))

        h_ref[...] = _dot(x_ref[...], wbuf[...])

        @pl.when(jnp.logical_and(i == nrow - 1, j == N_DEV - 1))
        def _():
            for cp in first + onward:
                cp.wait_send()
            keep.wait()

    need = 2 * (tm * d * 2 + tm * d * 4) + d * d * 2 + tm * d * 4
    h, w_all = pl.pallas_call(
        body,
        grid_spec=pltpu.PrefetchScalarGridSpec(
            num_scalar_prefetch=1,
            grid=(N_DEV, nrow),
            in_specs=[pl.BlockSpec((tm, d), lambda j, i, order: (i, 0)), pl.BlockSpec(memory_space=pl.ANY)],
            out_specs=[pl.BlockSpec((None, tm, d), lambda j, i, order: (order[j], i, 0)), pl.BlockSpec(memory_space=pl.ANY)],
            scratch_shapes=[pltpu.VMEM((d, d), BF16), pltpu.SemaphoreType.DMA, pltpu.SemaphoreType.DMA((per,)),
                            pltpu.SemaphoreType.DMA((per,)), pltpu.SemaphoreType.DMA],
        ),
        out_shape=[jax.ShapeDtypeStruct((N_DEV, t, d), F32), jax.ShapeDtypeStruct((N_DEV, d, d), BF16)],
        compiler_params=_params(("arbitrary", "arbitrary"), need),
        name="gather_project",
    )(order, x_b, shard)
    return h, w_all


def _pair_sums(g):
    n, r, cols = g.shape
    half = n // 2

    def swap(g_ref, got_ref, send_sems, recv_sems):
        x, y, c = lax.axis_index("x"), lax.axis_index("y"), lax.axis_index("c")
        copies = [pltpu.make_async_remote_copy(
            src_ref=g_ref.at[2 * j + 1 - c], dst_ref=got_ref.at[j], send_sem=send_sems.at[j], recv_sem=recv_sems.at[j],
            device_id=(x, y, 1 - c), device_id_type=pl.DeviceIdType.MESH) for j in range(half)]
        for cp in copies:
            cp.start()
        for cp in copies:
            cp.wait()

    got = pl.pallas_call(
        swap,
        in_specs=[pl.BlockSpec(memory_space=pl.ANY)],
        out_specs=pl.BlockSpec(memory_space=pl.ANY),
        out_shape=jax.ShapeDtypeStruct((half, r, cols), g.dtype),
        scratch_shapes=[pltpu.SemaphoreType.DMA((half,))] * 2,
        name="pair_swap",
    )(g)

    def add(mine_ref, got_ref, out_ref):
        mine = jnp.where(lax.axis_index("c") == 0, mine_ref[0].astype(F32), mine_ref[1].astype(F32))
        out_ref[...] = (mine + got_ref[...].astype(F32)).astype(out_ref.dtype)

    tile = _pick(r, (512, 256, 128))
    return pl.pallas_call(
        add,
        grid=(half, r // tile),
        in_specs=[pl.BlockSpec((None, 2, tile, cols), lambda j, i: (j, 0, i, 0)),
                  pl.BlockSpec((None, tile, cols), lambda j, i: (j, i, 0))],
        out_specs=pl.BlockSpec((None, tile, cols), lambda j, i: (j, i, 0)),
        out_shape=jax.ShapeDtypeStruct((half, r, cols), g.dtype),
        compiler_params=_params(("parallel", "parallel")),
        name="pair_add",
    )(g.reshape(half, 2, r, cols), got)


def _local_step(x, p, tgt, small, comm):
    t = x.shape[0]
    tile = _pick(t, (256, 128))
    tall = _pick(t, (512, 256, 128))
    d = D_MODEL
    act_b, act_f = (d, BF16), (d, F32)
    x_b, p_b = x.astype(BF16), p.astype(BF16)

    chunk_id = jnp.arange(SGU_BLOCK) // CHUNK
    mask = chunk_id[:, None] >= chunk_id[None, :]
    wm = jnp.where(mask[None], small["sgu_w_s"], 0.0)
    wm_b = wm.astype(BF16)
    wm_t = jnp.swapaxes(wm, 1, 2).astype(BF16)
    bs_t = small["sgu_b_s"].T

    h, w_in = comm.project_in(x_b)
    y_a, got_a = _sgu_fwd(h, wm_b, bs_t, small["sgu_norm_g"], small["sgu_norm_b"], ex=comm.weights_exchange(0))
    y_b, o_all, states, got_b = _hgrn_fwd(h, small["lb_logits"], small["hgrn_norm_g"], ex=comm.weights_exchange(1))
    wts, conv_w = comm.weights(list(got_a) + list(got_b))
    z_a = _mm(y_a, wts["w_a"], out_dtype=F32, name="mm_za")
    z_b = _mm(y_b, wts["w_b"], out_dtype=F32, name="mm_zb")
    gates = [(h, SLOT_GA), (h, SLOT_GB)]
    merged, = _rowwise(_merge_fwd, gates + [z_a, z_b], [], [act_b], [], tile=tall, name="merge_fwd")
    r1 = _mm(merged, wts["w_o"], out_dtype=F32, name="mm_r1")
    x1, x1_b = _rowwise(_ln1_fwd, [x, r1], [small["ln1_g"], small["ln1_b"]], [act_f, act_b], [], tile=tall, name="ln1_fwd")
    gate = _mm(x1_b, wts["w_g"], out_dtype=F32, name="mm_gate")
    val = _mm(x1_b, wts["w_v"], out_dtype=F32, name="mm_val")
    pg = _mm(x1_b, wts["w_pg"], out_dtype=F32, name="mm_pg")
    pp = _mm(p_b, wts["w_pp"], out_dtype=F32, name="mm_pp")
    hid, = _rowwise(functools.partial(_conv_fwd, tile), [gate, (gate, _halo_spec(D_FF, tile, t, False)), val],
                    [conv_w, small["conv_b"]], [(D_FF, BF16)], [], tile=tile, name="conv_fwd")
    fo = _mm(hid, wts["w_down"], out_dtype=F32, name="mm_down")
    dz2, dz2_b, dpg, dpp, loss, dg2, db2 = _rowwise(
        _ln2_loss, [x1, fo, pg, pp, tgt], [small["ln2_g"], small["ln2_b"]],
        [act_f, act_b, act_b, act_b], [(8, LANE), (1, d), (1, d)], tile=tall, name="ln2_loss")

    dhid = _mm(dz2_b, wts["w_down"], out_dtype=BF16, name="mm_dhid", trans_b=True)
    g_down = _mm_tn(hid, dz2_b, out_dtype=BF16, name="mm_g_down")
    dcg, dval, dcw0, dcw1, dcw2, dcb = _rowwise(
        functools.partial(_conv_bwd_a, tile), [gate, (gate, _halo_spec(D_FF, tile, t, False)), val, dhid],
        [conv_w, small["conv_b"]], [(D_FF, BF16), (D_FF, BF16)], [(1, D_FF)] * 4, tile=tile, name="conv_bwd_a")
    dgate, = _rowwise(functools.partial(_conv_bwd_b, tile), [dcg, (dcg, _halo_spec(D_FF, tile, t, True, rows=16))],
                      [conv_w], [(D_FF, BF16)], [], tile=tile, name="conv_bwd_b")
    g_g = _mm_tn(x1_b, dgate, out_dtype=BF16, name="mm_g_gate")
    g_v = _mm_tn(x1_b, dval, out_dtype=BF16, name="mm_g_val")
    g_pg = _mm_tn(x1_b, dpg, out_dtype=BF16, name="mm_g_pg")
    g_pp = _mm_tn(p_b, dpp, out_dtype=BF16, name="mm_g_pp")
    dx1 = _mm(dgate, wts["w_g"], out_dtype=F32, name="mm_dx1_gate", trans_b=True, adds=[(dz2, ALPHA)])
    dx1 = _mm(dval, wts["w_v"], out_dtype=F32, name="mm_dx1_val", trans_b=True, adds=[(dx1, 1.0)])
    dx1 = _mm(dpg, wts["w_pg"], out_dtype=F32, name="mm_dx1_pg", trans_b=True, adds=[(dx1, 1.0)])
    dz1, dz1_b, dg1, db1 = _rowwise(_ln1_bwd, [x, r1, dx1], [small["ln1_g"]], [act_f, act_b], [(1, d), (1, d)],
                                    tile=tall, name="ln1_bwd")
    g_o = _mm_tn(merged, dz1_b, out_dtype=BF16, name="mm_g_o")
    dm = _mm(dz1_b, wts["w_o"], out_dtype=BF16, name="mm_dm", trans_b=True)
    dh, dza, dzb = _rowwise(_merge_bwd, gates + [z_a, z_b, dm], [],
                            [("stack", 2, SLOT_GA // 2, 8, d, BF16), act_b, act_b], [], tile=tall, name="merge_bwd")
    g_a = _mm_tn(y_a, dza, out_dtype=BF16, name="mm_g_a")
    g_b = _mm_tn(y_b, dzb, out_dtype=BF16, name="mm_g_b")
    dy_a = _mm(dza, wts["w_a"], out_dtype=BF16, name="mm_dya", trans_b=True)
    dy_b = _mm(dzb, wts["w_b"], out_dtype=F32, name="mm_dyb", trans_b=True)
    dh, dws, dbs, dgv_n, dbv_n = _sgu_bwd(h, dy_a, dh, wm_b, wm_t, bs_t, small["sgu_norm_g"], small["sgu_norm_b"])
    big = dict(w_a=g_a, w_b=g_b, w_o=g_o, w_g=g_g, w_v=g_v, w_down=g_down, w_pp=g_pp, w_pg=g_pg)
    sm = dict(sgu_w_s=jnp.where(mask[None], dws, 0.0), sgu_b_s=dbs[:, :GROUPS].T, sgu_norm_g=dgv_n, sgu_norm_b=dbv_n,
              ln1_g=dg1, ln1_b=db1, conv_w=jnp.concatenate([dcw0, dcw1, dcw2], axis=0), conv_b=dcb, ln2_g=dg2, ln2_b=db2,
              loss=loss)
    dh, dlogits, dgn, got = _hgrn_bwd(h, o_all, dy_b, states, dh, small["lb_logits"], small["hgrn_norm_g"],
                                      ex=comm.grads_exchange(big, sm))
    comm.grads_done(got)
    g_in = _mm_tn(x_b, dh, out_dtype=BF16, name="mm_g_in")
    ex = comm.last_exchange(g_in, dict(lb_logits=dlogits, hgrn_norm_g=dgn))
    res = _mm(dh, w_in, out_dtype=F32, name="mm_dx", trans_b=True, reduce_b=True, adds=[(dz1, ALPHA)], ex=ex)
    grad_x, got = res if ex else (res, ())
    comm.last_done(got)
    return grad_x


_SMALL_EARLY = ["sgu_w_s", "sgu_b_s", "sgu_norm_g", "sgu_norm_b", "ln1_g", "ln1_b", "ffn_conv_b", "ln2_g", "ln2_b"]
_SMALL_LATE = ["hgrn_lb_logits", "hgrn_norm_g"]
N_TAPS = D_FF // N_DEV
UP_COLS = 2 * D_FF // N_DEV


class _StepExchanges:
    def __init__(self, w_in_shard, shards):
        self.w_in_shard = w_in_shard
        self.shards = shards

    def project_in(self, x_b):
        return _gather_project(x_b, self.w_in_shard)

    def weights_exchange(self, part):
        return _Exchange([(s, "gather+relay") for s in (self.shards[:2] if part == 0 else self.shards[2:])])

    def weights(self, got):
        d, f = D_MODEL, D_FF
        w_br_g, w_o_g, w_up_g, w_down_g, w_pp_g, w_pg_g, conv_g = got
        w_br = w_br_g.transpose(1, 0, 2, 3).reshape(2, d, d)
        w_up = w_up_g.transpose(1, 0, 2).reshape(d, 2, f).transpose(1, 0, 2)
        wts = dict(w_a=w_br[0], w_b=w_br[1], w_o=w_o_g.reshape(d, d), w_g=w_up[0], w_v=w_up[1],
                   w_down=w_down_g.reshape(f, d), w_pp=w_pp_g.transpose(1, 0, 2).reshape(256, d), w_pg=w_pg_g.reshape(d, d))
        return wts, conv_g.transpose(1, 0, 2).reshape(3, f)

    def grads_exchange(self, big, sm):
        d = D_MODEL
        parts = [jnp.stack([big["w_a"], big["w_b"]]).reshape(2, N_DEV, 128, d).transpose(1, 0, 2, 3),
                 big["w_o"].reshape(N_DEV, 128, d),
                 jnp.concatenate([big["w_g"], big["w_v"]], axis=1).reshape(d, N_DEV, UP_COLS).transpose(1, 0, 2),
                 big["w_down"].reshape(N_DEV, N_TAPS, d),
                 big["w_pp"].reshape(256, N_DEV, 128).transpose(1, 0, 2),
                 big["w_pg"].reshape(N_DEV, 128, d)]
        packed, self.rows_early = _pack([sm[k] for k in ("sgu_w_s", "sgu_b_s", "sgu_norm_g", "sgu_norm_b", "ln1_g", "ln1_b",
                                                         "conv_b", "ln2_g", "ln2_b", "conv_w", "loss")])
        return _Exchange([(a, "scatter") for a in parts] + [(packed, "gather")])

    def grads_done(self, got):
        self.recv, self.small_early = got[:6], got[6]

    def last_exchange(self, g_in, sm):
        packed, self.rows_late = _pack([sm["lb_logits"], sm["hgrn_norm_g"]])
        return _Exchange([(_pair_sums(g_in), "scatter+pairs"), (packed, "gather")])

    def last_done(self, got):
        self.recv_in, self.small_late = got


def _rows128(a):
    flat = a.reshape(-1)
    rows = -(-flat.shape[0] // (8 * LANE)) * 8
    return jnp.pad(flat, (0, rows * LANE - flat.shape[0])).reshape(rows, LANE)


def _pack(parts):
    blocks = [_rows128(a) for a in parts]
    return jnp.concatenate(blocks, axis=0), [b.shape[0] for b in blocks]


def _unpack(packed, shapes, rows):
    out, r0 = [], 0
    for shp, r in zip(shapes, rows):
        n = math.prod(shp)
        out.append(packed[r0:r0 + r].reshape(-1)[:n].reshape(shp))
        r0 += r
    return out


def kernel(x, p, w_in, sgu_w_s, sgu_b_s, sgu_norm_g, sgu_norm_b, hgrn_lb_logits, hgrn_norm_g, w_branch, w_out, ln1_g, ln1_b, ffn_w_up, ffn_conv_w, ffn_conv_b, ffn_w_down, ln2_g, ln2_b, ple_w_proj, ple_w_gate, loss_target, m_w_in, m_sgu_w_s, m_sgu_b_s, m_sgu_norm_g, m_sgu_norm_b, m_hgrn_lb_logits, m_hgrn_norm_g, m_w_branch, m_w_out, m_ln1_g, m_ln1_b, m_ffn_w_up, m_ffn_conv_w, m_ffn_conv_b, m_ffn_w_down, m_ln2_g, m_ln2_b, m_ple_w_proj, m_ple_w_gate, v_w_in, v_sgu_w_s, v_sgu_b_s, v_sgu_norm_g, v_sgu_norm_b, v_hgrn_lb_logits, v_hgrn_norm_g, v_w_branch, v_w_out, v_ln1_g, v_ln1_b, v_ffn_w_up, v_ffn_conv_w, v_ffn_conv_b, v_ffn_w_down, v_ln2_g, v_ln2_b, v_ple_w_proj, v_ple_w_gate):
    weights = dict(w_in=w_in, sgu_w_s=sgu_w_s, sgu_b_s=sgu_b_s, sgu_norm_g=sgu_norm_g, sgu_norm_b=sgu_norm_b,
                   hgrn_lb_logits=hgrn_lb_logits, hgrn_norm_g=hgrn_norm_g, w_branch=w_branch, w_out=w_out,
                   ln1_g=ln1_g, ln1_b=ln1_b, ffn_w_up=ffn_w_up, ffn_conv_w=ffn_conv_w, ffn_conv_b=ffn_conv_b,
                   ffn_w_down=ffn_w_down, ln2_g=ln2_g, ln2_b=ln2_b, ple_w_proj=ple_w_proj, ple_w_gate=ple_w_gate)
    mom_m = dict(w_in=m_w_in, sgu_w_s=m_sgu_w_s, sgu_b_s=m_sgu_b_s, sgu_norm_g=m_sgu_norm_g, sgu_norm_b=m_sgu_norm_b,
                 hgrn_lb_logits=m_hgrn_lb_logits, hgrn_norm_g=m_hgrn_norm_g, w_branch=m_w_branch, w_out=m_w_out,
                 ln1_g=m_ln1_g, ln1_b=m_ln1_b, ffn_w_up=m_ffn_w_up, ffn_conv_w=m_ffn_conv_w, ffn_conv_b=m_ffn_conv_b,
                 ffn_w_down=m_ffn_w_down, ln2_g=m_ln2_g, ln2_b=m_ln2_b, ple_w_proj=m_ple_w_proj, ple_w_gate=m_ple_w_gate)
    mom_v = dict(w_in=v_w_in, sgu_w_s=v_sgu_w_s, sgu_b_s=v_sgu_b_s, sgu_norm_g=v_sgu_norm_g, sgu_norm_b=v_sgu_norm_b,
                 hgrn_lb_logits=v_hgrn_lb_logits, hgrn_norm_g=v_hgrn_norm_g, w_branch=v_w_branch, w_out=v_w_out,
                 ln1_g=v_ln1_g, ln1_b=v_ln1_b, ffn_w_up=v_ffn_w_up, ffn_conv_w=v_ffn_conv_w, ffn_conv_b=v_ffn_conv_b,
                 ffn_w_down=v_ffn_w_down, ln2_g=v_ln2_g, ln2_b=v_ln2_b, ple_w_proj=v_ple_w_proj, ple_w_gate=v_ple_w_gate)
    d, f = D_MODEL, D_FF
    me = _my_index()

    comm = _StepExchanges(w_in[0].astype(BF16),
                          [w_branch[0].astype(BF16), w_out[0].astype(BF16), ffn_w_up[0].astype(BF16),
                           ffn_w_down[0].astype(BF16), ple_w_proj[0].astype(BF16), ple_w_gate[0].astype(BF16), ffn_conv_w[0]])
    small = dict(sgu_w_s=sgu_w_s[0], sgu_b_s=sgu_b_s[0], sgu_norm_g=sgu_norm_g, sgu_norm_b=sgu_norm_b,
                 lb_logits=hgrn_lb_logits, hgrn_norm_g=hgrn_norm_g, ln1_g=ln1_g, ln1_b=ln1_b, ln2_g=ln2_g, ln2_b=ln2_b,
                 conv_b=ffn_conv_b)
    grad_x = _local_step(x[0], p[0, 0], loss_target[0], small, comm)

    out = {}

    def adam(name, parts8, shape2d):
        w2, m2, v2 = (a.reshape(shape2d) for a in (weights[name], mom_m[name], mom_v[name]))
        res = _adam_call(w2, m2, v2, parts8.reshape(parts8.shape[:1] + shape2d), "adam_" + name)
        out[name] = tuple(r.reshape(weights[name].shape) for r in res)

    adam("w_in", comm.recv_in, (d, d))
    adam("w_branch", comm.recv[0], (256, d))
    adam("w_out", comm.recv[1], (128, d))
    adam("ffn_w_up", comm.recv[2], (d, UP_COLS))
    adam("ffn_w_down", comm.recv[3], (N_TAPS, d))
    adam("ple_w_proj", comm.recv[4], (256, 128))
    adam("ple_w_gate", comm.recv[5], (128, d))

    def adam_small(names, extra_w, extra_m, extra_v, extra_shapes, parts8, rows, label):
        pk = lambda src, extra: _pack([src[n] for n in names] + extra)[0]
        res = _adam_call(pk(weights, extra_w), pk(mom_m, extra_m), pk(mom_v, extra_v), parts8, label)
        shapes = [weights[n].shape for n in names] + extra_shapes
        unpacked = [_unpack(r, shapes, rows) for r in res]
        for j, n in enumerate(names):
            out[n] = tuple(u[j] for u in unpacked)
        return [[u[len(names) + j] for u in unpacked] for j in range(len(extra_shapes))]

    blank = jnp.zeros((8, LANE), F32)
    taps, loss_rows = adam_small(
        _SMALL_EARLY, [_place_taps(ffn_conv_w[0], me, f), blank], [_place_taps(m_ffn_conv_w[0], me, f), blank],
        [_place_taps(v_ffn_conv_w[0], me, f), blank + 1.0], [(3, f), (8, LANE)], comm.small_early, comm.rows_early,
        "adam_small_early")
    adam_small(_SMALL_LATE, [], [], [], [], comm.small_late, comm.rows_late, "adam_small_late")
    out["ffn_conv_w"] = tuple(lax.dynamic_slice_in_dim(u, me * N_TAPS, N_TAPS, axis=1)[None] for u in taps)
    loss = loss_rows[0][0, 0]

    order = ["w_in", "sgu_w_s", "sgu_b_s", "sgu_norm_g", "sgu_norm_b", "hgrn_lb_logits", "hgrn_norm_g", "w_branch", "w_out",
             "ln1_g", "ln1_b", "ffn_w_up", "ffn_conv_w", "ffn_conv_b", "ffn_w_down", "ln2_g", "ln2_b", "ple_w_proj", "ple_w_gate"]
    return (loss, grad_x[None], *[out[n][0] for n in order], *[out[n][1] for n in order],
            *[out[n][2] for n in order], *[out[n][3] for n in order])


def _place_taps(shard, me, f):
    return lax.dynamic_update_slice_in_dim(jnp.zeros((3, f), F32), shard, me * N_TAPS, axis=1)
```

```python
import functools
import math

import jax
import jax.numpy as jnp
from jax import lax
from jax.experimental import pallas as pl
from jax.experimental.pallas import tpu as pltpu

F32 = jnp.float32
BF16 = jnp.bfloat16

N_DEV = 8
N_CHIP = 4
D_MODEL = 1024
CHUNK = 64
SUB = 16
SGU_BLOCK = 128
GROUPS = 8
HEAD = 128
HEADS = 8
CHUNKS_PER_STEP = 2
HEAD_UNROLL = 8
D_FF = 2816
LN_EPS = 1e-5
RMS_EPS = 1e-6
ALPHA = 2.0 ** 0.25
GELU_K = math.sqrt(2.0 / math.pi)
GELU_C = 0.044715
NEG = -1e30
ADAM_LR, ADAM_B1, ADAM_B2, ADAM_EPS, ADAM_WD, ADAM_STEP = 0.001, 0.9, 0.999, 1e-08, 0.01, 10
LANE = 128
SLOT_Q, SLOT_F, SLOT_I, SLOT_OG, SLOT_U, SLOT_V, SLOT_GA, SLOT_GB = range(8)


def _slot_of_group(k):
    return jnp.where(k < 2, k + 4, jnp.where(k < 6, k - 2, k))


MIB = 1024 * 1024
VMEM_V7X = 64 * MIB
VMEM_FLOOR = 32 * MIB
MM_TILES = (1024, 1408, 512, 256, 128)


def _params(sem, need=0):
    limit = min(max(need + need // 4, VMEM_FLOOR), VMEM_V7X - 4 * MIB)
    return pltpu.CompilerParams(dimension_semantics=sem, vmem_limit_bytes=limit)


def _pick(n, prefs):
    for t in prefs:
        if n % t == 0:
            return t
    return n


def _gelu(x):
    return 0.5 * x * (1.0 + jnp.tanh(GELU_K * (x + GELU_C * x * x * x)))


def _gelu_and_grad(x):
    x2 = x * x
    t = jnp.tanh(GELU_K * x * (1.0 + GELU_C * x2))
    half = 0.5 * (1.0 + t)
    return x * half, half + 0.5 * x * (1.0 - t * t) * GELU_K * (1.0 + 3.0 * GELU_C * x2)


def _silu_grad(x, s):
    return s * (1.0 + x * (1.0 - s))


def _dot(a, b):
    return jnp.dot(a.astype(BF16), b.astype(BF16), preferred_element_type=F32)


def _dot_nt(a, b):
    return lax.dot_general(a.astype(BF16), b.astype(BF16), (((1,), (1,)), ((), ())), preferred_element_type=F32)


def _dot_tn(a, b):
    return lax.dot_general(a.astype(BF16), b.astype(BF16), (((0,), (0,)), ((), ())), preferred_element_type=F32)


def _mean(x):
    return jnp.mean(x, axis=-1, keepdims=True)


def _sum0(x):
    return jnp.sum(x, axis=0, keepdims=True)


def _mm(a, b, *, out_dtype, name, trans_b=False, reduce_b=False, adds=(), ex=None):
    squeeze = b.ndim == 2
    a3 = a if a.ndim == 3 else a[None]
    b3 = b if b.ndim == 3 else b[None]
    ba, m, k = a3.shape
    bb = b3.shape[0]
    n = b3.shape[1] if trans_b else b3.shape[2]
    tm = _pick(m, MM_TILES)
    tn = _pick(n, MM_TILES)
    if reduce_b:
        bo, steps = 1, bb
        a_map = lambda o, i, j, r: (r if ba > 1 else 0, i, 0)
        b_map = (lambda o, i, j, r: (r, j, 0)) if trans_b else (lambda o, i, j, r: (r, 0, j))
    else:
        bo, steps = bb, 1
        a_map = lambda o, i, j, r: (o if ba > 1 else 0, i, 0)
        b_map = (lambda o, i, j, r: (o, j, 0)) if trans_b else (lambda o, i, j, r: (o, 0, j))
    o_map = lambda o, i, j, r: (o, i, j)
    add_arrays = [x if x.ndim == 3 else x[None] for x, _ in adds]
    add_scales = [s for _, s in adds]
    n_add = len(adds)
    dot = _dot_nt if trans_b else _dot

    def finish(acc, add_refs, o_ref):
        for ref, s in zip(add_refs, add_scales):
            acc = acc + s * ref[...].astype(F32)
        o_ref[...] = acc.astype(o_ref.dtype)

    grid = (bo, m // tm, n // tn, steps)

    def body(*refs):
        ins, (o_ref,), scratch, xrefs = _split_refs(refs, 2 + n_add, 1, 1 if reduce_b else 0, ex)
        a_ref, b_ref, add_refs = ins[0], ins[1], ins[2:]
        step = ((pl.program_id(0) * grid[1] + pl.program_id(1)) * grid[2] + pl.program_id(2)) * grid[3] + pl.program_id(3)
        if ex:
            @pl.when(step == 0)
            def _():
                ex.start(*xrefs)

        if reduce_b:
            acc, = scratch
            r = pl.program_id(3)

            @pl.when(r == 0)
            def _():
                acc[...] = jnp.zeros_like(acc)

            acc[...] += dot(a_ref[...], b_ref[...])

            @pl.when(r == steps - 1)
            def _():
                finish(acc[...], add_refs, o_ref)
        else:
            finish(dot(a_ref[...], b_ref[...]), add_refs, o_ref)

        if ex:
            @pl.when(step == math.prod(grid) - 1)
            def _():
                ex.finish(*xrefs)

    b_block = (None, tn, k) if trans_b else (None, k, tn)
    out_bytes = tm * tn * jnp.dtype(out_dtype).itemsize
    need = 2 * (tm * k * a3.dtype.itemsize + k * tn * b3.dtype.itemsize + out_bytes + n_add * tm * tn * 4)
    need += 2 * tm * tn * 4
    sem = ("arbitrary",) * 4 if ex else ("parallel", "parallel", "parallel", "arbitrary")
    res = pl.pallas_call(
        body,
        grid=grid,
        in_specs=[pl.BlockSpec((None, tm, k), a_map), pl.BlockSpec(b_block, b_map)]
        + [pl.BlockSpec((None, tm, tn), o_map) for _ in adds] + (ex.in_specs if ex else []),
        out_specs=[pl.BlockSpec((None, tm, tn), o_map)] + (ex.out_specs if ex else []),
        out_shape=[jax.ShapeDtypeStruct((bo, m, n), out_dtype)] + (ex.out_shape if ex else []),
        scratch_shapes=([pltpu.VMEM((tm, tn), F32)] if reduce_b else []) + (ex.scratch if ex else []),
        compiler_params=_params(sem, need),
        name=name,
    )(a3, b3, *add_arrays, *(ex.arrays if ex else []))
    out = res[0][0] if (reduce_b or squeeze) else res[0]
    return (out, res[1:]) if ex else out


def _mm_tn(a, b, *, out_dtype, name):
    squeeze = b.ndim == 2
    b3 = b if b.ndim == 3 else b[None]
    t, m = a.shape
    bb, _, n = b3.shape
    tm = _pick(m, MM_TILES)
    tn = _pick(n, MM_TILES)
    tt = _pick(t, (1024, 512, 256, 128))
    steps = t // tt
    need = 2 * (tt * tm * a.dtype.itemsize + tt * tn * b3.dtype.itemsize + tm * tn * jnp.dtype(out_dtype).itemsize)
    need += 2 * tm * tn * 4

    def body(a_ref, b_ref, o_ref, acc):
        r = pl.program_id(3)

        @pl.when(r == 0)
        def _():
            acc[...] = jnp.zeros_like(acc)

        acc[...] += _dot_tn(a_ref[...], b_ref[...])

        @pl.when(r == steps - 1)
        def _():
            o_ref[...] = acc[...].astype(o_ref.dtype)

    out = pl.pallas_call(
        body,
        grid=(bb, m // tm, n // tn, steps),
        in_specs=[pl.BlockSpec((tt, tm), lambda o, i, j, r: (r, i)),
                  pl.BlockSpec((None, tt, tn), lambda o, i, j, r: (o, r, j))],
        out_specs=pl.BlockSpec((None, tm, tn), lambda o, i, j, r: (o, i, j)),
        out_shape=jax.ShapeDtypeStruct((bb, m, n), out_dtype),
        scratch_shapes=[pltpu.VMEM((tm, tn), F32)],
        compiler_params=_params(("parallel", "parallel", "parallel", "arbitrary"), need),
        name=name,
    )(a, b3)
    return out[0] if squeeze else out


def _rowwise(fn, rows, consts, row_outs, acc_outs, *, tile, name, ex=None):
    first = rows[0][0] if isinstance(rows[0], tuple) else rows[0]
    t = first.shape[-2]
    steps = t // tile
    arrays, in_specs = [], []
    for r in rows:
        if isinstance(r, tuple) and isinstance(r[1], pl.BlockSpec):
            arrays.append(r[0])
            in_specs.append(r[1])
        elif isinstance(r, tuple):
            arr, bidx = r
            arrays.append(arr)
            in_specs.append(pl.BlockSpec((None, tile, arr.shape[-1]), functools.partial(lambda i, b: (b, i, 0), b=bidx)))
        else:
            arrays.append(r)
            in_specs.append(pl.BlockSpec((tile, r.shape[-1]), lambda i: (i, 0)))
    for c in consts:
        arrays.append(c)
        in_specs.append(pl.BlockSpec(c.shape, lambda i: (0, 0)))
    n_in, n_row = len(arrays), len(row_outs)
    out_shape, out_specs = [], []
    for ro in row_outs:
        if ro[0] == "stack":
            _, cnt, blk, total, w, dt = ro
            out_shape.append(jax.ShapeDtypeStruct((total, t, w), dt))
            out_specs.append(pl.BlockSpec((cnt, tile, w), functools.partial(lambda i, b: (b, i, 0), b=blk)))
        else:
            w, dt = ro
            out_shape.append(jax.ShapeDtypeStruct((t, w), dt))
            out_specs.append(pl.BlockSpec((tile, w), lambda i: (i, 0)))
    out_shape += [jax.ShapeDtypeStruct(s, F32) for s in acc_outs]
    out_specs += [pl.BlockSpec(s, lambda i: (0, 0)) for s in acc_outs]
    blocks = [math.prod(d for d in sp.block_shape if d) * arr.dtype.itemsize for sp, arr in zip(in_specs, arrays)]
    blocks += [math.prod(d for d in sp.block_shape if d) * jnp.dtype(sh.dtype).itemsize
               for sp, sh in zip(out_specs, out_shape)]
    need = 2 * sum(blocks) + 6 * tile * max(a.shape[-1] for a in arrays) * 4

    def body(*refs):
        ins, outs, _, xrefs = _split_refs(refs, n_in, len(out_shape), 0, ex)
        i = pl.program_id(0)
        _ride(ex, xrefs, i, steps)
        res = fn(i, steps, *[r[...] for r in ins])
        res = res if isinstance(res, (tuple, list)) else (res,)
        for ref, val in zip(outs[:n_row], res[:n_row]):
            ref[...] = val.astype(ref.dtype)
        if acc_outs:
            @pl.when(i == 0)
            def _():
                for ref in outs[n_row:]:
                    ref[...] = jnp.zeros_like(ref)

            for ref, val in zip(outs[n_row:], res[n_row:]):
                ref[...] += val

    res = pl.pallas_call(
        body,
        grid=(steps,),
        in_specs=in_specs + (ex.in_specs if ex else []),
        out_specs=out_specs + (ex.out_specs if ex else []),
        out_shape=out_shape + (ex.out_shape if ex else []),
        scratch_shapes=ex.scratch if ex else [],
        compiler_params=_params(("arbitrary",), need),
        name=name,
    )(*arrays, *(ex.arrays if ex else []))
    return (res[:len(out_shape)], res[len(out_shape):]) if ex else res


def _ln_stats(z):
    mu = _mean(z)
    zc = z - mu
    rstd = lax.rsqrt(_mean(zc * zc) + LN_EPS)
    return zc * rstd, rstd


def _ln_bwd(dy, xhat, rstd, g):
    dxh = dy * g
    return rstd * (dxh - _mean(dxh) - xhat * _mean(dxh * xhat))


def _ride(ex, xrefs, step, steps):
    if not ex:
        return
    for at, act in ((0, ex.start), (steps - 2, ex.relay), (steps - 1, ex.finish)):
        @pl.when(step == at)
        def _(act=act):
            act(*xrefs)


def _sgu_fwd(h, wm, bs_t, g_v, b_v, ex=None):
    t = h.shape[1]
    steps = t // SGU_BLOCK

    def body(*refs):
        (u_ref, v_ref, wm_ref, bs_ref, g_ref, b_ref), (y_ref,), _, xrefs = _split_refs(refs, 6, 1, 0, ex)
        xhat, _ = _ln_stats(_gelu(v_ref[...]))
        vn = (xhat * g_ref[...] + b_ref[...]).astype(BF16)
        gu = _gelu(u_ref[...])
        for g in range(GROUPS):
            sl = slice(g * HEAD, (g + 1) * HEAD)
            mixed = _dot(wm_ref[g], vn[:, sl]) + bs_ref[:, g:g + 1]
            y_ref[:, sl] = (gu[:, sl] * mixed).astype(BF16)
        _ride(ex, xrefs, pl.program_id(0), steps)

    blk = lambda b: pl.BlockSpec((None, SGU_BLOCK, D_MODEL), functools.partial(lambda i, b: (b, i, 0), b=b))
    whole = lambda s: pl.BlockSpec(s, lambda i: (0,) * len(s))
    res = pl.pallas_call(
        body,
        grid=(steps,),
        in_specs=[blk(SLOT_U), blk(SLOT_V), whole(wm.shape), whole(bs_t.shape), whole(g_v.shape), whole(b_v.shape)]
        + (ex.in_specs if ex else []),
        out_specs=[pl.BlockSpec((SGU_BLOCK, D_MODEL), lambda i: (i, 0))] + (ex.out_specs if ex else []),
        out_shape=[jax.ShapeDtypeStruct((t, D_MODEL), BF16)] + (ex.out_shape if ex else []),
        scratch_shapes=ex.scratch if ex else [],
        compiler_params=_params(("arbitrary",)),
        name="sgu_fwd",
    )(h, h, wm, bs_t, g_v, b_v, *(ex.arrays if ex else []))
    return res[0], res[1:]


def _sgu_bwd(h, dy, dh, wm, wm_t, bs_t, g_v, b_v):
    t = h.shape[1]

    def body(u_ref, v_ref, dy_ref, dh_in, wm_ref, wmt_ref, bs_ref, g_ref, b_ref,
             duv_ref, dw_ref, dbs_ref, dg_ref, db_ref, dvn_ref):
        del dh_in
        du_ref, dv_ref = duv_ref.at[0], duv_ref.at[1]
        i = pl.program_id(0)

        @pl.when(i == 0)
        def _():
            dw_ref[...] = jnp.zeros_like(dw_ref)
            dbs_ref[...] = jnp.zeros_like(dbs_ref)
            dg_ref[...] = jnp.zeros_like(dg_ref)
            db_ref[...] = jnp.zeros_like(db_ref)

        u = u_ref[...]
        v = v_ref[...]
        gv, gvp = _gelu_and_grad(v)
        xhat, rstd = _ln_stats(gv)
        vn = (xhat * g_ref[...] + b_ref[...]).astype(BF16)
        gu, gup = _gelu_and_grad(u)
        lane = lax.broadcasted_iota(jnp.int32, (SGU_BLOCK, LANE), 1)
        dbs = jnp.zeros((SGU_BLOCK, LANE), F32)
        for g in range(GROUPS):
            sl = slice(g * HEAD, (g + 1) * HEAD)
            vn_g = vn[:, sl]
            mixed = _dot(wm_ref[g], vn_g) + bs_ref[:, g:g + 1]
            dy_g = dy_ref[:, sl]
            du_ref[:, sl] = (dy_g * mixed * gup[:, sl]).astype(BF16)
            dmix = dy_g * gu[:, sl]
            dmb = dmix.astype(BF16)
            dvn_ref[:, sl] = _dot(wmt_ref[g], dmb)
            dw_ref[g] += _dot_nt(dmb, vn_g)
            dbs = dbs + jnp.where(lane == g, jnp.sum(dmix, axis=1, keepdims=True), 0.0)
        dbs_ref[...] += dbs
        dvn = dvn_ref[...]
        dg_ref[...] += _sum0(dvn * xhat)
        db_ref[...] += _sum0(dvn)
        dv_ref[...] = (_ln_bwd(dvn, xhat, rstd, g_ref[...]) * gvp).astype(BF16)

    blk = lambda b: pl.BlockSpec((None, SGU_BLOCK, D_MODEL), functools.partial(lambda i, b: (b, i, 0), b=b))
    row = pl.BlockSpec((SGU_BLOCK, D_MODEL), lambda i: (i, 0))
    whole = lambda s: pl.BlockSpec(s, lambda i: (0,) * len(s))
    vec = (1, D_MODEL)
    return pl.pallas_call(
        body,
        grid=(t // SGU_BLOCK,),
        in_specs=[blk(SLOT_U), blk(SLOT_V), row, pl.BlockSpec(memory_space=pl.ANY),
                  whole(wm.shape), whole(wm_t.shape), whole(bs_t.shape), whole(vec), whole(vec)],
        out_specs=[pl.BlockSpec((2, SGU_BLOCK, D_MODEL), lambda i: (SLOT_U // 2, i, 0)),
                   whole(wm.shape), whole((SGU_BLOCK, LANE)), whole(vec), whole(vec)],
        out_shape=[jax.ShapeDtypeStruct(dh.shape, BF16),
                   jax.ShapeDtypeStruct(wm.shape, F32), jax.ShapeDtypeStruct((SGU_BLOCK, LANE), F32),
                   jax.ShapeDtypeStruct(vec, F32), jax.ShapeDtypeStruct(vec, F32)],
        scratch_shapes=[pltpu.VMEM((SGU_BLOCK, D_MODEL), F32)],
        input_output_aliases={3: 0},
        compiler_params=_params(("arbitrary",)),
        name="sgu_bwd",
    )(h, h, dy, dh, wm, wm_t, bs_t, g_v, b_v)


def _split3(x):
    hi = x.astype(BF16)
    r1 = x - hi.astype(F32)
    mid = r1.astype(BF16)
    lo = (r1 - mid.astype(F32)).astype(BF16)
    return hi, mid, lo


def _tri_matmul(tri, x):
    hi, mid, lo = _split3(x)
    dot = lambda p: jnp.dot(tri, p, preferred_element_type=F32)
    return dot(hi) + dot(mid) + dot(lo)


def _lower_bound(logits):
    l0, l1 = logits[0:1, :], logits[1:2, :]
    mx = jnp.maximum(l0, l1)
    e0, e1 = jnp.exp(l0 - mx), jnp.exp(l1 - mx)
    return e0 / (e0 + e1)


def _hgrn_gates(q_raw, f_raw, lb):
    q = q_raw * jax.nn.sigmoid(q_raw)
    sig = jax.nn.sigmoid(f_raw)
    f = lb + (1.0 - lb) * sig
    row = lax.broadcasted_iota(jnp.int32, (CHUNK, CHUNK), 0)
    col = lax.broadcasted_iota(jnp.int32, (CHUNK, CHUNK), 1)
    c = _tri_matmul((row >= col).astype(BF16), jnp.log(f))
    return q, sig, f, 1.0 - f, c


def _offdiag_terms(qh, kh, ch, tb):
    rows = slice(tb * SUB, (tb + 1) * SUB)
    r = ch[tb * SUB - 1:tb * SUB, :]
    eqh = jnp.exp(ch[rows] - r)
    ekh = jnp.exp(jnp.minimum(r - ch, 0.0))
    return rows, eqh, qh[rows] * eqh, ekh, kh * ekh


def _diag_decay(cb, s, trow):
    return jnp.exp(jnp.where(trow >= s, cb - cb[s:s + 1, :], NEG))


def _split_refs(refs, n_in, n_out, n_scratch, ex):
    nx = ex.n if ex else 0
    ins, refs = refs[:n_in], refs[n_in:]
    xsrc, refs = refs[:nx], refs[nx:]
    outs, refs = refs[:n_out], refs[n_out:]
    xout, refs = refs[:nx], refs[nx:]
    return ins, outs, refs[:n_scratch], (xsrc, xout, refs[n_scratch:])


def _hgrn_fwd(h, logits, g_norm, ex=None):
    t = h.shape[1]
    nc = t // CHUNK
    per_step = CHUNKS_PER_STEP if nc % CHUNKS_PER_STEP == 0 else 1
    steps = nc // per_step

    def body(*refs):
        ins, outs, scratch, xrefs = _split_refs(refs, 3, 3, 4, ex)
        hgrn_ref, lg_ref, gn_ref = ins
        y_ref, o_ref, sall_ref = outs
        st_ref, q_s, k_s, c_s = scratch

        @pl.when(pl.program_id(0) == 0)
        def _():
            st_ref[...] = jnp.zeros_like(st_ref)

        lb = _lower_bound(lg_ref[...])
        for sub in range(per_step):
            rows = pl.ds(sub * CHUNK, CHUNK)
            chunk(lb, sall_ref.at[sub], *(r.at[rows] for r in (hgrn_ref.at[SLOT_Q], hgrn_ref.at[SLOT_F], hgrn_ref.at[SLOT_I],
                                                               hgrn_ref.at[SLOT_OG], y_ref, o_ref)),
                  gn_ref, st_ref, q_s.at[sub], k_s.at[sub], c_s.at[sub])
        _ride(ex, xrefs, pl.program_id(0), steps)

    def chunk(lb, sall_ref, q_ref, f_ref, i_ref, og_ref, y_ref, o_ref, gn_ref, st_ref, q_s, k_s, c_s):
        q, _, _, k, c = _hgrn_gates(q_ref[...], f_ref[...], lb)
        q_s[...] = q
        k_s[...] = k
        c_s[...] = c
        col64 = lax.broadcasted_iota(jnp.int32, (SUB, CHUNK), 1)
        trow = lax.broadcasted_iota(jnp.int32, (SUB, HEAD), 0)

        def head(hd, carry):
            sl = pl.ds(pl.multiple_of(hd * HEAD, HEAD), HEAD)
            qh, kh, ch, ih = q_s[:, sl], k_s[:, sl], c_s[:, sl], i_ref[:, sl]
            st = st_ref[hd]
            sall_ref[hd] = st
            c_last = ch[CHUNK - 1:CHUNK, :]
            o = _dot_nt(qh * jnp.exp(ch), st)
            st_ref[hd] = st * jnp.exp(c_last) + _dot_tn(ih, kh * jnp.exp(c_last - ch))
            a_rows = [jnp.zeros((SUB, CHUNK), F32)]
            for tb in range(1, CHUNK // SUB):
                _, _, q_hat, _, k_hat = _offdiag_terms(qh, kh, ch, tb)
                a_rows.append(jnp.where(col64 < tb * SUB, _dot_nt(q_hat, k_hat), 0.0))
            o = o + _dot(jnp.concatenate(a_rows, axis=0), ih)
            o_rows = []
            for b in range(CHUNK // SUB):
                rows = slice(b * SUB, (b + 1) * SUB)
                qb, cb, kb, ib = qh[rows], ch[rows], kh[rows], ih[rows]
                ob = jnp.zeros((SUB, HEAD), F32)
                for s in range(SUB):
                    a = jnp.sum(qb * _diag_decay(cb, s, trow) * kb[s:s + 1, :], axis=1, keepdims=True)
                    ob = ob + a * ib[s:s + 1, :]
                o_rows.append(ob)
            o = o + jnp.concatenate(o_rows, axis=0)
            o_ref[:, sl] = o
            og = og_ref[:, sl]
            on = o * lax.rsqrt(_mean(o * o) + RMS_EPS)
            y_ref[:, sl] = (on * gn_ref[:, sl] * (og * jax.nn.sigmoid(og))).astype(BF16)
            return carry

        lax.fori_loop(0, HEADS, head, 0, unroll=HEAD_UNROLL)

    rows = per_step * CHUNK
    row = pl.BlockSpec((rows, D_MODEL), lambda n: (n, 0))
    whole = lambda s: pl.BlockSpec(s, lambda n: (0,) * len(s))
    res = pl.pallas_call(
        body,
        grid=(steps,),
        in_specs=[pl.BlockSpec((4, rows, D_MODEL), lambda n: (SLOT_Q // 4, n, 0)), whole(logits.shape), whole(g_norm.shape)]
        + (ex.in_specs if ex else []),
        out_specs=[row, row, pl.BlockSpec((per_step, HEADS, HEAD, HEAD), lambda n: (n, 0, 0, 0))]
        + (ex.out_specs if ex else []),
        out_shape=[jax.ShapeDtypeStruct((t, D_MODEL), BF16), jax.ShapeDtypeStruct((t, D_MODEL), F32),
                   jax.ShapeDtypeStruct((nc, HEADS, HEAD, HEAD), F32)] + (ex.out_shape if ex else []),
        scratch_shapes=[pltpu.VMEM((HEADS, HEAD, HEAD), F32)] + [pltpu.VMEM((per_step, CHUNK, D_MODEL), F32)] * 3
        + (ex.scratch if ex else []),
        compiler_params=_params(("arbitrary",)),
        name="hgrn_fwd",
    )(h, logits, g_norm, *(ex.arrays if ex else []))
    return res[0], res[1], res[2], res[3:]


def _hgrn_bwd(h, o_all, dy, states, dh, logits, g_norm, ex=None):
    t = h.shape[1]
    nc = t // CHUNK

    def body(*refs):
        ins, outs, scratch, xrefs = _split_refs(refs, 7, 3, 8, ex)
        hgrn_ref, o_ref, dy_ref, sall_ref, _, lg_ref, gn_ref = ins
        q_ref, f_ref, i_ref, og_ref = (hgrn_ref.at[s] for s in (SLOT_Q, SLOT_F, SLOT_I, SLOT_OG))
        dqfio_ref, dlg_ref, dgn_ref = outs
        dst_ref, dlb_ref, q_s, k_s, c_s, dq_s, dk_s, dc_s = scratch
        dq_ref, df_ref, di_ref, dog_ref = (dqfio_ref.at[s] for s in (SLOT_Q, SLOT_F, SLOT_I, SLOT_OG))
        n = pl.program_id(0)

        @pl.when(n == 0)
        def _():
            dst_ref[...] = jnp.zeros_like(dst_ref)
            dlb_ref[...] = jnp.zeros_like(dlb_ref)
            dgn_ref[...] = jnp.zeros_like(dgn_ref)
            if ex:
                ex.start(*xrefs)

        lb = _lower_bound(lg_ref[...])
        q_raw = q_ref[...]
        q, sig, f, k, c = _hgrn_gates(q_raw, f_ref[...], lb)
        q_s[...] = q
        k_s[...] = k
        c_s[...] = c
        col64 = lax.broadcasted_iota(jnp.int32, (SUB, CHUNK), 1)
        trow = lax.broadcasted_iota(jnp.int32, (SUB, HEAD), 0)
        row64 = lax.broadcasted_iota(jnp.int32, (CHUNK, HEAD), 0)

        def head(hd, carry):
            sl = pl.ds(pl.multiple_of(hd * HEAD, HEAD), HEAD)
            qh, kh, ch, ih = q_s[:, sl], k_s[:, sl], c_s[:, sl], i_ref[:, sl]
            st = sall_ref[hd]
            dst = dst_ref[hd]
            oh, dyh, og, gn = o_ref[:, sl], dy_ref[:, sl], og_ref[:, sl], gn_ref[:, sl]
            sg = jax.nn.sigmoid(og)
            sil = og * sg
            rms = lax.rsqrt(_mean(oh * oh) + RMS_EPS)
            on = oh * rms
            dog_ref[:, sl] = (dyh * on * gn * _silu_grad(og, sg)).astype(BF16)
            dgn_ref[:, sl] += _sum0(dyh * on * sil)
            don = dyh * gn * sil
            do = rms * (don - on * _mean(don * on))
            dob = do.astype(BF16)

            c_last = ch[CHUNK - 1:CHUNK, :]
            eq = jnp.exp(ch)
            q_til = qh * eq
            ekl = jnp.exp(c_last - ch)
            k_til = kh * ekl
            ecl = jnp.exp(c_last)
            dq_til = _dot(dob, st)
            dk_til = _dot(ih, dst)
            di = _dot_nt(k_til, dst)
            dc_last = _sum0(dk_til * k_til) + _sum0(dst * st) * ecl
            dst_ref[hd] = _dot_tn(dob, q_til) + dst * ecl
            dq = dq_til * eq
            dc = dq_til * q_til - dk_til * k_til
            dk = dk_til * ekl

            da_full = _dot_nt(dob, ih)
            a_rows = [jnp.zeros((SUB, CHUNK), F32)]
            dq_rows = [jnp.zeros((SUB, HEAD), F32)]
            dc_rows = [jnp.zeros((SUB, HEAD), F32)]
            for tb in range(1, CHUNK // SUB):
                rows, eqh, q_hat, ekh, k_hat = _offdiag_terms(qh, kh, ch, tb)
                keep = col64 < tb * SUB
                a_rows.append(jnp.where(keep, _dot_nt(q_hat, k_hat), 0.0))
                da = jnp.where(keep, da_full[rows], 0.0)
                dq_hat = _dot(da, k_hat)
                dk_hat = _dot_tn(da, q_hat)
                dq_rows.append(dq_hat * eqh)
                dc_rows.append(dq_hat * q_hat)
                dk = dk + dk_hat * ekh
                dc = dc - dk_hat * k_hat
            di = di + _dot_tn(jnp.concatenate(a_rows, axis=0), dob)

            dk_rows, di_rows = [], []
            for b in range(CHUNK // SUB):
                rows = slice(b * SUB, (b + 1) * SUB)
                qb, cb, kb, ib, dob_ = qh[rows], ch[rows], kh[rows], ih[rows], do[rows]
                dq_diag = jnp.zeros((SUB, HEAD), F32)
                dk_diag = jnp.zeros((SUB, HEAD), F32)
                di_diag = jnp.zeros((SUB, HEAD), F32)
                for s in range(SUB):
                    ks = kb[s:s + 1, :]
                    dec = _diag_decay(cb, s, trow)
                    a = jnp.sum(qb * dec * ks, axis=1, keepdims=True)
                    gk = jnp.sum(dob_ * ib[s:s + 1, :], axis=1, keepdims=True) * dec
                    dq_diag = dq_diag + gk * ks
                    dk_diag = dk_diag + jnp.where(trow == s, _sum0(gk * qb), 0.0)
                    di_diag = di_diag + jnp.where(trow == s, _sum0(a * dob_), 0.0)
                dq_rows[b] = dq_rows[b] + dq_diag
                dc_rows[b] = dc_rows[b] + qb * dq_diag - kb * dk_diag
                dk_rows.append(dk_diag)
                di_rows.append(di_diag)
            dq = dq + jnp.concatenate(dq_rows, axis=0)
            dk = dk + jnp.concatenate(dk_rows, axis=0)
            dc = dc + jnp.concatenate(dc_rows, axis=0) + jnp.where(row64 == CHUNK - 1, dc_last, 0.0)
            di_ref[:, sl] = (di + jnp.concatenate(di_rows, axis=0)).astype(BF16)
            dq_s[:, sl] = dq
            dk_s[:, sl] = dk
            dc_s[:, sl] = dc
            return carry

        lax.fori_loop(0, HEADS, head, 0, unroll=HEAD_UNROLL)

        row = lax.broadcasted_iota(jnp.int32, (CHUNK, CHUNK), 0)
        col = lax.broadcasted_iota(jnp.int32, (CHUNK, CHUNK), 1)
        dlf = _tri_matmul((row <= col).astype(BF16), dc_s[...])
        df = dlf / f - dk_s[...]
        dlb_ref[...] += _sum0(df * (1.0 - sig))
        df_ref[...] = (df * (1.0 - lb) * sig * (1.0 - sig)).astype(BF16)
        dq_ref[...] = (dq_s[...] * _silu_grad(q_raw, jax.nn.sigmoid(q_raw))).astype(BF16)

        @pl.when(n == nc - 1)
        def _():
            d0 = dlb_ref[...] * lb * (1.0 - lb)
            dlg_ref[0:1, :] = d0
            dlg_ref[1:2, :] = -d0
            if ex:
                ex.finish(*xrefs)

    rev = lambda n: nc - 1 - n
    slots = pl.BlockSpec((4, CHUNK, D_MODEL), lambda n: (SLOT_Q // 4, rev(n), 0))
    row = pl.BlockSpec((CHUNK, D_MODEL), lambda n: (rev(n), 0))
    whole = lambda s: pl.BlockSpec(s, lambda n: (0,) * len(s))
    vec = (1, D_MODEL)
    res = pl.pallas_call(
        body,
        grid=(nc,),
        in_specs=[slots, row, row, pl.BlockSpec((None, HEADS, HEAD, HEAD), lambda n: (rev(n), 0, 0, 0)),
                  pl.BlockSpec(memory_space=pl.ANY), whole(logits.shape), whole(vec)] + (ex.in_specs if ex else []),
        out_specs=[slots, whole((2, D_MODEL)), whole(vec)] + (ex.out_specs if ex else []),
        out_shape=[jax.ShapeDtypeStruct(dh.shape, BF16), jax.ShapeDtypeStruct((2, D_MODEL), F32),
                   jax.ShapeDtypeStruct(vec, F32)] + (ex.out_shape if ex else []),
        scratch_shapes=[pltpu.VMEM((HEADS, HEAD, HEAD), F32), pltpu.VMEM(vec, F32)]
        + [pltpu.VMEM((CHUNK, D_MODEL), F32)] * 6 + (ex.scratch if ex else []),
        input_output_aliases={4: 0},
        compiler_params=_params(("arbitrary",)),
        name="hgrn_bwd",
    )(h, o_all, dy, states, dh, logits, g_norm, *(ex.arrays if ex else []))
    return res[0], res[1], res[2], res[3:]


def _merge_fwd(i, n, ga, gb, za, zb):
    return jax.nn.sigmoid(ga) * za + jax.nn.sigmoid(gb) * zb


def _merge_bwd(i, n, ga, gb, za, zb, dm):
    sa, sb = jax.nn.sigmoid(ga), jax.nn.sigmoid(gb)
    dgates = jnp.stack([(dm * za * sa * (1.0 - sa)).astype(BF16), (dm * zb * sb * (1.0 - sb)).astype(BF16)])
    return dgates, dm * sa, dm * sb


def _ln1_fwd(i, n, x, r1, g, b):
    xhat, _ = _ln_stats(ALPHA * x + r1)
    x1 = xhat * g + b
    return x1, x1


def _ln1_bwd(i, n, x, r1, dx1, g):
    xhat, rstd = _ln_stats(ALPHA * x + r1)
    dz = _ln_bwd(dx1, xhat, rstd, g)
    return dz, dz, _sum0(dx1 * xhat), _sum0(dx1)


def _ln2_loss(i, n, x1, fo, pg, pp, tgt, g, b):
    sg = jax.nn.sigmoid(pg)
    xhat, rstd = _ln_stats(ALPHA * x1 + fo + sg * pp)
    diff = xhat * g + b - tgt
    loss = 0.5 * jnp.sum(_mean(diff * diff), axis=0, keepdims=True)
    dy = diff * (1.0 / D_MODEL)
    dz = _ln_bwd(dy, xhat, rstd, g)
    return (dz, dz, dz * pp * sg * (1.0 - sg), dz * sg,
            jnp.broadcast_to(loss, (8, LANE)), _sum0(dy * xhat), _sum0(dy))


def _shift_down(cur, halo, tile):
    row = lax.broadcasted_iota(jnp.int32, cur.shape, 0)
    m1 = jnp.where(row == 0, halo[7:8, :], pltpu.roll(cur, 1, 0))
    m2 = jnp.where(row == 0, halo[6:7, :], jnp.where(row == 1, halo[7:8, :], pltpu.roll(cur, 2, 0)))
    return m1, m2


def _shift_up(cur, halo, tile):
    row = lax.broadcasted_iota(jnp.int32, cur.shape, 0)
    p1 = jnp.where(row == tile - 1, halo[0:1, :], pltpu.roll(cur, tile - 1, 0))
    p2 = jnp.where(row == tile - 2, halo[0:1, :], jnp.where(row == tile - 1, halo[1:2, :], pltpu.roll(cur, tile - 2, 0)))
    return p1, p2


def _conv_pre(i, gate, halo, w, b, tile):
    halo = jnp.where(i == 0, 0.0, halo)
    m1, m2 = _shift_down(gate, halo, tile)
    return w[0:1, :] * m2 + w[1:2, :] * m1 + w[2:3, :] * gate + b, m1, m2


def _conv_fwd(tile, i, n, gate, halo, val, w, b):
    cg, _, _ = _conv_pre(i, gate, halo, w, b, tile)
    return _gelu(cg) * val


def _conv_bwd_a(tile, i, n, gate, halo, val, dhid, w, b):
    cg, m1, m2 = _conv_pre(i, gate, halo, w, b, tile)
    act, slope = _gelu_and_grad(cg)
    dcg = dhid * val * slope
    return dcg, dhid * act, _sum0(dcg * m2), _sum0(dcg * m1), _sum0(dcg * gate), _sum0(dcg)


def _conv_bwd_b(tile, i, n, dcg, halo, w):
    dcg = dcg.astype(F32)
    halo = jnp.where(i == n - 1, 0.0, halo.astype(F32))
    p1, p2 = _shift_up(dcg, halo, tile)
    return w[2:3, :] * dcg + w[1:2, :] * p1 + w[0:1, :] * p2


def _halo_spec(width, tile, t, nxt, rows=8):
    per = tile // rows
    last = t // rows - 1
    if nxt:
        return pl.BlockSpec((rows, width), lambda i: (jnp.minimum((i + 1) * per, last), 0))
    return pl.BlockSpec((rows, width), lambda i: (jnp.maximum(i * per - 1, 0), 0))


def _adamw(i, n, w, m, v, parts):
    g = parts[0].astype(F32)
    for j in range(1, parts.shape[0]):
        g = g + parts[j].astype(F32)
    m_new = ADAM_B1 * m + (1.0 - ADAM_B1) * g
    v_new = ADAM_B2 * v + (1.0 - ADAM_B2) * (g * g)
    m_hat = m_new / (1.0 - ADAM_B1 ** ADAM_STEP)
    v_hat = v_new / (1.0 - ADAM_B2 ** ADAM_STEP)
    delta = -ADAM_LR * (m_hat / (jnp.sqrt(v_hat) + ADAM_EPS) + ADAM_WD * w)
    return g, delta, m_new, v_new


def _adam_call(w, m, v, parts, name):
    r, c = w.shape
    tile = _pick(r, (256, 128)) if r > 256 else r
    spec = pl.BlockSpec((parts.shape[0], tile, c), lambda i: (0, i, 0))
    return _rowwise(_adamw, [w, m, v, (parts, spec)], [], [(c, F32)] * 4, [], tile=tile, name=name)


def _peer(k):
    x, y, c = lax.axis_index("x"), lax.axis_index("y"), lax.axis_index("c")
    px = x ^ ((k >> 2) & 1)
    py = y ^ ((k >> 1) & 1)
    pc = c ^ (k & 1)
    return (px, py, pc), 4 * px + 2 * py + pc


def _my_index():
    return 4 * lax.axis_index("x") + 2 * lax.axis_index("y") + lax.axis_index("c")


class _Exchange:
    KINDS = ("gather", "gather+relay", "scatter", "scatter+pairs")

    def __init__(self, entries):
        assert all(k in self.KINDS for _, k in entries), [k for _, k in entries]
        self.arrays = [a for a, _ in entries]
        self.scatter = [k.startswith("scatter") for _, k in entries]
        self.relayed = ["+relay" in k for _, k in entries]
        self.pairs = ["+pairs" in k for _, k in entries]
        self.n = len(entries)
        self.in_specs = [pl.BlockSpec(memory_space=pl.ANY)] * self.n
        self.out_specs = [pl.BlockSpec(memory_space=pl.ANY)] * self.n
        shapes = [tuple(a.shape[1:]) if sc else tuple(a.shape) for a, sc in zip(self.arrays, self.scatter)]
        counts = [N_CHIP if p else N_DEV for p in self.pairs]
        self.out_shape = [jax.ShapeDtypeStruct((n,) + s, a.dtype) for n, s, a in zip(counts, shapes, self.arrays)]
        per = N_DEV - 1
        self.scratch = [pltpu.SemaphoreType.DMA((self.n * per,)), pltpu.SemaphoreType.DMA((self.n * per,)),
                        pltpu.SemaphoreType.DMA((self.n,))]

    def _copies(self, srcs, outs, sems):
        send_sems, recv_sems, local_sems = sems
        x, y, c = lax.axis_index("x"), lax.axis_index("y"), lax.axis_index("c")
        me = _my_index()
        per = N_DEV - 1
        local, first, passed, relay_arrivals, arrivals = [], [], [], [], []
        for a in range(self.n):

            def copy(k, src, dst, dev, a=a):
                return pltpu.make_async_remote_copy(
                    src_ref=src, dst_ref=dst, send_sem=send_sems.at[a * per + k], recv_sem=recv_sems.at[a * per + k],
                    device_id=dev, device_id_type=pl.DeviceIdType.MESH)

            if self.pairs[a]:
                chip = 2 * x + y
                for k in range(N_CHIP):
                    to = chip ^ k
                    piece = srcs[a].at[_slot_of_group(2 * to + c) // 2]
                    if k == 0:
                        local.append(pltpu.make_async_copy(piece, outs[a].at[chip], local_sems.at[a]))
                    else:
                        dev = (to // 2, to % 2, c)
                        first.append(copy(k - 1, piece, outs[a].at[chip], dev))
                        arrivals.append(copy(k - 1, piece, outs[a].at[to], dev))
                continue
            mine = srcs[a].at[me] if self.scatter[a] else srcs[a]
            land = outs[a].at[me]
            local.append(pltpu.make_async_copy(mine, land, local_sems.at[a]))
            if self.relayed[a]:
                block = lambda px, py, pc, a=a: outs[a].at[4 * px + 2 * py + pc]
                chips = [(1 - x, y), (x, 1 - y), (1 - x, 1 - y)]
                first.append(copy(0, mine, land, (x, y, 1 - c)))
                arrivals.append(copy(0, mine, block(x, y, 1 - c), (x, y, 1 - c)))
                for j, (px, py) in enumerate(chips):
                    first.append(copy(1 + j, mine, land, (px, py, c)))
                    relay_arrivals.append(copy(1 + j, mine, block(px, py, c), (px, py, c)))
                    passed.append(copy(4 + j, block(px, py, c), block(px, py, c), (x, y, 1 - c)))
                    arrivals.append(copy(4 + j, mine, block(px, py, 1 - c), (x, y, 1 - c)))
                continue
            for k in range(1, N_DEV):
                dev, idx = _peer(k)
                if self.scatter[a]:
                    first.append(copy(k - 1, srcs[a].at[idx], land, dev))
                else:
                    first.append(copy(k - 1, mine, land, dev))
                arrivals.append(copy(k - 1, mine, outs[a].at[idx], dev))
        return local, first, passed, relay_arrivals, arrivals

    def start(self, srcs, outs, sems):
        local, first, _, _, _ = self._copies(srcs, outs, sems)
        for cp in local + first:
            cp.start()

    def relay(self, srcs, outs, sems):
        _, _, passed, relay_arrivals, _ = self._copies(srcs, outs, sems)
        for landed, onward in zip(relay_arrivals, passed):
            landed.wait_recv()
            onward.start()

    def finish(self, srcs, outs, sems):
        local, first, passed, _, arrivals = self._copies(srcs, outs, sems)
        for cp in arrivals:
            cp.wait_recv()
        for cp in first + passed:
            cp.wait_send()
        for cp in local:
            cp.wait()


def _gather_project(x_b, shard):
    t, d = x_b.shape
    tm = _pick(t, MM_TILES)
    nrow = t // tm
    per = N_DEV - 1

    def parties():
        x, y, c = lax.axis_index("x"), lax.axis_index("y"), lax.axis_index("c")
        chips = [(1 - x, y), (x, 1 - y), (1 - x, 1 - y)]
        return (x, y, c), (x, y, 1 - c), [(px, py, c) for px, py in chips], [(px, py, 1 - c) for px, py in chips]

    slot = lambda dev: _slot_of_group(4 * dev[0] + 2 * dev[1] + dev[2])
    ici_step, passed_step = (2, 3, 6), (4, 5, 7)
    me, sibling, over_ici, passed_on = parties()
    by_step = {0: me, 1: sibling, **dict(zip(ici_step, over_ici)), **dict(zip(passed_step, passed_on))}
    order = jnp.stack([slot(by_step[j]) for j in range(N_DEV)]).astype(jnp.int32)

    def body(order_ref, x_ref, shard_ref, h_ref, wall_ref, wbuf, fetch_sem, send_sems, recv_sems, local_sem):
        del order_ref
        me, sibling, over_ici, passed_on = parties()
        j, i = pl.program_id(0), pl.program_id(1)
        land = lambda dev: wall_ref.at[slot(dev)]

        def copy(k, src, block, to):
            return pltpu.make_async_remote_copy(src_ref=src, dst_ref=land(block), send_sem=send_sems.at[k],
                                                recv_sem=recv_sems.at[k], device_id=to, device_id_type=pl.DeviceIdType.MESH)

        def fetch(src):
            cp = pltpu.make_async_copy(src, wbuf, fetch_sem)
            cp.start()
            cp.wait()

        keep = pltpu.make_async_copy(shard_ref, land(me), local_sem)
        first = [copy(0, shard_ref, me, sibling)] + [copy(1 + n, shard_ref, me, dev) for n, dev in enumerate(over_ici)]
        onward = [copy(4 + n, land(dev), dev, sibling) for n, dev in enumerate(over_ici)]

        @pl.when(jnp.logical_and(i == 0, j == 0))
        def _():
            keep.start()
            for cp in first:
                cp.start()
            fetch(shard_ref)

        @pl.when(jnp.logical_and(i == 0, j == 1))
        def _():
            copy(0, shard_ref, sibling, me).wait_recv()
            fetch(land(sibling))

        for n, dev in enumerate(over_ici):
            @pl.when(jnp.logical_and(i == 0, j == ici_step[n]))
            def _(n=n, dev=dev):
                copy(1 + n, shard_ref, dev, me).wait_recv()
                onward[n].start()
                fetch(land(dev))

        for n, dev in enumerate(passed_on):
            @pl.when(jnp.logical_and(i == 0, j == passed_step[n]))
            def _(n=n, dev=dev):
                copy(4 + n, shard_ref, dev, me).wait_recv()
                fetch(land(dev))

        h_ref[...] = _dot(x_ref[...], wbuf[...])

        @pl.when(jnp.logical_and(i == nrow - 1, j == N_DEV - 1))
        def _():
            for cp in first + onward:
                cp.wait_send()
            keep.wait()

    need = 2 * (tm * d * 2 + tm * d * 4) + d * d * 2 + tm * d * 4
    h, w_all = pl.pallas_call(
        body,
        grid_spec=pltpu.PrefetchScalarGridSpec(
            num_scalar_prefetch=1,
            grid=(N_DEV, nrow),
            in_specs=[pl.BlockSpec((tm, d), lambda j, i, order: (i, 0)), pl.BlockSpec(memory_space=pl.ANY)],
            out_specs=[pl.BlockSpec((None, tm, d), lambda j, i, order: (order[j], i, 0)), pl.BlockSpec(memory_space=pl.ANY)],
            scratch_shapes=[pltpu.VMEM((d, d), BF16), pltpu.SemaphoreType.DMA, pltpu.SemaphoreType.DMA((per,)),
                            pltpu.SemaphoreType.DMA((per,)), pltpu.SemaphoreType.DMA],
        ),
        out_shape=[jax.ShapeDtypeStruct((N_DEV, t, d), F32), jax.ShapeDtypeStruct((N_DEV, d, d), BF16)],
        compiler_params=_params(("arbitrary", "arbitrary"), need),
        name="gather_project",
    )(order, x_b, shard)
    return h, w_all


def _pair_sums(g):
    n, r, cols = g.shape
    half = n // 2

    def swap(g_ref, got_ref, send_sems, recv_sems):
        x, y, c = lax.axis_index("x"), lax.axis_index("y"), lax.axis_index("c")
        copies = [pltpu.make_async_remote_copy(
            src_ref=g_ref.at[2 * j + 1 - c], dst_ref=got_ref.at[j], send_sem=send_sems.at[j], recv_sem=recv_sems.at[j],
            device_id=(x, y, 1 - c), device_id_type=pl.DeviceIdType.MESH) for j in range(half)]
        for cp in copies:
            cp.start()
        for cp in copies:
            cp.wait()

    got = pl.pallas_call(
        swap,
        in_specs=[pl.BlockSpec(memory_space=pl.ANY)],
        out_specs=pl.BlockSpec(memory_space=pl.ANY),
        out_shape=jax.ShapeDtypeStruct((half, r, cols), g.dtype),
        scratch_shapes=[pltpu.SemaphoreType.DMA((half,))] * 2,
        name="pair_swap",
    )(g)

    def add(mine_ref, got_ref, out_ref):
        mine = jnp.where(lax.axis_index("c") == 0, mine_ref[0].astype(F32), mine_ref[1].astype(F32))
        out_ref[...] = (mine + got_ref[...].astype(F32)).astype(out_ref.dtype)

    tile = _pick(r, (512, 256, 128))
    return pl.pallas_call(
        add,
        grid=(half, r // tile),
        in_specs=[pl.BlockSpec((None, 2, tile, cols), lambda j, i: (j, 0, i, 0)),
                  pl.BlockSpec((None, tile, cols), lambda j, i: (j, i, 0))],
        out_specs=pl.BlockSpec((None, tile, cols), lambda j, i: (j, i, 0)),
        out_shape=jax.ShapeDtypeStruct((half, r, cols), g.dtype),
        compiler_params=_params(("parallel", "parallel")),
        name="pair_add",
    )(g.reshape(half, 2, r, cols), got)


def _local_step(x, p, tgt, small, comm):
    t = x.shape[0]
    tile = _pick(t, (256, 128))
    tall = _pick(t, (512, 256, 128))
    d = D_MODEL
    act_b, act_f = (d, BF16), (d, F32)
    x_b, p_b = x.astype(BF16), p.astype(BF16)

    chunk_id = jnp.arange(SGU_BLOCK) // CHUNK
    mask = chunk_id[:, None] >= chunk_id[None, :]
    wm = jnp.where(mask[None], small["sgu_w_s"], 0.0)
    wm_b = wm.astype(BF16)
    wm_t = jnp.swapaxes(wm, 1, 2).astype(BF16)
    bs_t = small["sgu_b_s"].T

    h, w_in = comm.project_in(x_b)
    y_a, got_a = _sgu_fwd(h, wm_b, bs_t, small["sgu_norm_g"], small["sgu_norm_b"], ex=comm.weights_exchange(0))
    y_b, o_all, states, got_b = _hgrn_fwd(h, small["lb_logits"], small["hgrn_norm_g"], ex=comm.weights_exchange(1))
    wts, conv_w = comm.weights(got_a, got_b)
    z_a = _mm(y_a, wts["w_a"], out_dtype=F32, name="mm_za")
    z_b = _mm(y_b, wts["w_b"], out_dtype=F32, name="mm_zb")
    gates = [(h, SLOT_GA), (h, SLOT_GB)]
    merged, = _rowwise(_merge_fwd, gates + [z_a, z_b], [], [act_b], [], tile=tall, name="merge_fwd")
    r1 = _mm(merged, wts["w_o"], out_dtype=F32, name="mm_r1")
    x1, x1_b = _rowwise(_ln1_fwd, [x, r1], [small["ln1_g"], small["ln1_b"]], [act_f, act_b], [], tile=tall, name="ln1_fwd")
    gate = _mm(x1_b, wts["w_g"], out_dtype=F32, name="mm_gate")
    val = _mm(x1_b, wts["w_v"], out_dtype=F32, name="mm_val")
    ex = comm.weights_exchange(2)
    res = _rowwise(functools.partial(_conv_fwd, tile), [gate, (gate, _halo_spec(D_FF, tile, t, False)), val],
                   [conv_w, small["conv_b"]], [(D_FF, BF16)], [], tile=tile, name="conv_fwd", ex=ex)
    (hid,), got_c = res if ex else (res, ())
    wts.update(comm.late_weights(got_c))
    pg = _mm(x1_b, wts["w_pg"], out_dtype=F32, name="mm_pg")
    pp = _mm(p_b, wts["w_pp"], out_dtype=F32, name="mm_pp")
    fo = _mm(hid, wts["w_down"], out_dtype=F32, name="mm_down")
    dz2, dz2_b, dpg, dpp, loss, dg2, db2 = _rowwise(
        _ln2_loss, [x1, fo, pg, pp, tgt], [small["ln2_g"], small["ln2_b"]],
        [act_f, act_b, act_b, act_b], [(8, LANE), (1, d), (1, d)], tile=tile, name="ln2_loss")

    dhid = _mm(dz2_b, wts["w_down"], out_dtype=BF16, name="mm_dhid", trans_b=True)
    g_down = _mm_tn(hid, dz2_b, out_dtype=BF16, name="mm_g_down")
    dcg, dval, dcw0, dcw1, dcw2, dcb = _rowwise(
        functools.partial(_conv_bwd_a, tile), [gate, (gate, _halo_spec(D_FF, tile, t, False)), val, dhid],
        [conv_w, small["conv_b"]], [(D_FF, BF16), (D_FF, BF16)], [(1, D_FF)] * 4, tile=tile, name="conv_bwd_a")
    dgate, = _rowwise(functools.partial(_conv_bwd_b, tile), [dcg, (dcg, _halo_spec(D_FF, tile, t, True, rows=16))],
                      [conv_w], [(D_FF, BF16)], [], tile=tile, name="conv_bwd_b")
    g_g = _mm_tn(x1_b, dgate, out_dtype=BF16, name="mm_g_gate")
    g_v = _mm_tn(x1_b, dval, out_dtype=BF16, name="mm_g_val")
    g_pg = _mm_tn(x1_b, dpg, out_dtype=BF16, name="mm_g_pg")
    g_pp = _mm_tn(p_b, dpp, out_dtype=BF16, name="mm_g_pp")
    dx1 = _mm(dgate, wts["w_g"], out_dtype=F32, name="mm_dx1_gate", trans_b=True, adds=[(dz2, ALPHA)])
    dx1 = _mm(dval, wts["w_v"], out_dtype=F32, name="mm_dx1_val", trans_b=True, adds=[(dx1, 1.0)])
    dx1 = _mm(dpg, wts["w_pg"], out_dtype=F32, name="mm_dx1_pg", trans_b=True, adds=[(dx1, 1.0)])
    dz1, dz1_b, dg1, db1 = _rowwise(_ln1_bwd, [x, r1, dx1], [small["ln1_g"]], [act_f, act_b], [(1, d), (1, d)],
                                    tile=tall, name="ln1_bwd")
    g_o = _mm_tn(merged, dz1_b, out_dtype=BF16, name="mm_g_o")
    dm = _mm(dz1_b, wts["w_o"], out_dtype=BF16, name="mm_dm", trans_b=True)
    dh, dza, dzb = _rowwise(_merge_bwd, gates + [z_a, z_b, dm], [],
                            [("stack", 2, SLOT_GA // 2, 8, d, BF16), act_b, act_b], [], tile=tall, name="merge_bwd")
    g_a = _mm_tn(y_a, dza, out_dtype=BF16, name="mm_g_a")
    g_b = _mm_tn(y_b, dzb, out_dtype=BF16, name="mm_g_b")
    dy_a = _mm(dza, wts["w_a"], out_dtype=BF16, name="mm_dya", trans_b=True)
    dy_b = _mm(dzb, wts["w_b"], out_dtype=F32, name="mm_dyb", trans_b=True)
    dh, dws, dbs, dgv_n, dbv_n = _sgu_bwd(h, dy_a, dh, wm_b, wm_t, bs_t, small["sgu_norm_g"], small["sgu_norm_b"])
    big = dict(w_a=g_a, w_b=g_b, w_o=g_o, w_g=g_g, w_v=g_v, w_down=g_down, w_pp=g_pp, w_pg=g_pg)
    sm = dict(sgu_w_s=jnp.where(mask[None], dws, 0.0), sgu_b_s=dbs[:, :GROUPS].T, sgu_norm_g=dgv_n, sgu_norm_b=dbv_n,
              ln1_g=dg1, ln1_b=db1, conv_w=jnp.concatenate([dcw0, dcw1, dcw2], axis=0), conv_b=dcb, ln2_g=dg2, ln2_b=db2,
              loss=loss)
    dh, dlogits, dgn, got = _hgrn_bwd(h, o_all, dy_b, states, dh, small["lb_logits"], small["hgrn_norm_g"],
                                      ex=comm.grads_exchange(big, sm))
    comm.grads_done(got)
    g_in = _mm_tn(x_b, dh, out_dtype=BF16, name="mm_g_in")
    ex = comm.last_exchange(g_in, dict(lb_logits=dlogits, hgrn_norm_g=dgn))
    res = _mm(dh, w_in, out_dtype=F32, name="mm_dx", trans_b=True, reduce_b=True, adds=[(dz1, ALPHA)], ex=ex)
    grad_x, got = res if ex else (res, ())
    comm.last_done(got)
    return grad_x


_SMALL_EARLY = ["sgu_w_s", "sgu_b_s", "sgu_norm_g", "sgu_norm_b", "ln1_g", "ln1_b", "ffn_conv_b", "ln2_g", "ln2_b"]
_SMALL_LATE = ["hgrn_lb_logits", "hgrn_norm_g"]
N_TAPS = D_FF // N_DEV
UP_COLS = 2 * D_FF // N_DEV


class _StepExchanges:
    def __init__(self, w_in_shard, shards):
        self.w_in_shard = w_in_shard
        self.shards = shards

    def project_in(self, x_b):
        return _gather_project(x_b, self.w_in_shard)

    PARTS = ((0, 1), (2, 6), (3, 4, 5))

    def weights_exchange(self, part):
        return _Exchange([(self.shards[n], "gather+relay") for n in self.PARTS[part]])

    def weights(self, got_first, got_second):
        d, f = D_MODEL, D_FF
        (w_br_g, w_o_g), (w_up_g, conv_g) = got_first, got_second
        w_br = w_br_g.transpose(1, 0, 2, 3).reshape(2, d, d)
        w_up = w_up_g.transpose(1, 0, 2).reshape(d, 2, f).transpose(1, 0, 2)
        wts = dict(w_a=w_br[0], w_b=w_br[1], w_o=w_o_g.reshape(d, d), w_g=w_up[0], w_v=w_up[1])
        return wts, conv_g.transpose(1, 0, 2).reshape(3, f)

    def late_weights(self, got_third):
        w_down_g, w_pp_g, w_pg_g = got_third
        return dict(w_down=w_down_g.reshape(D_FF, D_MODEL), w_pp=w_pp_g.transpose(1, 0, 2).reshape(256, D_MODEL),
                    w_pg=w_pg_g.reshape(D_MODEL, D_MODEL))

    def grads_exchange(self, big, sm):
        d = D_MODEL
        parts = [jnp.stack([big["w_a"], big["w_b"]]).reshape(2, N_DEV, 128, d).transpose(1, 0, 2, 3),
                 big["w_o"].reshape(N_DEV, 128, d),
                 jnp.concatenate([big["w_g"], big["w_v"]], axis=1).reshape(d, N_DEV, UP_COLS).transpose(1, 0, 2),
                 big["w_down"].reshape(N_DEV, N_TAPS, d),
                 big["w_pp"].reshape(256, N_DEV, 128).transpose(1, 0, 2),
                 big["w_pg"].reshape(N_DEV, 128, d)]
        packed, self.rows_early = _pack([sm[k] for k in ("sgu_w_s", "sgu_b_s", "sgu_norm_g", "sgu_norm_b", "ln1_g", "ln1_b",
                                                         "conv_b", "ln2_g", "ln2_b", "conv_w", "loss")])
        return _Exchange([(a, "scatter") for a in parts] + [(packed, "gather")])

    def grads_done(self, got):
        self.recv, self.small_early = got[:6], got[6]

    def last_exchange(self, g_in, sm):
        packed, self.rows_late = _pack([sm["lb_logits"], sm["hgrn_norm_g"]])
        return _Exchange([(_pair_sums(g_in), "scatter+pairs"), (packed, "gather")])

    def last_done(self, got):
        self.recv_in, self.small_late = got


def _rows128(a):
    flat = a.reshape(-1)
    rows = -(-flat.shape[0] // (8 * LANE)) * 8
    return jnp.pad(flat, (0, rows * LANE - flat.shape[0])).reshape(rows, LANE)


def _pack(parts):
    blocks = [_rows128(a) for a in parts]
    return jnp.concatenate(blocks, axis=0), [b.shape[0] for b in blocks]


def _unpack(packed, shapes, rows):
    out, r0 = [], 0
    for shp, r in zip(shapes, rows):
        n = math.prod(shp)
        out.append(packed[r0:r0 + r].reshape(-1)[:n].reshape(shp))
        r0 += r
    return out


def kernel(x, p, w_in, sgu_w_s, sgu_b_s, sgu_norm_g, sgu_norm_b, hgrn_lb_logits, hgrn_norm_g, w_branch, w_out, ln1_g, ln1_b, ffn_w_up, ffn_conv_w, ffn_conv_b, ffn_w_down, ln2_g, ln2_b, ple_w_proj, ple_w_gate, loss_target, m_w_in, m_sgu_w_s, m_sgu_b_s, m_sgu_norm_g, m_sgu_norm_b, m_hgrn_lb_logits, m_hgrn_norm_g, m_w_branch, m_w_out, m_ln1_g, m_ln1_b, m_ffn_w_up, m_ffn_conv_w, m_ffn_conv_b, m_ffn_w_down, m_ln2_g, m_ln2_b, m_ple_w_proj, m_ple_w_gate, v_w_in, v_sgu_w_s, v_sgu_b_s, v_sgu_norm_g, v_sgu_norm_b, v_hgrn_lb_logits, v_hgrn_norm_g, v_w_branch, v_w_out, v_ln1_g, v_ln1_b, v_ffn_w_up, v_ffn_conv_w, v_ffn_conv_b, v_ffn_w_down, v_ln2_g, v_ln2_b, v_ple_w_proj, v_ple_w_gate):
    weights = dict(w_in=w_in, sgu_w_s=sgu_w_s, sgu_b_s=sgu_b_s, sgu_norm_g=sgu_norm_g, sgu_norm_b=sgu_norm_b,
                   hgrn_lb_logits=hgrn_lb_logits, hgrn_norm_g=hgrn_norm_g, w_branch=w_branch, w_out=w_out,
                   ln1_g=ln1_g, ln1_b=ln1_b, ffn_w_up=ffn_w_up, ffn_conv_w=ffn_conv_w, ffn_conv_b=ffn_conv_b,
                   ffn_w_down=ffn_w_down, ln2_g=ln2_g, ln2_b=ln2_b, ple_w_proj=ple_w_proj, ple_w_gate=ple_w_gate)
    mom_m = dict(w_in=m_w_in, sgu_w_s=m_sgu_w_s, sgu_b_s=m_sgu_b_s, sgu_norm_g=m_sgu_norm_g, sgu_norm_b=m_sgu_norm_b,
                 hgrn_lb_logits=m_hgrn_lb_logits, hgrn_norm_g=m_hgrn_norm_g, w_branch=m_w_branch, w_out=m_w_out,
                 ln1_g=m_ln1_g, ln1_b=m_ln1_b, ffn_w_up=m_ffn_w_up, ffn_conv_w=m_ffn_conv_w, ffn_conv_b=m_ffn_conv_b,
                 ffn_w_down=m_ffn_w_down, ln2_g=m_ln2_g, ln2_b=m_ln2_b, ple_w_proj=m_ple_w_proj, ple_w_gate=m_ple_w_gate)
    mom_v = dict(w_in=v_w_in, sgu_w_s=v_sgu_w_s, sgu_b_s=v_sgu_b_s, sgu_norm_g=v_sgu_norm_g, sgu_norm_b=v_sgu_norm_b,
                 hgrn_lb_logits=v_hgrn_lb_logits, hgrn_norm_g=v_hgrn_norm_g, w_branch=v_w_branch, w_out=v_w_out,
                 ln1_g=v_ln1_g, ln1_b=v_ln1_b, ffn_w_up=v_ffn_w_up, ffn_conv_w=v_ffn_conv_w, ffn_conv_b=v_ffn_conv_b,
                 ffn_w_down=v_ffn_w_down, ln2_g=v_ln2_g, ln2_b=v_ln2_b, ple_w_proj=v_ple_w_proj, ple_w_gate=v_ple_w_gate)
    d, f = D_MODEL, D_FF
    me = _my_index()

    comm = _StepExchanges(w_in[0].astype(BF16),
                          [w_branch[0].astype(BF16), w_out[0].astype(BF16), ffn_w_up[0].astype(BF16),
                           ffn_w_down[0].astype(BF16), ple_w_proj[0].astype(BF16), ple_w_gate[0].astype(BF16), ffn_conv_w[0]])
    small = dict(sgu_w_s=sgu_w_s[0], sgu_b_s=sgu_b_s[0], sgu_norm_g=sgu_norm_g, sgu_norm_b=sgu_norm_b,
                 lb_logits=hgrn_lb_logits, hgrn_norm_g=hgrn_norm_g, ln1_g=ln1_g, ln1_b=ln1_b, ln2_g=ln2_g, ln2_b=ln2_b,
                 conv_b=ffn_conv_b)
    grad_x = _local_step(x[0], p[0, 0], loss_target[0], small, comm)

    out = {}

    def adam(name, parts8, shape2d):
        w2, m2, v2 = (a.reshape(shape2d) for a in (weights[name], mom_m[name], mom_v[name]))
        res = _adam_call(w2, m2, v2, parts8.reshape(parts8.shape[:1] + shape2d), "adam_" + name)
        out[name] = tuple(r.reshape(weights[name].shape) for r in res)

    adam("w_in", comm.recv_in, (d, d))
    adam("w_branch", comm.recv[0], (256, d))
    adam("w_out", comm.recv[1], (128, d))
    adam("ffn_w_up", comm.recv[2], (d, UP_COLS))
    adam("ffn_w_down", comm.recv[3], (N_TAPS, d))
    adam("ple_w_proj", comm.recv[4], (256, 128))
    adam("ple_w_gate", comm.recv[5], (128, d))

    def adam_small(names, extra_w, extra_m, extra_v, extra_shapes, parts8, rows, label):
        pk = lambda src, extra: _pack([src[n] for n in names] + extra)[0]
        res = _adam_call(pk(weights, extra_w), pk(mom_m, extra_m), pk(mom_v, extra_v), parts8, label)
        shapes = [weights[n].shape for n in names] + extra_shapes
        unpacked = [_unpack(r, shapes, rows) for r in res]
        for j, n in enumerate(names):
            out[n] = tuple(u[j] for u in unpacked)
        return [[u[len(names) + j] for u in unpacked] for j in range(len(extra_shapes))]

    blank = jnp.zeros((8, LANE), F32)
    taps, loss_rows = adam_small(
        _SMALL_EARLY, [_place_taps(ffn_conv_w[0], me, f), blank], [_place_taps(m_ffn_conv_w[0], me, f), blank],
        [_place_taps(v_ffn_conv_w[0], me, f), blank + 1.0], [(3, f), (8, LANE)], comm.small_early, comm.rows_early,
        "adam_small_early")
    adam_small(_SMALL_LATE, [], [], [], [], comm.small_late, comm.rows_late, "adam_small_late")
    out["ffn_conv_w"] = tuple(lax.dynamic_slice_in_dim(u, me * N_TAPS, N_TAPS, axis=1)[None] for u in taps)
    loss = loss_rows[0][0, 0]

    order = ["w_in", "sgu_w_s", "sgu_b_s", "sgu_norm_g", "sgu_norm_b", "hgrn_lb_logits", "hgrn_norm_g", "w_branch", "w_out",
             "ln1_g", "ln1_b", "ffn_w_up", "ffn_conv_w", "ffn_conv_b", "ffn_w_down", "ln2_g", "ln2_b", "ple_w_proj", "ple_w_gate"]
    return (loss, grad_x[None], *[out[n][0] for n in order], *[out[n][1] for n in order],
            *[out[n][2] for n in order], *[out[n][3] for n in order])


def _place_taps(shard, me, f):
    return lax.dynamic_update_slice_in_dim(jnp.zeros((3, f), F32), shard, me * N_TAPS, axis=1)
```

```python
import functools
import math

import jax
import jax.numpy as jnp
from jax import lax
from jax.experimental import pallas as pl
from jax.experimental.pallas import tpu as pltpu

F32 = jnp.float32
BF16 = jnp.bfloat16

N_DEV = 8
N_CHIP = 4
D_MODEL = 1024
CHUNK = 64
SUB = 16
SGU_BLOCK = 128
GROUPS = 8
HEAD = 128
HEADS = 8
CHUNKS_PER_STEP = 2
HEAD_UNROLL = 8
D_FF = 2816
LN_EPS = 1e-5
RMS_EPS = 1e-6
ALPHA = 2.0 ** 0.25
GELU_K = math.sqrt(2.0 / math.pi)
GELU_C = 0.044715
NEG = -1e30
ADAM_LR, ADAM_B1, ADAM_B2, ADAM_EPS, ADAM_WD, ADAM_STEP = 0.001, 0.9, 0.999, 1e-08, 0.01, 10
LANE = 128
SLOT_Q, SLOT_F, SLOT_I, SLOT_OG, SLOT_U, SLOT_V, SLOT_GA, SLOT_GB = range(8)


def _slot_of_group(k):
    return jnp.where(k < 2, k + 4, jnp.where(k < 6, k - 2, k))


MIB = 1024 * 1024
VMEM_V7X = 64 * MIB
VMEM_FLOOR = 32 * MIB
MM_TILES = (1024, 1408, 512, 256, 128)


def _params(sem, need=0):
    limit = min(max(need + need // 4, VMEM_FLOOR), VMEM_V7X - 4 * MIB)
    return pltpu.CompilerParams(dimension_semantics=sem, vmem_limit_bytes=limit)


def _pick(n, prefs):
    for t in prefs:
        if n % t == 0:
            return t
    return n


def _gelu(x):
    return 0.5 * x * (1.0 + jnp.tanh(GELU_K * (x + GELU_C * x * x * x)))


def _gelu_and_grad(x):
    x2 = x * x
    t = jnp.tanh(GELU_K * x * (1.0 + GELU_C * x2))
    half = 0.5 * (1.0 + t)
    return x * half, half + 0.5 * x * (1.0 - t * t) * GELU_K * (1.0 + 3.0 * GELU_C * x2)


def _silu_grad(x, s):
    return s * (1.0 + x * (1.0 - s))


def _dot(a, b):
    return jnp.dot(a.astype(BF16), b.astype(BF16), preferred_element_type=F32)


def _dot_nt(a, b):
    return lax.dot_general(a.astype(BF16), b.astype(BF16), (((1,), (1,)), ((), ())), preferred_element_type=F32)


def _dot_tn(a, b):
    return lax.dot_general(a.astype(BF16), b.astype(BF16), (((0,), (0,)), ((), ())), preferred_element_type=F32)


def _mean(x):
    return jnp.mean(x, axis=-1, keepdims=True)


def _sum0(x):
    return jnp.sum(x, axis=0, keepdims=True)


def _mm(a, b, *, out_dtype, name, trans_b=False, reduce_b=False, adds=(), ex=None):
    squeeze = b.ndim == 2
    a3 = a if a.ndim == 3 else a[None]
    b3 = b if b.ndim == 3 else b[None]
    ba, m, k = a3.shape
    bb = b3.shape[0]
    n = b3.shape[1] if trans_b else b3.shape[2]
    tm = _pick(m, MM_TILES)
    tn = _pick(n, MM_TILES)
    if reduce_b:
        bo, steps = 1, bb
        a_map = lambda o, i, j, r: (r if ba > 1 else 0, i, 0)
        b_map = (lambda o, i, j, r: (r, j, 0)) if trans_b else (lambda o, i, j, r: (r, 0, j))
    else:
        bo, steps = bb, 1
        a_map = lambda o, i, j, r: (o if ba > 1 else 0, i, 0)
        b_map = (lambda o, i, j, r: (o, j, 0)) if trans_b else (lambda o, i, j, r: (o, 0, j))
    o_map = lambda o, i, j, r: (o, i, j)
    add_arrays = [x if x.ndim == 3 else x[None] for x, _ in adds]
    add_scales = [s for _, s in adds]
    n_add = len(adds)
    dot = _dot_nt if trans_b else _dot

    def finish(acc, add_refs, o_ref):
        for ref, s in zip(add_refs, add_scales):
            acc = acc + s * ref[...].astype(F32)
        o_ref[...] = acc.astype(o_ref.dtype)

    grid = (bo, m // tm, n // tn, steps)

    def body(*refs):
        ins, (o_ref,), scratch, xrefs = _split_refs(refs, 2 + n_add, 1, 1 if reduce_b else 0, ex)
        a_ref, b_ref, add_refs = ins[0], ins[1], ins[2:]
        step = ((pl.program_id(0) * grid[1] + pl.program_id(1)) * grid[2] + pl.program_id(2)) * grid[3] + pl.program_id(3)
        if ex:
            @pl.when(step == 0)
            def _():
                ex.start(*xrefs)

        if reduce_b:
            acc, = scratch
            r = pl.program_id(3)

            @pl.when(r == 0)
            def _():
                acc[...] = jnp.zeros_like(acc)

            acc[...] += dot(a_ref[...], b_ref[...])

            @pl.when(r == steps - 1)
            def _():
                finish(acc[...], add_refs, o_ref)
        else:
            finish(dot(a_ref[...], b_ref[...]), add_refs, o_ref)

        if ex:
            @pl.when(step == math.prod(grid) - 1)
            def _():
                ex.finish(*xrefs)

    b_block = (None, tn, k) if trans_b else (None, k, tn)
    out_bytes = tm * tn * jnp.dtype(out_dtype).itemsize
    need = 2 * (tm * k * a3.dtype.itemsize + k * tn * b3.dtype.itemsize + out_bytes + n_add * tm * tn * 4)
    need += 2 * tm * tn * 4
    sem = ("arbitrary",) * 4 if ex else ("parallel", "parallel", "parallel", "arbitrary")
    res = pl.pallas_call(
        body,
        grid=grid,
        in_specs=[pl.BlockSpec((None, tm, k), a_map), pl.BlockSpec(b_block, b_map)]
        + [pl.BlockSpec((None, tm, tn), o_map) for _ in adds] + (ex.in_specs if ex else []),
        out_specs=[pl.BlockSpec((None, tm, tn), o_map)] + (ex.out_specs if ex else []),
        out_shape=[jax.ShapeDtypeStruct((bo, m, n), out_dtype)] + (ex.out_shape if ex else []),
        scratch_shapes=([pltpu.VMEM((tm, tn), F32)] if reduce_b else []) + (ex.scratch if ex else []),
        compiler_params=_params(sem, need),
        name=name,
    )(a3, b3, *add_arrays, *(ex.arrays if ex else []))
    out = res[0][0] if (reduce_b or squeeze) else res[0]
    return (out, res[1:]) if ex else out


def _mm_tn(a, b, *, out_dtype, name):
    squeeze = b.ndim == 2
    b3 = b if b.ndim == 3 else b[None]
    t, m = a.shape
    bb, _, n = b3.shape
    tm = _pick(m, MM_TILES)
    tn = _pick(n, MM_TILES)
    tt = _pick(t, (1024, 512, 256, 128))
    steps = t // tt
    need = 2 * (tt * tm * a.dtype.itemsize + tt * tn * b3.dtype.itemsize + tm * tn * jnp.dtype(out_dtype).itemsize)
    need += 2 * tm * tn * 4

    def body(a_ref, b_ref, o_ref, acc):
        r = pl.program_id(3)

        @pl.when(r == 0)
        def _():
            acc[...] = jnp.zeros_like(acc)

        acc[...] += _dot_tn(a_ref[...], b_ref[...])

        @pl.when(r == steps - 1)
        def _():
            o_ref[...] = acc[...].astype(o_ref.dtype)

    out = pl.pallas_call(
        body,
        grid=(bb, m // tm, n // tn, steps),
        in_specs=[pl.BlockSpec((tt, tm), lambda o, i, j, r: (r, i)),
                  pl.BlockSpec((None, tt, tn), lambda o, i, j, r: (o, r, j))],
        out_specs=pl.BlockSpec((None, tm, tn), lambda o, i, j, r: (o, i, j)),
        out_shape=jax.ShapeDtypeStruct((bb, m, n), out_dtype),
        scratch_shapes=[pltpu.VMEM((tm, tn), F32)],
        compiler_params=_params(("parallel", "parallel", "parallel", "arbitrary"), need),
        name=name,
    )(a, b3)
    return out[0] if squeeze else out


def _rowwise(fn, rows, consts, row_outs, acc_outs, *, tile, name):
    first = rows[0][0] if isinstance(rows[0], tuple) else rows[0]
    t = first.shape[-2]
    steps = t // tile
    arrays, in_specs = [], []
    for r in rows:
        if isinstance(r, tuple) and isinstance(r[1], pl.BlockSpec):
            arrays.append(r[0])
            in_specs.append(r[1])
        elif isinstance(r, tuple):
            arr, bidx = r
            arrays.append(arr)
            in_specs.append(pl.BlockSpec((None, tile, arr.shape[-1]), functools.partial(lambda i, b: (b, i, 0), b=bidx)))
        else:
            arrays.append(r)
            in_specs.append(pl.BlockSpec((tile, r.shape[-1]), lambda i: (i, 0)))
    for c in consts:
        arrays.append(c)
        in_specs.append(pl.BlockSpec(c.shape, lambda i: (0, 0)))
    n_in, n_row = len(arrays), len(row_outs)
    out_shape, out_specs = [], []
    for ro in row_outs:
        if ro[0] == "stack":
            _, cnt, blk, total, w, dt = ro
            out_shape.append(jax.ShapeDtypeStruct((total, t, w), dt))
            out_specs.append(pl.BlockSpec((cnt, tile, w), functools.partial(lambda i, b: (b, i, 0), b=blk)))
        else:
            w, dt = ro
            out_shape.append(jax.ShapeDtypeStruct((t, w), dt))
            out_specs.append(pl.BlockSpec((tile, w), lambda i: (i, 0)))
    out_shape += [jax.ShapeDtypeStruct(s, F32) for s in acc_outs]
    out_specs += [pl.BlockSpec(s, lambda i: (0, 0)) for s in acc_outs]
    blocks = [math.prod(d for d in sp.block_shape if d) * arr.dtype.itemsize for sp, arr in zip(in_specs, arrays)]
    blocks += [math.prod(d for d in sp.block_shape if d) * jnp.dtype(sh.dtype).itemsize
               for sp, sh in zip(out_specs, out_shape)]
    need = 2 * sum(blocks) + 6 * tile * max(a.shape[-1] for a in arrays) * 4

    def body(*refs):
        ins, outs = refs[:n_in], refs[n_in:]
        i = pl.program_id(0)
        res = fn(i, steps, *[r[...] for r in ins])
        res = res if isinstance(res, (tuple, list)) else (res,)
        for ref, val in zip(outs[:n_row], res[:n_row]):
            ref[...] = val.astype(ref.dtype)
        if acc_outs:
            @pl.when(i == 0)
            def _():
                for ref in outs[n_row:]:
                    ref[...] = jnp.zeros_like(ref)

            for ref, val in zip(outs[n_row:], res[n_row:]):
                ref[...] += val

    return pl.pallas_call(
        body,
        grid=(steps,),
        in_specs=in_specs,
        out_specs=out_specs,
        out_shape=out_shape,
        compiler_params=_params(("arbitrary",), need),
        name=name,
    )(*arrays)


def _ln_stats(z):
    mu = _mean(z)
    zc = z - mu
    rstd = lax.rsqrt(_mean(zc * zc) + LN_EPS)
    return zc * rstd, rstd


def _ln_bwd(dy, xhat, rstd, g):
    dxh = dy * g
    return rstd * (dxh - _mean(dxh) - xhat * _mean(dxh * xhat))


def _ride(ex, xrefs, step, steps):
    if not ex:
        return
    for at, act in ((0, ex.start), (steps - 2, ex.relay), (steps - 1, ex.finish)):
        @pl.when(step == at)
        def _(act=act):
            act(*xrefs)


def _sgu_fwd(h, wm, bs_t, g_v, b_v, ex=None):
    t = h.shape[1]
    steps = t // SGU_BLOCK

    def body(*refs):
        (u_ref, v_ref, wm_ref, bs_ref, g_ref, b_ref), (y_ref,), _, xrefs = _split_refs(refs, 6, 1, 0, ex)
        xhat, _ = _ln_stats(_gelu(v_ref[...]))
        vn = (xhat * g_ref[...] + b_ref[...]).astype(BF16)
        gu = _gelu(u_ref[...])
        for g in range(GROUPS):
            sl = slice(g * HEAD, (g + 1) * HEAD)
            mixed = _dot(wm_ref[g], vn[:, sl]) + bs_ref[:, g:g + 1]
            y_ref[:, sl] = (gu[:, sl] * mixed).astype(BF16)
        _ride(ex, xrefs, pl.program_id(0), steps)

    blk = lambda b: pl.BlockSpec((None, SGU_BLOCK, D_MODEL), functools.partial(lambda i, b: (b, i, 0), b=b))
    whole = lambda s: pl.BlockSpec(s, lambda i: (0,) * len(s))
    res = pl.pallas_call(
        body,
        grid=(steps,),
        in_specs=[blk(SLOT_U), blk(SLOT_V), whole(wm.shape), whole(bs_t.shape), whole(g_v.shape), whole(b_v.shape)]
        + (ex.in_specs if ex else []),
        out_specs=[pl.BlockSpec((SGU_BLOCK, D_MODEL), lambda i: (i, 0))] + (ex.out_specs if ex else []),
        out_shape=[jax.ShapeDtypeStruct((t, D_MODEL), BF16)] + (ex.out_shape if ex else []),
        scratch_shapes=ex.scratch if ex else [],
        compiler_params=_params(("arbitrary",)),
        name="sgu_fwd",
    )(h, h, wm, bs_t, g_v, b_v, *(ex.arrays if ex else []))
    return res[0], res[1:]


def _sgu_bwd(h, dy, dh, wm, wm_t, bs_t, g_v, b_v):
    t = h.shape[1]

    def body(u_ref, v_ref, dy_ref, dh_in, wm_ref, wmt_ref, bs_ref, g_ref, b_ref,
             duv_ref, dw_ref, dbs_ref, dg_ref, db_ref, dvn_ref):
        del dh_in
        du_ref, dv_ref = duv_ref.at[0], duv_ref.at[1]
        i = pl.program_id(0)

        @pl.when(i == 0)
        def _():
            dw_ref[...] = jnp.zeros_like(dw_ref)
            dbs_ref[...] = jnp.zeros_like(dbs_ref)
            dg_ref[...] = jnp.zeros_like(dg_ref)
            db_ref[...] = jnp.zeros_like(db_ref)

        u = u_ref[...]
        v = v_ref[...]
        gv, gvp = _gelu_and_grad(v)
        xhat, rstd = _ln_stats(gv)
        vn = (xhat * g_ref[...] + b_ref[...]).astype(BF16)
        gu, gup = _gelu_and_grad(u)
        lane = lax.broadcasted_iota(jnp.int32, (SGU_BLOCK, LANE), 1)
        dbs = jnp.zeros((SGU_BLOCK, LANE), F32)
        for g in range(GROUPS):
            sl = slice(g * HEAD, (g + 1) * HEAD)
            vn_g = vn[:, sl]
            mixed = _dot(wm_ref[g], vn_g) + bs_ref[:, g:g + 1]
            dy_g = dy_ref[:, sl]
            du_ref[:, sl] = (dy_g * mixed * gup[:, sl]).astype(BF16)
            dmix = dy_g * gu[:, sl]
            dmb = dmix.astype(BF16)
            dvn_ref[:, sl] = _dot(wmt_ref[g], dmb)
            dw_ref[g] += _dot_nt(dmb, vn_g)
            dbs = dbs + jnp.where(lane == g, jnp.sum(dmix, axis=1, keepdims=True), 0.0)
        dbs_ref[...] += dbs
        dvn = dvn_ref[...]
        dg_ref[...] += _sum0(dvn * xhat)
        db_ref[...] += _sum0(dvn)
        dv_ref[...] = (_ln_bwd(dvn, xhat, rstd, g_ref[...]) * gvp).astype(BF16)

    blk = lambda b: pl.BlockSpec((None, SGU_BLOCK, D_MODEL), functools.partial(lambda i, b: (b, i, 0), b=b))
    row = pl.BlockSpec((SGU_BLOCK, D_MODEL), lambda i: (i, 0))
    whole = lambda s: pl.BlockSpec(s, lambda i: (0,) * len(s))
    vec = (1, D_MODEL)
    return pl.pallas_call(
        body,
        grid=(t // SGU_BLOCK,),
        in_specs=[blk(SLOT_U), blk(SLOT_V), row, pl.BlockSpec(memory_space=pl.ANY),
                  whole(wm.shape), whole(wm_t.shape), whole(bs_t.shape), whole(vec), whole(vec)],
        out_specs=[pl.BlockSpec((2, SGU_BLOCK, D_MODEL), lambda i: (SLOT_U // 2, i, 0)),
                   whole(wm.shape), whole((SGU_BLOCK, LANE)), whole(vec), whole(vec)],
        out_shape=[jax.ShapeDtypeStruct(dh.shape, BF16),
                   jax.ShapeDtypeStruct(wm.shape, F32), jax.ShapeDtypeStruct((SGU_BLOCK, LANE), F32),
                   jax.ShapeDtypeStruct(vec, F32), jax.ShapeDtypeStruct(vec, F32)],
        scratch_shapes=[pltpu.VMEM((SGU_BLOCK, D_MODEL), F32)],
        input_output_aliases={3: 0},
        compiler_params=_params(("arbitrary",)),
        name="sgu_bwd",
    )(h, h, dy, dh, wm, wm_t, bs_t, g_v, b_v)


def _split3(x):
    hi = x.astype(BF16)
    r1 = x - hi.astype(F32)
    mid = r1.astype(BF16)
    lo = (r1 - mid.astype(F32)).astype(BF16)
    return hi, mid, lo


def _tri_matmul(tri, x):
    hi, mid, lo = _split3(x)
    dot = lambda p: jnp.dot(tri, p, preferred_element_type=F32)
    return dot(hi) + dot(mid) + dot(lo)


def _lower_bound(logits):
    l0, l1 = logits[0:1, :], logits[1:2, :]
    mx = jnp.maximum(l0, l1)
    e0, e1 = jnp.exp(l0 - mx), jnp.exp(l1 - mx)
    return e0 / (e0 + e1)


def _hgrn_gates(q_raw, f_raw, lb):
    q = q_raw * jax.nn.sigmoid(q_raw)
    sig = jax.nn.sigmoid(f_raw)
    f = lb + (1.0 - lb) * sig
    row = lax.broadcasted_iota(jnp.int32, (CHUNK, CHUNK), 0)
    col = lax.broadcasted_iota(jnp.int32, (CHUNK, CHUNK), 1)
    c = _tri_matmul((row >= col).astype(BF16), jnp.log(f))
    return q, sig, f, 1.0 - f, c


def _offdiag_terms(qh, kh, ch, tb):
    rows = slice(tb * SUB, (tb + 1) * SUB)
    r = ch[tb * SUB - 1:tb * SUB, :]
    eqh = jnp.exp(ch[rows] - r)
    ekh = jnp.exp(jnp.minimum(r - ch, 0.0))
    return rows, eqh, qh[rows] * eqh, ekh, kh * ekh


def _diag_decay(cb, s, trow):
    return jnp.exp(jnp.where(trow >= s, cb - cb[s:s + 1, :], NEG))


def _split_refs(refs, n_in, n_out, n_scratch, ex):
    nx = ex.n if ex else 0
    ins, refs = refs[:n_in], refs[n_in:]
    xsrc, refs = refs[:nx], refs[nx:]
    outs, refs = refs[:n_out], refs[n_out:]
    xout, refs = refs[:nx], refs[nx:]
    return ins, outs, refs[:n_scratch], (xsrc, xout, refs[n_scratch:])


def _hgrn_fwd(h, logits, g_norm, ex=None):
    t = h.shape[1]
    nc = t // CHUNK
    per_step = CHUNKS_PER_STEP if nc % CHUNKS_PER_STEP == 0 else 1
    steps = nc // per_step

    def body(*refs):
        ins, outs, scratch, xrefs = _split_refs(refs, 3, 3, 4, ex)
        hgrn_ref, lg_ref, gn_ref = ins
        y_ref, o_ref, sall_ref = outs
        st_ref, q_s, k_s, c_s = scratch

        @pl.when(pl.program_id(0) == 0)
        def _():
            st_ref[...] = jnp.zeros_like(st_ref)

        lb = _lower_bound(lg_ref[...])
        for sub in range(per_step):
            rows = pl.ds(sub * CHUNK, CHUNK)
            chunk(lb, sall_ref.at[sub], *(r.at[rows] for r in (hgrn_ref.at[SLOT_Q], hgrn_ref.at[SLOT_F], hgrn_ref.at[SLOT_I],
                                                               hgrn_ref.at[SLOT_OG], y_ref, o_ref)),
                  gn_ref, st_ref, q_s.at[sub], k_s.at[sub], c_s.at[sub])
        _ride(ex, xrefs, pl.program_id(0), steps)

    def chunk(lb, sall_ref, q_ref, f_ref, i_ref, og_ref, y_ref, o_ref, gn_ref, st_ref, q_s, k_s, c_s):
        q, _, _, k, c = _hgrn_gates(q_ref[...], f_ref[...], lb)
        q_s[...] = q
        k_s[...] = k
        c_s[...] = c
        col64 = lax.broadcasted_iota(jnp.int32, (SUB, CHUNK), 1)
        trow = lax.broadcasted_iota(jnp.int32, (SUB, HEAD), 0)

        def head(hd, carry):
            sl = pl.ds(pl.multiple_of(hd * HEAD, HEAD), HEAD)
            qh, kh, ch, ih = q_s[:, sl], k_s[:, sl], c_s[:, sl], i_ref[:, sl]
            st = st_ref[hd]
            sall_ref[hd] = st
            c_last = ch[CHUNK - 1:CHUNK, :]
            o = _dot_nt(qh * jnp.exp(ch), st)
            st_ref[hd] = st * jnp.exp(c_last) + _dot_tn(ih, kh * jnp.exp(c_last - ch))
            a_rows = [jnp.zeros((SUB, CHUNK), F32)]
            for tb in range(1, CHUNK // SUB):
                _, _, q_hat, _, k_hat = _offdiag_terms(qh, kh, ch, tb)
                a_rows.append(jnp.where(col64 < tb * SUB, _dot_nt(q_hat, k_hat), 0.0))
            o = o + _dot(jnp.concatenate(a_rows, axis=0), ih)
            o_rows = []
            for b in range(CHUNK // SUB):
                rows = slice(b * SUB, (b + 1) * SUB)
                qb, cb, kb, ib = qh[rows], ch[rows], kh[rows], ih[rows]
                ob = jnp.zeros((SUB, HEAD), F32)
                for s in range(SUB):
                    a = jnp.sum(qb * _diag_decay(cb, s, trow) * kb[s:s + 1, :], axis=1, keepdims=True)
                    ob = ob + a * ib[s:s + 1, :]
                o_rows.append(ob)
            o = o + jnp.concatenate(o_rows, axis=0)
            o_ref[:, sl] = o
            og = og_ref[:, sl]
            on = o * lax.rsqrt(_mean(o * o) + RMS_EPS)
            y_ref[:, sl] = (on * gn_ref[:, sl] * (og * jax.nn.sigmoid(og))).astype(BF16)
            return carry

        lax.fori_loop(0, HEADS, head, 0, unroll=HEAD_UNROLL)

    rows = per_step * CHUNK
    row = pl.BlockSpec((rows, D_MODEL), lambda n: (n, 0))
    whole = lambda s: pl.BlockSpec(s, lambda n: (0,) * len(s))
    res = pl.pallas_call(
        body,
        grid=(steps,),
        in_specs=[pl.BlockSpec((4, rows, D_MODEL), lambda n: (SLOT_Q // 4, n, 0)), whole(logits.shape), whole(g_norm.shape)]
        + (ex.in_specs if ex else []),
        out_specs=[row, row, pl.BlockSpec((per_step, HEADS, HEAD, HEAD), lambda n: (n, 0, 0, 0))]
        + (ex.out_specs if ex else []),
        out_shape=[jax.ShapeDtypeStruct((t, D_MODEL), BF16), jax.ShapeDtypeStruct((t, D_MODEL), F32),
                   jax.ShapeDtypeStruct((nc, HEADS, HEAD, HEAD), F32)] + (ex.out_shape if ex else []),
        scratch_shapes=[pltpu.VMEM((HEADS, HEAD, HEAD), F32)] + [pltpu.VMEM((per_step, CHUNK, D_MODEL), F32)] * 3
        + (ex.scratch if ex else []),
        compiler_params=_params(("arbitrary",)),
        name="hgrn_fwd",
    )(h, logits, g_norm, *(ex.arrays if ex else []))
    return res[0], res[1], res[2], res[3:]


def _hgrn_bwd(h, o_all, dy, states, dh, logits, g_norm, ex=None):
    t = h.shape[1]
    nc = t // CHUNK

    def body(*refs):
        ins, outs, scratch, xrefs = _split_refs(refs, 7, 3, 8, ex)
        hgrn_ref, o_ref, dy_ref, sall_ref, _, lg_ref, gn_ref = ins
        q_ref, f_ref, i_ref, og_ref = (hgrn_ref.at[s] for s in (SLOT_Q, SLOT_F, SLOT_I, SLOT_OG))
        dqfio_ref, dlg_ref, dgn_ref = outs
        dst_ref, dlb_ref, q_s, k_s, c_s, dq_s, dk_s, dc_s = scratch
        dq_ref, df_ref, di_ref, dog_ref = (dqfio_ref.at[s] for s in (SLOT_Q, SLOT_F, SLOT_I, SLOT_OG))
        n = pl.program_id(0)

        @pl.when(n == 0)
        def _():
            dst_ref[...] = jnp.zeros_like(dst_ref)
            dlb_ref[...] = jnp.zeros_like(dlb_ref)
            dgn_ref[...] = jnp.zeros_like(dgn_ref)
            if ex:
                ex.start(*xrefs)

        lb = _lower_bound(lg_ref[...])
        q_raw = q_ref[...]
        q, sig, f, k, c = _hgrn_gates(q_raw, f_ref[...], lb)
        q_s[...] = q
        k_s[...] = k
        c_s[...] = c
        col64 = lax.broadcasted_iota(jnp.int32, (SUB, CHUNK), 1)
        trow = lax.broadcasted_iota(jnp.int32, (SUB, HEAD), 0)
        row64 = lax.broadcasted_iota(jnp.int32, (CHUNK, HEAD), 0)

        def head(hd, carry):
            sl = pl.ds(pl.multiple_of(hd * HEAD, HEAD), HEAD)
            qh, kh, ch, ih = q_s[:, sl], k_s[:, sl], c_s[:, sl], i_ref[:, sl]
            st = sall_ref[hd]
            dst = dst_ref[hd]
            oh, dyh, og, gn = o_ref[:, sl], dy_ref[:, sl], og_ref[:, sl], gn_ref[:, sl]
            sg = jax.nn.sigmoid(og)
            sil = og * sg
            rms = lax.rsqrt(_mean(oh * oh) + RMS_EPS)
            on = oh * rms
            dog_ref[:, sl] = (dyh * on * gn * _silu_grad(og, sg)).astype(BF16)
            dgn_ref[:, sl] += _sum0(dyh * on * sil)
            don = dyh * gn * sil
            do = rms * (don - on * _mean(don * on))
            dob = do.astype(BF16)

            c_last = ch[CHUNK - 1:CHUNK, :]
            eq = jnp.exp(ch)
            q_til = qh * eq
            ekl = jnp.exp(c_last - ch)
            k_til = kh * ekl
            ecl = jnp.exp(c_last)
            dq_til = _dot(dob, st)
            dk_til = _dot(ih, dst)
            di = _dot_nt(k_til, dst)
            dc_last = _sum0(dk_til * k_til) + _sum0(dst * st) * ecl
            dst_ref[hd] = _dot_tn(dob, q_til) + dst * ecl
            dq = dq_til * eq
            dc = dq_til * q_til - dk_til * k_til
            dk = dk_til * ekl

            da_full = _dot_nt(dob, ih)
            a_rows = [jnp.zeros((SUB, CHUNK), F32)]
            dq_rows = [jnp.zeros((SUB, HEAD), F32)]
            dc_rows = [jnp.zeros((SUB, HEAD), F32)]
            for tb in range(1, CHUNK // SUB):
                rows, eqh, q_hat, ekh, k_hat = _offdiag_terms(qh, kh, ch, tb)
                keep = col64 < tb * SUB
                a_rows.append(jnp.where(keep, _dot_nt(q_hat, k_hat), 0.0))
                da = jnp.where(keep, da_full[rows], 0.0)
                dq_hat = _dot(da, k_hat)
                dk_hat = _dot_tn(da, q_hat)
                dq_rows.append(dq_hat * eqh)
                dc_rows.append(dq_hat * q_hat)
                dk = dk + dk_hat * ekh
                dc = dc - dk_hat * k_hat
            di = di + _dot_tn(jnp.concatenate(a_rows, axis=0), dob)

            dk_rows, di_rows = [], []
            for b in range(CHUNK // SUB):
                rows = slice(b * SUB, (b + 1) * SUB)
                qb, cb, kb, ib, dob_ = qh[rows], ch[rows], kh[rows], ih[rows], do[rows]
                dq_diag = jnp.zeros((SUB, HEAD), F32)
                dk_diag = jnp.zeros((SUB, HEAD), F32)
                di_diag = jnp.zeros((SUB, HEAD), F32)
                for s in range(SUB):
                    ks = kb[s:s + 1, :]
                    dec = _diag_decay(cb, s, trow)
                    a = jnp.sum(qb * dec * ks, axis=1, keepdims=True)
                    gk = jnp.sum(dob_ * ib[s:s + 1, :], axis=1, keepdims=True) * dec
                    dq_diag = dq_diag + gk * ks
                    dk_diag = dk_diag + jnp.where(trow == s, _sum0(gk * qb), 0.0)
                    di_diag = di_diag + jnp.where(trow == s, _sum0(a * dob_), 0.0)
                dq_rows[b] = dq_rows[b] + dq_diag
                dc_rows[b] = dc_rows[b] + qb * dq_diag - kb * dk_diag
                dk_rows.append(dk_diag)
                di_rows.append(di_diag)
            dq = dq + jnp.concatenate(dq_rows, axis=0)
            dk = dk + jnp.concatenate(dk_rows, axis=0)
            dc = dc + jnp.concatenate(dc_rows, axis=0) + jnp.where(row64 == CHUNK - 1, dc_last, 0.0)
            di_ref[:, sl] = (di + jnp.concatenate(di_rows, axis=0)).astype(BF16)
            dq_s[:, sl] = dq
            dk_s[:, sl] = dk
            dc_s[:, sl] = dc
            return carry

        lax.fori_loop(0, HEADS, head, 0, unroll=HEAD_UNROLL)

        row = lax.broadcasted_iota(jnp.int32, (CHUNK, CHUNK), 0)
        col = lax.broadcasted_iota(jnp.int32, (CHUNK, CHUNK), 1)
        dlf = _tri_matmul((row <= col).astype(BF16), dc_s[...])
        df = dlf / f - dk_s[...]
        dlb_ref[...] += _sum0(df * (1.0 - sig))
        df_ref[...] = (df * (1.0 - lb) * sig * (1.0 - sig)).astype(BF16)
        dq_ref[...] = (dq_s[...] * _silu_grad(q_raw, jax.nn.sigmoid(q_raw))).astype(BF16)

        @pl.when(n == nc - 1)
        def _():
            d0 = dlb_ref[...] * lb * (1.0 - lb)
            dlg_ref[0:1, :] = d0
            dlg_ref[1:2, :] = -d0
            if ex:
                ex.finish(*xrefs)

    rev = lambda n: nc - 1 - n
    slots = pl.BlockSpec((4, CHUNK, D_MODEL), lambda n: (SLOT_Q // 4, rev(n), 0))
    row = pl.BlockSpec((CHUNK, D_MODEL), lambda n: (rev(n), 0))
    whole = lambda s: pl.BlockSpec(s, lambda n: (0,) * len(s))
    vec = (1, D_MODEL)
    res = pl.pallas_call(
        body,
        grid=(nc,),
        in_specs=[slots, row, row, pl.BlockSpec((None, HEADS, HEAD, HEAD), lambda n: (rev(n), 0, 0, 0)),
                  pl.BlockSpec(memory_space=pl.ANY), whole(logits.shape), whole(vec)] + (ex.in_specs if ex else []),
        out_specs=[slots, whole((2, D_MODEL)), whole(vec)] + (ex.out_specs if ex else []),
        out_shape=[jax.ShapeDtypeStruct(dh.shape, BF16), jax.ShapeDtypeStruct((2, D_MODEL), F32),
                   jax.ShapeDtypeStruct(vec, F32)] + (ex.out_shape if ex else []),
        scratch_shapes=[pltpu.VMEM((HEADS, HEAD, HEAD), F32), pltpu.VMEM(vec, F32)]
        + [pltpu.VMEM((CHUNK, D_MODEL), F32)] * 6 + (ex.scratch if ex else []),
        input_output_aliases={4: 0},
        compiler_params=_params(("arbitrary",)),
        name="hgrn_bwd",
    )(h, o_all, dy, states, dh, logits, g_norm, *(ex.arrays if ex else []))
    return res[0], res[1], res[2], res[3:]


def _merge_fwd(i, n, ga, gb, za, zb):
    return jax.nn.sigmoid(ga) * za + jax.nn.sigmoid(gb) * zb


def _merge_bwd(i, n, ga, gb, za, zb, dm):
    sa, sb = jax.nn.sigmoid(ga), jax.nn.sigmoid(gb)
    dgates = jnp.stack([(dm * za * sa * (1.0 - sa)).astype(BF16), (dm * zb * sb * (1.0 - sb)).astype(BF16)])
    return dgates, dm * sa, dm * sb


def _ln1_fwd(i, n, x, r1, g, b):
    xhat, _ = _ln_stats(ALPHA * x + r1)
    x1 = xhat * g + b
    return x1, x1


def _ln1_bwd(i, n, x, r1, dx1, g):
    xhat, rstd = _ln_stats(ALPHA * x + r1)
    dz = _ln_bwd(dx1, xhat, rstd, g)
    return dz, dz, _sum0(dx1 * xhat), _sum0(dx1)


def _ln2_loss(i, n, x1, fo, pg, pp, tgt, g, b):
    sg = jax.nn.sigmoid(pg)
    xhat, rstd = _ln_stats(ALPHA * x1 + fo + sg * pp)
    diff = xhat * g + b - tgt
    loss = 0.5 * jnp.sum(_mean(diff * diff), axis=0, keepdims=True)
    dy = diff * (1.0 / D_MODEL)
    dz = _ln_bwd(dy, xhat, rstd, g)
    return (dz, dz, dz * pp * sg * (1.0 - sg), dz * sg,
            jnp.broadcast_to(loss, (8, LANE)), _sum0(dy * xhat), _sum0(dy))


def _shift_down(cur, halo, tile):
    row = lax.broadcasted_iota(jnp.int32, cur.shape, 0)
    m1 = jnp.where(row == 0, halo[7:8, :], pltpu.roll(cur, 1, 0))
    m2 = jnp.where(row == 0, halo[6:7, :], jnp.where(row == 1, halo[7:8, :], pltpu.roll(cur, 2, 0)))
    return m1, m2


def _shift_up(cur, halo, tile):
    row = lax.broadcasted_iota(jnp.int32, cur.shape, 0)
    p1 = jnp.where(row == tile - 1, halo[0:1, :], pltpu.roll(cur, tile - 1, 0))
    p2 = jnp.where(row == tile - 2, halo[0:1, :], jnp.where(row == tile - 1, halo[1:2, :], pltpu.roll(cur, tile - 2, 0)))
    return p1, p2


def _conv_pre(i, gate, halo, w, b, tile):
    halo = jnp.where(i == 0, 0.0, halo)
    m1, m2 = _shift_down(gate, halo, tile)
    return w[0:1, :] * m2 + w[1:2, :] * m1 + w[2:3, :] * gate + b, m1, m2


def _conv_fwd(tile, i, n, gate, halo, val, w, b):
    cg, _, _ = _conv_pre(i, gate, halo, w, b, tile)
    return _gelu(cg) * val


def _conv_bwd_a(tile, i, n, gate, halo, val, dhid, w, b):
    cg, m1, m2 = _conv_pre(i, gate, halo, w, b, tile)
    act, slope = _gelu_and_grad(cg)
    dcg = dhid * val * slope
    return dcg, dhid * act, _sum0(dcg * m2), _sum0(dcg * m1), _sum0(dcg * gate), _sum0(dcg)


def _conv_bwd_b(tile, i, n, dcg, halo, w):
    dcg = dcg.astype(F32)
    halo = jnp.where(i == n - 1, 0.0, halo.astype(F32))
    p1, p2 = _shift_up(dcg, halo, tile)
    return w[2:3, :] * dcg + w[1:2, :] * p1 + w[0:1, :] * p2


def _halo_spec(width, tile, t, nxt, rows=8):
    per = tile // rows
    last = t // rows - 1
    if nxt:
        return pl.BlockSpec((rows, width), lambda i: (jnp.minimum((i + 1) * per, last), 0))
    return pl.BlockSpec((rows, width), lambda i: (jnp.maximum(i * per - 1, 0), 0))


def _adamw(i, n, w, m, v, parts):
    g = parts[0].astype(F32)
    for j in range(1, parts.shape[0]):
        g = g + parts[j].astype(F32)
    m_new = ADAM_B1 * m + (1.0 - ADAM_B1) * g
    v_new = ADAM_B2 * v + (1.0 - ADAM_B2) * (g * g)
    m_hat = m_new / (1.0 - ADAM_B1 ** ADAM_STEP)
    v_hat = v_new / (1.0 - ADAM_B2 ** ADAM_STEP)
    delta = -ADAM_LR * (m_hat / (jnp.sqrt(v_hat) + ADAM_EPS) + ADAM_WD * w)
    return g, delta, m_new, v_new


def _adam_call(w, m, v, parts, name):
    r, c = w.shape
    tile = _pick(r, (256, 128)) if r > 256 else r
    spec = pl.BlockSpec((parts.shape[0], tile, c), lambda i: (0, i, 0))
    return _rowwise(_adamw, [w, m, v, (parts, spec)], [], [(c, F32)] * 4, [], tile=tile, name=name)


def _peer(k):
    x, y, c = lax.axis_index("x"), lax.axis_index("y"), lax.axis_index("c")
    px = x ^ ((k >> 2) & 1)
    py = y ^ ((k >> 1) & 1)
    pc = c ^ (k & 1)
    return (px, py, pc), 4 * px + 2 * py + pc


def _my_index():
    return 4 * lax.axis_index("x") + 2 * lax.axis_index("y") + lax.axis_index("c")


class _Exchange:
    KINDS = ("gather", "gather+relay", "scatter", "scatter+pairs")

    def __init__(self, entries):
        assert all(k in self.KINDS for _, k in entries), [k for _, k in entries]
        self.arrays = [a for a, _ in entries]
        self.scatter = [k.startswith("scatter") for _, k in entries]
        self.relayed = ["+relay" in k for _, k in entries]
        self.pairs = ["+pairs" in k for _, k in entries]
        self.n = len(entries)
        self.in_specs = [pl.BlockSpec(memory_space=pl.ANY)] * self.n
        self.out_specs = [pl.BlockSpec(memory_space=pl.ANY)] * self.n
        shapes = [tuple(a.shape[1:]) if sc else tuple(a.shape) for a, sc in zip(self.arrays, self.scatter)]
        counts = [N_CHIP if p else N_DEV for p in self.pairs]
        self.out_shape = [jax.ShapeDtypeStruct((n,) + s, a.dtype) for n, s, a in zip(counts, shapes, self.arrays)]
        per = N_DEV - 1
        self.scratch = [pltpu.SemaphoreType.DMA((self.n * per,)), pltpu.SemaphoreType.DMA((self.n * per,)),
                        pltpu.SemaphoreType.DMA((self.n,))]

    def _copies(self, srcs, outs, sems):
        send_sems, recv_sems, local_sems = sems
        x, y, c = lax.axis_index("x"), lax.axis_index("y"), lax.axis_index("c")
        me = _my_index()
        per = N_DEV - 1
        local, first, passed, relay_arrivals, arrivals = [], [], [], [], []
        for a in range(self.n):

            def copy(k, src, dst, dev, a=a):
                return pltpu.make_async_remote_copy(
                    src_ref=src, dst_ref=dst, send_sem=send_sems.at[a * per + k], recv_sem=recv_sems.at[a * per + k],
                    device_id=dev, device_id_type=pl.DeviceIdType.MESH)

            if self.pairs[a]:
                chip = 2 * x + y
                for k in range(N_CHIP):
                    to = chip ^ k
                    piece = srcs[a].at[_slot_of_group(2 * to + c) // 2]
                    if k == 0:
                        local.append(pltpu.make_async_copy(piece, outs[a].at[chip], local_sems.at[a]))
                    else:
                        dev = (to // 2, to % 2, c)
                        first.append(copy(k - 1, piece, outs[a].at[chip], dev))
                        arrivals.append(copy(k - 1, piece, outs[a].at[to], dev))
                continue
            mine = srcs[a].at[me] if self.scatter[a] else srcs[a]
            land = outs[a].at[me]
            local.append(pltpu.make_async_copy(mine, land, local_sems.at[a]))
            if self.relayed[a]:
                block = lambda px, py, pc, a=a: outs[a].at[4 * px + 2 * py + pc]
                chips = [(1 - x, y), (x, 1 - y), (1 - x, 1 - y)]
                first.append(copy(0, mine, land, (x, y, 1 - c)))
                arrivals.append(copy(0, mine, block(x, y, 1 - c), (x, y, 1 - c)))
                for j, (px, py) in enumerate(chips):
                    first.append(copy(1 + j, mine, land, (px, py, c)))
                    relay_arrivals.append(copy(1 + j, mine, block(px, py, c), (px, py, c)))
                    passed.append(copy(4 + j, block(px, py, c), block(px, py, c), (x, y, 1 - c)))
                    arrivals.append(copy(4 + j, mine, block(px, py, 1 - c), (x, y, 1 - c)))
                continue
            for k in range(1, N_DEV):
                dev, idx = _peer(k)
                if self.scatter[a]:
                    first.append(copy(k - 1, srcs[a].at[idx], land, dev))
                else:
                    first.append(copy(k - 1, mine, land, dev))
                arrivals.append(copy(k - 1, mine, outs[a].at[idx], dev))
        return local, first, passed, relay_arrivals, arrivals

    def start(self, srcs, outs, sems):
        local, first, _, _, _ = self._copies(srcs, outs, sems)
        for cp in local + first:
            cp.start()

    def relay(self, srcs, outs, sems):
        _, _, passed, relay_arrivals, _ = self._copies(srcs, outs, sems)
        for landed, onward in zip(relay_arrivals, passed):
            landed.wait_recv()
            onward.start()

    def finish(self, srcs, outs, sems):
        local, first, passed, _, arrivals = self._copies(srcs, outs, sems)
        for cp in arrivals:
            cp.wait_recv()
        for cp in first + passed:
            cp.wait_send()
        for cp in local:
            cp.wait()


def _gather_project(x_b, shard):
    t, d = x_b.shape
    tm = _pick(t, MM_TILES)
    nrow = t // tm
    per = N_DEV - 1

    def parties():
        x, y, c = lax.axis_index("x"), lax.axis_index("y"), lax.axis_index("c")
        chips = [(1 - x, y), (x, 1 - y), (1 - x, 1 - y)]
        return (x, y, c), (x, y, 1 - c), [(px, py, c) for px, py in chips], [(px, py, 1 - c) for px, py in chips]

    slot = lambda dev: _slot_of_group(4 * dev[0] + 2 * dev[1] + dev[2])
    ici_step, passed_step = (4, 5, 2), (6, 7, 3)
    me, sibling, over_ici, passed_on = parties()
    by_step = {0: me, 1: sibling, **dict(zip(ici_step, over_ici)), **dict(zip(passed_step, passed_on))}
    order = jnp.stack([slot(by_step[j]) for j in range(N_DEV)]).astype(jnp.int32)

    def body(order_ref, x_ref, shard_ref, h_ref, wall_ref, wbuf, fetch_sem, send_sems, recv_sems, local_sem):
        del order_ref
        me, sibling, over_ici, passed_on = parties()
        j, i = pl.program_id(0), pl.program_id(1)
        land = lambda dev: wall_ref.at[slot(dev)]

        def copy(k, src, block, to):
            return pltpu.make_async_remote_copy(src_ref=src, dst_ref=land(block), send_sem=send_sems.at[k],
                                                recv_sem=recv_sems.at[k], device_id=to, device_id_type=pl.DeviceIdType.MESH)

        def fetch(src):
            cp = pltpu.make_async_copy(src, wbuf, fetch_sem)
            cp.start()
            cp.wait()

        keep = pltpu.make_async_copy(shard_ref, land(me), local_sem)
        first = [copy(0, shard_ref, me, sibling)] + [copy(1 + n, shard_ref, me, dev) for n, dev in enumerate(over_ici)]
        onward = [copy(4 + n, land(dev), dev, sibling) for n, dev in enumerate(over_ici)]

        @pl.when(jnp.logical_and(i == 0, j == 0))
        def _():
            for cp in reversed(first):
                cp.start()
            keep.start()
            fetch(shard_ref)

        @pl.when(jnp.logical_and(i == 0, j == 1))
        def _():
            copy(0, shard_ref, sibling, me).wait_recv()
            fetch(land(sibling))

        for n, dev in enumerate(over_ici):
            @pl.when(jnp.logical_and(i == 0, j == ici_step[n]))
            def _(n=n, dev=dev):
                copy(1 + n, shard_ref, dev, me).wait_recv()
                onward[n].start()
                fetch(land(dev))

        for n, dev in enumerate(passed_on):
            @pl.when(jnp.logical_and(i == 0, j == passed_step[n]))
            def _(n=n, dev=dev):
                copy(4 + n, shard_ref, dev, me).wait_recv()
                fetch(land(dev))

        h_ref[...] = _dot(x_ref[...], wbuf[...])

        @pl.when(jnp.logical_and(i == nrow - 1, j == N_DEV - 1))
        def _():
            for cp in first + onward:
                cp.wait_send()
            keep.wait()

    need = 2 * (tm * d * 2 + tm * d * 4) + d * d * 2 + tm * d * 4
    h, w_all = pl.pallas_call(
        body,
        grid_spec=pltpu.PrefetchScalarGridSpec(
            num_scalar_prefetch=1,
            grid=(N_DEV, nrow),
            in_specs=[pl.BlockSpec((tm, d), lambda j, i, order: (i, 0)), pl.BlockSpec(memory_space=pl.ANY)],
            out_specs=[pl.BlockSpec((None, tm, d), lambda j, i, order: (order[j], i, 0)), pl.BlockSpec(memory_space=pl.ANY)],
            scratch_shapes=[pltpu.VMEM((d, d), BF16), pltpu.SemaphoreType.DMA, pltpu.SemaphoreType.DMA((per,)),
                            pltpu.SemaphoreType.DMA((per,)), pltpu.SemaphoreType.DMA],
        ),
        out_shape=[jax.ShapeDtypeStruct((N_DEV, t, d), F32), jax.ShapeDtypeStruct((N_DEV, d, d), BF16)],
        compiler_params=_params(("arbitrary", "arbitrary"), need),
        name="gather_project",
    )(order, x_b, shard)
    return h, w_all


def _pair_sums(g):
    n, r, cols = g.shape
    half = n // 2

    def swap(g_ref, got_ref, send_sems, recv_sems):
        x, y, c = lax.axis_index("x"), lax.axis_index("y"), lax.axis_index("c")
        copies = [pltpu.make_async_remote_copy(
            src_ref=g_ref.at[2 * j + 1 - c], dst_ref=got_ref.at[j], send_sem=send_sems.at[j], recv_sem=recv_sems.at[j],
            device_id=(x, y, 1 - c), device_id_type=pl.DeviceIdType.MESH) for j in range(half)]
        for cp in copies:
            cp.start()
        for cp in copies:
            cp.wait()

    got = pl.pallas_call(
        swap,
        in_specs=[pl.BlockSpec(memory_space=pl.ANY)],
        out_specs=pl.BlockSpec(memory_space=pl.ANY),
        out_shape=jax.ShapeDtypeStruct((half, r, cols), g.dtype),
        scratch_shapes=[pltpu.SemaphoreType.DMA((half,))] * 2,
        name="pair_swap",
    )(g)

    def add(mine_ref, got_ref, out_ref):
        mine = jnp.where(lax.axis_index("c") == 0, mine_ref[0].astype(F32), mine_ref[1].astype(F32))
        out_ref[...] = (mine + got_ref[...].astype(F32)).astype(out_ref.dtype)

    tile = _pick(r, (512, 256, 128))
    return pl.pallas_call(
        add,
        grid=(half, r // tile),
        in_specs=[pl.BlockSpec((None, 2, tile, cols), lambda j, i: (j, 0, i, 0)),
                  pl.BlockSpec((None, tile, cols), lambda j, i: (j, i, 0))],
        out_specs=pl.BlockSpec((None, tile, cols), lambda j, i: (j, i, 0)),
        out_shape=jax.ShapeDtypeStruct((half, r, cols), g.dtype),
        compiler_params=_params(("parallel", "parallel")),
        name="pair_add",
    )(g.reshape(half, 2, r, cols), got)


def _local_step(x, p, tgt, small, comm):
    t = x.shape[0]
    tile = _pick(t, (256, 128))
    tall = _pick(t, (512, 256, 128))
    d = D_MODEL
    act_b, act_f = (d, BF16), (d, F32)
    x_b, p_b = x.astype(BF16), p.astype(BF16)

    chunk_id = jnp.arange(SGU_BLOCK) // CHUNK
    mask = chunk_id[:, None] >= chunk_id[None, :]
    wm = jnp.where(mask[None], small["sgu_w_s"], 0.0)
    wm_b = wm.astype(BF16)
    wm_t = jnp.swapaxes(wm, 1, 2).astype(BF16)
    bs_t = small["sgu_b_s"].T

    h, w_in = comm.project_in(x_b)
    y_a, got_a = _sgu_fwd(h, wm_b, bs_t, small["sgu_norm_g"], small["sgu_norm_b"], ex=comm.weights_exchange(0))
    y_b, o_all, states, got_b = _hgrn_fwd(h, small["lb_logits"], small["hgrn_norm_g"], ex=comm.weights_exchange(1))
    wts, conv_w = comm.weights(list(got_a) + list(got_b))
    z_a = _mm(y_a, wts["w_a"], out_dtype=F32, name="mm_za")
    z_b = _mm(y_b, wts["w_b"], out_dtype=F32, name="mm_zb")
    gates = [(h, SLOT_GA), (h, SLOT_GB)]
    merged, = _rowwise(_merge_fwd, gates + [z_a, z_b], [], [act_b], [], tile=tall, name="merge_fwd")
    r1 = _mm(merged, wts["w_o"], out_dtype=F32, name="mm_r1")
    x1, x1_b = _rowwise(_ln1_fwd, [x, r1], [small["ln1_g"], small["ln1_b"]], [act_f, act_b], [], tile=tall, name="ln1_fwd")
    gate = _mm(x1_b, wts["w_g"], out_dtype=F32, name="mm_gate")
    val = _mm(x1_b, wts["w_v"], out_dtype=F32, name="mm_val")
    pg = _mm(x1_b, wts["w_pg"], out_dtype=F32, name="mm_pg")
    pp = _mm(p_b, wts["w_pp"], out_dtype=F32, name="mm_pp")
    hid, = _rowwise(functools.partial(_conv_fwd, tile), [gate, (gate, _halo_spec(D_FF, tile, t, False)), val],
                    [conv_w, small["conv_b"]], [(D_FF, BF16)], [], tile=tile, name="conv_fwd")
    fo = _mm(hid, wts["w_down"], out_dtype=F32, name="mm_down")
    dz2, dz2_b, dpg, dpp, loss, dg2, db2 = _rowwise(
        _ln2_loss, [x1, fo, pg, pp, tgt], [small["ln2_g"], small["ln2_b"]],
        [act_f, act_b, act_b, act_b], [(8, LANE), (1, d), (1, d)], tile=tall, name="ln2_loss")

    dhid = _mm(dz2_b, wts["w_down"], out_dtype=BF16, name="mm_dhid", trans_b=True)
    g_down = _mm_tn(hid, dz2_b, out_dtype=BF16, name="mm_g_down")
    dcg, dval, dcw0, dcw1, dcw2, dcb = _rowwise(
        functools.partial(_conv_bwd_a, tile), [gate, (gate, _halo_spec(D_FF, tile, t, False)), val, dhid],
        [conv_w, small["conv_b"]], [(D_FF, BF16), (D_FF, BF16)], [(1, D_FF)] * 4, tile=tile, name="conv_bwd_a")
    dgate, = _rowwise(functools.partial(_conv_bwd_b, tile), [dcg, (dcg, _halo_spec(D_FF, tile, t, True, rows=16))],
                      [conv_w], [(D_FF, BF16)], [], tile=tile, name="conv_bwd_b")
    g_g = _mm_tn(x1_b, dgate, out_dtype=BF16, name="mm_g_gate")
    g_v = _mm_tn(x1_b, dval, out_dtype=BF16, name="mm_g_val")
    g_pg = _mm_tn(x1_b, dpg, out_dtype=BF16, name="mm_g_pg")
    g_pp = _mm_tn(p_b, dpp, out_dtype=BF16, name="mm_g_pp")
    dx1 = _mm(dgate, wts["w_g"], out_dtype=F32, name="mm_dx1_gate", trans_b=True, adds=[(dz2, ALPHA)])
    dx1 = _mm(dval, wts["w_v"], out_dtype=F32, name="mm_dx1_val", trans_b=True, adds=[(dx1, 1.0)])
    dx1 = _mm(dpg, wts["w_pg"], out_dtype=F32, name="mm_dx1_pg", trans_b=True, adds=[(dx1, 1.0)])
    dz1, dz1_b, dg1, db1 = _rowwise(_ln1_bwd, [x, r1, dx1], [small["ln1_g"]], [act_f, act_b], [(1, d), (1, d)],
                                    tile=tall, name="ln1_bwd")
    g_o = _mm_tn(merged, dz1_b, out_dtype=BF16, name="mm_g_o")
    dm = _mm(dz1_b, wts["w_o"], out_dtype=BF16, name="mm_dm", trans_b=True)
    dh, dza, dzb = _rowwise(_merge_bwd, gates + [z_a, z_b, dm], [],
                            [("stack", 2, SLOT_GA // 2, 8, d, BF16), act_b, act_b], [], tile=tall, name="merge_bwd")
    g_a = _mm_tn(y_a, dza, out_dtype=BF16, name="mm_g_a")
    g_b = _mm_tn(y_b, dzb, out_dtype=BF16, name="mm_g_b")
    dy_a = _mm(dza, wts["w_a"], out_dtype=BF16, name="mm_dya", trans_b=True)
    dy_b = _mm(dzb, wts["w_b"], out_dtype=F32, name="mm_dyb", trans_b=True)
    dh, dws, dbs, dgv_n, dbv_n = _sgu_bwd(h, dy_a, dh, wm_b, wm_t, bs_t, small["sgu_norm_g"], small["sgu_norm_b"])
    big = dict(w_a=g_a, w_b=g_b, w_o=g_o, w_g=g_g, w_v=g_v, w_down=g_down, w_pp=g_pp, w_pg=g_pg)
    sm = dict(sgu_w_s=jnp.where(mask[None], dws, 0.0), sgu_b_s=dbs[:, :GROUPS].T, sgu_norm_g=dgv_n, sgu_norm_b=dbv_n,
              ln1_g=dg1, ln1_b=db1, conv_w=jnp.concatenate([dcw0, dcw1, dcw2], axis=0), conv_b=dcb, ln2_g=dg2, ln2_b=db2,
              loss=loss)
    dh, dlogits, dgn, got = _hgrn_bwd(h, o_all, dy_b, states, dh, small["lb_logits"], small["hgrn_norm_g"],
                                      ex=comm.grads_exchange(big, sm))
    comm.grads_done(got)
    g_in = _mm_tn(x_b, dh, out_dtype=BF16, name="mm_g_in")
    ex = comm.last_exchange(g_in, dict(lb_logits=dlogits, hgrn_norm_g=dgn))
    res = _mm(dh, w_in, out_dtype=F32, name="mm_dx", trans_b=True, reduce_b=True, adds=[(dz1, ALPHA)], ex=ex)
    grad_x, got = res if ex else (res, ())
    comm.last_done(got)
    return grad_x


_SMALL_EARLY = ["sgu_w_s", "sgu_b_s", "sgu_norm_g", "sgu_norm_b", "ln1_g", "ln1_b", "ffn_conv_b", "ln2_g", "ln2_b"]
_SMALL_LATE = ["hgrn_lb_logits", "hgrn_norm_g"]
N_TAPS = D_FF // N_DEV
UP_COLS = 2 * D_FF // N_DEV


class _StepExchanges:
    def __init__(self, w_in_shard, shards):
        self.w_in_shard = w_in_shard
        self.shards = shards

    def project_in(self, x_b):
        return _gather_project(x_b, self.w_in_shard)

    def weights_exchange(self, part):
        return _Exchange([(s, "gather+relay") for s in (self.shards[:2] if part == 0 else self.shards[2:])])

    def weights(self, got):
        d, f = D_MODEL, D_FF
        w_br_g, w_o_g, w_up_g, w_down_g, w_pp_g, w_pg_g, conv_g = got
        w_br = w_br_g.transpose(1, 0, 2, 3).reshape(2, d, d)
        w_up = w_up_g.transpose(1, 0, 2).reshape(d, 2, f).transpose(1, 0, 2)
        wts = dict(w_a=w_br[0], w_b=w_br[1], w_o=w_o_g.reshape(d, d), w_g=w_up[0], w_v=w_up[1],
                   w_down=w_down_g.reshape(f, d), w_pp=w_pp_g.transpose(1, 0, 2).reshape(256, d), w_pg=w_pg_g.reshape(d, d))
        return wts, conv_g.transpose(1, 0, 2).reshape(3, f)

    def grads_exchange(self, big, sm):
        d = D_MODEL
        parts = [jnp.stack([big["w_a"], big["w_b"]]).reshape(2, N_DEV, 128, d).transpose(1, 0, 2, 3),
                 big["w_o"].reshape(N_DEV, 128, d),
                 jnp.concatenate([big["w_g"], big["w_v"]], axis=1).reshape(d, N_DEV, UP_COLS).transpose(1, 0, 2),
                 big["w_down"].reshape(N_DEV, N_TAPS, d),
                 big["w_pp"].reshape(256, N_DEV, 128).transpose(1, 0, 2),
                 big["w_pg"].reshape(N_DEV, 128, d)]
        packed, self.rows_early = _pack([sm[k] for k in ("sgu_w_s", "sgu_b_s", "sgu_norm_g", "sgu_norm_b", "ln1_g", "ln1_b",
                                                         "conv_b", "ln2_g", "ln2_b", "conv_w", "loss")])
        return _Exchange([(a, "scatter") for a in parts] + [(packed, "gather")])

    def grads_done(self, got):
        self.recv, self.small_early = got[:6], got[6]

    def last_exchange(self, g_in, sm):
        packed, self.rows_late = _pack([sm["lb_logits"], sm["hgrn_norm_g"]])
        return _Exchange([(_pair_sums(g_in), "scatter+pairs"), (packed, "gather")])

    def last_done(self, got):
        self.recv_in, self.small_late = got


def _rows128(a):
    flat = a.reshape(-1)
    rows = -(-flat.shape[0] // (8 * LANE)) * 8
    return jnp.pad(flat, (0, rows * LANE - flat.shape[0])).reshape(rows, LANE)


def _pack(parts):
    blocks = [_rows128(a) for a in parts]
    return jnp.concatenate(blocks, axis=0), [b.shape[0] for b in blocks]


def _unpack(packed, shapes, rows):
    out, r0 = [], 0
    for shp, r in zip(shapes, rows):
        n = math.prod(shp)
        out.append(packed[r0:r0 + r].reshape(-1)[:n].reshape(shp))
        r0 += r
    return out


def kernel(x, p, w_in, sgu_w_s, sgu_b_s, sgu_norm_g, sgu_norm_b, hgrn_lb_logits, hgrn_norm_g, w_branch, w_out, ln1_g, ln1_b, ffn_w_up, ffn_conv_w, ffn_conv_b, ffn_w_down, ln2_g, ln2_b, ple_w_proj, ple_w_gate, loss_target, m_w_in, m_sgu_w_s, m_sgu_b_s, m_sgu_norm_g, m_sgu_norm_b, m_hgrn_lb_logits, m_hgrn_norm_g, m_w_branch, m_w_out, m_ln1_g, m_ln1_b, m_ffn_w_up, m_ffn_conv_w, m_ffn_conv_b, m_ffn_w_down, m_ln2_g, m_ln2_b, m_ple_w_proj, m_ple_w_gate, v_w_in, v_sgu_w_s, v_sgu_b_s, v_sgu_norm_g, v_sgu_norm_b, v_hgrn_lb_logits, v_hgrn_norm_g, v_w_branch, v_w_out, v_ln1_g, v_ln1_b, v_ffn_w_up, v_ffn_conv_w, v_ffn_conv_b, v_ffn_w_down, v_ln2_g, v_ln2_b, v_ple_w_proj, v_ple_w_gate):
    weights = dict(w_in=w_in, sgu_w_s=sgu_w_s, sgu_b_s=sgu_b_s, sgu_norm_g=sgu_norm_g, sgu_norm_b=sgu_norm_b,
                   hgrn_lb_logits=hgrn_lb_logits, hgrn_norm_g=hgrn_norm_g, w_branch=w_branch, w_out=w_out,
                   ln1_g=ln1_g, ln1_b=ln1_b, ffn_w_up=ffn_w_up, ffn_conv_w=ffn_conv_w, ffn_conv_b=ffn_conv_b,
                   ffn_w_down=ffn_w_down, ln2_g=ln2_g, ln2_b=ln2_b, ple_w_proj=ple_w_proj, ple_w_gate=ple_w_gate)
    mom_m = dict(w_in=m_w_in, sgu_w_s=m_sgu_w_s, sgu_b_s=m_sgu_b_s, sgu_norm_g=m_sgu_norm_g, sgu_norm_b=m_sgu_norm_b,
                 hgrn_lb_logits=m_hgrn_lb_logits, hgrn_norm_g=m_hgrn_norm_g, w_branch=m_w_branch, w_out=m_w_out,
                 ln1_g=m_ln1_g, ln1_b=m_ln1_b, ffn_w_up=m_ffn_w_up, ffn_conv_w=m_ffn_conv_w, ffn_conv_b=m_ffn_conv_b,
                 ffn_w_down=m_ffn_w_down, ln2_g=m_ln2_g, ln2_b=m_ln2_b, ple_w_proj=m_ple_w_proj, ple_w_gate=m_ple_w_gate)
    mom_v = dict(w_in=v_w_in, sgu_w_s=v_sgu_w_s, sgu_b_s=v_sgu_b_s, sgu_norm_g=v_sgu_norm_g, sgu_norm_b=v_sgu_norm_b,
                 hgrn_lb_logits=v_hgrn_lb_logits, hgrn_norm_g=v_hgrn_norm_g, w_branch=v_w_branch, w_out=v_w_out,
                 ln1_g=v_ln1_g, ln1_b=v_ln1_b, ffn_w_up=v_ffn_w_up, ffn_conv_w=v_ffn_conv_w, ffn_conv_b=v_ffn_conv_b,
                 ffn_w_down=v_ffn_w_down, ln2_g=v_ln2_g, ln2_b=v_ln2_b, ple_w_proj=v_ple_w_proj, ple_w_gate=v_ple_w_gate)
    d, f = D_MODEL, D_FF
    me = _my_index()

    comm = _StepExchanges(w_in[0].astype(BF16),
                          [w_branch[0].astype(BF16), w_out[0].astype(BF16), ffn_w_up[0].astype(BF16),
                           ffn_w_down[0].astype(BF16), ple_w_proj[0].astype(BF16), ple_w_gate[0].astype(BF16), ffn_conv_w[0]])
    small = dict(sgu_w_s=sgu_w_s[0], sgu_b_s=sgu_b_s[0], sgu_norm_g=sgu_norm_g, sgu_norm_b=sgu_norm_b,
                 lb_logits=hgrn_lb_logits, hgrn_norm_g=hgrn_norm_g, ln1_g=ln1_g, ln1_b=ln1_b, ln2_g=ln2_g, ln2_b=ln2_b,
                 conv_b=ffn_conv_b)
    grad_x = _local_step(x[0], p[0, 0], loss_target[0], small, comm)

    out = {}

    def adam(name, parts8, shape2d):
        w2, m2, v2 = (a.reshape(shape2d) for a in (weights[name], mom_m[name], mom_v[name]))
        res = _adam_call(w2, m2, v2, parts8.reshape(parts8.shape[:1] + shape2d), "adam_" + name)
        out[name] = tuple(r.reshape(weights[name].shape) for r in res)

    adam("w_in", comm.recv_in, (d, d))
    adam("w_branch", comm.recv[0], (256, d))
    adam("w_out", comm.recv[1], (128, d))
    adam("ffn_w_up", comm.recv[2], (d, UP_COLS))
    adam("ffn_w_down", comm.recv[3], (N_TAPS, d))
    adam("ple_w_proj", comm.recv[4], (256, 128))
    adam("ple_w_gate", comm.recv[5], (128, d))

    def adam_small(names, extra_w, extra_m, extra_v, extra_shapes, parts8, rows, label):
        pk = lambda src, extra: _pack([src[n] for n in names] + extra)[0]
        res = _adam_call(pk(weights, extra_w), pk(mom_m, extra_m), pk(mom_v, extra_v), parts8, label)
        shapes = [weights[n].shape for n in names] + extra_shapes
        unpacked = [_unpack(r, shapes, rows) for r in res]
        for j, n in enumerate(names):
            out[n] = tuple(u[j] for u in unpacked)
        return [[u[len(names) + j] for u in unpacked] for j in range(len(extra_shapes))]

    blank = jnp.zeros((8, LANE), F32)
    taps, loss_rows = adam_small(
        _SMALL_EARLY, [_place_taps(ffn_conv_w[0], me, f), blank], [_place_taps(m_ffn_conv_w[0], me, f), blank],
        [_place_taps(v_ffn_conv_w[0], me, f), blank + 1.0], [(3, f), (8, LANE)], comm.small_early, comm.rows_early,
        "adam_small_early")
    adam_small(_SMALL_LATE, [], [], [], [], comm.small_late, comm.rows_late, "adam_small_late")
    out["ffn_conv_w"] = tuple(lax.dynamic_slice_in_dim(u, me * N_TAPS, N_TAPS, axis=1)[None] for u in taps)
    loss = loss_rows[0][0, 0]

    order = ["w_in", "sgu_w_s", "sgu_b_s", "sgu_norm_g", "sgu_norm_b", "hgrn_lb_logits", "hgrn_norm_g", "w_branch", "w_out",
             "ln1_g", "ln1_b", "ffn_w_up", "ffn_conv_w", "ffn_conv_b", "ffn_w_down", "ln2_g", "ln2_b", "ple_w_proj", "ple_w_gate"]
    return (loss, grad_x[None], *[out[n][0] for n in order], *[out[n][1] for n in order],
            *[out[n][2] for n in order], *[out[n][3] for n in order])


def _place_taps(shard, me, f):
    return lax.dynamic_update_slice_in_dim(jnp.zeros((3, f), F32), shard, me * N_TAPS, axis=1)
```

```python
import functools
import math

import jax
import jax.numpy as jnp
from jax import lax
from jax.experimental import pallas as pl
from jax.experimental.pallas import tpu as pltpu

F32 = jnp.float32
BF16 = jnp.bfloat16

N_DEV = 8
N_CHIP = 4
D_MODEL = 1024
CHUNK = 64
SUB = 16
SGU_BLOCK = 128
GROUPS = 8
HEAD = 128
HEADS = 8
CHUNKS_PER_STEP = 2
HEAD_UNROLL = 8
D_FF = 2816
LN_EPS = 1e-5
RMS_EPS = 1e-6
ALPHA = 2.0 ** 0.25
GELU_K = math.sqrt(2.0 / math.pi)
GELU_C = 0.044715
NEG = -1e30
ADAM_LR, ADAM_B1, ADAM_B2, ADAM_EPS, ADAM_WD, ADAM_STEP = 0.001, 0.9, 0.999, 1e-08, 0.01, 10
LANE = 128
SLOT_Q, SLOT_F, SLOT_I, SLOT_OG, SLOT_U, SLOT_V, SLOT_GA, SLOT_GB = range(8)


def _slot_of_group(k):
    return jnp.where(k < 2, k + 4, jnp.where(k < 6, k - 2, k))


MIB = 1024 * 1024
VMEM_V7X = 64 * MIB
VMEM_FLOOR = 32 * MIB
MM_TILES = (1024, 1408, 512, 256, 128)


def _params(sem, need=0):
    limit = min(max(need + need // 4, VMEM_FLOOR), VMEM_V7X - 4 * MIB)
    return pltpu.CompilerParams(dimension_semantics=sem, vmem_limit_bytes=limit)


def _pick(n, prefs):
    for t in prefs:
        if n % t == 0:
            return t
    return n


def _gelu(x):
    return 0.5 * x * (1.0 + jnp.tanh(GELU_K * (x + GELU_C * x * x * x)))


def _gelu_and_grad(x):
    x2 = x * x
    t = jnp.tanh(GELU_K * x * (1.0 + GELU_C * x2))
    half = 0.5 * (1.0 + t)
    return x * half, half + 0.5 * x * (1.0 - t * t) * GELU_K * (1.0 + 3.0 * GELU_C * x2)


def _silu_grad(x, s):
    return s * (1.0 + x * (1.0 - s))


def _dot(a, b):
    return jnp.dot(a.astype(BF16), b.astype(BF16), preferred_element_type=F32)


def _dot_nt(a, b):
    return lax.dot_general(a.astype(BF16), b.astype(BF16), (((1,), (1,)), ((), ())), preferred_element_type=F32)


def _dot_tn(a, b):
    return lax.dot_general(a.astype(BF16), b.astype(BF16), (((0,), (0,)), ((), ())), preferred_element_type=F32)


def _mean(x):
    return jnp.mean(x, axis=-1, keepdims=True)


def _sum0(x):
    return jnp.sum(x, axis=0, keepdims=True)


def _mm(a, b, *, out_dtype, name, trans_b=False, reduce_b=False, adds=(), ex=None):
    squeeze = b.ndim == 2
    a3 = a if a.ndim == 3 else a[None]
    b3 = b if b.ndim == 3 else b[None]
    ba, m, k = a3.shape
    bb = b3.shape[0]
    n = b3.shape[1] if trans_b else b3.shape[2]
    tm = _pick(m, MM_TILES)
    tn = _pick(n, MM_TILES)
    if reduce_b:
        bo, steps = 1, bb
        a_map = lambda o, i, j, r: (r if ba > 1 else 0, i, 0)
        b_map = (lambda o, i, j, r: (r, j, 0)) if trans_b else (lambda o, i, j, r: (r, 0, j))
    else:
        bo, steps = bb, 1
        a_map = lambda o, i, j, r: (o if ba > 1 else 0, i, 0)
        b_map = (lambda o, i, j, r: (o, j, 0)) if trans_b else (lambda o, i, j, r: (o, 0, j))
    o_map = lambda o, i, j, r: (o, i, j)
    add_arrays = [x if x.ndim == 3 else x[None] for x, _ in adds]
    add_scales = [s for _, s in adds]
    n_add = len(adds)
    dot = _dot_nt if trans_b else _dot

    def finish(acc, add_refs, o_ref):
        for ref, s in zip(add_refs, add_scales):
            acc = acc + s * ref[...].astype(F32)
        o_ref[...] = acc.astype(o_ref.dtype)

    grid = (bo, m // tm, n // tn, steps)

    def body(*refs):
        ins, (o_ref,), scratch, xrefs = _split_refs(refs, 2 + n_add, 1, 1 if reduce_b else 0, ex)
        a_ref, b_ref, add_refs = ins[0], ins[1], ins[2:]
        step = ((pl.program_id(0) * grid[1] + pl.program_id(1)) * grid[2] + pl.program_id(2)) * grid[3] + pl.program_id(3)
        if ex:
            @pl.when(step == 0)
            def _():
                ex.start(*xrefs)

        if reduce_b:
            acc, = scratch
            r = pl.program_id(3)

            @pl.when(r == 0)
            def _():
                acc[...] = jnp.zeros_like(acc)

            acc[...] += dot(a_ref[...], b_ref[...])

            @pl.when(r == steps - 1)
            def _():
                finish(acc[...], add_refs, o_ref)
        else:
            finish(dot(a_ref[...], b_ref[...]), add_refs, o_ref)

        if ex:
            @pl.when(step == math.prod(grid) - 1)
            def _():
                ex.finish(*xrefs)

    b_block = (None, tn, k) if trans_b else (None, k, tn)
    out_bytes = tm * tn * jnp.dtype(out_dtype).itemsize
    need = 2 * (tm * k * a3.dtype.itemsize + k * tn * b3.dtype.itemsize + out_bytes + n_add * tm * tn * 4)
    need += 2 * tm * tn * 4
    sem = ("arbitrary",) * 4 if ex else ("parallel", "parallel", "parallel", "arbitrary")
    res = pl.pallas_call(
        body,
        grid=grid,
        in_specs=[pl.BlockSpec((None, tm, k), a_map), pl.BlockSpec(b_block, b_map)]
        + [pl.BlockSpec((None, tm, tn), o_map) for _ in adds] + (ex.in_specs if ex else []),
        out_specs=[pl.BlockSpec((None, tm, tn), o_map)] + (ex.out_specs if ex else []),
        out_shape=[jax.ShapeDtypeStruct((bo, m, n), out_dtype)] + (ex.out_shape if ex else []),
        scratch_shapes=([pltpu.VMEM((tm, tn), F32)] if reduce_b else []) + (ex.scratch if ex else []),
        compiler_params=_params(sem, need),
        name=name,
    )(a3, b3, *add_arrays, *(ex.arrays if ex else []))
    out = res[0][0] if (reduce_b or squeeze) else res[0]
    return (out, res[1:]) if ex else out


def _mm_tn(a, b, *, out_dtype, name):
    squeeze = b.ndim == 2
    b3 = b if b.ndim == 3 else b[None]
    t, m = a.shape
    bb, _, n = b3.shape
    tm = _pick(m, MM_TILES)
    tn = _pick(n, MM_TILES)
    tt = _pick(t, (1024, 512, 256, 128))
    steps = t // tt
    need = 2 * (tt * tm * a.dtype.itemsize + tt * tn * b3.dtype.itemsize + tm * tn * jnp.dtype(out_dtype).itemsize)
    need += 2 * tm * tn * 4

    def body(a_ref, b_ref, o_ref, acc):
        r = pl.program_id(3)

        @pl.when(r == 0)
        def _():
            acc[...] = jnp.zeros_like(acc)

        acc[...] += _dot_tn(a_ref[...], b_ref[...])

        @pl.when(r == steps - 1)
        def _():
            o_ref[...] = acc[...].astype(o_ref.dtype)

    out = pl.pallas_call(
        body,
        grid=(bb, m // tm, n // tn, steps),
        in_specs=[pl.BlockSpec((tt, tm), lambda o, i, j, r: (r, i)),
                  pl.BlockSpec((None, tt, tn), lambda o, i, j, r: (o, r, j))],
        out_specs=pl.BlockSpec((None, tm, tn), lambda o, i, j, r: (o, i, j)),
        out_shape=jax.ShapeDtypeStruct((bb, m, n), out_dtype),
        scratch_shapes=[pltpu.VMEM((tm, tn), F32)],
        compiler_params=_params(("parallel", "parallel", "parallel", "arbitrary"), need),
        name=name,
    )(a, b3)
    return out[0] if squeeze else out


def _rowwise(fn, rows, consts, row_outs, acc_outs, *, tile, name):
    first = rows[0][0] if isinstance(rows[0], tuple) else rows[0]
    t = first.shape[-2]
    steps = t // tile
    arrays, in_specs = [], []
    for r in rows:
        if isinstance(r, tuple) and isinstance(r[1], pl.BlockSpec):
            arrays.append(r[0])
            in_specs.append(r[1])
        elif isinstance(r, tuple):
            arr, bidx = r
            arrays.append(arr)
            in_specs.append(pl.BlockSpec((None, tile, arr.shape[-1]), functools.partial(lambda i, b: (b, i, 0), b=bidx)))
        else:
            arrays.append(r)
            in_specs.append(pl.BlockSpec((tile, r.shape[-1]), lambda i: (i, 0)))
    for c in consts:
        arrays.append(c)
        in_specs.append(pl.BlockSpec(c.shape, lambda i: (0, 0)))
    n_in, n_row = len(arrays), len(row_outs)
    out_shape, out_specs = [], []
    for ro in row_outs:
        if ro[0] == "stack":
            _, cnt, blk, total, w, dt = ro
            out_shape.append(jax.ShapeDtypeStruct((total, t, w), dt))
            out_specs.append(pl.BlockSpec((cnt, tile, w), functools.partial(lambda i, b: (b, i, 0), b=blk)))
        else:
            w, dt = ro
            out_shape.append(jax.ShapeDtypeStruct((t, w), dt))
            out_specs.append(pl.BlockSpec((tile, w), lambda i: (i, 0)))
    out_shape += [jax.ShapeDtypeStruct(s, F32) for s in acc_outs]
    out_specs += [pl.BlockSpec(s, lambda i: (0, 0)) for s in acc_outs]
    blocks = [math.prod(d for d in sp.block_shape if d) * arr.dtype.itemsize for sp, arr in zip(in_specs, arrays)]
    blocks += [math.prod(d for d in sp.block_shape if d) * jnp.dtype(sh.dtype).itemsize
               for sp, sh in zip(out_specs, out_shape)]
    need = 2 * sum(blocks) + 6 * tile * max(a.shape[-1] for a in arrays) * 4

    def body(*refs):
        ins, outs = refs[:n_in], refs[n_in:]
        i = pl.program_id(0)
        res = fn(i, steps, *[r[...] for r in ins])
        res = res if isinstance(res, (tuple, list)) else (res,)
        for ref, val in zip(outs[:n_row], res[:n_row]):
            ref[...] = val.astype(ref.dtype)
        if acc_outs:
            @pl.when(i == 0)
            def _():
                for ref in outs[n_row:]:
                    ref[...] = jnp.zeros_like(ref)

            for ref, val in zip(outs[n_row:], res[n_row:]):
                ref[...] += val

    return pl.pallas_call(
        body,
        grid=(steps,),
        in_specs=in_specs,
        out_specs=out_specs,
        out_shape=out_shape,
        compiler_params=_params(("arbitrary",), need),
        name=name,
    )(*arrays)


def _ln_stats(z):
    mu = _mean(z)
    zc = z - mu
    rstd = lax.rsqrt(_mean(zc * zc) + LN_EPS)
    return zc * rstd, rstd


def _ln_bwd(dy, xhat, rstd, g):
    dxh = dy * g
    return rstd * (dxh - _mean(dxh) - xhat * _mean(dxh * xhat))


def _ride(ex, xrefs, step, steps):
    if not ex:
        return
    for at, act in ((0, ex.start), (steps - 2, ex.relay), (steps - 1, ex.finish)):
        @pl.when(step == at)
        def _(act=act):
            act(*xrefs)


def _sgu_fwd(h, wm, bs_t, g_v, b_v, ex=None):
    t = h.shape[1]
    steps = t // SGU_BLOCK

    def body(*refs):
        (u_ref, v_ref, wm_ref, bs_ref, g_ref, b_ref), (y_ref,), _, xrefs = _split_refs(refs, 6, 1, 0, ex)
        xhat, _ = _ln_stats(_gelu(v_ref[...]))
        vn = (xhat * g_ref[...] + b_ref[...]).astype(BF16)
        gu = _gelu(u_ref[...])
        for g in range(GROUPS):
            sl = slice(g * HEAD, (g + 1) * HEAD)
            mixed = _dot(wm_ref[g], vn[:, sl]) + bs_ref[:, g:g + 1]
            y_ref[:, sl] = (gu[:, sl] * mixed).astype(BF16)
        _ride(ex, xrefs, pl.program_id(0), steps)

    blk = lambda b: pl.BlockSpec((None, SGU_BLOCK, D_MODEL), functools.partial(lambda i, b: (b, i, 0), b=b))
    whole = lambda s: pl.BlockSpec(s, lambda i: (0,) * len(s))
    res = pl.pallas_call(
        body,
        grid=(steps,),
        in_specs=[blk(SLOT_U), blk(SLOT_V), whole(wm.shape), whole(bs_t.shape), whole(g_v.shape), whole(b_v.shape)]
        + (ex.in_specs if ex else []),
        out_specs=[pl.BlockSpec((SGU_BLOCK, D_MODEL), lambda i: (i, 0))] + (ex.out_specs if ex else []),
        out_shape=[jax.ShapeDtypeStruct((t, D_MODEL), BF16)] + (ex.out_shape if ex else []),
        scratch_shapes=ex.scratch if ex else [],
        compiler_params=_params(("arbitrary",)),
        name="sgu_fwd",
    )(h, h, wm, bs_t, g_v, b_v, *(ex.arrays if ex else []))
    return res[0], res[1:]


def _sgu_bwd(h, dy, dh, wm, wm_t, bs_t, g_v, b_v):
    t = h.shape[1]

    def body(u_ref, v_ref, dy_ref, dh_in, wm_ref, wmt_ref, bs_ref, g_ref, b_ref,
             duv_ref, dw_ref, dbs_ref, dg_ref, db_ref, dvn_ref):
        del dh_in
        du_ref, dv_ref = duv_ref.at[0], duv_ref.at[1]
        i = pl.program_id(0)

        @pl.when(i == 0)
        def _():
            dw_ref[...] = jnp.zeros_like(dw_ref)
            dbs_ref[...] = jnp.zeros_like(dbs_ref)
            dg_ref[...] = jnp.zeros_like(dg_ref)
            db_ref[...] = jnp.zeros_like(db_ref)

        u = u_ref[...]
        v = v_ref[...]
        gv, gvp = _gelu_and_grad(v)
        xhat, rstd = _ln_stats(gv)
        vn = (xhat * g_ref[...] + b_ref[...]).astype(BF16)
        gu, gup = _gelu_and_grad(u)
        lane = lax.broadcasted_iota(jnp.int32, (SGU_BLOCK, LANE), 1)
        dbs = jnp.zeros((SGU_BLOCK, LANE), F32)
        for g in range(GROUPS):
            sl = slice(g * HEAD, (g + 1) * HEAD)
            vn_g = vn[:, sl]
            mixed = _dot(wm_ref[g], vn_g) + bs_ref[:, g:g + 1]
            dy_g = dy_ref[:, sl]
            du_ref[:, sl] = (dy_g * mixed * gup[:, sl]).astype(BF16)
            dmix = dy_g * gu[:, sl]
            dmb = dmix.astype(BF16)
            dvn_ref[:, sl] = _dot(wmt_ref[g], dmb)
            dw_ref[g] += _dot_nt(dmb, vn_g)
            dbs = dbs + jnp.where(lane == g, jnp.sum(dmix, axis=1, keepdims=True), 0.0)
        dbs_ref[...] += dbs
        dvn = dvn_ref[...]
        dg_ref[...] += _sum0(dvn * xhat)
        db_ref[...] += _sum0(dvn)
        dv_ref[...] = (_ln_bwd(dvn, xhat, rstd, g_ref[...]) * gvp).astype(BF16)

    blk = lambda b: pl.BlockSpec((None, SGU_BLOCK, D_MODEL), functools.partial(lambda i, b: (b, i, 0), b=b))
    row = pl.BlockSpec((SGU_BLOCK, D_MODEL), lambda i: (i, 0))
    whole = lambda s: pl.BlockSpec(s, lambda i: (0,) * len(s))
    vec = (1, D_MODEL)
    return pl.pallas_call(
        body,
        grid=(t // SGU_BLOCK,),
        in_specs=[blk(SLOT_U), blk(SLOT_V), row, pl.BlockSpec(memory_space=pl.ANY),
                  whole(wm.shape), whole(wm_t.shape), whole(bs_t.shape), whole(vec), whole(vec)],
        out_specs=[pl.BlockSpec((2, SGU_BLOCK, D_MODEL), lambda i: (SLOT_U // 2, i, 0)),
                   whole(wm.shape), whole((SGU_BLOCK, LANE)), whole(vec), whole(vec)],
        out_shape=[jax.ShapeDtypeStruct(dh.shape, BF16),
                   jax.ShapeDtypeStruct(wm.shape, F32), jax.ShapeDtypeStruct((SGU_BLOCK, LANE), F32),
                   jax.ShapeDtypeStruct(vec, F32), jax.ShapeDtypeStruct(vec, F32)],
        scratch_shapes=[pltpu.VMEM((SGU_BLOCK, D_MODEL), F32)],
        input_output_aliases={3: 0},
        compiler_params=_params(("arbitrary",)),
        name="sgu_bwd",
    )(h, h, dy, dh, wm, wm_t, bs_t, g_v, b_v)


def _split3(x):
    hi = x.astype(BF16)
    r1 = x - hi.astype(F32)
    mid = r1.astype(BF16)
    lo = (r1 - mid.astype(F32)).astype(BF16)
    return hi, mid, lo


def _tri_matmul(tri, x):
    hi, mid, lo = _split3(x)
    dot = lambda p: jnp.dot(tri, p, preferred_element_type=F32)
    return dot(hi) + dot(mid) + dot(lo)


def _lower_bound(logits):
    l0, l1 = logits[0:1, :], logits[1:2, :]
    mx = jnp.maximum(l0, l1)
    e0, e1 = jnp.exp(l0 - mx), jnp.exp(l1 - mx)
    return e0 / (e0 + e1)


def _hgrn_gates(q_raw, f_raw, lb):
    q = q_raw * jax.nn.sigmoid(q_raw)
    sig = jax.nn.sigmoid(f_raw)
    f = lb + (1.0 - lb) * sig
    row = lax.broadcasted_iota(jnp.int32, (CHUNK, CHUNK), 0)
    col = lax.broadcasted_iota(jnp.int32, (CHUNK, CHUNK), 1)
    c = _tri_matmul((row >= col).astype(BF16), jnp.log(f))
    return q, sig, f, 1.0 - f, c


def _offdiag_terms(qh, kh, ch, tb):
    rows = slice(tb * SUB, (tb + 1) * SUB)
    r = ch[tb * SUB - 1:tb * SUB, :]
    eqh = jnp.exp(ch[rows] - r)
    ekh = jnp.exp(jnp.minimum(r - ch, 0.0))
    return rows, eqh, qh[rows] * eqh, ekh, kh * ekh


def _diag_decay(cb, s, trow):
    return jnp.exp(jnp.where(trow >= s, cb - cb[s:s + 1, :], NEG))


def _split_refs(refs, n_in, n_out, n_scratch, ex):
    nx = ex.n if ex else 0
    ins, refs = refs[:n_in], refs[n_in:]
    xsrc, refs = refs[:nx], refs[nx:]
    outs, refs = refs[:n_out], refs[n_out:]
    xout, refs = refs[:nx], refs[nx:]
    return ins, outs, refs[:n_scratch], (xsrc, xout, refs[n_scratch:])


def _hgrn_fwd(h, logits, g_norm, ex=None):
    t = h.shape[1]
    nc = t // CHUNK
    per_step = CHUNKS_PER_STEP if nc % CHUNKS_PER_STEP == 0 else 1
    steps = nc // per_step

    def body(*refs):
        ins, outs, scratch, xrefs = _split_refs(refs, 3, 3, 4, ex)
        hgrn_ref, lg_ref, gn_ref = ins
        y_ref, o_ref, sall_ref = outs
        st_ref, q_s, k_s, c_s = scratch

        @pl.when(pl.program_id(0) == 0)
        def _():
            st_ref[...] = jnp.zeros_like(st_ref)

        lb = _lower_bound(lg_ref[...])
        for sub in range(per_step):
            rows = pl.ds(sub * CHUNK, CHUNK)
            chunk(lb, sall_ref.at[sub], *(r.at[rows] for r in (hgrn_ref.at[SLOT_Q], hgrn_ref.at[SLOT_F], hgrn_ref.at[SLOT_I],
                                                               hgrn_ref.at[SLOT_OG], y_ref, o_ref)),
                  gn_ref, st_ref, q_s.at[sub], k_s.at[sub], c_s.at[sub])
        _ride(ex, xrefs, pl.program_id(0), steps)

    def chunk(lb, sall_ref, q_ref, f_ref, i_ref, og_ref, y_ref, o_ref, gn_ref, st_ref, q_s, k_s, c_s):
        q, _, _, k, c = _hgrn_gates(q_ref[...], f_ref[...], lb)
        q_s[...] = q
        k_s[...] = k
        c_s[...] = c
        col64 = lax.broadcasted_iota(jnp.int32, (SUB, CHUNK), 1)
        trow = lax.broadcasted_iota(jnp.int32, (SUB, HEAD), 0)

        def head(hd, carry):
            sl = pl.ds(pl.multiple_of(hd * HEAD, HEAD), HEAD)
            qh, kh, ch, ih = q_s[:, sl], k_s[:, sl], c_s[:, sl], i_ref[:, sl]
            st = st_ref[hd]
            sall_ref[hd] = st
            c_last = ch[CHUNK - 1:CHUNK, :]
            o = _dot_nt(qh * jnp.exp(ch), st)
            st_ref[hd] = st * jnp.exp(c_last) + _dot_tn(ih, kh * jnp.exp(c_last - ch))
            a_rows = [jnp.zeros((SUB, CHUNK), F32)]
            for tb in range(1, CHUNK // SUB):
                _, _, q_hat, _, k_hat = _offdiag_terms(qh, kh, ch, tb)
                a_rows.append(jnp.where(col64 < tb * SUB, _dot_nt(q_hat, k_hat), 0.0))
            o = o + _dot(jnp.concatenate(a_rows, axis=0), ih)
            o_rows = []
            for b in range(CHUNK // SUB):
                rows = slice(b * SUB, (b + 1) * SUB)
                qb, cb, kb, ib = qh[rows], ch[rows], kh[rows], ih[rows]
                ob = jnp.zeros((SUB, HEAD), F32)
                for s in range(SUB):
                    a = jnp.sum(qb * _diag_decay(cb, s, trow) * kb[s:s + 1, :], axis=1, keepdims=True)
                    ob = ob + a * ib[s:s + 1, :]
                o_rows.append(ob)
            o = o + jnp.concatenate(o_rows, axis=0)
            o_ref[:, sl] = o
            og = og_ref[:, sl]
            on = o * lax.rsqrt(_mean(o * o) + RMS_EPS)
            y_ref[:, sl] = (on * gn_ref[:, sl] * (og * jax.nn.sigmoid(og))).astype(BF16)
            return carry

        lax.fori_loop(0, HEADS, head, 0, unroll=HEAD_UNROLL)

    rows = per_step * CHUNK
    row = pl.BlockSpec((rows, D_MODEL), lambda n: (n, 0))
    whole = lambda s: pl.BlockSpec(s, lambda n: (0,) * len(s))
    res = pl.pallas_call(
        body,
        grid=(steps,),
        in_specs=[pl.BlockSpec((4, rows, D_MODEL), lambda n: (SLOT_Q // 4, n, 0)), whole(logits.shape), whole(g_norm.shape)]
        + (ex.in_specs if ex else []),
        out_specs=[row, row, pl.BlockSpec((per_step, HEADS, HEAD, HEAD), lambda n: (n, 0, 0, 0))]
        + (ex.out_specs if ex else []),
        out_shape=[jax.ShapeDtypeStruct((t, D_MODEL), BF16), jax.ShapeDtypeStruct((t, D_MODEL), F32),
                   jax.ShapeDtypeStruct((nc, HEADS, HEAD, HEAD), F32)] + (ex.out_shape if ex else []),
        scratch_shapes=[pltpu.VMEM((HEADS, HEAD, HEAD), F32)] + [pltpu.VMEM((per_step, CHUNK, D_MODEL), F32)] * 3
        + (ex.scratch if ex else []),
        compiler_params=_params(("arbitrary",)),
        name="hgrn_fwd",
    )(h, logits, g_norm, *(ex.arrays if ex else []))
    return res[0], res[1], res[2], res[3:]


def _hgrn_bwd(h, o_all, dy, states, dh, logits, g_norm, ex=None):
    t = h.shape[1]
    nc = t // CHUNK

    def body(*refs):
        ins, outs, scratch, xrefs = _split_refs(refs, 7, 3, 8, ex)
        hgrn_ref, o_ref, dy_ref, sall_ref, _, lg_ref, gn_ref = ins
        q_ref, f_ref, i_ref, og_ref = (hgrn_ref.at[s] for s in (SLOT_Q, SLOT_F, SLOT_I, SLOT_OG))
        dqfio_ref, dlg_ref, dgn_ref = outs
        dst_ref, dlb_ref, q_s, k_s, c_s, dq_s, dk_s, dc_s = scratch
        dq_ref, df_ref, di_ref, dog_ref = (dqfio_ref.at[s] for s in (SLOT_Q, SLOT_F, SLOT_I, SLOT_OG))
        n = pl.program_id(0)

        @pl.when(n == 0)
        def _():
            dst_ref[...] = jnp.zeros_like(dst_ref)
            dlb_ref[...] = jnp.zeros_like(dlb_ref)
            dgn_ref[...] = jnp.zeros_like(dgn_ref)
            if ex:
                ex.start(*xrefs)

        lb = _lower_bound(lg_ref[...])
        q_raw = q_ref[...]
        q, sig, f, k, c = _hgrn_gates(q_raw, f_ref[...], lb)
        q_s[...] = q
        k_s[...] = k
        c_s[...] = c
        col64 = lax.broadcasted_iota(jnp.int32, (SUB, CHUNK), 1)
        trow = lax.broadcasted_iota(jnp.int32, (SUB, HEAD), 0)
        row64 = lax.broadcasted_iota(jnp.int32, (CHUNK, HEAD), 0)

        def head(hd, carry):
            sl = pl.ds(pl.multiple_of(hd * HEAD, HEAD), HEAD)
            qh, kh, ch, ih = q_s[:, sl], k_s[:, sl], c_s[:, sl], i_ref[:, sl]
            st = sall_ref[hd]
            dst = dst_ref[hd]
            oh, dyh, og, gn = o_ref[:, sl], dy_ref[:, sl], og_ref[:, sl], gn_ref[:, sl]
            sg = jax.nn.sigmoid(og)
            sil = og * sg
            rms = lax.rsqrt(_mean(oh * oh) + RMS_EPS)
            on = oh * rms
            dog_ref[:, sl] = (dyh * on * gn * _silu_grad(og, sg)).astype(BF16)
            dgn_ref[:, sl] += _sum0(dyh * on * sil)
            don = dyh * gn * sil
            do = rms * (don - on * _mean(don * on))
            dob = do.astype(BF16)

            c_last = ch[CHUNK - 1:CHUNK, :]
            eq = jnp.exp(ch)
            q_til = qh * eq
            ekl = jnp.exp(c_last - ch)
            k_til = kh * ekl
            ecl = jnp.exp(c_last)
            dq_til = _dot(dob, st)
            dk_til = _dot(ih, dst)
            di = _dot_nt(k_til, dst)
            dc_last = _sum0(dk_til * k_til) + _sum0(dst * st) * ecl
            dst_ref[hd] = _dot_tn(dob, q_til) + dst * ecl
            dq = dq_til * eq
            dc = dq_til * q_til - dk_til * k_til
            dk = dk_til * ekl

            da_full = _dot_nt(dob, ih)
            a_rows = [jnp.zeros((SUB, CHUNK), F32)]
            dq_rows = [jnp.zeros((SUB, HEAD), F32)]
            dc_rows = [jnp.zeros((SUB, HEAD), F32)]
            for tb in range(1, CHUNK // SUB):
                rows, eqh, q_hat, ekh, k_hat = _offdiag_terms(qh, kh, ch, tb)
                keep = col64 < tb * SUB
                a_rows.append(jnp.where(keep, _dot_nt(q_hat, k_hat), 0.0))
                da = jnp.where(keep, da_full[rows], 0.0)
                dq_hat = _dot(da, k_hat)
                dk_hat = _dot_tn(da, q_hat)
                dq_rows.append(dq_hat * eqh)
                dc_rows.append(dq_hat * q_hat)
                dk = dk + dk_hat * ekh
                dc = dc - dk_hat * k_hat
            di = di + _dot_tn(jnp.concatenate(a_rows, axis=0), dob)

            dk_rows, di_rows = [], []
            for b in range(CHUNK // SUB):
                rows = slice(b * SUB, (b + 1) * SUB)
                qb, cb, kb, ib, dob_ = qh[rows], ch[rows], kh[rows], ih[rows], do[rows]
                dq_diag = jnp.zeros((SUB, HEAD), F32)
                dk_diag = jnp.zeros((SUB, HEAD), F32)
                di_diag = jnp.zeros((SUB, HEAD), F32)
                for s in range(SUB):
                    ks = kb[s:s + 1, :]
                    dec = _diag_decay(cb, s, trow)
                    a = jnp.sum(qb * dec * ks, axis=1, keepdims=True)
                    gk = jnp.sum(dob_ * ib[s:s + 1, :], axis=1, keepdims=True) * dec
                    dq_diag = dq_diag + gk * ks
                    dk_diag = dk_diag + jnp.where(trow == s, _sum0(gk * qb), 0.0)
                    di_diag = di_diag + jnp.where(trow == s, _sum0(a * dob_), 0.0)
                dq_rows[b] = dq_rows[b] + dq_diag
                dc_rows[b] = dc_rows[b] + qb * dq_diag - kb * dk_diag
                dk_rows.append(dk_diag)
                di_rows.append(di_diag)
            dq = dq + jnp.concatenate(dq_rows, axis=0)
            dk = dk + jnp.concatenate(dk_rows, axis=0)
            dc = dc + jnp.concatenate(dc_rows, axis=0) + jnp.where(row64 == CHUNK - 1, dc_last, 0.0)
            di_ref[:, sl] = (di + jnp.concatenate(di_rows, axis=0)).astype(BF16)
            dq_s[:, sl] = dq
            dk_s[:, sl] = dk
            dc_s[:, sl] = dc
            return carry

        lax.fori_loop(0, HEADS, head, 0, unroll=HEAD_UNROLL)

        row = lax.broadcasted_iota(jnp.int32, (CHUNK, CHUNK), 0)
        col = lax.broadcasted_iota(jnp.int32, (CHUNK, CHUNK), 1)
        dlf = _tri_matmul((row <= col).astype(BF16), dc_s[...])
        df = dlf / f - dk_s[...]
        dlb_ref[...] += _sum0(df * (1.0 - sig))
        df_ref[...] = (df * (1.0 - lb) * sig * (1.0 - sig)).astype(BF16)
        dq_ref[...] = (dq_s[...] * _silu_grad(q_raw, jax.nn.sigmoid(q_raw))).astype(BF16)

        @pl.when(n == nc - 1)
        def _():
            d0 = dlb_ref[...] * lb * (1.0 - lb)
            dlg_ref[0:1, :] = d0
            dlg_ref[1:2, :] = -d0
            if ex:
                ex.finish(*xrefs)

    rev = lambda n: nc - 1 - n
    slots = pl.BlockSpec((4, CHUNK, D_MODEL), lambda n: (SLOT_Q // 4, rev(n), 0))
    row = pl.BlockSpec((CHUNK, D_MODEL), lambda n: (rev(n), 0))
    whole = lambda s: pl.BlockSpec(s, lambda n: (0,) * len(s))
    vec = (1, D_MODEL)
    res = pl.pallas_call(
        body,
        grid=(nc,),
        in_specs=[slots, row, row, pl.BlockSpec((None, HEADS, HEAD, HEAD), lambda n: (rev(n), 0, 0, 0)),
                  pl.BlockSpec(memory_space=pl.ANY), whole(logits.shape), whole(vec)] + (ex.in_specs if ex else []),
        out_specs=[slots, whole((2, D_MODEL)), whole(vec)] + (ex.out_specs if ex else []),
        out_shape=[jax.ShapeDtypeStruct(dh.shape, BF16), jax.ShapeDtypeStruct((2, D_MODEL), F32),
                   jax.ShapeDtypeStruct(vec, F32)] + (ex.out_shape if ex else []),
        scratch_shapes=[pltpu.VMEM((HEADS, HEAD, HEAD), F32), pltpu.VMEM(vec, F32)]
        + [pltpu.VMEM((CHUNK, D_MODEL), F32)] * 6 + (ex.scratch if ex else []),
        input_output_aliases={4: 0},
        compiler_params=_params(("arbitrary",)),
        name="hgrn_bwd",
    )(h, o_all, dy, states, dh, logits, g_norm, *(ex.arrays if ex else []))
    return res[0], res[1], res[2], res[3:]


def _merge_fwd(i, n, ga, gb, za, zb):
    return jax.nn.sigmoid(ga) * za + jax.nn.sigmoid(gb) * zb


def _merge_bwd(i, n, ga, gb, za, zb, dm):
    sa, sb = jax.nn.sigmoid(ga), jax.nn.sigmoid(gb)
    dgates = jnp.stack([(dm * za * sa * (1.0 - sa)).astype(BF16), (dm * zb * sb * (1.0 - sb)).astype(BF16)])
    return dgates, dm * sa, dm * sb


def _ln1_fwd(i, n, x, r1, g, b):
    xhat, _ = _ln_stats(ALPHA * x + r1)
    x1 = xhat * g + b
    return x1, x1


def _ln1_bwd(i, n, x, r1, dx1, g):
    xhat, rstd = _ln_stats(ALPHA * x + r1)
    dz = _ln_bwd(dx1, xhat, rstd, g)
    return dz, dz, _sum0(dx1 * xhat), _sum0(dx1)


def _ln2_loss(i, n, x1, fo, pg, pp, tgt, g, b):
    sg = jax.nn.sigmoid(pg)
    xhat, rstd = _ln_stats(ALPHA * x1 + fo + sg * pp)
    diff = xhat * g + b - tgt
    loss = 0.5 * jnp.sum(_mean(diff * diff), axis=0, keepdims=True)
    dy = diff * (1.0 / D_MODEL)
    dz = _ln_bwd(dy, xhat, rstd, g)
    return (dz, dz, dz * pp * sg * (1.0 - sg), dz * sg,
            jnp.broadcast_to(loss, (8, LANE)), _sum0(dy * xhat), _sum0(dy))


def _shift_down(cur, halo, tile):
    row = lax.broadcasted_iota(jnp.int32, cur.shape, 0)
    m1 = jnp.where(row == 0, halo[7:8, :], pltpu.roll(cur, 1, 0))
    m2 = jnp.where(row == 0, halo[6:7, :], jnp.where(row == 1, halo[7:8, :], pltpu.roll(cur, 2, 0)))
    return m1, m2


def _shift_up(cur, halo, tile):
    row = lax.broadcasted_iota(jnp.int32, cur.shape, 0)
    p1 = jnp.where(row == tile - 1, halo[0:1, :], pltpu.roll(cur, tile - 1, 0))
    p2 = jnp.where(row == tile - 2, halo[0:1, :], jnp.where(row == tile - 1, halo[1:2, :], pltpu.roll(cur, tile - 2, 0)))
    return p1, p2


def _conv_pre(i, gate, halo, w, b, tile):
    halo = jnp.where(i == 0, 0.0, halo)
    m1, m2 = _shift_down(gate, halo, tile)
    return w[0:1, :] * m2 + w[1:2, :] * m1 + w[2:3, :] * gate + b, m1, m2


def _conv_fwd(tile, i, n, gate, halo, val, w, b):
    cg, _, _ = _conv_pre(i, gate, halo, w, b, tile)
    return _gelu(cg) * val


def _conv_bwd_a(tile, i, n, gate, halo, val, dhid, w, b):
    cg, m1, m2 = _conv_pre(i, gate, halo, w, b, tile)
    act, slope = _gelu_and_grad(cg)
    dcg = dhid * val * slope
    return dcg, dhid * act, _sum0(dcg * m2), _sum0(dcg * m1), _sum0(dcg * gate), _sum0(dcg)


def _conv_bwd_b(tile, i, n, dcg, halo, w):
    dcg = dcg.astype(F32)
    halo = jnp.where(i == n - 1, 0.0, halo.astype(F32))
    p1, p2 = _shift_up(dcg, halo, tile)
    return w[2:3, :] * dcg + w[1:2, :] * p1 + w[0:1, :] * p2


def _halo_spec(width, tile, t, nxt, rows=8):
    per = tile // rows
    last = t // rows - 1
    if nxt:
        return pl.BlockSpec((rows, width), lambda i: (jnp.minimum((i + 1) * per, last), 0))
    return pl.BlockSpec((rows, width), lambda i: (jnp.maximum(i * per - 1, 0), 0))


def _adamw(i, n, w, m, v, parts):
    g = parts[0].astype(F32)
    for j in range(1, parts.shape[0]):
        g = g + parts[j].astype(F32)
    m_new = ADAM_B1 * m + (1.0 - ADAM_B1) * g
    v_new = ADAM_B2 * v + (1.0 - ADAM_B2) * (g * g)
    m_hat = m_new / (1.0 - ADAM_B1 ** ADAM_STEP)
    v_hat = v_new / (1.0 - ADAM_B2 ** ADAM_STEP)
    delta = -ADAM_LR * (m_hat / (jnp.sqrt(v_hat) + ADAM_EPS) + ADAM_WD * w)
    return g, delta, m_new, v_new


def _adam_call(w, m, v, parts, name):
    r, c = w.shape
    tile = _pick(r, (256, 128)) if r > 256 else r
    spec = pl.BlockSpec((parts.shape[0], tile, c), lambda i: (0, i, 0))
    return _rowwise(_adamw, [w, m, v, (parts, spec)], [], [(c, F32)] * 4, [], tile=tile, name=name)


def _peer(k):
    x, y, c = lax.axis_index("x"), lax.axis_index("y"), lax.axis_index("c")
    px = x ^ ((k >> 2) & 1)
    py = y ^ ((k >> 1) & 1)
    pc = c ^ (k & 1)
    return (px, py, pc), 4 * px + 2 * py + pc


def _my_index():
    return 4 * lax.axis_index("x") + 2 * lax.axis_index("y") + lax.axis_index("c")


class _Exchange:
    KINDS = ("gather", "gather+relay", "scatter", "scatter+pairs")

    def __init__(self, entries):
        assert all(k in self.KINDS for _, k in entries), [k for _, k in entries]
        self.arrays = [a for a, _ in entries]
        self.scatter = [k.startswith("scatter") for _, k in entries]
        self.relayed = ["+relay" in k for _, k in entries]
        self.pairs = ["+pairs" in k for _, k in entries]
        self.n = len(entries)
        self.in_specs = [pl.BlockSpec(memory_space=pl.ANY)] * self.n
        self.out_specs = [pl.BlockSpec(memory_space=pl.ANY)] * self.n
        shapes = [tuple(a.shape[1:]) if sc else tuple(a.shape) for a, sc in zip(self.arrays, self.scatter)]
        counts = [N_CHIP if p else N_DEV for p in self.pairs]
        self.out_shape = [jax.ShapeDtypeStruct((n,) + s, a.dtype) for n, s, a in zip(counts, shapes, self.arrays)]
        per = N_DEV - 1
        self.scratch = [pltpu.SemaphoreType.DMA((self.n * per,)), pltpu.SemaphoreType.DMA((self.n * per,)),
                        pltpu.SemaphoreType.DMA((self.n,))]

    def _copies(self, srcs, outs, sems):
        send_sems, recv_sems, local_sems = sems
        x, y, c = lax.axis_index("x"), lax.axis_index("y"), lax.axis_index("c")
        me = _my_index()
        per = N_DEV - 1
        local, first, passed, relay_arrivals, arrivals = [], [], [], [], []
        far, near = [], []
        for a in range(self.n):

            def copy(k, src, dst, dev, a=a):
                return pltpu.make_async_remote_copy(
                    src_ref=src, dst_ref=dst, send_sem=send_sems.at[a * per + k], recv_sem=recv_sems.at[a * per + k],
                    device_id=dev, device_id_type=pl.DeviceIdType.MESH)

            if self.pairs[a]:
                chip = 2 * x + y
                for k in range(N_CHIP):
                    to = chip ^ k
                    piece = srcs[a].at[_slot_of_group(2 * to + c) // 2]
                    if k == 0:
                        local.append(pltpu.make_async_copy(piece, outs[a].at[chip], local_sems.at[a]))
                    else:
                        dev = (to // 2, to % 2, c)
                        (far if k == 3 else first).append(copy(k - 1, piece, outs[a].at[chip], dev))
                        arrivals.append(copy(k - 1, piece, outs[a].at[to], dev))
                continue
            mine = srcs[a].at[me] if self.scatter[a] else srcs[a]
            land = outs[a].at[me]
            local.append(pltpu.make_async_copy(mine, land, local_sems.at[a]))
            if self.relayed[a]:
                block = lambda px, py, pc, a=a: outs[a].at[4 * px + 2 * py + pc]
                chips = [(1 - x, y), (x, 1 - y), (1 - x, 1 - y)]
                near.append(copy(0, mine, land, (x, y, 1 - c)))
                arrivals.append(copy(0, mine, block(x, y, 1 - c), (x, y, 1 - c)))
                for j, (px, py) in enumerate(chips):
                    (far if j == 2 else first).append(copy(1 + j, mine, land, (px, py, c)))
                    relay_arrivals.append(copy(1 + j, mine, block(px, py, c), (px, py, c)))
                    passed.append(copy(4 + j, block(px, py, c), block(px, py, c), (x, y, 1 - c)))
                    arrivals.append(copy(4 + j, mine, block(px, py, 1 - c), (x, y, 1 - c)))
                continue
            for k in range(1, N_DEV):
                dev, idx = _peer(k)
                queue = far if k & 6 == 6 else near if k == 1 else first
                queue.append(copy(k - 1, srcs[a].at[idx] if self.scatter[a] else mine, land, dev))
                arrivals.append(copy(k - 1, mine, outs[a].at[idx], dev))
        return local, far + first + near, passed, relay_arrivals, arrivals

    def start(self, srcs, outs, sems):
        local, first, _, _, _ = self._copies(srcs, outs, sems)
        for cp in first + local:
            cp.start()

    def relay(self, srcs, outs, sems):
        _, _, passed, relay_arrivals, _ = self._copies(srcs, outs, sems)
        for landed, onward in zip(relay_arrivals, passed):
            landed.wait_recv()
            onward.start()

    def finish(self, srcs, outs, sems):
        local, first, passed, _, arrivals = self._copies(srcs, outs, sems)
        for cp in arrivals:
            cp.wait_recv()
        for cp in first + passed:
            cp.wait_send()
        for cp in local:
            cp.wait()


def _gather_project(x_b, shard):
    t, d = x_b.shape
    tm = _pick(t, MM_TILES)
    nrow = t // tm
    per = N_DEV - 1

    def parties():
        x, y, c = lax.axis_index("x"), lax.axis_index("y"), lax.axis_index("c")
        chips = [(1 - x, y), (x, 1 - y), (1 - x, 1 - y)]
        return (x, y, c), (x, y, 1 - c), [(px, py, c) for px, py in chips], [(px, py, 1 - c) for px, py in chips]

    slot = lambda dev: _slot_of_group(4 * dev[0] + 2 * dev[1] + dev[2])
    ici_step, passed_step = (4, 5, 2), (6, 7, 3)
    me, sibling, over_ici, passed_on = parties()
    by_step = {0: me, 1: sibling, **dict(zip(ici_step, over_ici)), **dict(zip(passed_step, passed_on))}
    order = jnp.stack([slot(by_step[j]) for j in range(N_DEV)]).astype(jnp.int32)

    def body(order_ref, x_ref, shard_ref, h_ref, wall_ref, wbuf, fetch_sem, send_sems, recv_sems, local_sem):
        del order_ref
        me, sibling, over_ici, passed_on = parties()
        j, i = pl.program_id(0), pl.program_id(1)
        land = lambda dev: wall_ref.at[slot(dev)]

        def copy(k, src, block, to):
            return pltpu.make_async_remote_copy(src_ref=src, dst_ref=land(block), send_sem=send_sems.at[k],
                                                recv_sem=recv_sems.at[k], device_id=to, device_id_type=pl.DeviceIdType.MESH)

        def fetch(src):
            cp = pltpu.make_async_copy(src, wbuf, fetch_sem)
            cp.start()
            cp.wait()

        keep = pltpu.make_async_copy(shard_ref, land(me), local_sem)
        first = [copy(0, shard_ref, me, sibling)] + [copy(1 + n, shard_ref, me, dev) for n, dev in enumerate(over_ici)]
        onward = [copy(4 + n, land(dev), dev, sibling) for n, dev in enumerate(over_ici)]

        @pl.when(jnp.logical_and(i == 0, j == 0))
        def _():
            for cp in reversed(first):
                cp.start()
            keep.start()
            fetch(shard_ref)

        @pl.when(jnp.logical_and(i == 0, j == 1))
        def _():
            copy(0, shard_ref, sibling, me).wait_recv()
            fetch(land(sibling))

        for n, dev in enumerate(over_ici):
            @pl.when(jnp.logical_and(i == 0, j == ici_step[n]))
            def _(n=n, dev=dev):
                copy(1 + n, shard_ref, dev, me).wait_recv()
                onward[n].start()
                fetch(land(dev))

        for n, dev in enumerate(passed_on):
            @pl.when(jnp.logical_and(i == 0, j == passed_step[n]))
            def _(n=n, dev=dev):
                copy(4 + n, shard_ref, dev, me).wait_recv()
                fetch(land(dev))

        h_ref[...] = _dot(x_ref[...], wbuf[...])

        @pl.when(jnp.logical_and(i == nrow - 1, j == N_DEV - 1))
        def _():
            for cp in first + onward:
                cp.wait_send()
            keep.wait()

    need = 2 * (tm * d * 2 + tm * d * 4) + d * d * 2 + tm * d * 4
    h, w_all = pl.pallas_call(
        body,
        grid_spec=pltpu.PrefetchScalarGridSpec(
            num_scalar_prefetch=1,
            grid=(N_DEV, nrow),
            in_specs=[pl.BlockSpec((tm, d), lambda j, i, order: (i, 0)), pl.BlockSpec(memory_space=pl.ANY)],
            out_specs=[pl.BlockSpec((None, tm, d), lambda j, i, order: (order[j], i, 0)), pl.BlockSpec(memory_space=pl.ANY)],
            scratch_shapes=[pltpu.VMEM((d, d), BF16), pltpu.SemaphoreType.DMA, pltpu.SemaphoreType.DMA((per,)),
                            pltpu.SemaphoreType.DMA((per,)), pltpu.SemaphoreType.DMA],
        ),
        out_shape=[jax.ShapeDtypeStruct((N_DEV, t, d), F32), jax.ShapeDtypeStruct((N_DEV, d, d), BF16)],
        compiler_params=_params(("arbitrary", "arbitrary"), need),
        name="gather_project",
    )(order, x_b, shard)
    return h, w_all


def _pair_sums(g):
    n, r, cols = g.shape
    half = n // 2

    def swap(g_ref, got_ref, send_sems, recv_sems):
        x, y, c = lax.axis_index("x"), lax.axis_index("y"), lax.axis_index("c")
        copies = [pltpu.make_async_remote_copy(
            src_ref=g_ref.at[2 * j + 1 - c], dst_ref=got_ref.at[j], send_sem=send_sems.at[j], recv_sem=recv_sems.at[j],
            device_id=(x, y, 1 - c), device_id_type=pl.DeviceIdType.MESH) for j in range(half)]
        for cp in copies:
            cp.start()
        for cp in copies:
            cp.wait()

    got = pl.pallas_call(
        swap,
        in_specs=[pl.BlockSpec(memory_space=pl.ANY)],
        out_specs=pl.BlockSpec(memory_space=pl.ANY),
        out_shape=jax.ShapeDtypeStruct((half, r, cols), g.dtype),
        scratch_shapes=[pltpu.SemaphoreType.DMA((half,))] * 2,
        name="pair_swap",
    )(g)

    def add(mine_ref, got_ref, out_ref):
        mine = jnp.where(lax.axis_index("c") == 0, mine_ref[0].astype(F32), mine_ref[1].astype(F32))
        out_ref[...] = (mine + got_ref[...].astype(F32)).astype(out_ref.dtype)

    tile = _pick(r, (512, 256, 128))
    return pl.pallas_call(
        add,
        grid=(half, r // tile),
        in_specs=[pl.BlockSpec((None, 2, tile, cols), lambda j, i: (j, 0, i, 0)),
                  pl.BlockSpec((None, tile, cols), lambda j, i: (j, i, 0))],
        out_specs=pl.BlockSpec((None, tile, cols), lambda j, i: (j, i, 0)),
        out_shape=jax.ShapeDtypeStruct((half, r, cols), g.dtype),
        compiler_params=_params(("parallel", "parallel")),
        name="pair_add",
    )(g.reshape(half, 2, r, cols), got)


def _local_step(x, p, tgt, small, comm):
    t = x.shape[0]
    tile = _pick(t, (256, 128))
    tall = _pick(t, (512, 256, 128))
    d = D_MODEL
    act_b, act_f = (d, BF16), (d, F32)
    x_b, p_b = x.astype(BF16), p.astype(BF16)

    chunk_id = jnp.arange(SGU_BLOCK) // CHUNK
    mask = chunk_id[:, None] >= chunk_id[None, :]
    wm = jnp.where(mask[None], small["sgu_w_s"], 0.0)
    wm_b = wm.astype(BF16)
    wm_t = jnp.swapaxes(wm, 1, 2).astype(BF16)
    bs_t = small["sgu_b_s"].T

    h, w_in = comm.project_in(x_b)
    y_a, got_a = _sgu_fwd(h, wm_b, bs_t, small["sgu_norm_g"], small["sgu_norm_b"], ex=comm.weights_exchange(0))
    y_b, o_all, states, got_b = _hgrn_fwd(h, small["lb_logits"], small["hgrn_norm_g"], ex=comm.weights_exchange(1))
    wts, conv_w = comm.weights(list(got_a) + list(got_b))
    z_a = _mm(y_a, wts["w_a"], out_dtype=F32, name="mm_za")
    z_b = _mm(y_b, wts["w_b"], out_dtype=F32, name="mm_zb")
    gates = [(h, SLOT_GA), (h, SLOT_GB)]
    merged, = _rowwise(_merge_fwd, gates + [z_a, z_b], [], [act_b], [], tile=tall, name="merge_fwd")
    r1 = _mm(merged, wts["w_o"], out_dtype=F32, name="mm_r1")
    x1, x1_b = _rowwise(_ln1_fwd, [x, r1], [small["ln1_g"], small["ln1_b"]], [act_f, act_b], [], tile=tall, name="ln1_fwd")
    gate = _mm(x1_b, wts["w_g"], out_dtype=F32, name="mm_gate")
    val = _mm(x1_b, wts["w_v"], out_dtype=F32, name="mm_val")
    pg = _mm(x1_b, wts["w_pg"], out_dtype=F32, name="mm_pg")
    pp = _mm(p_b, wts["w_pp"], out_dtype=F32, name="mm_pp")
    hid, = _rowwise(functools.partial(_conv_fwd, tile), [gate, (gate, _halo_spec(D_FF, tile, t, False)), val],
                    [conv_w, small["conv_b"]], [(D_FF, BF16)], [], tile=tile, name="conv_fwd")
    fo = _mm(hid, wts["w_down"], out_dtype=F32, name="mm_down")
    dz2, dz2_b, dpg, dpp, loss, dg2, db2 = _rowwise(
        _ln2_loss, [x1, fo, pg, pp, tgt], [small["ln2_g"], small["ln2_b"]],
        [act_f, act_b, act_b, act_b], [(8, LANE), (1, d), (1, d)], tile=tall, name="ln2_loss")

    dhid = _mm(dz2_b, wts["w_down"], out_dtype=BF16, name="mm_dhid", trans_b=True)
    g_down = _mm_tn(hid, dz2_b, out_dtype=BF16, name="mm_g_down")
    dcg, dval, dcw0, dcw1, dcw2, dcb = _rowwise(
        functools.partial(_conv_bwd_a, tile), [gate, (gate, _halo_spec(D_FF, tile, t, False)), val, dhid],
        [conv_w, small["conv_b"]], [(D_FF, BF16), (D_FF, BF16)], [(1, D_FF)] * 4, tile=tile, name="conv_bwd_a")
    dgate, = _rowwise(functools.partial(_conv_bwd_b, tile), [dcg, (dcg, _halo_spec(D_FF, tile, t, True, rows=16))],
                      [conv_w], [(D_FF, BF16)], [], tile=tile, name="conv_bwd_b")
    g_g = _mm_tn(x1_b, dgate, out_dtype=BF16, name="mm_g_gate")
    g_v = _mm_tn(x1_b, dval, out_dtype=BF16, name="mm_g_val")
    g_pg = _mm_tn(x1_b, dpg, out_dtype=BF16, name="mm_g_pg")
    g_pp = _mm_tn(p_b, dpp, out_dtype=BF16, name="mm_g_pp")
    dx1 = _mm(dgate, wts["w_g"], out_dtype=F32, name="mm_dx1_gate", trans_b=True, adds=[(dz2, ALPHA)])
    dx1 = _mm(dval, wts["w_v"], out_dtype=F32, name="mm_dx1_val", trans_b=True, adds=[(dx1, 1.0)])
    dx1 = _mm(dpg, wts["w_pg"], out_dtype=F32, name="mm_dx1_pg", trans_b=True, adds=[(dx1, 1.0)])
    dz1, dz1_b, dg1, db1 = _rowwise(_ln1_bwd, [x, r1, dx1], [small["ln1_g"]], [act_f, act_b], [(1, d), (1, d)],
                                    tile=tall, name="ln1_bwd")
    g_o = _mm_tn(merged, dz1_b, out_dtype=BF16, name="mm_g_o")
    dm = _mm(dz1_b, wts["w_o"], out_dtype=BF16, name="mm_dm", trans_b=True)
    dh, dza, dzb = _rowwise(_merge_bwd, gates + [z_a, z_b, dm], [],
                            [("stack", 2, SLOT_GA // 2, 8, d, BF16), act_b, act_b], [], tile=tall, name="merge_bwd")
    g_a = _mm_tn(y_a, dza, out_dtype=BF16, name="mm_g_a")
    g_b = _mm_tn(y_b, dzb, out_dtype=BF16, name="mm_g_b")
    dy_a = _mm(dza, wts["w_a"], out_dtype=BF16, name="mm_dya", trans_b=True)
    dy_b = _mm(dzb, wts["w_b"], out_dtype=F32, name="mm_dyb", trans_b=True)
    dh, dws, dbs, dgv_n, dbv_n = _sgu_bwd(h, dy_a, dh, wm_b, wm_t, bs_t, small["sgu_norm_g"], small["sgu_norm_b"])
    big = dict(w_a=g_a, w_b=g_b, w_o=g_o, w_g=g_g, w_v=g_v, w_down=g_down, w_pp=g_pp, w_pg=g_pg)
    sm = dict(sgu_w_s=jnp.where(mask[None], dws, 0.0), sgu_b_s=dbs[:, :GROUPS].T, sgu_norm_g=dgv_n, sgu_norm_b=dbv_n,
              ln1_g=dg1, ln1_b=db1, conv_w=jnp.concatenate([dcw0, dcw1, dcw2], axis=0), conv_b=dcb, ln2_g=dg2, ln2_b=db2,
              loss=loss)
    dh, dlogits, dgn, got = _hgrn_bwd(h, o_all, dy_b, states, dh, small["lb_logits"], small["hgrn_norm_g"],
                                      ex=comm.grads_exchange(big, sm))
    comm.grads_done(got)
    g_in = _mm_tn(x_b, dh, out_dtype=BF16, name="mm_g_in")
    ex = comm.last_exchange(g_in, dict(lb_logits=dlogits, hgrn_norm_g=dgn))
    res = _mm(dh, w_in, out_dtype=F32, name="mm_dx", trans_b=True, reduce_b=True, adds=[(dz1, ALPHA)], ex=ex)
    grad_x, got = res if ex else (res, ())
    comm.last_done(got)
    return grad_x


_SMALL_EARLY = ["sgu_w_s", "sgu_b_s", "sgu_norm_g", "sgu_norm_b", "ln1_g", "ln1_b", "ffn_conv_b", "ln2_g", "ln2_b"]
_SMALL_LATE = ["hgrn_lb_logits", "hgrn_norm_g"]
N_TAPS = D_FF // N_DEV
UP_COLS = 2 * D_FF // N_DEV


class _StepExchanges:
    def __init__(self, w_in_shard, shards):
        self.w_in_shard = w_in_shard
        self.shards = shards

    def project_in(self, x_b):
        return _gather_project(x_b, self.w_in_shard)

    def weights_exchange(self, part):
        return _Exchange([(s, "gather+relay") for s in (self.shards[:2] if part == 0 else self.shards[2:])])

    def weights(self, got):
        d, f = D_MODEL, D_FF
        w_br_g, w_o_g, w_up_g, w_down_g, w_pp_g, w_pg_g, conv_g = got
        w_br = w_br_g.transpose(1, 0, 2, 3).reshape(2, d, d)
        w_up = w_up_g.transpose(1, 0, 2).reshape(d, 2, f).transpose(1, 0, 2)
        wts = dict(w_a=w_br[0], w_b=w_br[1], w_o=w_o_g.reshape(d, d), w_g=w_up[0], w_v=w_up[1],
                   w_down=w_down_g.reshape(f, d), w_pp=w_pp_g.transpose(1, 0, 2).reshape(256, d), w_pg=w_pg_g.reshape(d, d))
        return wts, conv_g.transpose(1, 0, 2).reshape(3, f)

    def grads_exchange(self, big, sm):
        d = D_MODEL
        parts = [jnp.stack([big["w_a"], big["w_b"]]).reshape(2, N_DEV, 128, d).transpose(1, 0, 2, 3),
                 big["w_o"].reshape(N_DEV, 128, d),
                 jnp.concatenate([big["w_g"], big["w_v"]], axis=1).reshape(d, N_DEV, UP_COLS).transpose(1, 0, 2),
                 big["w_down"].reshape(N_DEV, N_TAPS, d),
                 big["w_pp"].reshape(256, N_DEV, 128).transpose(1, 0, 2),
                 big["w_pg"].reshape(N_DEV, 128, d)]
        packed, self.rows_early = _pack([sm[k] for k in ("sgu_w_s", "sgu_b_s", "sgu_norm_g", "sgu_norm_b", "ln1_g", "ln1_b",
                                                         "conv_b", "ln2_g", "ln2_b", "conv_w", "loss")])
        return _Exchange([(a, "scatter") for a in parts] + [(packed, "gather")])

    def grads_done(self, got):
        self.recv, self.small_early = got[:6], got[6]

    def last_exchange(self, g_in, sm):
        packed, self.rows_late = _pack([sm["lb_logits"], sm["hgrn_norm_g"]])
        return _Exchange([(_pair_sums(g_in), "scatter+pairs"), (packed, "gather")])

    def last_done(self, got):
        self.recv_in, self.small_late = got


def _rows128(a):
    flat = a.reshape(-1)
    rows = -(-flat.shape[0] // (8 * LANE)) * 8
    return jnp.pad(flat, (0, rows * LANE - flat.shape[0])).reshape(rows, LANE)


def _pack(parts):
    blocks = [_rows128(a) for a in parts]
    return jnp.concatenate(blocks, axis=0), [b.shape[0] for b in blocks]


def _unpack(packed, shapes, rows):
    out, r0 = [], 0
    for shp, r in zip(shapes, rows):
        n = math.prod(shp)
        out.append(packed[r0:r0 + r].reshape(-1)[:n].reshape(shp))
        r0 += r
    return out


def kernel(x, p, w_in, sgu_w_s, sgu_b_s, sgu_norm_g, sgu_norm_b, hgrn_lb_logits, hgrn_norm_g, w_branch, w_out, ln1_g, ln1_b, ffn_w_up, ffn_conv_w, ffn_conv_b, ffn_w_down, ln2_g, ln2_b, ple_w_proj, ple_w_gate, loss_target, m_w_in, m_sgu_w_s, m_sgu_b_s, m_sgu_norm_g, m_sgu_norm_b, m_hgrn_lb_logits, m_hgrn_norm_g, m_w_branch, m_w_out, m_ln1_g, m_ln1_b, m_ffn_w_up, m_ffn_conv_w, m_ffn_conv_b, m_ffn_w_down, m_ln2_g, m_ln2_b, m_ple_w_proj, m_ple_w_gate, v_w_in, v_sgu_w_s, v_sgu_b_s, v_sgu_norm_g, v_sgu_norm_b, v_hgrn_lb_logits, v_hgrn_norm_g, v_w_branch, v_w_out, v_ln1_g, v_ln1_b, v_ffn_w_up, v_ffn_conv_w, v_ffn_conv_b, v_ffn_w_down, v_ln2_g, v_ln2_b, v_ple_w_proj, v_ple_w_gate):
    weights = dict(w_in=w_in, sgu_w_s=sgu_w_s, sgu_b_s=sgu_b_s, sgu_norm_g=sgu_norm_g, sgu_norm_b=sgu_norm_b,
                   hgrn_lb_logits=hgrn_lb_logits, hgrn_norm_g=hgrn_norm_g, w_branch=w_branch, w_out=w_out,
                   ln1_g=ln1_g, ln1_b=ln1_b, ffn_w_up=ffn_w_up, ffn_conv_w=ffn_conv_w, ffn_conv_b=ffn_conv_b,
                   ffn_w_down=ffn_w_down, ln2_g=ln2_g, ln2_b=ln2_b, ple_w_proj=ple_w_proj, ple_w_gate=ple_w_gate)
    mom_m = dict(w_in=m_w_in, sgu_w_s=m_sgu_w_s, sgu_b_s=m_sgu_b_s, sgu_norm_g=m_sgu_norm_g, sgu_norm_b=m_sgu_norm_b,
                 hgrn_lb_logits=m_hgrn_lb_logits, hgrn_norm_g=m_hgrn_norm_g, w_branch=m_w_branch, w_out=m_w_out,
                 ln1_g=m_ln1_g, ln1_b=m_ln1_b, ffn_w_up=m_ffn_w_up, ffn_conv_w=m_ffn_conv_w, ffn_conv_b=m_ffn_conv_b,
                 ffn_w_down=m_ffn_w_down, ln2_g=m_ln2_g, ln2_b=m_ln2_b, ple_w_proj=m_ple_w_proj, ple_w_gate=m_ple_w_gate)
    mom_v = dict(w_in=v_w_in, sgu_w_s=v_sgu_w_s, sgu_b_s=v_sgu_b_s, sgu_norm_g=v_sgu_norm_g, sgu_norm_b=v_sgu_norm_b,
                 hgrn_lb_logits=v_hgrn_lb_logits, hgrn_norm_g=v_hgrn_norm_g, w_branch=v_w_branch, w_out=v_w_out,
                 ln1_g=v_ln1_g, ln1_b=v_ln1_b, ffn_w_up=v_ffn_w_up, ffn_conv_w=v_ffn_conv_w, ffn_conv_b=v_ffn_conv_b,
                 ffn_w_down=v_ffn_w_down, ln2_g=v_ln2_g, ln2_b=v_ln2_b, ple_w_proj=v_ple_w_proj, ple_w_gate=v_ple_w_gate)
    d, f = D_MODEL, D_FF
    me = _my_index()

    comm = _StepExchanges(w_in[0].astype(BF16),
                          [w_branch[0].astype(BF16), w_out[0].astype(BF16), ffn_w_up[0].astype(BF16),
                           ffn_w_down[0].astype(BF16), ple_w_proj[0].astype(BF16), ple_w_gate[0].astype(BF16), ffn_conv_w[0]])
    small = dict(sgu_w_s=sgu_w_s[0], sgu_b_s=sgu_b_s[0], sgu_norm_g=sgu_norm_g, sgu_norm_b=sgu_norm_b,
                 lb_logits=hgrn_lb_logits, hgrn_norm_g=hgrn_norm_g, ln1_g=ln1_g, ln1_b=ln1_b, ln2_g=ln2_g, ln2_b=ln2_b,
                 conv_b=ffn_conv_b)
    grad_x = _local_step(x[0], p[0, 0], loss_target[0], small, comm)

    out = {}

    def adam(name, parts8, shape2d):
        w2, m2, v2 = (a.reshape(shape2d) for a in (weights[name], mom_m[name], mom_v[name]))
        res = _adam_call(w2, m2, v2, parts8.reshape(parts8.shape[:1] + shape2d), "adam_" + name)
        out[name] = tuple(r.reshape(weights[name].shape) for r in res)

    adam("w_in", comm.recv_in, (d, d))
    adam("w_branch", comm.recv[0], (256, d))
    adam("w_out", comm.recv[1], (128, d))
    adam("ffn_w_up", comm.recv[2], (d, UP_COLS))
    adam("ffn_w_down", comm.recv[3], (N_TAPS, d))
    adam("ple_w_proj", comm.recv[4], (256, 128))
    adam("ple_w_gate", comm.recv[5], (128, d))

    def adam_small(names, extra_w, extra_m, extra_v, extra_shapes, parts8, rows, label):
        pk = lambda src, extra: _pack([src[n] for n in names] + extra)[0]
        res = _adam_call(pk(weights, extra_w), pk(mom_m, extra_m), pk(mom_v, extra_v), parts8, label)
        shapes = [weights[n].shape for n in names] + extra_shapes
        unpacked = [_unpack(r, shapes, rows) for r in res]
        for j, n in enumerate(names):
            out[n] = tuple(u[j] for u in unpacked)
        return [[u[len(names) + j] for u in unpacked] for j in range(len(extra_shapes))]

    blank = jnp.zeros((8, LANE), F32)
    taps, loss_rows = adam_small(
        _SMALL_EARLY, [_place_taps(ffn_conv_w[0], me, f), blank], [_place_taps(m_ffn_conv_w[0], me, f), blank],
        [_place_taps(v_ffn_conv_w[0], me, f), blank + 1.0], [(3, f), (8, LANE)], comm.small_early, comm.rows_early,
        "adam_small_early")
    adam_small(_SMALL_LATE, [], [], [], [], comm.small_late, comm.rows_late, "adam_small_late")
    out["ffn_conv_w"] = tuple(lax.dynamic_slice_in_dim(u, me * N_TAPS, N_TAPS, axis=1)[None] for u in taps)
    loss = loss_rows[0][0, 0]

    order = ["w_in", "sgu_w_s", "sgu_b_s", "sgu_norm_g", "sgu_norm_b", "hgrn_lb_logits", "hgrn_norm_g", "w_branch", "w_out",
             "ln1_g", "ln1_b", "ffn_w_up", "ffn_conv_w", "ffn_conv_b", "ffn_w_down", "ln2_g", "ln2_b", "ple_w_proj", "ple_w_gate"]
    return (loss, grad_x[None], *[out[n][0] for n in order], *[out[n][1] for n in order],
            *[out[n][2] for n in order], *[out[n][3] for n in order])


def _place_taps(shard, me, f):
    return lax.dynamic_update_slice_in_dim(jnp.zeros((3, f), F32), shard, me * N_TAPS, axis=1)
```

```python
import functools
import math

import jax
import jax.numpy as jnp
from jax import lax
from jax.experimental import pallas as pl
from jax.experimental.pallas import tpu as pltpu

F32 = jnp.float32
BF16 = jnp.bfloat16

N_DEV = 8
N_CHIP = 4
D_MODEL = 1024
CHUNK = 64
SUB = 16
SGU_BLOCK = 128
GROUPS = 8
HEAD = 128
HEADS = 8
CHUNKS_PER_STEP = 2
HEAD_UNROLL = 8
D_FF = 2816
LN_EPS = 1e-5
RMS_EPS = 1e-6
ALPHA = 2.0 ** 0.25
GELU_K = math.sqrt(2.0 / math.pi)
GELU_C = 0.044715
NEG = -1e30
ADAM_LR, ADAM_B1, ADAM_B2, ADAM_EPS, ADAM_WD, ADAM_STEP = 0.001, 0.9, 0.999, 1e-08, 0.01, 10
LANE = 128
SLOT_Q, SLOT_F, SLOT_I, SLOT_OG, SLOT_U, SLOT_V, SLOT_GA, SLOT_GB = range(8)


def _slot_of_group(k):
    return jnp.where(k < 2, k + 4, jnp.where(k < 6, k - 2, k))


MIB = 1024 * 1024
VMEM_V7X = 64 * MIB
VMEM_FLOOR = 32 * MIB
MM_TILES = (1024, 1408, 512, 256, 128)


def _params(sem, need=0):
    limit = min(max(need + need // 4, VMEM_FLOOR), VMEM_V7X - 4 * MIB)
    return pltpu.CompilerParams(dimension_semantics=sem, vmem_limit_bytes=limit)


def _pick(n, prefs):
    for t in prefs:
        if n % t == 0:
            return t
    return n


def _gelu(x):
    return 0.5 * x * (1.0 + jnp.tanh(GELU_K * (x + GELU_C * x * x * x)))


def _gelu_and_grad(x):
    x2 = x * x
    t = jnp.tanh(GELU_K * x * (1.0 + GELU_C * x2))
    half = 0.5 * (1.0 + t)
    return x * half, half + 0.5 * x * (1.0 - t * t) * GELU_K * (1.0 + 3.0 * GELU_C * x2)


def _silu_grad(x, s):
    return s * (1.0 + x * (1.0 - s))


def _dot(a, b):
    return jnp.dot(a.astype(BF16), b.astype(BF16), preferred_element_type=F32)


def _dot_nt(a, b):
    return lax.dot_general(a.astype(BF16), b.astype(BF16), (((1,), (1,)), ((), ())), preferred_element_type=F32)


def _dot_tn(a, b):
    return lax.dot_general(a.astype(BF16), b.astype(BF16), (((0,), (0,)), ((), ())), preferred_element_type=F32)


def _mean(x):
    return jnp.mean(x, axis=-1, keepdims=True)


def _sum0(x):
    return jnp.sum(x, axis=0, keepdims=True)


def _mm(a, b, *, out_dtype, name, trans_b=False, reduce_b=False, adds=(), ex=None):
    squeeze = b.ndim == 2
    a3 = a if a.ndim == 3 else a[None]
    b3 = b if b.ndim == 3 else b[None]
    ba, m, k = a3.shape
    bb = b3.shape[0]
    n = b3.shape[1] if trans_b else b3.shape[2]
    tm = _pick(m, MM_TILES)
    tn = _pick(n, MM_TILES)
    if reduce_b:
        bo, steps = 1, bb
        a_map = lambda o, i, j, r: (r if ba > 1 else 0, i, 0)
        b_map = (lambda o, i, j, r: (r, j, 0)) if trans_b else (lambda o, i, j, r: (r, 0, j))
    else:
        bo, steps = bb, 1
        a_map = lambda o, i, j, r: (o if ba > 1 else 0, i, 0)
        b_map = (lambda o, i, j, r: (o, j, 0)) if trans_b else (lambda o, i, j, r: (o, 0, j))
    o_map = lambda o, i, j, r: (o, i, j)
    add_arrays = [x if x.ndim == 3 else x[None] for x, _ in adds]
    add_scales = [s for _, s in adds]
    n_add = len(adds)
    dot = _dot_nt if trans_b else _dot

    def finish(acc, add_refs, o_ref):
        for ref, s in zip(add_refs, add_scales):
            acc = acc + s * ref[...].astype(F32)
        o_ref[...] = acc.astype(o_ref.dtype)

    grid = (bo, m // tm, n // tn, steps)

    def body(*refs):
        ins, (o_ref,), scratch, xrefs = _split_refs(refs, 2 + n_add, 1, 1 if reduce_b else 0, ex)
        a_ref, b_ref, add_refs = ins[0], ins[1], ins[2:]
        step = ((pl.program_id(0) * grid[1] + pl.program_id(1)) * grid[2] + pl.program_id(2)) * grid[3] + pl.program_id(3)
        if ex:
            @pl.when(step == 0)
            def _():
                ex.start(*xrefs)

        if reduce_b:
            acc, = scratch
            r = pl.program_id(3)

            @pl.when(r == 0)
            def _():
                acc[...] = jnp.zeros_like(acc)

            acc[...] += dot(a_ref[...], b_ref[...])

            @pl.when(r == steps - 1)
            def _():
                finish(acc[...], add_refs, o_ref)
        else:
            finish(dot(a_ref[...], b_ref[...]), add_refs, o_ref)

        if ex:
            @pl.when(step == math.prod(grid) - 1)
            def _():
                ex.finish(*xrefs)

    b_block = (None, tn, k) if trans_b else (None, k, tn)
    out_bytes = tm * tn * jnp.dtype(out_dtype).itemsize
    need = 2 * (tm * k * a3.dtype.itemsize + k * tn * b3.dtype.itemsize + out_bytes + n_add * tm * tn * 4)
    need += 2 * tm * tn * 4
    sem = ("arbitrary",) * 4 if ex else ("parallel", "parallel", "parallel", "arbitrary")
    res = pl.pallas_call(
        body,
        grid=grid,
        in_specs=[pl.BlockSpec((None, tm, k), a_map), pl.BlockSpec(b_block, b_map)]
        + [pl.BlockSpec((None, tm, tn), o_map) for _ in adds] + (ex.in_specs if ex else []),
        out_specs=[pl.BlockSpec((None, tm, tn), o_map)] + (ex.out_specs if ex else []),
        out_shape=[jax.ShapeDtypeStruct((bo, m, n), out_dtype)] + (ex.out_shape if ex else []),
        scratch_shapes=([pltpu.VMEM((tm, tn), F32)] if reduce_b else []) + (ex.scratch if ex else []),
        compiler_params=_params(sem, need),
        name=name,
    )(a3, b3, *add_arrays, *(ex.arrays if ex else []))
    out = res[0][0] if (reduce_b or squeeze) else res[0]
    return (out, res[1:]) if ex else out


def _mm_tn(a, b, *, out_dtype, name):
    squeeze = b.ndim == 2
    b3 = b if b.ndim == 3 else b[None]
    t, m = a.shape
    bb, _, n = b3.shape
    tm = _pick(m, MM_TILES)
    tn = _pick(n, MM_TILES)
    tt = _pick(t, (1024, 512, 256, 128))
    steps = t // tt
    need = 2 * (tt * tm * a.dtype.itemsize + tt * tn * b3.dtype.itemsize + tm * tn * jnp.dtype(out_dtype).itemsize)
    need += 2 * tm * tn * 4

    def body(a_ref, b_ref, o_ref, acc):
        r = pl.program_id(3)

        @pl.when(r == 0)
        def _():
            acc[...] = jnp.zeros_like(acc)

        acc[...] += _dot_tn(a_ref[...], b_ref[...])

        @pl.when(r == steps - 1)
        def _():
            o_ref[...] = acc[...].astype(o_ref.dtype)

    out = pl.pallas_call(
        body,
        grid=(bb, m // tm, n // tn, steps),
        in_specs=[pl.BlockSpec((tt, tm), lambda o, i, j, r: (r, i)),
                  pl.BlockSpec((None, tt, tn), lambda o, i, j, r: (o, r, j))],
        out_specs=pl.BlockSpec((None, tm, tn), lambda o, i, j, r: (o, i, j)),
        out_shape=jax.ShapeDtypeStruct((bb, m, n), out_dtype),
        scratch_shapes=[pltpu.VMEM((tm, tn), F32)],
        compiler_params=_params(("parallel", "parallel", "parallel", "arbitrary"), need),
        name=name,
    )(a, b3)
    return out[0] if squeeze else out


def _rowwise(fn, rows, consts, row_outs, acc_outs, *, tile, name):
    first = rows[0][0] if isinstance(rows[0], tuple) else rows[0]
    t = first.shape[-2]
    steps = t // tile
    arrays, in_specs = [], []
    for r in rows:
        if isinstance(r, tuple) and isinstance(r[1], pl.BlockSpec):
            arrays.append(r[0])
            in_specs.append(r[1])
        elif isinstance(r, tuple):
            arr, bidx = r
            arrays.append(arr)
            in_specs.append(pl.BlockSpec((None, tile, arr.shape[-1]), functools.partial(lambda i, b: (b, i, 0), b=bidx)))
        else:
            arrays.append(r)
            in_specs.append(pl.BlockSpec((tile, r.shape[-1]), lambda i: (i, 0)))
    for c in consts:
        arrays.append(c)
        in_specs.append(pl.BlockSpec(c.shape, lambda i: (0, 0)))
    n_in, n_row = len(arrays), len(row_outs)
    out_shape, out_specs = [], []
    for ro in row_outs:
        if ro[0] == "stack":
            _, cnt, blk, total, w, dt = ro
            out_shape.append(jax.ShapeDtypeStruct((total, t, w), dt))
            out_specs.append(pl.BlockSpec((cnt, tile, w), functools.partial(lambda i, b: (b, i, 0), b=blk)))
        else:
            w, dt = ro
            out_shape.append(jax.ShapeDtypeStruct((t, w), dt))
            out_specs.append(pl.BlockSpec((tile, w), lambda i: (i, 0)))
    out_shape += [jax.ShapeDtypeStruct(s, F32) for s in acc_outs]
    out_specs += [pl.BlockSpec(s, lambda i: (0, 0)) for s in acc_outs]
    blocks = [math.prod(d for d in sp.block_shape if d) * arr.dtype.itemsize for sp, arr in zip(in_specs, arrays)]
    blocks += [math.prod(d for d in sp.block_shape if d) * jnp.dtype(sh.dtype).itemsize
               for sp, sh in zip(out_specs, out_shape)]
    need = 2 * sum(blocks) + 6 * tile * max(a.shape[-1] for a in arrays) * 4

    def body(*refs):
        ins, outs = refs[:n_in], refs[n_in:]
        i = pl.program_id(0)
        res = fn(i, steps, *[r[...] for r in ins])
        res = res if isinstance(res, (tuple, list)) else (res,)
        for ref, val in zip(outs[:n_row], res[:n_row]):
            ref[...] = val.astype(ref.dtype)
        if acc_outs:
            @pl.when(i == 0)
            def _():
                for ref in outs[n_row:]:
                    ref[...] = jnp.zeros_like(ref)

            for ref, val in zip(outs[n_row:], res[n_row:]):
                ref[...] += val

    return pl.pallas_call(
        body,
        grid=(steps,),
        in_specs=in_specs,
        out_specs=out_specs,
        out_shape=out_shape,
        compiler_params=_params(("arbitrary",), need),
        name=name,
    )(*arrays)


def _ln_stats(z):
    mu = _mean(z)
    zc = z - mu
    rstd = lax.rsqrt(_mean(zc * zc) + LN_EPS)
    return zc * rstd, rstd


def _ln_bwd(dy, xhat, rstd, g):
    dxh = dy * g
    return rstd * (dxh - _mean(dxh) - xhat * _mean(dxh * xhat))


def _ride(ex, xrefs, step, steps):
    if not ex:
        return
    for at, act in ((0, ex.start), (steps - 2, ex.relay), (steps - 1, ex.finish)):
        @pl.when(step == at)
        def _(act=act):
            act(*xrefs)


def _sgu_fwd(h, wm, bs_t, g_v, b_v, ex=None):
    t = h.shape[1]
    steps = t // SGU_BLOCK

    def body(*refs):
        (u_ref, v_ref, wm_ref, bs_ref, g_ref, b_ref), (y_ref,), _, xrefs = _split_refs(refs, 6, 1, 0, ex)
        xhat, _ = _ln_stats(_gelu(v_ref[...]))
        vn = (xhat * g_ref[...] + b_ref[...]).astype(BF16)
        gu = _gelu(u_ref[...])
        for g in range(GROUPS):
            sl = slice(g * HEAD, (g + 1) * HEAD)
            mixed = _dot(wm_ref[g], vn[:, sl]) + bs_ref[:, g:g + 1]
            y_ref[:, sl] = (gu[:, sl] * mixed).astype(BF16)
        _ride(ex, xrefs, pl.program_id(0), steps)

    blk = lambda b: pl.BlockSpec((None, SGU_BLOCK, D_MODEL), functools.partial(lambda i, b: (b, i, 0), b=b))
    whole = lambda s: pl.BlockSpec(s, lambda i: (0,) * len(s))
    res = pl.pallas_call(
        body,
        grid=(steps,),
        in_specs=[blk(SLOT_U), blk(SLOT_V), whole(wm.shape), whole(bs_t.shape), whole(g_v.shape), whole(b_v.shape)]
        + (ex.in_specs if ex else []),
        out_specs=[pl.BlockSpec((SGU_BLOCK, D_MODEL), lambda i: (i, 0))] + (ex.out_specs if ex else []),
        out_shape=[jax.ShapeDtypeStruct((t, D_MODEL), BF16)] + (ex.out_shape if ex else []),
        scratch_shapes=ex.scratch if ex else [],
        compiler_params=_params(("arbitrary",)),
        name="sgu_fwd",
    )(h, h, wm, bs_t, g_v, b_v, *(ex.arrays if ex else []))
    return res[0], res[1:]


def _sgu_bwd(h, dy, dh, wm, wm_t, bs_t, g_v, b_v):
    t = h.shape[1]

    def body(u_ref, v_ref, dy_ref, dh_in, wm_ref, wmt_ref, bs_ref, g_ref, b_ref,
             duv_ref, dw_ref, dbs_ref, dg_ref, db_ref, dvn_ref):
        del dh_in
        du_ref, dv_ref = duv_ref.at[0], duv_ref.at[1]
        i = pl.program_id(0)

        @pl.when(i == 0)
        def _():
            dw_ref[...] = jnp.zeros_like(dw_ref)
            dbs_ref[...] = jnp.zeros_like(dbs_ref)
            dg_ref[...] = jnp.zeros_like(dg_ref)
            db_ref[...] = jnp.zeros_like(db_ref)

        u = u_ref[...]
        v = v_ref[...]
        gv, gvp = _gelu_and_grad(v)
        xhat, rstd = _ln_stats(gv)
        vn = (xhat * g_ref[...] + b_ref[...]).astype(BF16)
        gu, gup = _gelu_and_grad(u)
        lane = lax.broadcasted_iota(jnp.int32, (SGU_BLOCK, LANE), 1)
        dbs = jnp.zeros((SGU_BLOCK, LANE), F32)
        for g in range(GROUPS):
            sl = slice(g * HEAD, (g + 1) * HEAD)
            vn_g = vn[:, sl]
            mixed = _dot(wm_ref[g], vn_g) + bs_ref[:, g:g + 1]
            dy_g = dy_ref[:, sl]
            du_ref[:, sl] = (dy_g * mixed * gup[:, sl]).astype(BF16)
            dmix = dy_g * gu[:, sl]
            dmb = dmix.astype(BF16)
            dvn_ref[:, sl] = _dot(wmt_ref[g], dmb)
            dw_ref[g] += _dot_nt(dmb, vn_g)
            dbs = dbs + jnp.where(lane == g, jnp.sum(dmix, axis=1, keepdims=True), 0.0)
        dbs_ref[...] += dbs
        dvn = dvn_ref[...]
        dg_ref[...] += _sum0(dvn * xhat)
        db_ref[...] += _sum0(dvn)
        dv_ref[...] = (_ln_bwd(dvn, xhat, rstd, g_ref[...]) * gvp).astype(BF16)

    blk = lambda b: pl.BlockSpec((None, SGU_BLOCK, D_MODEL), functools.partial(lambda i, b: (b, i, 0), b=b))
    row = pl.BlockSpec((SGU_BLOCK, D_MODEL), lambda i: (i, 0))
    whole = lambda s: pl.BlockSpec(s, lambda i: (0,) * len(s))
    vec = (1, D_MODEL)
    return pl.pallas_call(
        body,
        grid=(t // SGU_BLOCK,),
        in_specs=[blk(SLOT_U), blk(SLOT_V), row, pl.BlockSpec(memory_space=pl.ANY),
                  whole(wm.shape), whole(wm_t.shape), whole(bs_t.shape), whole(vec), whole(vec)],
        out_specs=[pl.BlockSpec((2, SGU_BLOCK, D_MODEL), lambda i: (SLOT_U // 2, i, 0)),
                   whole(wm.shape), whole((SGU_BLOCK, LANE)), whole(vec), whole(vec)],
        out_shape=[jax.ShapeDtypeStruct(dh.shape, BF16),
                   jax.ShapeDtypeStruct(wm.shape, F32), jax.ShapeDtypeStruct((SGU_BLOCK, LANE), F32),
                   jax.ShapeDtypeStruct(vec, F32), jax.ShapeDtypeStruct(vec, F32)],
        scratch_shapes=[pltpu.VMEM((SGU_BLOCK, D_MODEL), F32)],
        input_output_aliases={3: 0},
        compiler_params=_params(("arbitrary",)),
        name="sgu_bwd",
    )(h, h, dy, dh, wm, wm_t, bs_t, g_v, b_v)


def _split3(x):
    hi = x.astype(BF16)
    r1 = x - hi.astype(F32)
    mid = r1.astype(BF16)
    lo = (r1 - mid.astype(F32)).astype(BF16)
    return hi, mid, lo


def _tri_matmul(tri, x):
    hi, mid, lo = _split3(x)
    dot = lambda p: jnp.dot(tri, p, preferred_element_type=F32)
    return dot(hi) + dot(mid) + dot(lo)


def _lower_bound(logits):
    l0, l1 = logits[0:1, :], logits[1:2, :]
    mx = jnp.maximum(l0, l1)
    e0, e1 = jnp.exp(l0 - mx), jnp.exp(l1 - mx)
    return e0 / (e0 + e1)


def _hgrn_gates(q_raw, f_raw, lb):
    q = q_raw * jax.nn.sigmoid(q_raw)
    sig = jax.nn.sigmoid(f_raw)
    f = lb + (1.0 - lb) * sig
    row = lax.broadcasted_iota(jnp.int32, (CHUNK, CHUNK), 0)
    col = lax.broadcasted_iota(jnp.int32, (CHUNK, CHUNK), 1)
    c = _tri_matmul((row >= col).astype(BF16), jnp.log(f))
    return q, sig, f, 1.0 - f, c


def _offdiag_terms(qh, kh, ch, tb):
    rows = slice(tb * SUB, (tb + 1) * SUB)
    r = ch[tb * SUB - 1:tb * SUB, :]
    eqh = jnp.exp(ch[rows] - r)
    ekh = jnp.exp(jnp.minimum(r - ch, 0.0))
    return rows, eqh, qh[rows] * eqh, ekh, kh * ekh


def _diag_decay(cb, s, trow):
    return jnp.exp(jnp.where(trow >= s, cb - cb[s:s + 1, :], NEG))


def _split_refs(refs, n_in, n_out, n_scratch, ex):
    nx = ex.n if ex else 0
    ins, refs = refs[:n_in], refs[n_in:]
    xsrc, refs = refs[:nx], refs[nx:]
    outs, refs = refs[:n_out], refs[n_out:]
    xout, refs = refs[:nx], refs[nx:]
    return ins, outs, refs[:n_scratch], (xsrc, xout, refs[n_scratch:])


def _hgrn_fwd(h, logits, g_norm, ex=None):
    t = h.shape[1]
    nc = t // CHUNK
    per_step = CHUNKS_PER_STEP if nc % CHUNKS_PER_STEP == 0 else 1
    steps = nc // per_step

    def body(*refs):
        ins, outs, scratch, xrefs = _split_refs(refs, 3, 3, 4, ex)
        hgrn_ref, lg_ref, gn_ref = ins
        y_ref, o_ref, sall_ref = outs
        st_ref, q_s, k_s, c_s = scratch

        @pl.when(pl.program_id(0) == 0)
        def _():
            st_ref[...] = jnp.zeros_like(st_ref)

        lb = _lower_bound(lg_ref[...])
        for sub in range(per_step):
            rows = pl.ds(sub * CHUNK, CHUNK)
            chunk(lb, sall_ref.at[sub], *(r.at[rows] for r in (hgrn_ref.at[SLOT_Q], hgrn_ref.at[SLOT_F], hgrn_ref.at[SLOT_I],
                                                               hgrn_ref.at[SLOT_OG], y_ref, o_ref)),
                  gn_ref, st_ref, q_s.at[sub], k_s.at[sub], c_s.at[sub])
        _ride(ex, xrefs, pl.program_id(0), steps)

    def chunk(lb, sall_ref, q_ref, f_ref, i_ref, og_ref, y_ref, o_ref, gn_ref, st_ref, q_s, k_s, c_s):
        q, _, _, k, c = _hgrn_gates(q_ref[...], f_ref[...], lb)
        q_s[...] = q
        k_s[...] = k
        c_s[...] = c
        col64 = lax.broadcasted_iota(jnp.int32, (SUB, CHUNK), 1)
        trow = lax.broadcasted_iota(jnp.int32, (SUB, HEAD), 0)

        def head(hd, carry):
            sl = pl.ds(pl.multiple_of(hd * HEAD, HEAD), HEAD)
            qh, kh, ch, ih = q_s[:, sl], k_s[:, sl], c_s[:, sl], i_ref[:, sl]
            st = st_ref[hd]
            sall_ref[hd] = st
            c_last = ch[CHUNK - 1:CHUNK, :]
            o = _dot_nt(qh * jnp.exp(ch), st)
            st_ref[hd] = st * jnp.exp(c_last) + _dot_tn(ih, kh * jnp.exp(c_last - ch))
            a_rows = [jnp.zeros((SUB, CHUNK), F32)]
            for tb in range(1, CHUNK // SUB):
                _, _, q_hat, _, k_hat = _offdiag_terms(qh, kh, ch, tb)
                a_rows.append(jnp.where(col64 < tb * SUB, _dot_nt(q_hat, k_hat), 0.0))
            o = o + _dot(jnp.concatenate(a_rows, axis=0), ih)
            o_rows = []
            for b in range(CHUNK // SUB):
                rows = slice(b * SUB, (b + 1) * SUB)
                qb, cb, kb, ib = qh[rows], ch[rows], kh[rows], ih[rows]
                ob = jnp.zeros((SUB, HEAD), F32)
                for s in range(SUB):
                    a = jnp.sum(qb * _diag_decay(cb, s, trow) * kb[s:s + 1, :], axis=1, keepdims=True)
                    ob = ob + a * ib[s:s + 1, :]
                o_rows.append(ob)
            o = o + jnp.concatenate(o_rows, axis=0)
            o_ref[:, sl] = o
            og = og_ref[:, sl]
            on = o * lax.rsqrt(_mean(o * o) + RMS_EPS)
            y_ref[:, sl] = (on * gn_ref[:, sl] * (og * jax.nn.sigmoid(og))).astype(BF16)
            return carry

        lax.fori_loop(0, HEADS, head, 0, unroll=HEAD_UNROLL)

    rows = per_step * CHUNK
    row = pl.BlockSpec((rows, D_MODEL), lambda n: (n, 0))
    whole = lambda s: pl.BlockSpec(s, lambda n: (0,) * len(s))
    res = pl.pallas_call(
        body,
        grid=(steps,),
        in_specs=[pl.BlockSpec((4, rows, D_MODEL), lambda n: (SLOT_Q // 4, n, 0)), whole(logits.shape), whole(g_norm.shape)]
        + (ex.in_specs if ex else []),
        out_specs=[row, row, pl.BlockSpec((per_step, HEADS, HEAD, HEAD), lambda n: (n, 0, 0, 0))]
        + (ex.out_specs if ex else []),
        out_shape=[jax.ShapeDtypeStruct((t, D_MODEL), BF16), jax.ShapeDtypeStruct((t, D_MODEL), F32),
                   jax.ShapeDtypeStruct((nc, HEADS, HEAD, HEAD), F32)] + (ex.out_shape if ex else []),
        scratch_shapes=[pltpu.VMEM((HEADS, HEAD, HEAD), F32)] + [pltpu.VMEM((per_step, CHUNK, D_MODEL), F32)] * 3
        + (ex.scratch if ex else []),
        compiler_params=_params(("arbitrary",)),
        name="hgrn_fwd",
    )(h, logits, g_norm, *(ex.arrays if ex else []))
    return res[0], res[1], res[2], res[3:]


def _hgrn_bwd(h, o_all, dy, states, dh, logits, g_norm, ex=None):
    t = h.shape[1]
    nc = t // CHUNK

    def body(*refs):
        ins, outs, scratch, xrefs = _split_refs(refs, 7, 3, 8, ex)
        hgrn_ref, o_ref, dy_ref, sall_ref, _, lg_ref, gn_ref = ins
        q_ref, f_ref, i_ref, og_ref = (hgrn_ref.at[s] for s in (SLOT_Q, SLOT_F, SLOT_I, SLOT_OG))
        dqfio_ref, dlg_ref, dgn_ref = outs
        dst_ref, dlb_ref, q_s, k_s, c_s, dq_s, dk_s, dc_s = scratch
        dq_ref, df_ref, di_ref, dog_ref = (dqfio_ref.at[s] for s in (SLOT_Q, SLOT_F, SLOT_I, SLOT_OG))
        n = pl.program_id(0)

        @pl.when(n == 0)
        def _():
            dst_ref[...] = jnp.zeros_like(dst_ref)
            dlb_ref[...] = jnp.zeros_like(dlb_ref)
            dgn_ref[...] = jnp.zeros_like(dgn_ref)
            if ex:
                ex.start(*xrefs)

        lb = _lower_bound(lg_ref[...])
        q_raw = q_ref[...]
        q, sig, f, k, c = _hgrn_gates(q_raw, f_ref[...], lb)
        q_s[...] = q
        k_s[...] = k
        c_s[...] = c
        col64 = lax.broadcasted_iota(jnp.int32, (SUB, CHUNK), 1)
        trow = lax.broadcasted_iota(jnp.int32, (SUB, HEAD), 0)
        row64 = lax.broadcasted_iota(jnp.int32, (CHUNK, HEAD), 0)

        def head(hd, carry):
            sl = pl.ds(pl.multiple_of(hd * HEAD, HEAD), HEAD)
            qh, kh, ch, ih = q_s[:, sl], k_s[:, sl], c_s[:, sl], i_ref[:, sl]
            st = sall_ref[hd]
            dst = dst_ref[hd]
            oh, dyh, og, gn = o_ref[:, sl], dy_ref[:, sl], og_ref[:, sl], gn_ref[:, sl]
            sg = jax.nn.sigmoid(og)
            sil = og * sg
            rms = lax.rsqrt(_mean(oh * oh) + RMS_EPS)
            on = oh * rms
            dog_ref[:, sl] = (dyh * on * gn * _silu_grad(og, sg)).astype(BF16)
            dgn_ref[:, sl] += _sum0(dyh * on * sil)
            don = dyh * gn * sil
            do = rms * (don - on * _mean(don * on))
            dob = do.astype(BF16)

            c_last = ch[CHUNK - 1:CHUNK, :]
            eq = jnp.exp(ch)
            q_til = qh * eq
            ekl = jnp.exp(c_last - ch)
            k_til = kh * ekl
            ecl = jnp.exp(c_last)
            dq_til = _dot(dob, st)
            dk_til = _dot(ih, dst)
            di = _dot_nt(k_til, dst)
            dc_last = _sum0(dk_til * k_til) + _sum0(dst * st) * ecl
            dst_ref[hd] = _dot_tn(dob, q_til) + dst * ecl
            dq = dq_til * eq
            dc = dq_til * q_til - dk_til * k_til
            dk = dk_til * ekl

            da_full = _dot_nt(dob, ih)
            a_rows = [jnp.zeros((SUB, CHUNK), F32)]
            dq_rows = [jnp.zeros((SUB, HEAD), F32)]
            dc_rows = [jnp.zeros((SUB, HEAD), F32)]
            for tb in range(1, CHUNK // SUB):
                rows, eqh, q_hat, ekh, k_hat = _offdiag_terms(qh, kh, ch, tb)
                keep = col64 < tb * SUB
                a_rows.append(jnp.where(keep, _dot_nt(q_hat, k_hat), 0.0))
                da = jnp.where(keep, da_full[rows], 0.0)
                dq_hat = _dot(da, k_hat)
                dk_hat = _dot_tn(da, q_hat)
                dq_rows.append(dq_hat * eqh)
                dc_rows.append(dq_hat * q_hat)
                dk = dk + dk_hat * ekh
                dc = dc - dk_hat * k_hat
            di = di + _dot_tn(jnp.concatenate(a_rows, axis=0), dob)

            dk_rows, di_rows = [], []
            for b in range(CHUNK // SUB):
                rows = slice(b * SUB, (b + 1) * SUB)
                qb, cb, kb, ib, dob_ = qh[rows], ch[rows], kh[rows], ih[rows], do[rows]
                dq_diag = jnp.zeros((SUB, HEAD), F32)
                dk_diag = jnp.zeros((SUB, HEAD), F32)
                di_diag = jnp.zeros((SUB, HEAD), F32)
                for s in range(SUB):
                    ks = kb[s:s + 1, :]
                    dec = _diag_decay(cb, s, trow)
                    a = jnp.sum(qb * dec * ks, axis=1, keepdims=True)
                    gk = jnp.sum(dob_ * ib[s:s + 1, :], axis=1, keepdims=True) * dec
                    dq_diag = dq_diag + gk * ks
                    dk_diag = dk_diag + jnp.where(trow == s, _sum0(gk * qb), 0.0)
                    di_diag = di_diag + jnp.where(trow == s, _sum0(a * dob_), 0.0)
                dq_rows[b] = dq_rows[b] + dq_diag
                dc_rows[b] = dc_rows[b] + qb * dq_diag - kb * dk_diag
                dk_rows.append(dk_diag)
                di_rows.append(di_diag)
            dq = dq + jnp.concatenate(dq_rows, axis=0)
            dk = dk + jnp.concatenate(dk_rows, axis=0)
            dc = dc + jnp.concatenate(dc_rows, axis=0) + jnp.where(row64 == CHUNK - 1, dc_last, 0.0)
            di_ref[:, sl] = (di + jnp.concatenate(di_rows, axis=0)).astype(BF16)
            dq_s[:, sl] = dq
            dk_s[:, sl] = dk
            dc_s[:, sl] = dc
            return carry

        lax.fori_loop(0, HEADS, head, 0, unroll=HEAD_UNROLL)

        row = lax.broadcasted_iota(jnp.int32, (CHUNK, CHUNK), 0)
        col = lax.broadcasted_iota(jnp.int32, (CHUNK, CHUNK), 1)
        dlf = _tri_matmul((row <= col).astype(BF16), dc_s[...])
        df = dlf / f - dk_s[...]
        dlb_ref[...] += _sum0(df * (1.0 - sig))
        df_ref[...] = (df * (1.0 - lb) * sig * (1.0 - sig)).astype(BF16)
        dq_ref[...] = (dq_s[...] * _silu_grad(q_raw, jax.nn.sigmoid(q_raw))).astype(BF16)

        @pl.when(n == nc - 1)
        def _():
            d0 = dlb_ref[...] * lb * (1.0 - lb)
            dlg_ref[0:1, :] = d0
            dlg_ref[1:2, :] = -d0
            if ex:
                ex.finish(*xrefs)

    rev = lambda n: nc - 1 - n
    slots = pl.BlockSpec((4, CHUNK, D_MODEL), lambda n: (SLOT_Q // 4, rev(n), 0))
    row = pl.BlockSpec((CHUNK, D_MODEL), lambda n: (rev(n), 0))
    whole = lambda s: pl.BlockSpec(s, lambda n: (0,) * len(s))
    vec = (1, D_MODEL)
    res = pl.pallas_call(
        body,
        grid=(nc,),
        in_specs=[slots, row, row, pl.BlockSpec((None, HEADS, HEAD, HEAD), lambda n: (rev(n), 0, 0, 0)),
                  pl.BlockSpec(memory_space=pl.ANY), whole(logits.shape), whole(vec)] + (ex.in_specs if ex else []),
        out_specs=[slots, whole((2, D_MODEL)), whole(vec)] + (ex.out_specs if ex else []),
        out_shape=[jax.ShapeDtypeStruct(dh.shape, BF16), jax.ShapeDtypeStruct((2, D_MODEL), F32),
                   jax.ShapeDtypeStruct(vec, F32)] + (ex.out_shape if ex else []),
        scratch_shapes=[pltpu.VMEM((HEADS, HEAD, HEAD), F32), pltpu.VMEM(vec, F32)]
        + [pltpu.VMEM((CHUNK, D_MODEL), F32)] * 6 + (ex.scratch if ex else []),
        input_output_aliases={4: 0},
        compiler_params=_params(("arbitrary",)),
        name="hgrn_bwd",
    )(h, o_all, dy, states, dh, logits, g_norm, *(ex.arrays if ex else []))
    return res[0], res[1], res[2], res[3:]


def _merge_fwd(i, n, ga, gb, za, zb):
    return jax.nn.sigmoid(ga) * za + jax.nn.sigmoid(gb) * zb


def _merge_bwd(i, n, ga, gb, za, zb, dm):
    sa, sb = jax.nn.sigmoid(ga), jax.nn.sigmoid(gb)
    dgates = jnp.stack([(dm * za * sa * (1.0 - sa)).astype(BF16), (dm * zb * sb * (1.0 - sb)).astype(BF16)])
    return dgates, dm * sa, dm * sb


def _ln1_fwd(i, n, x, r1, g, b):
    xhat, _ = _ln_stats(ALPHA * x + r1)
    x1 = xhat * g + b
    return x1, x1


def _ln1_bwd(i, n, x, r1, dx1, g):
    xhat, rstd = _ln_stats(ALPHA * x + r1)
    dz = _ln_bwd(dx1, xhat, rstd, g)
    return dz, dz, _sum0(dx1 * xhat), _sum0(dx1)


def _ln2_loss(i, n, x1, fo, pg, pp, tgt, g, b):
    sg = jax.nn.sigmoid(pg)
    xhat, rstd = _ln_stats(ALPHA * x1 + fo + sg * pp)
    diff = xhat * g + b - tgt
    loss = 0.5 * jnp.sum(_mean(diff * diff), axis=0, keepdims=True)
    dy = diff * (1.0 / D_MODEL)
    dz = _ln_bwd(dy, xhat, rstd, g)
    return (dz, dz, dz * pp * sg * (1.0 - sg), dz * sg,
            jnp.broadcast_to(loss, (8, LANE)), _sum0(dy * xhat), _sum0(dy))


def _shift_down(cur, halo, tile):
    row = lax.broadcasted_iota(jnp.int32, cur.shape, 0)
    m1 = jnp.where(row == 0, halo[7:8, :], pltpu.roll(cur, 1, 0))
    m2 = jnp.where(row == 0, halo[6:7, :], jnp.where(row == 1, halo[7:8, :], pltpu.roll(cur, 2, 0)))
    return m1, m2


def _shift_up(cur, halo, tile):
    row = lax.broadcasted_iota(jnp.int32, cur.shape, 0)
    p1 = jnp.where(row == tile - 1, halo[0:1, :], pltpu.roll(cur, tile - 1, 0))
    p2 = jnp.where(row == tile - 2, halo[0:1, :], jnp.where(row == tile - 1, halo[1:2, :], pltpu.roll(cur, tile - 2, 0)))
    return p1, p2


def _conv_pre(i, gate, halo, w, b, tile):
    halo = jnp.where(i == 0, 0.0, halo)
    m1, m2 = _shift_down(gate, halo, tile)
    return w[0:1, :] * m2 + w[1:2, :] * m1 + w[2:3, :] * gate + b, m1, m2


def _conv_fwd(tile, i, n, gate, halo, val, w, b):
    cg, _, _ = _conv_pre(i, gate, halo, w, b, tile)
    return _gelu(cg) * val


def _conv_bwd_a(tile, i, n, gate, halo, val, dhid, w, b):
    cg, m1, m2 = _conv_pre(i, gate, halo, w, b, tile)
    act, slope = _gelu_and_grad(cg)
    dcg = dhid * val * slope
    return dcg, dhid * act, _sum0(dcg * m2), _sum0(dcg * m1), _sum0(dcg * gate), _sum0(dcg)


def _conv_bwd_b(tile, i, n, dcg, halo, w):
    dcg = dcg.astype(F32)
    halo = jnp.where(i == n - 1, 0.0, halo.astype(F32))
    p1, p2 = _shift_up(dcg, halo, tile)
    return w[2:3, :] * dcg + w[1:2, :] * p1 + w[0:1, :] * p2


def _halo_spec(width, tile, t, nxt, rows=8):
    per = tile // rows
    last = t // rows - 1
    if nxt:
        return pl.BlockSpec((rows, width), lambda i: (jnp.minimum((i + 1) * per, last), 0))
    return pl.BlockSpec((rows, width), lambda i: (jnp.maximum(i * per - 1, 0), 0))


def _adamw(i, n, w, m, v, parts):
    g = parts[0].astype(F32)
    for j in range(1, parts.shape[0]):
        g = g + parts[j].astype(F32)
    m_new = ADAM_B1 * m + (1.0 - ADAM_B1) * g
    v_new = ADAM_B2 * v + (1.0 - ADAM_B2) * (g * g)
    m_hat = m_new / (1.0 - ADAM_B1 ** ADAM_STEP)
    v_hat = v_new / (1.0 - ADAM_B2 ** ADAM_STEP)
    delta = -ADAM_LR * (m_hat / (jnp.sqrt(v_hat) + ADAM_EPS) + ADAM_WD * w)
    return g, delta, m_new, v_new


def _adam_call(w, m, v, parts, name):
    r, c = w.shape
    tile = _pick(r, (256, 128)) if r > 256 else r
    spec = pl.BlockSpec((parts.shape[0], tile, c), lambda i: (0, i, 0))
    return _rowwise(_adamw, [w, m, v, (parts, spec)], [], [(c, F32)] * 4, [], tile=tile, name=name)


def _peer(k):
    x, y, c = lax.axis_index("x"), lax.axis_index("y"), lax.axis_index("c")
    px = x ^ ((k >> 2) & 1)
    py = y ^ ((k >> 1) & 1)
    pc = c ^ (k & 1)
    return (px, py, pc), 4 * px + 2 * py + pc


def _my_index():
    return 4 * lax.axis_index("x") + 2 * lax.axis_index("y") + lax.axis_index("c")


class _Exchange:
    KINDS = ("gather", "gather+relay", "scatter", "scatter+pairs")

    def __init__(self, entries):
        assert all(k in self.KINDS for _, k in entries), [k for _, k in entries]
        self.arrays = [a for a, _ in entries]
        self.scatter = [k.startswith("scatter") for _, k in entries]
        self.relayed = ["+relay" in k for _, k in entries]
        self.pairs = ["+pairs" in k for _, k in entries]
        self.n = len(entries)
        self.in_specs = [pl.BlockSpec(memory_space=pl.ANY)] * self.n
        self.out_specs = [pl.BlockSpec(memory_space=pl.ANY)] * self.n
        shapes = [tuple(a.shape[1:]) if sc else tuple(a.shape) for a, sc in zip(self.arrays, self.scatter)]
        counts = [N_CHIP if p else N_DEV for p in self.pairs]
        self.out_shape = [jax.ShapeDtypeStruct((n,) + s, a.dtype) for n, s, a in zip(counts, shapes, self.arrays)]
        per = N_DEV - 1
        self.scratch = [pltpu.SemaphoreType.DMA((self.n * per,)), pltpu.SemaphoreType.DMA((self.n * per,)),
                        pltpu.SemaphoreType.DMA((self.n,))]

    def _copies(self, srcs, outs, sems):
        send_sems, recv_sems, local_sems = sems
        x, y, c = lax.axis_index("x"), lax.axis_index("y"), lax.axis_index("c")
        me = _my_index()
        per = N_DEV - 1
        local, first, passed, relay_arrivals, arrivals = [], [], [], [], []
        for a in range(self.n):

            def copy(k, src, dst, dev, a=a):
                return pltpu.make_async_remote_copy(
                    src_ref=src, dst_ref=dst, send_sem=send_sems.at[a * per + k], recv_sem=recv_sems.at[a * per + k],
                    device_id=dev, device_id_type=pl.DeviceIdType.MESH)

            if self.pairs[a]:
                chip = 2 * x + y
                for k in range(N_CHIP):
                    to = chip ^ k
                    piece = srcs[a].at[_slot_of_group(2 * to + c) // 2]
                    if k == 0:
                        local.append(pltpu.make_async_copy(piece, outs[a].at[chip], local_sems.at[a]))
                    else:
                        dev = (to // 2, to % 2, c)
                        first.append(copy(k - 1, piece, outs[a].at[chip], dev))
                        arrivals.append(copy(k - 1, piece, outs[a].at[to], dev))
                continue
            mine = srcs[a].at[me] if self.scatter[a] else srcs[a]
            land = outs[a].at[me]
            local.append(pltpu.make_async_copy(mine, land, local_sems.at[a]))
            if self.relayed[a]:
                block = lambda px, py, pc, a=a: outs[a].at[4 * px + 2 * py + pc]
                chips = [(1 - x, y), (x, 1 - y), (1 - x, 1 - y)]
                first.append(copy(0, mine, land, (x, y, 1 - c)))
                arrivals.append(copy(0, mine, block(x, y, 1 - c), (x, y, 1 - c)))
                for j, (px, py) in enumerate(chips):
                    first.append(copy(1 + j, mine, land, (px, py, c)))
                    relay_arrivals.append(copy(1 + j, mine, block(px, py, c), (px, py, c)))
                    passed.append(copy(4 + j, block(px, py, c), block(px, py, c), (x, y, 1 - c)))
                    arrivals.append(copy(4 + j, mine, block(px, py, 1 - c), (x, y, 1 - c)))
                continue
            for k in range(1, N_DEV):
                dev, idx = _peer(k)
                if self.scatter[a]:
                    first.append(copy(k - 1, srcs[a].at[idx], land, dev))
                else:
                    first.append(copy(k - 1, mine, land, dev))
                arrivals.append(copy(k - 1, mine, outs[a].at[idx], dev))
        return local, first, passed, relay_arrivals, arrivals

    def start(self, srcs, outs, sems):
        local, first, _, _, _ = self._copies(srcs, outs, sems)
        for cp in local + first:
            cp.start()

    def relay(self, srcs, outs, sems):
        _, _, passed, relay_arrivals, _ = self._copies(srcs, outs, sems)
        for landed, onward in zip(relay_arrivals, passed):
            landed.wait_recv()
            onward.start()

    def finish(self, srcs, outs, sems):
        local, first, passed, _, arrivals = self._copies(srcs, outs, sems)
        for cp in arrivals:
            cp.wait_recv()
        for cp in first + passed:
            cp.wait_send()
        for cp in local:
            cp.wait()


def _gather_project(x_b, shard):
    t, d = x_b.shape
    tm = _pick(t, MM_TILES)
    nrow = t // tm
    per = N_DEV - 1

    def parties():
        x, y, c = lax.axis_index("x"), lax.axis_index("y"), lax.axis_index("c")
        first = (jnp.where(c == 0, 1 - x, x), jnp.where(c == 0, y, 1 - y))
        second = (jnp.where(c == 0, x, 1 - x), jnp.where(c == 0, 1 - y, y))
        diagonal = (1 - x, 1 - y)
        return ((x, y, c), (x, y, 1 - c), [(px, py, c) for px, py in (first, second, diagonal)],
                [(px, py, 1 - c) for px, py in (second, first, diagonal)])

    slot = lambda dev: _slot_of_group(4 * dev[0] + 2 * dev[1] + dev[2])
    ici_step, passed_step = (2, 3, 6), (4, 5, 7)
    me, sibling, over_ici, passed_on = parties()
    by_step = {0: me, 1: sibling, **dict(zip(ici_step, over_ici)), **dict(zip(passed_step, passed_on))}
    order = jnp.stack([slot(by_step[j]) for j in range(N_DEV)]).astype(jnp.int32)

    def body(order_ref, x_ref, shard_ref, h_ref, wall_ref, wbuf, fetch_sem, send_sems, recv_sems, local_sem):
        del order_ref
        me, sibling, over_ici, passed_on = parties()
        j, i = pl.program_id(0), pl.program_id(1)
        land = lambda dev: wall_ref.at[slot(dev)]

        def copy(k, src, block, to):
            return pltpu.make_async_remote_copy(src_ref=src, dst_ref=land(block), send_sem=send_sems.at[k],
                                                recv_sem=recv_sems.at[k], device_id=to, device_id_type=pl.DeviceIdType.MESH)

        def fetch(src):
            cp = pltpu.make_async_copy(src, wbuf, fetch_sem)
            cp.start()
            cp.wait()

        keep = pltpu.make_async_copy(shard_ref, land(me), local_sem)
        first = [copy(0, shard_ref, me, sibling)] + [copy(1 + n, shard_ref, me, dev) for n, dev in enumerate(over_ici[:2])]
        relay = copy(3, land(over_ici[0]), over_ici[0], over_ici[1])
        onward = [copy(4 + n, land(dev), dev, sibling) for n, dev in enumerate(over_ici)]

        @pl.when(jnp.logical_and(i == 0, j == 0))
        def _():
            for cp in reversed(first):
                cp.start()
            keep.start()
            fetch(shard_ref)

        @pl.when(jnp.logical_and(i == 0, j == 1))
        def _():
            copy(0, shard_ref, sibling, me).wait_recv()
            fetch(land(sibling))

        for n, dev in enumerate(over_ici):
            @pl.when(jnp.logical_and(i == 0, j == ici_step[n]))
            def _(n=n, dev=dev):
                copy(1 + n, shard_ref, dev, me).wait_recv()
                if n == 0:
                    relay.start()
                onward[n].start()
                fetch(land(dev))

        for n, dev in enumerate(passed_on):
            @pl.when(jnp.logical_and(i == 0, j == passed_step[n]))
            def _(n=n, dev=dev):
                copy(4 + n, shard_ref, dev, me).wait_recv()
                fetch(land(dev))

        h_ref[...] = _dot(x_ref[...], wbuf[...])

        @pl.when(jnp.logical_and(i == nrow - 1, j == N_DEV - 1))
        def _():
            for cp in first + [relay] + onward:
                cp.wait_send()
            keep.wait()

    need = 2 * (tm * d * 2 + tm * d * 4) + d * d * 2 + tm * d * 4
    h, w_all = pl.pallas_call(
        body,
        grid_spec=pltpu.PrefetchScalarGridSpec(
            num_scalar_prefetch=1,
            grid=(N_DEV, nrow),
            in_specs=[pl.BlockSpec((tm, d), lambda j, i, order: (i, 0)), pl.BlockSpec(memory_space=pl.ANY)],
            out_specs=[pl.BlockSpec((None, tm, d), lambda j, i, order: (order[j], i, 0)), pl.BlockSpec(memory_space=pl.ANY)],
            scratch_shapes=[pltpu.VMEM((d, d), BF16), pltpu.SemaphoreType.DMA, pltpu.SemaphoreType.DMA((per,)),
                            pltpu.SemaphoreType.DMA((per,)), pltpu.SemaphoreType.DMA],
        ),
        out_shape=[jax.ShapeDtypeStruct((N_DEV, t, d), F32), jax.ShapeDtypeStruct((N_DEV, d, d), BF16)],
        compiler_params=_params(("arbitrary", "arbitrary"), need),
        name="gather_project",
    )(order, x_b, shard)
    return h, w_all


def _pair_sums(g):
    n, r, cols = g.shape
    half = n // 2

    def swap(g_ref, got_ref, send_sems, recv_sems):
        x, y, c = lax.axis_index("x"), lax.axis_index("y"), lax.axis_index("c")
        copies = [pltpu.make_async_remote_copy(
            src_ref=g_ref.at[2 * j + 1 - c], dst_ref=got_ref.at[j], send_sem=send_sems.at[j], recv_sem=recv_sems.at[j],
            device_id=(x, y, 1 - c), device_id_type=pl.DeviceIdType.MESH) for j in range(half)]
        for cp in copies:
            cp.start()
        for cp in copies:
            cp.wait()

    got = pl.pallas_call(
        swap,
        in_specs=[pl.BlockSpec(memory_space=pl.ANY)],
        out_specs=pl.BlockSpec(memory_space=pl.ANY),
        out_shape=jax.ShapeDtypeStruct((half, r, cols), g.dtype),
        scratch_shapes=[pltpu.SemaphoreType.DMA((half,))] * 2,
        name="pair_swap",
    )(g)

    def add(mine_ref, got_ref, out_ref):
        mine = jnp.where(lax.axis_index("c") == 0, mine_ref[0].astype(F32), mine_ref[1].astype(F32))
        out_ref[...] = (mine + got_ref[...].astype(F32)).astype(out_ref.dtype)

    tile = _pick(r, (512, 256, 128))
    return pl.pallas_call(
        add,
        grid=(half, r // tile),
        in_specs=[pl.BlockSpec((None, 2, tile, cols), lambda j, i: (j, 0, i, 0)),
                  pl.BlockSpec((None, tile, cols), lambda j, i: (j, i, 0))],
        out_specs=pl.BlockSpec((None, tile, cols), lambda j, i: (j, i, 0)),
        out_shape=jax.ShapeDtypeStruct((half, r, cols), g.dtype),
        compiler_params=_params(("parallel", "parallel")),
        name="pair_add",
    )(g.reshape(half, 2, r, cols), got)


def _local_step(x, p, tgt, small, comm):
    t = x.shape[0]
    tile = _pick(t, (256, 128))
    tall = _pick(t, (512, 256, 128))
    d = D_MODEL
    act_b, act_f = (d, BF16), (d, F32)
    x_b, p_b = x.astype(BF16), p.astype(BF16)

    chunk_id = jnp.arange(SGU_BLOCK) // CHUNK
    mask = chunk_id[:, None] >= chunk_id[None, :]
    wm = jnp.where(mask[None], small["sgu_w_s"], 0.0)
    wm_b = wm.astype(BF16)
    wm_t = jnp.swapaxes(wm, 1, 2).astype(BF16)
    bs_t = small["sgu_b_s"].T

    h, w_in = comm.project_in(x_b)
    y_a, got_a = _sgu_fwd(h, wm_b, bs_t, small["sgu_norm_g"], small["sgu_norm_b"], ex=comm.weights_exchange(0))
    y_b, o_all, states, got_b = _hgrn_fwd(h, small["lb_logits"], small["hgrn_norm_g"], ex=comm.weights_exchange(1))
    wts, conv_w = comm.weights(list(got_a) + list(got_b))
    z_a = _mm(y_a, wts["w_a"], out_dtype=F32, name="mm_za")
    z_b = _mm(y_b, wts["w_b"], out_dtype=F32, name="mm_zb")
    gates = [(h, SLOT_GA), (h, SLOT_GB)]
    merged, = _rowwise(_merge_fwd, gates + [z_a, z_b], [], [act_b], [], tile=tall, name="merge_fwd")
    r1 = _mm(merged, wts["w_o"], out_dtype=F32, name="mm_r1")
    x1, x1_b = _rowwise(_ln1_fwd, [x, r1], [small["ln1_g"], small["ln1_b"]], [act_f, act_b], [], tile=tall, name="ln1_fwd")
    gate = _mm(x1_b, wts["w_g"], out_dtype=F32, name="mm_gate")
    val = _mm(x1_b, wts["w_v"], out_dtype=F32, name="mm_val")
    pg = _mm(x1_b, wts["w_pg"], out_dtype=F32, name="mm_pg")
    pp = _mm(p_b, wts["w_pp"], out_dtype=F32, name="mm_pp")
    hid, = _rowwise(functools.partial(_conv_fwd, tile), [gate, (gate, _halo_spec(D_FF, tile, t, False)), val],
                    [conv_w, small["conv_b"]], [(D_FF, BF16)], [], tile=tile, name="conv_fwd")
    fo = _mm(hid, wts["w_down"], out_dtype=F32, name="mm_down")
    dz2, dz2_b, dpg, dpp, loss, dg2, db2 = _rowwise(
        _ln2_loss, [x1, fo, pg, pp, tgt], [small["ln2_g"], small["ln2_b"]],
        [act_f, act_b, act_b, act_b], [(8, LANE), (1, d), (1, d)], tile=tall, name="ln2_loss")

    dhid = _mm(dz2_b, wts["w_down"], out_dtype=BF16, name="mm_dhid", trans_b=True)
    g_down = _mm_tn(hid, dz2_b, out_dtype=BF16, name="mm_g_down")
    dcg, dval, dcw0, dcw1, dcw2, dcb = _rowwise(
        functools.partial(_conv_bwd_a, tile), [gate, (gate, _halo_spec(D_FF, tile, t, False)), val, dhid],
        [conv_w, small["conv_b"]], [(D_FF, BF16), (D_FF, BF16)], [(1, D_FF)] * 4, tile=tile, name="conv_bwd_a")
    dgate, = _rowwise(functools.partial(_conv_bwd_b, tile), [dcg, (dcg, _halo_spec(D_FF, tile, t, True, rows=16))],
                      [conv_w], [(D_FF, BF16)], [], tile=tile, name="conv_bwd_b")
    g_g = _mm_tn(x1_b, dgate, out_dtype=BF16, name="mm_g_gate")
    g_v = _mm_tn(x1_b, dval, out_dtype=BF16, name="mm_g_val")
    g_pg = _mm_tn(x1_b, dpg, out_dtype=BF16, name="mm_g_pg")
    g_pp = _mm_tn(p_b, dpp, out_dtype=BF16, name="mm_g_pp")
    dx1 = _mm(dgate, wts["w_g"], out_dtype=F32, name="mm_dx1_gate", trans_b=True, adds=[(dz2, ALPHA)])
    dx1 = _mm(dval, wts["w_v"], out_dtype=F32, name="mm_dx1_val", trans_b=True, adds=[(dx1, 1.0)])
    dx1 = _mm(dpg, wts["w_pg"], out_dtype=F32, name="mm_dx1_pg", trans_b=True, adds=[(dx1, 1.0)])
    dz1, dz1_b, dg1, db1 = _rowwise(_ln1_bwd, [x, r1, dx1], [small["ln1_g"]], [act_f, act_b], [(1, d), (1, d)],
                                    tile=tall, name="ln1_bwd")
    g_o = _mm_tn(merged, dz1_b, out_dtype=BF16, name="mm_g_o")
    dm = _mm(dz1_b, wts["w_o"], out_dtype=BF16, name="mm_dm", trans_b=True)
    dh, dza, dzb = _rowwise(_merge_bwd, gates + [z_a, z_b, dm], [],
                            [("stack", 2, SLOT_GA // 2, 8, d, BF16), act_b, act_b], [], tile=tall, name="merge_bwd")
    g_a = _mm_tn(y_a, dza, out_dtype=BF16, name="mm_g_a")
    g_b = _mm_tn(y_b, dzb, out_dtype=BF16, name="mm_g_b")
    dy_a = _mm(dza, wts["w_a"], out_dtype=BF16, name="mm_dya", trans_b=True)
    dy_b = _mm(dzb, wts["w_b"], out_dtype=F32, name="mm_dyb", trans_b=True)
    dh, dws, dbs, dgv_n, dbv_n = _sgu_bwd(h, dy_a, dh, wm_b, wm_t, bs_t, small["sgu_norm_g"], small["sgu_norm_b"])
    big = dict(w_a=g_a, w_b=g_b, w_o=g_o, w_g=g_g, w_v=g_v, w_down=g_down, w_pp=g_pp, w_pg=g_pg)
    sm = dict(sgu_w_s=jnp.where(mask[None], dws, 0.0), sgu_b_s=dbs[:, :GROUPS].T, sgu_norm_g=dgv_n, sgu_norm_b=dbv_n,
              ln1_g=dg1, ln1_b=db1, conv_w=jnp.concatenate([dcw0, dcw1, dcw2], axis=0), conv_b=dcb, ln2_g=dg2, ln2_b=db2,
              loss=loss)
    dh, dlogits, dgn, got = _hgrn_bwd(h, o_all, dy_b, states, dh, small["lb_logits"], small["hgrn_norm_g"],
                                      ex=comm.grads_exchange(big, sm))
    comm.grads_done(got)
    g_in = _mm_tn(x_b, dh, out_dtype=BF16, name="mm_g_in")
    ex = comm.last_exchange(g_in, dict(lb_logits=dlogits, hgrn_norm_g=dgn))
    res = _mm(dh, w_in, out_dtype=F32, name="mm_dx", trans_b=True, reduce_b=True, adds=[(dz1, ALPHA)], ex=ex)
    grad_x, got = res if ex else (res, ())
    comm.last_done(got)
    return grad_x


_SMALL_EARLY = ["sgu_w_s", "sgu_b_s", "sgu_norm_g", "sgu_norm_b", "ln1_g", "ln1_b", "ffn_conv_b", "ln2_g", "ln2_b"]
_SMALL_LATE = ["hgrn_lb_logits", "hgrn_norm_g"]
N_TAPS = D_FF // N_DEV
UP_COLS = 2 * D_FF // N_DEV


class _StepExchanges:
    def __init__(self, w_in_shard, shards):
        self.w_in_shard = w_in_shard
        self.shards = shards

    def project_in(self, x_b):
        return _gather_project(x_b, self.w_in_shard)

    def weights_exchange(self, part):
        return _Exchange([(s, "gather+relay") for s in (self.shards[:2] if part == 0 else self.shards[2:])])

    def weights(self, got):
        d, f = D_MODEL, D_FF
        w_br_g, w_o_g, w_up_g, w_down_g, w_pp_g, w_pg_g, conv_g = got
        w_br = w_br_g.transpose(1, 0, 2, 3).reshape(2, d, d)
        w_up = w_up_g.transpose(1, 0, 2).reshape(d, 2, f).transpose(1, 0, 2)
        wts = dict(w_a=w_br[0], w_b=w_br[1], w_o=w_o_g.reshape(d, d), w_g=w_up[0], w_v=w_up[1],
                   w_down=w_down_g.reshape(f, d), w_pp=w_pp_g.transpose(1, 0, 2).reshape(256, d), w_pg=w_pg_g.reshape(d, d))
        return wts, conv_g.transpose(1, 0, 2).reshape(3, f)

    def grads_exchange(self, big, sm):
        d = D_MODEL
        parts = [jnp.stack([big["w_a"], big["w_b"]]).reshape(2, N_DEV, 128, d).transpose(1, 0, 2, 3),
                 big["w_o"].reshape(N_DEV, 128, d),
                 jnp.concatenate([big["w_g"], big["w_v"]], axis=1).reshape(d, N_DEV, UP_COLS).transpose(1, 0, 2),
                 big["w_down"].reshape(N_DEV, N_TAPS, d),
                 big["w_pp"].reshape(256, N_DEV, 128).transpose(1, 0, 2),
                 big["w_pg"].reshape(N_DEV, 128, d)]
        packed, self.rows_early = _pack([sm[k] for k in ("sgu_w_s", "sgu_b_s", "sgu_norm_g", "sgu_norm_b", "ln1_g", "ln1_b",
                                                         "conv_b", "ln2_g", "ln2_b", "conv_w", "loss")])
        return _Exchange([(a, "scatter") for a in parts] + [(packed, "gather")])

    def grads_done(self, got):
        self.recv, self.small_early = got[:6], got[6]

    def last_exchange(self, g_in, sm):
        packed, self.rows_late = _pack([sm["lb_logits"], sm["hgrn_norm_g"]])
        return _Exchange([(_pair_sums(g_in), "scatter+pairs"), (packed, "gather")])

    def last_done(self, got):
        self.recv_in, self.small_late = got


def _rows128(a):
    flat = a.reshape(-1)
    rows = -(-flat.shape[0] // (8 * LANE)) * 8
    return jnp.pad(flat, (0, rows * LANE - flat.shape[0])).reshape(rows, LANE)


def _pack(parts):
    blocks = [_rows128(a) for a in parts]
    return jnp.concatenate(blocks, axis=0), [b.shape[0] for b in blocks]


def _unpack(packed, shapes, rows):
    out, r0 = [], 0
    for shp, r in zip(shapes, rows):
        n = math.prod(shp)
        out.append(packed[r0:r0 + r].reshape(-1)[:n].reshape(shp))
        r0 += r
    return out


def kernel(x, p, w_in, sgu_w_s, sgu_b_s, sgu_norm_g, sgu_norm_b, hgrn_lb_logits, hgrn_norm_g, w_branch, w_out, ln1_g, ln1_b, ffn_w_up, ffn_conv_w, ffn_conv_b, ffn_w_down, ln2_g, ln2_b, ple_w_proj, ple_w_gate, loss_target, m_w_in, m_sgu_w_s, m_sgu_b_s, m_sgu_norm_g, m_sgu_norm_b, m_hgrn_lb_logits, m_hgrn_norm_g, m_w_branch, m_w_out, m_ln1_g, m_ln1_b, m_ffn_w_up, m_ffn_conv_w, m_ffn_conv_b, m_ffn_w_down, m_ln2_g, m_ln2_b, m_ple_w_proj, m_ple_w_gate, v_w_in, v_sgu_w_s, v_sgu_b_s, v_sgu_norm_g, v_sgu_norm_b, v_hgrn_lb_logits, v_hgrn_norm_g, v_w_branch, v_w_out, v_ln1_g, v_ln1_b, v_ffn_w_up, v_ffn_conv_w, v_ffn_conv_b, v_ffn_w_down, v_ln2_g, v_ln2_b, v_ple_w_proj, v_ple_w_gate):
    weights = dict(w_in=w_in, sgu_w_s=sgu_w_s, sgu_b_s=sgu_b_s, sgu_norm_g=sgu_norm_g, sgu_norm_b=sgu_norm_b,
                   hgrn_lb_logits=hgrn_lb_logits, hgrn_norm_g=hgrn_norm_g, w_branch=w_branch, w_out=w_out,
                   ln1_g=ln1_g, ln1_b=ln1_b, ffn_w_up=ffn_w_up, ffn_conv_w=ffn_conv_w, ffn_conv_b=ffn_conv_b,
                   ffn_w_down=ffn_w_down, ln2_g=ln2_g, ln2_b=ln2_b, ple_w_proj=ple_w_proj, ple_w_gate=ple_w_gate)
    mom_m = dict(w_in=m_w_in, sgu_w_s=m_sgu_w_s, sgu_b_s=m_sgu_b_s, sgu_norm_g=m_sgu_norm_g, sgu_norm_b=m_sgu_norm_b,
                 hgrn_lb_logits=m_hgrn_lb_logits, hgrn_norm_g=m_hgrn_norm_g, w_branch=m_w_branch, w_out=m_w_out,
                 ln1_g=m_ln1_g, ln1_b=m_ln1_b, ffn_w_up=m_ffn_w_up, ffn_conv_w=m_ffn_conv_w, ffn_conv_b=m_ffn_conv_b,
                 ffn_w_down=m_ffn_w_down, ln2_g=m_ln2_g, ln2_b=m_ln2_b, ple_w_proj=m_ple_w_proj, ple_w_gate=m_ple_w_gate)
    mom_v = dict(w_in=v_w_in, sgu_w_s=v_sgu_w_s, sgu_b_s=v_sgu_b_s, sgu_norm_g=v_sgu_norm_g, sgu_norm_b=v_sgu_norm_b,
                 hgrn_lb_logits=v_hgrn_lb_logits, hgrn_norm_g=v_hgrn_norm_g, w_branch=v_w_branch, w_out=v_w_out,
                 ln1_g=v_ln1_g, ln1_b=v_ln1_b, ffn_w_up=v_ffn_w_up, ffn_conv_w=v_ffn_conv_w, ffn_conv_b=v_ffn_conv_b,
                 ffn_w_down=v_ffn_w_down, ln2_g=v_ln2_g, ln2_b=v_ln2_b, ple_w_proj=v_ple_w_proj, ple_w_gate=v_ple_w_gate)
    d, f = D_MODEL, D_FF
    me = _my_index()

    comm = _StepExchanges(w_in[0].astype(BF16),
                          [w_branch[0].astype(BF16), w_out[0].astype(BF16), ffn_w_up[0].astype(BF16),
                           ffn_w_down[0].astype(BF16), ple_w_proj[0].astype(BF16), ple_w_gate[0].astype(BF16), ffn_conv_w[0]])
    small = dict(sgu_w_s=sgu_w_s[0], sgu_b_s=sgu_b_s[0], sgu_norm_g=sgu_norm_g, sgu_norm_b=sgu_norm_b,
                 lb_logits=hgrn_lb_logits, hgrn_norm_g=hgrn_norm_g, ln1_g=ln1_g, ln1_b=ln1_b, ln2_g=ln2_g, ln2_b=ln2_b,
                 conv_b=ffn_conv_b)
    grad_x = _local_step(x[0], p[0, 0], loss_target[0], small, comm)

    out = {}

    def adam(name, parts8, shape2d):
        w2, m2, v2 = (a.reshape(shape2d) for a in (weights[name], mom_m[name], mom_v[name]))
        res = _adam_call(w2, m2, v2, parts8.reshape(parts8.shape[:1] + shape2d), "adam_" + name)
        out[name] = tuple(r.reshape(weights[name].shape) for r in res)

    adam("w_in", comm.recv_in, (d, d))
    adam("w_branch", comm.recv[0], (256, d))
    adam("w_out", comm.recv[1], (128, d))
    adam("ffn_w_up", comm.recv[2], (d, UP_COLS))
    adam("ffn_w_down", comm.recv[3], (N_TAPS, d))
    adam("ple_w_proj", comm.recv[4], (256, 128))
    adam("ple_w_gate", comm.recv[5], (128, d))

    def adam_small(names, extra_w, extra_m, extra_v, extra_shapes, parts8, rows, label):
        pk = lambda src, extra: _pack([src[n] for n in names] + extra)[0]
        res = _adam_call(pk(weights, extra_w), pk(mom_m, extra_m), pk(mom_v, extra_v), parts8, label)
        shapes = [weights[n].shape for n in names] + extra_shapes
        unpacked = [_unpack(r, shapes, rows) for r in res]
        for j, n in enumerate(names):
            out[n] = tuple(u[j] for u in unpacked)
        return [[u[len(names) + j] for u in unpacked] for j in range(len(extra_shapes))]

    blank = jnp.zeros((8, LANE), F32)
    taps, loss_rows = adam_small(
        _SMALL_EARLY, [_place_taps(ffn_conv_w[0], me, f), blank], [_place_taps(m_ffn_conv_w[0], me, f), blank],
        [_place_taps(v_ffn_conv_w[0], me, f), blank + 1.0], [(3, f), (8, LANE)], comm.small_early, comm.rows_early,
        "adam_small_early")
    adam_small(_SMALL_LATE, [], [], [], [], comm.small_late, comm.rows_late, "adam_small_late")
    out["ffn_conv_w"] = tuple(lax.dynamic_slice_in_dim(u, me * N_TAPS, N_TAPS, axis=1)[None] for u in taps)
    loss = loss_rows[0][0, 0]

    order = ["w_in", "sgu_w_s", "sgu_b_s", "sgu_norm_g", "sgu_norm_b", "hgrn_lb_logits", "hgrn_norm_g", "w_branch", "w_out",
             "ln1_g", "ln1_b", "ffn_w_up", "ffn_conv_w", "ffn_conv_b", "ffn_w_down", "ln2_g", "ln2_b", "ple_w_proj", "ple_w_gate"]
    return (loss, grad_x[None], *[out[n][0] for n in order], *[out[n][1] for n in order],
            *[out[n][2] for n in order], *[out[n][3] for n in order])


def _place_taps(shard, me, f):
    return lax.dynamic_update_slice_in_dim(jnp.zeros((3, f), F32), shard, me * N_TAPS, axis=1)
```

```python
import functools
import math

import jax
import jax.numpy as jnp
from jax import lax
from jax.experimental import pallas as pl
from jax.experimental.pallas import tpu as pltpu

F32 = jnp.float32
BF16 = jnp.bfloat16

N_DEV = 8
N_CHIP = 4
D_MODEL = 1024
CHUNK = 64
SUB = 16
SGU_BLOCK = 128
GROUPS = 8
HEAD = 128
HEADS = 8
CHUNKS_PER_STEP = 2
HEAD_UNROLL = 8
D_FF = 2816
LN_EPS = 1e-5
RMS_EPS = 1e-6
ALPHA = 2.0 ** 0.25
GELU_K = math.sqrt(2.0 / math.pi)
GELU_C = 0.044715
NEG = -1e30
ADAM_LR, ADAM_B1, ADAM_B2, ADAM_EPS, ADAM_WD, ADAM_STEP = 0.001, 0.9, 0.999, 1e-08, 0.01, 10
LANE = 128
SLOT_Q, SLOT_F, SLOT_I, SLOT_OG, SLOT_U, SLOT_V, SLOT_GA, SLOT_GB = range(8)


def _slot_of_group(k):
    return jnp.where(k < 2, k + 4, jnp.where(k < 6, k - 2, k))


MIB = 1024 * 1024
VMEM_V7X = 64 * MIB
VMEM_FLOOR = 32 * MIB
MM_TILES = (1024, 1408, 512, 256, 128)


def _params(sem, need=0):
    limit = min(max(need + need // 4, VMEM_FLOOR), VMEM_V7X - 4 * MIB)
    return pltpu.CompilerParams(dimension_semantics=sem, vmem_limit_bytes=limit)


def _pick(n, prefs):
    for t in prefs:
        if n % t == 0:
            return t
    return n


def _gelu(x):
    return 0.5 * x * (1.0 + jnp.tanh(GELU_K * (x + GELU_C * x * x * x)))


def _gelu_and_grad(x):
    x2 = x * x
    t = jnp.tanh(GELU_K * x * (1.0 + GELU_C * x2))
    half = 0.5 * (1.0 + t)
    return x * half, half + 0.5 * x * (1.0 - t * t) * GELU_K * (1.0 + 3.0 * GELU_C * x2)


def _silu_grad(x, s):
    return s * (1.0 + x * (1.0 - s))


def _dot(a, b):
    return jnp.dot(a.astype(BF16), b.astype(BF16), preferred_element_type=F32)


def _dot_nt(a, b):
    return lax.dot_general(a.astype(BF16), b.astype(BF16), (((1,), (1,)), ((), ())), preferred_element_type=F32)


def _dot_tn(a, b):
    return lax.dot_general(a.astype(BF16), b.astype(BF16), (((0,), (0,)), ((), ())), preferred_element_type=F32)


def _mean(x):
    return jnp.mean(x, axis=-1, keepdims=True)


def _sum0(x):
    return jnp.sum(x, axis=0, keepdims=True)


def _mm(a, b, *, out_dtype, name, trans_b=False, reduce_b=False, adds=(), ex=None):
    squeeze = b.ndim == 2
    a3 = a if a.ndim == 3 else a[None]
    b3 = b if b.ndim == 3 else b[None]
    ba, m, k = a3.shape
    bb = b3.shape[0]
    n = b3.shape[1] if trans_b else b3.shape[2]
    tm = _pick(m, MM_TILES)
    tn = _pick(n, MM_TILES)
    if reduce_b:
        bo, steps = 1, bb
        a_map = lambda o, i, j, r: (r if ba > 1 else 0, i, 0)
        b_map = (lambda o, i, j, r: (r, j, 0)) if trans_b else (lambda o, i, j, r: (r, 0, j))
    else:
        bo, steps = bb, 1
        a_map = lambda o, i, j, r: (o if ba > 1 else 0, i, 0)
        b_map = (lambda o, i, j, r: (o, j, 0)) if trans_b else (lambda o, i, j, r: (o, 0, j))
    o_map = lambda o, i, j, r: (o, i, j)
    add_arrays = [x if x.ndim == 3 else x[None] for x, _ in adds]
    add_scales = [s for _, s in adds]
    n_add = len(adds)
    dot = _dot_nt if trans_b else _dot

    def finish(acc, add_refs, o_ref):
        for ref, s in zip(add_refs, add_scales):
            acc = acc + s * ref[...].astype(F32)
        o_ref[...] = acc.astype(o_ref.dtype)

    grid = (bo, m // tm, n // tn, steps)

    def body(*refs):
        ins, (o_ref,), scratch, xrefs = _split_refs(refs, 2 + n_add, 1, 1 if reduce_b else 0, ex)
        a_ref, b_ref, add_refs = ins[0], ins[1], ins[2:]
        step = ((pl.program_id(0) * grid[1] + pl.program_id(1)) * grid[2] + pl.program_id(2)) * grid[3] + pl.program_id(3)
        if ex:
            @pl.when(step == 0)
            def _():
                ex.start(*xrefs)

        if reduce_b:
            acc, = scratch
            r = pl.program_id(3)

            @pl.when(r == 0)
            def _():
                acc[...] = jnp.zeros_like(acc)

            acc[...] += dot(a_ref[...], b_ref[...])

            @pl.when(r == steps - 1)
            def _():
                finish(acc[...], add_refs, o_ref)
        else:
            finish(dot(a_ref[...], b_ref[...]), add_refs, o_ref)

        if ex:
            @pl.when(step == math.prod(grid) - 1)
            def _():
                ex.finish(*xrefs)

    b_block = (None, tn, k) if trans_b else (None, k, tn)
    out_bytes = tm * tn * jnp.dtype(out_dtype).itemsize
    need = 2 * (tm * k * a3.dtype.itemsize + k * tn * b3.dtype.itemsize + out_bytes + n_add * tm * tn * 4)
    need += 2 * tm * tn * 4
    sem = ("arbitrary",) * 4 if ex else ("parallel", "parallel", "parallel", "arbitrary")
    res = pl.pallas_call(
        body,
        grid=grid,
        in_specs=[pl.BlockSpec((None, tm, k), a_map), pl.BlockSpec(b_block, b_map)]
        + [pl.BlockSpec((None, tm, tn), o_map) for _ in adds] + (ex.in_specs if ex else []),
        out_specs=[pl.BlockSpec((None, tm, tn), o_map)] + (ex.out_specs if ex else []),
        out_shape=[jax.ShapeDtypeStruct((bo, m, n), out_dtype)] + (ex.out_shape if ex else []),
        scratch_shapes=([pltpu.VMEM((tm, tn), F32)] if reduce_b else []) + (ex.scratch if ex else []),
        compiler_params=_params(sem, need),
        name=name,
    )(a3, b3, *add_arrays, *(ex.arrays if ex else []))
    out = res[0][0] if (reduce_b or squeeze) else res[0]
    return (out, res[1:]) if ex else out


def _mm_tn(a, b, *, out_dtype, name):
    squeeze = b.ndim == 2
    b3 = b if b.ndim == 3 else b[None]
    t, m = a.shape
    bb, _, n = b3.shape
    tm = _pick(m, MM_TILES)
    tn = _pick(n, MM_TILES)
    tt = _pick(t, (1024, 512, 256, 128))
    steps = t // tt
    need = 2 * (tt * tm * a.dtype.itemsize + tt * tn * b3.dtype.itemsize + tm * tn * jnp.dtype(out_dtype).itemsize)
    need += 2 * tm * tn * 4

    def body(a_ref, b_ref, o_ref, acc):
        r = pl.program_id(3)

        @pl.when(r == 0)
        def _():
            acc[...] = jnp.zeros_like(acc)

        acc[...] += _dot_tn(a_ref[...], b_ref[...])

        @pl.when(r == steps - 1)
        def _():
            o_ref[...] = acc[...].astype(o_ref.dtype)

    out = pl.pallas_call(
        body,
        grid=(bb, m // tm, n // tn, steps),
        in_specs=[pl.BlockSpec((tt, tm), lambda o, i, j, r: (r, i)),
                  pl.BlockSpec((None, tt, tn), lambda o, i, j, r: (o, r, j))],
        out_specs=pl.BlockSpec((None, tm, tn), lambda o, i, j, r: (o, i, j)),
        out_shape=jax.ShapeDtypeStruct((bb, m, n), out_dtype),
        scratch_shapes=[pltpu.VMEM((tm, tn), F32)],
        compiler_params=_params(("parallel", "parallel", "parallel", "arbitrary"), need),
        name=name,
    )(a, b3)
    return out[0] if squeeze else out


def _rowwise(fn, rows, consts, row_outs, acc_outs, *, tile, name):
    first = rows[0][0] if isinstance(rows[0], tuple) else rows[0]
    t = first.shape[-2]
    steps = t // tile
    arrays, in_specs = [], []
    for r in rows:
        if isinstance(r, tuple) and isinstance(r[1], pl.BlockSpec):
            arrays.append(r[0])
            in_specs.append(r[1])
        elif isinstance(r, tuple):
            arr, bidx = r
            arrays.append(arr)
            in_specs.append(pl.BlockSpec((None, tile, arr.shape[-1]), functools.partial(lambda i, b: (b, i, 0), b=bidx)))
        else:
            arrays.append(r)
            in_specs.append(pl.BlockSpec((tile, r.shape[-1]), lambda i: (i, 0)))
    for c in consts:
        arrays.append(c)
        in_specs.append(pl.BlockSpec(c.shape, lambda i: (0, 0)))
    n_in, n_row = len(arrays), len(row_outs)
    out_shape, out_specs = [], []
    for ro in row_outs:
        if ro[0] == "stack":
            _, cnt, blk, total, w, dt = ro
            out_shape.append(jax.ShapeDtypeStruct((total, t, w), dt))
            out_specs.append(pl.BlockSpec((cnt, tile, w), functools.partial(lambda i, b: (b, i, 0), b=blk)))
        else:
            w, dt = ro
            out_shape.append(jax.ShapeDtypeStruct((t, w), dt))
            out_specs.append(pl.BlockSpec((tile, w), lambda i: (i, 0)))
    out_shape += [jax.ShapeDtypeStruct(s, F32) for s in acc_outs]
    out_specs += [pl.BlockSpec(s, lambda i: (0, 0)) for s in acc_outs]
    blocks = [math.prod(d for d in sp.block_shape if d) * arr.dtype.itemsize for sp, arr in zip(in_specs, arrays)]
    blocks += [math.prod(d for d in sp.block_shape if d) * jnp.dtype(sh.dtype).itemsize
               for sp, sh in zip(out_specs, out_shape)]
    need = 2 * sum(blocks) + 6 * tile * max(a.shape[-1] for a in arrays) * 4

    def body(*refs):
        ins, outs = refs[:n_in], refs[n_in:]
        i = pl.program_id(0)
        res = fn(i, steps, *[r[...] for r in ins])
        res = res if isinstance(res, (tuple, list)) else (res,)
        for ref, val in zip(outs[:n_row], res[:n_row]):
            ref[...] = val.astype(ref.dtype)
        if acc_outs:
            @pl.when(i == 0)
            def _():
                for ref in outs[n_row:]:
                    ref[...] = jnp.zeros_like(ref)

            for ref, val in zip(outs[n_row:], res[n_row:]):
                ref[...] += val

    return pl.pallas_call(
        body,
        grid=(steps,),
        in_specs=in_specs,
        out_specs=out_specs,
        out_shape=out_shape,
        compiler_params=_params(("arbitrary",), need),
        name=name,
    )(*arrays)


def _ln_stats(z):
    mu = _mean(z)
    zc = z - mu
    rstd = lax.rsqrt(_mean(zc * zc) + LN_EPS)
    return zc * rstd, rstd


def _ln_bwd(dy, xhat, rstd, g):
    dxh = dy * g
    return rstd * (dxh - _mean(dxh) - xhat * _mean(dxh * xhat))


def _ride(ex, xrefs, step, steps):
    if not ex:
        return
    for at, act in ((0, ex.start), (steps - 2, ex.relay), (steps - 1, ex.finish)):
        @pl.when(step == at)
        def _(act=act):
            act(*xrefs)


def _sgu_fwd(h, wm, bs_t, g_v, b_v, ex=None):
    t = h.shape[1]
    steps = t // SGU_BLOCK

    def body(*refs):
        (u_ref, v_ref, wm_ref, bs_ref, g_ref, b_ref), (y_ref,), _, xrefs = _split_refs(refs, 6, 1, 0, ex)
        xhat, _ = _ln_stats(_gelu(v_ref[...]))
        vn = (xhat * g_ref[...] + b_ref[...]).astype(BF16)
        gu = _gelu(u_ref[...])
        for g in range(GROUPS):
            sl = slice(g * HEAD, (g + 1) * HEAD)
            mixed = _dot(wm_ref[g], vn[:, sl]) + bs_ref[:, g:g + 1]
            y_ref[:, sl] = (gu[:, sl] * mixed).astype(BF16)
        _ride(ex, xrefs, pl.program_id(0), steps)

    blk = lambda b: pl.BlockSpec((None, SGU_BLOCK, D_MODEL), functools.partial(lambda i, b: (b, i, 0), b=b))
    whole = lambda s: pl.BlockSpec(s, lambda i: (0,) * len(s))
    res = pl.pallas_call(
        body,
        grid=(steps,),
        in_specs=[blk(SLOT_U), blk(SLOT_V), whole(wm.shape), whole(bs_t.shape), whole(g_v.shape), whole(b_v.shape)]
        + (ex.in_specs if ex else []),
        out_specs=[pl.BlockSpec((SGU_BLOCK, D_MODEL), lambda i: (i, 0))] + (ex.out_specs if ex else []),
        out_shape=[jax.ShapeDtypeStruct((t, D_MODEL), BF16)] + (ex.out_shape if ex else []),
        scratch_shapes=ex.scratch if ex else [],
        compiler_params=_params(("arbitrary",)),
        name="sgu_fwd",
    )(h, h, wm, bs_t, g_v, b_v, *(ex.arrays if ex else []))
    return res[0], res[1:]


def _sgu_bwd(h, dy, dh, wm, wm_t, bs_t, g_v, b_v):
    t = h.shape[1]

    def body(u_ref, v_ref, dy_ref, dh_in, wm_ref, wmt_ref, bs_ref, g_ref, b_ref,
             duv_ref, dw_ref, dbs_ref, dg_ref, db_ref, dvn_ref):
        del dh_in
        du_ref, dv_ref = duv_ref.at[0], duv_ref.at[1]
        i = pl.program_id(0)

        @pl.when(i == 0)
        def _():
            dw_ref[...] = jnp.zeros_like(dw_ref)
            dbs_ref[...] = jnp.zeros_like(dbs_ref)
            dg_ref[...] = jnp.zeros_like(dg_ref)
            db_ref[...] = jnp.zeros_like(db_ref)

        u = u_ref[...]
        v = v_ref[...]
        gv, gvp = _gelu_and_grad(v)
        xhat, rstd = _ln_stats(gv)
        vn = (xhat * g_ref[...] + b_ref[...]).astype(BF16)
        gu, gup = _gelu_and_grad(u)
        lane = lax.broadcasted_iota(jnp.int32, (SGU_BLOCK, LANE), 1)
        dbs = jnp.zeros((SGU_BLOCK, LANE), F32)
        for g in range(GROUPS):
            sl = slice(g * HEAD, (g + 1) * HEAD)
            vn_g = vn[:, sl]
            mixed = _dot(wm_ref[g], vn_g) + bs_ref[:, g:g + 1]
            dy_g = dy_ref[:, sl]
            du_ref[:, sl] = (dy_g * mixed * gup[:, sl]).astype(BF16)
            dmix = dy_g * gu[:, sl]
            dmb = dmix.astype(BF16)
            dvn_ref[:, sl] = _dot(wmt_ref[g], dmb)
            dw_ref[g] += _dot_nt(dmb, vn_g)
            dbs = dbs + jnp.where(lane == g, jnp.sum(dmix, axis=1, keepdims=True), 0.0)
        dbs_ref[...] += dbs
        dvn = dvn_ref[...]
        dg_ref[...] += _sum0(dvn * xhat)
        db_ref[...] += _sum0(dvn)
        dv_ref[...] = (_ln_bwd(dvn, xhat, rstd, g_ref[...]) * gvp).astype(BF16)

    blk = lambda b: pl.BlockSpec((None, SGU_BLOCK, D_MODEL), functools.partial(lambda i, b: (b, i, 0), b=b))
    row = pl.BlockSpec((SGU_BLOCK, D_MODEL), lambda i: (i, 0))
    whole = lambda s: pl.BlockSpec(s, lambda i: (0,) * len(s))
    vec = (1, D_MODEL)
    return pl.pallas_call(
        body,
        grid=(t // SGU_BLOCK,),
        in_specs=[blk(SLOT_U), blk(SLOT_V), row, pl.BlockSpec(memory_space=pl.ANY),
                  whole(wm.shape), whole(wm_t.shape), whole(bs_t.shape), whole(vec), whole(vec)],
        out_specs=[pl.BlockSpec((2, SGU_BLOCK, D_MODEL), lambda i: (SLOT_U // 2, i, 0)),
                   whole(wm.shape), whole((SGU_BLOCK, LANE)), whole(vec), whole(vec)],
        out_shape=[jax.ShapeDtypeStruct(dh.shape, BF16),
                   jax.ShapeDtypeStruct(wm.shape, F32), jax.ShapeDtypeStruct((SGU_BLOCK, LANE), F32),
                   jax.ShapeDtypeStruct(vec, F32), jax.ShapeDtypeStruct(vec, F32)],
        scratch_shapes=[pltpu.VMEM((SGU_BLOCK, D_MODEL), F32)],
        input_output_aliases={3: 0},
        compiler_params=_params(("arbitrary",)),
        name="sgu_bwd",
    )(h, h, dy, dh, wm, wm_t, bs_t, g_v, b_v)


def _split3(x):
    hi = x.astype(BF16)
    r1 = x - hi.astype(F32)
    mid = r1.astype(BF16)
    lo = (r1 - mid.astype(F32)).astype(BF16)
    return hi, mid, lo


def _tri_matmul(tri, x):
    hi, mid, lo = _split3(x)
    dot = lambda p: jnp.dot(tri, p, preferred_element_type=F32)
    return dot(hi) + dot(mid) + dot(lo)


def _lower_bound(logits):
    l0, l1 = logits[0:1, :], logits[1:2, :]
    mx = jnp.maximum(l0, l1)
    e0, e1 = jnp.exp(l0 - mx), jnp.exp(l1 - mx)
    return e0 / (e0 + e1)


def _hgrn_gates(q_raw, f_raw, lb):
    q = q_raw * jax.nn.sigmoid(q_raw)
    sig = jax.nn.sigmoid(f_raw)
    f = lb + (1.0 - lb) * sig
    row = lax.broadcasted_iota(jnp.int32, (CHUNK, CHUNK), 0)
    col = lax.broadcasted_iota(jnp.int32, (CHUNK, CHUNK), 1)
    c = _tri_matmul((row >= col).astype(BF16), jnp.log(f))
    return q, sig, f, 1.0 - f, c


def _offdiag_terms(qh, kh, ch, tb):
    rows = slice(tb * SUB, (tb + 1) * SUB)
    r = ch[tb * SUB - 1:tb * SUB, :]
    eqh = jnp.exp(ch[rows] - r)
    ekh = jnp.exp(jnp.minimum(r - ch, 0.0))
    return rows, eqh, qh[rows] * eqh, ekh, kh * ekh


def _diag_decay(cb, s, trow):
    return jnp.exp(jnp.where(trow >= s, cb - cb[s:s + 1, :], NEG))


def _split_refs(refs, n_in, n_out, n_scratch, ex):
    nx = ex.n if ex else 0
    ins, refs = refs[:n_in], refs[n_in:]
    xsrc, refs = refs[:nx], refs[nx:]
    outs, refs = refs[:n_out], refs[n_out:]
    xout, refs = refs[:nx], refs[nx:]
    return ins, outs, refs[:n_scratch], (xsrc, xout, refs[n_scratch:])


def _hgrn_fwd(h, logits, g_norm, ex=None):
    t = h.shape[1]
    nc = t // CHUNK
    per_step = CHUNKS_PER_STEP if nc % CHUNKS_PER_STEP == 0 else 1
    steps = nc // per_step

    def body(*refs):
        ins, outs, scratch, xrefs = _split_refs(refs, 3, 3, 4, ex)
        hgrn_ref, lg_ref, gn_ref = ins
        y_ref, o_ref, sall_ref = outs
        st_ref, q_s, k_s, c_s = scratch

        @pl.when(pl.program_id(0) == 0)
        def _():
            st_ref[...] = jnp.zeros_like(st_ref)

        lb = _lower_bound(lg_ref[...])
        for sub in range(per_step):
            rows = pl.ds(sub * CHUNK, CHUNK)
            chunk(lb, sall_ref.at[sub], *(r.at[rows] for r in (hgrn_ref.at[SLOT_Q], hgrn_ref.at[SLOT_F], hgrn_ref.at[SLOT_I],
                                                               hgrn_ref.at[SLOT_OG], y_ref, o_ref)),
                  gn_ref, st_ref, q_s.at[sub], k_s.at[sub], c_s.at[sub])
        _ride(ex, xrefs, pl.program_id(0), steps)

    def chunk(lb, sall_ref, q_ref, f_ref, i_ref, og_ref, y_ref, o_ref, gn_ref, st_ref, q_s, k_s, c_s):
        q, _, _, k, c = _hgrn_gates(q_ref[...], f_ref[...], lb)
        q_s[...] = q
        k_s[...] = k
        c_s[...] = c
        col64 = lax.broadcasted_iota(jnp.int32, (SUB, CHUNK), 1)
        trow = lax.broadcasted_iota(jnp.int32, (SUB, HEAD), 0)

        def head(hd, carry):
            sl = pl.ds(pl.multiple_of(hd * HEAD, HEAD), HEAD)
            qh, kh, ch, ih = q_s[:, sl], k_s[:, sl], c_s[:, sl], i_ref[:, sl]
            st = st_ref[hd]
            sall_ref[hd] = st
            c_last = ch[CHUNK - 1:CHUNK, :]
            o = _dot_nt(qh * jnp.exp(ch), st)
            st_ref[hd] = st * jnp.exp(c_last) + _dot_tn(ih, kh * jnp.exp(c_last - ch))
            a_rows = [jnp.zeros((SUB, CHUNK), F32)]
            for tb in range(1, CHUNK // SUB):
                _, _, q_hat, _, k_hat = _offdiag_terms(qh, kh, ch, tb)
                a_rows.append(jnp.where(col64 < tb * SUB, _dot_nt(q_hat, k_hat), 0.0))
            o = o + _dot(jnp.concatenate(a_rows, axis=0), ih)
            o_rows = []
            for b in range(CHUNK // SUB):
                rows = slice(b * SUB, (b + 1) * SUB)
                qb, cb, kb, ib = qh[rows], ch[rows], kh[rows], ih[rows]
                ob = jnp.zeros((SUB, HEAD), F32)
                for s in range(SUB):
                    a = jnp.sum(qb * _diag_decay(cb, s, trow) * kb[s:s + 1, :], axis=1, keepdims=True)
                    ob = ob + a * ib[s:s + 1, :]
                o_rows.append(ob)
            o = o + jnp.concatenate(o_rows, axis=0)
            o_ref[:, sl] = o
            og = og_ref[:, sl]
            on = o * lax.rsqrt(_mean(o * o) + RMS_EPS)
            y_ref[:, sl] = (on * gn_ref[:, sl] * (og * jax.nn.sigmoid(og))).astype(BF16)
            return carry

        lax.fori_loop(0, HEADS, head, 0, unroll=HEAD_UNROLL)

    rows = per_step * CHUNK
    row = pl.BlockSpec((rows, D_MODEL), lambda n: (n, 0))
    whole = lambda s: pl.BlockSpec(s, lambda n: (0,) * len(s))
    res = pl.pallas_call(
        body,
        grid=(steps,),
        in_specs=[pl.BlockSpec((4, rows, D_MODEL), lambda n: (SLOT_Q // 4, n, 0)), whole(logits.shape), whole(g_norm.shape)]
        + (ex.in_specs if ex else []),
        out_specs=[row, row, pl.BlockSpec((per_step, HEADS, HEAD, HEAD), lambda n: (n, 0, 0, 0))]
        + (ex.out_specs if ex else []),
        out_shape=[jax.ShapeDtypeStruct((t, D_MODEL), BF16), jax.ShapeDtypeStruct((t, D_MODEL), F32),
                   jax.ShapeDtypeStruct((nc, HEADS, HEAD, HEAD), F32)] + (ex.out_shape if ex else []),
        scratch_shapes=[pltpu.VMEM((HEADS, HEAD, HEAD), F32)] + [pltpu.VMEM((per_step, CHUNK, D_MODEL), F32)] * 3
        + (ex.scratch if ex else []),
        compiler_params=_params(("arbitrary",)),
        name="hgrn_fwd",
    )(h, logits, g_norm, *(ex.arrays if ex else []))
    return res[0], res[1], res[2], res[3:]


def _hgrn_bwd(h, o_all, dy, states, dh, logits, g_norm, ex=None):
    t = h.shape[1]
    nc = t // CHUNK

    def body(*refs):
        ins, outs, scratch, xrefs = _split_refs(refs, 7, 3, 8, ex)
        hgrn_ref, o_ref, dy_ref, sall_ref, _, lg_ref, gn_ref = ins
        q_ref, f_ref, i_ref, og_ref = (hgrn_ref.at[s] for s in (SLOT_Q, SLOT_F, SLOT_I, SLOT_OG))
        dqfio_ref, dlg_ref, dgn_ref = outs
        dst_ref, dlb_ref, q_s, k_s, c_s, dq_s, dk_s, dc_s = scratch
        dq_ref, df_ref, di_ref, dog_ref = (dqfio_ref.at[s] for s in (SLOT_Q, SLOT_F, SLOT_I, SLOT_OG))
        n = pl.program_id(0)

        @pl.when(n == 0)
        def _():
            dst_ref[...] = jnp.zeros_like(dst_ref)
            dlb_ref[...] = jnp.zeros_like(dlb_ref)
            dgn_ref[...] = jnp.zeros_like(dgn_ref)
            if ex:
                ex.start(*xrefs)

        lb = _lower_bound(lg_ref[...])
        q_raw = q_ref[...]
        q, sig, f, k, c = _hgrn_gates(q_raw, f_ref[...], lb)
        q_s[...] = q
        k_s[...] = k
        c_s[...] = c
        col64 = lax.broadcasted_iota(jnp.int32, (SUB, CHUNK), 1)
        trow = lax.broadcasted_iota(jnp.int32, (SUB, HEAD), 0)
        row64 = lax.broadcasted_iota(jnp.int32, (CHUNK, HEAD), 0)

        def head(hd, carry):
            sl = pl.ds(pl.multiple_of(hd * HEAD, HEAD), HEAD)
            qh, kh, ch, ih = q_s[:, sl], k_s[:, sl], c_s[:, sl], i_ref[:, sl]
            st = sall_ref[hd]
            dst = dst_ref[hd]
            oh, dyh, og, gn = o_ref[:, sl], dy_ref[:, sl], og_ref[:, sl], gn_ref[:, sl]
            sg = jax.nn.sigmoid(og)
            sil = og * sg
            rms = lax.rsqrt(_mean(oh * oh) + RMS_EPS)
            on = oh * rms
            dog_ref[:, sl] = (dyh * on * gn * _silu_grad(og, sg)).astype(BF16)
            dgn_ref[:, sl] += _sum0(dyh * on * sil)
            don = dyh * gn * sil
            do = rms * (don - on * _mean(don * on))
            dob = do.astype(BF16)

            c_last = ch[CHUNK - 1:CHUNK, :]
            eq = jnp.exp(ch)
            q_til = qh * eq
            ekl = jnp.exp(c_last - ch)
            k_til = kh * ekl
            ecl = jnp.exp(c_last)
            dq_til = _dot(dob, st)
            dk_til = _dot(ih, dst)
            di = _dot_nt(k_til, dst)
            dc_last = _sum0(dk_til * k_til) + _sum0(dst * st) * ecl
            dst_ref[hd] = _dot_tn(dob, q_til) + dst * ecl
            dq = dq_til * eq
            dc = dq_til * q_til - dk_til * k_til
            dk = dk_til * ekl

            da_full = _dot_nt(dob, ih)
            a_rows = [jnp.zeros((SUB, CHUNK), F32)]
            dq_rows = [jnp.zeros((SUB, HEAD), F32)]
            dc_rows = [jnp.zeros((SUB, HEAD), F32)]
            for tb in range(1, CHUNK // SUB):
                rows, eqh, q_hat, ekh, k_hat = _offdiag_terms(qh, kh, ch, tb)
                keep = col64 < tb * SUB
                a_rows.append(jnp.where(keep, _dot_nt(q_hat, k_hat), 0.0))
                da = jnp.where(keep, da_full[rows], 0.0)
                dq_hat = _dot(da, k_hat)
                dk_hat = _dot_tn(da, q_hat)
                dq_rows.append(dq_hat * eqh)
                dc_rows.append(dq_hat * q_hat)
                dk = dk + dk_hat * ekh
                dc = dc - dk_hat * k_hat
            di = di + _dot_tn(jnp.concatenate(a_rows, axis=0), dob)

            dk_rows, di_rows = [], []
            for b in range(CHUNK // SUB):
                rows = slice(b * SUB, (b + 1) * SUB)
                qb, cb, kb, ib, dob_ = qh[rows], ch[rows], kh[rows], ih[rows], do[rows]
                dq_diag = jnp.zeros((SUB, HEAD), F32)
                dk_diag = jnp.zeros((SUB, HEAD), F32)
                di_diag = jnp.zeros((SUB, HEAD), F32)
                for s in range(SUB):
                    ks = kb[s:s + 1, :]
                    dec = _diag_decay(cb, s, trow)
                    a = jnp.sum(qb * dec * ks, axis=1, keepdims=True)
                    gk = jnp.sum(dob_ * ib[s:s + 1, :], axis=1, keepdims=True) * dec
                    dq_diag = dq_diag + gk * ks
                    dk_diag = dk_diag + jnp.where(trow == s, _sum0(gk * qb), 0.0)
                    di_diag = di_diag + jnp.where(trow == s, _sum0(a * dob_), 0.0)
                dq_rows[b] = dq_rows[b] + dq_diag
                dc_rows[b] = dc_rows[b] + qb * dq_diag - kb * dk_diag
                dk_rows.append(dk_diag)
                di_rows.append(di_diag)
            dq = dq + jnp.concatenate(dq_rows, axis=0)
            dk = dk + jnp.concatenate(dk_rows, axis=0)
            dc = dc + jnp.concatenate(dc_rows, axis=0) + jnp.where(row64 == CHUNK - 1, dc_last, 0.0)
            di_ref[:, sl] = (di + jnp.concatenate(di_rows, axis=0)).astype(BF16)
            dq_s[:, sl] = dq
            dk_s[:, sl] = dk
            dc_s[:, sl] = dc
            return carry

        lax.fori_loop(0, HEADS, head, 0, unroll=HEAD_UNROLL)

        row = lax.broadcasted_iota(jnp.int32, (CHUNK, CHUNK), 0)
        col = lax.broadcasted_iota(jnp.int32, (CHUNK, CHUNK), 1)
        dlf = _tri_matmul((row <= col).astype(BF16), dc_s[...])
        df = dlf / f - dk_s[...]
        dlb_ref[...] += _sum0(df * (1.0 - sig))
        df_ref[...] = (df * (1.0 - lb) * sig * (1.0 - sig)).astype(BF16)
        dq_ref[...] = (dq_s[...] * _silu_grad(q_raw, jax.nn.sigmoid(q_raw))).astype(BF16)

        @pl.when(n == nc - 1)
        def _():
            d0 = dlb_ref[...] * lb * (1.0 - lb)
            dlg_ref[0:1, :] = d0
            dlg_ref[1:2, :] = -d0
            if ex:
                ex.finish(*xrefs)

    rev = lambda n: nc - 1 - n
    slots = pl.BlockSpec((4, CHUNK, D_MODEL), lambda n: (SLOT_Q // 4, rev(n), 0))
    row = pl.BlockSpec((CHUNK, D_MODEL), lambda n: (rev(n), 0))
    whole = lambda s: pl.BlockSpec(s, lambda n: (0,) * len(s))
    vec = (1, D_MODEL)
    res = pl.pallas_call(
        body,
        grid=(nc,),
        in_specs=[slots, row, row, pl.BlockSpec((None, HEADS, HEAD, HEAD), lambda n: (rev(n), 0, 0, 0)),
                  pl.BlockSpec(memory_space=pl.ANY), whole(logits.shape), whole(vec)] + (ex.in_specs if ex else []),
        out_specs=[slots, whole((2, D_MODEL)), whole(vec)] + (ex.out_specs if ex else []),
        out_shape=[jax.ShapeDtypeStruct(dh.shape, BF16), jax.ShapeDtypeStruct((2, D_MODEL), F32),
                   jax.ShapeDtypeStruct(vec, F32)] + (ex.out_shape if ex else []),
        scratch_shapes=[pltpu.VMEM((HEADS, HEAD, HEAD), F32), pltpu.VMEM(vec, F32)]
        + [pltpu.VMEM((CHUNK, D_MODEL), F32)] * 6 + (ex.scratch if ex else []),
        input_output_aliases={4: 0},
        compiler_params=_params(("arbitrary",)),
        name="hgrn_bwd",
    )(h, o_all, dy, states, dh, logits, g_norm, *(ex.arrays if ex else []))
    return res[0], res[1], res[2], res[3:]


def _merge_fwd(i, n, ga, gb, za, zb):
    return jax.nn.sigmoid(ga) * za + jax.nn.sigmoid(gb) * zb


def _merge_bwd(i, n, ga, gb, za, zb, dm):
    sa, sb = jax.nn.sigmoid(ga), jax.nn.sigmoid(gb)
    dgates = jnp.stack([(dm * za * sa * (1.0 - sa)).astype(BF16), (dm * zb * sb * (1.0 - sb)).astype(BF16)])
    return dgates, dm * sa, dm * sb


def _ln1_fwd(i, n, x, r1, g, b):
    xhat, _ = _ln_stats(ALPHA * x + r1)
    x1 = xhat * g + b
    return x1, x1


def _ln1_bwd(i, n, x, r1, dx1, g):
    xhat, rstd = _ln_stats(ALPHA * x + r1)
    dz = _ln_bwd(dx1, xhat, rstd, g)
    return dz, dz, _sum0(dx1 * xhat), _sum0(dx1)


def _ln2_loss(i, n, x1, fo, pg, pp, tgt, g, b):
    sg = jax.nn.sigmoid(pg)
    xhat, rstd = _ln_stats(ALPHA * x1 + fo + sg * pp)
    diff = xhat * g + b - tgt
    loss = 0.5 * jnp.sum(_mean(diff * diff), axis=0, keepdims=True)
    dy = diff * (1.0 / D_MODEL)
    dz = _ln_bwd(dy, xhat, rstd, g)
    return (dz, dz, dz * pp * sg * (1.0 - sg), dz * sg,
            jnp.broadcast_to(loss, (8, LANE)), _sum0(dy * xhat), _sum0(dy))


def _shift_down(cur, halo, tile):
    row = lax.broadcasted_iota(jnp.int32, cur.shape, 0)
    m1 = jnp.where(row == 0, halo[7:8, :], pltpu.roll(cur, 1, 0))
    m2 = jnp.where(row == 0, halo[6:7, :], jnp.where(row == 1, halo[7:8, :], pltpu.roll(cur, 2, 0)))
    return m1, m2


def _shift_up(cur, halo, tile):
    row = lax.broadcasted_iota(jnp.int32, cur.shape, 0)
    p1 = jnp.where(row == tile - 1, halo[0:1, :], pltpu.roll(cur, tile - 1, 0))
    p2 = jnp.where(row == tile - 2, halo[0:1, :], jnp.where(row == tile - 1, halo[1:2, :], pltpu.roll(cur, tile - 2, 0)))
    return p1, p2


def _conv_pre(i, gate, halo, w, b, tile):
    halo = jnp.where(i == 0, 0.0, halo)
    m1, m2 = _shift_down(gate, halo, tile)
    return w[0:1, :] * m2 + w[1:2, :] * m1 + w[2:3, :] * gate + b, m1, m2


def _conv_fwd(tile, i, n, gate, halo, val, w, b):
    cg, _, _ = _conv_pre(i, gate, halo, w, b, tile)
    return _gelu(cg) * val


def _conv_bwd_a(tile, i, n, gate, halo, val, dhid, w, b):
    cg, m1, m2 = _conv_pre(i, gate, halo, w, b, tile)
    act, slope = _gelu_and_grad(cg)
    dcg = dhid * val * slope
    return dcg, dhid * act, _sum0(dcg * m2), _sum0(dcg * m1), _sum0(dcg * gate), _sum0(dcg)


def _conv_bwd_b(tile, i, n, dcg, halo, w):
    dcg = dcg.astype(F32)
    halo = jnp.where(i == n - 1, 0.0, halo.astype(F32))
    p1, p2 = _shift_up(dcg, halo, tile)
    return w[2:3, :] * dcg + w[1:2, :] * p1 + w[0:1, :] * p2


def _halo_spec(width, tile, t, nxt, rows=8):
    per = tile // rows
    last = t // rows - 1
    if nxt:
        return pl.BlockSpec((rows, width), lambda i: (jnp.minimum((i + 1) * per, last), 0))
    return pl.BlockSpec((rows, width), lambda i: (jnp.maximum(i * per - 1, 0), 0))


def _adamw(i, n, w, m, v, parts):
    g = parts[0].astype(F32)
    for j in range(1, parts.shape[0]):
        g = g + parts[j].astype(F32)
    m_new = ADAM_B1 * m + (1.0 - ADAM_B1) * g
    v_new = ADAM_B2 * v + (1.0 - ADAM_B2) * (g * g)
    m_hat = m_new / (1.0 - ADAM_B1 ** ADAM_STEP)
    v_hat = v_new / (1.0 - ADAM_B2 ** ADAM_STEP)
    delta = -ADAM_LR * (m_hat / (jnp.sqrt(v_hat) + ADAM_EPS) + ADAM_WD * w)
    return g, delta, m_new, v_new


def _adam_call(w, m, v, parts, name):
    r, c = w.shape
    tile = _pick(r, (256, 128)) if r > 256 else r
    spec = pl.BlockSpec((parts.shape[0], tile, c), lambda i: (0, i, 0))
    return _rowwise(_adamw, [w, m, v, (parts, spec)], [], [(c, F32)] * 4, [], tile=tile, name=name)


def _peer(k):
    x, y, c = lax.axis_index("x"), lax.axis_index("y"), lax.axis_index("c")
    px = x ^ ((k >> 2) & 1)
    py = y ^ ((k >> 1) & 1)
    pc = c ^ (k & 1)
    return (px, py, pc), 4 * px + 2 * py + pc


def _my_index():
    return 4 * lax.axis_index("x") + 2 * lax.axis_index("y") + lax.axis_index("c")


class _Exchange:
    KINDS = ("gather", "gather+relay", "scatter", "scatter+pairs")

    def __init__(self, entries):
        assert all(k in self.KINDS for _, k in entries), [k for _, k in entries]
        self.arrays = [a for a, _ in entries]
        self.scatter = [k.startswith("scatter") for _, k in entries]
        self.relayed = ["+relay" in k for _, k in entries]
        self.pairs = ["+pairs" in k for _, k in entries]
        self.n = len(entries)
        self.in_specs = [pl.BlockSpec(memory_space=pl.ANY)] * self.n
        self.out_specs = [pl.BlockSpec(memory_space=pl.ANY)] * self.n
        shapes = [tuple(a.shape[1:]) if sc else tuple(a.shape) for a, sc in zip(self.arrays, self.scatter)]
        counts = [N_CHIP if p else N_DEV for p in self.pairs]
        self.out_shape = [jax.ShapeDtypeStruct((n,) + s, a.dtype) for n, s, a in zip(counts, shapes, self.arrays)]
        per = N_DEV - 1
        self.scratch = [pltpu.SemaphoreType.DMA((self.n * per,)), pltpu.SemaphoreType.DMA((self.n * per,)),
                        pltpu.SemaphoreType.DMA((self.n,))]

    def _copies(self, srcs, outs, sems):
        send_sems, recv_sems, local_sems = sems
        x, y, c = lax.axis_index("x"), lax.axis_index("y"), lax.axis_index("c")
        me = _my_index()
        per = N_DEV - 1
        local, first, passed, relay_arrivals, arrivals = [], [], [], [], []
        for a in range(self.n):

            def copy(k, src, dst, dev, a=a):
                return pltpu.make_async_remote_copy(
                    src_ref=src, dst_ref=dst, send_sem=send_sems.at[a * per + k], recv_sem=recv_sems.at[a * per + k],
                    device_id=dev, device_id_type=pl.DeviceIdType.MESH)

            if self.pairs[a]:
                chip = 2 * x + y
                for k in range(N_CHIP):
                    to = chip ^ k
                    piece = srcs[a].at[_slot_of_group(2 * to + c) // 2]
                    if k == 0:
                        local.append(pltpu.make_async_copy(piece, outs[a].at[chip], local_sems.at[a]))
                    else:
                        dev = (to // 2, to % 2, c)
                        first.append(copy(k - 1, piece, outs[a].at[chip], dev))
                        arrivals.append(copy(k - 1, piece, outs[a].at[to], dev))
                continue
            mine = srcs[a].at[me] if self.scatter[a] else srcs[a]
            land = outs[a].at[me]
            local.append(pltpu.make_async_copy(mine, land, local_sems.at[a]))
            if self.relayed[a]:
                block = lambda px, py, pc, a=a: outs[a].at[4 * px + 2 * py + pc]
                chips = [(1 - x, y), (x, 1 - y), (1 - x, 1 - y)]
                first.append(copy(0, mine, land, (x, y, 1 - c)))
                arrivals.append(copy(0, mine, block(x, y, 1 - c), (x, y, 1 - c)))
                for j, (px, py) in enumerate(chips):
                    first.append(copy(1 + j, mine, land, (px, py, c)))
                    relay_arrivals.append(copy(1 + j, mine, block(px, py, c), (px, py, c)))
                    passed.append(copy(4 + j, block(px, py, c), block(px, py, c), (x, y, 1 - c)))
                    arrivals.append(copy(4 + j, mine, block(px, py, 1 - c), (x, y, 1 - c)))
                continue
            for k in range(1, N_DEV):
                dev, idx = _peer(k)
                if self.scatter[a]:
                    first.append(copy(k - 1, srcs[a].at[idx], land, dev))
                else:
                    first.append(copy(k - 1, mine, land, dev))
                arrivals.append(copy(k - 1, mine, outs[a].at[idx], dev))
        return local, first, passed, relay_arrivals, arrivals

    def start(self, srcs, outs, sems):
        local, first, _, _, _ = self._copies(srcs, outs, sems)
        for cp in local + first:
            cp.start()

    def relay(self, srcs, outs, sems):
        _, _, passed, relay_arrivals, _ = self._copies(srcs, outs, sems)
        for landed, onward in zip(relay_arrivals, passed):
            landed.wait_recv()
            onward.start()

    def finish(self, srcs, outs, sems):
        local, first, passed, _, arrivals = self._copies(srcs, outs, sems)
        for cp in arrivals:
            cp.wait_recv()
        for cp in first + passed:
            cp.wait_send()
        for cp in local:
            cp.wait()


def _gather_project(x_b, shard):
    t, d = x_b.shape
    tm = _pick(t, MM_TILES)
    nrow = t // tm
    per = N_DEV - 1

    def parties():
        x, y, c = lax.axis_index("x"), lax.axis_index("y"), lax.axis_index("c")
        first = (jnp.where(c == 0, 1 - x, x), jnp.where(c == 0, y, 1 - y))
        second = (jnp.where(c == 0, x, 1 - x), jnp.where(c == 0, 1 - y, y))
        diagonal = (1 - x, 1 - y)
        return ((x, y, c), (x, y, 1 - c), [(px, py, c) for px, py in (first, second, diagonal)],
                [(px, py, 1 - c) for px, py in (second, first, diagonal)])

    slot = lambda dev: _slot_of_group(4 * dev[0] + 2 * dev[1] + dev[2])
    ici_step, passed_step = (2, 3, 6), (4, 5, 7)
    me, sibling, over_ici, passed_on = parties()
    by_step = {0: me, 1: sibling, **dict(zip(ici_step, over_ici)), **dict(zip(passed_step, passed_on))}
    order = jnp.stack([slot(by_step[j]) for j in range(N_DEV)]).astype(jnp.int32)

    def body(order_ref, x_ref, shard_ref, h_ref, wall_ref, wbuf, fetch_sem, send_sems, recv_sems, local_sem):
        del order_ref
        me, sibling, over_ici, passed_on = parties()
        j, i = pl.program_id(0), pl.program_id(1)
        land = lambda dev: wall_ref.at[slot(dev)]

        def copy(k, src, block, to):
            return pltpu.make_async_remote_copy(src_ref=src, dst_ref=land(block), send_sem=send_sems.at[k],
                                                recv_sem=recv_sems.at[k], device_id=to, device_id_type=pl.DeviceIdType.MESH)

        def fetch(src):
            cp = pltpu.make_async_copy(src, wbuf, fetch_sem)
            cp.start()
            cp.wait()

        keep = pltpu.make_async_copy(shard_ref, land(me), local_sem)
        first = [copy(0, shard_ref, me, sibling)] + [copy(1 + n, shard_ref, me, dev) for n, dev in enumerate(over_ici[:2])]
        relay = copy(3, land(over_ici[0]), over_ici[0], over_ici[1])
        onward = [copy(4 + n, land(dev), dev, sibling) for n, dev in enumerate(over_ici)]

        @pl.when(jnp.logical_and(i == 0, j == 0))
        def _():
            for cp in reversed(first):
                cp.start()
            keep.start()
            fetch(shard_ref)

        @pl.when(jnp.logical_and(i == 0, j == 1))
        def _():
            copy(0, shard_ref, sibling, me).wait_recv()
            fetch(land(sibling))

        for n, dev in enumerate(over_ici):
            @pl.when(jnp.logical_and(i == 0, j == ici_step[n]))
            def _(n=n, dev=dev):
                copy(1 + n, shard_ref, dev, me).wait_recv()
                if n == 0:
                    relay.start()
                onward[n].start()
                fetch(land(dev))

        for n, dev in enumerate(passed_on):
            @pl.when(jnp.logical_and(i == 0, j == passed_step[n]))
            def _(n=n, dev=dev):
                copy(4 + n, shard_ref, dev, me).wait_recv()
                fetch(land(dev))

        h_ref[...] = _dot(x_ref[...], wbuf[...])

        @pl.when(jnp.logical_and(i == nrow - 1, j == N_DEV - 1))
        def _():
            for cp in first + [relay] + onward:
                cp.wait_send()
            keep.wait()

    need = 2 * (tm * d * 2 + tm * d * 4) + d * d * 2 + tm * d * 4
    h, w_all = pl.pallas_call(
        body,
        grid_spec=pltpu.PrefetchScalarGridSpec(
            num_scalar_prefetch=1,
            grid=(N_DEV, nrow),
            in_specs=[pl.BlockSpec((tm, d), lambda j, i, order: (i, 0)), pl.BlockSpec(memory_space=pl.ANY)],
            out_specs=[pl.BlockSpec((None, tm, d), lambda j, i, order: (order[j], i, 0)), pl.BlockSpec(memory_space=pl.ANY)],
            scratch_shapes=[pltpu.VMEM((d, d), BF16), pltpu.SemaphoreType.DMA, pltpu.SemaphoreType.DMA((per,)),
                            pltpu.SemaphoreType.DMA((per,)), pltpu.SemaphoreType.DMA],
        ),
        out_shape=[jax.ShapeDtypeStruct((N_DEV, t, d), F32), jax.ShapeDtypeStruct((N_DEV, d, d), BF16)],
        compiler_params=_params(("arbitrary", "arbitrary"), need),
        name="gather_project",
    )(order, x_b, shard)
    return h, w_all


def _pair_sums(g):
    n, r, cols = g.shape
    half = n // 2

    def swap(g_ref, got_ref, send_sems, recv_sems):
        x, y, c = lax.axis_index("x"), lax.axis_index("y"), lax.axis_index("c")
        copies = [pltpu.make_async_remote_copy(
            src_ref=g_ref.at[2 * j + 1 - c], dst_ref=got_ref.at[j], send_sem=send_sems.at[j], recv_sem=recv_sems.at[j],
            device_id=(x, y, 1 - c), device_id_type=pl.DeviceIdType.MESH) for j in range(half)]
        for cp in copies:
            cp.start()
        for cp in copies:
            cp.wait()

    got = pl.pallas_call(
        swap,
        in_specs=[pl.BlockSpec(memory_space=pl.ANY)],
        out_specs=pl.BlockSpec(memory_space=pl.ANY),
        out_shape=jax.ShapeDtypeStruct((half, r, cols), g.dtype),
        scratch_shapes=[pltpu.SemaphoreType.DMA((half,))] * 2,
        name="pair_swap",
    )(g)

    def add(mine_ref, got_ref, out_ref):
        mine = jnp.where(lax.axis_index("c") == 0, mine_ref[0].astype(F32), mine_ref[1].astype(F32))
        out_ref[...] = (mine + got_ref[...].astype(F32)).astype(out_ref.dtype)

    tile = _pick(r, (512, 256, 128))
    return pl.pallas_call(
        add,
        grid=(half, r // tile),
        in_specs=[pl.BlockSpec((None, 2, tile, cols), lambda j, i: (j, 0, i, 0)),
                  pl.BlockSpec((None, tile, cols), lambda j, i: (j, i, 0))],
        out_specs=pl.BlockSpec((None, tile, cols), lambda j, i: (j, i, 0)),
        out_shape=jax.ShapeDtypeStruct((half, r, cols), g.dtype),
        compiler_params=_params(("parallel", "parallel")),
        name="pair_add",
    )(g.reshape(half, 2, r, cols), got)


def _local_step(x, p, tgt, small, comm):
    t = x.shape[0]
    tile = _pick(t, (256, 128))
    tall = _pick(t, (512, 256, 128))
    d = D_MODEL
    act_b, act_f = (d, BF16), (d, F32)
    x_b, p_b = x.astype(BF16), p.astype(BF16)

    chunk_id = jnp.arange(SGU_BLOCK) // CHUNK
    mask = chunk_id[:, None] >= chunk_id[None, :]
    wm = jnp.where(mask[None], small["sgu_w_s"], 0.0)
    wm_b = wm.astype(BF16)
    wm_t = jnp.swapaxes(wm, 1, 2).astype(BF16)
    bs_t = small["sgu_b_s"].T

    h, w_in = comm.project_in(x_b)
    y_a, got_a = _sgu_fwd(h, wm_b, bs_t, small["sgu_norm_g"], small["sgu_norm_b"], ex=comm.weights_exchange(0))
    y_b, o_all, states, got_b = _hgrn_fwd(h, small["lb_logits"], small["hgrn_norm_g"], ex=comm.weights_exchange(1))
    wts, conv_w = comm.weights(list(got_a) + list(got_b))
    z_a = _mm(y_a, wts["w_a"], out_dtype=F32, name="mm_za")
    z_b = _mm(y_b, wts["w_b"], out_dtype=F32, name="mm_zb")
    gates = [(h, SLOT_GA), (h, SLOT_GB)]
    merged, = _rowwise(_merge_fwd, gates + [z_a, z_b], [], [act_b], [], tile=tall, name="merge_fwd")
    r1 = _mm(merged, wts["w_o"], out_dtype=F32, name="mm_r1")
    x1, x1_b = _rowwise(_ln1_fwd, [x, r1], [small["ln1_g"], small["ln1_b"]], [act_f, act_b], [], tile=tall, name="ln1_fwd")
    gate = _mm(x1_b, wts["w_g"], out_dtype=F32, name="mm_gate")
    val = _mm(x1_b, wts["w_v"], out_dtype=F32, name="mm_val")
    pg = _mm(x1_b, wts["w_pg"], out_dtype=F32, name="mm_pg")
    pp = _mm(p_b, wts["w_pp"], out_dtype=F32, name="mm_pp")
    hid, = _rowwise(functools.partial(_conv_fwd, tile), [gate, (gate, _halo_spec(D_FF, tile, t, False)), val],
                    [conv_w, small["conv_b"]], [(D_FF, BF16)], [], tile=tile, name="conv_fwd")
    fo = _mm(hid, wts["w_down"], out_dtype=F32, name="mm_down")
    dz2, dz2_b, dpg, dpp, loss, dg2, db2 = _rowwise(
        _ln2_loss, [x1, fo, pg, pp, tgt], [small["ln2_g"], small["ln2_b"]],
        [act_f, act_b, act_b, act_b], [(8, LANE), (1, d), (1, d)], tile=tall, name="ln2_loss")

    dhid = _mm(dz2_b, wts["w_down"], out_dtype=BF16, name="mm_dhid", trans_b=True)
    g_down = _mm_tn(hid, dz2_b, out_dtype=BF16, name="mm_g_down")
    dcg, dval, dcw0, dcw1, dcw2, dcb = _rowwise(
        functools.partial(_conv_bwd_a, tile), [gate, (gate, _halo_spec(D_FF, tile, t, False)), val, dhid],
        [conv_w, small["conv_b"]], [(D_FF, BF16), (D_FF, BF16)], [(1, D_FF)] * 4, tile=tile, name="conv_bwd_a")
    dgate, = _rowwise(functools.partial(_conv_bwd_b, tile), [dcg, (dcg, _halo_spec(D_FF, tile, t, True, rows=16))],
                      [conv_w], [(D_FF, BF16)], [], tile=tile, name="conv_bwd_b")
    g_g = _mm_tn(x1_b, dgate, out_dtype=BF16, name="mm_g_gate")
    g_v = _mm_tn(x1_b, dval, out_dtype=BF16, name="mm_g_val")
    g_pg = _mm_tn(x1_b, dpg, out_dtype=BF16, name="mm_g_pg")
    g_pp = _mm_tn(p_b, dpp, out_dtype=BF16, name="mm_g_pp")
    dx1 = _mm(dgate, wts["w_g"], out_dtype=F32, name="mm_dx1_gate", trans_b=True, adds=[(dz2, ALPHA)])
    dx1 = _mm(dval, wts["w_v"], out_dtype=F32, name="mm_dx1_val", trans_b=True, adds=[(dx1, 1.0)])
    dx1 = _mm(dpg, wts["w_pg"], out_dtype=F32, name="mm_dx1_pg", trans_b=True, adds=[(dx1, 1.0)])
    dz1, dz1_b, dg1, db1 = _rowwise(_ln1_bwd, [x, r1, dx1], [small["ln1_g"]], [act_f, act_b], [(1, d), (1, d)],
                                    tile=tall, name="ln1_bwd")
    g_o = _mm_tn(merged, dz1_b, out_dtype=BF16, name="mm_g_o")
    dm = _mm(dz1_b, wts["w_o"], out_dtype=BF16, name="mm_dm", trans_b=True)
    dh, dza, dzb = _rowwise(_merge_bwd, gates + [z_a, z_b, dm], [],
                            [("stack", 2, SLOT_GA // 2, 8, d, BF16), act_b, act_b], [], tile=tall, name="merge_bwd")
    g_a = _mm_tn(y_a, dza, out_dtype=BF16, name="mm_g_a")
    g_b = _mm_tn(y_b, dzb, out_dtype=BF16, name="mm_g_b")
    dy_a = _mm(dza, wts["w_a"], out_dtype=BF16, name="mm_dya", trans_b=True)
    dy_b = _mm(dzb, wts["w_b"], out_dtype=F32, name="mm_dyb", trans_b=True)
    dh, dws, dbs, dgv_n, dbv_n = _sgu_bwd(h, dy_a, dh, wm_b, wm_t, bs_t, small["sgu_norm_g"], small["sgu_norm_b"])
    big = dict(w_a=g_a, w_b=g_b, w_o=g_o, w_g=g_g, w_v=g_v, w_down=g_down, w_pp=g_pp, w_pg=g_pg)
    sm = dict(sgu_w_s=jnp.where(mask[None], dws, 0.0), sgu_b_s=dbs[:, :GROUPS].T, sgu_norm_g=dgv_n, sgu_norm_b=dbv_n,
              ln1_g=dg1, ln1_b=db1, conv_w=jnp.concatenate([dcw0, dcw1, dcw2], axis=0), conv_b=dcb, ln2_g=dg2, ln2_b=db2,
              loss=loss)
    dh, dlogits, dgn, got = _hgrn_bwd(h, o_all, dy_b, states, dh, small["lb_logits"], small["hgrn_norm_g"],
                                      ex=comm.grads_exchange(big, sm))
    comm.grads_done(got)
    g_in = _mm_tn(x_b, dh, out_dtype=BF16, name="mm_g_in")
    ex = comm.last_exchange(g_in, dict(lb_logits=dlogits, hgrn_norm_g=dgn))
    res = _mm(dh, w_in, out_dtype=F32, name="mm_dx", trans_b=True, reduce_b=True, adds=[(dz1, ALPHA)], ex=ex)
    grad_x, got = res if ex else (res, ())
    comm.last_done(got)
    return grad_x


_SMALL_EARLY = ["sgu_w_s", "sgu_b_s", "sgu_norm_g", "sgu_norm_b", "ln1_g", "ln1_b", "ffn_conv_b", "ln2_g", "ln2_b"]
_SMALL_LATE = ["hgrn_lb_logits", "hgrn_norm_g"]
N_TAPS = D_FF // N_DEV
UP_COLS = 2 * D_FF // N_DEV


class _StepExchanges:
    def __init__(self, w_in_shard, shards):
        self.w_in_shard = w_in_shard
        self.shards = shards

    def project_in(self, x_b):
        return _gather_project(x_b, self.w_in_shard)

    def weights_exchange(self, part):
        return _Exchange([(s, "gather+relay") for s in self.shards]) if part == 1 else None

    def weights(self, got):
        d, f = D_MODEL, D_FF
        w_br_g, w_o_g, w_up_g, w_down_g, w_pp_g, w_pg_g, conv_g = got
        w_br = w_br_g.transpose(1, 0, 2, 3).reshape(2, d, d)
        w_up = w_up_g.transpose(1, 0, 2).reshape(d, 2, f).transpose(1, 0, 2)
        wts = dict(w_a=w_br[0], w_b=w_br[1], w_o=w_o_g.reshape(d, d), w_g=w_up[0], w_v=w_up[1],
                   w_down=w_down_g.reshape(f, d), w_pp=w_pp_g.transpose(1, 0, 2).reshape(256, d), w_pg=w_pg_g.reshape(d, d))
        return wts, conv_g.transpose(1, 0, 2).reshape(3, f)

    def grads_exchange(self, big, sm):
        d = D_MODEL
        parts = [jnp.stack([big["w_a"], big["w_b"]]).reshape(2, N_DEV, 128, d).transpose(1, 0, 2, 3),
                 big["w_o"].reshape(N_DEV, 128, d),
                 jnp.concatenate([big["w_g"], big["w_v"]], axis=1).reshape(d, N_DEV, UP_COLS).transpose(1, 0, 2),
                 big["w_down"].reshape(N_DEV, N_TAPS, d),
                 big["w_pp"].reshape(256, N_DEV, 128).transpose(1, 0, 2),
                 big["w_pg"].reshape(N_DEV, 128, d)]
        packed, self.rows_early = _pack([sm[k] for k in ("sgu_w_s", "sgu_b_s", "sgu_norm_g", "sgu_norm_b", "ln1_g", "ln1_b",
                                                         "conv_b", "ln2_g", "ln2_b", "conv_w", "loss")])
        return _Exchange([(a, "scatter") for a in parts] + [(packed, "gather")])

    def grads_done(self, got):
        self.recv, self.small_early = got[:6], got[6]

    def last_exchange(self, g_in, sm):
        packed, self.rows_late = _pack([sm["lb_logits"], sm["hgrn_norm_g"]])
        return _Exchange([(_pair_sums(g_in), "scatter+pairs"), (packed, "gather")])

    def last_done(self, got):
        self.recv_in, self.small_late = got


def _rows128(a):
    flat = a.reshape(-1)
    rows = -(-flat.shape[0] // (8 * LANE)) * 8
    return jnp.pad(flat, (0, rows * LANE - flat.shape[0])).reshape(rows, LANE)


def _pack(parts):
    blocks = [_rows128(a) for a in parts]
    return jnp.concatenate(blocks, axis=0), [b.shape[0] for b in blocks]


def _unpack(packed, shapes, rows):
    out, r0 = [], 0
    for shp, r in zip(shapes, rows):
        n = math.prod(shp)
        out.append(packed[r0:r0 + r].reshape(-1)[:n].reshape(shp))
        r0 += r
    return out


def kernel(x, p, w_in, sgu_w_s, sgu_b_s, sgu_norm_g, sgu_norm_b, hgrn_lb_logits, hgrn_norm_g, w_branch, w_out, ln1_g, ln1_b, ffn_w_up, ffn_conv_w, ffn_conv_b, ffn_w_down, ln2_g, ln2_b, ple_w_proj, ple_w_gate, loss_target, m_w_in, m_sgu_w_s, m_sgu_b_s, m_sgu_norm_g, m_sgu_norm_b, m_hgrn_lb_logits, m_hgrn_norm_g, m_w_branch, m_w_out, m_ln1_g, m_ln1_b, m_ffn_w_up, m_ffn_conv_w, m_ffn_conv_b, m_ffn_w_down, m_ln2_g, m_ln2_b, m_ple_w_proj, m_ple_w_gate, v_w_in, v_sgu_w_s, v_sgu_b_s, v_sgu_norm_g, v_sgu_norm_b, v_hgrn_lb_logits, v_hgrn_norm_g, v_w_branch, v_w_out, v_ln1_g, v_ln1_b, v_ffn_w_up, v_ffn_conv_w, v_ffn_conv_b, v_ffn_w_down, v_ln2_g, v_ln2_b, v_ple_w_proj, v_ple_w_gate):
    weights = dict(w_in=w_in, sgu_w_s=sgu_w_s, sgu_b_s=sgu_b_s, sgu_norm_g=sgu_norm_g, sgu_norm_b=sgu_norm_b,
                   hgrn_lb_logits=hgrn_lb_logits, hgrn_norm_g=hgrn_norm_g, w_branch=w_branch, w_out=w_out,
                   ln1_g=ln1_g, ln1_b=ln1_b, ffn_w_up=ffn_w_up, ffn_conv_w=ffn_conv_w, ffn_conv_b=ffn_conv_b,
                   ffn_w_down=ffn_w_down, ln2_g=ln2_g, ln2_b=ln2_b, ple_w_proj=ple_w_proj, ple_w_gate=ple_w_gate)
    mom_m = dict(w_in=m_w_in, sgu_w_s=m_sgu_w_s, sgu_b_s=m_sgu_b_s, sgu_norm_g=m_sgu_norm_g, sgu_norm_b=m_sgu_norm_b,
                 hgrn_lb_logits=m_hgrn_lb_logits, hgrn_norm_g=m_hgrn_norm_g, w_branch=m_w_branch, w_out=m_w_out,
                 ln1_g=m_ln1_g, ln1_b=m_ln1_b, ffn_w_up=m_ffn_w_up, ffn_conv_w=m_ffn_conv_w, ffn_conv_b=m_ffn_conv_b,
                 ffn_w_down=m_ffn_w_down, ln2_g=m_ln2_g, ln2_b=m_ln2_b, ple_w_proj=m_ple_w_proj, ple_w_gate=m_ple_w_gate)
    mom_v = dict(w_in=v_w_in, sgu_w_s=v_sgu_w_s, sgu_b_s=v_sgu_b_s, sgu_norm_g=v_sgu_norm_g, sgu_norm_b=v_sgu_norm_b,
                 hgrn_lb_logits=v_hgrn_lb_logits, hgrn_norm_g=v_hgrn_norm_g, w_branch=v_w_branch, w_out=v_w_out,
                 ln1_g=v_ln1_g, ln1_b=v_ln1_b, ffn_w_up=v_ffn_w_up, ffn_conv_w=v_ffn_conv_w, ffn_conv_b=v_ffn_conv_b,
                 ffn_w_down=v_ffn_w_down, ln2_g=v_ln2_g, ln2_b=v_ln2_b, ple_w_proj=v_ple_w_proj, ple_w_gate=v_ple_w_gate)
    d, f = D_MODEL, D_FF
    me = _my_index()

    comm = _StepExchanges(w_in[0].astype(BF16),
                          [w_branch[0].astype(BF16), w_out[0].astype(BF16), ffn_w_up[0].astype(BF16),
                           ffn_w_down[0].astype(BF16), ple_w_proj[0].astype(BF16), ple_w_gate[0].astype(BF16), ffn_conv_w[0]])
    small = dict(sgu_w_s=sgu_w_s[0], sgu_b_s=sgu_b_s[0], sgu_norm_g=sgu_norm_g, sgu_norm_b=sgu_norm_b,
                 lb_logits=hgrn_lb_logits, hgrn_norm_g=hgrn_norm_g, ln1_g=ln1_g, ln1_b=ln1_b, ln2_g=ln2_g, ln2_b=ln2_b,
                 conv_b=ffn_conv_b)
    grad_x = _local_step(x[0], p[0, 0], loss_target[0], small, comm)

    out = {}

    def adam(name, parts8, shape2d):
        w2, m2, v2 = (a.reshape(shape2d) for a in (weights[name], mom_m[name], mom_v[name]))
        res = _adam_call(w2, m2, v2, parts8.reshape(parts8.shape[:1] + shape2d), "adam_" + name)
        out[name] = tuple(r.reshape(weights[name].shape) for r in res)

    adam("w_in", comm.recv_in, (d, d))
    adam("w_branch", comm.recv[0], (256, d))
    adam("w_out", comm.recv[1], (128, d))
    adam("ffn_w_up", comm.recv[2], (d, UP_COLS))
    adam("ffn_w_down", comm.recv[3], (N_TAPS, d))
    adam("ple_w_proj", comm.recv[4], (256, 128))
    adam("ple_w_gate", comm.recv[5], (128, d))

    def adam_small(names, extra_w, extra_m, extra_v, extra_shapes, parts8, rows, label):
        pk = lambda src, extra: _pack([src[n] for n in names] + extra)[0]
        res = _adam_call(pk(weights, extra_w), pk(mom_m, extra_m), pk(mom_v, extra_v), parts8, label)
        shapes = [weights[n].shape for n in names] + extra_shapes
        unpacked = [_unpack(r, shapes, rows) for r in res]
        for j, n in enumerate(names):
            out[n] = tuple(u[j] for u in unpacked)
        return [[u[len(names) + j] for u in unpacked] for j in range(len(extra_shapes))]

    blank = jnp.zeros((8, LANE), F32)
    taps, loss_rows = adam_small(
        _SMALL_EARLY, [_place_taps(ffn_conv_w[0], me, f), blank], [_place_taps(m_ffn_conv_w[0], me, f), blank],
        [_place_taps(v_ffn_conv_w[0], me, f), blank + 1.0], [(3, f), (8, LANE)], comm.small_early, comm.rows_early,
        "adam_small_early")
    adam_small(_SMALL_LATE, [], [], [], [], comm.small_late, comm.rows_late, "adam_small_late")
    out["ffn_conv_w"] = tuple(lax.dynamic_slice_in_dim(u, me * N_TAPS, N_TAPS, axis=1)[None] for u in taps)
    loss = loss_rows[0][0, 0]

    order = ["w_in", "sgu_w_s", "sgu_b_s", "sgu_norm_g", "sgu_norm_b", "hgrn_lb_logits", "hgrn_norm_g", "w_branch", "w_out",
             "ln1_g", "ln1_b", "ffn_w_up", "ffn_conv_w", "ffn_conv_b", "ffn_w_down", "ln2_g", "ln2_b", "ple_w_proj", "ple_w_gate"]
    return (loss, grad_x[None], *[out[n][0] for n in order], *[out[n][1] for n in order],
            *[out[n][2] for n in order], *[out[n][3] for n in order])


def _place_taps(shard, me, f):
    return lax.dynamic_update_slice_in_dim(jnp.zeros((3, f), F32), shard, me * N_TAPS, axis=1)
```

```python
import functools
import math

import jax
import jax.numpy as jnp
from jax import lax
from jax.experimental import pallas as pl
from jax.experimental.pallas import tpu as pltpu

F32 = jnp.float32
BF16 = jnp.bfloat16

N_DEV = 8
N_CHIP = 4
D_MODEL = 1024
CHUNK = 64
SUB = 16
SGU_BLOCK = 128
GROUPS = 8
HEAD = 128
HEADS = 8
CHUNKS_PER_STEP = 2
HEAD_UNROLL = 8
D_FF = 2816
LN_EPS = 1e-5
RMS_EPS = 1e-6
ALPHA = 2.0 ** 0.25
GELU_K = math.sqrt(2.0 / math.pi)
GELU_C = 0.044715
NEG = -1e30
ADAM_LR, ADAM_B1, ADAM_B2, ADAM_EPS, ADAM_WD, ADAM_STEP = 0.001, 0.9, 0.999, 1e-08, 0.01, 10
LANE = 128
SLOT_Q, SLOT_F, SLOT_I, SLOT_OG, SLOT_U, SLOT_V, SLOT_GA, SLOT_GB = range(8)


def _slot_of_group(k):
    return jnp.where(k < 2, k + 4, jnp.where(k < 6, k - 2, k))


MIB = 1024 * 1024
VMEM_V7X = 64 * MIB
VMEM_FLOOR = 32 * MIB
MM_TILES = (1024, 1408, 512, 256, 128)


def _params(sem, need=0):
    limit = min(max(need + need // 4, VMEM_FLOOR), VMEM_V7X - 4 * MIB)
    return pltpu.CompilerParams(dimension_semantics=sem, vmem_limit_bytes=limit)


def _pick(n, prefs):
    for t in prefs:
        if n % t == 0:
            return t
    return n


def _gelu(x):
    return 0.5 * x * (1.0 + jnp.tanh(GELU_K * (x + GELU_C * x * x * x)))


def _gelu_and_grad(x):
    x2 = x * x
    t = jnp.tanh(GELU_K * x * (1.0 + GELU_C * x2))
    half = 0.5 * (1.0 + t)
    return x * half, half + 0.5 * x * (1.0 - t * t) * GELU_K * (1.0 + 3.0 * GELU_C * x2)


def _silu_grad(x, s):
    return s * (1.0 + x * (1.0 - s))


def _dot(a, b):
    return jnp.dot(a.astype(BF16), b.astype(BF16), preferred_element_type=F32)


def _dot_nt(a, b):
    return lax.dot_general(a.astype(BF16), b.astype(BF16), (((1,), (1,)), ((), ())), preferred_element_type=F32)


def _dot_tn(a, b):
    return lax.dot_general(a.astype(BF16), b.astype(BF16), (((0,), (0,)), ((), ())), preferred_element_type=F32)


def _mean(x):
    return jnp.mean(x, axis=-1, keepdims=True)


def _sum0(x):
    return jnp.sum(x, axis=0, keepdims=True)


def _mm(a, b, *, out_dtype, name, trans_b=False, reduce_b=False, adds=(), ex=None):
    squeeze = b.ndim == 2
    a3 = a if a.ndim == 3 else a[None]
    b3 = b if b.ndim == 3 else b[None]
    ba, m, k = a3.shape
    bb = b3.shape[0]
    n = b3.shape[1] if trans_b else b3.shape[2]
    tm = _pick(m, MM_TILES)
    tn = _pick(n, MM_TILES)
    if reduce_b:
        bo, steps = 1, bb
        a_map = lambda o, i, j, r: (r if ba > 1 else 0, i, 0)
        b_map = (lambda o, i, j, r: (r, j, 0)) if trans_b else (lambda o, i, j, r: (r, 0, j))
    else:
        bo, steps = bb, 1
        a_map = lambda o, i, j, r: (o if ba > 1 else 0, i, 0)
        b_map = (lambda o, i, j, r: (o, j, 0)) if trans_b else (lambda o, i, j, r: (o, 0, j))
    o_map = lambda o, i, j, r: (o, i, j)
    add_arrays = [x if x.ndim == 3 else x[None] for x, _ in adds]
    add_scales = [s for _, s in adds]
    n_add = len(adds)
    dot = _dot_nt if trans_b else _dot

    def finish(acc, add_refs, o_ref):
        for ref, s in zip(add_refs, add_scales):
            acc = acc + s * ref[...].astype(F32)
        o_ref[...] = acc.astype(o_ref.dtype)

    grid = (bo, m // tm, n // tn, steps)

    def body(*refs):
        ins, (o_ref,), scratch, xrefs = _split_refs(refs, 2 + n_add, 1, 1 if reduce_b else 0, ex)
        a_ref, b_ref, add_refs = ins[0], ins[1], ins[2:]
        step = ((pl.program_id(0) * grid[1] + pl.program_id(1)) * grid[2] + pl.program_id(2)) * grid[3] + pl.program_id(3)
        if ex:
            @pl.when(step == 0)
            def _():
                ex.start(*xrefs)

        if reduce_b:
            acc, = scratch
            r = pl.program_id(3)

            @pl.when(r == 0)
            def _():
                acc[...] = jnp.zeros_like(acc)

            acc[...] += dot(a_ref[...], b_ref[...])

            @pl.when(r == steps - 1)
            def _():
                finish(acc[...], add_refs, o_ref)
        else:
            finish(dot(a_ref[...], b_ref[...]), add_refs, o_ref)

        if ex:
            @pl.when(step == math.prod(grid) - 1)
            def _():
                ex.finish(*xrefs)

    b_block = (None, tn, k) if trans_b else (None, k, tn)
    out_bytes = tm * tn * jnp.dtype(out_dtype).itemsize
    need = 2 * (tm * k * a3.dtype.itemsize + k * tn * b3.dtype.itemsize + out_bytes + n_add * tm * tn * 4)
    need += 2 * tm * tn * 4
    sem = ("arbitrary",) * 4 if ex else ("parallel", "parallel", "parallel", "arbitrary")
    res = pl.pallas_call(
        body,
        grid=grid,
        in_specs=[pl.BlockSpec((None, tm, k), a_map), pl.BlockSpec(b_block, b_map)]
        + [pl.BlockSpec((None, tm, tn), o_map) for _ in adds] + (ex.in_specs if ex else []),
        out_specs=[pl.BlockSpec((None, tm, tn), o_map)] + (ex.out_specs if ex else []),
        out_shape=[jax.ShapeDtypeStruct((bo, m, n), out_dtype)] + (ex.out_shape if ex else []),
        scratch_shapes=([pltpu.VMEM((tm, tn), F32)] if reduce_b else []) + (ex.scratch if ex else []),
        compiler_params=_params(sem, need),
        name=name,
    )(a3, b3, *add_arrays, *(ex.arrays if ex else []))
    out = res[0][0] if (reduce_b or squeeze) else res[0]
    return (out, res[1:]) if ex else out


def _mm_tn(a, b, *, out_dtype, name):
    squeeze = b.ndim == 2
    b3 = b if b.ndim == 3 else b[None]
    t, m = a.shape
    bb, _, n = b3.shape
    tm = _pick(m, MM_TILES)
    tn = _pick(n, MM_TILES)
    tt = _pick(t, (1024, 512, 256, 128))
    steps = t // tt
    need = 2 * (tt * tm * a.dtype.itemsize + tt * tn * b3.dtype.itemsize + tm * tn * jnp.dtype(out_dtype).itemsize)
    need += 2 * tm * tn * 4

    def body(a_ref, b_ref, o_ref, acc):
        r = pl.program_id(3)

        @pl.when(r == 0)
        def _():
            acc[...] = jnp.zeros_like(acc)

        acc[...] += _dot_tn(a_ref[...], b_ref[...])

        @pl.when(r == steps - 1)
        def _():
            o_ref[...] = acc[...].astype(o_ref.dtype)

    out = pl.pallas_call(
        body,
        grid=(bb, m // tm, n // tn, steps),
        in_specs=[pl.BlockSpec((tt, tm), lambda o, i, j, r: (r, i)),
                  pl.BlockSpec((None, tt, tn), lambda o, i, j, r: (o, r, j))],
        out_specs=pl.BlockSpec((None, tm, tn), lambda o, i, j, r: (o, i, j)),
        out_shape=jax.ShapeDtypeStruct((bb, m, n), out_dtype),
        scratch_shapes=[pltpu.VMEM((tm, tn), F32)],
        compiler_params=_params(("parallel", "parallel", "parallel", "arbitrary"), need),
        name=name,
    )(a, b3)
    return out[0] if squeeze else out


def _rowwise(fn, rows, consts, row_outs, acc_outs, *, tile, name):
    first = rows[0][0] if isinstance(rows[0], tuple) else rows[0]
    t = first.shape[-2]
    steps = t // tile
    arrays, in_specs = [], []
    for r in rows:
        if isinstance(r, tuple) and isinstance(r[1], pl.BlockSpec):
            arrays.append(r[0])
            in_specs.append(r[1])
        elif isinstance(r, tuple):
            arr, bidx = r
            arrays.append(arr)
            in_specs.append(pl.BlockSpec((None, tile, arr.shape[-1]), functools.partial(lambda i, b: (b, i, 0), b=bidx)))
        else:
            arrays.append(r)
            in_specs.append(pl.BlockSpec((tile, r.shape[-1]), lambda i: (i, 0)))
    for c in consts:
        arrays.append(c)
        in_specs.append(pl.BlockSpec(c.shape, lambda i: (0, 0)))
    n_in, n_row = len(arrays), len(row_outs)
    out_shape, out_specs = [], []
    for ro in row_outs:
        if ro[0] == "stack":
            _, cnt, blk, total, w, dt = ro
            out_shape.append(jax.ShapeDtypeStruct((total, t, w), dt))
            out_specs.append(pl.BlockSpec((cnt, tile, w), functools.partial(lambda i, b: (b, i, 0), b=blk)))
        else:
            w, dt = ro
            out_shape.append(jax.ShapeDtypeStruct((t, w), dt))
            out_specs.append(pl.BlockSpec((tile, w), lambda i: (i, 0)))
    out_shape += [jax.ShapeDtypeStruct(s, F32) for s in acc_outs]
    out_specs += [pl.BlockSpec(s, lambda i: (0, 0)) for s in acc_outs]
    blocks = [math.prod(d for d in sp.block_shape if d) * arr.dtype.itemsize for sp, arr in zip(in_specs, arrays)]
    blocks += [math.prod(d for d in sp.block_shape if d) * jnp.dtype(sh.dtype).itemsize
               for sp, sh in zip(out_specs, out_shape)]
    need = 2 * sum(blocks) + 6 * tile * max(a.shape[-1] for a in arrays) * 4

    def body(*refs):
        ins, outs = refs[:n_in], refs[n_in:]
        i = pl.program_id(0)
        res = fn(i, steps, *[r[...] for r in ins])
        res = res if isinstance(res, (tuple, list)) else (res,)
        for ref, val in zip(outs[:n_row], res[:n_row]):
            ref[...] = val.astype(ref.dtype)
        if acc_outs:
            @pl.when(i == 0)
            def _():
                for ref in outs[n_row:]:
                    ref[...] = jnp.zeros_like(ref)

            for ref, val in zip(outs[n_row:], res[n_row:]):
                ref[...] += val

    return pl.pallas_call(
        body,
        grid=(steps,),
        in_specs=in_specs,
        out_specs=out_specs,
        out_shape=out_shape,
        compiler_params=_params(("arbitrary",), need),
        name=name,
    )(*arrays)


def _ln_stats(z):
    mu = _mean(z)
    zc = z - mu
    rstd = lax.rsqrt(_mean(zc * zc) + LN_EPS)
    return zc * rstd, rstd


def _ln_bwd(dy, xhat, rstd, g):
    dxh = dy * g
    return rstd * (dxh - _mean(dxh) - xhat * _mean(dxh * xhat))


def _ride(ex, xrefs, step, steps):
    if not ex:
        return
    for at, act in ((0, ex.start), (steps - 2, ex.relay), (steps - 1, ex.finish)):
        @pl.when(step == at)
        def _(act=act):
            act(*xrefs)


def _sgu_fwd(h, wm, bs_t, g_v, b_v, ex=None):
    t = h.shape[1]
    steps = t // SGU_BLOCK

    def body(*refs):
        (u_ref, v_ref, wm_ref, bs_ref, g_ref, b_ref), (y_ref,), _, xrefs = _split_refs(refs, 6, 1, 0, ex)
        xhat, _ = _ln_stats(_gelu(v_ref[...]))
        vn = (xhat * g_ref[...] + b_ref[...]).astype(BF16)
        gu = _gelu(u_ref[...])
        for g in range(GROUPS):
            sl = slice(g * HEAD, (g + 1) * HEAD)
            mixed = _dot(wm_ref[g], vn[:, sl]) + bs_ref[:, g:g + 1]
            y_ref[:, sl] = (gu[:, sl] * mixed).astype(BF16)
        _ride(ex, xrefs, pl.program_id(0), steps)

    blk = lambda b: pl.BlockSpec((None, SGU_BLOCK, D_MODEL), functools.partial(lambda i, b: (b, i, 0), b=b))
    whole = lambda s: pl.BlockSpec(s, lambda i: (0,) * len(s))
    res = pl.pallas_call(
        body,
        grid=(steps,),
        in_specs=[blk(SLOT_U), blk(SLOT_V), whole(wm.shape), whole(bs_t.shape), whole(g_v.shape), whole(b_v.shape)]
        + (ex.in_specs if ex else []),
        out_specs=[pl.BlockSpec((SGU_BLOCK, D_MODEL), lambda i: (i, 0))] + (ex.out_specs if ex else []),
        out_shape=[jax.ShapeDtypeStruct((t, D_MODEL), BF16)] + (ex.out_shape if ex else []),
        scratch_shapes=ex.scratch if ex else [],
        compiler_params=_params(("arbitrary",)),
        name="sgu_fwd",
    )(h, h, wm, bs_t, g_v, b_v, *(ex.arrays if ex else []))
    return res[0], res[1:]


def _sgu_bwd(h, dy, dh, wm, wm_t, bs_t, g_v, b_v):
    t = h.shape[1]

    def body(u_ref, v_ref, dy_ref, dh_in, wm_ref, wmt_ref, bs_ref, g_ref, b_ref,
             duv_ref, dw_ref, dbs_ref, dg_ref, db_ref, dvn_ref):
        del dh_in
        du_ref, dv_ref = duv_ref.at[0], duv_ref.at[1]
        i = pl.program_id(0)

        @pl.when(i == 0)
        def _():
            dw_ref[...] = jnp.zeros_like(dw_ref)
            dbs_ref[...] = jnp.zeros_like(dbs_ref)
            dg_ref[...] = jnp.zeros_like(dg_ref)
            db_ref[...] = jnp.zeros_like(db_ref)

        u = u_ref[...]
        v = v_ref[...]
        gv, gvp = _gelu_and_grad(v)
        xhat, rstd = _ln_stats(gv)
        vn = (xhat * g_ref[...] + b_ref[...]).astype(BF16)
        gu, gup = _gelu_and_grad(u)
        lane = lax.broadcasted_iota(jnp.int32, (SGU_BLOCK, LANE), 1)
        dbs = jnp.zeros((SGU_BLOCK, LANE), F32)
        for g in range(GROUPS):
            sl = slice(g * HEAD, (g + 1) * HEAD)
            vn_g = vn[:, sl]
            mixed = _dot(wm_ref[g], vn_g) + bs_ref[:, g:g + 1]
            dy_g = dy_ref[:, sl]
            du_ref[:, sl] = (dy_g * mixed * gup[:, sl]).astype(BF16)
            dmix = dy_g * gu[:, sl]
            dmb = dmix.astype(BF16)
            dvn_ref[:, sl] = _dot(wmt_ref[g], dmb)
            dw_ref[g] += _dot_nt(dmb, vn_g)
            dbs = dbs + jnp.where(lane == g, jnp.sum(dmix, axis=1, keepdims=True), 0.0)
        dbs_ref[...] += dbs
        dvn = dvn_ref[...]
        dg_ref[...] += _sum0(dvn * xhat)
        db_ref[...] += _sum0(dvn)
        dv_ref[...] = (_ln_bwd(dvn, xhat, rstd, g_ref[...]) * gvp).astype(BF16)

    blk = lambda b: pl.BlockSpec((None, SGU_BLOCK, D_MODEL), functools.partial(lambda i, b: (b, i, 0), b=b))
    row = pl.BlockSpec((SGU_BLOCK, D_MODEL), lambda i: (i, 0))
    whole = lambda s: pl.BlockSpec(s, lambda i: (0,) * len(s))
    vec = (1, D_MODEL)
    return pl.pallas_call(
        body,
        grid=(t // SGU_BLOCK,),
        in_specs=[blk(SLOT_U), blk(SLOT_V), row, pl.BlockSpec(memory_space=pl.ANY),
                  whole(wm.shape), whole(wm_t.shape), whole(bs_t.shape), whole(vec), whole(vec)],
        out_specs=[pl.BlockSpec((2, SGU_BLOCK, D_MODEL), lambda i: (SLOT_U // 2, i, 0)),
                   whole(wm.shape), whole((SGU_BLOCK, LANE)), whole(vec), whole(vec)],
        out_shape=[jax.ShapeDtypeStruct(dh.shape, BF16),
                   jax.ShapeDtypeStruct(wm.shape, F32), jax.ShapeDtypeStruct((SGU_BLOCK, LANE), F32),
                   jax.ShapeDtypeStruct(vec, F32), jax.ShapeDtypeStruct(vec, F32)],
        scratch_shapes=[pltpu.VMEM((SGU_BLOCK, D_MODEL), F32)],
        input_output_aliases={3: 0},
        compiler_params=_params(("arbitrary",)),
        name="sgu_bwd",
    )(h, h, dy, dh, wm, wm_t, bs_t, g_v, b_v)


def _split3(x):
    hi = x.astype(BF16)
    r1 = x - hi.astype(F32)
    mid = r1.astype(BF16)
    lo = (r1 - mid.astype(F32)).astype(BF16)
    return hi, mid, lo


def _tri_matmul(tri, x):
    hi, mid, lo = _split3(x)
    dot = lambda p: jnp.dot(tri, p, preferred_element_type=F32)
    return dot(hi) + dot(mid) + dot(lo)


def _lower_bound(logits):
    l0, l1 = logits[0:1, :], logits[1:2, :]
    mx = jnp.maximum(l0, l1)
    e0, e1 = jnp.exp(l0 - mx), jnp.exp(l1 - mx)
    return e0 / (e0 + e1)


def _hgrn_gates(q_raw, f_raw, lb):
    q = q_raw * jax.nn.sigmoid(q_raw)
    sig = jax.nn.sigmoid(f_raw)
    f = lb + (1.0 - lb) * sig
    row = lax.broadcasted_iota(jnp.int32, (CHUNK, CHUNK), 0)
    col = lax.broadcasted_iota(jnp.int32, (CHUNK, CHUNK), 1)
    c = _tri_matmul((row >= col).astype(BF16), jnp.log(f))
    return q, sig, f, 1.0 - f, c


def _offdiag_terms(qh, kh, ch, tb):
    rows = slice(tb * SUB, (tb + 1) * SUB)
    r = ch[tb * SUB - 1:tb * SUB, :]
    eqh = jnp.exp(ch[rows] - r)
    ekh = jnp.exp(jnp.minimum(r - ch, 0.0))
    return rows, eqh, qh[rows] * eqh, ekh, kh * ekh


def _diag_decay(cb, s, trow):
    return jnp.exp(jnp.where(trow >= s, cb - cb[s:s + 1, :], NEG))


def _split_refs(refs, n_in, n_out, n_scratch, ex):
    nx = ex.n if ex else 0
    ins, refs = refs[:n_in], refs[n_in:]
    xsrc, refs = refs[:nx], refs[nx:]
    outs, refs = refs[:n_out], refs[n_out:]
    xout, refs = refs[:nx], refs[nx:]
    return ins, outs, refs[:n_scratch], (xsrc, xout, refs[n_scratch:])


def _hgrn_fwd(h, logits, g_norm, ex=None):
    t = h.shape[1]
    nc = t // CHUNK
    per_step = CHUNKS_PER_STEP if nc % CHUNKS_PER_STEP == 0 else 1
    steps = nc // per_step

    def body(*refs):
        ins, outs, scratch, xrefs = _split_refs(refs, 3, 3, 4, ex)
        hgrn_ref, lg_ref, gn_ref = ins
        y_ref, o_ref, sall_ref = outs
        st_ref, q_s, k_s, c_s = scratch

        @pl.when(pl.program_id(0) == 0)
        def _():
            st_ref[...] = jnp.zeros_like(st_ref)

        lb = _lower_bound(lg_ref[...])
        for sub in range(per_step):
            rows = pl.ds(sub * CHUNK, CHUNK)
            chunk(lb, sall_ref.at[sub], *(r.at[rows] for r in (hgrn_ref.at[SLOT_Q], hgrn_ref.at[SLOT_F], hgrn_ref.at[SLOT_I],
                                                               hgrn_ref.at[SLOT_OG], y_ref, o_ref)),
                  gn_ref, st_ref, q_s.at[sub], k_s.at[sub], c_s.at[sub])
        _ride(ex, xrefs, pl.program_id(0), steps)

    def chunk(lb, sall_ref, q_ref, f_ref, i_ref, og_ref, y_ref, o_ref, gn_ref, st_ref, q_s, k_s, c_s):
        q, _, _, k, c = _hgrn_gates(q_ref[...], f_ref[...], lb)
        q_s[...] = q
        k_s[...] = k
        c_s[...] = c
        col64 = lax.broadcasted_iota(jnp.int32, (SUB, CHUNK), 1)
        trow = lax.broadcasted_iota(jnp.int32, (SUB, HEAD), 0)

        def head(hd, carry):
            sl = pl.ds(pl.multiple_of(hd * HEAD, HEAD), HEAD)
            qh, kh, ch, ih = q_s[:, sl], k_s[:, sl], c_s[:, sl], i_ref[:, sl]
            st = st_ref[hd]
            sall_ref[hd] = st
            c_last = ch[CHUNK - 1:CHUNK, :]
            o = _dot_nt(qh * jnp.exp(ch), st)
            st_ref[hd] = st * jnp.exp(c_last) + _dot_tn(ih, kh * jnp.exp(c_last - ch))
            a_rows = [jnp.zeros((SUB, CHUNK), F32)]
            for tb in range(1, CHUNK // SUB):
                _, _, q_hat, _, k_hat = _offdiag_terms(qh, kh, ch, tb)
                a_rows.append(jnp.where(col64 < tb * SUB, _dot_nt(q_hat, k_hat), 0.0))
            o = o + _dot(jnp.concatenate(a_rows, axis=0), ih)
            o_rows = []
            for b in range(CHUNK // SUB):
                rows = slice(b * SUB, (b + 1) * SUB)
                qb, cb, kb, ib = qh[rows], ch[rows], kh[rows], ih[rows]
                ob = jnp.zeros((SUB, HEAD), F32)
                for s in range(SUB):
                    a = jnp.sum(qb * _diag_decay(cb, s, trow) * kb[s:s + 1, :], axis=1, keepdims=True)
                    ob = ob + a * ib[s:s + 1, :]
                o_rows.append(ob)
            o = o + jnp.concatenate(o_rows, axis=0)
            o_ref[:, sl] = o
            og = og_ref[:, sl]
            on = o * lax.rsqrt(_mean(o * o) + RMS_EPS)
            y_ref[:, sl] = (on * gn_ref[:, sl] * (og * jax.nn.sigmoid(og))).astype(BF16)
            return carry

        lax.fori_loop(0, HEADS, head, 0, unroll=HEAD_UNROLL)

    rows = per_step * CHUNK
    row = pl.BlockSpec((rows, D_MODEL), lambda n: (n, 0))
    whole = lambda s: pl.BlockSpec(s, lambda n: (0,) * len(s))
    res = pl.pallas_call(
        body,
        grid=(steps,),
        in_specs=[pl.BlockSpec((4, rows, D_MODEL), lambda n: (SLOT_Q // 4, n, 0)), whole(logits.shape), whole(g_norm.shape)]
        + (ex.in_specs if ex else []),
        out_specs=[row, row, pl.BlockSpec((per_step, HEADS, HEAD, HEAD), lambda n: (n, 0, 0, 0))]
        + (ex.out_specs if ex else []),
        out_shape=[jax.ShapeDtypeStruct((t, D_MODEL), BF16), jax.ShapeDtypeStruct((t, D_MODEL), F32),
                   jax.ShapeDtypeStruct((nc, HEADS, HEAD, HEAD), F32)] + (ex.out_shape if ex else []),
        scratch_shapes=[pltpu.VMEM((HEADS, HEAD, HEAD), F32)] + [pltpu.VMEM((per_step, CHUNK, D_MODEL), F32)] * 3
        + (ex.scratch if ex else []),
        compiler_params=_params(("arbitrary",)),
        name="hgrn_fwd",
    )(h, logits, g_norm, *(ex.arrays if ex else []))
    return res[0], res[1], res[2], res[3:]


def _hgrn_bwd(h, o_all, dy, states, dh, logits, g_norm, ex=None):
    t = h.shape[1]
    nc = t // CHUNK

    def body(*refs):
        ins, outs, scratch, xrefs = _split_refs(refs, 7, 3, 8, ex)
        hgrn_ref, o_ref, dy_ref, sall_ref, _, lg_ref, gn_ref = ins
        q_ref, f_ref, i_ref, og_ref = (hgrn_ref.at[s] for s in (SLOT_Q, SLOT_F, SLOT_I, SLOT_OG))
        dqfio_ref, dlg_ref, dgn_ref = outs
        dst_ref, dlb_ref, q_s, k_s, c_s, dq_s, dk_s, dc_s = scratch
        dq_ref, df_ref, di_ref, dog_ref = (dqfio_ref.at[s] for s in (SLOT_Q, SLOT_F, SLOT_I, SLOT_OG))
        n = pl.program_id(0)

        @pl.when(n == 0)
        def _():
            dst_ref[...] = jnp.zeros_like(dst_ref)
            dlb_ref[...] = jnp.zeros_like(dlb_ref)
            dgn_ref[...] = jnp.zeros_like(dgn_ref)
            if ex:
                ex.start(*xrefs)

        lb = _lower_bound(lg_ref[...])
        q_raw = q_ref[...]
        q, sig, f, k, c = _hgrn_gates(q_raw, f_ref[...], lb)
        q_s[...] = q
        k_s[...] = k
        c_s[...] = c
        col64 = lax.broadcasted_iota(jnp.int32, (SUB, CHUNK), 1)
        trow = lax.broadcasted_iota(jnp.int32, (SUB, HEAD), 0)
        row64 = lax.broadcasted_iota(jnp.int32, (CHUNK, HEAD), 0)

        def head(hd, carry):
            sl = pl.ds(pl.multiple_of(hd * HEAD, HEAD), HEAD)
            qh, kh, ch, ih = q_s[:, sl], k_s[:, sl], c_s[:, sl], i_ref[:, sl]
            st = sall_ref[hd]
            dst = dst_ref[hd]
            oh, dyh, og, gn = o_ref[:, sl], dy_ref[:, sl], og_ref[:, sl], gn_ref[:, sl]
            sg = jax.nn.sigmoid(og)
            sil = og * sg
            rms = lax.rsqrt(_mean(oh * oh) + RMS_EPS)
            on = oh * rms
            dog_ref[:, sl] = (dyh * on * gn * _silu_grad(og, sg)).astype(BF16)
            dgn_ref[:, sl] += _sum0(dyh * on * sil)
            don = dyh * gn * sil
            do = rms * (don - on * _mean(don * on))
            dob = do.astype(BF16)

            c_last = ch[CHUNK - 1:CHUNK, :]
            eq = jnp.exp(ch)
            q_til = qh * eq
            ekl = jnp.exp(c_last - ch)
            k_til = kh * ekl
            ecl = jnp.exp(c_last)
            dq_til = _dot(dob, st)
            dk_til = _dot(ih, dst)
            di = _dot_nt(k_til, dst)
            dc_last = _sum0(dk_til * k_til) + _sum0(dst * st) * ecl
            dst_ref[hd] = _dot_tn(dob, q_til) + dst * ecl
            dq = dq_til * eq
            dc = dq_til * q_til - dk_til * k_til
            dk = dk_til * ekl

            da_full = _dot_nt(dob, ih)
            a_rows = [jnp.zeros((SUB, CHUNK), F32)]
            dq_rows = [jnp.zeros((SUB, HEAD), F32)]
            dc_rows = [jnp.zeros((SUB, HEAD), F32)]
            for tb in range(1, CHUNK // SUB):
                rows, eqh, q_hat, ekh, k_hat = _offdiag_terms(qh, kh, ch, tb)
                keep = col64 < tb * SUB
                a_rows.append(jnp.where(keep, _dot_nt(q_hat, k_hat), 0.0))
                da = jnp.where(keep, da_full[rows], 0.0)
                dq_hat = _dot(da, k_hat)
                dk_hat = _dot_tn(da, q_hat)
                dq_rows.append(dq_hat * eqh)
                dc_rows.append(dq_hat * q_hat)
                dk = dk + dk_hat * ekh
                dc = dc - dk_hat * k_hat
            di = di + _dot_tn(jnp.concatenate(a_rows, axis=0), dob)

            dk_rows, di_rows = [], []
            for b in range(CHUNK // SUB):
                rows = slice(b * SUB, (b + 1) * SUB)
                qb, cb, kb, ib, dob_ = qh[rows], ch[rows], kh[rows], ih[rows], do[rows]
                dq_diag = jnp.zeros((SUB, HEAD), F32)
                dk_diag = jnp.zeros((SUB, HEAD), F32)
                di_diag = jnp.zeros((SUB, HEAD), F32)
                for s in range(SUB):
                    ks = kb[s:s + 1, :]
                    dec = _diag_decay(cb, s, trow)
                    a = jnp.sum(qb * dec * ks, axis=1, keepdims=True)
                    gk = jnp.sum(dob_ * ib[s:s + 1, :], axis=1, keepdims=True) * dec
                    dq_diag = dq_diag + gk * ks
                    dk_diag = dk_diag + jnp.where(trow == s, _sum0(gk * qb), 0.0)
                    di_diag = di_diag + jnp.where(trow == s, _sum0(a * dob_), 0.0)
                dq_rows[b] = dq_rows[b] + dq_diag
                dc_rows[b] = dc_rows[b] + qb * dq_diag - kb * dk_diag
                dk_rows.append(dk_diag)
                di_rows.append(di_diag)
            dq = dq + jnp.concatenate(dq_rows, axis=0)
            dk = dk + jnp.concatenate(dk_rows, axis=0)
            dc = dc + jnp.concatenate(dc_rows, axis=0) + jnp.where(row64 == CHUNK - 1, dc_last, 0.0)
            di_ref[:, sl] = (di + jnp.concatenate(di_rows, axis=0)).astype(BF16)
            dq_s[:, sl] = dq
            dk_s[:, sl] = dk
            dc_s[:, sl] = dc
            return carry

        lax.fori_loop(0, HEADS, head, 0, unroll=HEAD_UNROLL)

        row = lax.broadcasted_iota(jnp.int32, (CHUNK, CHUNK), 0)
        col = lax.broadcasted_iota(jnp.int32, (CHUNK, CHUNK), 1)
        dlf = _tri_matmul((row <= col).astype(BF16), dc_s[...])
        df = dlf / f - dk_s[...]
        dlb_ref[...] += _sum0(df * (1.0 - sig))
        df_ref[...] = (df * (1.0 - lb) * sig * (1.0 - sig)).astype(BF16)
        dq_ref[...] = (dq_s[...] * _silu_grad(q_raw, jax.nn.sigmoid(q_raw))).astype(BF16)

        @pl.when(n == nc - 1)
        def _():
            d0 = dlb_ref[...] * lb * (1.0 - lb)
            dlg_ref[0:1, :] = d0
            dlg_ref[1:2, :] = -d0
            if ex:
                ex.finish(*xrefs)

    rev = lambda n: nc - 1 - n
    slots = pl.BlockSpec((4, CHUNK, D_MODEL), lambda n: (SLOT_Q // 4, rev(n), 0))
    row = pl.BlockSpec((CHUNK, D_MODEL), lambda n: (rev(n), 0))
    whole = lambda s: pl.BlockSpec(s, lambda n: (0,) * len(s))
    vec = (1, D_MODEL)
    res = pl.pallas_call(
        body,
        grid=(nc,),
        in_specs=[slots, row, row, pl.BlockSpec((None, HEADS, HEAD, HEAD), lambda n: (rev(n), 0, 0, 0)),
                  pl.BlockSpec(memory_space=pl.ANY), whole(logits.shape), whole(vec)] + (ex.in_specs if ex else []),
        out_specs=[slots, whole((2, D_MODEL)), whole(vec)] + (ex.out_specs if ex else []),
        out_shape=[jax.ShapeDtypeStruct(dh.shape, BF16), jax.ShapeDtypeStruct((2, D_MODEL), F32),
                   jax.ShapeDtypeStruct(vec, F32)] + (ex.out_shape if ex else []),
        scratch_shapes=[pltpu.VMEM((HEADS, HEAD, HEAD), F32), pltpu.VMEM(vec, F32)]
        + [pltpu.VMEM((CHUNK, D_MODEL), F32)] * 6 + (ex.scratch if ex else []),
        input_output_aliases={4: 0},
        compiler_params=_params(("arbitrary",)),
        name="hgrn_bwd",
    )(h, o_all, dy, states, dh, logits, g_norm, *(ex.arrays if ex else []))
    return res[0], res[1], res[2], res[3:]


def _merge_fwd(i, n, ga, gb, za, zb):
    return jax.nn.sigmoid(ga) * za + jax.nn.sigmoid(gb) * zb


def _merge_bwd(i, n, ga, gb, za, zb, dm):
    sa, sb = jax.nn.sigmoid(ga), jax.nn.sigmoid(gb)
    dgates = jnp.stack([(dm * za * sa * (1.0 - sa)).astype(BF16), (dm * zb * sb * (1.0 - sb)).astype(BF16)])
    return dgates, dm * sa, dm * sb


def _ln1_fwd(i, n, x, r1, g, b):
    xhat, _ = _ln_stats(ALPHA * x + r1)
    x1 = xhat * g + b
    return x1, x1


def _ln1_bwd(i, n, x, r1, dx1, g):
    xhat, rstd = _ln_stats(ALPHA * x + r1)
    dz = _ln_bwd(dx1, xhat, rstd, g)
    return dz, dz, _sum0(dx1 * xhat), _sum0(dx1)


def _ln2_loss(i, n, x1, fo, pg, pp, tgt, g, b):
    sg = jax.nn.sigmoid(pg)
    xhat, rstd = _ln_stats(ALPHA * x1 + fo + sg * pp)
    diff = xhat * g + b - tgt
    loss = 0.5 * jnp.sum(_mean(diff * diff), axis=0, keepdims=True)
    dy = diff * (1.0 / D_MODEL)
    dz = _ln_bwd(dy, xhat, rstd, g)
    return (dz, dz, dz * pp * sg * (1.0 - sg), dz * sg,
            jnp.broadcast_to(loss, (8, LANE)), _sum0(dy * xhat), _sum0(dy))


def _shift_down(cur, halo, tile):
    row = lax.broadcasted_iota(jnp.int32, cur.shape, 0)
    m1 = jnp.where(row == 0, halo[7:8, :], pltpu.roll(cur, 1, 0))
    m2 = jnp.where(row == 0, halo[6:7, :], jnp.where(row == 1, halo[7:8, :], pltpu.roll(cur, 2, 0)))
    return m1, m2


def _shift_up(cur, halo, tile):
    row = lax.broadcasted_iota(jnp.int32, cur.shape, 0)
    p1 = jnp.where(row == tile - 1, halo[0:1, :], pltpu.roll(cur, tile - 1, 0))
    p2 = jnp.where(row == tile - 2, halo[0:1, :], jnp.where(row == tile - 1, halo[1:2, :], pltpu.roll(cur, tile - 2, 0)))
    return p1, p2


def _conv_pre(i, gate, halo, w, b, tile):
    halo = jnp.where(i == 0, 0.0, halo)
    m1, m2 = _shift_down(gate, halo, tile)
    return w[0:1, :] * m2 + w[1:2, :] * m1 + w[2:3, :] * gate + b, m1, m2


def _conv_fwd(tile, i, n, gate, halo, val, w, b):
    cg, _, _ = _conv_pre(i, gate, halo, w, b, tile)
    return _gelu(cg) * val


def _conv_bwd_a(tile, i, n, gate, halo, val, dhid, w, b):
    cg, m1, m2 = _conv_pre(i, gate, halo, w, b, tile)
    act, slope = _gelu_and_grad(cg)
    dcg = dhid * val * slope
    return dcg, dhid * act, _sum0(dcg * m2), _sum0(dcg * m1), _sum0(dcg * gate), _sum0(dcg)


def _conv_bwd_b(tile, i, n, dcg, halo, w):
    dcg = dcg.astype(F32)
    halo = jnp.where(i == n - 1, 0.0, halo.astype(F32))
    p1, p2 = _shift_up(dcg, halo, tile)
    return w[2:3, :] * dcg + w[1:2, :] * p1 + w[0:1, :] * p2


def _halo_spec(width, tile, t, nxt, rows=8):
    per = tile // rows
    last = t // rows - 1
    if nxt:
        return pl.BlockSpec((rows, width), lambda i: (jnp.minimum((i + 1) * per, last), 0))
    return pl.BlockSpec((rows, width), lambda i: (jnp.maximum(i * per - 1, 0), 0))


def _adamw(i, n, w, m, v, parts):
    g = parts[0].astype(F32)
    for j in range(1, parts.shape[0]):
        g = g + parts[j].astype(F32)
    m_new = ADAM_B1 * m + (1.0 - ADAM_B1) * g
    v_new = ADAM_B2 * v + (1.0 - ADAM_B2) * (g * g)
    m_hat = m_new / (1.0 - ADAM_B1 ** ADAM_STEP)
    v_hat = v_new / (1.0 - ADAM_B2 ** ADAM_STEP)
    delta = -ADAM_LR * (m_hat / (jnp.sqrt(v_hat) + ADAM_EPS) + ADAM_WD * w)
    return g, delta, m_new, v_new


def _adam_call(w, m, v, parts, name):
    r, c = w.shape
    tile = _pick(r, (256, 128)) if r > 256 else r
    spec = pl.BlockSpec((parts.shape[0], tile, c), lambda i: (0, i, 0))
    return _rowwise(_adamw, [w, m, v, (parts, spec)], [], [(c, F32)] * 4, [], tile=tile, name=name)


def _peer(k):
    x, y, c = lax.axis_index("x"), lax.axis_index("y"), lax.axis_index("c")
    px = x ^ ((k >> 2) & 1)
    py = y ^ ((k >> 1) & 1)
    pc = c ^ (k & 1)
    return (px, py, pc), 4 * px + 2 * py + pc


def _my_index():
    return 4 * lax.axis_index("x") + 2 * lax.axis_index("y") + lax.axis_index("c")


class _Exchange:
    KINDS = ("gather", "gather+relay", "scatter", "scatter+pairs")

    def __init__(self, entries):
        assert all(k in self.KINDS for _, k in entries), [k for _, k in entries]
        self.arrays = [a for a, _ in entries]
        self.scatter = [k.startswith("scatter") for _, k in entries]
        self.relayed = ["+relay" in k for _, k in entries]
        self.pairs = ["+pairs" in k for _, k in entries]
        self.n = len(entries)
        self.in_specs = [pl.BlockSpec(memory_space=pl.ANY)] * self.n
        self.out_specs = [pl.BlockSpec(memory_space=pl.ANY)] * self.n
        shapes = [tuple(a.shape[1:]) if sc else tuple(a.shape) for a, sc in zip(self.arrays, self.scatter)]
        counts = [N_CHIP if p else N_DEV for p in self.pairs]
        self.out_shape = [jax.ShapeDtypeStruct((n,) + s, a.dtype) for n, s, a in zip(counts, shapes, self.arrays)]
        per = N_DEV - 1
        self.scratch = [pltpu.SemaphoreType.DMA((self.n * per,)), pltpu.SemaphoreType.DMA((self.n * per,)),
                        pltpu.SemaphoreType.DMA((self.n,))]

    def _copies(self, srcs, outs, sems):
        send_sems, recv_sems, local_sems = sems
        x, y, c = lax.axis_index("x"), lax.axis_index("y"), lax.axis_index("c")
        me = _my_index()
        per = N_DEV - 1
        local, first, passed, relay_arrivals, arrivals = [], [], [], [], []
        for a in range(self.n):

            def copy(k, src, dst, dev, a=a):
                return pltpu.make_async_remote_copy(
                    src_ref=src, dst_ref=dst, send_sem=send_sems.at[a * per + k], recv_sem=recv_sems.at[a * per + k],
                    device_id=dev, device_id_type=pl.DeviceIdType.MESH)

            if self.pairs[a]:
                chip = 2 * x + y
                for k in range(N_CHIP):
                    to = chip ^ k
                    piece = srcs[a].at[_slot_of_group(2 * to + c) // 2]
                    if k == 0:
                        local.append(pltpu.make_async_copy(piece, outs[a].at[chip], local_sems.at[a]))
                    else:
                        dev = (to // 2, to % 2, c)
                        first.append(copy(k - 1, piece, outs[a].at[chip], dev))
                        arrivals.append(copy(k - 1, piece, outs[a].at[to], dev))
                continue
            mine = srcs[a].at[me] if self.scatter[a] else srcs[a]
            land = outs[a].at[me]
            local.append(pltpu.make_async_copy(mine, land, local_sems.at[a]))
            if self.relayed[a]:
                block = lambda px, py, pc, a=a: outs[a].at[4 * px + 2 * py + pc]
                chips = [(1 - x, y), (x, 1 - y), (1 - x, 1 - y)]
                first.append(copy(0, mine, land, (x, y, 1 - c)))
                arrivals.append(copy(0, mine, block(x, y, 1 - c), (x, y, 1 - c)))
                for j, (px, py) in enumerate(chips):
                    first.append(copy(1 + j, mine, land, (px, py, c)))
                    relay_arrivals.append(copy(1 + j, mine, block(px, py, c), (px, py, c)))
                    passed.append(copy(4 + j, block(px, py, c), block(px, py, c), (x, y, 1 - c)))
                    arrivals.append(copy(4 + j, mine, block(px, py, 1 - c), (x, y, 1 - c)))
                continue
            for k in range(1, N_DEV):
                dev, idx = _peer(k)
                if self.scatter[a]:
                    first.append(copy(k - 1, srcs[a].at[idx], land, dev))
                else:
                    first.append(copy(k - 1, mine, land, dev))
                arrivals.append(copy(k - 1, mine, outs[a].at[idx], dev))
        return local, first, passed, relay_arrivals, arrivals

    def start(self, srcs, outs, sems):
        local, first, _, _, _ = self._copies(srcs, outs, sems)
        for cp in local + first:
            cp.start()

    def relay(self, srcs, outs, sems):
        _, _, passed, relay_arrivals, _ = self._copies(srcs, outs, sems)
        for landed, onward in zip(relay_arrivals, passed):
            landed.wait_recv()
            onward.start()

    def finish(self, srcs, outs, sems):
        local, first, passed, _, arrivals = self._copies(srcs, outs, sems)
        for cp in arrivals:
            cp.wait_recv()
        for cp in first + passed:
            cp.wait_send()
        for cp in local:
            cp.wait()


def _gather_project(x_b, shard):
    t, d = x_b.shape
    tm = _pick(t, MM_TILES)
    nrow = t // tm
    per = N_DEV - 1

    def parties():
        x, y, c = lax.axis_index("x"), lax.axis_index("y"), lax.axis_index("c")
        first = (jnp.where(c == 0, 1 - x, x), jnp.where(c == 0, y, 1 - y))
        second = (jnp.where(c == 0, x, 1 - x), jnp.where(c == 0, 1 - y, y))
        diagonal = (1 - x, 1 - y)
        return ((x, y, c), (x, y, 1 - c), [(px, py, c) for px, py in (first, second, diagonal)],
                [(px, py, 1 - c) for px, py in (second, first, diagonal)])

    slot = lambda dev: _slot_of_group(4 * dev[0] + 2 * dev[1] + dev[2])
    ici_step, passed_step = (2, 3, 6), (4, 5, 7)
    me, sibling, over_ici, passed_on = parties()
    by_step = {0: me, 1: sibling, **dict(zip(ici_step, over_ici)), **dict(zip(passed_step, passed_on))}
    order = jnp.stack([slot(by_step[j]) for j in range(N_DEV)]).astype(jnp.int32)

    def body(order_ref, x_ref, shard_ref, h_ref, wall_ref, wbuf, fetch_sem, send_sems, recv_sems, local_sem):
        del order_ref
        me, sibling, over_ici, passed_on = parties()
        j, i = pl.program_id(0), pl.program_id(1)
        land = lambda dev: wall_ref.at[slot(dev)]

        def copy(k, src, block, to):
            return pltpu.make_async_remote_copy(src_ref=src, dst_ref=land(block), send_sem=send_sems.at[k],
                                                recv_sem=recv_sems.at[k], device_id=to, device_id_type=pl.DeviceIdType.MESH)

        def fetch(src):
            cp = pltpu.make_async_copy(src, wbuf, fetch_sem)
            cp.start()
            cp.wait()

        keep = pltpu.make_async_copy(shard_ref, land(me), local_sem)
        first = [copy(0, shard_ref, me, sibling)] + [copy(1 + n, shard_ref, me, dev) for n, dev in enumerate(over_ici[:2])]
        relay = copy(3, land(over_ici[0]), over_ici[0], over_ici[1])
        onward = [copy(4 + n, land(dev), dev, sibling) for n, dev in enumerate(over_ici)]

        @pl.when(jnp.logical_and(i == 0, j == 0))
        def _():
            for cp in (first[1], first[0]):
                cp.start()
            keep.start()
            fetch(shard_ref)

        @pl.when(jnp.logical_and(i == 0, j == 1))
        def _():
            first[2].start()
            copy(0, shard_ref, sibling, me).wait_recv()
            fetch(land(sibling))

        for n, dev in enumerate(over_ici):
            @pl.when(jnp.logical_and(i == 0, j == ici_step[n]))
            def _(n=n, dev=dev):
                copy(1 + n, shard_ref, dev, me).wait_recv()
                if n == 0:
                    relay.start()
                onward[n].start()
                fetch(land(dev))

        for n, dev in enumerate(passed_on):
            @pl.when(jnp.logical_and(i == 0, j == passed_step[n]))
            def _(n=n, dev=dev):
                copy(4 + n, shard_ref, dev, me).wait_recv()
                fetch(land(dev))

        h_ref[...] = _dot(x_ref[...], wbuf[...])

        @pl.when(jnp.logical_and(i == nrow - 1, j == N_DEV - 1))
        def _():
            for cp in first + [relay] + onward:
                cp.wait_send()
            keep.wait()

    need = 2 * (tm * d * 2 + tm * d * 4) + d * d * 2 + tm * d * 4
    h, w_all = pl.pallas_call(
        body,
        grid_spec=pltpu.PrefetchScalarGridSpec(
            num_scalar_prefetch=1,
            grid=(N_DEV, nrow),
            in_specs=[pl.BlockSpec((tm, d), lambda j, i, order: (i, 0)), pl.BlockSpec(memory_space=pl.ANY)],
            out_specs=[pl.BlockSpec((None, tm, d), lambda j, i, order: (order[j], i, 0)), pl.BlockSpec(memory_space=pl.ANY)],
            scratch_shapes=[pltpu.VMEM((d, d), BF16), pltpu.SemaphoreType.DMA, pltpu.SemaphoreType.DMA((per,)),
                            pltpu.SemaphoreType.DMA((per,)), pltpu.SemaphoreType.DMA],
        ),
        out_shape=[jax.ShapeDtypeStruct((N_DEV, t, d), F32), jax.ShapeDtypeStruct((N_DEV, d, d), BF16)],
        compiler_params=_params(("arbitrary", "arbitrary"), need),
        name="gather_project",
    )(order, x_b, shard)
    return h, w_all


def _pair_sums(g):
    n, r, cols = g.shape
    half = n // 2

    def swap(g_ref, got_ref, send_sems, recv_sems):
        x, y, c = lax.axis_index("x"), lax.axis_index("y"), lax.axis_index("c")
        copies = [pltpu.make_async_remote_copy(
            src_ref=g_ref.at[2 * j + 1 - c], dst_ref=got_ref.at[j], send_sem=send_sems.at[j], recv_sem=recv_sems.at[j],
            device_id=(x, y, 1 - c), device_id_type=pl.DeviceIdType.MESH) for j in range(half)]
        for cp in copies:
            cp.start()
        for cp in copies:
            cp.wait()

    got = pl.pallas_call(
        swap,
        in_specs=[pl.BlockSpec(memory_space=pl.ANY)],
        out_specs=pl.BlockSpec(memory_space=pl.ANY),
        out_shape=jax.ShapeDtypeStruct((half, r, cols), g.dtype),
        scratch_shapes=[pltpu.SemaphoreType.DMA((half,))] * 2,
        name="pair_swap",
    )(g)

    def add(mine_ref, got_ref, out_ref):
        mine = jnp.where(lax.axis_index("c") == 0, mine_ref[0].astype(F32), mine_ref[1].astype(F32))
        out_ref[...] = (mine + got_ref[...].astype(F32)).astype(out_ref.dtype)

    tile = _pick(r, (512, 256, 128))
    return pl.pallas_call(
        add,
        grid=(half, r // tile),
        in_specs=[pl.BlockSpec((None, 2, tile, cols), lambda j, i: (j, 0, i, 0)),
                  pl.BlockSpec((None, tile, cols), lambda j, i: (j, i, 0))],
        out_specs=pl.BlockSpec((None, tile, cols), lambda j, i: (j, i, 0)),
        out_shape=jax.ShapeDtypeStruct((half, r, cols), g.dtype),
        compiler_params=_params(("parallel", "parallel")),
        name="pair_add",
    )(g.reshape(half, 2, r, cols), got)


def _local_step(x, p, tgt, small, comm):
    t = x.shape[0]
    tile = _pick(t, (256, 128))
    tall = _pick(t, (512, 256, 128))
    d = D_MODEL
    act_b, act_f = (d, BF16), (d, F32)
    x_b, p_b = x.astype(BF16), p.astype(BF16)

    chunk_id = jnp.arange(SGU_BLOCK) // CHUNK
    mask = chunk_id[:, None] >= chunk_id[None, :]
    wm = jnp.where(mask[None], small["sgu_w_s"], 0.0)
    wm_b = wm.astype(BF16)
    wm_t = jnp.swapaxes(wm, 1, 2).astype(BF16)
    bs_t = small["sgu_b_s"].T

    h, w_in = comm.project_in(x_b)
    y_a, got_a = _sgu_fwd(h, wm_b, bs_t, small["sgu_norm_g"], small["sgu_norm_b"], ex=comm.weights_exchange(0))
    y_b, o_all, states, got_b = _hgrn_fwd(h, small["lb_logits"], small["hgrn_norm_g"], ex=comm.weights_exchange(1))
    wts, conv_w = comm.weights(list(got_a) + list(got_b))
    z_a = _mm(y_a, wts["w_a"], out_dtype=F32, name="mm_za")
    z_b = _mm(y_b, wts["w_b"], out_dtype=F32, name="mm_zb")
    gates = [(h, SLOT_GA), (h, SLOT_GB)]
    merged, = _rowwise(_merge_fwd, gates + [z_a, z_b], [], [act_b], [], tile=tall, name="merge_fwd")
    r1 = _mm(merged, wts["w_o"], out_dtype=F32, name="mm_r1")
    x1, x1_b = _rowwise(_ln1_fwd, [x, r1], [small["ln1_g"], small["ln1_b"]], [act_f, act_b], [], tile=tall, name="ln1_fwd")
    gate = _mm(x1_b, wts["w_g"], out_dtype=F32, name="mm_gate")
    val = _mm(x1_b, wts["w_v"], out_dtype=F32, name="mm_val")
    pg = _mm(x1_b, wts["w_pg"], out_dtype=F32, name="mm_pg")
    pp = _mm(p_b, wts["w_pp"], out_dtype=F32, name="mm_pp")
    hid, = _rowwise(functools.partial(_conv_fwd, tile), [gate, (gate, _halo_spec(D_FF, tile, t, False)), val],
                    [conv_w, small["conv_b"]], [(D_FF, BF16)], [], tile=tile, name="conv_fwd")
    fo = _mm(hid, wts["w_down"], out_dtype=F32, name="mm_down")
    dz2, dz2_b, dpg, dpp, loss, dg2, db2 = _rowwise(
        _ln2_loss, [x1, fo, pg, pp, tgt], [small["ln2_g"], small["ln2_b"]],
        [act_f, act_b, act_b, act_b], [(8, LANE), (1, d), (1, d)], tile=tall, name="ln2_loss")

    dhid = _mm(dz2_b, wts["w_down"], out_dtype=BF16, name="mm_dhid", trans_b=True)
    g_down = _mm_tn(hid, dz2_b, out_dtype=BF16, name="mm_g_down")
    dcg, dval, dcw0, dcw1, dcw2, dcb = _rowwise(
        functools.partial(_conv_bwd_a, tile), [gate, (gate, _halo_spec(D_FF, tile, t, False)), val, dhid],
        [conv_w, small["conv_b"]], [(D_FF, BF16), (D_FF, BF16)], [(1, D_FF)] * 4, tile=tile, name="conv_bwd_a")
    dgate, = _rowwise(functools.partial(_conv_bwd_b, tile), [dcg, (dcg, _halo_spec(D_FF, tile, t, True, rows=16))],
                      [conv_w], [(D_FF, BF16)], [], tile=tile, name="conv_bwd_b")
    g_g = _mm_tn(x1_b, dgate, out_dtype=BF16, name="mm_g_gate")
    g_v = _mm_tn(x1_b, dval, out_dtype=BF16, name="mm_g_val")
    g_pg = _mm_tn(x1_b, dpg, out_dtype=BF16, name="mm_g_pg")
    g_pp = _mm_tn(p_b, dpp, out_dtype=BF16, name="mm_g_pp")
    dx1 = _mm(dgate, wts["w_g"], out_dtype=F32, name="mm_dx1_gate", trans_b=True, adds=[(dz2, ALPHA)])
    dx1 = _mm(dval, wts["w_v"], out_dtype=F32, name="mm_dx1_val", trans_b=True, adds=[(dx1, 1.0)])
    dx1 = _mm(dpg, wts["w_pg"], out_dtype=F32, name="mm_dx1_pg", trans_b=True, adds=[(dx1, 1.0)])
    dz1, dz1_b, dg1, db1 = _rowwise(_ln1_bwd, [x, r1, dx1], [small["ln1_g"]], [act_f, act_b], [(1, d), (1, d)],
                                    tile=tall, name="ln1_bwd")
    g_o = _mm_tn(merged, dz1_b, out_dtype=BF16, name="mm_g_o")
    dm = _mm(dz1_b, wts["w_o"], out_dtype=BF16, name="mm_dm", trans_b=True)
    dh, dza, dzb = _rowwise(_merge_bwd, gates + [z_a, z_b, dm], [],
                            [("stack", 2, SLOT_GA // 2, 8, d, BF16), act_b, act_b], [], tile=tall, name="merge_bwd")
    g_a = _mm_tn(y_a, dza, out_dtype=BF16, name="mm_g_a")
    g_b = _mm_tn(y_b, dzb, out_dtype=BF16, name="mm_g_b")
    dy_a = _mm(dza, wts["w_a"], out_dtype=BF16, name="mm_dya", trans_b=True)
    dy_b = _mm(dzb, wts["w_b"], out_dtype=F32, name="mm_dyb", trans_b=True)
    dh, dws, dbs, dgv_n, dbv_n = _sgu_bwd(h, dy_a, dh, wm_b, wm_t, bs_t, small["sgu_norm_g"], small["sgu_norm_b"])
    big = dict(w_a=g_a, w_b=g_b, w_o=g_o, w_g=g_g, w_v=g_v, w_down=g_down, w_pp=g_pp, w_pg=g_pg)
    sm = dict(sgu_w_s=jnp.where(mask[None], dws, 0.0), sgu_b_s=dbs[:, :GROUPS].T, sgu_norm_g=dgv_n, sgu_norm_b=dbv_n,
              ln1_g=dg1, ln1_b=db1, conv_w=jnp.concatenate([dcw0, dcw1, dcw2], axis=0), conv_b=dcb, ln2_g=dg2, ln2_b=db2,
              loss=loss)
    dh, dlogits, dgn, got = _hgrn_bwd(h, o_all, dy_b, states, dh, small["lb_logits"], small["hgrn_norm_g"],
                                      ex=comm.grads_exchange(big, sm))
    comm.grads_done(got)
    g_in = _mm_tn(x_b, dh, out_dtype=BF16, name="mm_g_in")
    ex = comm.last_exchange(g_in, dict(lb_logits=dlogits, hgrn_norm_g=dgn))
    res = _mm(dh, w_in, out_dtype=F32, name="mm_dx", trans_b=True, reduce_b=True, adds=[(dz1, ALPHA)], ex=ex)
    grad_x, got = res if ex else (res, ())
    comm.last_done(got)
    return grad_x


_SMALL_EARLY = ["sgu_w_s", "sgu_b_s", "sgu_norm_g", "sgu_norm_b", "ln1_g", "ln1_b", "ffn_conv_b", "ln2_g", "ln2_b"]
_SMALL_LATE = ["hgrn_lb_logits", "hgrn_norm_g"]
N_TAPS = D_FF // N_DEV
UP_COLS = 2 * D_FF // N_DEV


class _StepExchanges:
    def __init__(self, w_in_shard, shards):
        self.w_in_shard = w_in_shard
        self.shards = shards

    def project_in(self, x_b):
        return _gather_project(x_b, self.w_in_shard)

    def weights_exchange(self, part):
        return _Exchange([(s, "gather+relay") for s in (self.shards[:2] if part == 0 else self.shards[2:])])

    def weights(self, got):
        d, f = D_MODEL, D_FF
        w_br_g, w_o_g, w_up_g, w_down_g, w_pp_g, w_pg_g, conv_g = got
        w_br = w_br_g.transpose(1, 0, 2, 3).reshape(2, d, d)
        w_up = w_up_g.transpose(1, 0, 2).reshape(d, 2, f).transpose(1, 0, 2)
        wts = dict(w_a=w_br[0], w_b=w_br[1], w_o=w_o_g.reshape(d, d), w_g=w_up[0], w_v=w_up[1],
                   w_down=w_down_g.reshape(f, d), w_pp=w_pp_g.transpose(1, 0, 2).reshape(256, d), w_pg=w_pg_g.reshape(d, d))
        return wts, conv_g.transpose(1, 0, 2).reshape(3, f)

    def grads_exchange(self, big, sm):
        d = D_MODEL
        parts = [jnp.stack([big["w_a"], big["w_b"]]).reshape(2, N_DEV, 128, d).transpose(1, 0, 2, 3),
                 big["w_o"].reshape(N_DEV, 128, d),
                 jnp.concatenate([big["w_g"], big["w_v"]], axis=1).reshape(d, N_DEV, UP_COLS).transpose(1, 0, 2),
                 big["w_down"].reshape(N_DEV, N_TAPS, d),
                 big["w_pp"].reshape(256, N_DEV, 128).transpose(1, 0, 2),
                 big["w_pg"].reshape(N_DEV, 128, d)]
        packed, self.rows_early = _pack([sm[k] for k in ("sgu_w_s", "sgu_b_s", "sgu_norm_g", "sgu_norm_b", "ln1_g", "ln1_b",
                                                         "conv_b", "ln2_g", "ln2_b", "conv_w", "loss")])
        return _Exchange([(a, "scatter") for a in parts] + [(packed, "gather")])

    def grads_done(self, got):
        self.recv, self.small_early = got[:6], got[6]

    def last_exchange(self, g_in, sm):
        packed, self.rows_late = _pack([sm["lb_logits"], sm["hgrn_norm_g"]])
        return _Exchange([(_pair_sums(g_in), "scatter+pairs"), (packed, "gather")])

    def last_done(self, got):
        self.recv_in, self.small_late = got


def _rows128(a):
    flat = a.reshape(-1)
    rows = -(-flat.shape[0] // (8 * LANE)) * 8
    return jnp.pad(flat, (0, rows * LANE - flat.shape[0])).reshape(rows, LANE)


def _pack(parts):
    blocks = [_rows128(a) for a in parts]
    return jnp.concatenate(blocks, axis=0), [b.shape[0] for b in blocks]


def _unpack(packed, shapes, rows):
    out, r0 = [], 0
    for shp, r in zip(shapes, rows):
        n = math.prod(shp)
        out.append(packed[r0:r0 + r].reshape(-1)[:n].reshape(shp))
        r0 += r
    return out


def kernel(x, p, w_in, sgu_w_s, sgu_b_s, sgu_norm_g, sgu_norm_b, hgrn_lb_logits, hgrn_norm_g, w_branch, w_out, ln1_g, ln1_b, ffn_w_up, ffn_conv_w, ffn_conv_b, ffn_w_down, ln2_g, ln2_b, ple_w_proj, ple_w_gate, loss_target, m_w_in, m_sgu_w_s, m_sgu_b_s, m_sgu_norm_g, m_sgu_norm_b, m_hgrn_lb_logits, m_hgrn_norm_g, m_w_branch, m_w_out, m_ln1_g, m_ln1_b, m_ffn_w_up, m_ffn_conv_w, m_ffn_conv_b, m_ffn_w_down, m_ln2_g, m_ln2_b, m_ple_w_proj, m_ple_w_gate, v_w_in, v_sgu_w_s, v_sgu_b_s, v_sgu_norm_g, v_sgu_norm_b, v_hgrn_lb_logits, v_hgrn_norm_g, v_w_branch, v_w_out, v_ln1_g, v_ln1_b, v_ffn_w_up, v_ffn_conv_w, v_ffn_conv_b, v_ffn_w_down, v_ln2_g, v_ln2_b, v_ple_w_proj, v_ple_w_gate):
    weights = dict(w_in=w_in, sgu_w_s=sgu_w_s, sgu_b_s=sgu_b_s, sgu_norm_g=sgu_norm_g, sgu_norm_b=sgu_norm_b,
                   hgrn_lb_logits=hgrn_lb_logits, hgrn_norm_g=hgrn_norm_g, w_branch=w_branch, w_out=w_out,
                   ln1_g=ln1_g, ln1_b=ln1_b, ffn_w_up=ffn_w_up, ffn_conv_w=ffn_conv_w, ffn_conv_b=ffn_conv_b,
                   ffn_w_down=ffn_w_down, ln2_g=ln2_g, ln2_b=ln2_b, ple_w_proj=ple_w_proj, ple_w_gate=ple_w_gate)
    mom_m = dict(w_in=m_w_in, sgu_w_s=m_sgu_w_s, sgu_b_s=m_sgu_b_s, sgu_norm_g=m_sgu_norm_g, sgu_norm_b=m_sgu_norm_b,
                 hgrn_lb_logits=m_hgrn_lb_logits, hgrn_norm_g=m_hgrn_norm_g, w_branch=m_w_branch, w_out=m_w_out,
                 ln1_g=m_ln1_g, ln1_b=m_ln1_b, ffn_w_up=m_ffn_w_up, ffn_conv_w=m_ffn_conv_w, ffn_conv_b=m_ffn_conv_b,
                 ffn_w_down=m_ffn_w_down, ln2_g=m_ln2_g, ln2_b=m_ln2_b, ple_w_proj=m_ple_w_proj, ple_w_gate=m_ple_w_gate)
    mom_v = dict(w_in=v_w_in, sgu_w_s=v_sgu_w_s, sgu_b_s=v_sgu_b_s, sgu_norm_g=v_sgu_norm_g, sgu_norm_b=v_sgu_norm_b,
                 hgrn_lb_logits=v_hgrn_lb_logits, hgrn_norm_g=v_hgrn_norm_g, w_branch=v_w_branch, w_out=v_w_out,
                 ln1_g=v_ln1_g, ln1_b=v_ln1_b, ffn_w_up=v_ffn_w_up, ffn_conv_w=v_ffn_conv_w, ffn_conv_b=v_ffn_conv_b,
                 ffn_w_down=v_ffn_w_down, ln2_g=v_ln2_g, ln2_b=v_ln2_b, ple_w_proj=v_ple_w_proj, ple_w_gate=v_ple_w_gate)
    d, f = D_MODEL, D_FF
    me = _my_index()

    comm = _StepExchanges(w_in[0].astype(BF16),
                          [w_branch[0].astype(BF16), w_out[0].astype(BF16), ffn_w_up[0].astype(BF16),
                           ffn_w_down[0].astype(BF16), ple_w_proj[0].astype(BF16), ple_w_gate[0].astype(BF16), ffn_conv_w[0]])
    small = dict(sgu_w_s=sgu_w_s[0], sgu_b_s=sgu_b_s[0], sgu_norm_g=sgu_norm_g, sgu_norm_b=sgu_norm_b,
                 lb_logits=hgrn_lb_logits, hgrn_norm_g=hgrn_norm_g, ln1_g=ln1_g, ln1_b=ln1_b, ln2_g=ln2_g, ln2_b=ln2_b,
                 conv_b=ffn_conv_b)
    grad_x = _local_step(x[0], p[0, 0], loss_target[0], small, comm)

    out = {}

    def adam(name, parts8, shape2d):
        w2, m2, v2 = (a.reshape(shape2d) for a in (weights[name], mom_m[name], mom_v[name]))
        res = _adam_call(w2, m2, v2, parts8.reshape(parts8.shape[:1] + shape2d), "adam_" + name)
        out[name] = tuple(r.reshape(weights[name].shape) for r in res)

    adam("w_in", comm.recv_in, (d, d))
    adam("w_branch", comm.recv[0], (256, d))
    adam("w_out", comm.recv[1], (128, d))
    adam("ffn_w_up", comm.recv[2], (d, UP_COLS))
    adam("ffn_w_down", comm.recv[3], (N_TAPS, d))
    adam("ple_w_proj", comm.recv[4], (256, 128))
    adam("ple_w_gate", comm.recv[5], (128, d))

    def adam_small(names, extra_w, extra_m, extra_v, extra_shapes, parts8, rows, label):
        pk = lambda src, extra: _pack([src[n] for n in names] + extra)[0]
        res = _adam_call(pk(weights, extra_w), pk(mom_m, extra_m), pk(mom_v, extra_v), parts8, label)
        shapes = [weights[n].shape for n in names] + extra_shapes
        unpacked = [_unpack(r, shapes, rows) for r in res]
        for j, n in enumerate(names):
            out[n] = tuple(u[j] for u in unpacked)
        return [[u[len(names) + j] for u in unpacked] for j in range(len(extra_shapes))]

    blank = jnp.zeros((8, LANE), F32)
    taps, loss_rows = adam_small(
        _SMALL_EARLY, [_place_taps(ffn_conv_w[0], me, f), blank], [_place_taps(m_ffn_conv_w[0], me, f), blank],
        [_place_taps(v_ffn_conv_w[0], me, f), blank + 1.0], [(3, f), (8, LANE)], comm.small_early, comm.rows_early,
        "adam_small_early")
    adam_small(_SMALL_LATE, [], [], [], [], comm.small_late, comm.rows_late, "adam_small_late")
    out["ffn_conv_w"] = tuple(lax.dynamic_slice_in_dim(u, me * N_TAPS, N_TAPS, axis=1)[None] for u in taps)
    loss = loss_rows[0][0, 0]

    order = ["w_in", "sgu_w_s", "sgu_b_s", "sgu_norm_g", "sgu_norm_b", "hgrn_lb_logits", "hgrn_norm_g", "w_branch", "w_out",
             "ln1_g", "ln1_b", "ffn_w_up", "ffn_conv_w", "ffn_conv_b", "ffn_w_down", "ln2_g", "ln2_b", "ple_w_proj", "ple_w_gate"]
    return (loss, grad_x[None], *[out[n][0] for n in order], *[out[n][1] for n in order],
            *[out[n][2] for n in order], *[out[n][3] for n in order])


def _place_taps(shard, me, f):
    return lax.dynamic_update_slice_in_dim(jnp.zeros((3, f), F32), shard, me * N_TAPS, axis=1)
```

```python
import functools
import math

import jax
import jax.numpy as jnp
from jax import lax
from jax.experimental import pallas as pl
from jax.experimental.pallas import tpu as pltpu

F32 = jnp.float32
BF16 = jnp.bfloat16

N_DEV = 8
N_CHIP = 4
D_MODEL = 1024
CHUNK = 64
SUB = 16
SGU_BLOCK = 128
GROUPS = 8
HEAD = 128
HEADS = 8
CHUNKS_PER_STEP = 2
HEAD_UNROLL = 8
D_FF = 2816
LN_EPS = 1e-5
RMS_EPS = 1e-6
ALPHA = 2.0 ** 0.25
GELU_K = math.sqrt(2.0 / math.pi)
GELU_C = 0.044715
NEG = -1e30
ADAM_LR, ADAM_B1, ADAM_B2, ADAM_EPS, ADAM_WD, ADAM_STEP = 0.001, 0.9, 0.999, 1e-08, 0.01, 10
LANE = 128
SLOT_Q, SLOT_F, SLOT_I, SLOT_OG, SLOT_U, SLOT_V, SLOT_GA, SLOT_GB = range(8)


def _slot_of_group(k):
    return jnp.where(k < 2, k + 4, jnp.where(k < 6, k - 2, k))


MIB = 1024 * 1024
VMEM_V7X = 64 * MIB
VMEM_FLOOR = 32 * MIB
MM_TILES = (1024, 1408, 512, 256, 128)


def _params(sem, need=0):
    limit = min(max(need + need // 4, VMEM_FLOOR), VMEM_V7X - 4 * MIB)
    return pltpu.CompilerParams(dimension_semantics=sem, vmem_limit_bytes=limit)


def _pick(n, prefs):
    for t in prefs:
        if n % t == 0:
            return t
    return n


def _gelu(x):
    return 0.5 * x * (1.0 + jnp.tanh(GELU_K * (x + GELU_C * x * x * x)))


def _gelu_and_grad(x):
    x2 = x * x
    t = jnp.tanh(GELU_K * x * (1.0 + GELU_C * x2))
    half = 0.5 * (1.0 + t)
    return x * half, half + 0.5 * x * (1.0 - t * t) * GELU_K * (1.0 + 3.0 * GELU_C * x2)


def _silu_grad(x, s):
    return s * (1.0 + x * (1.0 - s))


def _dot(a, b):
    return jnp.dot(a.astype(BF16), b.astype(BF16), preferred_element_type=F32)


def _dot_nt(a, b):
    return lax.dot_general(a.astype(BF16), b.astype(BF16), (((1,), (1,)), ((), ())), preferred_element_type=F32)


def _dot_tn(a, b):
    return lax.dot_general(a.astype(BF16), b.astype(BF16), (((0,), (0,)), ((), ())), preferred_element_type=F32)


def _mean(x):
    return jnp.mean(x, axis=-1, keepdims=True)


def _sum0(x):
    return jnp.sum(x, axis=0, keepdims=True)


def _mm(a, b, *, out_dtype, name, trans_b=False, reduce_b=False, adds=(), ex=None):
    squeeze = b.ndim == 2
    a3 = a if a.ndim == 3 else a[None]
    b3 = b if b.ndim == 3 else b[None]
    ba, m, k = a3.shape
    bb = b3.shape[0]
    n = b3.shape[1] if trans_b else b3.shape[2]
    tm = _pick(m, MM_TILES)
    tn = _pick(n, MM_TILES)
    if reduce_b:
        bo, steps = 1, bb
        a_map = lambda o, i, j, r: (r if ba > 1 else 0, i, 0)
        b_map = (lambda o, i, j, r: (r, j, 0)) if trans_b else (lambda o, i, j, r: (r, 0, j))
    else:
        bo, steps = bb, 1
        a_map = lambda o, i, j, r: (o if ba > 1 else 0, i, 0)
        b_map = (lambda o, i, j, r: (o, j, 0)) if trans_b else (lambda o, i, j, r: (o, 0, j))
    o_map = lambda o, i, j, r: (o, i, j)
    add_arrays = [x if x.ndim == 3 else x[None] for x, _ in adds]
    add_scales = [s for _, s in adds]
    n_add = len(adds)
    dot = _dot_nt if trans_b else _dot

    def finish(acc, add_refs, o_ref):
        for ref, s in zip(add_refs, add_scales):
            acc = acc + s * ref[...].astype(F32)
        o_ref[...] = acc.astype(o_ref.dtype)

    grid = (bo, m // tm, n // tn, steps)

    def body(*refs):
        ins, (o_ref,), scratch, xrefs = _split_refs(refs, 2 + n_add, 1, 1 if reduce_b else 0, ex)
        a_ref, b_ref, add_refs = ins[0], ins[1], ins[2:]
        step = ((pl.program_id(0) * grid[1] + pl.program_id(1)) * grid[2] + pl.program_id(2)) * grid[3] + pl.program_id(3)
        if ex:
            @pl.when(step == 0)
            def _():
                ex.start(*xrefs)

        if reduce_b:
            acc, = scratch
            r = pl.program_id(3)

            @pl.when(r == 0)
            def _():
                acc[...] = jnp.zeros_like(acc)

            acc[...] += dot(a_ref[...], b_ref[...])

            @pl.when(r == steps - 1)
            def _():
                finish(acc[...], add_refs, o_ref)
        else:
            finish(dot(a_ref[...], b_ref[...]), add_refs, o_ref)

        if ex:
            @pl.when(step == math.prod(grid) - 1)
            def _():
                ex.finish(*xrefs)

    b_block = (None, tn, k) if trans_b else (None, k, tn)
    out_bytes = tm * tn * jnp.dtype(out_dtype).itemsize
    need = 2 * (tm * k * a3.dtype.itemsize + k * tn * b3.dtype.itemsize + out_bytes + n_add * tm * tn * 4)
    need += 2 * tm * tn * 4
    sem = ("arbitrary",) * 4 if ex else ("parallel", "parallel", "parallel", "arbitrary")
    res = pl.pallas_call(
        body,
        grid=grid,
        in_specs=[pl.BlockSpec((None, tm, k), a_map), pl.BlockSpec(b_block, b_map)]
        + [pl.BlockSpec((None, tm, tn), o_map) for _ in adds] + (ex.in_specs if ex else []),
        out_specs=[pl.BlockSpec((None, tm, tn), o_map)] + (ex.out_specs if ex else []),
        out_shape=[jax.ShapeDtypeStruct((bo, m, n), out_dtype)] + (ex.out_shape if ex else []),
        scratch_shapes=([pltpu.VMEM((tm, tn), F32)] if reduce_b else []) + (ex.scratch if ex else []),
        compiler_params=_params(sem, need),
        name=name,
    )(a3, b3, *add_arrays, *(ex.arrays if ex else []))
    out = res[0][0] if (reduce_b or squeeze) else res[0]
    return (out, res[1:]) if ex else out


def _mm_tn(a, b, *, out_dtype, name):
    squeeze = b.ndim == 2
    b3 = b if b.ndim == 3 else b[None]
    t, m = a.shape
    bb, _, n = b3.shape
    tm = _pick(m, MM_TILES)
    tn = _pick(n, MM_TILES)
    tt = _pick(t, (1024, 512, 256, 128))
    steps = t // tt
    need = 2 * (tt * tm * a.dtype.itemsize + tt * tn * b3.dtype.itemsize + tm * tn * jnp.dtype(out_dtype).itemsize)
    need += 2 * tm * tn * 4

    def body(a_ref, b_ref, o_ref, acc):
        r = pl.program_id(3)

        @pl.when(r == 0)
        def _():
            acc[...] = jnp.zeros_like(acc)

        acc[...] += _dot_tn(a_ref[...], b_ref[...])

        @pl.when(r == steps - 1)
        def _():
            o_ref[...] = acc[...].astype(o_ref.dtype)

    out = pl.pallas_call(
        body,
        grid=(bb, m // tm, n // tn, steps),
        in_specs=[pl.BlockSpec((tt, tm), lambda o, i, j, r: (r, i)),
                  pl.BlockSpec((None, tt, tn), lambda o, i, j, r: (o, r, j))],
        out_specs=pl.BlockSpec((None, tm, tn), lambda o, i, j, r: (o, i, j)),
        out_shape=jax.ShapeDtypeStruct((bb, m, n), out_dtype),
        scratch_shapes=[pltpu.VMEM((tm, tn), F32)],
        compiler_params=_params(("parallel", "parallel", "parallel", "arbitrary"), need),
        name=name,
    )(a, b3)
    return out[0] if squeeze else out


def _rowwise(fn, rows, consts, row_outs, acc_outs, *, tile, name):
    first = rows[0][0] if isinstance(rows[0], tuple) else rows[0]
    t = first.shape[-2]
    steps = t // tile
    arrays, in_specs = [], []
    for r in rows:
        if isinstance(r, tuple) and isinstance(r[1], pl.BlockSpec):
            arrays.append(r[0])
            in_specs.append(r[1])
        elif isinstance(r, tuple):
            arr, bidx = r
            arrays.append(arr)
            in_specs.append(pl.BlockSpec((None, tile, arr.shape[-1]), functools.partial(lambda i, b: (b, i, 0), b=bidx)))
        else:
            arrays.append(r)
            in_specs.append(pl.BlockSpec((tile, r.shape[-1]), lambda i: (i, 0)))
    for c in consts:
        arrays.append(c)
        in_specs.append(pl.BlockSpec(c.shape, lambda i: (0, 0)))
    n_in, n_row = len(arrays), len(row_outs)
    out_shape, out_specs = [], []
    for ro in row_outs:
        if ro[0] == "stack":
            _, cnt, blk, total, w, dt = ro
            out_shape.append(jax.ShapeDtypeStruct((total, t, w), dt))
            out_specs.append(pl.BlockSpec((cnt, tile, w), functools.partial(lambda i, b: (b, i, 0), b=blk)))
        else:
            w, dt = ro
            out_shape.append(jax.ShapeDtypeStruct((t, w), dt))
            out_specs.append(pl.BlockSpec((tile, w), lambda i: (i, 0)))
    out_shape += [jax.ShapeDtypeStruct(s, F32) for s in acc_outs]
    out_specs += [pl.BlockSpec(s, lambda i: (0, 0)) for s in acc_outs]
    blocks = [math.prod(d for d in sp.block_shape if d) * arr.dtype.itemsize for sp, arr in zip(in_specs, arrays)]
    blocks += [math.prod(d for d in sp.block_shape if d) * jnp.dtype(sh.dtype).itemsize
               for sp, sh in zip(out_specs, out_shape)]
    need = 2 * sum(blocks) + 6 * tile * max(a.shape[-1] for a in arrays) * 4

    def body(*refs):
        ins, outs = refs[:n_in], refs[n_in:]
        i = pl.program_id(0)
        res = fn(i, steps, *[r[...] for r in ins])
        res = res if isinstance(res, (tuple, list)) else (res,)
        for ref, val in zip(outs[:n_row], res[:n_row]):
            ref[...] = val.astype(ref.dtype)
        if acc_outs:
            @pl.when(i == 0)
            def _():
                for ref in outs[n_row:]:
                    ref[...] = jnp.zeros_like(ref)

            for ref, val in zip(outs[n_row:], res[n_row:]):
                ref[...] += val

    return pl.pallas_call(
        body,
        grid=(steps,),
        in_specs=in_specs,
        out_specs=out_specs,
        out_shape=out_shape,
        compiler_params=_params(("arbitrary",), need),
        name=name,
    )(*arrays)


def _ln_stats(z):
    mu = _mean(z)
    zc = z - mu
    rstd = lax.rsqrt(_mean(zc * zc) + LN_EPS)
    return zc * rstd, rstd


def _ln_bwd(dy, xhat, rstd, g):
    dxh = dy * g
    return rstd * (dxh - _mean(dxh) - xhat * _mean(dxh * xhat))


def _ride(ex, xrefs, step, steps):
    if not ex:
        return
    for at, act in ((0, ex.start), (steps - 2, ex.relay), (steps - 1, ex.finish)):
        @pl.when(step == at)
        def _(act=act):
            act(*xrefs)


def _sgu_fwd(h, wm, bs_t, g_v, b_v, ex=None):
    t = h.shape[1]
    steps = t // SGU_BLOCK

    def body(*refs):
        (u_ref, v_ref, wm_ref, bs_ref, g_ref, b_ref), (y_ref,), _, xrefs = _split_refs(refs, 6, 1, 0, ex)
        xhat, _ = _ln_stats(_gelu(v_ref[...]))
        vn = (xhat * g_ref[...] + b_ref[...]).astype(BF16)
        gu = _gelu(u_ref[...])
        for g in range(GROUPS):
            sl = slice(g * HEAD, (g + 1) * HEAD)
            mixed = _dot(wm_ref[g], vn[:, sl]) + bs_ref[:, g:g + 1]
            y_ref[:, sl] = (gu[:, sl] * mixed).astype(BF16)
        _ride(ex, xrefs, pl.program_id(0), steps)

    blk = lambda b: pl.BlockSpec((None, SGU_BLOCK, D_MODEL), functools.partial(lambda i, b: (b, i, 0), b=b))
    whole = lambda s: pl.BlockSpec(s, lambda i: (0,) * len(s))
    res = pl.pallas_call(
        body,
        grid=(steps,),
        in_specs=[blk(SLOT_U), blk(SLOT_V), whole(wm.shape), whole(bs_t.shape), whole(g_v.shape), whole(b_v.shape)]
        + (ex.in_specs if ex else []),
        out_specs=[pl.BlockSpec((SGU_BLOCK, D_MODEL), lambda i: (i, 0))] + (ex.out_specs if ex else []),
        out_shape=[jax.ShapeDtypeStruct((t, D_MODEL), BF16)] + (ex.out_shape if ex else []),
        scratch_shapes=ex.scratch if ex else [],
        compiler_params=_params(("arbitrary",)),
        name="sgu_fwd",
    )(h, h, wm, bs_t, g_v, b_v, *(ex.arrays if ex else []))
    return res[0], res[1:]


def _sgu_bwd(h, dy, dh, wm, wm_t, bs_t, g_v, b_v):
    t = h.shape[1]

    def body(u_ref, v_ref, dy_ref, dh_in, wm_ref, wmt_ref, bs_ref, g_ref, b_ref,
             duv_ref, dw_ref, dbs_ref, dg_ref, db_ref, dvn_ref):
        del dh_in
        du_ref, dv_ref = duv_ref.at[0], duv_ref.at[1]
        i = pl.program_id(0)

        @pl.when(i == 0)
        def _():
            dw_ref[...] = jnp.zeros_like(dw_ref)
            dbs_ref[...] = jnp.zeros_like(dbs_ref)
            dg_ref[...] = jnp.zeros_like(dg_ref)
            db_ref[...] = jnp.zeros_like(db_ref)

        u = u_ref[...]
        v = v_ref[...]
        gv, gvp = _gelu_and_grad(v)
        xhat, rstd = _ln_stats(gv)
        vn = (xhat * g_ref[...] + b_ref[...]).astype(BF16)
        gu, gup = _gelu_and_grad(u)
        lane = lax.broadcasted_iota(jnp.int32, (SGU_BLOCK, LANE), 1)
        dbs = jnp.zeros((SGU_BLOCK, LANE), F32)
        for g in range(GROUPS):
            sl = slice(g * HEAD, (g + 1) * HEAD)
            vn_g = vn[:, sl]
            mixed = _dot(wm_ref[g], vn_g) + bs_ref[:, g:g + 1]
            dy_g = dy_ref[:, sl]
            du_ref[:, sl] = (dy_g * mixed * gup[:, sl]).astype(BF16)
            dmix = dy_g * gu[:, sl]
            dmb = dmix.astype(BF16)
            dvn_ref[:, sl] = _dot(wmt_ref[g], dmb)
            dw_ref[g] += _dot_nt(dmb, vn_g)
            dbs = dbs + jnp.where(lane == g, jnp.sum(dmix, axis=1, keepdims=True), 0.0)
        dbs_ref[...] += dbs
        dvn = dvn_ref[...]
        dg_ref[...] += _sum0(dvn * xhat)
        db_ref[...] += _sum0(dvn)
        dv_ref[...] = (_ln_bwd(dvn, xhat, rstd, g_ref[...]) * gvp).astype(BF16)

    blk = lambda b: pl.BlockSpec((None, SGU_BLOCK, D_MODEL), functools.partial(lambda i, b: (b, i, 0), b=b))
    row = pl.BlockSpec((SGU_BLOCK, D_MODEL), lambda i: (i, 0))
    whole = lambda s: pl.BlockSpec(s, lambda i: (0,) * len(s))
    vec = (1, D_MODEL)
    return pl.pallas_call(
        body,
        grid=(t // SGU_BLOCK,),
        in_specs=[blk(SLOT_U), blk(SLOT_V), row, pl.BlockSpec(memory_space=pl.ANY),
                  whole(wm.shape), whole(wm_t.shape), whole(bs_t.shape), whole(vec), whole(vec)],
        out_specs=[pl.BlockSpec((2, SGU_BLOCK, D_MODEL), lambda i: (SLOT_U // 2, i, 0)),
                   whole(wm.shape), whole((SGU_BLOCK, LANE)), whole(vec), whole(vec)],
        out_shape=[jax.ShapeDtypeStruct(dh.shape, BF16),
                   jax.ShapeDtypeStruct(wm.shape, F32), jax.ShapeDtypeStruct((SGU_BLOCK, LANE), F32),
                   jax.ShapeDtypeStruct(vec, F32), jax.ShapeDtypeStruct(vec, F32)],
        scratch_shapes=[pltpu.VMEM((SGU_BLOCK, D_MODEL), F32)],
        input_output_aliases={3: 0},
        compiler_params=_params(("arbitrary",)),
        name="sgu_bwd",
    )(h, h, dy, dh, wm, wm_t, bs_t, g_v, b_v)


def _split3(x):
    hi = x.astype(BF16)
    r1 = x - hi.astype(F32)
    mid = r1.astype(BF16)
    lo = (r1 - mid.astype(F32)).astype(BF16)
    return hi, mid, lo


def _tri_matmul(tri, x):
    hi, mid, lo = _split3(x)
    dot = lambda p: jnp.dot(tri, p, preferred_element_type=F32)
    return dot(hi) + dot(mid) + dot(lo)


def _lower_bound(logits):
    l0, l1 = logits[0:1, :], logits[1:2, :]
    mx = jnp.maximum(l0, l1)
    e0, e1 = jnp.exp(l0 - mx), jnp.exp(l1 - mx)
    return e0 / (e0 + e1)


def _hgrn_gates(q_raw, f_raw, lb):
    q = q_raw * jax.nn.sigmoid(q_raw)
    sig = jax.nn.sigmoid(f_raw)
    f = lb + (1.0 - lb) * sig
    row = lax.broadcasted_iota(jnp.int32, (CHUNK, CHUNK), 0)
    col = lax.broadcasted_iota(jnp.int32, (CHUNK, CHUNK), 1)
    c = _tri_matmul((row >= col).astype(BF16), jnp.log(f))
    return q, sig, f, 1.0 - f, c


def _offdiag_terms(qh, kh, ch, tb):
    rows = slice(tb * SUB, (tb + 1) * SUB)
    r = ch[tb * SUB - 1:tb * SUB, :]
    eqh = jnp.exp(ch[rows] - r)
    ekh = jnp.exp(jnp.minimum(r - ch, 0.0))
    return rows, eqh, qh[rows] * eqh, ekh, kh * ekh


def _diag_decay(cb, s, trow):
    return jnp.exp(jnp.where(trow >= s, cb - cb[s:s + 1, :], NEG))


def _split_refs(refs, n_in, n_out, n_scratch, ex):
    nx = ex.n if ex else 0
    ins, refs = refs[:n_in], refs[n_in:]
    xsrc, refs = refs[:nx], refs[nx:]
    outs, refs = refs[:n_out], refs[n_out:]
    xout, refs = refs[:nx], refs[nx:]
    return ins, outs, refs[:n_scratch], (xsrc, xout, refs[n_scratch:])


def _hgrn_fwd(h, logits, g_norm, ex=None):
    t = h.shape[1]
    nc = t // CHUNK
    per_step = CHUNKS_PER_STEP if nc % CHUNKS_PER_STEP == 0 else 1
    steps = nc // per_step

    def body(*refs):
        ins, outs, scratch, xrefs = _split_refs(refs, 3, 3, 4, ex)
        hgrn_ref, lg_ref, gn_ref = ins
        y_ref, o_ref, sall_ref = outs
        st_ref, q_s, k_s, c_s = scratch

        @pl.when(pl.program_id(0) == 0)
        def _():
            st_ref[...] = jnp.zeros_like(st_ref)

        lb = _lower_bound(lg_ref[...])
        for sub in range(per_step):
            rows = pl.ds(sub * CHUNK, CHUNK)
            chunk(lb, sall_ref.at[sub], *(r.at[rows] for r in (hgrn_ref.at[SLOT_Q], hgrn_ref.at[SLOT_F], hgrn_ref.at[SLOT_I],
                                                               hgrn_ref.at[SLOT_OG], y_ref, o_ref)),
                  gn_ref, st_ref, q_s.at[sub], k_s.at[sub], c_s.at[sub])
        _ride(ex, xrefs, pl.program_id(0), steps)

    def chunk(lb, sall_ref, q_ref, f_ref, i_ref, og_ref, y_ref, o_ref, gn_ref, st_ref, q_s, k_s, c_s):
        q, _, _, k, c = _hgrn_gates(q_ref[...], f_ref[...], lb)
        q_s[...] = q
        k_s[...] = k
        c_s[...] = c
        col64 = lax.broadcasted_iota(jnp.int32, (SUB, CHUNK), 1)
        trow = lax.broadcasted_iota(jnp.int32, (SUB, HEAD), 0)

        def head(hd, carry):
            sl = pl.ds(pl.multiple_of(hd * HEAD, HEAD), HEAD)
            qh, kh, ch, ih = q_s[:, sl], k_s[:, sl], c_s[:, sl], i_ref[:, sl]
            st = st_ref[hd]
            sall_ref[hd] = st
            c_last = ch[CHUNK - 1:CHUNK, :]
            o = _dot_nt(qh * jnp.exp(ch), st)
            st_ref[hd] = st * jnp.exp(c_last) + _dot_tn(ih, kh * jnp.exp(c_last - ch))
            a_rows = [jnp.zeros((SUB, CHUNK), F32)]
            for tb in range(1, CHUNK // SUB):
                _, _, q_hat, _, k_hat = _offdiag_terms(qh, kh, ch, tb)
                a_rows.append(jnp.where(col64 < tb * SUB, _dot_nt(q_hat, k_hat), 0.0))
            o = o + _dot(jnp.concatenate(a_rows, axis=0), ih)
            o_rows = []
            for b in range(CHUNK // SUB):
                rows = slice(b * SUB, (b + 1) * SUB)
                qb, cb, kb, ib = qh[rows], ch[rows], kh[rows], ih[rows]
                ob = jnp.zeros((SUB, HEAD), F32)
                for s in range(SUB):
                    a = jnp.sum(qb * _diag_decay(cb, s, trow) * kb[s:s + 1, :], axis=1, keepdims=True)
                    ob = ob + a * ib[s:s + 1, :]
                o_rows.append(ob)
            o = o + jnp.concatenate(o_rows, axis=0)
            o_ref[:, sl] = o
            og = og_ref[:, sl]
            on = o * lax.rsqrt(_mean(o * o) + RMS_EPS)
            y_ref[:, sl] = (on * gn_ref[:, sl] * (og * jax.nn.sigmoid(og))).astype(BF16)
            return carry

        lax.fori_loop(0, HEADS, head, 0, unroll=HEAD_UNROLL)

    rows = per_step * CHUNK
    row = pl.BlockSpec((rows, D_MODEL), lambda n: (n, 0))
    whole = lambda s: pl.BlockSpec(s, lambda n: (0,) * len(s))
    res = pl.pallas_call(
        body,
        grid=(steps,),
        in_specs=[pl.BlockSpec((4, rows, D_MODEL), lambda n: (SLOT_Q // 4, n, 0)), whole(logits.shape), whole(g_norm.shape)]
        + (ex.in_specs if ex else []),
        out_specs=[row, row, pl.BlockSpec((per_step, HEADS, HEAD, HEAD), lambda n: (n, 0, 0, 0))]
        + (ex.out_specs if ex else []),
        out_shape=[jax.ShapeDtypeStruct((t, D_MODEL), BF16), jax.ShapeDtypeStruct((t, D_MODEL), F32),
                   jax.ShapeDtypeStruct((nc, HEADS, HEAD, HEAD), F32)] + (ex.out_shape if ex else []),
        scratch_shapes=[pltpu.VMEM((HEADS, HEAD, HEAD), F32)] + [pltpu.VMEM((per_step, CHUNK, D_MODEL), F32)] * 3
        + (ex.scratch if ex else []),
        compiler_params=_params(("arbitrary",)),
        name="hgrn_fwd",
    )(h, logits, g_norm, *(ex.arrays if ex else []))
    return res[0], res[1], res[2], res[3:]


def _hgrn_bwd(h, o_all, dy, states, dh, logits, g_norm, ex=None):
    t = h.shape[1]
    nc = t // CHUNK

    def body(*refs):
        ins, outs, scratch, xrefs = _split_refs(refs, 7, 3, 8, ex)
        hgrn_ref, o_ref, dy_ref, sall_ref, _, lg_ref, gn_ref = ins
        q_ref, f_ref, i_ref, og_ref = (hgrn_ref.at[s] for s in (SLOT_Q, SLOT_F, SLOT_I, SLOT_OG))
        dqfio_ref, dlg_ref, dgn_ref = outs
        dst_ref, dlb_ref, q_s, k_s, c_s, dq_s, dk_s, dc_s = scratch
        dq_ref, df_ref, di_ref, dog_ref = (dqfio_ref.at[s] for s in (SLOT_Q, SLOT_F, SLOT_I, SLOT_OG))
        n = pl.program_id(0)

        @pl.when(n == 0)
        def _():
            dst_ref[...] = jnp.zeros_like(dst_ref)
            dlb_ref[...] = jnp.zeros_like(dlb_ref)
            dgn_ref[...] = jnp.zeros_like(dgn_ref)
            if ex:
                ex.start(*xrefs)

        lb = _lower_bound(lg_ref[...])
        q_raw = q_ref[...]
        q, sig, f, k, c = _hgrn_gates(q_raw, f_ref[...], lb)
        q_s[...] = q
        k_s[...] = k
        c_s[...] = c
        col64 = lax.broadcasted_iota(jnp.int32, (SUB, CHUNK), 1)
        trow = lax.broadcasted_iota(jnp.int32, (SUB, HEAD), 0)
        row64 = lax.broadcasted_iota(jnp.int32, (CHUNK, HEAD), 0)

        def head(hd, carry):
            sl = pl.ds(pl.multiple_of(hd * HEAD, HEAD), HEAD)
            qh, kh, ch, ih = q_s[:, sl], k_s[:, sl], c_s[:, sl], i_ref[:, sl]
            st = sall_ref[hd]
            dst = dst_ref[hd]
            oh, dyh, og, gn = o_ref[:, sl], dy_ref[:, sl], og_ref[:, sl], gn_ref[:, sl]
            sg = jax.nn.sigmoid(og)
            sil = og * sg
            rms = lax.rsqrt(_mean(oh * oh) + RMS_EPS)
            on = oh * rms
            dog_ref[:, sl] = (dyh * on * gn * _silu_grad(og, sg)).astype(BF16)
            dgn_ref[:, sl] += _sum0(dyh * on * sil)
            don = dyh * gn * sil
            do = rms * (don - on * _mean(don * on))
            dob = do.astype(BF16)

            c_last = ch[CHUNK - 1:CHUNK, :]
            eq = jnp.exp(ch)
            q_til = qh * eq
            ekl = jnp.exp(c_last - ch)
            k_til = kh * ekl
            ecl = jnp.exp(c_last)
            dq_til = _dot(dob, st)
            dk_til = _dot(ih, dst)
            di = _dot_nt(k_til, dst)
            dc_last = _sum0(dk_til * k_til) + _sum0(dst * st) * ecl
            dst_ref[hd] = _dot_tn(dob, q_til) + dst * ecl
            dq = dq_til * eq
            dc = dq_til * q_til - dk_til * k_til
            dk = dk_til * ekl

            da_full = _dot_nt(dob, ih)
            a_rows = [jnp.zeros((SUB, CHUNK), F32)]
            dq_rows = [jnp.zeros((SUB, HEAD), F32)]
            dc_rows = [jnp.zeros((SUB, HEAD), F32)]
            for tb in range(1, CHUNK // SUB):
                rows, eqh, q_hat, ekh, k_hat = _offdiag_terms(qh, kh, ch, tb)
                keep = col64 < tb * SUB
                a_rows.append(jnp.where(keep, _dot_nt(q_hat, k_hat), 0.0))
                da = jnp.where(keep, da_full[rows], 0.0)
                dq_hat = _dot(da, k_hat)
                dk_hat = _dot_tn(da, q_hat)
                dq_rows.append(dq_hat * eqh)
                dc_rows.append(dq_hat * q_hat)
                dk = dk + dk_hat * ekh
                dc = dc - dk_hat * k_hat
            di = di + _dot_tn(jnp.concatenate(a_rows, axis=0), dob)

            dk_rows, di_rows = [], []
            for b in range(CHUNK // SUB):
                rows = slice(b * SUB, (b + 1) * SUB)
                qb, cb, kb, ib, dob_ = qh[rows], ch[rows], kh[rows], ih[rows], do[rows]
                dq_diag = jnp.zeros((SUB, HEAD), F32)
                dk_diag = jnp.zeros((SUB, HEAD), F32)
                di_diag = jnp.zeros((SUB, HEAD), F32)
                for s in range(SUB):
                    ks = kb[s:s + 1, :]
                    dec = _diag_decay(cb, s, trow)
                    a = jnp.sum(qb * dec * ks, axis=1, keepdims=True)
                    gk = jnp.sum(dob_ * ib[s:s + 1, :], axis=1, keepdims=True) * dec
                    dq_diag = dq_diag + gk * ks
                    dk_diag = dk_diag + jnp.where(trow == s, _sum0(gk * qb), 0.0)
                    di_diag = di_diag + jnp.where(trow == s, _sum0(a * dob_), 0.0)
                dq_rows[b] = dq_rows[b] + dq_diag
                dc_rows[b] = dc_rows[b] + qb * dq_diag - kb * dk_diag
                dk_rows.append(dk_diag)
                di_rows.append(di_diag)
            dq = dq + jnp.concatenate(dq_rows, axis=0)
            dk = dk + jnp.concatenate(dk_rows, axis=0)
            dc = dc + jnp.concatenate(dc_rows, axis=0) + jnp.where(row64 == CHUNK - 1, dc_last, 0.0)
            di_ref[:, sl] = (di + jnp.concatenate(di_rows, axis=0)).astype(BF16)
            dq_s[:, sl] = dq
            dk_s[:, sl] = dk
            dc_s[:, sl] = dc
            return carry

        lax.fori_loop(0, HEADS, head, 0, unroll=HEAD_UNROLL)

        row = lax.broadcasted_iota(jnp.int32, (CHUNK, CHUNK), 0)
        col = lax.broadcasted_iota(jnp.int32, (CHUNK, CHUNK), 1)
        dlf = _tri_matmul((row <= col).astype(BF16), dc_s[...])
        df = dlf / f - dk_s[...]
        dlb_ref[...] += _sum0(df * (1.0 - sig))
        df_ref[...] = (df * (1.0 - lb) * sig * (1.0 - sig)).astype(BF16)
        dq_ref[...] = (dq_s[...] * _silu_grad(q_raw, jax.nn.sigmoid(q_raw))).astype(BF16)

        @pl.when(n == nc - 1)
        def _():
            d0 = dlb_ref[...] * lb * (1.0 - lb)
            dlg_ref[0:1, :] = d0
            dlg_ref[1:2, :] = -d0
            if ex:
                ex.finish(*xrefs)

    rev = lambda n: nc - 1 - n
    slots = pl.BlockSpec((4, CHUNK, D_MODEL), lambda n: (SLOT_Q // 4, rev(n), 0))
    row = pl.BlockSpec((CHUNK, D_MODEL), lambda n: (rev(n), 0))
    whole = lambda s: pl.BlockSpec(s, lambda n: (0,) * len(s))
    vec = (1, D_MODEL)
    res = pl.pallas_call(
        body,
        grid=(nc,),
        in_specs=[slots, row, row, pl.BlockSpec((None, HEADS, HEAD, HEAD), lambda n: (rev(n), 0, 0, 0)),
                  pl.BlockSpec(memory_space=pl.ANY), whole(logits.shape), whole(vec)] + (ex.in_specs if ex else []),
        out_specs=[slots, whole((2, D_MODEL)), whole(vec)] + (ex.out_specs if ex else []),
        out_shape=[jax.ShapeDtypeStruct(dh.shape, BF16), jax.ShapeDtypeStruct((2, D_MODEL), F32),
                   jax.ShapeDtypeStruct(vec, F32)] + (ex.out_shape if ex else []),
        scratch_shapes=[pltpu.VMEM((HEADS, HEAD, HEAD), F32), pltpu.VMEM(vec, F32)]
        + [pltpu.VMEM((CHUNK, D_MODEL), F32)] * 6 + (ex.scratch if ex else []),
        input_output_aliases={4: 0},
        compiler_params=_params(("arbitrary",)),
        name="hgrn_bwd",
    )(h, o_all, dy, states, dh, logits, g_norm, *(ex.arrays if ex else []))
    return res[0], res[1], res[2], res[3:]


def _merge_fwd(i, n, ga, gb, za, zb):
    return jax.nn.sigmoid(ga) * za + jax.nn.sigmoid(gb) * zb


def _merge_bwd(i, n, ga, gb, za, zb, dm):
    sa, sb = jax.nn.sigmoid(ga), jax.nn.sigmoid(gb)
    dgates = jnp.stack([(dm * za * sa * (1.0 - sa)).astype(BF16), (dm * zb * sb * (1.0 - sb)).astype(BF16)])
    return dgates, dm * sa, dm * sb


def _ln1_fwd(i, n, x, r1, g, b):
    xhat, _ = _ln_stats(ALPHA * x + r1)
    x1 = xhat * g + b
    return x1, x1


def _ln1_bwd(i, n, x, r1, dx1, g):
    xhat, rstd = _ln_stats(ALPHA * x + r1)
    dz = _ln_bwd(dx1, xhat, rstd, g)
    return dz, dz, _sum0(dx1 * xhat), _sum0(dx1)


def _ln2_loss(i, n, x1, fo, pg, pp, tgt, g, b):
    sg = jax.nn.sigmoid(pg)
    xhat, rstd = _ln_stats(ALPHA * x1 + fo + sg * pp)
    diff = xhat * g + b - tgt
    loss = 0.5 * jnp.sum(_mean(diff * diff), axis=0, keepdims=True)
    dy = diff * (1.0 / D_MODEL)
    dz = _ln_bwd(dy, xhat, rstd, g)
    return (dz, dz, dz * pp * sg * (1.0 - sg), dz * sg,
            jnp.broadcast_to(loss, (8, LANE)), _sum0(dy * xhat), _sum0(dy))


def _shift_down(cur, halo, tile):
    row = lax.broadcasted_iota(jnp.int32, cur.shape, 0)
    m1 = jnp.where(row == 0, halo[7:8, :], pltpu.roll(cur, 1, 0))
    m2 = jnp.where(row == 0, halo[6:7, :], jnp.where(row == 1, halo[7:8, :], pltpu.roll(cur, 2, 0)))
    return m1, m2


def _shift_up(cur, halo, tile):
    row = lax.broadcasted_iota(jnp.int32, cur.shape, 0)
    p1 = jnp.where(row == tile - 1, halo[0:1, :], pltpu.roll(cur, tile - 1, 0))
    p2 = jnp.where(row == tile - 2, halo[0:1, :], jnp.where(row == tile - 1, halo[1:2, :], pltpu.roll(cur, tile - 2, 0)))
    return p1, p2


def _conv_pre(i, gate, halo, w, b, tile):
    halo = jnp.where(i == 0, 0.0, halo)
    m1, m2 = _shift_down(gate, halo, tile)
    return w[0:1, :] * m2 + w[1:2, :] * m1 + w[2:3, :] * gate + b, m1, m2


def _conv_fwd(tile, i, n, gate, halo, val, w, b):
    cg, _, _ = _conv_pre(i, gate, halo, w, b, tile)
    return _gelu(cg) * val


def _conv_bwd_a(tile, i, n, gate, halo, val, dhid, w, b):
    cg, m1, m2 = _conv_pre(i, gate, halo, w, b, tile)
    act, slope = _gelu_and_grad(cg)
    dcg = dhid * val * slope
    return dcg, dhid * act, _sum0(dcg * m2), _sum0(dcg * m1), _sum0(dcg * gate), _sum0(dcg)


def _conv_bwd_b(tile, i, n, dcg, halo, w):
    dcg = dcg.astype(F32)
    halo = jnp.where(i == n - 1, 0.0, halo.astype(F32))
    p1, p2 = _shift_up(dcg, halo, tile)
    return w[2:3, :] * dcg + w[1:2, :] * p1 + w[0:1, :] * p2


def _halo_spec(width, tile, t, nxt, rows=8):
    per = tile // rows
    last = t // rows - 1
    if nxt:
        return pl.BlockSpec((rows, width), lambda i: (jnp.minimum((i + 1) * per, last), 0))
    return pl.BlockSpec((rows, width), lambda i: (jnp.maximum(i * per - 1, 0), 0))


def _adamw(i, n, w, m, v, parts):
    g = parts[0].astype(F32)
    for j in range(1, parts.shape[0]):
        g = g + parts[j].astype(F32)
    m_new = ADAM_B1 * m + (1.0 - ADAM_B1) * g
    v_new = ADAM_B2 * v + (1.0 - ADAM_B2) * (g * g)
    m_hat = m_new / (1.0 - ADAM_B1 ** ADAM_STEP)
    v_hat = v_new / (1.0 - ADAM_B2 ** ADAM_STEP)
    delta = -ADAM_LR * (m_hat / (jnp.sqrt(v_hat) + ADAM_EPS) + ADAM_WD * w)
    return g, delta, m_new, v_new


def _adam_call(w, m, v, parts, name):
    r, c = w.shape
    tile = _pick(r, (256, 128)) if r > 256 else r
    spec = pl.BlockSpec((parts.shape[0], tile, c), lambda i: (0, i, 0))
    return _rowwise(_adamw, [w, m, v, (parts, spec)], [], [(c, F32)] * 4, [], tile=tile, name=name)


def _peer(k):
    x, y, c = lax.axis_index("x"), lax.axis_index("y"), lax.axis_index("c")
    px = x ^ ((k >> 2) & 1)
    py = y ^ ((k >> 1) & 1)
    pc = c ^ (k & 1)
    return (px, py, pc), 4 * px + 2 * py + pc


def _my_index():
    return 4 * lax.axis_index("x") + 2 * lax.axis_index("y") + lax.axis_index("c")


class _Exchange:
    KINDS = ("gather", "gather+relay", "scatter", "scatter+pairs")

    def __init__(self, entries):
        assert all(k in self.KINDS for _, k in entries), [k for _, k in entries]
        self.arrays = [a for a, _ in entries]
        self.scatter = [k.startswith("scatter") for _, k in entries]
        self.relayed = ["+relay" in k for _, k in entries]
        self.pairs = ["+pairs" in k for _, k in entries]
        self.n = len(entries)
        self.in_specs = [pl.BlockSpec(memory_space=pl.ANY)] * self.n
        self.out_specs = [pl.BlockSpec(memory_space=pl.ANY)] * self.n
        shapes = [tuple(a.shape[1:]) if sc else tuple(a.shape) for a, sc in zip(self.arrays, self.scatter)]
        counts = [N_CHIP if p else N_DEV for p in self.pairs]
        self.out_shape = [jax.ShapeDtypeStruct((n,) + s, a.dtype) for n, s, a in zip(counts, shapes, self.arrays)]
        per = N_DEV - 1
        self.scratch = [pltpu.SemaphoreType.DMA((self.n * per,)), pltpu.SemaphoreType.DMA((self.n * per,)),
                        pltpu.SemaphoreType.DMA((self.n,))]

    def _copies(self, srcs, outs, sems):
        send_sems, recv_sems, local_sems = sems
        x, y, c = lax.axis_index("x"), lax.axis_index("y"), lax.axis_index("c")
        me = _my_index()
        per = N_DEV - 1
        local, first, passed, relay_arrivals, arrivals = [], [], [], [], []
        for a in range(self.n):

            def copy(k, src, dst, dev, a=a):
                return pltpu.make_async_remote_copy(
                    src_ref=src, dst_ref=dst, send_sem=send_sems.at[a * per + k], recv_sem=recv_sems.at[a * per + k],
                    device_id=dev, device_id_type=pl.DeviceIdType.MESH)

            if self.pairs[a]:
                chip = 2 * x + y
                for k in range(N_CHIP):
                    to = chip ^ k
                    piece = srcs[a].at[_slot_of_group(2 * to + c) // 2]
                    if k == 0:
                        local.append(pltpu.make_async_copy(piece, outs[a].at[chip], local_sems.at[a]))
                    else:
                        dev = (to // 2, to % 2, c)
                        first.append(copy(k - 1, piece, outs[a].at[chip], dev))
                        arrivals.append(copy(k - 1, piece, outs[a].at[to], dev))
                continue
            mine = srcs[a].at[me] if self.scatter[a] else srcs[a]
            land = outs[a].at[me]
            local.append(pltpu.make_async_copy(mine, land, local_sems.at[a]))
            if self.relayed[a]:
                block = lambda px, py, pc, a=a: outs[a].at[4 * px + 2 * py + pc]
                chips = [(1 - x, y), (x, 1 - y), (1 - x, 1 - y)]
                first.append(copy(0, mine, land, (x, y, 1 - c)))
                arrivals.append(copy(0, mine, block(x, y, 1 - c), (x, y, 1 - c)))
                for j, (px, py) in enumerate(chips):
                    first.append(copy(1 + j, mine, land, (px, py, c)))
                    relay_arrivals.append(copy(1 + j, mine, block(px, py, c), (px, py, c)))
                    passed.append(copy(4 + j, block(px, py, c), block(px, py, c), (x, y, 1 - c)))
                    arrivals.append(copy(4 + j, mine, block(px, py, 1 - c), (x, y, 1 - c)))
                continue
            for k in range(1, N_DEV):
                dev, idx = _peer(k)
                if self.scatter[a]:
                    first.append(copy(k - 1, srcs[a].at[idx], land, dev))
                else:
                    first.append(copy(k - 1, mine, land, dev))
                arrivals.append(copy(k - 1, mine, outs[a].at[idx], dev))
        return local, first, passed, relay_arrivals, arrivals

    def start(self, srcs, outs, sems):
        local, first, _, _, _ = self._copies(srcs, outs, sems)
        for cp in local + first:
            cp.start()

    def relay(self, srcs, outs, sems):
        _, _, passed, relay_arrivals, _ = self._copies(srcs, outs, sems)
        for landed, onward in zip(relay_arrivals, passed):
            landed.wait_recv()
            onward.start()

    def finish(self, srcs, outs, sems):
        local, first, passed, _, arrivals = self._copies(srcs, outs, sems)
        for cp in arrivals:
            cp.wait_recv()
        for cp in first + passed:
            cp.wait_send()
        for cp in local:
            cp.wait()


def _gather_project(x_b, shard):
    t, d = x_b.shape
    tm = _pick(t, MM_TILES)
    nrow = t // tm
    per = N_DEV - 1

    def parties():
        x, y, c = lax.axis_index("x"), lax.axis_index("y"), lax.axis_index("c")
        first = (jnp.where(c == 0, 1 - x, x), jnp.where(c == 0, y, 1 - y))
        second = (jnp.where(c == 0, x, 1 - x), jnp.where(c == 0, 1 - y, y))
        diagonal = (1 - x, 1 - y)
        return ((x, y, c), (x, y, 1 - c), [(px, py, c) for px, py in (first, second, diagonal)],
                [(px, py, 1 - c) for px, py in (second, first, diagonal)])

    slot = lambda dev: _slot_of_group(4 * dev[0] + 2 * dev[1] + dev[2])
    ici_step, passed_step = (2, 3, 6), (4, 5, 7)
    me, sibling, over_ici, passed_on = parties()
    by_step = {0: me, 1: sibling, **dict(zip(ici_step, over_ici)), **dict(zip(passed_step, passed_on))}
    order = jnp.stack([slot(by_step[j]) for j in range(N_DEV)]).astype(jnp.int32)

    def body(order_ref, x_ref, shard_ref, h_ref, wall_ref, wbuf, fetch_sems, send_sems, recv_sems, local_sem):
        del order_ref
        me, sibling, over_ici, passed_on = parties()
        j, i = pl.program_id(0), pl.program_id(1)
        land = lambda dev: wall_ref.at[slot(dev)]

        def copy(k, src, block, to):
            return pltpu.make_async_remote_copy(src_ref=src, dst_ref=land(block), send_sem=send_sems.at[k],
                                                recv_sem=recv_sems.at[k], device_id=to, device_id_type=pl.DeviceIdType.MESH)

        def fetch(step):
            src = shard_ref if step == 0 else land(by_step[step])
            return pltpu.make_async_copy(src, wbuf.at[step % 2], fetch_sems.at[step % 2])

        by_step = {0: me, 1: sibling, **dict(zip(ici_step, over_ici)), **dict(zip(passed_step, passed_on))}
        keep = pltpu.make_async_copy(shard_ref, land(me), local_sem)
        first = [copy(0, shard_ref, me, sibling)] + [copy(1 + n, shard_ref, me, dev) for n, dev in enumerate(over_ici[:2])]
        relay = copy(3, land(over_ici[0]), over_ici[0], over_ici[1])
        onward = [copy(4 + n, land(dev), dev, sibling) for n, dev in enumerate(over_ici)]

        @pl.when(jnp.logical_and(i == 0, j == 0))
        def _():
            for cp in (first[1], first[0]):
                cp.start()
            keep.start()
            fetch(0).start()
            fetch(0).wait()

        @pl.when(jnp.logical_and(i == 0, j == 1))
        def _():
            first[2].start()

        for step in range(1, N_DEV):
            @pl.when(jnp.logical_and(i == nrow - 1, j == step - 1))
            def _(step=step):
                if step == 1:
                    copy(0, shard_ref, sibling, me).wait_recv()
                for n, dev in enumerate(over_ici):
                    if step == ici_step[n]:
                        copy(1 + n, shard_ref, dev, me).wait_recv()
                        if n == 0:
                            relay.start()
                        onward[n].start()
                for n, dev in enumerate(passed_on):
                    if step == passed_step[n]:
                        copy(4 + n, shard_ref, dev, me).wait_recv()
                fetch(step).start()

            @pl.when(jnp.logical_and(i == 0, j == step))
            def _(step=step):
                fetch(step).wait()

        h_ref[...] = _dot(x_ref[...], wbuf[j % 2])

        @pl.when(jnp.logical_and(i == nrow - 1, j == N_DEV - 1))
        def _():
            for cp in first + [relay] + onward:
                cp.wait_send()
            keep.wait()

    assert nrow >= 2, "a group is fetched during the last row step of the group before it"
    need = 2 * (tm * d * 2 + tm * d * 4) + 2 * d * d * 2 + tm * d * 4
    h, w_all = pl.pallas_call(
        body,
        grid_spec=pltpu.PrefetchScalarGridSpec(
            num_scalar_prefetch=1,
            grid=(N_DEV, nrow),
            in_specs=[pl.BlockSpec((tm, d), lambda j, i, order: (i, 0)), pl.BlockSpec(memory_space=pl.ANY)],
            out_specs=[pl.BlockSpec((None, tm, d), lambda j, i, order: (order[j], i, 0)), pl.BlockSpec(memory_space=pl.ANY)],
            scratch_shapes=[pltpu.VMEM((2, d, d), BF16), pltpu.SemaphoreType.DMA((2,)), pltpu.SemaphoreType.DMA((per,)),
                            pltpu.SemaphoreType.DMA((per,)), pltpu.SemaphoreType.DMA],
        ),
        out_shape=[jax.ShapeDtypeStruct((N_DEV, t, d), F32), jax.ShapeDtypeStruct((N_DEV, d, d), BF16)],
        compiler_params=_params(("arbitrary", "arbitrary"), need),
        name="gather_project",
    )(order, x_b, shard)
    return h, w_all


def _pair_sums(g):
    n, r, cols = g.shape
    half = n // 2

    def swap(g_ref, got_ref, send_sems, recv_sems):
        x, y, c = lax.axis_index("x"), lax.axis_index("y"), lax.axis_index("c")
        copies = [pltpu.make_async_remote_copy(
            src_ref=g_ref.at[2 * j + 1 - c], dst_ref=got_ref.at[j], send_sem=send_sems.at[j], recv_sem=recv_sems.at[j],
            device_id=(x, y, 1 - c), device_id_type=pl.DeviceIdType.MESH) for j in range(half)]
        for cp in copies:
            cp.start()
        for cp in copies:
            cp.wait()

    got = pl.pallas_call(
        swap,
        in_specs=[pl.BlockSpec(memory_space=pl.ANY)],
        out_specs=pl.BlockSpec(memory_space=pl.ANY),
        out_shape=jax.ShapeDtypeStruct((half, r, cols), g.dtype),
        scratch_shapes=[pltpu.SemaphoreType.DMA((half,))] * 2,
        name="pair_swap",
    )(g)

    def add(mine_ref, got_ref, out_ref):
        mine = jnp.where(lax.axis_index("c") == 0, mine_ref[0].astype(F32), mine_ref[1].astype(F32))
        out_ref[...] = (mine + got_ref[...].astype(F32)).astype(out_ref.dtype)

    tile = _pick(r, (512, 256, 128))
    return pl.pallas_call(
        add,
        grid=(half, r // tile),
        in_specs=[pl.BlockSpec((None, 2, tile, cols), lambda j, i: (j, 0, i, 0)),
                  pl.BlockSpec((None, tile, cols), lambda j, i: (j, i, 0))],
        out_specs=pl.BlockSpec((None, tile, cols), lambda j, i: (j, i, 0)),
        out_shape=jax.ShapeDtypeStruct((half, r, cols), g.dtype),
        compiler_params=_params(("parallel", "parallel")),
        name="pair_add",
    )(g.reshape(half, 2, r, cols), got)


def _local_step(x, p, tgt, small, comm):
    t = x.shape[0]
    tile = _pick(t, (256, 128))
    tall = _pick(t, (512, 256, 128))
    d = D_MODEL
    act_b, act_f = (d, BF16), (d, F32)
    x_b, p_b = x.astype(BF16), p.astype(BF16)

    chunk_id = jnp.arange(SGU_BLOCK) // CHUNK
    mask = chunk_id[:, None] >= chunk_id[None, :]
    wm = jnp.where(mask[None], small["sgu_w_s"], 0.0)
    wm_b = wm.astype(BF16)
    wm_t = jnp.swapaxes(wm, 1, 2).astype(BF16)
    bs_t = small["sgu_b_s"].T

    h, w_in = comm.project_in(x_b)
    y_a, got_a = _sgu_fwd(h, wm_b, bs_t, small["sgu_norm_g"], small["sgu_norm_b"], ex=comm.weights_exchange(0))
    y_b, o_all, states, got_b = _hgrn_fwd(h, small["lb_logits"], small["hgrn_norm_g"], ex=comm.weights_exchange(1))
    wts, conv_w = comm.weights(list(got_a) + list(got_b))
    z_a = _mm(y_a, wts["w_a"], out_dtype=F32, name="mm_za")
    z_b = _mm(y_b, wts["w_b"], out_dtype=F32, name="mm_zb")
    gates = [(h, SLOT_GA), (h, SLOT_GB)]
    merged, = _rowwise(_merge_fwd, gates + [z_a, z_b], [], [act_b], [], tile=tall, name="merge_fwd")
    r1 = _mm(merged, wts["w_o"], out_dtype=F32, name="mm_r1")
    x1, x1_b = _rowwise(_ln1_fwd, [x, r1], [small["ln1_g"], small["ln1_b"]], [act_f, act_b], [], tile=tall, name="ln1_fwd")
    gate = _mm(x1_b, wts["w_g"], out_dtype=F32, name="mm_gate")
    val = _mm(x1_b, wts["w_v"], out_dtype=F32, name="mm_val")
    pg = _mm(x1_b, wts["w_pg"], out_dtype=F32, name="mm_pg")
    pp = _mm(p_b, wts["w_pp"], out_dtype=F32, name="mm_pp")
    hid, = _rowwise(functools.partial(_conv_fwd, tile), [gate, (gate, _halo_spec(D_FF, tile, t, False)), val],
                    [conv_w, small["conv_b"]], [(D_FF, BF16)], [], tile=tile, name="conv_fwd")
    fo = _mm(hid, wts["w_down"], out_dtype=F32, name="mm_down")
    dz2, dz2_b, dpg, dpp, loss, dg2, db2 = _rowwise(
        _ln2_loss, [x1, fo, pg, pp, tgt], [small["ln2_g"], small["ln2_b"]],
        [act_f, act_b, act_b, act_b], [(8, LANE), (1, d), (1, d)], tile=tall, name="ln2_loss")

    dhid = _mm(dz2_b, wts["w_down"], out_dtype=BF16, name="mm_dhid", trans_b=True)
    g_down = _mm_tn(hid, dz2_b, out_dtype=BF16, name="mm_g_down")
    dcg, dval, dcw0, dcw1, dcw2, dcb = _rowwise(
        functools.partial(_conv_bwd_a, tile), [gate, (gate, _halo_spec(D_FF, tile, t, False)), val, dhid],
        [conv_w, small["conv_b"]], [(D_FF, BF16), (D_FF, BF16)], [(1, D_FF)] * 4, tile=tile, name="conv_bwd_a")
    dgate, = _rowwise(functools.partial(_conv_bwd_b, tile), [dcg, (dcg, _halo_spec(D_FF, tile, t, True, rows=16))],
                      [conv_w], [(D_FF, BF16)], [], tile=tile, name="conv_bwd_b")
    g_g = _mm_tn(x1_b, dgate, out_dtype=BF16, name="mm_g_gate")
    g_v = _mm_tn(x1_b, dval, out_dtype=BF16, name="mm_g_val")
    g_pg = _mm_tn(x1_b, dpg, out_dtype=BF16, name="mm_g_pg")
    g_pp = _mm_tn(p_b, dpp, out_dtype=BF16, name="mm_g_pp")
    dx1 = _mm(dgate, wts["w_g"], out_dtype=F32, name="mm_dx1_gate", trans_b=True, adds=[(dz2, ALPHA)])
    dx1 = _mm(dval, wts["w_v"], out_dtype=F32, name="mm_dx1_val", trans_b=True, adds=[(dx1, 1.0)])
    dx1 = _mm(dpg, wts["w_pg"], out_dtype=F32, name="mm_dx1_pg", trans_b=True, adds=[(dx1, 1.0)])
    dz1, dz1_b, dg1, db1 = _rowwise(_ln1_bwd, [x, r1, dx1], [small["ln1_g"]], [act_f, act_b], [(1, d), (1, d)],
                                    tile=tall, name="ln1_bwd")
    g_o = _mm_tn(merged, dz1_b, out_dtype=BF16, name="mm_g_o")
    dm = _mm(dz1_b, wts["w_o"], out_dtype=BF16, name="mm_dm", trans_b=True)
    dh, dza, dzb = _rowwise(_merge_bwd, gates + [z_a, z_b, dm], [],
                            [("stack", 2, SLOT_GA // 2, 8, d, BF16), act_b, act_b], [], tile=tall, name="merge_bwd")
    g_a = _mm_tn(y_a, dza, out_dtype=BF16, name="mm_g_a")
    g_b = _mm_tn(y_b, dzb, out_dtype=BF16, name="mm_g_b")
    dy_a = _mm(dza, wts["w_a"], out_dtype=BF16, name="mm_dya", trans_b=True)
    dy_b = _mm(dzb, wts["w_b"], out_dtype=F32, name="mm_dyb", trans_b=True)
    dh, dws, dbs, dgv_n, dbv_n = _sgu_bwd(h, dy_a, dh, wm_b, wm_t, bs_t, small["sgu_norm_g"], small["sgu_norm_b"])
    big = dict(w_a=g_a, w_b=g_b, w_o=g_o, w_g=g_g, w_v=g_v, w_down=g_down, w_pp=g_pp, w_pg=g_pg)
    sm = dict(sgu_w_s=jnp.where(mask[None], dws, 0.0), sgu_b_s=dbs[:, :GROUPS].T, sgu_norm_g=dgv_n, sgu_norm_b=dbv_n,
              ln1_g=dg1, ln1_b=db1, conv_w=jnp.concatenate([dcw0, dcw1, dcw2], axis=0), conv_b=dcb, ln2_g=dg2, ln2_b=db2,
              loss=loss)
    dh, dlogits, dgn, got = _hgrn_bwd(h, o_all, dy_b, states, dh, small["lb_logits"], small["hgrn_norm_g"],
                                      ex=comm.grads_exchange(big, sm))
    comm.grads_done(got)
    g_in = _mm_tn(x_b, dh, out_dtype=BF16, name="mm_g_in")
    ex = comm.last_exchange(g_in, dict(lb_logits=dlogits, hgrn_norm_g=dgn))
    res = _mm(dh, w_in, out_dtype=F32, name="mm_dx", trans_b=True, reduce_b=True, adds=[(dz1, ALPHA)], ex=ex)
    grad_x, got = res if ex else (res, ())
    comm.last_done(got)
    return grad_x


_SMALL_EARLY = ["sgu_w_s", "sgu_b_s", "sgu_norm_g", "sgu_norm_b", "ln1_g", "ln1_b", "ffn_conv_b", "ln2_g", "ln2_b"]
_SMALL_LATE = ["hgrn_lb_logits", "hgrn_norm_g"]
N_TAPS = D_FF // N_DEV
UP_COLS = 2 * D_FF // N_DEV


class _StepExchanges:
    def __init__(self, w_in_shard, shards):
        self.w_in_shard = w_in_shard
        self.shards = shards

    def project_in(self, x_b):
        return _gather_project(x_b, self.w_in_shard)

    def weights_exchange(self, part):
        return _Exchange([(s, "gather+relay") for s in (self.shards[:2] if part == 0 else self.shards[2:])])

    def weights(self, got):
        d, f = D_MODEL, D_FF
        w_br_g, w_o_g, w_up_g, w_down_g, w_pp_g, w_pg_g, conv_g = got
        w_br = w_br_g.transpose(1, 0, 2, 3).reshape(2, d, d)
        w_up = w_up_g.transpose(1, 0, 2).reshape(d, 2, f).transpose(1, 0, 2)
        wts = dict(w_a=w_br[0], w_b=w_br[1], w_o=w_o_g.reshape(d, d), w_g=w_up[0], w_v=w_up[1],
                   w_down=w_down_g.reshape(f, d), w_pp=w_pp_g.transpose(1, 0, 2).reshape(256, d), w_pg=w_pg_g.reshape(d, d))
        return wts, conv_g.transpose(1, 0, 2).reshape(3, f)

    def grads_exchange(self, big, sm):
        d = D_MODEL
        parts = [jnp.stack([big["w_a"], big["w_b"]]).reshape(2, N_DEV, 128, d).transpose(1, 0, 2, 3),
                 big["w_o"].reshape(N_DEV, 128, d),
                 jnp.concatenate([big["w_g"], big["w_v"]], axis=1).reshape(d, N_DEV, UP_COLS).transpose(1, 0, 2),
                 big["w_down"].reshape(N_DEV, N_TAPS, d),
                 big["w_pp"].reshape(256, N_DEV, 128).transpose(1, 0, 2),
                 big["w_pg"].reshape(N_DEV, 128, d)]
        packed, self.rows_early = _pack([sm[k] for k in ("sgu_w_s", "sgu_b_s", "sgu_norm_g", "sgu_norm_b", "ln1_g", "ln1_b",
                                                         "conv_b", "ln2_g", "ln2_b", "conv_w", "loss")])
        return _Exchange([(a, "scatter") for a in parts] + [(packed, "gather")])

    def grads_done(self, got):
        self.recv, self.small_early = got[:6], got[6]

    def last_exchange(self, g_in, sm):
        packed, self.rows_late = _pack([sm["lb_logits"], sm["hgrn_norm_g"]])
        return _Exchange([(_pair_sums(g_in), "scatter+pairs"), (packed, "gather")])

    def last_done(self, got):
        self.recv_in, self.small_late = got


def _rows128(a):
    flat = a.reshape(-1)
    rows = -(-flat.shape[0] // (8 * LANE)) * 8
    return jnp.pad(flat, (0, rows * LANE - flat.shape[0])).reshape(rows, LANE)


def _pack(parts):
    blocks = [_rows128(a) for a in parts]
    return jnp.concatenate(blocks, axis=0), [b.shape[0] for b in blocks]


def _unpack(packed, shapes, rows):
    out, r0 = [], 0
    for shp, r in zip(shapes, rows):
        n = math.prod(shp)
        out.append(packed[r0:r0 + r].reshape(-1)[:n].reshape(shp))
        r0 += r
    return out


def kernel(x, p, w_in, sgu_w_s, sgu_b_s, sgu_norm_g, sgu_norm_b, hgrn_lb_logits, hgrn_norm_g, w_branch, w_out, ln1_g, ln1_b, ffn_w_up, ffn_conv_w, ffn_conv_b, ffn_w_down, ln2_g, ln2_b, ple_w_proj, ple_w_gate, loss_target, m_w_in, m_sgu_w_s, m_sgu_b_s, m_sgu_norm_g, m_sgu_norm_b, m_hgrn_lb_logits, m_hgrn_norm_g, m_w_branch, m_w_out, m_ln1_g, m_ln1_b, m_ffn_w_up, m_ffn_conv_w, m_ffn_conv_b, m_ffn_w_down, m_ln2_g, m_ln2_b, m_ple_w_proj, m_ple_w_gate, v_w_in, v_sgu_w_s, v_sgu_b_s, v_sgu_norm_g, v_sgu_norm_b, v_hgrn_lb_logits, v_hgrn_norm_g, v_w_branch, v_w_out, v_ln1_g, v_ln1_b, v_ffn_w_up, v_ffn_conv_w, v_ffn_conv_b, v_ffn_w_down, v_ln2_g, v_ln2_b, v_ple_w_proj, v_ple_w_gate):
    weights = dict(w_in=w_in, sgu_w_s=sgu_w_s, sgu_b_s=sgu_b_s, sgu_norm_g=sgu_norm_g, sgu_norm_b=sgu_norm_b,
                   hgrn_lb_logits=hgrn_lb_logits, hgrn_norm_g=hgrn_norm_g, w_branch=w_branch, w_out=w_out,
                   ln1_g=ln1_g, ln1_b=ln1_b, ffn_w_up=ffn_w_up, ffn_conv_w=ffn_conv_w, ffn_conv_b=ffn_conv_b,
                   ffn_w_down=ffn_w_down, ln2_g=ln2_g, ln2_b=ln2_b, ple_w_proj=ple_w_proj, ple_w_gate=ple_w_gate)
    mom_m = dict(w_in=m_w_in, sgu_w_s=m_sgu_w_s, sgu_b_s=m_sgu_b_s, sgu_norm_g=m_sgu_norm_g, sgu_norm_b=m_sgu_norm_b,
                 hgrn_lb_logits=m_hgrn_lb_logits, hgrn_norm_g=m_hgrn_norm_g, w_branch=m_w_branch, w_out=m_w_out,
                 ln1_g=m_ln1_g, ln1_b=m_ln1_b, ffn_w_up=m_ffn_w_up, ffn_conv_w=m_ffn_conv_w, ffn_conv_b=m_ffn_conv_b,
                 ffn_w_down=m_ffn_w_down, ln2_g=m_ln2_g, ln2_b=m_ln2_b, ple_w_proj=m_ple_w_proj, ple_w_gate=m_ple_w_gate)
    mom_v = dict(w_in=v_w_in, sgu_w_s=v_sgu_w_s, sgu_b_s=v_sgu_b_s, sgu_norm_g=v_sgu_norm_g, sgu_norm_b=v_sgu_norm_b,
                 hgrn_lb_logits=v_hgrn_lb_logits, hgrn_norm_g=v_hgrn_norm_g, w_branch=v_w_branch, w_out=v_w_out,
                 ln1_g=v_ln1_g, ln1_b=v_ln1_b, ffn_w_up=v_ffn_w_up, ffn_conv_w=v_ffn_conv_w, ffn_conv_b=v_ffn_conv_b,
                 ffn_w_down=v_ffn_w_down, ln2_g=v_ln2_g, ln2_b=v_ln2_b, ple_w_proj=v_ple_w_proj, ple_w_gate=v_ple_w_gate)
    d, f = D_MODEL, D_FF
    me = _my_index()

    comm = _StepExchanges(w_in[0].astype(BF16),
                          [w_branch[0].astype(BF16), w_out[0].astype(BF16), ffn_w_up[0].astype(BF16),
                           ffn_w_down[0].astype(BF16), ple_w_proj[0].astype(BF16), ple_w_gate[0].astype(BF16), ffn_conv_w[0]])
    small = dict(sgu_w_s=sgu_w_s[0], sgu_b_s=sgu_b_s[0], sgu_norm_g=sgu_norm_g, sgu_norm_b=sgu_norm_b,
                 lb_logits=hgrn_lb_logits, hgrn_norm_g=hgrn_norm_g, ln1_g=ln1_g, ln1_b=ln1_b, ln2_g=ln2_g, ln2_b=ln2_b,
                 conv_b=ffn_conv_b)
    grad_x = _local_step(x[0], p[0, 0], loss_target[0], small, comm)

    out = {}

    def adam(name, parts8, shape2d):
        w2, m2, v2 = (a.reshape(shape2d) for a in (weights[name], mom_m[name], mom_v[name]))
        res = _adam_call(w2, m2, v2, parts8.reshape(parts8.shape[:1] + shape2d), "adam_" + name)
        out[name] = tuple(r.reshape(weights[name].shape) for r in res)

    adam("w_in", comm.recv_in, (d, d))
    adam("w_branch", comm.recv[0], (256, d))
    adam("w_out", comm.recv[1], (128, d))
    adam("ffn_w_up", comm.recv[2], (d, UP_COLS))
    adam("ffn_w_down", comm.recv[3], (N_TAPS, d))
    adam("ple_w_proj", comm.recv[4], (256, 128))
    adam("ple_w_gate", comm.recv[5], (128, d))

    def adam_small(names, extra_w, extra_m, extra_v, extra_shapes, parts8, rows, label):
        pk = lambda src, extra: _pack([src[n] for n in names] + extra)[0]
        res = _adam_call(pk(weights, extra_w), pk(mom_m, extra_m), pk(mom_v, extra_v), parts8, label)
        shapes = [weights[n].shape for n in names] + extra_shapes
        unpacked = [_unpack(r, shapes, rows) for r in res]
        for j, n in enumerate(names):
            out[n] = tuple(u[j] for u in unpacked)
        return [[u[len(names) + j] for u in unpacked] for j in range(len(extra_shapes))]

    blank = jnp.zeros((8, LANE), F32)
    taps, loss_rows = adam_small(
        _SMALL_EARLY, [_place_taps(ffn_conv_w[0], me, f), blank], [_place_taps(m_ffn_conv_w[0], me, f), blank],
        [_place_taps(v_ffn_conv_w[0], me, f), blank + 1.0], [(3, f), (8, LANE)], comm.small_early, comm.rows_early,
        "adam_small_early")
    adam_small(_SMALL_LATE, [], [], [], [], comm.small_late, comm.rows_late, "adam_small_late")
    out["ffn_conv_w"] = tuple(lax.dynamic_slice_in_dim(u, me * N_TAPS, N_TAPS, axis=1)[None] for u in taps)
    loss = loss_rows[0][0, 0]

    order = ["w_in", "sgu_w_s", "sgu_b_s", "sgu_norm_g", "sgu_norm_b", "hgrn_lb_logits", "hgrn_norm_g", "w_branch", "w_out",
             "ln1_g", "ln1_b", "ffn_w_up", "ffn_conv_w", "ffn_conv_b", "ffn_w_down", "ln2_g", "ln2_b", "ple_w_proj", "ple_w_gate"]
    return (loss, grad_x[None], *[out[n][0] for n in order], *[out[n][1] for n in order],
            *[out[n][2] for n in order], *[out[n][3] for n in order])


def _place_taps(shard, me, f):
    return lax.dynamic_update_slice_in_dim(jnp.zeros((3, f), F32), shard, me * N_TAPS, axis=1)
```

```python
import functools
import math

import jax
import jax.numpy as jnp
from jax import lax
from jax.experimental import pallas as pl
from jax.experimental.pallas import tpu as pltpu

F32 = jnp.float32
BF16 = jnp.bfloat16

N_DEV = 8
N_CHIP = 4
D_MODEL = 1024
CHUNK = 64
SUB = 16
SGU_BLOCK = 128
GROUPS = 8
HEAD = 128
HEADS = 8
CHUNKS_PER_STEP = 2
HEAD_UNROLL = 8
D_FF = 2816
LN_EPS = 1e-5
RMS_EPS = 1e-6
ALPHA = 2.0 ** 0.25
GELU_K = math.sqrt(2.0 / math.pi)
GELU_C = 0.044715
NEG = -1e30
ADAM_LR, ADAM_B1, ADAM_B2, ADAM_EPS, ADAM_WD, ADAM_STEP = 0.001, 0.9, 0.999, 1e-08, 0.01, 10
LANE = 128
SLOT_Q, SLOT_F, SLOT_I, SLOT_OG, SLOT_U, SLOT_V, SLOT_GA, SLOT_GB = range(8)


def _slot_of_group(k):
    return jnp.where(k < 2, k + 4, jnp.where(k < 6, k - 2, k))


MIB = 1024 * 1024
VMEM_V7X = 64 * MIB
VMEM_FLOOR = 32 * MIB
MM_TILES = (1024, 1408, 512, 256, 128)


def _params(sem, need=0):
    limit = min(max(need + need // 4, VMEM_FLOOR), VMEM_V7X - 4 * MIB)
    return pltpu.CompilerParams(dimension_semantics=sem, vmem_limit_bytes=limit)


def _pick(n, prefs):
    for t in prefs:
        if n % t == 0:
            return t
    return n


def _gelu(x):
    return 0.5 * x * (1.0 + jnp.tanh(GELU_K * (x + GELU_C * x * x * x)))


def _gelu_and_grad(x):
    x2 = x * x
    t = jnp.tanh(GELU_K * x * (1.0 + GELU_C * x2))
    half = 0.5 * (1.0 + t)
    return x * half, half + 0.5 * x * (1.0 - t * t) * GELU_K * (1.0 + 3.0 * GELU_C * x2)


def _silu_grad(x, s):
    return s * (1.0 + x * (1.0 - s))


def _dot(a, b):
    return jnp.dot(a.astype(BF16), b.astype(BF16), preferred_element_type=F32)


def _dot_nt(a, b):
    return lax.dot_general(a.astype(BF16), b.astype(BF16), (((1,), (1,)), ((), ())), preferred_element_type=F32)


def _dot_tn(a, b):
    return lax.dot_general(a.astype(BF16), b.astype(BF16), (((0,), (0,)), ((), ())), preferred_element_type=F32)


def _mean(x):
    return jnp.mean(x, axis=-1, keepdims=True)


def _sum0(x):
    return jnp.sum(x, axis=0, keepdims=True)


def _mm(a, b, *, out_dtype, name, trans_b=False, reduce_b=False, adds=(), ex=None):
    squeeze = b.ndim == 2
    a3 = a if a.ndim == 3 else a[None]
    b3 = b if b.ndim == 3 else b[None]
    ba, m, k = a3.shape
    bb = b3.shape[0]
    n = b3.shape[1] if trans_b else b3.shape[2]
    tm = _pick(m, MM_TILES)
    tn = _pick(n, MM_TILES)
    if reduce_b:
        bo, steps = 1, bb
        a_map = lambda o, i, j, r: (r if ba > 1 else 0, i, 0)
        b_map = (lambda o, i, j, r: (r, j, 0)) if trans_b else (lambda o, i, j, r: (r, 0, j))
    else:
        bo, steps = bb, 1
        a_map = lambda o, i, j, r: (o if ba > 1 else 0, i, 0)
        b_map = (lambda o, i, j, r: (o, j, 0)) if trans_b else (lambda o, i, j, r: (o, 0, j))
    o_map = lambda o, i, j, r: (o, i, j)
    add_arrays = [x if x.ndim == 3 else x[None] for x, _ in adds]
    add_scales = [s for _, s in adds]
    n_add = len(adds)
    dot = _dot_nt if trans_b else _dot

    def finish(acc, add_refs, o_ref):
        for ref, s in zip(add_refs, add_scales):
            acc = acc + s * ref[...].astype(F32)
        o_ref[...] = acc.astype(o_ref.dtype)

    grid = (bo, m // tm, n // tn, steps)

    def body(*refs):
        ins, (o_ref,), scratch, xrefs = _split_refs(refs, 2 + n_add, 1, 1 if reduce_b else 0, ex)
        a_ref, b_ref, add_refs = ins[0], ins[1], ins[2:]
        step = ((pl.program_id(0) * grid[1] + pl.program_id(1)) * grid[2] + pl.program_id(2)) * grid[3] + pl.program_id(3)
        if ex:
            @pl.when(step == 0)
            def _():
                ex.start(*xrefs)

        if reduce_b:
            acc, = scratch
            r = pl.program_id(3)

            @pl.when(r == 0)
            def _():
                acc[...] = jnp.zeros_like(acc)

            acc[...] += dot(a_ref[...], b_ref[...])

            @pl.when(r == steps - 1)
            def _():
                finish(acc[...], add_refs, o_ref)
        else:
            finish(dot(a_ref[...], b_ref[...]), add_refs, o_ref)

        if ex:
            @pl.when(step == math.prod(grid) - 1)
            def _():
                ex.finish(*xrefs)

    b_block = (None, tn, k) if trans_b else (None, k, tn)
    out_bytes = tm * tn * jnp.dtype(out_dtype).itemsize
    need = 2 * (tm * k * a3.dtype.itemsize + k * tn * b3.dtype.itemsize + out_bytes + n_add * tm * tn * 4)
    need += 2 * tm * tn * 4
    sem = ("arbitrary",) * 4 if ex else ("parallel", "parallel", "parallel", "arbitrary")
    res = pl.pallas_call(
        body,
        grid=grid,
        in_specs=[pl.BlockSpec((None, tm, k), a_map), pl.BlockSpec(b_block, b_map)]
        + [pl.BlockSpec((None, tm, tn), o_map) for _ in adds] + (ex.in_specs if ex else []),
        out_specs=[pl.BlockSpec((None, tm, tn), o_map)] + (ex.out_specs if ex else []),
        out_shape=[jax.ShapeDtypeStruct((bo, m, n), out_dtype)] + (ex.out_shape if ex else []),
        scratch_shapes=([pltpu.VMEM((tm, tn), F32)] if reduce_b else []) + (ex.scratch if ex else []),
        compiler_params=_params(sem, need),
        name=name,
    )(a3, b3, *add_arrays, *(ex.arrays if ex else []))
    out = res[0][0] if (reduce_b or squeeze) else res[0]
    return (out, res[1:]) if ex else out


def _mm_tn(a, b, *, out_dtype, name):
    squeeze = b.ndim == 2
    b3 = b if b.ndim == 3 else b[None]
    t, m = a.shape
    bb, _, n = b3.shape
    tm = _pick(m, MM_TILES)
    tn = _pick(n, MM_TILES)
    tt = _pick(t, (1024, 512, 256, 128))
    steps = t // tt
    need = 2 * (tt * tm * a.dtype.itemsize + tt * tn * b3.dtype.itemsize + tm * tn * jnp.dtype(out_dtype).itemsize)
    need += 2 * tm * tn * 4

    def body(a_ref, b_ref, o_ref, acc):
        r = pl.program_id(3)

        @pl.when(r == 0)
        def _():
            acc[...] = jnp.zeros_like(acc)

        acc[...] += _dot_tn(a_ref[...], b_ref[...])

        @pl.when(r == steps - 1)
        def _():
            o_ref[...] = acc[...].astype(o_ref.dtype)

    out = pl.pallas_call(
        body,
        grid=(bb, m // tm, n // tn, steps),
        in_specs=[pl.BlockSpec((tt, tm), lambda o, i, j, r: (r, i)),
                  pl.BlockSpec((None, tt, tn), lambda o, i, j, r: (o, r, j))],
        out_specs=pl.BlockSpec((None, tm, tn), lambda o, i, j, r: (o, i, j)),
        out_shape=jax.ShapeDtypeStruct((bb, m, n), out_dtype),
        scratch_shapes=[pltpu.VMEM((tm, tn), F32)],
        compiler_params=_params(("parallel", "parallel", "parallel", "arbitrary"), need),
        name=name,
    )(a, b3)
    return out[0] if squeeze else out


def _rowwise(fn, rows, consts, row_outs, acc_outs, *, tile, name):
    first = rows[0][0] if isinstance(rows[0], tuple) else rows[0]
    t = first.shape[-2]
    steps = t // tile
    arrays, in_specs = [], []
    for r in rows:
        if isinstance(r, tuple) and isinstance(r[1], pl.BlockSpec):
            arrays.append(r[0])
            in_specs.append(r[1])
        elif isinstance(r, tuple):
            arr, bidx = r
            arrays.append(arr)
            in_specs.append(pl.BlockSpec((None, tile, arr.shape[-1]), functools.partial(lambda i, b: (b, i, 0), b=bidx)))
        else:
            arrays.append(r)
            in_specs.append(pl.BlockSpec((tile, r.shape[-1]), lambda i: (i, 0)))
    for c in consts:
        arrays.append(c)
        in_specs.append(pl.BlockSpec(c.shape, lambda i: (0, 0)))
    n_in, n_row = len(arrays), len(row_outs)
    out_shape, out_specs = [], []
    for ro in row_outs:
        if ro[0] == "stack":
            _, cnt, blk, total, w, dt = ro
            out_shape.append(jax.ShapeDtypeStruct((total, t, w), dt))
            out_specs.append(pl.BlockSpec((cnt, tile, w), functools.partial(lambda i, b: (b, i, 0), b=blk)))
        else:
            w, dt = ro
            out_shape.append(jax.ShapeDtypeStruct((t, w), dt))
            out_specs.append(pl.BlockSpec((tile, w), lambda i: (i, 0)))
    out_shape += [jax.ShapeDtypeStruct(s, F32) for s in acc_outs]
    out_specs += [pl.BlockSpec(s, lambda i: (0, 0)) for s in acc_outs]
    blocks = [math.prod(d for d in sp.block_shape if d) * arr.dtype.itemsize for sp, arr in zip(in_specs, arrays)]
    blocks += [math.prod(d for d in sp.block_shape if d) * jnp.dtype(sh.dtype).itemsize
               for sp, sh in zip(out_specs, out_shape)]
    need = 2 * sum(blocks) + 6 * tile * max(a.shape[-1] for a in arrays) * 4

    def body(*refs):
        ins, outs = refs[:n_in], refs[n_in:]
        i = pl.program_id(0)
        res = fn(i, steps, *[r[...] for r in ins])
        res = res if isinstance(res, (tuple, list)) else (res,)
        for ref, val in zip(outs[:n_row], res[:n_row]):
            ref[...] = val.astype(ref.dtype)
        if acc_outs:
            @pl.when(i == 0)
            def _():
                for ref in outs[n_row:]:
                    ref[...] = jnp.zeros_like(ref)

            for ref, val in zip(outs[n_row:], res[n_row:]):
                ref[...] += val

    return pl.pallas_call(
        body,
        grid=(steps,),
        in_specs=in_specs,
        out_specs=out_specs,
        out_shape=out_shape,
        compiler_params=_params(("arbitrary",), need),
        name=name,
    )(*arrays)


def _ln_stats(z):
    mu = _mean(z)
    zc = z - mu
    rstd = lax.rsqrt(_mean(zc * zc) + LN_EPS)
    return zc * rstd, rstd


def _ln_bwd(dy, xhat, rstd, g):
    dxh = dy * g
    return rstd * (dxh - _mean(dxh) - xhat * _mean(dxh * xhat))


def _ride(ex, xrefs, step, steps):
    if not ex:
        return
    for at, act in ((0, ex.start), (steps - 2, ex.relay), (steps - 1, ex.finish)):
        @pl.when(step == at)
        def _(act=act):
            act(*xrefs)


def _sgu_fwd(h, wm, bs_t, g_v, b_v, ex=None):
    t = h.shape[1]
    steps = t // SGU_BLOCK

    def body(*refs):
        (u_ref, v_ref, wm_ref, bs_ref, g_ref, b_ref), (y_ref,), _, xrefs = _split_refs(refs, 6, 1, 0, ex)
        xhat, _ = _ln_stats(_gelu(v_ref[...].astype(F32)))
        vn = (xhat * g_ref[...] + b_ref[...]).astype(BF16)
        gu = _gelu(u_ref[...].astype(F32))
        for g in range(GROUPS):
            sl = slice(g * HEAD, (g + 1) * HEAD)
            mixed = _dot(wm_ref[g], vn[:, sl]) + bs_ref[:, g:g + 1]
            y_ref[:, sl] = (gu[:, sl] * mixed).astype(BF16)
        _ride(ex, xrefs, pl.program_id(0), steps)

    blk = lambda b: pl.BlockSpec((None, SGU_BLOCK, D_MODEL), functools.partial(lambda i, b: (b, i, 0), b=b))
    whole = lambda s: pl.BlockSpec(s, lambda i: (0,) * len(s))
    res = pl.pallas_call(
        body,
        grid=(steps,),
        in_specs=[blk(SLOT_U), blk(SLOT_V), whole(wm.shape), whole(bs_t.shape), whole(g_v.shape), whole(b_v.shape)]
        + (ex.in_specs if ex else []),
        out_specs=[pl.BlockSpec((SGU_BLOCK, D_MODEL), lambda i: (i, 0))] + (ex.out_specs if ex else []),
        out_shape=[jax.ShapeDtypeStruct((t, D_MODEL), BF16)] + (ex.out_shape if ex else []),
        scratch_shapes=ex.scratch if ex else [],
        compiler_params=_params(("arbitrary",)),
        name="sgu_fwd",
    )(h, h, wm, bs_t, g_v, b_v, *(ex.arrays if ex else []))
    return res[0], res[1:]


def _sgu_bwd(h, dy, dh, wm, wm_t, bs_t, g_v, b_v):
    t = h.shape[1]

    def body(u_ref, v_ref, dy_ref, dh_in, wm_ref, wmt_ref, bs_ref, g_ref, b_ref,
             duv_ref, dw_ref, dbs_ref, dg_ref, db_ref, dvn_ref):
        del dh_in
        du_ref, dv_ref = duv_ref.at[0], duv_ref.at[1]
        i = pl.program_id(0)

        @pl.when(i == 0)
        def _():
            dw_ref[...] = jnp.zeros_like(dw_ref)
            dbs_ref[...] = jnp.zeros_like(dbs_ref)
            dg_ref[...] = jnp.zeros_like(dg_ref)
            db_ref[...] = jnp.zeros_like(db_ref)

        u = u_ref[...].astype(F32)
        v = v_ref[...].astype(F32)
        gv, gvp = _gelu_and_grad(v)
        xhat, rstd = _ln_stats(gv)
        vn = (xhat * g_ref[...] + b_ref[...]).astype(BF16)
        gu, gup = _gelu_and_grad(u)
        lane = lax.broadcasted_iota(jnp.int32, (SGU_BLOCK, LANE), 1)
        dbs = jnp.zeros((SGU_BLOCK, LANE), F32)
        for g in range(GROUPS):
            sl = slice(g * HEAD, (g + 1) * HEAD)
            vn_g = vn[:, sl]
            mixed = _dot(wm_ref[g], vn_g) + bs_ref[:, g:g + 1]
            dy_g = dy_ref[:, sl]
            du_ref[:, sl] = (dy_g * mixed * gup[:, sl]).astype(BF16)
            dmix = dy_g * gu[:, sl]
            dmb = dmix.astype(BF16)
            dvn_ref[:, sl] = _dot(wmt_ref[g], dmb)
            dw_ref[g] += _dot_nt(dmb, vn_g)
            dbs = dbs + jnp.where(lane == g, jnp.sum(dmix, axis=1, keepdims=True), 0.0)
        dbs_ref[...] += dbs
        dvn = dvn_ref[...]
        dg_ref[...] += _sum0(dvn * xhat)
        db_ref[...] += _sum0(dvn)
        dv_ref[...] = (_ln_bwd(dvn, xhat, rstd, g_ref[...]) * gvp).astype(BF16)

    blk = lambda b: pl.BlockSpec((None, SGU_BLOCK, D_MODEL), functools.partial(lambda i, b: (b, i, 0), b=b))
    row = pl.BlockSpec((SGU_BLOCK, D_MODEL), lambda i: (i, 0))
    whole = lambda s: pl.BlockSpec(s, lambda i: (0,) * len(s))
    vec = (1, D_MODEL)
    return pl.pallas_call(
        body,
        grid=(t // SGU_BLOCK,),
        in_specs=[blk(SLOT_U), blk(SLOT_V), row, pl.BlockSpec(memory_space=pl.ANY),
                  whole(wm.shape), whole(wm_t.shape), whole(bs_t.shape), whole(vec), whole(vec)],
        out_specs=[pl.BlockSpec((2, SGU_BLOCK, D_MODEL), lambda i: (SLOT_U // 2, i, 0)),
                   whole(wm.shape), whole((SGU_BLOCK, LANE)), whole(vec), whole(vec)],
        out_shape=[jax.ShapeDtypeStruct(dh.shape, BF16),
                   jax.ShapeDtypeStruct(wm.shape, F32), jax.ShapeDtypeStruct((SGU_BLOCK, LANE), F32),
                   jax.ShapeDtypeStruct(vec, F32), jax.ShapeDtypeStruct(vec, F32)],
        scratch_shapes=[pltpu.VMEM((SGU_BLOCK, D_MODEL), F32)],
        input_output_aliases={3: 0},
        compiler_params=_params(("arbitrary",)),
        name="sgu_bwd",
    )(h, h, dy, dh, wm, wm_t, bs_t, g_v, b_v)


def _split3(x):
    hi = x.astype(BF16)
    r1 = x - hi.astype(F32)
    mid = r1.astype(BF16)
    lo = (r1 - mid.astype(F32)).astype(BF16)
    return hi, mid, lo


def _tri_matmul(tri, x):
    hi, mid, lo = _split3(x)
    dot = lambda p: jnp.dot(tri, p, preferred_element_type=F32)
    return dot(hi) + dot(mid) + dot(lo)


def _lower_bound(logits):
    l0, l1 = logits[0:1, :], logits[1:2, :]
    mx = jnp.maximum(l0, l1)
    e0, e1 = jnp.exp(l0 - mx), jnp.exp(l1 - mx)
    return e0 / (e0 + e1)


def _hgrn_gates(q_raw, f_raw, lb):
    q = q_raw * jax.nn.sigmoid(q_raw)
    sig = jax.nn.sigmoid(f_raw)
    f = lb + (1.0 - lb) * sig
    row = lax.broadcasted_iota(jnp.int32, (CHUNK, CHUNK), 0)
    col = lax.broadcasted_iota(jnp.int32, (CHUNK, CHUNK), 1)
    c = _tri_matmul((row >= col).astype(BF16), jnp.log(f))
    return q, sig, f, 1.0 - f, c


def _offdiag_terms(qh, kh, ch, tb):
    rows = slice(tb * SUB, (tb + 1) * SUB)
    r = ch[tb * SUB - 1:tb * SUB, :]
    eqh = jnp.exp(ch[rows] - r)
    ekh = jnp.exp(jnp.minimum(r - ch, 0.0))
    return rows, eqh, qh[rows] * eqh, ekh, kh * ekh


def _diag_decay(cb, s, trow):
    return jnp.exp(jnp.where(trow >= s, cb - cb[s:s + 1, :], NEG))


def _split_refs(refs, n_in, n_out, n_scratch, ex):
    nx = ex.n if ex else 0
    ins, refs = refs[:n_in], refs[n_in:]
    xsrc, refs = refs[:nx], refs[nx:]
    outs, refs = refs[:n_out], refs[n_out:]
    xout, refs = refs[:nx], refs[nx:]
    return ins, outs, refs[:n_scratch], (xsrc, xout, refs[n_scratch:])


def _hgrn_fwd(h, logits, g_norm, ex=None):
    t = h.shape[1]
    nc = t // CHUNK
    per_step = CHUNKS_PER_STEP if nc % CHUNKS_PER_STEP == 0 else 1
    steps = nc // per_step

    def body(*refs):
        ins, outs, scratch, xrefs = _split_refs(refs, 3, 3, 4, ex)
        hgrn_ref, lg_ref, gn_ref = ins
        y_ref, o_ref, sall_ref = outs
        st_ref, q_s, k_s, c_s = scratch

        @pl.when(pl.program_id(0) == 0)
        def _():
            st_ref[...] = jnp.zeros_like(st_ref)

        lb = _lower_bound(lg_ref[...])
        for sub in range(per_step):
            rows = pl.ds(sub * CHUNK, CHUNK)
            chunk(lb, sall_ref.at[sub], *(r.at[rows] for r in (hgrn_ref.at[SLOT_Q], hgrn_ref.at[SLOT_F], hgrn_ref.at[SLOT_I],
                                                               hgrn_ref.at[SLOT_OG], y_ref, o_ref)),
                  gn_ref, st_ref, q_s.at[sub], k_s.at[sub], c_s.at[sub])
        _ride(ex, xrefs, pl.program_id(0), steps)

    def chunk(lb, sall_ref, q_ref, f_ref, i_ref, og_ref, y_ref, o_ref, gn_ref, st_ref, q_s, k_s, c_s):
        q, _, _, k, c = _hgrn_gates(q_ref[...].astype(F32), f_ref[...].astype(F32), lb)
        q_s[...] = q
        k_s[...] = k
        c_s[...] = c
        col64 = lax.broadcasted_iota(jnp.int32, (SUB, CHUNK), 1)
        trow = lax.broadcasted_iota(jnp.int32, (SUB, HEAD), 0)

        def head(hd, carry):
            sl = pl.ds(pl.multiple_of(hd * HEAD, HEAD), HEAD)
            qh, kh, ch, ih = q_s[:, sl], k_s[:, sl], c_s[:, sl], i_ref[:, sl].astype(F32)
            st = st_ref[hd]
            sall_ref[hd] = st
            c_last = ch[CHUNK - 1:CHUNK, :]
            o = _dot_nt(qh * jnp.exp(ch), st)
            st_ref[hd] = st * jnp.exp(c_last) + _dot_tn(ih, kh * jnp.exp(c_last - ch))
            a_rows = [jnp.zeros((SUB, CHUNK), F32)]
            for tb in range(1, CHUNK // SUB):
                _, _, q_hat, _, k_hat = _offdiag_terms(qh, kh, ch, tb)
                a_rows.append(jnp.where(col64 < tb * SUB, _dot_nt(q_hat, k_hat), 0.0))
            o = o + _dot(jnp.concatenate(a_rows, axis=0), ih)
            o_rows = []
            for b in range(CHUNK // SUB):
                rows = slice(b * SUB, (b + 1) * SUB)
                qb, cb, kb, ib = qh[rows], ch[rows], kh[rows], ih[rows]
                ob = jnp.zeros((SUB, HEAD), F32)
                for s in range(SUB):
                    a = jnp.sum(qb * _diag_decay(cb, s, trow) * kb[s:s + 1, :], axis=1, keepdims=True)
                    ob = ob + a * ib[s:s + 1, :]
                o_rows.append(ob)
            o = o + jnp.concatenate(o_rows, axis=0)
            o_ref[:, sl] = o
            og = og_ref[:, sl].astype(F32)
            on = o * lax.rsqrt(_mean(o * o) + RMS_EPS)
            y_ref[:, sl] = (on * gn_ref[:, sl] * (og * jax.nn.sigmoid(og))).astype(BF16)
            return carry

        lax.fori_loop(0, HEADS, head, 0, unroll=HEAD_UNROLL)

    rows = per_step * CHUNK
    row = pl.BlockSpec((rows, D_MODEL), lambda n: (n, 0))
    whole = lambda s: pl.BlockSpec(s, lambda n: (0,) * len(s))
    res = pl.pallas_call(
        body,
        grid=(steps,),
        in_specs=[pl.BlockSpec((4, rows, D_MODEL), lambda n: (SLOT_Q // 4, n, 0)), whole(logits.shape), whole(g_norm.shape)]
        + (ex.in_specs if ex else []),
        out_specs=[row, row, pl.BlockSpec((per_step, HEADS, HEAD, HEAD), lambda n: (n, 0, 0, 0))]
        + (ex.out_specs if ex else []),
        out_shape=[jax.ShapeDtypeStruct((t, D_MODEL), BF16), jax.ShapeDtypeStruct((t, D_MODEL), F32),
                   jax.ShapeDtypeStruct((nc, HEADS, HEAD, HEAD), F32)] + (ex.out_shape if ex else []),
        scratch_shapes=[pltpu.VMEM((HEADS, HEAD, HEAD), F32)] + [pltpu.VMEM((per_step, CHUNK, D_MODEL), F32)] * 3
        + (ex.scratch if ex else []),
        compiler_params=_params(("arbitrary",)),
        name="hgrn_fwd",
    )(h, logits, g_norm, *(ex.arrays if ex else []))
    return res[0], res[1], res[2], res[3:]


def _hgrn_bwd(h, o_all, dy, states, dh, logits, g_norm, ex=None):
    t = h.shape[1]
    nc = t // CHUNK

    def body(*refs):
        ins, outs, scratch, xrefs = _split_refs(refs, 7, 3, 8, ex)
        hgrn_ref, o_ref, dy_ref, sall_ref, _, lg_ref, gn_ref = ins
        q_ref, f_ref, i_ref, og_ref = (hgrn_ref.at[s] for s in (SLOT_Q, SLOT_F, SLOT_I, SLOT_OG))
        dqfio_ref, dlg_ref, dgn_ref = outs
        dst_ref, dlb_ref, q_s, k_s, c_s, dq_s, dk_s, dc_s = scratch
        dq_ref, df_ref, di_ref, dog_ref = (dqfio_ref.at[s] for s in (SLOT_Q, SLOT_F, SLOT_I, SLOT_OG))
        n = pl.program_id(0)

        @pl.when(n == 0)
        def _():
            dst_ref[...] = jnp.zeros_like(dst_ref)
            dlb_ref[...] = jnp.zeros_like(dlb_ref)
            dgn_ref[...] = jnp.zeros_like(dgn_ref)
            if ex:
                ex.start(*xrefs)

        lb = _lower_bound(lg_ref[...])
        q_raw = q_ref[...].astype(F32)
        q, sig, f, k, c = _hgrn_gates(q_raw, f_ref[...].astype(F32), lb)
        q_s[...] = q
        k_s[...] = k
        c_s[...] = c
        col64 = lax.broadcasted_iota(jnp.int32, (SUB, CHUNK), 1)
        trow = lax.broadcasted_iota(jnp.int32, (SUB, HEAD), 0)
        row64 = lax.broadcasted_iota(jnp.int32, (CHUNK, HEAD), 0)

        def head(hd, carry):
            sl = pl.ds(pl.multiple_of(hd * HEAD, HEAD), HEAD)
            qh, kh, ch, ih = q_s[:, sl], k_s[:, sl], c_s[:, sl], i_ref[:, sl].astype(F32)
            st = sall_ref[hd]
            dst = dst_ref[hd]
            oh, dyh, og, gn = o_ref[:, sl], dy_ref[:, sl], og_ref[:, sl].astype(F32), gn_ref[:, sl]
            sg = jax.nn.sigmoid(og)
            sil = og * sg
            rms = lax.rsqrt(_mean(oh * oh) + RMS_EPS)
            on = oh * rms
            dog_ref[:, sl] = (dyh * on * gn * _silu_grad(og, sg)).astype(BF16)
            dgn_ref[:, sl] += _sum0(dyh * on * sil)
            don = dyh * gn * sil
            do = rms * (don - on * _mean(don * on))
            dob = do.astype(BF16)

            c_last = ch[CHUNK - 1:CHUNK, :]
            eq = jnp.exp(ch)
            q_til = qh * eq
            ekl = jnp.exp(c_last - ch)
            k_til = kh * ekl
            ecl = jnp.exp(c_last)
            dq_til = _dot(dob, st)
            dk_til = _dot(ih, dst)
            di = _dot_nt(k_til, dst)
            dc_last = _sum0(dk_til * k_til) + _sum0(dst * st) * ecl
            dst_ref[hd] = _dot_tn(dob, q_til) + dst * ecl
            dq = dq_til * eq
            dc = dq_til * q_til - dk_til * k_til
            dk = dk_til * ekl

            da_full = _dot_nt(dob, ih)
            a_rows = [jnp.zeros((SUB, CHUNK), F32)]
            dq_rows = [jnp.zeros((SUB, HEAD), F32)]
            dc_rows = [jnp.zeros((SUB, HEAD), F32)]
            for tb in range(1, CHUNK // SUB):
                rows, eqh, q_hat, ekh, k_hat = _offdiag_terms(qh, kh, ch, tb)
                keep = col64 < tb * SUB
                a_rows.append(jnp.where(keep, _dot_nt(q_hat, k_hat), 0.0))
                da = jnp.where(keep, da_full[rows], 0.0)
                dq_hat = _dot(da, k_hat)
                dk_hat = _dot_tn(da, q_hat)
                dq_rows.append(dq_hat * eqh)
                dc_rows.append(dq_hat * q_hat)
                dk = dk + dk_hat * ekh
                dc = dc - dk_hat * k_hat
            di = di + _dot_tn(jnp.concatenate(a_rows, axis=0), dob)

            dk_rows, di_rows = [], []
            for b in range(CHUNK // SUB):
                rows = slice(b * SUB, (b + 1) * SUB)
                qb, cb, kb, ib, dob_ = qh[rows], ch[rows], kh[rows], ih[rows], do[rows]
                dq_diag = jnp.zeros((SUB, HEAD), F32)
                dk_diag = jnp.zeros((SUB, HEAD), F32)
                di_diag = jnp.zeros((SUB, HEAD), F32)
                for s in range(SUB):
                    ks = kb[s:s + 1, :]
                    dec = _diag_decay(cb, s, trow)
                    a = jnp.sum(qb * dec * ks, axis=1, keepdims=True)
                    gk = jnp.sum(dob_ * ib[s:s + 1, :], axis=1, keepdims=True) * dec
                    dq_diag = dq_diag + gk * ks
                    dk_diag = dk_diag + jnp.where(trow == s, _sum0(gk * qb), 0.0)
                    di_diag = di_diag + jnp.where(trow == s, _sum0(a * dob_), 0.0)
                dq_rows[b] = dq_rows[b] + dq_diag
                dc_rows[b] = dc_rows[b] + qb * dq_diag - kb * dk_diag
                dk_rows.append(dk_diag)
                di_rows.append(di_diag)
            dq = dq + jnp.concatenate(dq_rows, axis=0)
            dk = dk + jnp.concatenate(dk_rows, axis=0)
            dc = dc + jnp.concatenate(dc_rows, axis=0) + jnp.where(row64 == CHUNK - 1, dc_last, 0.0)
            di_ref[:, sl] = (di + jnp.concatenate(di_rows, axis=0)).astype(BF16)
            dq_s[:, sl] = dq
            dk_s[:, sl] = dk
            dc_s[:, sl] = dc
            return carry

        lax.fori_loop(0, HEADS, head, 0, unroll=HEAD_UNROLL)

        row = lax.broadcasted_iota(jnp.int32, (CHUNK, CHUNK), 0)
        col = lax.broadcasted_iota(jnp.int32, (CHUNK, CHUNK), 1)
        dlf = _tri_matmul((row <= col).astype(BF16), dc_s[...])
        df = dlf / f - dk_s[...]
        dlb_ref[...] += _sum0(df * (1.0 - sig))
        df_ref[...] = (df * (1.0 - lb) * sig * (1.0 - sig)).astype(BF16)
        dq_ref[...] = (dq_s[...] * _silu_grad(q_raw, jax.nn.sigmoid(q_raw))).astype(BF16)

        @pl.when(n == nc - 1)
        def _():
            d0 = dlb_ref[...] * lb * (1.0 - lb)
            dlg_ref[0:1, :] = d0
            dlg_ref[1:2, :] = -d0
            if ex:
                ex.finish(*xrefs)

    rev = lambda n: nc - 1 - n
    slots = pl.BlockSpec((4, CHUNK, D_MODEL), lambda n: (SLOT_Q // 4, rev(n), 0))
    row = pl.BlockSpec((CHUNK, D_MODEL), lambda n: (rev(n), 0))
    whole = lambda s: pl.BlockSpec(s, lambda n: (0,) * len(s))
    vec = (1, D_MODEL)
    res = pl.pallas_call(
        body,
        grid=(nc,),
        in_specs=[slots, row, row, pl.BlockSpec((None, HEADS, HEAD, HEAD), lambda n: (rev(n), 0, 0, 0)),
                  pl.BlockSpec(memory_space=pl.ANY), whole(logits.shape), whole(vec)] + (ex.in_specs if ex else []),
        out_specs=[slots, whole((2, D_MODEL)), whole(vec)] + (ex.out_specs if ex else []),
        out_shape=[jax.ShapeDtypeStruct(dh.shape, BF16), jax.ShapeDtypeStruct((2, D_MODEL), F32),
                   jax.ShapeDtypeStruct(vec, F32)] + (ex.out_shape if ex else []),
        scratch_shapes=[pltpu.VMEM((HEADS, HEAD, HEAD), F32), pltpu.VMEM(vec, F32)]
        + [pltpu.VMEM((CHUNK, D_MODEL), F32)] * 6 + (ex.scratch if ex else []),
        input_output_aliases={4: 0},
        compiler_params=_params(("arbitrary",)),
        name="hgrn_bwd",
    )(h, o_all, dy, states, dh, logits, g_norm, *(ex.arrays if ex else []))
    return res[0], res[1], res[2], res[3:]


def _merge_fwd(i, n, ga, gb, za, zb):
    return jax.nn.sigmoid(ga.astype(F32)) * za + jax.nn.sigmoid(gb.astype(F32)) * zb


def _merge_bwd(i, n, ga, gb, za, zb, dm):
    sa, sb = jax.nn.sigmoid(ga.astype(F32)), jax.nn.sigmoid(gb.astype(F32))
    dgates = jnp.stack([(dm * za * sa * (1.0 - sa)).astype(BF16), (dm * zb * sb * (1.0 - sb)).astype(BF16)])
    return dgates, dm * sa, dm * sb


def _ln1_fwd(i, n, x, r1, g, b):
    xhat, _ = _ln_stats(ALPHA * x + r1)
    x1 = xhat * g + b
    return x1, x1


def _ln1_bwd(i, n, x, r1, dx1, g):
    xhat, rstd = _ln_stats(ALPHA * x + r1)
    dz = _ln_bwd(dx1, xhat, rstd, g)
    return dz, dz, _sum0(dx1 * xhat), _sum0(dx1)


def _ln2_loss(i, n, x1, fo, pg, pp, tgt, g, b):
    sg = jax.nn.sigmoid(pg)
    xhat, rstd = _ln_stats(ALPHA * x1 + fo + sg * pp)
    diff = xhat * g + b - tgt
    loss = 0.5 * jnp.sum(_mean(diff * diff), axis=0, keepdims=True)
    dy = diff * (1.0 / D_MODEL)
    dz = _ln_bwd(dy, xhat, rstd, g)
    return (dz, dz, dz * pp * sg * (1.0 - sg), dz * sg,
            jnp.broadcast_to(loss, (8, LANE)), _sum0(dy * xhat), _sum0(dy))


def _shift_down(cur, halo, tile):
    row = lax.broadcasted_iota(jnp.int32, cur.shape, 0)
    m1 = jnp.where(row == 0, halo[7:8, :], pltpu.roll(cur, 1, 0))
    m2 = jnp.where(row == 0, halo[6:7, :], jnp.where(row == 1, halo[7:8, :], pltpu.roll(cur, 2, 0)))
    return m1, m2


def _shift_up(cur, halo, tile):
    row = lax.broadcasted_iota(jnp.int32, cur.shape, 0)
    p1 = jnp.where(row == tile - 1, halo[0:1, :], pltpu.roll(cur, tile - 1, 0))
    p2 = jnp.where(row == tile - 2, halo[0:1, :], jnp.where(row == tile - 1, halo[1:2, :], pltpu.roll(cur, tile - 2, 0)))
    return p1, p2


def _conv_pre(i, gate, halo, w, b, tile):
    halo = jnp.where(i == 0, 0.0, halo)
    m1, m2 = _shift_down(gate, halo, tile)
    return w[0:1, :] * m2 + w[1:2, :] * m1 + w[2:3, :] * gate + b, m1, m2


def _conv_fwd(tile, i, n, gate, halo, val, w, b):
    cg, _, _ = _conv_pre(i, gate, halo, w, b, tile)
    return _gelu(cg) * val


def _conv_bwd_a(tile, i, n, gate, halo, val, dhid, w, b):
    cg, m1, m2 = _conv_pre(i, gate, halo, w, b, tile)
    act, slope = _gelu_and_grad(cg)
    dcg = dhid * val * slope
    return dcg, dhid * act, _sum0(dcg * m2), _sum0(dcg * m1), _sum0(dcg * gate), _sum0(dcg)


def _conv_bwd_b(tile, i, n, dcg, halo, w):
    dcg = dcg.astype(F32)
    halo = jnp.where(i == n - 1, 0.0, halo.astype(F32))
    p1, p2 = _shift_up(dcg, halo, tile)
    return w[2:3, :] * dcg + w[1:2, :] * p1 + w[0:1, :] * p2


def _halo_spec(width, tile, t, nxt, rows=8):
    per = tile // rows
    last = t // rows - 1
    if nxt:
        return pl.BlockSpec((rows, width), lambda i: (jnp.minimum((i + 1) * per, last), 0))
    return pl.BlockSpec((rows, width), lambda i: (jnp.maximum(i * per - 1, 0), 0))


def _adamw(i, n, w, m, v, parts):
    g = parts[0].astype(F32)
    for j in range(1, parts.shape[0]):
        g = g + parts[j].astype(F32)
    m_new = ADAM_B1 * m + (1.0 - ADAM_B1) * g
    v_new = ADAM_B2 * v + (1.0 - ADAM_B2) * (g * g)
    m_hat = m_new / (1.0 - ADAM_B1 ** ADAM_STEP)
    v_hat = v_new / (1.0 - ADAM_B2 ** ADAM_STEP)
    delta = -ADAM_LR * (m_hat / (jnp.sqrt(v_hat) + ADAM_EPS) + ADAM_WD * w)
    return g, delta, m_new, v_new


def _adam_call(w, m, v, parts, name):
    r, c = w.shape
    tile = _pick(r, (256, 128)) if r > 256 else r
    spec = pl.BlockSpec((parts.shape[0], tile, c), lambda i: (0, i, 0))
    return _rowwise(_adamw, [w, m, v, (parts, spec)], [], [(c, F32)] * 4, [], tile=tile, name=name)


def _peer(k):
    x, y, c = lax.axis_index("x"), lax.axis_index("y"), lax.axis_index("c")
    px = x ^ ((k >> 2) & 1)
    py = y ^ ((k >> 1) & 1)
    pc = c ^ (k & 1)
    return (px, py, pc), 4 * px + 2 * py + pc


def _my_index():
    return 4 * lax.axis_index("x") + 2 * lax.axis_index("y") + lax.axis_index("c")


class _Exchange:
    KINDS = ("gather", "gather+relay", "scatter", "scatter+pairs")

    def __init__(self, entries):
        assert all(k in self.KINDS for _, k in entries), [k for _, k in entries]
        self.arrays = [a for a, _ in entries]
        self.scatter = [k.startswith("scatter") for _, k in entries]
        self.relayed = ["+relay" in k for _, k in entries]
        self.pairs = ["+pairs" in k for _, k in entries]
        self.n = len(entries)
        self.in_specs = [pl.BlockSpec(memory_space=pl.ANY)] * self.n
        self.out_specs = [pl.BlockSpec(memory_space=pl.ANY)] * self.n
        shapes = [tuple(a.shape[1:]) if sc else tuple(a.shape) for a, sc in zip(self.arrays, self.scatter)]
        counts = [N_CHIP if p else N_DEV for p in self.pairs]
        self.out_shape = [jax.ShapeDtypeStruct((n,) + s, a.dtype) for n, s, a in zip(counts, shapes, self.arrays)]
        per = N_DEV - 1
        self.scratch = [pltpu.SemaphoreType.DMA((self.n * per,)), pltpu.SemaphoreType.DMA((self.n * per,)),
                        pltpu.SemaphoreType.DMA((self.n,))]

    def _copies(self, srcs, outs, sems):
        send_sems, recv_sems, local_sems = sems
        x, y, c = lax.axis_index("x"), lax.axis_index("y"), lax.axis_index("c")
        me = _my_index()
        per = N_DEV - 1
        local, first, passed, relay_arrivals, arrivals = [], [], [], [], []
        for a in range(self.n):

            def copy(k, src, dst, dev, a=a):
                return pltpu.make_async_remote_copy(
                    src_ref=src, dst_ref=dst, send_sem=send_sems.at[a * per + k], recv_sem=recv_sems.at[a * per + k],
                    device_id=dev, device_id_type=pl.DeviceIdType.MESH)

            if self.pairs[a]:
                chip = 2 * x + y
                for k in range(N_CHIP):
                    to = chip ^ k
                    piece = srcs[a].at[_slot_of_group(2 * to + c) // 2]
                    if k == 0:
                        local.append(pltpu.make_async_copy(piece, outs[a].at[chip], local_sems.at[a]))
                    else:
                        dev = (to // 2, to % 2, c)
                        first.append(copy(k - 1, piece, outs[a].at[chip], dev))
                        arrivals.append(copy(k - 1, piece, outs[a].at[to], dev))
                continue
            mine = srcs[a].at[me] if self.scatter[a] else srcs[a]
            land = outs[a].at[me]
            local.append(pltpu.make_async_copy(mine, land, local_sems.at[a]))
            if self.relayed[a]:
                block = lambda px, py, pc, a=a: outs[a].at[4 * px + 2 * py + pc]
                chips = [(1 - x, y), (x, 1 - y), (1 - x, 1 - y)]
                first.append(copy(0, mine, land, (x, y, 1 - c)))
                arrivals.append(copy(0, mine, block(x, y, 1 - c), (x, y, 1 - c)))
                for j, (px, py) in enumerate(chips):
                    first.append(copy(1 + j, mine, land, (px, py, c)))
                    relay_arrivals.append(copy(1 + j, mine, block(px, py, c), (px, py, c)))
                    passed.append(copy(4 + j, block(px, py, c), block(px, py, c), (x, y, 1 - c)))
                    arrivals.append(copy(4 + j, mine, block(px, py, 1 - c), (x, y, 1 - c)))
                continue
            for k in range(1, N_DEV):
                dev, idx = _peer(k)
                if self.scatter[a]:
                    first.append(copy(k - 1, srcs[a].at[idx], land, dev))
                else:
                    first.append(copy(k - 1, mine, land, dev))
                arrivals.append(copy(k - 1, mine, outs[a].at[idx], dev))
        return local, first, passed, relay_arrivals, arrivals

    def start(self, srcs, outs, sems):
        local, first, _, _, _ = self._copies(srcs, outs, sems)
        for cp in local + first:
            cp.start()

    def relay(self, srcs, outs, sems):
        _, _, passed, relay_arrivals, _ = self._copies(srcs, outs, sems)
        for landed, onward in zip(relay_arrivals, passed):
            landed.wait_recv()
            onward.start()

    def finish(self, srcs, outs, sems):
        local, first, passed, _, arrivals = self._copies(srcs, outs, sems)
        for cp in arrivals:
            cp.wait_recv()
        for cp in first + passed:
            cp.wait_send()
        for cp in local:
            cp.wait()


def _gather_project(x_b, shard):
    t, d = x_b.shape
    tm = _pick(t, MM_TILES)
    nrow = t // tm
    per = N_DEV - 1

    def parties():
        x, y, c = lax.axis_index("x"), lax.axis_index("y"), lax.axis_index("c")
        first = (jnp.where(c == 0, 1 - x, x), jnp.where(c == 0, y, 1 - y))
        second = (jnp.where(c == 0, x, 1 - x), jnp.where(c == 0, 1 - y, y))
        diagonal = (1 - x, 1 - y)
        return ((x, y, c), (x, y, 1 - c), [(px, py, c) for px, py in (first, second, diagonal)],
                [(px, py, 1 - c) for px, py in (second, first, diagonal)])

    slot = lambda dev: _slot_of_group(4 * dev[0] + 2 * dev[1] + dev[2])
    ici_step, passed_step = (2, 3, 6), (4, 5, 7)
    me, sibling, over_ici, passed_on = parties()
    by_step = {0: me, 1: sibling, **dict(zip(ici_step, over_ici)), **dict(zip(passed_step, passed_on))}
    order = jnp.stack([slot(by_step[j]) for j in range(N_DEV)]).astype(jnp.int32)

    def body(order_ref, x_ref, shard_ref, h_ref, wall_ref, wbuf, fetch_sems, send_sems, recv_sems, local_sem):
        del order_ref
        me, sibling, over_ici, passed_on = parties()
        j, i = pl.program_id(0), pl.program_id(1)
        land = lambda dev: wall_ref.at[slot(dev)]

        def copy(k, src, block, to):
            return pltpu.make_async_remote_copy(src_ref=src, dst_ref=land(block), send_sem=send_sems.at[k],
                                                recv_sem=recv_sems.at[k], device_id=to, device_id_type=pl.DeviceIdType.MESH)

        def fetch(step):
            src = shard_ref if step == 0 else land(by_step[step])
            return pltpu.make_async_copy(src, wbuf.at[step % 2], fetch_sems.at[step % 2])

        by_step = {0: me, 1: sibling, **dict(zip(ici_step, over_ici)), **dict(zip(passed_step, passed_on))}
        keep = pltpu.make_async_copy(shard_ref, land(me), local_sem)
        first = [copy(0, shard_ref, me, sibling)] + [copy(1 + n, shard_ref, me, dev) for n, dev in enumerate(over_ici[:2])]
        relay = copy(3, land(over_ici[0]), over_ici[0], over_ici[1])
        onward = [copy(4 + n, land(dev), dev, sibling) for n, dev in enumerate(over_ici)]

        @pl.when(jnp.logical_and(i == 0, j == 0))
        def _():
            for cp in (first[1], first[0]):
                cp.start()
            keep.start()
            fetch(0).start()
            fetch(0).wait()

        @pl.when(jnp.logical_and(i == 0, j == 1))
        def _():
            first[2].start()

        for step in range(1, N_DEV):
            @pl.when(jnp.logical_and(i == nrow - 1, j == step - 1))
            def _(step=step):
                if step == 1:
                    copy(0, shard_ref, sibling, me).wait_recv()
                for n, dev in enumerate(over_ici):
                    if step == ici_step[n]:
                        copy(1 + n, shard_ref, dev, me).wait_recv()
                        if n == 0:
                            relay.start()
                        onward[n].start()
                for n, dev in enumerate(passed_on):
                    if step == passed_step[n]:
                        copy(4 + n, shard_ref, dev, me).wait_recv()
                fetch(step).start()

            @pl.when(jnp.logical_and(i == 0, j == step))
            def _(step=step):
                fetch(step).wait()

        h_ref[...] = _dot(x_ref[...], wbuf[j % 2]).astype(BF16)

        @pl.when(jnp.logical_and(i == nrow - 1, j == N_DEV - 1))
        def _():
            for cp in first + [relay] + onward:
                cp.wait_send()
            keep.wait()

    assert nrow >= 2, "a group is fetched during the last row step of the group before it"
    need = 2 * (tm * d * 2 + tm * d * 4) + 2 * d * d * 2 + tm * d * 4
    h, w_all = pl.pallas_call(
        body,
        grid_spec=pltpu.PrefetchScalarGridSpec(
            num_scalar_prefetch=1,
            grid=(N_DEV, nrow),
            in_specs=[pl.BlockSpec((tm, d), lambda j, i, order: (i, 0)), pl.BlockSpec(memory_space=pl.ANY)],
            out_specs=[pl.BlockSpec((None, tm, d), lambda j, i, order: (order[j], i, 0)), pl.BlockSpec(memory_space=pl.ANY)],
            scratch_shapes=[pltpu.VMEM((2, d, d), BF16), pltpu.SemaphoreType.DMA((2,)), pltpu.SemaphoreType.DMA((per,)),
                            pltpu.SemaphoreType.DMA((per,)), pltpu.SemaphoreType.DMA],
        ),
        out_shape=[jax.ShapeDtypeStruct((N_DEV, t, d), BF16), jax.ShapeDtypeStruct((N_DEV, d, d), BF16)],
        compiler_params=_params(("arbitrary", "arbitrary"), need),
        name="gather_project",
    )(order, x_b, shard)
    return h, w_all


def _pair_sums(g):
    n, r, cols = g.shape
    half = n // 2

    def swap(g_ref, got_ref, send_sems, recv_sems):
        x, y, c = lax.axis_index("x"), lax.axis_index("y"), lax.axis_index("c")
        copies = [pltpu.make_async_remote_copy(
            src_ref=g_ref.at[2 * j + 1 - c], dst_ref=got_ref.at[j], send_sem=send_sems.at[j], recv_sem=recv_sems.at[j],
            device_id=(x, y, 1 - c), device_id_type=pl.DeviceIdType.MESH) for j in range(half)]
        for cp in copies:
            cp.start()
        for cp in copies:
            cp.wait()

    got = pl.pallas_call(
        swap,
        in_specs=[pl.BlockSpec(memory_space=pl.ANY)],
        out_specs=pl.BlockSpec(memory_space=pl.ANY),
        out_shape=jax.ShapeDtypeStruct((half, r, cols), g.dtype),
        scratch_shapes=[pltpu.SemaphoreType.DMA((half,))] * 2,
        name="pair_swap",
    )(g)

    def add(mine_ref, got_ref, out_ref):
        mine = jnp.where(lax.axis_index("c") == 0, mine_ref[0].astype(F32), mine_ref[1].astype(F32))
        out_ref[...] = (mine + got_ref[...].astype(F32)).astype(out_ref.dtype)

    tile = _pick(r, (512, 256, 128))
    return pl.pallas_call(
        add,
        grid=(half, r // tile),
        in_specs=[pl.BlockSpec((None, 2, tile, cols), lambda j, i: (j, 0, i, 0)),
                  pl.BlockSpec((None, tile, cols), lambda j, i: (j, i, 0))],
        out_specs=pl.BlockSpec((None, tile, cols), lambda j, i: (j, i, 0)),
        out_shape=jax.ShapeDtypeStruct((half, r, cols), g.dtype),
        compiler_params=_params(("parallel", "parallel")),
        name="pair_add",
    )(g.reshape(half, 2, r, cols), got)


def _local_step(x, p, tgt, small, comm):
    t = x.shape[0]
    tile = _pick(t, (256, 128))
    tall = _pick(t, (512, 256, 128))
    d = D_MODEL
    act_b, act_f = (d, BF16), (d, F32)
    x_b, p_b = x.astype(BF16), p.astype(BF16)

    chunk_id = jnp.arange(SGU_BLOCK) // CHUNK
    mask = chunk_id[:, None] >= chunk_id[None, :]
    wm = jnp.where(mask[None], small["sgu_w_s"], 0.0)
    wm_b = wm.astype(BF16)
    wm_t = jnp.swapaxes(wm, 1, 2).astype(BF16)
    bs_t = small["sgu_b_s"].T

    h, w_in = comm.project_in(x_b)
    y_a, got_a = _sgu_fwd(h, wm_b, bs_t, small["sgu_norm_g"], small["sgu_norm_b"], ex=comm.weights_exchange(0))
    y_b, o_all, states, got_b = _hgrn_fwd(h, small["lb_logits"], small["hgrn_norm_g"], ex=comm.weights_exchange(1))
    wts, conv_w = comm.weights(list(got_a) + list(got_b))
    z_a = _mm(y_a, wts["w_a"], out_dtype=F32, name="mm_za")
    z_b = _mm(y_b, wts["w_b"], out_dtype=F32, name="mm_zb")
    gates = [(h, SLOT_GA), (h, SLOT_GB)]
    merged, = _rowwise(_merge_fwd, gates + [z_a, z_b], [], [act_b], [], tile=tall, name="merge_fwd")
    r1 = _mm(merged, wts["w_o"], out_dtype=F32, name="mm_r1")
    x1, x1_b = _rowwise(_ln1_fwd, [x, r1], [small["ln1_g"], small["ln1_b"]], [act_f, act_b], [], tile=tall, name="ln1_fwd")
    gate = _mm(x1_b, wts["w_g"], out_dtype=F32, name="mm_gate")
    val = _mm(x1_b, wts["w_v"], out_dtype=F32, name="mm_val")
    pg = _mm(x1_b, wts["w_pg"], out_dtype=F32, name="mm_pg")
    pp = _mm(p_b, wts["w_pp"], out_dtype=F32, name="mm_pp")
    hid, = _rowwise(functools.partial(_conv_fwd, tile), [gate, (gate, _halo_spec(D_FF, tile, t, False)), val],
                    [conv_w, small["conv_b"]], [(D_FF, BF16)], [], tile=tile, name="conv_fwd")
    fo = _mm(hid, wts["w_down"], out_dtype=F32, name="mm_down")
    dz2, dz2_b, dpg, dpp, loss, dg2, db2 = _rowwise(
        _ln2_loss, [x1, fo, pg, pp, tgt], [small["ln2_g"], small["ln2_b"]],
        [act_f, act_b, act_b, act_b], [(8, LANE), (1, d), (1, d)], tile=tall, name="ln2_loss")

    dhid = _mm(dz2_b, wts["w_down"], out_dtype=BF16, name="mm_dhid", trans_b=True)
    g_down = _mm_tn(hid, dz2_b, out_dtype=BF16, name="mm_g_down")
    dcg, dval, dcw0, dcw1, dcw2, dcb = _rowwise(
        functools.partial(_conv_bwd_a, tile), [gate, (gate, _halo_spec(D_FF, tile, t, False)), val, dhid],
        [conv_w, small["conv_b"]], [(D_FF, BF16), (D_FF, BF16)], [(1, D_FF)] * 4, tile=tile, name="conv_bwd_a")
    dgate, = _rowwise(functools.partial(_conv_bwd_b, tile), [dcg, (dcg, _halo_spec(D_FF, tile, t, True, rows=16))],
                      [conv_w], [(D_FF, BF16)], [], tile=tile, name="conv_bwd_b")
    g_g = _mm_tn(x1_b, dgate, out_dtype=BF16, name="mm_g_gate")
    g_v = _mm_tn(x1_b, dval, out_dtype=BF16, name="mm_g_val")
    g_pg = _mm_tn(x1_b, dpg, out_dtype=BF16, name="mm_g_pg")
    g_pp = _mm_tn(p_b, dpp, out_dtype=BF16, name="mm_g_pp")
    dx1 = _mm(dgate, wts["w_g"], out_dtype=F32, name="mm_dx1_gate", trans_b=True, adds=[(dz2, ALPHA)])
    dx1 = _mm(dval, wts["w_v"], out_dtype=F32, name="mm_dx1_val", trans_b=True, adds=[(dx1, 1.0)])
    dx1 = _mm(dpg, wts["w_pg"], out_dtype=F32, name="mm_dx1_pg", trans_b=True, adds=[(dx1, 1.0)])
    dz1, dz1_b, dg1, db1 = _rowwise(_ln1_bwd, [x, r1, dx1], [small["ln1_g"]], [act_f, act_b], [(1, d), (1, d)],
                                    tile=tall, name="ln1_bwd")
    g_o = _mm_tn(merged, dz1_b, out_dtype=BF16, name="mm_g_o")
    dm = _mm(dz1_b, wts["w_o"], out_dtype=BF16, name="mm_dm", trans_b=True)
    dh, dza, dzb = _rowwise(_merge_bwd, gates + [z_a, z_b, dm], [],
                            [("stack", 2, SLOT_GA // 2, 8, d, BF16), act_b, act_b], [], tile=tall, name="merge_bwd")
    g_a = _mm_tn(y_a, dza, out_dtype=BF16, name="mm_g_a")
    g_b = _mm_tn(y_b, dzb, out_dtype=BF16, name="mm_g_b")
    dy_a = _mm(dza, wts["w_a"], out_dtype=BF16, name="mm_dya", trans_b=True)
    dy_b = _mm(dzb, wts["w_b"], out_dtype=F32, name="mm_dyb", trans_b=True)
    dh, dws, dbs, dgv_n, dbv_n = _sgu_bwd(h, dy_a, dh, wm_b, wm_t, bs_t, small["sgu_norm_g"], small["sgu_norm_b"])
    big = dict(w_a=g_a, w_b=g_b, w_o=g_o, w_g=g_g, w_v=g_v, w_down=g_down, w_pp=g_pp, w_pg=g_pg)
    sm = dict(sgu_w_s=jnp.where(mask[None], dws, 0.0), sgu_b_s=dbs[:, :GROUPS].T, sgu_norm_g=dgv_n, sgu_norm_b=dbv_n,
              ln1_g=dg1, ln1_b=db1, conv_w=jnp.concatenate([dcw0, dcw1, dcw2], axis=0), conv_b=dcb, ln2_g=dg2, ln2_b=db2,
              loss=loss)
    dh, dlogits, dgn, got = _hgrn_bwd(h, o_all, dy_b, states, dh, small["lb_logits"], small["hgrn_norm_g"],
                                      ex=comm.grads_exchange(big, sm))
    comm.grads_done(got)
    g_in = _mm_tn(x_b, dh, out_dtype=BF16, name="mm_g_in")
    ex = comm.last_exchange(g_in, dict(lb_logits=dlogits, hgrn_norm_g=dgn))
    res = _mm(dh, w_in, out_dtype=F32, name="mm_dx", trans_b=True, reduce_b=True, adds=[(dz1, ALPHA)], ex=ex)
    grad_x, got = res if ex else (res, ())
    comm.last_done(got)
    return grad_x


_SMALL_EARLY = ["sgu_w_s", "sgu_b_s", "sgu_norm_g", "sgu_norm_b", "ln1_g", "ln1_b", "ffn_conv_b", "ln2_g", "ln2_b"]
_SMALL_LATE = ["hgrn_lb_logits", "hgrn_norm_g"]
N_TAPS = D_FF // N_DEV
UP_COLS = 2 * D_FF // N_DEV


class _StepExchanges:
    def __init__(self, w_in_shard, shards):
        self.w_in_shard = w_in_shard
        self.shards = shards

    def project_in(self, x_b):
        return _gather_project(x_b, self.w_in_shard)

    def weights_exchange(self, part):
        return _Exchange([(s, "gather+relay") for s in (self.shards[:2] if part == 0 else self.shards[2:])])

    def weights(self, got):
        d, f = D_MODEL, D_FF
        w_br_g, w_o_g, w_up_g, w_down_g, w_pp_g, w_pg_g, conv_g = got
        w_br = w_br_g.transpose(1, 0, 2, 3).reshape(2, d, d)
        w_up = w_up_g.transpose(1, 0, 2).reshape(d, 2, f).transpose(1, 0, 2)
        wts = dict(w_a=w_br[0], w_b=w_br[1], w_o=w_o_g.reshape(d, d), w_g=w_up[0], w_v=w_up[1],
                   w_down=w_down_g.reshape(f, d), w_pp=w_pp_g.transpose(1, 0, 2).reshape(256, d), w_pg=w_pg_g.reshape(d, d))
        return wts, conv_g.transpose(1, 0, 2).reshape(3, f)

    def grads_exchange(self, big, sm):
        d = D_MODEL
        parts = [jnp.stack([big["w_a"], big["w_b"]]).reshape(2, N_DEV, 128, d).transpose(1, 0, 2, 3),
                 big["w_o"].reshape(N_DEV, 128, d),
                 jnp.concatenate([big["w_g"], big["w_v"]], axis=1).reshape(d, N_DEV, UP_COLS).transpose(1, 0, 2),
                 big["w_down"].reshape(N_DEV, N_TAPS, d),
                 big["w_pp"].reshape(256, N_DEV, 128).transpose(1, 0, 2),
                 big["w_pg"].reshape(N_DEV, 128, d)]
        packed, self.rows_early = _pack([sm[k] for k in ("sgu_w_s", "sgu_b_s", "sgu_norm_g", "sgu_norm_b", "ln1_g", "ln1_b",
                                                         "conv_b", "ln2_g", "ln2_b", "conv_w", "loss")])
        return _Exchange([(a, "scatter") for a in parts] + [(packed, "gather")])

    def grads_done(self, got):
        self.recv, self.small_early = got[:6], got[6]

    def last_exchange(self, g_in, sm):
        packed, self.rows_late = _pack([sm["lb_logits"], sm["hgrn_norm_g"]])
        return _Exchange([(_pair_sums(g_in), "scatter+pairs"), (packed, "gather")])

    def last_done(self, got):
        self.recv_in, self.small_late = got


def _rows128(a):
    flat = a.reshape(-1)
    rows = -(-flat.shape[0] // (8 * LANE)) * 8
    return jnp.pad(flat, (0, rows * LANE - flat.shape[0])).reshape(rows, LANE)


def _pack(parts):
    blocks = [_rows128(a) for a in parts]
    return jnp.concatenate(blocks, axis=0), [b.shape[0] for b in blocks]


def _unpack(packed, shapes, rows):
    out, r0 = [], 0
    for shp, r in zip(shapes, rows):
        n = math.prod(shp)
        out.append(packed[r0:r0 + r].reshape(-1)[:n].reshape(shp))
        r0 += r
    return out


def kernel(x, p, w_in, sgu_w_s, sgu_b_s, sgu_norm_g, sgu_norm_b, hgrn_lb_logits, hgrn_norm_g, w_branch, w_out, ln1_g, ln1_b, ffn_w_up, ffn_conv_w, ffn_conv_b, ffn_w_down, ln2_g, ln2_b, ple_w_proj, ple_w_gate, loss_target, m_w_in, m_sgu_w_s, m_sgu_b_s, m_sgu_norm_g, m_sgu_norm_b, m_hgrn_lb_logits, m_hgrn_norm_g, m_w_branch, m_w_out, m_ln1_g, m_ln1_b, m_ffn_w_up, m_ffn_conv_w, m_ffn_conv_b, m_ffn_w_down, m_ln2_g, m_ln2_b, m_ple_w_proj, m_ple_w_gate, v_w_in, v_sgu_w_s, v_sgu_b_s, v_sgu_norm_g, v_sgu_norm_b, v_hgrn_lb_logits, v_hgrn_norm_g, v_w_branch, v_w_out, v_ln1_g, v_ln1_b, v_ffn_w_up, v_ffn_conv_w, v_ffn_conv_b, v_ffn_w_down, v_ln2_g, v_ln2_b, v_ple_w_proj, v_ple_w_gate):
    weights = dict(w_in=w_in, sgu_w_s=sgu_w_s, sgu_b_s=sgu_b_s, sgu_norm_g=sgu_norm_g, sgu_norm_b=sgu_norm_b,
                   hgrn_lb_logits=hgrn_lb_logits, hgrn_norm_g=hgrn_norm_g, w_branch=w_branch, w_out=w_out,
                   ln1_g=ln1_g, ln1_b=ln1_b, ffn_w_up=ffn_w_up, ffn_conv_w=ffn_conv_w, ffn_conv_b=ffn_conv_b,
                   ffn_w_down=ffn_w_down, ln2_g=ln2_g, ln2_b=ln2_b, ple_w_proj=ple_w_proj, ple_w_gate=ple_w_gate)
    mom_m = dict(w_in=m_w_in, sgu_w_s=m_sgu_w_s, sgu_b_s=m_sgu_b_s, sgu_norm_g=m_sgu_norm_g, sgu_norm_b=m_sgu_norm_b,
                 hgrn_lb_logits=m_hgrn_lb_logits, hgrn_norm_g=m_hgrn_norm_g, w_branch=m_w_branch, w_out=m_w_out,
                 ln1_g=m_ln1_g, ln1_b=m_ln1_b, ffn_w_up=m_ffn_w_up, ffn_conv_w=m_ffn_conv_w, ffn_conv_b=m_ffn_conv_b,
                 ffn_w_down=m_ffn_w_down, ln2_g=m_ln2_g, ln2_b=m_ln2_b, ple_w_proj=m_ple_w_proj, ple_w_gate=m_ple_w_gate)
    mom_v = dict(w_in=v_w_in, sgu_w_s=v_sgu_w_s, sgu_b_s=v_sgu_b_s, sgu_norm_g=v_sgu_norm_g, sgu_norm_b=v_sgu_norm_b,
                 hgrn_lb_logits=v_hgrn_lb_logits, hgrn_norm_g=v_hgrn_norm_g, w_branch=v_w_branch, w_out=v_w_out,
                 ln1_g=v_ln1_g, ln1_b=v_ln1_b, ffn_w_up=v_ffn_w_up, ffn_conv_w=v_ffn_conv_w, ffn_conv_b=v_ffn_conv_b,
                 ffn_w_down=v_ffn_w_down, ln2_g=v_ln2_g, ln2_b=v_ln2_b, ple_w_proj=v_ple_w_proj, ple_w_gate=v_ple_w_gate)
    d, f = D_MODEL, D_FF
    me = _my_index()

    comm = _StepExchanges(w_in[0].astype(BF16),
                          [w_branch[0].astype(BF16), w_out[0].astype(BF16), ffn_w_up[0].astype(BF16),
                           ffn_w_down[0].astype(BF16), ple_w_proj[0].astype(BF16), ple_w_gate[0].astype(BF16), ffn_conv_w[0]])
    small = dict(sgu_w_s=sgu_w_s[0], sgu_b_s=sgu_b_s[0], sgu_norm_g=sgu_norm_g, sgu_norm_b=sgu_norm_b,
                 lb_logits=hgrn_lb_logits, hgrn_norm_g=hgrn_norm_g, ln1_g=ln1_g, ln1_b=ln1_b, ln2_g=ln2_g, ln2_b=ln2_b,
                 conv_b=ffn_conv_b)
    grad_x = _local_step(x[0], p[0, 0], loss_target[0], small, comm)

    out = {}

    def adam(name, parts8, shape2d):
        w2, m2, v2 = (a.reshape(shape2d) for a in (weights[name], mom_m[name], mom_v[name]))
        res = _adam_call(w2, m2, v2, parts8.reshape(parts8.shape[:1] + shape2d), "adam_" + name)
        out[name] = tuple(r.reshape(weights[name].shape) for r in res)

    adam("w_in", comm.recv_in, (d, d))
    adam("w_branch", comm.recv[0], (256, d))
    adam("w_out", comm.recv[1], (128, d))
    adam("ffn_w_up", comm.recv[2], (d, UP_COLS))
    adam("ffn_w_down", comm.recv[3], (N_TAPS, d))
    adam("ple_w_proj", comm.recv[4], (256, 128))
    adam("ple_w_gate", comm.recv[5], (128, d))

    def adam_small(names, extra_w, extra_m, extra_v, extra_shapes, parts8, rows, label):
        pk = lambda src, extra: _pack([src[n] for n in names] + extra)[0]
        res = _adam_call(pk(weights, extra_w), pk(mom_m, extra_m), pk(mom_v, extra_v), parts8, label)
        shapes = [weights[n].shape for n in names] + extra_shapes
        unpacked = [_unpack(r, shapes, rows) for r in res]
        for j, n in enumerate(names):
            out[n] = tuple(u[j] for u in unpacked)
        return [[u[len(names) + j] for u in unpacked] for j in range(len(extra_shapes))]

    blank = jnp.zeros((8, LANE), F32)
    taps, loss_rows = adam_small(
        _SMALL_EARLY, [_place_taps(ffn_conv_w[0], me, f), blank], [_place_taps(m_ffn_conv_w[0], me, f), blank],
        [_place_taps(v_ffn_conv_w[0], me, f), blank + 1.0], [(3, f), (8, LANE)], comm.small_early, comm.rows_early,
        "adam_small_early")
    adam_small(_SMALL_LATE, [], [], [], [], comm.small_late, comm.rows_late, "adam_small_late")
    out["ffn_conv_w"] = tuple(lax.dynamic_slice_in_dim(u, me * N_TAPS, N_TAPS, axis=1)[None] for u in taps)
    loss = loss_rows[0][0, 0]

    order = ["w_in", "sgu_w_s", "sgu_b_s", "sgu_norm_g", "sgu_norm_b", "hgrn_lb_logits", "hgrn_norm_g", "w_branch", "w_out",
             "ln1_g", "ln1_b", "ffn_w_up", "ffn_conv_w", "ffn_conv_b", "ffn_w_down", "ln2_g", "ln2_b", "ple_w_proj", "ple_w_gate"]
    return (loss, grad_x[None], *[out[n][0] for n in order], *[out[n][1] for n in order],
            *[out[n][2] for n in order], *[out[n][3] for n in order])


def _place_taps(shard, me, f):
    return lax.dynamic_update_slice_in_dim(jnp.zeros((3, f), F32), shard, me * N_TAPS, axis=1)
```
